```python
import jax
import jax.numpy as jnp
from jax import lax
import numpy as np

D_MODEL = 2048
BATCH = 8
SEQ = 4096
DEPTH = 1

CHUNK = 64
SUB_CHUNK = 16
CONV_WIDTH = 4
GDN_HEAD_DIM = 128
GDN_HEADS = D_MODEL // (2 * GDN_HEAD_DIM)
GDN_WIDTH = GDN_HEADS * GDN_HEAD_DIM
HGRN_HEAD_DIM = 128
HGRN_VALUE_DIM = 128
HGRN_HEADS = D_MODEL // (2 * HGRN_VALUE_DIM)
HGRN_WIDTH = HGRN_HEADS * HGRN_HEAD_DIM
HGRN_V_WIDTH = HGRN_HEADS * HGRN_VALUE_DIM
D_MIX = GDN_WIDTH + HGRN_V_WIDTH
IN_PROJ_WIDTH = 4 * GDN_WIDTH + 2 * GDN_HEADS + 2 * HGRN_WIDTH + 2 * HGRN_V_WIDTH
D_FF = 4 * D_MODEL
NORM_EPS = 1e-6
L2_EPS = 1e-6

kernel_name = 'hybrid_gdn_hgrn2_block'


def rms_norm(x, w):
    xf = x.astype(jnp.float32)
    y = xf * lax.rsqrt(jnp.mean(xf * xf, axis=-1, keepdims=True) + NORM_EPS)
    return (y * w.astype(jnp.float32)).astype(x.dtype)


def head_rms_norm(o, w):
    return o * lax.rsqrt(jnp.mean(o * o, axis=-1, keepdims=True) + NORM_EPS) * w.astype(jnp.float32)


def l2_normalize(t):
    return t * lax.rsqrt(jnp.sum(t * t, axis=-1, keepdims=True) + L2_EPS)


def to_heads(t, n_heads):
    b, s, _ = t.shape
    return t.reshape(b, s, n_heads, -1).transpose(0, 2, 1, 3).astype(jnp.float32)


def from_heads(t):
    b, h, s, d = t.shape
    return t.transpose(0, 2, 1, 3).reshape(b, s, h * d)


def causal_depthwise_conv(t, w):
    return lax.conv_general_dilated(
        t, w[:, None, :].astype(t.dtype), window_strides=(1,),
        padding=[(w.shape[0] - 1, 0)], dimension_numbers=('NWC', 'WIO', 'NWC'),
        feature_group_count=t.shape[-1])


def gated_delta_rule_chunked(q, k, v, beta, g):
    b_, h_, t_, dk = q.shape
    dv = v.shape[-1]
    n_chunks = t_ // CHUNK
    q, k, v, beta, g = (t.reshape(b_, h_, n_chunks, CHUNK, *t.shape[3:]) for t in (q, k, v, beta, g))
    G = jnp.cumsum(g, axis=-1)
    pos = jnp.arange(CHUNK)
    incl = pos[:, None] >= pos[None, :]
    strict = pos[:, None] > pos[None, :]
    decay = jnp.exp(jnp.where(incl, G[..., :, None] - G[..., None, :], -jnp.inf))
    kk = jnp.einsum('bhncd,bhnsd->bhncs', k, k)
    unit_lower = jnp.where(strict, beta[..., :, None] * kk * decay, 0.0) + jnp.eye(CHUNK, dtype=jnp.float32)
    rhs = beta[..., None] * jnp.concatenate([v, jnp.exp(G)[..., None] * k], axis=-1)
    sol = lax.linalg.triangular_solve(unit_lower, rhs, left_side=True, lower=True, unit_diagonal=True)
    u_v, w = sol[..., :dv], sol[..., dv:]
    attn = jnp.einsum('bhncd,bhnsd->bhncs', q, k) * decay
    q_g = q * jnp.exp(G)[..., None]
    g_last = G[..., -1:]
    k_end = k * jnp.exp(g_last - G)[..., None]
    state_decay = jnp.exp(G[..., -1])

    def step(S, xs):
        u_v_c, w_c, q_c, attn_c, k_c, sd_c = xs
        u = u_v_c - jnp.einsum('bhcd,bhde->bhce', w_c, S)
        o = jnp.einsum('bhcd,bhde->bhce', q_c, S) + jnp.einsum('bhcs,bhse->bhce', attn_c, u)
        S = S * sd_c[..., None, None] + jnp.einsum('bhcd,bhce->bhde', k_c, u)
        return S, o

    xs = tuple(jnp.moveaxis(t, 2, 0) for t in (u_v, w, q_g, attn, k_end, state_decay))
    s0 = jnp.zeros((b_, h_, dk, dv), jnp.float32)
    _, o = lax.scan(step, s0, xs)
    return jnp.moveaxis(o, 0, 2).reshape(b_, h_, t_, dv)


def hgrn2_chunked(q, k, v, log_f):
    b_, h_, t_, dk = q.shape
    dv = v.shape[-1]
    n_chunks = t_ // CHUNK
    n_sub = CHUNK // SUB_CHUNK
    b_cum = jnp.cumsum(log_f.reshape(b_, h_, n_chunks, CHUNK, dk), axis=3)

    def blocks(t):
        return jnp.moveaxis(t.reshape(b_, h_, n_chunks, n_sub, SUB_CHUNK, t.shape[-1]), 2, 0)

    sub = jnp.arange(SUB_CHUNK)
    diag_mask = (sub[:, None] >= sub[None, :])[:, :, None]
    blk = jnp.arange(n_sub)
    off_mask = (blk[:, None] > blk[None, :])[:, :, None]

    def step(S, xs):
        qc, kc, vc, bc = xs
        b_end = bc[..., -1, :]
        b_start = jnp.concatenate([jnp.zeros_like(b_end[..., :1, :]), b_end[..., :-1, :]], axis=-2)
        o_inter = jnp.einsum('bhsid,bhde->bhsie', qc * jnp.exp(bc), S)
        d_diag = jnp.exp(jnp.where(diag_mask, bc[..., :, None, :] - bc[..., None, :, :], -jnp.inf))
        a_diag = jnp.einsum('bhsid,bhsijd,bhsjd->bhsij', qc, d_diag, kc)
        q_rel = qc * jnp.exp(bc - b_start[..., None, :])
        k_rel = kc * jnp.exp(b_end[..., None, :] - bc)
        d_off = jnp.exp(jnp.where(off_mask, b_start[..., :, None, :] - b_end[..., None, :, :], -jnp.inf))
        a_off = jnp.einsum('bhxid,bhxyd,bhyjd->bhxyij', q_rel, d_off, k_rel)
        o = (o_inter + jnp.einsum('bhsij,bhsje->bhsie', a_diag, vc)
             + jnp.einsum('bhxyij,bhyje->bhxie', a_off, vc))
        b_last = b_end[..., -1, :]
        k_state = kc * jnp.exp(b_last[:, :, None, None, :] - bc)
        S = S * jnp.exp(b_last)[..., None] + jnp.einsum('bhsjd,bhsje->bhde', k_state, vc)
        return S, o

    xs = (blocks(q), blocks(k), blocks(v), blocks(b_cum))
    s0 = jnp.zeros((b_, h_, dk, dv), jnp.float32)
    _, o = lax.scan(step, s0, xs)
    return jnp.moveaxis(o, 0, 2).reshape(b_, h_, t_, dv)


def gated_deltanet_group(qkv, z, a, b, conv_w, a_log, dt_bias, norm_w):
    qkv = jax.nn.silu(causal_depthwise_conv(qkv, conv_w))
    q, k, v = jnp.split(qkv, 3, axis=-1)
    q = l2_normalize(to_heads(q, GDN_HEADS)) * (GDN_HEAD_DIM ** -0.5)
    k = l2_normalize(to_heads(k, GDN_HEADS))
    v = to_heads(v, GDN_HEADS)
    beta = jax.nn.sigmoid(b.astype(jnp.float32)).transpose(0, 2, 1)
    g = (-jnp.exp(a_log.astype(jnp.float32))
         * jax.nn.softplus(a.astype(jnp.float32) + dt_bias.astype(jnp.float32))).transpose(0, 2, 1)
    o = gated_delta_rule_chunked(q, k, v, beta, g)
    o = head_rms_norm(o, norm_w) * jax.nn.silu(to_heads(z, GDN_HEADS))
    return from_heads(o)


def hgrn2_group(q, f, i, g, lower_bound, norm_w):
    lb = lower_bound.reshape(HGRN_HEADS, 1, HGRN_HEAD_DIM)
    f_logit = to_heads(f, HGRN_HEADS)
    forget = lb + (1.0 - lb) * jax.nn.sigmoid(f_logit)
    key = (1.0 - lb) * jax.nn.sigmoid(-f_logit)
    o = hgrn2_chunked(jax.nn.silu(to_heads(q, HGRN_HEADS)), key, to_heads(i, HGRN_HEADS), jnp.log(forget))
    o = head_rms_norm(o, norm_w) * jax.nn.silu(to_heads(g, HGRN_HEADS))
    return from_heads(o)


def hybrid_token_mixer(n, w_in, conv_w, a_log, dt_bias, gdn_norm_w, lower_bound, hgrn_norm_w, w_out):
    proj = n @ w_in
    o1 = 3 * GDN_WIDTH
    o2 = o1 + GDN_WIDTH
    o3 = o2 + GDN_HEADS
    o4 = o3 + GDN_HEADS
    o5 = o4 + HGRN_WIDTH
    o6 = o5 + HGRN_WIDTH
    o7 = o6 + HGRN_V_WIDTH
    qkv_a, z_a, a_a, b_a, q_b, f_b, i_b, g_b = jnp.split(proj, [o1, o2, o3, o4, o5, o6, o7], axis=-1)
    y_a = gated_deltanet_group(qkv_a, z_a, a_a, b_a, conv_w, a_log, dt_bias, gdn_norm_w)
    y_b = hgrn2_group(q_b, f_b, i_b, g_b, lower_bound, hgrn_norm_w)
    y = jnp.concatenate([y_a, y_b], axis=-1).astype(n.dtype)
    return y @ w_out


def squared_relu_mlp(n, w1, w2):
    return jnp.square(jax.nn.relu(n @ w1)) @ w2


def _fwd_setup_inputs(seed: int = 0) -> dict:
    key = jax.random.key(seed)
    ks = jax.random.split(key, 16)
    f32 = jnp.float32
    x = jax.random.normal(ks[0], (BATCH, SEQ, D_MODEL), f32)
    w_in = jax.random.normal(ks[1], (DEPTH, D_MODEL, IN_PROJ_WIDTH), f32) * D_MODEL ** -0.5
    conv_w = jax.random.normal(ks[2], (DEPTH, CONV_WIDTH, 3 * GDN_WIDTH), f32) * CONV_WIDTH ** -0.5
    gdn_a_log = jnp.log(jax.random.uniform(ks[3], (DEPTH, GDN_HEADS), f32, 1.0, 16.0))
    dt = jnp.exp(jax.random.uniform(ks[4], (DEPTH, GDN_HEADS), f32, np.log(1e-3), np.log(1e-1)))
    gdn_dt_bias = dt + jnp.log(-jnp.expm1(-dt))
    gdn_norm_w = 1.0 + 0.02 * jax.random.normal(ks[5], (DEPTH, GDN_HEAD_DIM), f32)
    hgrn_lb_logits = 0.1 * jax.random.normal(ks[6], (DEPTH + 1, HGRN_WIDTH), f32)
    hgrn_norm_w = 1.0 + 0.02 * jax.random.normal(ks[7], (DEPTH, HGRN_VALUE_DIM), f32)
    w_out = jax.random.normal(ks[8], (DEPTH, D_MIX, D_MODEL), f32) * D_MIX ** -0.5
    norm_mix_w = 1.0 + 0.02 * jax.random.normal(ks[9], (DEPTH, D_MODEL), f32)
    norm_ffn_w = 1.0 + 0.02 * jax.random.normal(ks[10], (DEPTH, D_MODEL), f32)
    w_ff1 = jax.random.normal(ks[11], (DEPTH, D_MODEL, D_FF), f32) * D_MODEL ** -0.5
    w_ff2 = jax.random.normal(ks[12], (DEPTH, D_FF, D_MODEL), f32) * D_FF ** -0.5
    norm_final_w = 1.0 + 0.02 * jax.random.normal(ks[13], (D_MODEL,), f32)
    return {'x': x, 'w_in': w_in, 'conv_w': conv_w, 'gdn_a_log': gdn_a_log,
            'gdn_dt_bias': gdn_dt_bias, 'gdn_norm_w': gdn_norm_w, 'hgrn_lb_logits': hgrn_lb_logits,
            'hgrn_norm_w': hgrn_norm_w, 'w_out': w_out, 'norm_mix_w': norm_mix_w,
            'norm_ffn_w': norm_ffn_w, 'w_ff1': w_ff1, 'w_ff2': w_ff2, 'norm_final_w': norm_final_w}


def _fwd_reference(x, w_in, conv_w, gdn_a_log, gdn_dt_bias, gdn_norm_w, hgrn_lb_logits, hgrn_norm_w,
              w_out, norm_mix_w, norm_ffn_w, w_ff1, w_ff2, norm_final_w):
    lower_bounds = jnp.cumsum(jax.nn.softmax(hgrn_lb_logits.astype(jnp.float32), axis=0), axis=0)
    h = x
    for layer in range(DEPTH):
        n = rms_norm(h, norm_mix_w[layer])
        h = h + hybrid_token_mixer(n, w_in[layer], conv_w[layer], gdn_a_log[layer], gdn_dt_bias[layer],
                                   gdn_norm_w[layer], lower_bounds[layer], hgrn_norm_w[layer], w_out[layer])
        n = rms_norm(h, norm_ffn_w[layer])
        h = h + squared_relu_mlp(n, w_ff1[layer], w_ff2[layer])
    return rms_norm(h, norm_final_w)


import jax as _jax
import jax.numpy as _jnp

TWIN_FORMAT = 'train_step'
FWD_PARAMS = ['x', 'w_in', 'conv_w', 'gdn_a_log', 'gdn_dt_bias', 'gdn_norm_w', 'hgrn_lb_logits', 'hgrn_norm_w', 'w_out', 'norm_mix_w', 'norm_ffn_w', 'w_ff1', 'w_ff2', 'norm_final_w']
TWIN_WEIGHTS = ['w_in', 'conv_w', 'gdn_a_log', 'gdn_dt_bias', 'gdn_norm_w', 'hgrn_lb_logits', 'hgrn_norm_w', 'w_out', 'norm_mix_w', 'norm_ffn_w', 'w_ff1', 'w_ff2', 'norm_final_w']
TWIN_DIFF_INPUT = 'x'
TWIN_INPUTS = ['x', 'w_in', 'conv_w', 'gdn_a_log', 'gdn_dt_bias', 'gdn_norm_w', 'hgrn_lb_logits', 'hgrn_norm_w', 'w_out', 'norm_mix_w', 'norm_ffn_w', 'w_ff1', 'w_ff2', 'norm_final_w', 'loss_target', 'm_w_in', 'm_conv_w', 'm_gdn_a_log', 'm_gdn_dt_bias', 'm_gdn_norm_w', 'm_hgrn_lb_logits', 'm_hgrn_norm_w', 'm_w_out', 'm_norm_mix_w', 'm_norm_ffn_w', 'm_w_ff1', 'm_w_ff2', 'm_norm_final_w', 'v_w_in', 'v_conv_w', 'v_gdn_a_log', 'v_gdn_dt_bias', 'v_gdn_norm_w', 'v_hgrn_lb_logits', 'v_hgrn_norm_w', 'v_w_out', 'v_norm_mix_w', 'v_norm_ffn_w', 'v_w_ff1', 'v_w_ff2', 'v_norm_final_w']
TWIN_OUTPUTS = ['loss', 'grad_x', 'grad_w_in', 'grad_conv_w', 'grad_gdn_a_log', 'grad_gdn_dt_bias', 'grad_gdn_norm_w', 'grad_hgrn_lb_logits', 'grad_hgrn_norm_w', 'grad_w_out', 'grad_norm_mix_w', 'grad_norm_ffn_w', 'grad_w_ff1', 'grad_w_ff2', 'grad_norm_final_w', 'delta_w_in', 'delta_conv_w', 'delta_gdn_a_log', 'delta_gdn_dt_bias', 'delta_gdn_norm_w', 'delta_hgrn_lb_logits', 'delta_hgrn_norm_w', 'delta_w_out', 'delta_norm_mix_w', 'delta_norm_ffn_w', 'delta_w_ff1', 'delta_w_ff2', 'delta_norm_final_w', 'new_m_w_in', 'new_m_conv_w', 'new_m_gdn_a_log', 'new_m_gdn_dt_bias', 'new_m_gdn_norm_w', 'new_m_hgrn_lb_logits', 'new_m_hgrn_norm_w', 'new_m_w_out', 'new_m_norm_mix_w', 'new_m_norm_ffn_w', 'new_m_w_ff1', 'new_m_w_ff2', 'new_m_norm_final_w', 'new_v_w_in', 'new_v_conv_w', 'new_v_gdn_a_log', 'new_v_gdn_dt_bias', 'new_v_gdn_norm_w', 'new_v_hgrn_lb_logits', 'new_v_hgrn_norm_w', 'new_v_w_out', 'new_v_norm_mix_w', 'new_v_norm_ffn_w', 'new_v_w_ff1', 'new_v_w_ff2', 'new_v_norm_final_w']
TWIN_LEAF_KINDS = {'loss': 'loss', 'grad_x': 'grad_x', 'grad_w_in': 'grad_w', 'grad_conv_w': 'grad_w', 'grad_gdn_a_log': 'grad_w', 'grad_gdn_dt_bias': 'grad_w', 'grad_gdn_norm_w': 'grad_w', 'grad_hgrn_lb_logits': 'grad_w', 'grad_hgrn_norm_w': 'grad_w', 'grad_w_out': 'grad_w', 'grad_norm_mix_w': 'grad_w', 'grad_norm_ffn_w': 'grad_w', 'grad_w_ff1': 'grad_w', 'grad_w_ff2': 'grad_w', 'grad_norm_final_w': 'grad_w', 'delta_w_in': 'delta_w', 'delta_conv_w': 'delta_w', 'delta_gdn_a_log': 'delta_w', 'delta_gdn_dt_bias': 'delta_w', 'delta_gdn_norm_w': 'delta_w', 'delta_hgrn_lb_logits': 'delta_w', 'delta_hgrn_norm_w': 'delta_w', 'delta_w_out': 'delta_w', 'delta_norm_mix_w': 'delta_w', 'delta_norm_ffn_w': 'delta_w', 'delta_w_ff1': 'delta_w', 'delta_w_ff2': 'delta_w', 'delta_norm_final_w': 'delta_w', 'new_m_w_in': 'new_m', 'new_m_conv_w': 'new_m', 'new_m_gdn_a_log': 'new_m', 'new_m_gdn_dt_bias': 'new_m', 'new_m_gdn_norm_w': 'new_m', 'new_m_hgrn_lb_logits': 'new_m', 'new_m_hgrn_norm_w': 'new_m', 'new_m_w_out': 'new_m', 'new_m_norm_mix_w': 'new_m', 'new_m_norm_ffn_w': 'new_m', 'new_m_w_ff1': 'new_m', 'new_m_w_ff2': 'new_m', 'new_m_norm_final_w': 'new_m', 'new_v_w_in': 'new_v', 'new_v_conv_w': 'new_v', 'new_v_gdn_a_log': 'new_v', 'new_v_gdn_dt_bias': 'new_v', 'new_v_gdn_norm_w': 'new_v', 'new_v_hgrn_lb_logits': 'new_v', 'new_v_hgrn_norm_w': 'new_v', 'new_v_w_out': 'new_v', 'new_v_norm_mix_w': 'new_v', 'new_v_norm_ffn_w': 'new_v', 'new_v_w_ff1': 'new_v', 'new_v_w_ff2': 'new_v', 'new_v_norm_final_w': 'new_v'}


def _forward(args):
    return _fwd_reference(*[args[k] for k in FWD_PARAMS])


def _output_shape():
    def fwd():
        inp = _fwd_setup_inputs(0)
        return _fwd_reference(*[inp[k] for k in FWD_PARAMS])
    out = _jax.eval_shape(fwd)
    return out.shape, out.dtype

N_MICROBATCH = 1
ADAM_LR = 0.001
ADAM_B1 = 0.9
ADAM_B2 = 0.999
ADAM_EPS = 1e-08
ADAM_WD = 0.01
ADAM_STEP = 10
PER_EXAMPLE_BATCH_AXIS = {'x': 0, 'loss_target': 0}
SHARED_INPUTS = []
_WEIGHT_DTYPES = {'w_in': _jnp.float32, 'conv_w': _jnp.float32, 'gdn_a_log': _jnp.float32, 'gdn_dt_bias': _jnp.float32, 'gdn_norm_w': _jnp.float32, 'hgrn_lb_logits': _jnp.float32, 'hgrn_norm_w': _jnp.float32, 'w_out': _jnp.float32, 'norm_mix_w': _jnp.float32, 'norm_ffn_w': _jnp.float32, 'w_ff1': _jnp.float32, 'w_ff2': _jnp.float32, 'norm_final_w': _jnp.float32}
MOMENT_SCALE = {'w_in': 3.882982e-02, 'conv_w': 3.967635e-02, 'gdn_a_log': 2.369881e-01, 'gdn_dt_bias': 2.307685e-01, 'gdn_norm_w': 1.691541e-01, 'hgrn_lb_logits': 4.728447e-03, 'hgrn_norm_w': 1.367816e-01, 'w_out': 4.963762e-02, 'norm_mix_w': 7.912434e-02, 'norm_ffn_w': 7.323252e-02, 'w_ff1': 3.725719e-02, 'w_ff2': 7.767213e-02, 'norm_final_w': 1.614593e+01}


def _to_microbatches(a, axis):
    t = _jnp.moveaxis(a, axis, 0)
    t = t.reshape((N_MICROBATCH, t.shape[0] // N_MICROBATCH) + t.shape[1:])
    return _jnp.moveaxis(t, 1, axis + 1)


def setup_inputs(seed: int = 0) -> dict:
    inp = _fwd_setup_inputs(seed)
    key = _jax.random.fold_in(_jax.random.key(seed), 7919)
    shape, _ = _output_shape()
    out = dict(inp)
    out["loss_target"] = _jax.random.normal(_jax.random.fold_in(key, 0), shape, _jnp.float32)
    for i, name in enumerate(TWIN_WEIGHTS):
        w = inp[name].astype(_jnp.float32)
        if MOMENT_SCALE is None:
            s = _jnp.sqrt(_jnp.mean(_jnp.square(w)) + 1e-30)
        else:
            s = MOMENT_SCALE[name]
        km, kv = _jax.random.split(_jax.random.fold_in(key, i + 1))
        out[name] = w
        out["m_" + name] = s * _jax.random.normal(km, w.shape, _jnp.float32)
        out["v_" + name] = (s * s) * _jax.random.uniform(kv, w.shape, _jnp.float32, 0.5, 1.5)
    if N_MICROBATCH > 1:
        for name, axis in PER_EXAMPLE_BATCH_AXIS.items():
            out[name] = _to_microbatches(out[name], axis)
    return {'x': out['x'], 'w_in': out['w_in'], 'conv_w': out['conv_w'], 'gdn_a_log': out['gdn_a_log'], 'gdn_dt_bias': out['gdn_dt_bias'], 'gdn_norm_w': out['gdn_norm_w'], 'hgrn_lb_logits': out['hgrn_lb_logits'], 'hgrn_norm_w': out['hgrn_norm_w'], 'w_out': out['w_out'], 'norm_mix_w': out['norm_mix_w'], 'norm_ffn_w': out['norm_ffn_w'], 'w_ff1': out['w_ff1'], 'w_ff2': out['w_ff2'], 'norm_final_w': out['norm_final_w'], 'loss_target': out['loss_target'], 'm_w_in': out['m_w_in'], 'm_conv_w': out['m_conv_w'], 'm_gdn_a_log': out['m_gdn_a_log'], 'm_gdn_dt_bias': out['m_gdn_dt_bias'], 'm_gdn_norm_w': out['m_gdn_norm_w'], 'm_hgrn_lb_logits': out['m_hgrn_lb_logits'], 'm_hgrn_norm_w': out['m_hgrn_norm_w'], 'm_w_out': out['m_w_out'], 'm_norm_mix_w': out['m_norm_mix_w'], 'm_norm_ffn_w': out['m_norm_ffn_w'], 'm_w_ff1': out['m_w_ff1'], 'm_w_ff2': out['m_w_ff2'], 'm_norm_final_w': out['m_norm_final_w'], 'v_w_in': out['v_w_in'], 'v_conv_w': out['v_conv_w'], 'v_gdn_a_log': out['v_gdn_a_log'], 'v_gdn_dt_bias': out['v_gdn_dt_bias'], 'v_gdn_norm_w': out['v_gdn_norm_w'], 'v_hgrn_lb_logits': out['v_hgrn_lb_logits'], 'v_hgrn_norm_w': out['v_hgrn_norm_w'], 'v_w_out': out['v_w_out'], 'v_norm_mix_w': out['v_norm_mix_w'], 'v_norm_ffn_w': out['v_norm_ffn_w'], 'v_w_ff1': out['v_w_ff1'], 'v_w_ff2': out['v_w_ff2'], 'v_norm_final_w': out['v_norm_final_w']}


def _loss(weights, diff, rest, loss_target):
    with _jax.named_scope("forward"):
        args = {**rest, TWIN_DIFF_INPUT: diff, **{k: w.astype(_WEIGHT_DTYPES[k]) for k, w in weights.items()}}
        y = _forward(args)
    with _jax.named_scope("loss_head"):
        err = _jnp.square(y.astype(_jnp.float32) - loss_target)
        return 0.5 * _jnp.sum(_jnp.mean(err, axis=-1)) if err.ndim else 0.5 * err


def _adamw(w, g, m, v):
    m = ADAM_B1 * m + (1.0 - ADAM_B1) * g
    v = ADAM_B2 * v + (1.0 - ADAM_B2) * _jnp.square(g)
    m_hat = m / (1.0 - ADAM_B1 ** ADAM_STEP)
    v_hat = v / (1.0 - ADAM_B2 ** ADAM_STEP)
    delta = -ADAM_LR * (m_hat / (_jnp.sqrt(v_hat) + ADAM_EPS) + ADAM_WD * w)
    return delta, m, v


def reference(x, w_in, conv_w, gdn_a_log, gdn_dt_bias, gdn_norm_w, hgrn_lb_logits, hgrn_norm_w, w_out, norm_mix_w, norm_ffn_w, w_ff1, w_ff2, norm_final_w, loss_target, m_w_in, m_conv_w, m_gdn_a_log, m_gdn_dt_bias, m_gdn_norm_w, m_hgrn_lb_logits, m_hgrn_norm_w, m_w_out, m_norm_mix_w, m_norm_ffn_w, m_w_ff1, m_w_ff2, m_norm_final_w, v_w_in, v_conv_w, v_gdn_a_log, v_gdn_dt_bias, v_gdn_norm_w, v_hgrn_lb_logits, v_hgrn_norm_w, v_w_out, v_norm_mix_w, v_norm_ffn_w, v_w_ff1, v_w_ff2, v_norm_final_w):
    given = dict(x=x, w_in=w_in, conv_w=conv_w, gdn_a_log=gdn_a_log, gdn_dt_bias=gdn_dt_bias, gdn_norm_w=gdn_norm_w, hgrn_lb_logits=hgrn_lb_logits, hgrn_norm_w=hgrn_norm_w, w_out=w_out, norm_mix_w=norm_mix_w, norm_ffn_w=norm_ffn_w, w_ff1=w_ff1, w_ff2=w_ff2, norm_final_w=norm_final_w, loss_target=loss_target, m_w_in=m_w_in, m_conv_w=m_conv_w, m_gdn_a_log=m_gdn_a_log, m_gdn_dt_bias=m_gdn_dt_bias, m_gdn_norm_w=m_gdn_norm_w, m_hgrn_lb_logits=m_hgrn_lb_logits, m_hgrn_norm_w=m_hgrn_norm_w, m_w_out=m_w_out, m_norm_mix_w=m_norm_mix_w, m_norm_ffn_w=m_norm_ffn_w, m_w_ff1=m_w_ff1, m_w_ff2=m_w_ff2, m_norm_final_w=m_norm_final_w, v_w_in=v_w_in, v_conv_w=v_conv_w, v_gdn_a_log=v_gdn_a_log, v_gdn_dt_bias=v_gdn_dt_bias, v_gdn_norm_w=v_gdn_norm_w, v_hgrn_lb_logits=v_hgrn_lb_logits, v_hgrn_norm_w=v_hgrn_norm_w, v_w_out=v_w_out, v_norm_mix_w=v_norm_mix_w, v_norm_ffn_w=v_norm_ffn_w, v_w_ff1=v_w_ff1, v_w_ff2=v_w_ff2, v_norm_final_w=v_norm_final_w)
    weights = {n: given[n] for n in TWIN_WEIGHTS}
    shared = {n: given[n] for n in SHARED_INPUTS}
    per_example = {n: given[n] for n in ['x']}
    grad_fn = _jax.value_and_grad(_loss, argnums=(0, 1))

    def one_microbatch(ex, loss_target):
        ex = dict(ex)
        diff = ex.pop(TWIN_DIFF_INPUT)
        return grad_fn(weights, diff, {**shared, **ex}, loss_target)

    if N_MICROBATCH == 1:
        loss, (grad_w, grad_x) = one_microbatch(per_example, given["loss_target"])
    else:
        def body(carry, xs):
            loss_sum, grad_sum = carry
            l_k, (gw_k, gx_k) = one_microbatch(xs[0], xs[1])
            with _jax.named_scope("update"):
                return (loss_sum + l_k, _jax.tree.map(_jnp.add, grad_sum, gw_k)), gx_k

        init = (_jnp.zeros((), _jnp.float32), _jax.tree.map(_jnp.zeros_like, weights))
        (loss, grad_w), grad_x = _jax.lax.scan(body, init, (per_example, given["loss_target"]))
    with _jax.named_scope("update"):
        delta_w, new_m, new_v = {}, {}, {}
        for n in TWIN_WEIGHTS:
            delta_w[n], new_m[n], new_v[n] = _adamw(weights[n], grad_w[n], given["m_" + n], given["v_" + n])
    return (loss, grad_x, *[grad_w[n] for n in TWIN_WEIGHTS], *[delta_w[n] for n in TWIN_WEIGHTS],
            *[new_m[n] for n in TWIN_WEIGHTS], *[new_v[n] for n in TWIN_WEIGHTS])
```

```python
import functools

import jax
import jax.numpy as jnp
from jax import lax
from jax.experimental import pallas as pl
from jax.experimental.pallas import tpu as pltpu

F32 = jnp.float32
BF16 = jnp.bfloat16
HI = lax.Precision.HIGHEST

N_DEV = 8
D_MODEL = 2048
CHUNK = 64
SUB_CHUNK = 16
HEAD_DIM = 128
N_HEADS = 8
GDN_WIDTH = N_HEADS * HEAD_DIM
D_FF = 4 * D_MODEL
QKV_WIDTH = 3 * GDN_WIDTH
MAIN_WIDTH = 8 * GDN_WIDTH
CAT_WIDTH = MAIN_WIDTH + 128
IN_PROJ_WIDTH = MAIN_WIDTH + 2 * N_HEADS
AB_BLOCK = MAIN_WIDTH // 128
NORM_EPS = 1e-6
L2_EPS = 1e-6
LANES = 128
VMEM_LIMIT = 56 * 1024 * 1024

ADAM_LR = 0.001
ADAM_B1 = 0.9
ADAM_B2 = 0.999
ADAM_EPS = 1e-08
ADAM_WD = 0.01
ADAM_STEP = 10

MESH = pl.DeviceIdType.MESH


def _params(sem=None):
    return pltpu.CompilerParams(dimension_semantics=sem, vmem_limit_bytes=VMEM_LIMIT)


def _dot(a, b, dims, prec=None):
    return lax.dot_general(a, b, (dims, ((), ())), precision=prec, preferred_element_type=F32)


NN = ((1,), (0,))
NT = ((1,), (1,))
TN = ((0,), (0,))


def _my_flat():
    return 4 * lax.axis_index("x") + 2 * lax.axis_index("y") + lax.axis_index("c")


def _peer(k):
    x, y, c = lax.axis_index("x"), lax.axis_index("y"), lax.axis_index("c")
    kx, ky, kc = (k >> 2) & 1, (k >> 1) & 1, k & 1
    px = (1 - x) if kx else x
    py = (1 - y) if ky else y
    pc = (1 - c) if kc else c
    return (px, py, pc), 4 * px + 2 * py + pc


def exchange(xs, gather, name):
    n = len(xs)

    def body(*refs):
        x_refs, y_refs = refs[:n], refs[n:2 * n]
        send_sems, recv_sems, local_sems = refs[2 * n:]
        me = _my_flat()
        local, sends = [], []
        for a in range(n):
            src = x_refs[a] if gather else x_refs[a].at[me]
            cp = pltpu.make_async_copy(src, y_refs[a].at[me], local_sems.at[a])
            cp.start()
            local.append(cp)
        for k in range(1, N_DEV):
            peer, peer_flat = _peer(k)
            for a in range(n):
                src = x_refs[a] if gather else x_refs[a].at[peer_flat]
                cp = pltpu.make_async_remote_copy(
                    src_ref=src, dst_ref=y_refs[a].at[me],
                    send_sem=send_sems.at[a, k], recv_sem=recv_sems.at[a, k],
                    device_id=peer, device_id_type=MESH)
                cp.start()
                sends.append(cp)
        for k in range(1, N_DEV):
            _, peer_flat = _peer(k)
            for a in range(n):
                src = x_refs[a] if gather else x_refs[a].at[peer_flat]
                pltpu.make_async_remote_copy(
                    src_ref=src, dst_ref=y_refs[a].at[peer_flat],
                    send_sem=send_sems.at[a, k], recv_sem=recv_sems.at[a, k],
                    device_id=_peer(k)[0], device_id_type=MESH).wait_recv()
        for cp in sends:
            cp.wait_send()
        for cp in local:
            cp.wait()

    out_shape = [jax.ShapeDtypeStruct(((N_DEV,) + x.shape) if gather else x.shape, x.dtype) for x in xs]
    any_spec = pl.BlockSpec(memory_space=pl.ANY)
    return pl.pallas_call(
        body, name=name, out_shape=out_shape,
        in_specs=[any_spec] * n, out_specs=[any_spec] * n,
        scratch_shapes=[pltpu.SemaphoreType.DMA((n, N_DEV)), pltpu.SemaphoreType.DMA((n, N_DEV)),
                        pltpu.SemaphoreType.DMA((n,))],
    )(*xs)


def allreduce_small(x, name):
    rows = x.shape[0]

    def body(x_ref, o_ref, buf, send_sems, recv_sems):
        me = _my_flat()
        buf[me] = x_ref[...]
        sends = []
        for k in range(1, N_DEV):
            peer, _ = _peer(k)
            cp = pltpu.make_async_remote_copy(
                src_ref=x_ref, dst_ref=buf.at[me], send_sem=send_sems.at[k], recv_sem=recv_sems.at[k],
                device_id=peer, device_id_type=MESH)
            cp.start()
            sends.append(cp)
        for k in range(1, N_DEV):
            peer, peer_flat = _peer(k)
            pltpu.make_async_remote_copy(
                src_ref=x_ref, dst_ref=buf.at[peer_flat], send_sem=send_sems.at[k], recv_sem=recv_sems.at[k],
                device_id=peer, device_id_type=MESH).wait_recv()
        for cp in sends:
            cp.wait_send()
        acc = buf[0]
        for d in range(1, N_DEV):
            acc = acc + buf[d]
        o_ref[...] = acc

    vmem = pl.BlockSpec(memory_space=pltpu.VMEM)
    return pl.pallas_call(
        body, name=name, out_shape=jax.ShapeDtypeStruct((rows, LANES), F32),
        in_specs=[vmem], out_specs=vmem,
        scratch_shapes=[pltpu.VMEM((N_DEV, rows, LANES), F32),
                        pltpu.SemaphoreType.DMA((N_DEV,)), pltpu.SemaphoreType.DMA((N_DEV,))],
    )(x)


def matmul(a, b, mode, name, out_dtypes=(F32,), epilogue=None, extra=None, tm=1024, tn=1024, tk=2048):
    if mode == "nn":
        (m, kd), n = a.shape, b.shape[1]
    elif mode == "nt":
        (m, kd), n = a.shape, b.shape[0]
    else:
        (kd, m), n = a.shape, b.shape[1]
    tm, tn, tk = min(tm, m), min(tn, n), min(tk, kd)
    assert m % tm == 0 and n % tn == 0 and kd % tk == 0, (name, m, n, kd, tm, tn, tk)
    ksteps = kd // tk
    dims = {"nn": NN, "nt": NT, "tn": TN}[mode]
    n_out = len(out_dtypes)

    def body(*refs):
        a_ref, b_ref = refs[0], refs[1]
        e_ref = refs[2] if extra is not None else None
        o_refs = refs[2 + (extra is not None):2 + (extra is not None) + n_out]
        acc_ref = refs[-1]
        kk = pl.program_id(2)

        @pl.when(kk == 0)
        def _():
            acc_ref[...] = jnp.zeros_like(acc_ref)

        acc_ref[...] += _dot(a_ref[...], b_ref[...], dims)

        @pl.when(kk == ksteps - 1)
        def _():
            acc = acc_ref[...]
            if epilogue is None:
                outs = (acc,)
            else:
                outs = epilogue(acc, e_ref[...] if e_ref is not None else None)
            for o_ref, o in zip(o_refs, outs):
                o_ref[...] = o.astype(o_ref.dtype)

    if mode == "nn":
        a_spec = pl.BlockSpec((tm, tk), lambda i, j, k: (i, k))
        b_spec = pl.BlockSpec((tk, tn), lambda i, j, k: (k, j))
    elif mode == "nt":
        a_spec = pl.BlockSpec((tm, tk), lambda i, j, k: (i, k))
        b_spec = pl.BlockSpec((tn, tk), lambda i, j, k: (j, k))
    else:
        a_spec = pl.BlockSpec((tk, tm), lambda i, j, k: (k, i))
        b_spec = pl.BlockSpec((tk, tn), lambda i, j, k: (k, j))
    o_spec = pl.BlockSpec((tm, tn), lambda i, j, k: (i, j))
    in_specs = [a_spec, b_spec] + ([o_spec] if extra is not None else [])
    args = (a, b) + ((extra,) if extra is not None else ())
    res = pl.pallas_call(
        body, name=name, grid=(m // tm, n // tn, ksteps),
        in_specs=in_specs, out_specs=[o_spec] * n_out,
        out_shape=[jax.ShapeDtypeStruct((m, n), dt) for dt in out_dtypes],
        scratch_shapes=[pltpu.VMEM((tm, tn), F32)],
        compiler_params=_params(("parallel", "parallel", "arbitrary")),
    )(*args)
    return res if n_out > 1 else res[0]


ROW_BLOCK = 256


def rms_fwd(x, w, add, name):
    t, d = x.shape
    has_add = add is not None

    def body(*refs):
        x_ref, w_ref = refs[0], refs[1]
        rest = refs[2:]
        if has_add:
            add_ref, h_ref, n_ref, r_ref = rest
            h = x_ref[...] + add_ref[...]
            h_ref[...] = h
        else:
            n_ref, r_ref = rest
            h = x_ref[...]
        r = lax.rsqrt(jnp.mean(h * h, axis=-1, keepdims=True) + NORM_EPS)
        n_ref[...] = (h * r * w_ref[...]).astype(BF16)
        r_ref[...] = r

    row = pl.BlockSpec((ROW_BLOCK, d), lambda i: (i, 0))
    wspec = pl.BlockSpec((1, d), lambda i: (0, 0))
    rspec = pl.BlockSpec((ROW_BLOCK, 1), lambda i: (i, 0))
    in_specs = [row, wspec] + ([row] if has_add else [])
    out_specs = ([row] if has_add else []) + [row, rspec]
    out_shape = ([jax.ShapeDtypeStruct((t, d), F32)] if has_add else []) + [
        jax.ShapeDtypeStruct((t, d), BF16), jax.ShapeDtypeStruct((t, 1), F32)]
    args = (x, w) + ((add,) if has_add else ())
    return pl.pallas_call(body, name=name, grid=(t // ROW_BLOCK,), in_specs=in_specs, out_specs=out_specs,
                          out_shape=out_shape, compiler_params=_params(("parallel",)))(*args)


def loss_head(h1, delta, w, target, name):
    t, d = h1.shape

    def body(h_ref, dl_ref, w_ref, t_ref, loss_ref, dh_ref, dhb_ref, dw_ref):
        @pl.when(pl.program_id(0) == 0)
        def _():
            loss_ref[...] = jnp.zeros_like(loss_ref)
            dw_ref[...] = jnp.zeros_like(dw_ref)

        h = h_ref[...] + dl_ref[...]
        wv = w_ref[...]
        r = lax.rsqrt(jnp.mean(h * h, axis=-1, keepdims=True) + NORM_EPS)
        yn = h * r
        e = yn * wv - t_ref[...]
        loss_ref[...] += 0.5 * jnp.sum(jnp.sum(e * e, axis=-1, keepdims=True), axis=0, keepdims=True) / d
        dy = e / d
        dw_ref[...] += jnp.sum(dy * yn, axis=0, keepdims=True)
        dyn = dy * wv
        dh = r * (dyn - yn * jnp.mean(dyn * yn, axis=-1, keepdims=True))
        dh_ref[...] = dh
        dhb_ref[...] = dh.astype(BF16)

    row = pl.BlockSpec((ROW_BLOCK, d), lambda i: (i, 0))
    wspec = pl.BlockSpec((1, d), lambda i: (0, 0))
    one = pl.BlockSpec((1, 1), lambda i: (0, 0))
    return pl.pallas_call(
        body, name=name, grid=(t // ROW_BLOCK,),
        in_specs=[row, row, wspec, row], out_specs=[one, row, row, wspec],
        out_shape=[jax.ShapeDtypeStruct((1, 1), F32), jax.ShapeDtypeStruct((t, d), F32),
                   jax.ShapeDtypeStruct((t, d), BF16), jax.ShapeDtypeStruct((1, d), F32)],
        compiler_params=_params(("arbitrary",)))(h1, delta, w, target)


def rms_bwd(h, r, w, dn, dres, name):
    t, d = h.shape

    def body(h_ref, r_ref, w_ref, dn_ref, dres_ref, dh_ref, dhb_ref, dw_ref):
        @pl.when(pl.program_id(0) == 0)
        def _():
            dw_ref[...] = jnp.zeros_like(dw_ref)

        rv = r_ref[...]
        yn = h_ref[...] * rv
        dnv = dn_ref[...]
        dw_ref[...] += jnp.sum(dnv * yn, axis=0, keepdims=True)
        dyn = dnv * w_ref[...]
        dh = dres_ref[...] + rv * (dyn - yn * jnp.mean(dyn * yn, axis=-1, keepdims=True))
        dh_ref[...] = dh
        dhb_ref[...] = dh.astype(BF16)

    row = pl.BlockSpec((ROW_BLOCK, d), lambda i: (i, 0))
    wspec = pl.BlockSpec((1, d), lambda i: (0, 0))
    rspec = pl.BlockSpec((ROW_BLOCK, 1), lambda i: (i, 0))
    return pl.pallas_call(
        body, name=name, grid=(t // ROW_BLOCK,),
        in_specs=[row, rspec, wspec, row, row], out_specs=[row, row, wspec],
        out_shape=[jax.ShapeDtypeStruct((t, d), F32), jax.ShapeDtypeStruct((t, d), BF16),
                   jax.ShapeDtypeStruct((1, d), F32)],
        compiler_params=_params(("arbitrary",)))(h, r, w, dn, dres)


CONV_TB = 512
CONV_CB = 512
HALO = 8


def _silu(x):
    return x * jax.nn.sigmoid(x)


def _conv_pre(xcat, w, rows):
    acc = None
    for j in range(4):
        sh = 3 - j
        xs = xcat if sh == 0 else pltpu.roll(xcat, sh, 0)
        term = xs[HALO:HALO + rows] * w[j:j + 1, :]
        acc = term if acc is None else acc + term
    return acc


def conv_fwd(proj, conv_w, name):
    t = proj.shape[0]
    nb = CONV_TB // HALO

    def body(x_ref, prev_ref, w_ref, o_ref):
        prev = jnp.where(pl.program_id(1) == 0, 0.0, prev_ref[...])
        xcat = jnp.concatenate([prev, x_ref[...]], axis=0)
        o_ref[...] = _silu(_conv_pre(xcat, w_ref[...], CONV_TB))

    return pl.pallas_call(
        body, name=name, grid=(QKV_WIDTH // CONV_CB, t // CONV_TB),
        in_specs=[pl.BlockSpec((CONV_TB, CONV_CB), lambda c, i: (i, c)),
                  pl.BlockSpec((HALO, CONV_CB), lambda c, i: (jnp.maximum(i * nb - 1, 0), c)),
                  pl.BlockSpec((4, CONV_CB), lambda c, i: (0, c))],
        out_specs=pl.BlockSpec((CONV_TB, CONV_CB), lambda c, i: (i, c)),
        out_shape=jax.ShapeDtypeStruct((t, QKV_WIDTH), F32),
        compiler_params=_params(("parallel", "parallel")))(proj, proj, conv_w)


def conv_bwd(proj, dout, conv_w, col0, name):
    t = proj.shape[0]
    nb = CONV_TB // HALO
    nt = t // CONV_TB
    cb0 = col0 // CONV_CB
    rows = CONV_TB + HALO

    def body(x_ref, prev_ref, next_ref, d_ref, dnext_ref, w_ref, dx_ref, dw_ref):
        i = pl.program_id(1)

        @pl.when(i == 0)
        def _():
            dw_ref[...] = jnp.zeros_like(dw_ref)

        w = w_ref[...]
        prev = jnp.where(i == 0, 0.0, prev_ref[...])
        last = i == nt - 1
        xcat = jnp.concatenate([prev, x_ref[...], next_ref[...]], axis=0)
        pre = _conv_pre(xcat, w, rows)
        dcat = jnp.concatenate([d_ref[...], jnp.where(last, 0.0, dnext_ref[...])], axis=0)
        sg = jax.nn.sigmoid(pre)
        dpre = dcat * (sg * (1.0 + pre * (1.0 - sg)))
        dx = None
        for j in range(4):
            sh = 3 - j
            ds = dpre if sh == 0 else pltpu.roll(dpre, rows - sh, 0)
            term = ds[:CONV_TB] * w[j:j + 1, :]
            dx = term if dx is None else dx + term
        dx_ref[...] = dx.astype(BF16)
        dcur = dpre[:CONV_TB]
        parts = []
        for j in range(4):
            sh = 3 - j
            xs = xcat if sh == 0 else pltpu.roll(xcat, sh, 0)
            parts.append(jnp.sum(dcur * xs[HALO:HALO + CONV_TB], axis=0, keepdims=True))
        dw_ref[...] += jnp.concatenate(parts, axis=0)

    return pl.pallas_call(
        body, name=name, grid=(GDN_WIDTH // CONV_CB, nt),
        in_specs=[pl.BlockSpec((CONV_TB, CONV_CB), lambda c, i: (i, cb0 + c)),
                  pl.BlockSpec((HALO, CONV_CB), lambda c, i: (jnp.maximum(i * nb - 1, 0), cb0 + c)),
                  pl.BlockSpec((HALO, CONV_CB), lambda c, i: (jnp.minimum((i + 1) * nb, nt * nb - 1), cb0 + c)),
                  pl.BlockSpec((CONV_TB, CONV_CB), lambda c, i: (i, c)),
                  pl.BlockSpec((HALO, CONV_CB), lambda c, i: (jnp.minimum((i + 1) * nb, nt * nb - 1), c)),
                  pl.BlockSpec((4, CONV_CB), lambda c, i: (0, cb0 + c))],
        out_specs=[pl.BlockSpec((CONV_TB, CONV_CB), lambda c, i: (i, c)),
                   pl.BlockSpec((4, CONV_CB), lambda c, i: (0, c))],
        out_shape=[jax.ShapeDtypeStruct((t, GDN_WIDTH), BF16), jax.ShapeDtypeStruct((4, GDN_WIDTH), F32)],
        compiler_params=_params(("parallel", "arbitrary")))(proj, proj, proj, dout, dout, conv_w)


def _iota2(shape, axis):
    return lax.broadcasted_iota(jnp.int32, shape, axis)


def _softplus(x):
    return jnp.maximum(x, 0.0) + jnp.log(1.0 + jnp.exp(-jnp.abs(x)))


def _head_norm_gate(o, norm_w, gate):
    return o * lax.rsqrt(jnp.mean(o * o, axis=-1, keepdims=True) + NORM_EPS) * norm_w * _silu(gate)


def gdn_chunk(h, qc, kc, vc, zc, ab, a_log_l, dt_l, norm_w, s):
    c = CHUNK
    ri, ci = _iota2((c, c), 0), _iota2((c, c), 1)
    incl, strict, eye = ri >= ci, ri > ci, ri == ci
    lane = _iota2((c, LANES), 1)

    def row(col):
        return jnp.sum(jnp.where(eye, col, 0.0), axis=0, keepdims=True)

    q = qc * lax.rsqrt(jnp.sum(qc * qc, axis=-1, keepdims=True) + L2_EPS) * (HEAD_DIM ** -0.5)
    k = kc * lax.rsqrt(jnp.sum(kc * kc, axis=-1, keepdims=True) + L2_EPS)
    a_col = jnp.sum(jnp.where(lane == h, ab, 0.0), axis=1, keepdims=True)
    b_col = jnp.sum(jnp.where(lane == h + N_HEADS, ab, 0.0), axis=1, keepdims=True)
    beta = jax.nn.sigmoid(b_col)
    g_full = -jnp.exp(a_log_l) * _softplus(a_col + dt_l)
    g = jnp.sum(jnp.where(lane == 0, g_full, 0.0), axis=1, keepdims=True)
    gcum = jnp.sum(jnp.where(incl, row(g), 0.0), axis=1, keepdims=True)
    g_last = jnp.sum(jnp.where(_iota2((c, 1), 0) == c - 1, gcum, 0.0), axis=0, keepdims=True)
    decay = jnp.exp(jnp.where(incl, gcum - row(gcum), -jnp.inf))
    kk = _dot(k, k, NT, HI)
    low = jnp.where(strict, beta * kk * decay, 0.0)
    power = -low
    inv = jnp.where(eye, 1.0, 0.0) + power
    for _ in range(5):
        power = _dot(power, power, NN, HI)
        inv = inv + _dot(inv, power, NN, HI)
    exp_g = jnp.exp(gcum)
    u_v = _dot(inv, beta * vc, NN, HI)
    w = _dot(inv, beta * exp_g * k, NN, HI)
    attn = _dot(q, k, NT, HI) * decay
    u = u_v - _dot(w, s, NN, HI)
    o = _dot(q * exp_g, s, NN, HI) + _dot(attn, u, NN, HI)
    k_end = k * jnp.exp(g_last - gcum)
    s_new = s * jnp.exp(g_last) + _dot(k_end, u, TN, HI)
    return _head_norm_gate(o, norm_w, zc), s_new


@functools.partial(jax.custom_vjp, nondiff_argnums=(1,))
def _sroll(x, shift):
    return x if shift == 0 else pltpu.roll(x, shift, 0)


def _sroll_fwd(x, shift):
    return _sroll(x, shift), None


def _sroll_bwd(shift, _, ct):
    return (ct if shift == 0 else pltpu.roll(ct, ct.shape[0] - shift, 0),)


_sroll.defvjp(_sroll_fwd, _sroll_bwd)


def hgrn_chunk(qb, fb, ib, gb, l0, l1, norm_w, st):
    c = CHUNK
    ri, ci = _iota2((c, c), 0), _iota2((c, c), 1)
    rcol = _iota2((c, 1), 0)
    blk0 = jnp.bitwise_and(ri, -SUB_CHUNK)
    one = lambda m: jnp.where(m, 1.0, 0.0)
    lb = jax.nn.sigmoid(l0 - l1)
    forget = lb + (1.0 - lb) * jax.nn.sigmoid(fb)
    key = (1.0 - lb) * jax.nn.sigmoid(-fb)
    q = _silu(qb)
    v = ib
    logf = jnp.log(forget)
    bc = _dot(one(ci <= ri), logf, NN, HI)
    b_start = _dot(one(ci < blk0), logf, NN, HI)
    b_end = _dot(one(ci < blk0 + SUB_CHUNK), logf, NN, HI)
    b_last = jnp.sum(logf, axis=0, keepdims=True)
    o = _dot(q * jnp.exp(bc), st, NT, HI)
    rmod = jnp.bitwise_and(rcol, SUB_CHUNK - 1)
    for off in range(SUB_CHUNK):
        k_o, b_o, v_o = _sroll(key, off), _sroll(bc, off), _sroll(v, off)
        e = jnp.exp(jnp.where(rmod >= off, bc - b_o, -jnp.inf))
        a_o = jnp.sum(q * k_o * e, axis=-1, keepdims=True)
        o = o + a_o * v_o
    q_rel = q * jnp.exp(bc - b_start)
    k_rel = key * jnp.exp(b_end - bc)
    for y in range(c // SUB_CHUNK - 1):
        end_y = jnp.sum(jnp.where(rcol == SUB_CHUNK * y + SUB_CHUNK - 1, bc, 0.0), axis=0, keepdims=True)
        dq = q_rel * jnp.exp(jnp.where(rcol >= SUB_CHUNK * (y + 1), b_start - end_y, -jnp.inf))
        a_y = _dot(dq, k_rel, NT, HI)
        a_y = jnp.where((ci >= SUB_CHUNK * y) & (ci < SUB_CHUNK * (y + 1)), a_y, 0.0)
        o = o + _dot(a_y, v, NN, HI)
    k_state = key * jnp.exp(b_last - bc)
    st_new = st * jnp.exp(b_last) + _dot(v, k_state, TN, HI)
    return _head_norm_gate(o, norm_w, gb), st_new


def _tile(col_of):
    return pl.BlockSpec((CHUNK, LANES), lambda c, h: (c, col_of(h)))


def _tile_rev(nc, col_of):
    return pl.BlockSpec((CHUNK, LANES), lambda c, h: (nc - 1 - c, col_of(h)))


def _whole(shape):
    return pl.BlockSpec(shape, lambda c, h: (0,) * len(shape))


HEAD_VEC = (N_HEADS, 1, LANES)
STATE = (None, None, HEAD_DIM, HEAD_DIM)


def gdn_fwd(qkv_c, proj, a_log_l, dt_l, norm_w, name):
    t = qkv_c.shape[0]
    nc = t // CHUNK

    def body(q_ref, k_ref, v_ref, z_ref, ab_ref, al_ref, dt_ref, nw_ref, y_ref, hist_ref, s_ref):
        c, h = pl.program_id(0), pl.program_id(1)

        @pl.when(c == 0)
        def _():
            s_ref[h] = jnp.zeros((HEAD_DIM, HEAD_DIM), F32)

        s = s_ref[h]
        hist_ref[...] = s
        y, s_new = gdn_chunk(h, q_ref[...], k_ref[...], v_ref[...], z_ref[...], ab_ref[...],
                             al_ref[h], dt_ref[h], nw_ref[...], s)
        y_ref[...] = y.astype(BF16)
        s_ref[h] = s_new

    return pl.pallas_call(
        body, name=name, grid=(nc, N_HEADS),
        in_specs=[_tile(lambda h: h), _tile(lambda h: N_HEADS + h), _tile(lambda h: 2 * N_HEADS + h),
                  _tile(lambda h: 3 * N_HEADS + h), _tile(lambda h: AB_BLOCK),
                  _whole(HEAD_VEC), _whole(HEAD_VEC), _whole((1, LANES))],
        out_specs=[_tile(lambda h: h), pl.BlockSpec(STATE, lambda c, h: (c, h, 0, 0))],
        out_shape=[jax.ShapeDtypeStruct((t, 2 * GDN_WIDTH), BF16),
                   jax.ShapeDtypeStruct((nc, N_HEADS, HEAD_DIM, HEAD_DIM), F32)],
        scratch_shapes=[pltpu.VMEM((N_HEADS, HEAD_DIM, HEAD_DIM), F32)],
        compiler_params=_params(("arbitrary", "arbitrary")),
    )(qkv_c, qkv_c, qkv_c, proj, proj, a_log_l, dt_l, norm_w)


def gdn_bwd(qkv_c, proj, a_log_l, dt_l, norm_w, hist, dy, name):
    t = qkv_c.shape[0]
    nc = t // CHUNK

    def body(q_ref, k_ref, v_ref, z_ref, ab_ref, al_ref, dt_ref, nw_ref, hist_ref, dy_ref,
             dq_ref, dk_ref, dv_ref, dz_ref, dab_ref, dal_ref, ddt_ref, dnw_ref, ds_ref, dab_acc):
        c, h = pl.program_id(0), pl.program_id(1)

        @pl.when((c == 0) & (h == 0))
        def _():
            dal_ref[...] = jnp.zeros_like(dal_ref)
            ddt_ref[...] = jnp.zeros_like(ddt_ref)
            dnw_ref[...] = jnp.zeros_like(dnw_ref)

        @pl.when(c == 0)
        def _():
            ds_ref[h] = jnp.zeros((HEAD_DIM, HEAD_DIM), F32)

        @pl.when(h == 0)
        def _():
            dab_acc[...] = jnp.zeros_like(dab_acc)

        _, vjp = jax.vjp(functools.partial(gdn_chunk, h), q_ref[...], k_ref[...], v_ref[...], z_ref[...],
                         ab_ref[...], al_ref[h], dt_ref[h], nw_ref[...], hist_ref[...])
        dq, dk, dv, dz, dab, dal, ddt, dnw, ds = vjp((dy_ref[...], ds_ref[h]))
        dq_ref[...] = dq
        dk_ref[...] = dk
        dv_ref[...] = dv
        dz_ref[...] = dz.astype(BF16)
        dab_acc[...] += dab
        dal_ref[h] += dal
        ddt_ref[h] += ddt
        dnw_ref[...] += dnw
        ds_ref[h] = ds

        @pl.when(h == N_HEADS - 1)
        def _():
            dab_ref[...] = dab_acc[...].astype(BF16)

    rev = functools.partial(_tile_rev, nc)
    col = rev(lambda h: h)
    return pl.pallas_call(
        body, name=name, grid=(nc, N_HEADS),
        in_specs=[rev(lambda h: h), rev(lambda h: N_HEADS + h), rev(lambda h: 2 * N_HEADS + h),
                  rev(lambda h: 3 * N_HEADS + h), rev(lambda h: AB_BLOCK),
                  _whole(HEAD_VEC), _whole(HEAD_VEC), _whole((1, LANES)),
                  pl.BlockSpec(STATE, lambda c, h: (nc - 1 - c, h, 0, 0)), rev(lambda h: h)],
        out_specs=[col, col, col, col, rev(lambda h: 0), _whole(HEAD_VEC), _whole(HEAD_VEC), _whole((1, LANES))],
        out_shape=[jax.ShapeDtypeStruct((t, GDN_WIDTH), F32)] * 3 + [
            jax.ShapeDtypeStruct((t, GDN_WIDTH), BF16), jax.ShapeDtypeStruct((t, LANES), BF16),
            jax.ShapeDtypeStruct(HEAD_VEC, F32), jax.ShapeDtypeStruct(HEAD_VEC, F32),
            jax.ShapeDtypeStruct((1, LANES), F32)],
        scratch_shapes=[pltpu.VMEM((N_HEADS, HEAD_DIM, HEAD_DIM), F32), pltpu.VMEM((CHUNK, LANES), F32)],
        compiler_params=_params(("arbitrary", "arbitrary")),
    )(qkv_c, qkv_c, qkv_c, proj, proj, a_log_l, dt_l, norm_w, hist, dy)


HG0 = 4 * N_HEADS


def hgrn_fwd(proj, l0, l1, norm_w, y, name):
    t = proj.shape[0]
    nc = t // CHUNK

    def body(q_ref, f_ref, i_ref, g_ref, l0_ref, l1_ref, nw_ref, y_in, y_ref, hist_ref, s_ref):
        del y_in
        c, h = pl.program_id(0), pl.program_id(1)

        @pl.when(c == 0)
        def _():
            s_ref[h] = jnp.zeros((HEAD_DIM, HEAD_DIM), F32)

        s = s_ref[h]
        hist_ref[...] = s
        out, s_new = hgrn_chunk(q_ref[...], f_ref[...], i_ref[...], g_ref[...], l0_ref[h], l1_ref[h], nw_ref[...], s)
        y_ref[...] = out.astype(BF16)
        s_ref[h] = s_new

    return pl.pallas_call(
        body, name=name, grid=(nc, N_HEADS),
        in_specs=[_tile(lambda h: HG0 + h), _tile(lambda h: HG0 + N_HEADS + h), _tile(lambda h: HG0 + 2 * N_HEADS + h),
                  _tile(lambda h: HG0 + 3 * N_HEADS + h), _whole(HEAD_VEC), _whole(HEAD_VEC), _whole((1, LANES)),
                  pl.BlockSpec(memory_space=pl.ANY)],
        out_specs=[_tile(lambda h: N_HEADS + h), pl.BlockSpec(STATE, lambda c, h: (c, h, 0, 0))],
        out_shape=[jax.ShapeDtypeStruct((t, 2 * GDN_WIDTH), BF16),
                   jax.ShapeDtypeStruct((nc, N_HEADS, HEAD_DIM, HEAD_DIM), F32)],
        scratch_shapes=[pltpu.VMEM((N_HEADS, HEAD_DIM, HEAD_DIM), F32)],
        input_output_aliases={7: 0},
        compiler_params=_params(("arbitrary", "arbitrary")),
    )(proj, proj, proj, proj, l0, l1, norm_w, y)


def hgrn_bwd(proj, l0, l1, norm_w, hist, dy, name):
    t = proj.shape[0]
    nc = t // CHUNK

    def body(q_ref, f_ref, i_ref, g_ref, l0_ref, l1_ref, nw_ref, hist_ref, dy_ref,
             dq_ref, df_ref, di_ref, dg_ref, dl0_ref, dl1_ref, dnw_ref, ds_ref):
        c, h = pl.program_id(0), pl.program_id(1)

        @pl.when((c == 0) & (h == 0))
        def _():
            dl0_ref[...] = jnp.zeros_like(dl0_ref)
            dl1_ref[...] = jnp.zeros_like(dl1_ref)
            dnw_ref[...] = jnp.zeros_like(dnw_ref)

        @pl.when(c == 0)
        def _():
            ds_ref[h] = jnp.zeros((HEAD_DIM, HEAD_DIM), F32)

        _, vjp = jax.vjp(hgrn_chunk, q_ref[...], f_ref[...], i_ref[...], g_ref[...], l0_ref[h], l1_ref[h],
                         nw_ref[...], hist_ref[...])
        dq, df, di, dg, dl0, dl1, dnw, ds = vjp((dy_ref[...], ds_ref[h]))
        dq_ref[...] = dq.astype(BF16)
        df_ref[...] = df.astype(BF16)
        di_ref[...] = di.astype(BF16)
        dg_ref[...] = dg.astype(BF16)
        dl0_ref[h] += dl0
        dl1_ref[h] += dl1
        dnw_ref[...] += dnw
        ds_ref[h] = ds

    rev = functools.partial(_tile_rev, nc)
    col = rev(lambda h: h)
    return pl.pallas_call(
        body, name=name, grid=(nc, N_HEADS),
        in_specs=[rev(lambda h: HG0 + h), rev(lambda h: HG0 + N_HEADS + h), rev(lambda h: HG0 + 2 * N_HEADS + h),
                  rev(lambda h: HG0 + 3 * N_HEADS + h), _whole(HEAD_VEC), _whole(HEAD_VEC), _whole((1, LANES)),
                  pl.BlockSpec(STATE, lambda c, h: (nc - 1 - c, h, 0, 0)), rev(lambda h: N_HEADS + h)],
        out_specs=[col, col, col, col, _whole(HEAD_VEC), _whole(HEAD_VEC), _whole((1, LANES))],
        out_shape=[jax.ShapeDtypeStruct((t, GDN_WIDTH), BF16)] * 4 + [
            jax.ShapeDtypeStruct(HEAD_VEC, F32), jax.ShapeDtypeStruct(HEAD_VEC, F32),
            jax.ShapeDtypeStruct((1, LANES), F32)],
        scratch_shapes=[pltpu.VMEM((N_HEADS, HEAD_DIM, HEAD_DIM), F32)],
        compiler_params=_params(("arbitrary", "arbitrary")),
    )(proj, proj, proj, proj, l0, l1, norm_w, hist, dy)


def _adamw(w, g, m, v):
    m = ADAM_B1 * m + (1.0 - ADAM_B1) * g
    v = ADAM_B2 * v + (1.0 - ADAM_B2) * jnp.square(g)
    m_hat = m / (1.0 - ADAM_B1 ** ADAM_STEP)
    v_hat = v / (1.0 - ADAM_B2 ** ADAM_STEP)
    delta = -ADAM_LR * (m_hat / (jnp.sqrt(v_hat) + ADAM_EPS) + ADAM_WD * w)
    return delta, m, v


def adamw_reduce(parts, w, m, v, name, rb=128):
    r, c = w.shape
    rb = min(rb, r)

    def body(p_ref, w_ref, m_ref, v_ref, g_ref, d_ref, mo_ref, vo_ref):
        g = p_ref[0].astype(F32)
        for d in range(1, N_DEV):
            g = g + p_ref[d].astype(F32)
        delta, mn, vn = _adamw(w_ref[...], g, m_ref[...], v_ref[...])
        g_ref[...] = g
        d_ref[...] = delta
        mo_ref[...] = mn
        vo_ref[...] = vn

    blk = pl.BlockSpec((rb, c), lambda i: (i, 0))
    return pl.pallas_call(
        body, name=name, grid=(r // rb,),
        in_specs=[pl.BlockSpec((N_DEV, rb, c), lambda i: (0, i, 0)), blk, blk, blk],
        out_specs=[blk] * 4, out_shape=[jax.ShapeDtypeStruct((r, c), F32)] * 4,
        compiler_params=_params(("parallel",)))(parts, w, m, v)


def adamw_small(w, g, m, v, name):
    def body(w_ref, g_ref, m_ref, v_ref, d_ref, mo_ref, vo_ref):
        delta, mn, vn = _adamw(w_ref[...], g_ref[...], m_ref[...], v_ref[...])
        d_ref[...] = delta
        mo_ref[...] = mn
        vo_ref[...] = vn

    vmem = pl.BlockSpec(memory_space=pltpu.VMEM)
    return pl.pallas_call(body, name=name, in_specs=[vmem] * 4, out_specs=[vmem] * 3,
                          out_shape=[jax.ShapeDtypeStruct(w.shape, F32)] * 3)(w, g, m, v)


def _pack(arrays):
    flat = jnp.concatenate([a.reshape(-1).astype(F32) for a in arrays])
    rows = -(-flat.shape[0] // (8 * LANES)) * 8
    return jnp.pad(flat, (0, rows * LANES - flat.shape[0])).reshape(rows, LANES)


def _unpack(packed, shapes):
    flat, out, off = packed.reshape(-1), [], 0
    for s in shapes:
        n = 1
        for d in s:
            n *= d
        out.append(flat[off:off + n].reshape(s))
        off += n
    return out


def _relu2_epilogue(acc, _):
    r = jnp.maximum(acc, 0.0)
    return acc, r * r


def _relu2_bwd_epilogue(acc, a1):
    return (acc * (2.0 * jnp.maximum(a1, 0.0)),)


def kernel(x, w_in, conv_w, gdn_a_log, gdn_dt_bias, gdn_norm_w, hgrn_lb_logits, hgrn_norm_w, w_out, norm_mix_w, norm_ffn_w, w_ff1, w_ff2, norm_final_w, loss_target, m_w_in, m_conv_w, m_gdn_a_log, m_gdn_dt_bias, m_gdn_norm_w, m_hgrn_lb_logits, m_hgrn_norm_w, m_w_out, m_norm_mix_w, m_norm_ffn_w, m_w_ff1, m_w_ff2, m_norm_final_w, v_w_in, v_conv_w, v_gdn_a_log, v_gdn_dt_bias, v_gdn_norm_w, v_hgrn_lb_logits, v_hgrn_norm_w, v_w_out, v_norm_mix_w, v_norm_ffn_w, v_w_ff1, v_w_ff2, v_norm_final_w):
    me = _my_flat()
    xs = x[0]
    target = loss_target[0]
    shard_in = w_in.shape[2]
    shard_conv = conv_w.shape[2]

    g_in, g_out, g_ff1, g_ff2, g_conv = exchange(
        [w_in[0].astype(BF16), w_out[0].astype(BF16), w_ff1[0].astype(BF16), w_ff2[0].astype(BF16), conv_w[0]],
        gather=True, name="gather_weights")
    w_in_full = jnp.transpose(g_in, (1, 0, 2)).reshape(D_MODEL, N_DEV * shard_in)
    o_z = 4 * GDN_WIDTH
    w_cat = jnp.concatenate([w_in_full[:, :o_z], w_in_full[:, o_z + 2 * N_HEADS:], w_in_full[:, o_z:o_z + 2 * N_HEADS],
                             jnp.zeros((D_MODEL, LANES - 2 * N_HEADS), BF16)], axis=1)
    w_out_full = g_out.reshape(D_MODEL, D_MODEL)
    w_ff1_full = jnp.transpose(g_ff1, (1, 0, 2)).reshape(D_MODEL, D_FF)
    w_ff2_full = g_ff2.reshape(D_FF, D_MODEL)
    conv_full = jnp.transpose(g_conv, (1, 0, 2)).reshape(4, QKV_WIDTH)

    lane_b = lambda p: jnp.broadcast_to(p.reshape(N_HEADS, 1, 1), HEAD_VEC)
    a_log_l, dt_l = lane_b(gdn_a_log[0]), lane_b(gdn_dt_bias[0])
    l0 = hgrn_lb_logits[0].reshape(HEAD_VEC)
    l1 = hgrn_lb_logits[1].reshape(HEAD_VEC)

    n1, r1 = rms_fwd(xs, norm_mix_w, None, "rms_mix")
    proj = matmul(n1, w_cat, "nn", "in_proj", tn=640)
    qkv_c = conv_fwd(proj, conv_full, "conv_fwd")
    y_half, hist_a = gdn_fwd(qkv_c, proj, a_log_l, dt_l, gdn_norm_w, "gdn_fwd")
    y, hist_b = hgrn_fwd(proj, l0, l1, hgrn_norm_w, y_half, "hgrn_fwd")
    mix = matmul(y, w_out_full, "nn", "out_proj")
    h1, n2, r2 = rms_fwd(xs, norm_ffn_w, mix, "rms_ffn")
    a1, act = matmul(n2, w_ff1_full, "nn", "ff1", out_dtypes=(F32, BF16), epilogue=_relu2_epilogue)
    ff = matmul(act, w_ff2_full, "nn", "ff2")
    loss_sum, dh2, dh2_b, d_final = loss_head(h1, ff, norm_final_w.reshape(1, D_MODEL), target, "loss_head")

    da1 = matmul(dh2_b, w_ff2_full, "nt", "d_act", out_dtypes=(BF16,), epilogue=_relu2_bwd_epilogue, extra=a1)
    dw_ff2 = matmul(act, dh2_b, "tn", "dw_ff2", tk=1024)
    dn2 = matmul(da1, w_ff1_full, "nt", "d_n2")
    dw_ff1 = matmul(n2, da1, "tn", "dw_ff1", tk=1024)
    dh1, dh1_b, d_ffn = rms_bwd(h1, r2, norm_ffn_w, dn2, dh2, "rms_ffn_bwd")
    dmix = matmul(dh1_b, w_out_full, "nt", "d_mix")
    dw_out = matmul(y, dh1_b, "tn", "dw_out", tk=1024)
    dq_c, dk_c, dv_c, dz, dab, d_alog_l, d_dt_l, d_gnw = gdn_bwd(
        qkv_c, proj, a_log_l, dt_l, gdn_norm_w, hist_a, dmix, "gdn_bwd")
    dqb, dfb, dib, dgb, dl0, dl1, d_hnw = hgrn_bwd(proj, l0, l1, hgrn_norm_w, hist_b, dmix, "hgrn_bwd")
    dq, dwc_q = conv_bwd(proj, dq_c, conv_full, 0, "conv_bwd_q")
    dk, dwc_k = conv_bwd(proj, dk_c, conv_full, GDN_WIDTH, "conv_bwd_k")
    dv, dwc_v = conv_bwd(proj, dv_c, conv_full, 2 * GDN_WIDTH, "conv_bwd_v")
    dproj = jnp.concatenate([dq, dk, dv, dz, dqb, dfb, dib, dgb, dab], axis=1)
    dn1 = matmul(dproj, w_cat, "nt", "d_n1", tk=CAT_WIDTH // 5)
    dw_cat = matmul(n1, dproj, "tn", "dw_in", tn=640, tk=1024)
    dx, _, d_mix = rms_bwd(xs, r1, norm_mix_w, dn1, dh1, "rms_mix_bwd")

    dw_in_full = jnp.concatenate([dw_cat[:, :o_z], dw_cat[:, MAIN_WIDTH:MAIN_WIDTH + 2 * N_HEADS],
                                  dw_cat[:, o_z:MAIN_WIDTH]], axis=1)
    p_in = jnp.transpose(dw_in_full.reshape(D_MODEL, N_DEV, shard_in), (1, 0, 2)).astype(BF16)
    p_out = dw_out.reshape(N_DEV, D_MODEL // N_DEV, D_MODEL).astype(BF16)
    p_ff1 = jnp.transpose(dw_ff1.reshape(D_MODEL, N_DEV, D_FF // N_DEV), (1, 0, 2)).astype(BF16)
    p_ff2 = dw_ff2.reshape(N_DEV, D_FF // N_DEV, D_MODEL).astype(BF16)
    r_in, r_out, r_ff1, r_ff2 = exchange([p_in, p_out, p_ff1, p_ff2], gather=False, name="scatter_grads")
    g_w_in, d_w_in, nm_w_in, nv_w_in = adamw_reduce(r_in, w_in[0], m_w_in[0], v_w_in[0], "adamw_w_in")
    g_w_out, d_w_out, nm_w_out, nv_w_out = adamw_reduce(r_out, w_out[0], m_w_out[0], v_w_out[0], "adamw_w_out")
    g_w_ff1, d_w_ff1, nm_w_ff1, nv_w_ff1 = adamw_reduce(r_ff1, w_ff1[0], m_w_ff1[0], v_w_ff1[0], "adamw_w_ff1")
    g_w_ff2, d_w_ff2, nm_w_ff2, nv_w_ff2 = adamw_reduce(r_ff2, w_ff2[0], m_w_ff2[0], v_w_ff2[0], "adamw_w_ff2")

    d_lb = jnp.stack([dl0.reshape(GDN_WIDTH), dl1.reshape(GDN_WIDTH)])
    d_conv_full = jnp.concatenate([dwc_q, dwc_k, dwc_v], axis=1)
    small_shapes = [(1, N_HEADS), (1, N_HEADS), (1, HEAD_DIM), (2, GDN_WIDTH), (1, HEAD_DIM), (1, D_MODEL),
                    (1, D_MODEL), (D_MODEL,), (4, QKV_WIDTH)]
    small = _pack([d_alog_l[:, 0, 0], d_dt_l[:, 0, 0], d_gnw, d_lb, d_hnw, d_mix, d_ffn, d_final, d_conv_full])
    red = allreduce_small(small, "allreduce_small")
    g_alog, g_dt, g_gnw, g_lb, g_hnw, g_mix, g_ffn, g_final, g_conv_full = _unpack(red, small_shapes)
    g_conv = lax.dynamic_slice(g_conv_full, (0, me * shard_conv), (4, shard_conv)).reshape(1, 4, shard_conv)
    small_g = [g_alog, g_dt, g_gnw, g_lb, g_hnw, g_mix, g_ffn, g_final, g_conv]
    small_w = [gdn_a_log, gdn_dt_bias, gdn_norm_w, hgrn_lb_logits, hgrn_norm_w, norm_mix_w, norm_ffn_w, norm_final_w, conv_w]
    small_m = [m_gdn_a_log, m_gdn_dt_bias, m_gdn_norm_w, m_hgrn_lb_logits, m_hgrn_norm_w, m_norm_mix_w, m_norm_ffn_w,
               m_norm_final_w, m_conv_w]
    small_v = [v_gdn_a_log, v_gdn_dt_bias, v_gdn_norm_w, v_hgrn_lb_logits, v_hgrn_norm_w, v_norm_mix_w, v_norm_ffn_w,
               v_norm_final_w, v_conv_w]
    shapes = [a.shape for a in small_w]
    d_s, m_s, v_s = adamw_small(_pack(small_w), _pack(small_g), _pack(small_m), _pack(small_v), "adamw_small")
    d_alog, d_dt, d_gn, d_lbl, d_hn, d_nm, d_nf, d_nfin, d_cw = _unpack(d_s, shapes)
    m_alog, m_dt, m_gn, m_lbl, m_hn, m_nm, m_nf, m_nfin, m_cw = _unpack(m_s, shapes)
    v_alog, v_dt, v_gn, v_lbl, v_hn, v_nm, v_nf, v_nfin, v_cw = _unpack(v_s, shapes)

    loss = lax.psum(loss_sum[0, 0], ("x", "y", "c"))
    lead = lambda a: a[None]
    grads = [lead(g_w_in), g_conv, g_alog, g_dt, g_gnw, g_lb, g_hnw, lead(g_w_out), g_mix, g_ffn,
             lead(g_w_ff1), lead(g_w_ff2), g_final]
    deltas = [lead(d_w_in), d_cw, d_alog, d_dt, d_gn, d_lbl, d_hn, lead(d_w_out), d_nm, d_nf,
              lead(d_w_ff1), lead(d_w_ff2), d_nfin]
    new_m = [lead(nm_w_in), m_cw, m_alog, m_dt, m_gn, m_lbl, m_hn, lead(nm_w_out), m_nm, m_nf,
             lead(nm_w_ff1), lead(nm_w_ff2), m_nfin]
    new_v = [lead(nv_w_in), v_cw, v_alog, v_dt, v_gn, v_lbl, v_hn, lead(nv_w_out), v_nm, v_nf,
             lead(nv_w_ff1), lead(nv_w_ff2), v_nfin]
    return (loss, dx[None], *grads, *deltas, *new_m, *new_v)
```

```python
import functools

import jax
import jax.numpy as jnp
from jax import lax
from jax.experimental import pallas as pl
from jax.experimental.pallas import tpu as pltpu

F32 = jnp.float32
BF16 = jnp.bfloat16
HI = lax.Precision.HIGHEST

N_DEV = 8
D_MODEL = 2048
CHUNK = 64
SUB_CHUNK = 16
HEAD_DIM = 128
N_HEADS = 8
GDN_WIDTH = N_HEADS * HEAD_DIM
D_FF = 4 * D_MODEL
QKV_WIDTH = 3 * GDN_WIDTH
MAIN_WIDTH = 8 * GDN_WIDTH
CAT_WIDTH = MAIN_WIDTH + 128
IN_PROJ_WIDTH = MAIN_WIDTH + 2 * N_HEADS
AB_BLOCK = MAIN_WIDTH // 128
NORM_EPS = 1e-6
L2_EPS = 1e-6
LANES = 128
VMEM_LIMIT = 56 * 1024 * 1024

ADAM_LR = 0.001
ADAM_B1 = 0.9
ADAM_B2 = 0.999
ADAM_EPS = 1e-08
ADAM_WD = 0.01
ADAM_STEP = 10

MESH = pl.DeviceIdType.MESH


def _params(sem=None):
    return pltpu.CompilerParams(dimension_semantics=sem, vmem_limit_bytes=VMEM_LIMIT)


def _dot(a, b, dims, prec=None):
    return lax.dot_general(a, b, (dims, ((), ())), precision=prec, preferred_element_type=F32)


NN = ((1,), (0,))
NT = ((1,), (1,))
TN = ((0,), (0,))


def _split_bf16(x, pieces):
    out = []
    for _ in range(pieces - 1):
        p = x.astype(BF16)
        out.append(p)
        x = x - p.astype(F32)
    out.append(x.astype(BF16))
    return out


def _mm_raw(a, b, dims, prec):
    if prec == "hi":
        return _dot(a, b, dims, HI)
    if prec == "bf":
        return _dot(a.astype(BF16), b.astype(BF16), dims)
    a_hi, a_lo = _split_bf16(a, 2)
    b_hi, b_lo = _split_bf16(b, 2)
    return _dot(a_hi, b_hi, dims) + (_dot(a_hi, b_lo, dims) + _dot(a_lo, b_hi, dims))


@functools.partial(jax.custom_vjp, nondiff_argnums=(2, 3))
def mm(a, b, dims, prec):
    return _mm_raw(a, b, dims, prec)


def _mm_fwd(a, b, dims, prec):
    return _mm_raw(a, b, dims, prec), (a, b)


def _mm_bwd(dims, prec, res, ct):
    a, b = res
    if dims == NN:
        return _mm_raw(ct, b, NT, prec), _mm_raw(a, ct, TN, prec)
    if dims == NT:
        return _mm_raw(ct, b, NN, prec), _mm_raw(ct, a, TN, prec)
    return _mm_raw(b, ct, NT, prec), _mm_raw(a, ct, NN, prec)


mm.defvjp(_mm_fwd, _mm_bwd)


def _sel_raw(sel, x, dims):
    sel = sel.astype(BF16)
    p0, p1, p2 = _split_bf16(x, 3)
    return _dot(sel, p0, dims) + (_dot(sel, p1, dims) + _dot(sel, p2, dims))


@jax.custom_vjp
def sel_mm3(sel, x):
    c = x.shape[0]
    full = _sel_raw(sel, x, NN)
    return full[:c], full[c:2 * c], full[2 * c:]


def _sel_fwd(sel, x):
    return sel_mm3(sel, x), sel


def _sel_bwd(sel, cts):
    return jnp.zeros_like(sel), _sel_raw(sel, jnp.concatenate(cts, axis=0), TN)


sel_mm3.defvjp(_sel_fwd, _sel_bwd)


def _my_flat():
    return 4 * lax.axis_index("x") + 2 * lax.axis_index("y") + lax.axis_index("c")


def _peer(k):
    x, y, c = lax.axis_index("x"), lax.axis_index("y"), lax.axis_index("c")
    kx, ky, kc = (k >> 2) & 1, (k >> 1) & 1, k & 1
    px = (1 - x) if kx else x
    py = (1 - y) if ky else y
    pc = (1 - c) if kc else c
    return (px, py, pc), 4 * px + 2 * py + pc


def exchange(xs, gather, name):
    n = len(xs)

    def body(*refs):
        x_refs, y_refs = refs[:n], refs[n:2 * n]
        send_sems, recv_sems, local_sems = refs[2 * n:]
        me = _my_flat()
        local, sends = [], []
        for a in range(n):
            src = x_refs[a] if gather else x_refs[a].at[me]
            cp = pltpu.make_async_copy(src, y_refs[a].at[me], local_sems.at[a])
            cp.start()
            local.append(cp)
        for k in range(1, N_DEV):
            peer, peer_flat = _peer(k)
            for a in range(n):
                src = x_refs[a] if gather else x_refs[a].at[peer_flat]
                cp = pltpu.make_async_remote_copy(
                    src_ref=src, dst_ref=y_refs[a].at[me],
                    send_sem=send_sems.at[a, k], recv_sem=recv_sems.at[a, k],
                    device_id=peer, device_id_type=MESH)
                cp.start()
                sends.append(cp)
        for k in range(1, N_DEV):
            _, peer_flat = _peer(k)
            for a in range(n):
                src = x_refs[a] if gather else x_refs[a].at[peer_flat]
                pltpu.make_async_remote_copy(
                    src_ref=src, dst_ref=y_refs[a].at[peer_flat],
                    send_sem=send_sems.at[a, k], recv_sem=recv_sems.at[a, k],
                    device_id=_peer(k)[0], device_id_type=MESH).wait_recv()
        for cp in sends:
            cp.wait_send()
        for cp in local:
            cp.wait()

    out_shape = [jax.ShapeDtypeStruct(((N_DEV,) + x.shape) if gather else x.shape, x.dtype) for x in xs]
    any_spec = pl.BlockSpec(memory_space=pl.ANY)
    return pl.pallas_call(
        body, name=name, out_shape=out_shape,
        in_specs=[any_spec] * n, out_specs=[any_spec] * n,
        scratch_shapes=[pltpu.SemaphoreType.DMA((n, N_DEV)), pltpu.SemaphoreType.DMA((n, N_DEV)),
                        pltpu.SemaphoreType.DMA((n,))],
    )(*xs)


def allreduce_small(x, name):
    rows = x.shape[0]

    def body(x_ref, o_ref, buf, send_sems, recv_sems):
        me = _my_flat()
        buf[me] = x_ref[...]
        sends = []
        for k in range(1, N_DEV):
            peer, _ = _peer(k)
            cp = pltpu.make_async_remote_copy(
                src_ref=x_ref, dst_ref=buf.at[me], send_sem=send_sems.at[k], recv_sem=recv_sems.at[k],
                device_id=peer, device_id_type=MESH)
            cp.start()
            sends.append(cp)
        for k in range(1, N_DEV):
            peer, peer_flat = _peer(k)
            pltpu.make_async_remote_copy(
                src_ref=x_ref, dst_ref=buf.at[peer_flat], send_sem=send_sems.at[k], recv_sem=recv_sems.at[k],
                device_id=peer, device_id_type=MESH).wait_recv()
        for cp in sends:
            cp.wait_send()
        acc = buf[0]
        for d in range(1, N_DEV):
            acc = acc + buf[d]
        o_ref[...] = acc

    vmem = pl.BlockSpec(memory_space=pltpu.VMEM)
    return pl.pallas_call(
        body, name=name, out_shape=jax.ShapeDtypeStruct((rows, LANES), F32),
        in_specs=[vmem], out_specs=vmem,
        scratch_shapes=[pltpu.VMEM((N_DEV, rows, LANES), F32),
                        pltpu.SemaphoreType.DMA((N_DEV,)), pltpu.SemaphoreType.DMA((N_DEV,))],
    )(x)


def matmul(a, b, mode, name, out_dtypes=(F32,), epilogue=None, extra=None, tm=1024, tn=1024, tk=2048):
    if mode == "nn":
        (m, kd), n = a.shape, b.shape[1]
    elif mode == "nt":
        (m, kd), n = a.shape, b.shape[0]
    else:
        (kd, m), n = a.shape, b.shape[1]
    tm, tn, tk = min(tm, m), min(tn, n), min(tk, kd)
    assert m % tm == 0 and n % tn == 0 and kd % tk == 0, (name, m, n, kd, tm, tn, tk)
    ksteps = kd // tk
    dims = {"nn": NN, "nt": NT, "tn": TN}[mode]
    n_out = len(out_dtypes)

    def body(*refs):
        a_ref, b_ref = refs[0], refs[1]
        e_ref = refs[2] if extra is not None else None
        o_refs = refs[2 + (extra is not None):2 + (extra is not None) + n_out]
        acc_ref = refs[-1]
        kk = pl.program_id(2)

        @pl.when(kk == 0)
        def _():
            acc_ref[...] = jnp.zeros_like(acc_ref)

        acc_ref[...] += _dot(a_ref[...], b_ref[...], dims)

        @pl.when(kk == ksteps - 1)
        def _():
            acc = acc_ref[...]
            if epilogue is None:
                outs = (acc,)
            else:
                outs = epilogue(acc, e_ref[...] if e_ref is not None else None)
            for o_ref, o in zip(o_refs, outs):
                o_ref[...] = o.astype(o_ref.dtype)

    if mode == "nn":
        a_spec = pl.BlockSpec((tm, tk), lambda i, j, k: (i, k))
        b_spec = pl.BlockSpec((tk, tn), lambda i, j, k: (k, j))
    elif mode == "nt":
        a_spec = pl.BlockSpec((tm, tk), lambda i, j, k: (i, k))
        b_spec = pl.BlockSpec((tn, tk), lambda i, j, k: (j, k))
    else:
        a_spec = pl.BlockSpec((tk, tm), lambda i, j, k: (k, i))
        b_spec = pl.BlockSpec((tk, tn), lambda i, j, k: (k, j))
    o_spec = pl.BlockSpec((tm, tn), lambda i, j, k: (i, j))
    in_specs = [a_spec, b_spec] + ([o_spec] if extra is not None else [])
    args = (a, b) + ((extra,) if extra is not None else ())
    res = pl.pallas_call(
        body, name=name, grid=(m // tm, n // tn, ksteps),
        in_specs=in_specs, out_specs=[o_spec] * n_out,
        out_shape=[jax.ShapeDtypeStruct((m, n), dt) for dt in out_dtypes],
        scratch_shapes=[pltpu.VMEM((tm, tn), F32)],
        compiler_params=_params(("parallel", "parallel", "arbitrary")),
    )(*args)
    return res if n_out > 1 else res[0]


ROW_BLOCK = 256


def rms_fwd(x, w, add, name):
    t, d = x.shape
    has_add = add is not None

    def body(*refs):
        x_ref, w_ref = refs[0], refs[1]
        rest = refs[2:]
        if has_add:
            add_ref, h_ref, n_ref, r_ref = rest
            h = x_ref[...] + add_ref[...]
            h_ref[...] = h
        else:
            n_ref, r_ref = rest
            h = x_ref[...]
        r = lax.rsqrt(jnp.mean(h * h, axis=-1, keepdims=True) + NORM_EPS)
        n_ref[...] = (h * r * w_ref[...]).astype(BF16)
        r_ref[...] = r

    row = pl.BlockSpec((ROW_BLOCK, d), lambda i: (i, 0))
    wspec = pl.BlockSpec((1, d), lambda i: (0, 0))
    rspec = pl.BlockSpec((ROW_BLOCK, 1), lambda i: (i, 0))
    in_specs = [row, wspec] + ([row] if has_add else [])
    out_specs = ([row] if has_add else []) + [row, rspec]
    out_shape = ([jax.ShapeDtypeStruct((t, d), F32)] if has_add else []) + [
        jax.ShapeDtypeStruct((t, d), BF16), jax.ShapeDtypeStruct((t, 1), F32)]
    args = (x, w) + ((add,) if has_add else ())
    return pl.pallas_call(body, name=name, grid=(t // ROW_BLOCK,), in_specs=in_specs, out_specs=out_specs,
                          out_shape=out_shape, compiler_params=_params(("parallel",)))(*args)


def loss_head(h1, delta, w, target, name):
    t, d = h1.shape

    def body(h_ref, dl_ref, w_ref, t_ref, loss_ref, dh_ref, dhb_ref, dw_ref):
        @pl.when(pl.program_id(0) == 0)
        def _():
            loss_ref[...] = jnp.zeros_like(loss_ref)
            dw_ref[...] = jnp.zeros_like(dw_ref)

        h = h_ref[...] + dl_ref[...]
        wv = w_ref[...]
        r = lax.rsqrt(jnp.mean(h * h, axis=-1, keepdims=True) + NORM_EPS)
        yn = h * r
        e = yn * wv - t_ref[...]
        loss_ref[...] += 0.5 * jnp.sum(jnp.sum(e * e, axis=-1, keepdims=True), axis=0, keepdims=True) / d
        dy = e / d
        dw_ref[...] += jnp.sum(dy * yn, axis=0, keepdims=True)
        dyn = dy * wv
        dh = r * (dyn - yn * jnp.mean(dyn * yn, axis=-1, keepdims=True))
        dh_ref[...] = dh
        dhb_ref[...] = dh.astype(BF16)

    row = pl.BlockSpec((ROW_BLOCK, d), lambda i: (i, 0))
    wspec = pl.BlockSpec((1, d), lambda i: (0, 0))
    one = pl.BlockSpec((1, 1), lambda i: (0, 0))
    return pl.pallas_call(
        body, name=name, grid=(t // ROW_BLOCK,),
        in_specs=[row, row, wspec, row], out_specs=[one, row, row, wspec],
        out_shape=[jax.ShapeDtypeStruct((1, 1), F32), jax.ShapeDtypeStruct((t, d), F32),
                   jax.ShapeDtypeStruct((t, d), BF16), jax.ShapeDtypeStruct((1, d), F32)],
        compiler_params=_params(("arbitrary",)))(h1, delta, w, target)


def rms_bwd(h, r, w, dn, dres, name):
    t, d = h.shape

    def body(h_ref, r_ref, w_ref, dn_ref, dres_ref, dh_ref, dhb_ref, dw_ref):
        @pl.when(pl.program_id(0) == 0)
        def _():
            dw_ref[...] = jnp.zeros_like(dw_ref)

        rv = r_ref[...]
        yn = h_ref[...] * rv
        dnv = dn_ref[...]
        dw_ref[...] += jnp.sum(dnv * yn, axis=0, keepdims=True)
        dyn = dnv * w_ref[...]
        dh = dres_ref[...] + rv * (dyn - yn * jnp.mean(dyn * yn, axis=-1, keepdims=True))
        dh_ref[...] = dh
        dhb_ref[...] = dh.astype(BF16)

    row = pl.BlockSpec((ROW_BLOCK, d), lambda i: (i, 0))
    wspec = pl.BlockSpec((1, d), lambda i: (0, 0))
    rspec = pl.BlockSpec((ROW_BLOCK, 1), lambda i: (i, 0))
    return pl.pallas_call(
        body, name=name, grid=(t // ROW_BLOCK,),
        in_specs=[row, rspec, wspec, row, row], out_specs=[row, row, wspec],
        out_shape=[jax.ShapeDtypeStruct((t, d), F32), jax.ShapeDtypeStruct((t, d), BF16),
                   jax.ShapeDtypeStruct((1, d), F32)],
        compiler_params=_params(("arbitrary",)))(h, r, w, dn, dres)


CONV_TB = 512
CONV_CB = 512
HALO = 8


def _silu(x):
    return x * jax.nn.sigmoid(x)


def _conv_pre(xcat, w, rows):
    acc = None
    for j in range(4):
        sh = 3 - j
        xs = xcat if sh == 0 else pltpu.roll(xcat, sh, 0)
        term = xs[HALO:HALO + rows] * w[j:j + 1, :]
        acc = term if acc is None else acc + term
    return acc


def conv_fwd(proj, conv_w, name):
    t = proj.shape[0]
    nb = CONV_TB // HALO

    def body(x_ref, prev_ref, w_ref, o_ref):
        prev = jnp.where(pl.program_id(1) == 0, 0.0, prev_ref[...])
        xcat = jnp.concatenate([prev, x_ref[...]], axis=0)
        o_ref[...] = _silu(_conv_pre(xcat, w_ref[...], CONV_TB))

    return pl.pallas_call(
        body, name=name, grid=(QKV_WIDTH // CONV_CB, t // CONV_TB),
        in_specs=[pl.BlockSpec((CONV_TB, CONV_CB), lambda c, i: (i, c)),
                  pl.BlockSpec((HALO, CONV_CB), lambda c, i: (jnp.maximum(i * nb - 1, 0), c)),
                  pl.BlockSpec((4, CONV_CB), lambda c, i: (0, c))],
        out_specs=pl.BlockSpec((CONV_TB, CONV_CB), lambda c, i: (i, c)),
        out_shape=jax.ShapeDtypeStruct((t, QKV_WIDTH), F32),
        compiler_params=_params(("parallel", "parallel")))(proj, proj, conv_w)


def conv_bwd(proj, dout, conv_w, col0, name):
    t = proj.shape[0]
    nb = CONV_TB // HALO
    nt = t // CONV_TB
    cb0 = col0 // CONV_CB
    rows = CONV_TB + HALO

    def body(x_ref, prev_ref, next_ref, d_ref, dnext_ref, w_ref, dx_ref, dw_ref):
        i = pl.program_id(1)

        @pl.when(i == 0)
        def _():
            dw_ref[...] = jnp.zeros_like(dw_ref)

        w = w_ref[...]
        prev = jnp.where(i == 0, 0.0, prev_ref[...])
        last = i == nt - 1
        xcat = jnp.concatenate([prev, x_ref[...], next_ref[...]], axis=0)
        pre = _conv_pre(xcat, w, rows)
        dcat = jnp.concatenate([d_ref[...], jnp.where(last, 0.0, dnext_ref[...])], axis=0)
        sg = jax.nn.sigmoid(pre)
        dpre = dcat * (sg * (1.0 + pre * (1.0 - sg)))
        dx = None
        for j in range(4):
            sh = 3 - j
            ds = dpre if sh == 0 else pltpu.roll(dpre, rows - sh, 0)
            term = ds[:CONV_TB] * w[j:j + 1, :]
            dx = term if dx is None else dx + term
        dx_ref[...] = dx.astype(BF16)
        dcur = dpre[:CONV_TB]
        parts = []
        for j in range(4):
            sh = 3 - j
            xs = xcat if sh == 0 else pltpu.roll(xcat, sh, 0)
            parts.append(jnp.sum(dcur * xs[HALO:HALO + CONV_TB], axis=0, keepdims=True))
        dw_ref[...] += jnp.concatenate(parts, axis=0)

    return pl.pallas_call(
        body, name=name, grid=(GDN_WIDTH // CONV_CB, nt),
        in_specs=[pl.BlockSpec((CONV_TB, CONV_CB), lambda c, i: (i, cb0 + c)),
                  pl.BlockSpec((HALO, CONV_CB), lambda c, i: (jnp.maximum(i * nb - 1, 0), cb0 + c)),
                  pl.BlockSpec((HALO, CONV_CB), lambda c, i: (jnp.minimum((i + 1) * nb, nt * nb - 1), cb0 + c)),
                  pl.BlockSpec((CONV_TB, CONV_CB), lambda c, i: (i, c)),
                  pl.BlockSpec((HALO, CONV_CB), lambda c, i: (jnp.minimum((i + 1) * nb, nt * nb - 1), c)),
                  pl.BlockSpec((4, CONV_CB), lambda c, i: (0, cb0 + c))],
        out_specs=[pl.BlockSpec((CONV_TB, CONV_CB), lambda c, i: (i, c)),
                   pl.BlockSpec((4, CONV_CB), lambda c, i: (0, c))],
        out_shape=[jax.ShapeDtypeStruct((t, GDN_WIDTH), BF16), jax.ShapeDtypeStruct((4, GDN_WIDTH), F32)],
        compiler_params=_params(("parallel", "arbitrary")))(proj, proj, proj, dout, dout, conv_w)


def _iota2(shape, axis):
    return lax.broadcasted_iota(jnp.int32, shape, axis)


def _softplus(x):
    return jnp.maximum(x, 0.0) + jnp.log(1.0 + jnp.exp(-jnp.abs(x)))


def _head_norm_gate(o, norm_w, gate):
    return o * lax.rsqrt(jnp.mean(o * o, axis=-1, keepdims=True) + NORM_EPS) * norm_w * _silu(gate)


GDN_PREC = ("bf", "bf")
HGRN_PREC = "bf"


def _each(fn, *cols):
    return [fn(*a) for a in zip(*cols)]


def gdn_chunks(hs, qc, kc, vc, zc, ab, a_log_l, dt_l, norm_w, s, prec=GDN_PREC):
    p_inv, p_mm = prec
    c = CHUNK
    ri, ci = _iota2((c, c), 0), _iota2((c, c), 1)
    incl, strict, eye = ri >= ci, ri > ci, ri == ci
    lane = _iota2((c, LANES), 1)
    last_row = _iota2((c, 1), 0) == c - 1
    rowsum = lambda x: jnp.sum(x, axis=1, keepdims=True)

    def row(col):
        return jnp.sum(jnp.where(eye, col, 0.0), axis=0, keepdims=True)

    q = _each(lambda x: x * lax.rsqrt(rowsum(x * x) + L2_EPS) * (HEAD_DIM ** -0.5), qc)
    k = _each(lambda x: x * lax.rsqrt(rowsum(x * x) + L2_EPS), kc)
    a_col = [rowsum(jnp.where(lane == h, ab, 0.0)) for h in hs]
    b_col = [rowsum(jnp.where(lane == h + N_HEADS, ab, 0.0)) for h in hs]
    beta = _each(jax.nn.sigmoid, b_col)
    g = _each(lambda a, al, dl: rowsum(jnp.where(lane == 0, -jnp.exp(al) * _softplus(a + dl), 0.0)), a_col, a_log_l, dt_l)
    gcum = _each(lambda x: rowsum(jnp.where(incl, row(x), 0.0)), g)
    g_last = _each(lambda x: jnp.sum(jnp.where(last_row, x, 0.0), axis=0, keepdims=True), gcum)
    decay = _each(lambda x: jnp.exp(jnp.where(incl, x - row(x), -jnp.inf)), gcum)
    kk = _each(lambda x: mm(x, x, NT, p_mm), k)
    low = _each(lambda b, x, d: jnp.where(strict, b * x * d, 0.0), beta, kk, decay)
    power = _each(lambda x: -x, low)
    inv = _each(lambda x: jnp.where(eye, 1.0, 0.0) + x, power)
    for _ in range(5):
        power = _each(lambda x: mm(x, x, NN, p_inv), power)
        inv = _each(lambda x, p: x + mm(x, p, NN, p_inv), inv, power)
    exp_g = _each(jnp.exp, gcum)
    u_v = _each(lambda i, b, x: mm(i, b * x, NN, p_mm), inv, beta, vc)
    w = _each(lambda i, b, e, x: mm(i, b * e * x, NN, p_mm), inv, beta, exp_g, k)
    attn = _each(lambda x, y, d: mm(x, y, NT, p_mm) * d, q, k, decay)
    u = _each(lambda x, y, z: x - mm(y, z, NN, p_mm), u_v, w, s)
    o = _each(lambda x, e, z: mm(x * e, z, NN, p_mm), q, exp_g, s)
    o = _each(lambda x, a, y: x + mm(a, y, NN, p_mm), o, attn, u)
    k_end = _each(lambda x, gl, gc: x * jnp.exp(gl - gc), k, g_last, gcum)
    s_new = _each(lambda z, gl, x, y: z * jnp.exp(gl) + mm(x, y, TN, p_mm), s, g_last, k_end, u)
    return _each(lambda x, z: _head_norm_gate(x, norm_w, z), o, zc), s_new


def gdn_chunk(h, qc, kc, vc, zc, ab, a_log_l, dt_l, norm_w, s, prec=GDN_PREC):
    y, s_new = gdn_chunks([h], [qc], [kc], [vc], [zc], ab, [a_log_l], [dt_l], norm_w, [s], prec)
    return y[0], s_new[0]


@functools.partial(jax.custom_vjp, nondiff_argnums=(1,))
def _sroll(x, shift):
    return x if shift == 0 else pltpu.roll(x, shift, 0)


def _sroll_fwd(x, shift):
    return _sroll(x, shift), None


def _sroll_bwd(shift, _, ct):
    return (ct if shift == 0 else pltpu.roll(ct, ct.shape[0] - shift, 0),)


_sroll.defvjp(_sroll_fwd, _sroll_bwd)


def hgrn_chunks(qb, fb, ib, gb, l0, l1, norm_w, st, prec=HGRN_PREC):
    c = CHUNK
    ri, ci = _iota2((3 * c, c), 0), _iota2((3 * c, c), 1)
    rcol = _iota2((c, 1), 0)
    blk0 = jnp.bitwise_and(ri, c - SUB_CHUNK)
    limit = jnp.where(ri < c, ri + 1, jnp.where(ri < 2 * c, blk0, blk0 + SUB_CHUNK))
    sel = jnp.where(ci < limit, 1.0, 0.0)
    ci = _iota2((c, c), 1)
    lb = _each(lambda a, b: jax.nn.sigmoid(a - b), l0, l1)
    forget = _each(lambda b, f: b + (1.0 - b) * jax.nn.sigmoid(f), lb, fb)
    key = _each(lambda b, f: (1.0 - b) * jax.nn.sigmoid(-f), lb, fb)
    q = _each(_silu, qb)
    v = ib
    logf = _each(jnp.log, forget)
    sums = _each(lambda x: sel_mm3(sel, x), logf)
    bc, b_start, b_end = [x[0] for x in sums], [x[1] for x in sums], [x[2] for x in sums]
    b_last = _each(lambda x: jnp.sum(x, axis=0, keepdims=True), logf)
    o = _each(lambda x, b, z: mm(x * jnp.exp(b), z, NT, prec), q, bc, st)
    rmod = jnp.bitwise_and(rcol, SUB_CHUNK - 1)
    for off in range(SUB_CHUNK):
        def diag(acc, x, ky, b, val):
            e = jnp.exp(jnp.where(rmod >= off, b - _sroll(b, off), -jnp.inf))
            a_o = jnp.sum(x * _sroll(ky, off) * e, axis=-1, keepdims=True)
            return acc + a_o * _sroll(val, off)
        o = _each(diag, o, q, key, bc, v)
    q_rel = _each(lambda x, b, bs: x * jnp.exp(b - bs), q, bc, b_start)
    k_rel = _each(lambda x, b, be: x * jnp.exp(be - b), key, bc, b_end)
    for y in range(c // SUB_CHUNK - 1):
        def scaled(x, b, bs):
            end_y = jnp.sum(jnp.where(rcol == SUB_CHUNK * y + SUB_CHUNK - 1, b, 0.0), axis=0, keepdims=True)
            return x * jnp.exp(jnp.where(rcol >= SUB_CHUNK * (y + 1), bs - end_y, -jnp.inf))
        dq = _each(scaled, q_rel, bc, b_start)
        in_y = (ci >= SUB_CHUNK * y) & (ci < SUB_CHUNK * (y + 1))
        a_y = _each(lambda x, z: jnp.where(in_y, mm(x, z, NT, prec), 0.0), dq, k_rel)
        o = _each(lambda acc, a, val: acc + mm(a, val, NN, prec), o, a_y, v)
    k_state = _each(lambda x, bl, b: x * jnp.exp(bl - b), key, b_last, bc)
    st_new = _each(lambda z, bl, val, x: z * jnp.exp(bl) + mm(val, x, TN, prec), st, b_last, v, k_state)
    return _each(lambda x, z: _head_norm_gate(x, norm_w, z), o, gb), st_new


def hgrn_chunk(qb, fb, ib, gb, l0, l1, norm_w, st, prec=HGRN_PREC):
    y, st_new = hgrn_chunks([qb], [fb], [ib], [gb], [l0], [l1], norm_w, [st], prec)
    return y[0], st_new[0]


HEAD_VEC = (N_HEADS, 1, LANES)


class _Groups:
    def __init__(self, nc, hb, rev):
        self.nc, self.hb, self.ng, self.rev = nc, hb, N_HEADS // hb, rev

    def _c(self, c):
        return self.nc - 1 - c if self.rev else c

    def cols(self, slab):
        return pl.BlockSpec((CHUNK, self.hb * LANES), lambda c, g: (self._c(c), slab * self.ng + g))

    def tile(self, block):
        return pl.BlockSpec((CHUNK, LANES), lambda c, g: (self._c(c), block))

    def state(self):
        return pl.BlockSpec((None, self.hb, HEAD_DIM, HEAD_DIM), lambda c, g: (self._c(c), g, 0, 0))

    @staticmethod
    def whole(shape):
        return pl.BlockSpec(shape, lambda c, g: (0,) * len(shape))

    def head(self, g, j):
        return j if self.ng == 1 else g * self.hb + j


def _lanes(j):
    return slice(j * LANES, (j + 1) * LANES)


def gdn_fwd(qkv_c, proj, a_log_l, dt_l, norm_w, name, hb=8):
    t = qkv_c.shape[0]
    gr = _Groups(t // CHUNK, hb, rev=False)

    def body(q_ref, k_ref, v_ref, z_ref, ab_ref, al_ref, dt_ref, nw_ref, y_ref, hist_ref, s_ref):
        c, g = pl.program_id(0), pl.program_id(1)

        @pl.when(c == 0)
        def _():
            for j in range(hb):
                s_ref[gr.head(g, j)] = jnp.zeros((HEAD_DIM, HEAD_DIM), F32)

        hs = [gr.head(g, j) for j in range(hb)]
        heads = lambda ref: [ref[:, _lanes(j)] for j in range(hb)]
        s = [s_ref[h] for h in hs]
        for j in range(hb):
            hist_ref[j] = s[j]
        y, s_new = gdn_chunks(hs, heads(q_ref), heads(k_ref), heads(v_ref), heads(z_ref), ab_ref[...],
                              [al_ref[h] for h in hs], [dt_ref[h] for h in hs], nw_ref[...], s)
        for j in range(hb):
            y_ref[:, _lanes(j)] = y[j].astype(BF16)
            s_ref[hs[j]] = s_new[j]

    return pl.pallas_call(
        body, name=name, grid=(gr.nc, gr.ng),
        in_specs=[gr.cols(0), gr.cols(1), gr.cols(2), gr.cols(3), gr.tile(AB_BLOCK),
                  gr.whole(HEAD_VEC), gr.whole(HEAD_VEC), gr.whole((1, LANES))],
        out_specs=[gr.cols(0), gr.state()],
        out_shape=[jax.ShapeDtypeStruct((t, 2 * GDN_WIDTH), BF16),
                   jax.ShapeDtypeStruct((gr.nc, N_HEADS, HEAD_DIM, HEAD_DIM), F32)],
        scratch_shapes=[pltpu.VMEM((N_HEADS, HEAD_DIM, HEAD_DIM), F32)],
        compiler_params=_params(("arbitrary", "arbitrary")),
    )(qkv_c, qkv_c, qkv_c, proj, proj, a_log_l, dt_l, norm_w)


def gdn_bwd(qkv_c, proj, a_log_l, dt_l, norm_w, hist, dy, name, hb=4):
    t = qkv_c.shape[0]
    gr = _Groups(t // CHUNK, hb, rev=True)

    def body(q_ref, k_ref, v_ref, z_ref, ab_ref, al_ref, dt_ref, nw_ref, hist_ref, dy_ref,
             dq_ref, dk_ref, dv_ref, dz_ref, dab_ref, dal_ref, ddt_ref, dnw_ref, ds_ref, dab_acc):
        c, g = pl.program_id(0), pl.program_id(1)

        @pl.when((c == 0) & (g == 0))
        def _():
            dal_ref[...] = jnp.zeros_like(dal_ref)
            ddt_ref[...] = jnp.zeros_like(ddt_ref)
            dnw_ref[...] = jnp.zeros_like(dnw_ref)

        @pl.when(c == 0)
        def _():
            for j in range(hb):
                ds_ref[gr.head(g, j)] = jnp.zeros((HEAD_DIM, HEAD_DIM), F32)

        @pl.when(g == 0)
        def _():
            dab_acc[...] = jnp.zeros_like(dab_acc)

        hs = [gr.head(g, j) for j in range(hb)]
        heads = lambda ref: [ref[:, _lanes(j)] for j in range(hb)]
        _, vjp = jax.vjp(functools.partial(gdn_chunks, hs), heads(q_ref), heads(k_ref), heads(v_ref), heads(z_ref),
                         ab_ref[...], [al_ref[h] for h in hs], [dt_ref[h] for h in hs], nw_ref[...],
                         [hist_ref[j] for j in range(hb)])
        dq, dk, dv, dz, dab, dal, ddt, dnw, ds = vjp((heads(dy_ref), [ds_ref[h] for h in hs]))
        for j in range(hb):
            sl = _lanes(j)
            dq_ref[:, sl] = dq[j]
            dk_ref[:, sl] = dk[j]
            dv_ref[:, sl] = dv[j]
            dz_ref[:, sl] = dz[j].astype(BF16)
            dal_ref[hs[j]] += dal[j]
            ddt_ref[hs[j]] += ddt[j]
            ds_ref[hs[j]] = ds[j]
        dab_acc[...] += dab
        dnw_ref[...] += dnw

        @pl.when(g == gr.ng - 1)
        def _():
            dab_ref[...] = dab_acc[...].astype(BF16)

    col = gr.cols(0)
    return pl.pallas_call(
        body, name=name, grid=(gr.nc, gr.ng),
        in_specs=[gr.cols(0), gr.cols(1), gr.cols(2), gr.cols(3), gr.tile(AB_BLOCK),
                  gr.whole(HEAD_VEC), gr.whole(HEAD_VEC), gr.whole((1, LANES)), gr.state(), gr.cols(0)],
        out_specs=[col, col, col, col, gr.tile(0), gr.whole(HEAD_VEC), gr.whole(HEAD_VEC), gr.whole((1, LANES))],
        out_shape=[jax.ShapeDtypeStruct((t, GDN_WIDTH), F32)] * 3 + [
            jax.ShapeDtypeStruct((t, GDN_WIDTH), BF16), jax.ShapeDtypeStruct((t, LANES), BF16),
            jax.ShapeDtypeStruct(HEAD_VEC, F32), jax.ShapeDtypeStruct(HEAD_VEC, F32),
            jax.ShapeDtypeStruct((1, LANES), F32)],
        scratch_shapes=[pltpu.VMEM((N_HEADS, HEAD_DIM, HEAD_DIM), F32), pltpu.VMEM((CHUNK, LANES), F32)],
        compiler_params=_params(("arbitrary", "arbitrary")),
    )(qkv_c, qkv_c, qkv_c, proj, proj, a_log_l, dt_l, norm_w, hist, dy)


def hgrn_fwd(proj, l0, l1, norm_w, y, name, hb=8):
    t = proj.shape[0]
    gr = _Groups(t // CHUNK, hb, rev=False)

    def body(q_ref, f_ref, i_ref, g_ref, l0_ref, l1_ref, nw_ref, y_in, y_ref, hist_ref, s_ref):
        del y_in
        c, g = pl.program_id(0), pl.program_id(1)

        @pl.when(c == 0)
        def _():
            for j in range(hb):
                s_ref[gr.head(g, j)] = jnp.zeros((HEAD_DIM, HEAD_DIM), F32)

        hs = [gr.head(g, j) for j in range(hb)]
        heads = lambda ref: [ref[:, _lanes(j)] for j in range(hb)]
        s = [s_ref[h] for h in hs]
        for j in range(hb):
            hist_ref[j] = s[j]
        out, s_new = hgrn_chunks(heads(q_ref), heads(f_ref), heads(i_ref), heads(g_ref), [l0_ref[h] for h in hs],
                                 [l1_ref[h] for h in hs], nw_ref[...], s)
        for j in range(hb):
            y_ref[:, _lanes(j)] = out[j].astype(BF16)
            s_ref[hs[j]] = s_new[j]

    return pl.pallas_call(
        body, name=name, grid=(gr.nc, gr.ng),
        in_specs=[gr.cols(4), gr.cols(5), gr.cols(6), gr.cols(7), gr.whole(HEAD_VEC), gr.whole(HEAD_VEC),
                  gr.whole((1, LANES)), pl.BlockSpec(memory_space=pl.ANY)],
        out_specs=[gr.cols(1), gr.state()],
        out_shape=[jax.ShapeDtypeStruct((t, 2 * GDN_WIDTH), BF16),
                   jax.ShapeDtypeStruct((gr.nc, N_HEADS, HEAD_DIM, HEAD_DIM), F32)],
        scratch_shapes=[pltpu.VMEM((N_HEADS, HEAD_DIM, HEAD_DIM), F32)],
        input_output_aliases={7: 0},
        compiler_params=_params(("arbitrary", "arbitrary")),
    )(proj, proj, proj, proj, l0, l1, norm_w, y)


def hgrn_bwd(proj, l0, l1, norm_w, hist, dy, name, hb=4):
    t = proj.shape[0]
    gr = _Groups(t // CHUNK, hb, rev=True)

    def body(q_ref, f_ref, i_ref, g_ref, l0_ref, l1_ref, nw_ref, hist_ref, dy_ref,
             dq_ref, df_ref, di_ref, dg_ref, dl0_ref, dl1_ref, dnw_ref, ds_ref):
        c, g = pl.program_id(0), pl.program_id(1)

        @pl.when((c == 0) & (g == 0))
        def _():
            dl0_ref[...] = jnp.zeros_like(dl0_ref)
            dl1_ref[...] = jnp.zeros_like(dl1_ref)
            dnw_ref[...] = jnp.zeros_like(dnw_ref)

        @pl.when(c == 0)
        def _():
            for j in range(hb):
                ds_ref[gr.head(g, j)] = jnp.zeros((HEAD_DIM, HEAD_DIM), F32)

        hs = [gr.head(g, j) for j in range(hb)]
        heads = lambda ref: [ref[:, _lanes(j)] for j in range(hb)]
        _, vjp = jax.vjp(hgrn_chunks, heads(q_ref), heads(f_ref), heads(i_ref), heads(g_ref), [l0_ref[h] for h in hs],
                         [l1_ref[h] for h in hs], nw_ref[...], [hist_ref[j] for j in range(hb)])
        dq, df, di, dg, dl0, dl1, dnw, ds = vjp((heads(dy_ref), [ds_ref[h] for h in hs]))
        for j in range(hb):
            sl = _lanes(j)
            dq_ref[:, sl] = dq[j].astype(BF16)
            df_ref[:, sl] = df[j].astype(BF16)
            di_ref[:, sl] = di[j].astype(BF16)
            dg_ref[:, sl] = dg[j].astype(BF16)
            dl0_ref[hs[j]] += dl0[j]
            dl1_ref[hs[j]] += dl1[j]
            ds_ref[hs[j]] = ds[j]
        dnw_ref[...] += dnw

    col = gr.cols(0)
    return pl.pallas_call(
        body, name=name, grid=(gr.nc, gr.ng),
        in_specs=[gr.cols(4), gr.cols(5), gr.cols(6), gr.cols(7), gr.whole(HEAD_VEC), gr.whole(HEAD_VEC),
                  gr.whole((1, LANES)), gr.state(), gr.cols(1)],
        out_specs=[col, col, col, col, gr.whole(HEAD_VEC), gr.whole(HEAD_VEC), gr.whole((1, LANES))],
        out_shape=[jax.ShapeDtypeStruct((t, GDN_WIDTH), BF16)] * 4 + [
            jax.ShapeDtypeStruct(HEAD_VEC, F32), jax.ShapeDtypeStruct(HEAD_VEC, F32),
            jax.ShapeDtypeStruct((1, LANES), F32)],
        scratch_shapes=[pltpu.VMEM((N_HEADS, HEAD_DIM, HEAD_DIM), F32)],
        compiler_params=_params(("arbitrary", "arbitrary")),
    )(proj, proj, proj, proj, l0, l1, norm_w, hist, dy)


def _adamw(w, g, m, v):
    m = ADAM_B1 * m + (1.0 - ADAM_B1) * g
    v = ADAM_B2 * v + (1.0 - ADAM_B2) * jnp.square(g)
    m_hat = m / (1.0 - ADAM_B1 ** ADAM_STEP)
    v_hat = v / (1.0 - ADAM_B2 ** ADAM_STEP)
    delta = -ADAM_LR * (m_hat / (jnp.sqrt(v_hat) + ADAM_EPS) + ADAM_WD * w)
    return delta, m, v


def adamw_reduce(parts, w, m, v, name, rb=128):
    r, c = w.shape
    rb = min(rb, r)

    def body(p_ref, w_ref, m_ref, v_ref, g_ref, d_ref, mo_ref, vo_ref):
        g = p_ref[0].astype(F32)
        for d in range(1, N_DEV):
            g = g + p_ref[d].astype(F32)
        delta, mn, vn = _adamw(w_ref[...], g, m_ref[...], v_ref[...])
        g_ref[...] = g
        d_ref[...] = delta
        mo_ref[...] = mn
        vo_ref[...] = vn

    blk = pl.BlockSpec((rb, c), lambda i: (i, 0))
    return pl.pallas_call(
        body, name=name, grid=(r // rb,),
        in_specs=[pl.BlockSpec((N_DEV, rb, c), lambda i: (0, i, 0)), blk, blk, blk],
        out_specs=[blk] * 4, out_shape=[jax.ShapeDtypeStruct((r, c), F32)] * 4,
        compiler_params=_params(("parallel",)))(parts, w, m, v)


def adamw_small(w, g, m, v, name):
    def body(w_ref, g_ref, m_ref, v_ref, d_ref, mo_ref, vo_ref):
        delta, mn, vn = _adamw(w_ref[...], g_ref[...], m_ref[...], v_ref[...])
        d_ref[...] = delta
        mo_ref[...] = mn
        vo_ref[...] = vn

    vmem = pl.BlockSpec(memory_space=pltpu.VMEM)
    return pl.pallas_call(body, name=name, in_specs=[vmem] * 4, out_specs=[vmem] * 3,
                          out_shape=[jax.ShapeDtypeStruct(w.shape, F32)] * 3)(w, g, m, v)


def _pack(arrays):
    flat = jnp.concatenate([a.reshape(-1).astype(F32) for a in arrays])
    rows = -(-flat.shape[0] // (8 * LANES)) * 8
    return jnp.pad(flat, (0, rows * LANES - flat.shape[0])).reshape(rows, LANES)


def _unpack(packed, shapes):
    flat, out, off = packed.reshape(-1), [], 0
    for s in shapes:
        n = 1
        for d in s:
            n *= d
        out.append(flat[off:off + n].reshape(s))
        off += n
    return out


def _relu2_epilogue(acc, _):
    r = jnp.maximum(acc, 0.0)
    return acc, r * r


def _relu2_bwd_epilogue(acc, a1):
    return (acc * (2.0 * jnp.maximum(a1, 0.0)),)


def kernel(x, w_in, conv_w, gdn_a_log, gdn_dt_bias, gdn_norm_w, hgrn_lb_logits, hgrn_norm_w, w_out, norm_mix_w, norm_ffn_w, w_ff1, w_ff2, norm_final_w, loss_target, m_w_in, m_conv_w, m_gdn_a_log, m_gdn_dt_bias, m_gdn_norm_w, m_hgrn_lb_logits, m_hgrn_norm_w, m_w_out, m_norm_mix_w, m_norm_ffn_w, m_w_ff1, m_w_ff2, m_norm_final_w, v_w_in, v_conv_w, v_gdn_a_log, v_gdn_dt_bias, v_gdn_norm_w, v_hgrn_lb_logits, v_hgrn_norm_w, v_w_out, v_norm_mix_w, v_norm_ffn_w, v_w_ff1, v_w_ff2, v_norm_final_w):
    me = _my_flat()
    xs = x[0]
    target = loss_target[0]
    shard_in = w_in.shape[2]
    shard_conv = conv_w.shape[2]

    g_in, g_out, g_ff1, g_ff2, g_conv = exchange(
        [w_in[0].astype(BF16), w_out[0].astype(BF16), w_ff1[0].astype(BF16), w_ff2[0].astype(BF16), conv_w[0]],
        gather=True, name="gather_weights")
    w_in_full = jnp.transpose(g_in, (1, 0, 2)).reshape(D_MODEL, N_DEV * shard_in)
    o_z = 4 * GDN_WIDTH
    w_cat = jnp.concatenate([w_in_full[:, :o_z], w_in_full[:, o_z + 2 * N_HEADS:], w_in_full[:, o_z:o_z + 2 * N_HEADS],
                             jnp.zeros((D_MODEL, LANES - 2 * N_HEADS), BF16)], axis=1)
    w_out_full = g_out.reshape(D_MODEL, D_MODEL)
    w_ff1_full = jnp.transpose(g_ff1, (1, 0, 2)).reshape(D_MODEL, D_FF)
    w_ff2_full = g_ff2.reshape(D_FF, D_MODEL)
    conv_full = jnp.transpose(g_conv, (1, 0, 2)).reshape(4, QKV_WIDTH)

    lane_b = lambda p: jnp.broadcast_to(p.reshape(N_HEADS, 1, 1), HEAD_VEC)
    a_log_l, dt_l = lane_b(gdn_a_log[0]), lane_b(gdn_dt_bias[0])
    l0 = hgrn_lb_logits[0].reshape(HEAD_VEC)
    l1 = hgrn_lb_logits[1].reshape(HEAD_VEC)

    n1, r1 = rms_fwd(xs, norm_mix_w, None, "rms_mix")
    proj = matmul(n1, w_cat, "nn", "in_proj", tn=640)
    qkv_c = conv_fwd(proj, conv_full, "conv_fwd")
    y_half, hist_a = gdn_fwd(qkv_c, proj, a_log_l, dt_l, gdn_norm_w, "gdn_fwd")
    y, hist_b = hgrn_fwd(proj, l0, l1, hgrn_norm_w, y_half, "hgrn_fwd")
    mix = matmul(y, w_out_full, "nn", "out_proj")
    h1, n2, r2 = rms_fwd(xs, norm_ffn_w, mix, "rms_ffn")
    a1, act = matmul(n2, w_ff1_full, "nn", "ff1", out_dtypes=(F32, BF16), epilogue=_relu2_epilogue)
    ff = matmul(act, w_ff2_full, "nn", "ff2")
    loss_sum, dh2, dh2_b, d_final = loss_head(h1, ff, norm_final_w.reshape(1, D_MODEL), target, "loss_head")

    da1 = matmul(dh2_b, w_ff2_full, "nt", "d_act", out_dtypes=(BF16,), epilogue=_relu2_bwd_epilogue, extra=a1)
    dw_ff2 = matmul(act, dh2_b, "tn", "dw_ff2", tk=1024)
    dn2 = matmul(da1, w_ff1_full, "nt", "d_n2")
    dw_ff1 = matmul(n2, da1, "tn", "dw_ff1", tk=1024)
    dh1, dh1_b, d_ffn = rms_bwd(h1, r2, norm_ffn_w, dn2, dh2, "rms_ffn_bwd")
    dmix = matmul(dh1_b, w_out_full, "nt", "d_mix")
    dw_out = matmul(y, dh1_b, "tn", "dw_out", tk=1024)
    dq_c, dk_c, dv_c, dz, dab, d_alog_l, d_dt_l, d_gnw = gdn_bwd(
        qkv_c, proj, a_log_l, dt_l, gdn_norm_w, hist_a, dmix, "gdn_bwd")
    dqb, dfb, dib, dgb, dl0, dl1, d_hnw = hgrn_bwd(proj, l0, l1, hgrn_norm_w, hist_b, dmix, "hgrn_bwd")
    dq, dwc_q = conv_bwd(proj, dq_c, conv_full, 0, "conv_bwd_q")
    dk, dwc_k = conv_bwd(proj, dk_c, conv_full, GDN_WIDTH, "conv_bwd_k")
    dv, dwc_v = conv_bwd(proj, dv_c, conv_full, 2 * GDN_WIDTH, "conv_bwd_v")
    dproj = jnp.concatenate([dq, dk, dv, dz, dqb, dfb, dib, dgb, dab], axis=1)
    dn1 = matmul(dproj, w_cat, "nt", "d_n1", tk=CAT_WIDTH // 5)
    dw_cat = matmul(n1, dproj, "tn", "dw_in", tn=640, tk=1024)
    dx, _, d_mix = rms_bwd(xs, r1, norm_mix_w, dn1, dh1, "rms_mix_bwd")

    dw_in_full = jnp.concatenate([dw_cat[:, :o_z], dw_cat[:, MAIN_WIDTH:MAIN_WIDTH + 2 * N_HEADS],
                                  dw_cat[:, o_z:MAIN_WIDTH]], axis=1)
    p_in = jnp.transpose(dw_in_full.reshape(D_MODEL, N_DEV, shard_in), (1, 0, 2)).astype(BF16)
    p_out = dw_out.reshape(N_DEV, D_MODEL // N_DEV, D_MODEL).astype(BF16)
    p_ff1 = jnp.transpose(dw_ff1.reshape(D_MODEL, N_DEV, D_FF // N_DEV), (1, 0, 2)).astype(BF16)
    p_ff2 = dw_ff2.reshape(N_DEV, D_FF // N_DEV, D_MODEL).astype(BF16)
    r_in, r_out, r_ff1, r_ff2 = exchange([p_in, p_out, p_ff1, p_ff2], gather=False, name="scatter_grads")
    g_w_in, d_w_in, nm_w_in, nv_w_in = adamw_reduce(r_in, w_in[0], m_w_in[0], v_w_in[0], "adamw_w_in")
    g_w_out, d_w_out, nm_w_out, nv_w_out = adamw_reduce(r_out, w_out[0], m_w_out[0], v_w_out[0], "adamw_w_out")
    g_w_ff1, d_w_ff1, nm_w_ff1, nv_w_ff1 = adamw_reduce(r_ff1, w_ff1[0], m_w_ff1[0], v_w_ff1[0], "adamw_w_ff1")
    g_w_ff2, d_w_ff2, nm_w_ff2, nv_w_ff2 = adamw_reduce(r_ff2, w_ff2[0], m_w_ff2[0], v_w_ff2[0], "adamw_w_ff2")

    d_lb = jnp.stack([dl0.reshape(GDN_WIDTH), dl1.reshape(GDN_WIDTH)])
    d_conv_full = jnp.concatenate([dwc_q, dwc_k, dwc_v], axis=1)
    small_shapes = [(1, N_HEADS), (1, N_HEADS), (1, HEAD_DIM), (2, GDN_WIDTH), (1, HEAD_DIM), (1, D_MODEL),
                    (1, D_MODEL), (D_MODEL,), (4, QKV_WIDTH)]
    small = _pack([d_alog_l[:, 0, 0], d_dt_l[:, 0, 0], d_gnw, d_lb, d_hnw, d_mix, d_ffn, d_final, d_conv_full])
    red = allreduce_small(small, "allreduce_small")
    g_alog, g_dt, g_gnw, g_lb, g_hnw, g_mix, g_ffn, g_final, g_conv_full = _unpack(red, small_shapes)
    g_conv = lax.dynamic_slice(g_conv_full, (0, me * shard_conv), (4, shard_conv)).reshape(1, 4, shard_conv)
    small_g = [g_alog, g_dt, g_gnw, g_lb, g_hnw, g_mix, g_ffn, g_final, g_conv]
    small_w = [gdn_a_log, gdn_dt_bias, gdn_norm_w, hgrn_lb_logits, hgrn_norm_w, norm_mix_w, norm_ffn_w, norm_final_w, conv_w]
    small_m = [m_gdn_a_log, m_gdn_dt_bias, m_gdn_norm_w, m_hgrn_lb_logits, m_hgrn_norm_w, m_norm_mix_w, m_norm_ffn_w,
               m_norm_final_w, m_conv_w]
    small_v = [v_gdn_a_log, v_gdn_dt_bias, v_gdn_norm_w, v_hgrn_lb_logits, v_hgrn_norm_w, v_norm_mix_w, v_norm_ffn_w,
               v_norm_final_w, v_conv_w]
    shapes = [a.shape for a in small_w]
    d_s, m_s, v_s = adamw_small(_pack(small_w), _pack(small_g), _pack(small_m), _pack(small_v), "adamw_small")
    d_alog, d_dt, d_gn, d_lbl, d_hn, d_nm, d_nf, d_nfin, d_cw = _unpack(d_s, shapes)
    m_alog, m_dt, m_gn, m_lbl, m_hn, m_nm, m_nf, m_nfin, m_cw = _unpack(m_s, shapes)
    v_alog, v_dt, v_gn, v_lbl, v_hn, v_nm, v_nf, v_nfin, v_cw = _unpack(v_s, shapes)

    loss = lax.psum(loss_sum[0, 0], ("x", "y", "c"))
    lead = lambda a: a[None]
    grads = [lead(g_w_in), g_conv, g_alog, g_dt, g_gnw, g_lb, g_hnw, lead(g_w_out), g_mix, g_ffn,
             lead(g_w_ff1), lead(g_w_ff2), g_final]
    deltas = [lead(d_w_in), d_cw, d_alog, d_dt, d_gn, d_lbl, d_hn, lead(d_w_out), d_nm, d_nf,
              lead(d_w_ff1), lead(d_w_ff2), d_nfin]
    new_m = [lead(nm_w_in), m_cw, m_alog, m_dt, m_gn, m_lbl, m_hn, lead(nm_w_out), m_nm, m_nf,
             lead(nm_w_ff1), lead(nm_w_ff2), m_nfin]
    new_v = [lead(nv_w_in), v_cw, v_alog, v_dt, v_gn, v_lbl, v_hn, lead(nv_w_out), v_nm, v_nf,
             lead(nv_w_ff1), lead(nv_w_ff2), v_nfin]
    return (loss, dx[None], *grads, *deltas, *new_m, *new_v)
```

```python
import functools

import jax
import jax.numpy as jnp
from jax import lax
from jax.experimental import pallas as pl
from jax.experimental.pallas import tpu as pltpu

F32 = jnp.float32
BF16 = jnp.bfloat16
HI = lax.Precision.HIGHEST

N_DEV = 8
D_MODEL = 2048
CHUNK = 64
SUB_CHUNK = 16
HEAD_DIM = 128
N_HEADS = 8
GDN_WIDTH = N_HEADS * HEAD_DIM
D_FF = 4 * D_MODEL
QKV_WIDTH = 3 * GDN_WIDTH
MAIN_WIDTH = 8 * GDN_WIDTH
CAT_WIDTH = MAIN_WIDTH + 128
IN_PROJ_WIDTH = MAIN_WIDTH + 2 * N_HEADS
AB_BLOCK = MAIN_WIDTH // 128
NORM_EPS = 1e-6
L2_EPS = 1e-6
LANES = 128
VMEM_LIMIT = 56 * 1024 * 1024

ADAM_LR = 0.001
ADAM_B1 = 0.9
ADAM_B2 = 0.999
ADAM_EPS = 1e-08
ADAM_WD = 0.01
ADAM_STEP = 10

MESH = pl.DeviceIdType.MESH


def _params(sem=None):
    return pltpu.CompilerParams(dimension_semantics=sem, vmem_limit_bytes=VMEM_LIMIT)


def _dot(a, b, dims, prec=None):
    return lax.dot_general(a, b, (dims, ((), ())), precision=prec, preferred_element_type=F32)


NN = ((1,), (0,))
NT = ((1,), (1,))
TN = ((0,), (0,))


def _split_bf16(x, pieces):
    out = []
    for _ in range(pieces - 1):
        p = x.astype(BF16)
        out.append(p)
        x = x - p.astype(F32)
    out.append(x.astype(BF16))
    return out


def _mm_raw(a, b, dims, prec):
    if prec == "hi":
        return _dot(a, b, dims, HI)
    if prec == "bf":
        return _dot(a.astype(BF16), b.astype(BF16), dims)
    a_hi, a_lo = _split_bf16(a, 2)
    b_hi, b_lo = _split_bf16(b, 2)
    return _dot(a_hi, b_hi, dims) + (_dot(a_hi, b_lo, dims) + _dot(a_lo, b_hi, dims))


@functools.partial(jax.custom_vjp, nondiff_argnums=(2, 3))
def mm(a, b, dims, prec):
    return _mm_raw(a, b, dims, prec)


def _mm_fwd(a, b, dims, prec):
    return _mm_raw(a, b, dims, prec), (a, b)


def _mm_bwd(dims, prec, res, ct):
    a, b = res
    if dims == NN:
        return _mm_raw(ct, b, NT, prec), _mm_raw(a, ct, TN, prec)
    if dims == NT:
        return _mm_raw(ct, b, NN, prec), _mm_raw(ct, a, TN, prec)
    return _mm_raw(b, ct, NT, prec), _mm_raw(a, ct, NN, prec)


mm.defvjp(_mm_fwd, _mm_bwd)


def _sel_raw(sel, x, dims):
    sel = sel.astype(BF16)
    p0, p1, p2 = _split_bf16(x, 3)
    return _dot(sel, p0, dims) + (_dot(sel, p1, dims) + _dot(sel, p2, dims))


@jax.custom_vjp
def sel_mm3(sel, x):
    c = x.shape[0]
    full = _sel_raw(sel, x, NN)
    return full[:c], full[c:2 * c], full[2 * c:]


def _sel_fwd(sel, x):
    return sel_mm3(sel, x), sel


def _sel_bwd(sel, cts):
    return jnp.zeros_like(sel), _sel_raw(sel, jnp.concatenate(cts, axis=0), TN)


sel_mm3.defvjp(_sel_fwd, _sel_bwd)


def _my_flat():
    return 4 * lax.axis_index("x") + 2 * lax.axis_index("y") + lax.axis_index("c")


def _peer(k):
    x, y, c = lax.axis_index("x"), lax.axis_index("y"), lax.axis_index("c")
    kx, ky, kc = (k >> 2) & 1, (k >> 1) & 1, k & 1
    px = (1 - x) if kx else x
    py = (1 - y) if ky else y
    pc = (1 - c) if kc else c
    return (px, py, pc), 4 * px + 2 * py + pc


def exchange(xs, gather, name):
    n = len(xs)

    def body(*refs):
        x_refs, y_refs = refs[:n], refs[n:2 * n]
        send_sems, recv_sems, local_sems = refs[2 * n:]
        me = _my_flat()
        local, sends = [], []
        for a in range(n):
            src = x_refs[a] if gather else x_refs[a].at[me]
            cp = pltpu.make_async_copy(src, y_refs[a].at[me], local_sems.at[a])
            cp.start()
            local.append(cp)
        for k in range(1, N_DEV):
            peer, peer_flat = _peer(k)
            for a in range(n):
                src = x_refs[a] if gather else x_refs[a].at[peer_flat]
                cp = pltpu.make_async_remote_copy(
                    src_ref=src, dst_ref=y_refs[a].at[me],
                    send_sem=send_sems.at[a, k], recv_sem=recv_sems.at[a, k],
                    device_id=peer, device_id_type=MESH)
                cp.start()
                sends.append(cp)
        for k in range(1, N_DEV):
            _, peer_flat = _peer(k)
            for a in range(n):
                src = x_refs[a] if gather else x_refs[a].at[peer_flat]
                pltpu.make_async_remote_copy(
                    src_ref=src, dst_ref=y_refs[a].at[peer_flat],
                    send_sem=send_sems.at[a, k], recv_sem=recv_sems.at[a, k],
                    device_id=_peer(k)[0], device_id_type=MESH).wait_recv()
        for cp in sends:
            cp.wait_send()
        for cp in local:
            cp.wait()

    out_shape = [jax.ShapeDtypeStruct(((N_DEV,) + x.shape) if gather else x.shape, x.dtype) for x in xs]
    any_spec = pl.BlockSpec(memory_space=pl.ANY)
    return pl.pallas_call(
        body, name=name, out_shape=out_shape,
        in_specs=[any_spec] * n, out_specs=[any_spec] * n,
        scratch_shapes=[pltpu.SemaphoreType.DMA((n, N_DEV)), pltpu.SemaphoreType.DMA((n, N_DEV)),
                        pltpu.SemaphoreType.DMA((n,))],
    )(*xs)


HBM_SPEC = pl.BlockSpec(memory_space=pltpu.HBM)
SEM_SPEC = pl.BlockSpec(memory_space=pltpu.SEMAPHORE)
ANY_SPEC = pl.BlockSpec(memory_space=pl.ANY)
DATAFLOW = pltpu.SideEffectType.DATAFLOW_SIDE_EFFECTING


def _in_hbm(x):
    return pltpu.with_memory_space_constraint(x, pltpu.HBM)


def exchange_start(xs, gather, name, after=()):
    n, n_after = len(xs), len(after)

    def body(*refs):
        x_refs, land_refs = refs[:n], refs[n:2 * n]
        sems = refs[2 * n + n_after:2 * n + n_after + 2 * n]
        token = refs[-1]
        me = _my_flat()
        for k in range(1, N_DEV):
            peer, peer_flat = _peer(k)
            for a in range(n):
                src = x_refs[a] if gather else x_refs[a].at[peer_flat]
                pltpu.make_async_remote_copy(src_ref=src, dst_ref=land_refs[a].at[me], send_sem=sems[a],
                                             recv_sem=sems[n + a], device_id=peer, device_id_type=MESH).start()
        token[...] = jnp.zeros_like(token)

    lands = [_in_hbm(lax.empty(((N_DEV,) + x.shape) if gather else x.shape, x.dtype)) for x in xs]
    hbm_out = [pltpu.HBM(x.shape, x.dtype) for x in xs] + [pltpu.HBM(l.shape, l.dtype) for l in lands]
    res = pl.pallas_call(
        body, name=name,
        out_shape=(*([pltpu.SemaphoreType.DMA(())] * (2 * n)), *hbm_out, jax.ShapeDtypeStruct((8, LANES), F32)),
        in_specs=[HBM_SPEC] * (2 * n) + [ANY_SPEC] * n_after,
        out_specs=(*([SEM_SPEC] * (2 * n)), *([HBM_SPEC] * (2 * n)), pl.BlockSpec(memory_space=pltpu.VMEM)),
        input_output_aliases={i: 2 * n + i for i in range(2 * n)},
        compiler_params=pltpu.CompilerParams(has_side_effects=DATAFLOW),
    )(*[_in_hbm(x) for x in xs], *lands, *after)
    return (list(res[:2 * n]), list(res[2 * n:3 * n]), list(res[3 * n:4 * n])), res[-1]


def exchange_wait(handle, name, after=()):
    sems, xs, lands = handle
    n, n_after = len(xs), len(after)

    def body(*refs):
        land_refs = refs[n:2 * n]
        sem_refs = refs[2 * n:4 * n]
        for a in range(n):
            seven = land_refs[a].at[pl.ds(0, N_DEV - 1)]
            cp = pltpu.make_async_remote_copy(src_ref=seven, dst_ref=seven, send_sem=sem_refs[a],
                                              recv_sem=sem_refs[n + a], device_id=_peer(1)[0], device_id_type=MESH)
            cp.wait_send()
            cp.wait_recv()

    res = pl.pallas_call(
        body, name=name,
        out_shape=[pltpu.HBM(x.shape, x.dtype) for x in xs] + [pltpu.HBM(l.shape, l.dtype) for l in lands],
        in_specs=[HBM_SPEC] * (2 * n) + [SEM_SPEC] * (2 * n) + [ANY_SPEC] * n_after,
        out_specs=[HBM_SPEC] * (2 * n),
        input_output_aliases={i: i for i in range(2 * n)},
        compiler_params=pltpu.CompilerParams(has_side_effects=DATAFLOW),
    )(*xs, *lands, *sems, *after)
    return list(res[:n]), list(res[n:])


def _own_slot(land, block):
    return lax.dynamic_update_slice(land, block[None], (_my_flat(),) + (0,) * block.ndim)


def allreduce_small(x, name):
    rows = x.shape[0]

    def body(x_ref, o_ref, buf, send_sems, recv_sems):
        me = _my_flat()
        buf[me] = x_ref[...]
        sends = []
        for k in range(1, N_DEV):
            peer, _ = _peer(k)
            cp = pltpu.make_async_remote_copy(
                src_ref=x_ref, dst_ref=buf.at[me], send_sem=send_sems.at[k], recv_sem=recv_sems.at[k],
                device_id=peer, device_id_type=MESH)
            cp.start()
            sends.append(cp)
        for k in range(1, N_DEV):
            peer, peer_flat = _peer(k)
            pltpu.make_async_remote_copy(
                src_ref=x_ref, dst_ref=buf.at[peer_flat], send_sem=send_sems.at[k], recv_sem=recv_sems.at[k],
                device_id=peer, device_id_type=MESH).wait_recv()
        for cp in sends:
            cp.wait_send()
        acc = buf[0]
        for d in range(1, N_DEV):
            acc = acc + buf[d]
        o_ref[...] = acc

    vmem = pl.BlockSpec(memory_space=pltpu.VMEM)
    return pl.pallas_call(
        body, name=name, out_shape=jax.ShapeDtypeStruct((rows, LANES), F32),
        in_specs=[vmem], out_specs=vmem,
        scratch_shapes=[pltpu.VMEM((N_DEV, rows, LANES), F32),
                        pltpu.SemaphoreType.DMA((N_DEV,)), pltpu.SemaphoreType.DMA((N_DEV,))],
    )(x)


def matmul(a, b, mode, name, out_dtypes=(F32,), epilogue=None, extra=None, tm=1024, tn=1024, tk=2048, after=()):
    if mode == "nn":
        (m, kd), n = a.shape, b.shape[1]
    elif mode == "nt":
        (m, kd), n = a.shape, b.shape[0]
    else:
        (kd, m), n = a.shape, b.shape[1]
    tm, tn, tk = min(tm, m), min(tn, n), min(tk, kd)
    assert m % tm == 0 and n % tn == 0 and kd % tk == 0, (name, m, n, kd, tm, tn, tk)
    ksteps = kd // tk
    dims = {"nn": NN, "nt": NT, "tn": TN}[mode]
    n_out = len(out_dtypes)

    def body(*refs):
        a_ref, b_ref = refs[0], refs[1]
        e_ref = refs[2] if extra is not None else None
        n_in = 2 + (extra is not None) + len(after)
        o_refs = refs[n_in:n_in + n_out]
        acc_ref = refs[-1]
        kk = pl.program_id(2)

        @pl.when(kk == 0)
        def _():
            acc_ref[...] = jnp.zeros_like(acc_ref)

        acc_ref[...] += _dot(a_ref[...], b_ref[...], dims)

        @pl.when(kk == ksteps - 1)
        def _():
            acc = acc_ref[...]
            if epilogue is None:
                outs = (acc,)
            else:
                outs = epilogue(acc, e_ref[...] if e_ref is not None else None)
            for o_ref, o in zip(o_refs, outs):
                o_ref[...] = o.astype(o_ref.dtype)

    if mode == "nn":
        a_spec = pl.BlockSpec((tm, tk), lambda i, j, k: (i, k))
        b_spec = pl.BlockSpec((tk, tn), lambda i, j, k: (k, j))
    elif mode == "nt":
        a_spec = pl.BlockSpec((tm, tk), lambda i, j, k: (i, k))
        b_spec = pl.BlockSpec((tn, tk), lambda i, j, k: (j, k))
    else:
        a_spec = pl.BlockSpec((tk, tm), lambda i, j, k: (k, i))
        b_spec = pl.BlockSpec((tk, tn), lambda i, j, k: (k, j))
    o_spec = pl.BlockSpec((tm, tn), lambda i, j, k: (i, j))
    in_specs = [a_spec, b_spec] + ([o_spec] if extra is not None else []) + [ANY_SPEC] * len(after)
    args = (a, b) + ((extra,) if extra is not None else ()) + tuple(after)
    res = pl.pallas_call(
        body, name=name, grid=(m // tm, n // tn, ksteps),
        in_specs=in_specs, out_specs=[o_spec] * n_out,
        out_shape=[jax.ShapeDtypeStruct((m, n), dt) for dt in out_dtypes],
        scratch_shapes=[pltpu.VMEM((tm, tn), F32)],
        compiler_params=_params(("parallel", "parallel", "arbitrary")),
    )(*args)
    return res if n_out > 1 else res[0]


ROW_BLOCK = 256


def rms_fwd(x, w, add, name):
    t, d = x.shape
    has_add = add is not None

    def body(*refs):
        x_ref, w_ref = refs[0], refs[1]
        rest = refs[2:]
        if has_add:
            add_ref, h_ref, n_ref, r_ref = rest
            h = x_ref[...] + add_ref[...]
            h_ref[...] = h
        else:
            n_ref, r_ref = rest
            h = x_ref[...]
        r = lax.rsqrt(jnp.mean(h * h, axis=-1, keepdims=True) + NORM_EPS)
        n_ref[...] = (h * r * w_ref[...]).astype(BF16)
        r_ref[...] = r

    row = pl.BlockSpec((ROW_BLOCK, d), lambda i: (i, 0))
    wspec = pl.BlockSpec((1, d), lambda i: (0, 0))
    rspec = pl.BlockSpec((ROW_BLOCK, 1), lambda i: (i, 0))
    in_specs = [row, wspec] + ([row] if has_add else [])
    out_specs = ([row] if has_add else []) + [row, rspec]
    out_shape = ([jax.ShapeDtypeStruct((t, d), F32)] if has_add else []) + [
        jax.ShapeDtypeStruct((t, d), BF16), jax.ShapeDtypeStruct((t, 1), F32)]
    args = (x, w) + ((add,) if has_add else ())
    return pl.pallas_call(body, name=name, grid=(t // ROW_BLOCK,), in_specs=in_specs, out_specs=out_specs,
                          out_shape=out_shape, compiler_params=_params(("parallel",)))(*args)


def loss_head(h1, delta, w, target, name):
    t, d = h1.shape

    def body(h_ref, dl_ref, w_ref, t_ref, loss_ref, dh_ref, dhb_ref, dw_ref):
        @pl.when(pl.program_id(0) == 0)
        def _():
            loss_ref[...] = jnp.zeros_like(loss_ref)
            dw_ref[...] = jnp.zeros_like(dw_ref)

        h = h_ref[...] + dl_ref[...]
        wv = w_ref[...]
        r = lax.rsqrt(jnp.mean(h * h, axis=-1, keepdims=True) + NORM_EPS)
        yn = h * r
        e = yn * wv - t_ref[...]
        loss_ref[...] += 0.5 * jnp.sum(jnp.sum(e * e, axis=-1, keepdims=True), axis=0, keepdims=True) / d
        dy = e / d
        dw_ref[...] += jnp.sum(dy * yn, axis=0, keepdims=True)
        dyn = dy * wv
        dh = r * (dyn - yn * jnp.mean(dyn * yn, axis=-1, keepdims=True))
        dh_ref[...] = dh
        dhb_ref[...] = dh.astype(BF16)

    row = pl.BlockSpec((ROW_BLOCK, d), lambda i: (i, 0))
    wspec = pl.BlockSpec((1, d), lambda i: (0, 0))
    one = pl.BlockSpec((1, 1), lambda i: (0, 0))
    return pl.pallas_call(
        body, name=name, grid=(t // ROW_BLOCK,),
        in_specs=[row, row, wspec, row], out_specs=[one, row, row, wspec],
        out_shape=[jax.ShapeDtypeStruct((1, 1), F32), jax.ShapeDtypeStruct((t, d), F32),
                   jax.ShapeDtypeStruct((t, d), BF16), jax.ShapeDtypeStruct((1, d), F32)],
        compiler_params=_params(("arbitrary",)))(h1, delta, w, target)


def rms_bwd(h, r, w, dn, dres, name):
    t, d = h.shape

    def body(h_ref, r_ref, w_ref, dn_ref, dres_ref, dh_ref, dhb_ref, dw_ref):
        @pl.when(pl.program_id(0) == 0)
        def _():
            dw_ref[...] = jnp.zeros_like(dw_ref)

        rv = r_ref[...]
        yn = h_ref[...] * rv
        dnv = dn_ref[...]
        dw_ref[...] += jnp.sum(dnv * yn, axis=0, keepdims=True)
        dyn = dnv * w_ref[...]
        dh = dres_ref[...] + rv * (dyn - yn * jnp.mean(dyn * yn, axis=-1, keepdims=True))
        dh_ref[...] = dh
        dhb_ref[...] = dh.astype(BF16)

    row = pl.BlockSpec((ROW_BLOCK, d), lambda i: (i, 0))
    wspec = pl.BlockSpec((1, d), lambda i: (0, 0))
    rspec = pl.BlockSpec((ROW_BLOCK, 1), lambda i: (i, 0))
    return pl.pallas_call(
        body, name=name, grid=(t // ROW_BLOCK,),
        in_specs=[row, rspec, wspec, row, row], out_specs=[row, row, wspec],
        out_shape=[jax.ShapeDtypeStruct((t, d), F32), jax.ShapeDtypeStruct((t, d), BF16),
                   jax.ShapeDtypeStruct((1, d), F32)],
        compiler_params=_params(("arbitrary",)))(h, r, w, dn, dres)


CONV_TB = 512
CONV_CB = 512
HALO = 8


def _silu(x):
    return x * jax.nn.sigmoid(x)


def _conv_pre(xcat, w, rows):
    acc = None
    for j in range(4):
        sh = 3 - j
        xs = xcat if sh == 0 else pltpu.roll(xcat, sh, 0)
        term = xs[HALO:HALO + rows] * w[j:j + 1, :]
        acc = term if acc is None else acc + term
    return acc


def conv_fwd(proj, conv_w, name):
    t = proj.shape[0]
    nb = CONV_TB // HALO

    def body(x_ref, prev_ref, w_ref, o_ref):
        prev = jnp.where(pl.program_id(1) == 0, 0.0, prev_ref[...])
        xcat = jnp.concatenate([prev, x_ref[...]], axis=0)
        o_ref[...] = _silu(_conv_pre(xcat, w_ref[...], CONV_TB))

    return pl.pallas_call(
        body, name=name, grid=(QKV_WIDTH // CONV_CB, t // CONV_TB),
        in_specs=[pl.BlockSpec((CONV_TB, CONV_CB), lambda c, i: (i, c)),
                  pl.BlockSpec((HALO, CONV_CB), lambda c, i: (jnp.maximum(i * nb - 1, 0), c)),
                  pl.BlockSpec((4, CONV_CB), lambda c, i: (0, c))],
        out_specs=pl.BlockSpec((CONV_TB, CONV_CB), lambda c, i: (i, c)),
        out_shape=jax.ShapeDtypeStruct((t, QKV_WIDTH), F32),
        compiler_params=_params(("parallel", "parallel")))(proj, proj, conv_w)


def conv_bwd(proj, dout, conv_w, col0, name):
    t = proj.shape[0]
    nb = CONV_TB // HALO
    nt = t // CONV_TB
    cb0 = col0 // CONV_CB
    rows = CONV_TB + HALO

    def body(x_ref, prev_ref, next_ref, d_ref, dnext_ref, w_ref, dx_ref, dw_ref):
        i = pl.program_id(1)

        @pl.when(i == 0)
        def _():
            dw_ref[...] = jnp.zeros_like(dw_ref)

        w = w_ref[...]
        prev = jnp.where(i == 0, 0.0, prev_ref[...])
        last = i == nt - 1
        xcat = jnp.concatenate([prev, x_ref[...], next_ref[...]], axis=0)
        pre = _conv_pre(xcat, w, rows)
        dcat = jnp.concatenate([d_ref[...], jnp.where(last, 0.0, dnext_ref[...])], axis=0)
        sg = jax.nn.sigmoid(pre)
        dpre = dcat * (sg * (1.0 + pre * (1.0 - sg)))
        dx = None
        for j in range(4):
            sh = 3 - j
            ds = dpre if sh == 0 else pltpu.roll(dpre, rows - sh, 0)
            term = ds[:CONV_TB] * w[j:j + 1, :]
            dx = term if dx is None else dx + term
        dx_ref[...] = dx.astype(BF16)
        dcur = dpre[:CONV_TB]
        parts = []
        for j in range(4):
            sh = 3 - j
            xs = xcat if sh == 0 else pltpu.roll(xcat, sh, 0)
            parts.append(jnp.sum(dcur * xs[HALO:HALO + CONV_TB], axis=0, keepdims=True))
        dw_ref[...] += jnp.concatenate(parts, axis=0)

    return pl.pallas_call(
        body, name=name, grid=(GDN_WIDTH // CONV_CB, nt),
        in_specs=[pl.BlockSpec((CONV_TB, CONV_CB), lambda c, i: (i, cb0 + c)),
                  pl.BlockSpec((HALO, CONV_CB), lambda c, i: (jnp.maximum(i * nb - 1, 0), cb0 + c)),
                  pl.BlockSpec((HALO, CONV_CB), lambda c, i: (jnp.minimum((i + 1) * nb, nt * nb - 1), cb0 + c)),
                  pl.BlockSpec((CONV_TB, CONV_CB), lambda c, i: (i, c)),
                  pl.BlockSpec((HALO, CONV_CB), lambda c, i: (jnp.minimum((i + 1) * nb, nt * nb - 1), c)),
                  pl.BlockSpec((4, CONV_CB), lambda c, i: (0, cb0 + c))],
        out_specs=[pl.BlockSpec((CONV_TB, CONV_CB), lambda c, i: (i, c)),
                   pl.BlockSpec((4, CONV_CB), lambda c, i: (0, c))],
        out_shape=[jax.ShapeDtypeStruct((t, GDN_WIDTH), BF16), jax.ShapeDtypeStruct((4, GDN_WIDTH), F32)],
        compiler_params=_params(("parallel", "arbitrary")))(proj, proj, proj, dout, dout, conv_w)


def _iota2(shape, axis):
    return lax.broadcasted_iota(jnp.int32, shape, axis)


def _softplus(x):
    return jnp.maximum(x, 0.0) + jnp.log(1.0 + jnp.exp(-jnp.abs(x)))


def _head_norm_gate(o, norm_w, gate):
    return o * lax.rsqrt(jnp.mean(o * o, axis=-1, keepdims=True) + NORM_EPS) * norm_w * _silu(gate)


GDN_PREC = ("bf", "bf")
HGRN_PREC = "bf"


def _each(fn, *cols):
    return [fn(*a) for a in zip(*cols)]


def gdn_chunks(hs, qc, kc, vc, zc, ab, a_log_l, dt_l, norm_w, s, prec=GDN_PREC):
    p_inv, p_mm = prec
    c = CHUNK
    ri, ci = _iota2((c, c), 0), _iota2((c, c), 1)
    incl, strict, eye = ri >= ci, ri > ci, ri == ci
    lane = _iota2((c, LANES), 1)
    last_row = _iota2((c, 1), 0) == c - 1
    rowsum = lambda x: jnp.sum(x, axis=1, keepdims=True)

    def row(col):
        return jnp.sum(jnp.where(eye, col, 0.0), axis=0, keepdims=True)

    q = _each(lambda x: x * lax.rsqrt(rowsum(x * x) + L2_EPS) * (HEAD_DIM ** -0.5), qc)
    k = _each(lambda x: x * lax.rsqrt(rowsum(x * x) + L2_EPS), kc)
    a_col = [rowsum(jnp.where(lane == h, ab, 0.0)) for h in hs]
    b_col = [rowsum(jnp.where(lane == h + N_HEADS, ab, 0.0)) for h in hs]
    beta = _each(jax.nn.sigmoid, b_col)
    g = _each(lambda a, al, dl: rowsum(jnp.where(lane == 0, -jnp.exp(al) * _softplus(a + dl), 0.0)), a_col, a_log_l, dt_l)
    gcum = _each(lambda x: rowsum(jnp.where(incl, row(x), 0.0)), g)
    g_last = _each(lambda x: jnp.sum(jnp.where(last_row, x, 0.0), axis=0, keepdims=True), gcum)
    decay = _each(lambda x: jnp.exp(jnp.where(incl, x - row(x), -jnp.inf)), gcum)
    kk = _each(lambda x: mm(x, x, NT, p_mm), k)
    low = _each(lambda b, x, d: jnp.where(strict, b * x * d, 0.0), beta, kk, decay)
    power = _each(lambda x: -x, low)
    inv = _each(lambda x: jnp.where(eye, 1.0, 0.0) + x, power)
    for _ in range(5):
        power = _each(lambda x: mm(x, x, NN, p_inv), power)
        inv = _each(lambda x, p: x + mm(x, p, NN, p_inv), inv, power)
    exp_g = _each(jnp.exp, gcum)
    u_v = _each(lambda i, b, x: mm(i, b * x, NN, p_mm), inv, beta, vc)
    w = _each(lambda i, b, e, x: mm(i, b * e * x, NN, p_mm), inv, beta, exp_g, k)
    attn = _each(lambda x, y, d: mm(x, y, NT, p_mm) * d, q, k, decay)
    u = _each(lambda x, y, z: x - mm(y, z, NN, p_mm), u_v, w, s)
    o = _each(lambda x, e, z: mm(x * e, z, NN, p_mm), q, exp_g, s)
    o = _each(lambda x, a, y: x + mm(a, y, NN, p_mm), o, attn, u)
    k_end = _each(lambda x, gl, gc: x * jnp.exp(gl - gc), k, g_last, gcum)
    s_new = _each(lambda z, gl, x, y: z * jnp.exp(gl) + mm(x, y, TN, p_mm), s, g_last, k_end, u)
    return _each(lambda x, z: _head_norm_gate(x, norm_w, z), o, zc), s_new


def gdn_chunk(h, qc, kc, vc, zc, ab, a_log_l, dt_l, norm_w, s, prec=GDN_PREC):
    y, s_new = gdn_chunks([h], [qc], [kc], [vc], [zc], ab, [a_log_l], [dt_l], norm_w, [s], prec)
    return y[0], s_new[0]


@functools.partial(jax.custom_vjp, nondiff_argnums=(1,))
def _sroll(x, shift):
    return x if shift == 0 else pltpu.roll(x, shift, 0)


def _sroll_fwd(x, shift):
    return _sroll(x, shift), None


def _sroll_bwd(shift, _, ct):
    return (ct if shift == 0 else pltpu.roll(ct, ct.shape[0] - shift, 0),)


_sroll.defvjp(_sroll_fwd, _sroll_bwd)


def hgrn_chunks(qb, fb, ib, gb, l0, l1, norm_w, st, prec=HGRN_PREC):
    c = CHUNK
    ri, ci = _iota2((3 * c, c), 0), _iota2((3 * c, c), 1)
    rcol = _iota2((c, 1), 0)
    blk0 = jnp.bitwise_and(ri, c - SUB_CHUNK)
    limit = jnp.where(ri < c, ri + 1, jnp.where(ri < 2 * c, blk0, blk0 + SUB_CHUNK))
    sel = jnp.where(ci < limit, 1.0, 0.0)
    ci = _iota2((c, c), 1)
    lb = _each(lambda a, b: jax.nn.sigmoid(a - b), l0, l1)
    forget = _each(lambda b, f: b + (1.0 - b) * jax.nn.sigmoid(f), lb, fb)
    key = _each(lambda b, f: (1.0 - b) * jax.nn.sigmoid(-f), lb, fb)
    q = _each(_silu, qb)
    v = ib
    logf = _each(jnp.log, forget)
    sums = _each(lambda x: sel_mm3(sel, x), logf)
    bc, b_start, b_end = [x[0] for x in sums], [x[1] for x in sums], [x[2] for x in sums]
    b_last = _each(lambda x: jnp.sum(x, axis=0, keepdims=True), logf)
    o = _each(lambda x, b, z: mm(x * jnp.exp(b), z, NT, prec), q, bc, st)
    rmod = jnp.bitwise_and(rcol, SUB_CHUNK - 1)
    for off in range(SUB_CHUNK):
        def diag(acc, x, ky, b, val):
            e = jnp.exp(jnp.where(rmod >= off, b - _sroll(b, off), -jnp.inf))
            a_o = jnp.sum(x * _sroll(ky, off) * e, axis=-1, keepdims=True)
            return acc + a_o * _sroll(val, off)
        o = _each(diag, o, q, key, bc, v)
    q_rel = _each(lambda x, b, bs: x * jnp.exp(b - bs), q, bc, b_start)
    k_rel = _each(lambda x, b, be: x * jnp.exp(be - b), key, bc, b_end)
    for y in range(c // SUB_CHUNK - 1):
        def scaled(x, b, bs):
            end_y = jnp.sum(jnp.where(rcol == SUB_CHUNK * y + SUB_CHUNK - 1, b, 0.0), axis=0, keepdims=True)
            return x * jnp.exp(jnp.where(rcol >= SUB_CHUNK * (y + 1), bs - end_y, -jnp.inf))
        dq = _each(scaled, q_rel, bc, b_start)
        in_y = (ci >= SUB_CHUNK * y) & (ci < SUB_CHUNK * (y + 1))
        a_y = _each(lambda x, z: jnp.where(in_y, mm(x, z, NT, prec), 0.0), dq, k_rel)
        o = _each(lambda acc, a, val: acc + mm(a, val, NN, prec), o, a_y, v)
    k_state = _each(lambda x, bl, b: x * jnp.exp(bl - b), key, b_last, bc)
    st_new = _each(lambda z, bl, val, x: z * jnp.exp(bl) + mm(val, x, TN, prec), st, b_last, v, k_state)
    return _each(lambda x, z: _head_norm_gate(x, norm_w, z), o, gb), st_new


def hgrn_chunk(qb, fb, ib, gb, l0, l1, norm_w, st, prec=HGRN_PREC):
    y, st_new = hgrn_chunks([qb], [fb], [ib], [gb], [l0], [l1], norm_w, [st], prec)
    return y[0], st_new[0]


HEAD_VEC = (N_HEADS, 1, LANES)


class _Groups:
    def __init__(self, nc, hb, rev):
        self.nc, self.hb, self.ng, self.rev = nc, hb, N_HEADS // hb, rev

    def _c(self, c):
        return self.nc - 1 - c if self.rev else c

    def cols(self, slab):
        return pl.BlockSpec((CHUNK, self.hb * LANES), lambda c, g: (self._c(c), slab * self.ng + g))

    def tile(self, block):
        return pl.BlockSpec((CHUNK, LANES), lambda c, g: (self._c(c), block))

    def state(self):
        return pl.BlockSpec((None, self.hb, HEAD_DIM, HEAD_DIM), lambda c, g: (self._c(c), g, 0, 0))

    @staticmethod
    def whole(shape):
        return pl.BlockSpec(shape, lambda c, g: (0,) * len(shape))

    def head(self, g, j):
        return j if self.ng == 1 else g * self.hb + j


def _lanes(j):
    return slice(j * LANES, (j + 1) * LANES)


def gdn_fwd(qkv_c, proj, a_log_l, dt_l, norm_w, name, hb=8):
    t = qkv_c.shape[0]
    gr = _Groups(t // CHUNK, hb, rev=False)

    def body(q_ref, k_ref, v_ref, z_ref, ab_ref, al_ref, dt_ref, nw_ref, y_ref, hist_ref, s_ref):
        c, g = pl.program_id(0), pl.program_id(1)

        @pl.when(c == 0)
        def _():
            for j in range(hb):
                s_ref[gr.head(g, j)] = jnp.zeros((HEAD_DIM, HEAD_DIM), F32)

        hs = [gr.head(g, j) for j in range(hb)]
        heads = lambda ref: [ref[:, _lanes(j)] for j in range(hb)]
        s = [s_ref[h] for h in hs]
        for j in range(hb):
            hist_ref[j] = s[j]
        y, s_new = gdn_chunks(hs, heads(q_ref), heads(k_ref), heads(v_ref), heads(z_ref), ab_ref[...],
                              [al_ref[h] for h in hs], [dt_ref[h] for h in hs], nw_ref[...], s)
        for j in range(hb):
            y_ref[:, _lanes(j)] = y[j].astype(BF16)
            s_ref[hs[j]] = s_new[j]

    return pl.pallas_call(
        body, name=name, grid=(gr.nc, gr.ng),
        in_specs=[gr.cols(0), gr.cols(1), gr.cols(2), gr.cols(3), gr.tile(AB_BLOCK),
                  gr.whole(HEAD_VEC), gr.whole(HEAD_VEC), gr.whole((1, LANES))],
        out_specs=[gr.cols(0), gr.state()],
        out_shape=[jax.ShapeDtypeStruct((t, 2 * GDN_WIDTH), BF16),
                   jax.ShapeDtypeStruct((gr.nc, N_HEADS, HEAD_DIM, HEAD_DIM), F32)],
        scratch_shapes=[pltpu.VMEM((N_HEADS, HEAD_DIM, HEAD_DIM), F32)],
        compiler_params=_params(("arbitrary", "arbitrary")),
    )(qkv_c, qkv_c, qkv_c, proj, proj, a_log_l, dt_l, norm_w)


def gdn_bwd(qkv_c, proj, a_log_l, dt_l, norm_w, hist, dy, name, hb=4):
    t = qkv_c.shape[0]
    gr = _Groups(t // CHUNK, hb, rev=True)

    def body(q_ref, k_ref, v_ref, z_ref, ab_ref, al_ref, dt_ref, nw_ref, hist_ref, dy_ref,
             dq_ref, dk_ref, dv_ref, dz_ref, dab_ref, dal_ref, ddt_ref, dnw_ref, ds_ref, dab_acc):
        c, g = pl.program_id(0), pl.program_id(1)

        @pl.when((c == 0) & (g == 0))
        def _():
            dal_ref[...] = jnp.zeros_like(dal_ref)
            ddt_ref[...] = jnp.zeros_like(ddt_ref)
            dnw_ref[...] = jnp.zeros_like(dnw_ref)

        @pl.when(c == 0)
        def _():
            for j in range(hb):
                ds_ref[gr.head(g, j)] = jnp.zeros((HEAD_DIM, HEAD_DIM), F32)

        @pl.when(g == 0)
        def _():
            dab_acc[...] = jnp.zeros_like(dab_acc)

        hs = [gr.head(g, j) for j in range(hb)]
        heads = lambda ref: [ref[:, _lanes(j)] for j in range(hb)]
        _, vjp = jax.vjp(functools.partial(gdn_chunks, hs), heads(q_ref), heads(k_ref), heads(v_ref), heads(z_ref),
                         ab_ref[...], [al_ref[h] for h in hs], [dt_ref[h] for h in hs], nw_ref[...],
                         [hist_ref[j] for j in range(hb)])
        dq, dk, dv, dz, dab, dal, ddt, dnw, ds = vjp((heads(dy_ref), [ds_ref[h] for h in hs]))
        for j in range(hb):
            sl = _lanes(j)
            dq_ref[:, sl] = dq[j]
            dk_ref[:, sl] = dk[j]
            dv_ref[:, sl] = dv[j]
            dz_ref[:, sl] = dz[j].astype(BF16)
            dal_ref[hs[j]] += dal[j]
            ddt_ref[hs[j]] += ddt[j]
            ds_ref[hs[j]] = ds[j]
        dab_acc[...] += dab
        dnw_ref[...] += dnw

        @pl.when(g == gr.ng - 1)
        def _():
            dab_ref[...] = dab_acc[...].astype(BF16)

    col = gr.cols(0)
    return pl.pallas_call(
        body, name=name, grid=(gr.nc, gr.ng),
        in_specs=[gr.cols(0), gr.cols(1), gr.cols(2), gr.cols(3), gr.tile(AB_BLOCK),
                  gr.whole(HEAD_VEC), gr.whole(HEAD_VEC), gr.whole((1, LANES)), gr.state(), gr.cols(0)],
        out_specs=[col, col, col, col, gr.tile(0), gr.whole(HEAD_VEC), gr.whole(HEAD_VEC), gr.whole((1, LANES))],
        out_shape=[jax.ShapeDtypeStruct((t, GDN_WIDTH), F32)] * 3 + [
            jax.ShapeDtypeStruct((t, GDN_WIDTH), BF16), jax.ShapeDtypeStruct((t, LANES), BF16),
            jax.ShapeDtypeStruct(HEAD_VEC, F32), jax.ShapeDtypeStruct(HEAD_VEC, F32),
            jax.ShapeDtypeStruct((1, LANES), F32)],
        scratch_shapes=[pltpu.VMEM((N_HEADS, HEAD_DIM, HEAD_DIM), F32), pltpu.VMEM((CHUNK, LANES), F32)],
        compiler_params=_params(("arbitrary", "arbitrary")),
    )(qkv_c, qkv_c, qkv_c, proj, proj, a_log_l, dt_l, norm_w, hist, dy)


def hgrn_fwd(proj, l0, l1, norm_w, y, name, hb=8):
    t = proj.shape[0]
    gr = _Groups(t // CHUNK, hb, rev=False)

    def body(q_ref, f_ref, i_ref, g_ref, l0_ref, l1_ref, nw_ref, y_in, y_ref, hist_ref, s_ref):
        del y_in
        c, g = pl.program_id(0), pl.program_id(1)

        @pl.when(c == 0)
        def _():
            for j in range(hb):
                s_ref[gr.head(g, j)] = jnp.zeros((HEAD_DIM, HEAD_DIM), F32)

        hs = [gr.head(g, j) for j in range(hb)]
        heads = lambda ref: [ref[:, _lanes(j)] for j in range(hb)]
        s = [s_ref[h] for h in hs]
        for j in range(hb):
            hist_ref[j] = s[j]
        out, s_new = hgrn_chunks(heads(q_ref), heads(f_ref), heads(i_ref), heads(g_ref), [l0_ref[h] for h in hs],
                                 [l1_ref[h] for h in hs], nw_ref[...], s)
        for j in range(hb):
            y_ref[:, _lanes(j)] = out[j].astype(BF16)
            s_ref[hs[j]] = s_new[j]

    return pl.pallas_call(
        body, name=name, grid=(gr.nc, gr.ng),
        in_specs=[gr.cols(4), gr.cols(5), gr.cols(6), gr.cols(7), gr.whole(HEAD_VEC), gr.whole(HEAD_VEC),
                  gr.whole((1, LANES)), pl.BlockSpec(memory_space=pl.ANY)],
        out_specs=[gr.cols(1), gr.state()],
        out_shape=[jax.ShapeDtypeStruct((t, 2 * GDN_WIDTH), BF16),
                   jax.ShapeDtypeStruct((gr.nc, N_HEADS, HEAD_DIM, HEAD_DIM), F32)],
        scratch_shapes=[pltpu.VMEM((N_HEADS, HEAD_DIM, HEAD_DIM), F32)],
        input_output_aliases={7: 0},
        compiler_params=_params(("arbitrary", "arbitrary")),
    )(proj, proj, proj, proj, l0, l1, norm_w, y)


def hgrn_bwd(proj, l0, l1, norm_w, hist, dy, name, hb=4):
    t = proj.shape[0]
    gr = _Groups(t // CHUNK, hb, rev=True)

    def body(q_ref, f_ref, i_ref, g_ref, l0_ref, l1_ref, nw_ref, hist_ref, dy_ref,
             dq_ref, df_ref, di_ref, dg_ref, dl0_ref, dl1_ref, dnw_ref, ds_ref):
        c, g = pl.program_id(0), pl.program_id(1)

        @pl.when((c == 0) & (g == 0))
        def _():
            dl0_ref[...] = jnp.zeros_like(dl0_ref)
            dl1_ref[...] = jnp.zeros_like(dl1_ref)
            dnw_ref[...] = jnp.zeros_like(dnw_ref)

        @pl.when(c == 0)
        def _():
            for j in range(hb):
                ds_ref[gr.head(g, j)] = jnp.zeros((HEAD_DIM, HEAD_DIM), F32)

        hs = [gr.head(g, j) for j in range(hb)]
        heads = lambda ref: [ref[:, _lanes(j)] for j in range(hb)]
        _, vjp = jax.vjp(hgrn_chunks, heads(q_ref), heads(f_ref), heads(i_ref), heads(g_ref), [l0_ref[h] for h in hs],
                         [l1_ref[h] for h in hs], nw_ref[...], [hist_ref[j] for j in range(hb)])
        dq, df, di, dg, dl0, dl1, dnw, ds = vjp((heads(dy_ref), [ds_ref[h] for h in hs]))
        for j in range(hb):
            sl = _lanes(j)
            dq_ref[:, sl] = dq[j].astype(BF16)
            df_ref[:, sl] = df[j].astype(BF16)
            di_ref[:, sl] = di[j].astype(BF16)
            dg_ref[:, sl] = dg[j].astype(BF16)
            dl0_ref[hs[j]] += dl0[j]
            dl1_ref[hs[j]] += dl1[j]
            ds_ref[hs[j]] = ds[j]
        dnw_ref[...] += dnw

    col = gr.cols(0)
    return pl.pallas_call(
        body, name=name, grid=(gr.nc, gr.ng),
        in_specs=[gr.cols(4), gr.cols(5), gr.cols(6), gr.cols(7), gr.whole(HEAD_VEC), gr.whole(HEAD_VEC),
                  gr.whole((1, LANES)), gr.state(), gr.cols(1)],
        out_specs=[col, col, col, col, gr.whole(HEAD_VEC), gr.whole(HEAD_VEC), gr.whole((1, LANES))],
        out_shape=[jax.ShapeDtypeStruct((t, GDN_WIDTH), BF16)] * 4 + [
            jax.ShapeDtypeStruct(HEAD_VEC, F32), jax.ShapeDtypeStruct(HEAD_VEC, F32),
            jax.ShapeDtypeStruct((1, LANES), F32)],
        scratch_shapes=[pltpu.VMEM((N_HEADS, HEAD_DIM, HEAD_DIM), F32)],
        compiler_params=_params(("arbitrary", "arbitrary")),
    )(proj, proj, proj, proj, l0, l1, norm_w, hist, dy)


def _adamw(w, g, m, v):
    m = ADAM_B1 * m + (1.0 - ADAM_B1) * g
    v = ADAM_B2 * v + (1.0 - ADAM_B2) * jnp.square(g)
    m_hat = m / (1.0 - ADAM_B1 ** ADAM_STEP)
    v_hat = v / (1.0 - ADAM_B2 ** ADAM_STEP)
    delta = -ADAM_LR * (m_hat / (jnp.sqrt(v_hat) + ADAM_EPS) + ADAM_WD * w)
    return delta, m, v


def adamw_reduce(parts, w, m, v, name, rb=128):
    r, c = w.shape
    rb = min(rb, r)

    def body(p_ref, w_ref, m_ref, v_ref, g_ref, d_ref, mo_ref, vo_ref):
        g = p_ref[0].astype(F32)
        for d in range(1, N_DEV):
            g = g + p_ref[d].astype(F32)
        delta, mn, vn = _adamw(w_ref[...], g, m_ref[...], v_ref[...])
        g_ref[...] = g
        d_ref[...] = delta
        mo_ref[...] = mn
        vo_ref[...] = vn

    blk = pl.BlockSpec((rb, c), lambda i: (i, 0))
    return pl.pallas_call(
        body, name=name, grid=(r // rb,),
        in_specs=[pl.BlockSpec((N_DEV, rb, c), lambda i: (0, i, 0)), blk, blk, blk],
        out_specs=[blk] * 4, out_shape=[jax.ShapeDtypeStruct((r, c), F32)] * 4,
        compiler_params=_params(("parallel",)))(parts, w, m, v)


def adamw_small(w, g, m, v, name):
    def body(w_ref, g_ref, m_ref, v_ref, d_ref, mo_ref, vo_ref):
        delta, mn, vn = _adamw(w_ref[...], g_ref[...], m_ref[...], v_ref[...])
        d_ref[...] = delta
        mo_ref[...] = mn
        vo_ref[...] = vn

    vmem = pl.BlockSpec(memory_space=pltpu.VMEM)
    return pl.pallas_call(body, name=name, in_specs=[vmem] * 4, out_specs=[vmem] * 3,
                          out_shape=[jax.ShapeDtypeStruct(w.shape, F32)] * 3)(w, g, m, v)


def _pack(arrays):
    flat = jnp.concatenate([a.reshape(-1).astype(F32) for a in arrays])
    rows = -(-flat.shape[0] // (8 * LANES)) * 8
    return jnp.pad(flat, (0, rows * LANES - flat.shape[0])).reshape(rows, LANES)


def _unpack(packed, shapes):
    flat, out, off = packed.reshape(-1), [], 0
    for s in shapes:
        n = 1
        for d in s:
            n *= d
        out.append(flat[off:off + n].reshape(s))
        off += n
    return out


def _relu2_epilogue(acc, _):
    r = jnp.maximum(acc, 0.0)
    return acc, r * r


def _relu2_bwd_epilogue(acc, a1):
    return (acc * (2.0 * jnp.maximum(a1, 0.0)),)


def kernel(x, w_in, conv_w, gdn_a_log, gdn_dt_bias, gdn_norm_w, hgrn_lb_logits, hgrn_norm_w, w_out, norm_mix_w, norm_ffn_w, w_ff1, w_ff2, norm_final_w, loss_target, m_w_in, m_conv_w, m_gdn_a_log, m_gdn_dt_bias, m_gdn_norm_w, m_hgrn_lb_logits, m_hgrn_norm_w, m_w_out, m_norm_mix_w, m_norm_ffn_w, m_w_ff1, m_w_ff2, m_norm_final_w, v_w_in, v_conv_w, v_gdn_a_log, v_gdn_dt_bias, v_gdn_norm_w, v_hgrn_lb_logits, v_hgrn_norm_w, v_w_out, v_norm_mix_w, v_norm_ffn_w, v_w_ff1, v_w_ff2, v_norm_final_w):
    me = _my_flat()
    xs = x[0]
    target = loss_target[0]
    shard_in = w_in.shape[2]
    shard_conv = conv_w.shape[2]

    tok = lambda t: t[0:1, 0:1]
    own = lambda src: lax.dynamic_index_in_dim(src, me, 0, keepdims=False)

    g_in, g_conv = exchange([w_in[0].astype(BF16), conv_w[0]], gather=True, name="gather_w_in")
    h_g1, t_g1 = exchange_start([w_out[0].astype(BF16), w_ff1[0].astype(BF16)], True, "gather_mid_start", after=[g_in])
    h_g2, t_g2 = exchange_start([w_ff2[0].astype(BF16)], True, "gather_ff2_start", after=[t_g1])
    w_in_full = jnp.transpose(g_in, (1, 0, 2)).reshape(D_MODEL, N_DEV * shard_in)
    o_z = 4 * GDN_WIDTH
    w_cat = jnp.concatenate([w_in_full[:, :o_z], w_in_full[:, o_z + 2 * N_HEADS:], w_in_full[:, o_z:o_z + 2 * N_HEADS],
                             jnp.zeros((D_MODEL, LANES - 2 * N_HEADS), BF16)], axis=1)
    conv_full = jnp.transpose(g_conv, (1, 0, 2)).reshape(4, QKV_WIDTH)

    lane_b = lambda p: jnp.broadcast_to(p.reshape(N_HEADS, 1, 1), HEAD_VEC)
    a_log_l, dt_l = lane_b(gdn_a_log[0]), lane_b(gdn_dt_bias[0])
    l0 = hgrn_lb_logits[0].reshape(HEAD_VEC)
    l1 = hgrn_lb_logits[1].reshape(HEAD_VEC)

    n1, r1 = rms_fwd(xs, norm_mix_w + tok(t_g1) + tok(t_g2), None, "rms_mix")
    proj = matmul(n1, w_cat, "nn", "in_proj", tn=640)
    qkv_c = conv_fwd(proj, conv_full, "conv_fwd")
    y_half, hist_a = gdn_fwd(qkv_c, proj, a_log_l, dt_l, gdn_norm_w, "gdn_fwd")
    y, hist_b = hgrn_fwd(proj, l0, l1, hgrn_norm_w, y_half, "hgrn_fwd")
    (s_out, s_ff1), (l_out, l_ff1) = exchange_wait(h_g1, "gather_mid_wait", after=[y])
    w_out_full = _own_slot(l_out, s_out).reshape(D_MODEL, D_MODEL)
    w_ff1_full = jnp.transpose(_own_slot(l_ff1, s_ff1), (1, 0, 2)).reshape(D_MODEL, D_FF)
    mix = matmul(y, w_out_full, "nn", "out_proj")
    h1, n2, r2 = rms_fwd(xs, norm_ffn_w, mix, "rms_ffn")
    a1, act = matmul(n2, w_ff1_full, "nn", "ff1", out_dtypes=(F32, BF16), epilogue=_relu2_epilogue)
    (s_ff2,), (l_ff2,) = exchange_wait(h_g2, "gather_ff2_wait", after=[act])
    w_ff2_full = _own_slot(l_ff2, s_ff2).reshape(D_FF, D_MODEL)
    ff = matmul(act, w_ff2_full, "nn", "ff2")
    loss_sum, dh2, dh2_b, d_final = loss_head(h1, ff, norm_final_w.reshape(1, D_MODEL), target, "loss_head")

    da1 = matmul(dh2_b, w_ff2_full, "nt", "d_act", out_dtypes=(BF16,), epilogue=_relu2_bwd_epilogue, extra=a1)
    dw_ff2 = matmul(act, dh2_b, "tn", "dw_ff2", tk=1024)
    p_ff2 = dw_ff2.reshape(N_DEV, D_FF // N_DEV, D_MODEL).astype(BF16)
    h_s1, t_s1 = exchange_start([p_ff2], False, "scatter_ff2_start")
    dn2 = matmul(da1, w_ff1_full, "nt", "d_n2", after=[t_s1])
    dw_ff1 = matmul(n2, da1, "tn", "dw_ff1", tk=1024, after=[t_s1])
    p_ff1 = jnp.transpose(dw_ff1.reshape(D_MODEL, N_DEV, D_FF // N_DEV), (1, 0, 2)).astype(BF16)
    h_s2, t_s2 = exchange_start([p_ff1], False, "scatter_ff1_start")
    dh1, dh1_b, d_ffn = rms_bwd(h1, r2, norm_ffn_w + tok(t_s2), dn2, dh2, "rms_ffn_bwd")
    dmix = matmul(dh1_b, w_out_full, "nt", "d_mix")
    dw_out = matmul(y, dh1_b, "tn", "dw_out", tk=1024)
    p_out = dw_out.reshape(N_DEV, D_MODEL // N_DEV, D_MODEL).astype(BF16)
    h_s3, t_s3 = exchange_start([p_out], False, "scatter_out_start")
    dq_c, dk_c, dv_c, dz, dab, d_alog_l, d_dt_l, d_gnw = gdn_bwd(
        qkv_c, proj, a_log_l, dt_l, gdn_norm_w + tok(t_s3), hist_a, dmix, "gdn_bwd")
    dqb, dfb, dib, dgb, dl0, dl1, d_hnw = hgrn_bwd(proj, l0, l1, hgrn_norm_w + tok(t_s3), hist_b, dmix, "hgrn_bwd")
    dq, dwc_q = conv_bwd(proj, dq_c, conv_full, 0, "conv_bwd_q")
    dk, dwc_k = conv_bwd(proj, dk_c, conv_full, GDN_WIDTH, "conv_bwd_k")
    dv, dwc_v = conv_bwd(proj, dv_c, conv_full, 2 * GDN_WIDTH, "conv_bwd_v")
    dproj = jnp.concatenate([dq, dk, dv, dz, dqb, dfb, dib, dgb, dab], axis=1)
    dw_cat = matmul(n1, dproj, "tn", "dw_in", tn=640, tk=1024)
    dw_in_full = jnp.concatenate([dw_cat[:, :o_z], dw_cat[:, MAIN_WIDTH:MAIN_WIDTH + 2 * N_HEADS],
                                  dw_cat[:, o_z:MAIN_WIDTH]], axis=1)
    p_in = jnp.transpose(dw_in_full.reshape(D_MODEL, N_DEV, shard_in), (1, 0, 2)).astype(BF16)
    h_s4, t_s4 = exchange_start([p_in], False, "scatter_in_start")

    (s_ff2g,), (r_ff2,) = exchange_wait(h_s1, "scatter_ff2_wait", after=[t_s4])
    (s_ff1g,), (r_ff1,) = exchange_wait(h_s2, "scatter_ff1_wait", after=[t_s4])
    (s_outg,), (r_out,) = exchange_wait(h_s3, "scatter_out_wait", after=[t_s4])
    g_w_ff2, d_w_ff2, nm_w_ff2, nv_w_ff2 = adamw_reduce(
        _own_slot(r_ff2, own(s_ff2g)), w_ff2[0], m_w_ff2[0], v_w_ff2[0], "adamw_w_ff2")
    g_w_ff1, d_w_ff1, nm_w_ff1, nv_w_ff1 = adamw_reduce(
        _own_slot(r_ff1, own(s_ff1g)), w_ff1[0], m_w_ff1[0], v_w_ff1[0], "adamw_w_ff1")
    g_w_out, d_w_out, nm_w_out, nv_w_out = adamw_reduce(
        _own_slot(r_out, own(s_outg)), w_out[0], m_w_out[0], v_w_out[0], "adamw_w_out")
    dn1 = matmul(dproj, w_cat, "nt", "d_n1", tk=CAT_WIDTH // 5, after=[t_s4])
    dx, _, d_mix = rms_bwd(xs, r1, norm_mix_w, dn1, dh1, "rms_mix_bwd")
    (s_ing,), (r_in,) = exchange_wait(h_s4, "scatter_in_wait", after=[dx, d_w_ff2, d_w_ff1, d_w_out])
    g_w_in, d_w_in, nm_w_in, nv_w_in = adamw_reduce(
        _own_slot(r_in, own(s_ing)), w_in[0], m_w_in[0], v_w_in[0], "adamw_w_in")

    d_lb = jnp.stack([dl0.reshape(GDN_WIDTH), dl1.reshape(GDN_WIDTH)])
    d_conv_full = jnp.concatenate([dwc_q, dwc_k, dwc_v], axis=1)
    small_shapes = [(1, N_HEADS), (1, N_HEADS), (1, HEAD_DIM), (2, GDN_WIDTH), (1, HEAD_DIM), (1, D_MODEL),
                    (1, D_MODEL), (D_MODEL,), (4, QKV_WIDTH)]
    small = _pack([d_alog_l[:, 0, 0], d_dt_l[:, 0, 0], d_gnw, d_lb, d_hnw, d_mix, d_ffn, d_final, d_conv_full])
    red = allreduce_small(small, "allreduce_small")
    g_alog, g_dt, g_gnw, g_lb, g_hnw, g_mix, g_ffn, g_final, g_conv_full = _unpack(red, small_shapes)
    g_conv = lax.dynamic_slice(g_conv_full, (0, me * shard_conv), (4, shard_conv)).reshape(1, 4, shard_conv)
    small_g = [g_alog, g_dt, g_gnw, g_lb, g_hnw, g_mix, g_ffn, g_final, g_conv]
    small_w = [gdn_a_log, gdn_dt_bias, gdn_norm_w, hgrn_lb_logits, hgrn_norm_w, norm_mix_w, norm_ffn_w, norm_final_w, conv_w]
    small_m = [m_gdn_a_log, m_gdn_dt_bias, m_gdn_norm_w, m_hgrn_lb_logits, m_hgrn_norm_w, m_norm_mix_w, m_norm_ffn_w,
               m_norm_final_w, m_conv_w]
    small_v = [v_gdn_a_log, v_gdn_dt_bias, v_gdn_norm_w, v_hgrn_lb_logits, v_hgrn_norm_w, v_norm_mix_w, v_norm_ffn_w,
               v_norm_final_w, v_conv_w]
    shapes = [a.shape for a in small_w]
    d_s, m_s, v_s = adamw_small(_pack(small_w), _pack(small_g), _pack(small_m), _pack(small_v), "adamw_small")
    d_alog, d_dt, d_gn, d_lbl, d_hn, d_nm, d_nf, d_nfin, d_cw = _unpack(d_s, shapes)
    m_alog, m_dt, m_gn, m_lbl, m_hn, m_nm, m_nf, m_nfin, m_cw = _unpack(m_s, shapes)
    v_alog, v_dt, v_gn, v_lbl, v_hn, v_nm, v_nf, v_nfin, v_cw = _unpack(v_s, shapes)

    loss = lax.psum(loss_sum[0, 0], ("x", "y", "c"))
    lead = lambda a: a[None]
    grads = [lead(g_w_in), g_conv, g_alog, g_dt, g_gnw, g_lb, g_hnw, lead(g_w_out), g_mix, g_ffn,
             lead(g_w_ff1), lead(g_w_ff2), g_final]
    deltas = [lead(d_w_in), d_cw, d_alog, d_dt, d_gn, d_lbl, d_hn, lead(d_w_out), d_nm, d_nf,
              lead(d_w_ff1), lead(d_w_ff2), d_nfin]
    new_m = [lead(nm_w_in), m_cw, m_alog, m_dt, m_gn, m_lbl, m_hn, lead(nm_w_out), m_nm, m_nf,
             lead(nm_w_ff1), lead(nm_w_ff2), m_nfin]
    new_v = [lead(nv_w_in), v_cw, v_alog, v_dt, v_gn, v_lbl, v_hn, lead(nv_w_out), v_nm, v_nf,
             lead(nv_w_ff1), lead(nv_w_ff2), v_nfin]
    return (loss, dx[None], *grads, *deltas, *new_m, *new_v)
```

```python
import functools

import jax
import jax.numpy as jnp
from jax import lax
from jax.experimental import pallas as pl
from jax.experimental.pallas import tpu as pltpu

F32 = jnp.float32
BF16 = jnp.bfloat16
HI = lax.Precision.HIGHEST

N_DEV = 8
D_MODEL = 2048
CHUNK = 64
SUB_CHUNK = 16
HEAD_DIM = 128
N_HEADS = 8
GDN_WIDTH = N_HEADS * HEAD_DIM
D_FF = 4 * D_MODEL
QKV_WIDTH = 3 * GDN_WIDTH
MAIN_WIDTH = 8 * GDN_WIDTH
CAT_WIDTH = MAIN_WIDTH + 128
IN_PROJ_WIDTH = MAIN_WIDTH + 2 * N_HEADS
AB_BLOCK = MAIN_WIDTH // 128
NORM_EPS = 1e-6
L2_EPS = 1e-6
LANES = 128
VMEM_LIMIT = 56 * 1024 * 1024

ADAM_LR = 0.001
ADAM_B1 = 0.9
ADAM_B2 = 0.999
ADAM_EPS = 1e-08
ADAM_WD = 0.01
ADAM_STEP = 10

MESH = pl.DeviceIdType.MESH


def _params(sem=None):
    return pltpu.CompilerParams(dimension_semantics=sem, vmem_limit_bytes=VMEM_LIMIT)


def _dot(a, b, dims, prec=None):
    return lax.dot_general(a, b, (dims, ((), ())), precision=prec, preferred_element_type=F32)


NN = ((1,), (0,))
NT = ((1,), (1,))
TN = ((0,), (0,))


def _split_bf16(x, pieces):
    out = []
    for _ in range(pieces - 1):
        p = x.astype(BF16)
        out.append(p)
        x = x - p.astype(F32)
    out.append(x.astype(BF16))
    return out


def _mm_raw(a, b, dims, prec):
    if prec == "hi":
        return _dot(a, b, dims, HI)
    if prec == "bf":
        return _dot(a.astype(BF16), b.astype(BF16), dims)
    a_hi, a_lo = _split_bf16(a, 2)
    b_hi, b_lo = _split_bf16(b, 2)
    return _dot(a_hi, b_hi, dims) + (_dot(a_hi, b_lo, dims) + _dot(a_lo, b_hi, dims))


@functools.partial(jax.custom_vjp, nondiff_argnums=(2, 3))
def mm(a, b, dims, prec):
    return _mm_raw(a, b, dims, prec)


def _mm_fwd(a, b, dims, prec):
    return _mm_raw(a, b, dims, prec), (a, b)


def _mm_bwd(dims, prec, res, ct):
    a, b = res
    if dims == NN:
        return _mm_raw(ct, b, NT, prec), _mm_raw(a, ct, TN, prec)
    if dims == NT:
        return _mm_raw(ct, b, NN, prec), _mm_raw(ct, a, TN, prec)
    return _mm_raw(b, ct, NT, prec), _mm_raw(a, ct, NN, prec)


mm.defvjp(_mm_fwd, _mm_bwd)


def _sel_raw(sel, x, dims):
    sel = sel.astype(BF16)
    p0, p1, p2 = _split_bf16(x, 3)
    return _dot(sel, p0, dims) + (_dot(sel, p1, dims) + _dot(sel, p2, dims))


@jax.custom_vjp
def sel_mm3(sel, x):
    c = x.shape[0]
    full = _sel_raw(sel, x, NN)
    return full[:c], full[c:2 * c], full[2 * c:]


def _sel_fwd(sel, x):
    return sel_mm3(sel, x), sel


def _sel_bwd(sel, cts):
    return jnp.zeros_like(sel), _sel_raw(sel, jnp.concatenate(cts, axis=0), TN)


sel_mm3.defvjp(_sel_fwd, _sel_bwd)


def _my_flat():
    return 4 * lax.axis_index("x") + 2 * lax.axis_index("y") + lax.axis_index("c")


def _peer(k):
    x, y, c = lax.axis_index("x"), lax.axis_index("y"), lax.axis_index("c")
    kx, ky, kc = (k >> 2) & 1, (k >> 1) & 1, k & 1
    px = (1 - x) if kx else x
    py = (1 - y) if ky else y
    pc = (1 - c) if kc else c
    return (px, py, pc), 4 * px + 2 * py + pc


def exchange(xs, gather, name):
    n = len(xs)

    def body(*refs):
        x_refs, y_refs = refs[:n], refs[n:2 * n]
        send_sems, recv_sems, local_sems = refs[2 * n:]
        me = _my_flat()
        local, sends = [], []
        for a in range(n):
            src = x_refs[a] if gather else x_refs[a].at[me]
            cp = pltpu.make_async_copy(src, y_refs[a].at[me], local_sems.at[a])
            cp.start()
            local.append(cp)
        for k in range(1, N_DEV):
            peer, peer_flat = _peer(k)
            for a in range(n):
                src = x_refs[a] if gather else x_refs[a].at[peer_flat]
                cp = pltpu.make_async_remote_copy(
                    src_ref=src, dst_ref=y_refs[a].at[me],
                    send_sem=send_sems.at[a, k], recv_sem=recv_sems.at[a, k],
                    device_id=peer, device_id_type=MESH)
                cp.start()
                sends.append(cp)
        for k in range(1, N_DEV):
            _, peer_flat = _peer(k)
            for a in range(n):
                src = x_refs[a] if gather else x_refs[a].at[peer_flat]
                pltpu.make_async_remote_copy(
                    src_ref=src, dst_ref=y_refs[a].at[peer_flat],
                    send_sem=send_sems.at[a, k], recv_sem=recv_sems.at[a, k],
                    device_id=_peer(k)[0], device_id_type=MESH).wait_recv()
        for cp in sends:
            cp.wait_send()
        for cp in local:
            cp.wait()

    out_shape = [jax.ShapeDtypeStruct(((N_DEV,) + x.shape) if gather else x.shape, x.dtype) for x in xs]
    any_spec = pl.BlockSpec(memory_space=pl.ANY)
    return pl.pallas_call(
        body, name=name, out_shape=out_shape,
        in_specs=[any_spec] * n, out_specs=[any_spec] * n,
        scratch_shapes=[pltpu.SemaphoreType.DMA((n, N_DEV)), pltpu.SemaphoreType.DMA((n, N_DEV)),
                        pltpu.SemaphoreType.DMA((n,))],
    )(*xs)


HBM_SPEC = pl.BlockSpec(memory_space=pltpu.HBM)
SEM_SPEC = pl.BlockSpec(memory_space=pltpu.SEMAPHORE)
ANY_SPEC = pl.BlockSpec(memory_space=pl.ANY)
DATAFLOW = pltpu.SideEffectType.DATAFLOW_SIDE_EFFECTING


def _in_hbm(x):
    return pltpu.with_memory_space_constraint(x, pltpu.HBM)


def exchange_start(xs, gather, name, after=()):
    n, n_after = len(xs), len(after)

    def body(*refs):
        x_refs, land_refs = refs[:n], refs[n:2 * n]
        sems = refs[2 * n + n_after:2 * n + n_after + 2 * n]
        token = refs[-1]
        me = _my_flat()
        for k in range(1, N_DEV):
            peer, peer_flat = _peer(k)
            for a in range(n):
                src = x_refs[a] if gather else x_refs[a].at[peer_flat]
                pltpu.make_async_remote_copy(src_ref=src, dst_ref=land_refs[a].at[me], send_sem=sems[a],
                                             recv_sem=sems[n + a], device_id=peer, device_id_type=MESH).start()
        token[...] = jnp.zeros_like(token)

    lands = [_in_hbm(lax.empty(((N_DEV,) + x.shape) if gather else x.shape, x.dtype)) for x in xs]
    hbm_out = [pltpu.HBM(x.shape, x.dtype) for x in xs] + [pltpu.HBM(l.shape, l.dtype) for l in lands]
    res = pl.pallas_call(
        body, name=name,
        out_shape=(*([pltpu.SemaphoreType.DMA(())] * (2 * n)), *hbm_out, jax.ShapeDtypeStruct((8, LANES), F32)),
        in_specs=[HBM_SPEC] * (2 * n) + [ANY_SPEC] * n_after,
        out_specs=(*([SEM_SPEC] * (2 * n)), *([HBM_SPEC] * (2 * n)), pl.BlockSpec(memory_space=pltpu.VMEM)),
        input_output_aliases={i: 2 * n + i for i in range(2 * n)},
        compiler_params=pltpu.CompilerParams(has_side_effects=DATAFLOW),
    )(*[_in_hbm(x) for x in xs], *lands, *after)
    return (list(res[:2 * n]), list(res[2 * n:3 * n]), list(res[3 * n:4 * n])), res[-1]


def exchange_wait(handle, name, after=()):
    sems, xs, lands = handle
    n, n_after = len(xs), len(after)

    def body(*refs):
        land_refs = refs[n:2 * n]
        sem_refs = refs[2 * n:4 * n]
        for a in range(n):
            seven = land_refs[a].at[pl.ds(0, N_DEV - 1)]
            cp = pltpu.make_async_remote_copy(src_ref=seven, dst_ref=seven, send_sem=sem_refs[a],
                                              recv_sem=sem_refs[n + a], device_id=_peer(1)[0], device_id_type=MESH)
            cp.wait_send()
            cp.wait_recv()

    res = pl.pallas_call(
        body, name=name,
        out_shape=[pltpu.HBM(x.shape, x.dtype) for x in xs] + [pltpu.HBM(l.shape, l.dtype) for l in lands],
        in_specs=[HBM_SPEC] * (2 * n) + [SEM_SPEC] * (2 * n) + [ANY_SPEC] * n_after,
        out_specs=[HBM_SPEC] * (2 * n),
        input_output_aliases={i: i for i in range(2 * n)},
        compiler_params=pltpu.CompilerParams(has_side_effects=DATAFLOW),
    )(*xs, *lands, *sems, *after)
    return list(res[:n]), list(res[n:])


def _own_slot(land, block):
    return lax.dynamic_update_slice(land, block[None], (_my_flat(),) + (0,) * block.ndim)


def allreduce_small(x, name):
    rows = x.shape[0]

    def body(x_ref, o_ref, buf, send_sems, recv_sems):
        me = _my_flat()
        buf[me] = x_ref[...]
        sends = []
        for k in range(1, N_DEV):
            peer, _ = _peer(k)
            cp = pltpu.make_async_remote_copy(
                src_ref=x_ref, dst_ref=buf.at[me], send_sem=send_sems.at[k], recv_sem=recv_sems.at[k],
                device_id=peer, device_id_type=MESH)
            cp.start()
            sends.append(cp)
        for k in range(1, N_DEV):
            peer, peer_flat = _peer(k)
            pltpu.make_async_remote_copy(
                src_ref=x_ref, dst_ref=buf.at[peer_flat], send_sem=send_sems.at[k], recv_sem=recv_sems.at[k],
                device_id=peer, device_id_type=MESH).wait_recv()
        for cp in sends:
            cp.wait_send()
        acc = buf[0]
        for d in range(1, N_DEV):
            acc = acc + buf[d]
        o_ref[...] = acc

    vmem = pl.BlockSpec(memory_space=pltpu.VMEM)
    return pl.pallas_call(
        body, name=name, out_shape=jax.ShapeDtypeStruct((rows, LANES), F32),
        in_specs=[vmem], out_specs=vmem,
        scratch_shapes=[pltpu.VMEM((N_DEV, rows, LANES), F32),
                        pltpu.SemaphoreType.DMA((N_DEV,)), pltpu.SemaphoreType.DMA((N_DEV,))],
    )(x)


def matmul(a, b, mode, name, out_dtypes=(F32,), epilogue=None, extra=None, tm=1024, tn=1024, tk=2048, after=(),
           b_shards=False, out_shards=False):
    if b_shards:
        n_sh, b_rows, b_cols = b.shape
    if mode == "nn":
        (m, kd), n = a.shape, (n_sh * b_cols if b_shards else b.shape[1])
        if b_shards:
            tn = b_cols
    elif mode == "nt":
        (m, kd), n = a.shape, (b_rows if b_shards else b.shape[0])
        if b_shards:
            tk = b_cols
    else:
        (kd, m), n = a.shape, b.shape[1]
    tm, tn, tk = min(tm, m), min(tn, n), min(tk, kd)
    assert m % tm == 0 and n % tn == 0 and kd % tk == 0, (name, m, n, kd, tm, tn, tk)
    ksteps = kd // tk
    dims = {"nn": NN, "nt": NT, "tn": TN}[mode]
    n_out = len(out_dtypes)
    n_in = 2 + (extra is not None) + len(after)

    def finish(acc, e_ref, o_refs):
        outs = (acc,) if epilogue is None else epilogue(acc, e_ref[...] if e_ref is not None else None)
        for o_ref, o in zip(o_refs, outs):
            o_ref[...] = o.astype(o_ref.dtype)

    def body(*refs):
        a_ref, b_ref = refs[0], refs[1]
        e_ref = refs[2] if extra is not None else None
        o_refs = refs[n_in:n_in + n_out]
        if ksteps == 1:
            finish(_dot(a_ref[...], b_ref[...], dims), e_ref, o_refs)
            return
        acc_ref = refs[-1]
        kk = pl.program_id(2)

        @pl.when(kk == 0)
        def _():
            acc_ref[...] = jnp.zeros_like(acc_ref)

        acc_ref[...] += _dot(a_ref[...], b_ref[...], dims)

        @pl.when(kk == ksteps - 1)
        def _():
            finish(acc_ref[...], e_ref, o_refs)

    if mode == "nn":
        a_spec = pl.BlockSpec((tm, tk), lambda i, j, k: (i, k))
        b_spec = (pl.BlockSpec((None, tk, tn), lambda i, j, k: (j, k, 0)) if b_shards
                  else pl.BlockSpec((tk, tn), lambda i, j, k: (k, j)))
    elif mode == "nt":
        a_spec = pl.BlockSpec((tm, tk), lambda i, j, k: (i, k))
        b_spec = (pl.BlockSpec((None, tn, tk), lambda i, j, k: (k, j, 0)) if b_shards
                  else pl.BlockSpec((tn, tk), lambda i, j, k: (j, k)))
    else:
        a_spec = pl.BlockSpec((tk, tm), lambda i, j, k: (k, i))
        b_spec = pl.BlockSpec((tk, tn), lambda i, j, k: (k, j))
    o_spec = pl.BlockSpec((tm, tn), lambda i, j, k: (i, j))
    res_spec = pl.BlockSpec((None, tm, tn), lambda i, j, k: (j, i, 0)) if out_shards else o_spec
    res_shape = (n // tn, m, tn) if out_shards else (m, n)
    in_specs = [a_spec, b_spec] + ([o_spec] if extra is not None else []) + [ANY_SPEC] * len(after)
    args = (a, b) + ((extra,) if extra is not None else ()) + tuple(after)
    res = pl.pallas_call(
        body, name=name, grid=(m // tm, n // tn, ksteps),
        in_specs=in_specs, out_specs=[res_spec] * n_out,
        out_shape=[jax.ShapeDtypeStruct(res_shape, dt) for dt in out_dtypes],
        scratch_shapes=[pltpu.VMEM((tm, tn), F32)] if ksteps > 1 else [],
        compiler_params=_params(("parallel", "parallel", "arbitrary")),
    )(*args)
    return res if n_out > 1 else res[0]


GATE_COL = 4 * GDN_WIDTH
RELAYOUT_ROWS = 256


def _cat_of_win(j):
    if j < GATE_COL:
        return j
    if j < GATE_COL + 2 * N_HEADS:
        return MAIN_WIDTH + (j - GATE_COL)
    return j - 2 * N_HEADS


def _win_of_cat(c):
    if c < GATE_COL:
        return c
    if c < MAIN_WIDTH:
        return c + 2 * N_HEADS
    if c < MAIN_WIDTH + 2 * N_HEADS:
        return GATE_COL + (c - MAIN_WIDTH)
    return None


def _runs(first, count, mapping):
    runs, i = [], 0
    while i < count:
        start, n = mapping(first + i), 1
        while i + n < count and mapping(first + i + n) == start + n:
            n += 1
        runs.append((start, n))
        i += n
    return runs


def weights_to_cat(g_in):
    n_dev, rows, shard = g_in.shape

    def body(x_ref, o_ref):
        for b in range(CAT_WIDTH // LANES):
            live = sum(_win_of_cat(LANES * b + i) is not None for i in range(LANES))
            parts = []
            for start, n in _runs(LANES * b, live, _win_of_cat):
                while n > 0:
                    d, o = divmod(start, shard)
                    take = min(n, shard - o)
                    parts.append(x_ref[d, :, o:o + take])
                    start, n = start + take, n - take
            if live < LANES:
                parts.append(jnp.zeros((RELAYOUT_ROWS, LANES - live), g_in.dtype))
            o_ref[:, LANES * b:LANES * (b + 1)] = parts[0] if len(parts) == 1 else jnp.concatenate(parts, axis=1)

    return pl.pallas_call(
        body, name="weights_to_cat", grid=(rows // RELAYOUT_ROWS,),
        in_specs=[pl.BlockSpec((n_dev, RELAYOUT_ROWS, shard), lambda i: (0, i, 0))],
        out_specs=pl.BlockSpec((RELAYOUT_ROWS, CAT_WIDTH), lambda i: (i, 0)),
        out_shape=jax.ShapeDtypeStruct((rows, CAT_WIDTH), g_in.dtype),
        compiler_params=_params(("parallel",)))(g_in)


def cat_to_shards(dw_cat, shard):
    rows = dw_cat.shape[0]

    def body(x_ref, o_ref):
        for d in range(N_DEV):
            for t0 in range(0, shard, LANES):
                width = min(LANES, shard - t0)
                parts = [x_ref[:, c:c + n] for c, n in _runs(d * shard + t0, width, _cat_of_win)]
                o_ref[d, :, t0:t0 + width] = parts[0] if len(parts) == 1 else jnp.concatenate(parts, axis=1)

    return pl.pallas_call(
        body, name="cat_to_shards", grid=(rows // RELAYOUT_ROWS,),
        in_specs=[pl.BlockSpec((RELAYOUT_ROWS, CAT_WIDTH), lambda i: (i, 0))],
        out_specs=pl.BlockSpec((N_DEV, RELAYOUT_ROWS, shard), lambda i: (0, i, 0)),
        out_shape=jax.ShapeDtypeStruct((N_DEV, rows, shard), dw_cat.dtype),
        compiler_params=_params(("parallel",)))(dw_cat)


ROW_BLOCK = 256


def rms_fwd(x, w, add, name):
    t, d = x.shape
    has_add = add is not None

    def body(*refs):
        x_ref, w_ref = refs[0], refs[1]
        rest = refs[2:]
        if has_add:
            add_ref, h_ref, n_ref, r_ref = rest
            h = x_ref[...] + add_ref[...]
            h_ref[...] = h
        else:
            n_ref, r_ref = rest
            h = x_ref[...]
        r = lax.rsqrt(jnp.mean(h * h, axis=-1, keepdims=True) + NORM_EPS)
        n_ref[...] = (h * r * w_ref[...]).astype(BF16)
        r_ref[...] = r

    row = pl.BlockSpec((ROW_BLOCK, d), lambda i: (i, 0))
    wspec = pl.BlockSpec((1, d), lambda i: (0, 0))
    rspec = pl.BlockSpec((ROW_BLOCK, 1), lambda i: (i, 0))
    in_specs = [row, wspec] + ([row] if has_add else [])
    out_specs = ([row] if has_add else []) + [row, rspec]
    out_shape = ([jax.ShapeDtypeStruct((t, d), F32)] if has_add else []) + [
        jax.ShapeDtypeStruct((t, d), BF16), jax.ShapeDtypeStruct((t, 1), F32)]
    args = (x, w) + ((add,) if has_add else ())
    return pl.pallas_call(body, name=name, grid=(t // ROW_BLOCK,), in_specs=in_specs, out_specs=out_specs,
                          out_shape=out_shape, compiler_params=_params(("parallel",)))(*args)


def loss_head(h1, delta, w, target, name):
    t, d = h1.shape

    def body(h_ref, dl_ref, w_ref, t_ref, loss_ref, dh_ref, dhb_ref, dw_ref):
        @pl.when(pl.program_id(0) == 0)
        def _():
            loss_ref[...] = jnp.zeros_like(loss_ref)
            dw_ref[...] = jnp.zeros_like(dw_ref)

        h = h_ref[...] + dl_ref[...]
        wv = w_ref[...]
        r = lax.rsqrt(jnp.mean(h * h, axis=-1, keepdims=True) + NORM_EPS)
        yn = h * r
        e = yn * wv - t_ref[...]
        loss_ref[...] += 0.5 * jnp.sum(jnp.sum(e * e, axis=-1, keepdims=True), axis=0, keepdims=True) / d
        dy = e / d
        dw_ref[...] += jnp.sum(dy * yn, axis=0, keepdims=True)
        dyn = dy * wv
        dh = r * (dyn - yn * jnp.mean(dyn * yn, axis=-1, keepdims=True))
        dh_ref[...] = dh
        dhb_ref[...] = dh.astype(BF16)

    row = pl.BlockSpec((ROW_BLOCK, d), lambda i: (i, 0))
    wspec = pl.BlockSpec((1, d), lambda i: (0, 0))
    one = pl.BlockSpec((1, 1), lambda i: (0, 0))
    return pl.pallas_call(
        body, name=name, grid=(t // ROW_BLOCK,),
        in_specs=[row, row, wspec, row], out_specs=[one, row, row, wspec],
        out_shape=[jax.ShapeDtypeStruct((1, 1), F32), jax.ShapeDtypeStruct((t, d), F32),
                   jax.ShapeDtypeStruct((t, d), BF16), jax.ShapeDtypeStruct((1, d), F32)],
        compiler_params=_params(("arbitrary",)))(h1, delta, w, target)


def rms_bwd(h, r, w, dn, dres, name):
    t, d = h.shape

    def body(h_ref, r_ref, w_ref, dn_ref, dres_ref, dh_ref, dhb_ref, dw_ref):
        @pl.when(pl.program_id(0) == 0)
        def _():
            dw_ref[...] = jnp.zeros_like(dw_ref)

        rv = r_ref[...]
        yn = h_ref[...] * rv
        dnv = dn_ref[...]
        dw_ref[...] += jnp.sum(dnv * yn, axis=0, keepdims=True)
        dyn = dnv * w_ref[...]
        dh = dres_ref[...] + rv * (dyn - yn * jnp.mean(dyn * yn, axis=-1, keepdims=True))
        dh_ref[...] = dh
        dhb_ref[...] = dh.astype(BF16)

    row = pl.BlockSpec((ROW_BLOCK, d), lambda i: (i, 0))
    wspec = pl.BlockSpec((1, d), lambda i: (0, 0))
    rspec = pl.BlockSpec((ROW_BLOCK, 1), lambda i: (i, 0))
    return pl.pallas_call(
        body, name=name, grid=(t // ROW_BLOCK,),
        in_specs=[row, rspec, wspec, row, row], out_specs=[row, row, wspec],
        out_shape=[jax.ShapeDtypeStruct((t, d), F32), jax.ShapeDtypeStruct((t, d), BF16),
                   jax.ShapeDtypeStruct((1, d), F32)],
        compiler_params=_params(("arbitrary",)))(h, r, w, dn, dres)


CONV_TB = 512
CONV_CB = 512
HALO = 8


def _silu(x):
    return x * jax.nn.sigmoid(x)


def _conv_pre(xcat, w, rows):
    acc = None
    for j in range(4):
        sh = 3 - j
        xs = xcat if sh == 0 else pltpu.roll(xcat, sh, 0)
        term = xs[HALO:HALO + rows] * w[j:j + 1, :]
        acc = term if acc is None else acc + term
    return acc


def conv_fwd(proj, conv_w, name):
    t = proj.shape[0]
    nb = CONV_TB // HALO

    def body(x_ref, prev_ref, w_ref, o_ref):
        prev = jnp.where(pl.program_id(1) == 0, 0.0, prev_ref[...])
        xcat = jnp.concatenate([prev, x_ref[...]], axis=0)
        o_ref[...] = _silu(_conv_pre(xcat, w_ref[...], CONV_TB))

    return pl.pallas_call(
        body, name=name, grid=(QKV_WIDTH // CONV_CB, t // CONV_TB),
        in_specs=[pl.BlockSpec((CONV_TB, CONV_CB), lambda c, i: (i, c)),
                  pl.BlockSpec((HALO, CONV_CB), lambda c, i: (jnp.maximum(i * nb - 1, 0), c)),
                  pl.BlockSpec((4, CONV_CB), lambda c, i: (0, c))],
        out_specs=pl.BlockSpec((CONV_TB, CONV_CB), lambda c, i: (i, c)),
        out_shape=jax.ShapeDtypeStruct((t, QKV_WIDTH), F32),
        compiler_params=_params(("parallel", "parallel")))(proj, proj, conv_w)


def conv_bwd(proj, dout, conv_w, dproj, name):
    t = proj.shape[0]
    nb = CONV_TB // HALO
    nt = t // CONV_TB
    rows = CONV_TB + HALO

    def body(x_ref, prev_ref, next_ref, d_ref, dnext_ref, w_ref, dproj_in, dx_ref, dw_ref):
        del dproj_in
        i = pl.program_id(1)

        @pl.when(i == 0)
        def _():
            dw_ref[...] = jnp.zeros_like(dw_ref)

        w = w_ref[...]
        prev = jnp.where(i == 0, 0.0, prev_ref[...])
        last = i == nt - 1
        xcat = jnp.concatenate([prev, x_ref[...], next_ref[...]], axis=0)
        pre = _conv_pre(xcat, w, rows)
        dcat = jnp.concatenate([d_ref[...], jnp.where(last, 0.0, dnext_ref[...])], axis=0)
        sg = jax.nn.sigmoid(pre)
        dpre = dcat * (sg * (1.0 + pre * (1.0 - sg)))
        dx = None
        for j in range(4):
            sh = 3 - j
            ds = dpre if sh == 0 else pltpu.roll(dpre, rows - sh, 0)
            term = ds[:CONV_TB] * w[j:j + 1, :]
            dx = term if dx is None else dx + term
        dx_ref[...] = dx.astype(BF16)
        dcur = dpre[:CONV_TB]
        parts = []
        for j in range(4):
            sh = 3 - j
            xs = xcat if sh == 0 else pltpu.roll(xcat, sh, 0)
            parts.append(jnp.sum(dcur * xs[HALO:HALO + CONV_TB], axis=0, keepdims=True))
        dw_ref[...] += jnp.concatenate(parts, axis=0)

    cur = pl.BlockSpec((CONV_TB, CONV_CB), lambda c, i: (i, c))
    halo_prev = pl.BlockSpec((HALO, CONV_CB), lambda c, i: (jnp.maximum(i * nb - 1, 0), c))
    halo_next = pl.BlockSpec((HALO, CONV_CB), lambda c, i: (jnp.minimum((i + 1) * nb, nt * nb - 1), c))
    taps = pl.BlockSpec((4, CONV_CB), lambda c, i: (0, c))
    return pl.pallas_call(
        body, name=name, grid=(QKV_WIDTH // CONV_CB, nt),
        in_specs=[cur, halo_prev, halo_next, cur, halo_next, taps, ANY_SPEC],
        out_specs=[cur, taps],
        out_shape=[jax.ShapeDtypeStruct(dproj.shape, BF16), jax.ShapeDtypeStruct((4, QKV_WIDTH), F32)],
        input_output_aliases={6: 0},
        compiler_params=_params(("parallel", "arbitrary")))(proj, proj, proj, dout, dout, conv_w, dproj)


def _iota2(shape, axis):
    return lax.broadcasted_iota(jnp.int32, shape, axis)


def _softplus(x):
    return jnp.maximum(x, 0.0) + jnp.log(1.0 + jnp.exp(-jnp.abs(x)))


def _head_norm_gate(o, norm_w, gate):
    return o * lax.rsqrt(jnp.mean(o * o, axis=-1, keepdims=True) + NORM_EPS) * norm_w * _silu(gate)


GDN_PREC = ("bf", "bf")
HGRN_PREC = "bf"


def _each(fn, *cols):
    return [fn(*a) for a in zip(*cols)]


def gdn_chunks(hs, qc, kc, vc, zc, ab, a_log_l, dt_l, norm_w, s, prec=GDN_PREC):
    p_inv, p_mm = prec
    c = CHUNK
    ri, ci = _iota2((c, c), 0), _iota2((c, c), 1)
    incl, strict, eye = ri >= ci, ri > ci, ri == ci
    lane = _iota2((c, LANES), 1)
    last_row = _iota2((c, 1), 0) == c - 1
    rowsum = lambda x: jnp.sum(x, axis=1, keepdims=True)

    def row(col):
        return jnp.sum(jnp.where(eye, col, 0.0), axis=0, keepdims=True)

    q = _each(lambda x: x * lax.rsqrt(rowsum(x * x) + L2_EPS) * (HEAD_DIM ** -0.5), qc)
    k = _each(lambda x: x * lax.rsqrt(rowsum(x * x) + L2_EPS), kc)
    a_col = [rowsum(jnp.where(lane == h, ab, 0.0)) for h in hs]
    b_col = [rowsum(jnp.where(lane == h + N_HEADS, ab, 0.0)) for h in hs]
    beta = _each(jax.nn.sigmoid, b_col)
    g = _each(lambda a, al, dl: rowsum(jnp.where(lane == 0, -jnp.exp(al) * _softplus(a + dl), 0.0)), a_col, a_log_l, dt_l)
    gcum = _each(lambda x: rowsum(jnp.where(incl, row(x), 0.0)), g)
    g_last = _each(lambda x: jnp.sum(jnp.where(last_row, x, 0.0), axis=0, keepdims=True), gcum)
    decay = _each(lambda x: jnp.exp(jnp.where(incl, x - row(x), -jnp.inf)), gcum)
    kk = _each(lambda x: mm(x, x, NT, p_mm), k)
    low = _each(lambda b, x, d: jnp.where(strict, b * x * d, 0.0), beta, kk, decay)
    power = _each(lambda x: -x, low)
    inv = _each(lambda x: jnp.where(eye, 1.0, 0.0) + x, power)
    for _ in range(5):
        power = _each(lambda x: mm(x, x, NN, p_inv), power)
        inv = _each(lambda x, p: x + mm(x, p, NN, p_inv), inv, power)
    exp_g = _each(jnp.exp, gcum)
    u_v = _each(lambda i, b, x: mm(i, b * x, NN, p_mm), inv, beta, vc)
    w = _each(lambda i, b, e, x: mm(i, b * e * x, NN, p_mm), inv, beta, exp_g, k)
    attn = _each(lambda x, y, d: mm(x, y, NT, p_mm) * d, q, k, decay)
    u = _each(lambda x, y, z: x - mm(y, z, NN, p_mm), u_v, w, s)
    o = _each(lambda x, e, z: mm(x * e, z, NN, p_mm), q, exp_g, s)
    o = _each(lambda x, a, y: x + mm(a, y, NN, p_mm), o, attn, u)
    k_end = _each(lambda x, gl, gc: x * jnp.exp(gl - gc), k, g_last, gcum)
    s_new = _each(lambda z, gl, x, y: z * jnp.exp(gl) + mm(x, y, TN, p_mm), s, g_last, k_end, u)
    return _each(lambda x, z: _head_norm_gate(x, norm_w, z), o, zc), s_new


def gdn_chunk(h, qc, kc, vc, zc, ab, a_log_l, dt_l, norm_w, s, prec=GDN_PREC):
    y, s_new = gdn_chunks([h], [qc], [kc], [vc], [zc], ab, [a_log_l], [dt_l], norm_w, [s], prec)
    return y[0], s_new[0]


@functools.partial(jax.custom_vjp, nondiff_argnums=(1,))
def _sroll(x, shift):
    return x if shift == 0 else pltpu.roll(x, shift, 0)


def _sroll_fwd(x, shift):
    return _sroll(x, shift), None


def _sroll_bwd(shift, _, ct):
    return (ct if shift == 0 else pltpu.roll(ct, ct.shape[0] - shift, 0),)


_sroll.defvjp(_sroll_fwd, _sroll_bwd)


def hgrn_chunks(qb, fb, ib, gb, l0, l1, norm_w, st, prec=HGRN_PREC):
    c = CHUNK
    ri, ci = _iota2((3 * c, c), 0), _iota2((3 * c, c), 1)
    rcol = _iota2((c, 1), 0)
    blk0 = jnp.bitwise_and(ri, c - SUB_CHUNK)
    limit = jnp.where(ri < c, ri + 1, jnp.where(ri < 2 * c, blk0, blk0 + SUB_CHUNK))
    sel = jnp.where(ci < limit, 1.0, 0.0)
    ci = _iota2((c, c), 1)
    lb = _each(lambda a, b: jax.nn.sigmoid(a - b), l0, l1)
    forget = _each(lambda b, f: b + (1.0 - b) * jax.nn.sigmoid(f), lb, fb)
    key = _each(lambda b, f: (1.0 - b) * jax.nn.sigmoid(-f), lb, fb)
    q = _each(_silu, qb)
    v = ib
    logf = _each(jnp.log, forget)
    sums = _each(lambda x: sel_mm3(sel, x), logf)
    bc, b_start, b_end = [x[0] for x in sums], [x[1] for x in sums], [x[2] for x in sums]
    b_last = _each(lambda x: jnp.sum(x, axis=0, keepdims=True), logf)
    o = _each(lambda x, b, z: mm(x * jnp.exp(b), z, NT, prec), q, bc, st)
    rmod = jnp.bitwise_and(rcol, SUB_CHUNK - 1)
    for off in range(SUB_CHUNK):
        def diag(acc, x, ky, b, val):
            e = jnp.exp(jnp.where(rmod >= off, b - _sroll(b, off), -jnp.inf))
            a_o = jnp.sum(x * _sroll(ky, off) * e, axis=-1, keepdims=True)
            return acc + a_o * _sroll(val, off)
        o = _each(diag, o, q, key, bc, v)
    q_rel = _each(lambda x, b, bs: x * jnp.exp(b - bs), q, bc, b_start)
    k_rel = _each(lambda x, b, be: x * jnp.exp(be - b), key, bc, b_end)
    for y in range(c // SUB_CHUNK - 1):
        def scaled(x, b, bs):
            end_y = jnp.sum(jnp.where(rcol == SUB_CHUNK * y + SUB_CHUNK - 1, b, 0.0), axis=0, keepdims=True)
            return x * jnp.exp(jnp.where(rcol >= SUB_CHUNK * (y + 1), bs - end_y, -jnp.inf))
        dq = _each(scaled, q_rel, bc, b_start)
        in_y = (ci >= SUB_CHUNK * y) & (ci < SUB_CHUNK * (y + 1))
        a_y = _each(lambda x, z: jnp.where(in_y, mm(x, z, NT, prec), 0.0), dq, k_rel)
        o = _each(lambda acc, a, val: acc + mm(a, val, NN, prec), o, a_y, v)
    k_state = _each(lambda x, bl, b: x * jnp.exp(bl - b), key, b_last, bc)
    st_new = _each(lambda z, bl, val, x: z * jnp.exp(bl) + mm(val, x, TN, prec), st, b_last, v, k_state)
    return _each(lambda x, z: _head_norm_gate(x, norm_w, z), o, gb), st_new


def hgrn_chunk(qb, fb, ib, gb, l0, l1, norm_w, st, prec=HGRN_PREC):
    y, st_new = hgrn_chunks([qb], [fb], [ib], [gb], [l0], [l1], norm_w, [st], prec)
    return y[0], st_new[0]


HEAD_VEC = (N_HEADS, 1, LANES)


class _Groups:
    def __init__(self, nc, hb, rev):
        self.nc, self.hb, self.ng, self.rev = nc, hb, N_HEADS // hb, rev

    def _c(self, c):
        return self.nc - 1 - c if self.rev else c

    def cols(self, slab):
        return pl.BlockSpec((CHUNK, self.hb * LANES), lambda c, g: (self._c(c), slab * self.ng + g))

    def tile(self, block):
        return pl.BlockSpec((CHUNK, LANES), lambda c, g: (self._c(c), block))

    def state(self):
        return pl.BlockSpec((None, self.hb, HEAD_DIM, HEAD_DIM), lambda c, g: (self._c(c), g, 0, 0))

    @staticmethod
    def whole(shape):
        return pl.BlockSpec(shape, lambda c, g: (0,) * len(shape))

    def head(self, g, j):
        return j if self.ng == 1 else g * self.hb + j


def _lanes(j):
    return slice(j * LANES, (j + 1) * LANES)


def gdn_fwd(qkv_c, proj, a_log_l, dt_l, norm_w, name, hb=8):
    t = qkv_c.shape[0]
    gr = _Groups(t // CHUNK, hb, rev=False)

    def body(q_ref, k_ref, v_ref, z_ref, ab_ref, al_ref, dt_ref, nw_ref, y_ref, hist_ref, s_ref):
        c, g = pl.program_id(0), pl.program_id(1)

        @pl.when(c == 0)
        def _():
            for j in range(hb):
                s_ref[gr.head(g, j)] = jnp.zeros((HEAD_DIM, HEAD_DIM), F32)

        hs = [gr.head(g, j) for j in range(hb)]
        heads = lambda ref: [ref[:, _lanes(j)] for j in range(hb)]
        s = [s_ref[h] for h in hs]
        for j in range(hb):
            hist_ref[j] = s[j]
        y, s_new = gdn_chunks(hs, heads(q_ref), heads(k_ref), heads(v_ref), heads(z_ref), ab_ref[...],
                              [al_ref[h] for h in hs], [dt_ref[h] for h in hs], nw_ref[...], s)
        for j in range(hb):
            y_ref[:, _lanes(j)] = y[j].astype(BF16)
            s_ref[hs[j]] = s_new[j]

    return pl.pallas_call(
        body, name=name, grid=(gr.nc, gr.ng),
        in_specs=[gr.cols(0), gr.cols(1), gr.cols(2), gr.cols(3), gr.tile(AB_BLOCK),
                  gr.whole(HEAD_VEC), gr.whole(HEAD_VEC), gr.whole((1, LANES))],
        out_specs=[gr.cols(0), gr.state()],
        out_shape=[jax.ShapeDtypeStruct((t, 2 * GDN_WIDTH), BF16),
                   jax.ShapeDtypeStruct((gr.nc, N_HEADS, HEAD_DIM, HEAD_DIM), F32)],
        scratch_shapes=[pltpu.VMEM((N_HEADS, HEAD_DIM, HEAD_DIM), F32)],
        compiler_params=_params(("arbitrary", "arbitrary")),
    )(qkv_c, qkv_c, qkv_c, proj, proj, a_log_l, dt_l, norm_w)


def gdn_bwd(qkv_c, proj, a_log_l, dt_l, norm_w, hist, dy, name):
    t = qkv_c.shape[0]
    hb = N_HEADS
    gr = _Groups(t // CHUNK, hb, rev=True)

    def body(q_ref, k_ref, v_ref, z_ref, ab_ref, al_ref, dt_ref, nw_ref, hist_ref, dy_ref,
             dqkv_ref, dz_ref, dab_ref, dal_ref, ddt_ref, dnw_ref, ds_ref):
        @pl.when(pl.program_id(0) == 0)
        def _():
            dal_ref[...] = jnp.zeros_like(dal_ref)
            ddt_ref[...] = jnp.zeros_like(ddt_ref)
            dnw_ref[...] = jnp.zeros_like(dnw_ref)
            ds_ref[...] = jnp.zeros_like(ds_ref)

        hs = list(range(hb))
        heads = lambda ref: [ref[:, _lanes(j)] for j in hs]
        _, vjp = jax.vjp(functools.partial(gdn_chunks, hs), heads(q_ref), heads(k_ref), heads(v_ref), heads(z_ref),
                         ab_ref[...], [al_ref[h] for h in hs], [dt_ref[h] for h in hs], nw_ref[...],
                         [hist_ref[h] for h in hs])
        dq, dk, dv, dz, dab, dal, ddt, dnw, ds = vjp((heads(dy_ref), [ds_ref[h] for h in hs]))
        for h in hs:
            dqkv_ref[:, _lanes(h)] = dq[h]
            dqkv_ref[:, _lanes(hb + h)] = dk[h]
            dqkv_ref[:, _lanes(2 * hb + h)] = dv[h]
            dz_ref[:, _lanes(h)] = dz[h].astype(BF16)
            dal_ref[h] += dal[h]
            ddt_ref[h] += ddt[h]
            ds_ref[h] = ds[h]
        dab_ref[...] = dab.astype(BF16)
        dnw_ref[...] += dnw

    return pl.pallas_call(
        body, name=name, grid=(gr.nc, 1),
        in_specs=[gr.cols(0), gr.cols(1), gr.cols(2), gr.cols(3), gr.tile(AB_BLOCK),
                  gr.whole(HEAD_VEC), gr.whole(HEAD_VEC), gr.whole((1, LANES)), gr.state(), gr.cols(0)],
        out_specs=[pl.BlockSpec((CHUNK, QKV_WIDTH), lambda c, g: (gr.nc - 1 - c, 0)), gr.cols(3), gr.tile(0),
                   gr.whole(HEAD_VEC), gr.whole(HEAD_VEC), gr.whole((1, LANES))],
        out_shape=[jax.ShapeDtypeStruct((t, QKV_WIDTH), F32), jax.ShapeDtypeStruct((t, CAT_WIDTH), BF16),
                   jax.ShapeDtypeStruct((t, LANES), BF16), jax.ShapeDtypeStruct(HEAD_VEC, F32),
                   jax.ShapeDtypeStruct(HEAD_VEC, F32), jax.ShapeDtypeStruct((1, LANES), F32)],
        scratch_shapes=[pltpu.VMEM((N_HEADS, HEAD_DIM, HEAD_DIM), F32)],
        compiler_params=_params(("arbitrary", "arbitrary")),
    )(qkv_c, qkv_c, qkv_c, proj, proj, a_log_l, dt_l, norm_w, hist, dy)


def hgrn_fwd(proj, l0, l1, norm_w, y, name, hb=8):
    t = proj.shape[0]
    gr = _Groups(t // CHUNK, hb, rev=False)

    def body(q_ref, f_ref, i_ref, g_ref, l0_ref, l1_ref, nw_ref, y_in, y_ref, hist_ref, s_ref):
        del y_in
        c, g = pl.program_id(0), pl.program_id(1)

        @pl.when(c == 0)
        def _():
            for j in range(hb):
                s_ref[gr.head(g, j)] = jnp.zeros((HEAD_DIM, HEAD_DIM), F32)

        hs = [gr.head(g, j) for j in range(hb)]
        heads = lambda ref: [ref[:, _lanes(j)] for j in range(hb)]
        s = [s_ref[h] for h in hs]
        for j in range(hb):
            hist_ref[j] = s[j]
        out, s_new = hgrn_chunks(heads(q_ref), heads(f_ref), heads(i_ref), heads(g_ref), [l0_ref[h] for h in hs],
                                 [l1_ref[h] for h in hs], nw_ref[...], s)
        for j in range(hb):
            y_ref[:, _lanes(j)] = out[j].astype(BF16)
            s_ref[hs[j]] = s_new[j]

    return pl.pallas_call(
        body, name=name, grid=(gr.nc, gr.ng),
        in_specs=[gr.cols(4), gr.cols(5), gr.cols(6), gr.cols(7), gr.whole(HEAD_VEC), gr.whole(HEAD_VEC),
                  gr.whole((1, LANES)), pl.BlockSpec(memory_space=pl.ANY)],
        out_specs=[gr.cols(1), gr.state()],
        out_shape=[jax.ShapeDtypeStruct((t, 2 * GDN_WIDTH), BF16),
                   jax.ShapeDtypeStruct((gr.nc, N_HEADS, HEAD_DIM, HEAD_DIM), F32)],
        scratch_shapes=[pltpu.VMEM((N_HEADS, HEAD_DIM, HEAD_DIM), F32)],
        input_output_aliases={7: 0},
        compiler_params=_params(("arbitrary", "arbitrary")),
    )(proj, proj, proj, proj, l0, l1, norm_w, y)


def hgrn_bwd(proj, l0, l1, norm_w, hist, dy, dproj, name):
    t = proj.shape[0]
    hb = N_HEADS
    gr = _Groups(t // CHUNK, hb, rev=True)

    def body(q_ref, f_ref, i_ref, g_ref, l0_ref, l1_ref, nw_ref, hist_ref, dy_ref, dproj_in,
             d_ref, dl0_ref, dl1_ref, dnw_ref, ds_ref):
        del dproj_in

        @pl.when(pl.program_id(0) == 0)
        def _():
            dl0_ref[...] = jnp.zeros_like(dl0_ref)
            dl1_ref[...] = jnp.zeros_like(dl1_ref)
            dnw_ref[...] = jnp.zeros_like(dnw_ref)
            ds_ref[...] = jnp.zeros_like(ds_ref)

        hs = list(range(hb))
        heads = lambda ref: [ref[:, _lanes(j)] for j in hs]
        _, vjp = jax.vjp(hgrn_chunks, heads(q_ref), heads(f_ref), heads(i_ref), heads(g_ref), [l0_ref[h] for h in hs],
                         [l1_ref[h] for h in hs], nw_ref[...], [hist_ref[h] for h in hs])
        dq, df, di, dg, dl0, dl1, dnw, ds = vjp((heads(dy_ref), [ds_ref[h] for h in hs]))
        for h in hs:
            for slab, val in enumerate((dq, df, di, dg)):
                d_ref[:, _lanes(slab * hb + h)] = val[h].astype(BF16)
            dl0_ref[h] += dl0[h]
            dl1_ref[h] += dl1[h]
            ds_ref[h] = ds[h]
        dnw_ref[...] += dnw

    return pl.pallas_call(
        body, name=name, grid=(gr.nc, 1),
        in_specs=[gr.cols(4), gr.cols(5), gr.cols(6), gr.cols(7), gr.whole(HEAD_VEC), gr.whole(HEAD_VEC),
                  gr.whole((1, LANES)), gr.state(), gr.cols(1), ANY_SPEC],
        out_specs=[pl.BlockSpec((CHUNK, 4 * GDN_WIDTH), lambda c, g: (gr.nc - 1 - c, 1)),
                   gr.whole(HEAD_VEC), gr.whole(HEAD_VEC), gr.whole((1, LANES))],
        out_shape=[jax.ShapeDtypeStruct(dproj.shape, BF16), jax.ShapeDtypeStruct(HEAD_VEC, F32),
                   jax.ShapeDtypeStruct(HEAD_VEC, F32), jax.ShapeDtypeStruct((1, LANES), F32)],
        scratch_shapes=[pltpu.VMEM((N_HEADS, HEAD_DIM, HEAD_DIM), F32)],
        input_output_aliases={9: 0},
        compiler_params=_params(("arbitrary", "arbitrary")),
    )(proj, proj, proj, proj, l0, l1, norm_w, hist, dy, dproj)


def _adamw(w, g, m, v):
    m = ADAM_B1 * m + (1.0 - ADAM_B1) * g
    v = ADAM_B2 * v + (1.0 - ADAM_B2) * jnp.square(g)
    m_hat = m / (1.0 - ADAM_B1 ** ADAM_STEP)
    v_hat = v / (1.0 - ADAM_B2 ** ADAM_STEP)
    delta = -ADAM_LR * (m_hat / (jnp.sqrt(v_hat) + ADAM_EPS) + ADAM_WD * w)
    return delta, m, v


def adamw_reduce(parts, w, m, v, name, rb=128):
    r, c = w.shape
    rb = min(rb, r)

    def body(p_ref, w_ref, m_ref, v_ref, g_ref, d_ref, mo_ref, vo_ref):
        g = p_ref[0].astype(F32)
        for d in range(1, N_DEV):
            g = g + p_ref[d].astype(F32)
        delta, mn, vn = _adamw(w_ref[...], g, m_ref[...], v_ref[...])
        g_ref[...] = g
        d_ref[...] = delta
        mo_ref[...] = mn
        vo_ref[...] = vn

    blk = pl.BlockSpec((rb, c), lambda i: (i, 0))
    return pl.pallas_call(
        body, name=name, grid=(r // rb,),
        in_specs=[pl.BlockSpec((N_DEV, rb, c), lambda i: (0, i, 0)), blk, blk, blk],
        out_specs=[blk] * 4, out_shape=[jax.ShapeDtypeStruct((r, c), F32)] * 4,
        compiler_params=_params(("parallel",)))(parts, w, m, v)


def adamw_small(w, g, m, v, name):
    def body(w_ref, g_ref, m_ref, v_ref, d_ref, mo_ref, vo_ref):
        delta, mn, vn = _adamw(w_ref[...], g_ref[...], m_ref[...], v_ref[...])
        d_ref[...] = delta
        mo_ref[...] = mn
        vo_ref[...] = vn

    vmem = pl.BlockSpec(memory_space=pltpu.VMEM)
    return pl.pallas_call(body, name=name, in_specs=[vmem] * 4, out_specs=[vmem] * 3,
                          out_shape=[jax.ShapeDtypeStruct(w.shape, F32)] * 3)(w, g, m, v)


def _pack(arrays):
    flat = jnp.concatenate([a.reshape(-1).astype(F32) for a in arrays])
    rows = -(-flat.shape[0] // (8 * LANES)) * 8
    return jnp.pad(flat, (0, rows * LANES - flat.shape[0])).reshape(rows, LANES)


def _unpack(packed, shapes):
    flat, out, off = packed.reshape(-1), [], 0
    for s in shapes:
        n = 1
        for d in s:
            n *= d
        out.append(flat[off:off + n].reshape(s))
        off += n
    return out


def _relu2_epilogue(acc, _):
    r = jnp.maximum(acc, 0.0)
    return acc, r * r


def _relu2_bwd_epilogue(acc, a1):
    return (acc * (2.0 * jnp.maximum(a1, 0.0)),)


def kernel(x, w_in, conv_w, gdn_a_log, gdn_dt_bias, gdn_norm_w, hgrn_lb_logits, hgrn_norm_w, w_out, norm_mix_w, norm_ffn_w, w_ff1, w_ff2, norm_final_w, loss_target, m_w_in, m_conv_w, m_gdn_a_log, m_gdn_dt_bias, m_gdn_norm_w, m_hgrn_lb_logits, m_hgrn_norm_w, m_w_out, m_norm_mix_w, m_norm_ffn_w, m_w_ff1, m_w_ff2, m_norm_final_w, v_w_in, v_conv_w, v_gdn_a_log, v_gdn_dt_bias, v_gdn_norm_w, v_hgrn_lb_logits, v_hgrn_norm_w, v_w_out, v_norm_mix_w, v_norm_ffn_w, v_w_ff1, v_w_ff2, v_norm_final_w):
    me = _my_flat()
    xs = x[0]
    target = loss_target[0]
    shard_in = w_in.shape[2]
    shard_conv = conv_w.shape[2]

    tok = lambda t: t[0:1, 0:1]
    own = lambda src: lax.dynamic_index_in_dim(src, me, 0, keepdims=False)

    g_in, g_conv = exchange([w_in[0].astype(BF16), conv_w[0]], gather=True, name="gather_w_in")
    h_g1, t_g1 = exchange_start([w_out[0].astype(BF16), w_ff1[0].astype(BF16)], True, "gather_mid_start", after=[g_in])
    h_g2, t_g2 = exchange_start([w_ff2[0].astype(BF16)], True, "gather_ff2_start", after=[t_g1])
    w_cat = weights_to_cat(g_in)
    conv_full = jnp.transpose(g_conv, (1, 0, 2)).reshape(4, QKV_WIDTH)

    lane_b = lambda p: jnp.broadcast_to(p.reshape(N_HEADS, 1, 1), HEAD_VEC)
    a_log_l, dt_l = lane_b(gdn_a_log[0]), lane_b(gdn_dt_bias[0])
    l0 = hgrn_lb_logits[0].reshape(HEAD_VEC)
    l1 = hgrn_lb_logits[1].reshape(HEAD_VEC)

    n1, r1 = rms_fwd(xs, norm_mix_w + tok(t_g1) + tok(t_g2), None, "rms_mix")
    proj = matmul(n1, w_cat, "nn", "in_proj", tn=CAT_WIDTH // 5)
    qkv_c = conv_fwd(proj, conv_full, "conv_fwd")
    y_half, hist_a = gdn_fwd(qkv_c, proj, a_log_l, dt_l, gdn_norm_w, "gdn_fwd")
    y, hist_b = hgrn_fwd(proj, l0, l1, hgrn_norm_w, y_half, "hgrn_fwd")
    (s_out, s_ff1), (l_out, l_ff1) = exchange_wait(h_g1, "gather_mid_wait", after=[y])
    w_out_full = _own_slot(l_out, s_out).reshape(D_MODEL, D_MODEL)
    w_ff1_sh = _own_slot(l_ff1, s_ff1)
    mix = matmul(y, w_out_full, "nn", "out_proj")
    h1, n2, r2 = rms_fwd(xs, norm_ffn_w, mix, "rms_ffn")
    a1, act = matmul(n2, w_ff1_sh, "nn", "ff1", out_dtypes=(F32, BF16), epilogue=_relu2_epilogue, b_shards=True)
    (s_ff2,), (l_ff2,) = exchange_wait(h_g2, "gather_ff2_wait", after=[act])
    w_ff2_full = _own_slot(l_ff2, s_ff2).reshape(D_FF, D_MODEL)
    ff = matmul(act, w_ff2_full, "nn", "ff2")
    loss_sum, dh2, dh2_b, d_final = loss_head(h1, ff, norm_final_w.reshape(1, D_MODEL), target, "loss_head")

    da1 = matmul(dh2_b, w_ff2_full, "nt", "d_act", out_dtypes=(BF16,), epilogue=_relu2_bwd_epilogue, extra=a1)
    dw_ff2 = matmul(act, dh2_b, "tn", "dw_ff2", out_dtypes=(BF16,), tk=1024)
    p_ff2 = dw_ff2.reshape(N_DEV, D_FF // N_DEV, D_MODEL)
    h_s1, t_s1 = exchange_start([p_ff2], False, "scatter_ff2_start")
    dn2 = matmul(da1, w_ff1_sh, "nt", "d_n2", after=[t_s1], b_shards=True)
    p_ff1 = matmul(n2, da1, "tn", "dw_ff1", out_dtypes=(BF16,), tn=D_FF // N_DEV, tk=1024, after=[t_s1], out_shards=True)
    h_s2, t_s2 = exchange_start([p_ff1], False, "scatter_ff1_start")
    dh1, dh1_b, d_ffn = rms_bwd(h1, r2, norm_ffn_w + tok(t_s2), dn2, dh2, "rms_ffn_bwd")
    dmix = matmul(dh1_b, w_out_full, "nt", "d_mix")
    dw_out = matmul(y, dh1_b, "tn", "dw_out", out_dtypes=(BF16,), tk=1024)
    p_out = dw_out.reshape(N_DEV, D_MODEL // N_DEV, D_MODEL)
    h_s3, t_s3 = exchange_start([p_out], False, "scatter_out_start")
    d_qkv_c, dproj, dab, d_alog_l, d_dt_l, d_gnw = gdn_bwd(
        qkv_c, proj, a_log_l, dt_l, gdn_norm_w + tok(t_s3), hist_a, dmix, "gdn_bwd")
    dproj, dl0, dl1, d_hnw = hgrn_bwd(proj, l0, l1, hgrn_norm_w + tok(t_s3), hist_b, dmix, dproj, "hgrn_bwd")
    dproj, d_conv_full = conv_bwd(proj, d_qkv_c, conv_full, dproj, "conv_bwd")
    dproj = lax.dynamic_update_slice(dproj, dab, (0, MAIN_WIDTH))
    dw_cat = matmul(n1, dproj, "tn", "dw_in", out_dtypes=(BF16,), tn=CAT_WIDTH // 5, tk=1024)
    p_in = cat_to_shards(dw_cat, shard_in)
    h_s4, t_s4 = exchange_start([p_in], False, "scatter_in_start")

    (s_ff2g,), (r_ff2,) = exchange_wait(h_s1, "scatter_ff2_wait", after=[t_s4])
    (s_ff1g,), (r_ff1,) = exchange_wait(h_s2, "scatter_ff1_wait", after=[t_s4])
    (s_outg,), (r_out,) = exchange_wait(h_s3, "scatter_out_wait", after=[t_s4])
    g_w_ff2, d_w_ff2, nm_w_ff2, nv_w_ff2 = adamw_reduce(
        _own_slot(r_ff2, own(s_ff2g)), w_ff2[0], m_w_ff2[0], v_w_ff2[0], "adamw_w_ff2")
    g_w_ff1, d_w_ff1, nm_w_ff1, nv_w_ff1 = adamw_reduce(
        _own_slot(r_ff1, own(s_ff1g)), w_ff1[0], m_w_ff1[0], v_w_ff1[0], "adamw_w_ff1")
    g_w_out, d_w_out, nm_w_out, nv_w_out = adamw_reduce(
        _own_slot(r_out, own(s_outg)), w_out[0], m_w_out[0], v_w_out[0], "adamw_w_out")
    dn1 = matmul(dproj, w_cat, "nt", "d_n1", tk=CAT_WIDTH // 5, after=[t_s4])
    dx, _, d_mix = rms_bwd(xs, r1, norm_mix_w, dn1, dh1, "rms_mix_bwd")
    (s_ing,), (r_in,) = exchange_wait(h_s4, "scatter_in_wait", after=[dx, d_w_ff2, d_w_ff1, d_w_out])
    g_w_in, d_w_in, nm_w_in, nv_w_in = adamw_reduce(
        _own_slot(r_in, own(s_ing)), w_in[0], m_w_in[0], v_w_in[0], "adamw_w_in")

    d_lb = jnp.stack([dl0.reshape(GDN_WIDTH), dl1.reshape(GDN_WIDTH)])
    small_shapes = [(1, N_HEADS), (1, N_HEADS), (1, HEAD_DIM), (2, GDN_WIDTH), (1, HEAD_DIM), (1, D_MODEL),
                    (1, D_MODEL), (D_MODEL,), (4, QKV_WIDTH)]
    small = _pack([d_alog_l[:, 0, 0], d_dt_l[:, 0, 0], d_gnw, d_lb, d_hnw, d_mix, d_ffn, d_final, d_conv_full])
    red = allreduce_small(small, "allreduce_small")
    g_alog, g_dt, g_gnw, g_lb, g_hnw, g_mix, g_ffn, g_final, g_conv_full = _unpack(red, small_shapes)
    g_conv = lax.dynamic_slice(g_conv_full, (0, me * shard_conv), (4, shard_conv)).reshape(1, 4, shard_conv)
    small_g = [g_alog, g_dt, g_gnw, g_lb, g_hnw, g_mix, g_ffn, g_final, g_conv]
    small_w = [gdn_a_log, gdn_dt_bias, gdn_norm_w, hgrn_lb_logits, hgrn_norm_w, norm_mix_w, norm_ffn_w, norm_final_w, conv_w]
    small_m = [m_gdn_a_log, m_gdn_dt_bias, m_gdn_norm_w, m_hgrn_lb_logits, m_hgrn_norm_w, m_norm_mix_w, m_norm_ffn_w,
               m_norm_final_w, m_conv_w]
    small_v = [v_gdn_a_log, v_gdn_dt_bias, v_gdn_norm_w, v_hgrn_lb_logits, v_hgrn_norm_w, v_norm_mix_w, v_norm_ffn_w,
               v_norm_final_w, v_conv_w]
    shapes = [a.shape for a in small_w]
    d_s, m_s, v_s = adamw_small(_pack(small_w), _pack(small_g), _pack(small_m), _pack(small_v), "adamw_small")
    d_alog, d_dt, d_gn, d_lbl, d_hn, d_nm, d_nf, d_nfin, d_cw = _unpack(d_s, shapes)
    m_alog, m_dt, m_gn, m_lbl, m_hn, m_nm, m_nf, m_nfin, m_cw = _unpack(m_s, shapes)
    v_alog, v_dt, v_gn, v_lbl, v_hn, v_nm, v_nf, v_nfin, v_cw = _unpack(v_s, shapes)

    loss = lax.psum(loss_sum[0, 0], ("x", "y", "c"))
    lead = lambda a: a[None]
    grads = [lead(g_w_in), g_conv, g_alog, g_dt, g_gnw, g_lb, g_hnw, lead(g_w_out), g_mix, g_ffn,
             lead(g_w_ff1), lead(g_w_ff2), g_final]
    deltas = [lead(d_w_in), d_cw, d_alog, d_dt, d_gn, d_lbl, d_hn, lead(d_w_out), d_nm, d_nf,
              lead(d_w_ff1), lead(d_w_ff2), d_nfin]
    new_m = [lead(nm_w_in), m_cw, m_alog, m_dt, m_gn, m_lbl, m_hn, lead(nm_w_out), m_nm, m_nf,
             lead(nm_w_ff1), lead(nm_w_ff2), m_nfin]
    new_v = [lead(nv_w_in), v_cw, v_alog, v_dt, v_gn, v_lbl, v_hn, lead(nv_w_out), v_nm, v_nf,
             lead(nv_w_ff1), lead(nv_w_ff2), v_nfin]
    return (loss, dx[None], *grads, *deltas, *new_m, *new_v)
```

```python
import functools

import jax
import jax.numpy as jnp
from jax import lax
from jax.experimental import pallas as pl
from jax.experimental.pallas import tpu as pltpu

F32 = jnp.float32
BF16 = jnp.bfloat16
HI = lax.Precision.HIGHEST

N_DEV = 8
D_MODEL = 2048
CHUNK = 64
SUB_CHUNK = 16
HEAD_DIM = 128
N_HEADS = 8
GDN_WIDTH = N_HEADS * HEAD_DIM
D_FF = 4 * D_MODEL
QKV_WIDTH = 3 * GDN_WIDTH
MAIN_WIDTH = 8 * GDN_WIDTH
CAT_WIDTH = MAIN_WIDTH + 128
AB_BLOCK = MAIN_WIDTH // 128
NORM_EPS = 1e-6
L2_EPS = 1e-6
LANES = 128
VMEM_LIMIT = 56 * 1024 * 1024

ADAM_LR = 0.001
ADAM_B1 = 0.9
ADAM_B2 = 0.999
ADAM_EPS = 1e-08
ADAM_WD = 0.01
ADAM_STEP = 10

MESH = pl.DeviceIdType.MESH


def _params(sem=None):
    return pltpu.CompilerParams(dimension_semantics=sem, vmem_limit_bytes=VMEM_LIMIT)


def _dot(a, b, dims, prec=None):
    return lax.dot_general(a, b, (dims, ((), ())), precision=prec, preferred_element_type=F32)


NN = ((1,), (0,))
NT = ((1,), (1,))
TN = ((0,), (0,))


def _split_bf16(x, pieces):
    out = []
    for _ in range(pieces - 1):
        p = x.astype(BF16)
        out.append(p)
        x = x - p.astype(F32)
    out.append(x.astype(BF16))
    return out


def _mm_raw(a, b, dims, prec):
    if prec == "hi":
        return _dot(a, b, dims, HI)
    if prec == "bf":
        return _dot(a.astype(BF16), b.astype(BF16), dims)
    a_hi, a_lo = _split_bf16(a, 2)
    b_hi, b_lo = _split_bf16(b, 2)
    return _dot(a_hi, b_hi, dims) + (_dot(a_hi, b_lo, dims) + _dot(a_lo, b_hi, dims))


@functools.partial(jax.custom_vjp, nondiff_argnums=(2, 3))
def mm(a, b, dims, prec):
    return _mm_raw(a, b, dims, prec)


def _mm_fwd(a, b, dims, prec):
    return _mm_raw(a, b, dims, prec), (a, b)


def _mm_bwd(dims, prec, res, ct):
    a, b = res
    if dims == NN:
        return _mm_raw(ct, b, NT, prec), _mm_raw(a, ct, TN, prec)
    if dims == NT:
        return _mm_raw(ct, b, NN, prec), _mm_raw(ct, a, TN, prec)
    return _mm_raw(b, ct, NT, prec), _mm_raw(a, ct, NN, prec)


mm.defvjp(_mm_fwd, _mm_bwd)


def _sel_raw(sel, x, dims):
    sel = sel.astype(BF16)
    p0, p1, p2 = _split_bf16(x, 3)
    return _dot(sel, p0, dims) + (_dot(sel, p1, dims) + _dot(sel, p2, dims))


@jax.custom_vjp
def sel_mm3(sel, x):
    c = x.shape[0]
    full = _sel_raw(sel, x, NN)
    return full[:c], full[c:2 * c], full[2 * c:]


def _sel_fwd(sel, x):
    return sel_mm3(sel, x), sel


def _sel_bwd(sel, cts):
    return jnp.zeros_like(sel), _sel_raw(sel, jnp.concatenate(cts, axis=0), TN)


sel_mm3.defvjp(_sel_fwd, _sel_bwd)


def _my_flat():
    return 4 * lax.axis_index("x") + 2 * lax.axis_index("y") + lax.axis_index("c")


def _peer(k):
    x, y, c = lax.axis_index("x"), lax.axis_index("y"), lax.axis_index("c")
    kx, ky, kc = (k >> 2) & 1, (k >> 1) & 1, k & 1
    px = (1 - x) if kx else x
    py = (1 - y) if ky else y
    pc = (1 - c) if kc else c
    return (px, py, pc), 4 * px + 2 * py + pc


def gather_two_level(xs, name):
    n = len(xs)

    def body(*refs):
        x_refs, y_refs = refs[:n], refs[n:2 * n]
        send_sems, recv_sems, local_sems = refs[2 * n:]
        x, y, c = lax.axis_index("x"), lax.axis_index("y"), lax.axis_index("c")
        me, sibling = (x, y, c), (x, y, 1 - c)
        chips = [(1 - x, y), (x, 1 - y), (1 - x, 1 - y)]
        flat = lambda p: 4 * p[0] + 2 * p[1] + p[2]

        def copy(a, k, block, to, src=None):
            return pltpu.make_async_remote_copy(
                src_ref=y_refs[a].at[flat(block)] if src is None else src, dst_ref=y_refs[a].at[flat(block)],
                send_sem=send_sems.at[a, k], recv_sem=recv_sems.at[a, k], device_id=to, device_id_type=MESH)

        mine = [pltpu.make_async_copy(x_refs[a], y_refs[a].at[flat(me)], local_sems.at[a]) for a in range(n)]
        for cp in mine:
            cp.start()
        first = [copy(a, 0, me, sibling, src=x_refs[a]) for a in range(n)]
        first += [copy(a, 1 + j, me, (*chip, c), src=x_refs[a]) for j, chip in enumerate(chips) for a in range(n)]
        for cp in first:
            cp.start()
        passed = []
        for j, chip in enumerate(chips):
            for a in range(n):
                copy(a, 1 + j, (*chip, c), me).wait_recv()
                cp = copy(a, 4 + j, (*chip, c), sibling)
                cp.start()
                passed.append(cp)
        for a in range(n):
            copy(a, 0, sibling, me).wait_recv()
        for j, chip in enumerate(chips):
            for a in range(n):
                copy(a, 4 + j, (*chip, 1 - c), me).wait_recv()
        for cp in first + passed:
            cp.wait_send()
        for cp in mine:
            cp.wait()

    any_spec = pl.BlockSpec(memory_space=pl.ANY)
    return pl.pallas_call(
        body, name=name, out_shape=[jax.ShapeDtypeStruct((N_DEV,) + x.shape, x.dtype) for x in xs],
        in_specs=[any_spec] * n, out_specs=[any_spec] * n,
        scratch_shapes=[pltpu.SemaphoreType.DMA((n, N_DEV - 1)), pltpu.SemaphoreType.DMA((n, N_DEV - 1)),
                        pltpu.SemaphoreType.DMA((n,))],
    )(*xs)


HBM_SPEC = pl.BlockSpec(memory_space=pltpu.HBM)
SEM_SPEC = pl.BlockSpec(memory_space=pltpu.SEMAPHORE)
ANY_SPEC = pl.BlockSpec(memory_space=pl.ANY)
DATAFLOW = pltpu.SideEffectType.DATAFLOW_SIDE_EFFECTING


def _in_hbm(x):
    return pltpu.with_memory_space_constraint(x, pltpu.HBM)


def exchange_start(xs, gather, name, after=()):
    n, n_after = len(xs), len(after)

    def body(*refs):
        x_refs, land_refs = refs[:n], refs[n:2 * n]
        sems = refs[2 * n + n_after:2 * n + n_after + 2 * n]
        token = refs[-1]
        me = _my_flat()
        for k in range(1, N_DEV):
            peer, peer_flat = _peer(k)
            for a in range(n):
                src = x_refs[a] if gather else x_refs[a].at[peer_flat]
                pltpu.make_async_remote_copy(src_ref=src, dst_ref=land_refs[a].at[me], send_sem=sems[a],
                                             recv_sem=sems[n + a], device_id=peer, device_id_type=MESH).start()
        token[...] = jnp.zeros_like(token)

    lands = [_in_hbm(lax.empty(((N_DEV,) + x.shape) if gather else x.shape, x.dtype)) for x in xs]
    hbm_out = [pltpu.HBM(x.shape, x.dtype) for x in xs] + [pltpu.HBM(l.shape, l.dtype) for l in lands]
    res = pl.pallas_call(
        body, name=name,
        out_shape=(*([pltpu.SemaphoreType.DMA(())] * (2 * n)), *hbm_out, jax.ShapeDtypeStruct((8, LANES), F32)),
        in_specs=[HBM_SPEC] * (2 * n) + [ANY_SPEC] * n_after,
        out_specs=(*([SEM_SPEC] * (2 * n)), *([HBM_SPEC] * (2 * n)), pl.BlockSpec(memory_space=pltpu.VMEM)),
        input_output_aliases={i: 2 * n + i for i in range(2 * n)},
        compiler_params=pltpu.CompilerParams(has_side_effects=DATAFLOW),
    )(*[_in_hbm(x) for x in xs], *lands, *after)
    return (list(res[:2 * n]), list(res[2 * n:3 * n]), list(res[3 * n:4 * n])), res[-1]


def exchange_wait(handle, name, after=()):
    sems, xs, lands = handle
    n, n_after = len(xs), len(after)

    def body(*refs):
        land_refs = refs[n:2 * n]
        sem_refs = refs[2 * n:4 * n]
        for a in range(n):
            seven = land_refs[a].at[pl.ds(0, N_DEV - 1)]
            cp = pltpu.make_async_remote_copy(src_ref=seven, dst_ref=seven, send_sem=sem_refs[a],
                                              recv_sem=sem_refs[n + a], device_id=_peer(1)[0], device_id_type=MESH)
            cp.wait_send()
            cp.wait_recv()

    res = pl.pallas_call(
        body, name=name,
        out_shape=[pltpu.HBM(x.shape, x.dtype) for x in xs] + [pltpu.HBM(l.shape, l.dtype) for l in lands],
        in_specs=[HBM_SPEC] * (2 * n) + [SEM_SPEC] * (2 * n) + [ANY_SPEC] * n_after,
        out_specs=[HBM_SPEC] * (2 * n),
        input_output_aliases={i: i for i in range(2 * n)},
        compiler_params=pltpu.CompilerParams(has_side_effects=DATAFLOW),
    )(*xs, *lands, *sems, *after)
    return list(res[:n]), list(res[n:])


def _own_slot(land, block):
    return lax.dynamic_update_slice(land, block[None], (_my_flat(),) + (0,) * block.ndim)


def allreduce_small(x, name):
    rows = x.shape[0]

    def body(x_ref, o_ref, buf, send_sems, recv_sems):
        me = _my_flat()
        buf[me] = x_ref[...]
        sends = []
        for k in range(1, N_DEV):
            peer, _ = _peer(k)
            cp = pltpu.make_async_remote_copy(
                src_ref=x_ref, dst_ref=buf.at[me], send_sem=send_sems.at[k], recv_sem=recv_sems.at[k],
                device_id=peer, device_id_type=MESH)
            cp.start()
            sends.append(cp)
        for k in range(1, N_DEV):
            peer, peer_flat = _peer(k)
            pltpu.make_async_remote_copy(
                src_ref=x_ref, dst_ref=buf.at[peer_flat], send_sem=send_sems.at[k], recv_sem=recv_sems.at[k],
                device_id=peer, device_id_type=MESH).wait_recv()
        for cp in sends:
            cp.wait_send()
        acc = buf[0]
        for d in range(1, N_DEV):
            acc = acc + buf[d]
        o_ref[...] = acc

    vmem = pl.BlockSpec(memory_space=pltpu.VMEM)
    return pl.pallas_call(
        body, name=name, out_shape=jax.ShapeDtypeStruct((rows, LANES), F32),
        in_specs=[vmem], out_specs=vmem,
        scratch_shapes=[pltpu.VMEM((N_DEV, rows, LANES), F32),
                        pltpu.SemaphoreType.DMA((N_DEV,)), pltpu.SemaphoreType.DMA((N_DEV,))],
    )(x)


def matmul(a, b, mode, name, out_dtypes=(F32,), epilogue=None, extra=None, tm=1024, tn=1024, tk=2048, after=(),
           b_shards=False, out_shards=False, k_group=1):
    if b_shards:
        n_sh, b_rows, b_cols = b.shape
    if mode == "nn":
        (m, kd), n = a.shape, (n_sh * b_cols if b_shards else b.shape[1])
        if b_shards:
            tn = b_cols
    elif mode == "nt":
        (m, kd), n = a.shape, (b_rows if b_shards else b.shape[0])
        if b_shards:
            tk = k_group * b_cols
    else:
        (kd, m), n = a.shape, b.shape[1]
    tm, tn, tk = min(tm, m), min(tn, n), min(tk, kd)
    assert m % tm == 0 and n % tn == 0 and kd % tk == 0, (name, m, n, kd, tm, tn, tk)
    ksteps = kd // tk
    dims = {"nn": NN, "nt": NT, "tn": TN}[mode]
    n_out = len(out_dtypes)
    n_in = 2 + (extra is not None) + len(after)

    def finish(acc, e_ref, o_refs):
        outs = (acc,) if epilogue is None else epilogue(acc, e_ref[...] if e_ref is not None else None)
        for o_ref, o in zip(o_refs, outs):
            o_ref[...] = o.astype(o_ref.dtype)

    def product(a_ref, b_ref):
        if mode == "nt" and b_shards:
            w = b_cols
            parts = [_dot(a_ref[:, s * w:(s + 1) * w], b_ref[s], dims) for s in range(k_group)]
            return functools.reduce(lambda p, q: p + q, parts)
        return _dot(a_ref[...], b_ref[...], dims)

    def body(*refs):
        a_ref, b_ref = refs[0], refs[1]
        e_ref = refs[2] if extra is not None else None
        o_refs = refs[n_in:n_in + n_out]
        if ksteps == 1:
            finish(product(a_ref, b_ref), e_ref, o_refs)
            return
        acc_ref = refs[-1]
        kk = pl.program_id(2)

        @pl.when(kk == 0)
        def _():
            acc_ref[...] = jnp.zeros_like(acc_ref)

        acc_ref[...] += product(a_ref, b_ref)

        @pl.when(kk == ksteps - 1)
        def _():
            finish(acc_ref[...], e_ref, o_refs)

    if mode == "nn":
        a_spec = pl.BlockSpec((tm, tk), lambda i, j, k: (i, k))
        b_spec = (pl.BlockSpec((None, tk, tn), lambda i, j, k: (j, k, 0)) if b_shards
                  else pl.BlockSpec((tk, tn), lambda i, j, k: (k, j)))
    elif mode == "nt":
        a_spec = pl.BlockSpec((tm, tk), lambda i, j, k: (i, k))
        b_spec = (pl.BlockSpec((k_group, tn, b_cols), lambda i, j, k: (k, j, 0)) if b_shards
                  else pl.BlockSpec((tn, tk), lambda i, j, k: (j, k)))
    else:
        a_spec = pl.BlockSpec((tk, tm), lambda i, j, k: (k, i))
        b_spec = pl.BlockSpec((tk, tn), lambda i, j, k: (k, j))
    o_spec = pl.BlockSpec((tm, tn), lambda i, j, k: (i, j))
    res_spec = pl.BlockSpec((None, tm, tn), lambda i, j, k: (j, i, 0)) if out_shards else o_spec
    res_shape = (n // tn, m, tn) if out_shards else (m, n)
    in_specs = [a_spec, b_spec] + ([o_spec] if extra is not None else []) + [ANY_SPEC] * len(after)
    args = (a, b) + ((extra,) if extra is not None else ()) + tuple(after)
    res = pl.pallas_call(
        body, name=name, grid=(m // tm, n // tn, ksteps),
        in_specs=in_specs, out_specs=[res_spec] * n_out,
        out_shape=[jax.ShapeDtypeStruct(res_shape, dt) for dt in out_dtypes],
        scratch_shapes=[pltpu.VMEM((tm, tn), F32)] if ksteps > 1 else [],
        compiler_params=_params(("parallel", "parallel", "arbitrary")),
    )(*args)
    return res if n_out > 1 else res[0]


GATE_COL = 4 * GDN_WIDTH
RELAYOUT_ROWS = 256


def _cat_of_win(j):
    if j < GATE_COL:
        return j
    if j < GATE_COL + 2 * N_HEADS:
        return MAIN_WIDTH + (j - GATE_COL)
    return j - 2 * N_HEADS


def _win_of_cat(c):
    if c < GATE_COL:
        return c
    if c < MAIN_WIDTH:
        return c + 2 * N_HEADS
    if c < MAIN_WIDTH + 2 * N_HEADS:
        return GATE_COL + (c - MAIN_WIDTH)
    return None


def _runs(first, count, mapping):
    runs, i = [], 0
    while i < count:
        start, n = mapping(first + i), 1
        while i + n < count and mapping(first + i + n) == start + n:
            n += 1
        runs.append((start, n))
        i += n
    return runs


def weights_to_cat(g_in):
    n_dev, rows, shard = g_in.shape

    def body(x_ref, o_ref):
        for b in range(CAT_WIDTH // LANES):
            live = sum(_win_of_cat(LANES * b + i) is not None for i in range(LANES))
            parts = []
            for start, n in _runs(LANES * b, live, _win_of_cat):
                while n > 0:
                    d, o = divmod(start, shard)
                    take = min(n, shard - o)
                    parts.append(x_ref[d, :, o:o + take])
                    start, n = start + take, n - take
            if live < LANES:
                parts.append(jnp.zeros((RELAYOUT_ROWS, LANES - live), g_in.dtype))
            o_ref[:, LANES * b:LANES * (b + 1)] = parts[0] if len(parts) == 1 else jnp.concatenate(parts, axis=1)

    return pl.pallas_call(
        body, name="weights_to_cat", grid=(rows // RELAYOUT_ROWS,),
        in_specs=[pl.BlockSpec((n_dev, RELAYOUT_ROWS, shard), lambda i: (0, i, 0))],
        out_specs=pl.BlockSpec((RELAYOUT_ROWS, CAT_WIDTH), lambda i: (i, 0)),
        out_shape=jax.ShapeDtypeStruct((rows, CAT_WIDTH), g_in.dtype),
        compiler_params=_params(("parallel",)))(g_in)


def cat_to_shards(dw_cat, shard):
    rows = dw_cat.shape[0]

    def body(x_ref, o_ref):
        for d in range(N_DEV):
            for t0 in range(0, shard, LANES):
                width = min(LANES, shard - t0)
                parts = [x_ref[:, c:c + n] for c, n in _runs(d * shard + t0, width, _cat_of_win)]
                o_ref[d, :, t0:t0 + width] = parts[0] if len(parts) == 1 else jnp.concatenate(parts, axis=1)

    return pl.pallas_call(
        body, name="cat_to_shards", grid=(rows // RELAYOUT_ROWS,),
        in_specs=[pl.BlockSpec((RELAYOUT_ROWS, CAT_WIDTH), lambda i: (i, 0))],
        out_specs=pl.BlockSpec((N_DEV, RELAYOUT_ROWS, shard), lambda i: (0, i, 0)),
        out_shape=jax.ShapeDtypeStruct((N_DEV, rows, shard), dw_cat.dtype),
        compiler_params=_params(("parallel",)))(dw_cat)


ROW_BLOCK = 256


def rms_fwd(x, w, add, name):
    t, d = x.shape
    has_add = add is not None

    def body(*refs):
        x_ref, w_ref = refs[0], refs[1]
        rest = refs[2:]
        if has_add:
            add_ref, h_ref, n_ref, r_ref = rest
            h = x_ref[...] + add_ref[...]
            h_ref[...] = h
        else:
            n_ref, r_ref = rest
            h = x_ref[...]
        r = lax.rsqrt(jnp.mean(h * h, axis=-1, keepdims=True) + NORM_EPS)
        n_ref[...] = (h * r * w_ref[...]).astype(BF16)
        r_ref[...] = r

    row = pl.BlockSpec((ROW_BLOCK, d), lambda i: (i, 0))
    wspec = pl.BlockSpec((1, d), lambda i: (0, 0))
    rspec = pl.BlockSpec((ROW_BLOCK, 1), lambda i: (i, 0))
    in_specs = [row, wspec] + ([row] if has_add else [])
    out_specs = ([row] if has_add else []) + [row, rspec]
    out_shape = ([jax.ShapeDtypeStruct((t, d), F32)] if has_add else []) + [
        jax.ShapeDtypeStruct((t, d), BF16), jax.ShapeDtypeStruct((t, 1), F32)]
    args = (x, w) + ((add,) if has_add else ())
    return pl.pallas_call(body, name=name, grid=(t // ROW_BLOCK,), in_specs=in_specs, out_specs=out_specs,
                          out_shape=out_shape, compiler_params=_params(("parallel",)))(*args)


def loss_head(h1, delta, w, target, name):
    t, d = h1.shape

    def body(h_ref, dl_ref, w_ref, t_ref, loss_ref, dh_ref, dhb_ref, dw_ref):
        @pl.when(pl.program_id(0) == 0)
        def _():
            loss_ref[...] = jnp.zeros_like(loss_ref)
            dw_ref[...] = jnp.zeros_like(dw_ref)

        h = h_ref[...] + dl_ref[...]
        wv = w_ref[...]
        r = lax.rsqrt(jnp.mean(h * h, axis=-1, keepdims=True) + NORM_EPS)
        yn = h * r
        e = yn * wv - t_ref[...]
        loss_ref[...] += 0.5 * jnp.sum(jnp.sum(e * e, axis=-1, keepdims=True), axis=0, keepdims=True) / d
        dy = e / d
        dw_ref[...] += jnp.sum(dy * yn, axis=0, keepdims=True)
        dyn = dy * wv
        dh = r * (dyn - yn * jnp.mean(dyn * yn, axis=-1, keepdims=True))
        dh_ref[...] = dh
        dhb_ref[...] = dh.astype(BF16)

    row = pl.BlockSpec((ROW_BLOCK, d), lambda i: (i, 0))
    wspec = pl.BlockSpec((1, d), lambda i: (0, 0))
    one = pl.BlockSpec((1, 1), lambda i: (0, 0))
    return pl.pallas_call(
        body, name=name, grid=(t // ROW_BLOCK,),
        in_specs=[row, row, wspec, row], out_specs=[one, row, row, wspec],
        out_shape=[jax.ShapeDtypeStruct((1, 1), F32), jax.ShapeDtypeStruct((t, d), F32),
                   jax.ShapeDtypeStruct((t, d), BF16), jax.ShapeDtypeStruct((1, d), F32)],
        compiler_params=_params(("arbitrary",)))(h1, delta, w, target)


def rms_bwd(h, r, w, dn, dres, name):
    t, d = h.shape

    def body(h_ref, r_ref, w_ref, dn_ref, dres_ref, dh_ref, dhb_ref, dw_ref):
        @pl.when(pl.program_id(0) == 0)
        def _():
            dw_ref[...] = jnp.zeros_like(dw_ref)

        rv = r_ref[...]
        yn = h_ref[...] * rv
        dnv = dn_ref[...]
        dw_ref[...] += jnp.sum(dnv * yn, axis=0, keepdims=True)
        dyn = dnv * w_ref[...]
        dh = dres_ref[...] + rv * (dyn - yn * jnp.mean(dyn * yn, axis=-1, keepdims=True))
        dh_ref[...] = dh
        dhb_ref[...] = dh.astype(BF16)

    row = pl.BlockSpec((ROW_BLOCK, d), lambda i: (i, 0))
    wspec = pl.BlockSpec((1, d), lambda i: (0, 0))
    rspec = pl.BlockSpec((ROW_BLOCK, 1), lambda i: (i, 0))
    return pl.pallas_call(
        body, name=name, grid=(t // ROW_BLOCK,),
        in_specs=[row, rspec, wspec, row, row], out_specs=[row, row, wspec],
        out_shape=[jax.ShapeDtypeStruct((t, d), F32), jax.ShapeDtypeStruct((t, d), BF16),
                   jax.ShapeDtypeStruct((1, d), F32)],
        compiler_params=_params(("arbitrary",)))(h, r, w, dn, dres)


CONV_TB = 512
CONV_CB = 512
HALO = 8


def _silu(x):
    return x * jax.nn.sigmoid(x)


def _conv_pre(xcat, w, rows):
    acc = None
    for j in range(4):
        sh = 3 - j
        xs = xcat if sh == 0 else pltpu.roll(xcat, sh, 0)
        term = xs[HALO:HALO + rows] * w[j:j + 1, :]
        acc = term if acc is None else acc + term
    return acc


def conv_fwd(proj, conv_w, name):
    t = proj.shape[0]
    nb = CONV_TB // HALO

    def body(x_ref, prev_ref, w_ref, o_ref):
        prev = jnp.where(pl.program_id(1) == 0, 0.0, prev_ref[...])
        xcat = jnp.concatenate([prev, x_ref[...]], axis=0)
        o_ref[...] = _silu(_conv_pre(xcat, w_ref[...], CONV_TB))

    return pl.pallas_call(
        body, name=name, grid=(QKV_WIDTH // CONV_CB, t // CONV_TB),
        in_specs=[pl.BlockSpec((CONV_TB, CONV_CB), lambda c, i: (i, c)),
                  pl.BlockSpec((HALO, CONV_CB), lambda c, i: (jnp.maximum(i * nb - 1, 0), c)),
                  pl.BlockSpec((4, CONV_CB), lambda c, i: (0, c))],
        out_specs=pl.BlockSpec((CONV_TB, CONV_CB), lambda c, i: (i, c)),
        out_shape=jax.ShapeDtypeStruct((t, QKV_WIDTH), F32),
        compiler_params=_params(("parallel", "parallel")))(proj, proj, conv_w)


def conv_bwd(proj, dout, conv_w, dproj, name):
    t = proj.shape[0]
    nb = CONV_TB // HALO
    nt = t // CONV_TB
    rows = CONV_TB + HALO

    def body(x_ref, prev_ref, next_ref, d_ref, dnext_ref, w_ref, dproj_in, dx_ref, dw_ref):
        del dproj_in
        i = pl.program_id(1)

        @pl.when(i == 0)
        def _():
            dw_ref[...] = jnp.zeros_like(dw_ref)

        w = w_ref[...]
        prev = jnp.where(i == 0, 0.0, prev_ref[...])
        last = i == nt - 1
        xcat = jnp.concatenate([prev, x_ref[...], next_ref[...]], axis=0)
        pre = _conv_pre(xcat, w, rows)
        dcat = jnp.concatenate([d_ref[...], jnp.where(last, 0.0, dnext_ref[...])], axis=0)
        sg = jax.nn.sigmoid(pre)
        dpre = dcat * (sg * (1.0 + pre * (1.0 - sg)))
        dx = None
        for j in range(4):
            sh = 3 - j
            ds = dpre if sh == 0 else pltpu.roll(dpre, rows - sh, 0)
            term = ds[:CONV_TB] * w[j:j + 1, :]
            dx = term if dx is None else dx + term
        dx_ref[...] = dx.astype(BF16)
        dcur = dpre[:CONV_TB]
        parts = []
        for j in range(4):
            sh = 3 - j
            xs = xcat if sh == 0 else pltpu.roll(xcat, sh, 0)
            parts.append(jnp.sum(dcur * xs[HALO:HALO + CONV_TB], axis=0, keepdims=True))
        dw_ref[...] += jnp.concatenate(parts, axis=0)

    cur = pl.BlockSpec((CONV_TB, CONV_CB), lambda c, i: (i, c))
    halo_prev = pl.BlockSpec((HALO, CONV_CB), lambda c, i: (jnp.maximum(i * nb - 1, 0), c))
    halo_next = pl.BlockSpec((HALO, CONV_CB), lambda c, i: (jnp.minimum((i + 1) * nb, nt * nb - 1), c))
    taps = pl.BlockSpec((4, CONV_CB), lambda c, i: (0, c))
    return pl.pallas_call(
        body, name=name, grid=(QKV_WIDTH // CONV_CB, nt),
        in_specs=[cur, halo_prev, halo_next, cur, halo_next, taps, ANY_SPEC],
        out_specs=[cur, taps],
        out_shape=[jax.ShapeDtypeStruct(dproj.shape, BF16), jax.ShapeDtypeStruct((4, QKV_WIDTH), F32)],
        input_output_aliases={6: 0},
        compiler_params=_params(("parallel", "arbitrary")))(proj, proj, proj, dout, dout, conv_w, dproj)


def _iota2(shape, axis):
    return lax.broadcasted_iota(jnp.int32, shape, axis)


def _softplus(x):
    return jnp.maximum(x, 0.0) + jnp.log(1.0 + jnp.exp(-jnp.abs(x)))


def _head_norm_gate(o, norm_w, gate):
    return o * lax.rsqrt(jnp.mean(o * o, axis=-1, keepdims=True) + NORM_EPS) * norm_w * _silu(gate)


GDN_PREC = ("bf", "bf")
HGRN_PREC = "bf"


def _each(fn, *cols):
    return [fn(*a) for a in zip(*cols)]


def gdn_chunks(hs, qc, kc, vc, zc, ab, a_log_l, dt_l, norm_w, s, prec=GDN_PREC):
    p_inv, p_mm = prec
    c = CHUNK
    ri, ci = _iota2((c, c), 0), _iota2((c, c), 1)
    incl, strict, eye = ri >= ci, ri > ci, ri == ci
    lane = _iota2((c, LANES), 1)
    last_row = _iota2((c, 1), 0) == c - 1
    rowsum = lambda x: jnp.sum(x, axis=1, keepdims=True)

    def row(col):
        return jnp.sum(jnp.where(eye, col, 0.0), axis=0, keepdims=True)

    q = _each(lambda x: x * lax.rsqrt(rowsum(x * x) + L2_EPS) * (HEAD_DIM ** -0.5), qc)
    k = _each(lambda x: x * lax.rsqrt(rowsum(x * x) + L2_EPS), kc)
    a_col = [rowsum(jnp.where(lane == h, ab, 0.0)) for h in hs]
    b_col = [rowsum(jnp.where(lane == h + N_HEADS, ab, 0.0)) for h in hs]
    beta = _each(jax.nn.sigmoid, b_col)
    g = _each(lambda a, al, dl: rowsum(jnp.where(lane == 0, -jnp.exp(al) * _softplus(a + dl), 0.0)), a_col, a_log_l, dt_l)
    gcum = _each(lambda x: rowsum(jnp.where(incl, row(x), 0.0)), g)
    g_last = _each(lambda x: jnp.sum(jnp.where(last_row, x, 0.0), axis=0, keepdims=True), gcum)
    decay = _each(lambda x: jnp.exp(jnp.where(incl, x - row(x), -jnp.inf)), gcum)
    kk = _each(lambda x: mm(x, x, NT, p_mm), k)
    low = _each(lambda b, x, d: jnp.where(strict, b * x * d, 0.0), beta, kk, decay)
    power = _each(lambda x: -x, low)
    inv = _each(lambda x: jnp.where(eye, 1.0, 0.0) + x, power)
    for _ in range(5):
        power = _each(lambda x: mm(x, x, NN, p_inv), power)
        inv = _each(lambda x, p: x + mm(x, p, NN, p_inv), inv, power)
    exp_g = _each(jnp.exp, gcum)
    u_v = _each(lambda i, b, x: mm(i, b * x, NN, p_mm), inv, beta, vc)
    w = _each(lambda i, b, e, x: mm(i, b * e * x, NN, p_mm), inv, beta, exp_g, k)
    attn = _each(lambda x, y, d: mm(x, y, NT, p_mm) * d, q, k, decay)
    u = _each(lambda x, y, z: x - mm(y, z, NN, p_mm), u_v, w, s)
    o = _each(lambda x, e, z: mm(x * e, z, NN, p_mm), q, exp_g, s)
    o = _each(lambda x, a, y: x + mm(a, y, NN, p_mm), o, attn, u)
    k_end = _each(lambda x, gl, gc: x * jnp.exp(gl - gc), k, g_last, gcum)
    s_new = _each(lambda z, gl, x, y: z * jnp.exp(gl) + mm(x, y, TN, p_mm), s, g_last, k_end, u)
    return _each(lambda x, z: _head_norm_gate(x, norm_w, z), o, zc), s_new


def gdn_chunk(h, qc, kc, vc, zc, ab, a_log_l, dt_l, norm_w, s, prec=GDN_PREC):
    y, s_new = gdn_chunks([h], [qc], [kc], [vc], [zc], ab, [a_log_l], [dt_l], norm_w, [s], prec)
    return y[0], s_new[0]


@functools.partial(jax.custom_vjp, nondiff_argnums=(1,))
def _sroll(x, shift):
    return x if shift == 0 else pltpu.roll(x, shift, 0)


def _sroll_fwd(x, shift):
    return _sroll(x, shift), None


def _sroll_bwd(shift, _, ct):
    return (ct if shift == 0 else pltpu.roll(ct, ct.shape[0] - shift, 0),)


_sroll.defvjp(_sroll_fwd, _sroll_bwd)


def hgrn_chunks(qb, fb, ib, gb, l0, l1, norm_w, st, prec=HGRN_PREC):
    c = CHUNK
    ri, ci = _iota2((3 * c, c), 0), _iota2((3 * c, c), 1)
    rcol = _iota2((c, 1), 0)
    blk0 = jnp.bitwise_and(ri, c - SUB_CHUNK)
    limit = jnp.where(ri < c, ri + 1, jnp.where(ri < 2 * c, blk0, blk0 + SUB_CHUNK))
    sel = jnp.where(ci < limit, 1.0, 0.0)
    ci = _iota2((c, c), 1)
    lb = _each(lambda a, b: jax.nn.sigmoid(a - b), l0, l1)
    forget = _each(lambda b, f: b + (1.0 - b) * jax.nn.sigmoid(f), lb, fb)
    key = _each(lambda b, f: (1.0 - b) * jax.nn.sigmoid(-f), lb, fb)
    q = _each(_silu, qb)
    v = ib
    logf = _each(jnp.log, forget)
    sums = _each(lambda x: sel_mm3(sel, x), logf)
    bc, b_start, b_end = [x[0] for x in sums], [x[1] for x in sums], [x[2] for x in sums]
    b_last = _each(lambda x: jnp.sum(x, axis=0, keepdims=True), logf)
    o = _each(lambda x, b, z: mm(x * jnp.exp(b), z, NT, prec), q, bc, st)
    rmod = jnp.bitwise_and(rcol, SUB_CHUNK - 1)
    for off in range(SUB_CHUNK):
        def diag(acc, x, ky, b, val):
            e = jnp.exp(jnp.where(rmod >= off, b - _sroll(b, off), -jnp.inf))
            a_o = jnp.sum(x * _sroll(ky, off) * e, axis=-1, keepdims=True)
            return acc + a_o * _sroll(val, off)
        o = _each(diag, o, q, key, bc, v)
    q_rel = _each(lambda x, b, bs: x * jnp.exp(b - bs), q, bc, b_start)
    k_rel = _each(lambda x, b, be: x * jnp.exp(be - b), key, bc, b_end)
    for y in range(c // SUB_CHUNK - 1):
        def scaled(x, b, bs):
            end_y = jnp.sum(jnp.where(rcol == SUB_CHUNK * y + SUB_CHUNK - 1, b, 0.0), axis=0, keepdims=True)
            return x * jnp.exp(jnp.where(rcol >= SUB_CHUNK * (y + 1), bs - end_y, -jnp.inf))
        dq = _each(scaled, q_rel, bc, b_start)
        in_y = (ci >= SUB_CHUNK * y) & (ci < SUB_CHUNK * (y + 1))
        a_y = _each(lambda x, z: jnp.where(in_y, mm(x, z, NT, prec), 0.0), dq, k_rel)
        o = _each(lambda acc, a, val: acc + mm(a, val, NN, prec), o, a_y, v)
    k_state = _each(lambda x, bl, b: x * jnp.exp(bl - b), key, b_last, bc)
    st_new = _each(lambda z, bl, val, x: z * jnp.exp(bl) + mm(val, x, TN, prec), st, b_last, v, k_state)
    return _each(lambda x, z: _head_norm_gate(x, norm_w, z), o, gb), st_new


def hgrn_chunk(qb, fb, ib, gb, l0, l1, norm_w, st, prec=HGRN_PREC):
    y, st_new = hgrn_chunks([qb], [fb], [ib], [gb], [l0], [l1], norm_w, [st], prec)
    return y[0], st_new[0]


HEAD_VEC = (N_HEADS, 1, LANES)


class _Groups:
    def __init__(self, nc, hb, rev):
        self.nc, self.hb, self.ng, self.rev = nc, hb, N_HEADS // hb, rev

    def _c(self, c):
        return self.nc - 1 - c if self.rev else c

    def cols(self, slab):
        return pl.BlockSpec((CHUNK, self.hb * LANES), lambda c, g: (self._c(c), slab * self.ng + g))

    def tile(self, block):
        return pl.BlockSpec((CHUNK, LANES), lambda c, g: (self._c(c), block))

    def state(self):
        return pl.BlockSpec((None, self.hb, HEAD_DIM, HEAD_DIM), lambda c, g: (self._c(c), g, 0, 0))

    @staticmethod
    def whole(shape):
        return pl.BlockSpec(shape, lambda c, g: (0,) * len(shape))

    def head(self, g, j):
        return j if self.ng == 1 else g * self.hb + j


def _lanes(j):
    return slice(j * LANES, (j + 1) * LANES)


def gdn_fwd(qkv_c, proj, a_log_l, dt_l, norm_w, name, hb=8):
    t = qkv_c.shape[0]
    gr = _Groups(t // CHUNK, hb, rev=False)

    def body(q_ref, k_ref, v_ref, z_ref, ab_ref, al_ref, dt_ref, nw_ref, y_ref, hist_ref, s_ref):
        c, g = pl.program_id(0), pl.program_id(1)

        @pl.when(c == 0)
        def _():
            for j in range(hb):
                s_ref[gr.head(g, j)] = jnp.zeros((HEAD_DIM, HEAD_DIM), F32)

        hs = [gr.head(g, j) for j in range(hb)]
        heads = lambda ref: [ref[:, _lanes(j)] for j in range(hb)]
        s = [s_ref[h] for h in hs]
        for j in range(hb):
            hist_ref[j] = s[j]
        y, s_new = gdn_chunks(hs, heads(q_ref), heads(k_ref), heads(v_ref), heads(z_ref), ab_ref[...],
                              [al_ref[h] for h in hs], [dt_ref[h] for h in hs], nw_ref[...], s)
        for j in range(hb):
            y_ref[:, _lanes(j)] = y[j].astype(BF16)
            s_ref[hs[j]] = s_new[j]

    return pl.pallas_call(
        body, name=name, grid=(gr.nc, gr.ng),
        in_specs=[gr.cols(0), gr.cols(1), gr.cols(2), gr.cols(3), gr.tile(AB_BLOCK),
                  gr.whole(HEAD_VEC), gr.whole(HEAD_VEC), gr.whole((1, LANES))],
        out_specs=[gr.cols(0), gr.state()],
        out_shape=[jax.ShapeDtypeStruct((t, 2 * GDN_WIDTH), BF16),
                   jax.ShapeDtypeStruct((gr.nc, N_HEADS, HEAD_DIM, HEAD_DIM), F32)],
        scratch_shapes=[pltpu.VMEM((N_HEADS, HEAD_DIM, HEAD_DIM), F32)],
        compiler_params=_params(("arbitrary", "arbitrary")),
    )(qkv_c, qkv_c, qkv_c, proj, proj, a_log_l, dt_l, norm_w)


def gdn_bwd(qkv_c, proj, a_log_l, dt_l, norm_w, hist, dy, name):
    t = qkv_c.shape[0]
    hb = N_HEADS
    gr = _Groups(t // CHUNK, hb, rev=True)

    def body(q_ref, k_ref, v_ref, z_ref, ab_ref, al_ref, dt_ref, nw_ref, hist_ref, dy_ref,
             dqkv_ref, dz_ref, dab_ref, dal_ref, ddt_ref, dnw_ref, ds_ref):
        @pl.when(pl.program_id(0) == 0)
        def _():
            dal_ref[...] = jnp.zeros_like(dal_ref)
            ddt_ref[...] = jnp.zeros_like(ddt_ref)
            dnw_ref[...] = jnp.zeros_like(dnw_ref)
            ds_ref[...] = jnp.zeros_like(ds_ref)

        hs = list(range(hb))
        heads = lambda ref: [ref[:, _lanes(j)] for j in hs]
        _, vjp = jax.vjp(functools.partial(gdn_chunks, hs), heads(q_ref), heads(k_ref), heads(v_ref), heads(z_ref),
                         ab_ref[...], [al_ref[h] for h in hs], [dt_ref[h] for h in hs], nw_ref[...],
                         [hist_ref[h] for h in hs])
        dq, dk, dv, dz, dab, dal, ddt, dnw, ds = vjp((heads(dy_ref), [ds_ref[h] for h in hs]))
        for h in hs:
            dqkv_ref[:, _lanes(h)] = dq[h]
            dqkv_ref[:, _lanes(hb + h)] = dk[h]
            dqkv_ref[:, _lanes(2 * hb + h)] = dv[h]
            dz_ref[:, _lanes(h)] = dz[h].astype(BF16)
            dal_ref[h] += dal[h]
            ddt_ref[h] += ddt[h]
            ds_ref[h] = ds[h]
        dab_ref[...] = dab.astype(BF16)
        dnw_ref[...] += dnw

    return pl.pallas_call(
        body, name=name, grid=(gr.nc, 1),
        in_specs=[gr.cols(0), gr.cols(1), gr.cols(2), gr.cols(3), gr.tile(AB_BLOCK),
                  gr.whole(HEAD_VEC), gr.whole(HEAD_VEC), gr.whole((1, LANES)), gr.state(), gr.cols(0)],
        out_specs=[pl.BlockSpec((CHUNK, QKV_WIDTH), lambda c, g: (gr.nc - 1 - c, 0)), gr.cols(3), gr.tile(0),
                   gr.whole(HEAD_VEC), gr.whole(HEAD_VEC), gr.whole((1, LANES))],
        out_shape=[jax.ShapeDtypeStruct((t, QKV_WIDTH), F32), jax.ShapeDtypeStruct((t, CAT_WIDTH), BF16),
                   jax.ShapeDtypeStruct((t, LANES), BF16), jax.ShapeDtypeStruct(HEAD_VEC, F32),
                   jax.ShapeDtypeStruct(HEAD_VEC, F32), jax.ShapeDtypeStruct((1, LANES), F32)],
        scratch_shapes=[pltpu.VMEM((N_HEADS, HEAD_DIM, HEAD_DIM), F32)],
        compiler_params=_params(("arbitrary", "arbitrary")),
    )(qkv_c, qkv_c, qkv_c, proj, proj, a_log_l, dt_l, norm_w, hist, dy)


def hgrn_fwd(proj, l0, l1, norm_w, y, name, hb=8):
    t = proj.shape[0]
    gr = _Groups(t // CHUNK, hb, rev=False)

    def body(q_ref, f_ref, i_ref, g_ref, l0_ref, l1_ref, nw_ref, y_in, y_ref, hist_ref, s_ref):
        del y_in
        c, g = pl.program_id(0), pl.program_id(1)

        @pl.when(c == 0)
        def _():
            for j in range(hb):
                s_ref[gr.head(g, j)] = jnp.zeros((HEAD_DIM, HEAD_DIM), F32)

        hs = [gr.head(g, j) for j in range(hb)]
        heads = lambda ref: [ref[:, _lanes(j)] for j in range(hb)]
        s = [s_ref[h] for h in hs]
        for j in range(hb):
            hist_ref[j] = s[j]
        out, s_new = hgrn_chunks(heads(q_ref), heads(f_ref), heads(i_ref), heads(g_ref), [l0_ref[h] for h in hs],
                                 [l1_ref[h] for h in hs], nw_ref[...], s)
        for j in range(hb):
            y_ref[:, _lanes(j)] = out[j].astype(BF16)
            s_ref[hs[j]] = s_new[j]

    return pl.pallas_call(
        body, name=name, grid=(gr.nc, gr.ng),
        in_specs=[gr.cols(4), gr.cols(5), gr.cols(6), gr.cols(7), gr.whole(HEAD_VEC), gr.whole(HEAD_VEC),
                  gr.whole((1, LANES)), pl.BlockSpec(memory_space=pl.ANY)],
        out_specs=[gr.cols(1), gr.state()],
        out_shape=[jax.ShapeDtypeStruct((t, 2 * GDN_WIDTH), BF16),
                   jax.ShapeDtypeStruct((gr.nc, N_HEADS, HEAD_DIM, HEAD_DIM), F32)],
        scratch_shapes=[pltpu.VMEM((N_HEADS, HEAD_DIM, HEAD_DIM), F32)],
        input_output_aliases={7: 0},
        compiler_params=_params(("arbitrary", "arbitrary")),
    )(proj, proj, proj, proj, l0, l1, norm_w, y)


def hgrn_bwd(proj, l0, l1, norm_w, hist, dy, dproj, name):
    t = proj.shape[0]
    hb = N_HEADS
    gr = _Groups(t // CHUNK, hb, rev=True)

    def body(q_ref, f_ref, i_ref, g_ref, l0_ref, l1_ref, nw_ref, hist_ref, dy_ref, dproj_in,
             d_ref, dl0_ref, dl1_ref, dnw_ref, ds_ref):
        del dproj_in

        @pl.when(pl.program_id(0) == 0)
        def _():
            dl0_ref[...] = jnp.zeros_like(dl0_ref)
            dl1_ref[...] = jnp.zeros_like(dl1_ref)
            dnw_ref[...] = jnp.zeros_like(dnw_ref)
            ds_ref[...] = jnp.zeros_like(ds_ref)

        hs = list(range(hb))
        heads = lambda ref: [ref[:, _lanes(j)] for j in hs]
        _, vjp = jax.vjp(hgrn_chunks, heads(q_ref), heads(f_ref), heads(i_ref), heads(g_ref), [l0_ref[h] for h in hs],
                         [l1_ref[h] for h in hs], nw_ref[...], [hist_ref[h] for h in hs])
        dq, df, di, dg, dl0, dl1, dnw, ds = vjp((heads(dy_ref), [ds_ref[h] for h in hs]))
        for h in hs:
            for slab, val in enumerate((dq, df, di, dg)):
                d_ref[:, _lanes(slab * hb + h)] = val[h].astype(BF16)
            dl0_ref[h] += dl0[h]
            dl1_ref[h] += dl1[h]
            ds_ref[h] = ds[h]
        dnw_ref[...] += dnw

    return pl.pallas_call(
        body, name=name, grid=(gr.nc, 1),
        in_specs=[gr.cols(4), gr.cols(5), gr.cols(6), gr.cols(7), gr.whole(HEAD_VEC), gr.whole(HEAD_VEC),
                  gr.whole((1, LANES)), gr.state(), gr.cols(1), ANY_SPEC],
        out_specs=[pl.BlockSpec((CHUNK, 4 * GDN_WIDTH), lambda c, g: (gr.nc - 1 - c, 1)),
                   gr.whole(HEAD_VEC), gr.whole(HEAD_VEC), gr.whole((1, LANES))],
        out_shape=[jax.ShapeDtypeStruct(dproj.shape, BF16), jax.ShapeDtypeStruct(HEAD_VEC, F32),
                   jax.ShapeDtypeStruct(HEAD_VEC, F32), jax.ShapeDtypeStruct((1, LANES), F32)],
        scratch_shapes=[pltpu.VMEM((N_HEADS, HEAD_DIM, HEAD_DIM), F32)],
        input_output_aliases={9: 0},
        compiler_params=_params(("arbitrary", "arbitrary")),
    )(proj, proj, proj, proj, l0, l1, norm_w, hist, dy, dproj)


def _adamw(w, g, m, v):
    m = ADAM_B1 * m + (1.0 - ADAM_B1) * g
    v = ADAM_B2 * v + (1.0 - ADAM_B2) * jnp.square(g)
    m_hat = m / (1.0 - ADAM_B1 ** ADAM_STEP)
    v_hat = v / (1.0 - ADAM_B2 ** ADAM_STEP)
    delta = -ADAM_LR * (m_hat / (jnp.sqrt(v_hat) + ADAM_EPS) + ADAM_WD * w)
    return delta, m, v


def adamw_reduce(parts, w, m, v, name, rb=128):
    r, c = w.shape
    rb = min(rb, r)

    def body(p_ref, w_ref, m_ref, v_ref, g_ref, d_ref, mo_ref, vo_ref):
        g = p_ref[0].astype(F32)
        for d in range(1, N_DEV):
            g = g + p_ref[d].astype(F32)
        delta, mn, vn = _adamw(w_ref[...], g, m_ref[...], v_ref[...])
        g_ref[...] = g
        d_ref[...] = delta
        mo_ref[...] = mn
        vo_ref[...] = vn

    blk = pl.BlockSpec((rb, c), lambda i: (i, 0))
    return pl.pallas_call(
        body, name=name, grid=(r // rb,),
        in_specs=[pl.BlockSpec((N_DEV, rb, c), lambda i: (0, i, 0)), blk, blk, blk],
        out_specs=[blk] * 4, out_shape=[jax.ShapeDtypeStruct((r, c), F32)] * 4,
        compiler_params=_params(("parallel",)))(parts, w, m, v)


def adamw_small(w, g, m, v, name):
    def body(w_ref, g_ref, m_ref, v_ref, d_ref, mo_ref, vo_ref):
        delta, mn, vn = _adamw(w_ref[...], g_ref[...], m_ref[...], v_ref[...])
        d_ref[...] = delta
        mo_ref[...] = mn
        vo_ref[...] = vn

    vmem = pl.BlockSpec(memory_space=pltpu.VMEM)
    return pl.pallas_call(body, name=name, in_specs=[vmem] * 4, out_specs=[vmem] * 3,
                          out_shape=[jax.ShapeDtypeStruct(w.shape, F32)] * 3)(w, g, m, v)


def _pack(arrays):
    flat = jnp.concatenate([a.reshape(-1).astype(F32) for a in arrays])
    rows = -(-flat.shape[0] // (8 * LANES)) * 8
    return jnp.pad(flat, (0, rows * LANES - flat.shape[0])).reshape(rows, LANES)


def _unpack(packed, shapes):
    flat, out, off = packed.reshape(-1), [], 0
    for s in shapes:
        n = 1
        for d in s:
            n *= d
        out.append(flat[off:off + n].reshape(s))
        off += n
    return out


def _relu2_epilogue(acc, _):
    r = jnp.maximum(acc, 0.0)
    return acc, r * r


def _relu2_bwd_epilogue(acc, a1):
    return (acc * (2.0 * jnp.maximum(a1, 0.0)),)


def kernel(x, w_in, conv_w, gdn_a_log, gdn_dt_bias, gdn_norm_w, hgrn_lb_logits, hgrn_norm_w, w_out, norm_mix_w, norm_ffn_w, w_ff1, w_ff2, norm_final_w, loss_target, m_w_in, m_conv_w, m_gdn_a_log, m_gdn_dt_bias, m_gdn_norm_w, m_hgrn_lb_logits, m_hgrn_norm_w, m_w_out, m_norm_mix_w, m_norm_ffn_w, m_w_ff1, m_w_ff2, m_norm_final_w, v_w_in, v_conv_w, v_gdn_a_log, v_gdn_dt_bias, v_gdn_norm_w, v_hgrn_lb_logits, v_hgrn_norm_w, v_w_out, v_norm_mix_w, v_norm_ffn_w, v_w_ff1, v_w_ff2, v_norm_final_w):
    me = _my_flat()
    xs = x[0]
    target = loss_target[0]
    shard_in = w_in.shape[2]
    shard_conv = conv_w.shape[2]

    tok = lambda t: t[0:1, 0:1]
    own = lambda src: lax.dynamic_index_in_dim(src, me, 0, keepdims=False)

    g_in, g_conv = gather_two_level([w_in[0].astype(BF16), conv_w[0]], "gather_w_in")
    h_g1, t_g1 = exchange_start([w_out[0].astype(BF16), w_ff1[0].astype(BF16)], True, "gather_mid_start", after=[g_in])
    h_g2, t_g2 = exchange_start([w_ff2[0].astype(BF16)], True, "gather_ff2_start", after=[t_g1])
    w_cat = weights_to_cat(g_in)
    conv_full = jnp.transpose(g_conv, (1, 0, 2)).reshape(4, QKV_WIDTH)

    lane_b = lambda p: jnp.broadcast_to(p.reshape(N_HEADS, 1, 1), HEAD_VEC)
    a_log_l, dt_l = lane_b(gdn_a_log[0]), lane_b(gdn_dt_bias[0])
    l0 = hgrn_lb_logits[0].reshape(HEAD_VEC)
    l1 = hgrn_lb_logits[1].reshape(HEAD_VEC)

    n1, r1 = rms_fwd(xs, norm_mix_w + tok(t_g1) + tok(t_g2), None, "rms_mix")
    proj = matmul(n1, w_cat, "nn", "in_proj", tn=CAT_WIDTH // 5)
    qkv_c = conv_fwd(proj, conv_full, "conv_fwd")
    y_half, hist_a = gdn_fwd(qkv_c, proj, a_log_l, dt_l, gdn_norm_w, "gdn_fwd")
    y, hist_b = hgrn_fwd(proj, l0, l1, hgrn_norm_w, y_half, "hgrn_fwd")
    (s_out, s_ff1), (l_out, l_ff1) = exchange_wait(h_g1, "gather_mid_wait", after=[y])
    w_out_full = _own_slot(l_out, s_out).reshape(D_MODEL, D_MODEL)
    w_ff1_sh = _own_slot(l_ff1, s_ff1)
    mix = matmul(y, w_out_full, "nn", "out_proj")
    h1, n2, r2 = rms_fwd(xs, norm_ffn_w, mix, "rms_ffn")
    a1, act = matmul(n2, w_ff1_sh, "nn", "ff1", out_dtypes=(F32, BF16), epilogue=_relu2_epilogue, b_shards=True)
    (s_ff2,), (l_ff2,) = exchange_wait(h_g2, "gather_ff2_wait", after=[act])
    w_ff2_full = _own_slot(l_ff2, s_ff2).reshape(D_FF, D_MODEL)
    ff = matmul(act, w_ff2_full, "nn", "ff2")
    loss_sum, dh2, dh2_b, d_final = loss_head(h1, ff, norm_final_w.reshape(1, D_MODEL), target, "loss_head")

    da1 = matmul(dh2_b, w_ff2_full, "nt", "d_act", out_dtypes=(BF16,), epilogue=_relu2_bwd_epilogue, extra=a1)
    t_all = xs.shape[0]
    dw_ff2 = matmul(act, dh2_b, "tn", "dw_ff2", out_dtypes=(BF16,), tk=t_all)
    p_ff2 = dw_ff2.reshape(N_DEV, D_FF // N_DEV, D_MODEL)
    h_s1, t_s1 = exchange_start([p_ff2], False, "scatter_ff2_start")
    dn2 = matmul(da1, w_ff1_sh, "nt", "d_n2", after=[t_s1], b_shards=True, k_group=4)
    p_ff1 = matmul(n2, da1, "tn", "dw_ff1", out_dtypes=(BF16,), tn=D_FF // N_DEV, tk=t_all, after=[t_s1], out_shards=True)
    h_s2, t_s2 = exchange_start([p_ff1], False, "scatter_ff1_start")
    dh1, dh1_b, d_ffn = rms_bwd(h1, r2, norm_ffn_w + tok(t_s2), dn2, dh2, "rms_ffn_bwd")
    dmix = matmul(dh1_b, w_out_full, "nt", "d_mix")
    dw_out = matmul(y, dh1_b, "tn", "dw_out", out_dtypes=(BF16,), tk=t_all)
    p_out = dw_out.reshape(N_DEV, D_MODEL // N_DEV, D_MODEL)
    h_s3, t_s3 = exchange_start([p_out], False, "scatter_out_start")
    d_qkv_c, dproj, dab, d_alog_l, d_dt_l, d_gnw = gdn_bwd(
        qkv_c, proj, a_log_l, dt_l, gdn_norm_w + tok(t_s3), hist_a, dmix, "gdn_bwd")
    dproj, dl0, dl1, d_hnw = hgrn_bwd(proj, l0, l1, hgrn_norm_w + tok(t_s3), hist_b, dmix, dproj, "hgrn_bwd")
    dproj, d_conv_full = conv_bwd(proj, d_qkv_c, conv_full, dproj, "conv_bwd")
    dproj = lax.dynamic_update_slice(dproj, dab, (0, MAIN_WIDTH))
    dw_cat = matmul(n1, dproj, "tn", "dw_in", out_dtypes=(BF16,), tm=512, tn=CAT_WIDTH // 5, tk=t_all)
    p_in = cat_to_shards(dw_cat, shard_in)
    h_s4, t_s4 = exchange_start([p_in], False, "scatter_in_start")

    (s_ff2g,), (r_ff2,) = exchange_wait(h_s1, "scatter_ff2_wait", after=[t_s4])
    (s_ff1g,), (r_ff1,) = exchange_wait(h_s2, "scatter_ff1_wait", after=[t_s4])
    (s_outg,), (r_out,) = exchange_wait(h_s3, "scatter_out_wait", after=[t_s4])
    g_w_ff2, d_w_ff2, nm_w_ff2, nv_w_ff2 = adamw_reduce(
        _own_slot(r_ff2, own(s_ff2g)), w_ff2[0], m_w_ff2[0], v_w_ff2[0], "adamw_w_ff2")
    g_w_ff1, d_w_ff1, nm_w_ff1, nv_w_ff1 = adamw_reduce(
        _own_slot(r_ff1, own(s_ff1g)), w_ff1[0], m_w_ff1[0], v_w_ff1[0], "adamw_w_ff1")
    g_w_out, d_w_out, nm_w_out, nv_w_out = adamw_reduce(
        _own_slot(r_out, own(s_outg)), w_out[0], m_w_out[0], v_w_out[0], "adamw_w_out")
    dn1 = matmul(dproj, w_cat, "nt", "d_n1", tk=CAT_WIDTH // 5, after=[t_s4])
    dx, _, d_mix = rms_bwd(xs, r1, norm_mix_w, dn1, dh1, "rms_mix_bwd")
    (s_ing,), (r_in,) = exchange_wait(h_s4, "scatter_in_wait", after=[dx, d_w_ff2, d_w_ff1, d_w_out])
    g_w_in, d_w_in, nm_w_in, nv_w_in = adamw_reduce(
        _own_slot(r_in, own(s_ing)), w_in[0], m_w_in[0], v_w_in[0], "adamw_w_in")

    d_lb = jnp.stack([dl0.reshape(GDN_WIDTH), dl1.reshape(GDN_WIDTH)])
    small_shapes = [(1, N_HEADS), (1, N_HEADS), (1, HEAD_DIM), (2, GDN_WIDTH), (1, HEAD_DIM), (1, D_MODEL),
                    (1, D_MODEL), (D_MODEL,), (4, QKV_WIDTH)]
    small = _pack([d_alog_l[:, 0, 0], d_dt_l[:, 0, 0], d_gnw, d_lb, d_hnw, d_mix, d_ffn, d_final, d_conv_full])
    red = allreduce_small(small, "allreduce_small")
    g_alog, g_dt, g_gnw, g_lb, g_hnw, g_mix, g_ffn, g_final, g_conv_full = _unpack(red, small_shapes)
    g_conv = lax.dynamic_slice(g_conv_full, (0, me * shard_conv), (4, shard_conv)).reshape(1, 4, shard_conv)
    small_g = [g_alog, g_dt, g_gnw, g_lb, g_hnw, g_mix, g_ffn, g_final, g_conv]
    small_w = [gdn_a_log, gdn_dt_bias, gdn_norm_w, hgrn_lb_logits, hgrn_norm_w, norm_mix_w, norm_ffn_w, norm_final_w, conv_w]
    small_m = [m_gdn_a_log, m_gdn_dt_bias, m_gdn_norm_w, m_hgrn_lb_logits, m_hgrn_norm_w, m_norm_mix_w, m_norm_ffn_w,
               m_norm_final_w, m_conv_w]
    small_v = [v_gdn_a_log, v_gdn_dt_bias, v_gdn_norm_w, v_hgrn_lb_logits, v_hgrn_norm_w, v_norm_mix_w, v_norm_ffn_w,
               v_norm_final_w, v_conv_w]
    shapes = [a.shape for a in small_w]
    d_s, m_s, v_s = adamw_small(_pack(small_w), _pack(small_g), _pack(small_m), _pack(small_v), "adamw_small")
    d_alog, d_dt, d_gn, d_lbl, d_hn, d_nm, d_nf, d_nfin, d_cw = _unpack(d_s, shapes)
    m_alog, m_dt, m_gn, m_lbl, m_hn, m_nm, m_nf, m_nfin, m_cw = _unpack(m_s, shapes)
    v_alog, v_dt, v_gn, v_lbl, v_hn, v_nm, v_nf, v_nfin, v_cw = _unpack(v_s, shapes)

    loss = lax.psum(loss_sum[0, 0], ("x", "y", "c"))
    lead = lambda a: a[None]
    grads = [lead(g_w_in), g_conv, g_alog, g_dt, g_gnw, g_lb, g_hnw, lead(g_w_out), g_mix, g_ffn,
             lead(g_w_ff1), lead(g_w_ff2), g_final]
    deltas = [lead(d_w_in), d_cw, d_alog, d_dt, d_gn, d_lbl, d_hn, lead(d_w_out), d_nm, d_nf,
              lead(d_w_ff1), lead(d_w_ff2), d_nfin]
    new_m = [lead(nm_w_in), m_cw, m_alog, m_dt, m_gn, m_lbl, m_hn, lead(nm_w_out), m_nm, m_nf,
             lead(nm_w_ff1), lead(nm_w_ff2), m_nfin]
    new_v = [lead(nv_w_in), v_cw, v_alog, v_dt, v_gn, v_lbl, v_hn, lead(nv_w_out), v_nm, v_nf,
             lead(nv_w_ff1), lead(nv_w_ff2), v_nfin]
    return (loss, dx[None], *grads, *deltas, *new_m, *new_v)
```

```python
import functools

import jax
import jax.numpy as jnp
from jax import lax
from jax.experimental import pallas as pl
from jax.experimental.pallas import tpu as pltpu

F32 = jnp.float32
BF16 = jnp.bfloat16
HI = lax.Precision.HIGHEST

N_DEV = 8
D_MODEL = 2048
CHUNK = 64
SUB_CHUNK = 16
HEAD_DIM = 128
N_HEADS = 8
GDN_WIDTH = N_HEADS * HEAD_DIM
D_FF = 4 * D_MODEL
QKV_WIDTH = 3 * GDN_WIDTH
MAIN_WIDTH = 8 * GDN_WIDTH
CAT_WIDTH = MAIN_WIDTH + 128
AB_BLOCK = MAIN_WIDTH // 128
NORM_EPS = 1e-6
L2_EPS = 1e-6
LANES = 128
VMEM_LIMIT = 56 * 1024 * 1024

ADAM_LR = 0.001
ADAM_B1 = 0.9
ADAM_B2 = 0.999
ADAM_EPS = 1e-08
ADAM_WD = 0.01
ADAM_STEP = 10

MESH = pl.DeviceIdType.MESH


def _params(sem=None):
    return pltpu.CompilerParams(dimension_semantics=sem, vmem_limit_bytes=VMEM_LIMIT)


def _dot(a, b, dims, prec=None):
    return lax.dot_general(a, b, (dims, ((), ())), precision=prec, preferred_element_type=F32)


NN = ((1,), (0,))
NT = ((1,), (1,))
TN = ((0,), (0,))


def _split_bf16(x, pieces):
    out = []
    for _ in range(pieces - 1):
        p = x.astype(BF16)
        out.append(p)
        x = x - p.astype(F32)
    out.append(x.astype(BF16))
    return out


def _mm_raw(a, b, dims, prec):
    if prec == "hi":
        return _dot(a, b, dims, HI)
    if prec == "bf":
        return _dot(a.astype(BF16), b.astype(BF16), dims)
    a_hi, a_lo = _split_bf16(a, 2)
    b_hi, b_lo = _split_bf16(b, 2)
    return _dot(a_hi, b_hi, dims) + (_dot(a_hi, b_lo, dims) + _dot(a_lo, b_hi, dims))


@functools.partial(jax.custom_vjp, nondiff_argnums=(2, 3))
def mm(a, b, dims, prec):
    return _mm_raw(a, b, dims, prec)


def _mm_fwd(a, b, dims, prec):
    return _mm_raw(a, b, dims, prec), (a, b)


def _mm_bwd(dims, prec, res, ct):
    a, b = res
    if dims == NN:
        return _mm_raw(ct, b, NT, prec), _mm_raw(a, ct, TN, prec)
    if dims == NT:
        return _mm_raw(ct, b, NN, prec), _mm_raw(ct, a, TN, prec)
    return _mm_raw(b, ct, NT, prec), _mm_raw(a, ct, NN, prec)


mm.defvjp(_mm_fwd, _mm_bwd)


def _sel_raw(sel, x, dims):
    sel = sel.astype(BF16)
    p0, p1, p2 = _split_bf16(x, 3)
    return _dot(sel, p0, dims) + (_dot(sel, p1, dims) + _dot(sel, p2, dims))


@jax.custom_vjp
def sel_mm3(sel, x):
    c = x.shape[0]
    full = _sel_raw(sel, x, NN)
    return full[:c], full[c:2 * c], full[2 * c:]


def _sel_fwd(sel, x):
    return sel_mm3(sel, x), sel


def _sel_bwd(sel, cts):
    return jnp.zeros_like(sel), _sel_raw(sel, jnp.concatenate(cts, axis=0), TN)


sel_mm3.defvjp(_sel_fwd, _sel_bwd)


def _my_flat():
    return 4 * lax.axis_index("x") + 2 * lax.axis_index("y") + lax.axis_index("c")


def _peer(k):
    x, y, c = lax.axis_index("x"), lax.axis_index("y"), lax.axis_index("c")
    kx, ky, kc = (k >> 2) & 1, (k >> 1) & 1, k & 1
    px = (1 - x) if kx else x
    py = (1 - y) if ky else y
    pc = (1 - c) if kc else c
    return (px, py, pc), 4 * px + 2 * py + pc


def gather_two_level(xs, name):
    n = len(xs)

    def body(*refs):
        x_refs, y_refs = refs[:n], refs[n:2 * n]
        send_sems, recv_sems, local_sems = refs[2 * n:]
        x, y, c = lax.axis_index("x"), lax.axis_index("y"), lax.axis_index("c")
        me, sibling = (x, y, c), (x, y, 1 - c)
        chips = [(1 - x, y), (x, 1 - y), (1 - x, 1 - y)]
        flat = lambda p: 4 * p[0] + 2 * p[1] + p[2]

        def copy(a, k, block, to, src=None):
            return pltpu.make_async_remote_copy(
                src_ref=y_refs[a].at[flat(block)] if src is None else src, dst_ref=y_refs[a].at[flat(block)],
                send_sem=send_sems.at[a, k], recv_sem=recv_sems.at[a, k], device_id=to, device_id_type=MESH)

        mine = [pltpu.make_async_copy(x_refs[a], y_refs[a].at[flat(me)], local_sems.at[a]) for a in range(n)]
        for cp in mine:
            cp.start()
        first = [copy(a, 0, me, sibling, src=x_refs[a]) for a in range(n)]
        first += [copy(a, 1 + j, me, (*chip, c), src=x_refs[a]) for j, chip in enumerate(chips) for a in range(n)]
        for cp in first:
            cp.start()
        passed = []
        for j, chip in enumerate(chips):
            for a in range(n):
                copy(a, 1 + j, (*chip, c), me).wait_recv()
                cp = copy(a, 4 + j, (*chip, c), sibling)
                cp.start()
                passed.append(cp)
        for a in range(n):
            copy(a, 0, sibling, me).wait_recv()
        for j, chip in enumerate(chips):
            for a in range(n):
                copy(a, 4 + j, (*chip, 1 - c), me).wait_recv()
        for cp in first + passed:
            cp.wait_send()
        for cp in mine:
            cp.wait()

    any_spec = pl.BlockSpec(memory_space=pl.ANY)
    return pl.pallas_call(
        body, name=name, out_shape=[jax.ShapeDtypeStruct((N_DEV,) + x.shape, x.dtype) for x in xs],
        in_specs=[any_spec] * n, out_specs=[any_spec] * n,
        scratch_shapes=[pltpu.SemaphoreType.DMA((n, N_DEV - 1)), pltpu.SemaphoreType.DMA((n, N_DEV - 1)),
                        pltpu.SemaphoreType.DMA((n,))],
    )(*xs)


HBM_SPEC = pl.BlockSpec(memory_space=pltpu.HBM)
SEM_SPEC = pl.BlockSpec(memory_space=pltpu.SEMAPHORE)
ANY_SPEC = pl.BlockSpec(memory_space=pl.ANY)
DATAFLOW = pltpu.SideEffectType.DATAFLOW_SIDE_EFFECTING


def _in_hbm(x):
    return pltpu.with_memory_space_constraint(x, pltpu.HBM)


def exchange_start(xs, gather, name, after=()):
    n, n_after = len(xs), len(after)

    def body(*refs):
        x_refs, land_refs = refs[:n], refs[n:2 * n]
        sems = refs[2 * n + n_after:2 * n + n_after + 2 * n]
        token = refs[-1]
        me = _my_flat()
        for k in range(1, N_DEV):
            peer, peer_flat = _peer(k)
            for a in range(n):
                src = x_refs[a] if gather else x_refs[a].at[peer_flat]
                pltpu.make_async_remote_copy(src_ref=src, dst_ref=land_refs[a].at[me], send_sem=sems[a],
                                             recv_sem=sems[n + a], device_id=peer, device_id_type=MESH).start()
        token[...] = jnp.zeros_like(token)

    lands = [_in_hbm(lax.empty(((N_DEV,) + x.shape) if gather else x.shape, x.dtype)) for x in xs]
    hbm_out = [pltpu.HBM(x.shape, x.dtype) for x in xs] + [pltpu.HBM(l.shape, l.dtype) for l in lands]
    res = pl.pallas_call(
        body, name=name,
        out_shape=(*([pltpu.SemaphoreType.DMA(())] * (2 * n)), *hbm_out, jax.ShapeDtypeStruct((8, LANES), F32)),
        in_specs=[HBM_SPEC] * (2 * n) + [ANY_SPEC] * n_after,
        out_specs=(*([SEM_SPEC] * (2 * n)), *([HBM_SPEC] * (2 * n)), pl.BlockSpec(memory_space=pltpu.VMEM)),
        input_output_aliases={i: 2 * n + i for i in range(2 * n)},
        compiler_params=pltpu.CompilerParams(has_side_effects=DATAFLOW),
    )(*[_in_hbm(x) for x in xs], *lands, *after)
    return (list(res[:2 * n]), list(res[2 * n:3 * n]), list(res[3 * n:4 * n])), res[-1]


def exchange_wait(handle, name, after=()):
    sems, xs, lands = handle
    n, n_after = len(xs), len(after)

    def body(*refs):
        land_refs = refs[n:2 * n]
        sem_refs = refs[2 * n:4 * n]
        for a in range(n):
            seven = land_refs[a].at[pl.ds(0, N_DEV - 1)]
            cp = pltpu.make_async_remote_copy(src_ref=seven, dst_ref=seven, send_sem=sem_refs[a],
                                              recv_sem=sem_refs[n + a], device_id=_peer(1)[0], device_id_type=MESH)
            cp.wait_send()
            cp.wait_recv()

    res = pl.pallas_call(
        body, name=name,
        out_shape=[pltpu.HBM(x.shape, x.dtype) for x in xs] + [pltpu.HBM(l.shape, l.dtype) for l in lands],
        in_specs=[HBM_SPEC] * (2 * n) + [SEM_SPEC] * (2 * n) + [ANY_SPEC] * n_after,
        out_specs=[HBM_SPEC] * (2 * n),
        input_output_aliases={i: i for i in range(2 * n)},
        compiler_params=pltpu.CompilerParams(has_side_effects=DATAFLOW),
    )(*xs, *lands, *sems, *after)
    return list(res[:n]), list(res[n:])


def _own_slot(land, block):
    return lax.dynamic_update_slice(land, block[None], (_my_flat(),) + (0,) * block.ndim)


def allreduce_small(x, name):
    rows = x.shape[0]

    def body(x_ref, o_ref, buf, send_sems, recv_sems):
        me = _my_flat()
        buf[me] = x_ref[...]
        sends = []
        for k in range(1, N_DEV):
            peer, _ = _peer(k)
            cp = pltpu.make_async_remote_copy(
                src_ref=x_ref, dst_ref=buf.at[me], send_sem=send_sems.at[k], recv_sem=recv_sems.at[k],
                device_id=peer, device_id_type=MESH)
            cp.start()
            sends.append(cp)
        for k in range(1, N_DEV):
            peer, peer_flat = _peer(k)
            pltpu.make_async_remote_copy(
                src_ref=x_ref, dst_ref=buf.at[peer_flat], send_sem=send_sems.at[k], recv_sem=recv_sems.at[k],
                device_id=peer, device_id_type=MESH).wait_recv()
        for cp in sends:
            cp.wait_send()
        acc = buf[0]
        for d in range(1, N_DEV):
            acc = acc + buf[d]
        o_ref[...] = acc

    vmem = pl.BlockSpec(memory_space=pltpu.VMEM)
    return pl.pallas_call(
        body, name=name, out_shape=jax.ShapeDtypeStruct((rows, LANES), F32),
        in_specs=[vmem], out_specs=vmem,
        scratch_shapes=[pltpu.VMEM((N_DEV, rows, LANES), F32),
                        pltpu.SemaphoreType.DMA((N_DEV,)), pltpu.SemaphoreType.DMA((N_DEV,))],
    )(x)


def matmul(a, b, mode, name, out_dtypes=(F32,), epilogue=None, extra=None, tm=1024, tn=1024, tk=2048, after=(),
           b_shards=False, out_shards=False, k_group=1):
    if b_shards:
        n_sh, b_rows, b_cols = b.shape
    if mode == "nn":
        (m, kd), n = a.shape, (n_sh * b_cols if b_shards else b.shape[1])
        if b_shards:
            tn = b_cols
    elif mode == "nt":
        (m, kd), n = a.shape, (b_rows if b_shards else b.shape[0])
        if b_shards:
            tk = k_group * b_cols
    else:
        (kd, m), n = a.shape, b.shape[1]
    tm, tn, tk = min(tm, m), min(tn, n), min(tk, kd)
    assert m % tm == 0 and n % tn == 0 and kd % tk == 0, (name, m, n, kd, tm, tn, tk)
    ksteps = kd // tk
    dims = {"nn": NN, "nt": NT, "tn": TN}[mode]
    n_out = len(out_dtypes)
    n_in = 2 + (extra is not None) + len(after)

    def finish(acc, e_ref, o_refs):
        outs = (acc,) if epilogue is None else epilogue(acc, e_ref[...] if e_ref is not None else None)
        for o_ref, o in zip(o_refs, outs):
            o_ref[...] = o.astype(o_ref.dtype)

    def product(a_ref, b_ref):
        if mode == "nt" and b_shards:
            w = b_cols
            parts = [_dot(a_ref[:, s * w:(s + 1) * w], b_ref[s], dims) for s in range(k_group)]
            return functools.reduce(lambda p, q: p + q, parts)
        return _dot(a_ref[...], b_ref[...], dims)

    def body(*refs):
        a_ref, b_ref = refs[0], refs[1]
        e_ref = refs[2] if extra is not None else None
        o_refs = refs[n_in:n_in + n_out]
        if ksteps == 1:
            finish(product(a_ref, b_ref), e_ref, o_refs)
            return
        acc_ref = refs[-1]
        kk = pl.program_id(2)

        @pl.when(kk == 0)
        def _():
            acc_ref[...] = jnp.zeros_like(acc_ref)

        acc_ref[...] += product(a_ref, b_ref)

        @pl.when(kk == ksteps - 1)
        def _():
            finish(acc_ref[...], e_ref, o_refs)

    if mode == "nn":
        a_spec = pl.BlockSpec((tm, tk), lambda i, j, k: (i, k))
        b_spec = (pl.BlockSpec((None, tk, tn), lambda i, j, k: (j, k, 0)) if b_shards
                  else pl.BlockSpec((tk, tn), lambda i, j, k: (k, j)))
    elif mode == "nt":
        a_spec = pl.BlockSpec((tm, tk), lambda i, j, k: (i, k))
        b_spec = (pl.BlockSpec((k_group, tn, b_cols), lambda i, j, k: (k, j, 0)) if b_shards
                  else pl.BlockSpec((tn, tk), lambda i, j, k: (j, k)))
    else:
        a_spec = pl.BlockSpec((tk, tm), lambda i, j, k: (k, i))
        b_spec = pl.BlockSpec((tk, tn), lambda i, j, k: (k, j))
    o_spec = pl.BlockSpec((tm, tn), lambda i, j, k: (i, j))
    res_spec = pl.BlockSpec((None, tm, tn), lambda i, j, k: (j, i, 0)) if out_shards else o_spec
    res_shape = (n // tn, m, tn) if out_shards else (m, n)
    in_specs = [a_spec, b_spec] + ([o_spec] if extra is not None else []) + [ANY_SPEC] * len(after)
    args = (a, b) + ((extra,) if extra is not None else ()) + tuple(after)
    res = pl.pallas_call(
        body, name=name, grid=(m // tm, n // tn, ksteps),
        in_specs=in_specs, out_specs=[res_spec] * n_out,
        out_shape=[jax.ShapeDtypeStruct(res_shape, dt) for dt in out_dtypes],
        scratch_shapes=[pltpu.VMEM((tm, tn), F32)] if ksteps > 1 else [],
        compiler_params=_params(("parallel", "parallel", "arbitrary")),
    )(*args)
    return res if n_out > 1 else res[0]


GATE_COL = 4 * GDN_WIDTH
RELAYOUT_ROWS = 256


def _cat_of_win(j):
    if j < GATE_COL:
        return j
    if j < GATE_COL + 2 * N_HEADS:
        return MAIN_WIDTH + (j - GATE_COL)
    return j - 2 * N_HEADS


def _win_of_cat(c):
    if c < GATE_COL:
        return c
    if c < MAIN_WIDTH:
        return c + 2 * N_HEADS
    if c < MAIN_WIDTH + 2 * N_HEADS:
        return GATE_COL + (c - MAIN_WIDTH)
    return None


def _runs(first, count, mapping):
    runs, i = [], 0
    while i < count:
        start, n = mapping(first + i), 1
        while i + n < count and mapping(first + i + n) == start + n:
            n += 1
        runs.append((start, n))
        i += n
    return runs


def weights_to_cat(g_in):
    n_dev, rows, shard = g_in.shape

    def body(x_ref, o_ref):
        for b in range(CAT_WIDTH // LANES):
            live = sum(_win_of_cat(LANES * b + i) is not None for i in range(LANES))
            parts = []
            for start, n in _runs(LANES * b, live, _win_of_cat):
                while n > 0:
                    d, o = divmod(start, shard)
                    take = min(n, shard - o)
                    parts.append(x_ref[d, :, o:o + take])
                    start, n = start + take, n - take
            if live < LANES:
                parts.append(jnp.zeros((RELAYOUT_ROWS, LANES - live), g_in.dtype))
            o_ref[:, LANES * b:LANES * (b + 1)] = parts[0] if len(parts) == 1 else jnp.concatenate(parts, axis=1)

    return pl.pallas_call(
        body, name="weights_to_cat", grid=(rows // RELAYOUT_ROWS,),
        in_specs=[pl.BlockSpec((n_dev, RELAYOUT_ROWS, shard), lambda i: (0, i, 0))],
        out_specs=pl.BlockSpec((RELAYOUT_ROWS, CAT_WIDTH), lambda i: (i, 0)),
        out_shape=jax.ShapeDtypeStruct((rows, CAT_WIDTH), g_in.dtype),
        compiler_params=_params(("parallel",)))(g_in)


def cat_to_shards(dw_cat, shard):
    rows = dw_cat.shape[0]

    def body(x_ref, o_ref):
        for d in range(N_DEV):
            for t0 in range(0, shard, LANES):
                width = min(LANES, shard - t0)
                parts = [x_ref[:, c:c + n] for c, n in _runs(d * shard + t0, width, _cat_of_win)]
                o_ref[d, :, t0:t0 + width] = parts[0] if len(parts) == 1 else jnp.concatenate(parts, axis=1)

    return pl.pallas_call(
        body, name="cat_to_shards", grid=(rows // RELAYOUT_ROWS,),
        in_specs=[pl.BlockSpec((RELAYOUT_ROWS, CAT_WIDTH), lambda i: (i, 0))],
        out_specs=pl.BlockSpec((N_DEV, RELAYOUT_ROWS, shard), lambda i: (0, i, 0)),
        out_shape=jax.ShapeDtypeStruct((N_DEV, rows, shard), dw_cat.dtype),
        compiler_params=_params(("parallel",)))(dw_cat)


ROW_BLOCK = 256


def rms_fwd(x, w, add, name):
    t, d = x.shape
    has_add = add is not None

    def body(*refs):
        x_ref, w_ref = refs[0], refs[1]
        rest = refs[2:]
        if has_add:
            add_ref, h_ref, n_ref, r_ref = rest
            h = x_ref[...] + add_ref[...]
            h_ref[...] = h
        else:
            n_ref, r_ref = rest
            h = x_ref[...]
        r = lax.rsqrt(jnp.mean(h * h, axis=-1, keepdims=True) + NORM_EPS)
        n_ref[...] = (h * r * w_ref[...]).astype(BF16)
        r_ref[...] = r

    row = pl.BlockSpec((ROW_BLOCK, d), lambda i: (i, 0))
    wspec = pl.BlockSpec((1, d), lambda i: (0, 0))
    rspec = pl.BlockSpec((ROW_BLOCK, 1), lambda i: (i, 0))
    in_specs = [row, wspec] + ([row] if has_add else [])
    out_specs = ([row] if has_add else []) + [row, rspec]
    out_shape = ([jax.ShapeDtypeStruct((t, d), F32)] if has_add else []) + [
        jax.ShapeDtypeStruct((t, d), BF16), jax.ShapeDtypeStruct((t, 1), F32)]
    args = (x, w) + ((add,) if has_add else ())
    return pl.pallas_call(body, name=name, grid=(t // ROW_BLOCK,), in_specs=in_specs, out_specs=out_specs,
                          out_shape=out_shape, compiler_params=_params(("parallel",)))(*args)


def loss_head(h1, delta, w, target, name):
    t, d = h1.shape

    def body(h_ref, dl_ref, w_ref, t_ref, loss_ref, dh_ref, dhb_ref, dw_ref):
        @pl.when(pl.program_id(0) == 0)
        def _():
            loss_ref[...] = jnp.zeros_like(loss_ref)
            dw_ref[...] = jnp.zeros_like(dw_ref)

        h = h_ref[...] + dl_ref[...]
        wv = w_ref[...]
        r = lax.rsqrt(jnp.mean(h * h, axis=-1, keepdims=True) + NORM_EPS)
        yn = h * r
        e = yn * wv - t_ref[...]
        loss_ref[...] += 0.5 * jnp.sum(jnp.sum(e * e, axis=-1, keepdims=True), axis=0, keepdims=True) / d
        dy = e / d
        dw_ref[...] += jnp.sum(dy * yn, axis=0, keepdims=True)
        dyn = dy * wv
        dh = r * (dyn - yn * jnp.mean(dyn * yn, axis=-1, keepdims=True))
        dh_ref[...] = dh
        dhb_ref[...] = dh.astype(BF16)

    row = pl.BlockSpec((ROW_BLOCK, d), lambda i: (i, 0))
    wspec = pl.BlockSpec((1, d), lambda i: (0, 0))
    one = pl.BlockSpec((1, 1), lambda i: (0, 0))
    return pl.pallas_call(
        body, name=name, grid=(t // ROW_BLOCK,),
        in_specs=[row, row, wspec, row], out_specs=[one, row, row, wspec],
        out_shape=[jax.ShapeDtypeStruct((1, 1), F32), jax.ShapeDtypeStruct((t, d), F32),
                   jax.ShapeDtypeStruct((t, d), BF16), jax.ShapeDtypeStruct((1, d), F32)],
        compiler_params=_params(("arbitrary",)))(h1, delta, w, target)


def rms_bwd(h, r, w, dn, dres, name):
    t, d = h.shape

    def body(h_ref, r_ref, w_ref, dn_ref, dres_ref, dh_ref, dhb_ref, dw_ref):
        @pl.when(pl.program_id(0) == 0)
        def _():
            dw_ref[...] = jnp.zeros_like(dw_ref)

        rv = r_ref[...]
        yn = h_ref[...] * rv
        dnv = dn_ref[...]
        dw_ref[...] += jnp.sum(dnv * yn, axis=0, keepdims=True)
        dyn = dnv * w_ref[...]
        dh = dres_ref[...] + rv * (dyn - yn * jnp.mean(dyn * yn, axis=-1, keepdims=True))
        dh_ref[...] = dh
        dhb_ref[...] = dh.astype(BF16)

    row = pl.BlockSpec((ROW_BLOCK, d), lambda i: (i, 0))
    wspec = pl.BlockSpec((1, d), lambda i: (0, 0))
    rspec = pl.BlockSpec((ROW_BLOCK, 1), lambda i: (i, 0))
    return pl.pallas_call(
        body, name=name, grid=(t // ROW_BLOCK,),
        in_specs=[row, rspec, wspec, row, row], out_specs=[row, row, wspec],
        out_shape=[jax.ShapeDtypeStruct((t, d), F32), jax.ShapeDtypeStruct((t, d), BF16),
                   jax.ShapeDtypeStruct((1, d), F32)],
        compiler_params=_params(("arbitrary",)))(h, r, w, dn, dres)


CONV_TB = 512
CONV_CB = 512
HALO = 8


def _silu(x):
    return x * jax.nn.sigmoid(x)


def _conv_pre(xcat, w, rows):
    acc = None
    for j in range(4):
        sh = 3 - j
        xs = xcat if sh == 0 else pltpu.roll(xcat, sh, 0)
        term = xs[HALO:HALO + rows] * w[j:j + 1, :]
        acc = term if acc is None else acc + term
    return acc


def conv_fwd(proj, conv_w, name):
    t = proj.shape[0]
    nb = CONV_TB // HALO

    def body(x_ref, prev_ref, w_ref, o_ref):
        prev = jnp.where(pl.program_id(1) == 0, 0.0, prev_ref[...])
        xcat = jnp.concatenate([prev, x_ref[...]], axis=0)
        o_ref[...] = _silu(_conv_pre(xcat, w_ref[...], CONV_TB))

    return pl.pallas_call(
        body, name=name, grid=(QKV_WIDTH // CONV_CB, t // CONV_TB),
        in_specs=[pl.BlockSpec((CONV_TB, CONV_CB), lambda c, i: (i, c)),
                  pl.BlockSpec((HALO, CONV_CB), lambda c, i: (jnp.maximum(i * nb - 1, 0), c)),
                  pl.BlockSpec((4, CONV_CB), lambda c, i: (0, c))],
        out_specs=pl.BlockSpec((CONV_TB, CONV_CB), lambda c, i: (i, c)),
        out_shape=jax.ShapeDtypeStruct((t, QKV_WIDTH), F32),
        compiler_params=_params(("parallel", "parallel")))(proj, proj, conv_w)


def conv_bwd(proj, dout, conv_w, dproj, name):
    t = proj.shape[0]
    nb = CONV_TB // HALO
    nt = t // CONV_TB
    rows = CONV_TB + HALO

    def body(x_ref, prev_ref, next_ref, d_ref, dnext_ref, w_ref, dproj_in, dx_ref, dw_ref):
        del dproj_in
        i = pl.program_id(1)

        @pl.when(i == 0)
        def _():
            dw_ref[...] = jnp.zeros_like(dw_ref)

        w = w_ref[...]
        prev = jnp.where(i == 0, 0.0, prev_ref[...])
        last = i == nt - 1
        xcat = jnp.concatenate([prev, x_ref[...], next_ref[...]], axis=0)
        pre = _conv_pre(xcat, w, rows)
        dcat = jnp.concatenate([d_ref[...], jnp.where(last, 0.0, dnext_ref[...])], axis=0)
        sg = jax.nn.sigmoid(pre)
        dpre = dcat * (sg * (1.0 + pre * (1.0 - sg)))
        dx = None
        for j in range(4):
            sh = 3 - j
            ds = dpre if sh == 0 else pltpu.roll(dpre, rows - sh, 0)
            term = ds[:CONV_TB] * w[j:j + 1, :]
            dx = term if dx is None else dx + term
        dx_ref[...] = dx.astype(BF16)
        dcur = dpre[:CONV_TB]
        parts = []
        for j in range(4):
            sh = 3 - j
            xs = xcat if sh == 0 else pltpu.roll(xcat, sh, 0)
            parts.append(jnp.sum(dcur * xs[HALO:HALO + CONV_TB], axis=0, keepdims=True))
        dw_ref[...] += jnp.concatenate(parts, axis=0)

    cur = pl.BlockSpec((CONV_TB, CONV_CB), lambda c, i: (i, c))
    halo_prev = pl.BlockSpec((HALO, CONV_CB), lambda c, i: (jnp.maximum(i * nb - 1, 0), c))
    halo_next = pl.BlockSpec((HALO, CONV_CB), lambda c, i: (jnp.minimum((i + 1) * nb, nt * nb - 1), c))
    taps = pl.BlockSpec((4, CONV_CB), lambda c, i: (0, c))
    return pl.pallas_call(
        body, name=name, grid=(QKV_WIDTH // CONV_CB, nt),
        in_specs=[cur, halo_prev, halo_next, cur, halo_next, taps, ANY_SPEC],
        out_specs=[cur, taps],
        out_shape=[jax.ShapeDtypeStruct(dproj.shape, BF16), jax.ShapeDtypeStruct((4, QKV_WIDTH), F32)],
        input_output_aliases={6: 0},
        compiler_params=_params(("parallel", "arbitrary")))(proj, proj, proj, dout, dout, conv_w, dproj)


def _iota2(shape, axis):
    return lax.broadcasted_iota(jnp.int32, shape, axis)


def _softplus(x):
    return jnp.maximum(x, 0.0) + jnp.log(1.0 + jnp.exp(-jnp.abs(x)))


def _head_norm_gate(o, norm_w, gate):
    return o * lax.rsqrt(jnp.mean(o * o, axis=-1, keepdims=True) + NORM_EPS) * norm_w * _silu(gate)


GDN_PREC = ("bf", "bf")
HGRN_PREC = "bf"


def _each(fn, *cols):
    return [fn(*a) for a in zip(*cols)]


@functools.partial(jax.custom_vjp, nondiff_argnums=(2,))
def _known_inverse(low, inv, prec):
    del low, prec
    return inv


def _known_inverse_fwd(low, inv, prec):
    del low
    return inv, inv


def _known_inverse_bwd(prec, inv, ct):
    return -_mm_raw(_mm_raw(inv, ct, TN, prec), inv, NT, prec), jnp.zeros_like(inv)


_known_inverse.defvjp(_known_inverse_fwd, _known_inverse_bwd)


def gdn_chunks(hs, qc, kc, vc, zc, ab, a_log_l, dt_l, norm_w, s, prec=GDN_PREC, inv_known=None):
    p_inv, p_mm = prec
    c = CHUNK
    ri, ci = _iota2((c, c), 0), _iota2((c, c), 1)
    incl, strict, eye = ri >= ci, ri > ci, ri == ci
    lane = _iota2((c, LANES), 1)
    last_row = _iota2((c, 1), 0) == c - 1
    rowsum = lambda x: jnp.sum(x, axis=1, keepdims=True)

    def row(col):
        return jnp.sum(jnp.where(eye, col, 0.0), axis=0, keepdims=True)

    q = _each(lambda x: x * lax.rsqrt(rowsum(x * x) + L2_EPS) * (HEAD_DIM ** -0.5), qc)
    k = _each(lambda x: x * lax.rsqrt(rowsum(x * x) + L2_EPS), kc)
    a_col = [rowsum(jnp.where(lane == h, ab, 0.0)) for h in hs]
    b_col = [rowsum(jnp.where(lane == h + N_HEADS, ab, 0.0)) for h in hs]
    beta = _each(jax.nn.sigmoid, b_col)
    g = _each(lambda a, al, dl: rowsum(jnp.where(lane == 0, -jnp.exp(al) * _softplus(a + dl), 0.0)), a_col, a_log_l, dt_l)
    gcum = _each(lambda x: rowsum(jnp.where(incl, row(x), 0.0)), g)
    g_last = _each(lambda x: jnp.sum(jnp.where(last_row, x, 0.0), axis=0, keepdims=True), gcum)
    decay = _each(lambda x: jnp.exp(jnp.where(incl, x - row(x), -jnp.inf)), gcum)
    kk = _each(lambda x: mm(x, x, NT, p_mm), k)
    low = _each(lambda b, x, d: jnp.where(strict, b * x * d, 0.0), beta, kk, decay)
    if inv_known is None:
        power = _each(lambda x: -x, low)
        inv = _each(lambda x: jnp.where(eye, 1.0, 0.0) + x, power)
        for _ in range(5):
            power = _each(lambda x: mm(x, x, NN, p_inv), power)
            inv = _each(lambda x, p: x + mm(x, p, NN, p_inv), inv, power)
    else:
        inv = _each(lambda x, known: _known_inverse(x, known, p_inv), low, inv_known)
    exp_g = _each(jnp.exp, gcum)
    u_v = _each(lambda i, b, x: mm(i, b * x, NN, p_mm), inv, beta, vc)
    w = _each(lambda i, b, e, x: mm(i, b * e * x, NN, p_mm), inv, beta, exp_g, k)
    attn = _each(lambda x, y, d: mm(x, y, NT, p_mm) * d, q, k, decay)
    u = _each(lambda x, y, z: x - mm(y, z, NN, p_mm), u_v, w, s)
    o = _each(lambda x, e, z: mm(x * e, z, NN, p_mm), q, exp_g, s)
    o = _each(lambda x, a, y: x + mm(a, y, NN, p_mm), o, attn, u)
    k_end = _each(lambda x, gl, gc: x * jnp.exp(gl - gc), k, g_last, gcum)
    s_new = _each(lambda z, gl, x, y: z * jnp.exp(gl) + mm(x, y, TN, p_mm), s, g_last, k_end, u)
    return (_each(lambda x, z: _head_norm_gate(x, norm_w, z), o, zc), s_new), inv


def gdn_chunk(h, qc, kc, vc, zc, ab, a_log_l, dt_l, norm_w, s, prec=GDN_PREC, reuse_inverse=False):
    args = ([h], [qc], [kc], [vc], [zc], ab, [a_log_l], [dt_l], norm_w, [s], prec)
    if reuse_inverse:
        inv = lax.stop_gradient(gdn_chunks(*args)[1])
        (y, s_new), _ = gdn_chunks(*args, inv_known=inv)
    else:
        (y, s_new), _ = gdn_chunks(*args)
    return y[0], s_new[0]


@functools.partial(jax.custom_vjp, nondiff_argnums=(1,))
def _sroll(x, shift):
    return x if shift == 0 else pltpu.roll(x, shift, 0)


def _sroll_fwd(x, shift):
    return _sroll(x, shift), None


def _sroll_bwd(shift, _, ct):
    return (ct if shift == 0 else pltpu.roll(ct, ct.shape[0] - shift, 0),)


_sroll.defvjp(_sroll_fwd, _sroll_bwd)


def _rolled_rows(x):
    return [_sroll(x, off) for off in range(SUB_CHUNK)]


SHIFT_PAD = 16
SHIFT_ROWS = SHIFT_PAD + CHUNK + SHIFT_PAD
SHIFT_WAYS = 4


def _moved_rows(x, slots):
    slots[0, SHIFT_PAD:SHIFT_PAD + CHUNK, :] = x
    return [x] + [slots[0, SHIFT_PAD - off:SHIFT_PAD + CHUNK - off, :] for off in range(1, SUB_CHUNK)]


def row_mover(slots, differentiable):
    raw = functools.partial(_moved_rows, slots=slots)
    if not differentiable:
        return raw
    mover = jax.custom_vjp(raw)

    def fwd(x):
        return raw(x), None

    def bwd(_, cts):
        acc = cts[0]
        for off in range(1, SUB_CHUNK):
            way = off % SHIFT_WAYS
            slots[way, SHIFT_PAD:SHIFT_PAD + CHUNK, :] = cts[off]
            acc = acc + slots[way, SHIFT_PAD + off:SHIFT_PAD + CHUNK + off, :]
        return (acc,)

    mover.defvjp(fwd, bwd)
    return mover


def hgrn_chunks(qb, fb, ib, gb, l0, l1, norm_w, st, prec=HGRN_PREC, shifters=None):
    c = CHUNK
    ri, ci = _iota2((3 * c, c), 0), _iota2((3 * c, c), 1)
    rcol = _iota2((c, 1), 0)
    blk0 = jnp.bitwise_and(ri, c - SUB_CHUNK)
    limit = jnp.where(ri < c, ri + 1, jnp.where(ri < 2 * c, blk0, blk0 + SUB_CHUNK))
    sel = jnp.where(ci < limit, 1.0, 0.0)
    ci = _iota2((c, c), 1)
    lb = _each(lambda a, b: jax.nn.sigmoid(a - b), l0, l1)
    forget = _each(lambda b, f: b + (1.0 - b) * jax.nn.sigmoid(f), lb, fb)
    key = _each(lambda b, f: (1.0 - b) * jax.nn.sigmoid(-f), lb, fb)
    q = _each(_silu, qb)
    v = ib
    logf = _each(jnp.log, forget)
    sums = _each(lambda x: sel_mm3(sel, x), logf)
    bc, b_start, b_end = [x[0] for x in sums], [x[1] for x in sums], [x[2] for x in sums]
    b_last = _each(lambda x: jnp.sum(x, axis=0, keepdims=True), logf)
    o = _each(lambda x, b, z: mm(x * jnp.exp(b), z, NT, prec), q, bc, st)
    rmod = jnp.bitwise_and(rcol, SUB_CHUNK - 1)
    if shifters is None:
        shifters = [[_rolled_rows] * 3] * len(qb)
    key_down = _each(lambda f, x: f[0](x), shifters, key)
    bc_down = _each(lambda f, x: f[1](x), shifters, bc)
    v_down = _each(lambda f, x: f[2](x), shifters, v)
    for off in range(SUB_CHUNK):
        def diag(acc, x, b, ky_d, b_d, val_d):
            e = jnp.exp(jnp.where(rmod >= off, b - b_d[off], -jnp.inf))
            a_o = jnp.sum(x * ky_d[off] * e, axis=-1, keepdims=True)
            return acc + a_o * val_d[off]
        o = _each(diag, o, q, bc, key_down, bc_down, v_down)
    q_rel = _each(lambda x, b, bs: x * jnp.exp(b - bs), q, bc, b_start)
    k_rel = _each(lambda x, b, be: x * jnp.exp(be - b), key, bc, b_end)
    for y in range(c // SUB_CHUNK - 1):
        def scaled(x, b, bs):
            end_y = jnp.sum(jnp.where(rcol == SUB_CHUNK * y + SUB_CHUNK - 1, b, 0.0), axis=0, keepdims=True)
            return x * jnp.exp(jnp.where(rcol >= SUB_CHUNK * (y + 1), bs - end_y, -jnp.inf))
        dq = _each(scaled, q_rel, bc, b_start)
        in_y = (ci >= SUB_CHUNK * y) & (ci < SUB_CHUNK * (y + 1))
        a_y = _each(lambda x, z: jnp.where(in_y, mm(x, z, NT, prec), 0.0), dq, k_rel)
        o = _each(lambda acc, a, val: acc + mm(a, val, NN, prec), o, a_y, v)
    k_state = _each(lambda x, bl, b: x * jnp.exp(bl - b), key, b_last, bc)
    st_new = _each(lambda z, bl, val, x: z * jnp.exp(bl) + mm(val, x, TN, prec), st, b_last, v, k_state)
    return _each(lambda x, z: _head_norm_gate(x, norm_w, z), o, gb), st_new


def hgrn_chunk(qb, fb, ib, gb, l0, l1, norm_w, st, prec=HGRN_PREC):
    y, st_new = hgrn_chunks([qb], [fb], [ib], [gb], [l0], [l1], norm_w, [st], prec)
    return y[0], st_new[0]


HEAD_VEC = (N_HEADS, 1, LANES)


class _Groups:
    def __init__(self, nc, hb, rev):
        self.nc, self.hb, self.ng, self.rev = nc, hb, N_HEADS // hb, rev

    def _c(self, c):
        return self.nc - 1 - c if self.rev else c

    def cols(self, slab):
        return pl.BlockSpec((CHUNK, self.hb * LANES), lambda c, g: (self._c(c), slab * self.ng + g))

    def tile(self, block):
        return pl.BlockSpec((CHUNK, LANES), lambda c, g: (self._c(c), block))

    def state(self):
        return pl.BlockSpec((None, self.hb, HEAD_DIM, HEAD_DIM), lambda c, g: (self._c(c), g, 0, 0))

    def inverse(self):
        return pl.BlockSpec((None, self.hb, CHUNK, CHUNK), lambda c, g: (self._c(c), g, 0, 0))

    @staticmethod
    def whole(shape):
        return pl.BlockSpec(shape, lambda c, g: (0,) * len(shape))

    def head(self, g, j):
        return j if self.ng == 1 else g * self.hb + j


def _lanes(j):
    return slice(j * LANES, (j + 1) * LANES)


def gdn_fwd(qkv_c, proj, a_log_l, dt_l, norm_w, name, hb=8):
    t = qkv_c.shape[0]
    gr = _Groups(t // CHUNK, hb, rev=False)

    def body(q_ref, k_ref, v_ref, z_ref, ab_ref, al_ref, dt_ref, nw_ref, y_ref, hist_ref, inv_ref, s_ref):
        c, g = pl.program_id(0), pl.program_id(1)

        @pl.when(c == 0)
        def _():
            for j in range(hb):
                s_ref[gr.head(g, j)] = jnp.zeros((HEAD_DIM, HEAD_DIM), F32)

        hs = [gr.head(g, j) for j in range(hb)]
        heads = lambda ref: [ref[:, _lanes(j)] for j in range(hb)]
        s = [s_ref[h] for h in hs]
        for j in range(hb):
            hist_ref[j] = s[j]
        (y, s_new), inv = gdn_chunks(hs, heads(q_ref), heads(k_ref), heads(v_ref), heads(z_ref), ab_ref[...],
                                     [al_ref[h] for h in hs], [dt_ref[h] for h in hs], nw_ref[...], s)
        for j in range(hb):
            y_ref[:, _lanes(j)] = y[j].astype(BF16)
            s_ref[hs[j]] = s_new[j]
            inv_ref[j] = inv[j]

    return pl.pallas_call(
        body, name=name, grid=(gr.nc, gr.ng),
        in_specs=[gr.cols(0), gr.cols(1), gr.cols(2), gr.cols(3), gr.tile(AB_BLOCK),
                  gr.whole(HEAD_VEC), gr.whole(HEAD_VEC), gr.whole((1, LANES))],
        out_specs=[gr.cols(0), gr.state(), gr.inverse()],
        out_shape=[jax.ShapeDtypeStruct((t, 2 * GDN_WIDTH), BF16),
                   jax.ShapeDtypeStruct((gr.nc, N_HEADS, HEAD_DIM, HEAD_DIM), F32),
                   jax.ShapeDtypeStruct((gr.nc, N_HEADS, CHUNK, CHUNK), F32)],
        scratch_shapes=[pltpu.VMEM((N_HEADS, HEAD_DIM, HEAD_DIM), F32)],
        compiler_params=_params(("arbitrary", "arbitrary")),
    )(qkv_c, qkv_c, qkv_c, proj, proj, a_log_l, dt_l, norm_w)


def gdn_bwd(qkv_c, proj, a_log_l, dt_l, norm_w, hist, inv_hist, dy, name):
    t = qkv_c.shape[0]
    hb = N_HEADS
    gr = _Groups(t // CHUNK, hb, rev=True)

    def body(q_ref, k_ref, v_ref, z_ref, ab_ref, al_ref, dt_ref, nw_ref, hist_ref, inv_ref, dy_ref,
             dqkv_ref, dz_ref, dab_ref, dal_ref, ddt_ref, dnw_ref, ds_ref):
        @pl.when(pl.program_id(0) == 0)
        def _():
            dal_ref[...] = jnp.zeros_like(dal_ref)
            ddt_ref[...] = jnp.zeros_like(ddt_ref)
            dnw_ref[...] = jnp.zeros_like(dnw_ref)
            ds_ref[...] = jnp.zeros_like(ds_ref)

        hs = list(range(hb))
        heads = lambda ref: [ref[:, _lanes(j)] for j in hs]
        chunk = functools.partial(gdn_chunks, hs, inv_known=[inv_ref[h] for h in hs])
        _, vjp, _ = jax.vjp(chunk, heads(q_ref), heads(k_ref), heads(v_ref), heads(z_ref),
                            ab_ref[...], [al_ref[h] for h in hs], [dt_ref[h] for h in hs], nw_ref[...],
                            [hist_ref[h] for h in hs], has_aux=True)
        dq, dk, dv, dz, dab, dal, ddt, dnw, ds = vjp((heads(dy_ref), [ds_ref[h] for h in hs]))
        for h in hs:
            dqkv_ref[:, _lanes(h)] = dq[h]
            dqkv_ref[:, _lanes(hb + h)] = dk[h]
            dqkv_ref[:, _lanes(2 * hb + h)] = dv[h]
            dz_ref[:, _lanes(h)] = dz[h].astype(BF16)
            dal_ref[h] += dal[h]
            ddt_ref[h] += ddt[h]
            ds_ref[h] = ds[h]
        dab_ref[...] = dab.astype(BF16)
        dnw_ref[...] += dnw

    return pl.pallas_call(
        body, name=name, grid=(gr.nc, 1),
        in_specs=[gr.cols(0), gr.cols(1), gr.cols(2), gr.cols(3), gr.tile(AB_BLOCK),
                  gr.whole(HEAD_VEC), gr.whole(HEAD_VEC), gr.whole((1, LANES)), gr.state(), gr.inverse(), gr.cols(0)],
        out_specs=[pl.BlockSpec((CHUNK, QKV_WIDTH), lambda c, g: (gr.nc - 1 - c, 0)), gr.cols(3), gr.tile(0),
                   gr.whole(HEAD_VEC), gr.whole(HEAD_VEC), gr.whole((1, LANES))],
        out_shape=[jax.ShapeDtypeStruct((t, QKV_WIDTH), F32), jax.ShapeDtypeStruct((t, CAT_WIDTH), BF16),
                   jax.ShapeDtypeStruct((t, LANES), BF16), jax.ShapeDtypeStruct(HEAD_VEC, F32),
                   jax.ShapeDtypeStruct(HEAD_VEC, F32), jax.ShapeDtypeStruct((1, LANES), F32)],
        scratch_shapes=[pltpu.VMEM((N_HEADS, HEAD_DIM, HEAD_DIM), F32)],
        compiler_params=_params(("arbitrary", "arbitrary")),
    )(qkv_c, qkv_c, qkv_c, proj, proj, a_log_l, dt_l, norm_w, hist, inv_hist, dy)


def hgrn_fwd(proj, l0, l1, norm_w, y, name, hb=8):
    t = proj.shape[0]
    gr = _Groups(t // CHUNK, hb, rev=False)

    def body(q_ref, f_ref, i_ref, g_ref, l0_ref, l1_ref, nw_ref, y_in, y_ref, hist_ref, s_ref, shift_ref):
        del y_in
        c, g = pl.program_id(0), pl.program_id(1)

        @pl.when((c == 0) & (g == 0))
        def _():
            shift_ref[...] = jnp.zeros_like(shift_ref)

        @pl.when(c == 0)
        def _():
            for j in range(hb):
                s_ref[gr.head(g, j)] = jnp.zeros((HEAD_DIM, HEAD_DIM), F32)

        hs = [gr.head(g, j) for j in range(hb)]
        heads = lambda ref: [ref[:, _lanes(j)] for j in range(hb)]
        s = [s_ref[h] for h in hs]
        for j in range(hb):
            hist_ref[j] = s[j]
        movers = [[row_mover(shift_ref.at[3 * j + a], differentiable=False) for a in range(3)] for j in range(hb)]
        out, s_new = hgrn_chunks(heads(q_ref), heads(f_ref), heads(i_ref), heads(g_ref), [l0_ref[h] for h in hs],
                                 [l1_ref[h] for h in hs], nw_ref[...], s, shifters=movers)
        for j in range(hb):
            y_ref[:, _lanes(j)] = out[j].astype(BF16)
            s_ref[hs[j]] = s_new[j]

    return pl.pallas_call(
        body, name=name, grid=(gr.nc, gr.ng),
        in_specs=[gr.cols(4), gr.cols(5), gr.cols(6), gr.cols(7), gr.whole(HEAD_VEC), gr.whole(HEAD_VEC),
                  gr.whole((1, LANES)), pl.BlockSpec(memory_space=pl.ANY)],
        out_specs=[gr.cols(1), gr.state()],
        out_shape=[jax.ShapeDtypeStruct((t, 2 * GDN_WIDTH), BF16),
                   jax.ShapeDtypeStruct((gr.nc, N_HEADS, HEAD_DIM, HEAD_DIM), F32)],
        scratch_shapes=[pltpu.VMEM((N_HEADS, HEAD_DIM, HEAD_DIM), F32),
                        pltpu.VMEM((3 * hb, SHIFT_WAYS, SHIFT_ROWS, LANES), F32)],
        input_output_aliases={7: 0},
        compiler_params=_params(("arbitrary", "arbitrary")),
    )(proj, proj, proj, proj, l0, l1, norm_w, y)


def hgrn_bwd(proj, l0, l1, norm_w, hist, dy, dproj, name):
    t = proj.shape[0]
    hb = N_HEADS
    gr = _Groups(t // CHUNK, hb, rev=True)

    def body(q_ref, f_ref, i_ref, g_ref, l0_ref, l1_ref, nw_ref, hist_ref, dy_ref, dproj_in,
             d_ref, dl0_ref, dl1_ref, dnw_ref, ds_ref, shift_ref):
        del dproj_in

        @pl.when(pl.program_id(0) == 0)
        def _():
            dl0_ref[...] = jnp.zeros_like(dl0_ref)
            dl1_ref[...] = jnp.zeros_like(dl1_ref)
            dnw_ref[...] = jnp.zeros_like(dnw_ref)
            ds_ref[...] = jnp.zeros_like(ds_ref)
            shift_ref[...] = jnp.zeros_like(shift_ref)

        hs = list(range(hb))
        heads = lambda ref: [ref[:, _lanes(j)] for j in hs]
        movers = [[row_mover(shift_ref.at[3 * h + a], differentiable=True) for a in range(3)] for h in hs]
        chunk = functools.partial(hgrn_chunks, shifters=movers)
        _, vjp = jax.vjp(chunk, heads(q_ref), heads(f_ref), heads(i_ref), heads(g_ref), [l0_ref[h] for h in hs],
                         [l1_ref[h] for h in hs], nw_ref[...], [hist_ref[h] for h in hs])
        dq, df, di, dg, dl0, dl1, dnw, ds = vjp((heads(dy_ref), [ds_ref[h] for h in hs]))
        for h in hs:
            for slab, val in enumerate((dq, df, di, dg)):
                d_ref[:, _lanes(slab * hb + h)] = val[h].astype(BF16)
            dl0_ref[h] += dl0[h]
            dl1_ref[h] += dl1[h]
            ds_ref[h] = ds[h]
        dnw_ref[...] += dnw

    return pl.pallas_call(
        body, name=name, grid=(gr.nc, 1),
        in_specs=[gr.cols(4), gr.cols(5), gr.cols(6), gr.cols(7), gr.whole(HEAD_VEC), gr.whole(HEAD_VEC),
                  gr.whole((1, LANES)), gr.state(), gr.cols(1), ANY_SPEC],
        out_specs=[pl.BlockSpec((CHUNK, 4 * GDN_WIDTH), lambda c, g: (gr.nc - 1 - c, 1)),
                   gr.whole(HEAD_VEC), gr.whole(HEAD_VEC), gr.whole((1, LANES))],
        out_shape=[jax.ShapeDtypeStruct(dproj.shape, BF16), jax.ShapeDtypeStruct(HEAD_VEC, F32),
                   jax.ShapeDtypeStruct(HEAD_VEC, F32), jax.ShapeDtypeStruct((1, LANES), F32)],
        scratch_shapes=[pltpu.VMEM((N_HEADS, HEAD_DIM, HEAD_DIM), F32),
                        pltpu.VMEM((3 * hb, SHIFT_WAYS, SHIFT_ROWS, LANES), F32)],
        input_output_aliases={9: 0},
        compiler_params=_params(("arbitrary", "arbitrary")),
    )(proj, proj, proj, proj, l0, l1, norm_w, hist, dy, dproj)


def _adamw(w, g, m, v):
    m = ADAM_B1 * m + (1.0 - ADAM_B1) * g
    v = ADAM_B2 * v + (1.0 - ADAM_B2) * jnp.square(g)
    m_hat = m / (1.0 - ADAM_B1 ** ADAM_STEP)
    v_hat = v / (1.0 - ADAM_B2 ** ADAM_STEP)
    delta = -ADAM_LR * (m_hat / (jnp.sqrt(v_hat) + ADAM_EPS) + ADAM_WD * w)
    return delta, m, v


def adamw_reduce(parts, w, m, v, name, rb=128):
    r, c = w.shape
    rb = min(rb, r)

    def body(p_ref, w_ref, m_ref, v_ref, g_ref, d_ref, mo_ref, vo_ref):
        g = p_ref[0].astype(F32)
        for d in range(1, N_DEV):
            g = g + p_ref[d].astype(F32)
        delta, mn, vn = _adamw(w_ref[...], g, m_ref[...], v_ref[...])
        g_ref[...] = g
        d_ref[...] = delta
        mo_ref[...] = mn
        vo_ref[...] = vn

    blk = pl.BlockSpec((rb, c), lambda i: (i, 0))
    return pl.pallas_call(
        body, name=name, grid=(r // rb,),
        in_specs=[pl.BlockSpec((N_DEV, rb, c), lambda i: (0, i, 0)), blk, blk, blk],
        out_specs=[blk] * 4, out_shape=[jax.ShapeDtypeStruct((r, c), F32)] * 4,
        compiler_params=_params(("parallel",)))(parts, w, m, v)


def adamw_small(w, g, m, v, name):
    def body(w_ref, g_ref, m_ref, v_ref, d_ref, mo_ref, vo_ref):
        delta, mn, vn = _adamw(w_ref[...], g_ref[...], m_ref[...], v_ref[...])
        d_ref[...] = delta
        mo_ref[...] = mn
        vo_ref[...] = vn

    vmem = pl.BlockSpec(memory_space=pltpu.VMEM)
    return pl.pallas_call(body, name=name, in_specs=[vmem] * 4, out_specs=[vmem] * 3,
                          out_shape=[jax.ShapeDtypeStruct(w.shape, F32)] * 3)(w, g, m, v)


def _pack(arrays):
    flat = jnp.concatenate([a.reshape(-1).astype(F32) for a in arrays])
    rows = -(-flat.shape[0] // (8 * LANES)) * 8
    return jnp.pad(flat, (0, rows * LANES - flat.shape[0])).reshape(rows, LANES)


def _unpack(packed, shapes):
    flat, out, off = packed.reshape(-1), [], 0
    for s in shapes:
        n = 1
        for d in s:
            n *= d
        out.append(flat[off:off + n].reshape(s))
        off += n
    return out


def _relu2_epilogue(acc, _):
    r = jnp.maximum(acc, 0.0)
    return acc, r * r


def _relu2_bwd_epilogue(acc, a1):
    return (acc * (2.0 * jnp.maximum(a1, 0.0)),)


def kernel(x, w_in, conv_w, gdn_a_log, gdn_dt_bias, gdn_norm_w, hgrn_lb_logits, hgrn_norm_w, w_out, norm_mix_w, norm_ffn_w, w_ff1, w_ff2, norm_final_w, loss_target, m_w_in, m_conv_w, m_gdn_a_log, m_gdn_dt_bias, m_gdn_norm_w, m_hgrn_lb_logits, m_hgrn_norm_w, m_w_out, m_norm_mix_w, m_norm_ffn_w, m_w_ff1, m_w_ff2, m_norm_final_w, v_w_in, v_conv_w, v_gdn_a_log, v_gdn_dt_bias, v_gdn_norm_w, v_hgrn_lb_logits, v_hgrn_norm_w, v_w_out, v_norm_mix_w, v_norm_ffn_w, v_w_ff1, v_w_ff2, v_norm_final_w):
    me = _my_flat()
    xs = x[0]
    target = loss_target[0]
    shard_in = w_in.shape[2]
    shard_conv = conv_w.shape[2]

    tok = lambda t: t[0:1, 0:1]
    own = lambda src: lax.dynamic_index_in_dim(src, me, 0, keepdims=False)

    g_in, g_conv = gather_two_level([w_in[0].astype(BF16), conv_w[0]], "gather_w_in")
    h_g1, t_g1 = exchange_start([w_out[0].astype(BF16), w_ff1[0].astype(BF16)], True, "gather_mid_start", after=[g_in])
    h_g2, t_g2 = exchange_start([w_ff2[0].astype(BF16)], True, "gather_ff2_start", after=[t_g1])
    w_cat = weights_to_cat(g_in)
    conv_full = jnp.transpose(g_conv, (1, 0, 2)).reshape(4, QKV_WIDTH)

    lane_b = lambda p: jnp.broadcast_to(p.reshape(N_HEADS, 1, 1), HEAD_VEC)
    a_log_l, dt_l = lane_b(gdn_a_log[0]), lane_b(gdn_dt_bias[0])
    l0 = hgrn_lb_logits[0].reshape(HEAD_VEC)
    l1 = hgrn_lb_logits[1].reshape(HEAD_VEC)

    n1, r1 = rms_fwd(xs, norm_mix_w + tok(t_g1) + tok(t_g2), None, "rms_mix")
    proj = matmul(n1, w_cat, "nn", "in_proj", tn=CAT_WIDTH // 5)
    qkv_c = conv_fwd(proj, conv_full, "conv_fwd")
    y_half, hist_a, inv_a = gdn_fwd(qkv_c, proj, a_log_l, dt_l, gdn_norm_w, "gdn_fwd")
    y, hist_b = hgrn_fwd(proj, l0, l1, hgrn_norm_w, y_half, "hgrn_fwd")
    (s_out, s_ff1), (l_out, l_ff1) = exchange_wait(h_g1, "gather_mid_wait", after=[y])
    w_out_full = _own_slot(l_out, s_out).reshape(D_MODEL, D_MODEL)
    w_ff1_sh = _own_slot(l_ff1, s_ff1)
    mix = matmul(y, w_out_full, "nn", "out_proj")
    h1, n2, r2 = rms_fwd(xs, norm_ffn_w, mix, "rms_ffn")
    a1, act = matmul(n2, w_ff1_sh, "nn", "ff1", out_dtypes=(F32, BF16), epilogue=_relu2_epilogue, b_shards=True)
    (s_ff2,), (l_ff2,) = exchange_wait(h_g2, "gather_ff2_wait", after=[act])
    w_ff2_full = _own_slot(l_ff2, s_ff2).reshape(D_FF, D_MODEL)
    ff = matmul(act, w_ff2_full, "nn", "ff2")
    loss_sum, dh2, dh2_b, d_final = loss_head(h1, ff, norm_final_w.reshape(1, D_MODEL), target, "loss_head")

    da1 = matmul(dh2_b, w_ff2_full, "nt", "d_act", out_dtypes=(BF16,), epilogue=_relu2_bwd_epilogue, extra=a1)
    t_all = xs.shape[0]
    dw_ff2 = matmul(act, dh2_b, "tn", "dw_ff2", out_dtypes=(BF16,), tk=t_all)
    p_ff2 = dw_ff2.reshape(N_DEV, D_FF // N_DEV, D_MODEL)
    h_s1, t_s1 = exchange_start([p_ff2], False, "scatter_ff2_start")
    dn2 = matmul(da1, w_ff1_sh, "nt", "d_n2", after=[t_s1], b_shards=True, k_group=4)
    p_ff1 = matmul(n2, da1, "tn", "dw_ff1", out_dtypes=(BF16,), tn=D_FF // N_DEV, tk=t_all, after=[t_s1], out_shards=True)
    h_s2, t_s2 = exchange_start([p_ff1], False, "scatter_ff1_start")
    dh1, dh1_b, d_ffn = rms_bwd(h1, r2, norm_ffn_w + tok(t_s2), dn2, dh2, "rms_ffn_bwd")
    dmix = matmul(dh1_b, w_out_full, "nt", "d_mix")
    dw_out = matmul(y, dh1_b, "tn", "dw_out", out_dtypes=(BF16,), tk=t_all)
    p_out = dw_out.reshape(N_DEV, D_MODEL // N_DEV, D_MODEL)
    h_s3, t_s3 = exchange_start([p_out], False, "scatter_out_start")
    d_qkv_c, dproj, dab, d_alog_l, d_dt_l, d_gnw = gdn_bwd(
        qkv_c, proj, a_log_l, dt_l, gdn_norm_w + tok(t_s3), hist_a, inv_a, dmix, "gdn_bwd")
    dproj, dl0, dl1, d_hnw = hgrn_bwd(proj, l0, l1, hgrn_norm_w + tok(t_s3), hist_b, dmix, dproj, "hgrn_bwd")
    dproj, d_conv_full = conv_bwd(proj, d_qkv_c, conv_full, dproj, "conv_bwd")
    dproj = lax.dynamic_update_slice(dproj, dab, (0, MAIN_WIDTH))
    dw_cat = matmul(n1, dproj, "tn", "dw_in", out_dtypes=(BF16,), tm=512, tn=CAT_WIDTH // 5, tk=t_all)
    p_in = cat_to_shards(dw_cat, shard_in)
    h_s4, t_s4 = exchange_start([p_in], False, "scatter_in_start")

    (s_ff2g,), (r_ff2,) = exchange_wait(h_s1, "scatter_ff2_wait", after=[t_s4])
    (s_ff1g,), (r_ff1,) = exchange_wait(h_s2, "scatter_ff1_wait", after=[t_s4])
    (s_outg,), (r_out,) = exchange_wait(h_s3, "scatter_out_wait", after=[t_s4])
    g_w_ff2, d_w_ff2, nm_w_ff2, nv_w_ff2 = adamw_reduce(
        _own_slot(r_ff2, own(s_ff2g)), w_ff2[0], m_w_ff2[0], v_w_ff2[0], "adamw_w_ff2")
    g_w_ff1, d_w_ff1, nm_w_ff1, nv_w_ff1 = adamw_reduce(
        _own_slot(r_ff1, own(s_ff1g)), w_ff1[0], m_w_ff1[0], v_w_ff1[0], "adamw_w_ff1")
    g_w_out, d_w_out, nm_w_out, nv_w_out = adamw_reduce(
        _own_slot(r_out, own(s_outg)), w_out[0], m_w_out[0], v_w_out[0], "adamw_w_out")
    dn1 = matmul(dproj, w_cat, "nt", "d_n1", tk=CAT_WIDTH // 5, after=[t_s4])
    dx, _, d_mix = rms_bwd(xs, r1, norm_mix_w, dn1, dh1, "rms_mix_bwd")
    (s_ing,), (r_in,) = exchange_wait(h_s4, "scatter_in_wait", after=[dx, d_w_ff2, d_w_ff1, d_w_out])
    g_w_in, d_w_in, nm_w_in, nv_w_in = adamw_reduce(
        _own_slot(r_in, own(s_ing)), w_in[0], m_w_in[0], v_w_in[0], "adamw_w_in")

    d_lb = jnp.stack([dl0.reshape(GDN_WIDTH), dl1.reshape(GDN_WIDTH)])
    small_shapes = [(1, N_HEADS), (1, N_HEADS), (1, HEAD_DIM), (2, GDN_WIDTH), (1, HEAD_DIM), (1, D_MODEL),
                    (1, D_MODEL), (D_MODEL,), (4, QKV_WIDTH)]
    small = _pack([d_alog_l[:, 0, 0], d_dt_l[:, 0, 0], d_gnw, d_lb, d_hnw, d_mix, d_ffn, d_final, d_conv_full])
    red = allreduce_small(small, "allreduce_small")
    g_alog, g_dt, g_gnw, g_lb, g_hnw, g_mix, g_ffn, g_final, g_conv_full = _unpack(red, small_shapes)
    g_conv = lax.dynamic_slice(g_conv_full, (0, me * shard_conv), (4, shard_conv)).reshape(1, 4, shard_conv)
    small_g = [g_alog, g_dt, g_gnw, g_lb, g_hnw, g_mix, g_ffn, g_final, g_conv]
    small_w = [gdn_a_log, gdn_dt_bias, gdn_norm_w, hgrn_lb_logits, hgrn_norm_w, norm_mix_w, norm_ffn_w, norm_final_w, conv_w]
    small_m = [m_gdn_a_log, m_gdn_dt_bias, m_gdn_norm_w, m_hgrn_lb_logits, m_hgrn_norm_w, m_norm_mix_w, m_norm_ffn_w,
               m_norm_final_w, m_conv_w]
    small_v = [v_gdn_a_log, v_gdn_dt_bias, v_gdn_norm_w, v_hgrn_lb_logits, v_hgrn_norm_w, v_norm_mix_w, v_norm_ffn_w,
               v_norm_final_w, v_conv_w]
    shapes = [a.shape for a in small_w]
    d_s, m_s, v_s = adamw_small(_pack(small_w), _pack(small_g), _pack(small_m), _pack(small_v), "adamw_small")
    d_alog, d_dt, d_gn, d_lbl, d_hn, d_nm, d_nf, d_nfin, d_cw = _unpack(d_s, shapes)
    m_alog, m_dt, m_gn, m_lbl, m_hn, m_nm, m_nf, m_nfin, m_cw = _unpack(m_s, shapes)
    v_alog, v_dt, v_gn, v_lbl, v_hn, v_nm, v_nf, v_nfin, v_cw = _unpack(v_s, shapes)

    loss = lax.psum(loss_sum[0, 0], ("x", "y", "c"))
    lead = lambda a: a[None]
    grads = [lead(g_w_in), g_conv, g_alog, g_dt, g_gnw, g_lb, g_hnw, lead(g_w_out), g_mix, g_ffn,
             lead(g_w_ff1), lead(g_w_ff2), g_final]
    deltas = [lead(d_w_in), d_cw, d_alog, d_dt, d_gn, d_lbl, d_hn, lead(d_w_out), d_nm, d_nf,
              lead(d_w_ff1), lead(d_w_ff2), d_nfin]
    new_m = [lead(nm_w_in), m_cw, m_alog, m_dt, m_gn, m_lbl, m_hn, lead(nm_w_out), m_nm, m_nf,
             lead(nm_w_ff1), lead(nm_w_ff2), m_nfin]
    new_v = [lead(nv_w_in), v_cw, v_alog, v_dt, v_gn, v_lbl, v_hn, lead(nv_w_out), v_nm, v_nf,
             lead(nv_w_ff1), lead(nv_w_ff2), v_nfin]
    return (loss, dx[None], *grads, *deltas, *new_m, *new_v)
```

```python
import functools

import jax
import jax.numpy as jnp
from jax import lax
from jax.experimental import pallas as pl
from jax.experimental.pallas import tpu as pltpu

F32 = jnp.float32
BF16 = jnp.bfloat16
HI = lax.Precision.HIGHEST

N_DEV = 8
D_MODEL = 2048
CHUNK = 64
SUB_CHUNK = 16
HEAD_DIM = 128
N_HEADS = 8
GDN_WIDTH = N_HEADS * HEAD_DIM
D_FF = 4 * D_MODEL
QKV_WIDTH = 3 * GDN_WIDTH
MAIN_WIDTH = 8 * GDN_WIDTH
CAT_WIDTH = MAIN_WIDTH + 128
AB_BLOCK = MAIN_WIDTH // 128
NORM_EPS = 1e-6
L2_EPS = 1e-6
LANES = 128
VMEM_LIMIT = 56 * 1024 * 1024

ADAM_LR = 0.001
ADAM_B1 = 0.9
ADAM_B2 = 0.999
ADAM_EPS = 1e-08
ADAM_WD = 0.01
ADAM_STEP = 10

MESH = pl.DeviceIdType.MESH


def _params(sem=None):
    return pltpu.CompilerParams(dimension_semantics=sem, vmem_limit_bytes=VMEM_LIMIT)


def _dot(a, b, dims, prec=None):
    return lax.dot_general(a, b, (dims, ((), ())), precision=prec, preferred_element_type=F32)


NN = ((1,), (0,))
NT = ((1,), (1,))
TN = ((0,), (0,))


def _split_bf16(x, pieces):
    out = []
    for _ in range(pieces - 1):
        p = x.astype(BF16)
        out.append(p)
        x = x - p.astype(F32)
    out.append(x.astype(BF16))
    return out


def _mm_raw(a, b, dims, prec):
    if prec == "hi":
        return _dot(a, b, dims, HI)
    if prec == "bf":
        return _dot(a.astype(BF16), b.astype(BF16), dims)
    a_hi, a_lo = _split_bf16(a, 2)
    b_hi, b_lo = _split_bf16(b, 2)
    return _dot(a_hi, b_hi, dims) + (_dot(a_hi, b_lo, dims) + _dot(a_lo, b_hi, dims))


@functools.partial(jax.custom_vjp, nondiff_argnums=(2, 3))
def mm(a, b, dims, prec):
    return _mm_raw(a, b, dims, prec)


def _mm_fwd(a, b, dims, prec):
    return _mm_raw(a, b, dims, prec), (a, b)


def _mm_bwd(dims, prec, res, ct):
    a, b = res
    if dims == NN:
        return _mm_raw(ct, b, NT, prec), _mm_raw(a, ct, TN, prec)
    if dims == NT:
        return _mm_raw(ct, b, NN, prec), _mm_raw(ct, a, TN, prec)
    return _mm_raw(b, ct, NT, prec), _mm_raw(a, ct, NN, prec)


mm.defvjp(_mm_fwd, _mm_bwd)


def _sel_raw(sel, x, dims):
    sel = sel.astype(BF16)
    p0, p1, p2 = _split_bf16(x, 3)
    return _dot(sel, p0, dims) + (_dot(sel, p1, dims) + _dot(sel, p2, dims))


def _sel_parts(sel, x):
    c = x.shape[0]
    full = _sel_raw(sel, x, NN)
    return tuple(full[i * c:(i + 1) * c] for i in range(sel.shape[0] // c))


@jax.custom_vjp
def sel_sums(sel, x):
    return _sel_parts(sel, x)


def _sel_fwd(sel, x):
    return _sel_parts(sel, x), sel


def _sel_bwd(sel, cts):
    return jnp.zeros_like(sel), _sel_raw(sel, jnp.concatenate(cts, axis=0), TN)


sel_sums.defvjp(_sel_fwd, _sel_bwd)


@jax.custom_vjp
def _known_value(computed, known):
    del computed
    return known


_known_value.defvjp(lambda computed, known: (known, None), lambda _, ct: (ct, jnp.zeros_like(ct)))


def _my_flat():
    return 4 * lax.axis_index("x") + 2 * lax.axis_index("y") + lax.axis_index("c")


def _peer(k):
    x, y, c = lax.axis_index("x"), lax.axis_index("y"), lax.axis_index("c")
    kx, ky, kc = (k >> 2) & 1, (k >> 1) & 1, k & 1
    px = (1 - x) if kx else x
    py = (1 - y) if ky else y
    pc = (1 - c) if kc else c
    return (px, py, pc), 4 * px + 2 * py + pc


def gather_two_level(xs, name):
    n = len(xs)

    def body(*refs):
        x_refs, y_refs = refs[:n], refs[n:2 * n]
        send_sems, recv_sems, local_sems = refs[2 * n:]
        x, y, c = lax.axis_index("x"), lax.axis_index("y"), lax.axis_index("c")
        me, sibling = (x, y, c), (x, y, 1 - c)
        chips = [(1 - x, y), (x, 1 - y), (1 - x, 1 - y)]
        flat = lambda p: 4 * p[0] + 2 * p[1] + p[2]

        def copy(a, k, block, to, src=None):
            return pltpu.make_async_remote_copy(
                src_ref=y_refs[a].at[flat(block)] if src is None else src, dst_ref=y_refs[a].at[flat(block)],
                send_sem=send_sems.at[a, k], recv_sem=recv_sems.at[a, k], device_id=to, device_id_type=MESH)

        mine = [pltpu.make_async_copy(x_refs[a], y_refs[a].at[flat(me)], local_sems.at[a]) for a in range(n)]
        for cp in mine:
            cp.start()
        first = [copy(a, 0, me, sibling, src=x_refs[a]) for a in range(n)]
        first += [copy(a, 1 + j, me, (*chip, c), src=x_refs[a]) for j, chip in enumerate(chips) for a in range(n)]
        for cp in first:
            cp.start()
        passed = []
        for j, chip in enumerate(chips):
            for a in range(n):
                copy(a, 1 + j, (*chip, c), me).wait_recv()
                cp = copy(a, 4 + j, (*chip, c), sibling)
                cp.start()
                passed.append(cp)
        for a in range(n):
            copy(a, 0, sibling, me).wait_recv()
        for j, chip in enumerate(chips):
            for a in range(n):
                copy(a, 4 + j, (*chip, 1 - c), me).wait_recv()
        for cp in first + passed:
            cp.wait_send()
        for cp in mine:
            cp.wait()

    any_spec = pl.BlockSpec(memory_space=pl.ANY)
    return pl.pallas_call(
        body, name=name, out_shape=[jax.ShapeDtypeStruct((N_DEV,) + x.shape, x.dtype) for x in xs],
        in_specs=[any_spec] * n, out_specs=[any_spec] * n,
        scratch_shapes=[pltpu.SemaphoreType.DMA((n, N_DEV - 1)), pltpu.SemaphoreType.DMA((n, N_DEV - 1)),
                        pltpu.SemaphoreType.DMA((n,))],
    )(*xs)


HBM_SPEC = pl.BlockSpec(memory_space=pltpu.HBM)
SEM_SPEC = pl.BlockSpec(memory_space=pltpu.SEMAPHORE)
ANY_SPEC = pl.BlockSpec(memory_space=pl.ANY)
DATAFLOW = pltpu.SideEffectType.DATAFLOW_SIDE_EFFECTING


def _in_hbm(x):
    return pltpu.with_memory_space_constraint(x, pltpu.HBM)


def exchange_start(xs, gather, name, after=()):
    n, n_after = len(xs), len(after)

    def body(*refs):
        x_refs, land_refs = refs[:n], refs[n:2 * n]
        sems = refs[2 * n + n_after:2 * n + n_after + 2 * n]
        token = refs[-1]
        me = _my_flat()
        for k in range(1, N_DEV):
            peer, peer_flat = _peer(k)
            for a in range(n):
                src = x_refs[a] if gather else x_refs[a].at[peer_flat]
                pltpu.make_async_remote_copy(src_ref=src, dst_ref=land_refs[a].at[me], send_sem=sems[a],
                                             recv_sem=sems[n + a], device_id=peer, device_id_type=MESH).start()
        token[...] = jnp.zeros_like(token)

    lands = [_in_hbm(lax.empty(((N_DEV,) + x.shape) if gather else x.shape, x.dtype)) for x in xs]
    hbm_out = [pltpu.HBM(x.shape, x.dtype) for x in xs] + [pltpu.HBM(l.shape, l.dtype) for l in lands]
    res = pl.pallas_call(
        body, name=name,
        out_shape=(*([pltpu.SemaphoreType.DMA(())] * (2 * n)), *hbm_out, jax.ShapeDtypeStruct((8, LANES), F32)),
        in_specs=[HBM_SPEC] * (2 * n) + [ANY_SPEC] * n_after,
        out_specs=(*([SEM_SPEC] * (2 * n)), *([HBM_SPEC] * (2 * n)), pl.BlockSpec(memory_space=pltpu.VMEM)),
        input_output_aliases={i: 2 * n + i for i in range(2 * n)},
        compiler_params=pltpu.CompilerParams(has_side_effects=DATAFLOW),
    )(*[_in_hbm(x) for x in xs], *lands, *after)
    return (list(res[:2 * n]), list(res[2 * n:3 * n]), list(res[3 * n:4 * n])), res[-1]


def exchange_wait(handle, name, after=()):
    sems, xs, lands = handle
    n, n_after = len(xs), len(after)

    def body(*refs):
        land_refs = refs[n:2 * n]
        sem_refs = refs[2 * n:4 * n]
        for a in range(n):
            seven = land_refs[a].at[pl.ds(0, N_DEV - 1)]
            cp = pltpu.make_async_remote_copy(src_ref=seven, dst_ref=seven, send_sem=sem_refs[a],
                                              recv_sem=sem_refs[n + a], device_id=_peer(1)[0], device_id_type=MESH)
            cp.wait_send()
            cp.wait_recv()

    res = pl.pallas_call(
        body, name=name,
        out_shape=[pltpu.HBM(x.shape, x.dtype) for x in xs] + [pltpu.HBM(l.shape, l.dtype) for l in lands],
        in_specs=[HBM_SPEC] * (2 * n) + [SEM_SPEC] * (2 * n) + [ANY_SPEC] * n_after,
        out_specs=[HBM_SPEC] * (2 * n),
        input_output_aliases={i: i for i in range(2 * n)},
        compiler_params=pltpu.CompilerParams(has_side_effects=DATAFLOW),
    )(*xs, *lands, *sems, *after)
    return list(res[:n]), list(res[n:])


def _own_slot(land, block):
    return lax.dynamic_update_slice(land, block[None], (_my_flat(),) + (0,) * block.ndim)


def allreduce_small(x, name):
    rows = x.shape[0]

    def body(x_ref, o_ref, buf, send_sems, recv_sems):
        me = _my_flat()
        buf[me] = x_ref[...]
        sends = []
        for k in range(1, N_DEV):
            peer, _ = _peer(k)
            cp = pltpu.make_async_remote_copy(
                src_ref=x_ref, dst_ref=buf.at[me], send_sem=send_sems.at[k], recv_sem=recv_sems.at[k],
                device_id=peer, device_id_type=MESH)
            cp.start()
            sends.append(cp)
        for k in range(1, N_DEV):
            peer, peer_flat = _peer(k)
            pltpu.make_async_remote_copy(
                src_ref=x_ref, dst_ref=buf.at[peer_flat], send_sem=send_sems.at[k], recv_sem=recv_sems.at[k],
                device_id=peer, device_id_type=MESH).wait_recv()
        for cp in sends:
            cp.wait_send()
        acc = buf[0]
        for d in range(1, N_DEV):
            acc = acc + buf[d]
        o_ref[...] = acc

    vmem = pl.BlockSpec(memory_space=pltpu.VMEM)
    return pl.pallas_call(
        body, name=name, out_shape=jax.ShapeDtypeStruct((rows, LANES), F32),
        in_specs=[vmem], out_specs=vmem,
        scratch_shapes=[pltpu.VMEM((N_DEV, rows, LANES), F32),
                        pltpu.SemaphoreType.DMA((N_DEV,)), pltpu.SemaphoreType.DMA((N_DEV,))],
    )(x)


def matmul(a, b, mode, name, out_dtypes=(F32,), epilogue=None, extra=None, tm=1024, tn=1024, tk=2048, after=(),
           b_shards=False, out_shards=False, k_group=1):
    if b_shards:
        n_sh, b_rows, b_cols = b.shape
    if mode == "nn":
        (m, kd), n = a.shape, (n_sh * b_cols if b_shards else b.shape[1])
        if b_shards:
            tn = b_cols
    elif mode == "nt":
        (m, kd), n = a.shape, (b_rows if b_shards else b.shape[0])
        if b_shards:
            tk = k_group * b_cols
    else:
        (kd, m), n = a.shape, b.shape[1]
    tm, tn, tk = min(tm, m), min(tn, n), min(tk, kd)
    assert m % tm == 0 and n % tn == 0 and kd % tk == 0, (name, m, n, kd, tm, tn, tk)
    ksteps = kd // tk
    dims = {"nn": NN, "nt": NT, "tn": TN}[mode]
    n_out = len(out_dtypes)
    n_in = 2 + (extra is not None) + len(after)

    def finish(acc, e_ref, o_refs):
        outs = (acc,) if epilogue is None else epilogue(acc, e_ref[...] if e_ref is not None else None)
        for o_ref, o in zip(o_refs, outs):
            o_ref[...] = o.astype(o_ref.dtype)

    def product(a_ref, b_ref):
        if mode == "nt" and b_shards:
            w = b_cols
            parts = [_dot(a_ref[:, s * w:(s + 1) * w], b_ref[s], dims) for s in range(k_group)]
            return functools.reduce(lambda p, q: p + q, parts)
        return _dot(a_ref[...], b_ref[...], dims)

    def body(*refs):
        a_ref, b_ref = refs[0], refs[1]
        e_ref = refs[2] if extra is not None else None
        o_refs = refs[n_in:n_in + n_out]
        if ksteps == 1:
            finish(product(a_ref, b_ref), e_ref, o_refs)
            return
        acc_ref = refs[-1]
        kk = pl.program_id(2)

        @pl.when(kk == 0)
        def _():
            acc_ref[...] = jnp.zeros_like(acc_ref)

        acc_ref[...] += product(a_ref, b_ref)

        @pl.when(kk == ksteps - 1)
        def _():
            finish(acc_ref[...], e_ref, o_refs)

    if mode == "nn":
        a_spec = pl.BlockSpec((tm, tk), lambda i, j, k: (i, k))
        b_spec = (pl.BlockSpec((None, tk, tn), lambda i, j, k: (j, k, 0)) if b_shards
                  else pl.BlockSpec((tk, tn), lambda i, j, k: (k, j)))
    elif mode == "nt":
        a_spec = pl.BlockSpec((tm, tk), lambda i, j, k: (i, k))
        b_spec = (pl.BlockSpec((k_group, tn, b_cols), lambda i, j, k: (k, j, 0)) if b_shards
                  else pl.BlockSpec((tn, tk), lambda i, j, k: (j, k)))
    else:
        a_spec = pl.BlockSpec((tk, tm), lambda i, j, k: (k, i))
        b_spec = pl.BlockSpec((tk, tn), lambda i, j, k: (k, j))
    o_spec = pl.BlockSpec((tm, tn), lambda i, j, k: (i, j))
    res_spec = pl.BlockSpec((None, tm, tn), lambda i, j, k: (j, i, 0)) if out_shards else o_spec
    res_shape = (n // tn, m, tn) if out_shards else (m, n)
    in_specs = [a_spec, b_spec] + ([o_spec] if extra is not None else []) + [ANY_SPEC] * len(after)
    args = (a, b) + ((extra,) if extra is not None else ()) + tuple(after)
    res = pl.pallas_call(
        body, name=name, grid=(m // tm, n // tn, ksteps),
        in_specs=in_specs, out_specs=[res_spec] * n_out,
        out_shape=[jax.ShapeDtypeStruct(res_shape, dt) for dt in out_dtypes],
        scratch_shapes=[pltpu.VMEM((tm, tn), F32)] if ksteps > 1 else [],
        compiler_params=_params(("parallel", "parallel", "arbitrary")),
    )(*args)
    return res if n_out > 1 else res[0]


GATE_COL = 4 * GDN_WIDTH
RELAYOUT_ROWS = 256


def _cat_of_win(j):
    if j < GATE_COL:
        return j
    if j < GATE_COL + 2 * N_HEADS:
        return MAIN_WIDTH + (j - GATE_COL)
    return j - 2 * N_HEADS


def _win_of_cat(c):
    if c < GATE_COL:
        return c
    if c < MAIN_WIDTH:
        return c + 2 * N_HEADS
    if c < MAIN_WIDTH + 2 * N_HEADS:
        return GATE_COL + (c - MAIN_WIDTH)
    return None


def _runs(first, count, mapping):
    runs, i = [], 0
    while i < count:
        start, n = mapping(first + i), 1
        while i + n < count and mapping(first + i + n) == start + n:
            n += 1
        runs.append((start, n))
        i += n
    return runs


def weights_to_cat(g_in):
    n_dev, rows, shard = g_in.shape

    def body(x_ref, o_ref):
        for b in range(CAT_WIDTH // LANES):
            live = sum(_win_of_cat(LANES * b + i) is not None for i in range(LANES))
            parts = []
            for start, n in _runs(LANES * b, live, _win_of_cat):
                while n > 0:
                    d, o = divmod(start, shard)
                    take = min(n, shard - o)
                    parts.append(x_ref[d, :, o:o + take])
                    start, n = start + take, n - take
            if live < LANES:
                parts.append(jnp.zeros((RELAYOUT_ROWS, LANES - live), g_in.dtype))
            o_ref[:, LANES * b:LANES * (b + 1)] = parts[0] if len(parts) == 1 else jnp.concatenate(parts, axis=1)

    return pl.pallas_call(
        body, name="weights_to_cat", grid=(rows // RELAYOUT_ROWS,),
        in_specs=[pl.BlockSpec((n_dev, RELAYOUT_ROWS, shard), lambda i: (0, i, 0))],
        out_specs=pl.BlockSpec((RELAYOUT_ROWS, CAT_WIDTH), lambda i: (i, 0)),
        out_shape=jax.ShapeDtypeStruct((rows, CAT_WIDTH), g_in.dtype),
        compiler_params=_params(("parallel",)))(g_in)


def cat_to_shards(dw_cat, shard):
    rows = dw_cat.shape[0]

    def body(x_ref, o_ref):
        for d in range(N_DEV):
            for t0 in range(0, shard, LANES):
                width = min(LANES, shard - t0)
                parts = [x_ref[:, c:c + n] for c, n in _runs(d * shard + t0, width, _cat_of_win)]
                o_ref[d, :, t0:t0 + width] = parts[0] if len(parts) == 1 else jnp.concatenate(parts, axis=1)

    return pl.pallas_call(
        body, name="cat_to_shards", grid=(rows // RELAYOUT_ROWS,),
        in_specs=[pl.BlockSpec((RELAYOUT_ROWS, CAT_WIDTH), lambda i: (i, 0))],
        out_specs=pl.BlockSpec((N_DEV, RELAYOUT_ROWS, shard), lambda i: (0, i, 0)),
        out_shape=jax.ShapeDtypeStruct((N_DEV, rows, shard), dw_cat.dtype),
        compiler_params=_params(("parallel",)))(dw_cat)


ROW_BLOCK = 256


def rms_fwd(x, w, add, name):
    t, d = x.shape
    has_add = add is not None

    def body(*refs):
        x_ref, w_ref = refs[0], refs[1]
        rest = refs[2:]
        if has_add:
            add_ref, h_ref, n_ref, r_ref = rest
            h = x_ref[...] + add_ref[...]
            h_ref[...] = h
        else:
            n_ref, r_ref = rest
            h = x_ref[...]
        r = lax.rsqrt(jnp.mean(h * h, axis=-1, keepdims=True) + NORM_EPS)
        n_ref[...] = (h * r * w_ref[...]).astype(BF16)
        r_ref[...] = r

    row = pl.BlockSpec((ROW_BLOCK, d), lambda i: (i, 0))
    wspec = pl.BlockSpec((1, d), lambda i: (0, 0))
    rspec = pl.BlockSpec((ROW_BLOCK, 1), lambda i: (i, 0))
    in_specs = [row, wspec] + ([row] if has_add else [])
    out_specs = ([row] if has_add else []) + [row, rspec]
    out_shape = ([jax.ShapeDtypeStruct((t, d), F32)] if has_add else []) + [
        jax.ShapeDtypeStruct((t, d), BF16), jax.ShapeDtypeStruct((t, 1), F32)]
    args = (x, w) + ((add,) if has_add else ())
    return pl.pallas_call(body, name=name, grid=(t // ROW_BLOCK,), in_specs=in_specs, out_specs=out_specs,
                          out_shape=out_shape, compiler_params=_params(("parallel",)))(*args)


def loss_head(h1, delta, w, target, name):
    t, d = h1.shape

    def body(h_ref, dl_ref, w_ref, t_ref, loss_ref, dh_ref, dhb_ref, dw_ref):
        @pl.when(pl.program_id(0) == 0)
        def _():
            loss_ref[...] = jnp.zeros_like(loss_ref)
            dw_ref[...] = jnp.zeros_like(dw_ref)

        h = h_ref[...] + dl_ref[...]
        wv = w_ref[...]
        r = lax.rsqrt(jnp.mean(h * h, axis=-1, keepdims=True) + NORM_EPS)
        yn = h * r
        e = yn * wv - t_ref[...]
        loss_ref[...] += 0.5 * jnp.sum(jnp.sum(e * e, axis=-1, keepdims=True), axis=0, keepdims=True) / d
        dy = e / d
        dw_ref[...] += jnp.sum(dy * yn, axis=0, keepdims=True)
        dyn = dy * wv
        dh = r * (dyn - yn * jnp.mean(dyn * yn, axis=-1, keepdims=True))
        dh_ref[...] = dh
        dhb_ref[...] = dh.astype(BF16)

    row = pl.BlockSpec((ROW_BLOCK, d), lambda i: (i, 0))
    wspec = pl.BlockSpec((1, d), lambda i: (0, 0))
    one = pl.BlockSpec((1, 1), lambda i: (0, 0))
    return pl.pallas_call(
        body, name=name, grid=(t // ROW_BLOCK,),
        in_specs=[row, row, wspec, row], out_specs=[one, row, row, wspec],
        out_shape=[jax.ShapeDtypeStruct((1, 1), F32), jax.ShapeDtypeStruct((t, d), F32),
                   jax.ShapeDtypeStruct((t, d), BF16), jax.ShapeDtypeStruct((1, d), F32)],
        compiler_params=_params(("arbitrary",)))(h1, delta, w, target)


def rms_bwd(h, r, w, dn, dres, name):
    t, d = h.shape

    def body(h_ref, r_ref, w_ref, dn_ref, dres_ref, dh_ref, dhb_ref, dw_ref):
        @pl.when(pl.program_id(0) == 0)
        def _():
            dw_ref[...] = jnp.zeros_like(dw_ref)

        rv = r_ref[...]
        yn = h_ref[...] * rv
        dnv = dn_ref[...]
        dw_ref[...] += jnp.sum(dnv * yn, axis=0, keepdims=True)
        dyn = dnv * w_ref[...]
        dh = dres_ref[...] + rv * (dyn - yn * jnp.mean(dyn * yn, axis=-1, keepdims=True))
        dh_ref[...] = dh
        dhb_ref[...] = dh.astype(BF16)

    row = pl.BlockSpec((ROW_BLOCK, d), lambda i: (i, 0))
    wspec = pl.BlockSpec((1, d), lambda i: (0, 0))
    rspec = pl.BlockSpec((ROW_BLOCK, 1), lambda i: (i, 0))
    return pl.pallas_call(
        body, name=name, grid=(t // ROW_BLOCK,),
        in_specs=[row, rspec, wspec, row, row], out_specs=[row, row, wspec],
        out_shape=[jax.ShapeDtypeStruct((t, d), F32), jax.ShapeDtypeStruct((t, d), BF16),
                   jax.ShapeDtypeStruct((1, d), F32)],
        compiler_params=_params(("arbitrary",)))(h, r, w, dn, dres)


CONV_TB = 512
CONV_CB = 512
HALO = 8


def _silu(x):
    return x * jax.nn.sigmoid(x)


def _conv_pre(xcat, w, rows):
    acc = None
    for j in range(4):
        sh = 3 - j
        xs = xcat if sh == 0 else pltpu.roll(xcat, sh, 0)
        term = xs[HALO:HALO + rows] * w[j:j + 1, :]
        acc = term if acc is None else acc + term
    return acc


def conv_fwd(proj, conv_w, name):
    t = proj.shape[0]
    nb = CONV_TB // HALO

    def body(x_ref, prev_ref, w_ref, o_ref):
        prev = jnp.where(pl.program_id(1) == 0, 0.0, prev_ref[...])
        xcat = jnp.concatenate([prev, x_ref[...]], axis=0)
        o_ref[...] = _silu(_conv_pre(xcat, w_ref[...], CONV_TB))

    return pl.pallas_call(
        body, name=name, grid=(QKV_WIDTH // CONV_CB, t // CONV_TB),
        in_specs=[pl.BlockSpec((CONV_TB, CONV_CB), lambda c, i: (i, c)),
                  pl.BlockSpec((HALO, CONV_CB), lambda c, i: (jnp.maximum(i * nb - 1, 0), c)),
                  pl.BlockSpec((4, CONV_CB), lambda c, i: (0, c))],
        out_specs=pl.BlockSpec((CONV_TB, CONV_CB), lambda c, i: (i, c)),
        out_shape=jax.ShapeDtypeStruct((t, QKV_WIDTH), F32),
        compiler_params=_params(("parallel", "parallel")))(proj, proj, conv_w)


def conv_bwd(proj, dout, conv_w, dproj, name):
    t = proj.shape[0]
    nb = CONV_TB // HALO
    nt = t // CONV_TB
    rows = CONV_TB + HALO

    def body(x_ref, prev_ref, next_ref, d_ref, dnext_ref, w_ref, dproj_in, dx_ref, dw_ref):
        del dproj_in
        i = pl.program_id(1)

        @pl.when(i == 0)
        def _():
            dw_ref[...] = jnp.zeros_like(dw_ref)

        w = w_ref[...]
        prev = jnp.where(i == 0, 0.0, prev_ref[...])
        last = i == nt - 1
        xcat = jnp.concatenate([prev, x_ref[...], next_ref[...]], axis=0)
        pre = _conv_pre(xcat, w, rows)
        dcat = jnp.concatenate([d_ref[...], jnp.where(last, 0.0, dnext_ref[...])], axis=0)
        sg = jax.nn.sigmoid(pre)
        dpre = dcat * (sg * (1.0 + pre * (1.0 - sg)))
        dx = None
        for j in range(4):
            sh = 3 - j
            ds = dpre if sh == 0 else pltpu.roll(dpre, rows - sh, 0)
            term = ds[:CONV_TB] * w[j:j + 1, :]
            dx = term if dx is None else dx + term
        dx_ref[...] = dx.astype(BF16)
        dcur = dpre[:CONV_TB]
        parts = []
        for j in range(4):
            sh = 3 - j
            xs = xcat if sh == 0 else pltpu.roll(xcat, sh, 0)
            parts.append(jnp.sum(dcur * xs[HALO:HALO + CONV_TB], axis=0, keepdims=True))
        dw_ref[...] += jnp.concatenate(parts, axis=0)

    cur = pl.BlockSpec((CONV_TB, CONV_CB), lambda c, i: (i, c))
    halo_prev = pl.BlockSpec((HALO, CONV_CB), lambda c, i: (jnp.maximum(i * nb - 1, 0), c))
    halo_next = pl.BlockSpec((HALO, CONV_CB), lambda c, i: (jnp.minimum((i + 1) * nb, nt * nb - 1), c))
    taps = pl.BlockSpec((4, CONV_CB), lambda c, i: (0, c))
    return pl.pallas_call(
        body, name=name, grid=(QKV_WIDTH // CONV_CB, nt),
        in_specs=[cur, halo_prev, halo_next, cur, halo_next, taps, ANY_SPEC],
        out_specs=[cur, taps],
        out_shape=[jax.ShapeDtypeStruct(dproj.shape, BF16), jax.ShapeDtypeStruct((4, QKV_WIDTH), F32)],
        input_output_aliases={6: 0},
        compiler_params=_params(("parallel", "arbitrary")))(proj, proj, proj, dout, dout, conv_w, dproj)


def _iota2(shape, axis):
    return lax.broadcasted_iota(jnp.int32, shape, axis)


def _softplus(x):
    return jnp.maximum(x, 0.0) + jnp.log(1.0 + jnp.exp(-jnp.abs(x)))


def _head_norm_gate(o, norm_w, gate):
    return o * lax.rsqrt(jnp.mean(o * o, axis=-1, keepdims=True) + NORM_EPS) * norm_w * _silu(gate)


GDN_PREC = ("bf", "bf")
HGRN_PREC = "bf"


def _each(fn, *cols):
    return [fn(*a) for a in zip(*cols)]


@functools.partial(jax.custom_vjp, nondiff_argnums=(2,))
def _known_inverse(low, inv, prec):
    del low, prec
    return inv


def _known_inverse_fwd(low, inv, prec):
    del low
    return inv, inv


def _known_inverse_bwd(prec, inv, ct):
    return -_mm_raw(_mm_raw(inv, ct, TN, prec), inv, NT, prec), jnp.zeros_like(inv)


_known_inverse.defvjp(_known_inverse_fwd, _known_inverse_bwd)


def gdn_chunks(hs, qc, kc, vc, zc, ab, a_log_l, dt_l, norm_w, s, prec=GDN_PREC, inv_known=None):
    p_inv, p_mm = prec
    c = CHUNK
    ri, ci = _iota2((c, c), 0), _iota2((c, c), 1)
    incl, strict, eye = ri >= ci, ri > ci, ri == ci
    lane = _iota2((c, LANES), 1)
    last_row = _iota2((c, 1), 0) == c - 1
    rowsum = lambda x: jnp.sum(x, axis=1, keepdims=True)

    def row(col):
        return jnp.sum(jnp.where(eye, col, 0.0), axis=0, keepdims=True)

    q = _each(lambda x: x * lax.rsqrt(rowsum(x * x) + L2_EPS) * (HEAD_DIM ** -0.5), qc)
    k = _each(lambda x: x * lax.rsqrt(rowsum(x * x) + L2_EPS), kc)
    a_col = [rowsum(jnp.where(lane == h, ab, 0.0)) for h in hs]
    b_col = [rowsum(jnp.where(lane == h + N_HEADS, ab, 0.0)) for h in hs]
    beta = _each(jax.nn.sigmoid, b_col)
    g = _each(lambda a, al, dl: rowsum(jnp.where(lane == 0, -jnp.exp(al) * _softplus(a + dl), 0.0)), a_col, a_log_l, dt_l)
    gcum = _each(lambda x: rowsum(jnp.where(incl, row(x), 0.0)), g)
    g_last = _each(lambda x: jnp.sum(jnp.where(last_row, x, 0.0), axis=0, keepdims=True), gcum)
    decay = _each(lambda x: jnp.exp(jnp.where(incl, x - row(x), -jnp.inf)), gcum)
    kk = _each(lambda x: mm(x, x, NT, p_mm), k)
    low = _each(lambda b, x, d: jnp.where(strict, b * x * d, 0.0), beta, kk, decay)
    if inv_known is None:
        power = _each(lambda x: -x, low)
        inv = _each(lambda x: jnp.where(eye, 1.0, 0.0) + x, power)
        for _ in range(5):
            power = _each(lambda x: mm(x, x, NN, p_inv), power)
            inv = _each(lambda x, p: x + mm(x, p, NN, p_inv), inv, power)
    else:
        inv = _each(lambda x, known: _known_inverse(x, known, p_inv), low, inv_known)
    exp_g = _each(jnp.exp, gcum)
    u_v = _each(lambda i, b, x: mm(i, b * x, NN, p_mm), inv, beta, vc)
    w = _each(lambda i, b, e, x: mm(i, b * e * x, NN, p_mm), inv, beta, exp_g, k)
    attn = _each(lambda x, y, d: mm(x, y, NT, p_mm) * d, q, k, decay)
    u = _each(lambda x, y, z: x - mm(y, z, NN, p_mm), u_v, w, s)
    o = _each(lambda x, e, z: mm(x * e, z, NN, p_mm), q, exp_g, s)
    o = _each(lambda x, a, y: x + mm(a, y, NN, p_mm), o, attn, u)
    k_end = _each(lambda x, gl, gc: x * jnp.exp(gl - gc), k, g_last, gcum)
    s_new = _each(lambda z, gl, x, y: z * jnp.exp(gl) + mm(x, y, TN, p_mm), s, g_last, k_end, u)
    return (_each(lambda x, z: _head_norm_gate(x, norm_w, z), o, zc), s_new), inv


def gdn_chunk(h, qc, kc, vc, zc, ab, a_log_l, dt_l, norm_w, s, prec=GDN_PREC, reuse_inverse=False):
    args = ([h], [qc], [kc], [vc], [zc], ab, [a_log_l], [dt_l], norm_w, [s], prec)
    if reuse_inverse:
        inv = lax.stop_gradient(gdn_chunks(*args)[1])
        (y, s_new), _ = gdn_chunks(*args, inv_known=inv)
    else:
        (y, s_new), _ = gdn_chunks(*args)
    return y[0], s_new[0]


DIAG_ROWS = SUB_CHUNK // 2
SHIFT_PAD = 8
SHIFT_ROWS = SHIFT_PAD + CHUNK + SHIFT_PAD
SHIFT_WAYS = 4


class RolledRows:
    def down(self, x, which):
        del which
        return [x] + [pltpu.roll(x, off, 0) for off in range(1, DIAG_ROWS)]

    def up_sum(self, parts, which):
        del which
        acc = parts[0]
        for off in range(1, DIAG_ROWS):
            acc = acc + pltpu.roll(parts[off], CHUNK - off, 0)
        return acc


class SlotRows:
    def __init__(self, slots):
        self.slots = slots

    def down(self, x, which):
        self.slots[which, 0, SHIFT_PAD:SHIFT_PAD + CHUNK, :] = x
        return [x] + [self.slots[which, 0, SHIFT_PAD - off:SHIFT_PAD + CHUNK - off, :] for off in range(1, DIAG_ROWS)]

    def up_sum(self, parts, which):
        acc = parts[0]
        for off in range(1, DIAG_ROWS):
            way = 1 + off % (SHIFT_WAYS - 1)
            self.slots[which, way, SHIFT_PAD:SHIFT_PAD + CHUNK, :] = parts[off]
            acc = acc + self.slots[which, way, SHIFT_PAD + off:SHIFT_PAD + CHUNK + off, :]
        return acc


def _sub_block_rows():
    return jnp.bitwise_and(_iota2((CHUNK, 1), 0), DIAG_ROWS - 1)


def _diag_forward(rows, q, key, bc, v):
    rmod = _sub_block_rows()
    k_d, b_d, v_d = rows.down(key, 0), rows.down(bc, 1), rows.down(v, 2)
    o = None
    for off in range(DIAG_ROWS):
        e = jnp.exp(jnp.where(rmod >= off, bc - b_d[off], -jnp.inf))
        term = jnp.sum(q * k_d[off] * e, axis=-1, keepdims=True) * v_d[off]
        o = term if o is None else o + term
    return o


def _diag_backward(rows, q, key, bc, v, do):
    rmod = _sub_block_rows()
    k_d, b_d, v_d = rows.down(key, 0), rows.down(bc, 1), rows.down(v, 2)
    dq = db = None
    dk_parts, db_parts, dv_parts = [], [], []
    for off in range(DIAG_ROWS):
        e = jnp.exp(jnp.where(rmod >= off, bc - b_d[off], -jnp.inf))
        qe = q * e
        a = jnp.sum(qe * k_d[off], axis=-1, keepdims=True)
        da = jnp.sum(do * v_d[off], axis=-1, keepdims=True)
        dv_parts.append(a * do)
        dq_term = (da * e) * k_d[off]
        dk_term = da * qe
        s = dk_term * k_d[off]
        dq = dq_term if dq is None else dq + dq_term
        db = s if db is None else db + s
        dk_parts.append(dk_term)
        db_parts.append(s)
    return dq, rows.up_sum(dk_parts, 0), db - rows.up_sum(db_parts, 1), rows.up_sum(dv_parts, 2)


def diag_part(rows, differentiable=True):
    forward = functools.partial(_diag_forward, rows)
    if not differentiable:
        return forward
    part = jax.custom_vjp(forward)
    part.defvjp(lambda q, key, bc, v: (forward(q, key, bc, v), (q, key, bc, v)),
                lambda res, do: _diag_backward(rows, *res, do))
    return part


def hgrn_chunks(qb, fb, ib, gb, l0, l1, norm_w, st, prec=HGRN_PREC, diags=None, o_known=None):
    c = CHUNK
    ri, ci = _iota2((4 * c, c), 0), _iota2((4 * c, c), 1)
    rcol = _iota2((c, 1), 0)
    blk0 = jnp.bitwise_and(ri, c - SUB_CHUNK)
    limit = jnp.where(ri < c, ri + 1, jnp.where(ri < 2 * c, blk0, jnp.where(ri < 3 * c, blk0 + SUB_CHUNK,
                                                                          blk0 + DIAG_ROWS)))
    sel = jnp.where(ci < limit, 1.0, 0.0)
    ri, ci = _iota2((c, c), 0), _iota2((c, c), 1)
    lb = _each(lambda a, b: jax.nn.sigmoid(a - b), l0, l1)
    forget = _each(lambda b, f: b + (1.0 - b) * jax.nn.sigmoid(f), lb, fb)
    key = _each(lambda b, f: (1.0 - b) * jax.nn.sigmoid(-f), lb, fb)
    q = _each(_silu, qb)
    v = ib
    logf = _each(jnp.log, forget)
    sums = _each(lambda x: sel_sums(sel, x), logf)
    bc, b_start, b_end, b_half = ([x[i] for x in sums] for i in range(4))
    b_last = _each(lambda x: jnp.sum(x, axis=0, keepdims=True), logf)
    o = _each(lambda x, b, z: mm(x * jnp.exp(b), z, NT, prec), q, bc, st)
    if diags is None:
        diags = [diag_part(RolledRows())] * len(qb)
    o = _each(lambda acc, part, x, ky, b, val: acc + part(x, ky, b, val), o, diags, q, key, bc, v)
    second = jnp.bitwise_and(rcol, SUB_CHUNK - 1) >= DIAG_ROWS
    same_sub = jnp.bitwise_and(ri, c - SUB_CHUNK) == jnp.bitwise_and(ci, c - SUB_CHUNK)
    q_half = _each(lambda x, b, bh: x * jnp.exp(jnp.where(second, b - bh, -jnp.inf)), q, bc, b_half)
    k_half = _each(lambda x, b, bh: x * jnp.exp(jnp.where(second, -jnp.inf, bh - b)), key, bc, b_half)
    a_half = _each(lambda x, z: jnp.where(same_sub, mm(x, z, NT, prec), 0.0), q_half, k_half)
    o = _each(lambda acc, a, val: acc + mm(a, val, NN, prec), o, a_half, v)
    q_rel = _each(lambda x, b, bs: x * jnp.exp(b - bs), q, bc, b_start)
    k_rel = _each(lambda x, b, be: x * jnp.exp(be - b), key, bc, b_end)
    for y in range(c // SUB_CHUNK - 1):
        def scaled(x, b, bs):
            end_y = jnp.sum(jnp.where(rcol == SUB_CHUNK * y + SUB_CHUNK - 1, b, 0.0), axis=0, keepdims=True)
            return x * jnp.exp(jnp.where(rcol >= SUB_CHUNK * (y + 1), bs - end_y, -jnp.inf))
        dq = _each(scaled, q_rel, bc, b_start)
        in_y = (ci >= SUB_CHUNK * y) & (ci < SUB_CHUNK * (y + 1))
        a_y = _each(lambda x, z: jnp.where(in_y, mm(x, z, NT, prec), 0.0), dq, k_rel)
        o = _each(lambda acc, a, val: acc + mm(a, val, NN, prec), o, a_y, v)
    k_state = _each(lambda x, bl, b: x * jnp.exp(bl - b), key, b_last, bc)
    st_new = _each(lambda z, bl, val, x: z * jnp.exp(bl) + mm(val, x, TN, prec), st, b_last, v, k_state)
    if o_known is not None:
        o = _each(_known_value, o, o_known)
    return (_each(lambda x, z: _head_norm_gate(x, norm_w, z), o, gb), st_new), o


def hgrn_chunk(qb, fb, ib, gb, l0, l1, norm_w, st, prec=HGRN_PREC, reuse_output=False):
    args = ([qb], [fb], [ib], [gb], [l0], [l1], norm_w, [st], prec)
    if reuse_output:
        known = lax.stop_gradient(hgrn_chunks(*args)[1])
        (y, st_new), _ = hgrn_chunks(*args, o_known=known)
    else:
        (y, st_new), _ = hgrn_chunks(*args)
    return y[0], st_new[0]


HEAD_VEC = (N_HEADS, 1, LANES)


class _Groups:
    def __init__(self, nc, hb, rev):
        self.nc, self.hb, self.ng, self.rev = nc, hb, N_HEADS // hb, rev

    def _c(self, c):
        return self.nc - 1 - c if self.rev else c

    def cols(self, slab):
        return pl.BlockSpec((CHUNK, self.hb * LANES), lambda c, g: (self._c(c), slab * self.ng + g))

    def tile(self, block):
        return pl.BlockSpec((CHUNK, LANES), lambda c, g: (self._c(c), block))

    def state(self):
        return pl.BlockSpec((None, self.hb, HEAD_DIM, HEAD_DIM), lambda c, g: (self._c(c), g, 0, 0))

    def inverse(self):
        return pl.BlockSpec((None, self.hb, CHUNK, CHUNK), lambda c, g: (self._c(c), g, 0, 0))

    @staticmethod
    def whole(shape):
        return pl.BlockSpec(shape, lambda c, g: (0,) * len(shape))

    def head(self, g, j):
        return j if self.ng == 1 else g * self.hb + j


def _lanes(j):
    return slice(j * LANES, (j + 1) * LANES)


def gdn_fwd(qkv_c, proj, a_log_l, dt_l, norm_w, name, hb=8):
    t = qkv_c.shape[0]
    gr = _Groups(t // CHUNK, hb, rev=False)

    def body(q_ref, k_ref, v_ref, z_ref, ab_ref, al_ref, dt_ref, nw_ref, y_ref, hist_ref, inv_ref, s_ref):
        c, g = pl.program_id(0), pl.program_id(1)

        @pl.when(c == 0)
        def _():
            for j in range(hb):
                s_ref[gr.head(g, j)] = jnp.zeros((HEAD_DIM, HEAD_DIM), F32)

        hs = [gr.head(g, j) for j in range(hb)]
        heads = lambda ref: [ref[:, _lanes(j)] for j in range(hb)]
        s = [s_ref[h] for h in hs]
        for j in range(hb):
            hist_ref[j] = s[j]
        (y, s_new), inv = gdn_chunks(hs, heads(q_ref), heads(k_ref), heads(v_ref), heads(z_ref), ab_ref[...],
                                     [al_ref[h] for h in hs], [dt_ref[h] for h in hs], nw_ref[...], s)
        for j in range(hb):
            y_ref[:, _lanes(j)] = y[j].astype(BF16)
            s_ref[hs[j]] = s_new[j]
            inv_ref[j] = inv[j]

    return pl.pallas_call(
        body, name=name, grid=(gr.nc, gr.ng),
        in_specs=[gr.cols(0), gr.cols(1), gr.cols(2), gr.cols(3), gr.tile(AB_BLOCK),
                  gr.whole(HEAD_VEC), gr.whole(HEAD_VEC), gr.whole((1, LANES))],
        out_specs=[gr.cols(0), gr.state(), gr.inverse()],
        out_shape=[jax.ShapeDtypeStruct((t, 2 * GDN_WIDTH), BF16),
                   jax.ShapeDtypeStruct((gr.nc, N_HEADS, HEAD_DIM, HEAD_DIM), F32),
                   jax.ShapeDtypeStruct((gr.nc, N_HEADS, CHUNK, CHUNK), F32)],
        scratch_shapes=[pltpu.VMEM((N_HEADS, HEAD_DIM, HEAD_DIM), F32)],
        compiler_params=_params(("arbitrary", "arbitrary")),
    )(qkv_c, qkv_c, qkv_c, proj, proj, a_log_l, dt_l, norm_w)


def gdn_bwd(qkv_c, proj, a_log_l, dt_l, norm_w, hist, inv_hist, dy, name):
    t = qkv_c.shape[0]
    hb = N_HEADS
    gr = _Groups(t // CHUNK, hb, rev=True)

    def body(q_ref, k_ref, v_ref, z_ref, ab_ref, al_ref, dt_ref, nw_ref, hist_ref, inv_ref, dy_ref,
             dqkv_ref, dz_ref, dab_ref, dal_ref, ddt_ref, dnw_ref, ds_ref):
        @pl.when(pl.program_id(0) == 0)
        def _():
            dal_ref[...] = jnp.zeros_like(dal_ref)
            ddt_ref[...] = jnp.zeros_like(ddt_ref)
            dnw_ref[...] = jnp.zeros_like(dnw_ref)
            ds_ref[...] = jnp.zeros_like(ds_ref)

        hs = list(range(hb))
        heads = lambda ref: [ref[:, _lanes(j)] for j in hs]
        chunk = functools.partial(gdn_chunks, hs, inv_known=[inv_ref[h] for h in hs])
        _, vjp, _ = jax.vjp(chunk, heads(q_ref), heads(k_ref), heads(v_ref), heads(z_ref),
                            ab_ref[...], [al_ref[h] for h in hs], [dt_ref[h] for h in hs], nw_ref[...],
                            [hist_ref[h] for h in hs], has_aux=True)
        dq, dk, dv, dz, dab, dal, ddt, dnw, ds = vjp((heads(dy_ref), [ds_ref[h] for h in hs]))
        for h in hs:
            dqkv_ref[:, _lanes(h)] = dq[h]
            dqkv_ref[:, _lanes(hb + h)] = dk[h]
            dqkv_ref[:, _lanes(2 * hb + h)] = dv[h]
            dz_ref[:, _lanes(h)] = dz[h].astype(BF16)
            dal_ref[h] += dal[h]
            ddt_ref[h] += ddt[h]
            ds_ref[h] = ds[h]
        dab_ref[...] = dab.astype(BF16)
        dnw_ref[...] += dnw

    return pl.pallas_call(
        body, name=name, grid=(gr.nc, 1),
        in_specs=[gr.cols(0), gr.cols(1), gr.cols(2), gr.cols(3), gr.tile(AB_BLOCK),
                  gr.whole(HEAD_VEC), gr.whole(HEAD_VEC), gr.whole((1, LANES)), gr.state(), gr.inverse(), gr.cols(0)],
        out_specs=[pl.BlockSpec((CHUNK, QKV_WIDTH), lambda c, g: (gr.nc - 1 - c, 0)), gr.cols(3), gr.tile(0),
                   gr.whole(HEAD_VEC), gr.whole(HEAD_VEC), gr.whole((1, LANES))],
        out_shape=[jax.ShapeDtypeStruct((t, QKV_WIDTH), F32), jax.ShapeDtypeStruct((t, CAT_WIDTH), BF16),
                   jax.ShapeDtypeStruct((t, LANES), BF16), jax.ShapeDtypeStruct(HEAD_VEC, F32),
                   jax.ShapeDtypeStruct(HEAD_VEC, F32), jax.ShapeDtypeStruct((1, LANES), F32)],
        scratch_shapes=[pltpu.VMEM((N_HEADS, HEAD_DIM, HEAD_DIM), F32)],
        compiler_params=_params(("arbitrary", "arbitrary")),
    )(qkv_c, qkv_c, qkv_c, proj, proj, a_log_l, dt_l, norm_w, hist, inv_hist, dy)


def hgrn_fwd(proj, l0, l1, norm_w, y, name, hb=8):
    t = proj.shape[0]
    gr = _Groups(t // CHUNK, hb, rev=False)

    def body(q_ref, f_ref, i_ref, g_ref, l0_ref, l1_ref, nw_ref, y_in, y_ref, hist_ref, o_ref, s_ref, shift_ref):
        del y_in
        c, g = pl.program_id(0), pl.program_id(1)

        @pl.when((c == 0) & (g == 0))
        def _():
            shift_ref[...] = jnp.zeros_like(shift_ref)

        @pl.when(c == 0)
        def _():
            for j in range(hb):
                s_ref[gr.head(g, j)] = jnp.zeros((HEAD_DIM, HEAD_DIM), F32)

        hs = [gr.head(g, j) for j in range(hb)]
        heads = lambda ref: [ref[:, _lanes(j)] for j in range(hb)]
        s = [s_ref[h] for h in hs]
        for j in range(hb):
            hist_ref[j] = s[j]
        diags = [diag_part(SlotRows(shift_ref.at[j]), differentiable=False) for j in range(hb)]
        (out, s_new), o_pre = hgrn_chunks(heads(q_ref), heads(f_ref), heads(i_ref), heads(g_ref),
                                          [l0_ref[h] for h in hs], [l1_ref[h] for h in hs], nw_ref[...], s, diags=diags)
        for j in range(hb):
            y_ref[:, _lanes(j)] = out[j].astype(BF16)
            o_ref[:, _lanes(j)] = o_pre[j]
            s_ref[hs[j]] = s_new[j]

    return pl.pallas_call(
        body, name=name, grid=(gr.nc, gr.ng),
        in_specs=[gr.cols(4), gr.cols(5), gr.cols(6), gr.cols(7), gr.whole(HEAD_VEC), gr.whole(HEAD_VEC),
                  gr.whole((1, LANES)), pl.BlockSpec(memory_space=pl.ANY)],
        out_specs=[gr.cols(1), gr.state(), gr.cols(0)],
        out_shape=[jax.ShapeDtypeStruct((t, 2 * GDN_WIDTH), BF16),
                   jax.ShapeDtypeStruct((gr.nc, N_HEADS, HEAD_DIM, HEAD_DIM), F32),
                   jax.ShapeDtypeStruct((t, GDN_WIDTH), F32)],
        scratch_shapes=[pltpu.VMEM((N_HEADS, HEAD_DIM, HEAD_DIM), F32),
                        pltpu.VMEM((hb, 3, SHIFT_WAYS, SHIFT_ROWS, LANES), F32)],
        input_output_aliases={7: 0},
        compiler_params=_params(("arbitrary", "arbitrary")),
    )(proj, proj, proj, proj, l0, l1, norm_w, y)


def hgrn_bwd(proj, l0, l1, norm_w, hist, o_pre, dy, dproj, name):
    t = proj.shape[0]
    hb = N_HEADS
    gr = _Groups(t // CHUNK, hb, rev=True)

    def body(q_ref, f_ref, i_ref, g_ref, l0_ref, l1_ref, nw_ref, hist_ref, o_ref, dy_ref, dproj_in,
             d_ref, dl0_ref, dl1_ref, dnw_ref, ds_ref, shift_ref):
        del dproj_in

        @pl.when(pl.program_id(0) == 0)
        def _():
            dl0_ref[...] = jnp.zeros_like(dl0_ref)
            dl1_ref[...] = jnp.zeros_like(dl1_ref)
            dnw_ref[...] = jnp.zeros_like(dnw_ref)
            ds_ref[...] = jnp.zeros_like(ds_ref)
            shift_ref[...] = jnp.zeros_like(shift_ref)

        hs = list(range(hb))
        heads = lambda ref: [ref[:, _lanes(j)] for j in hs]
        chunk = functools.partial(hgrn_chunks, diags=[diag_part(SlotRows(shift_ref.at[h])) for h in hs],
                                  o_known=heads(o_ref))
        _, vjp, _ = jax.vjp(chunk, heads(q_ref), heads(f_ref), heads(i_ref), heads(g_ref), [l0_ref[h] for h in hs],
                            [l1_ref[h] for h in hs], nw_ref[...], [hist_ref[h] for h in hs], has_aux=True)
        dq, df, di, dg, dl0, dl1, dnw, ds = vjp((heads(dy_ref), [ds_ref[h] for h in hs]))
        for h in hs:
            for slab, val in enumerate((dq, df, di, dg)):
                d_ref[:, _lanes(slab * hb + h)] = val[h].astype(BF16)
            dl0_ref[h] += dl0[h]
            dl1_ref[h] += dl1[h]
            ds_ref[h] = ds[h]
        dnw_ref[...] += dnw

    return pl.pallas_call(
        body, name=name, grid=(gr.nc, 1),
        in_specs=[gr.cols(4), gr.cols(5), gr.cols(6), gr.cols(7), gr.whole(HEAD_VEC), gr.whole(HEAD_VEC),
                  gr.whole((1, LANES)), gr.state(), gr.cols(0), gr.cols(1), ANY_SPEC],
        out_specs=[pl.BlockSpec((CHUNK, 4 * GDN_WIDTH), lambda c, g: (gr.nc - 1 - c, 1)),
                   gr.whole(HEAD_VEC), gr.whole(HEAD_VEC), gr.whole((1, LANES))],
        out_shape=[jax.ShapeDtypeStruct(dproj.shape, BF16), jax.ShapeDtypeStruct(HEAD_VEC, F32),
                   jax.ShapeDtypeStruct(HEAD_VEC, F32), jax.ShapeDtypeStruct((1, LANES), F32)],
        scratch_shapes=[pltpu.VMEM((N_HEADS, HEAD_DIM, HEAD_DIM), F32),
                        pltpu.VMEM((hb, 3, SHIFT_WAYS, SHIFT_ROWS, LANES), F32)],
        input_output_aliases={10: 0},
        compiler_params=_params(("arbitrary", "arbitrary")),
    )(proj, proj, proj, proj, l0, l1, norm_w, hist, o_pre, dy, dproj)


def _adamw(w, g, m, v):
    m = ADAM_B1 * m + (1.0 - ADAM_B1) * g
    v = ADAM_B2 * v + (1.0 - ADAM_B2) * jnp.square(g)
    m_hat = m / (1.0 - ADAM_B1 ** ADAM_STEP)
    v_hat = v / (1.0 - ADAM_B2 ** ADAM_STEP)
    delta = -ADAM_LR * (m_hat / (jnp.sqrt(v_hat) + ADAM_EPS) + ADAM_WD * w)
    return delta, m, v


def adamw_reduce(parts, w, m, v, name, rb=128):
    r, c = w.shape
    rb = min(rb, r)

    def body(p_ref, w_ref, m_ref, v_ref, g_ref, d_ref, mo_ref, vo_ref):
        g = p_ref[0].astype(F32)
        for d in range(1, N_DEV):
            g = g + p_ref[d].astype(F32)
        delta, mn, vn = _adamw(w_ref[...], g, m_ref[...], v_ref[...])
        g_ref[...] = g
        d_ref[...] = delta
        mo_ref[...] = mn
        vo_ref[...] = vn

    blk = pl.BlockSpec((rb, c), lambda i: (i, 0))
    return pl.pallas_call(
        body, name=name, grid=(r // rb,),
        in_specs=[pl.BlockSpec((N_DEV, rb, c), lambda i: (0, i, 0)), blk, blk, blk],
        out_specs=[blk] * 4, out_shape=[jax.ShapeDtypeStruct((r, c), F32)] * 4,
        compiler_params=_params(("parallel",)))(parts, w, m, v)


def adamw_small(w, g, m, v, name):
    def body(w_ref, g_ref, m_ref, v_ref, d_ref, mo_ref, vo_ref):
        delta, mn, vn = _adamw(w_ref[...], g_ref[...], m_ref[...], v_ref[...])
        d_ref[...] = delta
        mo_ref[...] = mn
        vo_ref[...] = vn

    vmem = pl.BlockSpec(memory_space=pltpu.VMEM)
    return pl.pallas_call(body, name=name, in_specs=[vmem] * 4, out_specs=[vmem] * 3,
                          out_shape=[jax.ShapeDtypeStruct(w.shape, F32)] * 3)(w, g, m, v)


def _pack(arrays):
    flat = jnp.concatenate([a.reshape(-1).astype(F32) for a in arrays])
    rows = -(-flat.shape[0] // (8 * LANES)) * 8
    return jnp.pad(flat, (0, rows * LANES - flat.shape[0])).reshape(rows, LANES)


def _unpack(packed, shapes):
    flat, out, off = packed.reshape(-1), [], 0
    for s in shapes:
        n = 1
        for d in s:
            n *= d
        out.append(flat[off:off + n].reshape(s))
        off += n
    return out


def _relu2_epilogue(acc, _):
    r = jnp.maximum(acc, 0.0)
    return acc, r * r


def _relu2_bwd_epilogue(acc, a1):
    return (acc * (2.0 * jnp.maximum(a1, 0.0)),)


def kernel(x, w_in, conv_w, gdn_a_log, gdn_dt_bias, gdn_norm_w, hgrn_lb_logits, hgrn_norm_w, w_out, norm_mix_w, norm_ffn_w, w_ff1, w_ff2, norm_final_w, loss_target, m_w_in, m_conv_w, m_gdn_a_log, m_gdn_dt_bias, m_gdn_norm_w, m_hgrn_lb_logits, m_hgrn_norm_w, m_w_out, m_norm_mix_w, m_norm_ffn_w, m_w_ff1, m_w_ff2, m_norm_final_w, v_w_in, v_conv_w, v_gdn_a_log, v_gdn_dt_bias, v_gdn_norm_w, v_hgrn_lb_logits, v_hgrn_norm_w, v_w_out, v_norm_mix_w, v_norm_ffn_w, v_w_ff1, v_w_ff2, v_norm_final_w):
    me = _my_flat()
    xs = x[0]
    target = loss_target[0]
    shard_in = w_in.shape[2]
    shard_conv = conv_w.shape[2]

    tok = lambda t: t[0:1, 0:1]
    own = lambda src: lax.dynamic_index_in_dim(src, me, 0, keepdims=False)

    g_in, g_conv = gather_two_level([w_in[0].astype(BF16), conv_w[0]], "gather_w_in")
    h_g1, t_g1 = exchange_start([w_out[0].astype(BF16), w_ff1[0].astype(BF16)], True, "gather_mid_start", after=[g_in])
    h_g2, t_g2 = exchange_start([w_ff2[0].astype(BF16)], True, "gather_ff2_start", after=[t_g1])
    w_cat = weights_to_cat(g_in)
    conv_full = jnp.transpose(g_conv, (1, 0, 2)).reshape(4, QKV_WIDTH)

    lane_b = lambda p: jnp.broadcast_to(p.reshape(N_HEADS, 1, 1), HEAD_VEC)
    a_log_l, dt_l = lane_b(gdn_a_log[0]), lane_b(gdn_dt_bias[0])
    l0 = hgrn_lb_logits[0].reshape(HEAD_VEC)
    l1 = hgrn_lb_logits[1].reshape(HEAD_VEC)

    n1, r1 = rms_fwd(xs, norm_mix_w + tok(t_g1) + tok(t_g2), None, "rms_mix")
    proj = matmul(n1, w_cat, "nn", "in_proj", tn=CAT_WIDTH // 5)
    qkv_c = conv_fwd(proj, conv_full, "conv_fwd")
    y_half, hist_a, inv_a = gdn_fwd(qkv_c, proj, a_log_l, dt_l, gdn_norm_w, "gdn_fwd")
    y, hist_b, o_b = hgrn_fwd(proj, l0, l1, hgrn_norm_w, y_half, "hgrn_fwd")
    (s_out, s_ff1), (l_out, l_ff1) = exchange_wait(h_g1, "gather_mid_wait", after=[y])
    w_out_full = _own_slot(l_out, s_out).reshape(D_MODEL, D_MODEL)
    w_ff1_sh = _own_slot(l_ff1, s_ff1)
    mix = matmul(y, w_out_full, "nn", "out_proj")
    h1, n2, r2 = rms_fwd(xs, norm_ffn_w, mix, "rms_ffn")
    a1, act = matmul(n2, w_ff1_sh, "nn", "ff1", out_dtypes=(F32, BF16), epilogue=_relu2_epilogue, b_shards=True)
    (s_ff2,), (l_ff2,) = exchange_wait(h_g2, "gather_ff2_wait", after=[act])
    w_ff2_full = _own_slot(l_ff2, s_ff2).reshape(D_FF, D_MODEL)
    ff = matmul(act, w_ff2_full, "nn", "ff2")
    loss_sum, dh2, dh2_b, d_final = loss_head(h1, ff, norm_final_w.reshape(1, D_MODEL), target, "loss_head")

    da1 = matmul(dh2_b, w_ff2_full, "nt", "d_act", out_dtypes=(BF16,), epilogue=_relu2_bwd_epilogue, extra=a1)
    t_all = xs.shape[0]
    dw_ff2 = matmul(act, dh2_b, "tn", "dw_ff2", out_dtypes=(BF16,), tk=t_all)
    p_ff2 = dw_ff2.reshape(N_DEV, D_FF // N_DEV, D_MODEL)
    h_s1, t_s1 = exchange_start([p_ff2], False, "scatter_ff2_start")
    dn2 = matmul(da1, w_ff1_sh, "nt", "d_n2", after=[t_s1], b_shards=True, k_group=4)
    p_ff1 = matmul(n2, da1, "tn", "dw_ff1", out_dtypes=(BF16,), tn=D_FF // N_DEV, tk=t_all, after=[t_s1], out_shards=True)
    h_s2, t_s2 = exchange_start([p_ff1], False, "scatter_ff1_start")
    dh1, dh1_b, d_ffn = rms_bwd(h1, r2, norm_ffn_w + tok(t_s2), dn2, dh2, "rms_ffn_bwd")
    dmix = matmul(dh1_b, w_out_full, "nt", "d_mix")
    dw_out = matmul(y, dh1_b, "tn", "dw_out", out_dtypes=(BF16,), tk=t_all)
    p_out = dw_out.reshape(N_DEV, D_MODEL // N_DEV, D_MODEL)
    h_s3, t_s3 = exchange_start([p_out], False, "scatter_out_start")
    d_qkv_c, dproj, dab, d_alog_l, d_dt_l, d_gnw = gdn_bwd(
        qkv_c, proj, a_log_l, dt_l, gdn_norm_w + tok(t_s3), hist_a, inv_a, dmix, "gdn_bwd")
    dproj, dl0, dl1, d_hnw = hgrn_bwd(proj, l0, l1, hgrn_norm_w + tok(t_s3), hist_b, o_b, dmix, dproj, "hgrn_bwd")
    dproj, d_conv_full = conv_bwd(proj, d_qkv_c, conv_full, dproj, "conv_bwd")
    dproj = lax.dynamic_update_slice(dproj, dab, (0, MAIN_WIDTH))
    dw_cat = matmul(n1, dproj, "tn", "dw_in", out_dtypes=(BF16,), tm=512, tn=CAT_WIDTH // 5, tk=t_all)
    p_in = cat_to_shards(dw_cat, shard_in)
    h_s4, t_s4 = exchange_start([p_in], False, "scatter_in_start")

    (s_ff2g,), (r_ff2,) = exchange_wait(h_s1, "scatter_ff2_wait", after=[t_s4])
    (s_ff1g,), (r_ff1,) = exchange_wait(h_s2, "scatter_ff1_wait", after=[t_s4])
    (s_outg,), (r_out,) = exchange_wait(h_s3, "scatter_out_wait", after=[t_s4])
    g_w_ff2, d_w_ff2, nm_w_ff2, nv_w_ff2 = adamw_reduce(
        _own_slot(r_ff2, own(s_ff2g)), w_ff2[0], m_w_ff2[0], v_w_ff2[0], "adamw_w_ff2")
    g_w_ff1, d_w_ff1, nm_w_ff1, nv_w_ff1 = adamw_reduce(
        _own_slot(r_ff1, own(s_ff1g)), w_ff1[0], m_w_ff1[0], v_w_ff1[0], "adamw_w_ff1")
    g_w_out, d_w_out, nm_w_out, nv_w_out = adamw_reduce(
        _own_slot(r_out, own(s_outg)), w_out[0], m_w_out[0], v_w_out[0], "adamw_w_out")
    dn1 = matmul(dproj, w_cat, "nt", "d_n1", tk=CAT_WIDTH // 5, after=[t_s4])
    dx, _, d_mix = rms_bwd(xs, r1, norm_mix_w, dn1, dh1, "rms_mix_bwd")
    (s_ing,), (r_in,) = exchange_wait(h_s4, "scatter_in_wait", after=[dx, d_w_ff2, d_w_ff1, d_w_out])
    g_w_in, d_w_in, nm_w_in, nv_w_in = adamw_reduce(
        _own_slot(r_in, own(s_ing)), w_in[0], m_w_in[0], v_w_in[0], "adamw_w_in")

    d_lb = jnp.stack([dl0.reshape(GDN_WIDTH), dl1.reshape(GDN_WIDTH)])
    small_shapes = [(1, N_HEADS), (1, N_HEADS), (1, HEAD_DIM), (2, GDN_WIDTH), (1, HEAD_DIM), (1, D_MODEL),
                    (1, D_MODEL), (D_MODEL,), (4, QKV_WIDTH)]
    small = _pack([d_alog_l[:, 0, 0], d_dt_l[:, 0, 0], d_gnw, d_lb, d_hnw, d_mix, d_ffn, d_final, d_conv_full])
    red = allreduce_small(small, "allreduce_small")
    g_alog, g_dt, g_gnw, g_lb, g_hnw, g_mix, g_ffn, g_final, g_conv_full = _unpack(red, small_shapes)
    g_conv = lax.dynamic_slice(g_conv_full, (0, me * shard_conv), (4, shard_conv)).reshape(1, 4, shard_conv)
    small_g = [g_alog, g_dt, g_gnw, g_lb, g_hnw, g_mix, g_ffn, g_final, g_conv]
    small_w = [gdn_a_log, gdn_dt_bias, gdn_norm_w, hgrn_lb_logits, hgrn_norm_w, norm_mix_w, norm_ffn_w, norm_final_w, conv_w]
    small_m = [m_gdn_a_log, m_gdn_dt_bias, m_gdn_norm_w, m_hgrn_lb_logits, m_hgrn_norm_w, m_norm_mix_w, m_norm_ffn_w,
               m_norm_final_w, m_conv_w]
    small_v = [v_gdn_a_log, v_gdn_dt_bias, v_gdn_norm_w, v_hgrn_lb_logits, v_hgrn_norm_w, v_norm_mix_w, v_norm_ffn_w,
               v_norm_final_w, v_conv_w]
    shapes = [a.shape for a in small_w]
    d_s, m_s, v_s = adamw_small(_pack(small_w), _pack(small_g), _pack(small_m), _pack(small_v), "adamw_small")
    d_alog, d_dt, d_gn, d_lbl, d_hn, d_nm, d_nf, d_nfin, d_cw = _unpack(d_s, shapes)
    m_alog, m_dt, m_gn, m_lbl, m_hn, m_nm, m_nf, m_nfin, m_cw = _unpack(m_s, shapes)
    v_alog, v_dt, v_gn, v_lbl, v_hn, v_nm, v_nf, v_nfin, v_cw = _unpack(v_s, shapes)

    loss = lax.psum(loss_sum[0, 0], ("x", "y", "c"))
    lead = lambda a: a[None]
    grads = [lead(g_w_in), g_conv, g_alog, g_dt, g_gnw, g_lb, g_hnw, lead(g_w_out), g_mix, g_ffn,
             lead(g_w_ff1), lead(g_w_ff2), g_final]
    deltas = [lead(d_w_in), d_cw, d_alog, d_dt, d_gn, d_lbl, d_hn, lead(d_w_out), d_nm, d_nf,
              lead(d_w_ff1), lead(d_w_ff2), d_nfin]
    new_m = [lead(nm_w_in), m_cw, m_alog, m_dt, m_gn, m_lbl, m_hn, lead(nm_w_out), m_nm, m_nf,
             lead(nm_w_ff1), lead(nm_w_ff2), m_nfin]
    new_v = [lead(nv_w_in), v_cw, v_alog, v_dt, v_gn, v_lbl, v_hn, lead(nv_w_out), v_nm, v_nf,
             lead(nv_w_ff1), lead(nv_w_ff2), v_nfin]
    return (loss, dx[None], *grads, *deltas, *new_m, *new_v)
```

```python
import functools

import jax
import jax.numpy as jnp
from jax import lax
from jax.experimental import pallas as pl
from jax.experimental.pallas import tpu as pltpu

F32 = jnp.float32
BF16 = jnp.bfloat16
HI = lax.Precision.HIGHEST

N_DEV = 8
D_MODEL = 2048
CHUNK = 64
SUB_CHUNK = 16
HEAD_DIM = 128
N_HEADS = 8
GDN_WIDTH = N_HEADS * HEAD_DIM
D_FF = 4 * D_MODEL
QKV_WIDTH = 3 * GDN_WIDTH
MAIN_WIDTH = 8 * GDN_WIDTH
CAT_WIDTH = MAIN_WIDTH + 128
AB_BLOCK = MAIN_WIDTH // 128
NORM_EPS = 1e-6
L2_EPS = 1e-6
LANES = 128
VMEM_LIMIT = 56 * 1024 * 1024

ADAM_LR = 0.001
ADAM_B1 = 0.9
ADAM_B2 = 0.999
ADAM_EPS = 1e-08
ADAM_WD = 0.01
ADAM_STEP = 10

MESH = pl.DeviceIdType.MESH


def _params(sem=None):
    return pltpu.CompilerParams(dimension_semantics=sem, vmem_limit_bytes=VMEM_LIMIT)


def _dot(a, b, dims, prec=None):
    return lax.dot_general(a, b, (dims, ((), ())), precision=prec, preferred_element_type=F32)


NN = ((1,), (0,))
NT = ((1,), (1,))
TN = ((0,), (0,))


def _split_bf16(x, pieces):
    out = []
    for _ in range(pieces - 1):
        p = x.astype(BF16)
        out.append(p)
        x = x - p.astype(F32)
    out.append(x.astype(BF16))
    return out


def _mm_raw(a, b, dims, prec):
    if prec == "hi":
        return _dot(a, b, dims, HI)
    if prec == "bf":
        return _dot(a.astype(BF16), b.astype(BF16), dims)
    a_hi, a_lo = _split_bf16(a, 2)
    b_hi, b_lo = _split_bf16(b, 2)
    return _dot(a_hi, b_hi, dims) + (_dot(a_hi, b_lo, dims) + _dot(a_lo, b_hi, dims))


@functools.partial(jax.custom_vjp, nondiff_argnums=(2, 3))
def mm(a, b, dims, prec):
    return _mm_raw(a, b, dims, prec)


def _mm_fwd(a, b, dims, prec):
    return _mm_raw(a, b, dims, prec), (a, b)


def _mm_bwd(dims, prec, res, ct):
    a, b = res
    if dims == NN:
        return _mm_raw(ct, b, NT, prec), _mm_raw(a, ct, TN, prec)
    if dims == NT:
        return _mm_raw(ct, b, NN, prec), _mm_raw(ct, a, TN, prec)
    return _mm_raw(b, ct, NT, prec), _mm_raw(a, ct, NN, prec)


mm.defvjp(_mm_fwd, _mm_bwd)


def _sel_raw(sel, x, dims):
    sel = sel.astype(BF16)
    p0, p1, p2 = _split_bf16(x, 3)
    return _dot(sel, p0, dims) + (_dot(sel, p1, dims) + _dot(sel, p2, dims))


def _sel_parts(sel, x):
    c = x.shape[0]
    full = _sel_raw(sel, x, NN)
    return tuple(full[i * c:(i + 1) * c] for i in range(sel.shape[0] // c))


@jax.custom_vjp
def sel_sums(sel, x):
    return _sel_parts(sel, x)


def _sel_fwd(sel, x):
    return _sel_parts(sel, x), sel


def _sel_bwd(sel, cts):
    return jnp.zeros_like(sel), _sel_raw(sel, jnp.concatenate(cts, axis=0), TN)


sel_sums.defvjp(_sel_fwd, _sel_bwd)


@jax.custom_vjp
def _known_value(computed, known):
    del computed
    return known


_known_value.defvjp(lambda computed, known: (known, None), lambda _, ct: (ct, jnp.zeros_like(ct)))


def _my_flat():
    return 4 * lax.axis_index("x") + 2 * lax.axis_index("y") + lax.axis_index("c")


def _peer(k):
    x, y, c = lax.axis_index("x"), lax.axis_index("y"), lax.axis_index("c")
    kx, ky, kc = (k >> 2) & 1, (k >> 1) & 1, k & 1
    px = (1 - x) if kx else x
    py = (1 - y) if ky else y
    pc = (1 - c) if kc else c
    return (px, py, pc), 4 * px + 2 * py + pc


def gather_two_level(xs, name):
    n = len(xs)

    def body(*refs):
        x_refs, y_refs = refs[:n], refs[n:2 * n]
        send_sems, recv_sems, local_sems = refs[2 * n:]
        x, y, c = lax.axis_index("x"), lax.axis_index("y"), lax.axis_index("c")
        me, sibling = (x, y, c), (x, y, 1 - c)
        chips = [(1 - x, y), (x, 1 - y), (1 - x, 1 - y)]
        flat = lambda p: 4 * p[0] + 2 * p[1] + p[2]

        def copy(a, k, block, to, src=None):
            return pltpu.make_async_remote_copy(
                src_ref=y_refs[a].at[flat(block)] if src is None else src, dst_ref=y_refs[a].at[flat(block)],
                send_sem=send_sems.at[a, k], recv_sem=recv_sems.at[a, k], device_id=to, device_id_type=MESH)

        mine = [pltpu.make_async_copy(x_refs[a], y_refs[a].at[flat(me)], local_sems.at[a]) for a in range(n)]
        for cp in mine:
            cp.start()
        first = [copy(a, 0, me, sibling, src=x_refs[a]) for a in range(n)]
        first += [copy(a, 1 + j, me, (*chip, c), src=x_refs[a]) for j, chip in enumerate(chips) for a in range(n)]
        for cp in first:
            cp.start()
        passed = []
        for j, chip in enumerate(chips):
            for a in range(n):
                copy(a, 1 + j, (*chip, c), me).wait_recv()
                cp = copy(a, 4 + j, (*chip, c), sibling)
                cp.start()
                passed.append(cp)
        for a in range(n):
            copy(a, 0, sibling, me).wait_recv()
        for j, chip in enumerate(chips):
            for a in range(n):
                copy(a, 4 + j, (*chip, 1 - c), me).wait_recv()
        for cp in first + passed:
            cp.wait_send()
        for cp in mine:
            cp.wait()

    any_spec = pl.BlockSpec(memory_space=pl.ANY)
    return pl.pallas_call(
        body, name=name, out_shape=[jax.ShapeDtypeStruct((N_DEV,) + x.shape, x.dtype) for x in xs],
        in_specs=[any_spec] * n, out_specs=[any_spec] * n,
        scratch_shapes=[pltpu.SemaphoreType.DMA((n, N_DEV - 1)), pltpu.SemaphoreType.DMA((n, N_DEV - 1)),
                        pltpu.SemaphoreType.DMA((n,))],
    )(*xs)


HBM_SPEC = pl.BlockSpec(memory_space=pltpu.HBM)
SEM_SPEC = pl.BlockSpec(memory_space=pltpu.SEMAPHORE)
ANY_SPEC = pl.BlockSpec(memory_space=pl.ANY)
DATAFLOW = pltpu.SideEffectType.DATAFLOW_SIDE_EFFECTING


def _in_hbm(x):
    return pltpu.with_memory_space_constraint(x, pltpu.HBM)


def exchange_start(xs, gather, name, after=()):
    n, n_after = len(xs), len(after)

    def body(*refs):
        x_refs, land_refs = refs[:n], refs[n:2 * n]
        sems = refs[2 * n + n_after:2 * n + n_after + 2 * n]
        token = refs[-1]
        me = _my_flat()
        for k in range(1, N_DEV):
            peer, peer_flat = _peer(k)
            for a in range(n):
                src = x_refs[a] if gather else x_refs[a].at[peer_flat]
                pltpu.make_async_remote_copy(src_ref=src, dst_ref=land_refs[a].at[me], send_sem=sems[a],
                                             recv_sem=sems[n + a], device_id=peer, device_id_type=MESH).start()
        token[...] = jnp.zeros_like(token)

    lands = [_in_hbm(lax.empty(((N_DEV,) + x.shape) if gather else x.shape, x.dtype)) for x in xs]
    hbm_out = [pltpu.HBM(x.shape, x.dtype) for x in xs] + [pltpu.HBM(l.shape, l.dtype) for l in lands]
    res = pl.pallas_call(
        body, name=name,
        out_shape=(*([pltpu.SemaphoreType.DMA(())] * (2 * n)), *hbm_out, jax.ShapeDtypeStruct((8, LANES), F32)),
        in_specs=[HBM_SPEC] * (2 * n) + [ANY_SPEC] * n_after,
        out_specs=(*([SEM_SPEC] * (2 * n)), *([HBM_SPEC] * (2 * n)), pl.BlockSpec(memory_space=pltpu.VMEM)),
        input_output_aliases={i: 2 * n + i for i in range(2 * n)},
        compiler_params=pltpu.CompilerParams(has_side_effects=DATAFLOW),
    )(*[_in_hbm(x) for x in xs], *lands, *after)
    return (list(res[:2 * n]), list(res[2 * n:3 * n]), list(res[3 * n:4 * n])), res[-1]


def exchange_wait(handle, name, after=()):
    sems, xs, lands = handle
    n, n_after = len(xs), len(after)

    def body(*refs):
        land_refs = refs[n:2 * n]
        sem_refs = refs[2 * n:4 * n]
        for a in range(n):
            seven = land_refs[a].at[pl.ds(0, N_DEV - 1)]
            cp = pltpu.make_async_remote_copy(src_ref=seven, dst_ref=seven, send_sem=sem_refs[a],
                                              recv_sem=sem_refs[n + a], device_id=_peer(1)[0], device_id_type=MESH)
            cp.wait_send()
            cp.wait_recv()

    res = pl.pallas_call(
        body, name=name,
        out_shape=[pltpu.HBM(x.shape, x.dtype) for x in xs] + [pltpu.HBM(l.shape, l.dtype) for l in lands],
        in_specs=[HBM_SPEC] * (2 * n) + [SEM_SPEC] * (2 * n) + [ANY_SPEC] * n_after,
        out_specs=[HBM_SPEC] * (2 * n),
        input_output_aliases={i: i for i in range(2 * n)},
        compiler_params=pltpu.CompilerParams(has_side_effects=DATAFLOW),
    )(*xs, *lands, *sems, *after)
    return list(res[:n]), list(res[n:])


def _own_slot(land, block):
    return lax.dynamic_update_slice(land, block[None], (_my_flat(),) + (0,) * block.ndim)


def allreduce_small(x, name):
    rows = x.shape[0]

    def body(x_ref, o_ref, buf, send_sems, recv_sems):
        me = _my_flat()
        buf[me] = x_ref[...]
        sends = []
        for k in range(1, N_DEV):
            peer, _ = _peer(k)
            cp = pltpu.make_async_remote_copy(
                src_ref=x_ref, dst_ref=buf.at[me], send_sem=send_sems.at[k], recv_sem=recv_sems.at[k],
                device_id=peer, device_id_type=MESH)
            cp.start()
            sends.append(cp)
        for k in range(1, N_DEV):
            peer, peer_flat = _peer(k)
            pltpu.make_async_remote_copy(
                src_ref=x_ref, dst_ref=buf.at[peer_flat], send_sem=send_sems.at[k], recv_sem=recv_sems.at[k],
                device_id=peer, device_id_type=MESH).wait_recv()
        for cp in sends:
            cp.wait_send()
        acc = buf[0]
        for d in range(1, N_DEV):
            acc = acc + buf[d]
        o_ref[...] = acc

    vmem = pl.BlockSpec(memory_space=pltpu.VMEM)
    return pl.pallas_call(
        body, name=name, out_shape=jax.ShapeDtypeStruct((rows, LANES), F32),
        in_specs=[vmem], out_specs=vmem,
        scratch_shapes=[pltpu.VMEM((N_DEV, rows, LANES), F32),
                        pltpu.SemaphoreType.DMA((N_DEV,)), pltpu.SemaphoreType.DMA((N_DEV,))],
    )(x)


def matmul(a, b, mode, name, out_dtypes=(F32,), epilogue=None, extra=None, tm=1024, tn=1024, tk=2048, after=(),
           b_shards=False, out_shards=False, k_group=1):
    if b_shards:
        n_sh, b_rows, b_cols = b.shape
    if mode == "nn":
        (m, kd), n = a.shape, (n_sh * b_cols if b_shards else b.shape[1])
        if b_shards:
            tn = b_cols
    elif mode == "nt":
        (m, kd), n = a.shape, (b_rows if b_shards else b.shape[0])
        if b_shards:
            tk = k_group * b_cols
    else:
        (kd, m), n = a.shape, b.shape[1]
    tm, tn, tk = min(tm, m), min(tn, n), min(tk, kd)
    assert m % tm == 0 and n % tn == 0 and kd % tk == 0, (name, m, n, kd, tm, tn, tk)
    ksteps = kd // tk
    dims = {"nn": NN, "nt": NT, "tn": TN}[mode]
    n_out = len(out_dtypes)
    n_in = 2 + (extra is not None) + len(after)

    def finish(acc, e_ref, o_refs):
        outs = (acc,) if epilogue is None else epilogue(acc, e_ref[...] if e_ref is not None else None)
        for o_ref, o in zip(o_refs, outs):
            o_ref[...] = o.astype(o_ref.dtype)

    def product(a_ref, b_ref):
        if mode == "nt" and b_shards:
            w = b_cols
            parts = [_dot(a_ref[:, s * w:(s + 1) * w], b_ref[s], dims) for s in range(k_group)]
            return functools.reduce(lambda p, q: p + q, parts)
        return _dot(a_ref[...], b_ref[...], dims)

    def body(*refs):
        a_ref, b_ref = refs[0], refs[1]
        e_ref = refs[2] if extra is not None else None
        o_refs = refs[n_in:n_in + n_out]
        if ksteps == 1:
            finish(product(a_ref, b_ref), e_ref, o_refs)
            return
        acc_ref = refs[-1]
        kk = pl.program_id(2)

        @pl.when(kk == 0)
        def _():
            acc_ref[...] = jnp.zeros_like(acc_ref)

        acc_ref[...] += product(a_ref, b_ref)

        @pl.when(kk == ksteps - 1)
        def _():
            finish(acc_ref[...], e_ref, o_refs)

    if mode == "nn":
        a_spec = pl.BlockSpec((tm, tk), lambda i, j, k: (i, k))
        b_spec = (pl.BlockSpec((None, tk, tn), lambda i, j, k: (j, k, 0)) if b_shards
                  else pl.BlockSpec((tk, tn), lambda i, j, k: (k, j)))
    elif mode == "nt":
        a_spec = pl.BlockSpec((tm, tk), lambda i, j, k: (i, k))
        b_spec = (pl.BlockSpec((k_group, tn, b_cols), lambda i, j, k: (k, j, 0)) if b_shards
                  else pl.BlockSpec((tn, tk), lambda i, j, k: (j, k)))
    else:
        a_spec = pl.BlockSpec((tk, tm), lambda i, j, k: (k, i))
        b_spec = pl.BlockSpec((tk, tn), lambda i, j, k: (k, j))
    o_spec = pl.BlockSpec((tm, tn), lambda i, j, k: (i, j))
    res_spec = pl.BlockSpec((None, tm, tn), lambda i, j, k: (j, i, 0)) if out_shards else o_spec
    res_shape = (n // tn, m, tn) if out_shards else (m, n)
    in_specs = [a_spec, b_spec] + ([o_spec] if extra is not None else []) + [ANY_SPEC] * len(after)
    args = (a, b) + ((extra,) if extra is not None else ()) + tuple(after)
    res = pl.pallas_call(
        body, name=name, grid=(m // tm, n // tn, ksteps),
        in_specs=in_specs, out_specs=[res_spec] * n_out,
        out_shape=[jax.ShapeDtypeStruct(res_shape, dt) for dt in out_dtypes],
        scratch_shapes=[pltpu.VMEM((tm, tn), F32)] if ksteps > 1 else [],
        compiler_params=_params(("parallel", "parallel", "arbitrary")),
    )(*args)
    return res if n_out > 1 else res[0]


GATE_COL = 4 * GDN_WIDTH
RELAYOUT_ROWS = 256


def _cat_of_win(j):
    if j < GATE_COL:
        return j
    if j < GATE_COL + 2 * N_HEADS:
        return MAIN_WIDTH + (j - GATE_COL)
    return j - 2 * N_HEADS


def _win_of_cat(c):
    if c < GATE_COL:
        return c
    if c < MAIN_WIDTH:
        return c + 2 * N_HEADS
    if c < MAIN_WIDTH + 2 * N_HEADS:
        return GATE_COL + (c - MAIN_WIDTH)
    return None


def _runs(first, count, mapping):
    runs, i = [], 0
    while i < count:
        start, n = mapping(first + i), 1
        while i + n < count and mapping(first + i + n) == start + n:
            n += 1
        runs.append((start, n))
        i += n
    return runs


def weights_to_cat(g_in):
    n_dev, rows, shard = g_in.shape

    def body(x_ref, o_ref):
        for b in range(CAT_WIDTH // LANES):
            live = sum(_win_of_cat(LANES * b + i) is not None for i in range(LANES))
            parts = []
            for start, n in _runs(LANES * b, live, _win_of_cat):
                while n > 0:
                    d, o = divmod(start, shard)
                    take = min(n, shard - o)
                    parts.append(x_ref[d, :, o:o + take])
                    start, n = start + take, n - take
            if live < LANES:
                parts.append(jnp.zeros((RELAYOUT_ROWS, LANES - live), g_in.dtype))
            o_ref[:, LANES * b:LANES * (b + 1)] = parts[0] if len(parts) == 1 else jnp.concatenate(parts, axis=1)

    return pl.pallas_call(
        body, name="weights_to_cat", grid=(rows // RELAYOUT_ROWS,),
        in_specs=[pl.BlockSpec((n_dev, RELAYOUT_ROWS, shard), lambda i: (0, i, 0))],
        out_specs=pl.BlockSpec((RELAYOUT_ROWS, CAT_WIDTH), lambda i: (i, 0)),
        out_shape=jax.ShapeDtypeStruct((rows, CAT_WIDTH), g_in.dtype),
        compiler_params=_params(("parallel",)))(g_in)


def cat_to_shards(dw_cat, shard):
    rows = dw_cat.shape[0]

    def body(x_ref, o_ref):
        for d in range(N_DEV):
            for t0 in range(0, shard, LANES):
                width = min(LANES, shard - t0)
                parts = [x_ref[:, c:c + n] for c, n in _runs(d * shard + t0, width, _cat_of_win)]
                o_ref[d, :, t0:t0 + width] = parts[0] if len(parts) == 1 else jnp.concatenate(parts, axis=1)

    return pl.pallas_call(
        body, name="cat_to_shards", grid=(rows // RELAYOUT_ROWS,),
        in_specs=[pl.BlockSpec((RELAYOUT_ROWS, CAT_WIDTH), lambda i: (i, 0))],
        out_specs=pl.BlockSpec((N_DEV, RELAYOUT_ROWS, shard), lambda i: (0, i, 0)),
        out_shape=jax.ShapeDtypeStruct((N_DEV, rows, shard), dw_cat.dtype),
        compiler_params=_params(("parallel",)))(dw_cat)


ROW_BLOCK = 512


def rms_fwd(x, w, add, name):
    t, d = x.shape
    has_add = add is not None

    def body(*refs):
        x_ref, w_ref = refs[0], refs[1]
        rest = refs[2:]
        if has_add:
            add_ref, h_ref, n_ref, r_ref = rest
            h = x_ref[...] + add_ref[...]
            h_ref[...] = h
        else:
            n_ref, r_ref = rest
            h = x_ref[...]
        r = lax.rsqrt(jnp.mean(h * h, axis=-1, keepdims=True) + NORM_EPS)
        n_ref[...] = (h * r * w_ref[...]).astype(BF16)
        r_ref[...] = r

    row = pl.BlockSpec((ROW_BLOCK, d), lambda i: (i, 0))
    wspec = pl.BlockSpec((1, d), lambda i: (0, 0))
    rspec = pl.BlockSpec((ROW_BLOCK, 1), lambda i: (i, 0))
    in_specs = [row, wspec] + ([row] if has_add else [])
    out_specs = ([row] if has_add else []) + [row, rspec]
    out_shape = ([jax.ShapeDtypeStruct((t, d), F32)] if has_add else []) + [
        jax.ShapeDtypeStruct((t, d), BF16), jax.ShapeDtypeStruct((t, 1), F32)]
    args = (x, w) + ((add,) if has_add else ())
    return pl.pallas_call(body, name=name, grid=(t // ROW_BLOCK,), in_specs=in_specs, out_specs=out_specs,
                          out_shape=out_shape, compiler_params=_params(("parallel",)))(*args)


def loss_head(h1, delta, w, target, name):
    t, d = h1.shape

    def body(h_ref, dl_ref, w_ref, t_ref, loss_ref, dhb_ref, dw_ref):
        @pl.when(pl.program_id(0) == 0)
        def _():
            loss_ref[...] = jnp.zeros_like(loss_ref)
            dw_ref[...] = jnp.zeros_like(dw_ref)

        h = h_ref[...] + dl_ref[...]
        wv = w_ref[...]
        r = lax.rsqrt(jnp.mean(h * h, axis=-1, keepdims=True) + NORM_EPS)
        yn = h * r
        e = yn * wv - t_ref[...]
        loss_ref[...] += 0.5 * jnp.sum(jnp.sum(e * e, axis=-1, keepdims=True), axis=0, keepdims=True) / d
        dy = e / d
        dw_ref[...] += jnp.sum(dy * yn, axis=0, keepdims=True)
        dyn = dy * wv
        dh = r * (dyn - yn * jnp.mean(dyn * yn, axis=-1, keepdims=True))
        dhb_ref[...] = dh.astype(BF16)

    row = pl.BlockSpec((ROW_BLOCK, d), lambda i: (i, 0))
    wspec = pl.BlockSpec((1, d), lambda i: (0, 0))
    one = pl.BlockSpec((1, 1), lambda i: (0, 0))
    return pl.pallas_call(
        body, name=name, grid=(t // ROW_BLOCK,),
        in_specs=[row, row, wspec, row], out_specs=[one, row, wspec],
        out_shape=[jax.ShapeDtypeStruct((1, 1), F32), jax.ShapeDtypeStruct((t, d), BF16),
                   jax.ShapeDtypeStruct((1, d), F32)],
        compiler_params=_params(("arbitrary",)))(h1, delta, w, target)


def rms_bwd(h, r, w, dn, dres, out_dtype, name):
    t, d = h.shape

    def body(h_ref, r_ref, w_ref, dn_ref, dres_ref, dh_ref, dw_ref):
        @pl.when(pl.program_id(0) == 0)
        def _():
            dw_ref[...] = jnp.zeros_like(dw_ref)

        rv = r_ref[...]
        yn = h_ref[...] * rv
        dnv = dn_ref[...].astype(F32)
        dw_ref[...] += jnp.sum(dnv * yn, axis=0, keepdims=True)
        dyn = dnv * w_ref[...]
        dh = dres_ref[...].astype(F32) + rv * (dyn - yn * jnp.mean(dyn * yn, axis=-1, keepdims=True))
        dh_ref[...] = dh.astype(out_dtype)

    row = pl.BlockSpec((ROW_BLOCK, d), lambda i: (i, 0))
    wspec = pl.BlockSpec((1, d), lambda i: (0, 0))
    rspec = pl.BlockSpec((ROW_BLOCK, 1), lambda i: (i, 0))
    return pl.pallas_call(
        body, name=name, grid=(t // ROW_BLOCK,),
        in_specs=[row, rspec, wspec, row, row], out_specs=[row, wspec],
        out_shape=[jax.ShapeDtypeStruct((t, d), out_dtype), jax.ShapeDtypeStruct((1, d), F32)],
        compiler_params=_params(("arbitrary",)))(h, r, w, dn, dres)


CONV_TB = 512
CONV_CB = 512
HALO = 8


def _silu(x):
    return x * jax.nn.sigmoid(x)


def _conv_pre(xcat, w, rows):
    acc = None
    for j in range(4):
        sh = 3 - j
        xs = xcat if sh == 0 else pltpu.roll(xcat, sh, 0)
        term = xs[HALO:HALO + rows] * w[j:j + 1, :]
        acc = term if acc is None else acc + term
    return acc


def conv_fwd(proj, conv_w, name):
    t = proj.shape[0]
    nb = CONV_TB // HALO

    def body(x_ref, prev_ref, w_ref, o_ref):
        prev = jnp.where(pl.program_id(1) == 0, 0.0, prev_ref[...])
        xcat = jnp.concatenate([prev, x_ref[...]], axis=0)
        o_ref[...] = _silu(_conv_pre(xcat, w_ref[...], CONV_TB))

    return pl.pallas_call(
        body, name=name, grid=(QKV_WIDTH // CONV_CB, t // CONV_TB),
        in_specs=[pl.BlockSpec((CONV_TB, CONV_CB), lambda c, i: (i, c)),
                  pl.BlockSpec((HALO, CONV_CB), lambda c, i: (jnp.maximum(i * nb - 1, 0), c)),
                  pl.BlockSpec((4, CONV_CB), lambda c, i: (0, c))],
        out_specs=pl.BlockSpec((CONV_TB, CONV_CB), lambda c, i: (i, c)),
        out_shape=jax.ShapeDtypeStruct((t, QKV_WIDTH), F32),
        compiler_params=_params(("parallel", "parallel")))(proj, proj, conv_w)


def conv_bwd(proj, dout, conv_w, dproj, name):
    t = proj.shape[0]
    nb = CONV_TB // HALO
    nt = t // CONV_TB
    rows = CONV_TB + HALO

    def body(x_ref, prev_ref, next_ref, d_ref, dnext_ref, w_ref, dproj_in, dx_ref, dw_ref):
        del dproj_in
        i = pl.program_id(1)

        @pl.when(i == 0)
        def _():
            dw_ref[...] = jnp.zeros_like(dw_ref)

        w = w_ref[...]
        prev = jnp.where(i == 0, 0.0, prev_ref[...])
        last = i == nt - 1
        xcat = jnp.concatenate([prev, x_ref[...], next_ref[...]], axis=0)
        pre = _conv_pre(xcat, w, rows)
        dcat = jnp.concatenate([d_ref[...], jnp.where(last, 0.0, dnext_ref[...])], axis=0)
        sg = jax.nn.sigmoid(pre)
        dpre = dcat * (sg * (1.0 + pre * (1.0 - sg)))
        dx = None
        for j in range(4):
            sh = 3 - j
            ds = dpre if sh == 0 else pltpu.roll(dpre, rows - sh, 0)
            term = ds[:CONV_TB] * w[j:j + 1, :]
            dx = term if dx is None else dx + term
        dx_ref[...] = dx.astype(BF16)
        dcur = dpre[:CONV_TB]
        parts = []
        for j in range(4):
            sh = 3 - j
            xs = xcat if sh == 0 else pltpu.roll(xcat, sh, 0)
            parts.append(jnp.sum(dcur * xs[HALO:HALO + CONV_TB], axis=0, keepdims=True))
        dw_ref[...] += jnp.concatenate(parts, axis=0)

    cur = pl.BlockSpec((CONV_TB, CONV_CB), lambda c, i: (i, c))
    halo_prev = pl.BlockSpec((HALO, CONV_CB), lambda c, i: (jnp.maximum(i * nb - 1, 0), c))
    halo_next = pl.BlockSpec((HALO, CONV_CB), lambda c, i: (jnp.minimum((i + 1) * nb, nt * nb - 1), c))
    taps = pl.BlockSpec((4, CONV_CB), lambda c, i: (0, c))
    return pl.pallas_call(
        body, name=name, grid=(QKV_WIDTH // CONV_CB, nt),
        in_specs=[cur, halo_prev, halo_next, cur, halo_next, taps, ANY_SPEC],
        out_specs=[cur, taps],
        out_shape=[jax.ShapeDtypeStruct(dproj.shape, BF16), jax.ShapeDtypeStruct((4, QKV_WIDTH), F32)],
        input_output_aliases={6: 0},
        compiler_params=_params(("parallel", "arbitrary")))(proj, proj, proj, dout, dout, conv_w, dproj)


def _iota2(shape, axis):
    return lax.broadcasted_iota(jnp.int32, shape, axis)


def _softplus(x):
    return jnp.maximum(x, 0.0) + jnp.log(1.0 + jnp.exp(-jnp.abs(x)))


def _head_norm_gate(o, norm_w, gate):
    return o * lax.rsqrt(jnp.mean(o * o, axis=-1, keepdims=True) + NORM_EPS) * norm_w * _silu(gate)


GDN_PREC = ("bf", "bf")
HGRN_PREC = "bf"


def _each(fn, *cols):
    return [fn(*a) for a in zip(*cols)]


@functools.partial(jax.custom_vjp, nondiff_argnums=(2,))
def _known_inverse(low, inv, prec):
    del low, prec
    return inv


def _known_inverse_fwd(low, inv, prec):
    del low
    return inv, inv


def _known_inverse_bwd(prec, inv, ct):
    return -_mm_raw(_mm_raw(inv, ct, TN, prec), inv, NT, prec), jnp.zeros_like(inv)


_known_inverse.defvjp(_known_inverse_fwd, _known_inverse_bwd)


def gdn_chunks(hs, qc, kc, vc, zc, ab, a_log_l, dt_l, norm_w, s, prec=GDN_PREC, inv_known=None):
    p_inv, p_mm = prec
    c = CHUNK
    ri, ci = _iota2((c, c), 0), _iota2((c, c), 1)
    incl, strict, eye = ri >= ci, ri > ci, ri == ci
    lane = _iota2((c, LANES), 1)
    last_row = _iota2((c, 1), 0) == c - 1
    rowsum = lambda x: jnp.sum(x, axis=1, keepdims=True)

    def row(col):
        return jnp.sum(jnp.where(eye, col, 0.0), axis=0, keepdims=True)

    q = _each(lambda x: x * lax.rsqrt(rowsum(x * x) + L2_EPS) * (HEAD_DIM ** -0.5), qc)
    k = _each(lambda x: x * lax.rsqrt(rowsum(x * x) + L2_EPS), kc)
    a_col = [rowsum(jnp.where(lane == h, ab, 0.0)) for h in hs]
    b_col = [rowsum(jnp.where(lane == h + N_HEADS, ab, 0.0)) for h in hs]
    beta = _each(jax.nn.sigmoid, b_col)
    g = _each(lambda a, al, dl: rowsum(jnp.where(lane == 0, -jnp.exp(al) * _softplus(a + dl), 0.0)), a_col, a_log_l, dt_l)
    gcum = _each(lambda x: rowsum(jnp.where(incl, row(x), 0.0)), g)
    g_last = _each(lambda x: jnp.sum(jnp.where(last_row, x, 0.0), axis=0, keepdims=True), gcum)
    decay = _each(lambda x: jnp.exp(jnp.where(incl, x - row(x), -jnp.inf)), gcum)
    kk = _each(lambda x: mm(x, x, NT, p_mm), k)
    low = _each(lambda b, x, d: jnp.where(strict, b * x * d, 0.0), beta, kk, decay)
    if inv_known is None:
        power = _each(lambda x: -x, low)
        inv = _each(lambda x: jnp.where(eye, 1.0, 0.0) + x, power)
        for _ in range(5):
            power = _each(lambda x: mm(x, x, NN, p_inv), power)
            inv = _each(lambda x, p: x + mm(x, p, NN, p_inv), inv, power)
    else:
        inv = _each(lambda x, known: _known_inverse(x, known, p_inv), low, inv_known)
    exp_g = _each(jnp.exp, gcum)
    u_v = _each(lambda i, b, x: mm(i, b * x, NN, p_mm), inv, beta, vc)
    w = _each(lambda i, b, e, x: mm(i, b * e * x, NN, p_mm), inv, beta, exp_g, k)
    attn = _each(lambda x, y, d: mm(x, y, NT, p_mm) * d, q, k, decay)
    u = _each(lambda x, y, z: x - mm(y, z, NN, p_mm), u_v, w, s)
    o = _each(lambda x, e, z: mm(x * e, z, NN, p_mm), q, exp_g, s)
    o = _each(lambda x, a, y: x + mm(a, y, NN, p_mm), o, attn, u)
    k_end = _each(lambda x, gl, gc: x * jnp.exp(gl - gc), k, g_last, gcum)
    s_new = _each(lambda z, gl, x, y: z * jnp.exp(gl) + mm(x, y, TN, p_mm), s, g_last, k_end, u)
    return (_each(lambda x, z: _head_norm_gate(x, norm_w, z), o, zc), s_new), inv


def gdn_chunk(h, qc, kc, vc, zc, ab, a_log_l, dt_l, norm_w, s, prec=GDN_PREC, reuse_inverse=False):
    args = ([h], [qc], [kc], [vc], [zc], ab, [a_log_l], [dt_l], norm_w, [s], prec)
    if reuse_inverse:
        inv = lax.stop_gradient(gdn_chunks(*args)[1])
        (y, s_new), _ = gdn_chunks(*args, inv_known=inv)
    else:
        (y, s_new), _ = gdn_chunks(*args)
    return y[0], s_new[0]


DIAG_ROWS = SUB_CHUNK // 2
SHIFT_PAD = 8
SHIFT_ROWS = SHIFT_PAD + CHUNK + SHIFT_PAD
SHIFT_WAYS = 4


class RolledRows:
    def down(self, x, which):
        del which
        return [x] + [pltpu.roll(x, off, 0) for off in range(1, DIAG_ROWS)]

    def up_sum(self, parts, which):
        del which
        acc = parts[0]
        for off in range(1, DIAG_ROWS):
            acc = acc + pltpu.roll(parts[off], CHUNK - off, 0)
        return acc


class SlotRows:
    def __init__(self, slots):
        self.slots = slots

    def down(self, x, which):
        self.slots[which, 0, SHIFT_PAD:SHIFT_PAD + CHUNK, :] = x
        return [x] + [self.slots[which, 0, SHIFT_PAD - off:SHIFT_PAD + CHUNK - off, :] for off in range(1, DIAG_ROWS)]

    def up_sum(self, parts, which):
        acc = parts[0]
        for off in range(1, DIAG_ROWS):
            way = 1 + off % (SHIFT_WAYS - 1)
            self.slots[which, way, SHIFT_PAD:SHIFT_PAD + CHUNK, :] = parts[off]
            acc = acc + self.slots[which, way, SHIFT_PAD + off:SHIFT_PAD + CHUNK + off, :]
        return acc


def _sub_block_rows():
    return jnp.bitwise_and(_iota2((CHUNK, 1), 0), DIAG_ROWS - 1)


def _diag_forward(rows, q, key, bc, v):
    rmod = _sub_block_rows()
    k_d, b_d, v_d = rows.down(key, 0), rows.down(bc, 1), rows.down(v, 2)
    o = None
    for off in range(DIAG_ROWS):
        e = jnp.exp(jnp.where(rmod >= off, bc - b_d[off], -jnp.inf))
        term = jnp.sum(q * k_d[off] * e, axis=-1, keepdims=True) * v_d[off]
        o = term if o is None else o + term
    return o


def _diag_backward(rows, q, key, bc, v, do):
    rmod = _sub_block_rows()
    k_d, b_d, v_d = rows.down(key, 0), rows.down(bc, 1), rows.down(v, 2)
    dq = db = None
    dk_parts, db_parts, dv_parts = [], [], []
    for off in range(DIAG_ROWS):
        e = jnp.exp(jnp.where(rmod >= off, bc - b_d[off], -jnp.inf))
        qe = q * e
        a = jnp.sum(qe * k_d[off], axis=-1, keepdims=True)
        da = jnp.sum(do * v_d[off], axis=-1, keepdims=True)
        dv_parts.append(a * do)
        dq_term = (da * e) * k_d[off]
        dk_term = da * qe
        s = dk_term * k_d[off]
        dq = dq_term if dq is None else dq + dq_term
        db = s if db is None else db + s
        dk_parts.append(dk_term)
        db_parts.append(s)
    return dq, rows.up_sum(dk_parts, 0), db - rows.up_sum(db_parts, 1), rows.up_sum(dv_parts, 2)


def diag_part(rows, differentiable=True):
    forward = functools.partial(_diag_forward, rows)
    if not differentiable:
        return forward
    part = jax.custom_vjp(forward)
    part.defvjp(lambda q, key, bc, v: (forward(q, key, bc, v), (q, key, bc, v)),
                lambda res, do: _diag_backward(rows, *res, do))
    return part


def hgrn_chunks(qb, fb, ib, gb, l0, l1, norm_w, st, prec=HGRN_PREC, diags=None, o_known=None):
    c = CHUNK
    ri, ci = _iota2((4 * c, c), 0), _iota2((4 * c, c), 1)
    rcol = _iota2((c, 1), 0)
    blk0 = jnp.bitwise_and(ri, c - SUB_CHUNK)
    limit = jnp.where(ri < c, ri + 1, jnp.where(ri < 2 * c, blk0, jnp.where(ri < 3 * c, blk0 + SUB_CHUNK,
                                                                          blk0 + DIAG_ROWS)))
    sel = jnp.where(ci < limit, 1.0, 0.0)
    ri, ci = _iota2((c, c), 0), _iota2((c, c), 1)
    lb = _each(lambda a, b: jax.nn.sigmoid(a - b), l0, l1)
    forget = _each(lambda b, f: b + (1.0 - b) * jax.nn.sigmoid(f), lb, fb)
    key = _each(lambda b, f: (1.0 - b) * jax.nn.sigmoid(-f), lb, fb)
    q = _each(_silu, qb)
    v = ib
    logf = _each(jnp.log, forget)
    sums = _each(lambda x: sel_sums(sel, x), logf)
    bc, b_start, b_end, b_half = ([x[i] for x in sums] for i in range(4))
    b_last = _each(lambda x: jnp.sum(x, axis=0, keepdims=True), logf)
    o = _each(lambda x, b, z: mm(x * jnp.exp(b), z, NT, prec), q, bc, st)
    if diags is None:
        diags = [diag_part(RolledRows())] * len(qb)
    o = _each(lambda acc, part, x, ky, b, val: acc + part(x, ky, b, val), o, diags, q, key, bc, v)
    second = jnp.bitwise_and(rcol, SUB_CHUNK - 1) >= DIAG_ROWS
    same_sub = jnp.bitwise_and(ri, c - SUB_CHUNK) == jnp.bitwise_and(ci, c - SUB_CHUNK)
    q_half = _each(lambda x, b, bh: x * jnp.exp(jnp.where(second, b - bh, -jnp.inf)), q, bc, b_half)
    k_half = _each(lambda x, b, bh: x * jnp.exp(jnp.where(second, -jnp.inf, bh - b)), key, bc, b_half)
    a_half = _each(lambda x, z: jnp.where(same_sub, mm(x, z, NT, prec), 0.0), q_half, k_half)
    o = _each(lambda acc, a, val: acc + mm(a, val, NN, prec), o, a_half, v)
    q_rel = _each(lambda x, b, bs: x * jnp.exp(b - bs), q, bc, b_start)
    k_rel = _each(lambda x, b, be: x * jnp.exp(be - b), key, bc, b_end)
    for y in range(c // SUB_CHUNK - 1):
        def scaled(x, b, bs):
            end_y = jnp.sum(jnp.where(rcol == SUB_CHUNK * y + SUB_CHUNK - 1, b, 0.0), axis=0, keepdims=True)
            return x * jnp.exp(jnp.where(rcol >= SUB_CHUNK * (y + 1), bs - end_y, -jnp.inf))
        dq = _each(scaled, q_rel, bc, b_start)
        in_y = (ci >= SUB_CHUNK * y) & (ci < SUB_CHUNK * (y + 1))
        a_y = _each(lambda x, z: jnp.where(in_y, mm(x, z, NT, prec), 0.0), dq, k_rel)
        o = _each(lambda acc, a, val: acc + mm(a, val, NN, prec), o, a_y, v)
    k_state = _each(lambda x, bl, b: x * jnp.exp(bl - b), key, b_last, bc)
    st_new = _each(lambda z, bl, val, x: z * jnp.exp(bl) + mm(val, x, TN, prec), st, b_last, v, k_state)
    if o_known is not None:
        o = _each(_known_value, o, o_known)
    return (_each(lambda x, z: _head_norm_gate(x, norm_w, z), o, gb), st_new), o


def hgrn_chunk(qb, fb, ib, gb, l0, l1, norm_w, st, prec=HGRN_PREC, reuse_output=False):
    args = ([qb], [fb], [ib], [gb], [l0], [l1], norm_w, [st], prec)
    if reuse_output:
        known = lax.stop_gradient(hgrn_chunks(*args)[1])
        (y, st_new), _ = hgrn_chunks(*args, o_known=known)
    else:
        (y, st_new), _ = hgrn_chunks(*args)
    return y[0], st_new[0]


HEAD_VEC = (N_HEADS, 1, LANES)


class _Groups:
    def __init__(self, nc, hb, rev):
        self.nc, self.hb, self.ng, self.rev = nc, hb, N_HEADS // hb, rev

    def _c(self, c):
        return self.nc - 1 - c if self.rev else c

    def cols(self, slab):
        return pl.BlockSpec((CHUNK, self.hb * LANES), lambda c, g: (self._c(c), slab * self.ng + g))

    def tile(self, block):
        return pl.BlockSpec((CHUNK, LANES), lambda c, g: (self._c(c), block))

    def state(self):
        return pl.BlockSpec((None, self.hb, HEAD_DIM, HEAD_DIM), lambda c, g: (self._c(c), g, 0, 0))

    def inverse(self):
        return pl.BlockSpec((None, self.hb, CHUNK, CHUNK), lambda c, g: (self._c(c), g, 0, 0))

    @staticmethod
    def whole(shape):
        return pl.BlockSpec(shape, lambda c, g: (0,) * len(shape))

    def head(self, g, j):
        return j if self.ng == 1 else g * self.hb + j


def _lanes(j):
    return slice(j * LANES, (j + 1) * LANES)


def gdn_fwd(qkv_c, proj, a_log_l, dt_l, norm_w, name, hb=8):
    t = qkv_c.shape[0]
    gr = _Groups(t // CHUNK, hb, rev=False)

    def body(q_ref, k_ref, v_ref, z_ref, ab_ref, al_ref, dt_ref, nw_ref, y_ref, hist_ref, inv_ref, s_ref):
        c, g = pl.program_id(0), pl.program_id(1)

        @pl.when(c == 0)
        def _():
            for j in range(hb):
                s_ref[gr.head(g, j)] = jnp.zeros((HEAD_DIM, HEAD_DIM), F32)

        hs = [gr.head(g, j) for j in range(hb)]
        heads = lambda ref: [ref[:, _lanes(j)] for j in range(hb)]
        s = [s_ref[h] for h in hs]
        for j in range(hb):
            hist_ref[j] = s[j]
        (y, s_new), inv = gdn_chunks(hs, heads(q_ref), heads(k_ref), heads(v_ref), heads(z_ref), ab_ref[...],
                                     [al_ref[h] for h in hs], [dt_ref[h] for h in hs], nw_ref[...], s)
        for j in range(hb):
            y_ref[:, _lanes(j)] = y[j].astype(BF16)
            s_ref[hs[j]] = s_new[j]
            inv_ref[j] = inv[j]

    return pl.pallas_call(
        body, name=name, grid=(gr.nc, gr.ng),
        in_specs=[gr.cols(0), gr.cols(1), gr.cols(2), gr.cols(3), gr.tile(AB_BLOCK),
                  gr.whole(HEAD_VEC), gr.whole(HEAD_VEC), gr.whole((1, LANES))],
        out_specs=[gr.cols(0), gr.state(), gr.inverse()],
        out_shape=[jax.ShapeDtypeStruct((t, 2 * GDN_WIDTH), BF16),
                   jax.ShapeDtypeStruct((gr.nc, N_HEADS, HEAD_DIM, HEAD_DIM), F32),
                   jax.ShapeDtypeStruct((gr.nc, N_HEADS, CHUNK, CHUNK), F32)],
        scratch_shapes=[pltpu.VMEM((N_HEADS, HEAD_DIM, HEAD_DIM), F32)],
        compiler_params=_params(("arbitrary", "arbitrary")),
    )(qkv_c, qkv_c, qkv_c, proj, proj, a_log_l, dt_l, norm_w)


def gdn_bwd(qkv_c, proj, a_log_l, dt_l, norm_w, hist, inv_hist, dy, name):
    t = qkv_c.shape[0]
    hb = N_HEADS
    gr = _Groups(t // CHUNK, hb, rev=True)

    def body(q_ref, k_ref, v_ref, z_ref, ab_ref, al_ref, dt_ref, nw_ref, hist_ref, inv_ref, dy_ref,
             dqkv_ref, dz_ref, dab_ref, dal_ref, ddt_ref, dnw_ref, ds_ref):
        @pl.when(pl.program_id(0) == 0)
        def _():
            dal_ref[...] = jnp.zeros_like(dal_ref)
            ddt_ref[...] = jnp.zeros_like(ddt_ref)
            dnw_ref[...] = jnp.zeros_like(dnw_ref)
            ds_ref[...] = jnp.zeros_like(ds_ref)

        hs = list(range(hb))
        heads = lambda ref: [ref[:, _lanes(j)] for j in hs]
        chunk = functools.partial(gdn_chunks, hs, inv_known=[inv_ref[h] for h in hs])
        _, vjp, _ = jax.vjp(chunk, heads(q_ref), heads(k_ref), heads(v_ref), heads(z_ref),
                            ab_ref[...], [al_ref[h] for h in hs], [dt_ref[h] for h in hs], nw_ref[...],
                            [hist_ref[h] for h in hs], has_aux=True)
        dy = [x.astype(F32) for x in heads(dy_ref)]
        dq, dk, dv, dz, dab, dal, ddt, dnw, ds = vjp((dy, [ds_ref[h] for h in hs]))
        for h in hs:
            dqkv_ref[:, _lanes(h)] = dq[h]
            dqkv_ref[:, _lanes(hb + h)] = dk[h]
            dqkv_ref[:, _lanes(2 * hb + h)] = dv[h]
            dz_ref[:, _lanes(h)] = dz[h].astype(BF16)
            dal_ref[h] += dal[h]
            ddt_ref[h] += ddt[h]
            ds_ref[h] = ds[h]
        dab_ref[...] = dab.astype(BF16)
        dnw_ref[...] += dnw

    return pl.pallas_call(
        body, name=name, grid=(gr.nc, 1),
        in_specs=[gr.cols(0), gr.cols(1), gr.cols(2), gr.cols(3), gr.tile(AB_BLOCK),
                  gr.whole(HEAD_VEC), gr.whole(HEAD_VEC), gr.whole((1, LANES)), gr.state(), gr.inverse(), gr.cols(0)],
        out_specs=[pl.BlockSpec((CHUNK, QKV_WIDTH), lambda c, g: (gr.nc - 1 - c, 0)), gr.cols(3), gr.tile(0),
                   gr.whole(HEAD_VEC), gr.whole(HEAD_VEC), gr.whole((1, LANES))],
        out_shape=[jax.ShapeDtypeStruct((t, QKV_WIDTH), F32), jax.ShapeDtypeStruct((t, CAT_WIDTH), BF16),
                   jax.ShapeDtypeStruct((t, LANES), BF16), jax.ShapeDtypeStruct(HEAD_VEC, F32),
                   jax.ShapeDtypeStruct(HEAD_VEC, F32), jax.ShapeDtypeStruct((1, LANES), F32)],
        scratch_shapes=[pltpu.VMEM((N_HEADS, HEAD_DIM, HEAD_DIM), F32)],
        compiler_params=_params(("arbitrary", "arbitrary")),
    )(qkv_c, qkv_c, qkv_c, proj, proj, a_log_l, dt_l, norm_w, hist, inv_hist, dy)


def hgrn_fwd(proj, l0, l1, norm_w, y, name, hb=8):
    t = proj.shape[0]
    gr = _Groups(t // CHUNK, hb, rev=False)

    def body(q_ref, f_ref, i_ref, g_ref, l0_ref, l1_ref, nw_ref, y_in, y_ref, hist_ref, o_ref, s_ref, shift_ref):
        del y_in
        c, g = pl.program_id(0), pl.program_id(1)

        @pl.when((c == 0) & (g == 0))
        def _():
            shift_ref[...] = jnp.zeros_like(shift_ref)

        @pl.when(c == 0)
        def _():
            for j in range(hb):
                s_ref[gr.head(g, j)] = jnp.zeros((HEAD_DIM, HEAD_DIM), F32)

        hs = [gr.head(g, j) for j in range(hb)]
        heads = lambda ref: [ref[:, _lanes(j)] for j in range(hb)]
        s = [s_ref[h] for h in hs]
        for j in range(hb):
            hist_ref[j] = s[j]
        diags = [diag_part(SlotRows(shift_ref.at[j]), differentiable=False) for j in range(hb)]
        (out, s_new), o_pre = hgrn_chunks(heads(q_ref), heads(f_ref), heads(i_ref), heads(g_ref),
                                          [l0_ref[h] for h in hs], [l1_ref[h] for h in hs], nw_ref[...], s, diags=diags)
        for j in range(hb):
            y_ref[:, _lanes(j)] = out[j].astype(BF16)
            o_ref[:, _lanes(j)] = o_pre[j]
            s_ref[hs[j]] = s_new[j]

    return pl.pallas_call(
        body, name=name, grid=(gr.nc, gr.ng),
        in_specs=[gr.cols(4), gr.cols(5), gr.cols(6), gr.cols(7), gr.whole(HEAD_VEC), gr.whole(HEAD_VEC),
                  gr.whole((1, LANES)), pl.BlockSpec(memory_space=pl.ANY)],
        out_specs=[gr.cols(1), gr.state(), gr.cols(0)],
        out_shape=[jax.ShapeDtypeStruct((t, 2 * GDN_WIDTH), BF16),
                   jax.ShapeDtypeStruct((gr.nc, N_HEADS, HEAD_DIM, HEAD_DIM), F32),
                   jax.ShapeDtypeStruct((t, GDN_WIDTH), F32)],
        scratch_shapes=[pltpu.VMEM((N_HEADS, HEAD_DIM, HEAD_DIM), F32),
                        pltpu.VMEM((hb, 3, SHIFT_WAYS, SHIFT_ROWS, LANES), F32)],
        input_output_aliases={7: 0},
        compiler_params=_params(("arbitrary", "arbitrary")),
    )(proj, proj, proj, proj, l0, l1, norm_w, y)


def hgrn_bwd(proj, l0, l1, norm_w, hist, o_pre, dy, dproj, name):
    t = proj.shape[0]
    hb = N_HEADS
    gr = _Groups(t // CHUNK, hb, rev=True)

    def body(q_ref, f_ref, i_ref, g_ref, l0_ref, l1_ref, nw_ref, hist_ref, o_ref, dy_ref, dproj_in,
             d_ref, dl0_ref, dl1_ref, dnw_ref, ds_ref, shift_ref):
        del dproj_in

        @pl.when(pl.program_id(0) == 0)
        def _():
            dl0_ref[...] = jnp.zeros_like(dl0_ref)
            dl1_ref[...] = jnp.zeros_like(dl1_ref)
            dnw_ref[...] = jnp.zeros_like(dnw_ref)
            ds_ref[...] = jnp.zeros_like(ds_ref)
            shift_ref[...] = jnp.zeros_like(shift_ref)

        hs = list(range(hb))
        heads = lambda ref: [ref[:, _lanes(j)] for j in hs]
        chunk = functools.partial(hgrn_chunks, diags=[diag_part(SlotRows(shift_ref.at[h])) for h in hs],
                                  o_known=heads(o_ref))
        _, vjp, _ = jax.vjp(chunk, heads(q_ref), heads(f_ref), heads(i_ref), heads(g_ref), [l0_ref[h] for h in hs],
                            [l1_ref[h] for h in hs], nw_ref[...], [hist_ref[h] for h in hs], has_aux=True)
        dy = [x.astype(F32) for x in heads(dy_ref)]
        dq, df, di, dg, dl0, dl1, dnw, ds = vjp((dy, [ds_ref[h] for h in hs]))
        for h in hs:
            for slab, val in enumerate((dq, df, di, dg)):
                d_ref[:, _lanes(slab * hb + h)] = val[h].astype(BF16)
            dl0_ref[h] += dl0[h]
            dl1_ref[h] += dl1[h]
            ds_ref[h] = ds[h]
        dnw_ref[...] += dnw

    return pl.pallas_call(
        body, name=name, grid=(gr.nc, 1),
        in_specs=[gr.cols(4), gr.cols(5), gr.cols(6), gr.cols(7), gr.whole(HEAD_VEC), gr.whole(HEAD_VEC),
                  gr.whole((1, LANES)), gr.state(), gr.cols(0), gr.cols(1), ANY_SPEC],
        out_specs=[pl.BlockSpec((CHUNK, 4 * GDN_WIDTH), lambda c, g: (gr.nc - 1 - c, 1)),
                   gr.whole(HEAD_VEC), gr.whole(HEAD_VEC), gr.whole((1, LANES))],
        out_shape=[jax.ShapeDtypeStruct(dproj.shape, BF16), jax.ShapeDtypeStruct(HEAD_VEC, F32),
                   jax.ShapeDtypeStruct(HEAD_VEC, F32), jax.ShapeDtypeStruct((1, LANES), F32)],
        scratch_shapes=[pltpu.VMEM((N_HEADS, HEAD_DIM, HEAD_DIM), F32),
                        pltpu.VMEM((hb, 3, SHIFT_WAYS, SHIFT_ROWS, LANES), F32)],
        input_output_aliases={10: 0},
        compiler_params=_params(("arbitrary", "arbitrary")),
    )(proj, proj, proj, proj, l0, l1, norm_w, hist, o_pre, dy, dproj)


def _adamw(w, g, m, v):
    m = ADAM_B1 * m + (1.0 - ADAM_B1) * g
    v = ADAM_B2 * v + (1.0 - ADAM_B2) * jnp.square(g)
    m_hat = m / (1.0 - ADAM_B1 ** ADAM_STEP)
    v_hat = v / (1.0 - ADAM_B2 ** ADAM_STEP)
    delta = -ADAM_LR * (m_hat / (jnp.sqrt(v_hat) + ADAM_EPS) + ADAM_WD * w)
    return delta, m, v


def adamw_reduce(parts, w, m, v, name, rb=128):
    r, c = w.shape
    rb = min(rb, r)

    def body(p_ref, w_ref, m_ref, v_ref, g_ref, d_ref, mo_ref, vo_ref):
        g = p_ref[0].astype(F32)
        for d in range(1, N_DEV):
            g = g + p_ref[d].astype(F32)
        delta, mn, vn = _adamw(w_ref[...], g, m_ref[...], v_ref[...])
        g_ref[...] = g
        d_ref[...] = delta
        mo_ref[...] = mn
        vo_ref[...] = vn

    blk = pl.BlockSpec((rb, c), lambda i: (i, 0))
    return pl.pallas_call(
        body, name=name, grid=(r // rb,),
        in_specs=[pl.BlockSpec((N_DEV, rb, c), lambda i: (0, i, 0)), blk, blk, blk],
        out_specs=[blk] * 4, out_shape=[jax.ShapeDtypeStruct((r, c), F32)] * 4,
        compiler_params=_params(("parallel",)))(parts, w, m, v)


def adamw_small(w, g, m, v, name):
    def body(w_ref, g_ref, m_ref, v_ref, d_ref, mo_ref, vo_ref):
        delta, mn, vn = _adamw(w_ref[...], g_ref[...], m_ref[...], v_ref[...])
        d_ref[...] = delta
        mo_ref[...] = mn
        vo_ref[...] = vn

    vmem = pl.BlockSpec(memory_space=pltpu.VMEM)
    return pl.pallas_call(body, name=name, in_specs=[vmem] * 4, out_specs=[vmem] * 3,
                          out_shape=[jax.ShapeDtypeStruct(w.shape, F32)] * 3)(w, g, m, v)


def _pack(arrays):
    flat = jnp.concatenate([a.reshape(-1).astype(F32) for a in arrays])
    rows = -(-flat.shape[0] // (8 * LANES)) * 8
    return jnp.pad(flat, (0, rows * LANES - flat.shape[0])).reshape(rows, LANES)


def _unpack(packed, shapes):
    flat, out, off = packed.reshape(-1), [], 0
    for s in shapes:
        n = 1
        for d in s:
            n *= d
        out.append(flat[off:off + n].reshape(s))
        off += n
    return out


def _relu2_epilogue(acc, _):
    r = jnp.maximum(acc, 0.0)
    return acc, r * r


def _relu2_bwd_epilogue(acc, a1):
    return (acc * (2.0 * jnp.maximum(a1, 0.0)),)


def kernel(x, w_in, conv_w, gdn_a_log, gdn_dt_bias, gdn_norm_w, hgrn_lb_logits, hgrn_norm_w, w_out, norm_mix_w, norm_ffn_w, w_ff1, w_ff2, norm_final_w, loss_target, m_w_in, m_conv_w, m_gdn_a_log, m_gdn_dt_bias, m_gdn_norm_w, m_hgrn_lb_logits, m_hgrn_norm_w, m_w_out, m_norm_mix_w, m_norm_ffn_w, m_w_ff1, m_w_ff2, m_norm_final_w, v_w_in, v_conv_w, v_gdn_a_log, v_gdn_dt_bias, v_gdn_norm_w, v_hgrn_lb_logits, v_hgrn_norm_w, v_w_out, v_norm_mix_w, v_norm_ffn_w, v_w_ff1, v_w_ff2, v_norm_final_w):
    me = _my_flat()
    xs = x[0]
    target = loss_target[0]
    shard_in = w_in.shape[2]
    shard_conv = conv_w.shape[2]

    tok = lambda t: t[0:1, 0:1]
    own = lambda src: lax.dynamic_index_in_dim(src, me, 0, keepdims=False)

    g_in, g_conv = gather_two_level([w_in[0].astype(BF16), conv_w[0]], "gather_w_in")
    h_g1, t_g1 = exchange_start([w_out[0].astype(BF16), w_ff1[0].astype(BF16)], True, "gather_mid_start", after=[g_in])
    h_g2, t_g2 = exchange_start([w_ff2[0].astype(BF16)], True, "gather_ff2_start", after=[t_g1])
    w_cat = weights_to_cat(g_in)
    conv_full = jnp.transpose(g_conv, (1, 0, 2)).reshape(4, QKV_WIDTH)

    lane_b = lambda p: jnp.broadcast_to(p.reshape(N_HEADS, 1, 1), HEAD_VEC)
    a_log_l, dt_l = lane_b(gdn_a_log[0]), lane_b(gdn_dt_bias[0])
    l0 = hgrn_lb_logits[0].reshape(HEAD_VEC)
    l1 = hgrn_lb_logits[1].reshape(HEAD_VEC)

    n1, r1 = rms_fwd(xs, norm_mix_w + tok(t_g1) + tok(t_g2), None, "rms_mix")
    proj = matmul(n1, w_cat, "nn", "in_proj", tn=CAT_WIDTH // 5)
    qkv_c = conv_fwd(proj, conv_full, "conv_fwd")
    y_half, hist_a, inv_a = gdn_fwd(qkv_c, proj, a_log_l, dt_l, gdn_norm_w, "gdn_fwd")
    y, hist_b, o_b = hgrn_fwd(proj, l0, l1, hgrn_norm_w, y_half, "hgrn_fwd")
    (s_out, s_ff1), (l_out, l_ff1) = exchange_wait(h_g1, "gather_mid_wait", after=[y])
    w_out_full = _own_slot(l_out, s_out).reshape(D_MODEL, D_MODEL)
    w_ff1_sh = _own_slot(l_ff1, s_ff1)
    mix = matmul(y, w_out_full, "nn", "out_proj")
    h1, n2, r2 = rms_fwd(xs, norm_ffn_w, mix, "rms_ffn")
    a1, act = matmul(n2, w_ff1_sh, "nn", "ff1", out_dtypes=(F32, BF16), epilogue=_relu2_epilogue, b_shards=True)
    (s_ff2,), (l_ff2,) = exchange_wait(h_g2, "gather_ff2_wait", after=[act])
    w_ff2_full = _own_slot(l_ff2, s_ff2).reshape(D_FF, D_MODEL)
    ff = matmul(act, w_ff2_full, "nn", "ff2")
    loss_sum, dh2_b, d_final = loss_head(h1, ff, norm_final_w.reshape(1, D_MODEL), target, "loss_head")

    da1 = matmul(dh2_b, w_ff2_full, "nt", "d_act", out_dtypes=(BF16,), epilogue=_relu2_bwd_epilogue, extra=a1)
    t_all = xs.shape[0]
    dw_ff2 = matmul(act, dh2_b, "tn", "dw_ff2", out_dtypes=(BF16,), tk=t_all)
    p_ff2 = dw_ff2.reshape(N_DEV, D_FF // N_DEV, D_MODEL)
    h_s1, t_s1 = exchange_start([p_ff2], False, "scatter_ff2_start")
    dn2 = matmul(da1, w_ff1_sh, "nt", "d_n2", out_dtypes=(BF16,), after=[t_s1], b_shards=True, k_group=4)
    p_ff1 = matmul(n2, da1, "tn", "dw_ff1", out_dtypes=(BF16,), tn=D_FF // N_DEV, tk=t_all, after=[t_s1], out_shards=True)
    h_s2, t_s2 = exchange_start([p_ff1], False, "scatter_ff1_start")
    dh1_b, d_ffn = rms_bwd(h1, r2, norm_ffn_w + tok(t_s2), dn2, dh2_b, BF16, "rms_ffn_bwd")
    dmix = matmul(dh1_b, w_out_full, "nt", "d_mix", out_dtypes=(BF16,))
    dw_out = matmul(y, dh1_b, "tn", "dw_out", out_dtypes=(BF16,), tk=t_all)
    p_out = dw_out.reshape(N_DEV, D_MODEL // N_DEV, D_MODEL)
    h_s3, t_s3 = exchange_start([p_out], False, "scatter_out_start")
    d_qkv_c, dproj, dab, d_alog_l, d_dt_l, d_gnw = gdn_bwd(
        qkv_c, proj, a_log_l, dt_l, gdn_norm_w + tok(t_s3), hist_a, inv_a, dmix, "gdn_bwd")
    dproj, dl0, dl1, d_hnw = hgrn_bwd(proj, l0, l1, hgrn_norm_w + tok(t_s3), hist_b, o_b, dmix, dproj, "hgrn_bwd")
    dproj, d_conv_full = conv_bwd(proj, d_qkv_c, conv_full, dproj, "conv_bwd")
    dproj = lax.dynamic_update_slice(dproj, dab, (0, MAIN_WIDTH))
    dw_cat = matmul(n1, dproj, "tn", "dw_in", out_dtypes=(BF16,), tm=512, tn=CAT_WIDTH // 5, tk=t_all)
    p_in = cat_to_shards(dw_cat, shard_in)
    h_s4, t_s4 = exchange_start([p_in], False, "scatter_in_start")

    (s_ff2g,), (r_ff2,) = exchange_wait(h_s1, "scatter_ff2_wait", after=[t_s4])
    (s_ff1g,), (r_ff1,) = exchange_wait(h_s2, "scatter_ff1_wait", after=[t_s4])
    (s_outg,), (r_out,) = exchange_wait(h_s3, "scatter_out_wait", after=[t_s4])
    g_w_ff2, d_w_ff2, nm_w_ff2, nv_w_ff2 = adamw_reduce(
        _own_slot(r_ff2, own(s_ff2g)), w_ff2[0], m_w_ff2[0], v_w_ff2[0], "adamw_w_ff2")
    g_w_ff1, d_w_ff1, nm_w_ff1, nv_w_ff1 = adamw_reduce(
        _own_slot(r_ff1, own(s_ff1g)), w_ff1[0], m_w_ff1[0], v_w_ff1[0], "adamw_w_ff1")
    g_w_out, d_w_out, nm_w_out, nv_w_out = adamw_reduce(
        _own_slot(r_out, own(s_outg)), w_out[0], m_w_out[0], v_w_out[0], "adamw_w_out")
    dn1 = matmul(dproj, w_cat, "nt", "d_n1", out_dtypes=(BF16,), tk=CAT_WIDTH // 5, after=[t_s4])
    dx, d_mix = rms_bwd(xs, r1, norm_mix_w, dn1, dh1_b, F32, "rms_mix_bwd")
    (s_ing,), (r_in,) = exchange_wait(h_s4, "scatter_in_wait", after=[dx, d_w_ff2, d_w_ff1, d_w_out])
    g_w_in, d_w_in, nm_w_in, nv_w_in = adamw_reduce(
        _own_slot(r_in, own(s_ing)), w_in[0], m_w_in[0], v_w_in[0], "adamw_w_in")

    d_lb = jnp.stack([dl0.reshape(GDN_WIDTH), dl1.reshape(GDN_WIDTH)])
    small_shapes = [(1, N_HEADS), (1, N_HEADS), (1, HEAD_DIM), (2, GDN_WIDTH), (1, HEAD_DIM), (1, D_MODEL),
                    (1, D_MODEL), (D_MODEL,), (4, QKV_WIDTH)]
    small = _pack([d_alog_l[:, 0, 0], d_dt_l[:, 0, 0], d_gnw, d_lb, d_hnw, d_mix, d_ffn, d_final, d_conv_full])
    red = allreduce_small(small, "allreduce_small")
    g_alog, g_dt, g_gnw, g_lb, g_hnw, g_mix, g_ffn, g_final, g_conv_full = _unpack(red, small_shapes)
    g_conv = lax.dynamic_slice(g_conv_full, (0, me * shard_conv), (4, shard_conv)).reshape(1, 4, shard_conv)
    small_g = [g_alog, g_dt, g_gnw, g_lb, g_hnw, g_mix, g_ffn, g_final, g_conv]
    small_w = [gdn_a_log, gdn_dt_bias, gdn_norm_w, hgrn_lb_logits, hgrn_norm_w, norm_mix_w, norm_ffn_w, norm_final_w, conv_w]
    small_m = [m_gdn_a_log, m_gdn_dt_bias, m_gdn_norm_w, m_hgrn_lb_logits, m_hgrn_norm_w, m_norm_mix_w, m_norm_ffn_w,
               m_norm_final_w, m_conv_w]
    small_v = [v_gdn_a_log, v_gdn_dt_bias, v_gdn_norm_w, v_hgrn_lb_logits, v_hgrn_norm_w, v_norm_mix_w, v_norm_ffn_w,
               v_norm_final_w, v_conv_w]
    shapes = [a.shape for a in small_w]
    d_s, m_s, v_s = adamw_small(_pack(small_w), _pack(small_g), _pack(small_m), _pack(small_v), "adamw_small")
    d_alog, d_dt, d_gn, d_lbl, d_hn, d_nm, d_nf, d_nfin, d_cw = _unpack(d_s, shapes)
    m_alog, m_dt, m_gn, m_lbl, m_hn, m_nm, m_nf, m_nfin, m_cw = _unpack(m_s, shapes)
    v_alog, v_dt, v_gn, v_lbl, v_hn, v_nm, v_nf, v_nfin, v_cw = _unpack(v_s, shapes)

    loss = lax.psum(loss_sum[0, 0], ("x", "y", "c"))
    lead = lambda a: a[None]
    grads = [lead(g_w_in), g_conv, g_alog, g_dt, g_gnw, g_lb, g_hnw, lead(g_w_out), g_mix, g_ffn,
             lead(g_w_ff1), lead(g_w_ff2), g_final]
    deltas = [lead(d_w_in), d_cw, d_alog, d_dt, d_gn, d_lbl, d_hn, lead(d_w_out), d_nm, d_nf,
              lead(d_w_ff1), lead(d_w_ff2), d_nfin]
    new_m = [lead(nm_w_in), m_cw, m_alog, m_dt, m_gn, m_lbl, m_hn, lead(nm_w_out), m_nm, m_nf,
             lead(nm_w_ff1), lead(nm_w_ff2), m_nfin]
    new_v = [lead(nv_w_in), v_cw, v_alog, v_dt, v_gn, v_lbl, v_hn, lead(nv_w_out), v_nm, v_nf,
             lead(nv_w_ff1), lead(nv_w_ff2), v_nfin]
    return (loss, dx[None], *grads, *deltas, *new_m, *new_v)
```

```python
import functools

import jax
import jax.numpy as jnp
from jax import lax
from jax.experimental import pallas as pl
from jax.experimental.pallas import tpu as pltpu

F32 = jnp.float32
BF16 = jnp.bfloat16
HI = lax.Precision.HIGHEST

N_DEV = 8
D_MODEL = 2048
CHUNK = 64
SUB_CHUNK = 16
HEAD_DIM = 128
N_HEADS = 8
GDN_WIDTH = N_HEADS * HEAD_DIM
D_FF = 4 * D_MODEL
QKV_WIDTH = 3 * GDN_WIDTH
MAIN_WIDTH = 8 * GDN_WIDTH
CAT_WIDTH = MAIN_WIDTH + 128
AB_BLOCK = MAIN_WIDTH // 128
NORM_EPS = 1e-6
L2_EPS = 1e-6
LANES = 128
VMEM_LIMIT = 56 * 1024 * 1024

ADAM_LR = 0.001
ADAM_B1 = 0.9
ADAM_B2 = 0.999
ADAM_EPS = 1e-08
ADAM_WD = 0.01
ADAM_STEP = 10

MESH = pl.DeviceIdType.MESH


def _params(sem=None):
    return pltpu.CompilerParams(dimension_semantics=sem, vmem_limit_bytes=VMEM_LIMIT)


def _dot(a, b, dims, prec=None):
    return lax.dot_general(a, b, (dims, ((), ())), precision=prec, preferred_element_type=F32)


NN = ((1,), (0,))
NT = ((1,), (1,))
TN = ((0,), (0,))


def _split_bf16(x, pieces):
    out = []
    for _ in range(pieces - 1):
        p = x.astype(BF16)
        out.append(p)
        x = x - p.astype(F32)
    out.append(x.astype(BF16))
    return out


def _mm_raw(a, b, dims, prec):
    if prec == "hi":
        return _dot(a, b, dims, HI)
    if prec == "bf":
        return _dot(a.astype(BF16), b.astype(BF16), dims)
    a_hi, a_lo = _split_bf16(a, 2)
    b_hi, b_lo = _split_bf16(b, 2)
    return _dot(a_hi, b_hi, dims) + (_dot(a_hi, b_lo, dims) + _dot(a_lo, b_hi, dims))


@functools.partial(jax.custom_vjp, nondiff_argnums=(2, 3))
def mm(a, b, dims, prec):
    return _mm_raw(a, b, dims, prec)


def _mm_fwd(a, b, dims, prec):
    return _mm_raw(a, b, dims, prec), (a, b)


def _mm_bwd(dims, prec, res, ct):
    a, b = res
    if dims == NN:
        return _mm_raw(ct, b, NT, prec), _mm_raw(a, ct, TN, prec)
    if dims == NT:
        return _mm_raw(ct, b, NN, prec), _mm_raw(ct, a, TN, prec)
    return _mm_raw(b, ct, NT, prec), _mm_raw(a, ct, NN, prec)


mm.defvjp(_mm_fwd, _mm_bwd)


def _sel_raw(sel, x, dims):
    sel = sel.astype(BF16)
    p0, p1, p2 = _split_bf16(x, 3)
    return _dot(sel, p0, dims) + (_dot(sel, p1, dims) + _dot(sel, p2, dims))


def _sel_parts(sel, x):
    c = x.shape[0]
    full = _sel_raw(sel, x, NN)
    return tuple(full[i * c:(i + 1) * c] for i in range(sel.shape[0] // c))


@jax.custom_vjp
def sel_sums(sel, x):
    return _sel_parts(sel, x)


def _sel_fwd(sel, x):
    return _sel_parts(sel, x), sel


def _sel_bwd(sel, cts):
    return jnp.zeros_like(sel), _sel_raw(sel, jnp.concatenate(cts, axis=0), TN)


sel_sums.defvjp(_sel_fwd, _sel_bwd)


@jax.custom_vjp
def _known_value(computed, known):
    del computed
    return known


_known_value.defvjp(lambda computed, known: (known, None), lambda _, ct: (ct, jnp.zeros_like(ct)))


def _my_flat():
    return 4 * lax.axis_index("x") + 2 * lax.axis_index("y") + lax.axis_index("c")


def _peer(k):
    x, y, c = lax.axis_index("x"), lax.axis_index("y"), lax.axis_index("c")
    kx, ky, kc = (k >> 2) & 1, (k >> 1) & 1, k & 1
    px = (1 - x) if kx else x
    py = (1 - y) if ky else y
    pc = (1 - c) if kc else c
    return (px, py, pc), 4 * px + 2 * py + pc


def gather_two_level(xs, name):
    n = len(xs)

    def body(*refs):
        x_refs, y_refs = refs[:n], refs[n:2 * n]
        send_sems, recv_sems, local_sems = refs[2 * n:]
        x, y, c = lax.axis_index("x"), lax.axis_index("y"), lax.axis_index("c")
        me, sibling = (x, y, c), (x, y, 1 - c)
        chips = [(1 - x, y), (x, 1 - y), (1 - x, 1 - y)]
        flat = lambda p: 4 * p[0] + 2 * p[1] + p[2]

        def copy(a, k, block, to, src=None):
            return pltpu.make_async_remote_copy(
                src_ref=y_refs[a].at[flat(block)] if src is None else src, dst_ref=y_refs[a].at[flat(block)],
                send_sem=send_sems.at[a, k], recv_sem=recv_sems.at[a, k], device_id=to, device_id_type=MESH)

        mine = [pltpu.make_async_copy(x_refs[a], y_refs[a].at[flat(me)], local_sems.at[a]) for a in range(n)]
        for cp in mine:
            cp.start()
        first = [copy(a, 0, me, sibling, src=x_refs[a]) for a in range(n)]
        first += [copy(a, 1 + j, me, (*chip, c), src=x_refs[a]) for j, chip in enumerate(chips) for a in range(n)]
        for cp in first:
            cp.start()
        passed = []
        for j, chip in enumerate(chips):
            for a in range(n):
                copy(a, 1 + j, (*chip, c), me).wait_recv()
                cp = copy(a, 4 + j, (*chip, c), sibling)
                cp.start()
                passed.append(cp)
        for a in range(n):
            copy(a, 0, sibling, me).wait_recv()
        for j, chip in enumerate(chips):
            for a in range(n):
                copy(a, 4 + j, (*chip, 1 - c), me).wait_recv()
        for cp in first + passed:
            cp.wait_send()
        for cp in mine:
            cp.wait()

    any_spec = pl.BlockSpec(memory_space=pl.ANY)
    return pl.pallas_call(
        body, name=name, out_shape=[jax.ShapeDtypeStruct((N_DEV,) + x.shape, x.dtype) for x in xs],
        in_specs=[any_spec] * n, out_specs=[any_spec] * n,
        scratch_shapes=[pltpu.SemaphoreType.DMA((n, N_DEV - 1)), pltpu.SemaphoreType.DMA((n, N_DEV - 1)),
                        pltpu.SemaphoreType.DMA((n,))],
    )(*xs)


HBM_SPEC = pl.BlockSpec(memory_space=pltpu.HBM)
SEM_SPEC = pl.BlockSpec(memory_space=pltpu.SEMAPHORE)
ANY_SPEC = pl.BlockSpec(memory_space=pl.ANY)
DATAFLOW = pltpu.SideEffectType.DATAFLOW_SIDE_EFFECTING


def _in_hbm(x):
    return pltpu.with_memory_space_constraint(x, pltpu.HBM)


def exchange_start(xs, gather, name, after=()):
    n, n_after = len(xs), len(after)

    def body(*refs):
        x_refs, land_refs = refs[:n], refs[n:2 * n]
        sems = refs[2 * n + n_after:2 * n + n_after + 2 * n]
        token = refs[-1]
        me = _my_flat()
        for k in range(1, N_DEV):
            peer, peer_flat = _peer(k)
            for a in range(n):
                src = x_refs[a] if gather else x_refs[a].at[peer_flat]
                pltpu.make_async_remote_copy(src_ref=src, dst_ref=land_refs[a].at[me], send_sem=sems[a],
                                             recv_sem=sems[n + a], device_id=peer, device_id_type=MESH).start()
        token[...] = jnp.zeros_like(token)

    lands = [_in_hbm(lax.empty(((N_DEV,) + x.shape) if gather else x.shape, x.dtype)) for x in xs]
    hbm_out = [pltpu.HBM(x.shape, x.dtype) for x in xs] + [pltpu.HBM(l.shape, l.dtype) for l in lands]
    res = pl.pallas_call(
        body, name=name,
        out_shape=(*([pltpu.SemaphoreType.DMA(())] * (2 * n)), *hbm_out, jax.ShapeDtypeStruct((8, LANES), F32)),
        in_specs=[HBM_SPEC] * (2 * n) + [ANY_SPEC] * n_after,
        out_specs=(*([SEM_SPEC] * (2 * n)), *([HBM_SPEC] * (2 * n)), pl.BlockSpec(memory_space=pltpu.VMEM)),
        input_output_aliases={i: 2 * n + i for i in range(2 * n)},
        compiler_params=pltpu.CompilerParams(has_side_effects=DATAFLOW),
    )(*[_in_hbm(x) for x in xs], *lands, *after)
    return (list(res[:2 * n]), list(res[2 * n:3 * n]), list(res[3 * n:4 * n])), res[-1]


def exchange_wait(handle, name, after=()):
    sems, xs, lands = handle
    n, n_after = len(xs), len(after)

    def body(*refs):
        land_refs = refs[n:2 * n]
        sem_refs = refs[2 * n:4 * n]
        for a in range(n):
            seven = land_refs[a].at[pl.ds(0, N_DEV - 1)]
            cp = pltpu.make_async_remote_copy(src_ref=seven, dst_ref=seven, send_sem=sem_refs[a],
                                              recv_sem=sem_refs[n + a], device_id=_peer(1)[0], device_id_type=MESH)
            cp.wait_send()
            cp.wait_recv()

    res = pl.pallas_call(
        body, name=name,
        out_shape=[pltpu.HBM(x.shape, x.dtype) for x in xs] + [pltpu.HBM(l.shape, l.dtype) for l in lands],
        in_specs=[HBM_SPEC] * (2 * n) + [SEM_SPEC] * (2 * n) + [ANY_SPEC] * n_after,
        out_specs=[HBM_SPEC] * (2 * n),
        input_output_aliases={i: i for i in range(2 * n)},
        compiler_params=pltpu.CompilerParams(has_side_effects=DATAFLOW),
    )(*xs, *lands, *sems, *after)
    return list(res[:n]), list(res[n:])


def _own_slot(land, block):
    return lax.dynamic_update_slice(land, block[None], (_my_flat(),) + (0,) * block.ndim)


def allreduce_small(x, name):
    rows = x.shape[0]

    def body(x_ref, o_ref, buf, send_sems, recv_sems):
        me = _my_flat()
        buf[me] = x_ref[...]
        sends = []
        for k in range(1, N_DEV):
            peer, _ = _peer(k)
            cp = pltpu.make_async_remote_copy(
                src_ref=x_ref, dst_ref=buf.at[me], send_sem=send_sems.at[k], recv_sem=recv_sems.at[k],
                device_id=peer, device_id_type=MESH)
            cp.start()
            sends.append(cp)
        for k in range(1, N_DEV):
            peer, peer_flat = _peer(k)
            pltpu.make_async_remote_copy(
                src_ref=x_ref, dst_ref=buf.at[peer_flat], send_sem=send_sems.at[k], recv_sem=recv_sems.at[k],
                device_id=peer, device_id_type=MESH).wait_recv()
        for cp in sends:
            cp.wait_send()
        acc = buf[0]
        for d in range(1, N_DEV):
            acc = acc + buf[d]
        o_ref[...] = acc

    vmem = pl.BlockSpec(memory_space=pltpu.VMEM)
    return pl.pallas_call(
        body, name=name, out_shape=jax.ShapeDtypeStruct((rows, LANES), F32),
        in_specs=[vmem], out_specs=vmem,
        scratch_shapes=[pltpu.VMEM((N_DEV, rows, LANES), F32),
                        pltpu.SemaphoreType.DMA((N_DEV,)), pltpu.SemaphoreType.DMA((N_DEV,))],
    )(x)


def matmul(a, b, mode, name, out_dtypes=(F32,), epilogue=None, extra=None, tm=1024, tn=1024, tk=2048, after=(),
           b_shards=False, out_shards=False, k_group=1):
    if b_shards:
        n_sh, b_rows, b_cols = b.shape
    if mode == "nn":
        (m, kd), n = a.shape, (n_sh * b_cols if b_shards else b.shape[1])
        if b_shards:
            tn = b_cols
    elif mode == "nt":
        (m, kd), n = a.shape, (b_rows if b_shards else b.shape[0])
        if b_shards:
            tk = k_group * b_cols
    else:
        (kd, m), n = a.shape, b.shape[1]
    tm, tn, tk = min(tm, m), min(tn, n), min(tk, kd)
    assert m % tm == 0 and n % tn == 0 and kd % tk == 0, (name, m, n, kd, tm, tn, tk)
    ksteps = kd // tk
    dims = {"nn": NN, "nt": NT, "tn": TN}[mode]
    n_out = len(out_dtypes)
    n_in = 2 + (extra is not None) + len(after)

    def finish(acc, e_ref, o_refs):
        outs = (acc,) if epilogue is None else epilogue(acc, e_ref[...] if e_ref is not None else None)
        for o_ref, o in zip(o_refs, outs):
            o_ref[...] = o.astype(o_ref.dtype)

    def product(a_ref, b_ref):
        if mode == "nt" and b_shards:
            w = b_cols
            parts = [_dot(a_ref[:, s * w:(s + 1) * w], b_ref[s], dims) for s in range(k_group)]
            return functools.reduce(lambda p, q: p + q, parts)
        return _dot(a_ref[...], b_ref[...], dims)

    def body(*refs):
        a_ref, b_ref = refs[0], refs[1]
        e_ref = refs[2] if extra is not None else None
        o_refs = refs[n_in:n_in + n_out]
        if ksteps == 1:
            finish(product(a_ref, b_ref), e_ref, o_refs)
            return
        acc_ref = refs[-1]
        kk = pl.program_id(2)

        @pl.when(kk == 0)
        def _():
            acc_ref[...] = jnp.zeros_like(acc_ref)

        acc_ref[...] += product(a_ref, b_ref)

        @pl.when(kk == ksteps - 1)
        def _():
            finish(acc_ref[...], e_ref, o_refs)

    if mode == "nn":
        a_spec = pl.BlockSpec((tm, tk), lambda i, j, k: (i, k))
        b_spec = (pl.BlockSpec((None, tk, tn), lambda i, j, k: (j, k, 0)) if b_shards
                  else pl.BlockSpec((tk, tn), lambda i, j, k: (k, j)))
    elif mode == "nt":
        a_spec = pl.BlockSpec((tm, tk), lambda i, j, k: (i, k))
        b_spec = (pl.BlockSpec((k_group, tn, b_cols), lambda i, j, k: (k, j, 0)) if b_shards
                  else pl.BlockSpec((tn, tk), lambda i, j, k: (j, k)))
    else:
        a_spec = pl.BlockSpec((tk, tm), lambda i, j, k: (k, i))
        b_spec = pl.BlockSpec((tk, tn), lambda i, j, k: (k, j))
    o_spec = pl.BlockSpec((tm, tn), lambda i, j, k: (i, j))
    res_spec = pl.BlockSpec((None, tm, tn), lambda i, j, k: (j, i, 0)) if out_shards else o_spec
    res_shape = (n // tn, m, tn) if out_shards else (m, n)
    in_specs = [a_spec, b_spec] + ([o_spec] if extra is not None else []) + [ANY_SPEC] * len(after)
    args = (a, b) + ((extra,) if extra is not None else ()) + tuple(after)
    res = pl.pallas_call(
        body, name=name, grid=(m // tm, n // tn, ksteps),
        in_specs=in_specs, out_specs=[res_spec] * n_out,
        out_shape=[jax.ShapeDtypeStruct(res_shape, dt) for dt in out_dtypes],
        scratch_shapes=[pltpu.VMEM((tm, tn), F32)] if ksteps > 1 else [],
        compiler_params=_params(("parallel", "parallel", "arbitrary")),
    )(*args)
    return res if n_out > 1 else res[0]


GATE_COL = 4 * GDN_WIDTH
RELAYOUT_ROWS = 256


def _cat_of_win(j):
    if j < GATE_COL:
        return j
    if j < GATE_COL + 2 * N_HEADS:
        return MAIN_WIDTH + (j - GATE_COL)
    return j - 2 * N_HEADS


def _win_of_cat(c):
    if c < GATE_COL:
        return c
    if c < MAIN_WIDTH:
        return c + 2 * N_HEADS
    if c < MAIN_WIDTH + 2 * N_HEADS:
        return GATE_COL + (c - MAIN_WIDTH)
    return None


def _runs(first, count, mapping):
    runs, i = [], 0
    while i < count:
        start, n = mapping(first + i), 1
        while i + n < count and mapping(first + i + n) == start + n:
            n += 1
        runs.append((start, n))
        i += n
    return runs


def weights_to_cat(g_in):
    n_dev, rows, shard = g_in.shape

    def body(x_ref, o_ref):
        for b in range(CAT_WIDTH // LANES):
            live = sum(_win_of_cat(LANES * b + i) is not None for i in range(LANES))
            parts = []
            for start, n in _runs(LANES * b, live, _win_of_cat):
                while n > 0:
                    d, o = divmod(start, shard)
                    take = min(n, shard - o)
                    parts.append(x_ref[d, :, o:o + take])
                    start, n = start + take, n - take
            if live < LANES:
                parts.append(jnp.zeros((RELAYOUT_ROWS, LANES - live), g_in.dtype))
            o_ref[:, LANES * b:LANES * (b + 1)] = parts[0] if len(parts) == 1 else jnp.concatenate(parts, axis=1)

    return pl.pallas_call(
        body, name="weights_to_cat", grid=(rows // RELAYOUT_ROWS,),
        in_specs=[pl.BlockSpec((n_dev, RELAYOUT_ROWS, shard), lambda i: (0, i, 0))],
        out_specs=pl.BlockSpec((RELAYOUT_ROWS, CAT_WIDTH), lambda i: (i, 0)),
        out_shape=jax.ShapeDtypeStruct((rows, CAT_WIDTH), g_in.dtype),
        compiler_params=_params(("parallel",)))(g_in)


def cat_to_shards(dw_cat, shard):
    rows = dw_cat.shape[0]

    def body(x_ref, o_ref):
        for d in range(N_DEV):
            for t0 in range(0, shard, LANES):
                width = min(LANES, shard - t0)
                parts = [x_ref[:, c:c + n] for c, n in _runs(d * shard + t0, width, _cat_of_win)]
                o_ref[d, :, t0:t0 + width] = parts[0] if len(parts) == 1 else jnp.concatenate(parts, axis=1)

    return pl.pallas_call(
        body, name="cat_to_shards", grid=(rows // RELAYOUT_ROWS,),
        in_specs=[pl.BlockSpec((RELAYOUT_ROWS, CAT_WIDTH), lambda i: (i, 0))],
        out_specs=pl.BlockSpec((N_DEV, RELAYOUT_ROWS, shard), lambda i: (0, i, 0)),
        out_shape=jax.ShapeDtypeStruct((N_DEV, rows, shard), dw_cat.dtype),
        compiler_params=_params(("parallel",)))(dw_cat)


ROW_BLOCK = 512


def rms_fwd(x, w, add, name):
    t, d = x.shape
    has_add = add is not None

    def body(*refs):
        x_ref, w_ref = refs[0], refs[1]
        rest = refs[2:]
        if has_add:
            add_ref, h_ref, n_ref, r_ref = rest
            h = x_ref[...] + add_ref[...]
            h_ref[...] = h
        else:
            n_ref, r_ref = rest
            h = x_ref[...]
        r = lax.rsqrt(jnp.mean(h * h, axis=-1, keepdims=True) + NORM_EPS)
        n_ref[...] = (h * r * w_ref[...]).astype(BF16)
        r_ref[...] = r

    row = pl.BlockSpec((ROW_BLOCK, d), lambda i: (i, 0))
    wspec = pl.BlockSpec((1, d), lambda i: (0, 0))
    rspec = pl.BlockSpec((ROW_BLOCK, 1), lambda i: (i, 0))
    in_specs = [row, wspec] + ([row] if has_add else [])
    out_specs = ([row] if has_add else []) + [row, rspec]
    out_shape = ([jax.ShapeDtypeStruct((t, d), F32)] if has_add else []) + [
        jax.ShapeDtypeStruct((t, d), BF16), jax.ShapeDtypeStruct((t, 1), F32)]
    args = (x, w) + ((add,) if has_add else ())
    return pl.pallas_call(body, name=name, grid=(t // ROW_BLOCK,), in_specs=in_specs, out_specs=out_specs,
                          out_shape=out_shape, compiler_params=_params(("parallel",)))(*args)


def loss_head(h1, delta, w, target, name):
    t, d = h1.shape

    def body(h_ref, dl_ref, w_ref, t_ref, loss_ref, dhb_ref, dw_ref):
        @pl.when(pl.program_id(0) == 0)
        def _():
            loss_ref[...] = jnp.zeros_like(loss_ref)
            dw_ref[...] = jnp.zeros_like(dw_ref)

        h = h_ref[...] + dl_ref[...]
        wv = w_ref[...]
        r = lax.rsqrt(jnp.mean(h * h, axis=-1, keepdims=True) + NORM_EPS)
        yn = h * r
        e = yn * wv - t_ref[...]
        loss_ref[...] += 0.5 * jnp.sum(jnp.sum(e * e, axis=-1, keepdims=True), axis=0, keepdims=True) / d
        dy = e / d
        dw_ref[...] += jnp.sum(dy * yn, axis=0, keepdims=True)
        dyn = dy * wv
        dh = r * (dyn - yn * jnp.mean(dyn * yn, axis=-1, keepdims=True))
        dhb_ref[...] = dh.astype(BF16)

    row = pl.BlockSpec((ROW_BLOCK, d), lambda i: (i, 0))
    wspec = pl.BlockSpec((1, d), lambda i: (0, 0))
    one = pl.BlockSpec((1, 1), lambda i: (0, 0))
    return pl.pallas_call(
        body, name=name, grid=(t // ROW_BLOCK,),
        in_specs=[row, row, wspec, row], out_specs=[one, row, wspec],
        out_shape=[jax.ShapeDtypeStruct((1, 1), F32), jax.ShapeDtypeStruct((t, d), BF16),
                   jax.ShapeDtypeStruct((1, d), F32)],
        compiler_params=_params(("arbitrary",)))(h1, delta, w, target)


def rms_bwd(h, r, w, dn, dres, out_dtype, name):
    t, d = h.shape

    def body(h_ref, r_ref, w_ref, dn_ref, dres_ref, dh_ref, dw_ref):
        @pl.when(pl.program_id(0) == 0)
        def _():
            dw_ref[...] = jnp.zeros_like(dw_ref)

        rv = r_ref[...]
        yn = h_ref[...] * rv
        dnv = dn_ref[...].astype(F32)
        dw_ref[...] += jnp.sum(dnv * yn, axis=0, keepdims=True)
        dyn = dnv * w_ref[...]
        dh = dres_ref[...].astype(F32) + rv * (dyn - yn * jnp.mean(dyn * yn, axis=-1, keepdims=True))
        dh_ref[...] = dh.astype(out_dtype)

    row = pl.BlockSpec((ROW_BLOCK, d), lambda i: (i, 0))
    wspec = pl.BlockSpec((1, d), lambda i: (0, 0))
    rspec = pl.BlockSpec((ROW_BLOCK, 1), lambda i: (i, 0))
    return pl.pallas_call(
        body, name=name, grid=(t // ROW_BLOCK,),
        in_specs=[row, rspec, wspec, row, row], out_specs=[row, wspec],
        out_shape=[jax.ShapeDtypeStruct((t, d), out_dtype), jax.ShapeDtypeStruct((1, d), F32)],
        compiler_params=_params(("arbitrary",)))(h, r, w, dn, dres)


CONV_TB = 512
CONV_CB = 512
HALO = 8


def _silu(x):
    return x * jax.nn.sigmoid(x)


def _conv_pre(xcat, w, rows):
    acc = None
    for j in range(4):
        sh = 3 - j
        xs = xcat if sh == 0 else pltpu.roll(xcat, sh, 0)
        term = xs[HALO:HALO + rows] * w[j:j + 1, :]
        acc = term if acc is None else acc + term
    return acc


def conv_fwd(proj, conv_w, name):
    t = proj.shape[0]
    nb = CONV_TB // HALO

    def body(x_ref, prev_ref, w_ref, o_ref):
        prev = jnp.where(pl.program_id(1) == 0, 0.0, prev_ref[...])
        xcat = jnp.concatenate([prev, x_ref[...]], axis=0)
        o_ref[...] = _silu(_conv_pre(xcat, w_ref[...], CONV_TB))

    return pl.pallas_call(
        body, name=name, grid=(QKV_WIDTH // CONV_CB, t // CONV_TB),
        in_specs=[pl.BlockSpec((CONV_TB, CONV_CB), lambda c, i: (i, c)),
                  pl.BlockSpec((HALO, CONV_CB), lambda c, i: (jnp.maximum(i * nb - 1, 0), c)),
                  pl.BlockSpec((4, CONV_CB), lambda c, i: (0, c))],
        out_specs=pl.BlockSpec((CONV_TB, CONV_CB), lambda c, i: (i, c)),
        out_shape=jax.ShapeDtypeStruct((t, QKV_WIDTH), F32),
        compiler_params=_params(("parallel", "parallel")))(proj, proj, conv_w)


def conv_bwd(proj, dout, conv_w, dproj, name):
    t = proj.shape[0]
    nb = CONV_TB // HALO
    nt = t // CONV_TB
    rows = CONV_TB + HALO

    def body(x_ref, prev_ref, next_ref, d_ref, dnext_ref, w_ref, dproj_in, dx_ref, dw_ref):
        del dproj_in
        i = pl.program_id(1)

        @pl.when(i == 0)
        def _():
            dw_ref[...] = jnp.zeros_like(dw_ref)

        w = w_ref[...]
        prev = jnp.where(i == 0, 0.0, prev_ref[...])
        last = i == nt - 1
        xcat = jnp.concatenate([prev, x_ref[...], next_ref[...]], axis=0)
        pre = _conv_pre(xcat, w, rows)
        dcat = jnp.concatenate([d_ref[...], jnp.where(last, 0.0, dnext_ref[...])], axis=0)
        sg = jax.nn.sigmoid(pre)
        dpre = dcat * (sg * (1.0 + pre * (1.0 - sg)))
        dx = None
        for j in range(4):
            sh = 3 - j
            ds = dpre if sh == 0 else pltpu.roll(dpre, rows - sh, 0)
            term = ds[:CONV_TB] * w[j:j + 1, :]
            dx = term if dx is None else dx + term
        dx_ref[...] = dx.astype(BF16)
        dcur = dpre[:CONV_TB]
        parts = []
        for j in range(4):
            sh = 3 - j
            xs = xcat if sh == 0 else pltpu.roll(xcat, sh, 0)
            parts.append(jnp.sum(dcur * xs[HALO:HALO + CONV_TB], axis=0, keepdims=True))
        dw_ref[...] += jnp.concatenate(parts, axis=0)

    cur = pl.BlockSpec((CONV_TB, CONV_CB), lambda c, i: (i, c))
    halo_prev = pl.BlockSpec((HALO, CONV_CB), lambda c, i: (jnp.maximum(i * nb - 1, 0), c))
    halo_next = pl.BlockSpec((HALO, CONV_CB), lambda c, i: (jnp.minimum((i + 1) * nb, nt * nb - 1), c))
    taps = pl.BlockSpec((4, CONV_CB), lambda c, i: (0, c))
    return pl.pallas_call(
        body, name=name, grid=(QKV_WIDTH // CONV_CB, nt),
        in_specs=[cur, halo_prev, halo_next, cur, halo_next, taps, ANY_SPEC],
        out_specs=[cur, taps],
        out_shape=[jax.ShapeDtypeStruct(dproj.shape, BF16), jax.ShapeDtypeStruct((4, QKV_WIDTH), F32)],
        input_output_aliases={6: 0},
        compiler_params=_params(("parallel", "arbitrary")))(proj, proj, proj, dout, dout, conv_w, dproj)


def _iota2(shape, axis):
    return lax.broadcasted_iota(jnp.int32, shape, axis)


def _softplus(x):
    return jnp.maximum(x, 0.0) + jnp.log(1.0 + jnp.exp(-jnp.abs(x)))


def _head_norm_gate(o, norm_w, gate):
    return o * lax.rsqrt(jnp.mean(o * o, axis=-1, keepdims=True) + NORM_EPS) * norm_w * _silu(gate)


GDN_PREC = ("bf", "bf")
HGRN_PREC = "bf"


def _each(fn, *cols):
    return [fn(*a) for a in zip(*cols)]


@functools.partial(jax.custom_vjp, nondiff_argnums=(2,))
def _known_inverse(low, inv, prec):
    del low, prec
    return inv


def _known_inverse_fwd(low, inv, prec):
    del low
    return inv, inv


def _known_inverse_bwd(prec, inv, ct):
    return -_mm_raw(_mm_raw(inv, ct, TN, prec), inv, NT, prec), jnp.zeros_like(inv)


_known_inverse.defvjp(_known_inverse_fwd, _known_inverse_bwd)


def gdn_stages(hs, qc, kc, vc, zc, ab, a_log_l, dt_l, norm_w, s, prec=GDN_PREC, inv_known=None):
    p_inv, p_mm = prec
    c = CHUNK
    ri, ci = _iota2((c, c), 0), _iota2((c, c), 1)
    incl, strict, eye = ri >= ci, ri > ci, ri == ci
    lane = _iota2((c, LANES), 1)
    last_row = _iota2((c, 1), 0) == c - 1
    rowsum = lambda x: jnp.sum(x, axis=1, keepdims=True)

    def row(col):
        return jnp.sum(jnp.where(eye, col, 0.0), axis=0, keepdims=True)

    q = _each(lambda x: x * lax.rsqrt(rowsum(x * x) + L2_EPS) * (HEAD_DIM ** -0.5), qc)
    k = _each(lambda x: x * lax.rsqrt(rowsum(x * x) + L2_EPS), kc)
    yield
    a_col = [rowsum(jnp.where(lane == h, ab, 0.0)) for h in hs]
    b_col = [rowsum(jnp.where(lane == h + N_HEADS, ab, 0.0)) for h in hs]
    beta = _each(jax.nn.sigmoid, b_col)
    g = _each(lambda a, al, dl: rowsum(jnp.where(lane == 0, -jnp.exp(al) * _softplus(a + dl), 0.0)), a_col, a_log_l, dt_l)
    gcum = _each(lambda x: rowsum(jnp.where(incl, row(x), 0.0)), g)
    g_last = _each(lambda x: jnp.sum(jnp.where(last_row, x, 0.0), axis=0, keepdims=True), gcum)
    decay = _each(lambda x: jnp.exp(jnp.where(incl, x - row(x), -jnp.inf)), gcum)
    yield
    kk = _each(lambda x: mm(x, x, NT, p_mm), k)
    low = _each(lambda b, x, d: jnp.where(strict, b * x * d, 0.0), beta, kk, decay)
    yield
    if inv_known is None:
        power = _each(lambda x: -x, low)
        inv = _each(lambda x: jnp.where(eye, 1.0, 0.0) + x, power)
        for _ in range(5):
            power = _each(lambda x: mm(x, x, NN, p_inv), power)
            yield
            inv = _each(lambda x, p: x + mm(x, p, NN, p_inv), inv, power)
            yield
    else:
        inv = _each(lambda x, known: _known_inverse(x, known, p_inv), low, inv_known)
    exp_g = _each(jnp.exp, gcum)
    yield
    u_v = _each(lambda i, b, x: mm(i, b * x, NN, p_mm), inv, beta, vc)
    w = _each(lambda i, b, e, x: mm(i, b * e * x, NN, p_mm), inv, beta, exp_g, k)
    yield
    attn = _each(lambda x, y, d: mm(x, y, NT, p_mm) * d, q, k, decay)
    yield
    u = _each(lambda x, y, z: x - mm(y, z, NN, p_mm), u_v, w, s)
    yield
    o = _each(lambda x, e, z: mm(x * e, z, NN, p_mm), q, exp_g, s)
    o = _each(lambda x, a, y: x + mm(a, y, NN, p_mm), o, attn, u)
    yield
    k_end = _each(lambda x, gl, gc: x * jnp.exp(gl - gc), k, g_last, gcum)
    s_new = _each(lambda z, gl, x, y: z * jnp.exp(gl) + mm(x, y, TN, p_mm), s, g_last, k_end, u)
    return (_each(lambda x, z: _head_norm_gate(x, norm_w, z), o, zc), s_new), inv


def gdn_chunk(h, qc, kc, vc, zc, ab, a_log_l, dt_l, norm_w, s, prec=GDN_PREC, reuse_inverse=False):
    args = ([h], [qc], [kc], [vc], [zc], ab, [a_log_l], [dt_l], norm_w, [s], prec)
    if reuse_inverse:
        inv = lax.stop_gradient(gdn_chunks(*args)[1])
        (y, s_new), _ = gdn_chunks(*args, inv_known=inv)
    else:
        (y, s_new), _ = gdn_chunks(*args)
    return y[0], s_new[0]


DIAG_ROWS = SUB_CHUNK // 2
SHIFT_PAD = 8
SHIFT_ROWS = SHIFT_PAD + CHUNK + SHIFT_PAD
SHIFT_WAYS = 4


class RolledRows:
    def down(self, x, which):
        del which
        return [x] + [pltpu.roll(x, off, 0) for off in range(1, DIAG_ROWS)]

    def up_sum(self, parts, which):
        del which
        acc = parts[0]
        for off in range(1, DIAG_ROWS):
            acc = acc + pltpu.roll(parts[off], CHUNK - off, 0)
        return acc


class SlotRows:
    def __init__(self, slots):
        self.slots = slots

    def down(self, x, which):
        self.slots[which, 0, SHIFT_PAD:SHIFT_PAD + CHUNK, :] = x
        return [x] + [self.slots[which, 0, SHIFT_PAD - off:SHIFT_PAD + CHUNK - off, :] for off in range(1, DIAG_ROWS)]

    def up_sum(self, parts, which):
        acc = parts[0]
        for off in range(1, DIAG_ROWS):
            way = 1 + off % (SHIFT_WAYS - 1)
            self.slots[which, way, SHIFT_PAD:SHIFT_PAD + CHUNK, :] = parts[off]
            acc = acc + self.slots[which, way, SHIFT_PAD + off:SHIFT_PAD + CHUNK + off, :]
        return acc


def _sub_block_rows():
    return jnp.bitwise_and(_iota2((CHUNK, 1), 0), DIAG_ROWS - 1)


def _diag_forward(rows, q, key, bc, v):
    rmod = _sub_block_rows()
    k_d, b_d, v_d = rows.down(key, 0), rows.down(bc, 1), rows.down(v, 2)
    o = None
    for off in range(DIAG_ROWS):
        e = jnp.exp(jnp.where(rmod >= off, bc - b_d[off], -jnp.inf))
        term = jnp.sum(q * k_d[off] * e, axis=-1, keepdims=True) * v_d[off]
        o = term if o is None else o + term
    return o


def _diag_backward(rows, q, key, bc, v, do):
    rmod = _sub_block_rows()
    k_d, b_d, v_d = rows.down(key, 0), rows.down(bc, 1), rows.down(v, 2)
    dq = db = None
    dk_parts, db_parts, dv_parts = [], [], []
    for off in range(DIAG_ROWS):
        e = jnp.exp(jnp.where(rmod >= off, bc - b_d[off], -jnp.inf))
        qe = q * e
        a = jnp.sum(qe * k_d[off], axis=-1, keepdims=True)
        da = jnp.sum(do * v_d[off], axis=-1, keepdims=True)
        dv_parts.append(a * do)
        dq_term = (da * e) * k_d[off]
        dk_term = da * qe
        s = dk_term * k_d[off]
        dq = dq_term if dq is None else dq + dq_term
        db = s if db is None else db + s
        dk_parts.append(dk_term)
        db_parts.append(s)
    return dq, rows.up_sum(dk_parts, 0), db - rows.up_sum(db_parts, 1), rows.up_sum(dv_parts, 2)


def diag_part(rows, differentiable=True):
    forward = functools.partial(_diag_forward, rows)
    if not differentiable:
        return forward
    part = jax.custom_vjp(forward)
    part.defvjp(lambda q, key, bc, v: (forward(q, key, bc, v), (q, key, bc, v)),
                lambda res, do: _diag_backward(rows, *res, do))
    return part


def hgrn_stages(qb, fb, ib, gb, l0, l1, norm_w, st, prec=HGRN_PREC, diags=None, o_known=None):
    c = CHUNK
    ri, ci = _iota2((4 * c, c), 0), _iota2((4 * c, c), 1)
    rcol = _iota2((c, 1), 0)
    blk0 = jnp.bitwise_and(ri, c - SUB_CHUNK)
    limit = jnp.where(ri < c, ri + 1, jnp.where(ri < 2 * c, blk0, jnp.where(ri < 3 * c, blk0 + SUB_CHUNK,
                                                                          blk0 + DIAG_ROWS)))
    sel = jnp.where(ci < limit, 1.0, 0.0)
    ri, ci = _iota2((c, c), 0), _iota2((c, c), 1)
    lb = _each(lambda a, b: jax.nn.sigmoid(a - b), l0, l1)
    forget = _each(lambda b, f: b + (1.0 - b) * jax.nn.sigmoid(f), lb, fb)
    key = _each(lambda b, f: (1.0 - b) * jax.nn.sigmoid(-f), lb, fb)
    q = _each(_silu, qb)
    v = ib
    logf = _each(jnp.log, forget)
    sums = _each(lambda x: sel_sums(sel, x), logf)
    bc, b_start, b_end, b_half = ([x[i] for x in sums] for i in range(4))
    b_last = _each(lambda x: jnp.sum(x, axis=0, keepdims=True), logf)
    o = _each(lambda x, b, z: mm(x * jnp.exp(b), z, NT, prec), q, bc, st)
    if diags is None:
        diags = [diag_part(RolledRows())] * len(qb)
    yield
    o = list(o)
    for h in range(len(o)):
        o[h] = o[h] + diags[h](q[h], key[h], bc[h], v[h])
        yield
    second = jnp.bitwise_and(rcol, SUB_CHUNK - 1) >= DIAG_ROWS
    same_sub = jnp.bitwise_and(ri, c - SUB_CHUNK) == jnp.bitwise_and(ci, c - SUB_CHUNK)
    q_half = _each(lambda x, b, bh: x * jnp.exp(jnp.where(second, b - bh, -jnp.inf)), q, bc, b_half)
    k_half = _each(lambda x, b, bh: x * jnp.exp(jnp.where(second, -jnp.inf, bh - b)), key, bc, b_half)
    a_half = _each(lambda x, z: jnp.where(same_sub, mm(x, z, NT, prec), 0.0), q_half, k_half)
    o = _each(lambda acc, a, val: acc + mm(a, val, NN, prec), o, a_half, v)
    yield
    q_rel = _each(lambda x, b, bs: x * jnp.exp(b - bs), q, bc, b_start)
    k_rel = _each(lambda x, b, be: x * jnp.exp(be - b), key, bc, b_end)
    for y in range(c // SUB_CHUNK - 1):
        def scaled(x, b, bs):
            end_y = jnp.sum(jnp.where(rcol == SUB_CHUNK * y + SUB_CHUNK - 1, b, 0.0), axis=0, keepdims=True)
            return x * jnp.exp(jnp.where(rcol >= SUB_CHUNK * (y + 1), bs - end_y, -jnp.inf))
        dq = _each(scaled, q_rel, bc, b_start)
        in_y = (ci >= SUB_CHUNK * y) & (ci < SUB_CHUNK * (y + 1))
        a_y = _each(lambda x, z: jnp.where(in_y, mm(x, z, NT, prec), 0.0), dq, k_rel)
        o = _each(lambda acc, a, val: acc + mm(a, val, NN, prec), o, a_y, v)
        yield
    k_state = _each(lambda x, bl, b: x * jnp.exp(bl - b), key, b_last, bc)
    st_new = _each(lambda z, bl, val, x: z * jnp.exp(bl) + mm(val, x, TN, prec), st, b_last, v, k_state)
    if o_known is not None:
        o = _each(_known_value, o, o_known)
    return (_each(lambda x, z: _head_norm_gate(x, norm_w, z), o, gb), st_new), o


def _drain(gen):
    try:
        while True:
            next(gen)
    except StopIteration as done:
        return done.value


def _alternate(gen_a, gen_b):
    out, live = [None, None], [gen_a, gen_b]
    while any(g is not None for g in live):
        for i, g in enumerate(live):
            if g is None:
                continue
            try:
                next(g)
            except StopIteration as done:
                out[i], live[i] = done.value, None
    return out


def gdn_chunks(*args, **kwargs):
    return _drain(gdn_stages(*args, **kwargs))


def hgrn_chunks(*args, **kwargs):
    return _drain(hgrn_stages(*args, **kwargs))


def hgrn_chunk(qb, fb, ib, gb, l0, l1, norm_w, st, prec=HGRN_PREC, reuse_output=False):
    args = ([qb], [fb], [ib], [gb], [l0], [l1], norm_w, [st], prec)
    if reuse_output:
        known = lax.stop_gradient(hgrn_chunks(*args)[1])
        (y, st_new), _ = hgrn_chunks(*args, o_known=known)
    else:
        (y, st_new), _ = hgrn_chunks(*args)
    return y[0], st_new[0]


HEAD_VEC = (N_HEADS, 1, LANES)


class _ChunkSpecs:
    def __init__(self, nc, rev):
        self.nc, self.rev = nc, rev

    def _c(self, c):
        return self.nc - 1 - c if self.rev else c

    def row(self, width, block=0):
        return pl.BlockSpec((CHUNK, width), lambda c: (self._c(c), block))

    def per_head(self, rows):
        return pl.BlockSpec((None, N_HEADS, rows, rows), lambda c: (self._c(c), 0, 0, 0))

    @staticmethod
    def whole(shape):
        return pl.BlockSpec(shape, lambda c: (0,) * len(shape))


def _lanes(j):
    return slice(j * LANES, (j + 1) * LANES)


def mixer_fwd(qkv_c, proj, a_log_l, dt_l, gdn_norm_w, l0, l1, hgrn_norm_w, name):
    t = qkv_c.shape[0]
    hb = N_HEADS
    sp = _ChunkSpecs(t // CHUNK, rev=False)
    hs = list(range(hb))

    def body(q_ref, k_ref, v_ref, z_ref, ab_ref, al_ref, dt_ref, gnw_ref, qb_ref, fb_ref, ib_ref, gb_ref, l0_ref, l1_ref,
             hnw_ref, y_ref, hist_a_ref, inv_ref, hist_b_ref, o_ref, sa_ref, sb_ref, shift_ref):
        @pl.when(pl.program_id(0) == 0)
        def _():
            sa_ref[...] = jnp.zeros_like(sa_ref)
            sb_ref[...] = jnp.zeros_like(sb_ref)
            shift_ref[...] = jnp.zeros_like(shift_ref)

        heads = lambda ref: [ref[:, _lanes(j)] for j in hs]
        s_a, s_b = [sa_ref[h] for h in hs], [sb_ref[h] for h in hs]
        for h in hs:
            hist_a_ref[h] = s_a[h]
            hist_b_ref[h] = s_b[h]
        diags = [diag_part(SlotRows(shift_ref.at[h]), differentiable=False) for h in hs]
        ((y_a, s_a_new), inv), ((y_b, s_b_new), o_pre) = _alternate(
            gdn_stages(hs, heads(q_ref), heads(k_ref), heads(v_ref), heads(z_ref), ab_ref[...],
                       [al_ref[h] for h in hs], [dt_ref[h] for h in hs], gnw_ref[...], s_a),
            hgrn_stages(heads(qb_ref), heads(fb_ref), heads(ib_ref), heads(gb_ref),
                        [l0_ref[h] for h in hs], [l1_ref[h] for h in hs], hnw_ref[...], s_b, diags=diags))
        for h in hs:
            y_ref[:, _lanes(h)] = y_a[h].astype(BF16)
            y_ref[:, _lanes(hb + h)] = y_b[h].astype(BF16)
            o_ref[:, _lanes(h)] = o_pre[h]
            sa_ref[h] = s_a_new[h]
            sb_ref[h] = s_b_new[h]
            inv_ref[h] = inv[h]

    vec, gain, slab = sp.whole(HEAD_VEC), sp.whole((1, LANES)), functools.partial(sp.row, GDN_WIDTH)
    states = jax.ShapeDtypeStruct((sp.nc, N_HEADS, HEAD_DIM, HEAD_DIM), F32)
    return pl.pallas_call(
        body, name=name, grid=(sp.nc,),
        in_specs=[slab(0), slab(1), slab(2), slab(3), sp.row(LANES, AB_BLOCK), vec, vec, gain,
                  slab(4), slab(5), slab(6), slab(7), vec, vec, gain],
        out_specs=[sp.row(2 * GDN_WIDTH), sp.per_head(HEAD_DIM), sp.per_head(CHUNK), sp.per_head(HEAD_DIM), slab(0)],
        out_shape=[jax.ShapeDtypeStruct((t, 2 * GDN_WIDTH), BF16), states,
                   jax.ShapeDtypeStruct((sp.nc, N_HEADS, CHUNK, CHUNK), F32), states,
                   jax.ShapeDtypeStruct((t, GDN_WIDTH), F32)],
        scratch_shapes=[pltpu.VMEM((N_HEADS, HEAD_DIM, HEAD_DIM), F32), pltpu.VMEM((N_HEADS, HEAD_DIM, HEAD_DIM), F32),
                        pltpu.VMEM((hb, 3, SHIFT_WAYS, SHIFT_ROWS, LANES), F32)],
        compiler_params=_params(("arbitrary",)),
    )(qkv_c, qkv_c, qkv_c, proj, proj, a_log_l, dt_l, gdn_norm_w, proj, proj, proj, proj, l0, l1, hgrn_norm_w)


def mixer_bwd(qkv_c, proj, a_log_l, dt_l, gdn_norm_w, l0, l1, hgrn_norm_w, hist_a, inv_hist, hist_b, o_pre, dy, name):
    t = qkv_c.shape[0]
    hb = N_HEADS
    sp = _ChunkSpecs(t // CHUNK, rev=True)
    hs = list(range(hb))

    def body(q_ref, k_ref, v_ref, z_ref, ab_ref, al_ref, dt_ref, gnw_ref, qb_ref, fb_ref, ib_ref, gb_ref, l0_ref, l1_ref,
             hnw_ref, hist_a_ref, inv_ref, hist_b_ref, o_ref, dy_ref,
             dqkv_ref, dproj_ref, dal_ref, ddt_ref, dgnw_ref, dl0_ref, dl1_ref, dhnw_ref, dsa_ref, dsb_ref, shift_ref):
        @pl.when(pl.program_id(0) == 0)
        def _():
            for ref in (dal_ref, ddt_ref, dgnw_ref, dl0_ref, dl1_ref, dhnw_ref, dsa_ref, dsb_ref, shift_ref):
                ref[...] = jnp.zeros_like(ref)

        heads = lambda ref, first=0: [ref[:, _lanes(first + j)] for j in hs]
        diags = [diag_part(SlotRows(shift_ref.at[h])) for h in hs]
        inv_known, o_known = [inv_ref[h] for h in hs], heads(o_ref)

        def both(ga, gb):
            (ra, inv), (rb, o_pre) = _alternate(gdn_stages(hs, *ga, inv_known=inv_known),
                                                hgrn_stages(*gb, diags=diags, o_known=o_known))
            return (ra, rb), (inv, o_pre)

        ga = (heads(q_ref), heads(k_ref), heads(v_ref), heads(z_ref), ab_ref[...], [al_ref[h] for h in hs],
              [dt_ref[h] for h in hs], gnw_ref[...], [hist_a_ref[h] for h in hs])
        gb = (heads(qb_ref), heads(fb_ref), heads(ib_ref), heads(gb_ref), [l0_ref[h] for h in hs],
              [l1_ref[h] for h in hs], hnw_ref[...], [hist_b_ref[h] for h in hs])
        _, vjp, _ = jax.vjp(both, ga, gb, has_aux=True)
        dy_a = [x.astype(F32) for x in heads(dy_ref)]
        dy_b = [x.astype(F32) for x in heads(dy_ref, hb)]
        (dq, dk, dv, dz, dab, dal, ddt, dgnw, ds_a), (dqb, dfb, dib, dgb, dl0, dl1, dhnw, ds_b) = vjp(
            ((dy_a, [dsa_ref[h] for h in hs]), (dy_b, [dsb_ref[h] for h in hs])))
        for h in hs:
            dqkv_ref[:, _lanes(h)] = dq[h]
            dqkv_ref[:, _lanes(hb + h)] = dk[h]
            dqkv_ref[:, _lanes(2 * hb + h)] = dv[h]
            for slab, val in enumerate((dz, dqb, dfb, dib, dgb)):
                dproj_ref[:, _lanes((3 + slab) * hb + h)] = val[h].astype(BF16)
            dal_ref[h] += dal[h]
            ddt_ref[h] += ddt[h]
            dl0_ref[h] += dl0[h]
            dl1_ref[h] += dl1[h]
            dsa_ref[h] = ds_a[h]
            dsb_ref[h] = ds_b[h]
        dproj_ref[:, MAIN_WIDTH:] = dab.astype(BF16)
        dgnw_ref[...] += dgnw
        dhnw_ref[...] += dhnw

    vec, gain, slab = sp.whole(HEAD_VEC), sp.whole((1, LANES)), functools.partial(sp.row, GDN_WIDTH)
    vec_shape, gain_shape = jax.ShapeDtypeStruct(HEAD_VEC, F32), jax.ShapeDtypeStruct((1, LANES), F32)
    return pl.pallas_call(
        body, name=name, grid=(sp.nc,),
        in_specs=[slab(0), slab(1), slab(2), slab(3), sp.row(LANES, AB_BLOCK), vec, vec, gain,
                  slab(4), slab(5), slab(6), slab(7), vec, vec, gain,
                  sp.per_head(HEAD_DIM), sp.per_head(CHUNK), sp.per_head(HEAD_DIM), slab(0), sp.row(2 * GDN_WIDTH)],
        out_specs=[sp.row(QKV_WIDTH), sp.row(CAT_WIDTH), vec, vec, gain, vec, vec, gain],
        out_shape=[jax.ShapeDtypeStruct((t, QKV_WIDTH), F32), jax.ShapeDtypeStruct((t, CAT_WIDTH), BF16),
                   vec_shape, vec_shape, gain_shape, vec_shape, vec_shape, gain_shape],
        scratch_shapes=[pltpu.VMEM((N_HEADS, HEAD_DIM, HEAD_DIM), F32), pltpu.VMEM((N_HEADS, HEAD_DIM, HEAD_DIM), F32),
                        pltpu.VMEM((hb, 3, SHIFT_WAYS, SHIFT_ROWS, LANES), F32)],
        compiler_params=_params(("arbitrary",)),
    )(qkv_c, qkv_c, qkv_c, proj, proj, a_log_l, dt_l, gdn_norm_w, proj, proj, proj, proj, l0, l1, hgrn_norm_w,
      hist_a, inv_hist, hist_b, o_pre, dy)


def _adamw(w, g, m, v):
    m = ADAM_B1 * m + (1.0 - ADAM_B1) * g
    v = ADAM_B2 * v + (1.0 - ADAM_B2) * jnp.square(g)
    m_hat = m / (1.0 - ADAM_B1 ** ADAM_STEP)
    v_hat = v / (1.0 - ADAM_B2 ** ADAM_STEP)
    delta = -ADAM_LR * (m_hat / (jnp.sqrt(v_hat) + ADAM_EPS) + ADAM_WD * w)
    return delta, m, v


def adamw_reduce(parts, w, m, v, name, rb=128):
    r, c = w.shape
    rb = min(rb, r)

    def body(p_ref, w_ref, m_ref, v_ref, g_ref, d_ref, mo_ref, vo_ref):
        g = p_ref[0].astype(F32)
        for d in range(1, N_DEV):
            g = g + p_ref[d].astype(F32)
        delta, mn, vn = _adamw(w_ref[...], g, m_ref[...], v_ref[...])
        g_ref[...] = g
        d_ref[...] = delta
        mo_ref[...] = mn
        vo_ref[...] = vn

    blk = pl.BlockSpec((rb, c), lambda i: (i, 0))
    return pl.pallas_call(
        body, name=name, grid=(r // rb,),
        in_specs=[pl.BlockSpec((N_DEV, rb, c), lambda i: (0, i, 0)), blk, blk, blk],
        out_specs=[blk] * 4, out_shape=[jax.ShapeDtypeStruct((r, c), F32)] * 4,
        compiler_params=_params(("parallel",)))(parts, w, m, v)


def adamw_small(w, g, m, v, name):
    def body(w_ref, g_ref, m_ref, v_ref, d_ref, mo_ref, vo_ref):
        delta, mn, vn = _adamw(w_ref[...], g_ref[...], m_ref[...], v_ref[...])
        d_ref[...] = delta
        mo_ref[...] = mn
        vo_ref[...] = vn

    vmem = pl.BlockSpec(memory_space=pltpu.VMEM)
    return pl.pallas_call(body, name=name, in_specs=[vmem] * 4, out_specs=[vmem] * 3,
                          out_shape=[jax.ShapeDtypeStruct(w.shape, F32)] * 3)(w, g, m, v)


def _pack(arrays):
    flat = jnp.concatenate([a.reshape(-1).astype(F32) for a in arrays])
    rows = -(-flat.shape[0] // (8 * LANES)) * 8
    return jnp.pad(flat, (0, rows * LANES - flat.shape[0])).reshape(rows, LANES)


def _unpack(packed, shapes):
    flat, out, off = packed.reshape(-1), [], 0
    for s in shapes:
        n = 1
        for d in s:
            n *= d
        out.append(flat[off:off + n].reshape(s))
        off += n
    return out


def _relu2_epilogue(acc, _):
    r = jnp.maximum(acc, 0.0)
    return acc, r * r


def _relu2_bwd_epilogue(acc, a1):
    return (acc * (2.0 * jnp.maximum(a1, 0.0)),)


def kernel(x, w_in, conv_w, gdn_a_log, gdn_dt_bias, gdn_norm_w, hgrn_lb_logits, hgrn_norm_w, w_out, norm_mix_w, norm_ffn_w, w_ff1, w_ff2, norm_final_w, loss_target, m_w_in, m_conv_w, m_gdn_a_log, m_gdn_dt_bias, m_gdn_norm_w, m_hgrn_lb_logits, m_hgrn_norm_w, m_w_out, m_norm_mix_w, m_norm_ffn_w, m_w_ff1, m_w_ff2, m_norm_final_w, v_w_in, v_conv_w, v_gdn_a_log, v_gdn_dt_bias, v_gdn_norm_w, v_hgrn_lb_logits, v_hgrn_norm_w, v_w_out, v_norm_mix_w, v_norm_ffn_w, v_w_ff1, v_w_ff2, v_norm_final_w):
    me = _my_flat()
    xs = x[0]
    target = loss_target[0]
    shard_in = w_in.shape[2]
    shard_conv = conv_w.shape[2]

    tok = lambda t: t[0:1, 0:1]
    own = lambda src: lax.dynamic_index_in_dim(src, me, 0, keepdims=False)

    g_in, g_conv = gather_two_level([w_in[0].astype(BF16), conv_w[0]], "gather_w_in")
    h_g1, t_g1 = exchange_start([w_out[0].astype(BF16), w_ff1[0].astype(BF16)], True, "gather_mid_start", after=[g_in])
    h_g2, t_g2 = exchange_start([w_ff2[0].astype(BF16)], True, "gather_ff2_start", after=[t_g1])
    w_cat = weights_to_cat(g_in)
    conv_full = jnp.transpose(g_conv, (1, 0, 2)).reshape(4, QKV_WIDTH)

    lane_b = lambda p: jnp.broadcast_to(p.reshape(N_HEADS, 1, 1), HEAD_VEC)
    a_log_l, dt_l = lane_b(gdn_a_log[0]), lane_b(gdn_dt_bias[0])
    l0 = hgrn_lb_logits[0].reshape(HEAD_VEC)
    l1 = hgrn_lb_logits[1].reshape(HEAD_VEC)

    n1, r1 = rms_fwd(xs, norm_mix_w + tok(t_g1) + tok(t_g2), None, "rms_mix")
    proj = matmul(n1, w_cat, "nn", "in_proj", tn=CAT_WIDTH // 5)
    qkv_c = conv_fwd(proj, conv_full, "conv_fwd")
    y, hist_a, inv_a, hist_b, o_b = mixer_fwd(qkv_c, proj, a_log_l, dt_l, gdn_norm_w, l0, l1, hgrn_norm_w, "mixer_fwd")
    (s_out, s_ff1), (l_out, l_ff1) = exchange_wait(h_g1, "gather_mid_wait", after=[y])
    w_out_full = _own_slot(l_out, s_out).reshape(D_MODEL, D_MODEL)
    w_ff1_sh = _own_slot(l_ff1, s_ff1)
    mix = matmul(y, w_out_full, "nn", "out_proj")
    h1, n2, r2 = rms_fwd(xs, norm_ffn_w, mix, "rms_ffn")
    a1, act = matmul(n2, w_ff1_sh, "nn", "ff1", out_dtypes=(F32, BF16), epilogue=_relu2_epilogue, b_shards=True)
    (s_ff2,), (l_ff2,) = exchange_wait(h_g2, "gather_ff2_wait", after=[act])
    w_ff2_full = _own_slot(l_ff2, s_ff2).reshape(D_FF, D_MODEL)
    ff = matmul(act, w_ff2_full, "nn", "ff2")
    loss_sum, dh2_b, d_final = loss_head(h1, ff, norm_final_w.reshape(1, D_MODEL), target, "loss_head")

    da1 = matmul(dh2_b, w_ff2_full, "nt", "d_act", out_dtypes=(BF16,), epilogue=_relu2_bwd_epilogue, extra=a1)
    t_all = xs.shape[0]
    dw_ff2 = matmul(act, dh2_b, "tn", "dw_ff2", out_dtypes=(BF16,), tk=t_all)
    p_ff2 = dw_ff2.reshape(N_DEV, D_FF // N_DEV, D_MODEL)
    h_s1, t_s1 = exchange_start([p_ff2], False, "scatter_ff2_start")
    dn2 = matmul(da1, w_ff1_sh, "nt", "d_n2", out_dtypes=(BF16,), after=[t_s1], b_shards=True, k_group=4)
    p_ff1 = matmul(n2, da1, "tn", "dw_ff1", out_dtypes=(BF16,), tn=D_FF // N_DEV, tk=t_all, after=[t_s1], out_shards=True)
    h_s2, t_s2 = exchange_start([p_ff1], False, "scatter_ff1_start")
    dh1_b, d_ffn = rms_bwd(h1, r2, norm_ffn_w + tok(t_s2), dn2, dh2_b, BF16, "rms_ffn_bwd")
    dmix = matmul(dh1_b, w_out_full, "nt", "d_mix", out_dtypes=(BF16,))
    dw_out = matmul(y, dh1_b, "tn", "dw_out", out_dtypes=(BF16,), tk=t_all)
    p_out = dw_out.reshape(N_DEV, D_MODEL // N_DEV, D_MODEL)
    h_s3, t_s3 = exchange_start([p_out], False, "scatter_out_start")
    d_qkv_c, dproj, d_alog_l, d_dt_l, d_gnw, dl0, dl1, d_hnw = mixer_bwd(
        qkv_c, proj, a_log_l, dt_l, gdn_norm_w + tok(t_s3), l0, l1, hgrn_norm_w, hist_a, inv_a, hist_b, o_b, dmix,
        "mixer_bwd")
    dproj, d_conv_full = conv_bwd(proj, d_qkv_c, conv_full, dproj, "conv_bwd")
    dw_cat = matmul(n1, dproj, "tn", "dw_in", out_dtypes=(BF16,), tm=512, tn=CAT_WIDTH // 5, tk=t_all)
    p_in = cat_to_shards(dw_cat, shard_in)
    h_s4, t_s4 = exchange_start([p_in], False, "scatter_in_start")

    (s_ff2g,), (r_ff2,) = exchange_wait(h_s1, "scatter_ff2_wait", after=[t_s4])
    (s_ff1g,), (r_ff1,) = exchange_wait(h_s2, "scatter_ff1_wait", after=[t_s4])
    (s_outg,), (r_out,) = exchange_wait(h_s3, "scatter_out_wait", after=[t_s4])
    g_w_ff2, d_w_ff2, nm_w_ff2, nv_w_ff2 = adamw_reduce(
        _own_slot(r_ff2, own(s_ff2g)), w_ff2[0], m_w_ff2[0], v_w_ff2[0], "adamw_w_ff2")
    g_w_ff1, d_w_ff1, nm_w_ff1, nv_w_ff1 = adamw_reduce(
        _own_slot(r_ff1, own(s_ff1g)), w_ff1[0], m_w_ff1[0], v_w_ff1[0], "adamw_w_ff1")
    g_w_out, d_w_out, nm_w_out, nv_w_out = adamw_reduce(
        _own_slot(r_out, own(s_outg)), w_out[0], m_w_out[0], v_w_out[0], "adamw_w_out")
    dn1 = matmul(dproj, w_cat, "nt", "d_n1", out_dtypes=(BF16,), tk=CAT_WIDTH // 5, after=[t_s4])
    dx, d_mix = rms_bwd(xs, r1, norm_mix_w, dn1, dh1_b, F32, "rms_mix_bwd")
    (s_ing,), (r_in,) = exchange_wait(h_s4, "scatter_in_wait", after=[dx, d_w_ff2, d_w_ff1, d_w_out])
    g_w_in, d_w_in, nm_w_in, nv_w_in = adamw_reduce(
        _own_slot(r_in, own(s_ing)), w_in[0], m_w_in[0], v_w_in[0], "adamw_w_in")

    d_lb = jnp.stack([dl0.reshape(GDN_WIDTH), dl1.reshape(GDN_WIDTH)])
    small_shapes = [(1, N_HEADS), (1, N_HEADS), (1, HEAD_DIM), (2, GDN_WIDTH), (1, HEAD_DIM), (1, D_MODEL),
                    (1, D_MODEL), (D_MODEL,), (4, QKV_WIDTH)]
    small = _pack([d_alog_l[:, 0, 0], d_dt_l[:, 0, 0], d_gnw, d_lb, d_hnw, d_mix, d_ffn, d_final, d_conv_full])
    red = allreduce_small(small, "allreduce_small")
    g_alog, g_dt, g_gnw, g_lb, g_hnw, g_mix, g_ffn, g_final, g_conv_full = _unpack(red, small_shapes)
    g_conv = lax.dynamic_slice(g_conv_full, (0, me * shard_conv), (4, shard_conv)).reshape(1, 4, shard_conv)
    small_g = [g_alog, g_dt, g_gnw, g_lb, g_hnw, g_mix, g_ffn, g_final, g_conv]
    small_w = [gdn_a_log, gdn_dt_bias, gdn_norm_w, hgrn_lb_logits, hgrn_norm_w, norm_mix_w, norm_ffn_w, norm_final_w, conv_w]
    small_m = [m_gdn_a_log, m_gdn_dt_bias, m_gdn_norm_w, m_hgrn_lb_logits, m_hgrn_norm_w, m_norm_mix_w, m_norm_ffn_w,
               m_norm_final_w, m_conv_w]
    small_v = [v_gdn_a_log, v_gdn_dt_bias, v_gdn_norm_w, v_hgrn_lb_logits, v_hgrn_norm_w, v_norm_mix_w, v_norm_ffn_w,
               v_norm_final_w, v_conv_w]
    shapes = [a.shape for a in small_w]
    d_s, m_s, v_s = adamw_small(_pack(small_w), _pack(small_g), _pack(small_m), _pack(small_v), "adamw_small")
    d_alog, d_dt, d_gn, d_lbl, d_hn, d_nm, d_nf, d_nfin, d_cw = _unpack(d_s, shapes)
    m_alog, m_dt, m_gn, m_lbl, m_hn, m_nm, m_nf, m_nfin, m_cw = _unpack(m_s, shapes)
    v_alog, v_dt, v_gn, v_lbl, v_hn, v_nm, v_nf, v_nfin, v_cw = _unpack(v_s, shapes)

    loss = lax.psum(loss_sum[0, 0], ("x", "y", "c"))
    lead = lambda a: a[None]
    grads = [lead(g_w_in), g_conv, g_alog, g_dt, g_gnw, g_lb, g_hnw, lead(g_w_out), g_mix, g_ffn,
             lead(g_w_ff1), lead(g_w_ff2), g_final]
    deltas = [lead(d_w_in), d_cw, d_alog, d_dt, d_gn, d_lbl, d_hn, lead(d_w_out), d_nm, d_nf,
              lead(d_w_ff1), lead(d_w_ff2), d_nfin]
    new_m = [lead(nm_w_in), m_cw, m_alog, m_dt, m_gn, m_lbl, m_hn, lead(nm_w_out), m_nm, m_nf,
             lead(nm_w_ff1), lead(nm_w_ff2), m_nfin]
    new_v = [lead(nv_w_in), v_cw, v_alog, v_dt, v_gn, v_lbl, v_hn, lead(nv_w_out), v_nm, v_nf,
             lead(nv_w_ff1), lead(nv_w_ff2), v_nfin]
    return (loss, dx[None], *grads, *deltas, *new_m, *new_v)
```

```python
import functools

import jax
import jax.numpy as jnp
from jax import lax
from jax.experimental import pallas as pl
from jax.experimental.pallas import tpu as pltpu

F32 = jnp.float32
BF16 = jnp.bfloat16
HI = lax.Precision.HIGHEST

N_DEV = 8
D_MODEL = 2048
CHUNK = 64
SUB_CHUNK = 16
HEAD_DIM = 128
N_HEADS = 8
GDN_WIDTH = N_HEADS * HEAD_DIM
D_FF = 4 * D_MODEL
QKV_WIDTH = 3 * GDN_WIDTH
MAIN_WIDTH = 8 * GDN_WIDTH
CAT_WIDTH = MAIN_WIDTH + 128
AB_BLOCK = MAIN_WIDTH // 128
NORM_EPS = 1e-6
L2_EPS = 1e-6
LANES = 128
VMEM_LIMIT = 56 * 1024 * 1024

ADAM_LR = 0.001
ADAM_B1 = 0.9
ADAM_B2 = 0.999
ADAM_EPS = 1e-08
ADAM_WD = 0.01
ADAM_STEP = 10

MESH = pl.DeviceIdType.MESH


def _params(sem=None):
    return pltpu.CompilerParams(dimension_semantics=sem, vmem_limit_bytes=VMEM_LIMIT)


def _dot(a, b, dims, prec=None):
    return lax.dot_general(a, b, (dims, ((), ())), precision=prec, preferred_element_type=F32)


NN = ((1,), (0,))
NT = ((1,), (1,))
TN = ((0,), (0,))


def _split_bf16(x, pieces):
    out = []
    for _ in range(pieces - 1):
        p = x.astype(BF16)
        out.append(p)
        x = x - p.astype(F32)
    out.append(x.astype(BF16))
    return out


def _mm_raw(a, b, dims, prec):
    if prec == "hi":
        return _dot(a, b, dims, HI)
    if prec == "bf":
        return _dot(a.astype(BF16), b.astype(BF16), dims)
    a_hi, a_lo = _split_bf16(a, 2)
    b_hi, b_lo = _split_bf16(b, 2)
    return _dot(a_hi, b_hi, dims) + (_dot(a_hi, b_lo, dims) + _dot(a_lo, b_hi, dims))


@functools.partial(jax.custom_vjp, nondiff_argnums=(2, 3))
def mm(a, b, dims, prec):
    return _mm_raw(a, b, dims, prec)


def _mm_fwd(a, b, dims, prec):
    return _mm_raw(a, b, dims, prec), (a, b)


def _mm_bwd(dims, prec, res, ct):
    a, b = res
    if dims == NN:
        return _mm_raw(ct, b, NT, prec), _mm_raw(a, ct, TN, prec)
    if dims == NT:
        return _mm_raw(ct, b, NN, prec), _mm_raw(ct, a, TN, prec)
    return _mm_raw(b, ct, NT, prec), _mm_raw(a, ct, NN, prec)


mm.defvjp(_mm_fwd, _mm_bwd)


def _sel_raw(sel, x, dims):
    sel = sel.astype(BF16)
    p0, p1, p2 = _split_bf16(x, 3)
    return _dot(sel, p0, dims) + (_dot(sel, p1, dims) + _dot(sel, p2, dims))


def _sel_parts(sel, x):
    c = x.shape[0]
    full = _sel_raw(sel, x, NN)
    return tuple(full[i * c:(i + 1) * c] for i in range(sel.shape[0] // c))


@jax.custom_vjp
def sel_sums(sel, x):
    return _sel_parts(sel, x)


def _sel_fwd(sel, x):
    return _sel_parts(sel, x), sel


def _sel_bwd(sel, cts):
    return jnp.zeros_like(sel), _sel_raw(sel, jnp.concatenate(cts, axis=0), TN)


sel_sums.defvjp(_sel_fwd, _sel_bwd)


@jax.custom_vjp
def _known_value(computed, known):
    del computed
    return known


_known_value.defvjp(lambda computed, known: (known, None), lambda _, ct: (ct, jnp.zeros_like(ct)))


def _my_flat():
    return 4 * lax.axis_index("x") + 2 * lax.axis_index("y") + lax.axis_index("c")


def _peer(k):
    x, y, c = lax.axis_index("x"), lax.axis_index("y"), lax.axis_index("c")
    kx, ky, kc = (k >> 2) & 1, (k >> 1) & 1, k & 1
    px = (1 - x) if kx else x
    py = (1 - y) if ky else y
    pc = (1 - c) if kc else c
    return (px, py, pc), 4 * px + 2 * py + pc


def gather_two_level(xs, name):
    n = len(xs)

    def body(*refs):
        x_refs, y_refs = refs[:n], refs[n:2 * n]
        send_sems, recv_sems, local_sems = refs[2 * n:]
        x, y, c = lax.axis_index("x"), lax.axis_index("y"), lax.axis_index("c")
        me, sibling = (x, y, c), (x, y, 1 - c)
        chips = [(1 - x, y), (x, 1 - y), (1 - x, 1 - y)]
        flat = lambda p: 4 * p[0] + 2 * p[1] + p[2]

        def copy(a, k, block, to, src=None):
            return pltpu.make_async_remote_copy(
                src_ref=y_refs[a].at[flat(block)] if src is None else src, dst_ref=y_refs[a].at[flat(block)],
                send_sem=send_sems.at[a, k], recv_sem=recv_sems.at[a, k], device_id=to, device_id_type=MESH)

        mine = [pltpu.make_async_copy(x_refs[a], y_refs[a].at[flat(me)], local_sems.at[a]) for a in range(n)]
        for cp in mine:
            cp.start()
        first = [copy(a, 0, me, sibling, src=x_refs[a]) for a in range(n)]
        first += [copy(a, 1 + j, me, (*chip, c), src=x_refs[a]) for j, chip in enumerate(chips) for a in range(n)]
        for cp in first:
            cp.start()
        passed = []
        for j, chip in enumerate(chips):
            for a in range(n):
                copy(a, 1 + j, (*chip, c), me).wait_recv()
                cp = copy(a, 4 + j, (*chip, c), sibling)
                cp.start()
                passed.append(cp)
        for a in range(n):
            copy(a, 0, sibling, me).wait_recv()
        for j, chip in enumerate(chips):
            for a in range(n):
                copy(a, 4 + j, (*chip, 1 - c), me).wait_recv()
        for cp in first + passed:
            cp.wait_send()
        for cp in mine:
            cp.wait()

    any_spec = pl.BlockSpec(memory_space=pl.ANY)
    return pl.pallas_call(
        body, name=name, out_shape=[jax.ShapeDtypeStruct((N_DEV,) + x.shape, x.dtype) for x in xs],
        in_specs=[any_spec] * n, out_specs=[any_spec] * n,
        scratch_shapes=[pltpu.SemaphoreType.DMA((n, N_DEV - 1)), pltpu.SemaphoreType.DMA((n, N_DEV - 1)),
                        pltpu.SemaphoreType.DMA((n,))],
    )(*xs)


HBM_SPEC = pl.BlockSpec(memory_space=pltpu.HBM)
SEM_SPEC = pl.BlockSpec(memory_space=pltpu.SEMAPHORE)
ANY_SPEC = pl.BlockSpec(memory_space=pl.ANY)
DATAFLOW = pltpu.SideEffectType.DATAFLOW_SIDE_EFFECTING


def _in_hbm(x):
    return pltpu.with_memory_space_constraint(x, pltpu.HBM)


def exchange_start(xs, gather, name, after=()):
    n, n_after = len(xs), len(after)

    def body(*refs):
        x_refs, land_refs = refs[:n], refs[n:2 * n]
        sems = refs[2 * n + n_after:2 * n + n_after + 2 * n]
        token = refs[-1]
        me = _my_flat()
        for k in range(1, N_DEV):
            peer, peer_flat = _peer(k)
            for a in range(n):
                src = x_refs[a] if gather else x_refs[a].at[peer_flat]
                pltpu.make_async_remote_copy(src_ref=src, dst_ref=land_refs[a].at[me], send_sem=sems[a],
                                             recv_sem=sems[n + a], device_id=peer, device_id_type=MESH).start()
        token[...] = jnp.zeros_like(token)

    lands = [_in_hbm(lax.empty(((N_DEV,) + x.shape) if gather else x.shape, x.dtype)) for x in xs]
    hbm_out = [pltpu.HBM(x.shape, x.dtype) for x in xs] + [pltpu.HBM(l.shape, l.dtype) for l in lands]
    res = pl.pallas_call(
        body, name=name,
        out_shape=(*([pltpu.SemaphoreType.DMA(())] * (2 * n)), *hbm_out, jax.ShapeDtypeStruct((8, LANES), F32)),
        in_specs=[HBM_SPEC] * (2 * n) + [ANY_SPEC] * n_after,
        out_specs=(*([SEM_SPEC] * (2 * n)), *([HBM_SPEC] * (2 * n)), pl.BlockSpec(memory_space=pltpu.VMEM)),
        input_output_aliases={i: 2 * n + i for i in range(2 * n)},
        compiler_params=pltpu.CompilerParams(has_side_effects=DATAFLOW),
    )(*[_in_hbm(x) for x in xs], *lands, *after)
    return (list(res[:2 * n]), list(res[2 * n:3 * n]), list(res[3 * n:4 * n])), res[-1]


def exchange_wait(handle, name, after=()):
    sems, xs, lands = handle
    n, n_after = len(xs), len(after)

    def body(*refs):
        land_refs = refs[n:2 * n]
        sem_refs = refs[2 * n:4 * n]
        for a in range(n):
            seven = land_refs[a].at[pl.ds(0, N_DEV - 1)]
            cp = pltpu.make_async_remote_copy(src_ref=seven, dst_ref=seven, send_sem=sem_refs[a],
                                              recv_sem=sem_refs[n + a], device_id=_peer(1)[0], device_id_type=MESH)
            cp.wait_send()
            cp.wait_recv()

    res = pl.pallas_call(
        body, name=name,
        out_shape=[pltpu.HBM(x.shape, x.dtype) for x in xs] + [pltpu.HBM(l.shape, l.dtype) for l in lands],
        in_specs=[HBM_SPEC] * (2 * n) + [SEM_SPEC] * (2 * n) + [ANY_SPEC] * n_after,
        out_specs=[HBM_SPEC] * (2 * n),
        input_output_aliases={i: i for i in range(2 * n)},
        compiler_params=pltpu.CompilerParams(has_side_effects=DATAFLOW),
    )(*xs, *lands, *sems, *after)
    return list(res[:n]), list(res[n:])


def _own_slot(land, block):
    return lax.dynamic_update_slice(land, block[None], (_my_flat(),) + (0,) * block.ndim)


def allreduce_small(x, name):
    rows = x.shape[0]

    def body(x_ref, o_ref, buf, send_sems, recv_sems):
        me = _my_flat()
        buf[me] = x_ref[...]
        sends = []
        for k in range(1, N_DEV):
            peer, _ = _peer(k)
            cp = pltpu.make_async_remote_copy(
                src_ref=x_ref, dst_ref=buf.at[me], send_sem=send_sems.at[k], recv_sem=recv_sems.at[k],
                device_id=peer, device_id_type=MESH)
            cp.start()
            sends.append(cp)
        for k in range(1, N_DEV):
            peer, peer_flat = _peer(k)
            pltpu.make_async_remote_copy(
                src_ref=x_ref, dst_ref=buf.at[peer_flat], send_sem=send_sems.at[k], recv_sem=recv_sems.at[k],
                device_id=peer, device_id_type=MESH).wait_recv()
        for cp in sends:
            cp.wait_send()
        acc = buf[0]
        for d in range(1, N_DEV):
            acc = acc + buf[d]
        o_ref[...] = acc

    vmem = pl.BlockSpec(memory_space=pltpu.VMEM)
    return pl.pallas_call(
        body, name=name, out_shape=jax.ShapeDtypeStruct((rows, LANES), F32),
        in_specs=[vmem], out_specs=vmem,
        scratch_shapes=[pltpu.VMEM((N_DEV, rows, LANES), F32),
                        pltpu.SemaphoreType.DMA((N_DEV,)), pltpu.SemaphoreType.DMA((N_DEV,))],
    )(x)


def matmul(a, b, mode, name, out_dtypes=(F32,), epilogue=None, extra=None, tm=1024, tn=1024, tk=2048, after=(),
           b_shards=False, out_shards=False, k_group=1):
    if b_shards:
        n_sh, b_rows, b_cols = b.shape
    if mode == "nn":
        (m, kd), n = a.shape, (n_sh * b_cols if b_shards else b.shape[1])
        if b_shards:
            tn = b_cols
    elif mode == "nt":
        (m, kd), n = a.shape, (b_rows if b_shards else b.shape[0])
        if b_shards:
            tk = k_group * b_cols
    else:
        (kd, m), n = a.shape, b.shape[1]
    tm, tn, tk = min(tm, m), min(tn, n), min(tk, kd)
    assert m % tm == 0 and n % tn == 0 and kd % tk == 0, (name, m, n, kd, tm, tn, tk)
    ksteps = kd // tk
    dims = {"nn": NN, "nt": NT, "tn": TN}[mode]
    n_out = len(out_dtypes)
    n_in = 2 + (extra is not None) + len(after)

    def finish(acc, e_ref, o_refs):
        outs = (acc,) if epilogue is None else epilogue(acc, e_ref[...] if e_ref is not None else None)
        for o_ref, o in zip(o_refs, outs):
            o_ref[...] = o.astype(o_ref.dtype)

    def product(a_ref, b_ref):
        if mode == "nt" and b_shards:
            w = b_cols
            parts = [_dot(a_ref[:, s * w:(s + 1) * w], b_ref[s], dims) for s in range(k_group)]
            return functools.reduce(lambda p, q: p + q, parts)
        return _dot(a_ref[...], b_ref[...], dims)

    def body(*refs):
        a_ref, b_ref = refs[0], refs[1]
        e_ref = refs[2] if extra is not None else None
        o_refs = refs[n_in:n_in + n_out]
        if ksteps == 1:
            finish(product(a_ref, b_ref), e_ref, o_refs)
            return
        acc_ref = refs[-1]
        kk = pl.program_id(2)

        @pl.when(kk == 0)
        def _():
            acc_ref[...] = jnp.zeros_like(acc_ref)

        acc_ref[...] += product(a_ref, b_ref)

        @pl.when(kk == ksteps - 1)
        def _():
            finish(acc_ref[...], e_ref, o_refs)

    if mode == "nn":
        a_spec = pl.BlockSpec((tm, tk), lambda i, j, k: (i, k))
        b_spec = (pl.BlockSpec((None, tk, tn), lambda i, j, k: (j, k, 0)) if b_shards
                  else pl.BlockSpec((tk, tn), lambda i, j, k: (k, j)))
    elif mode == "nt":
        a_spec = pl.BlockSpec((tm, tk), lambda i, j, k: (i, k))
        b_spec = (pl.BlockSpec((k_group, tn, b_cols), lambda i, j, k: (k, j, 0)) if b_shards
                  else pl.BlockSpec((tn, tk), lambda i, j, k: (j, k)))
    else:
        a_spec = pl.BlockSpec((tk, tm), lambda i, j, k: (k, i))
        b_spec = pl.BlockSpec((tk, tn), lambda i, j, k: (k, j))
    o_spec = pl.BlockSpec((tm, tn), lambda i, j, k: (i, j))
    res_spec = pl.BlockSpec((None, tm, tn), lambda i, j, k: (j, i, 0)) if out_shards else o_spec
    res_shape = (n // tn, m, tn) if out_shards else (m, n)
    in_specs = [a_spec, b_spec] + ([o_spec] if extra is not None else []) + [ANY_SPEC] * len(after)
    args = (a, b) + ((extra,) if extra is not None else ()) + tuple(after)
    res = pl.pallas_call(
        body, name=name, grid=(m // tm, n // tn, ksteps),
        in_specs=in_specs, out_specs=[res_spec] * n_out,
        out_shape=[jax.ShapeDtypeStruct(res_shape, dt) for dt in out_dtypes],
        scratch_shapes=[pltpu.VMEM((tm, tn), F32)] if ksteps > 1 else [],
        compiler_params=_params(("parallel", "parallel", "arbitrary")),
    )(*args)
    return res if n_out > 1 else res[0]


GATE_COL = 4 * GDN_WIDTH
RELAYOUT_ROWS = 256


def _cat_of_win(j):
    if j < GATE_COL:
        return j
    if j < GATE_COL + 2 * N_HEADS:
        return MAIN_WIDTH + (j - GATE_COL)
    return j - 2 * N_HEADS


def _win_of_cat(c):
    if c < GATE_COL:
        return c
    if c < MAIN_WIDTH:
        return c + 2 * N_HEADS
    if c < MAIN_WIDTH + 2 * N_HEADS:
        return GATE_COL + (c - MAIN_WIDTH)
    return None


def _runs(first, count, mapping):
    runs, i = [], 0
    while i < count:
        start, n = mapping(first + i), 1
        while i + n < count and mapping(first + i + n) == start + n:
            n += 1
        runs.append((start, n))
        i += n
    return runs


def weights_to_cat(g_in):
    n_dev, rows, shard = g_in.shape

    def body(x_ref, o_ref):
        for b in range(CAT_WIDTH // LANES):
            live = sum(_win_of_cat(LANES * b + i) is not None for i in range(LANES))
            parts = []
            for start, n in _runs(LANES * b, live, _win_of_cat):
                while n > 0:
                    d, o = divmod(start, shard)
                    take = min(n, shard - o)
                    parts.append(x_ref[d, :, o:o + take])
                    start, n = start + take, n - take
            if live < LANES:
                parts.append(jnp.zeros((RELAYOUT_ROWS, LANES - live), g_in.dtype))
            o_ref[:, LANES * b:LANES * (b + 1)] = parts[0] if len(parts) == 1 else jnp.concatenate(parts, axis=1)

    return pl.pallas_call(
        body, name="weights_to_cat", grid=(rows // RELAYOUT_ROWS,),
        in_specs=[pl.BlockSpec((n_dev, RELAYOUT_ROWS, shard), lambda i: (0, i, 0))],
        out_specs=pl.BlockSpec((RELAYOUT_ROWS, CAT_WIDTH), lambda i: (i, 0)),
        out_shape=jax.ShapeDtypeStruct((rows, CAT_WIDTH), g_in.dtype),
        compiler_params=_params(("parallel",)))(g_in)


def cat_to_shards(dw_cat, shard):
    rows = dw_cat.shape[0]

    def body(x_ref, o_ref):
        for d in range(N_DEV):
            for t0 in range(0, shard, LANES):
                width = min(LANES, shard - t0)
                parts = [x_ref[:, c:c + n] for c, n in _runs(d * shard + t0, width, _cat_of_win)]
                o_ref[d, :, t0:t0 + width] = parts[0] if len(parts) == 1 else jnp.concatenate(parts, axis=1)

    return pl.pallas_call(
        body, name="cat_to_shards", grid=(rows // RELAYOUT_ROWS,),
        in_specs=[pl.BlockSpec((RELAYOUT_ROWS, CAT_WIDTH), lambda i: (i, 0))],
        out_specs=pl.BlockSpec((N_DEV, RELAYOUT_ROWS, shard), lambda i: (0, i, 0)),
        out_shape=jax.ShapeDtypeStruct((N_DEV, rows, shard), dw_cat.dtype),
        compiler_params=_params(("parallel",)))(dw_cat)


ROW_BLOCK = 512


def rms_fwd(x, w, name):
    t, d = x.shape

    def body(x_ref, w_ref, n_ref, r_ref):
        h = x_ref[...]
        r = lax.rsqrt(jnp.mean(h * h, axis=-1, keepdims=True) + NORM_EPS)
        n_ref[...] = (h * r * w_ref[...]).astype(BF16)
        r_ref[...] = r

    row = pl.BlockSpec((ROW_BLOCK, d), lambda i: (i, 0))
    return pl.pallas_call(
        body, name=name, grid=(t // ROW_BLOCK,),
        in_specs=[row, pl.BlockSpec((1, d), lambda i: (0, 0))],
        out_specs=[row, pl.BlockSpec((ROW_BLOCK, 1), lambda i: (i, 0))],
        out_shape=[jax.ShapeDtypeStruct((t, d), BF16), jax.ShapeDtypeStruct((t, 1), F32)],
        compiler_params=_params(("parallel",)))(x, w)


FUSED_ROWS = 512


def out_proj_rms(y, w_out, x, w_norm, name):
    t, d = x.shape

    def body(y_ref, w_ref, x_ref, g_ref, h_ref, n_ref, r_ref):
        h = x_ref[...] + _dot(y_ref[...], w_ref[...], NN)
        r = lax.rsqrt(jnp.mean(h * h, axis=-1, keepdims=True) + NORM_EPS)
        h_ref[...] = h
        n_ref[...] = (h * r * g_ref[...]).astype(BF16)
        r_ref[...] = r

    row = pl.BlockSpec((FUSED_ROWS, d), lambda i: (i, 0))
    return pl.pallas_call(
        body, name=name, grid=(t // FUSED_ROWS,),
        in_specs=[pl.BlockSpec((FUSED_ROWS, y.shape[1]), lambda i: (i, 0)), pl.BlockSpec(w_out.shape, lambda i: (0, 0)),
                  row, pl.BlockSpec((1, d), lambda i: (0, 0))],
        out_specs=[row, row, pl.BlockSpec((FUSED_ROWS, 1), lambda i: (i, 0))],
        out_shape=[jax.ShapeDtypeStruct((t, d), F32), jax.ShapeDtypeStruct((t, d), BF16),
                   jax.ShapeDtypeStruct((t, 1), F32)],
        compiler_params=_params(("parallel",)))(y, w_out, x, w_norm)


def ff2_loss(act, w_ff2, h1, w, target, name, tk=2048):
    t, d = h1.shape
    ksteps = act.shape[1] // tk

    def body(a_ref, b_ref, h_ref, w_ref, t_ref, loss_ref, dhb_ref, dw_ref, acc_ref):
        i, kk = pl.program_id(0), pl.program_id(1)

        @pl.when((i == 0) & (kk == 0))
        def _():
            loss_ref[...] = jnp.zeros_like(loss_ref)
            dw_ref[...] = jnp.zeros_like(dw_ref)

        @pl.when(kk == 0)
        def _():
            acc_ref[...] = h_ref[...]

        acc_ref[...] += _dot(a_ref[...], b_ref[...], NN)

        @pl.when(kk == ksteps - 1)
        def _():
            h = acc_ref[...]
            wv = w_ref[...]
            r = lax.rsqrt(jnp.mean(h * h, axis=-1, keepdims=True) + NORM_EPS)
            yn = h * r
            e = yn * wv - t_ref[...]
            loss_ref[...] += 0.5 * jnp.sum(jnp.sum(e * e, axis=-1, keepdims=True), axis=0, keepdims=True) / d
            dy = e / d
            dw_ref[...] += jnp.sum(dy * yn, axis=0, keepdims=True)
            dyn = dy * wv
            dhb_ref[...] = (r * (dyn - yn * jnp.mean(dyn * yn, axis=-1, keepdims=True))).astype(BF16)

    row = pl.BlockSpec((FUSED_ROWS, d), lambda i, k: (i, 0))
    wspec = pl.BlockSpec((1, d), lambda i, k: (0, 0))
    return pl.pallas_call(
        body, name=name, grid=(t // FUSED_ROWS, ksteps),
        in_specs=[pl.BlockSpec((FUSED_ROWS, tk), lambda i, k: (i, k)), pl.BlockSpec((tk, d), lambda i, k: (k, 0)),
                  row, wspec, row],
        out_specs=[pl.BlockSpec((1, 1), lambda i, k: (0, 0)), row, wspec],
        out_shape=[jax.ShapeDtypeStruct((1, 1), F32), jax.ShapeDtypeStruct((t, d), BF16),
                   jax.ShapeDtypeStruct((1, d), F32)],
        scratch_shapes=[pltpu.VMEM((FUSED_ROWS, d), F32)],
        compiler_params=_params(("arbitrary", "arbitrary")))(act, w_ff2, h1, w, target)


def rms_bwd(h, r, w, dn, dres, out_dtype, name):
    t, d = h.shape

    def body(h_ref, r_ref, w_ref, dn_ref, dres_ref, dh_ref, dw_ref):
        @pl.when(pl.program_id(0) == 0)
        def _():
            dw_ref[...] = jnp.zeros_like(dw_ref)

        rv = r_ref[...]
        yn = h_ref[...] * rv
        dnv = dn_ref[...].astype(F32)
        dw_ref[...] += jnp.sum(dnv * yn, axis=0, keepdims=True)
        dyn = dnv * w_ref[...]
        dh = dres_ref[...].astype(F32) + rv * (dyn - yn * jnp.mean(dyn * yn, axis=-1, keepdims=True))
        dh_ref[...] = dh.astype(out_dtype)

    row = pl.BlockSpec((ROW_BLOCK, d), lambda i: (i, 0))
    wspec = pl.BlockSpec((1, d), lambda i: (0, 0))
    rspec = pl.BlockSpec((ROW_BLOCK, 1), lambda i: (i, 0))
    return pl.pallas_call(
        body, name=name, grid=(t // ROW_BLOCK,),
        in_specs=[row, rspec, wspec, row, row], out_specs=[row, wspec],
        out_shape=[jax.ShapeDtypeStruct((t, d), out_dtype), jax.ShapeDtypeStruct((1, d), F32)],
        compiler_params=_params(("arbitrary",)))(h, r, w, dn, dres)


CONV_TB = 512
CONV_CB = 512
HALO = 8


def _silu(x):
    return x * jax.nn.sigmoid(x)


def _conv_pre(xcat, w, rows):
    acc = None
    for j in range(4):
        sh = 3 - j
        xs = xcat if sh == 0 else pltpu.roll(xcat, sh, 0)
        term = xs[HALO:HALO + rows] * w[j:j + 1, :]
        acc = term if acc is None else acc + term
    return acc


def conv_fwd(proj, conv_w, name):
    t = proj.shape[0]
    nb = CONV_TB // HALO

    def body(x_ref, prev_ref, w_ref, o_ref):
        prev = jnp.where(pl.program_id(1) == 0, 0.0, prev_ref[...])
        xcat = jnp.concatenate([prev, x_ref[...]], axis=0)
        o_ref[...] = _silu(_conv_pre(xcat, w_ref[...], CONV_TB))

    return pl.pallas_call(
        body, name=name, grid=(QKV_WIDTH // CONV_CB, t // CONV_TB),
        in_specs=[pl.BlockSpec((CONV_TB, CONV_CB), lambda c, i: (i, c)),
                  pl.BlockSpec((HALO, CONV_CB), lambda c, i: (jnp.maximum(i * nb - 1, 0), c)),
                  pl.BlockSpec((4, CONV_CB), lambda c, i: (0, c))],
        out_specs=pl.BlockSpec((CONV_TB, CONV_CB), lambda c, i: (i, c)),
        out_shape=jax.ShapeDtypeStruct((t, QKV_WIDTH), F32),
        compiler_params=_params(("parallel", "parallel")))(proj, proj, conv_w)


def conv_bwd(proj, dout, conv_w, dproj, name):
    t = proj.shape[0]
    nb = CONV_TB // HALO
    nt = t // CONV_TB
    rows = CONV_TB + HALO

    def body(x_ref, prev_ref, next_ref, d_ref, dnext_ref, w_ref, dproj_in, dx_ref, dw_ref):
        del dproj_in
        i = pl.program_id(1)

        @pl.when(i == 0)
        def _():
            dw_ref[...] = jnp.zeros_like(dw_ref)

        w = w_ref[...]
        prev = jnp.where(i == 0, 0.0, prev_ref[...])
        last = i == nt - 1
        xcat = jnp.concatenate([prev, x_ref[...], next_ref[...]], axis=0)
        pre = _conv_pre(xcat, w, rows)
        dcat = jnp.concatenate([d_ref[...], jnp.where(last, 0.0, dnext_ref[...])], axis=0)
        sg = jax.nn.sigmoid(pre)
        dpre = dcat * (sg * (1.0 + pre * (1.0 - sg)))
        dx = None
        for j in range(4):
            sh = 3 - j
            ds = dpre if sh == 0 else pltpu.roll(dpre, rows - sh, 0)
            term = ds[:CONV_TB] * w[j:j + 1, :]
            dx = term if dx is None else dx + term
        dx_ref[...] = dx.astype(BF16)
        dcur = dpre[:CONV_TB]
        parts = []
        for j in range(4):
            sh = 3 - j
            xs = xcat if sh == 0 else pltpu.roll(xcat, sh, 0)
            parts.append(jnp.sum(dcur * xs[HALO:HALO + CONV_TB], axis=0, keepdims=True))
        dw_ref[...] += jnp.concatenate(parts, axis=0)

    cur = pl.BlockSpec((CONV_TB, CONV_CB), lambda c, i: (i, c))
    halo_prev = pl.BlockSpec((HALO, CONV_CB), lambda c, i: (jnp.maximum(i * nb - 1, 0), c))
    halo_next = pl.BlockSpec((HALO, CONV_CB), lambda c, i: (jnp.minimum((i + 1) * nb, nt * nb - 1), c))
    taps = pl.BlockSpec((4, CONV_CB), lambda c, i: (0, c))
    return pl.pallas_call(
        body, name=name, grid=(QKV_WIDTH // CONV_CB, nt),
        in_specs=[cur, halo_prev, halo_next, cur, halo_next, taps, ANY_SPEC],
        out_specs=[cur, taps],
        out_shape=[jax.ShapeDtypeStruct(dproj.shape, BF16), jax.ShapeDtypeStruct((4, QKV_WIDTH), F32)],
        input_output_aliases={6: 0},
        compiler_params=_params(("parallel", "arbitrary")))(proj, proj, proj, dout, dout, conv_w, dproj)


def _iota2(shape, axis):
    return lax.broadcasted_iota(jnp.int32, shape, axis)


def _softplus(x):
    return jnp.maximum(x, 0.0) + jnp.log(1.0 + jnp.exp(-jnp.abs(x)))


def _head_norm_gate(o, norm_w, gate):
    return o * lax.rsqrt(jnp.mean(o * o, axis=-1, keepdims=True) + NORM_EPS) * norm_w * _silu(gate)


GDN_PREC = ("bf", "bf")
HGRN_PREC = "bf"


def _each(fn, *cols):
    return [fn(*a) for a in zip(*cols)]


@functools.partial(jax.custom_vjp, nondiff_argnums=(2,))
def _known_inverse(low, inv, prec):
    del low, prec
    return inv


def _known_inverse_fwd(low, inv, prec):
    del low
    return inv, inv


def _known_inverse_bwd(prec, inv, ct):
    return -_mm_raw(_mm_raw(inv, ct, TN, prec), inv, NT, prec), jnp.zeros_like(inv)


_known_inverse.defvjp(_known_inverse_fwd, _known_inverse_bwd)


def gdn_stages(hs, qc, kc, vc, zc, ab, a_log_l, dt_l, norm_w, s, prec=GDN_PREC, inv_known=None):
    p_inv, p_mm = prec
    c = CHUNK
    ri, ci = _iota2((c, c), 0), _iota2((c, c), 1)
    incl, strict, eye = ri >= ci, ri > ci, ri == ci
    lane = _iota2((c, LANES), 1)
    last_row = _iota2((c, 1), 0) == c - 1
    rowsum = lambda x: jnp.sum(x, axis=1, keepdims=True)

    def row(col):
        return jnp.sum(jnp.where(eye, col, 0.0), axis=0, keepdims=True)

    q = _each(lambda x: x * lax.rsqrt(rowsum(x * x) + L2_EPS) * (HEAD_DIM ** -0.5), qc)
    k = _each(lambda x: x * lax.rsqrt(rowsum(x * x) + L2_EPS), kc)
    yield
    a_col = [rowsum(jnp.where(lane == h, ab, 0.0)) for h in hs]
    b_col = [rowsum(jnp.where(lane == h + N_HEADS, ab, 0.0)) for h in hs]
    beta = _each(jax.nn.sigmoid, b_col)
    g = _each(lambda a, al, dl: rowsum(jnp.where(lane == 0, -jnp.exp(al) * _softplus(a + dl), 0.0)), a_col, a_log_l, dt_l)
    gcum = _each(lambda x: rowsum(jnp.where(incl, row(x), 0.0)), g)
    g_last = _each(lambda x: jnp.sum(jnp.where(last_row, x, 0.0), axis=0, keepdims=True), gcum)
    decay = _each(lambda x: jnp.exp(jnp.where(incl, x - row(x), -jnp.inf)), gcum)
    yield
    kk = _each(lambda x: mm(x, x, NT, p_mm), k)
    low = _each(lambda b, x, d: jnp.where(strict, b * x * d, 0.0), beta, kk, decay)
    yield
    if inv_known is None:
        power = _each(lambda x: -x, low)
        inv = _each(lambda x: jnp.where(eye, 1.0, 0.0) + x, power)
        for _ in range(5):
            power = _each(lambda x: mm(x, x, NN, p_inv), power)
            yield
            inv = _each(lambda x, p: x + mm(x, p, NN, p_inv), inv, power)
            yield
    else:
        inv = _each(lambda x, known: _known_inverse(x, known, p_inv), low, inv_known)
    exp_g = _each(jnp.exp, gcum)
    yield
    u_v = _each(lambda i, b, x: mm(i, b * x, NN, p_mm), inv, beta, vc)
    w = _each(lambda i, b, e, x: mm(i, b * e * x, NN, p_mm), inv, beta, exp_g, k)
    yield
    attn = _each(lambda x, y, d: mm(x, y, NT, p_mm) * d, q, k, decay)
    yield
    u = _each(lambda x, y, z: x - mm(y, z, NN, p_mm), u_v, w, s)
    yield
    o = _each(lambda x, e, z: mm(x * e, z, NN, p_mm), q, exp_g, s)
    o = _each(lambda x, a, y: x + mm(a, y, NN, p_mm), o, attn, u)
    yield
    k_end = _each(lambda x, gl, gc: x * jnp.exp(gl - gc), k, g_last, gcum)
    s_new = _each(lambda z, gl, x, y: z * jnp.exp(gl) + mm(x, y, TN, p_mm), s, g_last, k_end, u)
    return (_each(lambda x, z: _head_norm_gate(x, norm_w, z), o, zc), s_new), inv


def gdn_chunk(h, qc, kc, vc, zc, ab, a_log_l, dt_l, norm_w, s, prec=GDN_PREC, reuse_inverse=False):
    args = ([h], [qc], [kc], [vc], [zc], ab, [a_log_l], [dt_l], norm_w, [s], prec)
    if reuse_inverse:
        inv = lax.stop_gradient(gdn_chunks(*args)[1])
        (y, s_new), _ = gdn_chunks(*args, inv_known=inv)
    else:
        (y, s_new), _ = gdn_chunks(*args)
    return y[0], s_new[0]


DIAG_ROWS = SUB_CHUNK // 2
SHIFT_PAD = 8
SHIFT_ROWS = SHIFT_PAD + CHUNK + SHIFT_PAD
SHIFT_WAYS = 4


class RolledRows:
    def down(self, x, which):
        del which
        return [x] + [pltpu.roll(x, off, 0) for off in range(1, DIAG_ROWS)]

    def up_sum(self, parts, which):
        del which
        acc = parts[0]
        for off in range(1, DIAG_ROWS):
            acc = acc + pltpu.roll(parts[off], CHUNK - off, 0)
        return acc


class SlotRows:
    def __init__(self, slots):
        self.slots = slots

    def down(self, x, which):
        self.slots[which, 0, SHIFT_PAD:SHIFT_PAD + CHUNK, :] = x
        return [x] + [self.slots[which, 0, SHIFT_PAD - off:SHIFT_PAD + CHUNK - off, :] for off in range(1, DIAG_ROWS)]

    def up_sum(self, parts, which):
        acc = parts[0]
        for off in range(1, DIAG_ROWS):
            way = 1 + off % (SHIFT_WAYS - 1)
            self.slots[which, way, SHIFT_PAD:SHIFT_PAD + CHUNK, :] = parts[off]
            acc = acc + self.slots[which, way, SHIFT_PAD + off:SHIFT_PAD + CHUNK + off, :]
        return acc


def _sub_block_rows():
    return jnp.bitwise_and(_iota2((CHUNK, 1), 0), DIAG_ROWS - 1)


def _diag_forward(rows, q, key, bc, v):
    rmod = _sub_block_rows()
    k_d, b_d, v_d = rows.down(key, 0), rows.down(bc, 1), rows.down(v, 2)
    o = None
    for off in range(DIAG_ROWS):
        e = jnp.exp(jnp.where(rmod >= off, bc - b_d[off], -jnp.inf))
        term = jnp.sum(q * k_d[off] * e, axis=-1, keepdims=True) * v_d[off]
        o = term if o is None else o + term
    return o


def _diag_backward(rows, q, key, bc, v, do):
    rmod = _sub_block_rows()
    k_d, b_d, v_d = rows.down(key, 0), rows.down(bc, 1), rows.down(v, 2)
    dq = db = None
    dk_parts, db_parts, dv_parts = [], [], []
    for off in range(DIAG_ROWS):
        e = jnp.exp(jnp.where(rmod >= off, bc - b_d[off], -jnp.inf))
        qe = q * e
        a = jnp.sum(qe * k_d[off], axis=-1, keepdims=True)
        da = jnp.sum(do * v_d[off], axis=-1, keepdims=True)
        dv_parts.append(a * do)
        dq_term = (da * e) * k_d[off]
        dk_term = da * qe
        s = dk_term * k_d[off]
        dq = dq_term if dq is None else dq + dq_term
        db = s if db is None else db + s
        dk_parts.append(dk_term)
        db_parts.append(s)
    return dq, rows.up_sum(dk_parts, 0), db - rows.up_sum(db_parts, 1), rows.up_sum(dv_parts, 2)


def diag_part(rows, differentiable=True):
    forward = functools.partial(_diag_forward, rows)
    if not differentiable:
        return forward
    part = jax.custom_vjp(forward)
    part.defvjp(lambda q, key, bc, v: (forward(q, key, bc, v), (q, key, bc, v)),
                lambda res, do: _diag_backward(rows, *res, do))
    return part


def hgrn_stages(qb, fb, ib, gb, l0, l1, norm_w, st, prec=HGRN_PREC, diags=None, o_known=None):
    c = CHUNK
    ri, ci = _iota2((4 * c, c), 0), _iota2((4 * c, c), 1)
    rcol = _iota2((c, 1), 0)
    blk0 = jnp.bitwise_and(ri, c - SUB_CHUNK)
    limit = jnp.where(ri < c, ri + 1, jnp.where(ri < 2 * c, blk0, jnp.where(ri < 3 * c, blk0 + SUB_CHUNK,
                                                                          blk0 + DIAG_ROWS)))
    sel = jnp.where(ci < limit, 1.0, 0.0)
    ri, ci = _iota2((c, c), 0), _iota2((c, c), 1)
    lb = _each(lambda a, b: jax.nn.sigmoid(a - b), l0, l1)
    forget = _each(lambda b, f: b + (1.0 - b) * jax.nn.sigmoid(f), lb, fb)
    key = _each(lambda b, f: (1.0 - b) * jax.nn.sigmoid(-f), lb, fb)
    q = _each(_silu, qb)
    v = ib
    logf = _each(jnp.log, forget)
    sums = _each(lambda x: sel_sums(sel, x), logf)
    bc, b_start, b_end, b_half = ([x[i] for x in sums] for i in range(4))
    b_last = _each(lambda x: jnp.sum(x, axis=0, keepdims=True), logf)
    o = _each(lambda x, b, z: mm(x * jnp.exp(b), z, NT, prec), q, bc, st)
    if diags is None:
        diags = [diag_part(RolledRows())] * len(qb)
    yield
    o = list(o)
    for h in range(len(o)):
        o[h] = o[h] + diags[h](q[h], key[h], bc[h], v[h])
        yield
    second = jnp.bitwise_and(rcol, SUB_CHUNK - 1) >= DIAG_ROWS
    same_sub = jnp.bitwise_and(ri, c - SUB_CHUNK) == jnp.bitwise_and(ci, c - SUB_CHUNK)
    q_half = _each(lambda x, b, bh: x * jnp.exp(jnp.where(second, b - bh, -jnp.inf)), q, bc, b_half)
    k_half = _each(lambda x, b, bh: x * jnp.exp(jnp.where(second, -jnp.inf, bh - b)), key, bc, b_half)
    a_half = _each(lambda x, z: jnp.where(same_sub, mm(x, z, NT, prec), 0.0), q_half, k_half)
    o = _each(lambda acc, a, val: acc + mm(a, val, NN, prec), o, a_half, v)
    yield
    q_rel = _each(lambda x, b, bs: x * jnp.exp(b - bs), q, bc, b_start)
    k_rel = _each(lambda x, b, be: x * jnp.exp(be - b), key, bc, b_end)
    for y in range(c // SUB_CHUNK - 1):
        def scaled(x, b, bs):
            end_y = jnp.sum(jnp.where(rcol == SUB_CHUNK * y + SUB_CHUNK - 1, b, 0.0), axis=0, keepdims=True)
            return x * jnp.exp(jnp.where(rcol >= SUB_CHUNK * (y + 1), bs - end_y, -jnp.inf))
        dq = _each(scaled, q_rel, bc, b_start)
        in_y = (ci >= SUB_CHUNK * y) & (ci < SUB_CHUNK * (y + 1))
        a_y = _each(lambda x, z: jnp.where(in_y, mm(x, z, NT, prec), 0.0), dq, k_rel)
        o = _each(lambda acc, a, val: acc + mm(a, val, NN, prec), o, a_y, v)
        yield
    k_state = _each(lambda x, bl, b: x * jnp.exp(bl - b), key, b_last, bc)
    st_new = _each(lambda z, bl, val, x: z * jnp.exp(bl) + mm(val, x, TN, prec), st, b_last, v, k_state)
    if o_known is not None:
        o = _each(_known_value, o, o_known)
    return (_each(lambda x, z: _head_norm_gate(x, norm_w, z), o, gb), st_new), o


def _drain(gen):
    try:
        while True:
            next(gen)
    except StopIteration as done:
        return done.value


def _alternate(gen_a, gen_b):
    out, live = [None, None], [gen_a, gen_b]
    while any(g is not None for g in live):
        for i, g in enumerate(live):
            if g is None:
                continue
            try:
                next(g)
            except StopIteration as done:
                out[i], live[i] = done.value, None
    return out


def gdn_chunks(*args, **kwargs):
    return _drain(gdn_stages(*args, **kwargs))


def hgrn_chunks(*args, **kwargs):
    return _drain(hgrn_stages(*args, **kwargs))


def hgrn_chunk(qb, fb, ib, gb, l0, l1, norm_w, st, prec=HGRN_PREC, reuse_output=False):
    args = ([qb], [fb], [ib], [gb], [l0], [l1], norm_w, [st], prec)
    if reuse_output:
        known = lax.stop_gradient(hgrn_chunks(*args)[1])
        (y, st_new), _ = hgrn_chunks(*args, o_known=known)
    else:
        (y, st_new), _ = hgrn_chunks(*args)
    return y[0], st_new[0]


HEAD_VEC = (N_HEADS, 1, LANES)


class _ChunkSpecs:
    def __init__(self, nc, rev):
        self.nc, self.rev = nc, rev

    def _c(self, c):
        return self.nc - 1 - c if self.rev else c

    def row(self, width, block=0):
        return pl.BlockSpec((CHUNK, width), lambda c: (self._c(c), block))

    def per_head(self, rows):
        return pl.BlockSpec((None, N_HEADS, rows, rows), lambda c: (self._c(c), 0, 0, 0))

    @staticmethod
    def whole(shape):
        return pl.BlockSpec(shape, lambda c: (0,) * len(shape))


def _lanes(j):
    return slice(j * LANES, (j + 1) * LANES)


def mixer_fwd(qkv_c, proj, a_log_l, dt_l, gdn_norm_w, l0, l1, hgrn_norm_w, name):
    t = qkv_c.shape[0]
    hb = N_HEADS
    sp = _ChunkSpecs(t // CHUNK, rev=False)
    hs = list(range(hb))

    def body(q_ref, k_ref, v_ref, z_ref, ab_ref, al_ref, dt_ref, gnw_ref, qb_ref, fb_ref, ib_ref, gb_ref, l0_ref, l1_ref,
             hnw_ref, y_ref, hist_a_ref, inv_ref, hist_b_ref, o_ref, sa_ref, sb_ref, shift_ref):
        @pl.when(pl.program_id(0) == 0)
        def _():
            sa_ref[...] = jnp.zeros_like(sa_ref)
            sb_ref[...] = jnp.zeros_like(sb_ref)
            shift_ref[...] = jnp.zeros_like(shift_ref)

        heads = lambda ref: [ref[:, _lanes(j)] for j in hs]
        s_a, s_b = [sa_ref[h] for h in hs], [sb_ref[h] for h in hs]
        for h in hs:
            hist_a_ref[h] = s_a[h]
            hist_b_ref[h] = s_b[h]
        diags = [diag_part(SlotRows(shift_ref.at[h]), differentiable=False) for h in hs]
        ((y_a, s_a_new), inv), ((y_b, s_b_new), o_pre) = _alternate(
            gdn_stages(hs, heads(q_ref), heads(k_ref), heads(v_ref), heads(z_ref), ab_ref[...],
                       [al_ref[h] for h in hs], [dt_ref[h] for h in hs], gnw_ref[...], s_a),
            hgrn_stages(heads(qb_ref), heads(fb_ref), heads(ib_ref), heads(gb_ref),
                        [l0_ref[h] for h in hs], [l1_ref[h] for h in hs], hnw_ref[...], s_b, diags=diags))
        for h in hs:
            y_ref[:, _lanes(h)] = y_a[h].astype(BF16)
            y_ref[:, _lanes(hb + h)] = y_b[h].astype(BF16)
            o_ref[:, _lanes(h)] = o_pre[h]
            sa_ref[h] = s_a_new[h]
            sb_ref[h] = s_b_new[h]
            inv_ref[h] = inv[h]

    vec, gain, slab = sp.whole(HEAD_VEC), sp.whole((1, LANES)), functools.partial(sp.row, GDN_WIDTH)
    states = jax.ShapeDtypeStruct((sp.nc, N_HEADS, HEAD_DIM, HEAD_DIM), F32)
    return pl.pallas_call(
        body, name=name, grid=(sp.nc,),
        in_specs=[slab(0), slab(1), slab(2), slab(3), sp.row(LANES, AB_BLOCK), vec, vec, gain,
                  slab(4), slab(5), slab(6), slab(7), vec, vec, gain],
        out_specs=[sp.row(2 * GDN_WIDTH), sp.per_head(HEAD_DIM), sp.per_head(CHUNK), sp.per_head(HEAD_DIM), slab(0)],
        out_shape=[jax.ShapeDtypeStruct((t, 2 * GDN_WIDTH), BF16), states,
                   jax.ShapeDtypeStruct((sp.nc, N_HEADS, CHUNK, CHUNK), F32), states,
                   jax.ShapeDtypeStruct((t, GDN_WIDTH), F32)],
        scratch_shapes=[pltpu.VMEM((N_HEADS, HEAD_DIM, HEAD_DIM), F32), pltpu.VMEM((N_HEADS, HEAD_DIM, HEAD_DIM), F32),
                        pltpu.VMEM((hb, 3, SHIFT_WAYS, SHIFT_ROWS, LANES), F32)],
        compiler_params=_params(("arbitrary",)),
    )(qkv_c, qkv_c, qkv_c, proj, proj, a_log_l, dt_l, gdn_norm_w, proj, proj, proj, proj, l0, l1, hgrn_norm_w)


def mixer_bwd(qkv_c, proj, a_log_l, dt_l, gdn_norm_w, l0, l1, hgrn_norm_w, hist_a, inv_hist, hist_b, o_pre, dy, name):
    t = qkv_c.shape[0]
    hb = N_HEADS
    sp = _ChunkSpecs(t // CHUNK, rev=True)
    hs = list(range(hb))

    def body(q_ref, k_ref, v_ref, z_ref, ab_ref, al_ref, dt_ref, gnw_ref, qb_ref, fb_ref, ib_ref, gb_ref, l0_ref, l1_ref,
             hnw_ref, hist_a_ref, inv_ref, hist_b_ref, o_ref, dy_ref,
             dqkv_ref, dproj_ref, dal_ref, ddt_ref, dgnw_ref, dl0_ref, dl1_ref, dhnw_ref, dsa_ref, dsb_ref, shift_ref):
        @pl.when(pl.program_id(0) == 0)
        def _():
            for ref in (dal_ref, ddt_ref, dgnw_ref, dl0_ref, dl1_ref, dhnw_ref, dsa_ref, dsb_ref, shift_ref):
                ref[...] = jnp.zeros_like(ref)

        heads = lambda ref, first=0: [ref[:, _lanes(first + j)] for j in hs]
        diags = [diag_part(SlotRows(shift_ref.at[h])) for h in hs]
        inv_known, o_known = [inv_ref[h] for h in hs], heads(o_ref)

        def both(ga, gb):
            (ra, inv), (rb, o_pre) = _alternate(gdn_stages(hs, *ga, inv_known=inv_known),
                                                hgrn_stages(*gb, diags=diags, o_known=o_known))
            return (ra, rb), (inv, o_pre)

        ga = (heads(q_ref), heads(k_ref), heads(v_ref), heads(z_ref), ab_ref[...], [al_ref[h] for h in hs],
              [dt_ref[h] for h in hs], gnw_ref[...], [hist_a_ref[h] for h in hs])
        gb = (heads(qb_ref), heads(fb_ref), heads(ib_ref), heads(gb_ref), [l0_ref[h] for h in hs],
              [l1_ref[h] for h in hs], hnw_ref[...], [hist_b_ref[h] for h in hs])
        _, vjp, _ = jax.vjp(both, ga, gb, has_aux=True)
        dy_a = [x.astype(F32) for x in heads(dy_ref)]
        dy_b = [x.astype(F32) for x in heads(dy_ref, hb)]
        (dq, dk, dv, dz, dab, dal, ddt, dgnw, ds_a), (dqb, dfb, dib, dgb, dl0, dl1, dhnw, ds_b) = vjp(
            ((dy_a, [dsa_ref[h] for h in hs]), (dy_b, [dsb_ref[h] for h in hs])))
        for h in hs:
            dqkv_ref[:, _lanes(h)] = dq[h]
            dqkv_ref[:, _lanes(hb + h)] = dk[h]
            dqkv_ref[:, _lanes(2 * hb + h)] = dv[h]
            for slab, val in enumerate((dz, dqb, dfb, dib, dgb)):
                dproj_ref[:, _lanes((3 + slab) * hb + h)] = val[h].astype(BF16)
            dal_ref[h] += dal[h]
            ddt_ref[h] += ddt[h]
            dl0_ref[h] += dl0[h]
            dl1_ref[h] += dl1[h]
            dsa_ref[h] = ds_a[h]
            dsb_ref[h] = ds_b[h]
        dproj_ref[:, MAIN_WIDTH:] = dab.astype(BF16)
        dgnw_ref[...] += dgnw
        dhnw_ref[...] += dhnw

    vec, gain, slab = sp.whole(HEAD_VEC), sp.whole((1, LANES)), functools.partial(sp.row, GDN_WIDTH)
    vec_shape, gain_shape = jax.ShapeDtypeStruct(HEAD_VEC, F32), jax.ShapeDtypeStruct((1, LANES), F32)
    return pl.pallas_call(
        body, name=name, grid=(sp.nc,),
        in_specs=[slab(0), slab(1), slab(2), slab(3), sp.row(LANES, AB_BLOCK), vec, vec, gain,
                  slab(4), slab(5), slab(6), slab(7), vec, vec, gain,
                  sp.per_head(HEAD_DIM), sp.per_head(CHUNK), sp.per_head(HEAD_DIM), slab(0), sp.row(2 * GDN_WIDTH)],
        out_specs=[sp.row(QKV_WIDTH), sp.row(CAT_WIDTH), vec, vec, gain, vec, vec, gain],
        out_shape=[jax.ShapeDtypeStruct((t, QKV_WIDTH), F32), jax.ShapeDtypeStruct((t, CAT_WIDTH), BF16),
                   vec_shape, vec_shape, gain_shape, vec_shape, vec_shape, gain_shape],
        scratch_shapes=[pltpu.VMEM((N_HEADS, HEAD_DIM, HEAD_DIM), F32), pltpu.VMEM((N_HEADS, HEAD_DIM, HEAD_DIM), F32),
                        pltpu.VMEM((hb, 3, SHIFT_WAYS, SHIFT_ROWS, LANES), F32)],
        compiler_params=_params(("arbitrary",)),
    )(qkv_c, qkv_c, qkv_c, proj, proj, a_log_l, dt_l, gdn_norm_w, proj, proj, proj, proj, l0, l1, hgrn_norm_w,
      hist_a, inv_hist, hist_b, o_pre, dy)


def _adamw(w, g, m, v):
    m = ADAM_B1 * m + (1.0 - ADAM_B1) * g
    v = ADAM_B2 * v + (1.0 - ADAM_B2) * jnp.square(g)
    m_hat = m / (1.0 - ADAM_B1 ** ADAM_STEP)
    v_hat = v / (1.0 - ADAM_B2 ** ADAM_STEP)
    delta = -ADAM_LR * (m_hat / (jnp.sqrt(v_hat) + ADAM_EPS) + ADAM_WD * w)
    return delta, m, v


def adamw_reduce(parts, w, m, v, name, rb=128):
    r, c = w.shape
    rb = min(rb, r)

    def body(p_ref, w_ref, m_ref, v_ref, g_ref, d_ref, mo_ref, vo_ref):
        g = p_ref[0].astype(F32)
        for d in range(1, N_DEV):
            g = g + p_ref[d].astype(F32)
        delta, mn, vn = _adamw(w_ref[...], g, m_ref[...], v_ref[...])
        g_ref[...] = g
        d_ref[...] = delta
        mo_ref[...] = mn
        vo_ref[...] = vn

    blk = pl.BlockSpec((rb, c), lambda i: (i, 0))
    return pl.pallas_call(
        body, name=name, grid=(r // rb,),
        in_specs=[pl.BlockSpec((N_DEV, rb, c), lambda i: (0, i, 0)), blk, blk, blk],
        out_specs=[blk] * 4, out_shape=[jax.ShapeDtypeStruct((r, c), F32)] * 4,
        compiler_params=_params(("parallel",)))(parts, w, m, v)


def adamw_small(w, g, m, v, name):
    def body(w_ref, g_ref, m_ref, v_ref, d_ref, mo_ref, vo_ref):
        delta, mn, vn = _adamw(w_ref[...], g_ref[...], m_ref[...], v_ref[...])
        d_ref[...] = delta
        mo_ref[...] = mn
        vo_ref[...] = vn

    vmem = pl.BlockSpec(memory_space=pltpu.VMEM)
    return pl.pallas_call(body, name=name, in_specs=[vmem] * 4, out_specs=[vmem] * 3,
                          out_shape=[jax.ShapeDtypeStruct(w.shape, F32)] * 3)(w, g, m, v)


def _pack(arrays):
    flat = jnp.concatenate([a.reshape(-1).astype(F32) for a in arrays])
    rows = -(-flat.shape[0] // (8 * LANES)) * 8
    return jnp.pad(flat, (0, rows * LANES - flat.shape[0])).reshape(rows, LANES)


def _unpack(packed, shapes):
    flat, out, off = packed.reshape(-1), [], 0
    for s in shapes:
        n = 1
        for d in s:
            n *= d
        out.append(flat[off:off + n].reshape(s))
        off += n
    return out


def _relu2_epilogue(acc, _):
    r = jnp.maximum(acc, 0.0)
    return acc, r * r


def _relu2_bwd_epilogue(acc, a1):
    return (acc * (2.0 * jnp.maximum(a1, 0.0)),)


def kernel(x, w_in, conv_w, gdn_a_log, gdn_dt_bias, gdn_norm_w, hgrn_lb_logits, hgrn_norm_w, w_out, norm_mix_w, norm_ffn_w, w_ff1, w_ff2, norm_final_w, loss_target, m_w_in, m_conv_w, m_gdn_a_log, m_gdn_dt_bias, m_gdn_norm_w, m_hgrn_lb_logits, m_hgrn_norm_w, m_w_out, m_norm_mix_w, m_norm_ffn_w, m_w_ff1, m_w_ff2, m_norm_final_w, v_w_in, v_conv_w, v_gdn_a_log, v_gdn_dt_bias, v_gdn_norm_w, v_hgrn_lb_logits, v_hgrn_norm_w, v_w_out, v_norm_mix_w, v_norm_ffn_w, v_w_ff1, v_w_ff2, v_norm_final_w):
    me = _my_flat()
    xs = x[0]
    target = loss_target[0]
    shard_in = w_in.shape[2]
    shard_conv = conv_w.shape[2]

    tok = lambda t: t[0:1, 0:1]
    own = lambda src: lax.dynamic_index_in_dim(src, me, 0, keepdims=False)

    g_in, g_conv = gather_two_level([w_in[0].astype(BF16), conv_w[0]], "gather_w_in")
    h_g1, t_g1 = exchange_start([w_out[0].astype(BF16), w_ff1[0].astype(BF16)], True, "gather_mid_start", after=[g_in])
    h_g2, t_g2 = exchange_start([w_ff2[0].astype(BF16)], True, "gather_ff2_start", after=[t_g1])
    w_cat = weights_to_cat(g_in)
    conv_full = jnp.transpose(g_conv, (1, 0, 2)).reshape(4, QKV_WIDTH)

    lane_b = lambda p: jnp.broadcast_to(p.reshape(N_HEADS, 1, 1), HEAD_VEC)
    a_log_l, dt_l = lane_b(gdn_a_log[0]), lane_b(gdn_dt_bias[0])
    l0 = hgrn_lb_logits[0].reshape(HEAD_VEC)
    l1 = hgrn_lb_logits[1].reshape(HEAD_VEC)

    n1, r1 = rms_fwd(xs, norm_mix_w + tok(t_g1) + tok(t_g2), "rms_mix")
    proj = matmul(n1, w_cat, "nn", "in_proj", tn=CAT_WIDTH // 5)
    qkv_c = conv_fwd(proj, conv_full, "conv_fwd")
    y, hist_a, inv_a, hist_b, o_b = mixer_fwd(qkv_c, proj, a_log_l, dt_l, gdn_norm_w, l0, l1, hgrn_norm_w, "mixer_fwd")
    (s_out, s_ff1), (l_out, l_ff1) = exchange_wait(h_g1, "gather_mid_wait", after=[y])
    w_out_full = _own_slot(l_out, s_out).reshape(D_MODEL, D_MODEL)
    w_ff1_sh = _own_slot(l_ff1, s_ff1)
    h1, n2, r2 = out_proj_rms(y, w_out_full, xs, norm_ffn_w, "out_proj_rms")
    a1, act = matmul(n2, w_ff1_sh, "nn", "ff1", out_dtypes=(F32, BF16), epilogue=_relu2_epilogue, b_shards=True)
    (s_ff2,), (l_ff2,) = exchange_wait(h_g2, "gather_ff2_wait", after=[act])
    w_ff2_full = _own_slot(l_ff2, s_ff2).reshape(D_FF, D_MODEL)
    loss_sum, dh2_b, d_final = ff2_loss(act, w_ff2_full, h1, norm_final_w.reshape(1, D_MODEL), target, "ff2_loss")

    da1 = matmul(dh2_b, w_ff2_full, "nt", "d_act", out_dtypes=(BF16,), epilogue=_relu2_bwd_epilogue, extra=a1)
    t_all = xs.shape[0]
    dw_ff2 = matmul(act, dh2_b, "tn", "dw_ff2", out_dtypes=(BF16,), tk=t_all)
    p_ff2 = dw_ff2.reshape(N_DEV, D_FF // N_DEV, D_MODEL)
    h_s1, t_s1 = exchange_start([p_ff2], False, "scatter_ff2_start")
    dn2 = matmul(da1, w_ff1_sh, "nt", "d_n2", out_dtypes=(BF16,), after=[t_s1], b_shards=True, k_group=4)
    p_ff1 = matmul(n2, da1, "tn", "dw_ff1", out_dtypes=(BF16,), tn=D_FF // N_DEV, tk=t_all, after=[t_s1], out_shards=True)
    h_s2, t_s2 = exchange_start([p_ff1], False, "scatter_ff1_start")
    dh1_b, d_ffn = rms_bwd(h1, r2, norm_ffn_w + tok(t_s2), dn2, dh2_b, BF16, "rms_ffn_bwd")
    dmix = matmul(dh1_b, w_out_full, "nt", "d_mix", out_dtypes=(BF16,))
    dw_out = matmul(y, dh1_b, "tn", "dw_out", out_dtypes=(BF16,), tk=t_all)
    p_out = dw_out.reshape(N_DEV, D_MODEL // N_DEV, D_MODEL)
    h_s3, t_s3 = exchange_start([p_out], False, "scatter_out_start")
    d_qkv_c, dproj, d_alog_l, d_dt_l, d_gnw, dl0, dl1, d_hnw = mixer_bwd(
        qkv_c, proj, a_log_l, dt_l, gdn_norm_w + tok(t_s3), l0, l1, hgrn_norm_w, hist_a, inv_a, hist_b, o_b, dmix,
        "mixer_bwd")
    dproj, d_conv_full = conv_bwd(proj, d_qkv_c, conv_full, dproj, "conv_bwd")
    dw_cat = matmul(n1, dproj, "tn", "dw_in", out_dtypes=(BF16,), tm=512, tn=CAT_WIDTH // 5, tk=t_all)
    p_in = cat_to_shards(dw_cat, shard_in)
    h_s4, t_s4 = exchange_start([p_in], False, "scatter_in_start")

    (s_ff2g,), (r_ff2,) = exchange_wait(h_s1, "scatter_ff2_wait", after=[t_s4])
    (s_ff1g,), (r_ff1,) = exchange_wait(h_s2, "scatter_ff1_wait", after=[t_s4])
    (s_outg,), (r_out,) = exchange_wait(h_s3, "scatter_out_wait", after=[t_s4])
    g_w_ff2, d_w_ff2, nm_w_ff2, nv_w_ff2 = adamw_reduce(
        _own_slot(r_ff2, own(s_ff2g)), w_ff2[0], m_w_ff2[0], v_w_ff2[0], "adamw_w_ff2")
    g_w_ff1, d_w_ff1, nm_w_ff1, nv_w_ff1 = adamw_reduce(
        _own_slot(r_ff1, own(s_ff1g)), w_ff1[0], m_w_ff1[0], v_w_ff1[0], "adamw_w_ff1")
    g_w_out, d_w_out, nm_w_out, nv_w_out = adamw_reduce(
        _own_slot(r_out, own(s_outg)), w_out[0], m_w_out[0], v_w_out[0], "adamw_w_out")
    dn1 = matmul(dproj, w_cat, "nt", "d_n1", out_dtypes=(BF16,), tk=CAT_WIDTH // 5, after=[t_s4])
    dx, d_mix = rms_bwd(xs, r1, norm_mix_w, dn1, dh1_b, F32, "rms_mix_bwd")
    (s_ing,), (r_in,) = exchange_wait(h_s4, "scatter_in_wait", after=[dx, d_w_ff2, d_w_ff1, d_w_out])
    g_w_in, d_w_in, nm_w_in, nv_w_in = adamw_reduce(
        _own_slot(r_in, own(s_ing)), w_in[0], m_w_in[0], v_w_in[0], "adamw_w_in")

    d_lb = jnp.stack([dl0.reshape(GDN_WIDTH), dl1.reshape(GDN_WIDTH)])
    small_shapes = [(1, N_HEADS), (1, N_HEADS), (1, HEAD_DIM), (2, GDN_WIDTH), (1, HEAD_DIM), (1, D_MODEL),
                    (1, D_MODEL), (D_MODEL,), (4, QKV_WIDTH)]
    small = _pack([d_alog_l[:, 0, 0], d_dt_l[:, 0, 0], d_gnw, d_lb, d_hnw, d_mix, d_ffn, d_final, d_conv_full])
    red = allreduce_small(small, "allreduce_small")
    g_alog, g_dt, g_gnw, g_lb, g_hnw, g_mix, g_ffn, g_final, g_conv_full = _unpack(red, small_shapes)
    g_conv = lax.dynamic_slice(g_conv_full, (0, me * shard_conv), (4, shard_conv)).reshape(1, 4, shard_conv)
    small_g = [g_alog, g_dt, g_gnw, g_lb, g_hnw, g_mix, g_ffn, g_final, g_conv]
    small_w = [gdn_a_log, gdn_dt_bias, gdn_norm_w, hgrn_lb_logits, hgrn_norm_w, norm_mix_w, norm_ffn_w, norm_final_w, conv_w]
    small_m = [m_gdn_a_log, m_gdn_dt_bias, m_gdn_norm_w, m_hgrn_lb_logits, m_hgrn_norm_w, m_norm_mix_w, m_norm_ffn_w,
               m_norm_final_w, m_conv_w]
    small_v = [v_gdn_a_log, v_gdn_dt_bias, v_gdn_norm_w, v_hgrn_lb_logits, v_hgrn_norm_w, v_norm_mix_w, v_norm_ffn_w,
               v_norm_final_w, v_conv_w]
    shapes = [a.shape for a in small_w]
    d_s, m_s, v_s = adamw_small(_pack(small_w), _pack(small_g), _pack(small_m), _pack(small_v), "adamw_small")
    d_alog, d_dt, d_gn, d_lbl, d_hn, d_nm, d_nf, d_nfin, d_cw = _unpack(d_s, shapes)
    m_alog, m_dt, m_gn, m_lbl, m_hn, m_nm, m_nf, m_nfin, m_cw = _unpack(m_s, shapes)
    v_alog, v_dt, v_gn, v_lbl, v_hn, v_nm, v_nf, v_nfin, v_cw = _unpack(v_s, shapes)

    loss = lax.psum(loss_sum[0, 0], ("x", "y", "c"))
    lead = lambda a: a[None]
    grads = [lead(g_w_in), g_conv, g_alog, g_dt, g_gnw, g_lb, g_hnw, lead(g_w_out), g_mix, g_ffn,
             lead(g_w_ff1), lead(g_w_ff2), g_final]
    deltas = [lead(d_w_in), d_cw, d_alog, d_dt, d_gn, d_lbl, d_hn, lead(d_w_out), d_nm, d_nf,
              lead(d_w_ff1), lead(d_w_ff2), d_nfin]
    new_m = [lead(nm_w_in), m_cw, m_alog, m_dt, m_gn, m_lbl, m_hn, lead(nm_w_out), m_nm, m_nf,
             lead(nm_w_ff1), lead(nm_w_ff2), m_nfin]
    new_v = [lead(nv_w_in), v_cw, v_alog, v_dt, v_gn, v_lbl, v_hn, lead(nv_w_out), v_nm, v_nf,
             lead(nv_w_ff1), lead(nv_w_ff2), v_nfin]
    return (loss, dx[None], *grads, *deltas, *new_m, *new_v)
```

```python
import functools

import jax
import jax.numpy as jnp
from jax import lax
from jax.experimental import pallas as pl
from jax.experimental.pallas import tpu as pltpu

F32 = jnp.float32
BF16 = jnp.bfloat16
HI = lax.Precision.HIGHEST

N_DEV = 8
D_MODEL = 2048
CHUNK = 64
SUB_CHUNK = 16
HEAD_DIM = 128
N_HEADS = 8
GDN_WIDTH = N_HEADS * HEAD_DIM
D_FF = 4 * D_MODEL
QKV_WIDTH = 3 * GDN_WIDTH
MAIN_WIDTH = 8 * GDN_WIDTH
CAT_WIDTH = MAIN_WIDTH + 128
AB_BLOCK = MAIN_WIDTH // 128
NORM_EPS = 1e-6
L2_EPS = 1e-6
LANES = 128
VMEM_LIMIT = 56 * 1024 * 1024

ADAM_LR = 0.001
ADAM_B1 = 0.9
ADAM_B2 = 0.999
ADAM_EPS = 1e-08
ADAM_WD = 0.01
ADAM_STEP = 10

MESH = pl.DeviceIdType.MESH


def _params(sem=None):
    return pltpu.CompilerParams(dimension_semantics=sem, vmem_limit_bytes=VMEM_LIMIT)


def _dot(a, b, dims, prec=None):
    return lax.dot_general(a, b, (dims, ((), ())), precision=prec, preferred_element_type=F32)


NN = ((1,), (0,))
NT = ((1,), (1,))
TN = ((0,), (0,))


def _split_bf16(x, pieces):
    out = []
    for _ in range(pieces - 1):
        p = x.astype(BF16)
        out.append(p)
        x = x - p.astype(F32)
    out.append(x.astype(BF16))
    return out


def _mm_raw(a, b, dims, prec):
    if prec == "hi":
        return _dot(a, b, dims, HI)
    if prec == "bf":
        return _dot(a.astype(BF16), b.astype(BF16), dims)
    a_hi, a_lo = _split_bf16(a, 2)
    b_hi, b_lo = _split_bf16(b, 2)
    return _dot(a_hi, b_hi, dims) + (_dot(a_hi, b_lo, dims) + _dot(a_lo, b_hi, dims))


@functools.partial(jax.custom_vjp, nondiff_argnums=(2, 3))
def mm(a, b, dims, prec):
    return _mm_raw(a, b, dims, prec)


def _mm_fwd(a, b, dims, prec):
    return _mm_raw(a, b, dims, prec), (a, b)


def _mm_bwd(dims, prec, res, ct):
    a, b = res
    if dims == NN:
        return _mm_raw(ct, b, NT, prec), _mm_raw(a, ct, TN, prec)
    if dims == NT:
        return _mm_raw(ct, b, NN, prec), _mm_raw(ct, a, TN, prec)
    return _mm_raw(b, ct, NT, prec), _mm_raw(a, ct, NN, prec)


mm.defvjp(_mm_fwd, _mm_bwd)


def _sel_raw(sel, x, dims):
    sel = sel.astype(BF16)
    p0, p1, p2 = _split_bf16(x, 3)
    return _dot(sel, p0, dims) + (_dot(sel, p1, dims) + _dot(sel, p2, dims))


def _sel_parts(sel, x):
    c = x.shape[0]
    full = _sel_raw(sel, x, NN)
    return tuple(full[i * c:(i + 1) * c] for i in range(sel.shape[0] // c))


@jax.custom_vjp
def sel_sums(sel, x):
    return _sel_parts(sel, x)


def _sel_fwd(sel, x):
    return _sel_parts(sel, x), sel


def _sel_bwd(sel, cts):
    return jnp.zeros_like(sel), _sel_raw(sel, jnp.concatenate(cts, axis=0), TN)


sel_sums.defvjp(_sel_fwd, _sel_bwd)


@jax.custom_vjp
def _known_value(computed, known):
    del computed
    return known


_known_value.defvjp(lambda computed, known: (known, None), lambda _, ct: (ct, jnp.zeros_like(ct)))


def _my_flat():
    return 4 * lax.axis_index("x") + 2 * lax.axis_index("y") + lax.axis_index("c")


def _peer(k):
    x, y, c = lax.axis_index("x"), lax.axis_index("y"), lax.axis_index("c")
    kx, ky, kc = (k >> 2) & 1, (k >> 1) & 1, k & 1
    px = (1 - x) if kx else x
    py = (1 - y) if ky else y
    pc = (1 - c) if kc else c
    return (px, py, pc), 4 * px + 2 * py + pc


def gather_two_level(xs, name):
    n = len(xs)

    def body(*refs):
        x_refs, y_refs = refs[:n], refs[n:2 * n]
        send_sems, recv_sems, local_sems = refs[2 * n:]
        x, y, c = lax.axis_index("x"), lax.axis_index("y"), lax.axis_index("c")
        me, sibling = (x, y, c), (x, y, 1 - c)
        chips = [(1 - x, y), (x, 1 - y), (1 - x, 1 - y)]
        flat = lambda p: 4 * p[0] + 2 * p[1] + p[2]

        def copy(a, k, block, to, src=None):
            return pltpu.make_async_remote_copy(
                src_ref=y_refs[a].at[flat(block)] if src is None else src, dst_ref=y_refs[a].at[flat(block)],
                send_sem=send_sems.at[a, k], recv_sem=recv_sems.at[a, k], device_id=to, device_id_type=MESH)

        mine = [pltpu.make_async_copy(x_refs[a], y_refs[a].at[flat(me)], local_sems.at[a]) for a in range(n)]
        for cp in mine:
            cp.start()
        first = [copy(a, 0, me, sibling, src=x_refs[a]) for a in range(n)]
        first += [copy(a, 1 + j, me, (*chip, c), src=x_refs[a]) for j, chip in enumerate(chips) for a in range(n)]
        for cp in first:
            cp.start()
        passed = []
        for j, chip in enumerate(chips):
            for a in range(n):
                copy(a, 1 + j, (*chip, c), me).wait_recv()
                cp = copy(a, 4 + j, (*chip, c), sibling)
                cp.start()
                passed.append(cp)
        for a in range(n):
            copy(a, 0, sibling, me).wait_recv()
        for j, chip in enumerate(chips):
            for a in range(n):
                copy(a, 4 + j, (*chip, 1 - c), me).wait_recv()
        for cp in first + passed:
            cp.wait_send()
        for cp in mine:
            cp.wait()

    any_spec = pl.BlockSpec(memory_space=pl.ANY)
    return pl.pallas_call(
        body, name=name, out_shape=[jax.ShapeDtypeStruct((N_DEV,) + x.shape, x.dtype) for x in xs],
        in_specs=[any_spec] * n, out_specs=[any_spec] * n,
        scratch_shapes=[pltpu.SemaphoreType.DMA((n, N_DEV - 1)), pltpu.SemaphoreType.DMA((n, N_DEV - 1)),
                        pltpu.SemaphoreType.DMA((n,))],
    )(*xs)


HBM_SPEC = pl.BlockSpec(memory_space=pltpu.HBM)
SEM_SPEC = pl.BlockSpec(memory_space=pltpu.SEMAPHORE)
ANY_SPEC = pl.BlockSpec(memory_space=pl.ANY)
DATAFLOW = pltpu.SideEffectType.DATAFLOW_SIDE_EFFECTING


def _in_hbm(x):
    return pltpu.with_memory_space_constraint(x, pltpu.HBM)


ALL_PEERS = tuple(range(1, N_DEV))
CHIP_PEERS = (1, 2, 4, 6)
OTHER_CHIPS = (2, 4, 6)


def exchange_start(xs, gather, name, after=(), peers=ALL_PEERS):
    n, n_after = len(xs), len(after)

    def body(*refs):
        x_refs, land_refs = refs[:n], refs[n:2 * n]
        sems = refs[2 * n + n_after:2 * n + n_after + 2 * n]
        token = refs[-1]
        me = _my_flat()
        for k in peers:
            peer, peer_flat = _peer(k)
            for a in range(n):
                src = x_refs[a] if gather else x_refs[a].at[peer_flat]
                pltpu.make_async_remote_copy(src_ref=src, dst_ref=land_refs[a].at[me], send_sem=sems[a],
                                             recv_sem=sems[n + a], device_id=peer, device_id_type=MESH).start()
        token[...] = jnp.zeros_like(token)

    lands = [_in_hbm(lax.empty(((N_DEV,) + x.shape) if gather else x.shape, x.dtype)) for x in xs]
    hbm_out = [pltpu.HBM(x.shape, x.dtype) for x in xs] + [pltpu.HBM(l.shape, l.dtype) for l in lands]
    res = pl.pallas_call(
        body, name=name,
        out_shape=(*([pltpu.SemaphoreType.DMA(())] * (2 * n)), *hbm_out, jax.ShapeDtypeStruct((8, LANES), F32)),
        in_specs=[HBM_SPEC] * (2 * n) + [ANY_SPEC] * n_after,
        out_specs=(*([SEM_SPEC] * (2 * n)), *([HBM_SPEC] * (2 * n)), pl.BlockSpec(memory_space=pltpu.VMEM)),
        input_output_aliases={i: 2 * n + i for i in range(2 * n)},
        compiler_params=pltpu.CompilerParams(has_side_effects=DATAFLOW),
    )(*[_in_hbm(x) for x in xs], *lands, *after)
    return (list(res[:2 * n]), list(res[2 * n:3 * n]), list(res[3 * n:4 * n])), res[-1]


def forward_start(lands, name, after=()):
    n, n_after = len(lands), len(after)

    def body(*refs):
        land_refs = refs[:n]
        sems = refs[n + n_after:n + n_after + 2 * n]
        token = refs[-1]
        sibling, _ = _peer(1)
        for k in OTHER_CHIPS:
            _, from_flat = _peer(k)
            for a in range(n):
                slot = land_refs[a].at[from_flat]
                pltpu.make_async_remote_copy(src_ref=slot, dst_ref=slot, send_sem=sems[a], recv_sem=sems[n + a],
                                             device_id=sibling, device_id_type=MESH).start()
        token[...] = jnp.zeros_like(token)

    res = pl.pallas_call(
        body, name=name,
        out_shape=(*([pltpu.SemaphoreType.DMA(())] * (2 * n)), *[pltpu.HBM(l.shape, l.dtype) for l in lands],
                   jax.ShapeDtypeStruct((8, LANES), F32)),
        in_specs=[HBM_SPEC] * n + [ANY_SPEC] * n_after,
        out_specs=(*([SEM_SPEC] * (2 * n)), *([HBM_SPEC] * n), pl.BlockSpec(memory_space=pltpu.VMEM)),
        input_output_aliases={i: 2 * n + i for i in range(n)},
        compiler_params=pltpu.CompilerParams(has_side_effects=DATAFLOW),
    )(*lands, *after)
    return (list(res[:2 * n]), [], list(res[2 * n:3 * n])), res[-1]


def exchange_wait(handle, name, after=(), copies=N_DEV - 1):
    sems, xs, lands = handle
    n, n_x, n_after = len(lands), len(xs), len(after)

    def body(*refs):
        land_refs = refs[n_x:n_x + n]
        sem_refs = refs[n_x + n:n_x + 3 * n]
        for a in range(n):
            every = land_refs[a].at[pl.ds(0, copies)]
            cp = pltpu.make_async_remote_copy(src_ref=every, dst_ref=every, send_sem=sem_refs[a],
                                              recv_sem=sem_refs[n + a], device_id=_peer(1)[0], device_id_type=MESH)
            cp.wait_send()
            cp.wait_recv()

    res = pl.pallas_call(
        body, name=name,
        out_shape=[pltpu.HBM(x.shape, x.dtype) for x in xs] + [pltpu.HBM(l.shape, l.dtype) for l in lands],
        in_specs=[HBM_SPEC] * (n_x + n) + [SEM_SPEC] * (2 * n) + [ANY_SPEC] * n_after,
        out_specs=[HBM_SPEC] * (n_x + n),
        input_output_aliases={i: i for i in range(n_x + n)},
        compiler_params=pltpu.CompilerParams(has_side_effects=DATAFLOW),
    )(*xs, *lands, *sems, *after)
    return list(res[:n_x]), list(res[n_x:])


def _one(handle, a):
    sems, xs, lands = handle
    n = len(lands)
    return [sems[a], sems[n + a]], xs[a:a + 1], [lands[a]]


def _own_slot(land, block):
    return lax.dynamic_update_slice(land, block[None], (_my_flat(),) + (0,) * block.ndim)


def allreduce_small(x, name):
    rows = x.shape[0]

    def body(x_ref, o_ref, buf, send_sems, recv_sems):
        me = _my_flat()
        buf[me] = x_ref[...]
        sends = []
        for k in range(1, N_DEV):
            peer, _ = _peer(k)
            cp = pltpu.make_async_remote_copy(
                src_ref=x_ref, dst_ref=buf.at[me], send_sem=send_sems.at[k], recv_sem=recv_sems.at[k],
                device_id=peer, device_id_type=MESH)
            cp.start()
            sends.append(cp)
        for k in range(1, N_DEV):
            peer, peer_flat = _peer(k)
            pltpu.make_async_remote_copy(
                src_ref=x_ref, dst_ref=buf.at[peer_flat], send_sem=send_sems.at[k], recv_sem=recv_sems.at[k],
                device_id=peer, device_id_type=MESH).wait_recv()
        for cp in sends:
            cp.wait_send()
        acc = buf[0]
        for d in range(1, N_DEV):
            acc = acc + buf[d]
        o_ref[...] = acc

    vmem = pl.BlockSpec(memory_space=pltpu.VMEM)
    return pl.pallas_call(
        body, name=name, out_shape=jax.ShapeDtypeStruct((rows, LANES), F32),
        in_specs=[vmem], out_specs=vmem,
        scratch_shapes=[pltpu.VMEM((N_DEV, rows, LANES), F32),
                        pltpu.SemaphoreType.DMA((N_DEV,)), pltpu.SemaphoreType.DMA((N_DEV,))],
    )(x)


def matmul(a, b, mode, name, out_dtypes=(F32,), epilogue=None, extra=None, tm=1024, tn=1024, tk=2048, after=(),
           b_shards=False, out_shards=False, k_group=1):
    if b_shards:
        n_sh, b_rows, b_cols = b.shape
    if mode == "nn":
        (m, kd), n = a.shape, (n_sh * b_cols if b_shards else b.shape[1])
        if b_shards:
            tn = b_cols
    elif mode == "nt":
        (m, kd), n = a.shape, (b_rows if b_shards else b.shape[0])
        if b_shards:
            tk = k_group * b_cols
    else:
        (kd, m), n = a.shape, b.shape[1]
    tm, tn, tk = min(tm, m), min(tn, n), min(tk, kd)
    assert m % tm == 0 and n % tn == 0 and kd % tk == 0, (name, m, n, kd, tm, tn, tk)
    ksteps = kd // tk
    dims = {"nn": NN, "nt": NT, "tn": TN}[mode]
    n_out = len(out_dtypes)
    n_in = 2 + (extra is not None) + len(after)

    def finish(acc, e_ref, o_refs):
        outs = (acc,) if epilogue is None else epilogue(acc, e_ref[...] if e_ref is not None else None)
        for o_ref, o in zip(o_refs, outs):
            o_ref[...] = o.astype(o_ref.dtype)

    def product(a_ref, b_ref):
        if mode == "nt" and b_shards:
            w = b_cols
            parts = [_dot(a_ref[:, s * w:(s + 1) * w], b_ref[s], dims) for s in range(k_group)]
            return functools.reduce(lambda p, q: p + q, parts)
        return _dot(a_ref[...], b_ref[...], dims)

    def body(*refs):
        a_ref, b_ref = refs[0], refs[1]
        e_ref = refs[2] if extra is not None else None
        o_refs = refs[n_in:n_in + n_out]
        if ksteps == 1:
            finish(product(a_ref, b_ref), e_ref, o_refs)
            return
        acc_ref = refs[-1]
        kk = pl.program_id(2)

        @pl.when(kk == 0)
        def _():
            acc_ref[...] = jnp.zeros_like(acc_ref)

        acc_ref[...] += product(a_ref, b_ref)

        @pl.when(kk == ksteps - 1)
        def _():
            finish(acc_ref[...], e_ref, o_refs)

    if mode == "nn":
        a_spec = pl.BlockSpec((tm, tk), lambda i, j, k: (i, k))
        b_spec = (pl.BlockSpec((None, tk, tn), lambda i, j, k: (j, k, 0)) if b_shards
                  else pl.BlockSpec((tk, tn), lambda i, j, k: (k, j)))
    elif mode == "nt":
        a_spec = pl.BlockSpec((tm, tk), lambda i, j, k: (i, k))
        b_spec = (pl.BlockSpec((k_group, tn, b_cols), lambda i, j, k: (k, j, 0)) if b_shards
                  else pl.BlockSpec((tn, tk), lambda i, j, k: (j, k)))
    else:
        a_spec = pl.BlockSpec((tk, tm), lambda i, j, k: (k, i))
        b_spec = pl.BlockSpec((tk, tn), lambda i, j, k: (k, j))
    o_spec = pl.BlockSpec((tm, tn), lambda i, j, k: (i, j))
    res_spec = pl.BlockSpec((None, tm, tn), lambda i, j, k: (j, i, 0)) if out_shards else o_spec
    res_shape = (n // tn, m, tn) if out_shards else (m, n)
    in_specs = [a_spec, b_spec] + ([o_spec] if extra is not None else []) + [ANY_SPEC] * len(after)
    args = (a, b) + ((extra,) if extra is not None else ()) + tuple(after)
    res = pl.pallas_call(
        body, name=name, grid=(m // tm, n // tn, ksteps),
        in_specs=in_specs, out_specs=[res_spec] * n_out,
        out_shape=[jax.ShapeDtypeStruct(res_shape, dt) for dt in out_dtypes],
        scratch_shapes=[pltpu.VMEM((tm, tn), F32)] if ksteps > 1 else [],
        compiler_params=_params(("parallel", "parallel", "arbitrary")),
    )(*args)
    return res if n_out > 1 else res[0]


GATE_COL = 4 * GDN_WIDTH
RELAYOUT_ROWS = 256


def _cat_of_win(j):
    if j < GATE_COL:
        return j
    if j < GATE_COL + 2 * N_HEADS:
        return MAIN_WIDTH + (j - GATE_COL)
    return j - 2 * N_HEADS


def _win_of_cat(c):
    if c < GATE_COL:
        return c
    if c < MAIN_WIDTH:
        return c + 2 * N_HEADS
    if c < MAIN_WIDTH + 2 * N_HEADS:
        return GATE_COL + (c - MAIN_WIDTH)
    return None


def _runs(first, count, mapping):
    runs, i = [], 0
    while i < count:
        start, n = mapping(first + i), 1
        while i + n < count and mapping(first + i + n) == start + n:
            n += 1
        runs.append((start, n))
        i += n
    return runs


def weights_to_cat(g_in):
    n_dev, rows, shard = g_in.shape

    def body(x_ref, o_ref):
        for b in range(CAT_WIDTH // LANES):
            live = sum(_win_of_cat(LANES * b + i) is not None for i in range(LANES))
            parts = []
            for start, n in _runs(LANES * b, live, _win_of_cat):
                while n > 0:
                    d, o = divmod(start, shard)
                    take = min(n, shard - o)
                    parts.append(x_ref[d, :, o:o + take])
                    start, n = start + take, n - take
            if live < LANES:
                parts.append(jnp.zeros((RELAYOUT_ROWS, LANES - live), g_in.dtype))
            o_ref[:, LANES * b:LANES * (b + 1)] = parts[0] if len(parts) == 1 else jnp.concatenate(parts, axis=1)

    return pl.pallas_call(
        body, name="weights_to_cat", grid=(rows // RELAYOUT_ROWS,),
        in_specs=[pl.BlockSpec((n_dev, RELAYOUT_ROWS, shard), lambda i: (0, i, 0))],
        out_specs=pl.BlockSpec((RELAYOUT_ROWS, CAT_WIDTH), lambda i: (i, 0)),
        out_shape=jax.ShapeDtypeStruct((rows, CAT_WIDTH), g_in.dtype),
        compiler_params=_params(("parallel",)))(g_in)


def cat_to_shards(dw_cat, shard):
    rows = dw_cat.shape[0]

    def body(x_ref, o_ref):
        for d in range(N_DEV):
            for t0 in range(0, shard, LANES):
                width = min(LANES, shard - t0)
                parts = [x_ref[:, c:c + n] for c, n in _runs(d * shard + t0, width, _cat_of_win)]
                o_ref[d, :, t0:t0 + width] = parts[0] if len(parts) == 1 else jnp.concatenate(parts, axis=1)

    return pl.pallas_call(
        body, name="cat_to_shards", grid=(rows // RELAYOUT_ROWS,),
        in_specs=[pl.BlockSpec((RELAYOUT_ROWS, CAT_WIDTH), lambda i: (i, 0))],
        out_specs=pl.BlockSpec((N_DEV, RELAYOUT_ROWS, shard), lambda i: (0, i, 0)),
        out_shape=jax.ShapeDtypeStruct((N_DEV, rows, shard), dw_cat.dtype),
        compiler_params=_params(("parallel",)))(dw_cat)


ROW_BLOCK = 512


def rms_fwd(x, w, name):
    t, d = x.shape

    def body(x_ref, w_ref, n_ref, r_ref):
        h = x_ref[...]
        r = lax.rsqrt(jnp.mean(h * h, axis=-1, keepdims=True) + NORM_EPS)
        n_ref[...] = (h * r * w_ref[...]).astype(BF16)
        r_ref[...] = r

    row = pl.BlockSpec((ROW_BLOCK, d), lambda i: (i, 0))
    return pl.pallas_call(
        body, name=name, grid=(t // ROW_BLOCK,),
        in_specs=[row, pl.BlockSpec((1, d), lambda i: (0, 0))],
        out_specs=[row, pl.BlockSpec((ROW_BLOCK, 1), lambda i: (i, 0))],
        out_shape=[jax.ShapeDtypeStruct((t, d), BF16), jax.ShapeDtypeStruct((t, 1), F32)],
        compiler_params=_params(("parallel",)))(x, w)


FUSED_ROWS = 512


def out_proj_rms(y, w_out, x, w_norm, name):
    t, d = x.shape

    def body(y_ref, w_ref, x_ref, g_ref, h_ref, n_ref, r_ref):
        h = x_ref[...] + _dot(y_ref[...], w_ref[...], NN)
        r = lax.rsqrt(jnp.mean(h * h, axis=-1, keepdims=True) + NORM_EPS)
        h_ref[...] = h
        n_ref[...] = (h * r * g_ref[...]).astype(BF16)
        r_ref[...] = r

    row = pl.BlockSpec((FUSED_ROWS, d), lambda i: (i, 0))
    return pl.pallas_call(
        body, name=name, grid=(t // FUSED_ROWS,),
        in_specs=[pl.BlockSpec((FUSED_ROWS, y.shape[1]), lambda i: (i, 0)), pl.BlockSpec(w_out.shape, lambda i: (0, 0)),
                  row, pl.BlockSpec((1, d), lambda i: (0, 0))],
        out_specs=[row, row, pl.BlockSpec((FUSED_ROWS, 1), lambda i: (i, 0))],
        out_shape=[jax.ShapeDtypeStruct((t, d), F32), jax.ShapeDtypeStruct((t, d), BF16),
                   jax.ShapeDtypeStruct((t, 1), F32)],
        compiler_params=_params(("parallel",)))(y, w_out, x, w_norm)


def ff2_loss(act, w_ff2, h1, w, target, name, tk=2048):
    t, d = h1.shape
    ksteps = act.shape[1] // tk

    def body(a_ref, b_ref, h_ref, w_ref, t_ref, loss_ref, dhb_ref, dw_ref, acc_ref):
        i, kk = pl.program_id(0), pl.program_id(1)

        @pl.when((i == 0) & (kk == 0))
        def _():
            loss_ref[...] = jnp.zeros_like(loss_ref)
            dw_ref[...] = jnp.zeros_like(dw_ref)

        @pl.when(kk == 0)
        def _():
            acc_ref[...] = h_ref[...]

        acc_ref[...] += _dot(a_ref[...], b_ref[...], NN)

        @pl.when(kk == ksteps - 1)
        def _():
            h = acc_ref[...]
            wv = w_ref[...]
            r = lax.rsqrt(jnp.mean(h * h, axis=-1, keepdims=True) + NORM_EPS)
            yn = h * r
            e = yn * wv - t_ref[...]
            loss_ref[...] += 0.5 * jnp.sum(jnp.sum(e * e, axis=-1, keepdims=True), axis=0, keepdims=True) / d
            dy = e / d
            dw_ref[...] += jnp.sum(dy * yn, axis=0, keepdims=True)
            dyn = dy * wv
            dhb_ref[...] = (r * (dyn - yn * jnp.mean(dyn * yn, axis=-1, keepdims=True))).astype(BF16)

    row = pl.BlockSpec((FUSED_ROWS, d), lambda i, k: (i, 0))
    wspec = pl.BlockSpec((1, d), lambda i, k: (0, 0))
    return pl.pallas_call(
        body, name=name, grid=(t // FUSED_ROWS, ksteps),
        in_specs=[pl.BlockSpec((FUSED_ROWS, tk), lambda i, k: (i, k)), pl.BlockSpec((tk, d), lambda i, k: (k, 0)),
                  row, wspec, row],
        out_specs=[pl.BlockSpec((1, 1), lambda i, k: (0, 0)), row, wspec],
        out_shape=[jax.ShapeDtypeStruct((1, 1), F32), jax.ShapeDtypeStruct((t, d), BF16),
                   jax.ShapeDtypeStruct((1, d), F32)],
        scratch_shapes=[pltpu.VMEM((FUSED_ROWS, d), F32)],
        compiler_params=_params(("arbitrary", "arbitrary")))(act, w_ff2, h1, w, target)


def rms_bwd(h, r, w, dn, dres, out_dtype, name):
    t, d = h.shape

    def body(h_ref, r_ref, w_ref, dn_ref, dres_ref, dh_ref, dw_ref):
        @pl.when(pl.program_id(0) == 0)
        def _():
            dw_ref[...] = jnp.zeros_like(dw_ref)

        rv = r_ref[...]
        yn = h_ref[...] * rv
        dnv = dn_ref[...].astype(F32)
        dw_ref[...] += jnp.sum(dnv * yn, axis=0, keepdims=True)
        dyn = dnv * w_ref[...]
        dh = dres_ref[...].astype(F32) + rv * (dyn - yn * jnp.mean(dyn * yn, axis=-1, keepdims=True))
        dh_ref[...] = dh.astype(out_dtype)

    row = pl.BlockSpec((ROW_BLOCK, d), lambda i: (i, 0))
    wspec = pl.BlockSpec((1, d), lambda i: (0, 0))
    rspec = pl.BlockSpec((ROW_BLOCK, 1), lambda i: (i, 0))
    return pl.pallas_call(
        body, name=name, grid=(t // ROW_BLOCK,),
        in_specs=[row, rspec, wspec, row, row], out_specs=[row, wspec],
        out_shape=[jax.ShapeDtypeStruct((t, d), out_dtype), jax.ShapeDtypeStruct((1, d), F32)],
        compiler_params=_params(("arbitrary",)))(h, r, w, dn, dres)


CONV_TB = 512
CONV_CB = 512
HALO = 8


def _silu(x):
    return x * jax.nn.sigmoid(x)


def _conv_pre(xcat, w, rows):
    acc = None
    for j in range(4):
        sh = 3 - j
        xs = xcat if sh == 0 else pltpu.roll(xcat, sh, 0)
        term = xs[HALO:HALO + rows] * w[j:j + 1, :]
        acc = term if acc is None else acc + term
    return acc


def conv_fwd(proj, conv_w, name):
    t = proj.shape[0]
    nb = CONV_TB // HALO

    def body(x_ref, prev_ref, w_ref, o_ref):
        prev = jnp.where(pl.program_id(1) == 0, 0.0, prev_ref[...])
        xcat = jnp.concatenate([prev, x_ref[...]], axis=0)
        o_ref[...] = _silu(_conv_pre(xcat, w_ref[...], CONV_TB))

    return pl.pallas_call(
        body, name=name, grid=(QKV_WIDTH // CONV_CB, t // CONV_TB),
        in_specs=[pl.BlockSpec((CONV_TB, CONV_CB), lambda c, i: (i, c)),
                  pl.BlockSpec((HALO, CONV_CB), lambda c, i: (jnp.maximum(i * nb - 1, 0), c)),
                  pl.BlockSpec((4, CONV_CB), lambda c, i: (0, c))],
        out_specs=pl.BlockSpec((CONV_TB, CONV_CB), lambda c, i: (i, c)),
        out_shape=jax.ShapeDtypeStruct((t, QKV_WIDTH), F32),
        compiler_params=_params(("parallel", "parallel")))(proj, proj, conv_w)


def conv_bwd(proj, dout, conv_w, dproj, name):
    t = proj.shape[0]
    nb = CONV_TB // HALO
    nt = t // CONV_TB
    rows = CONV_TB + HALO

    def body(x_ref, prev_ref, next_ref, d_ref, dnext_ref, w_ref, dproj_in, dx_ref, dw_ref):
        del dproj_in
        i = pl.program_id(1)

        @pl.when(i == 0)
        def _():
            dw_ref[...] = jnp.zeros_like(dw_ref)

        w = w_ref[...]
        prev = jnp.where(i == 0, 0.0, prev_ref[...])
        last = i == nt - 1
        xcat = jnp.concatenate([prev, x_ref[...], next_ref[...]], axis=0)
        pre = _conv_pre(xcat, w, rows)
        dcat = jnp.concatenate([d_ref[...], jnp.where(last, 0.0, dnext_ref[...])], axis=0)
        sg = jax.nn.sigmoid(pre)
        dpre = dcat * (sg * (1.0 + pre * (1.0 - sg)))
        dx = None
        for j in range(4):
            sh = 3 - j
            ds = dpre if sh == 0 else pltpu.roll(dpre, rows - sh, 0)
            term = ds[:CONV_TB] * w[j:j + 1, :]
            dx = term if dx is None else dx + term
        dx_ref[...] = dx.astype(BF16)
        dcur = dpre[:CONV_TB]
        parts = []
        for j in range(4):
            sh = 3 - j
            xs = xcat if sh == 0 else pltpu.roll(xcat, sh, 0)
            parts.append(jnp.sum(dcur * xs[HALO:HALO + CONV_TB], axis=0, keepdims=True))
        dw_ref[...] += jnp.concatenate(parts, axis=0)

    cur = pl.BlockSpec((CONV_TB, CONV_CB), lambda c, i: (i, c))
    halo_prev = pl.BlockSpec((HALO, CONV_CB), lambda c, i: (jnp.maximum(i * nb - 1, 0), c))
    halo_next = pl.BlockSpec((HALO, CONV_CB), lambda c, i: (jnp.minimum((i + 1) * nb, nt * nb - 1), c))
    taps = pl.BlockSpec((4, CONV_CB), lambda c, i: (0, c))
    return pl.pallas_call(
        body, name=name, grid=(QKV_WIDTH // CONV_CB, nt),
        in_specs=[cur, halo_prev, halo_next, cur, halo_next, taps, ANY_SPEC],
        out_specs=[cur, taps],
        out_shape=[jax.ShapeDtypeStruct(dproj.shape, BF16), jax.ShapeDtypeStruct((4, QKV_WIDTH), F32)],
        input_output_aliases={6: 0},
        compiler_params=_params(("parallel", "arbitrary")))(proj, proj, proj, dout, dout, conv_w, dproj)


def _iota2(shape, axis):
    return lax.broadcasted_iota(jnp.int32, shape, axis)


def _softplus(x):
    return jnp.maximum(x, 0.0) + jnp.log(1.0 + jnp.exp(-jnp.abs(x)))


def _head_norm_gate(o, norm_w, gate):
    return o * lax.rsqrt(jnp.mean(o * o, axis=-1, keepdims=True) + NORM_EPS) * norm_w * _silu(gate)


GDN_PREC = ("bf", "bf")
HGRN_PREC = "bf"


def _each(fn, *cols):
    return [fn(*a) for a in zip(*cols)]


@functools.partial(jax.custom_vjp, nondiff_argnums=(2,))
def _known_inverse(low, inv, prec):
    del low, prec
    return inv


def _known_inverse_fwd(low, inv, prec):
    del low
    return inv, inv


def _known_inverse_bwd(prec, inv, ct):
    return -_mm_raw(_mm_raw(inv, ct, TN, prec), inv, NT, prec), jnp.zeros_like(inv)


_known_inverse.defvjp(_known_inverse_fwd, _known_inverse_bwd)


def gdn_stages(hs, qc, kc, vc, zc, ab, a_log_l, dt_l, norm_w, s, prec=GDN_PREC, inv_known=None):
    p_inv, p_mm = prec
    c = CHUNK
    ri, ci = _iota2((c, c), 0), _iota2((c, c), 1)
    incl, strict, eye = ri >= ci, ri > ci, ri == ci
    lane = _iota2((c, LANES), 1)
    last_row = _iota2((c, 1), 0) == c - 1
    rowsum = lambda x: jnp.sum(x, axis=1, keepdims=True)

    def row(col):
        return jnp.sum(jnp.where(eye, col, 0.0), axis=0, keepdims=True)

    q = _each(lambda x: x * lax.rsqrt(rowsum(x * x) + L2_EPS) * (HEAD_DIM ** -0.5), qc)
    k = _each(lambda x: x * lax.rsqrt(rowsum(x * x) + L2_EPS), kc)
    yield
    a_col = [rowsum(jnp.where(lane == h, ab, 0.0)) for h in hs]
    b_col = [rowsum(jnp.where(lane == h + N_HEADS, ab, 0.0)) for h in hs]
    beta = _each(jax.nn.sigmoid, b_col)
    g = _each(lambda a, al, dl: rowsum(jnp.where(lane == 0, -jnp.exp(al) * _softplus(a + dl), 0.0)), a_col, a_log_l, dt_l)
    gcum = _each(lambda x: rowsum(jnp.where(incl, row(x), 0.0)), g)
    g_last = _each(lambda x: jnp.sum(jnp.where(last_row, x, 0.0), axis=0, keepdims=True), gcum)
    decay = _each(lambda x: jnp.exp(jnp.where(incl, x - row(x), -jnp.inf)), gcum)
    yield
    kk = _each(lambda x: mm(x, x, NT, p_mm), k)
    low = _each(lambda b, x, d: jnp.where(strict, b * x * d, 0.0), beta, kk, decay)
    yield
    if inv_known is None:
        power = _each(lambda x: -x, low)
        inv = _each(lambda x: jnp.where(eye, 1.0, 0.0) + x, power)
        for _ in range(5):
            power = _each(lambda x: mm(x, x, NN, p_inv), power)
            yield
            inv = _each(lambda x, p: x + mm(x, p, NN, p_inv), inv, power)
            yield
    else:
        inv = _each(lambda x, known: _known_inverse(x, known, p_inv), low, inv_known)
    exp_g = _each(jnp.exp, gcum)
    yield
    u_v = _each(lambda i, b, x: mm(i, b * x, NN, p_mm), inv, beta, vc)
    w = _each(lambda i, b, e, x: mm(i, b * e * x, NN, p_mm), inv, beta, exp_g, k)
    yield
    attn = _each(lambda x, y, d: mm(x, y, NT, p_mm) * d, q, k, decay)
    yield
    u = _each(lambda x, y, z: x - mm(y, z, NN, p_mm), u_v, w, s)
    yield
    o = _each(lambda x, e, z: mm(x * e, z, NN, p_mm), q, exp_g, s)
    o = _each(lambda x, a, y: x + mm(a, y, NN, p_mm), o, attn, u)
    yield
    k_end = _each(lambda x, gl, gc: x * jnp.exp(gl - gc), k, g_last, gcum)
    s_new = _each(lambda z, gl, x, y: z * jnp.exp(gl) + mm(x, y, TN, p_mm), s, g_last, k_end, u)
    return (_each(lambda x, z: _head_norm_gate(x, norm_w, z), o, zc), s_new), inv


def gdn_chunk(h, qc, kc, vc, zc, ab, a_log_l, dt_l, norm_w, s, prec=GDN_PREC, reuse_inverse=False):
    args = ([h], [qc], [kc], [vc], [zc], ab, [a_log_l], [dt_l], norm_w, [s], prec)
    if reuse_inverse:
        inv = lax.stop_gradient(gdn_chunks(*args)[1])
        (y, s_new), _ = gdn_chunks(*args, inv_known=inv)
    else:
        (y, s_new), _ = gdn_chunks(*args)
    return y[0], s_new[0]


DIAG_ROWS = SUB_CHUNK // 2
SHIFT_PAD = 8
SHIFT_ROWS = SHIFT_PAD + CHUNK + SHIFT_PAD
SHIFT_WAYS = 4


class RolledRows:
    def down(self, x, which):
        del which
        return [x] + [pltpu.roll(x, off, 0) for off in range(1, DIAG_ROWS)]

    def up_sum(self, parts, which):
        del which
        acc = parts[0]
        for off in range(1, DIAG_ROWS):
            acc = acc + pltpu.roll(parts[off], CHUNK - off, 0)
        return acc


class SlotRows:
    def __init__(self, slots):
        self.slots = slots

    def down(self, x, which):
        self.slots[which, 0, SHIFT_PAD:SHIFT_PAD + CHUNK, :] = x
        return [x] + [self.slots[which, 0, SHIFT_PAD - off:SHIFT_PAD + CHUNK - off, :] for off in range(1, DIAG_ROWS)]

    def up_sum(self, parts, which):
        acc = parts[0]
        for off in range(1, DIAG_ROWS):
            way = 1 + off % (SHIFT_WAYS - 1)
            self.slots[which, way, SHIFT_PAD:SHIFT_PAD + CHUNK, :] = parts[off]
            acc = acc + self.slots[which, way, SHIFT_PAD + off:SHIFT_PAD + CHUNK + off, :]
        return acc


def _sub_block_rows():
    return jnp.bitwise_and(_iota2((CHUNK, 1), 0), DIAG_ROWS - 1)


def _diag_forward(rows, q, key, bc, v):
    rmod = _sub_block_rows()
    k_d, b_d, v_d = rows.down(key, 0), rows.down(bc, 1), rows.down(v, 2)
    o = None
    for off in range(DIAG_ROWS):
        e = jnp.exp(jnp.where(rmod >= off, bc - b_d[off], -jnp.inf))
        term = jnp.sum(q * k_d[off] * e, axis=-1, keepdims=True) * v_d[off]
        o = term if o is None else o + term
    return o


def _diag_backward(rows, q, key, bc, v, do):
    rmod = _sub_block_rows()
    k_d, b_d, v_d = rows.down(key, 0), rows.down(bc, 1), rows.down(v, 2)
    dq = db = None
    dk_parts, db_parts, dv_parts = [], [], []
    for off in range(DIAG_ROWS):
        e = jnp.exp(jnp.where(rmod >= off, bc - b_d[off], -jnp.inf))
        qe = q * e
        a = jnp.sum(qe * k_d[off], axis=-1, keepdims=True)
        da = jnp.sum(do * v_d[off], axis=-1, keepdims=True)
        dv_parts.append(a * do)
        dq_term = (da * e) * k_d[off]
        dk_term = da * qe
        s = dk_term * k_d[off]
        dq = dq_term if dq is None else dq + dq_term
        db = s if db is None else db + s
        dk_parts.append(dk_term)
        db_parts.append(s)
    return dq, rows.up_sum(dk_parts, 0), db - rows.up_sum(db_parts, 1), rows.up_sum(dv_parts, 2)


def diag_part(rows, differentiable=True):
    forward = functools.partial(_diag_forward, rows)
    if not differentiable:
        return forward
    part = jax.custom_vjp(forward)
    part.defvjp(lambda q, key, bc, v: (forward(q, key, bc, v), (q, key, bc, v)),
                lambda res, do: _diag_backward(rows, *res, do))
    return part


def hgrn_stages(qb, fb, ib, gb, l0, l1, norm_w, st, prec=HGRN_PREC, diags=None, o_known=None):
    c = CHUNK
    ri, ci = _iota2((4 * c, c), 0), _iota2((4 * c, c), 1)
    rcol = _iota2((c, 1), 0)
    blk0 = jnp.bitwise_and(ri, c - SUB_CHUNK)
    limit = jnp.where(ri < c, ri + 1, jnp.where(ri < 2 * c, blk0, jnp.where(ri < 3 * c, blk0 + SUB_CHUNK,
                                                                          blk0 + DIAG_ROWS)))
    sel = jnp.where(ci < limit, 1.0, 0.0)
    ri, ci = _iota2((c, c), 0), _iota2((c, c), 1)
    lb = _each(lambda a, b: jax.nn.sigmoid(a - b), l0, l1)
    forget = _each(lambda b, f: b + (1.0 - b) * jax.nn.sigmoid(f), lb, fb)
    key = _each(lambda b, f: (1.0 - b) * jax.nn.sigmoid(-f), lb, fb)
    q = _each(_silu, qb)
    v = ib
    logf = _each(jnp.log, forget)
    sums = _each(lambda x: sel_sums(sel, x), logf)
    bc, b_start, b_end, b_half = ([x[i] for x in sums] for i in range(4))
    b_last = _each(lambda x: jnp.sum(x, axis=0, keepdims=True), logf)
    o = _each(lambda x, b, z: mm(x * jnp.exp(b), z, NT, prec), q, bc, st)
    if diags is None:
        diags = [diag_part(RolledRows())] * len(qb)
    yield
    o = list(o)
    for h in range(len(o)):
        o[h] = o[h] + diags[h](q[h], key[h], bc[h], v[h])
        yield
    second = jnp.bitwise_and(rcol, SUB_CHUNK - 1) >= DIAG_ROWS
    same_sub = jnp.bitwise_and(ri, c - SUB_CHUNK) == jnp.bitwise_and(ci, c - SUB_CHUNK)
    q_half = _each(lambda x, b, bh: x * jnp.exp(jnp.where(second, b - bh, -jnp.inf)), q, bc, b_half)
    k_half = _each(lambda x, b, bh: x * jnp.exp(jnp.where(second, -jnp.inf, bh - b)), key, bc, b_half)
    a_half = _each(lambda x, z: jnp.where(same_sub, mm(x, z, NT, prec), 0.0), q_half, k_half)
    o = _each(lambda acc, a, val: acc + mm(a, val, NN, prec), o, a_half, v)
    yield
    q_rel = _each(lambda x, b, bs: x * jnp.exp(b - bs), q, bc, b_start)
    k_rel = _each(lambda x, b, be: x * jnp.exp(be - b), key, bc, b_end)
    for y in range(c // SUB_CHUNK - 1):
        def scaled(x, b, bs):
            end_y = jnp.sum(jnp.where(rcol == SUB_CHUNK * y + SUB_CHUNK - 1, b, 0.0), axis=0, keepdims=True)
            return x * jnp.exp(jnp.where(rcol >= SUB_CHUNK * (y + 1), bs - end_y, -jnp.inf))
        dq = _each(scaled, q_rel, bc, b_start)
        in_y = (ci >= SUB_CHUNK * y) & (ci < SUB_CHUNK * (y + 1))
        a_y = _each(lambda x, z: jnp.where(in_y, mm(x, z, NT, prec), 0.0), dq, k_rel)
        o = _each(lambda acc, a, val: acc + mm(a, val, NN, prec), o, a_y, v)
        yield
    k_state = _each(lambda x, bl, b: x * jnp.exp(bl - b), key, b_last, bc)
    st_new = _each(lambda z, bl, val, x: z * jnp.exp(bl) + mm(val, x, TN, prec), st, b_last, v, k_state)
    if o_known is not None:
        o = _each(_known_value, o, o_known)
    return (_each(lambda x, z: _head_norm_gate(x, norm_w, z), o, gb), st_new), o


def _drain(gen):
    try:
        while True:
            next(gen)
    except StopIteration as done:
        return done.value


def _alternate(gen_a, gen_b):
    out, live = [None, None], [gen_a, gen_b]
    while any(g is not None for g in live):
        for i, g in enumerate(live):
            if g is None:
                continue
            try:
                next(g)
            except StopIteration as done:
                out[i], live[i] = done.value, None
    return out


def gdn_chunks(*args, **kwargs):
    return _drain(gdn_stages(*args, **kwargs))


def hgrn_chunks(*args, **kwargs):
    return _drain(hgrn_stages(*args, **kwargs))


def hgrn_chunk(qb, fb, ib, gb, l0, l1, norm_w, st, prec=HGRN_PREC, reuse_output=False):
    args = ([qb], [fb], [ib], [gb], [l0], [l1], norm_w, [st], prec)
    if reuse_output:
        known = lax.stop_gradient(hgrn_chunks(*args)[1])
        (y, st_new), _ = hgrn_chunks(*args, o_known=known)
    else:
        (y, st_new), _ = hgrn_chunks(*args)
    return y[0], st_new[0]


HEAD_VEC = (N_HEADS, 1, LANES)


class _ChunkSpecs:
    def __init__(self, nc, rev):
        self.nc, self.rev = nc, rev

    def _c(self, c):
        return self.nc - 1 - c if self.rev else c

    def row(self, width, block=0):
        return pl.BlockSpec((CHUNK, width), lambda c: (self._c(c), block))

    def per_head(self, rows):
        return pl.BlockSpec((None, N_HEADS, rows, rows), lambda c: (self._c(c), 0, 0, 0))

    @staticmethod
    def whole(shape):
        return pl.BlockSpec(shape, lambda c: (0,) * len(shape))


def _lanes(j):
    return slice(j * LANES, (j + 1) * LANES)


def mixer_fwd(qkv_c, proj, a_log_l, dt_l, gdn_norm_w, l0, l1, hgrn_norm_w, name):
    t = qkv_c.shape[0]
    hb = N_HEADS
    sp = _ChunkSpecs(t // CHUNK, rev=False)
    hs = list(range(hb))

    def body(q_ref, k_ref, v_ref, z_ref, ab_ref, al_ref, dt_ref, gnw_ref, qb_ref, fb_ref, ib_ref, gb_ref, l0_ref, l1_ref,
             hnw_ref, y_ref, hist_a_ref, inv_ref, hist_b_ref, o_ref, sa_ref, sb_ref, shift_ref):
        @pl.when(pl.program_id(0) == 0)
        def _():
            sa_ref[...] = jnp.zeros_like(sa_ref)
            sb_ref[...] = jnp.zeros_like(sb_ref)
            shift_ref[...] = jnp.zeros_like(shift_ref)

        heads = lambda ref: [ref[:, _lanes(j)] for j in hs]
        s_a, s_b = [sa_ref[h] for h in hs], [sb_ref[h] for h in hs]
        for h in hs:
            hist_a_ref[h] = s_a[h]
            hist_b_ref[h] = s_b[h]
        diags = [diag_part(SlotRows(shift_ref.at[h]), differentiable=False) for h in hs]
        ((y_a, s_a_new), inv), ((y_b, s_b_new), o_pre) = _alternate(
            gdn_stages(hs, heads(q_ref), heads(k_ref), heads(v_ref), heads(z_ref), ab_ref[...],
                       [al_ref[h] for h in hs], [dt_ref[h] for h in hs], gnw_ref[...], s_a),
            hgrn_stages(heads(qb_ref), heads(fb_ref), heads(ib_ref), heads(gb_ref),
                        [l0_ref[h] for h in hs], [l1_ref[h] for h in hs], hnw_ref[...], s_b, diags=diags))
        for h in hs:
            y_ref[:, _lanes(h)] = y_a[h].astype(BF16)
            y_ref[:, _lanes(hb + h)] = y_b[h].astype(BF16)
            o_ref[:, _lanes(h)] = o_pre[h]
            sa_ref[h] = s_a_new[h]
            sb_ref[h] = s_b_new[h]
            inv_ref[h] = inv[h]

    vec, gain, slab = sp.whole(HEAD_VEC), sp.whole((1, LANES)), functools.partial(sp.row, GDN_WIDTH)
    states = jax.ShapeDtypeStruct((sp.nc, N_HEADS, HEAD_DIM, HEAD_DIM), F32)
    return pl.pallas_call(
        body, name=name, grid=(sp.nc,),
        in_specs=[slab(0), slab(1), slab(2), slab(3), sp.row(LANES, AB_BLOCK), vec, vec, gain,
                  slab(4), slab(5), slab(6), slab(7), vec, vec, gain],
        out_specs=[sp.row(2 * GDN_WIDTH), sp.per_head(HEAD_DIM), sp.per_head(CHUNK), sp.per_head(HEAD_DIM), slab(0)],
        out_shape=[jax.ShapeDtypeStruct((t, 2 * GDN_WIDTH), BF16), states,
                   jax.ShapeDtypeStruct((sp.nc, N_HEADS, CHUNK, CHUNK), F32), states,
                   jax.ShapeDtypeStruct((t, GDN_WIDTH), F32)],
        scratch_shapes=[pltpu.VMEM((N_HEADS, HEAD_DIM, HEAD_DIM), F32), pltpu.VMEM((N_HEADS, HEAD_DIM, HEAD_DIM), F32),
                        pltpu.VMEM((hb, 3, SHIFT_WAYS, SHIFT_ROWS, LANES), F32)],
        compiler_params=_params(("arbitrary",)),
    )(qkv_c, qkv_c, qkv_c, proj, proj, a_log_l, dt_l, gdn_norm_w, proj, proj, proj, proj, l0, l1, hgrn_norm_w)


def mixer_bwd(qkv_c, proj, a_log_l, dt_l, gdn_norm_w, l0, l1, hgrn_norm_w, hist_a, inv_hist, hist_b, o_pre, dy, name):
    t = qkv_c.shape[0]
    hb = N_HEADS
    sp = _ChunkSpecs(t // CHUNK, rev=True)
    hs = list(range(hb))

    def body(q_ref, k_ref, v_ref, z_ref, ab_ref, al_ref, dt_ref, gnw_ref, qb_ref, fb_ref, ib_ref, gb_ref, l0_ref, l1_ref,
             hnw_ref, hist_a_ref, inv_ref, hist_b_ref, o_ref, dy_ref,
             dqkv_ref, dproj_ref, dal_ref, ddt_ref, dgnw_ref, dl0_ref, dl1_ref, dhnw_ref, dsa_ref, dsb_ref, shift_ref):
        @pl.when(pl.program_id(0) == 0)
        def _():
            for ref in (dal_ref, ddt_ref, dgnw_ref, dl0_ref, dl1_ref, dhnw_ref, dsa_ref, dsb_ref, shift_ref):
                ref[...] = jnp.zeros_like(ref)

        heads = lambda ref, first=0: [ref[:, _lanes(first + j)] for j in hs]
        diags = [diag_part(SlotRows(shift_ref.at[h])) for h in hs]
        inv_known, o_known = [inv_ref[h] for h in hs], heads(o_ref)

        def both(ga, gb):
            (ra, inv), (rb, o_pre) = _alternate(gdn_stages(hs, *ga, inv_known=inv_known),
                                                hgrn_stages(*gb, diags=diags, o_known=o_known))
            return (ra, rb), (inv, o_pre)

        ga = (heads(q_ref), heads(k_ref), heads(v_ref), heads(z_ref), ab_ref[...], [al_ref[h] for h in hs],
              [dt_ref[h] for h in hs], gnw_ref[...], [hist_a_ref[h] for h in hs])
        gb = (heads(qb_ref), heads(fb_ref), heads(ib_ref), heads(gb_ref), [l0_ref[h] for h in hs],
              [l1_ref[h] for h in hs], hnw_ref[...], [hist_b_ref[h] for h in hs])
        _, vjp, _ = jax.vjp(both, ga, gb, has_aux=True)
        dy_a = [x.astype(F32) for x in heads(dy_ref)]
        dy_b = [x.astype(F32) for x in heads(dy_ref, hb)]
        (dq, dk, dv, dz, dab, dal, ddt, dgnw, ds_a), (dqb, dfb, dib, dgb, dl0, dl1, dhnw, ds_b) = vjp(
            ((dy_a, [dsa_ref[h] for h in hs]), (dy_b, [dsb_ref[h] for h in hs])))
        for h in hs:
            dqkv_ref[:, _lanes(h)] = dq[h]
            dqkv_ref[:, _lanes(hb + h)] = dk[h]
            dqkv_ref[:, _lanes(2 * hb + h)] = dv[h]
            for slab, val in enumerate((dz, dqb, dfb, dib, dgb)):
                dproj_ref[:, _lanes((3 + slab) * hb + h)] = val[h].astype(BF16)
            dal_ref[h] += dal[h]
            ddt_ref[h] += ddt[h]
            dl0_ref[h] += dl0[h]
            dl1_ref[h] += dl1[h]
            dsa_ref[h] = ds_a[h]
            dsb_ref[h] = ds_b[h]
        dproj_ref[:, MAIN_WIDTH:] = dab.astype(BF16)
        dgnw_ref[...] += dgnw
        dhnw_ref[...] += dhnw

    vec, gain, slab = sp.whole(HEAD_VEC), sp.whole((1, LANES)), functools.partial(sp.row, GDN_WIDTH)
    vec_shape, gain_shape = jax.ShapeDtypeStruct(HEAD_VEC, F32), jax.ShapeDtypeStruct((1, LANES), F32)
    return pl.pallas_call(
        body, name=name, grid=(sp.nc,),
        in_specs=[slab(0), slab(1), slab(2), slab(3), sp.row(LANES, AB_BLOCK), vec, vec, gain,
                  slab(4), slab(5), slab(6), slab(7), vec, vec, gain,
                  sp.per_head(HEAD_DIM), sp.per_head(CHUNK), sp.per_head(HEAD_DIM), slab(0), sp.row(2 * GDN_WIDTH)],
        out_specs=[sp.row(QKV_WIDTH), sp.row(CAT_WIDTH), vec, vec, gain, vec, vec, gain],
        out_shape=[jax.ShapeDtypeStruct((t, QKV_WIDTH), F32), jax.ShapeDtypeStruct((t, CAT_WIDTH), BF16),
                   vec_shape, vec_shape, gain_shape, vec_shape, vec_shape, gain_shape],
        scratch_shapes=[pltpu.VMEM((N_HEADS, HEAD_DIM, HEAD_DIM), F32), pltpu.VMEM((N_HEADS, HEAD_DIM, HEAD_DIM), F32),
                        pltpu.VMEM((hb, 3, SHIFT_WAYS, SHIFT_ROWS, LANES), F32)],
        compiler_params=_params(("arbitrary",)),
    )(qkv_c, qkv_c, qkv_c, proj, proj, a_log_l, dt_l, gdn_norm_w, proj, proj, proj, proj, l0, l1, hgrn_norm_w,
      hist_a, inv_hist, hist_b, o_pre, dy)


def _adamw(w, g, m, v):
    m = ADAM_B1 * m + (1.0 - ADAM_B1) * g
    v = ADAM_B2 * v + (1.0 - ADAM_B2) * jnp.square(g)
    m_hat = m / (1.0 - ADAM_B1 ** ADAM_STEP)
    v_hat = v / (1.0 - ADAM_B2 ** ADAM_STEP)
    delta = -ADAM_LR * (m_hat / (jnp.sqrt(v_hat) + ADAM_EPS) + ADAM_WD * w)
    return delta, m, v


def adamw_reduce(parts, w, m, v, name, rb=128):
    r, c = w.shape
    rb = min(rb, r)

    def body(p_ref, w_ref, m_ref, v_ref, g_ref, d_ref, mo_ref, vo_ref):
        g = p_ref[0].astype(F32)
        for d in range(1, N_DEV):
            g = g + p_ref[d].astype(F32)
        delta, mn, vn = _adamw(w_ref[...], g, m_ref[...], v_ref[...])
        g_ref[...] = g
        d_ref[...] = delta
        mo_ref[...] = mn
        vo_ref[...] = vn

    blk = pl.BlockSpec((rb, c), lambda i: (i, 0))
    return pl.pallas_call(
        body, name=name, grid=(r // rb,),
        in_specs=[pl.BlockSpec((N_DEV, rb, c), lambda i: (0, i, 0)), blk, blk, blk],
        out_specs=[blk] * 4, out_shape=[jax.ShapeDtypeStruct((r, c), F32)] * 4,
        compiler_params=_params(("parallel",)))(parts, w, m, v)


def adamw_small(w, g, m, v, name):
    def body(w_ref, g_ref, m_ref, v_ref, d_ref, mo_ref, vo_ref):
        delta, mn, vn = _adamw(w_ref[...], g_ref[...], m_ref[...], v_ref[...])
        d_ref[...] = delta
        mo_ref[...] = mn
        vo_ref[...] = vn

    vmem = pl.BlockSpec(memory_space=pltpu.VMEM)
    return pl.pallas_call(body, name=name, in_specs=[vmem] * 4, out_specs=[vmem] * 3,
                          out_shape=[jax.ShapeDtypeStruct(w.shape, F32)] * 3)(w, g, m, v)


def _pack(arrays):
    flat = jnp.concatenate([a.reshape(-1).astype(F32) for a in arrays])
    rows = -(-flat.shape[0] // (8 * LANES)) * 8
    return jnp.pad(flat, (0, rows * LANES - flat.shape[0])).reshape(rows, LANES)


def _unpack(packed, shapes):
    flat, out, off = packed.reshape(-1), [], 0
    for s in shapes:
        n = 1
        for d in s:
            n *= d
        out.append(flat[off:off + n].reshape(s))
        off += n
    return out


def _relu2_epilogue(acc, _):
    r = jnp.maximum(acc, 0.0)
    return acc, r * r


def _relu2_bwd_epilogue(acc, a1):
    return (acc * (2.0 * jnp.maximum(a1, 0.0)),)


def kernel(x, w_in, conv_w, gdn_a_log, gdn_dt_bias, gdn_norm_w, hgrn_lb_logits, hgrn_norm_w, w_out, norm_mix_w, norm_ffn_w, w_ff1, w_ff2, norm_final_w, loss_target, m_w_in, m_conv_w, m_gdn_a_log, m_gdn_dt_bias, m_gdn_norm_w, m_hgrn_lb_logits, m_hgrn_norm_w, m_w_out, m_norm_mix_w, m_norm_ffn_w, m_w_ff1, m_w_ff2, m_norm_final_w, v_w_in, v_conv_w, v_gdn_a_log, v_gdn_dt_bias, v_gdn_norm_w, v_hgrn_lb_logits, v_hgrn_norm_w, v_w_out, v_norm_mix_w, v_norm_ffn_w, v_w_ff1, v_w_ff2, v_norm_final_w):
    me = _my_flat()
    xs = x[0]
    target = loss_target[0]
    shard_in = w_in.shape[2]
    shard_conv = conv_w.shape[2]

    tok = lambda t: t[0:1, 0:1]
    own = lambda src: lax.dynamic_index_in_dim(src, me, 0, keepdims=False)

    g_in, g_conv = gather_two_level([w_in[0].astype(BF16), conv_w[0]], "gather_w_in")
    h_g1, t_g1 = exchange_start([w_out[0].astype(BF16), w_ff1[0].astype(BF16)], True, "gather_mid_start", after=[g_in],
                                peers=CHIP_PEERS)
    h_g2, t_g2 = exchange_start([w_ff2[0].astype(BF16)], True, "gather_ff2_start", after=[t_g1], peers=CHIP_PEERS)
    w_cat = weights_to_cat(g_in)
    conv_full = jnp.transpose(g_conv, (1, 0, 2)).reshape(4, QKV_WIDTH)

    lane_b = lambda p: jnp.broadcast_to(p.reshape(N_HEADS, 1, 1), HEAD_VEC)
    a_log_l, dt_l = lane_b(gdn_a_log[0]), lane_b(gdn_dt_bias[0])
    l0 = hgrn_lb_logits[0].reshape(HEAD_VEC)
    l1 = hgrn_lb_logits[1].reshape(HEAD_VEC)

    n1, r1 = rms_fwd(xs, norm_mix_w + tok(t_g1) + tok(t_g2), "rms_mix")
    proj = matmul(n1, w_cat, "nn", "in_proj", tn=CAT_WIDTH // 5)
    qkv_c = conv_fwd(proj, conv_full, "conv_fwd")
    y, hist_a, inv_a, hist_b, o_b = mixer_fwd(qkv_c, proj, a_log_l, dt_l, gdn_norm_w, l0, l1, hgrn_norm_w, "mixer_fwd")
    (s_out, s_ff1), (l_out, l_ff1) = exchange_wait(h_g1, "gather_mid_wait", after=[y], copies=len(CHIP_PEERS))
    (s_ff2,), (l_ff2,) = exchange_wait(h_g2, "gather_ff2_wait", after=[y], copies=len(CHIP_PEERS))
    h_fw, _ = forward_start([l_out, l_ff1, l_ff2], "gather_forward_start")
    _, (l_out,) = exchange_wait(_one(h_fw, 0), "forward_out_wait", copies=len(OTHER_CHIPS))
    w_out_full = _own_slot(l_out, s_out).reshape(D_MODEL, D_MODEL)
    h1, n2, r2 = out_proj_rms(y, w_out_full, xs, norm_ffn_w, "out_proj_rms")
    _, (l_ff1,) = exchange_wait(_one(h_fw, 1), "forward_ff1_wait", after=[n2], copies=len(OTHER_CHIPS))
    w_ff1_sh = _own_slot(l_ff1, s_ff1)
    a1, act = matmul(n2, w_ff1_sh, "nn", "ff1", out_dtypes=(F32, BF16), epilogue=_relu2_epilogue, b_shards=True)
    _, (l_ff2,) = exchange_wait(_one(h_fw, 2), "forward_ff2_wait", after=[act], copies=len(OTHER_CHIPS))
    w_ff2_full = _own_slot(l_ff2, s_ff2).reshape(D_FF, D_MODEL)
    loss_sum, dh2_b, d_final = ff2_loss(act, w_ff2_full, h1, norm_final_w.reshape(1, D_MODEL), target, "ff2_loss")

    da1 = matmul(dh2_b, w_ff2_full, "nt", "d_act", out_dtypes=(BF16,), epilogue=_relu2_bwd_epilogue, extra=a1)
    t_all = xs.shape[0]
    dw_ff2 = matmul(act, dh2_b, "tn", "dw_ff2", out_dtypes=(BF16,), tk=t_all)
    p_ff2 = dw_ff2.reshape(N_DEV, D_FF // N_DEV, D_MODEL)
    h_s1, t_s1 = exchange_start([p_ff2], False, "scatter_ff2_start")
    dn2 = matmul(da1, w_ff1_sh, "nt", "d_n2", out_dtypes=(BF16,), after=[t_s1], b_shards=True, k_group=4)
    p_ff1 = matmul(n2, da1, "tn", "dw_ff1", out_dtypes=(BF16,), tn=D_FF // N_DEV, tk=t_all, after=[t_s1], out_shards=True)
    h_s2, t_s2 = exchange_start([p_ff1], False, "scatter_ff1_start")
    dh1_b, d_ffn = rms_bwd(h1, r2, norm_ffn_w + tok(t_s2), dn2, dh2_b, BF16, "rms_ffn_bwd")
    dmix = matmul(dh1_b, w_out_full, "nt", "d_mix", out_dtypes=(BF16,))
    dw_out = matmul(y, dh1_b, "tn", "dw_out", out_dtypes=(BF16,), tk=t_all)
    p_out = dw_out.reshape(N_DEV, D_MODEL // N_DEV, D_MODEL)
    h_s3, t_s3 = exchange_start([p_out], False, "scatter_out_start")
    d_qkv_c, dproj, d_alog_l, d_dt_l, d_gnw, dl0, dl1, d_hnw = mixer_bwd(
        qkv_c, proj, a_log_l, dt_l, gdn_norm_w + tok(t_s3), l0, l1, hgrn_norm_w, hist_a, inv_a, hist_b, o_b, dmix,
        "mixer_bwd")
    dproj, d_conv_full = conv_bwd(proj, d_qkv_c, conv_full, dproj, "conv_bwd")
    dw_cat = matmul(n1, dproj, "tn", "dw_in", out_dtypes=(BF16,), tm=512, tn=CAT_WIDTH // 5, tk=t_all)
    p_in = cat_to_shards(dw_cat, shard_in)
    h_s4, t_s4 = exchange_start([p_in], False, "scatter_in_start")

    (s_ff2g,), (r_ff2,) = exchange_wait(h_s1, "scatter_ff2_wait", after=[t_s4])
    (s_ff1g,), (r_ff1,) = exchange_wait(h_s2, "scatter_ff1_wait", after=[t_s4])
    (s_outg,), (r_out,) = exchange_wait(h_s3, "scatter_out_wait", after=[t_s4])
    g_w_ff2, d_w_ff2, nm_w_ff2, nv_w_ff2 = adamw_reduce(
        _own_slot(r_ff2, own(s_ff2g)), w_ff2[0], m_w_ff2[0], v_w_ff2[0], "adamw_w_ff2")
    g_w_ff1, d_w_ff1, nm_w_ff1, nv_w_ff1 = adamw_reduce(
        _own_slot(r_ff1, own(s_ff1g)), w_ff1[0], m_w_ff1[0], v_w_ff1[0], "adamw_w_ff1")
    g_w_out, d_w_out, nm_w_out, nv_w_out = adamw_reduce(
        _own_slot(r_out, own(s_outg)), w_out[0], m_w_out[0], v_w_out[0], "adamw_w_out")
    dn1 = matmul(dproj, w_cat, "nt", "d_n1", out_dtypes=(BF16,), tk=CAT_WIDTH // 5, after=[t_s4])
    dx, d_mix = rms_bwd(xs, r1, norm_mix_w, dn1, dh1_b, F32, "rms_mix_bwd")
    (s_ing,), (r_in,) = exchange_wait(h_s4, "scatter_in_wait", after=[dx, d_w_ff2, d_w_ff1, d_w_out])
    g_w_in, d_w_in, nm_w_in, nv_w_in = adamw_reduce(
        _own_slot(r_in, own(s_ing)), w_in[0], m_w_in[0], v_w_in[0], "adamw_w_in")

    d_lb = jnp.stack([dl0.reshape(GDN_WIDTH), dl1.reshape(GDN_WIDTH)])
    small_shapes = [(1, N_HEADS), (1, N_HEADS), (1, HEAD_DIM), (2, GDN_WIDTH), (1, HEAD_DIM), (1, D_MODEL),
                    (1, D_MODEL), (D_MODEL,), (4, QKV_WIDTH)]
    small = _pack([d_alog_l[:, 0, 0], d_dt_l[:, 0, 0], d_gnw, d_lb, d_hnw, d_mix, d_ffn, d_final, d_conv_full])
    red = allreduce_small(small, "allreduce_small")
    g_alog, g_dt, g_gnw, g_lb, g_hnw, g_mix, g_ffn, g_final, g_conv_full = _unpack(red, small_shapes)
    g_conv = lax.dynamic_slice(g_conv_full, (0, me * shard_conv), (4, shard_conv)).reshape(1, 4, shard_conv)
    small_g = [g_alog, g_dt, g_gnw, g_lb, g_hnw, g_mix, g_ffn, g_final, g_conv]
    small_w = [gdn_a_log, gdn_dt_bias, gdn_norm_w, hgrn_lb_logits, hgrn_norm_w, norm_mix_w, norm_ffn_w, norm_final_w, conv_w]
    small_m = [m_gdn_a_log, m_gdn_dt_bias, m_gdn_norm_w, m_hgrn_lb_logits, m_hgrn_norm_w, m_norm_mix_w, m_norm_ffn_w,
               m_norm_final_w, m_conv_w]
    small_v = [v_gdn_a_log, v_gdn_dt_bias, v_gdn_norm_w, v_hgrn_lb_logits, v_hgrn_norm_w, v_norm_mix_w, v_norm_ffn_w,
               v_norm_final_w, v_conv_w]
    shapes = [a.shape for a in small_w]
    d_s, m_s, v_s = adamw_small(_pack(small_w), _pack(small_g), _pack(small_m), _pack(small_v), "adamw_small")
    d_alog, d_dt, d_gn, d_lbl, d_hn, d_nm, d_nf, d_nfin, d_cw = _unpack(d_s, shapes)
    m_alog, m_dt, m_gn, m_lbl, m_hn, m_nm, m_nf, m_nfin, m_cw = _unpack(m_s, shapes)
    v_alog, v_dt, v_gn, v_lbl, v_hn, v_nm, v_nf, v_nfin, v_cw = _unpack(v_s, shapes)

    loss = lax.psum(loss_sum[0, 0], ("x", "y", "c"))
    lead = lambda a: a[None]
    grads = [lead(g_w_in), g_conv, g_alog, g_dt, g_gnw, g_lb, g_hnw, lead(g_w_out), g_mix, g_ffn,
             lead(g_w_ff1), lead(g_w_ff2), g_final]
    deltas = [lead(d_w_in), d_cw, d_alog, d_dt, d_gn, d_lbl, d_hn, lead(d_w_out), d_nm, d_nf,
              lead(d_w_ff1), lead(d_w_ff2), d_nfin]
    new_m = [lead(nm_w_in), m_cw, m_alog, m_dt, m_gn, m_lbl, m_hn, lead(nm_w_out), m_nm, m_nf,
             lead(nm_w_ff1), lead(nm_w_ff2), m_nfin]
    new_v = [lead(nv_w_in), v_cw, v_alog, v_dt, v_gn, v_lbl, v_hn, lead(nv_w_out), v_nm, v_nf,
             lead(nv_w_ff1), lead(nv_w_ff2), v_nfin]
    return (loss, dx[None], *grads, *deltas, *new_m, *new_v)
```

```python
import functools

import jax
import jax.numpy as jnp
from jax import lax
from jax.experimental import pallas as pl
from jax.experimental.pallas import tpu as pltpu

F32 = jnp.float32
BF16 = jnp.bfloat16
HI = lax.Precision.HIGHEST

N_DEV = 8
D_MODEL = 2048
CHUNK = 64
SUB_CHUNK = 16
HEAD_DIM = 128
N_HEADS = 8
GDN_WIDTH = N_HEADS * HEAD_DIM
D_FF = 4 * D_MODEL
QKV_WIDTH = 3 * GDN_WIDTH
MAIN_WIDTH = 8 * GDN_WIDTH
CAT_WIDTH = MAIN_WIDTH + 128
AB_BLOCK = MAIN_WIDTH // 128
NORM_EPS = 1e-6
L2_EPS = 1e-6
LANES = 128
VMEM_LIMIT = 56 * 1024 * 1024

ADAM_LR = 0.001
ADAM_B1 = 0.9
ADAM_B2 = 0.999
ADAM_EPS = 1e-08
ADAM_WD = 0.01
ADAM_STEP = 10

MESH = pl.DeviceIdType.MESH


def _params(sem=None):
    return pltpu.CompilerParams(dimension_semantics=sem, vmem_limit_bytes=VMEM_LIMIT)


def _dot(a, b, dims, prec=None):
    return lax.dot_general(a, b, (dims, ((), ())), precision=prec, preferred_element_type=F32)


NN = ((1,), (0,))
NT = ((1,), (1,))
TN = ((0,), (0,))


def _split_bf16(x, pieces):
    out = []
    for _ in range(pieces - 1):
        p = x.astype(BF16)
        out.append(p)
        x = x - p.astype(F32)
    out.append(x.astype(BF16))
    return out


def _mm_raw(a, b, dims, prec):
    if prec == "hi":
        return _dot(a, b, dims, HI)
    if prec == "bf":
        return _dot(a.astype(BF16), b.astype(BF16), dims)
    a_hi, a_lo = _split_bf16(a, 2)
    b_hi, b_lo = _split_bf16(b, 2)
    return _dot(a_hi, b_hi, dims) + (_dot(a_hi, b_lo, dims) + _dot(a_lo, b_hi, dims))


@functools.partial(jax.custom_vjp, nondiff_argnums=(2, 3))
def mm(a, b, dims, prec):
    return _mm_raw(a, b, dims, prec)


def _mm_fwd(a, b, dims, prec):
    return _mm_raw(a, b, dims, prec), (a, b)


def _mm_bwd(dims, prec, res, ct):
    a, b = res
    if dims == NN:
        return _mm_raw(ct, b, NT, prec), _mm_raw(a, ct, TN, prec)
    if dims == NT:
        return _mm_raw(ct, b, NN, prec), _mm_raw(ct, a, TN, prec)
    return _mm_raw(b, ct, NT, prec), _mm_raw(a, ct, NN, prec)


mm.defvjp(_mm_fwd, _mm_bwd)


def _sel_raw(sel, x, dims):
    sel = sel.astype(BF16)
    p0, p1, p2 = _split_bf16(x, 3)
    return _dot(sel, p0, dims) + (_dot(sel, p1, dims) + _dot(sel, p2, dims))


def _sel_parts(sel, x):
    c = x.shape[0]
    full = _sel_raw(sel, x, NN)
    return tuple(full[i * c:(i + 1) * c] for i in range(sel.shape[0] // c))


@jax.custom_vjp
def sel_sums(sel, x):
    return _sel_parts(sel, x)


def _sel_fwd(sel, x):
    return _sel_parts(sel, x), sel


def _sel_bwd(sel, cts):
    return jnp.zeros_like(sel), _sel_raw(sel, jnp.concatenate(cts, axis=0), TN)


sel_sums.defvjp(_sel_fwd, _sel_bwd)


@jax.custom_vjp
def _known_value(computed, known):
    del computed
    return known


_known_value.defvjp(lambda computed, known: (known, None), lambda _, ct: (ct, jnp.zeros_like(ct)))


def _my_flat():
    return 4 * lax.axis_index("x") + 2 * lax.axis_index("y") + lax.axis_index("c")


def _peer(k):
    x, y, c = lax.axis_index("x"), lax.axis_index("y"), lax.axis_index("c")
    kx, ky, kc = (k >> 2) & 1, (k >> 1) & 1, k & 1
    px = (1 - x) if kx else x
    py = (1 - y) if ky else y
    pc = (1 - c) if kc else c
    return (px, py, pc), 4 * px + 2 * py + pc


def gather_two_level(xs, name):
    n = len(xs)

    def body(*refs):
        x_refs, y_refs = refs[:n], refs[n:2 * n]
        send_sems, recv_sems, local_sems = refs[2 * n:]
        x, y, c = lax.axis_index("x"), lax.axis_index("y"), lax.axis_index("c")
        me, sibling = (x, y, c), (x, y, 1 - c)
        chips = [(1 - x, y), (x, 1 - y), (1 - x, 1 - y)]
        flat = lambda p: 4 * p[0] + 2 * p[1] + p[2]

        def copy(a, k, block, to, src=None):
            return pltpu.make_async_remote_copy(
                src_ref=y_refs[a].at[flat(block)] if src is None else src, dst_ref=y_refs[a].at[flat(block)],
                send_sem=send_sems.at[a, k], recv_sem=recv_sems.at[a, k], device_id=to, device_id_type=MESH)

        mine = [pltpu.make_async_copy(x_refs[a], y_refs[a].at[flat(me)], local_sems.at[a]) for a in range(n)]
        for cp in mine:
            cp.start()
        first = [copy(a, 0, me, sibling, src=x_refs[a]) for a in range(n)]
        first += [copy(a, 1 + j, me, (*chip, c), src=x_refs[a]) for j, chip in enumerate(chips) for a in range(n)]
        for cp in first:
            cp.start()
        passed = []
        for j, chip in enumerate(chips):
            for a in range(n):
                copy(a, 1 + j, (*chip, c), me).wait_recv()
                cp = copy(a, 4 + j, (*chip, c), sibling)
                cp.start()
                passed.append(cp)
        for a in range(n):
            copy(a, 0, sibling, me).wait_recv()
        for j, chip in enumerate(chips):
            for a in range(n):
                copy(a, 4 + j, (*chip, 1 - c), me).wait_recv()
        for cp in first + passed:
            cp.wait_send()
        for cp in mine:
            cp.wait()

    any_spec = pl.BlockSpec(memory_space=pl.ANY)
    return pl.pallas_call(
        body, name=name, out_shape=[jax.ShapeDtypeStruct((N_DEV,) + x.shape, x.dtype) for x in xs],
        in_specs=[any_spec] * n, out_specs=[any_spec] * n,
        scratch_shapes=[pltpu.SemaphoreType.DMA((n, N_DEV - 1)), pltpu.SemaphoreType.DMA((n, N_DEV - 1)),
                        pltpu.SemaphoreType.DMA((n,))],
    )(*xs)


HBM_SPEC = pl.BlockSpec(memory_space=pltpu.HBM)
SEM_SPEC = pl.BlockSpec(memory_space=pltpu.SEMAPHORE)
ANY_SPEC = pl.BlockSpec(memory_space=pl.ANY)
DATAFLOW = pltpu.SideEffectType.DATAFLOW_SIDE_EFFECTING


def _in_hbm(x):
    return pltpu.with_memory_space_constraint(x, pltpu.HBM)


ALL_PEERS = tuple(range(1, N_DEV))
CHIP_PEERS = (1, 2, 4, 6)
OTHER_CHIPS = (2, 4, 6)


def exchange_start(xs, gather, name, after=(), peers=ALL_PEERS):
    n, n_after = len(xs), len(after)

    def body(*refs):
        x_refs, land_refs = refs[:n], refs[n:2 * n]
        sems = refs[2 * n + n_after:2 * n + n_after + 2 * n]
        token = refs[-1]
        me = _my_flat()
        for k in peers:
            peer, peer_flat = _peer(k)
            for a in range(n):
                src = x_refs[a] if gather else x_refs[a].at[peer_flat]
                pltpu.make_async_remote_copy(src_ref=src, dst_ref=land_refs[a].at[me], send_sem=sems[a],
                                             recv_sem=sems[n + a], device_id=peer, device_id_type=MESH).start()
        token[...] = jnp.zeros_like(token)

    lands = [_in_hbm(lax.empty(((N_DEV,) + x.shape) if gather else x.shape, x.dtype)) for x in xs]
    hbm_out = [pltpu.HBM(x.shape, x.dtype) for x in xs] + [pltpu.HBM(l.shape, l.dtype) for l in lands]
    res = pl.pallas_call(
        body, name=name,
        out_shape=(*([pltpu.SemaphoreType.DMA(())] * (2 * n)), *hbm_out, jax.ShapeDtypeStruct((8, LANES), F32)),
        in_specs=[HBM_SPEC] * (2 * n) + [ANY_SPEC] * n_after,
        out_specs=(*([SEM_SPEC] * (2 * n)), *([HBM_SPEC] * (2 * n)), pl.BlockSpec(memory_space=pltpu.VMEM)),
        input_output_aliases={i: 2 * n + i for i in range(2 * n)},
        compiler_params=pltpu.CompilerParams(has_side_effects=DATAFLOW),
    )(*[_in_hbm(x) for x in xs], *lands, *after)
    return (list(res[:2 * n]), list(res[2 * n:3 * n]), list(res[3 * n:4 * n])), res[-1]


def forward_start(lands, name, after=()):
    n, n_after = len(lands), len(after)

    def body(*refs):
        land_refs = refs[:n]
        sems = refs[n + n_after:n + n_after + 2 * n]
        token = refs[-1]
        sibling, _ = _peer(1)
        for a in range(n):
            for k in OTHER_CHIPS:
                _, from_flat = _peer(k)
                slot = land_refs[a].at[from_flat]
                pltpu.make_async_remote_copy(src_ref=slot, dst_ref=slot, send_sem=sems[a], recv_sem=sems[n + a],
                                             device_id=sibling, device_id_type=MESH).start()
        token[...] = jnp.zeros_like(token)

    res = pl.pallas_call(
        body, name=name,
        out_shape=(*([pltpu.SemaphoreType.DMA(())] * (2 * n)), *[pltpu.HBM(l.shape, l.dtype) for l in lands],
                   jax.ShapeDtypeStruct((8, LANES), F32)),
        in_specs=[HBM_SPEC] * n + [ANY_SPEC] * n_after,
        out_specs=(*([SEM_SPEC] * (2 * n)), *([HBM_SPEC] * n), pl.BlockSpec(memory_space=pltpu.VMEM)),
        input_output_aliases={i: 2 * n + i for i in range(n)},
        compiler_params=pltpu.CompilerParams(has_side_effects=DATAFLOW),
    )(*lands, *after)
    return (list(res[:2 * n]), [], list(res[2 * n:3 * n])), res[-1]


def exchange_wait(handle, name, after=(), copies=N_DEV - 1):
    sems, xs, lands = handle
    n, n_x, n_after = len(lands), len(xs), len(after)

    def body(*refs):
        land_refs = refs[n_x:n_x + n]
        sem_refs = refs[n_x + n:n_x + 3 * n]
        for a in range(n):
            every = land_refs[a].at[pl.ds(0, copies)]
            cp = pltpu.make_async_remote_copy(src_ref=every, dst_ref=every, send_sem=sem_refs[a],
                                              recv_sem=sem_refs[n + a], device_id=_peer(1)[0], device_id_type=MESH)
            cp.wait_send()
            cp.wait_recv()

    res = pl.pallas_call(
        body, name=name,
        out_shape=[pltpu.HBM(x.shape, x.dtype) for x in xs] + [pltpu.HBM(l.shape, l.dtype) for l in lands],
        in_specs=[HBM_SPEC] * (n_x + n) + [SEM_SPEC] * (2 * n) + [ANY_SPEC] * n_after,
        out_specs=[HBM_SPEC] * (n_x + n),
        input_output_aliases={i: i for i in range(n_x + n)},
        compiler_params=pltpu.CompilerParams(has_side_effects=DATAFLOW),
    )(*xs, *lands, *sems, *after)
    return list(res[:n_x]), list(res[n_x:])


def _one(handle, a):
    sems, xs, lands = handle
    n = len(lands)
    return [sems[a], sems[n + a]], xs[a:a + 1], [lands[a]]


def _own_slot(land, block):
    return lax.dynamic_update_slice(land, block[None], (_my_flat(),) + (0,) * block.ndim)


def allreduce_small(x, name):
    rows = x.shape[0]

    def body(x_ref, o_ref, buf, send_sems, recv_sems):
        me = _my_flat()
        buf[me] = x_ref[...]
        sends = []
        for k in range(1, N_DEV):
            peer, _ = _peer(k)
            cp = pltpu.make_async_remote_copy(
                src_ref=x_ref, dst_ref=buf.at[me], send_sem=send_sems.at[k], recv_sem=recv_sems.at[k],
                device_id=peer, device_id_type=MESH)
            cp.start()
            sends.append(cp)
        for k in range(1, N_DEV):
            peer, peer_flat = _peer(k)
            pltpu.make_async_remote_copy(
                src_ref=x_ref, dst_ref=buf.at[peer_flat], send_sem=send_sems.at[k], recv_sem=recv_sems.at[k],
                device_id=peer, device_id_type=MESH).wait_recv()
        for cp in sends:
            cp.wait_send()
        acc = buf[0]
        for d in range(1, N_DEV):
            acc = acc + buf[d]
        o_ref[...] = acc

    vmem = pl.BlockSpec(memory_space=pltpu.VMEM)
    return pl.pallas_call(
        body, name=name, out_shape=jax.ShapeDtypeStruct((rows, LANES), F32),
        in_specs=[vmem], out_specs=vmem,
        scratch_shapes=[pltpu.VMEM((N_DEV, rows, LANES), F32),
                        pltpu.SemaphoreType.DMA((N_DEV,)), pltpu.SemaphoreType.DMA((N_DEV,))],
    )(x)


def matmul(a, b, mode, name, out_dtypes=(F32,), epilogue=None, extra=None, tm=1024, tn=1024, tk=2048, after=(),
           b_shards=False, out_shards=False, k_group=1):
    if b_shards:
        n_sh, b_rows, b_cols = b.shape
    if mode == "nn":
        (m, kd), n = a.shape, (n_sh * b_cols if b_shards else b.shape[1])
        if b_shards:
            tn = b_cols
    elif mode == "nt":
        (m, kd), n = a.shape, (b_rows if b_shards else b.shape[0])
        if b_shards:
            tk = k_group * b_cols
    else:
        (kd, m), n = a.shape, b.shape[1]
    tm, tn, tk = min(tm, m), min(tn, n), min(tk, kd)
    assert m % tm == 0 and n % tn == 0 and kd % tk == 0, (name, m, n, kd, tm, tn, tk)
    ksteps = kd // tk
    dims = {"nn": NN, "nt": NT, "tn": TN}[mode]
    n_out = len(out_dtypes)
    n_in = 2 + (extra is not None) + len(after)

    def finish(acc, e_ref, o_refs):
        outs = (acc,) if epilogue is None else epilogue(acc, e_ref[...] if e_ref is not None else None)
        for o_ref, o in zip(o_refs, outs):
            o_ref[...] = o.astype(o_ref.dtype)

    def product(a_ref, b_ref):
        if mode == "nt" and b_shards:
            w = b_cols
            parts = [_dot(a_ref[:, s * w:(s + 1) * w], b_ref[s], dims) for s in range(k_group)]
            return functools.reduce(lambda p, q: p + q, parts)
        return _dot(a_ref[...], b_ref[...], dims)

    def body(*refs):
        a_ref, b_ref = refs[0], refs[1]
        e_ref = refs[2] if extra is not None else None
        o_refs = refs[n_in:n_in + n_out]
        if ksteps == 1:
            finish(product(a_ref, b_ref), e_ref, o_refs)
            return
        acc_ref = refs[-1]
        kk = pl.program_id(2)

        @pl.when(kk == 0)
        def _():
            acc_ref[...] = jnp.zeros_like(acc_ref)

        acc_ref[...] += product(a_ref, b_ref)

        @pl.when(kk == ksteps - 1)
        def _():
            finish(acc_ref[...], e_ref, o_refs)

    if mode == "nn":
        a_spec = pl.BlockSpec((tm, tk), lambda i, j, k: (i, k))
        b_spec = (pl.BlockSpec((None, tk, tn), lambda i, j, k: (j, k, 0)) if b_shards
                  else pl.BlockSpec((tk, tn), lambda i, j, k: (k, j)))
    elif mode == "nt":
        a_spec = pl.BlockSpec((tm, tk), lambda i, j, k: (i, k))
        b_spec = (pl.BlockSpec((k_group, tn, b_cols), lambda i, j, k: (k, j, 0)) if b_shards
                  else pl.BlockSpec((tn, tk), lambda i, j, k: (j, k)))
    else:
        a_spec = pl.BlockSpec((tk, tm), lambda i, j, k: (k, i))
        b_spec = pl.BlockSpec((tk, tn), lambda i, j, k: (k, j))
    o_spec = pl.BlockSpec((tm, tn), lambda i, j, k: (i, j))
    res_spec = pl.BlockSpec((None, tm, tn), lambda i, j, k: (j, i, 0)) if out_shards else o_spec
    res_shape = (n // tn, m, tn) if out_shards else (m, n)
    in_specs = [a_spec, b_spec] + ([o_spec] if extra is not None else []) + [ANY_SPEC] * len(after)
    args = (a, b) + ((extra,) if extra is not None else ()) + tuple(after)
    res = pl.pallas_call(
        body, name=name, grid=(m // tm, n // tn, ksteps),
        in_specs=in_specs, out_specs=[res_spec] * n_out,
        out_shape=[jax.ShapeDtypeStruct(res_shape, dt) for dt in out_dtypes],
        scratch_shapes=[pltpu.VMEM((tm, tn), F32)] if ksteps > 1 else [],
        compiler_params=_params(("parallel", "parallel", "arbitrary")),
    )(*args)
    return res if n_out > 1 else res[0]


GATE_COL = 4 * GDN_WIDTH
RELAYOUT_ROWS = 256


def _cat_of_win(j):
    if j < GATE_COL:
        return j
    if j < GATE_COL + 2 * N_HEADS:
        return MAIN_WIDTH + (j - GATE_COL)
    return j - 2 * N_HEADS


def _win_of_cat(c):
    if c < GATE_COL:
        return c
    if c < MAIN_WIDTH:
        return c + 2 * N_HEADS
    if c < MAIN_WIDTH + 2 * N_HEADS:
        return GATE_COL + (c - MAIN_WIDTH)
    return None


def _runs(first, count, mapping):
    runs, i = [], 0
    while i < count:
        start, n = mapping(first + i), 1
        while i + n < count and mapping(first + i + n) == start + n:
            n += 1
        runs.append((start, n))
        i += n
    return runs


def weights_to_cat(g_in):
    n_dev, rows, shard = g_in.shape

    def body(x_ref, o_ref):
        for b in range(CAT_WIDTH // LANES):
            live = sum(_win_of_cat(LANES * b + i) is not None for i in range(LANES))
            parts = []
            for start, n in _runs(LANES * b, live, _win_of_cat):
                while n > 0:
                    d, o = divmod(start, shard)
                    take = min(n, shard - o)
                    parts.append(x_ref[d, :, o:o + take])
                    start, n = start + take, n - take
            if live < LANES:
                parts.append(jnp.zeros((RELAYOUT_ROWS, LANES - live), g_in.dtype))
            o_ref[:, LANES * b:LANES * (b + 1)] = parts[0] if len(parts) == 1 else jnp.concatenate(parts, axis=1)

    return pl.pallas_call(
        body, name="weights_to_cat", grid=(rows // RELAYOUT_ROWS,),
        in_specs=[pl.BlockSpec((n_dev, RELAYOUT_ROWS, shard), lambda i: (0, i, 0))],
        out_specs=pl.BlockSpec((RELAYOUT_ROWS, CAT_WIDTH), lambda i: (i, 0)),
        out_shape=jax.ShapeDtypeStruct((rows, CAT_WIDTH), g_in.dtype),
        compiler_params=_params(("parallel",)))(g_in)


def cat_to_shards(dw_cat, shard):
    rows = dw_cat.shape[0]

    def body(x_ref, o_ref):
        for d in range(N_DEV):
            for t0 in range(0, shard, LANES):
                width = min(LANES, shard - t0)
                parts = [x_ref[:, c:c + n] for c, n in _runs(d * shard + t0, width, _cat_of_win)]
                o_ref[d, :, t0:t0 + width] = parts[0] if len(parts) == 1 else jnp.concatenate(parts, axis=1)

    return pl.pallas_call(
        body, name="cat_to_shards", grid=(rows // RELAYOUT_ROWS,),
        in_specs=[pl.BlockSpec((RELAYOUT_ROWS, CAT_WIDTH), lambda i: (i, 0))],
        out_specs=pl.BlockSpec((N_DEV, RELAYOUT_ROWS, shard), lambda i: (0, i, 0)),
        out_shape=jax.ShapeDtypeStruct((N_DEV, rows, shard), dw_cat.dtype),
        compiler_params=_params(("parallel",)))(dw_cat)


ROW_BLOCK = 512


def rms_fwd(x, w, name):
    t, d = x.shape

    def body(x_ref, w_ref, n_ref, r_ref):
        h = x_ref[...]
        r = lax.rsqrt(jnp.mean(h * h, axis=-1, keepdims=True) + NORM_EPS)
        n_ref[...] = (h * r * w_ref[...]).astype(BF16)
        r_ref[...] = r

    row = pl.BlockSpec((ROW_BLOCK, d), lambda i: (i, 0))
    return pl.pallas_call(
        body, name=name, grid=(t // ROW_BLOCK,),
        in_specs=[row, pl.BlockSpec((1, d), lambda i: (0, 0))],
        out_specs=[row, pl.BlockSpec((ROW_BLOCK, 1), lambda i: (i, 0))],
        out_shape=[jax.ShapeDtypeStruct((t, d), BF16), jax.ShapeDtypeStruct((t, 1), F32)],
        compiler_params=_params(("parallel",)))(x, w)


FUSED_ROWS = 512


def out_proj_rms(y, w_out, x, w_norm, name):
    t, d = x.shape

    def body(y_ref, w_ref, x_ref, g_ref, h_ref, n_ref, r_ref):
        h = x_ref[...] + _dot(y_ref[...], w_ref[...], NN)
        r = lax.rsqrt(jnp.mean(h * h, axis=-1, keepdims=True) + NORM_EPS)
        h_ref[...] = h
        n_ref[...] = (h * r * g_ref[...]).astype(BF16)
        r_ref[...] = r

    row = pl.BlockSpec((FUSED_ROWS, d), lambda i: (i, 0))
    return pl.pallas_call(
        body, name=name, grid=(t // FUSED_ROWS,),
        in_specs=[pl.BlockSpec((FUSED_ROWS, y.shape[1]), lambda i: (i, 0)), pl.BlockSpec(w_out.shape, lambda i: (0, 0)),
                  row, pl.BlockSpec((1, d), lambda i: (0, 0))],
        out_specs=[row, row, pl.BlockSpec((FUSED_ROWS, 1), lambda i: (i, 0))],
        out_shape=[jax.ShapeDtypeStruct((t, d), F32), jax.ShapeDtypeStruct((t, d), BF16),
                   jax.ShapeDtypeStruct((t, 1), F32)],
        compiler_params=_params(("parallel",)))(y, w_out, x, w_norm)


def ff2_loss(act, w_ff2, h1, w, target, name, tk=2048):
    t, d = h1.shape
    ksteps = act.shape[1] // tk

    def body(a_ref, b_ref, h_ref, w_ref, t_ref, loss_ref, dhb_ref, dw_ref, acc_ref):
        i, kk = pl.program_id(0), pl.program_id(1)

        @pl.when((i == 0) & (kk == 0))
        def _():
            loss_ref[...] = jnp.zeros_like(loss_ref)
            dw_ref[...] = jnp.zeros_like(dw_ref)

        @pl.when(kk == 0)
        def _():
            acc_ref[...] = h_ref[...]

        acc_ref[...] += _dot(a_ref[...], b_ref[...], NN)

        @pl.when(kk == ksteps - 1)
        def _():
            h = acc_ref[...]
            wv = w_ref[...]
            r = lax.rsqrt(jnp.mean(h * h, axis=-1, keepdims=True) + NORM_EPS)
            yn = h * r
            e = yn * wv - t_ref[...]
            loss_ref[...] += 0.5 * jnp.sum(jnp.sum(e * e, axis=-1, keepdims=True), axis=0, keepdims=True) / d
            dy = e / d
            dw_ref[...] += jnp.sum(dy * yn, axis=0, keepdims=True)
            dyn = dy * wv
            dhb_ref[...] = (r * (dyn - yn * jnp.mean(dyn * yn, axis=-1, keepdims=True))).astype(BF16)

    row = pl.BlockSpec((FUSED_ROWS, d), lambda i, k: (i, 0))
    wspec = pl.BlockSpec((1, d), lambda i, k: (0, 0))
    return pl.pallas_call(
        body, name=name, grid=(t // FUSED_ROWS, ksteps),
        in_specs=[pl.BlockSpec((FUSED_ROWS, tk), lambda i, k: (i, k)), pl.BlockSpec((tk, d), lambda i, k: (k, 0)),
                  row, wspec, row],
        out_specs=[pl.BlockSpec((1, 1), lambda i, k: (0, 0)), row, wspec],
        out_shape=[jax.ShapeDtypeStruct((1, 1), F32), jax.ShapeDtypeStruct((t, d), BF16),
                   jax.ShapeDtypeStruct((1, d), F32)],
        scratch_shapes=[pltpu.VMEM((FUSED_ROWS, d), F32)],
        compiler_params=_params(("arbitrary", "arbitrary")))(act, w_ff2, h1, w, target)


def rms_bwd(h, r, w, dn, dres, out_dtype, name):
    t, d = h.shape

    def body(h_ref, r_ref, w_ref, dn_ref, dres_ref, dh_ref, dw_ref):
        @pl.when(pl.program_id(0) == 0)
        def _():
            dw_ref[...] = jnp.zeros_like(dw_ref)

        rv = r_ref[...]
        yn = h_ref[...] * rv
        dnv = dn_ref[...].astype(F32)
        dw_ref[...] += jnp.sum(dnv * yn, axis=0, keepdims=True)
        dyn = dnv * w_ref[...]
        dh = dres_ref[...].astype(F32) + rv * (dyn - yn * jnp.mean(dyn * yn, axis=-1, keepdims=True))
        dh_ref[...] = dh.astype(out_dtype)

    row = pl.BlockSpec((ROW_BLOCK, d), lambda i: (i, 0))
    wspec = pl.BlockSpec((1, d), lambda i: (0, 0))
    rspec = pl.BlockSpec((ROW_BLOCK, 1), lambda i: (i, 0))
    return pl.pallas_call(
        body, name=name, grid=(t // ROW_BLOCK,),
        in_specs=[row, rspec, wspec, row, row], out_specs=[row, wspec],
        out_shape=[jax.ShapeDtypeStruct((t, d), out_dtype), jax.ShapeDtypeStruct((1, d), F32)],
        compiler_params=_params(("arbitrary",)))(h, r, w, dn, dres)


CONV_TB = 512
CONV_CB = 512
HALO = 8


def _silu(x):
    return x * jax.nn.sigmoid(x)


def _conv_pre(xcat, w, rows):
    acc = None
    for j in range(4):
        sh = 3 - j
        xs = xcat if sh == 0 else pltpu.roll(xcat, sh, 0)
        term = xs[HALO:HALO + rows] * w[j:j + 1, :]
        acc = term if acc is None else acc + term
    return acc


def conv_fwd(proj, conv_w, name):
    t = proj.shape[0]
    nb = CONV_TB // HALO

    def body(x_ref, prev_ref, w_ref, o_ref):
        prev = jnp.where(pl.program_id(1) == 0, 0.0, prev_ref[...])
        xcat = jnp.concatenate([prev, x_ref[...]], axis=0)
        o_ref[...] = _silu(_conv_pre(xcat, w_ref[...], CONV_TB))

    return pl.pallas_call(
        body, name=name, grid=(QKV_WIDTH // CONV_CB, t // CONV_TB),
        in_specs=[pl.BlockSpec((CONV_TB, CONV_CB), lambda c, i: (i, c)),
                  pl.BlockSpec((HALO, CONV_CB), lambda c, i: (jnp.maximum(i * nb - 1, 0), c)),
                  pl.BlockSpec((4, CONV_CB), lambda c, i: (0, c))],
        out_specs=pl.BlockSpec((CONV_TB, CONV_CB), lambda c, i: (i, c)),
        out_shape=jax.ShapeDtypeStruct((t, QKV_WIDTH), F32),
        compiler_params=_params(("parallel", "parallel")))(proj, proj, conv_w)


def conv_bwd(proj, dout, conv_w, dproj, name):
    t = proj.shape[0]
    nb = CONV_TB // HALO
    nt = t // CONV_TB
    rows = CONV_TB + HALO

    def body(x_ref, prev_ref, next_ref, d_ref, dnext_ref, w_ref, dproj_in, dx_ref, dw_ref):
        del dproj_in
        i = pl.program_id(1)

        @pl.when(i == 0)
        def _():
            dw_ref[...] = jnp.zeros_like(dw_ref)

        w = w_ref[...]
        prev = jnp.where(i == 0, 0.0, prev_ref[...])
        last = i == nt - 1
        xcat = jnp.concatenate([prev, x_ref[...], next_ref[...]], axis=0)
        pre = _conv_pre(xcat, w, rows)
        dcat = jnp.concatenate([d_ref[...], jnp.where(last, 0.0, dnext_ref[...])], axis=0)
        sg = jax.nn.sigmoid(pre)
        dpre = dcat * (sg * (1.0 + pre * (1.0 - sg)))
        dx = None
        for j in range(4):
            sh = 3 - j
            ds = dpre if sh == 0 else pltpu.roll(dpre, rows - sh, 0)
            term = ds[:CONV_TB] * w[j:j + 1, :]
            dx = term if dx is None else dx + term
        dx_ref[...] = dx.astype(BF16)
        dcur = dpre[:CONV_TB]
        parts = []
        for j in range(4):
            sh = 3 - j
            xs = xcat if sh == 0 else pltpu.roll(xcat, sh, 0)
            parts.append(jnp.sum(dcur * xs[HALO:HALO + CONV_TB], axis=0, keepdims=True))
        dw_ref[...] += jnp.concatenate(parts, axis=0)

    cur = pl.BlockSpec((CONV_TB, CONV_CB), lambda c, i: (i, c))
    halo_prev = pl.BlockSpec((HALO, CONV_CB), lambda c, i: (jnp.maximum(i * nb - 1, 0), c))
    halo_next = pl.BlockSpec((HALO, CONV_CB), lambda c, i: (jnp.minimum((i + 1) * nb, nt * nb - 1), c))
    taps = pl.BlockSpec((4, CONV_CB), lambda c, i: (0, c))
    return pl.pallas_call(
        body, name=name, grid=(QKV_WIDTH // CONV_CB, nt),
        in_specs=[cur, halo_prev, halo_next, cur, halo_next, taps, ANY_SPEC],
        out_specs=[cur, taps],
        out_shape=[jax.ShapeDtypeStruct(dproj.shape, BF16), jax.ShapeDtypeStruct((4, QKV_WIDTH), F32)],
        input_output_aliases={6: 0},
        compiler_params=_params(("parallel", "arbitrary")))(proj, proj, proj, dout, dout, conv_w, dproj)


def _iota2(shape, axis):
    return lax.broadcasted_iota(jnp.int32, shape, axis)


def _softplus(x):
    return jnp.maximum(x, 0.0) + jnp.log(1.0 + jnp.exp(-jnp.abs(x)))


def _head_norm_gate(o, norm_w, gate):
    return o * lax.rsqrt(jnp.mean(o * o, axis=-1, keepdims=True) + NORM_EPS) * norm_w * _silu(gate)


GDN_PREC = ("bf", "bf")
HGRN_PREC = "bf"


def _each(fn, *cols):
    return [fn(*a) for a in zip(*cols)]


@functools.partial(jax.custom_vjp, nondiff_argnums=(2,))
def _known_inverse(low, inv, prec):
    del low, prec
    return inv


def _known_inverse_fwd(low, inv, prec):
    del low
    return inv, inv


def _known_inverse_bwd(prec, inv, ct):
    return -_mm_raw(_mm_raw(inv, ct, TN, prec), inv, NT, prec), jnp.zeros_like(inv)


_known_inverse.defvjp(_known_inverse_fwd, _known_inverse_bwd)


def gdn_stages(hs, qc, kc, vc, zc, ab, a_log_l, dt_l, norm_w, s, prec=GDN_PREC, inv_known=None):
    p_inv, p_mm = prec
    c = CHUNK
    ri, ci = _iota2((c, c), 0), _iota2((c, c), 1)
    incl, strict, eye = ri >= ci, ri > ci, ri == ci
    lane = _iota2((c, LANES), 1)
    last_row = _iota2((c, 1), 0) == c - 1
    rowsum = lambda x: jnp.sum(x, axis=1, keepdims=True)

    def row(col):
        return jnp.sum(jnp.where(eye, col, 0.0), axis=0, keepdims=True)

    q = _each(lambda x: x * lax.rsqrt(rowsum(x * x) + L2_EPS) * (HEAD_DIM ** -0.5), qc)
    k = _each(lambda x: x * lax.rsqrt(rowsum(x * x) + L2_EPS), kc)
    yield
    a_col = [rowsum(jnp.where(lane == h, ab, 0.0)) for h in hs]
    b_col = [rowsum(jnp.where(lane == h + N_HEADS, ab, 0.0)) for h in hs]
    beta = _each(jax.nn.sigmoid, b_col)
    g = _each(lambda a, al, dl: rowsum(jnp.where(lane == 0, -jnp.exp(al) * _softplus(a + dl), 0.0)), a_col, a_log_l, dt_l)
    gcum = _each(lambda x: rowsum(jnp.where(incl, row(x), 0.0)), g)
    g_last = _each(lambda x: jnp.sum(jnp.where(last_row, x, 0.0), axis=0, keepdims=True), gcum)
    decay = _each(lambda x: jnp.exp(jnp.where(incl, x - row(x), -jnp.inf)), gcum)
    yield
    kk = _each(lambda x: mm(x, x, NT, p_mm), k)
    low = _each(lambda b, x, d: jnp.where(strict, b * x * d, 0.0), beta, kk, decay)
    yield
    if inv_known is None:
        power = _each(lambda x: -x, low)
        inv = _each(lambda x: jnp.where(eye, 1.0, 0.0) + x, power)
        for _ in range(5):
            power = _each(lambda x: mm(x, x, NN, p_inv), power)
            yield
            inv = _each(lambda x, p: x + mm(x, p, NN, p_inv), inv, power)
            yield
    else:
        inv = _each(lambda x, known: _known_inverse(x, known, p_inv), low, inv_known)
    exp_g = _each(jnp.exp, gcum)
    yield
    u_v = _each(lambda i, b, x: mm(i, b * x, NN, p_mm), inv, beta, vc)
    w = _each(lambda i, b, e, x: mm(i, b * e * x, NN, p_mm), inv, beta, exp_g, k)
    yield
    attn = _each(lambda x, y, d: mm(x, y, NT, p_mm) * d, q, k, decay)
    yield
    u = _each(lambda x, y, z: x - mm(y, z, NN, p_mm), u_v, w, s)
    yield
    o = _each(lambda x, e, z: mm(x * e, z, NN, p_mm), q, exp_g, s)
    o = _each(lambda x, a, y: x + mm(a, y, NN, p_mm), o, attn, u)
    yield
    k_end = _each(lambda x, gl, gc: x * jnp.exp(gl - gc), k, g_last, gcum)
    s_new = _each(lambda z, gl, x, y: z * jnp.exp(gl) + mm(x, y, TN, p_mm), s, g_last, k_end, u)
    return (_each(lambda x, z: _head_norm_gate(x, norm_w, z), o, zc), s_new), inv


def gdn_chunk(h, qc, kc, vc, zc, ab, a_log_l, dt_l, norm_w, s, prec=GDN_PREC, reuse_inverse=False):
    args = ([h], [qc], [kc], [vc], [zc], ab, [a_log_l], [dt_l], norm_w, [s], prec)
    if reuse_inverse:
        inv = lax.stop_gradient(gdn_chunks(*args)[1])
        (y, s_new), _ = gdn_chunks(*args, inv_known=inv)
    else:
        (y, s_new), _ = gdn_chunks(*args)
    return y[0], s_new[0]


DIAG_ROWS = SUB_CHUNK // 2
SHIFT_PAD = 8
SHIFT_ROWS = SHIFT_PAD + CHUNK + SHIFT_PAD
SHIFT_WAYS = 4


class RolledRows:
    def down(self, x, which):
        del which
        return [x] + [pltpu.roll(x, off, 0) for off in range(1, DIAG_ROWS)]

    def up_sum(self, parts, which):
        del which
        acc = parts[0]
        for off in range(1, DIAG_ROWS):
            acc = acc + pltpu.roll(parts[off], CHUNK - off, 0)
        return acc


class SlotRows:
    def __init__(self, slots):
        self.slots = slots

    def down(self, x, which):
        self.slots[which, 0, SHIFT_PAD:SHIFT_PAD + CHUNK, :] = x
        return [x] + [self.slots[which, 0, SHIFT_PAD - off:SHIFT_PAD + CHUNK - off, :] for off in range(1, DIAG_ROWS)]

    def up_sum(self, parts, which):
        acc = parts[0]
        for off in range(1, DIAG_ROWS):
            way = 1 + off % (SHIFT_WAYS - 1)
            self.slots[which, way, SHIFT_PAD:SHIFT_PAD + CHUNK, :] = parts[off]
            acc = acc + self.slots[which, way, SHIFT_PAD + off:SHIFT_PAD + CHUNK + off, :]
        return acc


def _sub_block_rows():
    return jnp.bitwise_and(_iota2((CHUNK, 1), 0), DIAG_ROWS - 1)


def _diag_forward(rows, q, key, bc, v):
    rmod = _sub_block_rows()
    k_d, b_d, v_d = rows.down(key, 0), rows.down(bc, 1), rows.down(v, 2)
    o = None
    for off in range(DIAG_ROWS):
        e = jnp.exp(jnp.where(rmod >= off, bc - b_d[off], -jnp.inf))
        term = jnp.sum(q * k_d[off] * e, axis=-1, keepdims=True) * v_d[off]
        o = term if o is None else o + term
    return o


def _diag_backward(rows, q, key, bc, v, do):
    rmod = _sub_block_rows()
    k_d, b_d, v_d = rows.down(key, 0), rows.down(bc, 1), rows.down(v, 2)
    dq = db = None
    dk_parts, db_parts, dv_parts = [], [], []
    for off in range(DIAG_ROWS):
        e = jnp.exp(jnp.where(rmod >= off, bc - b_d[off], -jnp.inf))
        qe = q * e
        a = jnp.sum(qe * k_d[off], axis=-1, keepdims=True)
        da = jnp.sum(do * v_d[off], axis=-1, keepdims=True)
        dv_parts.append(a * do)
        dq_term = (da * e) * k_d[off]
        dk_term = da * qe
        s = dk_term * k_d[off]
        dq = dq_term if dq is None else dq + dq_term
        db = s if db is None else db + s
        dk_parts.append(dk_term)
        db_parts.append(s)
    return dq, rows.up_sum(dk_parts, 0), db - rows.up_sum(db_parts, 1), rows.up_sum(dv_parts, 2)


def diag_part(rows, differentiable=True):
    forward = functools.partial(_diag_forward, rows)
    if not differentiable:
        return forward
    part = jax.custom_vjp(forward)
    part.defvjp(lambda q, key, bc, v: (forward(q, key, bc, v), (q, key, bc, v)),
                lambda res, do: _diag_backward(rows, *res, do))
    return part


def hgrn_stages(qb, fb, ib, gb, l0, l1, norm_w, st, prec=HGRN_PREC, diags=None, o_known=None):
    c = CHUNK
    ri, ci = _iota2((4 * c, c), 0), _iota2((4 * c, c), 1)
    rcol = _iota2((c, 1), 0)
    blk0 = jnp.bitwise_and(ri, c - SUB_CHUNK)
    limit = jnp.where(ri < c, ri + 1, jnp.where(ri < 2 * c, blk0, jnp.where(ri < 3 * c, blk0 + SUB_CHUNK,
                                                                          blk0 + DIAG_ROWS)))
    sel = jnp.where(ci < limit, 1.0, 0.0)
    ri, ci = _iota2((c, c), 0), _iota2((c, c), 1)
    lb = _each(lambda a, b: jax.nn.sigmoid(a - b), l0, l1)
    forget = _each(lambda b, f: b + (1.0 - b) * jax.nn.sigmoid(f), lb, fb)
    key = _each(lambda b, f: (1.0 - b) * jax.nn.sigmoid(-f), lb, fb)
    q = _each(_silu, qb)
    v = ib
    logf = _each(jnp.log, forget)
    sums = _each(lambda x: sel_sums(sel, x), logf)
    bc, b_start, b_end, b_half = ([x[i] for x in sums] for i in range(4))
    b_last = _each(lambda x: jnp.sum(x, axis=0, keepdims=True), logf)
    o = _each(lambda x, b, z: mm(x * jnp.exp(b), z, NT, prec), q, bc, st)
    if diags is None:
        diags = [diag_part(RolledRows())] * len(qb)
    yield
    o = list(o)
    for h in range(len(o)):
        o[h] = o[h] + diags[h](q[h], key[h], bc[h], v[h])
        yield
    second = jnp.bitwise_and(rcol, SUB_CHUNK - 1) >= DIAG_ROWS
    same_sub = jnp.bitwise_and(ri, c - SUB_CHUNK) == jnp.bitwise_and(ci, c - SUB_CHUNK)
    q_half = _each(lambda x, b, bh: x * jnp.exp(jnp.where(second, b - bh, -jnp.inf)), q, bc, b_half)
    k_half = _each(lambda x, b, bh: x * jnp.exp(jnp.where(second, -jnp.inf, bh - b)), key, bc, b_half)
    a_half = _each(lambda x, z: jnp.where(same_sub, mm(x, z, NT, prec), 0.0), q_half, k_half)
    o = _each(lambda acc, a, val: acc + mm(a, val, NN, prec), o, a_half, v)
    yield
    q_rel = _each(lambda x, b, bs: x * jnp.exp(b - bs), q, bc, b_start)
    k_rel = _each(lambda x, b, be: x * jnp.exp(be - b), key, bc, b_end)
    for y in range(c // SUB_CHUNK - 1):
        def scaled(x, b, bs):
            end_y = jnp.sum(jnp.where(rcol == SUB_CHUNK * y + SUB_CHUNK - 1, b, 0.0), axis=0, keepdims=True)
            return x * jnp.exp(jnp.where(rcol >= SUB_CHUNK * (y + 1), bs - end_y, -jnp.inf))
        dq = _each(scaled, q_rel, bc, b_start)
        in_y = (ci >= SUB_CHUNK * y) & (ci < SUB_CHUNK * (y + 1))
        a_y = _each(lambda x, z: jnp.where(in_y, mm(x, z, NT, prec), 0.0), dq, k_rel)
        o = _each(lambda acc, a, val: acc + mm(a, val, NN, prec), o, a_y, v)
        yield
    k_state = _each(lambda x, bl, b: x * jnp.exp(bl - b), key, b_last, bc)
    st_new = _each(lambda z, bl, val, x: z * jnp.exp(bl) + mm(val, x, TN, prec), st, b_last, v, k_state)
    if o_known is not None:
        o = _each(_known_value, o, o_known)
    return (_each(lambda x, z: _head_norm_gate(x, norm_w, z), o, gb), st_new), o


def _drain(gen):
    try:
        while True:
            next(gen)
    except StopIteration as done:
        return done.value


def _alternate(gen_a, gen_b):
    out, live = [None, None], [gen_a, gen_b]
    while any(g is not None for g in live):
        for i, g in enumerate(live):
            if g is None:
                continue
            try:
                next(g)
            except StopIteration as done:
                out[i], live[i] = done.value, None
    return out


def gdn_chunks(*args, **kwargs):
    return _drain(gdn_stages(*args, **kwargs))


def hgrn_chunks(*args, **kwargs):
    return _drain(hgrn_stages(*args, **kwargs))


def hgrn_chunk(qb, fb, ib, gb, l0, l1, norm_w, st, prec=HGRN_PREC, reuse_output=False):
    args = ([qb], [fb], [ib], [gb], [l0], [l1], norm_w, [st], prec)
    if reuse_output:
        known = lax.stop_gradient(hgrn_chunks(*args)[1])
        (y, st_new), _ = hgrn_chunks(*args, o_known=known)
    else:
        (y, st_new), _ = hgrn_chunks(*args)
    return y[0], st_new[0]


HEAD_VEC = (N_HEADS, 1, LANES)


class _ChunkSpecs:
    def __init__(self, nc, rev):
        self.nc, self.rev = nc, rev

    def _c(self, c):
        return self.nc - 1 - c if self.rev else c

    def row(self, width, block=0):
        return pl.BlockSpec((CHUNK, width), lambda c: (self._c(c), block))

    def per_head(self, rows):
        return pl.BlockSpec((None, N_HEADS, rows, rows), lambda c: (self._c(c), 0, 0, 0))

    @staticmethod
    def whole(shape):
        return pl.BlockSpec(shape, lambda c: (0,) * len(shape))


def _lanes(j):
    return slice(j * LANES, (j + 1) * LANES)


def mixer_fwd(qkv_c, proj, a_log_l, dt_l, gdn_norm_w, l0, l1, hgrn_norm_w, name):
    t = qkv_c.shape[0]
    hb = N_HEADS
    sp = _ChunkSpecs(t // CHUNK, rev=False)
    hs = list(range(hb))

    def body(q_ref, k_ref, v_ref, z_ref, ab_ref, al_ref, dt_ref, gnw_ref, qb_ref, fb_ref, ib_ref, gb_ref, l0_ref, l1_ref,
             hnw_ref, y_ref, hist_a_ref, inv_ref, hist_b_ref, o_ref, sa_ref, sb_ref, shift_ref):
        @pl.when(pl.program_id(0) == 0)
        def _():
            sa_ref[...] = jnp.zeros_like(sa_ref)
            sb_ref[...] = jnp.zeros_like(sb_ref)
            shift_ref[...] = jnp.zeros_like(shift_ref)

        heads = lambda ref: [ref[:, _lanes(j)] for j in hs]
        s_a, s_b = [sa_ref[h] for h in hs], [sb_ref[h] for h in hs]
        for h in hs:
            hist_a_ref[h] = s_a[h]
            hist_b_ref[h] = s_b[h]
        diags = [diag_part(SlotRows(shift_ref.at[h]), differentiable=False) for h in hs]
        ((y_a, s_a_new), inv), ((y_b, s_b_new), o_pre) = _alternate(
            gdn_stages(hs, heads(q_ref), heads(k_ref), heads(v_ref), heads(z_ref), ab_ref[...],
                       [al_ref[h] for h in hs], [dt_ref[h] for h in hs], gnw_ref[...], s_a),
            hgrn_stages(heads(qb_ref), heads(fb_ref), heads(ib_ref), heads(gb_ref),
                        [l0_ref[h] for h in hs], [l1_ref[h] for h in hs], hnw_ref[...], s_b, diags=diags))
        for h in hs:
            y_ref[:, _lanes(h)] = y_a[h].astype(BF16)
            y_ref[:, _lanes(hb + h)] = y_b[h].astype(BF16)
            o_ref[:, _lanes(h)] = o_pre[h]
            sa_ref[h] = s_a_new[h]
            sb_ref[h] = s_b_new[h]
            inv_ref[h] = inv[h]

    vec, gain, slab = sp.whole(HEAD_VEC), sp.whole((1, LANES)), functools.partial(sp.row, GDN_WIDTH)
    states = jax.ShapeDtypeStruct((sp.nc, N_HEADS, HEAD_DIM, HEAD_DIM), F32)
    return pl.pallas_call(
        body, name=name, grid=(sp.nc,),
        in_specs=[slab(0), slab(1), slab(2), slab(3), sp.row(LANES, AB_BLOCK), vec, vec, gain,
                  slab(4), slab(5), slab(6), slab(7), vec, vec, gain],
        out_specs=[sp.row(2 * GDN_WIDTH), sp.per_head(HEAD_DIM), sp.per_head(CHUNK), sp.per_head(HEAD_DIM), slab(0)],
        out_shape=[jax.ShapeDtypeStruct((t, 2 * GDN_WIDTH), BF16), states,
                   jax.ShapeDtypeStruct((sp.nc, N_HEADS, CHUNK, CHUNK), F32), states,
                   jax.ShapeDtypeStruct((t, GDN_WIDTH), F32)],
        scratch_shapes=[pltpu.VMEM((N_HEADS, HEAD_DIM, HEAD_DIM), F32), pltpu.VMEM((N_HEADS, HEAD_DIM, HEAD_DIM), F32),
                        pltpu.VMEM((hb, 3, SHIFT_WAYS, SHIFT_ROWS, LANES), F32)],
        compiler_params=_params(("arbitrary",)),
    )(qkv_c, qkv_c, qkv_c, proj, proj, a_log_l, dt_l, gdn_norm_w, proj, proj, proj, proj, l0, l1, hgrn_norm_w)


def mixer_bwd(qkv_c, proj, a_log_l, dt_l, gdn_norm_w, l0, l1, hgrn_norm_w, hist_a, inv_hist, hist_b, o_pre, dy, name):
    t = qkv_c.shape[0]
    hb = N_HEADS
    sp = _ChunkSpecs(t // CHUNK, rev=True)
    hs = list(range(hb))

    def body(q_ref, k_ref, v_ref, z_ref, ab_ref, al_ref, dt_ref, gnw_ref, qb_ref, fb_ref, ib_ref, gb_ref, l0_ref, l1_ref,
             hnw_ref, hist_a_ref, inv_ref, hist_b_ref, o_ref, dy_ref,
             dqkv_ref, dproj_ref, dal_ref, ddt_ref, dgnw_ref, dl0_ref, dl1_ref, dhnw_ref, dsa_ref, dsb_ref, shift_ref):
        @pl.when(pl.program_id(0) == 0)
        def _():
            for ref in (dal_ref, ddt_ref, dgnw_ref, dl0_ref, dl1_ref, dhnw_ref, dsa_ref, dsb_ref, shift_ref):
                ref[...] = jnp.zeros_like(ref)

        heads = lambda ref, first=0: [ref[:, _lanes(first + j)] for j in hs]
        diags = [diag_part(SlotRows(shift_ref.at[h])) for h in hs]
        inv_known, o_known = [inv_ref[h] for h in hs], heads(o_ref)

        def both(ga, gb):
            (ra, inv), (rb, o_pre) = _alternate(gdn_stages(hs, *ga, inv_known=inv_known),
                                                hgrn_stages(*gb, diags=diags, o_known=o_known))
            return (ra, rb), (inv, o_pre)

        ga = (heads(q_ref), heads(k_ref), heads(v_ref), heads(z_ref), ab_ref[...], [al_ref[h] for h in hs],
              [dt_ref[h] for h in hs], gnw_ref[...], [hist_a_ref[h] for h in hs])
        gb = (heads(qb_ref), heads(fb_ref), heads(ib_ref), heads(gb_ref), [l0_ref[h] for h in hs],
              [l1_ref[h] for h in hs], hnw_ref[...], [hist_b_ref[h] for h in hs])
        _, vjp, _ = jax.vjp(both, ga, gb, has_aux=True)
        dy_a = [x.astype(F32) for x in heads(dy_ref)]
        dy_b = [x.astype(F32) for x in heads(dy_ref, hb)]
        (dq, dk, dv, dz, dab, dal, ddt, dgnw, ds_a), (dqb, dfb, dib, dgb, dl0, dl1, dhnw, ds_b) = vjp(
            ((dy_a, [dsa_ref[h] for h in hs]), (dy_b, [dsb_ref[h] for h in hs])))
        for h in hs:
            dqkv_ref[:, _lanes(h)] = dq[h]
            dqkv_ref[:, _lanes(hb + h)] = dk[h]
            dqkv_ref[:, _lanes(2 * hb + h)] = dv[h]
            for slab, val in enumerate((dz, dqb, dfb, dib, dgb)):
                dproj_ref[:, _lanes((3 + slab) * hb + h)] = val[h].astype(BF16)
            dal_ref[h] += dal[h]
            ddt_ref[h] += ddt[h]
            dl0_ref[h] += dl0[h]
            dl1_ref[h] += dl1[h]
            dsa_ref[h] = ds_a[h]
            dsb_ref[h] = ds_b[h]
        dproj_ref[:, MAIN_WIDTH:] = dab.astype(BF16)
        dgnw_ref[...] += dgnw
        dhnw_ref[...] += dhnw

    vec, gain, slab = sp.whole(HEAD_VEC), sp.whole((1, LANES)), functools.partial(sp.row, GDN_WIDTH)
    vec_shape, gain_shape = jax.ShapeDtypeStruct(HEAD_VEC, F32), jax.ShapeDtypeStruct((1, LANES), F32)
    return pl.pallas_call(
        body, name=name, grid=(sp.nc,),
        in_specs=[slab(0), slab(1), slab(2), slab(3), sp.row(LANES, AB_BLOCK), vec, vec, gain,
                  slab(4), slab(5), slab(6), slab(7), vec, vec, gain,
                  sp.per_head(HEAD_DIM), sp.per_head(CHUNK), sp.per_head(HEAD_DIM), slab(0), sp.row(2 * GDN_WIDTH)],
        out_specs=[sp.row(QKV_WIDTH), sp.row(CAT_WIDTH), vec, vec, gain, vec, vec, gain],
        out_shape=[jax.ShapeDtypeStruct((t, QKV_WIDTH), F32), jax.ShapeDtypeStruct((t, CAT_WIDTH), BF16),
                   vec_shape, vec_shape, gain_shape, vec_shape, vec_shape, gain_shape],
        scratch_shapes=[pltpu.VMEM((N_HEADS, HEAD_DIM, HEAD_DIM), F32), pltpu.VMEM((N_HEADS, HEAD_DIM, HEAD_DIM), F32),
                        pltpu.VMEM((hb, 3, SHIFT_WAYS, SHIFT_ROWS, LANES), F32)],
        compiler_params=_params(("arbitrary",)),
    )(qkv_c, qkv_c, qkv_c, proj, proj, a_log_l, dt_l, gdn_norm_w, proj, proj, proj, proj, l0, l1, hgrn_norm_w,
      hist_a, inv_hist, hist_b, o_pre, dy)


def _adamw(w, g, m, v):
    m = ADAM_B1 * m + (1.0 - ADAM_B1) * g
    v = ADAM_B2 * v + (1.0 - ADAM_B2) * jnp.square(g)
    m_hat = m / (1.0 - ADAM_B1 ** ADAM_STEP)
    v_hat = v / (1.0 - ADAM_B2 ** ADAM_STEP)
    delta = -ADAM_LR * (m_hat / (jnp.sqrt(v_hat) + ADAM_EPS) + ADAM_WD * w)
    return delta, m, v


def adamw_reduce(parts, w, m, v, name, rb=128):
    r, c = w.shape
    rb = min(rb, r)

    def body(p_ref, w_ref, m_ref, v_ref, g_ref, d_ref, mo_ref, vo_ref):
        g = p_ref[0].astype(F32)
        for d in range(1, N_DEV):
            g = g + p_ref[d].astype(F32)
        delta, mn, vn = _adamw(w_ref[...], g, m_ref[...], v_ref[...])
        g_ref[...] = g
        d_ref[...] = delta
        mo_ref[...] = mn
        vo_ref[...] = vn

    blk = pl.BlockSpec((rb, c), lambda i: (i, 0))
    return pl.pallas_call(
        body, name=name, grid=(r // rb,),
        in_specs=[pl.BlockSpec((N_DEV, rb, c), lambda i: (0, i, 0)), blk, blk, blk],
        out_specs=[blk] * 4, out_shape=[jax.ShapeDtypeStruct((r, c), F32)] * 4,
        compiler_params=_params(("parallel",)))(parts, w, m, v)


def adamw_small(w, g, m, v, name):
    def body(w_ref, g_ref, m_ref, v_ref, d_ref, mo_ref, vo_ref):
        delta, mn, vn = _adamw(w_ref[...], g_ref[...], m_ref[...], v_ref[...])
        d_ref[...] = delta
        mo_ref[...] = mn
        vo_ref[...] = vn

    vmem = pl.BlockSpec(memory_space=pltpu.VMEM)
    return pl.pallas_call(body, name=name, in_specs=[vmem] * 4, out_specs=[vmem] * 3,
                          out_shape=[jax.ShapeDtypeStruct(w.shape, F32)] * 3)(w, g, m, v)


def _pack(arrays):
    flat = jnp.concatenate([a.reshape(-1).astype(F32) for a in arrays])
    rows = -(-flat.shape[0] // (8 * LANES)) * 8
    return jnp.pad(flat, (0, rows * LANES - flat.shape[0])).reshape(rows, LANES)


def _unpack(packed, shapes):
    flat, out, off = packed.reshape(-1), [], 0
    for s in shapes:
        n = 1
        for d in s:
            n *= d
        out.append(flat[off:off + n].reshape(s))
        off += n
    return out


def _relu2_epilogue(acc, _):
    r = jnp.maximum(acc, 0.0)
    return acc, r * r


def _relu2_bwd_epilogue(acc, a1):
    return (acc * (2.0 * jnp.maximum(a1, 0.0)),)


def kernel(x, w_in, conv_w, gdn_a_log, gdn_dt_bias, gdn_norm_w, hgrn_lb_logits, hgrn_norm_w, w_out, norm_mix_w, norm_ffn_w, w_ff1, w_ff2, norm_final_w, loss_target, m_w_in, m_conv_w, m_gdn_a_log, m_gdn_dt_bias, m_gdn_norm_w, m_hgrn_lb_logits, m_hgrn_norm_w, m_w_out, m_norm_mix_w, m_norm_ffn_w, m_w_ff1, m_w_ff2, m_norm_final_w, v_w_in, v_conv_w, v_gdn_a_log, v_gdn_dt_bias, v_gdn_norm_w, v_hgrn_lb_logits, v_hgrn_norm_w, v_w_out, v_norm_mix_w, v_norm_ffn_w, v_w_ff1, v_w_ff2, v_norm_final_w):
    me = _my_flat()
    xs = x[0]
    target = loss_target[0]
    shard_in = w_in.shape[2]
    shard_conv = conv_w.shape[2]

    tok = lambda t: t[0:1, 0:1]
    own = lambda src: lax.dynamic_index_in_dim(src, me, 0, keepdims=False)

    g_in, g_conv = gather_two_level([w_in[0].astype(BF16), conv_w[0]], "gather_w_in")
    h_g1, t_g1 = exchange_start([w_out[0].astype(BF16), w_ff1[0].astype(BF16)], True, "gather_mid_start", after=[g_in],
                                peers=CHIP_PEERS)
    h_g2, t_g2 = exchange_start([w_ff2[0].astype(BF16)], True, "gather_ff2_start", after=[t_g1], peers=CHIP_PEERS)
    w_cat = weights_to_cat(g_in)
    conv_full = jnp.transpose(g_conv, (1, 0, 2)).reshape(4, QKV_WIDTH)

    lane_b = lambda p: jnp.broadcast_to(p.reshape(N_HEADS, 1, 1), HEAD_VEC)
    a_log_l, dt_l = lane_b(gdn_a_log[0]), lane_b(gdn_dt_bias[0])
    l0 = hgrn_lb_logits[0].reshape(HEAD_VEC)
    l1 = hgrn_lb_logits[1].reshape(HEAD_VEC)

    n1, r1 = rms_fwd(xs, norm_mix_w + tok(t_g1) + tok(t_g2), "rms_mix")
    proj = matmul(n1, w_cat, "nn", "in_proj", tn=CAT_WIDTH // 5)
    qkv_c = conv_fwd(proj, conv_full, "conv_fwd")
    y, hist_a, inv_a, hist_b, o_b = mixer_fwd(qkv_c, proj, a_log_l, dt_l, gdn_norm_w, l0, l1, hgrn_norm_w, "mixer_fwd")
    (s_out, s_ff1), (l_out, l_ff1) = exchange_wait(h_g1, "gather_mid_wait", after=[y], copies=len(CHIP_PEERS))
    (s_ff2,), (l_ff2,) = exchange_wait(h_g2, "gather_ff2_wait", after=[y], copies=len(CHIP_PEERS))
    h_fw, _ = forward_start([l_out, l_ff1, l_ff2], "gather_forward_start")
    _, (l_out,) = exchange_wait(_one(h_fw, 0), "forward_out_wait", copies=len(OTHER_CHIPS))
    w_out_full = _own_slot(l_out, s_out).reshape(D_MODEL, D_MODEL)
    h1, n2, r2 = out_proj_rms(y, w_out_full, xs, norm_ffn_w, "out_proj_rms")
    _, (l_ff1,) = exchange_wait(_one(h_fw, 1), "forward_ff1_wait", after=[n2], copies=len(OTHER_CHIPS))
    w_ff1_sh = _own_slot(l_ff1, s_ff1)
    a1, act = matmul(n2, w_ff1_sh, "nn", "ff1", out_dtypes=(F32, BF16), epilogue=_relu2_epilogue, b_shards=True)
    _, (l_ff2,) = exchange_wait(_one(h_fw, 2), "forward_ff2_wait", after=[act], copies=len(OTHER_CHIPS))
    w_ff2_full = _own_slot(l_ff2, s_ff2).reshape(D_FF, D_MODEL)
    loss_sum, dh2_b, d_final = ff2_loss(act, w_ff2_full, h1, norm_final_w.reshape(1, D_MODEL), target, "ff2_loss")

    da1 = matmul(dh2_b, w_ff2_full, "nt", "d_act", out_dtypes=(BF16,), epilogue=_relu2_bwd_epilogue, extra=a1)
    t_all = xs.shape[0]
    dw_ff2 = matmul(act, dh2_b, "tn", "dw_ff2", out_dtypes=(BF16,), tk=t_all)
    p_ff2 = dw_ff2.reshape(N_DEV, D_FF // N_DEV, D_MODEL)
    h_s1, t_s1 = exchange_start([p_ff2], False, "scatter_ff2_start")
    dn2 = matmul(da1, w_ff1_sh, "nt", "d_n2", out_dtypes=(BF16,), after=[t_s1], b_shards=True, k_group=4)
    p_ff1 = matmul(n2, da1, "tn", "dw_ff1", out_dtypes=(BF16,), tn=D_FF // N_DEV, tk=t_all, after=[t_s1], out_shards=True)
    h_s2, t_s2 = exchange_start([p_ff1], False, "scatter_ff1_start")
    dh1_b, d_ffn = rms_bwd(h1, r2, norm_ffn_w + tok(t_s2), dn2, dh2_b, BF16, "rms_ffn_bwd")
    dmix = matmul(dh1_b, w_out_full, "nt", "d_mix", out_dtypes=(BF16,))
    dw_out = matmul(y, dh1_b, "tn", "dw_out", out_dtypes=(BF16,), tk=t_all)
    p_out = dw_out.reshape(N_DEV, D_MODEL // N_DEV, D_MODEL)
    h_s3, t_s3 = exchange_start([p_out], False, "scatter_out_start")
    d_qkv_c, dproj, d_alog_l, d_dt_l, d_gnw, dl0, dl1, d_hnw = mixer_bwd(
        qkv_c, proj, a_log_l, dt_l, gdn_norm_w + tok(t_s3), l0, l1, hgrn_norm_w, hist_a, inv_a, hist_b, o_b, dmix,
        "mixer_bwd")
    dproj, d_conv_full = conv_bwd(proj, d_qkv_c, conv_full, dproj, "conv_bwd")
    dw_cat = matmul(n1, dproj, "tn", "dw_in", out_dtypes=(BF16,), tm=512, tn=CAT_WIDTH // 5, tk=t_all)
    p_in = cat_to_shards(dw_cat, shard_in)
    h_s4, t_s4 = exchange_start([p_in], False, "scatter_in_start")

    (s_ff2g,), (r_ff2,) = exchange_wait(h_s1, "scatter_ff2_wait", after=[t_s4])
    (s_ff1g,), (r_ff1,) = exchange_wait(h_s2, "scatter_ff1_wait", after=[t_s4])
    (s_outg,), (r_out,) = exchange_wait(h_s3, "scatter_out_wait", after=[t_s4])
    g_w_ff2, d_w_ff2, nm_w_ff2, nv_w_ff2 = adamw_reduce(
        _own_slot(r_ff2, own(s_ff2g)), w_ff2[0], m_w_ff2[0], v_w_ff2[0], "adamw_w_ff2")
    g_w_ff1, d_w_ff1, nm_w_ff1, nv_w_ff1 = adamw_reduce(
        _own_slot(r_ff1, own(s_ff1g)), w_ff1[0], m_w_ff1[0], v_w_ff1[0], "adamw_w_ff1")
    g_w_out, d_w_out, nm_w_out, nv_w_out = adamw_reduce(
        _own_slot(r_out, own(s_outg)), w_out[0], m_w_out[0], v_w_out[0], "adamw_w_out")
    dn1 = matmul(dproj, w_cat, "nt", "d_n1", out_dtypes=(BF16,), tk=CAT_WIDTH // 5, after=[t_s4])
    dx, d_mix = rms_bwd(xs, r1, norm_mix_w, dn1, dh1_b, F32, "rms_mix_bwd")
    (s_ing,), (r_in,) = exchange_wait(h_s4, "scatter_in_wait", after=[dx, d_w_ff2, d_w_ff1, d_w_out])
    g_w_in, d_w_in, nm_w_in, nv_w_in = adamw_reduce(
        _own_slot(r_in, own(s_ing)), w_in[0], m_w_in[0], v_w_in[0], "adamw_w_in")

    d_lb = jnp.stack([dl0.reshape(GDN_WIDTH), dl1.reshape(GDN_WIDTH)])
    small_shapes = [(1, N_HEADS), (1, N_HEADS), (1, HEAD_DIM), (2, GDN_WIDTH), (1, HEAD_DIM), (1, D_MODEL),
                    (1, D_MODEL), (D_MODEL,), (4, QKV_WIDTH)]
    small = _pack([d_alog_l[:, 0, 0], d_dt_l[:, 0, 0], d_gnw, d_lb, d_hnw, d_mix, d_ffn, d_final, d_conv_full])
    red = allreduce_small(small, "allreduce_small")
    g_alog, g_dt, g_gnw, g_lb, g_hnw, g_mix, g_ffn, g_final, g_conv_full = _unpack(red, small_shapes)
    g_conv = lax.dynamic_slice(g_conv_full, (0, me * shard_conv), (4, shard_conv)).reshape(1, 4, shard_conv)
    small_g = [g_alog, g_dt, g_gnw, g_lb, g_hnw, g_mix, g_ffn, g_final, g_conv]
    small_w = [gdn_a_log, gdn_dt_bias, gdn_norm_w, hgrn_lb_logits, hgrn_norm_w, norm_mix_w, norm_ffn_w, norm_final_w, conv_w]
    small_m = [m_gdn_a_log, m_gdn_dt_bias, m_gdn_norm_w, m_hgrn_lb_logits, m_hgrn_norm_w, m_norm_mix_w, m_norm_ffn_w,
               m_norm_final_w, m_conv_w]
    small_v = [v_gdn_a_log, v_gdn_dt_bias, v_gdn_norm_w, v_hgrn_lb_logits, v_hgrn_norm_w, v_norm_mix_w, v_norm_ffn_w,
               v_norm_final_w, v_conv_w]
    shapes = [a.shape for a in small_w]
    d_s, m_s, v_s = adamw_small(_pack(small_w), _pack(small_g), _pack(small_m), _pack(small_v), "adamw_small")
    d_alog, d_dt, d_gn, d_lbl, d_hn, d_nm, d_nf, d_nfin, d_cw = _unpack(d_s, shapes)
    m_alog, m_dt, m_gn, m_lbl, m_hn, m_nm, m_nf, m_nfin, m_cw = _unpack(m_s, shapes)
    v_alog, v_dt, v_gn, v_lbl, v_hn, v_nm, v_nf, v_nfin, v_cw = _unpack(v_s, shapes)

    loss = lax.psum(loss_sum[0, 0], ("x", "y", "c"))
    lead = lambda a: a[None]
    grads = [lead(g_w_in), g_conv, g_alog, g_dt, g_gnw, g_lb, g_hnw, lead(g_w_out), g_mix, g_ffn,
             lead(g_w_ff1), lead(g_w_ff2), g_final]
    deltas = [lead(d_w_in), d_cw, d_alog, d_dt, d_gn, d_lbl, d_hn, lead(d_w_out), d_nm, d_nf,
              lead(d_w_ff1), lead(d_w_ff2), d_nfin]
    new_m = [lead(nm_w_in), m_cw, m_alog, m_dt, m_gn, m_lbl, m_hn, lead(nm_w_out), m_nm, m_nf,
             lead(nm_w_ff1), lead(nm_w_ff2), m_nfin]
    new_v = [lead(nv_w_in), v_cw, v_alog, v_dt, v_gn, v_lbl, v_hn, lead(nv_w_out), v_nm, v_nf,
             lead(nv_w_ff1), lead(nv_w_ff2), v_nfin]
    return (loss, dx[None], *grads, *deltas, *new_m, *new_v)
```

```python
import functools

import jax
import jax.numpy as jnp
from jax import lax
from jax.experimental import pallas as pl
from jax.experimental.pallas import tpu as pltpu

F32 = jnp.float32
BF16 = jnp.bfloat16
HI = lax.Precision.HIGHEST

N_DEV = 8
D_MODEL = 2048
CHUNK = 64
SUB_CHUNK = 16
HEAD_DIM = 128
N_HEADS = 8
GDN_WIDTH = N_HEADS * HEAD_DIM
D_FF = 4 * D_MODEL
QKV_WIDTH = 3 * GDN_WIDTH
MAIN_WIDTH = 8 * GDN_WIDTH
CAT_WIDTH = MAIN_WIDTH + 128
AB_BLOCK = MAIN_WIDTH // 128
NORM_EPS = 1e-6
L2_EPS = 1e-6
LANES = 128
VMEM_LIMIT = 56 * 1024 * 1024

ADAM_LR = 0.001
ADAM_B1 = 0.9
ADAM_B2 = 0.999
ADAM_EPS = 1e-08
ADAM_WD = 0.01
ADAM_STEP = 10

MESH = pl.DeviceIdType.MESH


def _params(sem=None):
    return pltpu.CompilerParams(dimension_semantics=sem, vmem_limit_bytes=VMEM_LIMIT)


def _dot(a, b, dims, prec=None):
    return lax.dot_general(a, b, (dims, ((), ())), precision=prec, preferred_element_type=F32)


NN = ((1,), (0,))
NT = ((1,), (1,))
TN = ((0,), (0,))


def _split_bf16(x, pieces):
    out = []
    for _ in range(pieces - 1):
        p = x.astype(BF16)
        out.append(p)
        x = x - p.astype(F32)
    out.append(x.astype(BF16))
    return out


def _mm_raw(a, b, dims, prec):
    if prec == "hi":
        return _dot(a, b, dims, HI)
    if prec == "bf":
        return _dot(a.astype(BF16), b.astype(BF16), dims)
    a_hi, a_lo = _split_bf16(a, 2)
    b_hi, b_lo = _split_bf16(b, 2)
    return _dot(a_hi, b_hi, dims) + (_dot(a_hi, b_lo, dims) + _dot(a_lo, b_hi, dims))


@functools.partial(jax.custom_vjp, nondiff_argnums=(2, 3))
def mm(a, b, dims, prec):
    return _mm_raw(a, b, dims, prec)


def _mm_fwd(a, b, dims, prec):
    return _mm_raw(a, b, dims, prec), (a, b)


def _mm_bwd(dims, prec, res, ct):
    a, b = res
    if dims == NN:
        return _mm_raw(ct, b, NT, prec), _mm_raw(a, ct, TN, prec)
    if dims == NT:
        return _mm_raw(ct, b, NN, prec), _mm_raw(ct, a, TN, prec)
    return _mm_raw(b, ct, NT, prec), _mm_raw(a, ct, NN, prec)


mm.defvjp(_mm_fwd, _mm_bwd)


def _sel_raw(sel, x, dims):
    sel = sel.astype(BF16)
    p0, p1, p2 = _split_bf16(x, 3)
    return _dot(sel, p0, dims) + (_dot(sel, p1, dims) + _dot(sel, p2, dims))


def _sel_parts(sel, x):
    c = x.shape[0]
    full = _sel_raw(sel, x, NN)
    return tuple(full[i * c:(i + 1) * c] for i in range(sel.shape[0] // c))


@jax.custom_vjp
def sel_sums(sel, x):
    return _sel_parts(sel, x)


def _sel_fwd(sel, x):
    return _sel_parts(sel, x), sel


def _sel_bwd(sel, cts):
    return jnp.zeros_like(sel), _sel_raw(sel, jnp.concatenate(cts, axis=0), TN)


sel_sums.defvjp(_sel_fwd, _sel_bwd)


@jax.custom_vjp
def _known_value(computed, known):
    del computed
    return known


_known_value.defvjp(lambda computed, known: (known, None), lambda _, ct: (ct, jnp.zeros_like(ct)))


def _my_flat():
    return 4 * lax.axis_index("x") + 2 * lax.axis_index("y") + lax.axis_index("c")


def _peer(k):
    x, y, c = lax.axis_index("x"), lax.axis_index("y"), lax.axis_index("c")
    kx, ky, kc = (k >> 2) & 1, (k >> 1) & 1, k & 1
    px = (1 - x) if kx else x
    py = (1 - y) if ky else y
    pc = (1 - c) if kc else c
    return (px, py, pc), 4 * px + 2 * py + pc


def gather_two_level(xs, name):
    n = len(xs)

    def body(*refs):
        x_refs, y_refs = refs[:n], refs[n:2 * n]
        send_sems, recv_sems, local_sems = refs[2 * n:]
        x, y, c = lax.axis_index("x"), lax.axis_index("y"), lax.axis_index("c")
        me, sibling = (x, y, c), (x, y, 1 - c)
        chips = [(1 - x, y), (x, 1 - y), (1 - x, 1 - y)]
        flat = lambda p: 4 * p[0] + 2 * p[1] + p[2]

        def copy(a, k, block, to, src=None):
            return pltpu.make_async_remote_copy(
                src_ref=y_refs[a].at[flat(block)] if src is None else src, dst_ref=y_refs[a].at[flat(block)],
                send_sem=send_sems.at[a, k], recv_sem=recv_sems.at[a, k], device_id=to, device_id_type=MESH)

        mine = [pltpu.make_async_copy(x_refs[a], y_refs[a].at[flat(me)], local_sems.at[a]) for a in range(n)]
        for cp in mine:
            cp.start()
        first = [copy(a, 0, me, sibling, src=x_refs[a]) for a in range(n)]
        first += [copy(a, 1 + j, me, (*chip, c), src=x_refs[a]) for j, chip in enumerate(chips) for a in range(n)]
        for cp in first:
            cp.start()
        passed = []
        for j, chip in enumerate(chips):
            for a in range(n):
                copy(a, 1 + j, (*chip, c), me).wait_recv()
                cp = copy(a, 4 + j, (*chip, c), sibling)
                cp.start()
                passed.append(cp)
        for a in range(n):
            copy(a, 0, sibling, me).wait_recv()
        for j, chip in enumerate(chips):
            for a in range(n):
                copy(a, 4 + j, (*chip, 1 - c), me).wait_recv()
        for cp in first + passed:
            cp.wait_send()
        for cp in mine:
            cp.wait()

    any_spec = pl.BlockSpec(memory_space=pl.ANY)
    return pl.pallas_call(
        body, name=name, out_shape=[jax.ShapeDtypeStruct((N_DEV,) + x.shape, x.dtype) for x in xs],
        in_specs=[any_spec] * n, out_specs=[any_spec] * n,
        scratch_shapes=[pltpu.SemaphoreType.DMA((n, N_DEV - 1)), pltpu.SemaphoreType.DMA((n, N_DEV - 1)),
                        pltpu.SemaphoreType.DMA((n,))],
    )(*xs)


HBM_SPEC = pl.BlockSpec(memory_space=pltpu.HBM)
SEM_SPEC = pl.BlockSpec(memory_space=pltpu.SEMAPHORE)
ANY_SPEC = pl.BlockSpec(memory_space=pl.ANY)
DATAFLOW = pltpu.SideEffectType.DATAFLOW_SIDE_EFFECTING


def _in_hbm(x):
    return pltpu.with_memory_space_constraint(x, pltpu.HBM)


ALL_PEERS = tuple(range(1, N_DEV))
CHIP_PEERS = (1, 2, 4, 6)
OTHER_CHIPS = (2, 4, 6)


def exchange_start(xs, gather, name, after=(), peers=ALL_PEERS):
    n, n_after = len(xs), len(after)

    def body(*refs):
        x_refs, land_refs = refs[:n], refs[n:2 * n]
        sems = refs[2 * n + n_after:2 * n + n_after + 2 * n]
        token = refs[-1]
        me = _my_flat()
        for k in peers:
            peer, peer_flat = _peer(k)
            for a in range(n):
                src = x_refs[a] if gather else x_refs[a].at[peer_flat]
                pltpu.make_async_remote_copy(src_ref=src, dst_ref=land_refs[a].at[me], send_sem=sems[a],
                                             recv_sem=sems[n + a], device_id=peer, device_id_type=MESH).start()
        token[...] = jnp.zeros_like(token)

    lands = [_in_hbm(lax.empty(((N_DEV,) + x.shape) if gather else x.shape, x.dtype)) for x in xs]
    hbm_out = [pltpu.HBM(x.shape, x.dtype) for x in xs] + [pltpu.HBM(l.shape, l.dtype) for l in lands]
    res = pl.pallas_call(
        body, name=name,
        out_shape=(*([pltpu.SemaphoreType.DMA(())] * (2 * n)), *hbm_out, jax.ShapeDtypeStruct((8, LANES), F32)),
        in_specs=[HBM_SPEC] * (2 * n) + [ANY_SPEC] * n_after,
        out_specs=(*([SEM_SPEC] * (2 * n)), *([HBM_SPEC] * (2 * n)), pl.BlockSpec(memory_space=pltpu.VMEM)),
        input_output_aliases={i: 2 * n + i for i in range(2 * n)},
        compiler_params=pltpu.CompilerParams(has_side_effects=DATAFLOW),
    )(*[_in_hbm(x) for x in xs], *lands, *after)
    return (list(res[:2 * n]), list(res[2 * n:3 * n]), list(res[3 * n:4 * n])), res[-1]


def forward_start(lands, name, after=()):
    n, n_after = len(lands), len(after)

    def body(*refs):
        land_refs = refs[:n]
        sems = refs[n + n_after:n + n_after + 2 * n]
        token = refs[-1]
        sibling, _ = _peer(1)
        for a in range(n):
            for k in OTHER_CHIPS:
                _, from_flat = _peer(k)
                slot = land_refs[a].at[from_flat]
                pltpu.make_async_remote_copy(src_ref=slot, dst_ref=slot, send_sem=sems[a], recv_sem=sems[n + a],
                                             device_id=sibling, device_id_type=MESH).start()
        token[...] = jnp.zeros_like(token)

    res = pl.pallas_call(
        body, name=name,
        out_shape=(*([pltpu.SemaphoreType.DMA(())] * (2 * n)), *[pltpu.HBM(l.shape, l.dtype) for l in lands],
                   jax.ShapeDtypeStruct((8, LANES), F32)),
        in_specs=[HBM_SPEC] * n + [ANY_SPEC] * n_after,
        out_specs=(*([SEM_SPEC] * (2 * n)), *([HBM_SPEC] * n), pl.BlockSpec(memory_space=pltpu.VMEM)),
        input_output_aliases={i: 2 * n + i for i in range(n)},
        compiler_params=pltpu.CompilerParams(has_side_effects=DATAFLOW),
    )(*lands, *after)
    return (list(res[:2 * n]), [], list(res[2 * n:3 * n])), res[-1]


def exchange_wait(handle, name, after=(), copies=N_DEV - 1):
    sems, xs, lands = handle
    n, n_x, n_after = len(lands), len(xs), len(after)

    def body(*refs):
        land_refs = refs[n_x:n_x + n]
        sem_refs = refs[n_x + n:n_x + 3 * n]
        for a in range(n):
            every = land_refs[a].at[pl.ds(0, copies)]
            cp = pltpu.make_async_remote_copy(src_ref=every, dst_ref=every, send_sem=sem_refs[a],
                                              recv_sem=sem_refs[n + a], device_id=_peer(1)[0], device_id_type=MESH)
            cp.wait_send()
            cp.wait_recv()

    res = pl.pallas_call(
        body, name=name,
        out_shape=[pltpu.HBM(x.shape, x.dtype) for x in xs] + [pltpu.HBM(l.shape, l.dtype) for l in lands],
        in_specs=[HBM_SPEC] * (n_x + n) + [SEM_SPEC] * (2 * n) + [ANY_SPEC] * n_after,
        out_specs=[HBM_SPEC] * (n_x + n),
        input_output_aliases={i: i for i in range(n_x + n)},
        compiler_params=pltpu.CompilerParams(has_side_effects=DATAFLOW),
    )(*xs, *lands, *sems, *after)
    return list(res[:n_x]), list(res[n_x:])


def _one(handle, a):
    sems, xs, lands = handle
    n = len(lands)
    return [sems[a], sems[n + a]], xs[a:a + 1], [lands[a]]


def _own_slot(land, block):
    return lax.dynamic_update_slice(land, block[None], (_my_flat(),) + (0,) * block.ndim)


def allreduce_small(x, name):
    rows = x.shape[0]

    def body(x_ref, o_ref, buf, send_sems, recv_sems):
        me = _my_flat()
        buf[me] = x_ref[...]
        sends = []
        for k in range(1, N_DEV):
            peer, _ = _peer(k)
            cp = pltpu.make_async_remote_copy(
                src_ref=x_ref, dst_ref=buf.at[me], send_sem=send_sems.at[k], recv_sem=recv_sems.at[k],
                device_id=peer, device_id_type=MESH)
            cp.start()
            sends.append(cp)
        for k in range(1, N_DEV):
            peer, peer_flat = _peer(k)
            pltpu.make_async_remote_copy(
                src_ref=x_ref, dst_ref=buf.at[peer_flat], send_sem=send_sems.at[k], recv_sem=recv_sems.at[k],
                device_id=peer, device_id_type=MESH).wait_recv()
        for cp in sends:
            cp.wait_send()
        acc = buf[0]
        for d in range(1, N_DEV):
            acc = acc + buf[d]
        o_ref[...] = acc

    vmem = pl.BlockSpec(memory_space=pltpu.VMEM)
    return pl.pallas_call(
        body, name=name, out_shape=jax.ShapeDtypeStruct((rows, LANES), F32),
        in_specs=[vmem], out_specs=vmem,
        scratch_shapes=[pltpu.VMEM((N_DEV, rows, LANES), F32),
                        pltpu.SemaphoreType.DMA((N_DEV,)), pltpu.SemaphoreType.DMA((N_DEV,))],
    )(x)


def matmul(a, b, mode, name, out_dtypes=(F32,), epilogue=None, extra=None, tm=1024, tn=1024, tk=2048, after=(),
           b_shards=False, out_shards=False, k_group=1):
    if b_shards:
        n_sh, b_rows, b_cols = b.shape
    if mode == "nn":
        (m, kd), n = a.shape, (n_sh * b_cols if b_shards else b.shape[1])
        if b_shards:
            tn = b_cols
    elif mode == "nt":
        (m, kd), n = a.shape, (b_rows if b_shards else b.shape[0])
        if b_shards:
            tk = k_group * b_cols
    else:
        (kd, m), n = a.shape, b.shape[1]
    tm, tn, tk = min(tm, m), min(tn, n), min(tk, kd)
    assert m % tm == 0 and n % tn == 0 and kd % tk == 0, (name, m, n, kd, tm, tn, tk)
    ksteps = kd // tk
    dims = {"nn": NN, "nt": NT, "tn": TN}[mode]
    n_out = len(out_dtypes)
    n_in = 2 + (extra is not None) + len(after)

    def finish(acc, e_ref, o_refs):
        outs = (acc,) if epilogue is None else epilogue(acc, e_ref[...] if e_ref is not None else None)
        for o_ref, o in zip(o_refs, outs):
            o_ref[...] = o.astype(o_ref.dtype)

    def product(a_ref, b_ref):
        if mode == "nt" and b_shards:
            w = b_cols
            parts = [_dot(a_ref[:, s * w:(s + 1) * w], b_ref[s], dims) for s in range(k_group)]
            return functools.reduce(lambda p, q: p + q, parts)
        return _dot(a_ref[...], b_ref[...], dims)

    def body(*refs):
        a_ref, b_ref = refs[0], refs[1]
        e_ref = refs[2] if extra is not None else None
        o_refs = refs[n_in:n_in + n_out]
        if ksteps == 1:
            finish(product(a_ref, b_ref), e_ref, o_refs)
            return
        acc_ref = refs[-1]
        kk = pl.program_id(2)

        @pl.when(kk == 0)
        def _():
            acc_ref[...] = jnp.zeros_like(acc_ref)

        acc_ref[...] += product(a_ref, b_ref)

        @pl.when(kk == ksteps - 1)
        def _():
            finish(acc_ref[...], e_ref, o_refs)

    if mode == "nn":
        a_spec = pl.BlockSpec((tm, tk), lambda i, j, k: (i, k))
        b_spec = (pl.BlockSpec((None, tk, tn), lambda i, j, k: (j, k, 0)) if b_shards
                  else pl.BlockSpec((tk, tn), lambda i, j, k: (k, j)))
    elif mode == "nt":
        a_spec = pl.BlockSpec((tm, tk), lambda i, j, k: (i, k))
        b_spec = (pl.BlockSpec((k_group, tn, b_cols), lambda i, j, k: (k, j, 0)) if b_shards
                  else pl.BlockSpec((tn, tk), lambda i, j, k: (j, k)))
    else:
        a_spec = pl.BlockSpec((tk, tm), lambda i, j, k: (k, i))
        b_spec = pl.BlockSpec((tk, tn), lambda i, j, k: (k, j))
    o_spec = pl.BlockSpec((tm, tn), lambda i, j, k: (i, j))
    res_spec = pl.BlockSpec((None, tm, tn), lambda i, j, k: (j, i, 0)) if out_shards else o_spec
    res_shape = (n // tn, m, tn) if out_shards else (m, n)
    in_specs = [a_spec, b_spec] + ([o_spec] if extra is not None else []) + [ANY_SPEC] * len(after)
    args = (a, b) + ((extra,) if extra is not None else ()) + tuple(after)
    res = pl.pallas_call(
        body, name=name, grid=(m // tm, n // tn, ksteps),
        in_specs=in_specs, out_specs=[res_spec] * n_out,
        out_shape=[jax.ShapeDtypeStruct(res_shape, dt) for dt in out_dtypes],
        scratch_shapes=[pltpu.VMEM((tm, tn), F32)] if ksteps > 1 else [],
        compiler_params=_params(("parallel", "parallel", "arbitrary")),
    )(*args)
    return res if n_out > 1 else res[0]


GATE_COL = 4 * GDN_WIDTH
RELAYOUT_ROWS = 256


def _cat_of_win(j):
    if j < GATE_COL:
        return j
    if j < GATE_COL + 2 * N_HEADS:
        return MAIN_WIDTH + (j - GATE_COL)
    return j - 2 * N_HEADS


def _win_of_cat(c):
    if c < GATE_COL:
        return c
    if c < MAIN_WIDTH:
        return c + 2 * N_HEADS
    if c < MAIN_WIDTH + 2 * N_HEADS:
        return GATE_COL + (c - MAIN_WIDTH)
    return None


def _runs(first, count, mapping):
    runs, i = [], 0
    while i < count:
        start, n = mapping(first + i), 1
        while i + n < count and mapping(first + i + n) == start + n:
            n += 1
        runs.append((start, n))
        i += n
    return runs


def weights_to_cat(g_in):
    n_dev, rows, shard = g_in.shape

    def body(x_ref, o_ref):
        for b in range(CAT_WIDTH // LANES):
            live = sum(_win_of_cat(LANES * b + i) is not None for i in range(LANES))
            parts = []
            for start, n in _runs(LANES * b, live, _win_of_cat):
                while n > 0:
                    d, o = divmod(start, shard)
                    take = min(n, shard - o)
                    parts.append(x_ref[d, :, o:o + take])
                    start, n = start + take, n - take
            if live < LANES:
                parts.append(jnp.zeros((RELAYOUT_ROWS, LANES - live), g_in.dtype))
            o_ref[:, LANES * b:LANES * (b + 1)] = parts[0] if len(parts) == 1 else jnp.concatenate(parts, axis=1)

    return pl.pallas_call(
        body, name="weights_to_cat", grid=(rows // RELAYOUT_ROWS,),
        in_specs=[pl.BlockSpec((n_dev, RELAYOUT_ROWS, shard), lambda i: (0, i, 0))],
        out_specs=pl.BlockSpec((RELAYOUT_ROWS, CAT_WIDTH), lambda i: (i, 0)),
        out_shape=jax.ShapeDtypeStruct((rows, CAT_WIDTH), g_in.dtype),
        compiler_params=_params(("parallel",)))(g_in)


def cat_to_shards(dw_cat, shard):
    rows = dw_cat.shape[0]

    def body(x_ref, o_ref):
        for d in range(N_DEV):
            for t0 in range(0, shard, LANES):
                width = min(LANES, shard - t0)
                parts = [x_ref[:, c:c + n] for c, n in _runs(d * shard + t0, width, _cat_of_win)]
                o_ref[d, :, t0:t0 + width] = parts[0] if len(parts) == 1 else jnp.concatenate(parts, axis=1)

    return pl.pallas_call(
        body, name="cat_to_shards", grid=(rows // RELAYOUT_ROWS,),
        in_specs=[pl.BlockSpec((RELAYOUT_ROWS, CAT_WIDTH), lambda i: (i, 0))],
        out_specs=pl.BlockSpec((N_DEV, RELAYOUT_ROWS, shard), lambda i: (0, i, 0)),
        out_shape=jax.ShapeDtypeStruct((N_DEV, rows, shard), dw_cat.dtype),
        compiler_params=_params(("parallel",)))(dw_cat)


ROW_BLOCK = 512


def rms_fwd(x, w, name):
    t, d = x.shape

    def body(x_ref, w_ref, n_ref, r_ref):
        h = x_ref[...]
        r = lax.rsqrt(jnp.mean(h * h, axis=-1, keepdims=True) + NORM_EPS)
        n_ref[...] = (h * r * w_ref[...]).astype(BF16)
        r_ref[...] = r

    row = pl.BlockSpec((ROW_BLOCK, d), lambda i: (i, 0))
    return pl.pallas_call(
        body, name=name, grid=(t // ROW_BLOCK,),
        in_specs=[row, pl.BlockSpec((1, d), lambda i: (0, 0))],
        out_specs=[row, pl.BlockSpec((ROW_BLOCK, 1), lambda i: (i, 0))],
        out_shape=[jax.ShapeDtypeStruct((t, d), BF16), jax.ShapeDtypeStruct((t, 1), F32)],
        compiler_params=_params(("parallel",)))(x, w)


FUSED_ROWS = 512


def out_proj_rms(y, w_out, x, w_norm, name):
    t, d = x.shape

    def body(y_ref, w_ref, x_ref, g_ref, h_ref, n_ref, r_ref):
        h = x_ref[...] + _dot(y_ref[...], w_ref[...], NN)
        r = lax.rsqrt(jnp.mean(h * h, axis=-1, keepdims=True) + NORM_EPS)
        h_ref[...] = h
        n_ref[...] = (h * r * g_ref[...]).astype(BF16)
        r_ref[...] = r

    row = pl.BlockSpec((FUSED_ROWS, d), lambda i: (i, 0))
    return pl.pallas_call(
        body, name=name, grid=(t // FUSED_ROWS,),
        in_specs=[pl.BlockSpec((FUSED_ROWS, y.shape[1]), lambda i: (i, 0)), pl.BlockSpec(w_out.shape, lambda i: (0, 0)),
                  row, pl.BlockSpec((1, d), lambda i: (0, 0))],
        out_specs=[row, row, pl.BlockSpec((FUSED_ROWS, 1), lambda i: (i, 0))],
        out_shape=[jax.ShapeDtypeStruct((t, d), F32), jax.ShapeDtypeStruct((t, d), BF16),
                   jax.ShapeDtypeStruct((t, 1), F32)],
        compiler_params=_params(("parallel",)))(y, w_out, x, w_norm)


def ff2_loss(act, w_ff2, h1, w, target, name, tk=2048):
    t, d = h1.shape
    ksteps = act.shape[1] // tk

    def body(a_ref, b_ref, h_ref, w_ref, t_ref, loss_ref, dhb_ref, dw_ref, acc_ref):
        i, kk = pl.program_id(0), pl.program_id(1)

        @pl.when((i == 0) & (kk == 0))
        def _():
            loss_ref[...] = jnp.zeros_like(loss_ref)
            dw_ref[...] = jnp.zeros_like(dw_ref)

        @pl.when(kk == 0)
        def _():
            acc_ref[...] = h_ref[...]

        acc_ref[...] += _dot(a_ref[...], b_ref[...], NN)

        @pl.when(kk == ksteps - 1)
        def _():
            h = acc_ref[...]
            wv = w_ref[...]
            r = lax.rsqrt(jnp.mean(h * h, axis=-1, keepdims=True) + NORM_EPS)
            yn = h * r
            e = yn * wv - t_ref[...]
            loss_ref[...] += 0.5 * jnp.sum(jnp.sum(e * e, axis=-1, keepdims=True), axis=0, keepdims=True) / d
            dy = e / d
            dw_ref[...] += jnp.sum(dy * yn, axis=0, keepdims=True)
            dyn = dy * wv
            dhb_ref[...] = (r * (dyn - yn * jnp.mean(dyn * yn, axis=-1, keepdims=True))).astype(BF16)

    row = pl.BlockSpec((FUSED_ROWS, d), lambda i, k: (i, 0))
    wspec = pl.BlockSpec((1, d), lambda i, k: (0, 0))
    return pl.pallas_call(
        body, name=name, grid=(t // FUSED_ROWS, ksteps),
        in_specs=[pl.BlockSpec((FUSED_ROWS, tk), lambda i, k: (i, k)), pl.BlockSpec((tk, d), lambda i, k: (k, 0)),
                  row, wspec, row],
        out_specs=[pl.BlockSpec((1, 1), lambda i, k: (0, 0)), row, wspec],
        out_shape=[jax.ShapeDtypeStruct((1, 1), F32), jax.ShapeDtypeStruct((t, d), BF16),
                   jax.ShapeDtypeStruct((1, d), F32)],
        scratch_shapes=[pltpu.VMEM((FUSED_ROWS, d), F32)],
        compiler_params=_params(("arbitrary", "arbitrary")))(act, w_ff2, h1, w, target)


def rms_bwd(h, r, w, dn, dres, out_dtype, name):
    t, d = h.shape

    def body(h_ref, r_ref, w_ref, dn_ref, dres_ref, dh_ref, dw_ref):
        @pl.when(pl.program_id(0) == 0)
        def _():
            dw_ref[...] = jnp.zeros_like(dw_ref)

        rv = r_ref[...]
        yn = h_ref[...] * rv
        dnv = dn_ref[...].astype(F32)
        dw_ref[...] += jnp.sum(dnv * yn, axis=0, keepdims=True)
        dyn = dnv * w_ref[...]
        dh = dres_ref[...].astype(F32) + rv * (dyn - yn * jnp.mean(dyn * yn, axis=-1, keepdims=True))
        dh_ref[...] = dh.astype(out_dtype)

    row = pl.BlockSpec((ROW_BLOCK, d), lambda i: (i, 0))
    wspec = pl.BlockSpec((1, d), lambda i: (0, 0))
    rspec = pl.BlockSpec((ROW_BLOCK, 1), lambda i: (i, 0))
    return pl.pallas_call(
        body, name=name, grid=(t // ROW_BLOCK,),
        in_specs=[row, rspec, wspec, row, row], out_specs=[row, wspec],
        out_shape=[jax.ShapeDtypeStruct((t, d), out_dtype), jax.ShapeDtypeStruct((1, d), F32)],
        compiler_params=_params(("arbitrary",)))(h, r, w, dn, dres)


CONV_ROWS = 512
TILE_ROWS = 8


def _iota2(shape, axis):
    return lax.broadcasted_iota(jnp.int32, shape, axis)


def _silu(x):
    return x * jax.nn.sigmoid(x)


def _conv_rows(x_ref, w, first, rows):
    acc = None
    for j in range(4):
        term = x_ref[first - 3 + j:first - 3 + j + rows, :] * w[j:j + 1, :]
        acc = term if acc is None else acc + term
    return acc


def _head_shifts(head):
    rows = _iota2((TILE_ROWS, 1), 0)
    return [jnp.where(rows >= 3 - j, head if j == 3 else pltpu.roll(head, 3 - j, 0), 0.0) for j in range(4)]


def _conv_chunks(t):
    pieces = [(TILE_ROWS, min(CONV_ROWS, t) - TILE_ROWS)]
    pieces += [(r, CONV_ROWS) for r in range(CONV_ROWS, t, CONV_ROWS)]
    return pieces


def conv_fwd(proj, conv_w, name):
    t = proj.shape[0]

    def body(x_ref, w_ref, o_ref):
        w = w_ref[...]
        shifted = _head_shifts(x_ref[0:TILE_ROWS, :])
        o_ref[0:TILE_ROWS, :] = _silu(sum(shifted[j] * w[j:j + 1, :] for j in range(4)))
        for first, rows in _conv_chunks(t):
            o_ref[first:first + rows, :] = _silu(_conv_rows(x_ref, w, first, rows))

    col = pl.BlockSpec((t, LANES), lambda c: (0, c))
    return pl.pallas_call(
        body, name=name, grid=(QKV_WIDTH // LANES,),
        in_specs=[col, pl.BlockSpec((4, LANES), lambda c: (0, c))], out_specs=col,
        out_shape=jax.ShapeDtypeStruct((t, QKV_WIDTH), F32),
        compiler_params=_params(("parallel",)))(proj, conv_w)


def conv_bwd(proj, dout, conv_w, dproj, name):
    t = proj.shape[0]

    def dsilu(pre):
        sg = jax.nn.sigmoid(pre)
        return sg * (1.0 + pre * (1.0 - sg))

    def body(x_ref, d_ref, w_ref, dproj_in, dx_ref, dw_ref, stage):
        del dproj_in
        w = w_ref[...]
        shifted = _head_shifts(x_ref[0:TILE_ROWS, :])
        head_dpre = d_ref[0:TILE_ROWS, :] * dsilu(sum(shifted[j] * w[j:j + 1, :] for j in range(4)))
        stage[0:TILE_ROWS, :] = head_dpre
        for first, rows in _conv_chunks(t):
            stage[first:first + rows, :] = d_ref[first:first + rows, :] * dsilu(_conv_rows(x_ref, w, first, rows))
        stage[t:t + TILE_ROWS, :] = jnp.zeros((TILE_ROWS, LANES), F32)
        for first, rows in [(0, TILE_ROWS)] + _conv_chunks(t):
            dx = None
            for j in range(4):
                term = stage[first + 3 - j:first + 3 - j + rows, :] * w[j:j + 1, :]
                dx = term if dx is None else dx + term
            dx_ref[first:first + rows, :] = dx.astype(BF16)
        dw = [jnp.sum(head_dpre * shifted[j], axis=0, keepdims=True) for j in range(4)]
        for first, rows in _conv_chunks(t):
            dpre = stage[first:first + rows, :]
            for j in range(4):
                dw[j] = dw[j] + jnp.sum(dpre * x_ref[first - 3 + j:first - 3 + j + rows, :], axis=0, keepdims=True)
        dw_ref[...] = jnp.concatenate(dw, axis=0)

    col = pl.BlockSpec((t, LANES), lambda c: (0, c))
    taps = pl.BlockSpec((4, LANES), lambda c: (0, c))
    return pl.pallas_call(
        body, name=name, grid=(QKV_WIDTH // LANES,),
        in_specs=[col, col, taps, ANY_SPEC], out_specs=[col, taps],
        out_shape=[jax.ShapeDtypeStruct(dproj.shape, BF16), jax.ShapeDtypeStruct((4, QKV_WIDTH), F32)],
        scratch_shapes=[pltpu.VMEM((t + TILE_ROWS, LANES), F32)],
        input_output_aliases={3: 0},
        compiler_params=_params(("parallel",)))(proj, dout, conv_w, dproj)


def _softplus(x):
    return jnp.maximum(x, 0.0) + jnp.log(1.0 + jnp.exp(-jnp.abs(x)))


def _head_norm_gate(o, norm_w, gate):
    return o * lax.rsqrt(jnp.mean(o * o, axis=-1, keepdims=True) + NORM_EPS) * norm_w * _silu(gate)


GDN_PREC = ("bf", "bf")
HGRN_PREC = "bf"


def _each(fn, *cols):
    return [fn(*a) for a in zip(*cols)]


@functools.partial(jax.custom_vjp, nondiff_argnums=(2,))
def _known_inverse(low, inv, prec):
    del low, prec
    return inv


def _known_inverse_fwd(low, inv, prec):
    del low
    return inv, inv


def _known_inverse_bwd(prec, inv, ct):
    return -_mm_raw(_mm_raw(inv, ct, TN, prec), inv, NT, prec), jnp.zeros_like(inv)


_known_inverse.defvjp(_known_inverse_fwd, _known_inverse_bwd)


def gdn_stages(hs, qc, kc, vc, zc, ab, a_log_l, dt_l, norm_w, s, prec=GDN_PREC, inv_known=None):
    p_inv, p_mm = prec
    c = CHUNK
    ri, ci = _iota2((c, c), 0), _iota2((c, c), 1)
    incl, strict, eye = ri >= ci, ri > ci, ri == ci
    lane = _iota2((c, LANES), 1)
    last_row = _iota2((c, 1), 0) == c - 1
    rowsum = lambda x: jnp.sum(x, axis=1, keepdims=True)

    def row(col):
        return jnp.sum(jnp.where(eye, col, 0.0), axis=0, keepdims=True)

    q = _each(lambda x: x * lax.rsqrt(rowsum(x * x) + L2_EPS) * (HEAD_DIM ** -0.5), qc)
    k = _each(lambda x: x * lax.rsqrt(rowsum(x * x) + L2_EPS), kc)
    yield
    a_col = [rowsum(jnp.where(lane == h, ab, 0.0)) for h in hs]
    b_col = [rowsum(jnp.where(lane == h + N_HEADS, ab, 0.0)) for h in hs]
    beta = _each(jax.nn.sigmoid, b_col)
    g = _each(lambda a, al, dl: rowsum(jnp.where(lane == 0, -jnp.exp(al) * _softplus(a + dl), 0.0)), a_col, a_log_l, dt_l)
    gcum = _each(lambda x: rowsum(jnp.where(incl, row(x), 0.0)), g)
    g_last = _each(lambda x: jnp.sum(jnp.where(last_row, x, 0.0), axis=0, keepdims=True), gcum)
    decay = _each(lambda x: jnp.exp(jnp.where(incl, x - row(x), -jnp.inf)), gcum)
    yield
    kk = _each(lambda x: mm(x, x, NT, p_mm), k)
    low = _each(lambda b, x, d: jnp.where(strict, b * x * d, 0.0), beta, kk, decay)
    yield
    if inv_known is None:
        power = _each(lambda x: -x, low)
        inv = _each(lambda x: jnp.where(eye, 1.0, 0.0) + x, power)
        for _ in range(5):
            power = _each(lambda x: mm(x, x, NN, p_inv), power)
            yield
            inv = _each(lambda x, p: x + mm(x, p, NN, p_inv), inv, power)
            yield
    else:
        inv = _each(lambda x, known: _known_inverse(x, known, p_inv), low, inv_known)
    exp_g = _each(jnp.exp, gcum)
    yield
    u_v = _each(lambda i, b, x: mm(i, b * x, NN, p_mm), inv, beta, vc)
    w = _each(lambda i, b, e, x: mm(i, b * e * x, NN, p_mm), inv, beta, exp_g, k)
    yield
    attn = _each(lambda x, y, d: mm(x, y, NT, p_mm) * d, q, k, decay)
    yield
    u = _each(lambda x, y, z: x - mm(y, z, NN, p_mm), u_v, w, s)
    yield
    o = _each(lambda x, e, z: mm(x * e, z, NN, p_mm), q, exp_g, s)
    o = _each(lambda x, a, y: x + mm(a, y, NN, p_mm), o, attn, u)
    yield
    k_end = _each(lambda x, gl, gc: x * jnp.exp(gl - gc), k, g_last, gcum)
    s_new = _each(lambda z, gl, x, y: z * jnp.exp(gl) + mm(x, y, TN, p_mm), s, g_last, k_end, u)
    return (_each(lambda x, z: _head_norm_gate(x, norm_w, z), o, zc), s_new), inv


def gdn_chunk(h, qc, kc, vc, zc, ab, a_log_l, dt_l, norm_w, s, prec=GDN_PREC, reuse_inverse=False):
    args = ([h], [qc], [kc], [vc], [zc], ab, [a_log_l], [dt_l], norm_w, [s], prec)
    if reuse_inverse:
        inv = lax.stop_gradient(gdn_chunks(*args)[1])
        (y, s_new), _ = gdn_chunks(*args, inv_known=inv)
    else:
        (y, s_new), _ = gdn_chunks(*args)
    return y[0], s_new[0]


DIAG_ROWS = SUB_CHUNK // 2
SHIFT_PAD = 8
SHIFT_ROWS = SHIFT_PAD + CHUNK + SHIFT_PAD
SHIFT_WAYS = 4


class RolledRows:
    def down(self, x, which):
        del which
        return [x] + [pltpu.roll(x, off, 0) for off in range(1, DIAG_ROWS)]

    def up_sum(self, parts, which):
        del which
        acc = parts[0]
        for off in range(1, DIAG_ROWS):
            acc = acc + pltpu.roll(parts[off], CHUNK - off, 0)
        return acc


class SlotRows:
    def __init__(self, slots):
        self.slots = slots

    def down(self, x, which):
        self.slots[which, 0, SHIFT_PAD:SHIFT_PAD + CHUNK, :] = x
        return [x] + [self.slots[which, 0, SHIFT_PAD - off:SHIFT_PAD + CHUNK - off, :] for off in range(1, DIAG_ROWS)]

    def up_sum(self, parts, which):
        acc = parts[0]
        for off in range(1, DIAG_ROWS):
            way = 1 + off % (SHIFT_WAYS - 1)
            self.slots[which, way, SHIFT_PAD:SHIFT_PAD + CHUNK, :] = parts[off]
            acc = acc + self.slots[which, way, SHIFT_PAD + off:SHIFT_PAD + CHUNK + off, :]
        return acc


def _sub_block_rows():
    return jnp.bitwise_and(_iota2((CHUNK, 1), 0), DIAG_ROWS - 1)


def _diag_forward(rows, q, key, bc, v):
    rmod = _sub_block_rows()
    k_d, b_d, v_d = rows.down(key, 0), rows.down(bc, 1), rows.down(v, 2)
    o = None
    for off in range(DIAG_ROWS):
        e = jnp.exp(jnp.where(rmod >= off, bc - b_d[off], -jnp.inf))
        term = jnp.sum(q * k_d[off] * e, axis=-1, keepdims=True) * v_d[off]
        o = term if o is None else o + term
    return o


def _diag_backward(rows, q, key, bc, v, do):
    rmod = _sub_block_rows()
    k_d, b_d, v_d = rows.down(key, 0), rows.down(bc, 1), rows.down(v, 2)
    dq = db = None
    dk_parts, db_parts, dv_parts = [], [], []
    for off in range(DIAG_ROWS):
        e = jnp.exp(jnp.where(rmod >= off, bc - b_d[off], -jnp.inf))
        qe = q * e
        a = jnp.sum(qe * k_d[off], axis=-1, keepdims=True)
        da = jnp.sum(do * v_d[off], axis=-1, keepdims=True)
        dv_parts.append(a * do)
        dq_term = (da * e) * k_d[off]
        dk_term = da * qe
        s = dk_term * k_d[off]
        dq = dq_term if dq is None else dq + dq_term
        db = s if db is None else db + s
        dk_parts.append(dk_term)
        db_parts.append(s)
    return dq, rows.up_sum(dk_parts, 0), db - rows.up_sum(db_parts, 1), rows.up_sum(dv_parts, 2)


def diag_part(rows, differentiable=True):
    forward = functools.partial(_diag_forward, rows)
    if not differentiable:
        return forward
    part = jax.custom_vjp(forward)
    part.defvjp(lambda q, key, bc, v: (forward(q, key, bc, v), (q, key, bc, v)),
                lambda res, do: _diag_backward(rows, *res, do))
    return part


def hgrn_stages(qb, fb, ib, gb, l0, l1, norm_w, st, prec=HGRN_PREC, diags=None, o_known=None):
    c = CHUNK
    ri, ci = _iota2((4 * c, c), 0), _iota2((4 * c, c), 1)
    rcol = _iota2((c, 1), 0)
    blk0 = jnp.bitwise_and(ri, c - SUB_CHUNK)
    limit = jnp.where(ri < c, ri + 1, jnp.where(ri < 2 * c, blk0, jnp.where(ri < 3 * c, blk0 + SUB_CHUNK,
                                                                          blk0 + DIAG_ROWS)))
    sel = jnp.where(ci < limit, 1.0, 0.0)
    ri, ci = _iota2((c, c), 0), _iota2((c, c), 1)
    lb = _each(lambda a, b: jax.nn.sigmoid(a - b), l0, l1)
    forget = _each(lambda b, f: b + (1.0 - b) * jax.nn.sigmoid(f), lb, fb)
    key = _each(lambda b, f: (1.0 - b) * jax.nn.sigmoid(-f), lb, fb)
    q = _each(_silu, qb)
    v = ib
    logf = _each(jnp.log, forget)
    sums = _each(lambda x: sel_sums(sel, x), logf)
    bc, b_start, b_end, b_half = ([x[i] for x in sums] for i in range(4))
    b_last = _each(lambda x: jnp.sum(x, axis=0, keepdims=True), logf)
    o = _each(lambda x, b, z: mm(x * jnp.exp(b), z, NT, prec), q, bc, st)
    if diags is None:
        diags = [diag_part(RolledRows())] * len(qb)
    yield
    o = list(o)
    for h in range(len(o)):
        o[h] = o[h] + diags[h](q[h], key[h], bc[h], v[h])
        yield
    second = jnp.bitwise_and(rcol, SUB_CHUNK - 1) >= DIAG_ROWS
    same_sub = jnp.bitwise_and(ri, c - SUB_CHUNK) == jnp.bitwise_and(ci, c - SUB_CHUNK)
    q_half = _each(lambda x, b, bh: x * jnp.exp(jnp.where(second, b - bh, -jnp.inf)), q, bc, b_half)
    k_half = _each(lambda x, b, bh: x * jnp.exp(jnp.where(second, -jnp.inf, bh - b)), key, bc, b_half)
    a_half = _each(lambda x, z: jnp.where(same_sub, mm(x, z, NT, prec), 0.0), q_half, k_half)
    o = _each(lambda acc, a, val: acc + mm(a, val, NN, prec), o, a_half, v)
    yield
    q_rel = _each(lambda x, b, bs: x * jnp.exp(b - bs), q, bc, b_start)
    k_rel = _each(lambda x, b, be: x * jnp.exp(be - b), key, bc, b_end)
    for y in range(c // SUB_CHUNK - 1):
        def scaled(x, b, bs):
            end_y = jnp.sum(jnp.where(rcol == SUB_CHUNK * y + SUB_CHUNK - 1, b, 0.0), axis=0, keepdims=True)
            return x * jnp.exp(jnp.where(rcol >= SUB_CHUNK * (y + 1), bs - end_y, -jnp.inf))
        dq = _each(scaled, q_rel, bc, b_start)
        in_y = (ci >= SUB_CHUNK * y) & (ci < SUB_CHUNK * (y + 1))
        a_y = _each(lambda x, z: jnp.where(in_y, mm(x, z, NT, prec), 0.0), dq, k_rel)
        o = _each(lambda acc, a, val: acc + mm(a, val, NN, prec), o, a_y, v)
        yield
    k_state = _each(lambda x, bl, b: x * jnp.exp(bl - b), key, b_last, bc)
    st_new = _each(lambda z, bl, val, x: z * jnp.exp(bl) + mm(val, x, TN, prec), st, b_last, v, k_state)
    if o_known is not None:
        o = _each(_known_value, o, o_known)
    return (_each(lambda x, z: _head_norm_gate(x, norm_w, z), o, gb), st_new), o


def _drain(gen):
    try:
        while True:
            next(gen)
    except StopIteration as done:
        return done.value


def _alternate(gen_a, gen_b):
    out, live = [None, None], [gen_a, gen_b]
    while any(g is not None for g in live):
        for i, g in enumerate(live):
            if g is None:
                continue
            try:
                next(g)
            except StopIteration as done:
                out[i], live[i] = done.value, None
    return out


def gdn_chunks(*args, **kwargs):
    return _drain(gdn_stages(*args, **kwargs))


def hgrn_chunks(*args, **kwargs):
    return _drain(hgrn_stages(*args, **kwargs))


def hgrn_chunk(qb, fb, ib, gb, l0, l1, norm_w, st, prec=HGRN_PREC, reuse_output=False):
    args = ([qb], [fb], [ib], [gb], [l0], [l1], norm_w, [st], prec)
    if reuse_output:
        known = lax.stop_gradient(hgrn_chunks(*args)[1])
        (y, st_new), _ = hgrn_chunks(*args, o_known=known)
    else:
        (y, st_new), _ = hgrn_chunks(*args)
    return y[0], st_new[0]


HEAD_VEC = (N_HEADS, 1, LANES)


class _ChunkSpecs:
    def __init__(self, nc, rev):
        self.nc, self.rev = nc, rev

    def _c(self, c):
        return self.nc - 1 - c if self.rev else c

    def row(self, width, block=0):
        return pl.BlockSpec((CHUNK, width), lambda c: (self._c(c), block))

    def per_head(self, rows):
        return pl.BlockSpec((None, N_HEADS, rows, rows), lambda c: (self._c(c), 0, 0, 0))

    @staticmethod
    def whole(shape):
        return pl.BlockSpec(shape, lambda c: (0,) * len(shape))


def _lanes(j):
    return slice(j * LANES, (j + 1) * LANES)


def mixer_fwd(qkv_c, proj, a_log_l, dt_l, gdn_norm_w, l0, l1, hgrn_norm_w, name):
    t = qkv_c.shape[0]
    hb = N_HEADS
    sp = _ChunkSpecs(t // CHUNK, rev=False)
    hs = list(range(hb))

    def body(q_ref, k_ref, v_ref, z_ref, ab_ref, al_ref, dt_ref, gnw_ref, qb_ref, fb_ref, ib_ref, gb_ref, l0_ref, l1_ref,
             hnw_ref, y_ref, hist_a_ref, inv_ref, hist_b_ref, o_ref, sa_ref, sb_ref, shift_ref):
        @pl.when(pl.program_id(0) == 0)
        def _():
            sa_ref[...] = jnp.zeros_like(sa_ref)
            sb_ref[...] = jnp.zeros_like(sb_ref)
            shift_ref[...] = jnp.zeros_like(shift_ref)

        heads = lambda ref: [ref[:, _lanes(j)] for j in hs]
        s_a, s_b = [sa_ref[h] for h in hs], [sb_ref[h] for h in hs]
        for h in hs:
            hist_a_ref[h] = s_a[h]
            hist_b_ref[h] = s_b[h]
        diags = [diag_part(SlotRows(shift_ref.at[h]), differentiable=False) for h in hs]
        ((y_a, s_a_new), inv), ((y_b, s_b_new), o_pre) = _alternate(
            gdn_stages(hs, heads(q_ref), heads(k_ref), heads(v_ref), heads(z_ref), ab_ref[...],
                       [al_ref[h] for h in hs], [dt_ref[h] for h in hs], gnw_ref[...], s_a),
            hgrn_stages(heads(qb_ref), heads(fb_ref), heads(ib_ref), heads(gb_ref),
                        [l0_ref[h] for h in hs], [l1_ref[h] for h in hs], hnw_ref[...], s_b, diags=diags))
        for h in hs:
            y_ref[:, _lanes(h)] = y_a[h].astype(BF16)
            y_ref[:, _lanes(hb + h)] = y_b[h].astype(BF16)
            o_ref[:, _lanes(h)] = o_pre[h]
            sa_ref[h] = s_a_new[h]
            sb_ref[h] = s_b_new[h]
            inv_ref[h] = inv[h]

    vec, gain, slab = sp.whole(HEAD_VEC), sp.whole((1, LANES)), functools.partial(sp.row, GDN_WIDTH)
    states = jax.ShapeDtypeStruct((sp.nc, N_HEADS, HEAD_DIM, HEAD_DIM), F32)
    return pl.pallas_call(
        body, name=name, grid=(sp.nc,),
        in_specs=[slab(0), slab(1), slab(2), slab(3), sp.row(LANES, AB_BLOCK), vec, vec, gain,
                  slab(4), slab(5), slab(6), slab(7), vec, vec, gain],
        out_specs=[sp.row(2 * GDN_WIDTH), sp.per_head(HEAD_DIM), sp.per_head(CHUNK), sp.per_head(HEAD_DIM), slab(0)],
        out_shape=[jax.ShapeDtypeStruct((t, 2 * GDN_WIDTH), BF16), states,
                   jax.ShapeDtypeStruct((sp.nc, N_HEADS, CHUNK, CHUNK), F32), states,
                   jax.ShapeDtypeStruct((t, GDN_WIDTH), F32)],
        scratch_shapes=[pltpu.VMEM((N_HEADS, HEAD_DIM, HEAD_DIM), F32), pltpu.VMEM((N_HEADS, HEAD_DIM, HEAD_DIM), F32),
                        pltpu.VMEM((hb, 3, SHIFT_WAYS, SHIFT_ROWS, LANES), F32)],
        compiler_params=_params(("arbitrary",)),
    )(qkv_c, qkv_c, qkv_c, proj, proj, a_log_l, dt_l, gdn_norm_w, proj, proj, proj, proj, l0, l1, hgrn_norm_w)


def mixer_bwd(qkv_c, proj, a_log_l, dt_l, gdn_norm_w, l0, l1, hgrn_norm_w, hist_a, inv_hist, hist_b, o_pre, dy, name):
    t = qkv_c.shape[0]
    hb = N_HEADS
    sp = _ChunkSpecs(t // CHUNK, rev=True)
    hs = list(range(hb))

    def body(q_ref, k_ref, v_ref, z_ref, ab_ref, al_ref, dt_ref, gnw_ref, qb_ref, fb_ref, ib_ref, gb_ref, l0_ref, l1_ref,
             hnw_ref, hist_a_ref, inv_ref, hist_b_ref, o_ref, dy_ref,
             dqkv_ref, dproj_ref, dal_ref, ddt_ref, dgnw_ref, dl0_ref, dl1_ref, dhnw_ref, dsa_ref, dsb_ref, shift_ref):
        @pl.when(pl.program_id(0) == 0)
        def _():
            for ref in (dal_ref, ddt_ref, dgnw_ref, dl0_ref, dl1_ref, dhnw_ref, dsa_ref, dsb_ref, shift_ref):
                ref[...] = jnp.zeros_like(ref)

        heads = lambda ref, first=0: [ref[:, _lanes(first + j)] for j in hs]
        diags = [diag_part(SlotRows(shift_ref.at[h])) for h in hs]
        inv_known, o_known = [inv_ref[h] for h in hs], heads(o_ref)

        def both(ga, gb):
            (ra, inv), (rb, o_pre) = _alternate(gdn_stages(hs, *ga, inv_known=inv_known),
                                                hgrn_stages(*gb, diags=diags, o_known=o_known))
            return (ra, rb), (inv, o_pre)

        ga = (heads(q_ref), heads(k_ref), heads(v_ref), heads(z_ref), ab_ref[...], [al_ref[h] for h in hs],
              [dt_ref[h] for h in hs], gnw_ref[...], [hist_a_ref[h] for h in hs])
        gb = (heads(qb_ref), heads(fb_ref), heads(ib_ref), heads(gb_ref), [l0_ref[h] for h in hs],
              [l1_ref[h] for h in hs], hnw_ref[...], [hist_b_ref[h] for h in hs])
        _, vjp, _ = jax.vjp(both, ga, gb, has_aux=True)
        dy_a = [x.astype(F32) for x in heads(dy_ref)]
        dy_b = [x.astype(F32) for x in heads(dy_ref, hb)]
        (dq, dk, dv, dz, dab, dal, ddt, dgnw, ds_a), (dqb, dfb, dib, dgb, dl0, dl1, dhnw, ds_b) = vjp(
            ((dy_a, [dsa_ref[h] for h in hs]), (dy_b, [dsb_ref[h] for h in hs])))
        for h in hs:
            dqkv_ref[:, _lanes(h)] = dq[h]
            dqkv_ref[:, _lanes(hb + h)] = dk[h]
            dqkv_ref[:, _lanes(2 * hb + h)] = dv[h]
            for slab, val in enumerate((dz, dqb, dfb, dib, dgb)):
                dproj_ref[:, _lanes((3 + slab) * hb + h)] = val[h].astype(BF16)
            dal_ref[h] += dal[h]
            ddt_ref[h] += ddt[h]
            dl0_ref[h] += dl0[h]
            dl1_ref[h] += dl1[h]
            dsa_ref[h] = ds_a[h]
            dsb_ref[h] = ds_b[h]
        dproj_ref[:, MAIN_WIDTH:] = dab.astype(BF16)
        dgnw_ref[...] += dgnw
        dhnw_ref[...] += dhnw

    vec, gain, slab = sp.whole(HEAD_VEC), sp.whole((1, LANES)), functools.partial(sp.row, GDN_WIDTH)
    vec_shape, gain_shape = jax.ShapeDtypeStruct(HEAD_VEC, F32), jax.ShapeDtypeStruct((1, LANES), F32)
    return pl.pallas_call(
        body, name=name, grid=(sp.nc,),
        in_specs=[slab(0), slab(1), slab(2), slab(3), sp.row(LANES, AB_BLOCK), vec, vec, gain,
                  slab(4), slab(5), slab(6), slab(7), vec, vec, gain,
                  sp.per_head(HEAD_DIM), sp.per_head(CHUNK), sp.per_head(HEAD_DIM), slab(0), sp.row(2 * GDN_WIDTH)],
        out_specs=[sp.row(QKV_WIDTH), sp.row(CAT_WIDTH), vec, vec, gain, vec, vec, gain],
        out_shape=[jax.ShapeDtypeStruct((t, QKV_WIDTH), F32), jax.ShapeDtypeStruct((t, CAT_WIDTH), BF16),
                   vec_shape, vec_shape, gain_shape, vec_shape, vec_shape, gain_shape],
        scratch_shapes=[pltpu.VMEM((N_HEADS, HEAD_DIM, HEAD_DIM), F32), pltpu.VMEM((N_HEADS, HEAD_DIM, HEAD_DIM), F32),
                        pltpu.VMEM((hb, 3, SHIFT_WAYS, SHIFT_ROWS, LANES), F32)],
        compiler_params=_params(("arbitrary",)),
    )(qkv_c, qkv_c, qkv_c, proj, proj, a_log_l, dt_l, gdn_norm_w, proj, proj, proj, proj, l0, l1, hgrn_norm_w,
      hist_a, inv_hist, hist_b, o_pre, dy)


def _adamw(w, g, m, v):
    m = ADAM_B1 * m + (1.0 - ADAM_B1) * g
    v = ADAM_B2 * v + (1.0 - ADAM_B2) * jnp.square(g)
    m_hat = m / (1.0 - ADAM_B1 ** ADAM_STEP)
    v_hat = v / (1.0 - ADAM_B2 ** ADAM_STEP)
    delta = -ADAM_LR * (m_hat / (jnp.sqrt(v_hat) + ADAM_EPS) + ADAM_WD * w)
    return delta, m, v


def adamw_reduce(parts, w, m, v, name, rb=128):
    r, c = w.shape
    rb = min(rb, r)

    def body(p_ref, w_ref, m_ref, v_ref, g_ref, d_ref, mo_ref, vo_ref):
        g = p_ref[0].astype(F32)
        for d in range(1, N_DEV):
            g = g + p_ref[d].astype(F32)
        delta, mn, vn = _adamw(w_ref[...], g, m_ref[...], v_ref[...])
        g_ref[...] = g
        d_ref[...] = delta
        mo_ref[...] = mn
        vo_ref[...] = vn

    blk = pl.BlockSpec((rb, c), lambda i: (i, 0))
    return pl.pallas_call(
        body, name=name, grid=(r // rb,),
        in_specs=[pl.BlockSpec((N_DEV, rb, c), lambda i: (0, i, 0)), blk, blk, blk],
        out_specs=[blk] * 4, out_shape=[jax.ShapeDtypeStruct((r, c), F32)] * 4,
        compiler_params=_params(("parallel",)))(parts, w, m, v)


def adamw_small(w, g, m, v, name):
    def body(w_ref, g_ref, m_ref, v_ref, d_ref, mo_ref, vo_ref):
        delta, mn, vn = _adamw(w_ref[...], g_ref[...], m_ref[...], v_ref[...])
        d_ref[...] = delta
        mo_ref[...] = mn
        vo_ref[...] = vn

    vmem = pl.BlockSpec(memory_space=pltpu.VMEM)
    return pl.pallas_call(body, name=name, in_specs=[vmem] * 4, out_specs=[vmem] * 3,
                          out_shape=[jax.ShapeDtypeStruct(w.shape, F32)] * 3)(w, g, m, v)


def _pack(arrays):
    flat = jnp.concatenate([a.reshape(-1).astype(F32) for a in arrays])
    rows = -(-flat.shape[0] // (8 * LANES)) * 8
    return jnp.pad(flat, (0, rows * LANES - flat.shape[0])).reshape(rows, LANES)


def _unpack(packed, shapes):
    flat, out, off = packed.reshape(-1), [], 0
    for s in shapes:
        n = 1
        for d in s:
            n *= d
        out.append(flat[off:off + n].reshape(s))
        off += n
    return out


def _relu2_epilogue(acc, _):
    r = jnp.maximum(acc, 0.0)
    return acc, r * r


def _relu2_bwd_epilogue(acc, a1):
    return (acc * (2.0 * jnp.maximum(a1, 0.0)),)


def kernel(x, w_in, conv_w, gdn_a_log, gdn_dt_bias, gdn_norm_w, hgrn_lb_logits, hgrn_norm_w, w_out, norm_mix_w, norm_ffn_w, w_ff1, w_ff2, norm_final_w, loss_target, m_w_in, m_conv_w, m_gdn_a_log, m_gdn_dt_bias, m_gdn_norm_w, m_hgrn_lb_logits, m_hgrn_norm_w, m_w_out, m_norm_mix_w, m_norm_ffn_w, m_w_ff1, m_w_ff2, m_norm_final_w, v_w_in, v_conv_w, v_gdn_a_log, v_gdn_dt_bias, v_gdn_norm_w, v_hgrn_lb_logits, v_hgrn_norm_w, v_w_out, v_norm_mix_w, v_norm_ffn_w, v_w_ff1, v_w_ff2, v_norm_final_w):
    me = _my_flat()
    xs = x[0]
    target = loss_target[0]
    shard_in = w_in.shape[2]
    shard_conv = conv_w.shape[2]

    tok = lambda t: t[0:1, 0:1]
    own = lambda src: lax.dynamic_index_in_dim(src, me, 0, keepdims=False)

    g_in, g_conv = gather_two_level([w_in[0].astype(BF16), conv_w[0]], "gather_w_in")
    h_g1, t_g1 = exchange_start([w_out[0].astype(BF16), w_ff1[0].astype(BF16)], True, "gather_mid_start", after=[g_in],
                                peers=CHIP_PEERS)
    h_g2, t_g2 = exchange_start([w_ff2[0].astype(BF16)], True, "gather_ff2_start", after=[t_g1], peers=CHIP_PEERS)
    w_cat = weights_to_cat(g_in)
    conv_full = jnp.transpose(g_conv, (1, 0, 2)).reshape(4, QKV_WIDTH)

    lane_b = lambda p: jnp.broadcast_to(p.reshape(N_HEADS, 1, 1), HEAD_VEC)
    a_log_l, dt_l = lane_b(gdn_a_log[0]), lane_b(gdn_dt_bias[0])
    l0 = hgrn_lb_logits[0].reshape(HEAD_VEC)
    l1 = hgrn_lb_logits[1].reshape(HEAD_VEC)

    n1, r1 = rms_fwd(xs, norm_mix_w + tok(t_g1) + tok(t_g2), "rms_mix")
    proj = matmul(n1, w_cat, "nn", "in_proj", tn=CAT_WIDTH // 5)
    qkv_c = conv_fwd(proj, conv_full, "conv_fwd")
    y, hist_a, inv_a, hist_b, o_b = mixer_fwd(qkv_c, proj, a_log_l, dt_l, gdn_norm_w, l0, l1, hgrn_norm_w, "mixer_fwd")
    (s_out, s_ff1), (l_out, l_ff1) = exchange_wait(h_g1, "gather_mid_wait", after=[y], copies=len(CHIP_PEERS))
    (s_ff2,), (l_ff2,) = exchange_wait(h_g2, "gather_ff2_wait", after=[y], copies=len(CHIP_PEERS))
    h_fw, _ = forward_start([l_out, l_ff1, l_ff2], "gather_forward_start")
    _, (l_out,) = exchange_wait(_one(h_fw, 0), "forward_out_wait", copies=len(OTHER_CHIPS))
    w_out_full = _own_slot(l_out, s_out).reshape(D_MODEL, D_MODEL)
    h1, n2, r2 = out_proj_rms(y, w_out_full, xs, norm_ffn_w, "out_proj_rms")
    _, (l_ff1,) = exchange_wait(_one(h_fw, 1), "forward_ff1_wait", after=[n2], copies=len(OTHER_CHIPS))
    w_ff1_sh = _own_slot(l_ff1, s_ff1)
    a1, act = matmul(n2, w_ff1_sh, "nn", "ff1", out_dtypes=(F32, BF16), epilogue=_relu2_epilogue, b_shards=True)
    _, (l_ff2,) = exchange_wait(_one(h_fw, 2), "forward_ff2_wait", after=[act], copies=len(OTHER_CHIPS))
    w_ff2_full = _own_slot(l_ff2, s_ff2).reshape(D_FF, D_MODEL)
    loss_sum, dh2_b, d_final = ff2_loss(act, w_ff2_full, h1, norm_final_w.reshape(1, D_MODEL), target, "ff2_loss")

    da1 = matmul(dh2_b, w_ff2_full, "nt", "d_act", out_dtypes=(BF16,), epilogue=_relu2_bwd_epilogue, extra=a1)
    t_all = xs.shape[0]
    dw_ff2 = matmul(act, dh2_b, "tn", "dw_ff2", out_dtypes=(BF16,), tk=t_all)
    p_ff2 = dw_ff2.reshape(N_DEV, D_FF // N_DEV, D_MODEL)
    h_s1, t_s1 = exchange_start([p_ff2], False, "scatter_ff2_start")
    dn2 = matmul(da1, w_ff1_sh, "nt", "d_n2", out_dtypes=(BF16,), after=[t_s1], b_shards=True, k_group=4)
    p_ff1 = matmul(n2, da1, "tn", "dw_ff1", out_dtypes=(BF16,), tn=D_FF // N_DEV, tk=t_all, after=[t_s1], out_shards=True)
    h_s2, t_s2 = exchange_start([p_ff1], False, "scatter_ff1_start")
    dh1_b, d_ffn = rms_bwd(h1, r2, norm_ffn_w + tok(t_s2), dn2, dh2_b, BF16, "rms_ffn_bwd")
    dmix = matmul(dh1_b, w_out_full, "nt", "d_mix", out_dtypes=(BF16,))
    dw_out = matmul(y, dh1_b, "tn", "dw_out", out_dtypes=(BF16,), tk=t_all)
    p_out = dw_out.reshape(N_DEV, D_MODEL // N_DEV, D_MODEL)
    h_s3, t_s3 = exchange_start([p_out], False, "scatter_out_start")
    d_qkv_c, dproj, d_alog_l, d_dt_l, d_gnw, dl0, dl1, d_hnw = mixer_bwd(
        qkv_c, proj, a_log_l, dt_l, gdn_norm_w + tok(t_s3), l0, l1, hgrn_norm_w, hist_a, inv_a, hist_b, o_b, dmix,
        "mixer_bwd")
    dproj, d_conv_full = conv_bwd(proj, d_qkv_c, conv_full, dproj, "conv_bwd")
    dw_cat = matmul(n1, dproj, "tn", "dw_in", out_dtypes=(BF16,), tm=512, tn=CAT_WIDTH // 5, tk=t_all)
    p_in = cat_to_shards(dw_cat, shard_in)
    h_s4, t_s4 = exchange_start([p_in], False, "scatter_in_start")

    (s_ff2g,), (r_ff2,) = exchange_wait(h_s1, "scatter_ff2_wait", after=[t_s4])
    (s_ff1g,), (r_ff1,) = exchange_wait(h_s2, "scatter_ff1_wait", after=[t_s4])
    (s_outg,), (r_out,) = exchange_wait(h_s3, "scatter_out_wait", after=[t_s4])
    g_w_ff2, d_w_ff2, nm_w_ff2, nv_w_ff2 = adamw_reduce(
        _own_slot(r_ff2, own(s_ff2g)), w_ff2[0], m_w_ff2[0], v_w_ff2[0], "adamw_w_ff2")
    g_w_ff1, d_w_ff1, nm_w_ff1, nv_w_ff1 = adamw_reduce(
        _own_slot(r_ff1, own(s_ff1g)), w_ff1[0], m_w_ff1[0], v_w_ff1[0], "adamw_w_ff1")
    g_w_out, d_w_out, nm_w_out, nv_w_out = adamw_reduce(
        _own_slot(r_out, own(s_outg)), w_out[0], m_w_out[0], v_w_out[0], "adamw_w_out")
    dn1 = matmul(dproj, w_cat, "nt", "d_n1", out_dtypes=(BF16,), tk=CAT_WIDTH // 5, after=[t_s4])
    dx, d_mix = rms_bwd(xs, r1, norm_mix_w, dn1, dh1_b, F32, "rms_mix_bwd")
    (s_ing,), (r_in,) = exchange_wait(h_s4, "scatter_in_wait", after=[dx, d_w_ff2, d_w_ff1, d_w_out])
    g_w_in, d_w_in, nm_w_in, nv_w_in = adamw_reduce(
        _own_slot(r_in, own(s_ing)), w_in[0], m_w_in[0], v_w_in[0], "adamw_w_in")

    d_lb = jnp.stack([dl0.reshape(GDN_WIDTH), dl1.reshape(GDN_WIDTH)])
    small_shapes = [(1, N_HEADS), (1, N_HEADS), (1, HEAD_DIM), (2, GDN_WIDTH), (1, HEAD_DIM), (1, D_MODEL),
                    (1, D_MODEL), (D_MODEL,), (4, QKV_WIDTH)]
    small = _pack([d_alog_l[:, 0, 0], d_dt_l[:, 0, 0], d_gnw, d_lb, d_hnw, d_mix, d_ffn, d_final, d_conv_full])
    red = allreduce_small(small, "allreduce_small")
    g_alog, g_dt, g_gnw, g_lb, g_hnw, g_mix, g_ffn, g_final, g_conv_full = _unpack(red, small_shapes)
    g_conv = lax.dynamic_slice(g_conv_full, (0, me * shard_conv), (4, shard_conv)).reshape(1, 4, shard_conv)
    small_g = [g_alog, g_dt, g_gnw, g_lb, g_hnw, g_mix, g_ffn, g_final, g_conv]
    small_w = [gdn_a_log, gdn_dt_bias, gdn_norm_w, hgrn_lb_logits, hgrn_norm_w, norm_mix_w, norm_ffn_w, norm_final_w, conv_w]
    small_m = [m_gdn_a_log, m_gdn_dt_bias, m_gdn_norm_w, m_hgrn_lb_logits, m_hgrn_norm_w, m_norm_mix_w, m_norm_ffn_w,
               m_norm_final_w, m_conv_w]
    small_v = [v_gdn_a_log, v_gdn_dt_bias, v_gdn_norm_w, v_hgrn_lb_logits, v_hgrn_norm_w, v_norm_mix_w, v_norm_ffn_w,
               v_norm_final_w, v_conv_w]
    shapes = [a.shape for a in small_w]
    d_s, m_s, v_s = adamw_small(_pack(small_w), _pack(small_g), _pack(small_m), _pack(small_v), "adamw_small")
    d_alog, d_dt, d_gn, d_lbl, d_hn, d_nm, d_nf, d_nfin, d_cw = _unpack(d_s, shapes)
    m_alog, m_dt, m_gn, m_lbl, m_hn, m_nm, m_nf, m_nfin, m_cw = _unpack(m_s, shapes)
    v_alog, v_dt, v_gn, v_lbl, v_hn, v_nm, v_nf, v_nfin, v_cw = _unpack(v_s, shapes)

    loss = lax.psum(loss_sum[0, 0], ("x", "y", "c"))
    lead = lambda a: a[None]
    grads = [lead(g_w_in), g_conv, g_alog, g_dt, g_gnw, g_lb, g_hnw, lead(g_w_out), g_mix, g_ffn,
             lead(g_w_ff1), lead(g_w_ff2), g_final]
    deltas = [lead(d_w_in), d_cw, d_alog, d_dt, d_gn, d_lbl, d_hn, lead(d_w_out), d_nm, d_nf,
              lead(d_w_ff1), lead(d_w_ff2), d_nfin]
    new_m = [lead(nm_w_in), m_cw, m_alog, m_dt, m_gn, m_lbl, m_hn, lead(nm_w_out), m_nm, m_nf,
             lead(nm_w_ff1), lead(nm_w_ff2), m_nfin]
    new_v = [lead(nv_w_in), v_cw, v_alog, v_dt, v_gn, v_lbl, v_hn, lead(nv_w_out), v_nm, v_nf,
             lead(nv_w_ff1), lead(nv_w_ff2), v_nfin]
    return (loss, dx[None], *grads, *deltas, *new_m, *new_v)
```

```python
import functools

import jax
import jax.numpy as jnp
from jax import lax
from jax.experimental import pallas as pl
from jax.experimental.pallas import tpu as pltpu

F32 = jnp.float32
BF16 = jnp.bfloat16
HI = lax.Precision.HIGHEST

N_DEV = 8
D_MODEL = 2048
CHUNK = 64
SUB_CHUNK = 16
HEAD_DIM = 128
N_HEADS = 8
GDN_WIDTH = N_HEADS * HEAD_DIM
D_FF = 4 * D_MODEL
QKV_WIDTH = 3 * GDN_WIDTH
MAIN_WIDTH = 8 * GDN_WIDTH
CAT_WIDTH = MAIN_WIDTH + 128
AB_BLOCK = MAIN_WIDTH // 128
NORM_EPS = 1e-6
L2_EPS = 1e-6
LANES = 128
VMEM_LIMIT = 56 * 1024 * 1024

ADAM_LR = 0.001
ADAM_B1 = 0.9
ADAM_B2 = 0.999
ADAM_EPS = 1e-08
ADAM_WD = 0.01
ADAM_STEP = 10

MESH = pl.DeviceIdType.MESH


def _params(sem=None):
    return pltpu.CompilerParams(dimension_semantics=sem, vmem_limit_bytes=VMEM_LIMIT)


def _dot(a, b, dims, prec=None):
    return lax.dot_general(a, b, (dims, ((), ())), precision=prec, preferred_element_type=F32)


NN = ((1,), (0,))
NT = ((1,), (1,))
TN = ((0,), (0,))


def _split_bf16(x, pieces):
    out = []
    for _ in range(pieces - 1):
        p = x.astype(BF16)
        out.append(p)
        x = x - p.astype(F32)
    out.append(x.astype(BF16))
    return out


def _mm_raw(a, b, dims, prec):
    if prec == "hi":
        return _dot(a, b, dims, HI)
    if prec == "bf":
        return _dot(a.astype(BF16), b.astype(BF16), dims)
    a_hi, a_lo = _split_bf16(a, 2)
    b_hi, b_lo = _split_bf16(b, 2)
    return _dot(a_hi, b_hi, dims) + (_dot(a_hi, b_lo, dims) + _dot(a_lo, b_hi, dims))


@functools.partial(jax.custom_vjp, nondiff_argnums=(2, 3))
def mm(a, b, dims, prec):
    return _mm_raw(a, b, dims, prec)


def _mm_fwd(a, b, dims, prec):
    return _mm_raw(a, b, dims, prec), (a, b)


def _mm_bwd(dims, prec, res, ct):
    a, b = res
    if dims == NN:
        return _mm_raw(ct, b, NT, prec), _mm_raw(a, ct, TN, prec)
    if dims == NT:
        return _mm_raw(ct, b, NN, prec), _mm_raw(ct, a, TN, prec)
    return _mm_raw(b, ct, NT, prec), _mm_raw(a, ct, NN, prec)


mm.defvjp(_mm_fwd, _mm_bwd)


def _sel_raw(sel, x, dims):
    sel = sel.astype(BF16)
    p0, p1, p2 = _split_bf16(x, 3)
    return _dot(sel, p0, dims) + (_dot(sel, p1, dims) + _dot(sel, p2, dims))


def _sel_parts(sel, x):
    c = x.shape[0]
    full = _sel_raw(sel, x, NN)
    return tuple(full[i * c:(i + 1) * c] for i in range(sel.shape[0] // c))


@jax.custom_vjp
def sel_sums(sel, x):
    return _sel_parts(sel, x)


def _sel_fwd(sel, x):
    return _sel_parts(sel, x), sel


def _sel_bwd(sel, cts):
    return jnp.zeros_like(sel), _sel_raw(sel, jnp.concatenate(cts, axis=0), TN)


sel_sums.defvjp(_sel_fwd, _sel_bwd)


@jax.custom_vjp
def _known_value(computed, known):
    del computed
    return known


_known_value.defvjp(lambda computed, known: (known, None), lambda _, ct: (ct, jnp.zeros_like(ct)))


def _my_flat():
    return 4 * lax.axis_index("x") + 2 * lax.axis_index("y") + lax.axis_index("c")


def _peer(k):
    x, y, c = lax.axis_index("x"), lax.axis_index("y"), lax.axis_index("c")
    kx, ky, kc = (k >> 2) & 1, (k >> 1) & 1, k & 1
    px = (1 - x) if kx else x
    py = (1 - y) if ky else y
    pc = (1 - c) if kc else c
    return (px, py, pc), 4 * px + 2 * py + pc


def gather_two_level(xs, name):
    n = len(xs)

    def body(*refs):
        x_refs, y_refs = refs[:n], refs[n:2 * n]
        send_sems, recv_sems, local_sems = refs[2 * n:]
        x, y, c = lax.axis_index("x"), lax.axis_index("y"), lax.axis_index("c")
        me, sibling = (x, y, c), (x, y, 1 - c)
        chips = [(1 - x, y), (x, 1 - y), (1 - x, 1 - y)]
        flat = lambda p: 4 * p[0] + 2 * p[1] + p[2]

        def copy(a, k, block, to, src=None):
            return pltpu.make_async_remote_copy(
                src_ref=y_refs[a].at[flat(block)] if src is None else src, dst_ref=y_refs[a].at[flat(block)],
                send_sem=send_sems.at[a, k], recv_sem=recv_sems.at[a, k], device_id=to, device_id_type=MESH)

        mine = [pltpu.make_async_copy(x_refs[a], y_refs[a].at[flat(me)], local_sems.at[a]) for a in range(n)]
        for cp in mine:
            cp.start()
        first = [copy(a, 0, me, sibling, src=x_refs[a]) for a in range(n)]
        first += [copy(a, 1 + j, me, (*chip, c), src=x_refs[a]) for j, chip in enumerate(chips) for a in range(n)]
        for cp in first:
            cp.start()
        passed = []
        for j, chip in enumerate(chips):
            for a in range(n):
                copy(a, 1 + j, (*chip, c), me).wait_recv()
                cp = copy(a, 4 + j, (*chip, c), sibling)
                cp.start()
                passed.append(cp)
        for a in range(n):
            copy(a, 0, sibling, me).wait_recv()
        for j, chip in enumerate(chips):
            for a in range(n):
                copy(a, 4 + j, (*chip, 1 - c), me).wait_recv()
        for cp in first + passed:
            cp.wait_send()
        for cp in mine:
            cp.wait()

    any_spec = pl.BlockSpec(memory_space=pl.ANY)
    return pl.pallas_call(
        body, name=name, out_shape=[jax.ShapeDtypeStruct((N_DEV,) + x.shape, x.dtype) for x in xs],
        in_specs=[any_spec] * n, out_specs=[any_spec] * n,
        scratch_shapes=[pltpu.SemaphoreType.DMA((n, N_DEV - 1)), pltpu.SemaphoreType.DMA((n, N_DEV - 1)),
                        pltpu.SemaphoreType.DMA((n,))],
    )(*xs)


HBM_SPEC = pl.BlockSpec(memory_space=pltpu.HBM)
SEM_SPEC = pl.BlockSpec(memory_space=pltpu.SEMAPHORE)
ANY_SPEC = pl.BlockSpec(memory_space=pl.ANY)
DATAFLOW = pltpu.SideEffectType.DATAFLOW_SIDE_EFFECTING


def _in_hbm(x):
    return pltpu.with_memory_space_constraint(x, pltpu.HBM)


ALL_PEERS = tuple(range(1, N_DEV))
CHIP_PEERS = (1, 2, 4, 6)
OTHER_CHIPS = (2, 4, 6)


def exchange_start(xs, gather, name, after=(), peers=ALL_PEERS):
    n, n_after = len(xs), len(after)

    def body(*refs):
        x_refs, land_refs = refs[:n], refs[n:2 * n]
        sems = refs[2 * n + n_after:2 * n + n_after + 2 * n]
        token = refs[-1]
        me = _my_flat()
        for k in peers:
            peer, peer_flat = _peer(k)
            for a in range(n):
                src = x_refs[a] if gather else x_refs[a].at[peer_flat]
                pltpu.make_async_remote_copy(src_ref=src, dst_ref=land_refs[a].at[me], send_sem=sems[a],
                                             recv_sem=sems[n + a], device_id=peer, device_id_type=MESH).start()
        token[...] = jnp.zeros_like(token)

    lands = [_in_hbm(lax.empty(((N_DEV,) + x.shape) if gather else x.shape, x.dtype)) for x in xs]
    hbm_out = [pltpu.HBM(x.shape, x.dtype) for x in xs] + [pltpu.HBM(l.shape, l.dtype) for l in lands]
    res = pl.pallas_call(
        body, name=name,
        out_shape=(*([pltpu.SemaphoreType.DMA(())] * (2 * n)), *hbm_out, jax.ShapeDtypeStruct((8, LANES), F32)),
        in_specs=[HBM_SPEC] * (2 * n) + [ANY_SPEC] * n_after,
        out_specs=(*([SEM_SPEC] * (2 * n)), *([HBM_SPEC] * (2 * n)), pl.BlockSpec(memory_space=pltpu.VMEM)),
        input_output_aliases={i: 2 * n + i for i in range(2 * n)},
        compiler_params=pltpu.CompilerParams(has_side_effects=DATAFLOW),
    )(*[_in_hbm(x) for x in xs], *lands, *after)
    return (list(res[:2 * n]), list(res[2 * n:3 * n]), list(res[3 * n:4 * n])), res[-1]


def forward_start(lands, name, after=()):
    n, n_after = len(lands), len(after)

    def body(*refs):
        land_refs = refs[:n]
        sems = refs[n + n_after:n + n_after + 2 * n]
        token = refs[-1]
        sibling, _ = _peer(1)
        for a in range(n):
            for k in OTHER_CHIPS:
                _, from_flat = _peer(k)
                slot = land_refs[a].at[from_flat]
                pltpu.make_async_remote_copy(src_ref=slot, dst_ref=slot, send_sem=sems[a], recv_sem=sems[n + a],
                                             device_id=sibling, device_id_type=MESH).start()
        token[...] = jnp.zeros_like(token)

    res = pl.pallas_call(
        body, name=name,
        out_shape=(*([pltpu.SemaphoreType.DMA(())] * (2 * n)), *[pltpu.HBM(l.shape, l.dtype) for l in lands],
                   jax.ShapeDtypeStruct((8, LANES), F32)),
        in_specs=[HBM_SPEC] * n + [ANY_SPEC] * n_after,
        out_specs=(*([SEM_SPEC] * (2 * n)), *([HBM_SPEC] * n), pl.BlockSpec(memory_space=pltpu.VMEM)),
        input_output_aliases={i: 2 * n + i for i in range(n)},
        compiler_params=pltpu.CompilerParams(has_side_effects=DATAFLOW),
    )(*lands, *after)
    return (list(res[:2 * n]), [], list(res[2 * n:3 * n])), res[-1]


def exchange_wait(handle, name, after=(), copies=N_DEV - 1):
    sems, xs, lands = handle
    n, n_x, n_after = len(lands), len(xs), len(after)

    def body(*refs):
        land_refs = refs[n_x:n_x + n]
        sem_refs = refs[n_x + n:n_x + 3 * n]
        for a in range(n):
            every = land_refs[a].at[pl.ds(0, copies)]
            cp = pltpu.make_async_remote_copy(src_ref=every, dst_ref=every, send_sem=sem_refs[a],
                                              recv_sem=sem_refs[n + a], device_id=_peer(1)[0], device_id_type=MESH)
            cp.wait_send()
            cp.wait_recv()

    res = pl.pallas_call(
        body, name=name,
        out_shape=[pltpu.HBM(x.shape, x.dtype) for x in xs] + [pltpu.HBM(l.shape, l.dtype) for l in lands],
        in_specs=[HBM_SPEC] * (n_x + n) + [SEM_SPEC] * (2 * n) + [ANY_SPEC] * n_after,
        out_specs=[HBM_SPEC] * (n_x + n),
        input_output_aliases={i: i for i in range(n_x + n)},
        compiler_params=pltpu.CompilerParams(has_side_effects=DATAFLOW),
    )(*xs, *lands, *sems, *after)
    return list(res[:n_x]), list(res[n_x:])


N_CHIPS = N_DEV // 2


def routed_start(x, routes, name, after=()):
    n_after = len(after)

    def body(*refs):
        x_ref, land_ref = refs[0], refs[1]
        send_sem, recv_sem = refs[2 + n_after], refs[3 + n_after]
        token = refs[-1]
        for src, dst, peer in routes():
            pltpu.make_async_remote_copy(src_ref=x_ref.at[src], dst_ref=land_ref.at[dst], send_sem=send_sem,
                                         recv_sem=recv_sem, device_id=peer, device_id_type=MESH).start()
        token[...] = jnp.zeros_like(token)

    land = _in_hbm(lax.empty((N_CHIPS,) + x.shape[1:], x.dtype))
    res = pl.pallas_call(
        body, name=name,
        out_shape=(pltpu.SemaphoreType.DMA(()), pltpu.SemaphoreType.DMA(()), pltpu.HBM(x.shape, x.dtype),
                   pltpu.HBM(land.shape, land.dtype), jax.ShapeDtypeStruct((8, LANES), F32)),
        in_specs=[HBM_SPEC, HBM_SPEC] + [ANY_SPEC] * n_after,
        out_specs=(SEM_SPEC, SEM_SPEC, HBM_SPEC, HBM_SPEC, pl.BlockSpec(memory_space=pltpu.VMEM)),
        input_output_aliases={0: 2, 1: 3},
        compiler_params=pltpu.CompilerParams(has_side_effects=DATAFLOW),
    )(_in_hbm(x), land, *after)
    return ([res[0], res[1]], [res[2]], [res[3]]), res[-1]


def _to_sibling_routes():
    c = lax.axis_index("c")
    sibling, _ = _peer(1)
    return [(2 * chip + 1 - c, chip, sibling) for chip in range(N_CHIPS)]


def _to_chips_routes():
    my_chip = _my_flat() // 2
    routes = []
    for k in OTHER_CHIPS:
        peer, peer_flat = _peer(k)
        routes.append((peer_flat // 2, my_chip, peer))
    return routes


def pair_sum(p, from_sibling, name, rb=256):
    _, r, c = p.shape
    mine = lax.axis_index("c").astype(jnp.int32).reshape(1)

    def body(kind_ref, p_ref, s_ref, o_ref):
        del kind_ref
        o_ref[...] = (p_ref[...].astype(F32) + s_ref[...].astype(F32)).astype(BF16)

    return pl.pallas_call(
        body, name=name,
        grid_spec=pltpu.PrefetchScalarGridSpec(
            num_scalar_prefetch=1, grid=(N_CHIPS, r // rb),
            in_specs=[pl.BlockSpec((None, None, rb, c), lambda chip, i, kind: (chip, kind[0], i, 0)),
                      pl.BlockSpec((None, rb, c), lambda chip, i, kind: (chip, i, 0))],
            out_specs=pl.BlockSpec((None, rb, c), lambda chip, i, kind: (chip, i, 0))),
        out_shape=jax.ShapeDtypeStruct((N_CHIPS, r, c), BF16),
        compiler_params=_params(("parallel", "parallel")))(mine, p.reshape(N_CHIPS, 2, r, c), from_sibling)


def _one(handle, a):
    sems, xs, lands = handle
    n = len(lands)
    return [sems[a], sems[n + a]], xs[a:a + 1], [lands[a]]


def _own_slot(land, block):
    return lax.dynamic_update_slice(land, block[None], (_my_flat(),) + (0,) * block.ndim)


def allreduce_small(x, name):
    rows = x.shape[0]

    def body(x_ref, o_ref, buf, send_sems, recv_sems):
        me = _my_flat()
        buf[me] = x_ref[...]
        sends = []
        for k in range(1, N_DEV):
            peer, _ = _peer(k)
            cp = pltpu.make_async_remote_copy(
                src_ref=x_ref, dst_ref=buf.at[me], send_sem=send_sems.at[k], recv_sem=recv_sems.at[k],
                device_id=peer, device_id_type=MESH)
            cp.start()
            sends.append(cp)
        for k in range(1, N_DEV):
            peer, peer_flat = _peer(k)
            pltpu.make_async_remote_copy(
                src_ref=x_ref, dst_ref=buf.at[peer_flat], send_sem=send_sems.at[k], recv_sem=recv_sems.at[k],
                device_id=peer, device_id_type=MESH).wait_recv()
        for cp in sends:
            cp.wait_send()
        acc = buf[0]
        for d in range(1, N_DEV):
            acc = acc + buf[d]
        o_ref[...] = acc

    vmem = pl.BlockSpec(memory_space=pltpu.VMEM)
    return pl.pallas_call(
        body, name=name, out_shape=jax.ShapeDtypeStruct((rows, LANES), F32),
        in_specs=[vmem], out_specs=vmem,
        scratch_shapes=[pltpu.VMEM((N_DEV, rows, LANES), F32),
                        pltpu.SemaphoreType.DMA((N_DEV,)), pltpu.SemaphoreType.DMA((N_DEV,))],
    )(x)


def matmul(a, b, mode, name, out_dtypes=(F32,), epilogue=None, extra=None, tm=1024, tn=1024, tk=2048, after=(),
           b_shards=False, out_shards=False, k_group=1):
    if b_shards:
        n_sh, b_rows, b_cols = b.shape
    if mode == "nn":
        (m, kd), n = a.shape, (n_sh * b_cols if b_shards else b.shape[1])
        if b_shards:
            tn = b_cols
    elif mode == "nt":
        (m, kd), n = a.shape, (b_rows if b_shards else b.shape[0])
        if b_shards:
            tk = k_group * b_cols
    else:
        (kd, m), n = a.shape, b.shape[1]
    tm, tn, tk = min(tm, m), min(tn, n), min(tk, kd)
    assert m % tm == 0 and n % tn == 0 and kd % tk == 0, (name, m, n, kd, tm, tn, tk)
    ksteps = kd // tk
    dims = {"nn": NN, "nt": NT, "tn": TN}[mode]
    n_out = len(out_dtypes)
    n_in = 2 + (extra is not None) + len(after)

    def finish(acc, e_ref, o_refs):
        outs = (acc,) if epilogue is None else epilogue(acc, e_ref[...] if e_ref is not None else None)
        for o_ref, o in zip(o_refs, outs):
            o_ref[...] = o.astype(o_ref.dtype)

    def product(a_ref, b_ref):
        if mode == "nt" and b_shards:
            w = b_cols
            parts = [_dot(a_ref[:, s * w:(s + 1) * w], b_ref[s], dims) for s in range(k_group)]
            return functools.reduce(lambda p, q: p + q, parts)
        return _dot(a_ref[...], b_ref[...], dims)

    def body(*refs):
        a_ref, b_ref = refs[0], refs[1]
        e_ref = refs[2] if extra is not None else None
        o_refs = refs[n_in:n_in + n_out]
        if ksteps == 1:
            finish(product(a_ref, b_ref), e_ref, o_refs)
            return
        acc_ref = refs[-1]
        kk = pl.program_id(2)

        @pl.when(kk == 0)
        def _():
            acc_ref[...] = jnp.zeros_like(acc_ref)

        acc_ref[...] += product(a_ref, b_ref)

        @pl.when(kk == ksteps - 1)
        def _():
            finish(acc_ref[...], e_ref, o_refs)

    if mode == "nn":
        a_spec = pl.BlockSpec((tm, tk), lambda i, j, k: (i, k))
        b_spec = (pl.BlockSpec((None, tk, tn), lambda i, j, k: (j, k, 0)) if b_shards
                  else pl.BlockSpec((tk, tn), lambda i, j, k: (k, j)))
    elif mode == "nt":
        a_spec = pl.BlockSpec((tm, tk), lambda i, j, k: (i, k))
        b_spec = (pl.BlockSpec((k_group, tn, b_cols), lambda i, j, k: (k, j, 0)) if b_shards
                  else pl.BlockSpec((tn, tk), lambda i, j, k: (j, k)))
    else:
        a_spec = pl.BlockSpec((tk, tm), lambda i, j, k: (k, i))
        b_spec = pl.BlockSpec((tk, tn), lambda i, j, k: (k, j))
    o_spec = pl.BlockSpec((tm, tn), lambda i, j, k: (i, j))
    res_spec = pl.BlockSpec((None, tm, tn), lambda i, j, k: (j, i, 0)) if out_shards else o_spec
    res_shape = (n // tn, m, tn) if out_shards else (m, n)
    in_specs = [a_spec, b_spec] + ([o_spec] if extra is not None else []) + [ANY_SPEC] * len(after)
    args = (a, b) + ((extra,) if extra is not None else ()) + tuple(after)
    res = pl.pallas_call(
        body, name=name, grid=(m // tm, n // tn, ksteps),
        in_specs=in_specs, out_specs=[res_spec] * n_out,
        out_shape=[jax.ShapeDtypeStruct(res_shape, dt) for dt in out_dtypes],
        scratch_shapes=[pltpu.VMEM((tm, tn), F32)] if ksteps > 1 else [],
        compiler_params=_params(("parallel", "parallel", "arbitrary")),
    )(*args)
    return res if n_out > 1 else res[0]


GATE_COL = 4 * GDN_WIDTH
RELAYOUT_ROWS = 256


def _cat_of_win(j):
    if j < GATE_COL:
        return j
    if j < GATE_COL + 2 * N_HEADS:
        return MAIN_WIDTH + (j - GATE_COL)
    return j - 2 * N_HEADS


def _win_of_cat(c):
    if c < GATE_COL:
        return c
    if c < MAIN_WIDTH:
        return c + 2 * N_HEADS
    if c < MAIN_WIDTH + 2 * N_HEADS:
        return GATE_COL + (c - MAIN_WIDTH)
    return None


def _runs(first, count, mapping):
    runs, i = [], 0
    while i < count:
        start, n = mapping(first + i), 1
        while i + n < count and mapping(first + i + n) == start + n:
            n += 1
        runs.append((start, n))
        i += n
    return runs


def weights_to_cat(g_in):
    n_dev, rows, shard = g_in.shape

    def body(x_ref, o_ref):
        for b in range(CAT_WIDTH // LANES):
            live = sum(_win_of_cat(LANES * b + i) is not None for i in range(LANES))
            parts = []
            for start, n in _runs(LANES * b, live, _win_of_cat):
                while n > 0:
                    d, o = divmod(start, shard)
                    take = min(n, shard - o)
                    parts.append(x_ref[d, :, o:o + take])
                    start, n = start + take, n - take
            if live < LANES:
                parts.append(jnp.zeros((RELAYOUT_ROWS, LANES - live), g_in.dtype))
            o_ref[:, LANES * b:LANES * (b + 1)] = parts[0] if len(parts) == 1 else jnp.concatenate(parts, axis=1)

    return pl.pallas_call(
        body, name="weights_to_cat", grid=(rows // RELAYOUT_ROWS,),
        in_specs=[pl.BlockSpec((n_dev, RELAYOUT_ROWS, shard), lambda i: (0, i, 0))],
        out_specs=pl.BlockSpec((RELAYOUT_ROWS, CAT_WIDTH), lambda i: (i, 0)),
        out_shape=jax.ShapeDtypeStruct((rows, CAT_WIDTH), g_in.dtype),
        compiler_params=_params(("parallel",)))(g_in)


def cat_to_shards(dw_cat, shard):
    rows = dw_cat.shape[0]

    def body(x_ref, o_ref):
        for d in range(N_DEV):
            for t0 in range(0, shard, LANES):
                width = min(LANES, shard - t0)
                parts = [x_ref[:, c:c + n] for c, n in _runs(d * shard + t0, width, _cat_of_win)]
                o_ref[d, :, t0:t0 + width] = parts[0] if len(parts) == 1 else jnp.concatenate(parts, axis=1)

    return pl.pallas_call(
        body, name="cat_to_shards", grid=(rows // RELAYOUT_ROWS,),
        in_specs=[pl.BlockSpec((RELAYOUT_ROWS, CAT_WIDTH), lambda i: (i, 0))],
        out_specs=pl.BlockSpec((N_DEV, RELAYOUT_ROWS, shard), lambda i: (0, i, 0)),
        out_shape=jax.ShapeDtypeStruct((N_DEV, rows, shard), dw_cat.dtype),
        compiler_params=_params(("parallel",)))(dw_cat)


ROW_BLOCK = 512


def rms_fwd(x, w, name):
    t, d = x.shape

    def body(x_ref, w_ref, n_ref, r_ref):
        h = x_ref[...]
        r = lax.rsqrt(jnp.mean(h * h, axis=-1, keepdims=True) + NORM_EPS)
        n_ref[...] = (h * r * w_ref[...]).astype(BF16)
        r_ref[...] = r

    row = pl.BlockSpec((ROW_BLOCK, d), lambda i: (i, 0))
    return pl.pallas_call(
        body, name=name, grid=(t // ROW_BLOCK,),
        in_specs=[row, pl.BlockSpec((1, d), lambda i: (0, 0))],
        out_specs=[row, pl.BlockSpec((ROW_BLOCK, 1), lambda i: (i, 0))],
        out_shape=[jax.ShapeDtypeStruct((t, d), BF16), jax.ShapeDtypeStruct((t, 1), F32)],
        compiler_params=_params(("parallel",)))(x, w)


FUSED_ROWS = 512


def out_proj_rms(y, w_out, x, w_norm, name):
    t, d = x.shape

    def body(y_ref, w_ref, x_ref, g_ref, h_ref, n_ref, r_ref):
        h = x_ref[...] + _dot(y_ref[...], w_ref[...], NN)
        r = lax.rsqrt(jnp.mean(h * h, axis=-1, keepdims=True) + NORM_EPS)
        h_ref[...] = h
        n_ref[...] = (h * r * g_ref[...]).astype(BF16)
        r_ref[...] = r

    row = pl.BlockSpec((FUSED_ROWS, d), lambda i: (i, 0))
    return pl.pallas_call(
        body, name=name, grid=(t // FUSED_ROWS,),
        in_specs=[pl.BlockSpec((FUSED_ROWS, y.shape[1]), lambda i: (i, 0)), pl.BlockSpec(w_out.shape, lambda i: (0, 0)),
                  row, pl.BlockSpec((1, d), lambda i: (0, 0))],
        out_specs=[row, row, pl.BlockSpec((FUSED_ROWS, 1), lambda i: (i, 0))],
        out_shape=[jax.ShapeDtypeStruct((t, d), F32), jax.ShapeDtypeStruct((t, d), BF16),
                   jax.ShapeDtypeStruct((t, 1), F32)],
        compiler_params=_params(("parallel",)))(y, w_out, x, w_norm)


def ff2_loss(act, w_ff2, h1, w, target, name, tk=2048):
    t, d = h1.shape
    ksteps = act.shape[1] // tk

    def body(a_ref, b_ref, h_ref, w_ref, t_ref, loss_ref, dhb_ref, dw_ref, acc_ref):
        i, kk = pl.program_id(0), pl.program_id(1)

        @pl.when((i == 0) & (kk == 0))
        def _():
            loss_ref[...] = jnp.zeros_like(loss_ref)
            dw_ref[...] = jnp.zeros_like(dw_ref)

        @pl.when(kk == 0)
        def _():
            acc_ref[...] = h_ref[...]

        acc_ref[...] += _dot(a_ref[...], b_ref[...], NN)

        @pl.when(kk == ksteps - 1)
        def _():
            h = acc_ref[...]
            wv = w_ref[...]
            r = lax.rsqrt(jnp.mean(h * h, axis=-1, keepdims=True) + NORM_EPS)
            yn = h * r
            e = yn * wv - t_ref[...]
            loss_ref[...] += 0.5 * jnp.sum(jnp.sum(e * e, axis=-1, keepdims=True), axis=0, keepdims=True) / d
            dy = e / d
            dw_ref[...] += jnp.sum(dy * yn, axis=0, keepdims=True)
            dyn = dy * wv
            dhb_ref[...] = (r * (dyn - yn * jnp.mean(dyn * yn, axis=-1, keepdims=True))).astype(BF16)

    row = pl.BlockSpec((FUSED_ROWS, d), lambda i, k: (i, 0))
    wspec = pl.BlockSpec((1, d), lambda i, k: (0, 0))
    return pl.pallas_call(
        body, name=name, grid=(t // FUSED_ROWS, ksteps),
        in_specs=[pl.BlockSpec((FUSED_ROWS, tk), lambda i, k: (i, k)), pl.BlockSpec((tk, d), lambda i, k: (k, 0)),
                  row, wspec, row],
        out_specs=[pl.BlockSpec((1, 1), lambda i, k: (0, 0)), row, wspec],
        out_shape=[jax.ShapeDtypeStruct((1, 1), F32), jax.ShapeDtypeStruct((t, d), BF16),
                   jax.ShapeDtypeStruct((1, d), F32)],
        scratch_shapes=[pltpu.VMEM((FUSED_ROWS, d), F32)],
        compiler_params=_params(("arbitrary", "arbitrary")))(act, w_ff2, h1, w, target)


def rms_bwd(h, r, w, dn, dres, out_dtype, name):
    t, d = h.shape

    def body(h_ref, r_ref, w_ref, dn_ref, dres_ref, dh_ref, dw_ref):
        @pl.when(pl.program_id(0) == 0)
        def _():
            dw_ref[...] = jnp.zeros_like(dw_ref)

        rv = r_ref[...]
        yn = h_ref[...] * rv
        dnv = dn_ref[...].astype(F32)
        dw_ref[...] += jnp.sum(dnv * yn, axis=0, keepdims=True)
        dyn = dnv * w_ref[...]
        dh = dres_ref[...].astype(F32) + rv * (dyn - yn * jnp.mean(dyn * yn, axis=-1, keepdims=True))
        dh_ref[...] = dh.astype(out_dtype)

    row = pl.BlockSpec((ROW_BLOCK, d), lambda i: (i, 0))
    wspec = pl.BlockSpec((1, d), lambda i: (0, 0))
    rspec = pl.BlockSpec((ROW_BLOCK, 1), lambda i: (i, 0))
    return pl.pallas_call(
        body, name=name, grid=(t // ROW_BLOCK,),
        in_specs=[row, rspec, wspec, row, row], out_specs=[row, wspec],
        out_shape=[jax.ShapeDtypeStruct((t, d), out_dtype), jax.ShapeDtypeStruct((1, d), F32)],
        compiler_params=_params(("arbitrary",)))(h, r, w, dn, dres)


CONV_ROWS = 512
TILE_ROWS = 8


def _iota2(shape, axis):
    return lax.broadcasted_iota(jnp.int32, shape, axis)


def _silu(x):
    return x * jax.nn.sigmoid(x)


def _conv_rows(x_ref, w, first, rows):
    acc = None
    for j in range(4):
        term = x_ref[first - 3 + j:first - 3 + j + rows, :] * w[j:j + 1, :]
        acc = term if acc is None else acc + term
    return acc


def _head_shifts(head):
    rows = _iota2((TILE_ROWS, 1), 0)
    return [jnp.where(rows >= 3 - j, head if j == 3 else pltpu.roll(head, 3 - j, 0), 0.0) for j in range(4)]


def _conv_chunks(t):
    pieces = [(TILE_ROWS, min(CONV_ROWS, t) - TILE_ROWS)]
    pieces += [(r, CONV_ROWS) for r in range(CONV_ROWS, t, CONV_ROWS)]
    return pieces


def conv_fwd(proj, conv_w, name):
    t = proj.shape[0]

    def body(x_ref, w_ref, o_ref):
        w = w_ref[...]
        shifted = _head_shifts(x_ref[0:TILE_ROWS, :])
        o_ref[0:TILE_ROWS, :] = _silu(sum(shifted[j] * w[j:j + 1, :] for j in range(4)))
        for first, rows in _conv_chunks(t):
            o_ref[first:first + rows, :] = _silu(_conv_rows(x_ref, w, first, rows))

    col = pl.BlockSpec((t, LANES), lambda c: (0, c))
    return pl.pallas_call(
        body, name=name, grid=(QKV_WIDTH // LANES,),
        in_specs=[col, pl.BlockSpec((4, LANES), lambda c: (0, c))], out_specs=col,
        out_shape=jax.ShapeDtypeStruct((t, QKV_WIDTH), F32),
        compiler_params=_params(("parallel",)))(proj, conv_w)


def conv_bwd(proj, dout, conv_w, dproj, name):
    t = proj.shape[0]

    def dsilu(pre):
        sg = jax.nn.sigmoid(pre)
        return sg * (1.0 + pre * (1.0 - sg))

    def body(x_ref, d_ref, w_ref, dproj_in, dx_ref, dw_ref, stage):
        del dproj_in
        w = w_ref[...]
        shifted = _head_shifts(x_ref[0:TILE_ROWS, :])
        head_dpre = d_ref[0:TILE_ROWS, :] * dsilu(sum(shifted[j] * w[j:j + 1, :] for j in range(4)))
        stage[0:TILE_ROWS, :] = head_dpre
        for first, rows in _conv_chunks(t):
            stage[first:first + rows, :] = d_ref[first:first + rows, :] * dsilu(_conv_rows(x_ref, w, first, rows))
        stage[t:t + TILE_ROWS, :] = jnp.zeros((TILE_ROWS, LANES), F32)
        for first, rows in [(0, TILE_ROWS)] + _conv_chunks(t):
            dx = None
            for j in range(4):
                term = stage[first + 3 - j:first + 3 - j + rows, :] * w[j:j + 1, :]
                dx = term if dx is None else dx + term
            dx_ref[first:first + rows, :] = dx.astype(BF16)
        dw = [jnp.sum(head_dpre * shifted[j], axis=0, keepdims=True) for j in range(4)]
        for first, rows in _conv_chunks(t):
            dpre = stage[first:first + rows, :]
            for j in range(4):
                dw[j] = dw[j] + jnp.sum(dpre * x_ref[first - 3 + j:first - 3 + j + rows, :], axis=0, keepdims=True)
        dw_ref[...] = jnp.concatenate(dw, axis=0)

    col = pl.BlockSpec((t, LANES), lambda c: (0, c))
    taps = pl.BlockSpec((4, LANES), lambda c: (0, c))
    return pl.pallas_call(
        body, name=name, grid=(QKV_WIDTH // LANES,),
        in_specs=[col, col, taps, ANY_SPEC], out_specs=[col, taps],
        out_shape=[jax.ShapeDtypeStruct(dproj.shape, BF16), jax.ShapeDtypeStruct((4, QKV_WIDTH), F32)],
        scratch_shapes=[pltpu.VMEM((t + TILE_ROWS, LANES), F32)],
        input_output_aliases={3: 0},
        compiler_params=_params(("parallel",)))(proj, dout, conv_w, dproj)


def _softplus(x):
    return jnp.maximum(x, 0.0) + jnp.log(1.0 + jnp.exp(-jnp.abs(x)))


def _head_norm_gate(o, norm_w, gate):
    return o * lax.rsqrt(jnp.mean(o * o, axis=-1, keepdims=True) + NORM_EPS) * norm_w * _silu(gate)


GDN_PREC = ("bf", "bf")
HGRN_PREC = "bf"


def _each(fn, *cols):
    return [fn(*a) for a in zip(*cols)]


@functools.partial(jax.custom_vjp, nondiff_argnums=(2,))
def _known_inverse(low, inv, prec):
    del low, prec
    return inv


def _known_inverse_fwd(low, inv, prec):
    del low
    return inv, inv


def _known_inverse_bwd(prec, inv, ct):
    return -_mm_raw(_mm_raw(inv, ct, TN, prec), inv, NT, prec), jnp.zeros_like(inv)


_known_inverse.defvjp(_known_inverse_fwd, _known_inverse_bwd)


def gdn_stages(hs, qc, kc, vc, zc, ab, a_log_l, dt_l, norm_w, s, prec=GDN_PREC, inv_known=None):
    p_inv, p_mm = prec
    c = CHUNK
    ri, ci = _iota2((c, c), 0), _iota2((c, c), 1)
    incl, strict, eye = ri >= ci, ri > ci, ri == ci
    lane = _iota2((c, LANES), 1)
    last_row = _iota2((c, 1), 0) == c - 1
    rowsum = lambda x: jnp.sum(x, axis=1, keepdims=True)

    def row(col):
        return jnp.sum(jnp.where(eye, col, 0.0), axis=0, keepdims=True)

    q = _each(lambda x: x * lax.rsqrt(rowsum(x * x) + L2_EPS) * (HEAD_DIM ** -0.5), qc)
    k = _each(lambda x: x * lax.rsqrt(rowsum(x * x) + L2_EPS), kc)
    yield
    a_col = [rowsum(jnp.where(lane == h, ab, 0.0)) for h in hs]
    b_col = [rowsum(jnp.where(lane == h + N_HEADS, ab, 0.0)) for h in hs]
    beta = _each(jax.nn.sigmoid, b_col)
    g = _each(lambda a, al, dl: rowsum(jnp.where(lane == 0, -jnp.exp(al) * _softplus(a + dl), 0.0)), a_col, a_log_l, dt_l)
    gcum = _each(lambda x: rowsum(jnp.where(incl, row(x), 0.0)), g)
    g_last = _each(lambda x: jnp.sum(jnp.where(last_row, x, 0.0), axis=0, keepdims=True), gcum)
    decay = _each(lambda x: jnp.exp(jnp.where(incl, x - row(x), -jnp.inf)), gcum)
    yield
    kk = _each(lambda x: mm(x, x, NT, p_mm), k)
    low = _each(lambda b, x, d: jnp.where(strict, b * x * d, 0.0), beta, kk, decay)
    yield
    if inv_known is None:
        power = _each(lambda x: -x, low)
        inv = _each(lambda x: jnp.where(eye, 1.0, 0.0) + x, power)
        for _ in range(5):
            power = _each(lambda x: mm(x, x, NN, p_inv), power)
            yield
            inv = _each(lambda x, p: x + mm(x, p, NN, p_inv), inv, power)
            yield
    else:
        inv = _each(lambda x, known: _known_inverse(x, known, p_inv), low, inv_known)
    exp_g = _each(jnp.exp, gcum)
    yield
    u_v = _each(lambda i, b, x: mm(i, b * x, NN, p_mm), inv, beta, vc)
    w = _each(lambda i, b, e, x: mm(i, b * e * x, NN, p_mm), inv, beta, exp_g, k)
    yield
    attn = _each(lambda x, y, d: mm(x, y, NT, p_mm) * d, q, k, decay)
    yield
    u = _each(lambda x, y, z: x - mm(y, z, NN, p_mm), u_v, w, s)
    yield
    o = _each(lambda x, e, z: mm(x * e, z, NN, p_mm), q, exp_g, s)
    o = _each(lambda x, a, y: x + mm(a, y, NN, p_mm), o, attn, u)
    yield
    k_end = _each(lambda x, gl, gc: x * jnp.exp(gl - gc), k, g_last, gcum)
    s_new = _each(lambda z, gl, x, y: z * jnp.exp(gl) + mm(x, y, TN, p_mm), s, g_last, k_end, u)
    return (_each(lambda x, z: _head_norm_gate(x, norm_w, z), o, zc), s_new), inv


def gdn_chunk(h, qc, kc, vc, zc, ab, a_log_l, dt_l, norm_w, s, prec=GDN_PREC, reuse_inverse=False):
    args = ([h], [qc], [kc], [vc], [zc], ab, [a_log_l], [dt_l], norm_w, [s], prec)
    if reuse_inverse:
        inv = lax.stop_gradient(gdn_chunks(*args)[1])
        (y, s_new), _ = gdn_chunks(*args, inv_known=inv)
    else:
        (y, s_new), _ = gdn_chunks(*args)
    return y[0], s_new[0]


DIAG_ROWS = SUB_CHUNK // 2
SHIFT_PAD = 8
SHIFT_ROWS = SHIFT_PAD + CHUNK + SHIFT_PAD
SHIFT_WAYS = 4


class RolledRows:
    def down(self, x, which):
        del which
        return [x] + [pltpu.roll(x, off, 0) for off in range(1, DIAG_ROWS)]

    def up_sum(self, parts, which):
        del which
        acc = parts[0]
        for off in range(1, DIAG_ROWS):
            acc = acc + pltpu.roll(parts[off], CHUNK - off, 0)
        return acc


class SlotRows:
    def __init__(self, slots):
        self.slots = slots

    def down(self, x, which):
        self.slots[which, 0, SHIFT_PAD:SHIFT_PAD + CHUNK, :] = x
        return [x] + [self.slots[which, 0, SHIFT_PAD - off:SHIFT_PAD + CHUNK - off, :] for off in range(1, DIAG_ROWS)]

    def up_sum(self, parts, which):
        acc = parts[0]
        for off in range(1, DIAG_ROWS):
            way = 1 + off % (SHIFT_WAYS - 1)
            self.slots[which, way, SHIFT_PAD:SHIFT_PAD + CHUNK, :] = parts[off]
            acc = acc + self.slots[which, way, SHIFT_PAD + off:SHIFT_PAD + CHUNK + off, :]
        return acc


def _sub_block_rows():
    return jnp.bitwise_and(_iota2((CHUNK, 1), 0), DIAG_ROWS - 1)


def _diag_forward(rows, q, key, bc, v):
    rmod = _sub_block_rows()
    k_d, b_d, v_d = rows.down(key, 0), rows.down(bc, 1), rows.down(v, 2)
    o = None
    for off in range(DIAG_ROWS):
        e = jnp.exp(jnp.where(rmod >= off, bc - b_d[off], -jnp.inf))
        term = jnp.sum(q * k_d[off] * e, axis=-1, keepdims=True) * v_d[off]
        o = term if o is None else o + term
    return o


def _diag_backward(rows, q, key, bc, v, do):
    rmod = _sub_block_rows()
    k_d, b_d, v_d = rows.down(key, 0), rows.down(bc, 1), rows.down(v, 2)
    dq = db = None
    dk_parts, db_parts, dv_parts = [], [], []
    for off in range(DIAG_ROWS):
        e = jnp.exp(jnp.where(rmod >= off, bc - b_d[off], -jnp.inf))
        qe = q * e
        a = jnp.sum(qe * k_d[off], axis=-1, keepdims=True)
        da = jnp.sum(do * v_d[off], axis=-1, keepdims=True)
        dv_parts.append(a * do)
        dq_term = (da * e) * k_d[off]
        dk_term = da * qe
        s = dk_term * k_d[off]
        dq = dq_term if dq is None else dq + dq_term
        db = s if db is None else db + s
        dk_parts.append(dk_term)
        db_parts.append(s)
    return dq, rows.up_sum(dk_parts, 0), db - rows.up_sum(db_parts, 1), rows.up_sum(dv_parts, 2)


def diag_part(rows, differentiable=True):
    forward = functools.partial(_diag_forward, rows)
    if not differentiable:
        return forward
    part = jax.custom_vjp(forward)
    part.defvjp(lambda q, key, bc, v: (forward(q, key, bc, v), (q, key, bc, v)),
                lambda res, do: _diag_backward(rows, *res, do))
    return part


def hgrn_stages(qb, fb, ib, gb, l0, l1, norm_w, st, prec=HGRN_PREC, diags=None, o_known=None):
    c = CHUNK
    ri, ci = _iota2((4 * c, c), 0), _iota2((4 * c, c), 1)
    rcol = _iota2((c, 1), 0)
    blk0 = jnp.bitwise_and(ri, c - SUB_CHUNK)
    limit = jnp.where(ri < c, ri + 1, jnp.where(ri < 2 * c, blk0, jnp.where(ri < 3 * c, blk0 + SUB_CHUNK,
                                                                          blk0 + DIAG_ROWS)))
    sel = jnp.where(ci < limit, 1.0, 0.0)
    ri, ci = _iota2((c, c), 0), _iota2((c, c), 1)
    lb = _each(lambda a, b: jax.nn.sigmoid(a - b), l0, l1)
    forget = _each(lambda b, f: b + (1.0 - b) * jax.nn.sigmoid(f), lb, fb)
    key = _each(lambda b, f: (1.0 - b) * jax.nn.sigmoid(-f), lb, fb)
    q = _each(_silu, qb)
    v = ib
    logf = _each(jnp.log, forget)
    sums = _each(lambda x: sel_sums(sel, x), logf)
    bc, b_start, b_end, b_half = ([x[i] for x in sums] for i in range(4))
    b_last = _each(lambda x: jnp.sum(x, axis=0, keepdims=True), logf)
    o = _each(lambda x, b, z: mm(x * jnp.exp(b), z, NT, prec), q, bc, st)
    if diags is None:
        diags = [diag_part(RolledRows())] * len(qb)
    yield
    o = list(o)
    for h in range(len(o)):
        o[h] = o[h] + diags[h](q[h], key[h], bc[h], v[h])
        yield
    second = jnp.bitwise_and(rcol, SUB_CHUNK - 1) >= DIAG_ROWS
    same_sub = jnp.bitwise_and(ri, c - SUB_CHUNK) == jnp.bitwise_and(ci, c - SUB_CHUNK)
    q_half = _each(lambda x, b, bh: x * jnp.exp(jnp.where(second, b - bh, -jnp.inf)), q, bc, b_half)
    k_half = _each(lambda x, b, bh: x * jnp.exp(jnp.where(second, -jnp.inf, bh - b)), key, bc, b_half)
    a_half = _each(lambda x, z: jnp.where(same_sub, mm(x, z, NT, prec), 0.0), q_half, k_half)
    o = _each(lambda acc, a, val: acc + mm(a, val, NN, prec), o, a_half, v)
    yield
    q_rel = _each(lambda x, b, bs: x * jnp.exp(b - bs), q, bc, b_start)
    k_rel = _each(lambda x, b, be: x * jnp.exp(be - b), key, bc, b_end)
    for y in range(c // SUB_CHUNK - 1):
        def scaled(x, b, bs):
            end_y = jnp.sum(jnp.where(rcol == SUB_CHUNK * y + SUB_CHUNK - 1, b, 0.0), axis=0, keepdims=True)
            return x * jnp.exp(jnp.where(rcol >= SUB_CHUNK * (y + 1), bs - end_y, -jnp.inf))
        dq = _each(scaled, q_rel, bc, b_start)
        in_y = (ci >= SUB_CHUNK * y) & (ci < SUB_CHUNK * (y + 1))
        a_y = _each(lambda x, z: jnp.where(in_y, mm(x, z, NT, prec), 0.0), dq, k_rel)
        o = _each(lambda acc, a, val: acc + mm(a, val, NN, prec), o, a_y, v)
        yield
    k_state = _each(lambda x, bl, b: x * jnp.exp(bl - b), key, b_last, bc)
    st_new = _each(lambda z, bl, val, x: z * jnp.exp(bl) + mm(val, x, TN, prec), st, b_last, v, k_state)
    if o_known is not None:
        o = _each(_known_value, o, o_known)
    return (_each(lambda x, z: _head_norm_gate(x, norm_w, z), o, gb), st_new), o


def _drain(gen):
    try:
        while True:
            next(gen)
    except StopIteration as done:
        return done.value


def _alternate(gen_a, gen_b):
    out, live = [None, None], [gen_a, gen_b]
    while any(g is not None for g in live):
        for i, g in enumerate(live):
            if g is None:
                continue
            try:
                next(g)
            except StopIteration as done:
                out[i], live[i] = done.value, None
    return out


def gdn_chunks(*args, **kwargs):
    return _drain(gdn_stages(*args, **kwargs))


def hgrn_chunks(*args, **kwargs):
    return _drain(hgrn_stages(*args, **kwargs))


def hgrn_chunk(qb, fb, ib, gb, l0, l1, norm_w, st, prec=HGRN_PREC, reuse_output=False):
    args = ([qb], [fb], [ib], [gb], [l0], [l1], norm_w, [st], prec)
    if reuse_output:
        known = lax.stop_gradient(hgrn_chunks(*args)[1])
        (y, st_new), _ = hgrn_chunks(*args, o_known=known)
    else:
        (y, st_new), _ = hgrn_chunks(*args)
    return y[0], st_new[0]


HEAD_VEC = (N_HEADS, 1, LANES)


class _ChunkSpecs:
    def __init__(self, nc, rev):
        self.nc, self.rev = nc, rev

    def _c(self, c):
        return self.nc - 1 - c if self.rev else c

    def row(self, width, block=0):
        return pl.BlockSpec((CHUNK, width), lambda c: (self._c(c), block))

    def per_head(self, rows):
        return pl.BlockSpec((None, N_HEADS, rows, rows), lambda c: (self._c(c), 0, 0, 0))

    @staticmethod
    def whole(shape):
        return pl.BlockSpec(shape, lambda c: (0,) * len(shape))


def _lanes(j):
    return slice(j * LANES, (j + 1) * LANES)


def mixer_fwd(qkv_c, proj, a_log_l, dt_l, gdn_norm_w, l0, l1, hgrn_norm_w, name):
    t = qkv_c.shape[0]
    hb = N_HEADS
    sp = _ChunkSpecs(t // CHUNK, rev=False)
    hs = list(range(hb))

    def body(q_ref, k_ref, v_ref, z_ref, ab_ref, al_ref, dt_ref, gnw_ref, qb_ref, fb_ref, ib_ref, gb_ref, l0_ref, l1_ref,
             hnw_ref, y_ref, hist_a_ref, inv_ref, hist_b_ref, o_ref, sa_ref, sb_ref, shift_ref):
        @pl.when(pl.program_id(0) == 0)
        def _():
            sa_ref[...] = jnp.zeros_like(sa_ref)
            sb_ref[...] = jnp.zeros_like(sb_ref)
            shift_ref[...] = jnp.zeros_like(shift_ref)

        heads = lambda ref: [ref[:, _lanes(j)] for j in hs]
        s_a, s_b = [sa_ref[h] for h in hs], [sb_ref[h] for h in hs]
        for h in hs:
            hist_a_ref[h] = s_a[h]
            hist_b_ref[h] = s_b[h]
        diags = [diag_part(SlotRows(shift_ref.at[h]), differentiable=False) for h in hs]
        ((y_a, s_a_new), inv), ((y_b, s_b_new), o_pre) = _alternate(
            gdn_stages(hs, heads(q_ref), heads(k_ref), heads(v_ref), heads(z_ref), ab_ref[...],
                       [al_ref[h] for h in hs], [dt_ref[h] for h in hs], gnw_ref[...], s_a),
            hgrn_stages(heads(qb_ref), heads(fb_ref), heads(ib_ref), heads(gb_ref),
                        [l0_ref[h] for h in hs], [l1_ref[h] for h in hs], hnw_ref[...], s_b, diags=diags))
        for h in hs:
            y_ref[:, _lanes(h)] = y_a[h].astype(BF16)
            y_ref[:, _lanes(hb + h)] = y_b[h].astype(BF16)
            o_ref[:, _lanes(h)] = o_pre[h]
            sa_ref[h] = s_a_new[h]
            sb_ref[h] = s_b_new[h]
            inv_ref[h] = inv[h]

    vec, gain, slab = sp.whole(HEAD_VEC), sp.whole((1, LANES)), functools.partial(sp.row, GDN_WIDTH)
    states = jax.ShapeDtypeStruct((sp.nc, N_HEADS, HEAD_DIM, HEAD_DIM), F32)
    return pl.pallas_call(
        body, name=name, grid=(sp.nc,),
        in_specs=[slab(0), slab(1), slab(2), slab(3), sp.row(LANES, AB_BLOCK), vec, vec, gain,
                  slab(4), slab(5), slab(6), slab(7), vec, vec, gain],
        out_specs=[sp.row(2 * GDN_WIDTH), sp.per_head(HEAD_DIM), sp.per_head(CHUNK), sp.per_head(HEAD_DIM), slab(0)],
        out_shape=[jax.ShapeDtypeStruct((t, 2 * GDN_WIDTH), BF16), states,
                   jax.ShapeDtypeStruct((sp.nc, N_HEADS, CHUNK, CHUNK), F32), states,
                   jax.ShapeDtypeStruct((t, GDN_WIDTH), F32)],
        scratch_shapes=[pltpu.VMEM((N_HEADS, HEAD_DIM, HEAD_DIM), F32), pltpu.VMEM((N_HEADS, HEAD_DIM, HEAD_DIM), F32),
                        pltpu.VMEM((hb, 3, SHIFT_WAYS, SHIFT_ROWS, LANES), F32)],
        compiler_params=_params(("arbitrary",)),
    )(qkv_c, qkv_c, qkv_c, proj, proj, a_log_l, dt_l, gdn_norm_w, proj, proj, proj, proj, l0, l1, hgrn_norm_w)


def mixer_bwd(qkv_c, proj, a_log_l, dt_l, gdn_norm_w, l0, l1, hgrn_norm_w, hist_a, inv_hist, hist_b, o_pre, dy, name):
    t = qkv_c.shape[0]
    hb = N_HEADS
    sp = _ChunkSpecs(t // CHUNK, rev=True)
    hs = list(range(hb))

    def body(q_ref, k_ref, v_ref, z_ref, ab_ref, al_ref, dt_ref, gnw_ref, qb_ref, fb_ref, ib_ref, gb_ref, l0_ref, l1_ref,
             hnw_ref, hist_a_ref, inv_ref, hist_b_ref, o_ref, dy_ref,
             dqkv_ref, dproj_ref, dal_ref, ddt_ref, dgnw_ref, dl0_ref, dl1_ref, dhnw_ref, dsa_ref, dsb_ref, shift_ref):
        @pl.when(pl.program_id(0) == 0)
        def _():
            for ref in (dal_ref, ddt_ref, dgnw_ref, dl0_ref, dl1_ref, dhnw_ref, dsa_ref, dsb_ref, shift_ref):
                ref[...] = jnp.zeros_like(ref)

        heads = lambda ref, first=0: [ref[:, _lanes(first + j)] for j in hs]
        diags = [diag_part(SlotRows(shift_ref.at[h])) for h in hs]
        inv_known, o_known = [inv_ref[h] for h in hs], heads(o_ref)

        def both(ga, gb):
            (ra, inv), (rb, o_pre) = _alternate(gdn_stages(hs, *ga, inv_known=inv_known),
                                                hgrn_stages(*gb, diags=diags, o_known=o_known))
            return (ra, rb), (inv, o_pre)

        ga = (heads(q_ref), heads(k_ref), heads(v_ref), heads(z_ref), ab_ref[...], [al_ref[h] for h in hs],
              [dt_ref[h] for h in hs], gnw_ref[...], [hist_a_ref[h] for h in hs])
        gb = (heads(qb_ref), heads(fb_ref), heads(ib_ref), heads(gb_ref), [l0_ref[h] for h in hs],
              [l1_ref[h] for h in hs], hnw_ref[...], [hist_b_ref[h] for h in hs])
        _, vjp, _ = jax.vjp(both, ga, gb, has_aux=True)
        dy_a = [x.astype(F32) for x in heads(dy_ref)]
        dy_b = [x.astype(F32) for x in heads(dy_ref, hb)]
        (dq, dk, dv, dz, dab, dal, ddt, dgnw, ds_a), (dqb, dfb, dib, dgb, dl0, dl1, dhnw, ds_b) = vjp(
            ((dy_a, [dsa_ref[h] for h in hs]), (dy_b, [dsb_ref[h] for h in hs])))
        for h in hs:
            dqkv_ref[:, _lanes(h)] = dq[h]
            dqkv_ref[:, _lanes(hb + h)] = dk[h]
            dqkv_ref[:, _lanes(2 * hb + h)] = dv[h]
            for slab, val in enumerate((dz, dqb, dfb, dib, dgb)):
                dproj_ref[:, _lanes((3 + slab) * hb + h)] = val[h].astype(BF16)
            dal_ref[h] += dal[h]
            ddt_ref[h] += ddt[h]
            dl0_ref[h] += dl0[h]
            dl1_ref[h] += dl1[h]
            dsa_ref[h] = ds_a[h]
            dsb_ref[h] = ds_b[h]
        dproj_ref[:, MAIN_WIDTH:] = dab.astype(BF16)
        dgnw_ref[...] += dgnw
        dhnw_ref[...] += dhnw

    vec, gain, slab = sp.whole(HEAD_VEC), sp.whole((1, LANES)), functools.partial(sp.row, GDN_WIDTH)
    vec_shape, gain_shape = jax.ShapeDtypeStruct(HEAD_VEC, F32), jax.ShapeDtypeStruct((1, LANES), F32)
    return pl.pallas_call(
        body, name=name, grid=(sp.nc,),
        in_specs=[slab(0), slab(1), slab(2), slab(3), sp.row(LANES, AB_BLOCK), vec, vec, gain,
                  slab(4), slab(5), slab(6), slab(7), vec, vec, gain,
                  sp.per_head(HEAD_DIM), sp.per_head(CHUNK), sp.per_head(HEAD_DIM), slab(0), sp.row(2 * GDN_WIDTH)],
        out_specs=[sp.row(QKV_WIDTH), sp.row(CAT_WIDTH), vec, vec, gain, vec, vec, gain],
        out_shape=[jax.ShapeDtypeStruct((t, QKV_WIDTH), F32), jax.ShapeDtypeStruct((t, CAT_WIDTH), BF16),
                   vec_shape, vec_shape, gain_shape, vec_shape, vec_shape, gain_shape],
        scratch_shapes=[pltpu.VMEM((N_HEADS, HEAD_DIM, HEAD_DIM), F32), pltpu.VMEM((N_HEADS, HEAD_DIM, HEAD_DIM), F32),
                        pltpu.VMEM((hb, 3, SHIFT_WAYS, SHIFT_ROWS, LANES), F32)],
        compiler_params=_params(("arbitrary",)),
    )(qkv_c, qkv_c, qkv_c, proj, proj, a_log_l, dt_l, gdn_norm_w, proj, proj, proj, proj, l0, l1, hgrn_norm_w,
      hist_a, inv_hist, hist_b, o_pre, dy)


def _adamw(w, g, m, v):
    m = ADAM_B1 * m + (1.0 - ADAM_B1) * g
    v = ADAM_B2 * v + (1.0 - ADAM_B2) * jnp.square(g)
    m_hat = m / (1.0 - ADAM_B1 ** ADAM_STEP)
    v_hat = v / (1.0 - ADAM_B2 ** ADAM_STEP)
    delta = -ADAM_LR * (m_hat / (jnp.sqrt(v_hat) + ADAM_EPS) + ADAM_WD * w)
    return delta, m, v


def adamw_reduce(parts, w, m, v, name, rb=128):
    r, c = w.shape
    rb = min(rb, r)
    n_parts = parts.shape[0]

    def body(p_ref, w_ref, m_ref, v_ref, g_ref, d_ref, mo_ref, vo_ref):
        g = p_ref[0].astype(F32)
        for d in range(1, n_parts):
            g = g + p_ref[d].astype(F32)
        delta, mn, vn = _adamw(w_ref[...], g, m_ref[...], v_ref[...])
        g_ref[...] = g
        d_ref[...] = delta
        mo_ref[...] = mn
        vo_ref[...] = vn

    blk = pl.BlockSpec((rb, c), lambda i: (i, 0))
    return pl.pallas_call(
        body, name=name, grid=(r // rb,),
        in_specs=[pl.BlockSpec((n_parts, rb, c), lambda i: (0, i, 0)), blk, blk, blk],
        out_specs=[blk] * 4, out_shape=[jax.ShapeDtypeStruct((r, c), F32)] * 4,
        compiler_params=_params(("parallel",)))(parts, w, m, v)


def adamw_small(w, g, m, v, name):
    def body(w_ref, g_ref, m_ref, v_ref, d_ref, mo_ref, vo_ref):
        delta, mn, vn = _adamw(w_ref[...], g_ref[...], m_ref[...], v_ref[...])
        d_ref[...] = delta
        mo_ref[...] = mn
        vo_ref[...] = vn

    vmem = pl.BlockSpec(memory_space=pltpu.VMEM)
    return pl.pallas_call(body, name=name, in_specs=[vmem] * 4, out_specs=[vmem] * 3,
                          out_shape=[jax.ShapeDtypeStruct(w.shape, F32)] * 3)(w, g, m, v)


def _pack(arrays):
    flat = jnp.concatenate([a.reshape(-1).astype(F32) for a in arrays])
    rows = -(-flat.shape[0] // (8 * LANES)) * 8
    return jnp.pad(flat, (0, rows * LANES - flat.shape[0])).reshape(rows, LANES)


def _unpack(packed, shapes):
    flat, out, off = packed.reshape(-1), [], 0
    for s in shapes:
        n = 1
        for d in s:
            n *= d
        out.append(flat[off:off + n].reshape(s))
        off += n
    return out


def _relu2_epilogue(acc, _):
    r = jnp.maximum(acc, 0.0)
    return acc, r * r


def _relu2_bwd_epilogue(acc, a1):
    return (acc * (2.0 * jnp.maximum(a1, 0.0)),)


def kernel(x, w_in, conv_w, gdn_a_log, gdn_dt_bias, gdn_norm_w, hgrn_lb_logits, hgrn_norm_w, w_out, norm_mix_w, norm_ffn_w, w_ff1, w_ff2, norm_final_w, loss_target, m_w_in, m_conv_w, m_gdn_a_log, m_gdn_dt_bias, m_gdn_norm_w, m_hgrn_lb_logits, m_hgrn_norm_w, m_w_out, m_norm_mix_w, m_norm_ffn_w, m_w_ff1, m_w_ff2, m_norm_final_w, v_w_in, v_conv_w, v_gdn_a_log, v_gdn_dt_bias, v_gdn_norm_w, v_hgrn_lb_logits, v_hgrn_norm_w, v_w_out, v_norm_mix_w, v_norm_ffn_w, v_w_ff1, v_w_ff2, v_norm_final_w):
    me = _my_flat()
    xs = x[0]
    target = loss_target[0]
    shard_in = w_in.shape[2]
    shard_conv = conv_w.shape[2]

    tok = lambda t: t[0:1, 0:1]
    own = lambda src: lax.dynamic_index_in_dim(src, me, 0, keepdims=False)

    g_in, g_conv = gather_two_level([w_in[0].astype(BF16), conv_w[0]], "gather_w_in")
    h_g1, t_g1 = exchange_start([w_out[0].astype(BF16), w_ff1[0].astype(BF16)], True, "gather_mid_start", after=[g_in],
                                peers=CHIP_PEERS)
    h_g2, t_g2 = exchange_start([w_ff2[0].astype(BF16)], True, "gather_ff2_start", after=[t_g1], peers=CHIP_PEERS)
    w_cat = weights_to_cat(g_in)
    conv_full = jnp.transpose(g_conv, (1, 0, 2)).reshape(4, QKV_WIDTH)

    lane_b = lambda p: jnp.broadcast_to(p.reshape(N_HEADS, 1, 1), HEAD_VEC)
    a_log_l, dt_l = lane_b(gdn_a_log[0]), lane_b(gdn_dt_bias[0])
    l0 = hgrn_lb_logits[0].reshape(HEAD_VEC)
    l1 = hgrn_lb_logits[1].reshape(HEAD_VEC)

    n1, r1 = rms_fwd(xs, norm_mix_w + tok(t_g1) + tok(t_g2), "rms_mix")
    proj = matmul(n1, w_cat, "nn", "in_proj", tn=CAT_WIDTH // 5)
    qkv_c = conv_fwd(proj, conv_full, "conv_fwd")
    y, hist_a, inv_a, hist_b, o_b = mixer_fwd(qkv_c, proj, a_log_l, dt_l, gdn_norm_w, l0, l1, hgrn_norm_w, "mixer_fwd")
    (s_out, s_ff1), (l_out, l_ff1) = exchange_wait(h_g1, "gather_mid_wait", after=[y], copies=len(CHIP_PEERS))
    (s_ff2,), (l_ff2,) = exchange_wait(h_g2, "gather_ff2_wait", after=[y], copies=len(CHIP_PEERS))
    h_fw, _ = forward_start([l_out, l_ff1, l_ff2], "gather_forward_start")
    _, (l_out,) = exchange_wait(_one(h_fw, 0), "forward_out_wait", copies=len(OTHER_CHIPS))
    w_out_full = _own_slot(l_out, s_out).reshape(D_MODEL, D_MODEL)
    h1, n2, r2 = out_proj_rms(y, w_out_full, xs, norm_ffn_w, "out_proj_rms")
    _, (l_ff1,) = exchange_wait(_one(h_fw, 1), "forward_ff1_wait", after=[n2], copies=len(OTHER_CHIPS))
    w_ff1_sh = _own_slot(l_ff1, s_ff1)
    a1, act = matmul(n2, w_ff1_sh, "nn", "ff1", out_dtypes=(F32, BF16), epilogue=_relu2_epilogue, b_shards=True)
    _, (l_ff2,) = exchange_wait(_one(h_fw, 2), "forward_ff2_wait", after=[act], copies=len(OTHER_CHIPS))
    w_ff2_full = _own_slot(l_ff2, s_ff2).reshape(D_FF, D_MODEL)
    loss_sum, dh2_b, d_final = ff2_loss(act, w_ff2_full, h1, norm_final_w.reshape(1, D_MODEL), target, "ff2_loss")

    da1 = matmul(dh2_b, w_ff2_full, "nt", "d_act", out_dtypes=(BF16,), epilogue=_relu2_bwd_epilogue, extra=a1)
    t_all = xs.shape[0]
    dw_ff2 = matmul(act, dh2_b, "tn", "dw_ff2", out_dtypes=(BF16,), tk=t_all)
    p_ff2 = dw_ff2.reshape(N_DEV, D_FF // N_DEV, D_MODEL)
    h_s1, t_s1 = exchange_start([p_ff2], False, "scatter_ff2_start")
    dn2 = matmul(da1, w_ff1_sh, "nt", "d_n2", out_dtypes=(BF16,), after=[t_s1], b_shards=True, k_group=4)
    p_ff1 = matmul(n2, da1, "tn", "dw_ff1", out_dtypes=(BF16,), tn=D_FF // N_DEV, tk=t_all, after=[t_s1], out_shards=True)
    h_s2, t_s2 = exchange_start([p_ff1], False, "scatter_ff1_start")
    dh1_b, d_ffn = rms_bwd(h1, r2, norm_ffn_w + tok(t_s2), dn2, dh2_b, BF16, "rms_ffn_bwd")
    dmix = matmul(dh1_b, w_out_full, "nt", "d_mix", out_dtypes=(BF16,))
    dw_out = matmul(y, dh1_b, "tn", "dw_out", out_dtypes=(BF16,), tk=t_all)
    p_out = dw_out.reshape(N_DEV, D_MODEL // N_DEV, D_MODEL)
    h_s3, t_s3 = exchange_start([p_out], False, "scatter_out_start")
    d_qkv_c, dproj, d_alog_l, d_dt_l, d_gnw, dl0, dl1, d_hnw = mixer_bwd(
        qkv_c, proj, a_log_l, dt_l, gdn_norm_w + tok(t_s3), l0, l1, hgrn_norm_w, hist_a, inv_a, hist_b, o_b, dmix,
        "mixer_bwd")
    dproj, d_conv_full = conv_bwd(proj, d_qkv_c, conv_full, dproj, "conv_bwd")
    dw_cat = matmul(n1, dproj, "tn", "dw_in", out_dtypes=(BF16,), tm=512, tn=CAT_WIDTH // 5, tk=t_all)
    p_in = cat_to_shards(dw_cat, shard_in)
    h_pair, t_s4 = routed_start(p_in, _to_sibling_routes, "scatter_in_pair_start")

    (s_ff2g,), (r_ff2,) = exchange_wait(h_s1, "scatter_ff2_wait", after=[t_s4])
    (s_ff1g,), (r_ff1,) = exchange_wait(h_s2, "scatter_ff1_wait", after=[t_s4])
    (s_outg,), (r_out,) = exchange_wait(h_s3, "scatter_out_wait", after=[t_s4])
    g_w_ff2, d_w_ff2, nm_w_ff2, nv_w_ff2 = adamw_reduce(
        _own_slot(r_ff2, own(s_ff2g)), w_ff2[0], m_w_ff2[0], v_w_ff2[0], "adamw_w_ff2")
    g_w_ff1, d_w_ff1, nm_w_ff1, nv_w_ff1 = adamw_reduce(
        _own_slot(r_ff1, own(s_ff1g)), w_ff1[0], m_w_ff1[0], v_w_ff1[0], "adamw_w_ff1")
    g_w_out, d_w_out, nm_w_out, nv_w_out = adamw_reduce(
        _own_slot(r_out, own(s_outg)), w_out[0], m_w_out[0], v_w_out[0], "adamw_w_out")
    (p_in,), (from_sibling,) = exchange_wait(h_pair, "scatter_in_pair_wait", after=[d_w_ff2, d_w_ff1, d_w_out],
                                             copies=N_CHIPS)
    chip_sums = pair_sum(p_in, from_sibling, "scatter_in_pair_sum")
    h_chips, t_s5 = routed_start(chip_sums, _to_chips_routes, "scatter_in_chips_start")
    dn1 = matmul(dproj, w_cat, "nt", "d_n1", out_dtypes=(BF16,), tk=CAT_WIDTH // 5, after=[t_s5])
    dx, d_mix = rms_bwd(xs, r1, norm_mix_w, dn1, dh1_b, F32, "rms_mix_bwd")
    (chip_sums,), (r_in,) = exchange_wait(h_chips, "scatter_in_chips_wait", after=[dx], copies=len(OTHER_CHIPS))
    my_chip = me // 2
    r_in = lax.dynamic_update_slice(r_in, lax.dynamic_index_in_dim(chip_sums, my_chip, 0, keepdims=True), (my_chip, 0, 0))
    g_w_in, d_w_in, nm_w_in, nv_w_in = adamw_reduce(r_in, w_in[0], m_w_in[0], v_w_in[0], "adamw_w_in")

    d_lb = jnp.stack([dl0.reshape(GDN_WIDTH), dl1.reshape(GDN_WIDTH)])
    small_shapes = [(1, N_HEADS), (1, N_HEADS), (1, HEAD_DIM), (2, GDN_WIDTH), (1, HEAD_DIM), (1, D_MODEL),
                    (1, D_MODEL), (D_MODEL,), (4, QKV_WIDTH)]
    small = _pack([d_alog_l[:, 0, 0], d_dt_l[:, 0, 0], d_gnw, d_lb, d_hnw, d_mix, d_ffn, d_final, d_conv_full])
    red = allreduce_small(small, "allreduce_small")
    g_alog, g_dt, g_gnw, g_lb, g_hnw, g_mix, g_ffn, g_final, g_conv_full = _unpack(red, small_shapes)
    g_conv = lax.dynamic_slice(g_conv_full, (0, me * shard_conv), (4, shard_conv)).reshape(1, 4, shard_conv)
    small_g = [g_alog, g_dt, g_gnw, g_lb, g_hnw, g_mix, g_ffn, g_final, g_conv]
    small_w = [gdn_a_log, gdn_dt_bias, gdn_norm_w, hgrn_lb_logits, hgrn_norm_w, norm_mix_w, norm_ffn_w, norm_final_w, conv_w]
    small_m = [m_gdn_a_log, m_gdn_dt_bias, m_gdn_norm_w, m_hgrn_lb_logits, m_hgrn_norm_w, m_norm_mix_w, m_norm_ffn_w,
               m_norm_final_w, m_conv_w]
    small_v = [v_gdn_a_log, v_gdn_dt_bias, v_gdn_norm_w, v_hgrn_lb_logits, v_hgrn_norm_w, v_norm_mix_w, v_norm_ffn_w,
               v_norm_final_w, v_conv_w]
    shapes = [a.shape for a in small_w]
    d_s, m_s, v_s = adamw_small(_pack(small_w), _pack(small_g), _pack(small_m), _pack(small_v), "adamw_small")
    d_alog, d_dt, d_gn, d_lbl, d_hn, d_nm, d_nf, d_nfin, d_cw = _unpack(d_s, shapes)
    m_alog, m_dt, m_gn, m_lbl, m_hn, m_nm, m_nf, m_nfin, m_cw = _unpack(m_s, shapes)
    v_alog, v_dt, v_gn, v_lbl, v_hn, v_nm, v_nf, v_nfin, v_cw = _unpack(v_s, shapes)

    loss = lax.psum(loss_sum[0, 0], ("x", "y", "c"))
    lead = lambda a: a[None]
    grads = [lead(g_w_in), g_conv, g_alog, g_dt, g_gnw, g_lb, g_hnw, lead(g_w_out), g_mix, g_ffn,
             lead(g_w_ff1), lead(g_w_ff2), g_final]
    deltas = [lead(d_w_in), d_cw, d_alog, d_dt, d_gn, d_lbl, d_hn, lead(d_w_out), d_nm, d_nf,
              lead(d_w_ff1), lead(d_w_ff2), d_nfin]
    new_m = [lead(nm_w_in), m_cw, m_alog, m_dt, m_gn, m_lbl, m_hn, lead(nm_w_out), m_nm, m_nf,
             lead(nm_w_ff1), lead(nm_w_ff2), m_nfin]
    new_v = [lead(nv_w_in), v_cw, v_alog, v_dt, v_gn, v_lbl, v_hn, lead(nv_w_out), v_nm, v_nf,
             lead(nv_w_ff1), lead(nv_w_ff2), v_nfin]
    return (loss, dx[None], *grads, *deltas, *new_m, *new_v)
```

```python
import functools

import jax
import jax.numpy as jnp
from jax import lax
from jax.experimental import pallas as pl
from jax.experimental.pallas import tpu as pltpu

F32 = jnp.float32
BF16 = jnp.bfloat16
HI = lax.Precision.HIGHEST

N_DEV = 8
D_MODEL = 2048
CHUNK = 64
SUB_CHUNK = 16
HEAD_DIM = 128
N_HEADS = 8
GDN_WIDTH = N_HEADS * HEAD_DIM
D_FF = 4 * D_MODEL
QKV_WIDTH = 3 * GDN_WIDTH
MAIN_WIDTH = 8 * GDN_WIDTH
CAT_WIDTH = MAIN_WIDTH + 128
AB_BLOCK = MAIN_WIDTH // 128
NORM_EPS = 1e-6
L2_EPS = 1e-6
LANES = 128
VMEM_LIMIT = 56 * 1024 * 1024

ADAM_LR = 0.001
ADAM_B1 = 0.9
ADAM_B2 = 0.999
ADAM_EPS = 1e-08
ADAM_WD = 0.01
ADAM_STEP = 10

MESH = pl.DeviceIdType.MESH


def _params(sem=None):
    return pltpu.CompilerParams(dimension_semantics=sem, vmem_limit_bytes=VMEM_LIMIT)


def _dot(a, b, dims, prec=None):
    return lax.dot_general(a, b, (dims, ((), ())), precision=prec, preferred_element_type=F32)


NN = ((1,), (0,))
NT = ((1,), (1,))
TN = ((0,), (0,))


def _split_bf16(x, pieces):
    out = []
    for _ in range(pieces - 1):
        p = x.astype(BF16)
        out.append(p)
        x = x - p.astype(F32)
    out.append(x.astype(BF16))
    return out


def _mm_raw(a, b, dims, prec):
    if prec == "hi":
        return _dot(a, b, dims, HI)
    if prec == "bf":
        return _dot(a.astype(BF16), b.astype(BF16), dims)
    a_hi, a_lo = _split_bf16(a, 2)
    b_hi, b_lo = _split_bf16(b, 2)
    return _dot(a_hi, b_hi, dims) + (_dot(a_hi, b_lo, dims) + _dot(a_lo, b_hi, dims))


@functools.partial(jax.custom_vjp, nondiff_argnums=(2, 3))
def mm(a, b, dims, prec):
    return _mm_raw(a, b, dims, prec)


def _mm_fwd(a, b, dims, prec):
    return _mm_raw(a, b, dims, prec), (a, b)


def _mm_bwd(dims, prec, res, ct):
    a, b = res
    if dims == NN:
        return _mm_raw(ct, b, NT, prec), _mm_raw(a, ct, TN, prec)
    if dims == NT:
        return _mm_raw(ct, b, NN, prec), _mm_raw(ct, a, TN, prec)
    return _mm_raw(b, ct, NT, prec), _mm_raw(a, ct, NN, prec)


mm.defvjp(_mm_fwd, _mm_bwd)


def _sel_raw(sel, x, dims):
    sel = sel.astype(BF16)
    p0, p1, p2 = _split_bf16(x, 3)
    return _dot(sel, p0, dims) + (_dot(sel, p1, dims) + _dot(sel, p2, dims))


def _sel_parts(sel, x):
    c = x.shape[0]
    full = _sel_raw(sel, x, NN)
    return tuple(full[i * c:(i + 1) * c] for i in range(sel.shape[0] // c))


@jax.custom_vjp
def sel_sums(sel, x):
    return _sel_parts(sel, x)


def _sel_fwd(sel, x):
    return _sel_parts(sel, x), sel


def _sel_bwd(sel, cts):
    return jnp.zeros_like(sel), _sel_raw(sel, jnp.concatenate(cts, axis=0), TN)


sel_sums.defvjp(_sel_fwd, _sel_bwd)


@jax.custom_vjp
def _known_value(computed, known):
    del computed
    return known


_known_value.defvjp(lambda computed, known: (known, None), lambda _, ct: (ct, jnp.zeros_like(ct)))


def _my_flat():
    return 4 * lax.axis_index("x") + 2 * lax.axis_index("y") + lax.axis_index("c")


def _peer(k):
    x, y, c = lax.axis_index("x"), lax.axis_index("y"), lax.axis_index("c")
    kx, ky, kc = (k >> 2) & 1, (k >> 1) & 1, k & 1
    px = (1 - x) if kx else x
    py = (1 - y) if ky else y
    pc = (1 - c) if kc else c
    return (px, py, pc), 4 * px + 2 * py + pc


def gather_two_level(xs, name):
    n = len(xs)

    def body(*refs):
        x_refs, y_refs = refs[:n], refs[n:2 * n]
        send_sems, recv_sems, local_sems = refs[2 * n:]
        x, y, c = lax.axis_index("x"), lax.axis_index("y"), lax.axis_index("c")
        me, sibling = (x, y, c), (x, y, 1 - c)
        chips = [(1 - x, y), (x, 1 - y), (1 - x, 1 - y)]
        flat = lambda p: 4 * p[0] + 2 * p[1] + p[2]

        def copy(a, k, block, to, src=None):
            return pltpu.make_async_remote_copy(
                src_ref=y_refs[a].at[flat(block)] if src is None else src, dst_ref=y_refs[a].at[flat(block)],
                send_sem=send_sems.at[a, k], recv_sem=recv_sems.at[a, k], device_id=to, device_id_type=MESH)

        mine = [pltpu.make_async_copy(x_refs[a], y_refs[a].at[flat(me)], local_sems.at[a]) for a in range(n)]
        for cp in mine:
            cp.start()
        first = [copy(a, 0, me, sibling, src=x_refs[a]) for a in range(n)]
        first += [copy(a, 1 + j, me, (*chip, c), src=x_refs[a]) for j, chip in enumerate(chips) for a in range(n)]
        for cp in first:
            cp.start()
        passed = []
        for j, chip in enumerate(chips):
            for a in range(n):
                copy(a, 1 + j, (*chip, c), me).wait_recv()
                cp = copy(a, 4 + j, (*chip, c), sibling)
                cp.start()
                passed.append(cp)
        for a in range(n):
            copy(a, 0, sibling, me).wait_recv()
        for j, chip in enumerate(chips):
            for a in range(n):
                copy(a, 4 + j, (*chip, 1 - c), me).wait_recv()
        for cp in first + passed:
            cp.wait_send()
        for cp in mine:
            cp.wait()

    any_spec = pl.BlockSpec(memory_space=pl.ANY)
    return pl.pallas_call(
        body, name=name, out_shape=[jax.ShapeDtypeStruct((N_DEV,) + x.shape, x.dtype) for x in xs],
        in_specs=[any_spec] * n, out_specs=[any_spec] * n,
        scratch_shapes=[pltpu.SemaphoreType.DMA((n, N_DEV - 1)), pltpu.SemaphoreType.DMA((n, N_DEV - 1)),
                        pltpu.SemaphoreType.DMA((n,))],
    )(*xs)


HBM_SPEC = pl.BlockSpec(memory_space=pltpu.HBM)
SEM_SPEC = pl.BlockSpec(memory_space=pltpu.SEMAPHORE)
ANY_SPEC = pl.BlockSpec(memory_space=pl.ANY)
DATAFLOW = pltpu.SideEffectType.DATAFLOW_SIDE_EFFECTING


def _in_hbm(x):
    return pltpu.with_memory_space_constraint(x, pltpu.HBM)


ALL_PEERS = tuple(range(1, N_DEV))
CHIP_PEERS = (1, 2, 4, 6)
OTHER_CHIPS = (2, 4, 6)


def exchange_start(xs, gather, name, after=(), peers=ALL_PEERS):
    n, n_after = len(xs), len(after)

    def body(*refs):
        x_refs, land_refs = refs[:n], refs[n:2 * n]
        sems = refs[2 * n + n_after:2 * n + n_after + 2 * n]
        token = refs[-1]
        me = _my_flat()
        for k in peers:
            peer, peer_flat = _peer(k)
            for a in range(n):
                src = x_refs[a] if gather else x_refs[a].at[peer_flat]
                pltpu.make_async_remote_copy(src_ref=src, dst_ref=land_refs[a].at[me], send_sem=sems[a],
                                             recv_sem=sems[n + a], device_id=peer, device_id_type=MESH).start()
        token[...] = jnp.zeros_like(token)

    lands = [_in_hbm(lax.empty(((N_DEV,) + x.shape) if gather else x.shape, x.dtype)) for x in xs]
    hbm_out = [pltpu.HBM(x.shape, x.dtype) for x in xs] + [pltpu.HBM(l.shape, l.dtype) for l in lands]
    res = pl.pallas_call(
        body, name=name,
        out_shape=(*([pltpu.SemaphoreType.DMA(())] * (2 * n)), *hbm_out, jax.ShapeDtypeStruct((8, LANES), F32)),
        in_specs=[HBM_SPEC] * (2 * n) + [ANY_SPEC] * n_after,
        out_specs=(*([SEM_SPEC] * (2 * n)), *([HBM_SPEC] * (2 * n)), pl.BlockSpec(memory_space=pltpu.VMEM)),
        input_output_aliases={i: 2 * n + i for i in range(2 * n)},
        compiler_params=pltpu.CompilerParams(has_side_effects=DATAFLOW),
    )(*[_in_hbm(x) for x in xs], *lands, *after)
    return (list(res[:2 * n]), list(res[2 * n:3 * n]), list(res[3 * n:4 * n])), res[-1]


def forward_start(lands, name, after=()):
    n, n_after = len(lands), len(after)

    def body(*refs):
        land_refs = refs[:n]
        sems = refs[n + n_after:n + n_after + 2 * n]
        token = refs[-1]
        sibling, _ = _peer(1)
        for a in range(n):
            for k in OTHER_CHIPS:
                _, from_flat = _peer(k)
                slot = land_refs[a].at[from_flat]
                pltpu.make_async_remote_copy(src_ref=slot, dst_ref=slot, send_sem=sems[a], recv_sem=sems[n + a],
                                             device_id=sibling, device_id_type=MESH).start()
        token[...] = jnp.zeros_like(token)

    res = pl.pallas_call(
        body, name=name,
        out_shape=(*([pltpu.SemaphoreType.DMA(())] * (2 * n)), *[pltpu.HBM(l.shape, l.dtype) for l in lands],
                   jax.ShapeDtypeStruct((8, LANES), F32)),
        in_specs=[HBM_SPEC] * n + [ANY_SPEC] * n_after,
        out_specs=(*([SEM_SPEC] * (2 * n)), *([HBM_SPEC] * n), pl.BlockSpec(memory_space=pltpu.VMEM)),
        input_output_aliases={i: 2 * n + i for i in range(n)},
        compiler_params=pltpu.CompilerParams(has_side_effects=DATAFLOW),
    )(*lands, *after)
    return (list(res[:2 * n]), [], list(res[2 * n:3 * n])), res[-1]


def exchange_wait(handle, name, after=(), copies=N_DEV - 1):
    sems, xs, lands = handle
    n, n_x, n_after = len(lands), len(xs), len(after)

    def body(*refs):
        land_refs = refs[n_x:n_x + n]
        sem_refs = refs[n_x + n:n_x + 3 * n]
        for a in range(n):
            every = land_refs[a].at[pl.ds(0, copies)]
            cp = pltpu.make_async_remote_copy(src_ref=every, dst_ref=every, send_sem=sem_refs[a],
                                              recv_sem=sem_refs[n + a], device_id=_peer(1)[0], device_id_type=MESH)
            cp.wait_send()
            cp.wait_recv()

    res = pl.pallas_call(
        body, name=name,
        out_shape=[pltpu.HBM(x.shape, x.dtype) for x in xs] + [pltpu.HBM(l.shape, l.dtype) for l in lands],
        in_specs=[HBM_SPEC] * (n_x + n) + [SEM_SPEC] * (2 * n) + [ANY_SPEC] * n_after,
        out_specs=[HBM_SPEC] * (n_x + n),
        input_output_aliases={i: i for i in range(n_x + n)},
        compiler_params=pltpu.CompilerParams(has_side_effects=DATAFLOW),
    )(*xs, *lands, *sems, *after)
    return list(res[:n_x]), list(res[n_x:])


N_CHIPS = N_DEV // 2


def routed_start(x, routes, name, after=()):
    n_after = len(after)

    def body(*refs):
        x_ref, land_ref = refs[0], refs[1]
        send_sem, recv_sem = refs[2 + n_after], refs[3 + n_after]
        token = refs[-1]
        for src, dst, peer in routes():
            pltpu.make_async_remote_copy(src_ref=x_ref.at[src], dst_ref=land_ref.at[dst], send_sem=send_sem,
                                         recv_sem=recv_sem, device_id=peer, device_id_type=MESH).start()
        token[...] = jnp.zeros_like(token)

    land = _in_hbm(lax.empty((N_CHIPS,) + x.shape[1:], x.dtype))
    res = pl.pallas_call(
        body, name=name,
        out_shape=(pltpu.SemaphoreType.DMA(()), pltpu.SemaphoreType.DMA(()), pltpu.HBM(x.shape, x.dtype),
                   pltpu.HBM(land.shape, land.dtype), jax.ShapeDtypeStruct((8, LANES), F32)),
        in_specs=[HBM_SPEC, HBM_SPEC] + [ANY_SPEC] * n_after,
        out_specs=(SEM_SPEC, SEM_SPEC, HBM_SPEC, HBM_SPEC, pl.BlockSpec(memory_space=pltpu.VMEM)),
        input_output_aliases={0: 2, 1: 3},
        compiler_params=pltpu.CompilerParams(has_side_effects=DATAFLOW),
    )(_in_hbm(x), land, *after)
    return ([res[0], res[1]], [res[2]], [res[3]]), res[-1]


def _to_sibling_routes():
    c = lax.axis_index("c")
    sibling, _ = _peer(1)
    return [(2 * chip + 1 - c, chip, sibling) for chip in range(N_CHIPS)]


def _to_chips_routes():
    my_chip = _my_flat() // 2
    routes = []
    for k in OTHER_CHIPS:
        peer, peer_flat = _peer(k)
        routes.append((peer_flat // 2, my_chip, peer))
    return routes


def pair_sum(p, from_sibling, name, rb=256):
    _, r, c = p.shape
    mine = lax.axis_index("c").astype(jnp.int32).reshape(1)

    def body(kind_ref, p_ref, s_ref, o_ref):
        del kind_ref
        o_ref[...] = (p_ref[...].astype(F32) + s_ref[...].astype(F32)).astype(BF16)

    return pl.pallas_call(
        body, name=name,
        grid_spec=pltpu.PrefetchScalarGridSpec(
            num_scalar_prefetch=1, grid=(N_CHIPS, r // rb),
            in_specs=[pl.BlockSpec((None, None, rb, c), lambda chip, i, kind: (chip, kind[0], i, 0)),
                      pl.BlockSpec((None, rb, c), lambda chip, i, kind: (chip, i, 0))],
            out_specs=pl.BlockSpec((None, rb, c), lambda chip, i, kind: (chip, i, 0))),
        out_shape=jax.ShapeDtypeStruct((N_CHIPS, r, c), BF16),
        compiler_params=_params(("parallel", "parallel")))(mine, p.reshape(N_CHIPS, 2, r, c), from_sibling)


def _one(handle, a):
    sems, xs, lands = handle
    n = len(lands)
    return [sems[a], sems[n + a]], xs[a:a + 1], [lands[a]]


def _own_slot(land, block):
    return lax.dynamic_update_slice(land, block[None], (_my_flat(),) + (0,) * block.ndim)


def allreduce_small(x, name):
    rows = x.shape[0]

    def body(x_ref, o_ref, buf, send_sems, recv_sems):
        me = _my_flat()
        buf[me] = x_ref[...]
        sends = []
        for k in range(1, N_DEV):
            peer, _ = _peer(k)
            cp = pltpu.make_async_remote_copy(
                src_ref=x_ref, dst_ref=buf.at[me], send_sem=send_sems.at[k], recv_sem=recv_sems.at[k],
                device_id=peer, device_id_type=MESH)
            cp.start()
            sends.append(cp)
        for k in range(1, N_DEV):
            peer, peer_flat = _peer(k)
            pltpu.make_async_remote_copy(
                src_ref=x_ref, dst_ref=buf.at[peer_flat], send_sem=send_sems.at[k], recv_sem=recv_sems.at[k],
                device_id=peer, device_id_type=MESH).wait_recv()
        for cp in sends:
            cp.wait_send()
        acc = buf[0]
        for d in range(1, N_DEV):
            acc = acc + buf[d]
        o_ref[...] = acc

    vmem = pl.BlockSpec(memory_space=pltpu.VMEM)
    return pl.pallas_call(
        body, name=name, out_shape=jax.ShapeDtypeStruct((rows, LANES), F32),
        in_specs=[vmem], out_specs=vmem,
        scratch_shapes=[pltpu.VMEM((N_DEV, rows, LANES), F32),
                        pltpu.SemaphoreType.DMA((N_DEV,)), pltpu.SemaphoreType.DMA((N_DEV,))],
    )(x)


def matmul(a, b, mode, name, out_dtypes=(F32,), epilogue=None, extra=None, tm=1024, tn=1024, tk=2048, after=(),
           b_shards=False, out_shards=False, k_group=1, k_blocks=None):
    if b_shards:
        n_sh, b_rows, b_cols = b.shape
    if mode == "nn":
        (m, kd), n = a.shape, (n_sh * b_cols if b_shards else b.shape[1])
        if b_shards:
            tn = b_cols
    elif mode == "nt":
        (m, kd), n = a.shape, (b_rows if b_shards else b.shape[0])
        if b_shards:
            tk = k_group * b_cols
    else:
        (kd, m), n = a.shape, b.shape[1]
    tm, tn, tk = min(tm, m), min(tn, n), min(tk, kd)
    assert m % tm == 0 and n % tn == 0 and kd % tk == 0, (name, m, n, kd, tm, tn, tk)
    k0, ksteps = (0, kd // tk) if k_blocks is None else k_blocks
    dims = {"nn": NN, "nt": NT, "tn": TN}[mode]
    n_out = len(out_dtypes)
    n_in = 2 + (extra is not None) + len(after)

    def finish(acc, e_ref, o_refs):
        outs = (acc,) if epilogue is None else epilogue(acc, e_ref[...] if e_ref is not None else None)
        for o_ref, o in zip(o_refs, outs):
            o_ref[...] = o.astype(o_ref.dtype)

    def product(a_ref, b_ref):
        if mode == "nt" and b_shards:
            w = b_cols
            parts = [_dot(a_ref[:, s * w:(s + 1) * w], b_ref[s], dims) for s in range(k_group)]
            return functools.reduce(lambda p, q: p + q, parts)
        return _dot(a_ref[...], b_ref[...], dims)

    def body(*refs):
        a_ref, b_ref = refs[0], refs[1]
        e_ref = refs[2] if extra is not None else None
        o_refs = refs[n_in:n_in + n_out]
        if ksteps == 1:
            finish(product(a_ref, b_ref), e_ref, o_refs)
            return
        acc_ref = refs[-1]
        kk = pl.program_id(2)

        @pl.when(kk == 0)
        def _():
            acc_ref[...] = jnp.zeros_like(acc_ref)

        acc_ref[...] += product(a_ref, b_ref)

        @pl.when(kk == ksteps - 1)
        def _():
            finish(acc_ref[...], e_ref, o_refs)

    if mode == "nn":
        a_spec = pl.BlockSpec((tm, tk), lambda i, j, k: (i, k0 + k))
        b_spec = (pl.BlockSpec((None, tk, tn), lambda i, j, k: (j, k, 0)) if b_shards
                  else pl.BlockSpec((tk, tn), lambda i, j, k: (k0 + k, j)))
    elif mode == "nt":
        a_spec = pl.BlockSpec((tm, tk), lambda i, j, k: (i, k))
        b_spec = (pl.BlockSpec((k_group, tn, b_cols), lambda i, j, k: (k, j, 0)) if b_shards
                  else pl.BlockSpec((tn, tk), lambda i, j, k: (j, k)))
    else:
        a_spec = pl.BlockSpec((tk, tm), lambda i, j, k: (k, i))
        b_spec = pl.BlockSpec((tk, tn), lambda i, j, k: (k, j))
    o_spec = pl.BlockSpec((tm, tn), lambda i, j, k: (i, j))
    res_spec = pl.BlockSpec((None, tm, tn), lambda i, j, k: (j, i, 0)) if out_shards else o_spec
    res_shape = (n // tn, m, tn) if out_shards else (m, n)
    in_specs = [a_spec, b_spec] + ([o_spec] if extra is not None else []) + [ANY_SPEC] * len(after)
    args = (a, b) + ((extra,) if extra is not None else ()) + tuple(after)
    res = pl.pallas_call(
        body, name=name, grid=(m // tm, n // tn, ksteps),
        in_specs=in_specs, out_specs=[res_spec] * n_out,
        out_shape=[jax.ShapeDtypeStruct(res_shape, dt) for dt in out_dtypes],
        scratch_shapes=[pltpu.VMEM((tm, tn), F32)] if ksteps > 1 else [],
        compiler_params=_params(("parallel", "parallel", "arbitrary")),
    )(*args)
    return res if n_out > 1 else res[0]


GATE_COL = 4 * GDN_WIDTH
RELAYOUT_ROWS = 256


def _cat_of_win(j):
    if j < GATE_COL:
        return j
    if j < GATE_COL + 2 * N_HEADS:
        return MAIN_WIDTH + (j - GATE_COL)
    return j - 2 * N_HEADS


def _win_of_cat(c):
    if c < GATE_COL:
        return c
    if c < MAIN_WIDTH:
        return c + 2 * N_HEADS
    if c < MAIN_WIDTH + 2 * N_HEADS:
        return GATE_COL + (c - MAIN_WIDTH)
    return None


def _runs(first, count, mapping):
    runs, i = [], 0
    while i < count:
        start, n = mapping(first + i), 1
        while i + n < count and mapping(first + i + n) == start + n:
            n += 1
        runs.append((start, n))
        i += n
    return runs


def weights_to_cat(g_in, name, total_rows, row0=0, into=None):
    n_dev, rows, shard = g_in.shape
    first = row0 // RELAYOUT_ROWS

    def body(x_ref, *rest):
        o_ref = rest[-1]
        for b in range(CAT_WIDTH // LANES):
            live = sum(_win_of_cat(LANES * b + i) is not None for i in range(LANES))
            parts = []
            for start, n in _runs(LANES * b, live, _win_of_cat):
                while n > 0:
                    d, o = divmod(start, shard)
                    take = min(n, shard - o)
                    parts.append(x_ref[d, :, o:o + take])
                    start, n = start + take, n - take
            if live < LANES:
                parts.append(jnp.zeros((RELAYOUT_ROWS, LANES - live), g_in.dtype))
            o_ref[:, LANES * b:LANES * (b + 1)] = parts[0] if len(parts) == 1 else jnp.concatenate(parts, axis=1)

    return pl.pallas_call(
        body, name=name, grid=(rows // RELAYOUT_ROWS,),
        in_specs=[pl.BlockSpec((n_dev, RELAYOUT_ROWS, shard), lambda i: (0, i, 0))] + ([ANY_SPEC] if into is not None else []),
        out_specs=pl.BlockSpec((RELAYOUT_ROWS, CAT_WIDTH), lambda i: (first + i, 0)),
        out_shape=jax.ShapeDtypeStruct((total_rows, CAT_WIDTH), g_in.dtype),
        input_output_aliases={1: 0} if into is not None else {},
        compiler_params=_params(("parallel",)))(*((g_in,) if into is None else (g_in, into)))


def cat_to_shards(dw_cat, shard):
    rows = dw_cat.shape[0]

    def body(x_ref, o_ref):
        for d in range(N_DEV):
            for t0 in range(0, shard, LANES):
                width = min(LANES, shard - t0)
                parts = [x_ref[:, c:c + n] for c, n in _runs(d * shard + t0, width, _cat_of_win)]
                o_ref[d, :, t0:t0 + width] = parts[0] if len(parts) == 1 else jnp.concatenate(parts, axis=1)

    return pl.pallas_call(
        body, name="cat_to_shards", grid=(rows // RELAYOUT_ROWS,),
        in_specs=[pl.BlockSpec((RELAYOUT_ROWS, CAT_WIDTH), lambda i: (i, 0))],
        out_specs=pl.BlockSpec((N_DEV, RELAYOUT_ROWS, shard), lambda i: (0, i, 0)),
        out_shape=jax.ShapeDtypeStruct((N_DEV, rows, shard), dw_cat.dtype),
        compiler_params=_params(("parallel",)))(dw_cat)


ROW_BLOCK = 512


def rms_fwd(x, w, name):
    t, d = x.shape

    def body(x_ref, w_ref, n_ref, r_ref):
        h = x_ref[...]
        r = lax.rsqrt(jnp.mean(h * h, axis=-1, keepdims=True) + NORM_EPS)
        n_ref[...] = (h * r * w_ref[...]).astype(BF16)
        r_ref[...] = r

    row = pl.BlockSpec((ROW_BLOCK, d), lambda i: (i, 0))
    return pl.pallas_call(
        body, name=name, grid=(t // ROW_BLOCK,),
        in_specs=[row, pl.BlockSpec((1, d), lambda i: (0, 0))],
        out_specs=[row, pl.BlockSpec((ROW_BLOCK, 1), lambda i: (i, 0))],
        out_shape=[jax.ShapeDtypeStruct((t, d), BF16), jax.ShapeDtypeStruct((t, 1), F32)],
        compiler_params=_params(("parallel",)))(x, w)


FUSED_ROWS = 512


def out_proj_rms(y, w_out, x, w_norm, name):
    t, d = x.shape

    def body(y_ref, w_ref, x_ref, g_ref, h_ref, n_ref, r_ref):
        h = x_ref[...] + _dot(y_ref[...], w_ref[...], NN)
        r = lax.rsqrt(jnp.mean(h * h, axis=-1, keepdims=True) + NORM_EPS)
        h_ref[...] = h
        n_ref[...] = (h * r * g_ref[...]).astype(BF16)
        r_ref[...] = r

    row = pl.BlockSpec((FUSED_ROWS, d), lambda i: (i, 0))
    return pl.pallas_call(
        body, name=name, grid=(t // FUSED_ROWS,),
        in_specs=[pl.BlockSpec((FUSED_ROWS, y.shape[1]), lambda i: (i, 0)), pl.BlockSpec(w_out.shape, lambda i: (0, 0)),
                  row, pl.BlockSpec((1, d), lambda i: (0, 0))],
        out_specs=[row, row, pl.BlockSpec((FUSED_ROWS, 1), lambda i: (i, 0))],
        out_shape=[jax.ShapeDtypeStruct((t, d), F32), jax.ShapeDtypeStruct((t, d), BF16),
                   jax.ShapeDtypeStruct((t, 1), F32)],
        compiler_params=_params(("parallel",)))(y, w_out, x, w_norm)


def ff2_loss(act, w_ff2, h1, w, target, name, tk=2048):
    t, d = h1.shape
    ksteps = act.shape[1] // tk

    def body(a_ref, b_ref, h_ref, w_ref, t_ref, loss_ref, dhb_ref, dw_ref, acc_ref):
        i, kk = pl.program_id(0), pl.program_id(1)

        @pl.when((i == 0) & (kk == 0))
        def _():
            loss_ref[...] = jnp.zeros_like(loss_ref)
            dw_ref[...] = jnp.zeros_like(dw_ref)

        @pl.when(kk == 0)
        def _():
            acc_ref[...] = h_ref[...]

        acc_ref[...] += _dot(a_ref[...], b_ref[...], NN)

        @pl.when(kk == ksteps - 1)
        def _():
            h = acc_ref[...]
            wv = w_ref[...]
            r = lax.rsqrt(jnp.mean(h * h, axis=-1, keepdims=True) + NORM_EPS)
            yn = h * r
            e = yn * wv - t_ref[...]
            loss_ref[...] += 0.5 * jnp.sum(jnp.sum(e * e, axis=-1, keepdims=True), axis=0, keepdims=True) / d
            dy = e / d
            dw_ref[...] += jnp.sum(dy * yn, axis=0, keepdims=True)
            dyn = dy * wv
            dhb_ref[...] = (r * (dyn - yn * jnp.mean(dyn * yn, axis=-1, keepdims=True))).astype(BF16)

    row = pl.BlockSpec((FUSED_ROWS, d), lambda i, k: (i, 0))
    wspec = pl.BlockSpec((1, d), lambda i, k: (0, 0))
    return pl.pallas_call(
        body, name=name, grid=(t // FUSED_ROWS, ksteps),
        in_specs=[pl.BlockSpec((FUSED_ROWS, tk), lambda i, k: (i, k)), pl.BlockSpec((tk, d), lambda i, k: (k, 0)),
                  row, wspec, row],
        out_specs=[pl.BlockSpec((1, 1), lambda i, k: (0, 0)), row, wspec],
        out_shape=[jax.ShapeDtypeStruct((1, 1), F32), jax.ShapeDtypeStruct((t, d), BF16),
                   jax.ShapeDtypeStruct((1, d), F32)],
        scratch_shapes=[pltpu.VMEM((FUSED_ROWS, d), F32)],
        compiler_params=_params(("arbitrary", "arbitrary")))(act, w_ff2, h1, w, target)


def rms_bwd(h, r, w, dn, dres, out_dtype, name):
    t, d = h.shape

    def body(h_ref, r_ref, w_ref, dn_ref, dres_ref, dh_ref, dw_ref):
        @pl.when(pl.program_id(0) == 0)
        def _():
            dw_ref[...] = jnp.zeros_like(dw_ref)

        rv = r_ref[...]
        yn = h_ref[...] * rv
        dnv = dn_ref[...].astype(F32)
        dw_ref[...] += jnp.sum(dnv * yn, axis=0, keepdims=True)
        dyn = dnv * w_ref[...]
        dh = dres_ref[...].astype(F32) + rv * (dyn - yn * jnp.mean(dyn * yn, axis=-1, keepdims=True))
        dh_ref[...] = dh.astype(out_dtype)

    row = pl.BlockSpec((ROW_BLOCK, d), lambda i: (i, 0))
    wspec = pl.BlockSpec((1, d), lambda i: (0, 0))
    rspec = pl.BlockSpec((ROW_BLOCK, 1), lambda i: (i, 0))
    return pl.pallas_call(
        body, name=name, grid=(t // ROW_BLOCK,),
        in_specs=[row, rspec, wspec, row, row], out_specs=[row, wspec],
        out_shape=[jax.ShapeDtypeStruct((t, d), out_dtype), jax.ShapeDtypeStruct((1, d), F32)],
        compiler_params=_params(("arbitrary",)))(h, r, w, dn, dres)


CONV_ROWS = 512
TILE_ROWS = 8


def _iota2(shape, axis):
    return lax.broadcasted_iota(jnp.int32, shape, axis)


def _silu(x):
    return x * jax.nn.sigmoid(x)


def _conv_rows(x_ref, w, first, rows):
    acc = None
    for j in range(4):
        term = x_ref[first - 3 + j:first - 3 + j + rows, :] * w[j:j + 1, :]
        acc = term if acc is None else acc + term
    return acc


def _head_shifts(head):
    rows = _iota2((TILE_ROWS, 1), 0)
    return [jnp.where(rows >= 3 - j, head if j == 3 else pltpu.roll(head, 3 - j, 0), 0.0) for j in range(4)]


def _conv_chunks(t):
    pieces = [(TILE_ROWS, min(CONV_ROWS, t) - TILE_ROWS)]
    pieces += [(r, CONV_ROWS) for r in range(CONV_ROWS, t, CONV_ROWS)]
    return pieces


def conv_fwd(proj, conv_w, name):
    t = proj.shape[0]

    def body(x_ref, w_ref, o_ref):
        w = w_ref[...]
        shifted = _head_shifts(x_ref[0:TILE_ROWS, :])
        o_ref[0:TILE_ROWS, :] = _silu(sum(shifted[j] * w[j:j + 1, :] for j in range(4)))
        for first, rows in _conv_chunks(t):
            o_ref[first:first + rows, :] = _silu(_conv_rows(x_ref, w, first, rows))

    col = pl.BlockSpec((t, LANES), lambda c: (0, c))
    return pl.pallas_call(
        body, name=name, grid=(QKV_WIDTH // LANES,),
        in_specs=[col, pl.BlockSpec((4, LANES), lambda c: (0, c))], out_specs=col,
        out_shape=jax.ShapeDtypeStruct((t, QKV_WIDTH), F32),
        compiler_params=_params(("parallel",)))(proj, conv_w)


def conv_bwd(proj, dout, conv_w, dproj, name):
    t = proj.shape[0]

    def dsilu(pre):
        sg = jax.nn.sigmoid(pre)
        return sg * (1.0 + pre * (1.0 - sg))

    def body(x_ref, d_ref, w_ref, dproj_in, dx_ref, dw_ref, stage):
        del dproj_in
        w = w_ref[...]
        shifted = _head_shifts(x_ref[0:TILE_ROWS, :])
        head_dpre = d_ref[0:TILE_ROWS, :] * dsilu(sum(shifted[j] * w[j:j + 1, :] for j in range(4)))
        stage[0:TILE_ROWS, :] = head_dpre
        for first, rows in _conv_chunks(t):
            stage[first:first + rows, :] = d_ref[first:first + rows, :] * dsilu(_conv_rows(x_ref, w, first, rows))
        stage[t:t + TILE_ROWS, :] = jnp.zeros((TILE_ROWS, LANES), F32)
        for first, rows in [(0, TILE_ROWS)] + _conv_chunks(t):
            dx = None
            for j in range(4):
                term = stage[first + 3 - j:first + 3 - j + rows, :] * w[j:j + 1, :]
                dx = term if dx is None else dx + term
            dx_ref[first:first + rows, :] = dx.astype(BF16)
        dw = [jnp.sum(head_dpre * shifted[j], axis=0, keepdims=True) for j in range(4)]
        for first, rows in _conv_chunks(t):
            dpre = stage[first:first + rows, :]
            for j in range(4):
                dw[j] = dw[j] + jnp.sum(dpre * x_ref[first - 3 + j:first - 3 + j + rows, :], axis=0, keepdims=True)
        dw_ref[...] = jnp.concatenate(dw, axis=0)

    col = pl.BlockSpec((t, LANES), lambda c: (0, c))
    taps = pl.BlockSpec((4, LANES), lambda c: (0, c))
    return pl.pallas_call(
        body, name=name, grid=(QKV_WIDTH // LANES,),
        in_specs=[col, col, taps, ANY_SPEC], out_specs=[col, taps],
        out_shape=[jax.ShapeDtypeStruct(dproj.shape, BF16), jax.ShapeDtypeStruct((4, QKV_WIDTH), F32)],
        scratch_shapes=[pltpu.VMEM((t + TILE_ROWS, LANES), F32)],
        input_output_aliases={3: 0},
        compiler_params=_params(("parallel",)))(proj, dout, conv_w, dproj)


def _softplus(x):
    return jnp.maximum(x, 0.0) + jnp.log(1.0 + jnp.exp(-jnp.abs(x)))


def _head_norm_gate(o, norm_w, gate):
    return o * lax.rsqrt(jnp.mean(o * o, axis=-1, keepdims=True) + NORM_EPS) * norm_w * _silu(gate)


GDN_PREC = ("bf", "bf")
HGRN_PREC = "bf"


def _each(fn, *cols):
    return [fn(*a) for a in zip(*cols)]


@functools.partial(jax.custom_vjp, nondiff_argnums=(2,))
def _known_inverse(low, inv, prec):
    del low, prec
    return inv


def _known_inverse_fwd(low, inv, prec):
    del low
    return inv, inv


def _known_inverse_bwd(prec, inv, ct):
    return -_mm_raw(_mm_raw(inv, ct, TN, prec), inv, NT, prec), jnp.zeros_like(inv)


_known_inverse.defvjp(_known_inverse_fwd, _known_inverse_bwd)


def gdn_stages(hs, qc, kc, vc, zc, ab, a_log_l, dt_l, norm_w, s, prec=GDN_PREC, inv_known=None):
    p_inv, p_mm = prec
    c = CHUNK
    ri, ci = _iota2((c, c), 0), _iota2((c, c), 1)
    incl, strict, eye = ri >= ci, ri > ci, ri == ci
    lane = _iota2((c, LANES), 1)
    last_row = _iota2((c, 1), 0) == c - 1
    rowsum = lambda x: jnp.sum(x, axis=1, keepdims=True)

    def row(col):
        return jnp.sum(jnp.where(eye, col, 0.0), axis=0, keepdims=True)

    q = _each(lambda x: x * lax.rsqrt(rowsum(x * x) + L2_EPS) * (HEAD_DIM ** -0.5), qc)
    k = _each(lambda x: x * lax.rsqrt(rowsum(x * x) + L2_EPS), kc)
    yield
    a_col = [rowsum(jnp.where(lane == h, ab, 0.0)) for h in hs]
    b_col = [rowsum(jnp.where(lane == h + N_HEADS, ab, 0.0)) for h in hs]
    beta = _each(jax.nn.sigmoid, b_col)
    g = _each(lambda a, al, dl: rowsum(jnp.where(lane == 0, -jnp.exp(al) * _softplus(a + dl), 0.0)), a_col, a_log_l, dt_l)
    gcum = _each(lambda x: rowsum(jnp.where(incl, row(x), 0.0)), g)
    g_last = _each(lambda x: jnp.sum(jnp.where(last_row, x, 0.0), axis=0, keepdims=True), gcum)
    decay = _each(lambda x: jnp.exp(jnp.where(incl, x - row(x), -jnp.inf)), gcum)
    yield
    kk = _each(lambda x: mm(x, x, NT, p_mm), k)
    low = _each(lambda b, x, d: jnp.where(strict, b * x * d, 0.0), beta, kk, decay)
    yield
    if inv_known is None:
        power = _each(lambda x: -x, low)
        inv = _each(lambda x: jnp.where(eye, 1.0, 0.0) + x, power)
        for _ in range(5):
            power = _each(lambda x: mm(x, x, NN, p_inv), power)
            yield
            inv = _each(lambda x, p: x + mm(x, p, NN, p_inv), inv, power)
            yield
    else:
        inv = _each(lambda x, known: _known_inverse(x, known, p_inv), low, inv_known)
    exp_g = _each(jnp.exp, gcum)
    yield
    u_v = _each(lambda i, b, x: mm(i, b * x, NN, p_mm), inv, beta, vc)
    w = _each(lambda i, b, e, x: mm(i, b * e * x, NN, p_mm), inv, beta, exp_g, k)
    yield
    attn = _each(lambda x, y, d: mm(x, y, NT, p_mm) * d, q, k, decay)
    yield
    u = _each(lambda x, y, z: x - mm(y, z, NN, p_mm), u_v, w, s)
    yield
    o = _each(lambda x, e, z: mm(x * e, z, NN, p_mm), q, exp_g, s)
    o = _each(lambda x, a, y: x + mm(a, y, NN, p_mm), o, attn, u)
    yield
    k_end = _each(lambda x, gl, gc: x * jnp.exp(gl - gc), k, g_last, gcum)
    s_new = _each(lambda z, gl, x, y: z * jnp.exp(gl) + mm(x, y, TN, p_mm), s, g_last, k_end, u)
    return (_each(lambda x, z: _head_norm_gate(x, norm_w, z), o, zc), s_new), inv


def gdn_chunk(h, qc, kc, vc, zc, ab, a_log_l, dt_l, norm_w, s, prec=GDN_PREC, reuse_inverse=False):
    args = ([h], [qc], [kc], [vc], [zc], ab, [a_log_l], [dt_l], norm_w, [s], prec)
    if reuse_inverse:
        inv = lax.stop_gradient(gdn_chunks(*args)[1])
        (y, s_new), _ = gdn_chunks(*args, inv_known=inv)
    else:
        (y, s_new), _ = gdn_chunks(*args)
    return y[0], s_new[0]


DIAG_ROWS = SUB_CHUNK // 2
SHIFT_PAD = 8
SHIFT_ROWS = SHIFT_PAD + CHUNK + SHIFT_PAD
SHIFT_WAYS = 4


class RolledRows:
    def down(self, x, which):
        del which
        return [x] + [pltpu.roll(x, off, 0) for off in range(1, DIAG_ROWS)]

    def up_sum(self, parts, which):
        del which
        acc = parts[0]
        for off in range(1, DIAG_ROWS):
            acc = acc + pltpu.roll(parts[off], CHUNK - off, 0)
        return acc


class SlotRows:
    def __init__(self, slots):
        self.slots = slots

    def down(self, x, which):
        self.slots[which, 0, SHIFT_PAD:SHIFT_PAD + CHUNK, :] = x
        return [x] + [self.slots[which, 0, SHIFT_PAD - off:SHIFT_PAD + CHUNK - off, :] for off in range(1, DIAG_ROWS)]

    def up_sum(self, parts, which):
        acc = parts[0]
        for off in range(1, DIAG_ROWS):
            way = 1 + off % (SHIFT_WAYS - 1)
            self.slots[which, way, SHIFT_PAD:SHIFT_PAD + CHUNK, :] = parts[off]
            acc = acc + self.slots[which, way, SHIFT_PAD + off:SHIFT_PAD + CHUNK + off, :]
        return acc


def _sub_block_rows():
    return jnp.bitwise_and(_iota2((CHUNK, 1), 0), DIAG_ROWS - 1)


def _diag_forward(rows, q, key, bc, v):
    rmod = _sub_block_rows()
    k_d, b_d, v_d = rows.down(key, 0), rows.down(bc, 1), rows.down(v, 2)
    o = None
    for off in range(DIAG_ROWS):
        e = jnp.exp(jnp.where(rmod >= off, bc - b_d[off], -jnp.inf))
        term = jnp.sum(q * k_d[off] * e, axis=-1, keepdims=True) * v_d[off]
        o = term if o is None else o + term
    return o


def _diag_backward(rows, q, key, bc, v, do):
    rmod = _sub_block_rows()
    k_d, b_d, v_d = rows.down(key, 0), rows.down(bc, 1), rows.down(v, 2)
    dq = db = None
    dk_parts, db_parts, dv_parts = [], [], []
    for off in range(DIAG_ROWS):
        e = jnp.exp(jnp.where(rmod >= off, bc - b_d[off], -jnp.inf))
        qe = q * e
        a = jnp.sum(qe * k_d[off], axis=-1, keepdims=True)
        da = jnp.sum(do * v_d[off], axis=-1, keepdims=True)
        dv_parts.append(a * do)
        dq_term = (da * e) * k_d[off]
        dk_term = da * qe
        s = dk_term * k_d[off]
        dq = dq_term if dq is None else dq + dq_term
        db = s if db is None else db + s
        dk_parts.append(dk_term)
        db_parts.append(s)
    return dq, rows.up_sum(dk_parts, 0), db - rows.up_sum(db_parts, 1), rows.up_sum(dv_parts, 2)


def diag_part(rows, differentiable=True):
    forward = functools.partial(_diag_forward, rows)
    if not differentiable:
        return forward
    part = jax.custom_vjp(forward)
    part.defvjp(lambda q, key, bc, v: (forward(q, key, bc, v), (q, key, bc, v)),
                lambda res, do: _diag_backward(rows, *res, do))
    return part


def hgrn_stages(qb, fb, ib, gb, l0, l1, norm_w, st, prec=HGRN_PREC, diags=None, o_known=None):
    c = CHUNK
    ri, ci = _iota2((4 * c, c), 0), _iota2((4 * c, c), 1)
    rcol = _iota2((c, 1), 0)
    blk0 = jnp.bitwise_and(ri, c - SUB_CHUNK)
    limit = jnp.where(ri < c, ri + 1, jnp.where(ri < 2 * c, blk0, jnp.where(ri < 3 * c, blk0 + SUB_CHUNK,
                                                                          blk0 + DIAG_ROWS)))
    sel = jnp.where(ci < limit, 1.0, 0.0)
    ri, ci = _iota2((c, c), 0), _iota2((c, c), 1)
    lb = _each(lambda a, b: jax.nn.sigmoid(a - b), l0, l1)
    forget = _each(lambda b, f: b + (1.0 - b) * jax.nn.sigmoid(f), lb, fb)
    key = _each(lambda b, f: (1.0 - b) * jax.nn.sigmoid(-f), lb, fb)
    q = _each(_silu, qb)
    v = ib
    logf = _each(jnp.log, forget)
    sums = _each(lambda x: sel_sums(sel, x), logf)
    bc, b_start, b_end, b_half = ([x[i] for x in sums] for i in range(4))
    b_last = _each(lambda x: jnp.sum(x, axis=0, keepdims=True), logf)
    o = _each(lambda x, b, z: mm(x * jnp.exp(b), z, NT, prec), q, bc, st)
    if diags is None:
        diags = [diag_part(RolledRows())] * len(qb)
    yield
    o = list(o)
    for h in range(len(o)):
        o[h] = o[h] + diags[h](q[h], key[h], bc[h], v[h])
        yield
    second = jnp.bitwise_and(rcol, SUB_CHUNK - 1) >= DIAG_ROWS
    same_sub = jnp.bitwise_and(ri, c - SUB_CHUNK) == jnp.bitwise_and(ci, c - SUB_CHUNK)
    q_half = _each(lambda x, b, bh: x * jnp.exp(jnp.where(second, b - bh, -jnp.inf)), q, bc, b_half)
    k_half = _each(lambda x, b, bh: x * jnp.exp(jnp.where(second, -jnp.inf, bh - b)), key, bc, b_half)
    a_half = _each(lambda x, z: jnp.where(same_sub, mm(x, z, NT, prec), 0.0), q_half, k_half)
    o = _each(lambda acc, a, val: acc + mm(a, val, NN, prec), o, a_half, v)
    yield
    q_rel = _each(lambda x, b, bs: x * jnp.exp(b - bs), q, bc, b_start)
    k_rel = _each(lambda x, b, be: x * jnp.exp(be - b), key, bc, b_end)
    for y in range(c // SUB_CHUNK - 1):
        def scaled(x, b, bs):
            end_y = jnp.sum(jnp.where(rcol == SUB_CHUNK * y + SUB_CHUNK - 1, b, 0.0), axis=0, keepdims=True)
            return x * jnp.exp(jnp.where(rcol >= SUB_CHUNK * (y + 1), bs - end_y, -jnp.inf))
        dq = _each(scaled, q_rel, bc, b_start)
        in_y = (ci >= SUB_CHUNK * y) & (ci < SUB_CHUNK * (y + 1))
        a_y = _each(lambda x, z: jnp.where(in_y, mm(x, z, NT, prec), 0.0), dq, k_rel)
        o = _each(lambda acc, a, val: acc + mm(a, val, NN, prec), o, a_y, v)
        yield
    k_state = _each(lambda x, bl, b: x * jnp.exp(bl - b), key, b_last, bc)
    st_new = _each(lambda z, bl, val, x: z * jnp.exp(bl) + mm(val, x, TN, prec), st, b_last, v, k_state)
    if o_known is not None:
        o = _each(_known_value, o, o_known)
    return (_each(lambda x, z: _head_norm_gate(x, norm_w, z), o, gb), st_new), o


def _drain(gen):
    try:
        while True:
            next(gen)
    except StopIteration as done:
        return done.value


def _alternate(gen_a, gen_b):
    out, live = [None, None], [gen_a, gen_b]
    while any(g is not None for g in live):
        for i, g in enumerate(live):
            if g is None:
                continue
            try:
                next(g)
            except StopIteration as done:
                out[i], live[i] = done.value, None
    return out


def gdn_chunks(*args, **kwargs):
    return _drain(gdn_stages(*args, **kwargs))


def hgrn_chunks(*args, **kwargs):
    return _drain(hgrn_stages(*args, **kwargs))


def hgrn_chunk(qb, fb, ib, gb, l0, l1, norm_w, st, prec=HGRN_PREC, reuse_output=False):
    args = ([qb], [fb], [ib], [gb], [l0], [l1], norm_w, [st], prec)
    if reuse_output:
        known = lax.stop_gradient(hgrn_chunks(*args)[1])
        (y, st_new), _ = hgrn_chunks(*args, o_known=known)
    else:
        (y, st_new), _ = hgrn_chunks(*args)
    return y[0], st_new[0]


HEAD_VEC = (N_HEADS, 1, LANES)


class _ChunkSpecs:
    def __init__(self, nc, rev):
        self.nc, self.rev = nc, rev

    def _c(self, c):
        return self.nc - 1 - c if self.rev else c

    def row(self, width, block=0):
        return pl.BlockSpec((CHUNK, width), lambda c: (self._c(c), block))

    def per_head(self, rows):
        return pl.BlockSpec((None, N_HEADS, rows, rows), lambda c: (self._c(c), 0, 0, 0))

    @staticmethod
    def whole(shape):
        return pl.BlockSpec(shape, lambda c: (0,) * len(shape))


def _lanes(j):
    return slice(j * LANES, (j + 1) * LANES)


def mixer_fwd(qkv_c, proj, a_log_l, dt_l, gdn_norm_w, l0, l1, hgrn_norm_w, name):
    t = qkv_c.shape[0]
    hb = N_HEADS
    sp = _ChunkSpecs(t // CHUNK, rev=False)
    hs = list(range(hb))

    def body(q_ref, k_ref, v_ref, z_ref, ab_ref, al_ref, dt_ref, gnw_ref, qb_ref, fb_ref, ib_ref, gb_ref, l0_ref, l1_ref,
             hnw_ref, y_ref, hist_a_ref, inv_ref, hist_b_ref, o_ref, sa_ref, sb_ref, shift_ref):
        @pl.when(pl.program_id(0) == 0)
        def _():
            sa_ref[...] = jnp.zeros_like(sa_ref)
            sb_ref[...] = jnp.zeros_like(sb_ref)
            shift_ref[...] = jnp.zeros_like(shift_ref)

        heads = lambda ref: [ref[:, _lanes(j)] for j in hs]
        s_a, s_b = [sa_ref[h] for h in hs], [sb_ref[h] for h in hs]
        for h in hs:
            hist_a_ref[h] = s_a[h]
            hist_b_ref[h] = s_b[h]
        diags = [diag_part(SlotRows(shift_ref.at[h]), differentiable=False) for h in hs]
        ((y_a, s_a_new), inv), ((y_b, s_b_new), o_pre) = _alternate(
            gdn_stages(hs, heads(q_ref), heads(k_ref), heads(v_ref), heads(z_ref), ab_ref[...],
                       [al_ref[h] for h in hs], [dt_ref[h] for h in hs], gnw_ref[...], s_a),
            hgrn_stages(heads(qb_ref), heads(fb_ref), heads(ib_ref), heads(gb_ref),
                        [l0_ref[h] for h in hs], [l1_ref[h] for h in hs], hnw_ref[...], s_b, diags=diags))
        for h in hs:
            y_ref[:, _lanes(h)] = y_a[h].astype(BF16)
            y_ref[:, _lanes(hb + h)] = y_b[h].astype(BF16)
            o_ref[:, _lanes(h)] = o_pre[h]
            sa_ref[h] = s_a_new[h]
            sb_ref[h] = s_b_new[h]
            inv_ref[h] = inv[h]

    vec, gain, slab = sp.whole(HEAD_VEC), sp.whole((1, LANES)), functools.partial(sp.row, GDN_WIDTH)
    states = jax.ShapeDtypeStruct((sp.nc, N_HEADS, HEAD_DIM, HEAD_DIM), F32)
    return pl.pallas_call(
        body, name=name, grid=(sp.nc,),
        in_specs=[slab(0), slab(1), slab(2), slab(3), sp.row(LANES, AB_BLOCK), vec, vec, gain,
                  slab(4), slab(5), slab(6), slab(7), vec, vec, gain],
        out_specs=[sp.row(2 * GDN_WIDTH), sp.per_head(HEAD_DIM), sp.per_head(CHUNK), sp.per_head(HEAD_DIM), slab(0)],
        out_shape=[jax.ShapeDtypeStruct((t, 2 * GDN_WIDTH), BF16), states,
                   jax.ShapeDtypeStruct((sp.nc, N_HEADS, CHUNK, CHUNK), F32), states,
                   jax.ShapeDtypeStruct((t, GDN_WIDTH), F32)],
        scratch_shapes=[pltpu.VMEM((N_HEADS, HEAD_DIM, HEAD_DIM), F32), pltpu.VMEM((N_HEADS, HEAD_DIM, HEAD_DIM), F32),
                        pltpu.VMEM((hb, 3, SHIFT_WAYS, SHIFT_ROWS, LANES), F32)],
        compiler_params=_params(("arbitrary",)),
    )(qkv_c, qkv_c, qkv_c, proj, proj, a_log_l, dt_l, gdn_norm_w, proj, proj, proj, proj, l0, l1, hgrn_norm_w)


def mixer_bwd(qkv_c, proj, a_log_l, dt_l, gdn_norm_w, l0, l1, hgrn_norm_w, hist_a, inv_hist, hist_b, o_pre, dy, name):
    t = qkv_c.shape[0]
    hb = N_HEADS
    sp = _ChunkSpecs(t // CHUNK, rev=True)
    hs = list(range(hb))

    def body(q_ref, k_ref, v_ref, z_ref, ab_ref, al_ref, dt_ref, gnw_ref, qb_ref, fb_ref, ib_ref, gb_ref, l0_ref, l1_ref,
             hnw_ref, hist_a_ref, inv_ref, hist_b_ref, o_ref, dy_ref,
             dqkv_ref, dproj_ref, dal_ref, ddt_ref, dgnw_ref, dl0_ref, dl1_ref, dhnw_ref, dsa_ref, dsb_ref, shift_ref):
        @pl.when(pl.program_id(0) == 0)
        def _():
            for ref in (dal_ref, ddt_ref, dgnw_ref, dl0_ref, dl1_ref, dhnw_ref, dsa_ref, dsb_ref, shift_ref):
                ref[...] = jnp.zeros_like(ref)

        heads = lambda ref, first=0: [ref[:, _lanes(first + j)] for j in hs]
        diags = [diag_part(SlotRows(shift_ref.at[h])) for h in hs]
        inv_known, o_known = [inv_ref[h] for h in hs], heads(o_ref)

        def both(ga, gb):
            (ra, inv), (rb, o_pre) = _alternate(gdn_stages(hs, *ga, inv_known=inv_known),
                                                hgrn_stages(*gb, diags=diags, o_known=o_known))
            return (ra, rb), (inv, o_pre)

        ga = (heads(q_ref), heads(k_ref), heads(v_ref), heads(z_ref), ab_ref[...], [al_ref[h] for h in hs],
              [dt_ref[h] for h in hs], gnw_ref[...], [hist_a_ref[h] for h in hs])
        gb = (heads(qb_ref), heads(fb_ref), heads(ib_ref), heads(gb_ref), [l0_ref[h] for h in hs],
              [l1_ref[h] for h in hs], hnw_ref[...], [hist_b_ref[h] for h in hs])
        _, vjp, _ = jax.vjp(both, ga, gb, has_aux=True)
        dy_a = [x.astype(F32) for x in heads(dy_ref)]
        dy_b = [x.astype(F32) for x in heads(dy_ref, hb)]
        (dq, dk, dv, dz, dab, dal, ddt, dgnw, ds_a), (dqb, dfb, dib, dgb, dl0, dl1, dhnw, ds_b) = vjp(
            ((dy_a, [dsa_ref[h] for h in hs]), (dy_b, [dsb_ref[h] for h in hs])))
        for h in hs:
            dqkv_ref[:, _lanes(h)] = dq[h]
            dqkv_ref[:, _lanes(hb + h)] = dk[h]
            dqkv_ref[:, _lanes(2 * hb + h)] = dv[h]
            for slab, val in enumerate((dz, dqb, dfb, dib, dgb)):
                dproj_ref[:, _lanes((3 + slab) * hb + h)] = val[h].astype(BF16)
            dal_ref[h] += dal[h]
            ddt_ref[h] += ddt[h]
            dl0_ref[h] += dl0[h]
            dl1_ref[h] += dl1[h]
            dsa_ref[h] = ds_a[h]
            dsb_ref[h] = ds_b[h]
        dproj_ref[:, MAIN_WIDTH:] = dab.astype(BF16)
        dgnw_ref[...] += dgnw
        dhnw_ref[...] += dhnw

    vec, gain, slab = sp.whole(HEAD_VEC), sp.whole((1, LANES)), functools.partial(sp.row, GDN_WIDTH)
    vec_shape, gain_shape = jax.ShapeDtypeStruct(HEAD_VEC, F32), jax.ShapeDtypeStruct((1, LANES), F32)
    return pl.pallas_call(
        body, name=name, grid=(sp.nc,),
        in_specs=[slab(0), slab(1), slab(2), slab(3), sp.row(LANES, AB_BLOCK), vec, vec, gain,
                  slab(4), slab(5), slab(6), slab(7), vec, vec, gain,
                  sp.per_head(HEAD_DIM), sp.per_head(CHUNK), sp.per_head(HEAD_DIM), slab(0), sp.row(2 * GDN_WIDTH)],
        out_specs=[sp.row(QKV_WIDTH), sp.row(CAT_WIDTH), vec, vec, gain, vec, vec, gain],
        out_shape=[jax.ShapeDtypeStruct((t, QKV_WIDTH), F32), jax.ShapeDtypeStruct((t, CAT_WIDTH), BF16),
                   vec_shape, vec_shape, gain_shape, vec_shape, vec_shape, gain_shape],
        scratch_shapes=[pltpu.VMEM((N_HEADS, HEAD_DIM, HEAD_DIM), F32), pltpu.VMEM((N_HEADS, HEAD_DIM, HEAD_DIM), F32),
                        pltpu.VMEM((hb, 3, SHIFT_WAYS, SHIFT_ROWS, LANES), F32)],
        compiler_params=_params(("arbitrary",)),
    )(qkv_c, qkv_c, qkv_c, proj, proj, a_log_l, dt_l, gdn_norm_w, proj, proj, proj, proj, l0, l1, hgrn_norm_w,
      hist_a, inv_hist, hist_b, o_pre, dy)


def _adamw(w, g, m, v):
    m = ADAM_B1 * m + (1.0 - ADAM_B1) * g
    v = ADAM_B2 * v + (1.0 - ADAM_B2) * jnp.square(g)
    m_hat = m / (1.0 - ADAM_B1 ** ADAM_STEP)
    v_hat = v / (1.0 - ADAM_B2 ** ADAM_STEP)
    delta = -ADAM_LR * (m_hat / (jnp.sqrt(v_hat) + ADAM_EPS) + ADAM_WD * w)
    return delta, m, v


def adamw_reduce(parts, w, m, v, name, rb=128):
    r, c = w.shape
    rb = min(rb, r)
    n_parts = parts.shape[0]

    def body(p_ref, w_ref, m_ref, v_ref, g_ref, d_ref, mo_ref, vo_ref):
        g = p_ref[0].astype(F32)
        for d in range(1, n_parts):
            g = g + p_ref[d].astype(F32)
        delta, mn, vn = _adamw(w_ref[...], g, m_ref[...], v_ref[...])
        g_ref[...] = g
        d_ref[...] = delta
        mo_ref[...] = mn
        vo_ref[...] = vn

    blk = pl.BlockSpec((rb, c), lambda i: (i, 0))
    return pl.pallas_call(
        body, name=name, grid=(r // rb,),
        in_specs=[pl.BlockSpec((n_parts, rb, c), lambda i: (0, i, 0)), blk, blk, blk],
        out_specs=[blk] * 4, out_shape=[jax.ShapeDtypeStruct((r, c), F32)] * 4,
        compiler_params=_params(("parallel",)))(parts, w, m, v)


def adamw_small(w, g, m, v, name):
    def body(w_ref, g_ref, m_ref, v_ref, d_ref, mo_ref, vo_ref):
        delta, mn, vn = _adamw(w_ref[...], g_ref[...], m_ref[...], v_ref[...])
        d_ref[...] = delta
        mo_ref[...] = mn
        vo_ref[...] = vn

    vmem = pl.BlockSpec(memory_space=pltpu.VMEM)
    return pl.pallas_call(body, name=name, in_specs=[vmem] * 4, out_specs=[vmem] * 3,
                          out_shape=[jax.ShapeDtypeStruct(w.shape, F32)] * 3)(w, g, m, v)


def _pack(arrays):
    flat = jnp.concatenate([a.reshape(-1).astype(F32) for a in arrays])
    rows = -(-flat.shape[0] // (8 * LANES)) * 8
    return jnp.pad(flat, (0, rows * LANES - flat.shape[0])).reshape(rows, LANES)


def _unpack(packed, shapes):
    flat, out, off = packed.reshape(-1), [], 0
    for s in shapes:
        n = 1
        for d in s:
            n *= d
        out.append(flat[off:off + n].reshape(s))
        off += n
    return out


def _relu2_epilogue(acc, _):
    r = jnp.maximum(acc, 0.0)
    return acc, r * r


def _relu2_bwd_epilogue(acc, a1):
    return (acc * (2.0 * jnp.maximum(a1, 0.0)),)


def kernel(x, w_in, conv_w, gdn_a_log, gdn_dt_bias, gdn_norm_w, hgrn_lb_logits, hgrn_norm_w, w_out, norm_mix_w, norm_ffn_w, w_ff1, w_ff2, norm_final_w, loss_target, m_w_in, m_conv_w, m_gdn_a_log, m_gdn_dt_bias, m_gdn_norm_w, m_hgrn_lb_logits, m_hgrn_norm_w, m_w_out, m_norm_mix_w, m_norm_ffn_w, m_w_ff1, m_w_ff2, m_norm_final_w, v_w_in, v_conv_w, v_gdn_a_log, v_gdn_dt_bias, v_gdn_norm_w, v_hgrn_lb_logits, v_hgrn_norm_w, v_w_out, v_norm_mix_w, v_norm_ffn_w, v_w_ff1, v_w_ff2, v_norm_final_w):
    me = _my_flat()
    xs = x[0]
    target = loss_target[0]
    shard_in = w_in.shape[2]
    shard_conv = conv_w.shape[2]

    tok = lambda t: t[0:1, 0:1]
    own = lambda src: lax.dynamic_index_in_dim(src, me, 0, keepdims=False)

    half = D_MODEL // 2
    w_in_b = w_in[0].astype(BF16)
    g_in_a, g_conv = gather_two_level([w_in_b[:half], conv_w[0]], "gather_w_in")
    h_g0, t_g0 = exchange_start([w_in_b[half:]], True, "gather_w_in_low_start", after=[g_in_a], peers=CHIP_PEERS)
    h_g1, t_g1 = exchange_start([w_out[0].astype(BF16), w_ff1[0].astype(BF16)], True, "gather_mid_start", after=[t_g0],
                                peers=CHIP_PEERS)
    h_g2, t_g2 = exchange_start([w_ff2[0].astype(BF16)], True, "gather_ff2_start", after=[t_g1], peers=CHIP_PEERS)
    w_cat = weights_to_cat(g_in_a, "weights_to_cat", D_MODEL)
    conv_full = jnp.transpose(g_conv, (1, 0, 2)).reshape(4, QKV_WIDTH)

    lane_b = lambda p: jnp.broadcast_to(p.reshape(N_HEADS, 1, 1), HEAD_VEC)
    a_log_l, dt_l = lane_b(gdn_a_log[0]), lane_b(gdn_dt_bias[0])
    l0 = hgrn_lb_logits[0].reshape(HEAD_VEC)
    l1 = hgrn_lb_logits[1].reshape(HEAD_VEC)

    n1, r1 = rms_fwd(xs, norm_mix_w + tok(t_g1) + tok(t_g2), "rms_mix")
    proj = matmul(n1, w_cat, "nn", "in_proj_high", tn=CAT_WIDTH // 5, tk=half, k_blocks=(0, 1))
    (s_low,), (l_low,) = exchange_wait(h_g0, "gather_w_in_low_wait", after=[proj], copies=len(CHIP_PEERS))
    h_f0, _ = forward_start([l_low], "gather_w_in_low_forward")
    _, (l_low,) = exchange_wait(_one(h_f0, 0), "forward_w_in_low_wait", copies=len(OTHER_CHIPS))
    w_cat = weights_to_cat(_own_slot(l_low, s_low), "weights_to_cat_low", D_MODEL, row0=half, into=w_cat)
    proj = matmul(n1, w_cat, "nn", "in_proj_low", tn=CAT_WIDTH // 5, tk=half, k_blocks=(1, 1), extra=proj,
                  epilogue=lambda acc, high: (acc + high,))
    qkv_c = conv_fwd(proj, conv_full, "conv_fwd")
    y, hist_a, inv_a, hist_b, o_b = mixer_fwd(qkv_c, proj, a_log_l, dt_l, gdn_norm_w, l0, l1, hgrn_norm_w, "mixer_fwd")
    (s_out, s_ff1), (l_out, l_ff1) = exchange_wait(h_g1, "gather_mid_wait", after=[y], copies=len(CHIP_PEERS))
    (s_ff2,), (l_ff2,) = exchange_wait(h_g2, "gather_ff2_wait", after=[y], copies=len(CHIP_PEERS))
    h_fw, _ = forward_start([l_out, l_ff1, l_ff2], "gather_forward_start")
    _, (l_out,) = exchange_wait(_one(h_fw, 0), "forward_out_wait", copies=len(OTHER_CHIPS))
    w_out_full = _own_slot(l_out, s_out).reshape(D_MODEL, D_MODEL)
    h1, n2, r2 = out_proj_rms(y, w_out_full, xs, norm_ffn_w, "out_proj_rms")
    _, (l_ff1,) = exchange_wait(_one(h_fw, 1), "forward_ff1_wait", after=[n2], copies=len(OTHER_CHIPS))
    w_ff1_sh = _own_slot(l_ff1, s_ff1)
    a1, act = matmul(n2, w_ff1_sh, "nn", "ff1", out_dtypes=(F32, BF16), epilogue=_relu2_epilogue, b_shards=True)
    _, (l_ff2,) = exchange_wait(_one(h_fw, 2), "forward_ff2_wait", after=[act], copies=len(OTHER_CHIPS))
    w_ff2_full = _own_slot(l_ff2, s_ff2).reshape(D_FF, D_MODEL)
    loss_sum, dh2_b, d_final = ff2_loss(act, w_ff2_full, h1, norm_final_w.reshape(1, D_MODEL), target, "ff2_loss")

    da1 = matmul(dh2_b, w_ff2_full, "nt", "d_act", out_dtypes=(BF16,), epilogue=_relu2_bwd_epilogue, extra=a1)
    t_all = xs.shape[0]
    dw_ff2 = matmul(act, dh2_b, "tn", "dw_ff2", out_dtypes=(BF16,), tk=t_all)
    p_ff2 = dw_ff2.reshape(N_DEV, D_FF // N_DEV, D_MODEL)
    h_s1, t_s1 = exchange_start([p_ff2], False, "scatter_ff2_start")
    dn2 = matmul(da1, w_ff1_sh, "nt", "d_n2", out_dtypes=(BF16,), after=[t_s1], b_shards=True, k_group=4)
    p_ff1 = matmul(n2, da1, "tn", "dw_ff1", out_dtypes=(BF16,), tn=D_FF // N_DEV, tk=t_all, after=[t_s1], out_shards=True)
    h_s2, t_s2 = exchange_start([p_ff1], False, "scatter_ff1_start")
    dh1_b, d_ffn = rms_bwd(h1, r2, norm_ffn_w + tok(t_s2), dn2, dh2_b, BF16, "rms_ffn_bwd")
    dmix = matmul(dh1_b, w_out_full, "nt", "d_mix", out_dtypes=(BF16,))
    dw_out = matmul(y, dh1_b, "tn", "dw_out", out_dtypes=(BF16,), tk=t_all)
    p_out = dw_out.reshape(N_DEV, D_MODEL // N_DEV, D_MODEL)
    h_s3, t_s3 = exchange_start([p_out], False, "scatter_out_start")
    d_qkv_c, dproj, d_alog_l, d_dt_l, d_gnw, dl0, dl1, d_hnw = mixer_bwd(
        qkv_c, proj, a_log_l, dt_l, gdn_norm_w + tok(t_s3), l0, l1, hgrn_norm_w, hist_a, inv_a, hist_b, o_b, dmix,
        "mixer_bwd")
    dproj, d_conv_full = conv_bwd(proj, d_qkv_c, conv_full, dproj, "conv_bwd")
    dw_cat = matmul(n1, dproj, "tn", "dw_in", out_dtypes=(BF16,), tm=512, tn=CAT_WIDTH // 5, tk=t_all)
    p_in = cat_to_shards(dw_cat, shard_in)
    h_pair, t_s4 = routed_start(p_in, _to_sibling_routes, "scatter_in_pair_start")

    (s_ff2g,), (r_ff2,) = exchange_wait(h_s1, "scatter_ff2_wait", after=[t_s4])
    (s_ff1g,), (r_ff1,) = exchange_wait(h_s2, "scatter_ff1_wait", after=[t_s4])
    (s_outg,), (r_out,) = exchange_wait(h_s3, "scatter_out_wait", after=[t_s4])
    g_w_ff2, d_w_ff2, nm_w_ff2, nv_w_ff2 = adamw_reduce(
        _own_slot(r_ff2, own(s_ff2g)), w_ff2[0], m_w_ff2[0], v_w_ff2[0], "adamw_w_ff2")
    g_w_ff1, d_w_ff1, nm_w_ff1, nv_w_ff1 = adamw_reduce(
        _own_slot(r_ff1, own(s_ff1g)), w_ff1[0], m_w_ff1[0], v_w_ff1[0], "adamw_w_ff1")
    g_w_out, d_w_out, nm_w_out, nv_w_out = adamw_reduce(
        _own_slot(r_out, own(s_outg)), w_out[0], m_w_out[0], v_w_out[0], "adamw_w_out")
    (p_in,), (from_sibling,) = exchange_wait(h_pair, "scatter_in_pair_wait", after=[d_w_ff2, d_w_ff1, d_w_out],
                                             copies=N_CHIPS)
    chip_sums = pair_sum(p_in, from_sibling, "scatter_in_pair_sum")
    h_chips, t_s5 = routed_start(chip_sums, _to_chips_routes, "scatter_in_chips_start")
    dn1 = matmul(dproj, w_cat, "nt", "d_n1", out_dtypes=(BF16,), tk=CAT_WIDTH // 5, after=[t_s5])
    dx, d_mix = rms_bwd(xs, r1, norm_mix_w, dn1, dh1_b, F32, "rms_mix_bwd")
    (chip_sums,), (r_in,) = exchange_wait(h_chips, "scatter_in_chips_wait", after=[dx], copies=len(OTHER_CHIPS))
    my_chip = me // 2
    r_in = lax.dynamic_update_slice(r_in, lax.dynamic_index_in_dim(chip_sums, my_chip, 0, keepdims=True), (my_chip, 0, 0))
    g_w_in, d_w_in, nm_w_in, nv_w_in = adamw_reduce(r_in, w_in[0], m_w_in[0], v_w_in[0], "adamw_w_in")

    d_lb = jnp.stack([dl0.reshape(GDN_WIDTH), dl1.reshape(GDN_WIDTH)])
    small_shapes = [(1, N_HEADS), (1, N_HEADS), (1, HEAD_DIM), (2, GDN_WIDTH), (1, HEAD_DIM), (1, D_MODEL),
                    (1, D_MODEL), (D_MODEL,), (4, QKV_WIDTH)]
    small = _pack([d_alog_l[:, 0, 0], d_dt_l[:, 0, 0], d_gnw, d_lb, d_hnw, d_mix, d_ffn, d_final, d_conv_full])
    red = allreduce_small(small, "allreduce_small")
    g_alog, g_dt, g_gnw, g_lb, g_hnw, g_mix, g_ffn, g_final, g_conv_full = _unpack(red, small_shapes)
    g_conv = lax.dynamic_slice(g_conv_full, (0, me * shard_conv), (4, shard_conv)).reshape(1, 4, shard_conv)
    small_g = [g_alog, g_dt, g_gnw, g_lb, g_hnw, g_mix, g_ffn, g_final, g_conv]
    small_w = [gdn_a_log, gdn_dt_bias, gdn_norm_w, hgrn_lb_logits, hgrn_norm_w, norm_mix_w, norm_ffn_w, norm_final_w, conv_w]
    small_m = [m_gdn_a_log, m_gdn_dt_bias, m_gdn_norm_w, m_hgrn_lb_logits, m_hgrn_norm_w, m_norm_mix_w, m_norm_ffn_w,
               m_norm_final_w, m_conv_w]
    small_v = [v_gdn_a_log, v_gdn_dt_bias, v_gdn_norm_w, v_hgrn_lb_logits, v_hgrn_norm_w, v_norm_mix_w, v_norm_ffn_w,
               v_norm_final_w, v_conv_w]
    shapes = [a.shape for a in small_w]
    d_s, m_s, v_s = adamw_small(_pack(small_w), _pack(small_g), _pack(small_m), _pack(small_v), "adamw_small")
    d_alog, d_dt, d_gn, d_lbl, d_hn, d_nm, d_nf, d_nfin, d_cw = _unpack(d_s, shapes)
    m_alog, m_dt, m_gn, m_lbl, m_hn, m_nm, m_nf, m_nfin, m_cw = _unpack(m_s, shapes)
    v_alog, v_dt, v_gn, v_lbl, v_hn, v_nm, v_nf, v_nfin, v_cw = _unpack(v_s, shapes)

    loss = lax.psum(loss_sum[0, 0], ("x", "y", "c"))
    lead = lambda a: a[None]
    grads = [lead(g_w_in), g_conv, g_alog, g_dt, g_gnw, g_lb, g_hnw, lead(g_w_out), g_mix, g_ffn,
             lead(g_w_ff1), lead(g_w_ff2), g_final]
    deltas = [lead(d_w_in), d_cw, d_alog, d_dt, d_gn, d_lbl, d_hn, lead(d_w_out), d_nm, d_nf,
              lead(d_w_ff1), lead(d_w_ff2), d_nfin]
    new_m = [lead(nm_w_in), m_cw, m_alog, m_dt, m_gn, m_lbl, m_hn, lead(nm_w_out), m_nm, m_nf,
             lead(nm_w_ff1), lead(nm_w_ff2), m_nfin]
    new_v = [lead(nv_w_in), v_cw, v_alog, v_dt, v_gn, v_lbl, v_hn, lead(nv_w_out), v_nm, v_nf,
             lead(nv_w_ff1), lead(nv_w_ff2), v_nfin]
    return (loss, dx[None], *grads, *deltas, *new_m, *new_v)
```

```python
import functools

import jax
import jax.numpy as jnp
from jax import lax
from jax.experimental import pallas as pl
from jax.experimental.pallas import tpu as pltpu

F32 = jnp.float32
BF16 = jnp.bfloat16
HI = lax.Precision.HIGHEST

N_DEV = 8
D_MODEL = 2048
CHUNK = 64
SUB_CHUNK = 16
HEAD_DIM = 128
N_HEADS = 8
GDN_WIDTH = N_HEADS * HEAD_DIM
D_FF = 4 * D_MODEL
QKV_WIDTH = 3 * GDN_WIDTH
MAIN_WIDTH = 8 * GDN_WIDTH
CAT_WIDTH = MAIN_WIDTH + 128
AB_BLOCK = MAIN_WIDTH // 128
NORM_EPS = 1e-6
L2_EPS = 1e-6
LANES = 128
VMEM_LIMIT = 56 * 1024 * 1024

ADAM_LR = 0.001
ADAM_B1 = 0.9
ADAM_B2 = 0.999
ADAM_EPS = 1e-08
ADAM_WD = 0.01
ADAM_STEP = 10

MESH = pl.DeviceIdType.MESH


def _params(sem=None):
    return pltpu.CompilerParams(dimension_semantics=sem, vmem_limit_bytes=VMEM_LIMIT)


def _dot(a, b, dims, prec=None):
    return lax.dot_general(a, b, (dims, ((), ())), precision=prec, preferred_element_type=F32)


NN = ((1,), (0,))
NT = ((1,), (1,))
TN = ((0,), (0,))


def _split_bf16(x, pieces):
    out = []
    for _ in range(pieces - 1):
        p = x.astype(BF16)
        out.append(p)
        x = x - p.astype(F32)
    out.append(x.astype(BF16))
    return out


def _mm_raw(a, b, dims, prec):
    if prec == "hi":
        return _dot(a, b, dims, HI)
    if prec == "bf":
        return _dot(a.astype(BF16), b.astype(BF16), dims)
    a_hi, a_lo = _split_bf16(a, 2)
    b_hi, b_lo = _split_bf16(b, 2)
    return _dot(a_hi, b_hi, dims) + (_dot(a_hi, b_lo, dims) + _dot(a_lo, b_hi, dims))


@functools.partial(jax.custom_vjp, nondiff_argnums=(2, 3))
def mm(a, b, dims, prec):
    return _mm_raw(a, b, dims, prec)


def _mm_fwd(a, b, dims, prec):
    return _mm_raw(a, b, dims, prec), (a, b)


def _mm_bwd(dims, prec, res, ct):
    a, b = res
    if dims == NN:
        return _mm_raw(ct, b, NT, prec), _mm_raw(a, ct, TN, prec)
    if dims == NT:
        return _mm_raw(ct, b, NN, prec), _mm_raw(ct, a, TN, prec)
    return _mm_raw(b, ct, NT, prec), _mm_raw(a, ct, NN, prec)


mm.defvjp(_mm_fwd, _mm_bwd)


def _sel_raw(sel, x, dims):
    sel = sel.astype(BF16)
    p0, p1, p2 = _split_bf16(x, 3)
    return _dot(sel, p0, dims) + (_dot(sel, p1, dims) + _dot(sel, p2, dims))


def _sel_parts(sel, x):
    c = x.shape[0]
    full = _sel_raw(sel, x, NN)
    return tuple(full[i * c:(i + 1) * c] for i in range(sel.shape[0] // c))


@jax.custom_vjp
def sel_sums(sel, x):
    return _sel_parts(sel, x)


def _sel_fwd(sel, x):
    return _sel_parts(sel, x), sel


def _sel_bwd(sel, cts):
    return jnp.zeros_like(sel), _sel_raw(sel, jnp.concatenate(cts, axis=0), TN)


sel_sums.defvjp(_sel_fwd, _sel_bwd)


@jax.custom_vjp
def _known_value(computed, known):
    del computed
    return known


_known_value.defvjp(lambda computed, known: (known, None), lambda _, ct: (ct, jnp.zeros_like(ct)))


def _my_flat():
    return 4 * lax.axis_index("x") + 2 * lax.axis_index("y") + lax.axis_index("c")


def _peer(k):
    x, y, c = lax.axis_index("x"), lax.axis_index("y"), lax.axis_index("c")
    kx, ky, kc = (k >> 2) & 1, (k >> 1) & 1, k & 1
    px = (1 - x) if kx else x
    py = (1 - y) if ky else y
    pc = (1 - c) if kc else c
    return (px, py, pc), 4 * px + 2 * py + pc


def gather_two_level(xs, name):
    n = len(xs)

    def body(*refs):
        x_refs, y_refs = refs[:n], refs[n:2 * n]
        send_sems, recv_sems, local_sems = refs[2 * n:]
        x, y, c = lax.axis_index("x"), lax.axis_index("y"), lax.axis_index("c")
        me, sibling = (x, y, c), (x, y, 1 - c)
        chips = [(1 - x, y), (x, 1 - y), (1 - x, 1 - y)]
        flat = lambda p: 4 * p[0] + 2 * p[1] + p[2]

        def copy(a, k, block, to, src=None):
            return pltpu.make_async_remote_copy(
                src_ref=y_refs[a].at[flat(block)] if src is None else src, dst_ref=y_refs[a].at[flat(block)],
                send_sem=send_sems.at[a, k], recv_sem=recv_sems.at[a, k], device_id=to, device_id_type=MESH)

        mine = [pltpu.make_async_copy(x_refs[a], y_refs[a].at[flat(me)], local_sems.at[a]) for a in range(n)]
        for cp in mine:
            cp.start()
        first = [copy(a, 0, me, sibling, src=x_refs[a]) for a in range(n)]
        first += [copy(a, 1 + j, me, (*chip, c), src=x_refs[a]) for j, chip in enumerate(chips) for a in range(n)]
        for cp in first:
            cp.start()
        passed = []
        for j, chip in enumerate(chips):
            for a in range(n):
                copy(a, 1 + j, (*chip, c), me).wait_recv()
                cp = copy(a, 4 + j, (*chip, c), sibling)
                cp.start()
                passed.append(cp)
        for a in range(n):
            copy(a, 0, sibling, me).wait_recv()
        for j, chip in enumerate(chips):
            for a in range(n):
                copy(a, 4 + j, (*chip, 1 - c), me).wait_recv()
        for cp in first + passed:
            cp.wait_send()
        for cp in mine:
            cp.wait()

    any_spec = pl.BlockSpec(memory_space=pl.ANY)
    return pl.pallas_call(
        body, name=name, out_shape=[jax.ShapeDtypeStruct((N_DEV,) + x.shape, x.dtype) for x in xs],
        in_specs=[any_spec] * n, out_specs=[any_spec] * n,
        scratch_shapes=[pltpu.SemaphoreType.DMA((n, N_DEV - 1)), pltpu.SemaphoreType.DMA((n, N_DEV - 1)),
                        pltpu.SemaphoreType.DMA((n,))],
    )(*xs)


HBM_SPEC = pl.BlockSpec(memory_space=pltpu.HBM)
SEM_SPEC = pl.BlockSpec(memory_space=pltpu.SEMAPHORE)
ANY_SPEC = pl.BlockSpec(memory_space=pl.ANY)
DATAFLOW = pltpu.SideEffectType.DATAFLOW_SIDE_EFFECTING


def _in_hbm(x):
    return pltpu.with_memory_space_constraint(x, pltpu.HBM)


ALL_PEERS = tuple(range(1, N_DEV))
CHIP_PEERS = (1, 2, 4, 6)
OTHER_CHIPS = (2, 4, 6)


def exchange_start(xs, gather, name, after=(), peers=ALL_PEERS):
    n, n_after = len(xs), len(after)

    def body(*refs):
        x_refs, land_refs = refs[:n], refs[n:2 * n]
        sems = refs[2 * n + n_after:2 * n + n_after + 2 * n]
        token = refs[-1]
        me = _my_flat()
        for k in peers:
            peer, peer_flat = _peer(k)
            for a in range(n):
                src = x_refs[a] if gather else x_refs[a].at[peer_flat]
                pltpu.make_async_remote_copy(src_ref=src, dst_ref=land_refs[a].at[me], send_sem=sems[a],
                                             recv_sem=sems[n + a], device_id=peer, device_id_type=MESH).start()
        token[...] = jnp.zeros_like(token)

    lands = [_in_hbm(lax.empty(((N_DEV,) + x.shape) if gather else x.shape, x.dtype)) for x in xs]
    hbm_out = [pltpu.HBM(x.shape, x.dtype) for x in xs] + [pltpu.HBM(l.shape, l.dtype) for l in lands]
    res = pl.pallas_call(
        body, name=name,
        out_shape=(*([pltpu.SemaphoreType.DMA(())] * (2 * n)), *hbm_out, jax.ShapeDtypeStruct((8, LANES), F32)),
        in_specs=[HBM_SPEC] * (2 * n) + [ANY_SPEC] * n_after,
        out_specs=(*([SEM_SPEC] * (2 * n)), *([HBM_SPEC] * (2 * n)), pl.BlockSpec(memory_space=pltpu.VMEM)),
        input_output_aliases={i: 2 * n + i for i in range(2 * n)},
        compiler_params=pltpu.CompilerParams(has_side_effects=DATAFLOW),
    )(*[_in_hbm(x) for x in xs], *lands, *after)
    return (list(res[:2 * n]), list(res[2 * n:3 * n]), list(res[3 * n:4 * n])), res[-1]


def forward_start(lands, name, after=()):
    n, n_after = len(lands), len(after)

    def body(*refs):
        land_refs = refs[:n]
        sems = refs[n + n_after:n + n_after + 2 * n]
        token = refs[-1]
        sibling, _ = _peer(1)
        for a in range(n):
            for k in OTHER_CHIPS:
                _, from_flat = _peer(k)
                slot = land_refs[a].at[from_flat]
                pltpu.make_async_remote_copy(src_ref=slot, dst_ref=slot, send_sem=sems[a], recv_sem=sems[n + a],
                                             device_id=sibling, device_id_type=MESH).start()
        token[...] = jnp.zeros_like(token)

    res = pl.pallas_call(
        body, name=name,
        out_shape=(*([pltpu.SemaphoreType.DMA(())] * (2 * n)), *[pltpu.HBM(l.shape, l.dtype) for l in lands],
                   jax.ShapeDtypeStruct((8, LANES), F32)),
        in_specs=[HBM_SPEC] * n + [ANY_SPEC] * n_after,
        out_specs=(*([SEM_SPEC] * (2 * n)), *([HBM_SPEC] * n), pl.BlockSpec(memory_space=pltpu.VMEM)),
        input_output_aliases={i: 2 * n + i for i in range(n)},
        compiler_params=pltpu.CompilerParams(has_side_effects=DATAFLOW),
    )(*lands, *after)
    return (list(res[:2 * n]), [], list(res[2 * n:3 * n])), res[-1]


def exchange_wait(handle, name, after=(), copies=N_DEV - 1):
    sems, xs, lands = handle
    n, n_x, n_after = len(lands), len(xs), len(after)

    def body(*refs):
        land_refs = refs[n_x:n_x + n]
        sem_refs = refs[n_x + n:n_x + 3 * n]
        for a in range(n):
            every = land_refs[a].at[pl.ds(0, copies)]
            cp = pltpu.make_async_remote_copy(src_ref=every, dst_ref=every, send_sem=sem_refs[a],
                                              recv_sem=sem_refs[n + a], device_id=_peer(1)[0], device_id_type=MESH)
            cp.wait_send()
            cp.wait_recv()

    res = pl.pallas_call(
        body, name=name,
        out_shape=[pltpu.HBM(x.shape, x.dtype) for x in xs] + [pltpu.HBM(l.shape, l.dtype) for l in lands],
        in_specs=[HBM_SPEC] * (n_x + n) + [SEM_SPEC] * (2 * n) + [ANY_SPEC] * n_after,
        out_specs=[HBM_SPEC] * (n_x + n),
        input_output_aliases={i: i for i in range(n_x + n)},
        compiler_params=pltpu.CompilerParams(has_side_effects=DATAFLOW),
    )(*xs, *lands, *sems, *after)
    return list(res[:n_x]), list(res[n_x:])


N_CHIPS = N_DEV // 2


def routed_start(x, routes, name, after=()):
    n_after = len(after)

    def body(*refs):
        x_ref, land_ref = refs[0], refs[1]
        send_sem, recv_sem = refs[2 + n_after], refs[3 + n_after]
        token = refs[-1]
        for src, dst, peer in routes():
            pltpu.make_async_remote_copy(src_ref=x_ref.at[src], dst_ref=land_ref.at[dst], send_sem=send_sem,
                                         recv_sem=recv_sem, device_id=peer, device_id_type=MESH).start()
        token[...] = jnp.zeros_like(token)

    land = _in_hbm(lax.empty((N_CHIPS,) + x.shape[1:], x.dtype))
    res = pl.pallas_call(
        body, name=name,
        out_shape=(pltpu.SemaphoreType.DMA(()), pltpu.SemaphoreType.DMA(()), pltpu.HBM(x.shape, x.dtype),
                   pltpu.HBM(land.shape, land.dtype), jax.ShapeDtypeStruct((8, LANES), F32)),
        in_specs=[HBM_SPEC, HBM_SPEC] + [ANY_SPEC] * n_after,
        out_specs=(SEM_SPEC, SEM_SPEC, HBM_SPEC, HBM_SPEC, pl.BlockSpec(memory_space=pltpu.VMEM)),
        input_output_aliases={0: 2, 1: 3},
        compiler_params=pltpu.CompilerParams(has_side_effects=DATAFLOW),
    )(_in_hbm(x), land, *after)
    return ([res[0], res[1]], [res[2]], [res[3]]), res[-1]


def _to_sibling_routes():
    c = lax.axis_index("c")
    sibling, _ = _peer(1)
    return [(2 * chip + 1 - c, chip, sibling) for chip in range(N_CHIPS)]


def _to_chips_routes():
    my_chip = _my_flat() // 2
    routes = []
    for k in OTHER_CHIPS:
        peer, peer_flat = _peer(k)
        routes.append((peer_flat // 2, my_chip, peer))
    return routes


def pair_sum(p, from_sibling, name, rb=256):
    _, r, c = p.shape
    mine = lax.axis_index("c").astype(jnp.int32).reshape(1)

    def body(kind_ref, p_ref, s_ref, o_ref):
        del kind_ref
        o_ref[...] = (p_ref[...].astype(F32) + s_ref[...].astype(F32)).astype(BF16)

    return pl.pallas_call(
        body, name=name,
        grid_spec=pltpu.PrefetchScalarGridSpec(
            num_scalar_prefetch=1, grid=(N_CHIPS, r // rb),
            in_specs=[pl.BlockSpec((None, None, rb, c), lambda chip, i, kind: (chip, kind[0], i, 0)),
                      pl.BlockSpec((None, rb, c), lambda chip, i, kind: (chip, i, 0))],
            out_specs=pl.BlockSpec((None, rb, c), lambda chip, i, kind: (chip, i, 0))),
        out_shape=jax.ShapeDtypeStruct((N_CHIPS, r, c), BF16),
        compiler_params=_params(("parallel", "parallel")))(mine, p.reshape(N_CHIPS, 2, r, c), from_sibling)


def _one(handle, a):
    sems, xs, lands = handle
    n = len(lands)
    return [sems[a], sems[n + a]], xs[a:a + 1], [lands[a]]


def _own_slot(land, block):
    return lax.dynamic_update_slice(land, block[None], (_my_flat(),) + (0,) * block.ndim)


def allreduce_small(x, name):
    rows = x.shape[0]

    def body(x_ref, o_ref, buf, send_sems, recv_sems):
        me = _my_flat()
        buf[me] = x_ref[...]
        sends = []
        for k in range(1, N_DEV):
            peer, _ = _peer(k)
            cp = pltpu.make_async_remote_copy(
                src_ref=x_ref, dst_ref=buf.at[me], send_sem=send_sems.at[k], recv_sem=recv_sems.at[k],
                device_id=peer, device_id_type=MESH)
            cp.start()
            sends.append(cp)
        for k in range(1, N_DEV):
            peer, peer_flat = _peer(k)
            pltpu.make_async_remote_copy(
                src_ref=x_ref, dst_ref=buf.at[peer_flat], send_sem=send_sems.at[k], recv_sem=recv_sems.at[k],
                device_id=peer, device_id_type=MESH).wait_recv()
        for cp in sends:
            cp.wait_send()
        acc = buf[0]
        for d in range(1, N_DEV):
            acc = acc + buf[d]
        o_ref[...] = acc

    vmem = pl.BlockSpec(memory_space=pltpu.VMEM)
    return pl.pallas_call(
        body, name=name, out_shape=jax.ShapeDtypeStruct((rows, LANES), F32),
        in_specs=[vmem], out_specs=vmem,
        scratch_shapes=[pltpu.VMEM((N_DEV, rows, LANES), F32),
                        pltpu.SemaphoreType.DMA((N_DEV,)), pltpu.SemaphoreType.DMA((N_DEV,))],
    )(x)


def matmul(a, b, mode, name, out_dtypes=(F32,), epilogue=None, extra=None, tm=1024, tn=1024, tk=2048, after=(),
           b_shards=False, out_shards=False, k_group=1, k_blocks=None):
    if b_shards:
        n_sh, b_rows, b_cols = b.shape
    if mode == "nn":
        (m, kd), n = a.shape, (n_sh * b_cols if b_shards else b.shape[1])
        if b_shards:
            tn = b_cols
    elif mode == "nt":
        (m, kd), n = a.shape, (b_rows if b_shards else b.shape[0])
        if b_shards:
            tk = k_group * b_cols
    else:
        (kd, m), n = a.shape, b.shape[1]
    tm, tn, tk = min(tm, m), min(tn, n), min(tk, kd)
    assert m % tm == 0 and n % tn == 0 and kd % tk == 0, (name, m, n, kd, tm, tn, tk)
    k0, ksteps = (0, kd // tk) if k_blocks is None else k_blocks
    dims = {"nn": NN, "nt": NT, "tn": TN}[mode]
    n_out = len(out_dtypes)
    n_in = 2 + (extra is not None) + len(after)

    def finish(acc, e_ref, o_refs):
        outs = (acc,) if epilogue is None else epilogue(acc, e_ref[...] if e_ref is not None else None)
        for o_ref, o in zip(o_refs, outs):
            o_ref[...] = o.astype(o_ref.dtype)

    def product(a_ref, b_ref):
        if mode == "nt" and b_shards:
            w = b_cols
            parts = [_dot(a_ref[:, s * w:(s + 1) * w], b_ref[s], dims) for s in range(k_group)]
            return functools.reduce(lambda p, q: p + q, parts)
        return _dot(a_ref[...], b_ref[...], dims)

    def body(*refs):
        a_ref, b_ref = refs[0], refs[1]
        e_ref = refs[2] if extra is not None else None
        o_refs = refs[n_in:n_in + n_out]
        if ksteps == 1:
            finish(product(a_ref, b_ref), e_ref, o_refs)
            return
        acc_ref = refs[-1]
        kk = pl.program_id(2)

        @pl.when(kk == 0)
        def _():
            acc_ref[...] = jnp.zeros_like(acc_ref)

        acc_ref[...] += product(a_ref, b_ref)

        @pl.when(kk == ksteps - 1)
        def _():
            finish(acc_ref[...], e_ref, o_refs)

    if mode == "nn":
        a_spec = pl.BlockSpec((tm, tk), lambda i, j, k: (i, k0 + k))
        b_spec = (pl.BlockSpec((None, tk, tn), lambda i, j, k: (j, k, 0)) if b_shards
                  else pl.BlockSpec((tk, tn), lambda i, j, k: (k0 + k, j)))
    elif mode == "nt":
        a_spec = pl.BlockSpec((tm, tk), lambda i, j, k: (i, k))
        b_spec = (pl.BlockSpec((k_group, tn, b_cols), lambda i, j, k: (k, j, 0)) if b_shards
                  else pl.BlockSpec((tn, tk), lambda i, j, k: (j, k)))
    else:
        a_spec = pl.BlockSpec((tk, tm), lambda i, j, k: (k, i))
        b_spec = pl.BlockSpec((tk, tn), lambda i, j, k: (k, j))
    o_spec = pl.BlockSpec((tm, tn), lambda i, j, k: (i, j))
    res_spec = pl.BlockSpec((None, tm, tn), lambda i, j, k: (j, i, 0)) if out_shards else o_spec
    res_shape = (n // tn, m, tn) if out_shards else (m, n)
    in_specs = [a_spec, b_spec] + ([o_spec] if extra is not None else []) + [ANY_SPEC] * len(after)
    args = (a, b) + ((extra,) if extra is not None else ()) + tuple(after)
    res = pl.pallas_call(
        body, name=name, grid=(m // tm, n // tn, ksteps),
        in_specs=in_specs, out_specs=[res_spec] * n_out,
        out_shape=[jax.ShapeDtypeStruct(res_shape, dt) for dt in out_dtypes],
        scratch_shapes=[pltpu.VMEM((tm, tn), F32)] if ksteps > 1 else [],
        compiler_params=_params(("parallel", "parallel", "arbitrary")),
    )(*args)
    return res if n_out > 1 else res[0]


GATE_COL = 4 * GDN_WIDTH
RELAYOUT_ROWS = 256


def _cat_of_win(j):
    if j < GATE_COL:
        return j
    if j < GATE_COL + 2 * N_HEADS:
        return MAIN_WIDTH + (j - GATE_COL)
    return j - 2 * N_HEADS


def _win_of_cat(c):
    if c < GATE_COL:
        return c
    if c < MAIN_WIDTH:
        return c + 2 * N_HEADS
    if c < MAIN_WIDTH + 2 * N_HEADS:
        return GATE_COL + (c - MAIN_WIDTH)
    return None


def _runs(first, count, mapping):
    runs, i = [], 0
    while i < count:
        start, n = mapping(first + i), 1
        while i + n < count and mapping(first + i + n) == start + n:
            n += 1
        runs.append((start, n))
        i += n
    return runs


def weights_to_cat(g_in, name, total_rows, row0=0, into=None):
    n_dev, rows, shard = g_in.shape
    first = row0 // RELAYOUT_ROWS

    def body(x_ref, *rest):
        o_ref = rest[-1]
        for b in range(CAT_WIDTH // LANES):
            live = sum(_win_of_cat(LANES * b + i) is not None for i in range(LANES))
            parts = []
            for start, n in _runs(LANES * b, live, _win_of_cat):
                while n > 0:
                    d, o = divmod(start, shard)
                    take = min(n, shard - o)
                    parts.append(x_ref[d, :, o:o + take])
                    start, n = start + take, n - take
            if live < LANES:
                parts.append(jnp.zeros((RELAYOUT_ROWS, LANES - live), g_in.dtype))
            o_ref[:, LANES * b:LANES * (b + 1)] = parts[0] if len(parts) == 1 else jnp.concatenate(parts, axis=1)

    return pl.pallas_call(
        body, name=name, grid=(rows // RELAYOUT_ROWS,),
        in_specs=[pl.BlockSpec((n_dev, RELAYOUT_ROWS, shard), lambda i: (0, i, 0))] + ([ANY_SPEC] if into is not None else []),
        out_specs=pl.BlockSpec((RELAYOUT_ROWS, CAT_WIDTH), lambda i: (first + i, 0)),
        out_shape=jax.ShapeDtypeStruct((total_rows, CAT_WIDTH), g_in.dtype),
        input_output_aliases={1: 0} if into is not None else {},
        compiler_params=_params(("parallel",)))(*((g_in,) if into is None else (g_in, into)))


def cat_to_shards(dw_cat, shard):
    rows = dw_cat.shape[0]

    def body(x_ref, o_ref):
        for d in range(N_DEV):
            for t0 in range(0, shard, LANES):
                width = min(LANES, shard - t0)
                parts = [x_ref[:, c:c + n] for c, n in _runs(d * shard + t0, width, _cat_of_win)]
                o_ref[d, :, t0:t0 + width] = parts[0] if len(parts) == 1 else jnp.concatenate(parts, axis=1)

    return pl.pallas_call(
        body, name="cat_to_shards", grid=(rows // RELAYOUT_ROWS,),
        in_specs=[pl.BlockSpec((RELAYOUT_ROWS, CAT_WIDTH), lambda i: (i, 0))],
        out_specs=pl.BlockSpec((N_DEV, RELAYOUT_ROWS, shard), lambda i: (0, i, 0)),
        out_shape=jax.ShapeDtypeStruct((N_DEV, rows, shard), dw_cat.dtype),
        compiler_params=_params(("parallel",)))(dw_cat)


ROW_BLOCK = 512


def rms_fwd(x, w, name):
    t, d = x.shape

    def body(x_ref, w_ref, n_ref, r_ref):
        h = x_ref[...]
        r = lax.rsqrt(jnp.mean(h * h, axis=-1, keepdims=True) + NORM_EPS)
        n_ref[...] = (h * r * w_ref[...]).astype(BF16)
        r_ref[...] = r

    row = pl.BlockSpec((ROW_BLOCK, d), lambda i: (i, 0))
    return pl.pallas_call(
        body, name=name, grid=(t // ROW_BLOCK,),
        in_specs=[row, pl.BlockSpec((1, d), lambda i: (0, 0))],
        out_specs=[row, pl.BlockSpec((ROW_BLOCK, 1), lambda i: (i, 0))],
        out_shape=[jax.ShapeDtypeStruct((t, d), BF16), jax.ShapeDtypeStruct((t, 1), F32)],
        compiler_params=_params(("parallel",)))(x, w)


FUSED_ROWS = 512


def out_proj_rms(y, w_out, x, w_norm, name):
    t, d = x.shape

    def body(y_ref, w_ref, x_ref, g_ref, h_ref, n_ref, r_ref):
        h = x_ref[...] + _dot(y_ref[...], w_ref[...], NN)
        r = lax.rsqrt(jnp.mean(h * h, axis=-1, keepdims=True) + NORM_EPS)
        h_ref[...] = h
        n_ref[...] = (h * r * g_ref[...]).astype(BF16)
        r_ref[...] = r

    row = pl.BlockSpec((FUSED_ROWS, d), lambda i: (i, 0))
    return pl.pallas_call(
        body, name=name, grid=(t // FUSED_ROWS,),
        in_specs=[pl.BlockSpec((FUSED_ROWS, y.shape[1]), lambda i: (i, 0)), pl.BlockSpec(w_out.shape, lambda i: (0, 0)),
                  row, pl.BlockSpec((1, d), lambda i: (0, 0))],
        out_specs=[row, row, pl.BlockSpec((FUSED_ROWS, 1), lambda i: (i, 0))],
        out_shape=[jax.ShapeDtypeStruct((t, d), F32), jax.ShapeDtypeStruct((t, d), BF16),
                   jax.ShapeDtypeStruct((t, 1), F32)],
        compiler_params=_params(("parallel",)))(y, w_out, x, w_norm)


def ff2_loss(act, w_ff2, h1, w, target, name, tk=2048):
    t, d = h1.shape
    ksteps = act.shape[1] // tk

    def body(a_ref, b_ref, h_ref, w_ref, t_ref, loss_ref, dhb_ref, dw_ref, acc_ref):
        i, kk = pl.program_id(0), pl.program_id(1)

        @pl.when((i == 0) & (kk == 0))
        def _():
            loss_ref[...] = jnp.zeros_like(loss_ref)
            dw_ref[...] = jnp.zeros_like(dw_ref)

        @pl.when(kk == 0)
        def _():
            acc_ref[...] = h_ref[...]

        acc_ref[...] += _dot(a_ref[...], b_ref[...], NN)

        @pl.when(kk == ksteps - 1)
        def _():
            h = acc_ref[...]
            wv = w_ref[...]
            r = lax.rsqrt(jnp.mean(h * h, axis=-1, keepdims=True) + NORM_EPS)
            yn = h * r
            e = yn * wv - t_ref[...]
            loss_ref[...] += 0.5 * jnp.sum(jnp.sum(e * e, axis=-1, keepdims=True), axis=0, keepdims=True) / d
            dy = e / d
            dw_ref[...] += jnp.sum(dy * yn, axis=0, keepdims=True)
            dyn = dy * wv
            dhb_ref[...] = (r * (dyn - yn * jnp.mean(dyn * yn, axis=-1, keepdims=True))).astype(BF16)

    row = pl.BlockSpec((FUSED_ROWS, d), lambda i, k: (i, 0))
    wspec = pl.BlockSpec((1, d), lambda i, k: (0, 0))
    return pl.pallas_call(
        body, name=name, grid=(t // FUSED_ROWS, ksteps),
        in_specs=[pl.BlockSpec((FUSED_ROWS, tk), lambda i, k: (i, k)), pl.BlockSpec((tk, d), lambda i, k: (k, 0)),
                  row, wspec, row],
        out_specs=[pl.BlockSpec((1, 1), lambda i, k: (0, 0)), row, wspec],
        out_shape=[jax.ShapeDtypeStruct((1, 1), F32), jax.ShapeDtypeStruct((t, d), BF16),
                   jax.ShapeDtypeStruct((1, d), F32)],
        scratch_shapes=[pltpu.VMEM((FUSED_ROWS, d), F32)],
        compiler_params=_params(("arbitrary", "arbitrary")))(act, w_ff2, h1, w, target)


def rms_bwd(h, r, w, dn, dres, out_dtype, name):
    t, d = h.shape

    def body(h_ref, r_ref, w_ref, dn_ref, dres_ref, dh_ref, dw_ref):
        @pl.when(pl.program_id(0) == 0)
        def _():
            dw_ref[...] = jnp.zeros_like(dw_ref)

        rv = r_ref[...]
        yn = h_ref[...] * rv
        dnv = dn_ref[...].astype(F32)
        dw_ref[...] += jnp.sum(dnv * yn, axis=0, keepdims=True)
        dyn = dnv * w_ref[...]
        dh = dres_ref[...].astype(F32) + rv * (dyn - yn * jnp.mean(dyn * yn, axis=-1, keepdims=True))
        dh_ref[...] = dh.astype(out_dtype)

    row = pl.BlockSpec((ROW_BLOCK, d), lambda i: (i, 0))
    wspec = pl.BlockSpec((1, d), lambda i: (0, 0))
    rspec = pl.BlockSpec((ROW_BLOCK, 1), lambda i: (i, 0))
    return pl.pallas_call(
        body, name=name, grid=(t // ROW_BLOCK,),
        in_specs=[row, rspec, wspec, row, row], out_specs=[row, wspec],
        out_shape=[jax.ShapeDtypeStruct((t, d), out_dtype), jax.ShapeDtypeStruct((1, d), F32)],
        compiler_params=_params(("arbitrary",)))(h, r, w, dn, dres)


CONV_ROWS = 512
TILE_ROWS = 8


def _iota2(shape, axis):
    return lax.broadcasted_iota(jnp.int32, shape, axis)


def _silu(x):
    return x * jax.nn.sigmoid(x)


def _conv_rows(x_ref, w, first, rows):
    acc = None
    for j in range(4):
        term = x_ref[first - 3 + j:first - 3 + j + rows, :] * w[j:j + 1, :]
        acc = term if acc is None else acc + term
    return acc


def _head_shifts(head):
    rows = _iota2((TILE_ROWS, 1), 0)
    return [jnp.where(rows >= 3 - j, head if j == 3 else pltpu.roll(head, 3 - j, 0), 0.0) for j in range(4)]


def _conv_chunks(t):
    pieces = [(TILE_ROWS, min(CONV_ROWS, t) - TILE_ROWS)]
    pieces += [(r, CONV_ROWS) for r in range(CONV_ROWS, t, CONV_ROWS)]
    return pieces


def conv_fwd(proj, conv_w, name):
    t = proj.shape[0]

    def body(x_ref, w_ref, o_ref):
        w = w_ref[...]
        shifted = _head_shifts(x_ref[0:TILE_ROWS, :])
        o_ref[0:TILE_ROWS, :] = _silu(sum(shifted[j] * w[j:j + 1, :] for j in range(4)))
        for first, rows in _conv_chunks(t):
            o_ref[first:first + rows, :] = _silu(_conv_rows(x_ref, w, first, rows))

    col = pl.BlockSpec((t, LANES), lambda c: (0, c))
    return pl.pallas_call(
        body, name=name, grid=(QKV_WIDTH // LANES,),
        in_specs=[col, pl.BlockSpec((4, LANES), lambda c: (0, c))], out_specs=col,
        out_shape=jax.ShapeDtypeStruct((t, QKV_WIDTH), F32),
        compiler_params=_params(("parallel",)))(proj, conv_w)


def conv_bwd(proj, dout, conv_w, dproj, name):
    t = proj.shape[0]

    def dsilu(pre):
        sg = jax.nn.sigmoid(pre)
        return sg * (1.0 + pre * (1.0 - sg))

    def body(x_ref, d_ref, w_ref, dproj_in, dx_ref, dw_ref, stage):
        del dproj_in
        w = w_ref[...]
        shifted = _head_shifts(x_ref[0:TILE_ROWS, :])
        head_dpre = d_ref[0:TILE_ROWS, :] * dsilu(sum(shifted[j] * w[j:j + 1, :] for j in range(4)))
        stage[0:TILE_ROWS, :] = head_dpre
        for first, rows in _conv_chunks(t):
            stage[first:first + rows, :] = d_ref[first:first + rows, :] * dsilu(_conv_rows(x_ref, w, first, rows))
        stage[t:t + TILE_ROWS, :] = jnp.zeros((TILE_ROWS, LANES), F32)
        for first, rows in [(0, TILE_ROWS)] + _conv_chunks(t):
            dx = None
            for j in range(4):
                term = stage[first + 3 - j:first + 3 - j + rows, :] * w[j:j + 1, :]
                dx = term if dx is None else dx + term
            dx_ref[first:first + rows, :] = dx.astype(BF16)
        dw = [jnp.sum(head_dpre * shifted[j], axis=0, keepdims=True) for j in range(4)]
        for first, rows in _conv_chunks(t):
            dpre = stage[first:first + rows, :]
            for j in range(4):
                dw[j] = dw[j] + jnp.sum(dpre * x_ref[first - 3 + j:first - 3 + j + rows, :], axis=0, keepdims=True)
        dw_ref[...] = jnp.concatenate(dw, axis=0)

    col = pl.BlockSpec((t, LANES), lambda c: (0, c))
    taps = pl.BlockSpec((4, LANES), lambda c: (0, c))
    return pl.pallas_call(
        body, name=name, grid=(QKV_WIDTH // LANES,),
        in_specs=[col, col, taps, ANY_SPEC], out_specs=[col, taps],
        out_shape=[jax.ShapeDtypeStruct(dproj.shape, BF16), jax.ShapeDtypeStruct((4, QKV_WIDTH), F32)],
        scratch_shapes=[pltpu.VMEM((t + TILE_ROWS, LANES), F32)],
        input_output_aliases={3: 0},
        compiler_params=_params(("parallel",)))(proj, dout, conv_w, dproj)


def _softplus(x):
    return jnp.maximum(x, 0.0) + jnp.log(1.0 + jnp.exp(-jnp.abs(x)))


def _head_norm_gate(o, norm_w, gate):
    return o * lax.rsqrt(jnp.mean(o * o, axis=-1, keepdims=True) + NORM_EPS) * norm_w * _silu(gate)


GDN_PREC = ("bf", "bf")
HGRN_PREC = "bf"


def _each(fn, *cols):
    return [fn(*a) for a in zip(*cols)]


@functools.partial(jax.custom_vjp, nondiff_argnums=(2,))
def _known_inverse(low, inv, prec):
    del low, prec
    return inv


def _known_inverse_fwd(low, inv, prec):
    del low
    return inv, inv


def _known_inverse_bwd(prec, inv, ct):
    return -_mm_raw(_mm_raw(inv, ct, TN, prec), inv, NT, prec), jnp.zeros_like(inv)


_known_inverse.defvjp(_known_inverse_fwd, _known_inverse_bwd)


def gdn_stages(hs, qc, kc, vc, zc, ab, a_log_l, dt_l, norm_w, s, prec=GDN_PREC, inv_known=None):
    p_inv, p_mm = prec
    c = CHUNK
    ri, ci = _iota2((c, c), 0), _iota2((c, c), 1)
    incl, strict, eye = ri >= ci, ri > ci, ri == ci
    lane = _iota2((c, LANES), 1)
    last_row = _iota2((c, 1), 0) == c - 1
    rowsum = lambda x: jnp.sum(x, axis=1, keepdims=True)

    def row(col):
        return jnp.sum(jnp.where(eye, col, 0.0), axis=0, keepdims=True)

    q = _each(lambda x: x * lax.rsqrt(rowsum(x * x) + L2_EPS) * (HEAD_DIM ** -0.5), qc)
    k = _each(lambda x: x * lax.rsqrt(rowsum(x * x) + L2_EPS), kc)
    yield
    a_col = [rowsum(jnp.where(lane == h, ab, 0.0)) for h in hs]
    b_col = [rowsum(jnp.where(lane == h + N_HEADS, ab, 0.0)) for h in hs]
    beta = _each(jax.nn.sigmoid, b_col)
    g = _each(lambda a, al, dl: rowsum(jnp.where(lane == 0, -jnp.exp(al) * _softplus(a + dl), 0.0)), a_col, a_log_l, dt_l)
    gcum = _each(lambda x: rowsum(jnp.where(incl, row(x), 0.0)), g)
    g_last = _each(lambda x: jnp.sum(jnp.where(last_row, x, 0.0), axis=0, keepdims=True), gcum)
    decay = _each(lambda x: jnp.exp(jnp.where(incl, x - row(x), -jnp.inf)), gcum)
    yield
    kk = _each(lambda x: mm(x, x, NT, p_mm), k)
    low = _each(lambda b, x, d: jnp.where(strict, b * x * d, 0.0), beta, kk, decay)
    yield
    if inv_known is None:
        power = _each(lambda x: -x, low)
        inv = _each(lambda x: jnp.where(eye, 1.0, 0.0) + x, power)
        for _ in range(5):
            power = _each(lambda x: mm(x, x, NN, p_inv), power)
            yield
            inv = _each(lambda x, p: x + mm(x, p, NN, p_inv), inv, power)
            yield
    else:
        inv = _each(lambda x, known: _known_inverse(x, known, p_inv), low, inv_known)
    exp_g = _each(jnp.exp, gcum)
    yield
    u_v = _each(lambda i, b, x: mm(i, b * x, NN, p_mm), inv, beta, vc)
    w = _each(lambda i, b, e, x: mm(i, b * e * x, NN, p_mm), inv, beta, exp_g, k)
    yield
    attn = _each(lambda x, y, d: mm(x, y, NT, p_mm) * d, q, k, decay)
    yield
    u = _each(lambda x, y, z: x - mm(y, z, NN, p_mm), u_v, w, s)
    yield
    o = _each(lambda x, e, z: mm(x * e, z, NN, p_mm), q, exp_g, s)
    o = _each(lambda x, a, y: x + mm(a, y, NN, p_mm), o, attn, u)
    yield
    k_end = _each(lambda x, gl, gc: x * jnp.exp(gl - gc), k, g_last, gcum)
    s_new = _each(lambda z, gl, x, y: z * jnp.exp(gl) + mm(x, y, TN, p_mm), s, g_last, k_end, u)
    return (_each(lambda x, z: _head_norm_gate(x, norm_w, z), o, zc), s_new), inv


def gdn_chunk(h, qc, kc, vc, zc, ab, a_log_l, dt_l, norm_w, s, prec=GDN_PREC, reuse_inverse=False):
    args = ([h], [qc], [kc], [vc], [zc], ab, [a_log_l], [dt_l], norm_w, [s], prec)
    if reuse_inverse:
        inv = lax.stop_gradient(gdn_chunks(*args)[1])
        (y, s_new), _ = gdn_chunks(*args, inv_known=inv)
    else:
        (y, s_new), _ = gdn_chunks(*args)
    return y[0], s_new[0]


DIAG_ROWS = SUB_CHUNK // 2
SHIFT_PAD = 8
SHIFT_ROWS = SHIFT_PAD + CHUNK + SHIFT_PAD
SHIFT_WAYS = 4


class RolledRows:
    def down(self, x, which):
        del which
        return [x] + [pltpu.roll(x, off, 0) for off in range(1, DIAG_ROWS)]

    def up_sum(self, parts, which):
        del which
        acc = parts[0]
        for off in range(1, DIAG_ROWS):
            acc = acc + pltpu.roll(parts[off], CHUNK - off, 0)
        return acc


class SlotRows:
    def __init__(self, slots):
        self.slots = slots

    def down(self, x, which):
        self.slots[which, 0, SHIFT_PAD:SHIFT_PAD + CHUNK, :] = x
        return [x] + [self.slots[which, 0, SHIFT_PAD - off:SHIFT_PAD + CHUNK - off, :] for off in range(1, DIAG_ROWS)]

    def up_sum(self, parts, which):
        acc = parts[0]
        for off in range(1, DIAG_ROWS):
            way = 1 + off % (SHIFT_WAYS - 1)
            self.slots[which, way, SHIFT_PAD:SHIFT_PAD + CHUNK, :] = parts[off]
            acc = acc + self.slots[which, way, SHIFT_PAD + off:SHIFT_PAD + CHUNK + off, :]
        return acc


def _sub_block_rows():
    return jnp.bitwise_and(_iota2((CHUNK, 1), 0), DIAG_ROWS - 1)


def _diag_forward(rows, q, key, bc, v):
    rmod = _sub_block_rows()
    k_d, b_d, v_d = rows.down(key, 0), rows.down(bc, 1), rows.down(v, 2)
    o = None
    for off in range(DIAG_ROWS):
        e = jnp.exp(jnp.where(rmod >= off, bc - b_d[off], -jnp.inf))
        term = jnp.sum(q * k_d[off] * e, axis=-1, keepdims=True) * v_d[off]
        o = term if o is None else o + term
    return o


def _diag_backward(rows, q, key, bc, v, do):
    rmod = _sub_block_rows()
    k_d, b_d, v_d = rows.down(key, 0), rows.down(bc, 1), rows.down(v, 2)
    dq = db = None
    dk_parts, db_parts, dv_parts = [], [], []
    for off in range(DIAG_ROWS):
        e = jnp.exp(jnp.where(rmod >= off, bc - b_d[off], -jnp.inf))
        qe = q * e
        a = jnp.sum(qe * k_d[off], axis=-1, keepdims=True)
        da = jnp.sum(do * v_d[off], axis=-1, keepdims=True)
        dv_parts.append(a * do)
        dq_term = (da * e) * k_d[off]
        dk_term = da * qe
        s = dk_term * k_d[off]
        dq = dq_term if dq is None else dq + dq_term
        db = s if db is None else db + s
        dk_parts.append(dk_term)
        db_parts.append(s)
    return dq, rows.up_sum(dk_parts, 0), db - rows.up_sum(db_parts, 1), rows.up_sum(dv_parts, 2)


def diag_part(rows, differentiable=True):
    forward = functools.partial(_diag_forward, rows)
    if not differentiable:
        return forward
    part = jax.custom_vjp(forward)
    part.defvjp(lambda q, key, bc, v: (forward(q, key, bc, v), (q, key, bc, v)),
                lambda res, do: _diag_backward(rows, *res, do))
    return part


def hgrn_stages(qb, fb, ib, gb, l0, l1, norm_w, st, prec=HGRN_PREC, diags=None, o_known=None):
    c = CHUNK
    ri, ci = _iota2((4 * c, c), 0), _iota2((4 * c, c), 1)
    rcol = _iota2((c, 1), 0)
    blk0 = jnp.bitwise_and(ri, c - SUB_CHUNK)
    limit = jnp.where(ri < c, ri + 1, jnp.where(ri < 2 * c, blk0, jnp.where(ri < 3 * c, blk0 + SUB_CHUNK,
                                                                          blk0 + DIAG_ROWS)))
    sel = jnp.where(ci < limit, 1.0, 0.0)
    ri, ci = _iota2((c, c), 0), _iota2((c, c), 1)
    lb = _each(lambda a, b: jax.nn.sigmoid(a - b), l0, l1)
    forget = _each(lambda b, f: b + (1.0 - b) * jax.nn.sigmoid(f), lb, fb)
    key = _each(lambda b, f: (1.0 - b) * jax.nn.sigmoid(-f), lb, fb)
    q = _each(_silu, qb)
    v = ib
    logf = _each(jnp.log, forget)
    sums = _each(lambda x: sel_sums(sel, x), logf)
    bc, b_start, b_end, b_half = ([x[i] for x in sums] for i in range(4))
    b_last = _each(lambda x: jnp.sum(x, axis=0, keepdims=True), logf)
    o = _each(lambda x, b, z: mm(x * jnp.exp(b), z, NT, prec), q, bc, st)
    if diags is None:
        diags = [diag_part(RolledRows())] * len(qb)
    yield
    o = list(o)
    for h in range(len(o)):
        o[h] = o[h] + diags[h](q[h], key[h], bc[h], v[h])
        yield
    second = jnp.bitwise_and(rcol, SUB_CHUNK - 1) >= DIAG_ROWS
    same_sub = jnp.bitwise_and(ri, c - SUB_CHUNK) == jnp.bitwise_and(ci, c - SUB_CHUNK)
    q_half = _each(lambda x, b, bh: x * jnp.exp(jnp.where(second, b - bh, -jnp.inf)), q, bc, b_half)
    k_half = _each(lambda x, b, bh: x * jnp.exp(jnp.where(second, -jnp.inf, bh - b)), key, bc, b_half)
    a_half = _each(lambda x, z: jnp.where(same_sub, mm(x, z, NT, prec), 0.0), q_half, k_half)
    o = _each(lambda acc, a, val: acc + mm(a, val, NN, prec), o, a_half, v)
    yield
    q_rel = _each(lambda x, b, bs: x * jnp.exp(b - bs), q, bc, b_start)
    k_rel = _each(lambda x, b, be: x * jnp.exp(be - b), key, bc, b_end)
    for y in range(c // SUB_CHUNK - 1):
        def scaled(x, b, bs):
            end_y = jnp.sum(jnp.where(rcol == SUB_CHUNK * y + SUB_CHUNK - 1, b, 0.0), axis=0, keepdims=True)
            return x * jnp.exp(jnp.where(rcol >= SUB_CHUNK * (y + 1), bs - end_y, -jnp.inf))
        dq = _each(scaled, q_rel, bc, b_start)
        in_y = (ci >= SUB_CHUNK * y) & (ci < SUB_CHUNK * (y + 1))
        a_y = _each(lambda x, z: jnp.where(in_y, mm(x, z, NT, prec), 0.0), dq, k_rel)
        o = _each(lambda acc, a, val: acc + mm(a, val, NN, prec), o, a_y, v)
        yield
    k_state = _each(lambda x, bl, b: x * jnp.exp(bl - b), key, b_last, bc)
    st_new = _each(lambda z, bl, val, x: z * jnp.exp(bl) + mm(val, x, TN, prec), st, b_last, v, k_state)
    if o_known is not None:
        o = _each(_known_value, o, o_known)
    return (_each(lambda x, z: _head_norm_gate(x, norm_w, z), o, gb), st_new), o


def _drain(gen):
    try:
        while True:
            next(gen)
    except StopIteration as done:
        return done.value


def _alternate(gen_a, gen_b):
    out, live = [None, None], [gen_a, gen_b]
    while any(g is not None for g in live):
        for i, g in enumerate(live):
            if g is None:
                continue
            try:
                next(g)
            except StopIteration as done:
                out[i], live[i] = done.value, None
    return out


def gdn_chunks(*args, **kwargs):
    return _drain(gdn_stages(*args, **kwargs))


def hgrn_chunks(*args, **kwargs):
    return _drain(hgrn_stages(*args, **kwargs))


def hgrn_chunk(qb, fb, ib, gb, l0, l1, norm_w, st, prec=HGRN_PREC, reuse_output=False):
    args = ([qb], [fb], [ib], [gb], [l0], [l1], norm_w, [st], prec)
    if reuse_output:
        known = lax.stop_gradient(hgrn_chunks(*args)[1])
        (y, st_new), _ = hgrn_chunks(*args, o_known=known)
    else:
        (y, st_new), _ = hgrn_chunks(*args)
    return y[0], st_new[0]


HEAD_VEC = (N_HEADS, 1, LANES)


class _ChunkSpecs:
    def __init__(self, nc, rev):
        self.nc, self.rev = nc, rev

    def _c(self, c):
        return self.nc - 1 - c if self.rev else c

    def row(self, width, block=0):
        return pl.BlockSpec((CHUNK, width), lambda c: (self._c(c), block))

    def per_head(self, rows):
        return pl.BlockSpec((None, N_HEADS, rows, rows), lambda c: (self._c(c), 0, 0, 0))

    @staticmethod
    def whole(shape):
        return pl.BlockSpec(shape, lambda c: (0,) * len(shape))


def _lanes(j):
    return slice(j * LANES, (j + 1) * LANES)


def mixer_fwd(qkv_c, proj, a_log_l, dt_l, gdn_norm_w, l0, l1, hgrn_norm_w, name):
    t = qkv_c.shape[0]
    hb = N_HEADS
    sp = _ChunkSpecs(t // CHUNK, rev=False)
    hs = list(range(hb))

    def body(q_ref, k_ref, v_ref, z_ref, ab_ref, al_ref, dt_ref, gnw_ref, qb_ref, fb_ref, ib_ref, gb_ref, l0_ref, l1_ref,
             hnw_ref, y_ref, hist_a_ref, inv_ref, hist_b_ref, o_ref, sa_ref, sb_ref, shift_ref):
        @pl.when(pl.program_id(0) == 0)
        def _():
            sa_ref[...] = jnp.zeros_like(sa_ref)
            sb_ref[...] = jnp.zeros_like(sb_ref)
            shift_ref[...] = jnp.zeros_like(shift_ref)

        heads = lambda ref: [ref[:, _lanes(j)] for j in hs]
        s_a, s_b = [sa_ref[h] for h in hs], [sb_ref[h] for h in hs]
        for h in hs:
            hist_a_ref[h] = s_a[h]
            hist_b_ref[h] = s_b[h]
        diags = [diag_part(SlotRows(shift_ref.at[h]), differentiable=False) for h in hs]
        ((y_a, s_a_new), inv), ((y_b, s_b_new), o_pre) = _alternate(
            gdn_stages(hs, heads(q_ref), heads(k_ref), heads(v_ref), heads(z_ref), ab_ref[...],
                       [al_ref[h] for h in hs], [dt_ref[h] for h in hs], gnw_ref[...], s_a),
            hgrn_stages(heads(qb_ref), heads(fb_ref), heads(ib_ref), heads(gb_ref),
                        [l0_ref[h] for h in hs], [l1_ref[h] for h in hs], hnw_ref[...], s_b, diags=diags))
        for h in hs:
            y_ref[:, _lanes(h)] = y_a[h].astype(BF16)
            y_ref[:, _lanes(hb + h)] = y_b[h].astype(BF16)
            o_ref[:, _lanes(h)] = o_pre[h]
            sa_ref[h] = s_a_new[h]
            sb_ref[h] = s_b_new[h]
            inv_ref[h] = inv[h]

    vec, gain, slab = sp.whole(HEAD_VEC), sp.whole((1, LANES)), functools.partial(sp.row, GDN_WIDTH)
    states = jax.ShapeDtypeStruct((sp.nc, N_HEADS, HEAD_DIM, HEAD_DIM), F32)
    return pl.pallas_call(
        body, name=name, grid=(sp.nc,),
        in_specs=[slab(0), slab(1), slab(2), slab(3), sp.row(LANES, AB_BLOCK), vec, vec, gain,
                  slab(4), slab(5), slab(6), slab(7), vec, vec, gain],
        out_specs=[sp.row(2 * GDN_WIDTH), sp.per_head(HEAD_DIM), sp.per_head(CHUNK), sp.per_head(HEAD_DIM), slab(0)],
        out_shape=[jax.ShapeDtypeStruct((t, 2 * GDN_WIDTH), BF16), states,
                   jax.ShapeDtypeStruct((sp.nc, N_HEADS, CHUNK, CHUNK), F32), states,
                   jax.ShapeDtypeStruct((t, GDN_WIDTH), F32)],
        scratch_shapes=[pltpu.VMEM((N_HEADS, HEAD_DIM, HEAD_DIM), F32), pltpu.VMEM((N_HEADS, HEAD_DIM, HEAD_DIM), F32),
                        pltpu.VMEM((hb, 3, SHIFT_WAYS, SHIFT_ROWS, LANES), F32)],
        compiler_params=_params(("arbitrary",)),
    )(qkv_c, qkv_c, qkv_c, proj, proj, a_log_l, dt_l, gdn_norm_w, proj, proj, proj, proj, l0, l1, hgrn_norm_w)


def mixer_bwd(qkv_c, proj, a_log_l, dt_l, gdn_norm_w, l0, l1, hgrn_norm_w, hist_a, inv_hist, hist_b, o_pre, dy, name):
    t = qkv_c.shape[0]
    hb = N_HEADS
    sp = _ChunkSpecs(t // CHUNK, rev=True)
    hs = list(range(hb))

    def body(q_ref, k_ref, v_ref, z_ref, ab_ref, al_ref, dt_ref, gnw_ref, qb_ref, fb_ref, ib_ref, gb_ref, l0_ref, l1_ref,
             hnw_ref, hist_a_ref, inv_ref, hist_b_ref, o_ref, dy_ref,
             dqkv_ref, dproj_ref, dal_ref, ddt_ref, dgnw_ref, dl0_ref, dl1_ref, dhnw_ref, dsa_ref, dsb_ref, shift_ref):
        @pl.when(pl.program_id(0) == 0)
        def _():
            for ref in (dal_ref, ddt_ref, dgnw_ref, dl0_ref, dl1_ref, dhnw_ref, dsa_ref, dsb_ref, shift_ref):
                ref[...] = jnp.zeros_like(ref)

        heads = lambda ref, first=0: [ref[:, _lanes(first + j)] for j in hs]
        diags = [diag_part(SlotRows(shift_ref.at[h])) for h in hs]
        inv_known, o_known = [inv_ref[h] for h in hs], heads(o_ref)

        def both(ga, gb):
            (ra, inv), (rb, o_pre) = _alternate(gdn_stages(hs, *ga, inv_known=inv_known),
                                                hgrn_stages(*gb, diags=diags, o_known=o_known))
            return (ra, rb), (inv, o_pre)

        ga = (heads(q_ref), heads(k_ref), heads(v_ref), heads(z_ref), ab_ref[...], [al_ref[h] for h in hs],
              [dt_ref[h] for h in hs], gnw_ref[...], [hist_a_ref[h] for h in hs])
        gb = (heads(qb_ref), heads(fb_ref), heads(ib_ref), heads(gb_ref), [l0_ref[h] for h in hs],
              [l1_ref[h] for h in hs], hnw_ref[...], [hist_b_ref[h] for h in hs])
        _, vjp, _ = jax.vjp(both, ga, gb, has_aux=True)
        dy_a = [x.astype(F32) for x in heads(dy_ref)]
        dy_b = [x.astype(F32) for x in heads(dy_ref, hb)]
        (dq, dk, dv, dz, dab, dal, ddt, dgnw, ds_a), (dqb, dfb, dib, dgb, dl0, dl1, dhnw, ds_b) = vjp(
            ((dy_a, [dsa_ref[h] for h in hs]), (dy_b, [dsb_ref[h] for h in hs])))
        for h in hs:
            dqkv_ref[:, _lanes(h)] = dq[h]
            dqkv_ref[:, _lanes(hb + h)] = dk[h]
            dqkv_ref[:, _lanes(2 * hb + h)] = dv[h]
            for slab, val in enumerate((dz, dqb, dfb, dib, dgb)):
                dproj_ref[:, _lanes((3 + slab) * hb + h)] = val[h].astype(BF16)
            dal_ref[h] += dal[h]
            ddt_ref[h] += ddt[h]
            dl0_ref[h] += dl0[h]
            dl1_ref[h] += dl1[h]
            dsa_ref[h] = ds_a[h]
            dsb_ref[h] = ds_b[h]
        dproj_ref[:, MAIN_WIDTH:] = dab.astype(BF16)
        dgnw_ref[...] += dgnw
        dhnw_ref[...] += dhnw

    vec, gain, slab = sp.whole(HEAD_VEC), sp.whole((1, LANES)), functools.partial(sp.row, GDN_WIDTH)
    vec_shape, gain_shape = jax.ShapeDtypeStruct(HEAD_VEC, F32), jax.ShapeDtypeStruct((1, LANES), F32)
    return pl.pallas_call(
        body, name=name, grid=(sp.nc,),
        in_specs=[slab(0), slab(1), slab(2), slab(3), sp.row(LANES, AB_BLOCK), vec, vec, gain,
                  slab(4), slab(5), slab(6), slab(7), vec, vec, gain,
                  sp.per_head(HEAD_DIM), sp.per_head(CHUNK), sp.per_head(HEAD_DIM), slab(0), sp.row(2 * GDN_WIDTH)],
        out_specs=[sp.row(QKV_WIDTH), sp.row(CAT_WIDTH), vec, vec, gain, vec, vec, gain],
        out_shape=[jax.ShapeDtypeStruct((t, QKV_WIDTH), F32), jax.ShapeDtypeStruct((t, CAT_WIDTH), BF16),
                   vec_shape, vec_shape, gain_shape, vec_shape, vec_shape, gain_shape],
        scratch_shapes=[pltpu.VMEM((N_HEADS, HEAD_DIM, HEAD_DIM), F32), pltpu.VMEM((N_HEADS, HEAD_DIM, HEAD_DIM), F32),
                        pltpu.VMEM((hb, 3, SHIFT_WAYS, SHIFT_ROWS, LANES), F32)],
        compiler_params=_params(("arbitrary",)),
    )(qkv_c, qkv_c, qkv_c, proj, proj, a_log_l, dt_l, gdn_norm_w, proj, proj, proj, proj, l0, l1, hgrn_norm_w,
      hist_a, inv_hist, hist_b, o_pre, dy)


def _adamw(w, g, m, v):
    m = ADAM_B1 * m + (1.0 - ADAM_B1) * g
    v = ADAM_B2 * v + (1.0 - ADAM_B2) * jnp.square(g)
    m_hat = m / (1.0 - ADAM_B1 ** ADAM_STEP)
    v_hat = v / (1.0 - ADAM_B2 ** ADAM_STEP)
    delta = -ADAM_LR * (m_hat / (jnp.sqrt(v_hat) + ADAM_EPS) + ADAM_WD * w)
    return delta, m, v


def adamw_reduce(parts, w, m, v, name, rb=128):
    r, c = w.shape
    rb = min(rb, r)
    n_parts = parts.shape[0]

    def body(p_ref, w_ref, m_ref, v_ref, g_ref, d_ref, mo_ref, vo_ref):
        g = p_ref[0].astype(F32)
        for d in range(1, n_parts):
            g = g + p_ref[d].astype(F32)
        delta, mn, vn = _adamw(w_ref[...], g, m_ref[...], v_ref[...])
        g_ref[...] = g
        d_ref[...] = delta
        mo_ref[...] = mn
        vo_ref[...] = vn

    blk = pl.BlockSpec((rb, c), lambda i: (i, 0))
    return pl.pallas_call(
        body, name=name, grid=(r // rb,),
        in_specs=[pl.BlockSpec((n_parts, rb, c), lambda i: (0, i, 0)), blk, blk, blk],
        out_specs=[blk] * 4, out_shape=[jax.ShapeDtypeStruct((r, c), F32)] * 4,
        compiler_params=_params(("parallel",)))(parts, w, m, v)


def adamw_small(w, g, m, v, name):
    def body(w_ref, g_ref, m_ref, v_ref, d_ref, mo_ref, vo_ref):
        delta, mn, vn = _adamw(w_ref[...], g_ref[...], m_ref[...], v_ref[...])
        d_ref[...] = delta
        mo_ref[...] = mn
        vo_ref[...] = vn

    vmem = pl.BlockSpec(memory_space=pltpu.VMEM)
    return pl.pallas_call(body, name=name, in_specs=[vmem] * 4, out_specs=[vmem] * 3,
                          out_shape=[jax.ShapeDtypeStruct(w.shape, F32)] * 3)(w, g, m, v)


def _pack(arrays):
    flat = jnp.concatenate([a.reshape(-1).astype(F32) for a in arrays])
    rows = -(-flat.shape[0] // (8 * LANES)) * 8
    return jnp.pad(flat, (0, rows * LANES - flat.shape[0])).reshape(rows, LANES)


def _unpack(packed, shapes):
    flat, out, off = packed.reshape(-1), [], 0
    for s in shapes:
        n = 1
        for d in s:
            n *= d
        out.append(flat[off:off + n].reshape(s))
        off += n
    return out


def _relu2_epilogue(acc, _):
    r = jnp.maximum(acc, 0.0)
    return acc, r * r


def _relu2_bwd_epilogue(acc, a1):
    return (acc * (2.0 * jnp.maximum(a1, 0.0)),)


def kernel(x, w_in, conv_w, gdn_a_log, gdn_dt_bias, gdn_norm_w, hgrn_lb_logits, hgrn_norm_w, w_out, norm_mix_w, norm_ffn_w, w_ff1, w_ff2, norm_final_w, loss_target, m_w_in, m_conv_w, m_gdn_a_log, m_gdn_dt_bias, m_gdn_norm_w, m_hgrn_lb_logits, m_hgrn_norm_w, m_w_out, m_norm_mix_w, m_norm_ffn_w, m_w_ff1, m_w_ff2, m_norm_final_w, v_w_in, v_conv_w, v_gdn_a_log, v_gdn_dt_bias, v_gdn_norm_w, v_hgrn_lb_logits, v_hgrn_norm_w, v_w_out, v_norm_mix_w, v_norm_ffn_w, v_w_ff1, v_w_ff2, v_norm_final_w):
    me = _my_flat()
    xs = x[0]
    target = loss_target[0]
    shard_in = w_in.shape[2]
    shard_conv = conv_w.shape[2]

    tok = lambda t: t[0:1, 0:1]
    own = lambda src: lax.dynamic_index_in_dim(src, me, 0, keepdims=False)

    half = D_MODEL // 2
    w_in_b = w_in[0].astype(BF16)
    g_in_a, g_conv = gather_two_level([w_in_b[:half], conv_w[0]], "gather_w_in")
    h_g0, t_g0 = exchange_start([w_in_b[half:]], True, "gather_w_in_low_start", after=[g_in_a], peers=CHIP_PEERS)
    h_g1, t_g1 = exchange_start([w_out[0].astype(BF16), w_ff1[0].astype(BF16)], True, "gather_mid_start", after=[t_g0],
                                peers=CHIP_PEERS)
    h_g2, t_g2 = exchange_start([w_ff2[0].astype(BF16)], True, "gather_ff2_start", after=[t_g1], peers=CHIP_PEERS)
    w_cat = weights_to_cat(g_in_a, "weights_to_cat", D_MODEL)
    conv_full = jnp.transpose(g_conv, (1, 0, 2)).reshape(4, QKV_WIDTH)

    lane_b = lambda p: jnp.broadcast_to(p.reshape(N_HEADS, 1, 1), HEAD_VEC)
    a_log_l, dt_l = lane_b(gdn_a_log[0]), lane_b(gdn_dt_bias[0])
    l0 = hgrn_lb_logits[0].reshape(HEAD_VEC)
    l1 = hgrn_lb_logits[1].reshape(HEAD_VEC)

    n1, r1 = rms_fwd(xs, norm_mix_w + tok(t_g1) + tok(t_g2), "rms_mix")
    proj = matmul(n1, w_cat, "nn", "in_proj_high", (BF16,), tn=CAT_WIDTH // 5, tk=half, k_blocks=(0, 1))
    (s_low,), (l_low,) = exchange_wait(h_g0, "gather_w_in_low_wait", after=[proj], copies=len(CHIP_PEERS))
    h_f0, _ = forward_start([l_low], "gather_w_in_low_forward")
    _, (l_low,) = exchange_wait(_one(h_f0, 0), "forward_w_in_low_wait", copies=len(OTHER_CHIPS))
    w_cat = weights_to_cat(_own_slot(l_low, s_low), "weights_to_cat_low", D_MODEL, row0=half, into=w_cat)
    proj = matmul(n1, w_cat, "nn", "in_proj_low", tn=CAT_WIDTH // 5, tk=half, k_blocks=(1, 1), extra=proj,
                  epilogue=lambda acc, high: (acc + high,))
    qkv_c = conv_fwd(proj, conv_full, "conv_fwd")
    y, hist_a, inv_a, hist_b, o_b = mixer_fwd(qkv_c, proj, a_log_l, dt_l, gdn_norm_w, l0, l1, hgrn_norm_w, "mixer_fwd")
    (s_out, s_ff1), (l_out, l_ff1) = exchange_wait(h_g1, "gather_mid_wait", after=[y], copies=len(CHIP_PEERS))
    (s_ff2,), (l_ff2,) = exchange_wait(h_g2, "gather_ff2_wait", after=[y], copies=len(CHIP_PEERS))
    h_fw, _ = forward_start([l_out, l_ff1, l_ff2], "gather_forward_start")
    _, (l_out,) = exchange_wait(_one(h_fw, 0), "forward_out_wait", copies=len(OTHER_CHIPS))
    w_out_full = _own_slot(l_out, s_out).reshape(D_MODEL, D_MODEL)
    h1, n2, r2 = out_proj_rms(y, w_out_full, xs, norm_ffn_w, "out_proj_rms")
    _, (l_ff1,) = exchange_wait(_one(h_fw, 1), "forward_ff1_wait", after=[n2], copies=len(OTHER_CHIPS))
    w_ff1_sh = _own_slot(l_ff1, s_ff1)
    a1, act = matmul(n2, w_ff1_sh, "nn", "ff1", out_dtypes=(F32, BF16), epilogue=_relu2_epilogue, b_shards=True)
    _, (l_ff2,) = exchange_wait(_one(h_fw, 2), "forward_ff2_wait", after=[act], copies=len(OTHER_CHIPS))
    w_ff2_full = _own_slot(l_ff2, s_ff2).reshape(D_FF, D_MODEL)
    loss_sum, dh2_b, d_final = ff2_loss(act, w_ff2_full, h1, norm_final_w.reshape(1, D_MODEL), target, "ff2_loss")

    da1 = matmul(dh2_b, w_ff2_full, "nt", "d_act", out_dtypes=(BF16,), epilogue=_relu2_bwd_epilogue, extra=a1)
    t_all = xs.shape[0]
    dw_ff2 = matmul(act, dh2_b, "tn", "dw_ff2", out_dtypes=(BF16,), tk=t_all)
    p_ff2 = dw_ff2.reshape(N_DEV, D_FF // N_DEV, D_MODEL)
    h_s1, t_s1 = exchange_start([p_ff2], False, "scatter_ff2_start")
    dn2 = matmul(da1, w_ff1_sh, "nt", "d_n2", out_dtypes=(BF16,), after=[t_s1], b_shards=True, k_group=4)
    p_ff1 = matmul(n2, da1, "tn", "dw_ff1", out_dtypes=(BF16,), tn=D_FF // N_DEV, tk=t_all, after=[t_s1], out_shards=True)
    h_s2, t_s2 = exchange_start([p_ff1], False, "scatter_ff1_start")
    dh1_b, d_ffn = rms_bwd(h1, r2, norm_ffn_w + tok(t_s2), dn2, dh2_b, BF16, "rms_ffn_bwd")
    dmix = matmul(dh1_b, w_out_full, "nt", "d_mix", out_dtypes=(BF16,))
    dw_out = matmul(y, dh1_b, "tn", "dw_out", out_dtypes=(BF16,), tk=t_all)
    p_out = dw_out.reshape(N_DEV, D_MODEL // N_DEV, D_MODEL)
    h_s3, t_s3 = exchange_start([p_out], False, "scatter_out_start")
    d_qkv_c, dproj, d_alog_l, d_dt_l, d_gnw, dl0, dl1, d_hnw = mixer_bwd(
        qkv_c, proj, a_log_l, dt_l, gdn_norm_w + tok(t_s3), l0, l1, hgrn_norm_w, hist_a, inv_a, hist_b, o_b, dmix,
        "mixer_bwd")
    dproj, d_conv_full = conv_bwd(proj, d_qkv_c, conv_full, dproj, "conv_bwd")
    dw_cat = matmul(n1, dproj, "tn", "dw_in", out_dtypes=(BF16,), tm=512, tn=CAT_WIDTH // 5, tk=t_all)
    p_in = cat_to_shards(dw_cat, shard_in)
    h_pair, t_s4 = routed_start(p_in, _to_sibling_routes, "scatter_in_pair_start")

    (s_ff2g,), (r_ff2,) = exchange_wait(h_s1, "scatter_ff2_wait", after=[t_s4])
    (s_ff1g,), (r_ff1,) = exchange_wait(h_s2, "scatter_ff1_wait", after=[t_s4])
    (s_outg,), (r_out,) = exchange_wait(h_s3, "scatter_out_wait", after=[t_s4])
    g_w_ff2, d_w_ff2, nm_w_ff2, nv_w_ff2 = adamw_reduce(
        _own_slot(r_ff2, own(s_ff2g)), w_ff2[0], m_w_ff2[0], v_w_ff2[0], "adamw_w_ff2")
    g_w_ff1, d_w_ff1, nm_w_ff1, nv_w_ff1 = adamw_reduce(
        _own_slot(r_ff1, own(s_ff1g)), w_ff1[0], m_w_ff1[0], v_w_ff1[0], "adamw_w_ff1")
    g_w_out, d_w_out, nm_w_out, nv_w_out = adamw_reduce(
        _own_slot(r_out, own(s_outg)), w_out[0], m_w_out[0], v_w_out[0], "adamw_w_out")
    (p_in,), (from_sibling,) = exchange_wait(h_pair, "scatter_in_pair_wait", after=[d_w_ff2, d_w_ff1, d_w_out],
                                             copies=N_CHIPS)
    chip_sums = pair_sum(p_in, from_sibling, "scatter_in_pair_sum")
    h_chips, t_s5 = routed_start(chip_sums, _to_chips_routes, "scatter_in_chips_start")
    dn1 = matmul(dproj, w_cat, "nt", "d_n1", out_dtypes=(BF16,), tk=CAT_WIDTH // 5, after=[t_s5])
    dx, d_mix = rms_bwd(xs, r1, norm_mix_w, dn1, dh1_b, F32, "rms_mix_bwd")
    (chip_sums,), (r_in,) = exchange_wait(h_chips, "scatter_in_chips_wait", after=[dx], copies=len(OTHER_CHIPS))
    my_chip = me // 2
    r_in = lax.dynamic_update_slice(r_in, lax.dynamic_index_in_dim(chip_sums, my_chip, 0, keepdims=True), (my_chip, 0, 0))
    g_w_in, d_w_in, nm_w_in, nv_w_in = adamw_reduce(r_in, w_in[0], m_w_in[0], v_w_in[0], "adamw_w_in")

    d_lb = jnp.stack([dl0.reshape(GDN_WIDTH), dl1.reshape(GDN_WIDTH)])
    small_shapes = [(1, N_HEADS), (1, N_HEADS), (1, HEAD_DIM), (2, GDN_WIDTH), (1, HEAD_DIM), (1, D_MODEL),
                    (1, D_MODEL), (D_MODEL,), (4, QKV_WIDTH)]
    small = _pack([d_alog_l[:, 0, 0], d_dt_l[:, 0, 0], d_gnw, d_lb, d_hnw, d_mix, d_ffn, d_final, d_conv_full])
    red = allreduce_small(small, "allreduce_small")
    g_alog, g_dt, g_gnw, g_lb, g_hnw, g_mix, g_ffn, g_final, g_conv_full = _unpack(red, small_shapes)
    g_conv = lax.dynamic_slice(g_conv_full, (0, me * shard_conv), (4, shard_conv)).reshape(1, 4, shard_conv)
    small_g = [g_alog, g_dt, g_gnw, g_lb, g_hnw, g_mix, g_ffn, g_final, g_conv]
    small_w = [gdn_a_log, gdn_dt_bias, gdn_norm_w, hgrn_lb_logits, hgrn_norm_w, norm_mix_w, norm_ffn_w, norm_final_w, conv_w]
    small_m = [m_gdn_a_log, m_gdn_dt_bias, m_gdn_norm_w, m_hgrn_lb_logits, m_hgrn_norm_w, m_norm_mix_w, m_norm_ffn_w,
               m_norm_final_w, m_conv_w]
    small_v = [v_gdn_a_log, v_gdn_dt_bias, v_gdn_norm_w, v_hgrn_lb_logits, v_hgrn_norm_w, v_norm_mix_w, v_norm_ffn_w,
               v_norm_final_w, v_conv_w]
    shapes = [a.shape for a in small_w]
    d_s, m_s, v_s = adamw_small(_pack(small_w), _pack(small_g), _pack(small_m), _pack(small_v), "adamw_small")
    d_alog, d_dt, d_gn, d_lbl, d_hn, d_nm, d_nf, d_nfin, d_cw = _unpack(d_s, shapes)
    m_alog, m_dt, m_gn, m_lbl, m_hn, m_nm, m_nf, m_nfin, m_cw = _unpack(m_s, shapes)
    v_alog, v_dt, v_gn, v_lbl, v_hn, v_nm, v_nf, v_nfin, v_cw = _unpack(v_s, shapes)

    loss = lax.psum(loss_sum[0, 0], ("x", "y", "c"))
    lead = lambda a: a[None]
    grads = [lead(g_w_in), g_conv, g_alog, g_dt, g_gnw, g_lb, g_hnw, lead(g_w_out), g_mix, g_ffn,
             lead(g_w_ff1), lead(g_w_ff2), g_final]
    deltas = [lead(d_w_in), d_cw, d_alog, d_dt, d_gn, d_lbl, d_hn, lead(d_w_out), d_nm, d_nf,
              lead(d_w_ff1), lead(d_w_ff2), d_nfin]
    new_m = [lead(nm_w_in), m_cw, m_alog, m_dt, m_gn, m_lbl, m_hn, lead(nm_w_out), m_nm, m_nf,
             lead(nm_w_ff1), lead(nm_w_ff2), m_nfin]
    new_v = [lead(nv_w_in), v_cw, v_alog, v_dt, v_gn, v_lbl, v_hn, lead(nv_w_out), v_nm, v_nf,
             lead(nv_w_ff1), lead(nv_w_ff2), v_nfin]
    return (loss, dx[None], *grads, *deltas, *new_m, *new_v)
```

```python
import functools

import jax
import jax.numpy as jnp
from jax import lax
from jax.experimental import pallas as pl
from jax.experimental.pallas import tpu as pltpu

F32 = jnp.float32
BF16 = jnp.bfloat16
HI = lax.Precision.HIGHEST

N_DEV = 8
D_MODEL = 2048
CHUNK = 64
SUB_CHUNK = 16
HEAD_DIM = 128
N_HEADS = 8
GDN_WIDTH = N_HEADS * HEAD_DIM
D_FF = 4 * D_MODEL
QKV_WIDTH = 3 * GDN_WIDTH
MAIN_WIDTH = 8 * GDN_WIDTH
CAT_WIDTH = MAIN_WIDTH + 128
AB_BLOCK = MAIN_WIDTH // 128
NORM_EPS = 1e-6
L2_EPS = 1e-6
LANES = 128
VMEM_LIMIT = 56 * 1024 * 1024

ADAM_LR = 0.001
ADAM_B1 = 0.9
ADAM_B2 = 0.999
ADAM_EPS = 1e-08
ADAM_WD = 0.01
ADAM_STEP = 10

MESH = pl.DeviceIdType.MESH


def _params(sem=None):
    return pltpu.CompilerParams(dimension_semantics=sem, vmem_limit_bytes=VMEM_LIMIT)


def _dot(a, b, dims, prec=None):
    return lax.dot_general(a, b, (dims, ((), ())), precision=prec, preferred_element_type=F32)


NN = ((1,), (0,))
NT = ((1,), (1,))
TN = ((0,), (0,))


def _split_bf16(x, pieces):
    out = []
    for _ in range(pieces - 1):
        p = x.astype(BF16)
        out.append(p)
        x = x - p.astype(F32)
    out.append(x.astype(BF16))
    return out


def _mm_raw(a, b, dims, prec):
    if prec == "hi":
        return _dot(a, b, dims, HI)
    if prec == "bf":
        return _dot(a.astype(BF16), b.astype(BF16), dims)
    a_hi, a_lo = _split_bf16(a, 2)
    b_hi, b_lo = _split_bf16(b, 2)
    return _dot(a_hi, b_hi, dims) + (_dot(a_hi, b_lo, dims) + _dot(a_lo, b_hi, dims))


@functools.partial(jax.custom_vjp, nondiff_argnums=(2, 3))
def mm(a, b, dims, prec):
    return _mm_raw(a, b, dims, prec)


def _mm_fwd(a, b, dims, prec):
    return _mm_raw(a, b, dims, prec), (a, b)


def _mm_bwd(dims, prec, res, ct):
    a, b = res
    if dims == NN:
        return _mm_raw(ct, b, NT, prec), _mm_raw(a, ct, TN, prec)
    if dims == NT:
        return _mm_raw(ct, b, NN, prec), _mm_raw(ct, a, TN, prec)
    return _mm_raw(b, ct, NT, prec), _mm_raw(a, ct, NN, prec)


mm.defvjp(_mm_fwd, _mm_bwd)


def _sel_raw(sel, x, dims):
    sel = sel.astype(BF16)
    p0, p1, p2 = _split_bf16(x, 3)
    return _dot(sel, p0, dims) + (_dot(sel, p1, dims) + _dot(sel, p2, dims))


def _sel_parts(sel, x):
    c = x.shape[0]
    full = _sel_raw(sel, x, NN)
    return tuple(full[i * c:(i + 1) * c] for i in range(sel.shape[0] // c))


@jax.custom_vjp
def sel_sums(sel, x):
    return _sel_parts(sel, x)


def _sel_fwd(sel, x):
    return _sel_parts(sel, x), sel


def _sel_bwd(sel, cts):
    return jnp.zeros_like(sel), _sel_raw(sel, jnp.concatenate(cts, axis=0), TN)


sel_sums.defvjp(_sel_fwd, _sel_bwd)


@jax.custom_vjp
def _known_value(computed, known):
    del computed
    return known


_known_value.defvjp(lambda computed, known: (known, None), lambda _, ct: (ct, jnp.zeros_like(ct)))


def _my_flat():
    return 4 * lax.axis_index("x") + 2 * lax.axis_index("y") + lax.axis_index("c")


def _peer(k):
    x, y, c = lax.axis_index("x"), lax.axis_index("y"), lax.axis_index("c")
    kx, ky, kc = (k >> 2) & 1, (k >> 1) & 1, k & 1
    px = (1 - x) if kx else x
    py = (1 - y) if ky else y
    pc = (1 - c) if kc else c
    return (px, py, pc), 4 * px + 2 * py + pc


def gather_two_level(xs, name):
    n = len(xs)

    def body(*refs):
        x_refs, y_refs = refs[:n], refs[n:2 * n]
        send_sems, recv_sems, local_sems = refs[2 * n:]
        x, y, c = lax.axis_index("x"), lax.axis_index("y"), lax.axis_index("c")
        me, sibling = (x, y, c), (x, y, 1 - c)
        chips = [(1 - x, y), (x, 1 - y), (1 - x, 1 - y)]
        flat = lambda p: 4 * p[0] + 2 * p[1] + p[2]

        def copy(a, k, block, to, src=None):
            return pltpu.make_async_remote_copy(
                src_ref=y_refs[a].at[flat(block)] if src is None else src, dst_ref=y_refs[a].at[flat(block)],
                send_sem=send_sems.at[a, k], recv_sem=recv_sems.at[a, k], device_id=to, device_id_type=MESH)

        mine = [pltpu.make_async_copy(x_refs[a], y_refs[a].at[flat(me)], local_sems.at[a]) for a in range(n)]
        for cp in mine:
            cp.start()
        first = [copy(a, 0, me, sibling, src=x_refs[a]) for a in range(n)]
        first += [copy(a, 1 + j, me, (*chip, c), src=x_refs[a]) for j, chip in enumerate(chips) for a in range(n)]
        for cp in first:
            cp.start()
        passed = []
        for j, chip in enumerate(chips):
            for a in range(n):
                copy(a, 1 + j, (*chip, c), me).wait_recv()
                cp = copy(a, 4 + j, (*chip, c), sibling)
                cp.start()
                passed.append(cp)
        for a in range(n):
            copy(a, 0, sibling, me).wait_recv()
        for j, chip in enumerate(chips):
            for a in range(n):
                copy(a, 4 + j, (*chip, 1 - c), me).wait_recv()
        for cp in first + passed:
            cp.wait_send()
        for cp in mine:
            cp.wait()

    any_spec = pl.BlockSpec(memory_space=pl.ANY)
    return pl.pallas_call(
        body, name=name, out_shape=[jax.ShapeDtypeStruct((N_DEV,) + x.shape, x.dtype) for x in xs],
        in_specs=[any_spec] * n, out_specs=[any_spec] * n,
        scratch_shapes=[pltpu.SemaphoreType.DMA((n, N_DEV - 1)), pltpu.SemaphoreType.DMA((n, N_DEV - 1)),
                        pltpu.SemaphoreType.DMA((n,))],
    )(*xs)


HBM_SPEC = pl.BlockSpec(memory_space=pltpu.HBM)
SEM_SPEC = pl.BlockSpec(memory_space=pltpu.SEMAPHORE)
ANY_SPEC = pl.BlockSpec(memory_space=pl.ANY)
DATAFLOW = pltpu.SideEffectType.DATAFLOW_SIDE_EFFECTING


def _in_hbm(x):
    return pltpu.with_memory_space_constraint(x, pltpu.HBM)


ALL_PEERS = tuple(range(1, N_DEV))
CHIP_PEERS = (1, 2, 4, 6)
OTHER_CHIPS = (2, 4, 6)


def exchange_start(xs, gather, name, after=(), peers=ALL_PEERS):
    n, n_after = len(xs), len(after)

    def body(*refs):
        x_refs, land_refs = refs[:n], refs[n:2 * n]
        sems = refs[2 * n + n_after:2 * n + n_after + 2 * n]
        token = refs[-1]
        me = _my_flat()
        for k in peers:
            peer, peer_flat = _peer(k)
            for a in range(n):
                src = x_refs[a] if gather else x_refs[a].at[peer_flat]
                pltpu.make_async_remote_copy(src_ref=src, dst_ref=land_refs[a].at[me], send_sem=sems[a],
                                             recv_sem=sems[n + a], device_id=peer, device_id_type=MESH).start()
        token[...] = jnp.zeros_like(token)

    lands = [_in_hbm(lax.empty(((N_DEV,) + x.shape) if gather else x.shape, x.dtype)) for x in xs]
    hbm_out = [pltpu.HBM(x.shape, x.dtype) for x in xs] + [pltpu.HBM(l.shape, l.dtype) for l in lands]
    res = pl.pallas_call(
        body, name=name,
        out_shape=(*([pltpu.SemaphoreType.DMA(())] * (2 * n)), *hbm_out, jax.ShapeDtypeStruct((8, LANES), F32)),
        in_specs=[HBM_SPEC] * (2 * n) + [ANY_SPEC] * n_after,
        out_specs=(*([SEM_SPEC] * (2 * n)), *([HBM_SPEC] * (2 * n)), pl.BlockSpec(memory_space=pltpu.VMEM)),
        input_output_aliases={i: 2 * n + i for i in range(2 * n)},
        compiler_params=pltpu.CompilerParams(has_side_effects=DATAFLOW),
    )(*[_in_hbm(x) for x in xs], *lands, *after)
    return (list(res[:2 * n]), list(res[2 * n:3 * n]), list(res[3 * n:4 * n])), res[-1]


def forward_start(lands, name, after=()):
    n, n_after = len(lands), len(after)

    def body(*refs):
        land_refs = refs[:n]
        sems = refs[n + n_after:n + n_after + 2 * n]
        token = refs[-1]
        sibling, _ = _peer(1)
        for a in range(n):
            for k in OTHER_CHIPS:
                _, from_flat = _peer(k)
                slot = land_refs[a].at[from_flat]
                pltpu.make_async_remote_copy(src_ref=slot, dst_ref=slot, send_sem=sems[a], recv_sem=sems[n + a],
                                             device_id=sibling, device_id_type=MESH).start()
        token[...] = jnp.zeros_like(token)

    res = pl.pallas_call(
        body, name=name,
        out_shape=(*([pltpu.SemaphoreType.DMA(())] * (2 * n)), *[pltpu.HBM(l.shape, l.dtype) for l in lands],
                   jax.ShapeDtypeStruct((8, LANES), F32)),
        in_specs=[HBM_SPEC] * n + [ANY_SPEC] * n_after,
        out_specs=(*([SEM_SPEC] * (2 * n)), *([HBM_SPEC] * n), pl.BlockSpec(memory_space=pltpu.VMEM)),
        input_output_aliases={i: 2 * n + i for i in range(n)},
        compiler_params=pltpu.CompilerParams(has_side_effects=DATAFLOW),
    )(*lands, *after)
    return (list(res[:2 * n]), [], list(res[2 * n:3 * n])), res[-1]


def exchange_wait(handle, name, after=(), copies=N_DEV - 1):
    sems, xs, lands = handle
    n, n_x, n_after = len(lands), len(xs), len(after)

    def body(*refs):
        land_refs = refs[n_x:n_x + n]
        sem_refs = refs[n_x + n:n_x + 3 * n]
        for a in range(n):
            every = land_refs[a].at[pl.ds(0, copies)]
            cp = pltpu.make_async_remote_copy(src_ref=every, dst_ref=every, send_sem=sem_refs[a],
                                              recv_sem=sem_refs[n + a], device_id=_peer(1)[0], device_id_type=MESH)
            cp.wait_send()
            cp.wait_recv()

    res = pl.pallas_call(
        body, name=name,
        out_shape=[pltpu.HBM(x.shape, x.dtype) for x in xs] + [pltpu.HBM(l.shape, l.dtype) for l in lands],
        in_specs=[HBM_SPEC] * (n_x + n) + [SEM_SPEC] * (2 * n) + [ANY_SPEC] * n_after,
        out_specs=[HBM_SPEC] * (n_x + n),
        input_output_aliases={i: i for i in range(n_x + n)},
        compiler_params=pltpu.CompilerParams(has_side_effects=DATAFLOW),
    )(*xs, *lands, *sems, *after)
    return list(res[:n_x]), list(res[n_x:])


N_CHIPS = N_DEV // 2


def routed_start(x, routes, name, after=()):
    n_after = len(after)

    def body(*refs):
        x_ref, land_ref = refs[0], refs[1]
        send_sem, recv_sem = refs[2 + n_after], refs[3 + n_after]
        token = refs[-1]
        for src, dst, peer in routes():
            pltpu.make_async_remote_copy(src_ref=x_ref.at[src], dst_ref=land_ref.at[dst], send_sem=send_sem,
                                         recv_sem=recv_sem, device_id=peer, device_id_type=MESH).start()
        token[...] = jnp.zeros_like(token)

    land = _in_hbm(lax.empty((N_CHIPS,) + x.shape[1:], x.dtype))
    res = pl.pallas_call(
        body, name=name,
        out_shape=(pltpu.SemaphoreType.DMA(()), pltpu.SemaphoreType.DMA(()), pltpu.HBM(x.shape, x.dtype),
                   pltpu.HBM(land.shape, land.dtype), jax.ShapeDtypeStruct((8, LANES), F32)),
        in_specs=[HBM_SPEC, HBM_SPEC] + [ANY_SPEC] * n_after,
        out_specs=(SEM_SPEC, SEM_SPEC, HBM_SPEC, HBM_SPEC, pl.BlockSpec(memory_space=pltpu.VMEM)),
        input_output_aliases={0: 2, 1: 3},
        compiler_params=pltpu.CompilerParams(has_side_effects=DATAFLOW),
    )(_in_hbm(x), land, *after)
    return ([res[0], res[1]], [res[2]], [res[3]]), res[-1]


def _to_sibling_routes():
    c = lax.axis_index("c")
    sibling, _ = _peer(1)
    return [(2 * chip + 1 - c, chip, sibling) for chip in range(N_CHIPS)]


def _to_chips_routes():
    my_chip = _my_flat() // 2
    routes = []
    for k in OTHER_CHIPS:
        peer, peer_flat = _peer(k)
        routes.append((peer_flat // 2, my_chip, peer))
    return routes


def pair_sum(p, from_sibling, name, rb=256):
    _, r, c = p.shape
    mine = lax.axis_index("c").astype(jnp.int32).reshape(1)

    def body(kind_ref, p_ref, s_ref, o_ref):
        del kind_ref
        o_ref[...] = (p_ref[...].astype(F32) + s_ref[...].astype(F32)).astype(BF16)

    return pl.pallas_call(
        body, name=name,
        grid_spec=pltpu.PrefetchScalarGridSpec(
            num_scalar_prefetch=1, grid=(N_CHIPS, r // rb),
            in_specs=[pl.BlockSpec((None, None, rb, c), lambda chip, i, kind: (chip, kind[0], i, 0)),
                      pl.BlockSpec((None, rb, c), lambda chip, i, kind: (chip, i, 0))],
            out_specs=pl.BlockSpec((None, rb, c), lambda chip, i, kind: (chip, i, 0))),
        out_shape=jax.ShapeDtypeStruct((N_CHIPS, r, c), BF16),
        compiler_params=_params(("parallel", "parallel")))(mine, p.reshape(N_CHIPS, 2, r, c), from_sibling)


def _one(handle, a):
    sems, xs, lands = handle
    n = len(lands)
    return [sems[a], sems[n + a]], xs[a:a + 1], [lands[a]]


def _own_slot(land, block):
    return lax.dynamic_update_slice(land, block[None], (_my_flat(),) + (0,) * block.ndim)


def allreduce_small(x, name):
    rows = x.shape[0]

    def body(x_ref, o_ref, buf, send_sems, recv_sems):
        me = _my_flat()
        buf[me] = x_ref[...]
        sends = []
        for k in range(1, N_DEV):
            peer, _ = _peer(k)
            cp = pltpu.make_async_remote_copy(
                src_ref=x_ref, dst_ref=buf.at[me], send_sem=send_sems.at[k], recv_sem=recv_sems.at[k],
                device_id=peer, device_id_type=MESH)
            cp.start()
            sends.append(cp)
        for k in range(1, N_DEV):
            peer, peer_flat = _peer(k)
            pltpu.make_async_remote_copy(
                src_ref=x_ref, dst_ref=buf.at[peer_flat], send_sem=send_sems.at[k], recv_sem=recv_sems.at[k],
                device_id=peer, device_id_type=MESH).wait_recv()
        for cp in sends:
            cp.wait_send()
        acc = buf[0]
        for d in range(1, N_DEV):
            acc = acc + buf[d]
        o_ref[...] = acc

    vmem = pl.BlockSpec(memory_space=pltpu.VMEM)
    return pl.pallas_call(
        body, name=name, out_shape=jax.ShapeDtypeStruct((rows, LANES), F32),
        in_specs=[vmem], out_specs=vmem,
        scratch_shapes=[pltpu.VMEM((N_DEV, rows, LANES), F32),
                        pltpu.SemaphoreType.DMA((N_DEV,)), pltpu.SemaphoreType.DMA((N_DEV,))],
    )(x)


def matmul(a, b, mode, name, out_dtypes=(F32,), epilogue=None, extra=None, tm=1024, tn=1024, tk=2048, after=(),
           b_shards=False, out_shards=False, k_group=1, k_blocks=None):
    if b_shards:
        n_sh, b_rows, b_cols = b.shape
    if mode == "nn":
        (m, kd), n = a.shape, (n_sh * b_cols if b_shards else b.shape[1])
        if b_shards:
            tn = b_cols
    elif mode == "nt":
        (m, kd), n = a.shape, (b_rows if b_shards else b.shape[0])
        if b_shards:
            tk = k_group * b_cols
    else:
        (kd, m), n = a.shape, b.shape[1]
    tm, tn, tk = min(tm, m), min(tn, n), min(tk, kd)
    assert m % tm == 0 and n % tn == 0 and kd % tk == 0, (name, m, n, kd, tm, tn, tk)
    k0, ksteps = (0, kd // tk) if k_blocks is None else k_blocks
    dims = {"nn": NN, "nt": NT, "tn": TN}[mode]
    n_out = len(out_dtypes)
    n_in = 2 + (extra is not None) + len(after)

    def finish(acc, e_ref, o_refs):
        outs = (acc,) if epilogue is None else epilogue(acc, e_ref[...] if e_ref is not None else None)
        for o_ref, o in zip(o_refs, outs):
            o_ref[...] = o.astype(o_ref.dtype)

    def product(a_ref, b_ref):
        if mode == "nt" and b_shards:
            w = b_cols
            parts = [_dot(a_ref[:, s * w:(s + 1) * w], b_ref[s], dims) for s in range(k_group)]
            return functools.reduce(lambda p, q: p + q, parts)
        return _dot(a_ref[...], b_ref[...], dims)

    def body(*refs):
        a_ref, b_ref = refs[0], refs[1]
        e_ref = refs[2] if extra is not None else None
        o_refs = refs[n_in:n_in + n_out]
        if ksteps == 1:
            finish(product(a_ref, b_ref), e_ref, o_refs)
            return
        acc_ref = refs[-1]
        kk = pl.program_id(2)

        @pl.when(kk == 0)
        def _():
            acc_ref[...] = jnp.zeros_like(acc_ref)

        acc_ref[...] += product(a_ref, b_ref)

        @pl.when(kk == ksteps - 1)
        def _():
            finish(acc_ref[...], e_ref, o_refs)

    if mode == "nn":
        a_spec = pl.BlockSpec((tm, tk), lambda i, j, k: (i, k0 + k))
        b_spec = (pl.BlockSpec((None, tk, tn), lambda i, j, k: (j, k, 0)) if b_shards
                  else pl.BlockSpec((tk, tn), lambda i, j, k: (k0 + k, j)))
    elif mode == "nt":
        a_spec = pl.BlockSpec((tm, tk), lambda i, j, k: (i, k))
        b_spec = (pl.BlockSpec((k_group, tn, b_cols), lambda i, j, k: (k, j, 0)) if b_shards
                  else pl.BlockSpec((tn, tk), lambda i, j, k: (j, k)))
    else:
        a_spec = pl.BlockSpec((tk, tm), lambda i, j, k: (k, i))
        b_spec = pl.BlockSpec((tk, tn), lambda i, j, k: (k, j))
    o_spec = pl.BlockSpec((tm, tn), lambda i, j, k: (i, j))
    res_spec = pl.BlockSpec((None, tm, tn), lambda i, j, k: (j, i, 0)) if out_shards else o_spec
    res_shape = (n // tn, m, tn) if out_shards else (m, n)
    in_specs = [a_spec, b_spec] + ([o_spec] if extra is not None else []) + [ANY_SPEC] * len(after)
    args = (a, b) + ((extra,) if extra is not None else ()) + tuple(after)
    res = pl.pallas_call(
        body, name=name, grid=(m // tm, n // tn, ksteps),
        in_specs=in_specs, out_specs=[res_spec] * n_out,
        out_shape=[jax.ShapeDtypeStruct(res_shape, dt) for dt in out_dtypes],
        scratch_shapes=[pltpu.VMEM((tm, tn), F32)] if ksteps > 1 else [],
        compiler_params=_params(("parallel", "parallel", "arbitrary")),
    )(*args)
    return res if n_out > 1 else res[0]


GATE_COL = 4 * GDN_WIDTH
RELAYOUT_ROWS = 256


def _cat_of_win(j):
    if j < GATE_COL:
        return j
    if j < GATE_COL + 2 * N_HEADS:
        return MAIN_WIDTH + (j - GATE_COL)
    return j - 2 * N_HEADS


def _win_of_cat(c):
    if c < GATE_COL:
        return c
    if c < MAIN_WIDTH:
        return c + 2 * N_HEADS
    if c < MAIN_WIDTH + 2 * N_HEADS:
        return GATE_COL + (c - MAIN_WIDTH)
    return None


def _runs(first, count, mapping):
    runs, i = [], 0
    while i < count:
        start, n = mapping(first + i), 1
        while i + n < count and mapping(first + i + n) == start + n:
            n += 1
        runs.append((start, n))
        i += n
    return runs


def weights_to_cat(g_in, name, total_rows, row0=0, into=None):
    n_dev, rows, shard = g_in.shape
    first = row0 // RELAYOUT_ROWS

    def body(x_ref, *rest):
        o_ref = rest[-1]
        for b in range(CAT_WIDTH // LANES):
            live = sum(_win_of_cat(LANES * b + i) is not None for i in range(LANES))
            parts = []
            for start, n in _runs(LANES * b, live, _win_of_cat):
                while n > 0:
                    d, o = divmod(start, shard)
                    take = min(n, shard - o)
                    parts.append(x_ref[d, :, o:o + take])
                    start, n = start + take, n - take
            if live < LANES:
                parts.append(jnp.zeros((RELAYOUT_ROWS, LANES - live), g_in.dtype))
            o_ref[:, LANES * b:LANES * (b + 1)] = parts[0] if len(parts) == 1 else jnp.concatenate(parts, axis=1)

    return pl.pallas_call(
        body, name=name, grid=(rows // RELAYOUT_ROWS,),
        in_specs=[pl.BlockSpec((n_dev, RELAYOUT_ROWS, shard), lambda i: (0, i, 0))] + ([ANY_SPEC] if into is not None else []),
        out_specs=pl.BlockSpec((RELAYOUT_ROWS, CAT_WIDTH), lambda i: (first + i, 0)),
        out_shape=jax.ShapeDtypeStruct((total_rows, CAT_WIDTH), g_in.dtype),
        input_output_aliases={1: 0} if into is not None else {},
        compiler_params=_params(("parallel",)))(*((g_in,) if into is None else (g_in, into)))


def cat_to_shards(dw_cat, shard):
    rows = dw_cat.shape[0]

    def body(x_ref, o_ref):
        for d in range(N_DEV):
            for t0 in range(0, shard, LANES):
                width = min(LANES, shard - t0)
                parts = [x_ref[:, c:c + n] for c, n in _runs(d * shard + t0, width, _cat_of_win)]
                o_ref[d, :, t0:t0 + width] = parts[0] if len(parts) == 1 else jnp.concatenate(parts, axis=1)

    return pl.pallas_call(
        body, name="cat_to_shards", grid=(rows // RELAYOUT_ROWS,),
        in_specs=[pl.BlockSpec((RELAYOUT_ROWS, CAT_WIDTH), lambda i: (i, 0))],
        out_specs=pl.BlockSpec((N_DEV, RELAYOUT_ROWS, shard), lambda i: (0, i, 0)),
        out_shape=jax.ShapeDtypeStruct((N_DEV, rows, shard), dw_cat.dtype),
        compiler_params=_params(("parallel",)))(dw_cat)


ROW_BLOCK = 512


def rms_fwd(x, w, name):
    t, d = x.shape

    def body(x_ref, w_ref, n_ref, r_ref):
        h = x_ref[...]
        r = lax.rsqrt(jnp.mean(h * h, axis=-1, keepdims=True) + NORM_EPS)
        n_ref[...] = (h * r * w_ref[...]).astype(BF16)
        r_ref[...] = r

    row = pl.BlockSpec((ROW_BLOCK, d), lambda i: (i, 0))
    return pl.pallas_call(
        body, name=name, grid=(t // ROW_BLOCK,),
        in_specs=[row, pl.BlockSpec((1, d), lambda i: (0, 0))],
        out_specs=[row, pl.BlockSpec((ROW_BLOCK, 1), lambda i: (i, 0))],
        out_shape=[jax.ShapeDtypeStruct((t, d), BF16), jax.ShapeDtypeStruct((t, 1), F32)],
        compiler_params=_params(("parallel",)))(x, w)


FUSED_ROWS = 512


def out_proj_rms(y, w_out, x, w_norm, name):
    t, d = x.shape

    def body(y_ref, w_ref, x_ref, g_ref, h_ref, n_ref, r_ref):
        h = x_ref[...] + _dot(y_ref[...], w_ref[...], NN)
        r = lax.rsqrt(jnp.mean(h * h, axis=-1, keepdims=True) + NORM_EPS)
        h_ref[...] = h
        n_ref[...] = (h * r * g_ref[...]).astype(BF16)
        r_ref[...] = r

    row = pl.BlockSpec((FUSED_ROWS, d), lambda i: (i, 0))
    return pl.pallas_call(
        body, name=name, grid=(t // FUSED_ROWS,),
        in_specs=[pl.BlockSpec((FUSED_ROWS, y.shape[1]), lambda i: (i, 0)), pl.BlockSpec(w_out.shape, lambda i: (0, 0)),
                  row, pl.BlockSpec((1, d), lambda i: (0, 0))],
        out_specs=[row, row, pl.BlockSpec((FUSED_ROWS, 1), lambda i: (i, 0))],
        out_shape=[jax.ShapeDtypeStruct((t, d), F32), jax.ShapeDtypeStruct((t, d), BF16),
                   jax.ShapeDtypeStruct((t, 1), F32)],
        compiler_params=_params(("parallel",)))(y, w_out, x, w_norm)


def ff2_loss(act, w_ff2, h1, w, target, name, tk=2048):
    t, d = h1.shape
    ksteps = act.shape[1] // tk

    def body(a_ref, b_ref, h_ref, w_ref, t_ref, loss_ref, dhb_ref, dw_ref, acc_ref):
        i, kk = pl.program_id(0), pl.program_id(1)

        @pl.when((i == 0) & (kk == 0))
        def _():
            loss_ref[...] = jnp.zeros_like(loss_ref)
            dw_ref[...] = jnp.zeros_like(dw_ref)

        @pl.when(kk == 0)
        def _():
            acc_ref[...] = h_ref[...]

        acc_ref[...] += _dot(a_ref[...], b_ref[...], NN)

        @pl.when(kk == ksteps - 1)
        def _():
            h = acc_ref[...]
            wv = w_ref[...]
            r = lax.rsqrt(jnp.mean(h * h, axis=-1, keepdims=True) + NORM_EPS)
            yn = h * r
            e = yn * wv - t_ref[...]
            loss_ref[...] += 0.5 * jnp.sum(jnp.sum(e * e, axis=-1, keepdims=True), axis=0, keepdims=True) / d
            dy = e / d
            dw_ref[...] += jnp.sum(dy * yn, axis=0, keepdims=True)
            dyn = dy * wv
            dhb_ref[...] = (r * (dyn - yn * jnp.mean(dyn * yn, axis=-1, keepdims=True))).astype(BF16)

    row = pl.BlockSpec((FUSED_ROWS, d), lambda i, k: (i, 0))
    wspec = pl.BlockSpec((1, d), lambda i, k: (0, 0))
    return pl.pallas_call(
        body, name=name, grid=(t // FUSED_ROWS, ksteps),
        in_specs=[pl.BlockSpec((FUSED_ROWS, tk), lambda i, k: (i, k)), pl.BlockSpec((tk, d), lambda i, k: (k, 0)),
                  row, wspec, row],
        out_specs=[pl.BlockSpec((1, 1), lambda i, k: (0, 0)), row, wspec],
        out_shape=[jax.ShapeDtypeStruct((1, 1), F32), jax.ShapeDtypeStruct((t, d), BF16),
                   jax.ShapeDtypeStruct((1, d), F32)],
        scratch_shapes=[pltpu.VMEM((FUSED_ROWS, d), F32)],
        compiler_params=_params(("arbitrary", "arbitrary")))(act, w_ff2, h1, w, target)


def rms_bwd(h, r, w, dn, dres, out_dtype, name):
    t, d = h.shape

    def body(h_ref, r_ref, w_ref, dn_ref, dres_ref, dh_ref, dw_ref):
        @pl.when(pl.program_id(0) == 0)
        def _():
            dw_ref[...] = jnp.zeros_like(dw_ref)

        rv = r_ref[...]
        yn = h_ref[...] * rv
        dnv = dn_ref[...].astype(F32)
        dw_ref[...] += jnp.sum(dnv * yn, axis=0, keepdims=True)
        dyn = dnv * w_ref[...]
        dh = dres_ref[...].astype(F32) + rv * (dyn - yn * jnp.mean(dyn * yn, axis=-1, keepdims=True))
        dh_ref[...] = dh.astype(out_dtype)

    row = pl.BlockSpec((ROW_BLOCK, d), lambda i: (i, 0))
    wspec = pl.BlockSpec((1, d), lambda i: (0, 0))
    rspec = pl.BlockSpec((ROW_BLOCK, 1), lambda i: (i, 0))
    return pl.pallas_call(
        body, name=name, grid=(t // ROW_BLOCK,),
        in_specs=[row, rspec, wspec, row, row], out_specs=[row, wspec],
        out_shape=[jax.ShapeDtypeStruct((t, d), out_dtype), jax.ShapeDtypeStruct((1, d), F32)],
        compiler_params=_params(("arbitrary",)))(h, r, w, dn, dres)


CONV_ROWS = 512
TILE_ROWS = 8


def _iota2(shape, axis):
    return lax.broadcasted_iota(jnp.int32, shape, axis)


def _silu(x):
    return x * jax.nn.sigmoid(x)


def _conv_rows(x_ref, w, first, rows):
    acc = None
    for j in range(4):
        term = x_ref[first - 3 + j:first - 3 + j + rows, :] * w[j:j + 1, :]
        acc = term if acc is None else acc + term
    return acc


def _head_shifts(head):
    rows = _iota2((TILE_ROWS, 1), 0)
    return [jnp.where(rows >= 3 - j, head if j == 3 else pltpu.roll(head, 3 - j, 0), 0.0) for j in range(4)]


def _conv_chunks(t):
    pieces = [(TILE_ROWS, min(CONV_ROWS, t) - TILE_ROWS)]
    pieces += [(r, CONV_ROWS) for r in range(CONV_ROWS, t, CONV_ROWS)]
    return pieces


def conv_fwd(proj, conv_w, name):
    t = proj.shape[0]

    def body(x_ref, w_ref, o_ref):
        w = w_ref[...]
        shifted = _head_shifts(x_ref[0:TILE_ROWS, :])
        o_ref[0:TILE_ROWS, :] = _silu(sum(shifted[j] * w[j:j + 1, :] for j in range(4)))
        for first, rows in _conv_chunks(t):
            o_ref[first:first + rows, :] = _silu(_conv_rows(x_ref, w, first, rows))

    col = pl.BlockSpec((t, LANES), lambda c: (0, c))
    return pl.pallas_call(
        body, name=name, grid=(QKV_WIDTH // LANES,),
        in_specs=[col, pl.BlockSpec((4, LANES), lambda c: (0, c))], out_specs=col,
        out_shape=jax.ShapeDtypeStruct((t, QKV_WIDTH), F32),
        compiler_params=_params(("parallel",)))(proj, conv_w)


def conv_bwd(proj, dout, conv_w, dproj, name):
    t = proj.shape[0]

    def dsilu(pre):
        sg = jax.nn.sigmoid(pre)
        return sg * (1.0 + pre * (1.0 - sg))

    def body(x_ref, d_ref, w_ref, dproj_in, dx_ref, dw_ref, stage):
        del dproj_in
        w = w_ref[...]
        shifted = _head_shifts(x_ref[0:TILE_ROWS, :])
        head_dpre = d_ref[0:TILE_ROWS, :] * dsilu(sum(shifted[j] * w[j:j + 1, :] for j in range(4)))
        stage[0:TILE_ROWS, :] = head_dpre
        for first, rows in _conv_chunks(t):
            stage[first:first + rows, :] = d_ref[first:first + rows, :] * dsilu(_conv_rows(x_ref, w, first, rows))
        stage[t:t + TILE_ROWS, :] = jnp.zeros((TILE_ROWS, LANES), F32)
        for first, rows in [(0, TILE_ROWS)] + _conv_chunks(t):
            dx = None
            for j in range(4):
                term = stage[first + 3 - j:first + 3 - j + rows, :] * w[j:j + 1, :]
                dx = term if dx is None else dx + term
            dx_ref[first:first + rows, :] = dx.astype(BF16)
        dw = [jnp.sum(head_dpre * shifted[j], axis=0, keepdims=True) for j in range(4)]
        for first, rows in _conv_chunks(t):
            dpre = stage[first:first + rows, :]
            for j in range(4):
                dw[j] = dw[j] + jnp.sum(dpre * x_ref[first - 3 + j:first - 3 + j + rows, :], axis=0, keepdims=True)
        dw_ref[...] = jnp.concatenate(dw, axis=0)

    col = pl.BlockSpec((t, LANES), lambda c: (0, c))
    taps = pl.BlockSpec((4, LANES), lambda c: (0, c))
    return pl.pallas_call(
        body, name=name, grid=(QKV_WIDTH // LANES,),
        in_specs=[col, col, taps, ANY_SPEC], out_specs=[col, taps],
        out_shape=[jax.ShapeDtypeStruct(dproj.shape, BF16), jax.ShapeDtypeStruct((4, QKV_WIDTH), F32)],
        scratch_shapes=[pltpu.VMEM((t + TILE_ROWS, LANES), F32)],
        input_output_aliases={3: 0},
        compiler_params=_params(("parallel",)))(proj, dout, conv_w, dproj)


def _softplus(x):
    return jnp.maximum(x, 0.0) + jnp.log(1.0 + jnp.exp(-jnp.abs(x)))


def _head_norm_gate(o, norm_w, gate):
    return o * lax.rsqrt(jnp.mean(o * o, axis=-1, keepdims=True) + NORM_EPS) * norm_w * _silu(gate)


GDN_PREC = ("bf", "bf")
HGRN_PREC = "bf"


def _each(fn, *cols):
    return [fn(*a) for a in zip(*cols)]


@functools.partial(jax.custom_vjp, nondiff_argnums=(2,))
def _known_inverse(low, inv, prec):
    del low, prec
    return inv


def _known_inverse_fwd(low, inv, prec):
    del low
    return inv, inv


def _known_inverse_bwd(prec, inv, ct):
    return -_mm_raw(_mm_raw(inv, ct, TN, prec), inv, NT, prec), jnp.zeros_like(inv)


_known_inverse.defvjp(_known_inverse_fwd, _known_inverse_bwd)


def gdn_stages(hs, qc, kc, vc, zc, ab, a_log_l, dt_l, norm_w, s, prec=GDN_PREC, inv_known=None):
    p_inv, p_mm = prec
    c = CHUNK
    ri, ci = _iota2((c, c), 0), _iota2((c, c), 1)
    incl, strict, eye = ri >= ci, ri > ci, ri == ci
    lane = _iota2((c, LANES), 1)
    last_row = _iota2((c, 1), 0) == c - 1
    rowsum = lambda x: jnp.sum(x, axis=1, keepdims=True)

    def row(col):
        return jnp.sum(jnp.where(eye, col, 0.0), axis=0, keepdims=True)

    q = _each(lambda x: x * lax.rsqrt(rowsum(x * x) + L2_EPS) * (HEAD_DIM ** -0.5), qc)
    k = _each(lambda x: x * lax.rsqrt(rowsum(x * x) + L2_EPS), kc)
    yield
    a_col = [rowsum(jnp.where(lane == h, ab, 0.0)) for h in hs]
    b_col = [rowsum(jnp.where(lane == h + N_HEADS, ab, 0.0)) for h in hs]
    beta = _each(jax.nn.sigmoid, b_col)
    g = _each(lambda a, al, dl: rowsum(jnp.where(lane == 0, -jnp.exp(al) * _softplus(a + dl), 0.0)), a_col, a_log_l, dt_l)
    gcum = _each(lambda x: rowsum(jnp.where(incl, row(x), 0.0)), g)
    g_last = _each(lambda x: jnp.sum(jnp.where(last_row, x, 0.0), axis=0, keepdims=True), gcum)
    decay = _each(lambda x: jnp.exp(jnp.where(incl, x - row(x), -jnp.inf)), gcum)
    yield
    kk = _each(lambda x: mm(x, x, NT, p_mm), k)
    low = _each(lambda b, x, d: jnp.where(strict, b * x * d, 0.0), beta, kk, decay)
    yield
    if inv_known is None:
        power = _each(lambda x: -x, low)
        inv = _each(lambda x: jnp.where(eye, 1.0, 0.0) + x, power)
        for _ in range(5):
            power = _each(lambda x: mm(x, x, NN, p_inv), power)
            yield
            inv = _each(lambda x, p: x + mm(x, p, NN, p_inv), inv, power)
            yield
    else:
        inv = _each(lambda x, known: _known_inverse(x, known, p_inv), low, inv_known)
    exp_g = _each(jnp.exp, gcum)
    yield
    u_v = _each(lambda i, b, x: mm(i, b * x, NN, p_mm), inv, beta, vc)
    w = _each(lambda i, b, e, x: mm(i, b * e * x, NN, p_mm), inv, beta, exp_g, k)
    yield
    attn = _each(lambda x, y, d: mm(x, y, NT, p_mm) * d, q, k, decay)
    yield
    u = _each(lambda x, y, z: x - mm(y, z, NN, p_mm), u_v, w, s)
    yield
    o = _each(lambda x, e, z: mm(x * e, z, NN, p_mm), q, exp_g, s)
    o = _each(lambda x, a, y: x + mm(a, y, NN, p_mm), o, attn, u)
    yield
    k_end = _each(lambda x, gl, gc: x * jnp.exp(gl - gc), k, g_last, gcum)
    s_new = _each(lambda z, gl, x, y: z * jnp.exp(gl) + mm(x, y, TN, p_mm), s, g_last, k_end, u)
    return (_each(lambda x, z: _head_norm_gate(x, norm_w, z), o, zc), s_new), inv


def gdn_chunk(h, qc, kc, vc, zc, ab, a_log_l, dt_l, norm_w, s, prec=GDN_PREC, reuse_inverse=False):
    args = ([h], [qc], [kc], [vc], [zc], ab, [a_log_l], [dt_l], norm_w, [s], prec)
    if reuse_inverse:
        inv = lax.stop_gradient(gdn_chunks(*args)[1])
        (y, s_new), _ = gdn_chunks(*args, inv_known=inv)
    else:
        (y, s_new), _ = gdn_chunks(*args)
    return y[0], s_new[0]


DIAG_ROWS = SUB_CHUNK // 2
SHIFT_PAD = 8
SHIFT_ROWS = SHIFT_PAD + CHUNK + SHIFT_PAD
SHIFT_WAYS = 4


class RolledRows:
    def down(self, x, which):
        del which
        return [x] + [pltpu.roll(x, off, 0) for off in range(1, DIAG_ROWS)]

    def up_sum(self, parts, which):
        del which
        acc = parts[0]
        for off in range(1, DIAG_ROWS):
            acc = acc + pltpu.roll(parts[off], CHUNK - off, 0)
        return acc


class SlotRows:
    def __init__(self, slots):
        self.slots = slots

    def down(self, x, which):
        self.slots[which, 0, SHIFT_PAD:SHIFT_PAD + CHUNK, :] = x
        return [x] + [self.slots[which, 0, SHIFT_PAD - off:SHIFT_PAD + CHUNK - off, :] for off in range(1, DIAG_ROWS)]

    def up_sum(self, parts, which):
        acc = parts[0]
        for off in range(1, DIAG_ROWS):
            way = 1 + off % (SHIFT_WAYS - 1)
            self.slots[which, way, SHIFT_PAD:SHIFT_PAD + CHUNK, :] = parts[off]
            acc = acc + self.slots[which, way, SHIFT_PAD + off:SHIFT_PAD + CHUNK + off, :]
        return acc


def _sub_block_rows():
    return jnp.bitwise_and(_iota2((CHUNK, 1), 0), DIAG_ROWS - 1)


def _diag_forward(rows, q, key, bc, v):
    rmod = _sub_block_rows()
    k_d, b_d, v_d = rows.down(key, 0), rows.down(bc, 1), rows.down(v, 2)
    o = None
    for off in range(DIAG_ROWS):
        e = jnp.exp(jnp.where(rmod >= off, bc - b_d[off], -jnp.inf))
        term = jnp.sum(q * k_d[off] * e, axis=-1, keepdims=True) * v_d[off]
        o = term if o is None else o + term
    return o


def _diag_backward(rows, q, key, bc, v, do):
    rmod = _sub_block_rows()
    k_d, b_d, v_d = rows.down(key, 0), rows.down(bc, 1), rows.down(v, 2)
    dq = db = None
    dk_parts, db_parts, dv_parts = [], [], []
    for off in range(DIAG_ROWS):
        e = jnp.exp(jnp.where(rmod >= off, bc - b_d[off], -jnp.inf))
        qe = q * e
        a = jnp.sum(qe * k_d[off], axis=-1, keepdims=True)
        da = jnp.sum(do * v_d[off], axis=-1, keepdims=True)
        dv_parts.append(a * do)
        dq_term = (da * e) * k_d[off]
        dk_term = da * qe
        s = dk_term * k_d[off]
        dq = dq_term if dq is None else dq + dq_term
        db = s if db is None else db + s
        dk_parts.append(dk_term)
        db_parts.append(s)
    return dq, rows.up_sum(dk_parts, 0), db - rows.up_sum(db_parts, 1), rows.up_sum(dv_parts, 2)


def diag_part(rows, differentiable=True):
    forward = functools.partial(_diag_forward, rows)
    if not differentiable:
        return forward
    part = jax.custom_vjp(forward)
    part.defvjp(lambda q, key, bc, v: (forward(q, key, bc, v), (q, key, bc, v)),
                lambda res, do: _diag_backward(rows, *res, do))
    return part


def hgrn_stages(qb, fb, ib, gb, l0, l1, norm_w, st, prec=HGRN_PREC, diags=None, o_known=None):
    c = CHUNK
    ri, ci = _iota2((4 * c, c), 0), _iota2((4 * c, c), 1)
    rcol = _iota2((c, 1), 0)
    blk0 = jnp.bitwise_and(ri, c - SUB_CHUNK)
    limit = jnp.where(ri < c, ri + 1, jnp.where(ri < 2 * c, blk0, jnp.where(ri < 3 * c, blk0 + SUB_CHUNK,
                                                                          blk0 + DIAG_ROWS)))
    sel = jnp.where(ci < limit, 1.0, 0.0)
    ri, ci = _iota2((c, c), 0), _iota2((c, c), 1)
    lb = _each(lambda a, b: jax.nn.sigmoid(a - b), l0, l1)
    forget = _each(lambda b, f: b + (1.0 - b) * jax.nn.sigmoid(f), lb, fb)
    key = _each(lambda b, f: (1.0 - b) * jax.nn.sigmoid(-f), lb, fb)
    q = _each(_silu, qb)
    v = ib
    logf = _each(jnp.log, forget)
    sums = _each(lambda x: sel_sums(sel, x), logf)
    bc, b_start, b_end, b_half = ([x[i] for x in sums] for i in range(4))
    b_last = _each(lambda x: jnp.sum(x, axis=0, keepdims=True), logf)
    o = _each(lambda x, b, z: mm(x * jnp.exp(b), z, NT, prec), q, bc, st)
    if diags is None:
        diags = [diag_part(RolledRows())] * len(qb)
    yield
    o = list(o)
    for h in range(len(o)):
        o[h] = o[h] + diags[h](q[h], key[h], bc[h], v[h])
        yield
    second = jnp.bitwise_and(rcol, SUB_CHUNK - 1) >= DIAG_ROWS
    same_sub = jnp.bitwise_and(ri, c - SUB_CHUNK) == jnp.bitwise_and(ci, c - SUB_CHUNK)
    q_half = _each(lambda x, b, bh: x * jnp.exp(jnp.where(second, b - bh, -jnp.inf)), q, bc, b_half)
    k_half = _each(lambda x, b, bh: x * jnp.exp(jnp.where(second, -jnp.inf, bh - b)), key, bc, b_half)
    a_half = _each(lambda x, z: jnp.where(same_sub, mm(x, z, NT, prec), 0.0), q_half, k_half)
    o = _each(lambda acc, a, val: acc + mm(a, val, NN, prec), o, a_half, v)
    yield
    q_rel = _each(lambda x, b, bs: x * jnp.exp(b - bs), q, bc, b_start)
    k_rel = _each(lambda x, b, be: x * jnp.exp(be - b), key, bc, b_end)
    for y in range(c // SUB_CHUNK - 1):
        def scaled(x, b, bs):
            end_y = jnp.sum(jnp.where(rcol == SUB_CHUNK * y + SUB_CHUNK - 1, b, 0.0), axis=0, keepdims=True)
            return x * jnp.exp(jnp.where(rcol >= SUB_CHUNK * (y + 1), bs - end_y, -jnp.inf))
        dq = _each(scaled, q_rel, bc, b_start)
        in_y = (ci >= SUB_CHUNK * y) & (ci < SUB_CHUNK * (y + 1))
        a_y = _each(lambda x, z: jnp.where(in_y, mm(x, z, NT, prec), 0.0), dq, k_rel)
        o = _each(lambda acc, a, val: acc + mm(a, val, NN, prec), o, a_y, v)
        yield
    k_state = _each(lambda x, bl, b: x * jnp.exp(bl - b), key, b_last, bc)
    st_new = _each(lambda z, bl, val, x: z * jnp.exp(bl) + mm(val, x, TN, prec), st, b_last, v, k_state)
    if o_known is not None:
        o = _each(_known_value, o, o_known)
    return (_each(lambda x, z: _head_norm_gate(x, norm_w, z), o, gb), st_new), o


def _drain(gen):
    try:
        while True:
            next(gen)
    except StopIteration as done:
        return done.value


def _alternate(gen_a, gen_b):
    out, live = [None, None], [gen_a, gen_b]
    while any(g is not None for g in live):
        for i, g in enumerate(live):
            if g is None:
                continue
            try:
                next(g)
            except StopIteration as done:
                out[i], live[i] = done.value, None
    return out


def gdn_chunks(*args, **kwargs):
    return _drain(gdn_stages(*args, **kwargs))


def hgrn_chunks(*args, **kwargs):
    return _drain(hgrn_stages(*args, **kwargs))


def hgrn_chunk(qb, fb, ib, gb, l0, l1, norm_w, st, prec=HGRN_PREC, reuse_output=False):
    args = ([qb], [fb], [ib], [gb], [l0], [l1], norm_w, [st], prec)
    if reuse_output:
        known = lax.stop_gradient(hgrn_chunks(*args)[1])
        (y, st_new), _ = hgrn_chunks(*args, o_known=known)
    else:
        (y, st_new), _ = hgrn_chunks(*args)
    return y[0], st_new[0]


HEAD_VEC = (N_HEADS, 1, LANES)


class _ChunkSpecs:
    def __init__(self, nc, rev):
        self.nc, self.rev = nc, rev

    def _c(self, c):
        return self.nc - 1 - c if self.rev else c

    def row(self, width, block=0):
        return pl.BlockSpec((CHUNK, width), lambda c: (self._c(c), block))

    def per_head(self, rows):
        return pl.BlockSpec((None, N_HEADS, rows, rows), lambda c: (self._c(c), 0, 0, 0))

    @staticmethod
    def whole(shape):
        return pl.BlockSpec(shape, lambda c: (0,) * len(shape))


def _lanes(j):
    return slice(j * LANES, (j + 1) * LANES)


def mixer_fwd(qkv_c, proj, a_log_l, dt_l, gdn_norm_w, l0, l1, hgrn_norm_w, name):
    t = qkv_c.shape[0]
    hb = N_HEADS
    sp = _ChunkSpecs(t // CHUNK, rev=False)
    hs = list(range(hb))

    def body(q_ref, k_ref, v_ref, z_ref, ab_ref, al_ref, dt_ref, gnw_ref, qb_ref, fb_ref, ib_ref, gb_ref, l0_ref, l1_ref,
             hnw_ref, y_ref, hist_a_ref, inv_ref, hist_b_ref, o_ref, sa_ref, sb_ref, shift_ref):
        @pl.when(pl.program_id(0) == 0)
        def _():
            sa_ref[...] = jnp.zeros_like(sa_ref)
            sb_ref[...] = jnp.zeros_like(sb_ref)
            shift_ref[...] = jnp.zeros_like(shift_ref)

        heads = lambda ref: [ref[:, _lanes(j)] for j in hs]
        s_a, s_b = [sa_ref[h] for h in hs], [sb_ref[h] for h in hs]
        for h in hs:
            hist_a_ref[h] = s_a[h]
            hist_b_ref[h] = s_b[h]
        diags = [diag_part(SlotRows(shift_ref.at[h]), differentiable=False) for h in hs]
        ((y_a, s_a_new), inv), ((y_b, s_b_new), o_pre) = _alternate(
            gdn_stages(hs, heads(q_ref), heads(k_ref), heads(v_ref), heads(z_ref), ab_ref[...],
                       [al_ref[h] for h in hs], [dt_ref[h] for h in hs], gnw_ref[...], s_a),
            hgrn_stages(heads(qb_ref), heads(fb_ref), heads(ib_ref), heads(gb_ref),
                        [l0_ref[h] for h in hs], [l1_ref[h] for h in hs], hnw_ref[...], s_b, diags=diags))
        for h in hs:
            y_ref[:, _lanes(h)] = y_a[h].astype(BF16)
            y_ref[:, _lanes(hb + h)] = y_b[h].astype(BF16)
            o_ref[:, _lanes(h)] = o_pre[h]
            sa_ref[h] = s_a_new[h]
            sb_ref[h] = s_b_new[h]
            inv_ref[h] = inv[h]

    vec, gain, slab = sp.whole(HEAD_VEC), sp.whole((1, LANES)), functools.partial(sp.row, GDN_WIDTH)
    states = jax.ShapeDtypeStruct((sp.nc, N_HEADS, HEAD_DIM, HEAD_DIM), F32)
    return pl.pallas_call(
        body, name=name, grid=(sp.nc,),
        in_specs=[slab(0), slab(1), slab(2), slab(3), sp.row(LANES, AB_BLOCK), vec, vec, gain,
                  slab(4), slab(5), slab(6), slab(7), vec, vec, gain],
        out_specs=[sp.row(2 * GDN_WIDTH), sp.per_head(HEAD_DIM), sp.per_head(CHUNK), sp.per_head(HEAD_DIM), slab(0)],
        out_shape=[jax.ShapeDtypeStruct((t, 2 * GDN_WIDTH), BF16), states,
                   jax.ShapeDtypeStruct((sp.nc, N_HEADS, CHUNK, CHUNK), F32), states,
                   jax.ShapeDtypeStruct((t, GDN_WIDTH), F32)],
        scratch_shapes=[pltpu.VMEM((N_HEADS, HEAD_DIM, HEAD_DIM), F32), pltpu.VMEM((N_HEADS, HEAD_DIM, HEAD_DIM), F32),
                        pltpu.VMEM((hb, 3, SHIFT_WAYS, SHIFT_ROWS, LANES), F32)],
        compiler_params=_params(("arbitrary",)),
    )(qkv_c, qkv_c, qkv_c, proj, proj, a_log_l, dt_l, gdn_norm_w, proj, proj, proj, proj, l0, l1, hgrn_norm_w)


def mixer_bwd(qkv_c, proj, a_log_l, dt_l, gdn_norm_w, l0, l1, hgrn_norm_w, hist_a, inv_hist, hist_b, o_pre, dy, name):
    t = qkv_c.shape[0]
    hb = N_HEADS
    sp = _ChunkSpecs(t // CHUNK, rev=True)
    hs = list(range(hb))

    def body(q_ref, k_ref, v_ref, z_ref, ab_ref, al_ref, dt_ref, gnw_ref, qb_ref, fb_ref, ib_ref, gb_ref, l0_ref, l1_ref,
             hnw_ref, hist_a_ref, inv_ref, hist_b_ref, o_ref, dy_ref,
             dqkv_ref, dproj_ref, dal_ref, ddt_ref, dgnw_ref, dl0_ref, dl1_ref, dhnw_ref, dsa_ref, dsb_ref, shift_ref):
        @pl.when(pl.program_id(0) == 0)
        def _():
            for ref in (dal_ref, ddt_ref, dgnw_ref, dl0_ref, dl1_ref, dhnw_ref, dsa_ref, dsb_ref, shift_ref):
                ref[...] = jnp.zeros_like(ref)

        heads = lambda ref, first=0: [ref[:, _lanes(first + j)] for j in hs]
        diags = [diag_part(SlotRows(shift_ref.at[h])) for h in hs]
        inv_known, o_known = [inv_ref[h] for h in hs], heads(o_ref)

        def both(ga, gb):
            (ra, inv), (rb, o_pre) = _alternate(gdn_stages(hs, *ga, inv_known=inv_known),
                                                hgrn_stages(*gb, diags=diags, o_known=o_known))
            return (ra, rb), (inv, o_pre)

        ga = (heads(q_ref), heads(k_ref), heads(v_ref), heads(z_ref), ab_ref[...], [al_ref[h] for h in hs],
              [dt_ref[h] for h in hs], gnw_ref[...], [hist_a_ref[h] for h in hs])
        gb = (heads(qb_ref), heads(fb_ref), heads(ib_ref), heads(gb_ref), [l0_ref[h] for h in hs],
              [l1_ref[h] for h in hs], hnw_ref[...], [hist_b_ref[h] for h in hs])
        _, vjp, _ = jax.vjp(both, ga, gb, has_aux=True)
        dy_a = [x.astype(F32) for x in heads(dy_ref)]
        dy_b = [x.astype(F32) for x in heads(dy_ref, hb)]
        (dq, dk, dv, dz, dab, dal, ddt, dgnw, ds_a), (dqb, dfb, dib, dgb, dl0, dl1, dhnw, ds_b) = vjp(
            ((dy_a, [dsa_ref[h] for h in hs]), (dy_b, [dsb_ref[h] for h in hs])))
        for h in hs:
            dqkv_ref[:, _lanes(h)] = dq[h]
            dqkv_ref[:, _lanes(hb + h)] = dk[h]
            dqkv_ref[:, _lanes(2 * hb + h)] = dv[h]
            for slab, val in enumerate((dz, dqb, dfb, dib, dgb)):
                dproj_ref[:, _lanes((3 + slab) * hb + h)] = val[h].astype(BF16)
            dal_ref[h] += dal[h]
            ddt_ref[h] += ddt[h]
            dl0_ref[h] += dl0[h]
            dl1_ref[h] += dl1[h]
            dsa_ref[h] = ds_a[h]
            dsb_ref[h] = ds_b[h]
        dproj_ref[:, MAIN_WIDTH:] = dab.astype(BF16)
        dgnw_ref[...] += dgnw
        dhnw_ref[...] += dhnw

    vec, gain, slab = sp.whole(HEAD_VEC), sp.whole((1, LANES)), functools.partial(sp.row, GDN_WIDTH)
    vec_shape, gain_shape = jax.ShapeDtypeStruct(HEAD_VEC, F32), jax.ShapeDtypeStruct((1, LANES), F32)
    return pl.pallas_call(
        body, name=name, grid=(sp.nc,),
        in_specs=[slab(0), slab(1), slab(2), slab(3), sp.row(LANES, AB_BLOCK), vec, vec, gain,
                  slab(4), slab(5), slab(6), slab(7), vec, vec, gain,
                  sp.per_head(HEAD_DIM), sp.per_head(CHUNK), sp.per_head(HEAD_DIM), slab(0), sp.row(2 * GDN_WIDTH)],
        out_specs=[sp.row(QKV_WIDTH), sp.row(CAT_WIDTH), vec, vec, gain, vec, vec, gain],
        out_shape=[jax.ShapeDtypeStruct((t, QKV_WIDTH), F32), jax.ShapeDtypeStruct((t, CAT_WIDTH), BF16),
                   vec_shape, vec_shape, gain_shape, vec_shape, vec_shape, gain_shape],
        scratch_shapes=[pltpu.VMEM((N_HEADS, HEAD_DIM, HEAD_DIM), F32), pltpu.VMEM((N_HEADS, HEAD_DIM, HEAD_DIM), F32),
                        pltpu.VMEM((hb, 3, SHIFT_WAYS, SHIFT_ROWS, LANES), F32)],
        compiler_params=_params(("arbitrary",)),
    )(qkv_c, qkv_c, qkv_c, proj, proj, a_log_l, dt_l, gdn_norm_w, proj, proj, proj, proj, l0, l1, hgrn_norm_w,
      hist_a, inv_hist, hist_b, o_pre, dy)


def _adamw(w, g, m, v):
    m = ADAM_B1 * m + (1.0 - ADAM_B1) * g
    v = ADAM_B2 * v + (1.0 - ADAM_B2) * jnp.square(g)
    m_hat = m / (1.0 - ADAM_B1 ** ADAM_STEP)
    v_hat = v / (1.0 - ADAM_B2 ** ADAM_STEP)
    delta = -ADAM_LR * (m_hat / (jnp.sqrt(v_hat) + ADAM_EPS) + ADAM_WD * w)
    return delta, m, v


def adamw_reduce(parts, w, m, v, name, rb=128):
    r, c = w.shape
    rb = min(rb, r)
    n_parts = parts.shape[0]

    def body(p_ref, w_ref, m_ref, v_ref, g_ref, d_ref, mo_ref, vo_ref):
        g = p_ref[0].astype(F32)
        for d in range(1, n_parts):
            g = g + p_ref[d].astype(F32)
        delta, mn, vn = _adamw(w_ref[...], g, m_ref[...], v_ref[...])
        g_ref[...] = g
        d_ref[...] = delta
        mo_ref[...] = mn
        vo_ref[...] = vn

    blk = pl.BlockSpec((rb, c), lambda i: (i, 0))
    return pl.pallas_call(
        body, name=name, grid=(r // rb,),
        in_specs=[pl.BlockSpec((n_parts, rb, c), lambda i: (0, i, 0)), blk, blk, blk],
        out_specs=[blk] * 4, out_shape=[jax.ShapeDtypeStruct((r, c), F32)] * 4,
        compiler_params=_params(("parallel",)))(parts, w, m, v)


def adamw_small(w, g, m, v, name):
    def body(w_ref, g_ref, m_ref, v_ref, d_ref, mo_ref, vo_ref):
        delta, mn, vn = _adamw(w_ref[...], g_ref[...], m_ref[...], v_ref[...])
        d_ref[...] = delta
        mo_ref[...] = mn
        vo_ref[...] = vn

    vmem = pl.BlockSpec(memory_space=pltpu.VMEM)
    return pl.pallas_call(body, name=name, in_specs=[vmem] * 4, out_specs=[vmem] * 3,
                          out_shape=[jax.ShapeDtypeStruct(w.shape, F32)] * 3)(w, g, m, v)


def _pack(arrays):
    flat = jnp.concatenate([a.reshape(-1).astype(F32) for a in arrays])
    rows = -(-flat.shape[0] // (8 * LANES)) * 8
    return jnp.pad(flat, (0, rows * LANES - flat.shape[0])).reshape(rows, LANES)


def _unpack(packed, shapes):
    flat, out, off = packed.reshape(-1), [], 0
    for s in shapes:
        n = 1
        for d in s:
            n *= d
        out.append(flat[off:off + n].reshape(s))
        off += n
    return out


def _relu2_epilogue(acc, _):
    r = jnp.maximum(acc, 0.0)
    return acc, r * r


def _relu2_bwd_epilogue(acc, a1):
    return (acc * (2.0 * jnp.maximum(a1, 0.0)),)


def kernel(x, w_in, conv_w, gdn_a_log, gdn_dt_bias, gdn_norm_w, hgrn_lb_logits, hgrn_norm_w, w_out, norm_mix_w, norm_ffn_w, w_ff1, w_ff2, norm_final_w, loss_target, m_w_in, m_conv_w, m_gdn_a_log, m_gdn_dt_bias, m_gdn_norm_w, m_hgrn_lb_logits, m_hgrn_norm_w, m_w_out, m_norm_mix_w, m_norm_ffn_w, m_w_ff1, m_w_ff2, m_norm_final_w, v_w_in, v_conv_w, v_gdn_a_log, v_gdn_dt_bias, v_gdn_norm_w, v_hgrn_lb_logits, v_hgrn_norm_w, v_w_out, v_norm_mix_w, v_norm_ffn_w, v_w_ff1, v_w_ff2, v_norm_final_w):
    me = _my_flat()
    xs = x[0]
    target = loss_target[0]
    shard_in = w_in.shape[2]
    shard_conv = conv_w.shape[2]

    tok = lambda t: t[0:1, 0:1]
    own = lambda src: lax.dynamic_index_in_dim(src, me, 0, keepdims=False)

    half = D_MODEL // 2
    w_in_b = w_in[0].astype(BF16)
    g_in_a, g_conv = gather_two_level([w_in_b[:half], conv_w[0]], "gather_w_in")
    h_g0, t_g0 = exchange_start([w_in_b[half:]], True, "gather_w_in_low_start", after=[g_in_a], peers=CHIP_PEERS)
    h_g1, t_g1 = exchange_start([w_out[0].astype(BF16), w_ff1[0].astype(BF16)], True, "gather_mid_start", after=[t_g0],
                                peers=CHIP_PEERS)
    h_g2, t_g2 = exchange_start([w_ff2[0].astype(BF16)], True, "gather_ff2_start", after=[t_g1], peers=CHIP_PEERS)
    w_cat = weights_to_cat(g_in_a, "weights_to_cat", D_MODEL)
    conv_full = jnp.transpose(g_conv, (1, 0, 2)).reshape(4, QKV_WIDTH)

    lane_b = lambda p: jnp.broadcast_to(p.reshape(N_HEADS, 1, 1), HEAD_VEC)
    a_log_l, dt_l = lane_b(gdn_a_log[0]), lane_b(gdn_dt_bias[0])
    l0 = hgrn_lb_logits[0].reshape(HEAD_VEC)
    l1 = hgrn_lb_logits[1].reshape(HEAD_VEC)

    n1, r1 = rms_fwd(xs, norm_mix_w + tok(t_g1) + tok(t_g2), "rms_mix")
    proj = matmul(n1, w_cat, "nn", "in_proj_high", (BF16,), tn=CAT_WIDTH // 5, tk=half, k_blocks=(0, 1))
    (s_low,), (l_low,) = exchange_wait(h_g0, "gather_w_in_low_wait", after=[proj], copies=len(CHIP_PEERS))
    h_f0, _ = forward_start([l_low], "gather_w_in_low_forward")
    _, (l_low,) = exchange_wait(_one(h_f0, 0), "forward_w_in_low_wait", copies=len(OTHER_CHIPS))
    w_cat = weights_to_cat(_own_slot(l_low, s_low), "weights_to_cat_low", D_MODEL, row0=half, into=w_cat)
    proj = matmul(n1, w_cat, "nn", "in_proj_low", tn=CAT_WIDTH // 5, tk=half, k_blocks=(1, 1), extra=proj,
                  epilogue=lambda acc, high: (acc + high,))
    qkv_c = conv_fwd(proj, conv_full, "conv_fwd")
    y, hist_a, inv_a, hist_b, o_b = mixer_fwd(qkv_c, proj, a_log_l, dt_l, gdn_norm_w, l0, l1, hgrn_norm_w, "mixer_fwd")
    (s_out, s_ff1), (l_out, l_ff1) = exchange_wait(h_g1, "gather_mid_wait", after=[y], copies=len(CHIP_PEERS))
    (s_ff2,), (l_ff2,) = exchange_wait(h_g2, "gather_ff2_wait", after=[y], copies=len(CHIP_PEERS))
    h_fw, _ = forward_start([l_out, l_ff1, l_ff2], "gather_forward_start")
    _, (l_out,) = exchange_wait(_one(h_fw, 0), "forward_out_wait", copies=len(OTHER_CHIPS))
    w_out_full = _own_slot(l_out, s_out).reshape(D_MODEL, D_MODEL)
    h1, n2, r2 = out_proj_rms(y, w_out_full, xs, norm_ffn_w, "out_proj_rms")
    _, (l_ff1,) = exchange_wait(_one(h_fw, 1), "forward_ff1_wait", after=[n2], copies=len(OTHER_CHIPS))
    w_ff1_sh = _own_slot(l_ff1, s_ff1)
    a1, act = matmul(n2, w_ff1_sh, "nn", "ff1", out_dtypes=(F32, BF16), epilogue=_relu2_epilogue, b_shards=True)
    _, (l_ff2,) = exchange_wait(_one(h_fw, 2), "forward_ff2_wait", after=[act], copies=len(OTHER_CHIPS))
    w_ff2_full = _own_slot(l_ff2, s_ff2).reshape(D_FF, D_MODEL)
    loss_sum, dh2_b, d_final = ff2_loss(act, w_ff2_full, h1, norm_final_w.reshape(1, D_MODEL), target, "ff2_loss")

    da1 = matmul(dh2_b, w_ff2_full, "nt", "d_act", out_dtypes=(BF16,), epilogue=_relu2_bwd_epilogue, extra=a1)
    t_all = xs.shape[0]
    dw_ff2 = matmul(act, dh2_b, "tn", "dw_ff2", out_dtypes=(BF16,), tk=t_all)
    p_ff2 = dw_ff2.reshape(N_DEV, D_FF // N_DEV, D_MODEL)
    h_s1, t_s1 = exchange_start([p_ff2], False, "scatter_ff2_start")
    dn2 = matmul(da1, w_ff1_sh, "nt", "d_n2", out_dtypes=(BF16,), after=[t_s1], b_shards=True, k_group=4)
    p_ff1 = matmul(n2, da1, "tn", "dw_ff1", out_dtypes=(BF16,), tn=D_FF // N_DEV, tk=t_all, after=[t_s1], out_shards=True)
    h_s2, t_s2 = exchange_start([p_ff1], False, "scatter_ff1_start")
    dh1_b, d_ffn = rms_bwd(h1, r2, norm_ffn_w + tok(t_s2), dn2, dh2_b, BF16, "rms_ffn_bwd")
    dmix = matmul(dh1_b, w_out_full, "nt", "d_mix", out_dtypes=(BF16,))
    dw_out = matmul(y, dh1_b, "tn", "dw_out", out_dtypes=(BF16,), tk=t_all)
    p_out = dw_out.reshape(N_DEV, D_MODEL // N_DEV, D_MODEL)
    h_s3, t_s3 = exchange_start([p_out], False, "scatter_out_start")
    d_qkv_c, dproj, d_alog_l, d_dt_l, d_gnw, dl0, dl1, d_hnw = mixer_bwd(
        qkv_c, proj, a_log_l, dt_l, gdn_norm_w + tok(t_s3), l0, l1, hgrn_norm_w, hist_a, inv_a, hist_b, o_b, dmix,
        "mixer_bwd")
    dproj, d_conv_full = conv_bwd(proj, d_qkv_c, conv_full, dproj, "conv_bwd")
    dw_cat = matmul(n1, dproj, "tn", "dw_in", out_dtypes=(BF16,), tm=512, tn=CAT_WIDTH // 5, tk=t_all)
    p_in = cat_to_shards(dw_cat, shard_in)
    h_pair, t_s4 = routed_start(p_in, _to_sibling_routes, "scatter_in_pair_start")

    (s_ff2g,), (r_ff2,) = exchange_wait(h_s1, "scatter_ff2_wait", after=[t_s4])
    (s_ff1g,), (r_ff1,) = exchange_wait(h_s2, "scatter_ff1_wait", after=[t_s4])
    (s_outg,), (r_out,) = exchange_wait(h_s3, "scatter_out_wait", after=[t_s4])
    g_w_ff2, d_w_ff2, nm_w_ff2, nv_w_ff2 = adamw_reduce(
        _own_slot(r_ff2, own(s_ff2g)), w_ff2[0], m_w_ff2[0], v_w_ff2[0], "adamw_w_ff2")
    g_w_ff1, d_w_ff1, nm_w_ff1, nv_w_ff1 = adamw_reduce(
        _own_slot(r_ff1, own(s_ff1g)), w_ff1[0], m_w_ff1[0], v_w_ff1[0], "adamw_w_ff1")
    g_w_out, d_w_out, nm_w_out, nv_w_out = adamw_reduce(
        _own_slot(r_out, own(s_outg)), w_out[0], m_w_out[0], v_w_out[0], "adamw_w_out")
    (p_in,), (from_sibling,) = exchange_wait(h_pair, "scatter_in_pair_wait", after=[d_w_ff2, d_w_ff1, d_w_out],
                                             copies=N_CHIPS)
    chip_sums = pair_sum(p_in, from_sibling, "scatter_in_pair_sum")
    h_chips, t_s5 = routed_start(chip_sums, _to_chips_routes, "scatter_in_chips_start")
    dn1 = matmul(dproj, w_cat, "nt", "d_n1", out_dtypes=(BF16,), tm=512, tn=512, tk=CAT_WIDTH, after=[t_s5])
    dx, d_mix = rms_bwd(xs, r1, norm_mix_w, dn1, dh1_b, F32, "rms_mix_bwd")
    (chip_sums,), (r_in,) = exchange_wait(h_chips, "scatter_in_chips_wait", after=[dx], copies=len(OTHER_CHIPS))
    my_chip = me // 2
    r_in = lax.dynamic_update_slice(r_in, lax.dynamic_index_in_dim(chip_sums, my_chip, 0, keepdims=True), (my_chip, 0, 0))
    g_w_in, d_w_in, nm_w_in, nv_w_in = adamw_reduce(r_in, w_in[0], m_w_in[0], v_w_in[0], "adamw_w_in")

    d_lb = jnp.stack([dl0.reshape(GDN_WIDTH), dl1.reshape(GDN_WIDTH)])
    small_shapes = [(1, N_HEADS), (1, N_HEADS), (1, HEAD_DIM), (2, GDN_WIDTH), (1, HEAD_DIM), (1, D_MODEL),
                    (1, D_MODEL), (D_MODEL,), (4, QKV_WIDTH), ()]
    small = _pack([d_alog_l[:, 0, 0], d_dt_l[:, 0, 0], d_gnw, d_lb, d_hnw, d_mix, d_ffn, d_final, d_conv_full,
                   loss_sum[0, 0]])
    red = allreduce_small(small, "allreduce_small")
    g_alog, g_dt, g_gnw, g_lb, g_hnw, g_mix, g_ffn, g_final, g_conv_full, loss = _unpack(red, small_shapes)
    g_conv = lax.dynamic_slice(g_conv_full, (0, me * shard_conv), (4, shard_conv)).reshape(1, 4, shard_conv)
    small_g = [g_alog, g_dt, g_gnw, g_lb, g_hnw, g_mix, g_ffn, g_final, g_conv]
    small_w = [gdn_a_log, gdn_dt_bias, gdn_norm_w, hgrn_lb_logits, hgrn_norm_w, norm_mix_w, norm_ffn_w, norm_final_w, conv_w]
    small_m = [m_gdn_a_log, m_gdn_dt_bias, m_gdn_norm_w, m_hgrn_lb_logits, m_hgrn_norm_w, m_norm_mix_w, m_norm_ffn_w,
               m_norm_final_w, m_conv_w]
    small_v = [v_gdn_a_log, v_gdn_dt_bias, v_gdn_norm_w, v_hgrn_lb_logits, v_hgrn_norm_w, v_norm_mix_w, v_norm_ffn_w,
               v_norm_final_w, v_conv_w]
    shapes = [a.shape for a in small_w]
    d_s, m_s, v_s = adamw_small(_pack(small_w), _pack(small_g), _pack(small_m), _pack(small_v), "adamw_small")
    d_alog, d_dt, d_gn, d_lbl, d_hn, d_nm, d_nf, d_nfin, d_cw = _unpack(d_s, shapes)
    m_alog, m_dt, m_gn, m_lbl, m_hn, m_nm, m_nf, m_nfin, m_cw = _unpack(m_s, shapes)
    v_alog, v_dt, v_gn, v_lbl, v_hn, v_nm, v_nf, v_nfin, v_cw = _unpack(v_s, shapes)

    lead = lambda a: a[None]
    grads = [lead(g_w_in), g_conv, g_alog, g_dt, g_gnw, g_lb, g_hnw, lead(g_w_out), g_mix, g_ffn,
             lead(g_w_ff1), lead(g_w_ff2), g_final]
    deltas = [lead(d_w_in), d_cw, d_alog, d_dt, d_gn, d_lbl, d_hn, lead(d_w_out), d_nm, d_nf,
              lead(d_w_ff1), lead(d_w_ff2), d_nfin]
    new_m = [lead(nm_w_in), m_cw, m_alog, m_dt, m_gn, m_lbl, m_hn, lead(nm_w_out), m_nm, m_nf,
             lead(nm_w_ff1), lead(nm_w_ff2), m_nfin]
    new_v = [lead(nv_w_in), v_cw, v_alog, v_dt, v_gn, v_lbl, v_hn, lead(nv_w_out), v_nm, v_nf,
             lead(nv_w_ff1), lead(nv_w_ff2), v_nfin]
    return (loss, dx[None], *grads, *deltas, *new_m, *new_v)
```

```python
import functools

import jax
import jax.numpy as jnp
from jax import lax
from jax.experimental import pallas as pl
from jax.experimental.pallas import tpu as pltpu

F32 = jnp.float32
BF16 = jnp.bfloat16
HI = lax.Precision.HIGHEST

N_DEV = 8
D_MODEL = 2048
CHUNK = 64
SUB_CHUNK = 16
HEAD_DIM = 128
N_HEADS = 8
GDN_WIDTH = N_HEADS * HEAD_DIM
D_FF = 4 * D_MODEL
QKV_WIDTH = 3 * GDN_WIDTH
MAIN_WIDTH = 8 * GDN_WIDTH
CAT_WIDTH = MAIN_WIDTH + 128
AB_BLOCK = MAIN_WIDTH // 128
NORM_EPS = 1e-6
L2_EPS = 1e-6
LANES = 128
VMEM_LIMIT = 56 * 1024 * 1024

ADAM_LR = 0.001
ADAM_B1 = 0.9
ADAM_B2 = 0.999
ADAM_EPS = 1e-08
ADAM_WD = 0.01
ADAM_STEP = 10

MESH = pl.DeviceIdType.MESH


def _params(sem=None):
    return pltpu.CompilerParams(dimension_semantics=sem, vmem_limit_bytes=VMEM_LIMIT)


def _dot(a, b, dims, prec=None):
    return lax.dot_general(a, b, (dims, ((), ())), precision=prec, preferred_element_type=F32)


NN = ((1,), (0,))
NT = ((1,), (1,))
TN = ((0,), (0,))


def _split_bf16(x, pieces):
    out = []
    for _ in range(pieces - 1):
        p = x.astype(BF16)
        out.append(p)
        x = x - p.astype(F32)
    out.append(x.astype(BF16))
    return out


def _mm_raw(a, b, dims, prec):
    if prec == "hi":
        return _dot(a, b, dims, HI)
    if prec == "bf":
        return _dot(a.astype(BF16), b.astype(BF16), dims)
    a_hi, a_lo = _split_bf16(a, 2)
    b_hi, b_lo = _split_bf16(b, 2)
    return _dot(a_hi, b_hi, dims) + (_dot(a_hi, b_lo, dims) + _dot(a_lo, b_hi, dims))


@functools.partial(jax.custom_vjp, nondiff_argnums=(2, 3))
def mm(a, b, dims, prec):
    return _mm_raw(a, b, dims, prec)


def _mm_fwd(a, b, dims, prec):
    return _mm_raw(a, b, dims, prec), (a, b)


def _mm_bwd(dims, prec, res, ct):
    a, b = res
    if dims == NN:
        return _mm_raw(ct, b, NT, prec), _mm_raw(a, ct, TN, prec)
    if dims == NT:
        return _mm_raw(ct, b, NN, prec), _mm_raw(ct, a, TN, prec)
    return _mm_raw(b, ct, NT, prec), _mm_raw(a, ct, NN, prec)


mm.defvjp(_mm_fwd, _mm_bwd)


def _sel_raw(sel, x, dims):
    sel = sel.astype(BF16)
    p0, p1, p2 = _split_bf16(x, 3)
    return _dot(sel, p0, dims) + (_dot(sel, p1, dims) + _dot(sel, p2, dims))


def _sel_parts(sel, x):
    c = x.shape[0]
    full = _sel_raw(sel, x, NN)
    return tuple(full[i * c:(i + 1) * c] for i in range(sel.shape[0] // c))


@jax.custom_vjp
def sel_sums(sel, x):
    return _sel_parts(sel, x)


def _sel_fwd(sel, x):
    return _sel_parts(sel, x), sel


def _sel_bwd(sel, cts):
    return jnp.zeros_like(sel), _sel_raw(sel, jnp.concatenate(cts, axis=0), TN)


sel_sums.defvjp(_sel_fwd, _sel_bwd)


@jax.custom_vjp
def _known_value(computed, known):
    del computed
    return known


_known_value.defvjp(lambda computed, known: (known, None), lambda _, ct: (ct, jnp.zeros_like(ct)))


def _my_flat():
    return 4 * lax.axis_index("x") + 2 * lax.axis_index("y") + lax.axis_index("c")


def _peer(k):
    x, y, c = lax.axis_index("x"), lax.axis_index("y"), lax.axis_index("c")
    kx, ky, kc = (k >> 2) & 1, (k >> 1) & 1, k & 1
    px = (1 - x) if kx else x
    py = (1 - y) if ky else y
    pc = (1 - c) if kc else c
    return (px, py, pc), 4 * px + 2 * py + pc


def gather_two_level(xs, name):
    n = len(xs)

    def body(*refs):
        x_refs, y_refs = refs[:n], refs[n:2 * n]
        send_sems, recv_sems, local_sems = refs[2 * n:]
        x, y, c = lax.axis_index("x"), lax.axis_index("y"), lax.axis_index("c")
        me, sibling = (x, y, c), (x, y, 1 - c)
        chips = [(1 - x, y), (x, 1 - y), (1 - x, 1 - y)]
        flat = lambda p: 4 * p[0] + 2 * p[1] + p[2]

        def copy(a, k, block, to, src=None):
            return pltpu.make_async_remote_copy(
                src_ref=y_refs[a].at[flat(block)] if src is None else src, dst_ref=y_refs[a].at[flat(block)],
                send_sem=send_sems.at[a, k], recv_sem=recv_sems.at[a, k], device_id=to, device_id_type=MESH)

        mine = [pltpu.make_async_copy(x_refs[a], y_refs[a].at[flat(me)], local_sems.at[a]) for a in range(n)]
        for cp in mine:
            cp.start()
        first = [copy(a, 0, me, sibling, src=x_refs[a]) for a in range(n)]
        first += [copy(a, 1 + j, me, (*chip, c), src=x_refs[a]) for j, chip in enumerate(chips) for a in range(n)]
        for cp in first:
            cp.start()
        passed = []
        for j, chip in enumerate(chips):
            for a in range(n):
                copy(a, 1 + j, (*chip, c), me).wait_recv()
                cp = copy(a, 4 + j, (*chip, c), sibling)
                cp.start()
                passed.append(cp)
        for a in range(n):
            copy(a, 0, sibling, me).wait_recv()
        for j, chip in enumerate(chips):
            for a in range(n):
                copy(a, 4 + j, (*chip, 1 - c), me).wait_recv()
        for cp in first + passed:
            cp.wait_send()
        for cp in mine:
            cp.wait()

    any_spec = pl.BlockSpec(memory_space=pl.ANY)
    return pl.pallas_call(
        body, name=name, out_shape=[jax.ShapeDtypeStruct((N_DEV,) + x.shape, x.dtype) for x in xs],
        in_specs=[any_spec] * n, out_specs=[any_spec] * n,
        scratch_shapes=[pltpu.SemaphoreType.DMA((n, N_DEV - 1)), pltpu.SemaphoreType.DMA((n, N_DEV - 1)),
                        pltpu.SemaphoreType.DMA((n,))],
    )(*xs)


HBM_SPEC = pl.BlockSpec(memory_space=pltpu.HBM)
SEM_SPEC = pl.BlockSpec(memory_space=pltpu.SEMAPHORE)
ANY_SPEC = pl.BlockSpec(memory_space=pl.ANY)
DATAFLOW = pltpu.SideEffectType.DATAFLOW_SIDE_EFFECTING


def _in_hbm(x):
    return pltpu.with_memory_space_constraint(x, pltpu.HBM)


ALL_PEERS = tuple(range(1, N_DEV))
CHIP_PEERS = (1, 2, 4, 6)
OTHER_CHIPS = (2, 4, 6)


def exchange_start(xs, gather, name, after=(), peers=ALL_PEERS):
    n, n_after = len(xs), len(after)

    def body(*refs):
        x_refs, land_refs = refs[:n], refs[n:2 * n]
        sems = refs[2 * n + n_after:2 * n + n_after + 2 * n]
        token, own_sems = refs[-2], refs[-1]
        me = _my_flat()
        own = [pltpu.make_async_copy(x_refs[a] if gather else x_refs[a].at[me], land_refs[a].at[me], own_sems.at[a])
               for a in range(n)]
        for cp in own:
            cp.start()
        for k in peers:
            peer, peer_flat = _peer(k)
            for a in range(n):
                src = x_refs[a] if gather else x_refs[a].at[peer_flat]
                pltpu.make_async_remote_copy(src_ref=src, dst_ref=land_refs[a].at[me], send_sem=sems[a],
                                             recv_sem=sems[n + a], device_id=peer, device_id_type=MESH).start()
        token[...] = jnp.zeros_like(token)
        for cp in own:
            cp.wait()

    lands = [_in_hbm(lax.empty(((N_DEV,) + x.shape) if gather else x.shape, x.dtype)) for x in xs]
    hbm_out = [pltpu.HBM(x.shape, x.dtype) for x in xs] + [pltpu.HBM(l.shape, l.dtype) for l in lands]
    res = pl.pallas_call(
        body, name=name,
        out_shape=(*([pltpu.SemaphoreType.DMA(())] * (2 * n)), *hbm_out, jax.ShapeDtypeStruct((8, LANES), F32)),
        in_specs=[HBM_SPEC] * (2 * n) + [ANY_SPEC] * n_after,
        out_specs=(*([SEM_SPEC] * (2 * n)), *([HBM_SPEC] * (2 * n)), pl.BlockSpec(memory_space=pltpu.VMEM)),
        input_output_aliases={i: 2 * n + i for i in range(2 * n)},
        scratch_shapes=[pltpu.SemaphoreType.DMA((n,))],
        compiler_params=pltpu.CompilerParams(has_side_effects=DATAFLOW),
    )(*[_in_hbm(x) for x in xs], *lands, *after)
    return (list(res[:2 * n]), list(res[2 * n:3 * n]), list(res[3 * n:4 * n])), res[-1]


def forward_start(lands, name, after=()):
    n, n_after = len(lands), len(after)

    def body(*refs):
        land_refs = refs[:n]
        sems = refs[n + n_after:n + n_after + 2 * n]
        token = refs[-1]
        sibling, _ = _peer(1)
        for a in range(n):
            for k in OTHER_CHIPS:
                _, from_flat = _peer(k)
                slot = land_refs[a].at[from_flat]
                pltpu.make_async_remote_copy(src_ref=slot, dst_ref=slot, send_sem=sems[a], recv_sem=sems[n + a],
                                             device_id=sibling, device_id_type=MESH).start()
        token[...] = jnp.zeros_like(token)

    res = pl.pallas_call(
        body, name=name,
        out_shape=(*([pltpu.SemaphoreType.DMA(())] * (2 * n)), *[pltpu.HBM(l.shape, l.dtype) for l in lands],
                   jax.ShapeDtypeStruct((8, LANES), F32)),
        in_specs=[HBM_SPEC] * n + [ANY_SPEC] * n_after,
        out_specs=(*([SEM_SPEC] * (2 * n)), *([HBM_SPEC] * n), pl.BlockSpec(memory_space=pltpu.VMEM)),
        input_output_aliases={i: 2 * n + i for i in range(n)},
        compiler_params=pltpu.CompilerParams(has_side_effects=DATAFLOW),
    )(*lands, *after)
    return (list(res[:2 * n]), [], list(res[2 * n:3 * n])), res[-1]


def exchange_wait(handle, name, after=(), copies=N_DEV - 1):
    sems, xs, lands = handle
    n, n_x, n_after = len(lands), len(xs), len(after)

    def body(*refs):
        land_refs = refs[n_x:n_x + n]
        sem_refs = refs[n_x + n:n_x + 3 * n]
        for a in range(n):
            every = land_refs[a].at[pl.ds(0, copies)]
            cp = pltpu.make_async_remote_copy(src_ref=every, dst_ref=every, send_sem=sem_refs[a],
                                              recv_sem=sem_refs[n + a], device_id=_peer(1)[0], device_id_type=MESH)
            cp.wait_send()
            cp.wait_recv()

    res = pl.pallas_call(
        body, name=name,
        out_shape=[pltpu.HBM(x.shape, x.dtype) for x in xs] + [pltpu.HBM(l.shape, l.dtype) for l in lands],
        in_specs=[HBM_SPEC] * (n_x + n) + [SEM_SPEC] * (2 * n) + [ANY_SPEC] * n_after,
        out_specs=[HBM_SPEC] * (n_x + n),
        input_output_aliases={i: i for i in range(n_x + n)},
        compiler_params=pltpu.CompilerParams(has_side_effects=DATAFLOW),
    )(*xs, *lands, *sems, *after)
    return list(res[:n_x]), list(res[n_x:])


N_CHIPS = N_DEV // 2


def routed_start(x, routes, name, after=()):
    n_after = len(after)

    def body(*refs):
        x_ref, land_ref = refs[0], refs[1]
        send_sem, recv_sem = refs[2 + n_after], refs[3 + n_after]
        token = refs[-1]
        for src, dst, peer in routes():
            pltpu.make_async_remote_copy(src_ref=x_ref.at[src], dst_ref=land_ref.at[dst], send_sem=send_sem,
                                         recv_sem=recv_sem, device_id=peer, device_id_type=MESH).start()
        token[...] = jnp.zeros_like(token)

    land = _in_hbm(lax.empty((N_CHIPS,) + x.shape[1:], x.dtype))
    res = pl.pallas_call(
        body, name=name,
        out_shape=(pltpu.SemaphoreType.DMA(()), pltpu.SemaphoreType.DMA(()), pltpu.HBM(x.shape, x.dtype),
                   pltpu.HBM(land.shape, land.dtype), jax.ShapeDtypeStruct((8, LANES), F32)),
        in_specs=[HBM_SPEC, HBM_SPEC] + [ANY_SPEC] * n_after,
        out_specs=(SEM_SPEC, SEM_SPEC, HBM_SPEC, HBM_SPEC, pl.BlockSpec(memory_space=pltpu.VMEM)),
        input_output_aliases={0: 2, 1: 3},
        compiler_params=pltpu.CompilerParams(has_side_effects=DATAFLOW),
    )(_in_hbm(x), land, *after)
    return ([res[0], res[1]], [res[2]], [res[3]]), res[-1]


def _to_sibling_routes():
    c = lax.axis_index("c")
    sibling, _ = _peer(1)
    return [(2 * chip + 1 - c, chip, sibling) for chip in range(N_CHIPS)]


def _to_chips_routes():
    my_chip = _my_flat() // 2
    routes = []
    for k in OTHER_CHIPS:
        peer, peer_flat = _peer(k)
        routes.append((peer_flat // 2, my_chip, peer))
    return routes


def pair_sum(p, from_sibling, name, rb=1024):
    _, r, c = p.shape
    mine = lax.axis_index("c").astype(jnp.int32).reshape(1)

    def body(kind_ref, p_ref, s_ref, o_ref):
        del kind_ref
        o_ref[...] = (p_ref[...].astype(F32) + s_ref[...].astype(F32)).astype(BF16)

    return pl.pallas_call(
        body, name=name,
        grid_spec=pltpu.PrefetchScalarGridSpec(
            num_scalar_prefetch=1, grid=(N_CHIPS, r // rb),
            in_specs=[pl.BlockSpec((None, None, rb, c), lambda chip, i, kind: (chip, kind[0], i, 0)),
                      pl.BlockSpec((None, rb, c), lambda chip, i, kind: (chip, i, 0))],
            out_specs=pl.BlockSpec((None, rb, c), lambda chip, i, kind: (chip, i, 0))),
        out_shape=jax.ShapeDtypeStruct((N_CHIPS, r, c), BF16),
        compiler_params=_params(("parallel", "parallel")))(mine, p.reshape(N_CHIPS, 2, r, c), from_sibling)


def _one(handle, a):
    sems, xs, lands = handle
    n = len(lands)
    return [sems[a], sems[n + a]], xs[a:a + 1], [lands[a]]


def allreduce_small(x, name):
    rows = x.shape[0]

    def body(x_ref, o_ref, buf, send_sems, recv_sems):
        me = _my_flat()
        buf[me] = x_ref[...]
        sends = []
        for k in range(1, N_DEV):
            peer, _ = _peer(k)
            cp = pltpu.make_async_remote_copy(
                src_ref=x_ref, dst_ref=buf.at[me], send_sem=send_sems.at[k], recv_sem=recv_sems.at[k],
                device_id=peer, device_id_type=MESH)
            cp.start()
            sends.append(cp)
        for k in range(1, N_DEV):
            peer, peer_flat = _peer(k)
            pltpu.make_async_remote_copy(
                src_ref=x_ref, dst_ref=buf.at[peer_flat], send_sem=send_sems.at[k], recv_sem=recv_sems.at[k],
                device_id=peer, device_id_type=MESH).wait_recv()
        for cp in sends:
            cp.wait_send()
        acc = buf[0]
        for d in range(1, N_DEV):
            acc = acc + buf[d]
        o_ref[...] = acc

    vmem = pl.BlockSpec(memory_space=pltpu.VMEM)
    return pl.pallas_call(
        body, name=name, out_shape=jax.ShapeDtypeStruct((rows, LANES), F32),
        in_specs=[vmem], out_specs=vmem,
        scratch_shapes=[pltpu.VMEM((N_DEV, rows, LANES), F32),
                        pltpu.SemaphoreType.DMA((N_DEV,)), pltpu.SemaphoreType.DMA((N_DEV,))],
    )(x)


def matmul(a, b, mode, name, out_dtypes=(F32,), epilogue=None, extra=None, tm=1024, tn=1024, tk=2048, after=(),
           b_shards=False, out_shards=False, k_group=1, k_blocks=None):
    if b_shards:
        n_sh, b_rows, b_cols = b.shape
    if mode == "nn":
        (m, kd), n = a.shape, (n_sh * b_cols if b_shards else b.shape[1])
        if b_shards:
            tn = b_cols
    elif mode == "nt":
        (m, kd), n = a.shape, (b_rows if b_shards else b.shape[0])
        if b_shards:
            tk = k_group * b_cols
    else:
        (kd, m), n = a.shape, b.shape[1]
    tm, tn, tk = min(tm, m), min(tn, n), min(tk, kd)
    assert m % tm == 0 and n % tn == 0 and kd % tk == 0, (name, m, n, kd, tm, tn, tk)
    k0, ksteps = (0, kd // tk) if k_blocks is None else k_blocks
    dims = {"nn": NN, "nt": NT, "tn": TN}[mode]
    n_out = len(out_dtypes)
    n_in = 2 + (extra is not None) + len(after)

    def finish(acc, e_ref, o_refs):
        outs = (acc,) if epilogue is None else epilogue(acc, e_ref[...] if e_ref is not None else None)
        for o_ref, o in zip(o_refs, outs):
            o_ref[...] = o.astype(o_ref.dtype)

    def product(a_ref, b_ref):
        if mode == "nt" and b_shards:
            w = b_cols
            parts = [_dot(a_ref[:, s * w:(s + 1) * w], b_ref[s], dims) for s in range(k_group)]
            return functools.reduce(lambda p, q: p + q, parts)
        return _dot(a_ref[...], b_ref[...], dims)

    def body(*refs):
        a_ref, b_ref = refs[0], refs[1]
        e_ref = refs[2] if extra is not None else None
        o_refs = refs[n_in:n_in + n_out]
        if ksteps == 1:
            finish(product(a_ref, b_ref), e_ref, o_refs)
            return
        acc_ref = refs[-1]
        kk = pl.program_id(2)

        @pl.when(kk == 0)
        def _():
            acc_ref[...] = jnp.zeros_like(acc_ref)

        acc_ref[...] += product(a_ref, b_ref)

        @pl.when(kk == ksteps - 1)
        def _():
            finish(acc_ref[...], e_ref, o_refs)

    if mode == "nn":
        a_spec = pl.BlockSpec((tm, tk), lambda i, j, k: (i, k0 + k))
        b_spec = (pl.BlockSpec((None, tk, tn), lambda i, j, k: (j, k, 0)) if b_shards
                  else pl.BlockSpec((tk, tn), lambda i, j, k: (k0 + k, j)))
    elif mode == "nt":
        a_spec = pl.BlockSpec((tm, tk), lambda i, j, k: (i, k))
        b_spec = (pl.BlockSpec((k_group, tn, b_cols), lambda i, j, k: (k, j, 0)) if b_shards
                  else pl.BlockSpec((tn, tk), lambda i, j, k: (j, k)))
    else:
        a_spec = pl.BlockSpec((tk, tm), lambda i, j, k: (k, i))
        b_spec = pl.BlockSpec((tk, tn), lambda i, j, k: (k, j))
    o_spec = pl.BlockSpec((tm, tn), lambda i, j, k: (i, j))
    res_spec = pl.BlockSpec((None, tm, tn), lambda i, j, k: (j, i, 0)) if out_shards else o_spec
    res_shape = (n // tn, m, tn) if out_shards else (m, n)
    in_specs = [a_spec, b_spec] + ([o_spec] if extra is not None else []) + [ANY_SPEC] * len(after)
    args = (a, b) + ((extra,) if extra is not None else ()) + tuple(after)
    res = pl.pallas_call(
        body, name=name, grid=(m // tm, n // tn, ksteps),
        in_specs=in_specs, out_specs=[res_spec] * n_out,
        out_shape=[jax.ShapeDtypeStruct(res_shape, dt) for dt in out_dtypes],
        scratch_shapes=[pltpu.VMEM((tm, tn), F32)] if ksteps > 1 else [],
        compiler_params=_params(("parallel", "parallel", "arbitrary")),
    )(*args)
    return res if n_out > 1 else res[0]


GATE_COL = 4 * GDN_WIDTH
RELAYOUT_ROWS = 256


def _cat_of_win(j):
    if j < GATE_COL:
        return j
    if j < GATE_COL + 2 * N_HEADS:
        return MAIN_WIDTH + (j - GATE_COL)
    return j - 2 * N_HEADS


def _win_of_cat(c):
    if c < GATE_COL:
        return c
    if c < MAIN_WIDTH:
        return c + 2 * N_HEADS
    if c < MAIN_WIDTH + 2 * N_HEADS:
        return GATE_COL + (c - MAIN_WIDTH)
    return None


def _runs(first, count, mapping):
    runs, i = [], 0
    while i < count:
        start, n = mapping(first + i), 1
        while i + n < count and mapping(first + i + n) == start + n:
            n += 1
        runs.append((start, n))
        i += n
    return runs


def weights_to_cat(g_in, name, total_rows, row0=0, into=None):
    n_dev, rows, shard = g_in.shape
    first = row0 // RELAYOUT_ROWS

    def body(x_ref, *rest):
        o_ref = rest[-1]
        for b in range(CAT_WIDTH // LANES):
            live = sum(_win_of_cat(LANES * b + i) is not None for i in range(LANES))
            parts = []
            for start, n in _runs(LANES * b, live, _win_of_cat):
                while n > 0:
                    d, o = divmod(start, shard)
                    take = min(n, shard - o)
                    parts.append(x_ref[d, :, o:o + take])
                    start, n = start + take, n - take
            if live < LANES:
                parts.append(jnp.zeros((RELAYOUT_ROWS, LANES - live), g_in.dtype))
            o_ref[:, LANES * b:LANES * (b + 1)] = parts[0] if len(parts) == 1 else jnp.concatenate(parts, axis=1)

    return pl.pallas_call(
        body, name=name, grid=(rows // RELAYOUT_ROWS,),
        in_specs=[pl.BlockSpec((n_dev, RELAYOUT_ROWS, shard), lambda i: (0, i, 0))] + ([ANY_SPEC] if into is not None else []),
        out_specs=pl.BlockSpec((RELAYOUT_ROWS, CAT_WIDTH), lambda i: (first + i, 0)),
        out_shape=jax.ShapeDtypeStruct((total_rows, CAT_WIDTH), g_in.dtype),
        input_output_aliases={1: 0} if into is not None else {},
        compiler_params=_params(("parallel",)))(*((g_in,) if into is None else (g_in, into)))


def cat_to_shards(dw_cat, shard):
    rows = dw_cat.shape[0]

    def body(x_ref, o_ref):
        for d in range(N_DEV):
            for t0 in range(0, shard, LANES):
                width = min(LANES, shard - t0)
                parts = [x_ref[:, c:c + n] for c, n in _runs(d * shard + t0, width, _cat_of_win)]
                o_ref[d, :, t0:t0 + width] = parts[0] if len(parts) == 1 else jnp.concatenate(parts, axis=1)

    return pl.pallas_call(
        body, name="cat_to_shards", grid=(rows // RELAYOUT_ROWS,),
        in_specs=[pl.BlockSpec((RELAYOUT_ROWS, CAT_WIDTH), lambda i: (i, 0))],
        out_specs=pl.BlockSpec((N_DEV, RELAYOUT_ROWS, shard), lambda i: (0, i, 0)),
        out_shape=jax.ShapeDtypeStruct((N_DEV, rows, shard), dw_cat.dtype),
        compiler_params=_params(("parallel",)))(dw_cat)


ROW_BLOCK = 512


def rms_fwd(x, w, name):
    t, d = x.shape

    def body(x_ref, w_ref, n_ref, r_ref):
        h = x_ref[...]
        r = lax.rsqrt(jnp.mean(h * h, axis=-1, keepdims=True) + NORM_EPS)
        n_ref[...] = (h * r * w_ref[...]).astype(BF16)
        r_ref[...] = r

    row = pl.BlockSpec((ROW_BLOCK, d), lambda i: (i, 0))
    return pl.pallas_call(
        body, name=name, grid=(t // ROW_BLOCK,),
        in_specs=[row, pl.BlockSpec((1, d), lambda i: (0, 0))],
        out_specs=[row, pl.BlockSpec((ROW_BLOCK, 1), lambda i: (i, 0))],
        out_shape=[jax.ShapeDtypeStruct((t, d), BF16), jax.ShapeDtypeStruct((t, 1), F32)],
        compiler_params=_params(("parallel",)))(x, w)


FUSED_ROWS = 512


def out_proj_rms(y, w_out, x, w_norm, name):
    t, d = x.shape

    def body(y_ref, w_ref, x_ref, g_ref, h_ref, n_ref, r_ref):
        h = x_ref[...] + _dot(y_ref[...], w_ref[...], NN)
        r = lax.rsqrt(jnp.mean(h * h, axis=-1, keepdims=True) + NORM_EPS)
        h_ref[...] = h
        n_ref[...] = (h * r * g_ref[...]).astype(BF16)
        r_ref[...] = r

    row = pl.BlockSpec((FUSED_ROWS, d), lambda i: (i, 0))
    return pl.pallas_call(
        body, name=name, grid=(t // FUSED_ROWS,),
        in_specs=[pl.BlockSpec((FUSED_ROWS, y.shape[1]), lambda i: (i, 0)), pl.BlockSpec(w_out.shape, lambda i: (0, 0)),
                  row, pl.BlockSpec((1, d), lambda i: (0, 0))],
        out_specs=[row, row, pl.BlockSpec((FUSED_ROWS, 1), lambda i: (i, 0))],
        out_shape=[jax.ShapeDtypeStruct((t, d), F32), jax.ShapeDtypeStruct((t, d), BF16),
                   jax.ShapeDtypeStruct((t, 1), F32)],
        compiler_params=_params(("parallel",)))(y, w_out, x, w_norm)


def ff2_loss(act, w_ff2, h1, w, target, name, tk=2048):
    t, d = h1.shape
    ksteps = act.shape[1] // tk

    def body(a_ref, b_ref, h_ref, w_ref, t_ref, loss_ref, dhb_ref, dw_ref, acc_ref):
        i, kk = pl.program_id(0), pl.program_id(1)

        @pl.when((i == 0) & (kk == 0))
        def _():
            loss_ref[...] = jnp.zeros_like(loss_ref)
            dw_ref[...] = jnp.zeros_like(dw_ref)

        @pl.when(kk == 0)
        def _():
            acc_ref[...] = h_ref[...]

        acc_ref[...] += _dot(a_ref[...], b_ref[...], NN)

        @pl.when(kk == ksteps - 1)
        def _():
            h = acc_ref[...]
            wv = w_ref[...]
            r = lax.rsqrt(jnp.mean(h * h, axis=-1, keepdims=True) + NORM_EPS)
            yn = h * r
            e = yn * wv - t_ref[...]
            loss_ref[...] += 0.5 * jnp.sum(jnp.sum(e * e, axis=-1, keepdims=True), axis=0, keepdims=True) / d
            dy = e / d
            dw_ref[...] += jnp.sum(dy * yn, axis=0, keepdims=True)
            dyn = dy * wv
            dhb_ref[...] = (r * (dyn - yn * jnp.mean(dyn * yn, axis=-1, keepdims=True))).astype(BF16)

    row = pl.BlockSpec((FUSED_ROWS, d), lambda i, k: (i, 0))
    wspec = pl.BlockSpec((1, d), lambda i, k: (0, 0))
    return pl.pallas_call(
        body, name=name, grid=(t // FUSED_ROWS, ksteps),
        in_specs=[pl.BlockSpec((FUSED_ROWS, tk), lambda i, k: (i, k)), pl.BlockSpec((tk, d), lambda i, k: (k, 0)),
                  row, wspec, row],
        out_specs=[pl.BlockSpec((1, 1), lambda i, k: (0, 0)), row, wspec],
        out_shape=[jax.ShapeDtypeStruct((1, 1), F32), jax.ShapeDtypeStruct((t, d), BF16),
                   jax.ShapeDtypeStruct((1, d), F32)],
        scratch_shapes=[pltpu.VMEM((FUSED_ROWS, d), F32)],
        compiler_params=_params(("arbitrary", "arbitrary")))(act, w_ff2, h1, w, target)


def rms_bwd(h, r, w, dn, dres, out_dtype, name):
    t, d = h.shape

    def body(h_ref, r_ref, w_ref, dn_ref, dres_ref, dh_ref, dw_ref):
        @pl.when(pl.program_id(0) == 0)
        def _():
            dw_ref[...] = jnp.zeros_like(dw_ref)

        rv = r_ref[...]
        yn = h_ref[...] * rv
        dnv = dn_ref[...].astype(F32)
        dw_ref[...] += jnp.sum(dnv * yn, axis=0, keepdims=True)
        dyn = dnv * w_ref[...]
        dh = dres_ref[...].astype(F32) + rv * (dyn - yn * jnp.mean(dyn * yn, axis=-1, keepdims=True))
        dh_ref[...] = dh.astype(out_dtype)

    row = pl.BlockSpec((ROW_BLOCK, d), lambda i: (i, 0))
    wspec = pl.BlockSpec((1, d), lambda i: (0, 0))
    rspec = pl.BlockSpec((ROW_BLOCK, 1), lambda i: (i, 0))
    return pl.pallas_call(
        body, name=name, grid=(t // ROW_BLOCK,),
        in_specs=[row, rspec, wspec, row, row], out_specs=[row, wspec],
        out_shape=[jax.ShapeDtypeStruct((t, d), out_dtype), jax.ShapeDtypeStruct((1, d), F32)],
        compiler_params=_params(("arbitrary",)))(h, r, w, dn, dres)


CONV_ROWS = 512
TILE_ROWS = 8


def _iota2(shape, axis):
    return lax.broadcasted_iota(jnp.int32, shape, axis)


def _silu(x):
    return x * jax.nn.sigmoid(x)


def _conv_rows(x_ref, w, first, rows):
    acc = None
    for j in range(4):
        term = x_ref[first - 3 + j:first - 3 + j + rows, :] * w[j:j + 1, :]
        acc = term if acc is None else acc + term
    return acc


def _head_shifts(head):
    rows = _iota2((TILE_ROWS, 1), 0)
    return [jnp.where(rows >= 3 - j, head if j == 3 else pltpu.roll(head, 3 - j, 0), 0.0) for j in range(4)]


def _conv_chunks(t):
    pieces = [(TILE_ROWS, min(CONV_ROWS, t) - TILE_ROWS)]
    pieces += [(r, CONV_ROWS) for r in range(CONV_ROWS, t, CONV_ROWS)]
    return pieces


def conv_fwd(proj, conv_w, name):
    t = proj.shape[0]

    def body(x_ref, w_ref, o_ref):
        w = w_ref[...]
        shifted = _head_shifts(x_ref[0:TILE_ROWS, :])
        o_ref[0:TILE_ROWS, :] = _silu(sum(shifted[j] * w[j:j + 1, :] for j in range(4)))
        for first, rows in _conv_chunks(t):
            o_ref[first:first + rows, :] = _silu(_conv_rows(x_ref, w, first, rows))

    col = pl.BlockSpec((t, LANES), lambda c: (0, c))
    return pl.pallas_call(
        body, name=name, grid=(QKV_WIDTH // LANES,),
        in_specs=[col, pl.BlockSpec((4, LANES), lambda c: (0, c))], out_specs=col,
        out_shape=jax.ShapeDtypeStruct((t, QKV_WIDTH), F32),
        compiler_params=_params(("parallel",)))(proj, conv_w)


def conv_bwd(proj, dout, conv_w, dproj, name):
    t = proj.shape[0]

    def dsilu(pre):
        sg = jax.nn.sigmoid(pre)
        return sg * (1.0 + pre * (1.0 - sg))

    def body(x_ref, d_ref, w_ref, dproj_in, dx_ref, dw_ref, stage):
        del dproj_in
        w = w_ref[...]
        shifted = _head_shifts(x_ref[0:TILE_ROWS, :])
        head_dpre = d_ref[0:TILE_ROWS, :] * dsilu(sum(shifted[j] * w[j:j + 1, :] for j in range(4)))
        stage[0:TILE_ROWS, :] = head_dpre
        for first, rows in _conv_chunks(t):
            stage[first:first + rows, :] = d_ref[first:first + rows, :] * dsilu(_conv_rows(x_ref, w, first, rows))
        stage[t:t + TILE_ROWS, :] = jnp.zeros((TILE_ROWS, LANES), F32)
        for first, rows in [(0, TILE_ROWS)] + _conv_chunks(t):
            dx = None
            for j in range(4):
                term = stage[first + 3 - j:first + 3 - j + rows, :] * w[j:j + 1, :]
                dx = term if dx is None else dx + term
            dx_ref[first:first + rows, :] = dx.astype(BF16)
        dw = [jnp.sum(head_dpre * shifted[j], axis=0, keepdims=True) for j in range(4)]
        for first, rows in _conv_chunks(t):
            dpre = stage[first:first + rows, :]
            for j in range(4):
                dw[j] = dw[j] + jnp.sum(dpre * x_ref[first - 3 + j:first - 3 + j + rows, :], axis=0, keepdims=True)
        dw_ref[...] = jnp.concatenate(dw, axis=0)

    col = pl.BlockSpec((t, LANES), lambda c: (0, c))
    taps = pl.BlockSpec((4, LANES), lambda c: (0, c))
    return pl.pallas_call(
        body, name=name, grid=(QKV_WIDTH // LANES,),
        in_specs=[col, col, taps, ANY_SPEC], out_specs=[col, taps],
        out_shape=[jax.ShapeDtypeStruct(dproj.shape, BF16), jax.ShapeDtypeStruct((4, QKV_WIDTH), F32)],
        scratch_shapes=[pltpu.VMEM((t + TILE_ROWS, LANES), F32)],
        input_output_aliases={3: 0},
        compiler_params=_params(("parallel",)))(proj, dout, conv_w, dproj)


def _softplus(x):
    return jnp.maximum(x, 0.0) + jnp.log(1.0 + jnp.exp(-jnp.abs(x)))


def _head_norm_gate(o, norm_w, gate):
    return o * lax.rsqrt(jnp.mean(o * o, axis=-1, keepdims=True) + NORM_EPS) * norm_w * _silu(gate)


GDN_PREC = ("bf", "bf")
HGRN_PREC = "bf"


def _each(fn, *cols):
    return [fn(*a) for a in zip(*cols)]


@functools.partial(jax.custom_vjp, nondiff_argnums=(2,))
def _known_inverse(low, inv, prec):
    del low, prec
    return inv


def _known_inverse_fwd(low, inv, prec):
    del low
    return inv, inv


def _known_inverse_bwd(prec, inv, ct):
    return -_mm_raw(_mm_raw(inv, ct, TN, prec), inv, NT, prec), jnp.zeros_like(inv)


_known_inverse.defvjp(_known_inverse_fwd, _known_inverse_bwd)


def gdn_stages(hs, qc, kc, vc, zc, ab, a_log_l, dt_l, norm_w, s, prec=GDN_PREC, inv_known=None):
    p_inv, p_mm = prec
    c = CHUNK
    ri, ci = _iota2((c, c), 0), _iota2((c, c), 1)
    incl, strict, eye = ri >= ci, ri > ci, ri == ci
    lane = _iota2((c, LANES), 1)
    last_row = _iota2((c, 1), 0) == c - 1
    rowsum = lambda x: jnp.sum(x, axis=1, keepdims=True)

    def row(col):
        return jnp.sum(jnp.where(eye, col, 0.0), axis=0, keepdims=True)

    q = _each(lambda x: x * lax.rsqrt(rowsum(x * x) + L2_EPS) * (HEAD_DIM ** -0.5), qc)
    k = _each(lambda x: x * lax.rsqrt(rowsum(x * x) + L2_EPS), kc)
    yield
    a_col = [rowsum(jnp.where(lane == h, ab, 0.0)) for h in hs]
    b_col = [rowsum(jnp.where(lane == h + N_HEADS, ab, 0.0)) for h in hs]
    beta = _each(jax.nn.sigmoid, b_col)
    g = _each(lambda a, al, dl: rowsum(jnp.where(lane == 0, -jnp.exp(al) * _softplus(a + dl), 0.0)), a_col, a_log_l, dt_l)
    gcum = _each(lambda x: rowsum(jnp.where(incl, row(x), 0.0)), g)
    g_last = _each(lambda x: jnp.sum(jnp.where(last_row, x, 0.0), axis=0, keepdims=True), gcum)
    decay = _each(lambda x: jnp.exp(jnp.where(incl, x - row(x), -jnp.inf)), gcum)
    yield
    kk = _each(lambda x: mm(x, x, NT, p_mm), k)
    low = _each(lambda b, x, d: jnp.where(strict, b * x * d, 0.0), beta, kk, decay)
    yield
    if inv_known is None:
        power = _each(lambda x: -x, low)
        inv = _each(lambda x: jnp.where(eye, 1.0, 0.0) + x, power)
        for _ in range(5):
            power = _each(lambda x: mm(x, x, NN, p_inv), power)
            yield
            inv = _each(lambda x, p: x + mm(x, p, NN, p_inv), inv, power)
            yield
    else:
        inv = _each(lambda x, known: _known_inverse(x, known, p_inv), low, inv_known)
    exp_g = _each(jnp.exp, gcum)
    yield
    u_v = _each(lambda i, b, x: mm(i, b * x, NN, p_mm), inv, beta, vc)
    w = _each(lambda i, b, e, x: mm(i, b * e * x, NN, p_mm), inv, beta, exp_g, k)
    yield
    attn = _each(lambda x, y, d: mm(x, y, NT, p_mm) * d, q, k, decay)
    yield
    u = _each(lambda x, y, z: x - mm(y, z, NN, p_mm), u_v, w, s)
    yield
    o = _each(lambda x, e, z: mm(x * e, z, NN, p_mm), q, exp_g, s)
    o = _each(lambda x, a, y: x + mm(a, y, NN, p_mm), o, attn, u)
    yield
    k_end = _each(lambda x, gl, gc: x * jnp.exp(gl - gc), k, g_last, gcum)
    s_new = _each(lambda z, gl, x, y: z * jnp.exp(gl) + mm(x, y, TN, p_mm), s, g_last, k_end, u)
    return (_each(lambda x, z: _head_norm_gate(x, norm_w, z), o, zc), s_new), inv


def gdn_chunk(h, qc, kc, vc, zc, ab, a_log_l, dt_l, norm_w, s, prec=GDN_PREC, reuse_inverse=False):
    args = ([h], [qc], [kc], [vc], [zc], ab, [a_log_l], [dt_l], norm_w, [s], prec)
    if reuse_inverse:
        inv = lax.stop_gradient(gdn_chunks(*args)[1])
        (y, s_new), _ = gdn_chunks(*args, inv_known=inv)
    else:
        (y, s_new), _ = gdn_chunks(*args)
    return y[0], s_new[0]


DIAG_ROWS = SUB_CHUNK // 2
SHIFT_PAD = 8
SHIFT_ROWS = SHIFT_PAD + CHUNK + SHIFT_PAD
SHIFT_WAYS = 4


class RolledRows:
    def down(self, x, which):
        del which
        return [x] + [pltpu.roll(x, off, 0) for off in range(1, DIAG_ROWS)]

    def up_sum(self, parts, which):
        del which
        acc = parts[0]
        for off in range(1, DIAG_ROWS):
            acc = acc + pltpu.roll(parts[off], CHUNK - off, 0)
        return acc


class SlotRows:
    def __init__(self, slots):
        self.slots = slots

    def down(self, x, which):
        self.slots[which, 0, SHIFT_PAD:SHIFT_PAD + CHUNK, :] = x
        return [x] + [self.slots[which, 0, SHIFT_PAD - off:SHIFT_PAD + CHUNK - off, :] for off in range(1, DIAG_ROWS)]

    def up_sum(self, parts, which):
        acc = parts[0]
        for off in range(1, DIAG_ROWS):
            way = 1 + off % (SHIFT_WAYS - 1)
            self.slots[which, way, SHIFT_PAD:SHIFT_PAD + CHUNK, :] = parts[off]
            acc = acc + self.slots[which, way, SHIFT_PAD + off:SHIFT_PAD + CHUNK + off, :]
        return acc


def _sub_block_rows():
    return jnp.bitwise_and(_iota2((CHUNK, 1), 0), DIAG_ROWS - 1)


def _diag_forward(rows, q, key, bc, v):
    rmod = _sub_block_rows()
    k_d, b_d, v_d = rows.down(key, 0), rows.down(bc, 1), rows.down(v, 2)
    o = None
    for off in range(DIAG_ROWS):
        e = jnp.exp(jnp.where(rmod >= off, bc - b_d[off], -jnp.inf))
        term = jnp.sum(q * k_d[off] * e, axis=-1, keepdims=True) * v_d[off]
        o = term if o is None else o + term
    return o


def _diag_backward(rows, q, key, bc, v, do):
    rmod = _sub_block_rows()
    k_d, b_d, v_d = rows.down(key, 0), rows.down(bc, 1), rows.down(v, 2)
    dq = db = None
    dk_parts, db_parts, dv_parts = [], [], []
    for off in range(DIAG_ROWS):
        e = jnp.exp(jnp.where(rmod >= off, bc - b_d[off], -jnp.inf))
        qe = q * e
        a = jnp.sum(qe * k_d[off], axis=-1, keepdims=True)
        da = jnp.sum(do * v_d[off], axis=-1, keepdims=True)
        dv_parts.append(a * do)
        dq_term = (da * e) * k_d[off]
        dk_term = da * qe
        s = dk_term * k_d[off]
        dq = dq_term if dq is None else dq + dq_term
        db = s if db is None else db + s
        dk_parts.append(dk_term)
        db_parts.append(s)
    return dq, rows.up_sum(dk_parts, 0), db - rows.up_sum(db_parts, 1), rows.up_sum(dv_parts, 2)


def diag_part(rows, differentiable=True):
    forward = functools.partial(_diag_forward, rows)
    if not differentiable:
        return forward
    part = jax.custom_vjp(forward)
    part.defvjp(lambda q, key, bc, v: (forward(q, key, bc, v), (q, key, bc, v)),
                lambda res, do: _diag_backward(rows, *res, do))
    return part


def hgrn_stages(qb, fb, ib, gb, l0, l1, norm_w, st, prec=HGRN_PREC, diags=None, o_known=None):
    c = CHUNK
    ri, ci = _iota2((4 * c, c), 0), _iota2((4 * c, c), 1)
    rcol = _iota2((c, 1), 0)
    blk0 = jnp.bitwise_and(ri, c - SUB_CHUNK)
    limit = jnp.where(ri < c, ri + 1, jnp.where(ri < 2 * c, blk0, jnp.where(ri < 3 * c, blk0 + SUB_CHUNK,
                                                                          blk0 + DIAG_ROWS)))
    sel = jnp.where(ci < limit, 1.0, 0.0)
    ri, ci = _iota2((c, c), 0), _iota2((c, c), 1)
    lb = _each(lambda a, b: jax.nn.sigmoid(a - b), l0, l1)
    forget = _each(lambda b, f: b + (1.0 - b) * jax.nn.sigmoid(f), lb, fb)
    key = _each(lambda b, f: (1.0 - b) * jax.nn.sigmoid(-f), lb, fb)
    q = _each(_silu, qb)
    v = ib
    logf = _each(jnp.log, forget)
    sums = _each(lambda x: sel_sums(sel, x), logf)
    bc, b_start, b_end, b_half = ([x[i] for x in sums] for i in range(4))
    b_last = _each(lambda x: jnp.sum(x, axis=0, keepdims=True), logf)
    o = _each(lambda x, b, z: mm(x * jnp.exp(b), z, NT, prec), q, bc, st)
    if diags is None:
        diags = [diag_part(RolledRows())] * len(qb)
    yield
    o = list(o)
    for h in range(len(o)):
        o[h] = o[h] + diags[h](q[h], key[h], bc[h], v[h])
        yield
    second = jnp.bitwise_and(rcol, SUB_CHUNK - 1) >= DIAG_ROWS
    same_sub = jnp.bitwise_and(ri, c - SUB_CHUNK) == jnp.bitwise_and(ci, c - SUB_CHUNK)
    q_half = _each(lambda x, b, bh: x * jnp.exp(jnp.where(second, b - bh, -jnp.inf)), q, bc, b_half)
    k_half = _each(lambda x, b, bh: x * jnp.exp(jnp.where(second, -jnp.inf, bh - b)), key, bc, b_half)
    a_half = _each(lambda x, z: jnp.where(same_sub, mm(x, z, NT, prec), 0.0), q_half, k_half)
    o = _each(lambda acc, a, val: acc + mm(a, val, NN, prec), o, a_half, v)
    yield
    q_rel = _each(lambda x, b, bs: x * jnp.exp(b - bs), q, bc, b_start)
    k_rel = _each(lambda x, b, be: x * jnp.exp(be - b), key, bc, b_end)
    for y in range(c // SUB_CHUNK - 1):
        def scaled(x, b, bs):
            end_y = jnp.sum(jnp.where(rcol == SUB_CHUNK * y + SUB_CHUNK - 1, b, 0.0), axis=0, keepdims=True)
            return x * jnp.exp(jnp.where(rcol >= SUB_CHUNK * (y + 1), bs - end_y, -jnp.inf))
        dq = _each(scaled, q_rel, bc, b_start)
        in_y = (ci >= SUB_CHUNK * y) & (ci < SUB_CHUNK * (y + 1))
        a_y = _each(lambda x, z: jnp.where(in_y, mm(x, z, NT, prec), 0.0), dq, k_rel)
        o = _each(lambda acc, a, val: acc + mm(a, val, NN, prec), o, a_y, v)
        yield
    k_state = _each(lambda x, bl, b: x * jnp.exp(bl - b), key, b_last, bc)
    st_new = _each(lambda z, bl, val, x: z * jnp.exp(bl) + mm(val, x, TN, prec), st, b_last, v, k_state)
    if o_known is not None:
        o = _each(_known_value, o, o_known)
    return (_each(lambda x, z: _head_norm_gate(x, norm_w, z), o, gb), st_new), o


def _drain(gen):
    try:
        while True:
            next(gen)
    except StopIteration as done:
        return done.value


def _alternate(gen_a, gen_b):
    out, live = [None, None], [gen_a, gen_b]
    while any(g is not None for g in live):
        for i, g in enumerate(live):
            if g is None:
                continue
            try:
                next(g)
            except StopIteration as done:
                out[i], live[i] = done.value, None
    return out


def gdn_chunks(*args, **kwargs):
    return _drain(gdn_stages(*args, **kwargs))


def hgrn_chunks(*args, **kwargs):
    return _drain(hgrn_stages(*args, **kwargs))


def hgrn_chunk(qb, fb, ib, gb, l0, l1, norm_w, st, prec=HGRN_PREC, reuse_output=False):
    args = ([qb], [fb], [ib], [gb], [l0], [l1], norm_w, [st], prec)
    if reuse_output:
        known = lax.stop_gradient(hgrn_chunks(*args)[1])
        (y, st_new), _ = hgrn_chunks(*args, o_known=known)
    else:
        (y, st_new), _ = hgrn_chunks(*args)
    return y[0], st_new[0]


HEAD_VEC = (N_HEADS, 1, LANES)


class _ChunkSpecs:
    def __init__(self, nc, rev):
        self.nc, self.rev = nc, rev

    def _c(self, c):
        return self.nc - 1 - c if self.rev else c

    def row(self, width, block=0):
        return pl.BlockSpec((CHUNK, width), lambda c: (self._c(c), block))

    def per_head(self, rows):
        return pl.BlockSpec((None, N_HEADS, rows, rows), lambda c: (self._c(c), 0, 0, 0))

    @staticmethod
    def whole(shape):
        return pl.BlockSpec(shape, lambda c: (0,) * len(shape))


def _lanes(j):
    return slice(j * LANES, (j + 1) * LANES)


def mixer_fwd(qkv_c, proj, a_log_l, dt_l, gdn_norm_w, l0, l1, hgrn_norm_w, name):
    t = qkv_c.shape[0]
    hb = N_HEADS
    sp = _ChunkSpecs(t // CHUNK, rev=False)
    hs = list(range(hb))

    def body(q_ref, k_ref, v_ref, z_ref, ab_ref, al_ref, dt_ref, gnw_ref, qb_ref, fb_ref, ib_ref, gb_ref, l0_ref, l1_ref,
             hnw_ref, y_ref, hist_a_ref, inv_ref, hist_b_ref, o_ref, sa_ref, sb_ref, shift_ref):
        @pl.when(pl.program_id(0) == 0)
        def _():
            sa_ref[...] = jnp.zeros_like(sa_ref)
            sb_ref[...] = jnp.zeros_like(sb_ref)
            shift_ref[...] = jnp.zeros_like(shift_ref)

        heads = lambda ref: [ref[:, _lanes(j)] for j in hs]
        s_a, s_b = [sa_ref[h] for h in hs], [sb_ref[h] for h in hs]
        for h in hs:
            hist_a_ref[h] = s_a[h]
            hist_b_ref[h] = s_b[h]
        diags = [diag_part(SlotRows(shift_ref.at[h]), differentiable=False) for h in hs]
        ((y_a, s_a_new), inv), ((y_b, s_b_new), o_pre) = _alternate(
            gdn_stages(hs, heads(q_ref), heads(k_ref), heads(v_ref), heads(z_ref), ab_ref[...],
                       [al_ref[h] for h in hs], [dt_ref[h] for h in hs], gnw_ref[...], s_a),
            hgrn_stages(heads(qb_ref), heads(fb_ref), heads(ib_ref), heads(gb_ref),
                        [l0_ref[h] for h in hs], [l1_ref[h] for h in hs], hnw_ref[...], s_b, diags=diags))
        for h in hs:
            y_ref[:, _lanes(h)] = y_a[h].astype(BF16)
            y_ref[:, _lanes(hb + h)] = y_b[h].astype(BF16)
            o_ref[:, _lanes(h)] = o_pre[h]
            sa_ref[h] = s_a_new[h]
            sb_ref[h] = s_b_new[h]
            inv_ref[h] = inv[h]

    vec, gain, slab = sp.whole(HEAD_VEC), sp.whole((1, LANES)), functools.partial(sp.row, GDN_WIDTH)
    states = jax.ShapeDtypeStruct((sp.nc, N_HEADS, HEAD_DIM, HEAD_DIM), F32)
    return pl.pallas_call(
        body, name=name, grid=(sp.nc,),
        in_specs=[slab(0), slab(1), slab(2), slab(3), sp.row(LANES, AB_BLOCK), vec, vec, gain,
                  slab(4), slab(5), slab(6), slab(7), vec, vec, gain],
        out_specs=[sp.row(2 * GDN_WIDTH), sp.per_head(HEAD_DIM), sp.per_head(CHUNK), sp.per_head(HEAD_DIM), slab(0)],
        out_shape=[jax.ShapeDtypeStruct((t, 2 * GDN_WIDTH), BF16), states,
                   jax.ShapeDtypeStruct((sp.nc, N_HEADS, CHUNK, CHUNK), F32), states,
                   jax.ShapeDtypeStruct((t, GDN_WIDTH), F32)],
        scratch_shapes=[pltpu.VMEM((N_HEADS, HEAD_DIM, HEAD_DIM), F32), pltpu.VMEM((N_HEADS, HEAD_DIM, HEAD_DIM), F32),
                        pltpu.VMEM((hb, 3, SHIFT_WAYS, SHIFT_ROWS, LANES), F32)],
        compiler_params=_params(("arbitrary",)),
    )(qkv_c, qkv_c, qkv_c, proj, proj, a_log_l, dt_l, gdn_norm_w, proj, proj, proj, proj, l0, l1, hgrn_norm_w)


def mixer_bwd(qkv_c, proj, a_log_l, dt_l, gdn_norm_w, l0, l1, hgrn_norm_w, hist_a, inv_hist, hist_b, o_pre, dy, name):
    t = qkv_c.shape[0]
    hb = N_HEADS
    sp = _ChunkSpecs(t // CHUNK, rev=True)
    hs = list(range(hb))

    def body(q_ref, k_ref, v_ref, z_ref, ab_ref, al_ref, dt_ref, gnw_ref, qb_ref, fb_ref, ib_ref, gb_ref, l0_ref, l1_ref,
             hnw_ref, hist_a_ref, inv_ref, hist_b_ref, o_ref, dy_ref,
             dqkv_ref, dproj_ref, dal_ref, ddt_ref, dgnw_ref, dl0_ref, dl1_ref, dhnw_ref, dsa_ref, dsb_ref, shift_ref):
        @pl.when(pl.program_id(0) == 0)
        def _():
            for ref in (dal_ref, ddt_ref, dgnw_ref, dl0_ref, dl1_ref, dhnw_ref, dsa_ref, dsb_ref, shift_ref):
                ref[...] = jnp.zeros_like(ref)

        heads = lambda ref, first=0: [ref[:, _lanes(first + j)] for j in hs]
        diags = [diag_part(SlotRows(shift_ref.at[h])) for h in hs]
        inv_known, o_known = [inv_ref[h] for h in hs], heads(o_ref)

        def both(ga, gb):
            (ra, inv), (rb, o_pre) = _alternate(gdn_stages(hs, *ga, inv_known=inv_known),
                                                hgrn_stages(*gb, diags=diags, o_known=o_known))
            return (ra, rb), (inv, o_pre)

        ga = (heads(q_ref), heads(k_ref), heads(v_ref), heads(z_ref), ab_ref[...], [al_ref[h] for h in hs],
              [dt_ref[h] for h in hs], gnw_ref[...], [hist_a_ref[h] for h in hs])
        gb = (heads(qb_ref), heads(fb_ref), heads(ib_ref), heads(gb_ref), [l0_ref[h] for h in hs],
              [l1_ref[h] for h in hs], hnw_ref[...], [hist_b_ref[h] for h in hs])
        _, vjp, _ = jax.vjp(both, ga, gb, has_aux=True)
        dy_a = [x.astype(F32) for x in heads(dy_ref)]
        dy_b = [x.astype(F32) for x in heads(dy_ref, hb)]
        (dq, dk, dv, dz, dab, dal, ddt, dgnw, ds_a), (dqb, dfb, dib, dgb, dl0, dl1, dhnw, ds_b) = vjp(
            ((dy_a, [dsa_ref[h] for h in hs]), (dy_b, [dsb_ref[h] for h in hs])))
        for h in hs:
            dqkv_ref[:, _lanes(h)] = dq[h]
            dqkv_ref[:, _lanes(hb + h)] = dk[h]
            dqkv_ref[:, _lanes(2 * hb + h)] = dv[h]
            for slab, val in enumerate((dz, dqb, dfb, dib, dgb)):
                dproj_ref[:, _lanes((3 + slab) * hb + h)] = val[h].astype(BF16)
            dal_ref[h] += dal[h]
            ddt_ref[h] += ddt[h]
            dl0_ref[h] += dl0[h]
            dl1_ref[h] += dl1[h]
            dsa_ref[h] = ds_a[h]
            dsb_ref[h] = ds_b[h]
        dproj_ref[:, MAIN_WIDTH:] = dab.astype(BF16)
        dgnw_ref[...] += dgnw
        dhnw_ref[...] += dhnw

    vec, gain, slab = sp.whole(HEAD_VEC), sp.whole((1, LANES)), functools.partial(sp.row, GDN_WIDTH)
    vec_shape, gain_shape = jax.ShapeDtypeStruct(HEAD_VEC, F32), jax.ShapeDtypeStruct((1, LANES), F32)
    return pl.pallas_call(
        body, name=name, grid=(sp.nc,),
        in_specs=[slab(0), slab(1), slab(2), slab(3), sp.row(LANES, AB_BLOCK), vec, vec, gain,
                  slab(4), slab(5), slab(6), slab(7), vec, vec, gain,
                  sp.per_head(HEAD_DIM), sp.per_head(CHUNK), sp.per_head(HEAD_DIM), slab(0), sp.row(2 * GDN_WIDTH)],
        out_specs=[sp.row(QKV_WIDTH), sp.row(CAT_WIDTH), vec, vec, gain, vec, vec, gain],
        out_shape=[jax.ShapeDtypeStruct((t, QKV_WIDTH), F32), jax.ShapeDtypeStruct((t, CAT_WIDTH), BF16),
                   vec_shape, vec_shape, gain_shape, vec_shape, vec_shape, gain_shape],
        scratch_shapes=[pltpu.VMEM((N_HEADS, HEAD_DIM, HEAD_DIM), F32), pltpu.VMEM((N_HEADS, HEAD_DIM, HEAD_DIM), F32),
                        pltpu.VMEM((hb, 3, SHIFT_WAYS, SHIFT_ROWS, LANES), F32)],
        compiler_params=_params(("arbitrary",)),
    )(qkv_c, qkv_c, qkv_c, proj, proj, a_log_l, dt_l, gdn_norm_w, proj, proj, proj, proj, l0, l1, hgrn_norm_w,
      hist_a, inv_hist, hist_b, o_pre, dy)


def _adamw(w, g, m, v):
    m = ADAM_B1 * m + (1.0 - ADAM_B1) * g
    v = ADAM_B2 * v + (1.0 - ADAM_B2) * jnp.square(g)
    m_hat = m / (1.0 - ADAM_B1 ** ADAM_STEP)
    v_hat = v / (1.0 - ADAM_B2 ** ADAM_STEP)
    delta = -ADAM_LR * (m_hat / (jnp.sqrt(v_hat) + ADAM_EPS) + ADAM_WD * w)
    return delta, m, v


def adamw_reduce(parts, w, m, v, name, rb=128):
    r, c = w.shape
    rb = min(rb, r)
    n_parts = parts.shape[0]

    def body(p_ref, w_ref, m_ref, v_ref, g_ref, d_ref, mo_ref, vo_ref):
        g = p_ref[0].astype(F32)
        for d in range(1, n_parts):
            g = g + p_ref[d].astype(F32)
        delta, mn, vn = _adamw(w_ref[...], g, m_ref[...], v_ref[...])
        g_ref[...] = g
        d_ref[...] = delta
        mo_ref[...] = mn
        vo_ref[...] = vn

    blk = pl.BlockSpec((rb, c), lambda i: (i, 0))
    return pl.pallas_call(
        body, name=name, grid=(r // rb,),
        in_specs=[pl.BlockSpec((n_parts, rb, c), lambda i: (0, i, 0)), blk, blk, blk],
        out_specs=[blk] * 4, out_shape=[jax.ShapeDtypeStruct((r, c), F32)] * 4,
        compiler_params=_params(("parallel",)))(parts, w, m, v)


def adamw_small(w, g, m, v, name):
    def body(w_ref, g_ref, m_ref, v_ref, d_ref, mo_ref, vo_ref):
        delta, mn, vn = _adamw(w_ref[...], g_ref[...], m_ref[...], v_ref[...])
        d_ref[...] = delta
        mo_ref[...] = mn
        vo_ref[...] = vn

    vmem = pl.BlockSpec(memory_space=pltpu.VMEM)
    return pl.pallas_call(body, name=name, in_specs=[vmem] * 4, out_specs=[vmem] * 3,
                          out_shape=[jax.ShapeDtypeStruct(w.shape, F32)] * 3)(w, g, m, v)


def _pack(arrays):
    flat = jnp.concatenate([a.reshape(-1).astype(F32) for a in arrays])
    rows = -(-flat.shape[0] // (8 * LANES)) * 8
    return jnp.pad(flat, (0, rows * LANES - flat.shape[0])).reshape(rows, LANES)


def _unpack(packed, shapes):
    flat, out, off = packed.reshape(-1), [], 0
    for s in shapes:
        n = 1
        for d in s:
            n *= d
        out.append(flat[off:off + n].reshape(s))
        off += n
    return out


def _relu2_epilogue(acc, _):
    r = jnp.maximum(acc, 0.0)
    return acc, r * r


def _relu2_bwd_epilogue(acc, a1):
    return (acc * (2.0 * jnp.maximum(a1, 0.0)),)


def kernel(x, w_in, conv_w, gdn_a_log, gdn_dt_bias, gdn_norm_w, hgrn_lb_logits, hgrn_norm_w, w_out, norm_mix_w, norm_ffn_w, w_ff1, w_ff2, norm_final_w, loss_target, m_w_in, m_conv_w, m_gdn_a_log, m_gdn_dt_bias, m_gdn_norm_w, m_hgrn_lb_logits, m_hgrn_norm_w, m_w_out, m_norm_mix_w, m_norm_ffn_w, m_w_ff1, m_w_ff2, m_norm_final_w, v_w_in, v_conv_w, v_gdn_a_log, v_gdn_dt_bias, v_gdn_norm_w, v_hgrn_lb_logits, v_hgrn_norm_w, v_w_out, v_norm_mix_w, v_norm_ffn_w, v_w_ff1, v_w_ff2, v_norm_final_w):
    me = _my_flat()
    xs = x[0]
    target = loss_target[0]
    shard_in = w_in.shape[2]
    shard_conv = conv_w.shape[2]

    tok = lambda t: t[0:1, 0:1]

    half = D_MODEL // 2
    w_in_b = w_in[0].astype(BF16)
    g_in_a, g_conv = gather_two_level([w_in_b[:half], conv_w[0]], "gather_w_in")
    h_g0, t_g0 = exchange_start([w_in_b[half:]], True, "gather_w_in_low_start", after=[g_in_a], peers=CHIP_PEERS)
    h_g1, t_g1 = exchange_start([w_out[0].astype(BF16), w_ff1[0].astype(BF16)], True, "gather_mid_start", after=[t_g0],
                                peers=CHIP_PEERS)
    h_g2, t_g2 = exchange_start([w_ff2[0].astype(BF16)], True, "gather_ff2_start", after=[t_g1], peers=CHIP_PEERS)
    w_cat = weights_to_cat(g_in_a, "weights_to_cat", D_MODEL)
    conv_full = jnp.transpose(g_conv, (1, 0, 2)).reshape(4, QKV_WIDTH)

    lane_b = lambda p: jnp.broadcast_to(p.reshape(N_HEADS, 1, 1), HEAD_VEC)
    a_log_l, dt_l = lane_b(gdn_a_log[0]), lane_b(gdn_dt_bias[0])
    l0 = hgrn_lb_logits[0].reshape(HEAD_VEC)
    l1 = hgrn_lb_logits[1].reshape(HEAD_VEC)

    n1, r1 = rms_fwd(xs, norm_mix_w + tok(t_g1) + tok(t_g2), "rms_mix")
    proj = matmul(n1, w_cat, "nn", "in_proj_high", (BF16,), tn=CAT_WIDTH // 5, tk=half, k_blocks=(0, 1))
    _, (l_low,) = exchange_wait(h_g0, "gather_w_in_low_wait", after=[proj], copies=len(CHIP_PEERS))
    h_f0, _ = forward_start([l_low], "gather_w_in_low_forward")
    _, (l_low,) = exchange_wait(_one(h_f0, 0), "forward_w_in_low_wait", copies=len(OTHER_CHIPS))
    w_cat = weights_to_cat(l_low, "weights_to_cat_low", D_MODEL, row0=half, into=w_cat)
    proj = matmul(n1, w_cat, "nn", "in_proj_low", tn=CAT_WIDTH // 5, tk=half, k_blocks=(1, 1), extra=proj,
                  epilogue=lambda acc, high: (acc + high,))
    qkv_c = conv_fwd(proj, conv_full, "conv_fwd")
    y, hist_a, inv_a, hist_b, o_b = mixer_fwd(qkv_c, proj, a_log_l, dt_l, gdn_norm_w, l0, l1, hgrn_norm_w, "mixer_fwd")
    _, (l_out, l_ff1) = exchange_wait(h_g1, "gather_mid_wait", after=[y], copies=len(CHIP_PEERS))
    _, (l_ff2,) = exchange_wait(h_g2, "gather_ff2_wait", after=[y], copies=len(CHIP_PEERS))
    h_fw, _ = forward_start([l_out, l_ff1, l_ff2], "gather_forward_start")
    _, (l_out,) = exchange_wait(_one(h_fw, 0), "forward_out_wait", copies=len(OTHER_CHIPS))
    w_out_full = l_out.reshape(D_MODEL, D_MODEL)
    h1, n2, r2 = out_proj_rms(y, w_out_full, xs, norm_ffn_w, "out_proj_rms")
    _, (w_ff1_sh,) = exchange_wait(_one(h_fw, 1), "forward_ff1_wait", after=[n2], copies=len(OTHER_CHIPS))
    a1, act = matmul(n2, w_ff1_sh, "nn", "ff1", out_dtypes=(F32, BF16), epilogue=_relu2_epilogue, b_shards=True)
    _, (l_ff2,) = exchange_wait(_one(h_fw, 2), "forward_ff2_wait", after=[act], copies=len(OTHER_CHIPS))
    w_ff2_full = l_ff2.reshape(D_FF, D_MODEL)
    loss_sum, dh2_b, d_final = ff2_loss(act, w_ff2_full, h1, norm_final_w.reshape(1, D_MODEL), target, "ff2_loss")

    da1 = matmul(dh2_b, w_ff2_full, "nt", "d_act", out_dtypes=(BF16,), epilogue=_relu2_bwd_epilogue, extra=a1)
    t_all = xs.shape[0]
    dw_ff2 = matmul(act, dh2_b, "tn", "dw_ff2", out_dtypes=(BF16,), tk=t_all)
    p_ff2 = dw_ff2.reshape(N_DEV, D_FF // N_DEV, D_MODEL)
    h_s1, t_s1 = exchange_start([p_ff2], False, "scatter_ff2_start")
    dn2 = matmul(da1, w_ff1_sh, "nt", "d_n2", out_dtypes=(BF16,), after=[t_s1], b_shards=True, k_group=4)
    p_ff1 = matmul(n2, da1, "tn", "dw_ff1", out_dtypes=(BF16,), tn=D_FF // N_DEV, tk=t_all, after=[t_s1], out_shards=True)
    h_s2, t_s2 = exchange_start([p_ff1], False, "scatter_ff1_start")
    dh1_b, d_ffn = rms_bwd(h1, r2, norm_ffn_w + tok(t_s2), dn2, dh2_b, BF16, "rms_ffn_bwd")
    dmix = matmul(dh1_b, w_out_full, "nt", "d_mix", out_dtypes=(BF16,))
    dw_out = matmul(y, dh1_b, "tn", "dw_out", out_dtypes=(BF16,), tk=t_all)
    p_out = dw_out.reshape(N_DEV, D_MODEL // N_DEV, D_MODEL)
    h_s3, t_s3 = exchange_start([p_out], False, "scatter_out_start")
    d_qkv_c, dproj, d_alog_l, d_dt_l, d_gnw, dl0, dl1, d_hnw = mixer_bwd(
        qkv_c, proj, a_log_l, dt_l, gdn_norm_w + tok(t_s3), l0, l1, hgrn_norm_w, hist_a, inv_a, hist_b, o_b, dmix,
        "mixer_bwd")
    dproj, d_conv_full = conv_bwd(proj, d_qkv_c, conv_full, dproj, "conv_bwd")
    dw_cat = matmul(n1, dproj, "tn", "dw_in", out_dtypes=(BF16,), tm=512, tn=CAT_WIDTH // 5, tk=t_all)
    p_in = cat_to_shards(dw_cat, shard_in)
    h_pair, t_s4 = routed_start(p_in, _to_sibling_routes, "scatter_in_pair_start")

    _, (r_ff2,) = exchange_wait(h_s1, "scatter_ff2_wait", after=[t_s4])
    _, (r_ff1,) = exchange_wait(h_s2, "scatter_ff1_wait", after=[t_s4])
    _, (r_out,) = exchange_wait(h_s3, "scatter_out_wait", after=[t_s4])
    g_w_ff2, d_w_ff2, nm_w_ff2, nv_w_ff2 = adamw_reduce(r_ff2, w_ff2[0], m_w_ff2[0], v_w_ff2[0], "adamw_w_ff2")
    g_w_ff1, d_w_ff1, nm_w_ff1, nv_w_ff1 = adamw_reduce(r_ff1, w_ff1[0], m_w_ff1[0], v_w_ff1[0], "adamw_w_ff1")
    g_w_out, d_w_out, nm_w_out, nv_w_out = adamw_reduce(r_out, w_out[0], m_w_out[0], v_w_out[0], "adamw_w_out")
    (p_in,), (from_sibling,) = exchange_wait(h_pair, "scatter_in_pair_wait", after=[d_w_ff2, d_w_ff1, d_w_out],
                                             copies=N_CHIPS)
    chip_sums = pair_sum(p_in, from_sibling, "scatter_in_pair_sum")
    h_chips, t_s5 = routed_start(chip_sums, _to_chips_routes, "scatter_in_chips_start")
    dn1 = matmul(dproj, w_cat, "nt", "d_n1", out_dtypes=(BF16,), tm=512, tn=512, tk=CAT_WIDTH, after=[t_s5])
    dx, d_mix = rms_bwd(xs, r1, norm_mix_w, dn1, dh1_b, F32, "rms_mix_bwd")
    (chip_sums,), (r_in,) = exchange_wait(h_chips, "scatter_in_chips_wait", after=[dx], copies=len(OTHER_CHIPS))
    my_chip = me // 2
    r_in = lax.dynamic_update_slice(r_in, lax.dynamic_index_in_dim(chip_sums, my_chip, 0, keepdims=True), (my_chip, 0, 0))
    g_w_in, d_w_in, nm_w_in, nv_w_in = adamw_reduce(r_in, w_in[0], m_w_in[0], v_w_in[0], "adamw_w_in")

    d_lb = jnp.stack([dl0.reshape(GDN_WIDTH), dl1.reshape(GDN_WIDTH)])
    small_shapes = [(1, N_HEADS), (1, N_HEADS), (1, HEAD_DIM), (2, GDN_WIDTH), (1, HEAD_DIM), (1, D_MODEL),
                    (1, D_MODEL), (D_MODEL,), (4, QKV_WIDTH), ()]
    small = _pack([d_alog_l[:, 0, 0], d_dt_l[:, 0, 0], d_gnw, d_lb, d_hnw, d_mix, d_ffn, d_final, d_conv_full,
                   loss_sum[0, 0]])
    red = allreduce_small(small, "allreduce_small")
    g_alog, g_dt, g_gnw, g_lb, g_hnw, g_mix, g_ffn, g_final, g_conv_full, loss = _unpack(red, small_shapes)
    g_conv = lax.dynamic_slice(g_conv_full, (0, me * shard_conv), (4, shard_conv)).reshape(1, 4, shard_conv)
    small_g = [g_alog, g_dt, g_gnw, g_lb, g_hnw, g_mix, g_ffn, g_final, g_conv]
    small_w = [gdn_a_log, gdn_dt_bias, gdn_norm_w, hgrn_lb_logits, hgrn_norm_w, norm_mix_w, norm_ffn_w, norm_final_w, conv_w]
    small_m = [m_gdn_a_log, m_gdn_dt_bias, m_gdn_norm_w, m_hgrn_lb_logits, m_hgrn_norm_w, m_norm_mix_w, m_norm_ffn_w,
               m_norm_final_w, m_conv_w]
    small_v = [v_gdn_a_log, v_gdn_dt_bias, v_gdn_norm_w, v_hgrn_lb_logits, v_hgrn_norm_w, v_norm_mix_w, v_norm_ffn_w,
               v_norm_final_w, v_conv_w]
    shapes = [a.shape for a in small_w]
    d_s, m_s, v_s = adamw_small(_pack(small_w), _pack(small_g), _pack(small_m), _pack(small_v), "adamw_small")
    d_alog, d_dt, d_gn, d_lbl, d_hn, d_nm, d_nf, d_nfin, d_cw = _unpack(d_s, shapes)
    m_alog, m_dt, m_gn, m_lbl, m_hn, m_nm, m_nf, m_nfin, m_cw = _unpack(m_s, shapes)
    v_alog, v_dt, v_gn, v_lbl, v_hn, v_nm, v_nf, v_nfin, v_cw = _unpack(v_s, shapes)

    lead = lambda a: a[None]
    grads = [lead(g_w_in), g_conv, g_alog, g_dt, g_gnw, g_lb, g_hnw, lead(g_w_out), g_mix, g_ffn,
             lead(g_w_ff1), lead(g_w_ff2), g_final]
    deltas = [lead(d_w_in), d_cw, d_alog, d_dt, d_gn, d_lbl, d_hn, lead(d_w_out), d_nm, d_nf,
              lead(d_w_ff1), lead(d_w_ff2), d_nfin]
    new_m = [lead(nm_w_in), m_cw, m_alog, m_dt, m_gn, m_lbl, m_hn, lead(nm_w_out), m_nm, m_nf,
             lead(nm_w_ff1), lead(nm_w_ff2), m_nfin]
    new_v = [lead(nv_w_in), v_cw, v_alog, v_dt, v_gn, v_lbl, v_hn, lead(nv_w_out), v_nm, v_nf,
             lead(nv_w_ff1), lead(nv_w_ff2), v_nfin]
    return (loss, dx[None], *grads, *deltas, *new_m, *new_v)
```

```python
import functools

import jax
import jax.numpy as jnp
from jax import lax
from jax.experimental import pallas as pl
from jax.experimental.pallas import tpu as pltpu

F32 = jnp.float32
BF16 = jnp.bfloat16
HI = lax.Precision.HIGHEST

N_DEV = 8
D_MODEL = 2048
CHUNK = 64
SUB_CHUNK = 16
HEAD_DIM = 128
N_HEADS = 8
GDN_WIDTH = N_HEADS * HEAD_DIM
D_FF = 4 * D_MODEL
QKV_WIDTH = 3 * GDN_WIDTH
MAIN_WIDTH = 8 * GDN_WIDTH
CAT_WIDTH = MAIN_WIDTH + 128
AB_BLOCK = MAIN_WIDTH // 128
NORM_EPS = 1e-6
L2_EPS = 1e-6
LANES = 128
VMEM_LIMIT = 56 * 1024 * 1024

ADAM_LR = 0.001
ADAM_B1 = 0.9
ADAM_B2 = 0.999
ADAM_EPS = 1e-08
ADAM_WD = 0.01
ADAM_STEP = 10

MESH = pl.DeviceIdType.MESH


def _params(sem=None):
    return pltpu.CompilerParams(dimension_semantics=sem, vmem_limit_bytes=VMEM_LIMIT)


def _dot(a, b, dims, prec=None):
    return lax.dot_general(a, b, (dims, ((), ())), precision=prec, preferred_element_type=F32)


NN = ((1,), (0,))
NT = ((1,), (1,))
TN = ((0,), (0,))


def _split_bf16(x, pieces):
    out = []
    for _ in range(pieces - 1):
        p = x.astype(BF16)
        out.append(p)
        x = x - p.astype(F32)
    out.append(x.astype(BF16))
    return out


def _mm_raw(a, b, dims, prec):
    if prec == "hi":
        return _dot(a, b, dims, HI)
    if prec == "bf":
        return _dot(a.astype(BF16), b.astype(BF16), dims)
    a_hi, a_lo = _split_bf16(a, 2)
    b_hi, b_lo = _split_bf16(b, 2)
    return _dot(a_hi, b_hi, dims) + (_dot(a_hi, b_lo, dims) + _dot(a_lo, b_hi, dims))


@functools.partial(jax.custom_vjp, nondiff_argnums=(2, 3))
def mm(a, b, dims, prec):
    return _mm_raw(a, b, dims, prec)


def _mm_fwd(a, b, dims, prec):
    return _mm_raw(a, b, dims, prec), (a, b)


def _mm_bwd(dims, prec, res, ct):
    a, b = res
    if dims == NN:
        return _mm_raw(ct, b, NT, prec), _mm_raw(a, ct, TN, prec)
    if dims == NT:
        return _mm_raw(ct, b, NN, prec), _mm_raw(ct, a, TN, prec)
    return _mm_raw(b, ct, NT, prec), _mm_raw(a, ct, NN, prec)


mm.defvjp(_mm_fwd, _mm_bwd)


def _sel_raw(sel, x, dims):
    sel = sel.astype(BF16)
    p0, p1, p2 = _split_bf16(x, 3)
    return _dot(sel, p0, dims) + (_dot(sel, p1, dims) + _dot(sel, p2, dims))


def _sel_parts(sel, x):
    c = x.shape[0]
    full = _sel_raw(sel, x, NN)
    return tuple(full[i * c:(i + 1) * c] for i in range(sel.shape[0] // c))


@jax.custom_vjp
def sel_sums(sel, x):
    return _sel_parts(sel, x)


def _sel_fwd(sel, x):
    return _sel_parts(sel, x), sel


def _sel_bwd(sel, cts):
    return jnp.zeros_like(sel), _sel_raw(sel, jnp.concatenate(cts, axis=0), TN)


sel_sums.defvjp(_sel_fwd, _sel_bwd)


@jax.custom_vjp
def _known_value(computed, known):
    del computed
    return known


_known_value.defvjp(lambda computed, known: (known, None), lambda _, ct: (ct, jnp.zeros_like(ct)))


def _my_flat():
    return 4 * lax.axis_index("x") + 2 * lax.axis_index("y") + lax.axis_index("c")


def _peer(k):
    x, y, c = lax.axis_index("x"), lax.axis_index("y"), lax.axis_index("c")
    kx, ky, kc = (k >> 2) & 1, (k >> 1) & 1, k & 1
    px = (1 - x) if kx else x
    py = (1 - y) if ky else y
    pc = (1 - c) if kc else c
    return (px, py, pc), 4 * px + 2 * py + pc


def gather_two_level(xs, name):
    n = len(xs)

    def body(*refs):
        x_refs, y_refs = refs[:n], refs[n:2 * n]
        send_sems, recv_sems, local_sems = refs[2 * n:]
        x, y, c = lax.axis_index("x"), lax.axis_index("y"), lax.axis_index("c")
        me, sibling = (x, y, c), (x, y, 1 - c)
        chips = [(1 - x, y), (x, 1 - y), (1 - x, 1 - y)]
        flat = lambda p: 4 * p[0] + 2 * p[1] + p[2]

        def copy(a, k, block, to, src=None):
            return pltpu.make_async_remote_copy(
                src_ref=y_refs[a].at[flat(block)] if src is None else src, dst_ref=y_refs[a].at[flat(block)],
                send_sem=send_sems.at[a, k], recv_sem=recv_sems.at[a, k], device_id=to, device_id_type=MESH)

        mine = [pltpu.make_async_copy(x_refs[a], y_refs[a].at[flat(me)], local_sems.at[a]) for a in range(n)]
        for cp in mine:
            cp.start()
        first = [copy(a, 0, me, sibling, src=x_refs[a]) for a in range(n)]
        first += [copy(a, 1 + j, me, (*chip, c), src=x_refs[a]) for j, chip in enumerate(chips) for a in range(n)]
        for cp in first:
            cp.start()
        passed = []
        for j, chip in enumerate(chips):
            for a in range(n):
                copy(a, 1 + j, (*chip, c), me).wait_recv()
                cp = copy(a, 4 + j, (*chip, c), sibling)
                cp.start()
                passed.append(cp)
        for a in range(n):
            copy(a, 0, sibling, me).wait_recv()
        for j, chip in enumerate(chips):
            for a in range(n):
                copy(a, 4 + j, (*chip, 1 - c), me).wait_recv()
        for cp in first + passed:
            cp.wait_send()
        for cp in mine:
            cp.wait()

    any_spec = pl.BlockSpec(memory_space=pl.ANY)
    return pl.pallas_call(
        body, name=name, out_shape=[jax.ShapeDtypeStruct((N_DEV,) + x.shape, x.dtype) for x in xs],
        in_specs=[any_spec] * n, out_specs=[any_spec] * n,
        scratch_shapes=[pltpu.SemaphoreType.DMA((n, N_DEV - 1)), pltpu.SemaphoreType.DMA((n, N_DEV - 1)),
                        pltpu.SemaphoreType.DMA((n,))],
    )(*xs)


HBM_SPEC = pl.BlockSpec(memory_space=pltpu.HBM)
SEM_SPEC = pl.BlockSpec(memory_space=pltpu.SEMAPHORE)
ANY_SPEC = pl.BlockSpec(memory_space=pl.ANY)
DATAFLOW = pltpu.SideEffectType.DATAFLOW_SIDE_EFFECTING


def _in_hbm(x):
    return pltpu.with_memory_space_constraint(x, pltpu.HBM)


ALL_PEERS = tuple(range(1, N_DEV))
CHIP_PEERS = (1, 2, 4, 6)
OTHER_CHIPS = (2, 4, 6)


def exchange_start(xs, gather, name, after=(), peers=ALL_PEERS):
    n, n_after = len(xs), len(after)

    def body(*refs):
        x_refs, land_refs = refs[:n], refs[n:2 * n]
        sems = refs[2 * n + n_after:2 * n + n_after + 2 * n]
        token = refs[-1]
        me = _my_flat()
        for k in peers:
            peer, peer_flat = _peer(k)
            for a in range(n):
                src = x_refs[a] if gather else x_refs[a].at[peer_flat]
                pltpu.make_async_remote_copy(src_ref=src, dst_ref=land_refs[a].at[me], send_sem=sems[a],
                                             recv_sem=sems[n + a], device_id=peer, device_id_type=MESH).start()
        token[...] = jnp.zeros_like(token)

    lands =[_in_hbm(lax.empty(((N_DEV,) + x.shape) if gather else x.shape, x.dtype)) for x in xs]
    hbm_out = [pltpu.HBM(x.shape, x.dtype) for x in xs] + [pltpu.HBM(l.shape, l.dtype) for l in lands]
    res = pl.pallas_call(
        body, name=name,
        out_shape=(*([pltpu.SemaphoreType.DMA(())] * (2 * n)), *hbm_out, jax.ShapeDtypeStruct((8, LANES), F32)),
        in_specs=[HBM_SPEC] * (2 * n) + [ANY_SPEC] * n_after,
        out_specs=(*([SEM_SPEC] * (2 * n)), *([HBM_SPEC] * (2 * n)), pl.BlockSpec(memory_space=pltpu.VMEM)),
        input_output_aliases={i: 2 * n + i for i in range(2 * n)},
        compiler_params=pltpu.CompilerParams(has_side_effects=DATAFLOW),
    )(*[_in_hbm(x) for x in xs], *lands, *after)
    return (list(res[:2 * n]), list(res[2 * n:3 * n]), list(res[3 * n:4 * n])), res[-1]


def forward_start(lands, name, after=()):
    n, n_after = len(lands), len(after)

    def body(*refs):
        land_refs = refs[:n]
        sems = refs[n + n_after:n + n_after + 2 * n]
        token = refs[-1]
        sibling, _ = _peer(1)
        for a in range(n):
            for k in OTHER_CHIPS:
                _, from_flat = _peer(k)
                slot = land_refs[a].at[from_flat]
                pltpu.make_async_remote_copy(src_ref=slot, dst_ref=slot, send_sem=sems[a], recv_sem=sems[n + a],
                                             device_id=sibling, device_id_type=MESH).start()
        token[...] = jnp.zeros_like(token)

    res = pl.pallas_call(
        body, name=name,
        out_shape=(*([pltpu.SemaphoreType.DMA(())] * (2 * n)), *[pltpu.HBM(l.shape, l.dtype) for l in lands],
                   jax.ShapeDtypeStruct((8, LANES), F32)),
        in_specs=[HBM_SPEC] * n + [ANY_SPEC] * n_after,
        out_specs=(*([SEM_SPEC] * (2 * n)), *([HBM_SPEC] * n), pl.BlockSpec(memory_space=pltpu.VMEM)),
        input_output_aliases={i: 2 * n + i for i in range(n)},
        compiler_params=pltpu.CompilerParams(has_side_effects=DATAFLOW),
    )(*lands, *after)
    return (list(res[:2 * n]), [], list(res[2 * n:3 * n])), res[-1]


def exchange_wait(handle, name, after=(), copies=N_DEV - 1):
    sems, xs, lands = handle
    n, n_x, n_after = len(lands), len(xs), len(after)

    def body(*refs):
        land_refs = refs[n_x:n_x + n]
        sem_refs = refs[n_x + n:n_x + 3 * n]
        for a in range(n):
            every = land_refs[a].at[pl.ds(0, copies)]
            cp = pltpu.make_async_remote_copy(src_ref=every, dst_ref=every, send_sem=sem_refs[a],
                                              recv_sem=sem_refs[n + a], device_id=_peer(1)[0], device_id_type=MESH)
            cp.wait_send()
            cp.wait_recv()

    res = pl.pallas_call(
        body, name=name,
        out_shape=[pltpu.HBM(x.shape, x.dtype) for x in xs] + [pltpu.HBM(l.shape, l.dtype) for l in lands],
        in_specs=[HBM_SPEC] * (n_x + n) + [SEM_SPEC] * (2 * n) + [ANY_SPEC] * n_after,
        out_specs=[HBM_SPEC] * (n_x + n),
        input_output_aliases={i: i for i in range(n_x + n)},
        compiler_params=pltpu.CompilerParams(has_side_effects=DATAFLOW),
    )(*xs, *lands, *sems, *after)
    return list(res[:n_x]), list(res[n_x:])


N_CHIPS = N_DEV // 2


def routed_start(x, routes, name, after=()):
    n_after = len(after)

    def body(*refs):
        x_ref, land_ref = refs[0], refs[1]
        send_sem, recv_sem = refs[2 + n_after], refs[3 + n_after]
        token = refs[-1]
        for src, dst, peer in routes():
            pltpu.make_async_remote_copy(src_ref=x_ref.at[src], dst_ref=land_ref.at[dst], send_sem=send_sem,
                                         recv_sem=recv_sem, device_id=peer, device_id_type=MESH).start()
        token[...] = jnp.zeros_like(token)

    land = _in_hbm(lax.empty((N_CHIPS,) + x.shape[1:], x.dtype))
    res = pl.pallas_call(
        body, name=name,
        out_shape=(pltpu.SemaphoreType.DMA(()), pltpu.SemaphoreType.DMA(()), pltpu.HBM(x.shape, x.dtype),
                   pltpu.HBM(land.shape, land.dtype), jax.ShapeDtypeStruct((8, LANES), F32)),
        in_specs=[HBM_SPEC, HBM_SPEC] + [ANY_SPEC] * n_after,
        out_specs=(SEM_SPEC, SEM_SPEC, HBM_SPEC, HBM_SPEC, pl.BlockSpec(memory_space=pltpu.VMEM)),
        input_output_aliases={0: 2, 1: 3},
        compiler_params=pltpu.CompilerParams(has_side_effects=DATAFLOW),
    )(_in_hbm(x), land, *after)
    return ([res[0], res[1]], [res[2]], [res[3]]), res[-1]


def _to_sibling_routes():
    c = lax.axis_index("c")
    sibling, _ = _peer(1)
    return [(2 * chip + 1 - c, chip, sibling) for chip in range(N_CHIPS)]


def _to_chips_routes():
    my_chip = _my_flat() // 2
    routes = []
    for k in OTHER_CHIPS:
        peer, peer_flat = _peer(k)
        routes.append((peer_flat // 2, my_chip, peer))
    return routes


def pair_sum(p, from_sibling, name, rb=1024):
    _, r, c = p.shape
    mine = lax.axis_index("c").astype(jnp.int32).reshape(1)

    def body(kind_ref, p_ref, s_ref, o_ref):
        del kind_ref
        o_ref[...] = (p_ref[...].astype(F32) + s_ref[...].astype(F32)).astype(BF16)

    return pl.pallas_call(
        body, name=name,
        grid_spec=pltpu.PrefetchScalarGridSpec(
            num_scalar_prefetch=1, grid=(N_CHIPS, r // rb),
            in_specs=[pl.BlockSpec((None, None, rb, c), lambda chip, i, kind: (chip, kind[0], i, 0)),
                      pl.BlockSpec((None, rb, c), lambda chip, i, kind: (chip, i, 0))],
            out_specs=pl.BlockSpec((None, rb, c), lambda chip, i, kind: (chip, i, 0))),
        out_shape=jax.ShapeDtypeStruct((N_CHIPS, r, c), BF16),
        compiler_params=_params(("parallel", "parallel")))(mine, p.reshape(N_CHIPS, 2, r, c), from_sibling)


def _one(handle, a):
    sems, xs, lands = handle
    n = len(lands)
    return [sems[a], sems[n + a]], xs[a:a + 1], [lands[a]]


def _own_slot(land, block):
    return lax.dynamic_update_slice(land, block[None], (_my_flat(),) + (0,) * block.ndim)


def allreduce_small(x, name):
    rows = x.shape[0]

    def body(x_ref, o_ref, buf, send_sems, recv_sems):
        me = _my_flat()
        buf[me] = x_ref[...]
        sends = []
        for k in range(1, N_DEV):
            peer, _ = _peer(k)
            cp = pltpu.make_async_remote_copy(
                src_ref=x_ref, dst_ref=buf.at[me], send_sem=send_sems.at[k], recv_sem=recv_sems.at[k],
                device_id=peer, device_id_type=MESH)
            cp.start()
            sends.append(cp)
        for k in range(1, N_DEV):
            peer, peer_flat = _peer(k)
            pltpu.make_async_remote_copy(
                src_ref=x_ref, dst_ref=buf.at[peer_flat], send_sem=send_sems.at[k], recv_sem=recv_sems.at[k],
                device_id=peer, device_id_type=MESH).wait_recv()
        for cp in sends:
            cp.wait_send()
        acc = buf[0]
        for d in range(1, N_DEV):
            acc = acc + buf[d]
        o_ref[...] = acc

    vmem = pl.BlockSpec(memory_space=pltpu.VMEM)
    return pl.pallas_call(
        body, name=name, out_shape=jax.ShapeDtypeStruct((rows, LANES), F32),
        in_specs=[vmem], out_specs=vmem,
        scratch_shapes=[pltpu.VMEM((N_DEV, rows, LANES), F32),
                        pltpu.SemaphoreType.DMA((N_DEV,)), pltpu.SemaphoreType.DMA((N_DEV,))],
    )(x)


def matmul(a, b, mode, name, out_dtypes=(F32,), epilogue=None, extra=None, tm=1024, tn=1024, tk=2048, after=(),
           b_shards=False, out_shards=False, k_group=1, k_blocks=None):
    if b_shards:
        n_sh, b_rows, b_cols = b.shape
    if mode == "nn":
        (m, kd), n = a.shape, (n_sh * b_cols if b_shards else b.shape[1])
        if b_shards:
            tn = b_cols
    elif mode == "nt":
        (m, kd), n = a.shape, (b_rows if b_shards else b.shape[0])
        if b_shards:
            tk = k_group * b_cols
    else:
        (kd, m), n = a.shape, b.shape[1]
    tm, tn, tk = min(tm, m), min(tn, n), min(tk, kd)
    assert m % tm == 0 and n % tn == 0 and kd % tk == 0, (name, m, n, kd, tm, tn, tk)
    k0, ksteps = (0, kd // tk) if k_blocks is None else k_blocks
    dims = {"nn": NN, "nt": NT, "tn": TN}[mode]
    n_out = len(out_dtypes)
    n_in = 2 + (extra is not None) + len(after)

    def finish(acc, e_ref, o_refs):
        outs = (acc,) if epilogue is None else epilogue(acc, e_ref[...] if e_ref is not None else None)
        for o_ref, o in zip(o_refs, outs):
            o_ref[...] = o.astype(o_ref.dtype)

    def product(a_ref, b_ref):
        if mode == "nt" and b_shards:
            w = b_cols
            parts = [_dot(a_ref[:, s * w:(s + 1) * w], b_ref[s], dims) for s in range(k_group)]
            return functools.reduce(lambda p, q: p + q, parts)
        return _dot(a_ref[...], b_ref[...], dims)

    def body(*refs):
        a_ref, b_ref = refs[0], refs[1]
        e_ref = refs[2] if extra is not None else None
        o_refs = refs[n_in:n_in + n_out]
        if ksteps == 1:
            finish(product(a_ref, b_ref), e_ref, o_refs)
            return
        acc_ref = refs[-1]
        kk = pl.program_id(2)

        @pl.when(kk == 0)
        def _():
            acc_ref[...] = jnp.zeros_like(acc_ref)

        acc_ref[...] += product(a_ref, b_ref)

        @pl.when(kk == ksteps - 1)
        def _():
            finish(acc_ref[...], e_ref, o_refs)

    if mode == "nn":
        a_spec = pl.BlockSpec((tm, tk), lambda i, j, k: (i, k0 + k))
        b_spec = (pl.BlockSpec((None, tk, tn), lambda i, j, k: (j, k, 0)) if b_shards
                  else pl.BlockSpec((tk, tn), lambda i, j, k: (k0 + k, j)))
    elif mode == "nt":
        a_spec = pl.BlockSpec((tm, tk), lambda i, j, k: (i, k))
        b_spec = (pl.BlockSpec((k_group, tn, b_cols), lambda i, j, k: (k, j, 0)) if b_shards
                  else pl.BlockSpec((tn, tk), lambda i, j, k: (j, k)))
    else:
        a_spec = pl.BlockSpec((tk, tm), lambda i, j, k: (k, i))
        b_spec = pl.BlockSpec((tk, tn), lambda i, j, k: (k, j))
    o_spec = pl.BlockSpec((tm, tn), lambda i, j, k: (i, j))
    res_spec = pl.BlockSpec((None, tm, tn), lambda i, j, k: (j, i, 0)) if out_shards else o_spec
    res_shape = (n // tn, m, tn) if out_shards else (m, n)
    in_specs = [a_spec, b_spec] + ([o_spec] if extra is not None else []) + [ANY_SPEC] * len(after)
    args = (a, b) + ((extra,) if extra is not None else ()) + tuple(after)
    res = pl.pallas_call(
        body, name=name, grid=(m // tm, n // tn, ksteps),
        in_specs=in_specs, out_specs=[res_spec] * n_out,
        out_shape=[jax.ShapeDtypeStruct(res_shape, dt) for dt in out_dtypes],
        scratch_shapes=[pltpu.VMEM((tm, tn), F32)] if ksteps > 1 else [],
        compiler_params=_params(("parallel", "parallel", "arbitrary")),
    )(*args)
    return res if n_out > 1 else res[0]


GATE_COL = 4 * GDN_WIDTH
RELAYOUT_ROWS = 256


def _cat_of_win(j):
    if j < GATE_COL:
        return j
    if j < GATE_COL + 2 * N_HEADS:
        return MAIN_WIDTH + (j - GATE_COL)
    return j - 2 * N_HEADS


def _win_of_cat(c):
    if c < GATE_COL:
        return c
    if c < MAIN_WIDTH:
        return c + 2 * N_HEADS
    if c < MAIN_WIDTH + 2 * N_HEADS:
        return GATE_COL + (c - MAIN_WIDTH)
    return None


def _runs(first, count, mapping):
    runs, i = [], 0
    while i < count:
        start, n = mapping(first + i), 1
        while i + n < count and mapping(first + i + n) == start + n:
            n += 1
        runs.append((start, n))
        i += n
    return runs


def weights_to_cat(g_in, name, total_rows, row0=0, into=None):
    n_dev, rows, shard = g_in.shape
    first = row0 // RELAYOUT_ROWS

    def body(x_ref, *rest):
        o_ref = rest[-1]
        for b in range(CAT_WIDTH // LANES):
            live = sum(_win_of_cat(LANES * b + i) is not None for i in range(LANES))
            parts = []
            for start, n in _runs(LANES * b, live, _win_of_cat):
                while n > 0:
                    d, o = divmod(start, shard)
                    take = min(n, shard - o)
                    parts.append(x_ref[d, :, o:o + take])
                    start, n = start + take, n - take
            if live < LANES:
                parts.append(jnp.zeros((RELAYOUT_ROWS, LANES - live), g_in.dtype))
            o_ref[:, LANES * b:LANES * (b + 1)] = parts[0] if len(parts) == 1 else jnp.concatenate(parts, axis=1)

    return pl.pallas_call(
        body, name=name, grid=(rows // RELAYOUT_ROWS,),
        in_specs=[pl.BlockSpec((n_dev, RELAYOUT_ROWS, shard), lambda i: (0, i, 0))] + ([ANY_SPEC] if into is not None else []),
        out_specs=pl.BlockSpec((RELAYOUT_ROWS, CAT_WIDTH), lambda i: (first + i, 0)),
        out_shape=jax.ShapeDtypeStruct((total_rows, CAT_WIDTH), g_in.dtype),
        input_output_aliases={1: 0} if into is not None else {},
        compiler_params=_params(("parallel",)))(*((g_in,) if into is None else (g_in, into)))


def cat_to_shards(dw_cat, shard):
    rows = dw_cat.shape[0]

    def body(x_ref, o_ref):
        for d in range(N_DEV):
            for t0 in range(0, shard, LANES):
                width = min(LANES, shard - t0)
                parts = [x_ref[:, c:c + n] for c, n in _runs(d * shard + t0, width, _cat_of_win)]
                o_ref[d, :, t0:t0 + width] = parts[0] if len(parts) == 1 else jnp.concatenate(parts, axis=1)

    return pl.pallas_call(
        body, name="cat_to_shards", grid=(rows // RELAYOUT_ROWS,),
        in_specs=[pl.BlockSpec((RELAYOUT_ROWS, CAT_WIDTH), lambda i: (i, 0))],
        out_specs=pl.BlockSpec((N_DEV, RELAYOUT_ROWS, shard), lambda i: (0, i, 0)),
        out_shape=jax.ShapeDtypeStruct((N_DEV, rows, shard), dw_cat.dtype),
        compiler_params=_params(("parallel",)))(dw_cat)


ROW_BLOCK = 512


def rms_fwd(x, w, name):
    t, d = x.shape

    def body(x_ref, w_ref, n_ref, r_ref):
        h = x_ref[...]
        r = lax.rsqrt(jnp.mean(h * h, axis=-1, keepdims=True) + NORM_EPS)
        n_ref[...] = (h * r * w_ref[...]).astype(BF16)
        r_ref[...] = r

    row = pl.BlockSpec((ROW_BLOCK, d), lambda i: (i, 0))
    return pl.pallas_call(
        body, name=name, grid=(t // ROW_BLOCK,),
        in_specs=[row, pl.BlockSpec((1, d), lambda i: (0, 0))],
        out_specs=[row, pl.BlockSpec((ROW_BLOCK, 1), lambda i: (i, 0))],
        out_shape=[jax.ShapeDtypeStruct((t, d), BF16), jax.ShapeDtypeStruct((t, 1), F32)],
        compiler_params=_params(("parallel",)))(x, w)


FUSED_ROWS = 512


def out_proj_rms(y, w_out, x, w_norm, name):
    t, d = x.shape

    def body(y_ref, w_ref, x_ref, g_ref, h_ref, n_ref, r_ref):
        h = x_ref[...] + _dot(y_ref[...], w_ref[...], NN)
        r = lax.rsqrt(jnp.mean(h * h, axis=-1, keepdims=True) + NORM_EPS)
        h_ref[...] = h
        n_ref[...] = (h * r * g_ref[...]).astype(BF16)
        r_ref[...] = r

    row = pl.BlockSpec((FUSED_ROWS, d), lambda i: (i, 0))
    return pl.pallas_call(
        body, name=name, grid=(t // FUSED_ROWS,),
        in_specs=[pl.BlockSpec((FUSED_ROWS, y.shape[1]), lambda i: (i, 0)), pl.BlockSpec(w_out.shape, lambda i: (0, 0)),
                  row, pl.BlockSpec((1, d), lambda i: (0, 0))],
        out_specs=[row, row, pl.BlockSpec((FUSED_ROWS, 1), lambda i: (i, 0))],
        out_shape=[jax.ShapeDtypeStruct((t, d), F32), jax.ShapeDtypeStruct((t, d), BF16),
                   jax.ShapeDtypeStruct((t, 1), F32)],
        compiler_params=_params(("parallel",)))(y, w_out, x, w_norm)


def ff2_loss(act, w_ff2, h1, w, target, name, tk=2048):
    t, d = h1.shape
    ksteps = act.shape[1] // tk

    def body(a_ref, b_ref, h_ref, w_ref, t_ref, loss_ref, dhb_ref, dw_ref, acc_ref):
        i, kk = pl.program_id(0), pl.program_id(1)

        @pl.when((i == 0) & (kk == 0))
        def _():
            loss_ref[...] = jnp.zeros_like(loss_ref)
            dw_ref[...] = jnp.zeros_like(dw_ref)

        @pl.when(kk == 0)
        def _():
            acc_ref[...] = h_ref[...]

        acc_ref[...] += _dot(a_ref[...], b_ref[...], NN)

        @pl.when(kk == ksteps - 1)
        def _():
            h = acc_ref[...]
            wv = w_ref[...]
            r = lax.rsqrt(jnp.mean(h * h, axis=-1, keepdims=True) + NORM_EPS)
            yn = h * r
            e = yn * wv - t_ref[...]
            loss_ref[...] += 0.5 * jnp.sum(jnp.sum(e * e, axis=-1, keepdims=True), axis=0, keepdims=True) / d
            dy = e / d
            dw_ref[...] += jnp.sum(dy * yn, axis=0, keepdims=True)
            dyn = dy * wv
            dhb_ref[...] = (r * (dyn - yn * jnp.mean(dyn * yn, axis=-1, keepdims=True))).astype(BF16)

    row = pl.BlockSpec((FUSED_ROWS, d), lambda i, k: (i, 0))
    wspec = pl.BlockSpec((1, d), lambda i, k: (0, 0))
    return pl.pallas_call(
        body, name=name, grid=(t // FUSED_ROWS, ksteps),
        in_specs=[pl.BlockSpec((FUSED_ROWS, tk), lambda i, k: (i, k)), pl.BlockSpec((tk, d), lambda i, k: (k, 0)),
                  row, wspec, row],
        out_specs=[pl.BlockSpec((1, 1), lambda i, k: (0, 0)), row, wspec],
        out_shape=[jax.ShapeDtypeStruct((1, 1), F32), jax.ShapeDtypeStruct((t, d), BF16),
                   jax.ShapeDtypeStruct((1, d), F32)],
        scratch_shapes=[pltpu.VMEM((FUSED_ROWS, d), F32)],
        compiler_params=_params(("arbitrary", "arbitrary")))(act, w_ff2, h1, w, target)


def rms_bwd(h, r, w, dn, dres, out_dtype, name):
    t, d = h.shape

    def body(h_ref, r_ref, w_ref, dn_ref, dres_ref, dh_ref, dw_ref):
        @pl.when(pl.program_id(0) == 0)
        def _():
            dw_ref[...] = jnp.zeros_like(dw_ref)

        rv = r_ref[...]
        yn = h_ref[...] * rv
        dnv = dn_ref[...].astype(F32)
        dw_ref[...] += jnp.sum(dnv * yn, axis=0, keepdims=True)
        dyn = dnv * w_ref[...]
        dh = dres_ref[...].astype(F32) + rv * (dyn - yn * jnp.mean(dyn * yn, axis=-1, keepdims=True))
        dh_ref[...] = dh.astype(out_dtype)

    row = pl.BlockSpec((ROW_BLOCK, d), lambda i: (i, 0))
    wspec = pl.BlockSpec((1, d), lambda i: (0, 0))
    rspec = pl.BlockSpec((ROW_BLOCK, 1), lambda i: (i, 0))
    return pl.pallas_call(
        body, name=name, grid=(t // ROW_BLOCK,),
        in_specs=[row, rspec, wspec, row, row], out_specs=[row, wspec],
        out_shape=[jax.ShapeDtypeStruct((t, d), out_dtype), jax.ShapeDtypeStruct((1, d), F32)],
        compiler_params=_params(("arbitrary",)))(h, r, w, dn, dres)


CONV_ROWS = 512
TILE_ROWS = 8


def _iota2(shape, axis):
    return lax.broadcasted_iota(jnp.int32, shape, axis)


def _silu(x):
    return x * jax.nn.sigmoid(x)


def _conv_rows(x_ref, w, first, rows):
    acc = None
    for j in range(4):
        term = x_ref[first - 3 + j:first - 3 + j + rows, :] * w[j:j + 1, :]
        acc = term if acc is None else acc + term
    return acc


def _head_shifts(head):
    rows = _iota2((TILE_ROWS, 1), 0)
    return [jnp.where(rows >= 3 - j, head if j == 3 else pltpu.roll(head, 3 - j, 0), 0.0) for j in range(4)]


def _conv_chunks(t):
    pieces = [(TILE_ROWS, min(CONV_ROWS, t) - TILE_ROWS)]
    pieces += [(r, CONV_ROWS) for r in range(CONV_ROWS, t, CONV_ROWS)]
    return pieces


def conv_fwd(proj, conv_w, name):
    t = proj.shape[0]

    def body(x_ref, w_ref, o_ref):
        w = w_ref[...]
        shifted = _head_shifts(x_ref[0:TILE_ROWS, :])
        o_ref[0:TILE_ROWS, :] = _silu(sum(shifted[j] * w[j:j + 1, :] for j in range(4)))
        for first, rows in _conv_chunks(t):
            o_ref[first:first + rows, :] = _silu(_conv_rows(x_ref, w, first, rows))

    col = pl.BlockSpec((t, LANES), lambda c: (0, c))
    return pl.pallas_call(
        body, name=name, grid=(QKV_WIDTH // LANES,),
        in_specs=[col, pl.BlockSpec((4, LANES), lambda c: (0, c))], out_specs=col,
        out_shape=jax.ShapeDtypeStruct((t, QKV_WIDTH), F32),
        compiler_params=_params(("parallel",)))(proj, conv_w)


def conv_bwd(proj, dout, conv_w, dproj, name):
    t = proj.shape[0]

    def dsilu(pre):
        sg = jax.nn.sigmoid(pre)
        return sg * (1.0 + pre * (1.0 - sg))

    def body(x_ref, d_ref, w_ref, dproj_in, dx_ref, dw_ref, stage):
        del dproj_in
        w = w_ref[...]
        shifted = _head_shifts(x_ref[0:TILE_ROWS, :])
        head_dpre = d_ref[0:TILE_ROWS, :] * dsilu(sum(shifted[j] * w[j:j + 1, :] for j in range(4)))
        stage[0:TILE_ROWS, :] = head_dpre
        for first, rows in _conv_chunks(t):
            stage[first:first + rows, :] = d_ref[first:first + rows, :] * dsilu(_conv_rows(x_ref, w, first, rows))
        stage[t:t + TILE_ROWS, :] = jnp.zeros((TILE_ROWS, LANES), F32)
        for first, rows in [(0, TILE_ROWS)] + _conv_chunks(t):
            dx = None
            for j in range(4):
                term = stage[first + 3 - j:first + 3 - j + rows, :] * w[j:j + 1, :]
                dx = term if dx is None else dx + term
            dx_ref[first:first + rows, :] = dx.astype(BF16)
        dw = [jnp.sum(head_dpre * shifted[j], axis=0, keepdims=True) for j in range(4)]
        for first, rows in _conv_chunks(t):
            dpre = stage[first:first + rows, :]
            for j in range(4):
                dw[j] = dw[j] + jnp.sum(dpre * x_ref[first - 3 + j:first - 3 + j + rows, :], axis=0, keepdims=True)
        dw_ref[...] = jnp.concatenate(dw, axis=0)

    col = pl.BlockSpec((t, LANES), lambda c: (0, c))
    taps = pl.BlockSpec((4, LANES), lambda c: (0, c))
    return pl.pallas_call(
        body, name=name, grid=(QKV_WIDTH // LANES,),
        in_specs=[col, col, taps, ANY_SPEC], out_specs=[col, taps],
        out_shape=[jax.ShapeDtypeStruct(dproj.shape, BF16), jax.ShapeDtypeStruct((4, QKV_WIDTH), F32)],
        scratch_shapes=[pltpu.VMEM((t + TILE_ROWS, LANES), F32)],
        input_output_aliases={3: 0},
        compiler_params=_params(("parallel",)))(proj, dout, conv_w, dproj)


def _softplus(x):
    return jnp.maximum(x, 0.0) + jnp.log(1.0 + jnp.exp(-jnp.abs(x)))


def _head_norm_gate(o, norm_w, gate):
    return o * lax.rsqrt(jnp.mean(o * o, axis=-1, keepdims=True) + NORM_EPS) * norm_w * _silu(gate)


GDN_PREC = ("bf", "bf")
HGRN_PREC = "bf"


def _each(fn, *cols):
    return [fn(*a) for a in zip(*cols)]


@functools.partial(jax.custom_vjp, nondiff_argnums=(2,))
def _known_inverse(low, inv, prec):
    del low, prec
    return inv


def _known_inverse_fwd(low, inv, prec):
    del low
    return inv, inv


def _known_inverse_bwd(prec, inv, ct):
    return -_mm_raw(_mm_raw(inv, ct, TN, prec), inv, NT, prec), jnp.zeros_like(inv)


_known_inverse.defvjp(_known_inverse_fwd, _known_inverse_bwd)


def gdn_stages(hs, qc, kc, vc, zc, ab, a_log_l, dt_l, norm_w, s, prec=GDN_PREC, inv_known=None):
    p_inv, p_mm = prec
    c = CHUNK
    ri, ci = _iota2((c, c), 0), _iota2((c, c), 1)
    incl, strict, eye = ri >= ci, ri > ci, ri == ci
    lane = _iota2((c, LANES), 1)
    last_row = _iota2((c, 1), 0) == c - 1
    rowsum = lambda x: jnp.sum(x, axis=1, keepdims=True)

    def row(col):
        return jnp.sum(jnp.where(eye, col, 0.0), axis=0, keepdims=True)

    q = _each(lambda x: x * lax.rsqrt(rowsum(x * x) + L2_EPS) * (HEAD_DIM ** -0.5), qc)
    k = _each(lambda x: x * lax.rsqrt(rowsum(x * x) + L2_EPS), kc)
    yield
    a_col = [rowsum(jnp.where(lane == h, ab, 0.0)) for h in hs]
    b_col = [rowsum(jnp.where(lane == h + N_HEADS, ab, 0.0)) for h in hs]
    beta = _each(jax.nn.sigmoid, b_col)
    g = _each(lambda a, al, dl: rowsum(jnp.where(lane == 0, -jnp.exp(al) * _softplus(a + dl), 0.0)), a_col, a_log_l, dt_l)
    gcum = _each(lambda x: rowsum(jnp.where(incl, row(x), 0.0)), g)
    g_last = _each(lambda x: jnp.sum(jnp.where(last_row, x, 0.0), axis=0, keepdims=True), gcum)
    decay = _each(lambda x: jnp.exp(jnp.where(incl, x - row(x), -jnp.inf)), gcum)
    yield
    kk = _each(lambda x: mm(x, x, NT, p_mm), k)
    low = _each(lambda b, x, d: jnp.where(strict, b * x * d, 0.0), beta, kk, decay)
    yield
    if inv_known is None:
        power = _each(lambda x: -x, low)
        inv = _each(lambda x: jnp.where(eye, 1.0, 0.0) + x, power)
        for _ in range(5):
            power = _each(lambda x: mm(x, x, NN, p_inv), power)
            yield
            inv = _each(lambda x, p: x + mm(x, p, NN, p_inv), inv, power)
            yield
    else:
        inv = _each(lambda x, known: _known_inverse(x, known, p_inv), low, inv_known)
    exp_g = _each(jnp.exp, gcum)
    yield
    u_v = _each(lambda i, b, x: mm(i, b * x, NN, p_mm), inv, beta, vc)
    w = _each(lambda i, b, e, x: mm(i, b * e * x, NN, p_mm), inv, beta, exp_g, k)
    yield
    attn = _each(lambda x, y, d: mm(x, y, NT, p_mm) * d, q, k, decay)
    yield
    u = _each(lambda x, y, z: x - mm(y, z, NN, p_mm), u_v, w, s)
    yield
    o = _each(lambda x, e, z: mm(x * e, z, NN, p_mm), q, exp_g, s)
    o = _each(lambda x, a, y: x + mm(a, y, NN, p_mm), o, attn, u)
    yield
    k_end = _each(lambda x, gl, gc: x * jnp.exp(gl - gc), k, g_last, gcum)
    s_new = _each(lambda z, gl, x, y: z * jnp.exp(gl) + mm(x, y, TN, p_mm), s, g_last, k_end, u)
    return (_each(lambda x, z: _head_norm_gate(x, norm_w, z), o, zc), s_new), inv


def gdn_chunk(h, qc, kc, vc, zc, ab, a_log_l, dt_l, norm_w, s, prec=GDN_PREC, reuse_inverse=False):
    args = ([h], [qc], [kc], [vc], [zc], ab, [a_log_l], [dt_l], norm_w, [s], prec)
    if reuse_inverse:
        inv = lax.stop_gradient(gdn_chunks(*args)[1])
        (y, s_new), _ = gdn_chunks(*args, inv_known=inv)
    else:
        (y, s_new), _ = gdn_chunks(*args)
    return y[0], s_new[0]


DIAG_ROWS = SUB_CHUNK // 2
SHIFT_PAD = 8
SHIFT_ROWS = SHIFT_PAD + CHUNK + SHIFT_PAD
SHIFT_WAYS = 4


class RolledRows:
    def down(self, x, which):
        del which
        return [x] + [pltpu.roll(x, off, 0) for off in range(1, DIAG_ROWS)]

    def up_sum(self, parts, which):
        del which
        acc = parts[0]
        for off in range(1, DIAG_ROWS):
            acc = acc + pltpu.roll(parts[off], CHUNK - off, 0)
        return acc


class SlotRows:
    def __init__(self, slots):
        self.slots = slots

    def down(self, x, which):
        self.slots[which, 0, SHIFT_PAD:SHIFT_PAD + CHUNK, :] = x
        return [x] + [self.slots[which, 0, SHIFT_PAD - off:SHIFT_PAD + CHUNK - off, :] for off in range(1, DIAG_ROWS)]

    def up_sum(self, parts, which):
        acc = parts[0]
        for off in range(1, DIAG_ROWS):
            way = 1 + off % (SHIFT_WAYS - 1)
            self.slots[which, way, SHIFT_PAD:SHIFT_PAD + CHUNK, :] = parts[off]
            acc = acc + self.slots[which, way, SHIFT_PAD + off:SHIFT_PAD + CHUNK + off, :]
        return acc


def _sub_block_rows():
    return jnp.bitwise_and(_iota2((CHUNK, 1), 0), DIAG_ROWS - 1)


def _diag_forward(rows, q, key, bc, v):
    rmod = _sub_block_rows()
    k_d, b_d, v_d = rows.down(key, 0), rows.down(bc, 1), rows.down(v, 2)
    o = None
    for off in range(DIAG_ROWS):
        e = jnp.exp(jnp.where(rmod >= off, bc - b_d[off], -jnp.inf))
        term = jnp.sum(q * k_d[off] * e, axis=-1, keepdims=True) * v_d[off]
        o = term if o is None else o + term
    return o


def _diag_backward(rows, q, key, bc, v, do):
    rmod = _sub_block_rows()
    k_d, b_d, v_d = rows.down(key, 0), rows.down(bc, 1), rows.down(v, 2)
    dq = db = None
    dk_parts, db_parts, dv_parts = [], [], []
    for off in range(DIAG_ROWS):
        e = jnp.exp(jnp.where(rmod >= off, bc - b_d[off], -jnp.inf))
        qe = q * e
        a = jnp.sum(qe * k_d[off], axis=-1, keepdims=True)
        da = jnp.sum(do * v_d[off], axis=-1, keepdims=True)
        dv_parts.append(a * do)
        dq_term = (da * e) * k_d[off]
        dk_term = da * qe
        s = dk_term * k_d[off]
        dq = dq_term if dq is None else dq + dq_term
        db = s if db is None else db + s
        dk_parts.append(dk_term)
        db_parts.append(s)
    return dq, rows.up_sum(dk_parts, 0), db - rows.up_sum(db_parts, 1), rows.up_sum(dv_parts, 2)


def diag_part(rows, differentiable=True):
    forward = functools.partial(_diag_forward, rows)
    if not differentiable:
        return forward
    part = jax.custom_vjp(forward)
    part.defvjp(lambda q, key, bc, v: (forward(q, key, bc, v), (q, key, bc, v)),
                lambda res, do: _diag_backward(rows, *res, do))
    return part


def hgrn_stages(qb, fb, ib, gb, l0, l1, norm_w, st, prec=HGRN_PREC, diags=None, o_known=None):
    c = CHUNK
    ri, ci = _iota2((4 * c, c), 0), _iota2((4 * c, c), 1)
    rcol = _iota2((c, 1), 0)
    blk0 = jnp.bitwise_and(ri, c - SUB_CHUNK)
    limit = jnp.where(ri < c, ri + 1, jnp.where(ri < 2 * c, blk0, jnp.where(ri < 3 * c, blk0 + SUB_CHUNK,
                                                                          blk0 + DIAG_ROWS)))
    sel = jnp.where(ci < limit, 1.0, 0.0)
    ri, ci = _iota2((c, c), 0), _iota2((c, c), 1)
    lb = _each(lambda a, b: jax.nn.sigmoid(a - b), l0, l1)
    forget = _each(lambda b, f: b + (1.0 - b) * jax.nn.sigmoid(f), lb, fb)
    key = _each(lambda b, f: (1.0 - b) * jax.nn.sigmoid(-f), lb, fb)
    q = _each(_silu, qb)
    v = ib
    logf = _each(jnp.log, forget)
    sums = _each(lambda x: sel_sums(sel, x), logf)
    bc, b_start, b_end, b_half = ([x[i] for x in sums] for i in range(4))
    b_last = _each(lambda x: jnp.sum(x, axis=0, keepdims=True), logf)
    o = _each(lambda x, b, z: mm(x * jnp.exp(b), z, NT, prec), q, bc, st)
    if diags is None:
        diags = [diag_part(RolledRows())] * len(qb)
    yield
    o = list(o)
    for h in range(len(o)):
        o[h] = o[h] + diags[h](q[h], key[h], bc[h], v[h])
        yield
    second = jnp.bitwise_and(rcol, SUB_CHUNK - 1) >= DIAG_ROWS
    same_sub = jnp.bitwise_and(ri, c - SUB_CHUNK) == jnp.bitwise_and(ci, c - SUB_CHUNK)
    q_half = _each(lambda x, b, bh: x * jnp.exp(jnp.where(second, b - bh, -jnp.inf)), q, bc, b_half)
    k_half = _each(lambda x, b, bh: x * jnp.exp(jnp.where(second, -jnp.inf, bh - b)), key, bc, b_half)
    a_half = _each(lambda x, z: jnp.where(same_sub, mm(x, z, NT, prec), 0.0), q_half, k_half)
    o = _each(lambda acc, a, val: acc + mm(a, val, NN, prec), o, a_half, v)
    yield
    q_rel = _each(lambda x, b, bs: x * jnp.exp(b - bs), q, bc, b_start)
    k_rel = _each(lambda x, b, be: x * jnp.exp(be - b), key, bc, b_end)
    for y in range(c // SUB_CHUNK - 1):
        def scaled(x, b, bs):
            end_y = jnp.sum(jnp.where(rcol == SUB_CHUNK * y + SUB_CHUNK - 1, b, 0.0), axis=0, keepdims=True)
            return x * jnp.exp(jnp.where(rcol >= SUB_CHUNK * (y + 1), bs - end_y, -jnp.inf))
        dq = _each(scaled, q_rel, bc, b_start)
        in_y = (ci >= SUB_CHUNK * y) & (ci < SUB_CHUNK * (y + 1))
        a_y = _each(lambda x, z: jnp.where(in_y, mm(x, z, NT, prec), 0.0), dq, k_rel)
        o = _each(lambda acc, a, val: acc + mm(a, val, NN, prec), o, a_y, v)
        yield
    k_state = _each(lambda x, bl, b: x * jnp.exp(bl - b), key, b_last, bc)
    st_new = _each(lambda z, bl, val, x: z * jnp.exp(bl) + mm(val, x, TN, prec), st, b_last, v, k_state)
    if o_known is not None:
        o = _each(_known_value, o, o_known)
    return (_each(lambda x, z: _head_norm_gate(x, norm_w, z), o, gb), st_new), o


def _drain(gen):
    try:
        while True:
            next(gen)
    except StopIteration as done:
        return done.value


def _alternate(gen_a, gen_b):
    out, live = [None, None], [gen_a, gen_b]
    while any(g is not None for g in live):
        for i, g in enumerate(live):
            if g is None:
                continue
            try:
                next(g)
            except StopIteration as done:
                out[i], live[i] = done.value, None
    return out


def gdn_chunks(*args, **kwargs):
    return _drain(gdn_stages(*args, **kwargs))


def hgrn_chunks(*args, **kwargs):
    return _drain(hgrn_stages(*args, **kwargs))


def hgrn_chunk(qb, fb, ib, gb, l0, l1, norm_w, st, prec=HGRN_PREC, reuse_output=False):
    args = ([qb], [fb], [ib], [gb], [l0], [l1], norm_w, [st], prec)
    if reuse_output:
        known = lax.stop_gradient(hgrn_chunks(*args)[1])
        (y, st_new), _ = hgrn_chunks(*args, o_known=known)
    else:
        (y, st_new), _ = hgrn_chunks(*args)
    return y[0], st_new[0]


HEAD_VEC = (N_HEADS, 1, LANES)


class _ChunkSpecs:
    def __init__(self, nc, rev):
        self.nc, self.rev = nc, rev

    def _c(self, c):
        return self.nc - 1 - c if self.rev else c

    def row(self, width, block=0):
        return pl.BlockSpec((CHUNK, width), lambda c: (self._c(c), block))

    def per_head(self, rows):
        return pl.BlockSpec((None, N_HEADS, rows, rows), lambda c: (self._c(c), 0, 0, 0))

    @staticmethod
    def whole(shape):
        return pl.BlockSpec(shape, lambda c: (0,) * len(shape))


def _lanes(j):
    return slice(j * LANES, (j + 1) * LANES)


def mixer_fwd(qkv_c, proj, a_log_l, dt_l, gdn_norm_w, l0, l1, hgrn_norm_w, name):
    t = qkv_c.shape[0]
    hb = N_HEADS
    sp = _ChunkSpecs(t // CHUNK, rev=False)
    hs = list(range(hb))

    def body(q_ref, k_ref, v_ref, z_ref, ab_ref, al_ref, dt_ref, gnw_ref, qb_ref, fb_ref, ib_ref, gb_ref, l0_ref, l1_ref,
             hnw_ref, y_ref, hist_a_ref, inv_ref, hist_b_ref, o_ref, sa_ref, sb_ref, shift_ref):
        @pl.when(pl.program_id(0) == 0)
        def _():
            sa_ref[...] = jnp.zeros_like(sa_ref)
            sb_ref[...] = jnp.zeros_like(sb_ref)
            shift_ref[...] = jnp.zeros_like(shift_ref)

        heads = lambda ref: [ref[:, _lanes(j)] for j in hs]
        s_a, s_b = [sa_ref[h] for h in hs], [sb_ref[h] for h in hs]
        for h in hs:
            hist_a_ref[h] = s_a[h]
            hist_b_ref[h] = s_b[h]
        diags = [diag_part(SlotRows(shift_ref.at[h]), differentiable=False) for h in hs]
        ((y_a, s_a_new), inv), ((y_b, s_b_new), o_pre) = _alternate(
            gdn_stages(hs, heads(q_ref), heads(k_ref), heads(v_ref), heads(z_ref), ab_ref[...],
                       [al_ref[h] for h in hs], [dt_ref[h] for h in hs], gnw_ref[...], s_a),
            hgrn_stages(heads(qb_ref), heads(fb_ref), heads(ib_ref), heads(gb_ref),
                        [l0_ref[h] for h in hs], [l1_ref[h] for h in hs], hnw_ref[...], s_b, diags=diags))
        for h in hs:
            y_ref[:, _lanes(h)] = y_a[h].astype(BF16)
            y_ref[:, _lanes(hb + h)] = y_b[h].astype(BF16)
            o_ref[:, _lanes(h)] = o_pre[h]
            sa_ref[h] = s_a_new[h]
            sb_ref[h] = s_b_new[h]
            inv_ref[h] = inv[h]

    vec, gain, slab = sp.whole(HEAD_VEC), sp.whole((1, LANES)), functools.partial(sp.row, GDN_WIDTH)
    states = jax.ShapeDtypeStruct((sp.nc, N_HEADS, HEAD_DIM, HEAD_DIM), F32)
    return pl.pallas_call(
        body, name=name, grid=(sp.nc,),
        in_specs=[slab(0), slab(1), slab(2), slab(3), sp.row(LANES, AB_BLOCK), vec, vec, gain,
                  slab(4), slab(5), slab(6), slab(7), vec, vec, gain],
        out_specs=[sp.row(2 * GDN_WIDTH), sp.per_head(HEAD_DIM), sp.per_head(CHUNK), sp.per_head(HEAD_DIM), slab(0)],
        out_shape=[jax.ShapeDtypeStruct((t, 2 * GDN_WIDTH), BF16), states,
                   jax.ShapeDtypeStruct((sp.nc, N_HEADS, CHUNK, CHUNK), F32), states,
                   jax.ShapeDtypeStruct((t, GDN_WIDTH), F32)],
        scratch_shapes=[pltpu.VMEM((N_HEADS, HEAD_DIM, HEAD_DIM), F32), pltpu.VMEM((N_HEADS, HEAD_DIM, HEAD_DIM), F32),
                        pltpu.VMEM((hb, 3, SHIFT_WAYS, SHIFT_ROWS, LANES), F32)],
        compiler_params=_params(("arbitrary",)),
    )(qkv_c, qkv_c, qkv_c, proj, proj, a_log_l, dt_l, gdn_norm_w, proj, proj, proj, proj, l0, l1, hgrn_norm_w)


def mixer_bwd(qkv_c, proj, a_log_l, dt_l, gdn_norm_w, l0, l1, hgrn_norm_w, hist_a, inv_hist, hist_b, o_pre, dy, name):
    t = qkv_c.shape[0]
    hb = N_HEADS
    sp = _ChunkSpecs(t // CHUNK, rev=True)
    hs = list(range(hb))

    def body(q_ref, k_ref, v_ref, z_ref, ab_ref, al_ref, dt_ref, gnw_ref, qb_ref, fb_ref, ib_ref, gb_ref, l0_ref, l1_ref,
             hnw_ref, hist_a_ref, inv_ref, hist_b_ref, o_ref, dy_ref,
             dqkv_ref, dproj_ref, dal_ref, ddt_ref, dgnw_ref, dl0_ref, dl1_ref, dhnw_ref, dsa_ref, dsb_ref, shift_ref):
        @pl.when(pl.program_id(0) == 0)
        def _():
            for ref in (dal_ref, ddt_ref, dgnw_ref, dl0_ref, dl1_ref, dhnw_ref, dsa_ref, dsb_ref, shift_ref):
                ref[...] = jnp.zeros_like(ref)

        heads = lambda ref, first=0: [ref[:, _lanes(first + j)] for j in hs]
        diags = [diag_part(SlotRows(shift_ref.at[h])) for h in hs]
        inv_known, o_known = [inv_ref[h] for h in hs], heads(o_ref)

        def both(ga, gb):
            (ra, inv), (rb, o_pre) = _alternate(gdn_stages(hs, *ga, inv_known=inv_known),
                                                hgrn_stages(*gb, diags=diags, o_known=o_known))
            return (ra, rb), (inv, o_pre)

        ga = (heads(q_ref), heads(k_ref), heads(v_ref), heads(z_ref), ab_ref[...], [al_ref[h] for h in hs],
              [dt_ref[h] for h in hs], gnw_ref[...], [hist_a_ref[h] for h in hs])
        gb = (heads(qb_ref), heads(fb_ref), heads(ib_ref), heads(gb_ref), [l0_ref[h] for h in hs],
              [l1_ref[h] for h in hs], hnw_ref[...], [hist_b_ref[h] for h in hs])
        _, vjp, _ = jax.vjp(both, ga, gb, has_aux=True)
        dy_a = [x.astype(F32) for x in heads(dy_ref)]
        dy_b = [x.astype(F32) for x in heads(dy_ref, hb)]
        (dq, dk, dv, dz, dab, dal, ddt, dgnw, ds_a), (dqb, dfb, dib, dgb, dl0, dl1, dhnw, ds_b) = vjp(
            ((dy_a, [dsa_ref[h] for h in hs]), (dy_b, [dsb_ref[h] for h in hs])))
        for h in hs:
            dqkv_ref[:, _lanes(h)] = dq[h]
            dqkv_ref[:, _lanes(hb + h)] = dk[h]
            dqkv_ref[:, _lanes(2 * hb + h)] = dv[h]
            for slab, val in enumerate((dz, dqb, dfb, dib, dgb)):
                dproj_ref[:, _lanes((3 + slab) * hb + h)] = val[h].astype(BF16)
            dal_ref[h] += dal[h]
            ddt_ref[h] += ddt[h]
            dl0_ref[h] += dl0[h]
            dl1_ref[h] += dl1[h]
            dsa_ref[h] = ds_a[h]
            dsb_ref[h] = ds_b[h]
        dproj_ref[:, MAIN_WIDTH:] = dab.astype(BF16)
        dgnw_ref[...] += dgnw
        dhnw_ref[...] += dhnw

    vec, gain, slab = sp.whole(HEAD_VEC), sp.whole((1, LANES)), functools.partial(sp.row, GDN_WIDTH)
    vec_shape, gain_shape = jax.ShapeDtypeStruct(HEAD_VEC, F32), jax.ShapeDtypeStruct((1, LANES), F32)
    return pl.pallas_call(
        body, name=name, grid=(sp.nc,),
        in_specs=[slab(0), slab(1), slab(2), slab(3), sp.row(LANES, AB_BLOCK), vec, vec, gain,
                  slab(4), slab(5), slab(6), slab(7), vec, vec, gain,
                  sp.per_head(HEAD_DIM), sp.per_head(CHUNK), sp.per_head(HEAD_DIM), slab(0), sp.row(2 * GDN_WIDTH)],
        out_specs=[sp.row(QKV_WIDTH), sp.row(CAT_WIDTH), vec, vec, gain, vec, vec, gain],
        out_shape=[jax.ShapeDtypeStruct((t, QKV_WIDTH), F32), jax.ShapeDtypeStruct((t, CAT_WIDTH), BF16),
                   vec_shape, vec_shape, gain_shape, vec_shape, vec_shape, gain_shape],
        scratch_shapes=[pltpu.VMEM((N_HEADS, HEAD_DIM, HEAD_DIM), F32), pltpu.VMEM((N_HEADS, HEAD_DIM, HEAD_DIM), F32),
                        pltpu.VMEM((hb, 3, SHIFT_WAYS, SHIFT_ROWS, LANES), F32)],
        compiler_params=_params(("arbitrary",)),
    )(qkv_c, qkv_c, qkv_c, proj, proj, a_log_l, dt_l, gdn_norm_w, proj, proj, proj, proj, l0, l1, hgrn_norm_w,
      hist_a, inv_hist, hist_b, o_pre, dy)


def _adamw(w, g, m, v):
    m = ADAM_B1 * m + (1.0 - ADAM_B1) * g
    v = ADAM_B2 * v + (1.0 - ADAM_B2) * jnp.square(g)
    m_hat = m / (1.0 - ADAM_B1 ** ADAM_STEP)
    v_hat = v / (1.0 - ADAM_B2 ** ADAM_STEP)
    delta = -ADAM_LR * (m_hat / (jnp.sqrt(v_hat) + ADAM_EPS) + ADAM_WD * w)
    return delta, m, v


def adamw_reduce(parts, mine, slot, w, m, v, name, rb=128):
    r, c = w.shape
    rb = min(rb, r)
    n_parts = parts.shape[0]

    def body(slot_ref, p_ref, own_ref, w_ref, m_ref, v_ref, g_ref, d_ref, mo_ref, vo_ref):
        part = lambda d: jnp.where(slot_ref[0] == d, own_ref[...], p_ref[d]).astype(F32)
        g = part(0)
        for d in range(1, n_parts):
            g = g + part(d)
        delta, mn, vn = _adamw(w_ref[...], g, m_ref[...], v_ref[...])
        g_ref[...] = g
        d_ref[...] = delta
        mo_ref[...] = mn
        vo_ref[...] = vn

    blk = pl.BlockSpec((rb, c), lambda i, s: (i, 0))
    return pl.pallas_call(
        body, name=name,
        grid_spec=pltpu.PrefetchScalarGridSpec(
            num_scalar_prefetch=1, grid=(r // rb,),
            in_specs=[pl.BlockSpec((n_parts, rb, c), lambda i, s: (0, i, 0)),
                      pl.BlockSpec((None, rb, c), lambda i, s: (s[0], i, 0)), blk, blk, blk],
            out_specs=[blk] * 4),
        out_shape=[jax.ShapeDtypeStruct((r, c), F32)] * 4,
        compiler_params=_params(("parallel",)))(slot.astype(jnp.int32).reshape(1), parts, mine, w, m, v)


def adamw_small(w, g, m, v, name):
    def body(w_ref, g_ref, m_ref, v_ref, d_ref, mo_ref, vo_ref):
        delta, mn, vn = _adamw(w_ref[...], g_ref[...], m_ref[...], v_ref[...])
        d_ref[...] = delta
        mo_ref[...] = mn
        vo_ref[...] = vn

    vmem = pl.BlockSpec(memory_space=pltpu.VMEM)
    return pl.pallas_call(body, name=name, in_specs=[vmem] * 4, out_specs=[vmem] * 3,
                          out_shape=[jax.ShapeDtypeStruct(w.shape, F32)] * 3)(w, g, m, v)


def _pack(arrays):
    flat = jnp.concatenate([a.reshape(-1).astype(F32) for a in arrays])
    rows = -(-flat.shape[0] // (8 * LANES)) * 8
    return jnp.pad(flat, (0, rows * LANES - flat.shape[0])).reshape(rows, LANES)


def _unpack(packed, shapes):
    flat, out, off = packed.reshape(-1), [], 0
    for s in shapes:
        n = 1
        for d in s:
            n *= d
        out.append(flat[off:off + n].reshape(s))
        off += n
    return out


def _relu2_epilogue(acc, _):
    r = jnp.maximum(acc, 0.0)
    return acc, r * r


def _relu2_bwd_epilogue(acc, a1):
    return (acc * (2.0 * jnp.maximum(a1, 0.0)),)


def kernel(x, w_in, conv_w, gdn_a_log, gdn_dt_bias, gdn_norm_w, hgrn_lb_logits, hgrn_norm_w, w_out, norm_mix_w, norm_ffn_w, w_ff1, w_ff2, norm_final_w, loss_target, m_w_in, m_conv_w, m_gdn_a_log, m_gdn_dt_bias, m_gdn_norm_w, m_hgrn_lb_logits, m_hgrn_norm_w, m_w_out, m_norm_mix_w, m_norm_ffn_w, m_w_ff1, m_w_ff2, m_norm_final_w, v_w_in, v_conv_w, v_gdn_a_log, v_gdn_dt_bias, v_gdn_norm_w, v_hgrn_lb_logits, v_hgrn_norm_w, v_w_out, v_norm_mix_w, v_norm_ffn_w, v_w_ff1, v_w_ff2, v_norm_final_w):
    me = _my_flat()
    xs = x[0]
    target = loss_target[0]
    shard_in = w_in.shape[2]
    shard_conv = conv_w.shape[2]

    tok = lambda t: t[0:1, 0:1]

    half = D_MODEL // 2
    w_in_b = w_in[0].astype(BF16)
    g_in_a, g_conv = gather_two_level([w_in_b[:half], conv_w[0]], "gather_w_in")
    h_g0, t_g0 = exchange_start([w_in_b[half:]], True, "gather_w_in_low_start", after=[g_in_a], peers=CHIP_PEERS)
    h_g1, t_g1 = exchange_start([w_out[0].astype(BF16), w_ff1[0].astype(BF16)], True, "gather_mid_start", after=[t_g0],
                                peers=CHIP_PEERS)
    h_g2, t_g2 = exchange_start([w_ff2[0].astype(BF16)], True, "gather_ff2_start", after=[t_g1], peers=CHIP_PEERS)
    w_cat = weights_to_cat(g_in_a, "weights_to_cat", D_MODEL)
    conv_full = jnp.transpose(g_conv, (1, 0, 2)).reshape(4, QKV_WIDTH)

    lane_b = lambda p: jnp.broadcast_to(p.reshape(N_HEADS, 1, 1), HEAD_VEC)
    a_log_l, dt_l = lane_b(gdn_a_log[0]), lane_b(gdn_dt_bias[0])
    l0 = hgrn_lb_logits[0].reshape(HEAD_VEC)
    l1 = hgrn_lb_logits[1].reshape(HEAD_VEC)

    n1, r1 = rms_fwd(xs, norm_mix_w + tok(t_g1) + tok(t_g2), "rms_mix")
    proj = matmul(n1, w_cat, "nn", "in_proj_high", (BF16,), tn=CAT_WIDTH // 5, tk=half, k_blocks=(0, 1))
    (s_low,), (l_low,) = exchange_wait(h_g0, "gather_w_in_low_wait", after=[proj], copies=len(CHIP_PEERS))
    h_f0, _ = forward_start([l_low], "gather_w_in_low_forward")
    _, (l_low,) = exchange_wait(_one(h_f0, 0), "forward_w_in_low_wait", copies=len(OTHER_CHIPS))
    w_cat = weights_to_cat(_own_slot(l_low, s_low), "weights_to_cat_low", D_MODEL, row0=half, into=w_cat)
    proj = matmul(n1, w_cat, "nn", "in_proj_low", tn=CAT_WIDTH // 5, tk=half, k_blocks=(1, 1), extra=proj,
                  epilogue=lambda acc, high: (acc + high,))
    qkv_c = conv_fwd(proj, conv_full, "conv_fwd")
    y, hist_a, inv_a, hist_b, o_b = mixer_fwd(qkv_c, proj, a_log_l, dt_l, gdn_norm_w, l0, l1, hgrn_norm_w, "mixer_fwd")
    (s_out, s_ff1), (l_out, l_ff1) = exchange_wait(h_g1, "gather_mid_wait", after=[y], copies=len(CHIP_PEERS))
    (s_ff2,), (l_ff2,) = exchange_wait(h_g2, "gather_ff2_wait", after=[y], copies=len(CHIP_PEERS))
    h_fw, _ = forward_start([l_out, l_ff1, l_ff2], "gather_forward_start")
    _, (l_out,) = exchange_wait(_one(h_fw, 0), "forward_out_wait", copies=len(OTHER_CHIPS))
    w_out_full = _own_slot(l_out, s_out).reshape(D_MODEL, D_MODEL)
    h1, n2, r2 = out_proj_rms(y, w_out_full, xs, norm_ffn_w, "out_proj_rms")
    _, (l_ff1,) = exchange_wait(_one(h_fw, 1), "forward_ff1_wait", after=[n2], copies=len(OTHER_CHIPS))
    w_ff1_sh = _own_slot(l_ff1, s_ff1)
    a1, act = matmul(n2, w_ff1_sh, "nn", "ff1", out_dtypes=(F32, BF16), epilogue=_relu2_epilogue, b_shards=True)
    _, (l_ff2,) = exchange_wait(_one(h_fw, 2), "forward_ff2_wait", after=[act], copies=len(OTHER_CHIPS))
    w_ff2_full = _own_slot(l_ff2, s_ff2).reshape(D_FF, D_MODEL)
    loss_sum, dh2_b, d_final = ff2_loss(act, w_ff2_full, h1, norm_final_w.reshape(1, D_MODEL), target, "ff2_loss")

    da1 = matmul(dh2_b, w_ff2_full, "nt", "d_act", out_dtypes=(BF16,), epilogue=_relu2_bwd_epilogue, extra=a1)
    t_all = xs.shape[0]
    dw_ff2 = matmul(act, dh2_b, "tn", "dw_ff2", out_dtypes=(BF16,), tk=t_all)
    p_ff2 = dw_ff2.reshape(N_DEV, D_FF // N_DEV, D_MODEL)
    h_s1, t_s1 = exchange_start([p_ff2], False, "scatter_ff2_start")
    dn2 = matmul(da1, w_ff1_sh, "nt", "d_n2", out_dtypes=(BF16,), after=[t_s1], b_shards=True, k_group=4)
    p_ff1 = matmul(n2, da1, "tn", "dw_ff1", out_dtypes=(BF16,), tn=D_FF // N_DEV, tk=t_all, after=[t_s1], out_shards=True)
    h_s2, t_s2 = exchange_start([p_ff1], False, "scatter_ff1_start")
    dh1_b, d_ffn = rms_bwd(h1, r2, norm_ffn_w + tok(t_s2), dn2, dh2_b, BF16, "rms_ffn_bwd")
    dmix = matmul(dh1_b, w_out_full, "nt", "d_mix", out_dtypes=(BF16,))
    dw_out = matmul(y, dh1_b, "tn", "dw_out", out_dtypes=(BF16,), tk=t_all)
    p_out = dw_out.reshape(N_DEV, D_MODEL // N_DEV, D_MODEL)
    h_s3, t_s3 = exchange_start([p_out], False, "scatter_out_start")
    d_qkv_c, dproj, d_alog_l, d_dt_l, d_gnw, dl0, dl1, d_hnw = mixer_bwd(
        qkv_c, proj, a_log_l, dt_l, gdn_norm_w + tok(t_s3), l0, l1, hgrn_norm_w, hist_a, inv_a, hist_b, o_b, dmix,
        "mixer_bwd")
    dproj, d_conv_full = conv_bwd(proj, d_qkv_c, conv_full, dproj, "conv_bwd")
    dw_cat = matmul(n1, dproj, "tn", "dw_in", out_dtypes=(BF16,), tm=512, tn=CAT_WIDTH // 5, tk=t_all)
    p_in = cat_to_shards(dw_cat, shard_in)
    h_pair, t_s4 = routed_start(p_in, _to_sibling_routes, "scatter_in_pair_start")

    (s_ff2g,), (r_ff2,) = exchange_wait(h_s1, "scatter_ff2_wait", after=[t_s4])
    (s_ff1g,), (r_ff1,) = exchange_wait(h_s2, "scatter_ff1_wait", after=[t_s4])
    (s_outg,), (r_out,) = exchange_wait(h_s3, "scatter_out_wait", after=[t_s4])
    g_w_ff2, d_w_ff2, nm_w_ff2, nv_w_ff2 = adamw_reduce(
        r_ff2, s_ff2g, me, w_ff2[0], m_w_ff2[0], v_w_ff2[0], "adamw_w_ff2")
    g_w_ff1, d_w_ff1, nm_w_ff1, nv_w_ff1 = adamw_reduce(
        r_ff1, s_ff1g, me, w_ff1[0], m_w_ff1[0], v_w_ff1[0], "adamw_w_ff1")
    g_w_out, d_w_out, nm_w_out, nv_w_out = adamw_reduce(
        r_out, s_outg, me, w_out[0], m_w_out[0], v_w_out[0], "adamw_w_out")
    (p_in,), (from_sibling,) = exchange_wait(h_pair, "scatter_in_pair_wait", after=[d_w_ff2, d_w_ff1, d_w_out],
                                             copies=N_CHIPS)
    chip_sums = pair_sum(p_in, from_sibling, "scatter_in_pair_sum")
    h_chips, t_s5 = routed_start(chip_sums, _to_chips_routes, "scatter_in_chips_start")
    dn1 = matmul(dproj, w_cat, "nt", "d_n1", out_dtypes=(BF16,), tm=512, tn=512, tk=CAT_WIDTH, after=[t_s5])
    dx, d_mix = rms_bwd(xs, r1, norm_mix_w, dn1, dh1_b, F32, "rms_mix_bwd")
    (chip_sums,), (r_in,) = exchange_wait(h_chips, "scatter_in_chips_wait", after=[dx], copies=len(OTHER_CHIPS))
    g_w_in, d_w_in, nm_w_in, nv_w_in = adamw_reduce(
        r_in, chip_sums, me // 2, w_in[0], m_w_in[0], v_w_in[0], "adamw_w_in")

    d_lb = jnp.stack([dl0.reshape(GDN_WIDTH), dl1.reshape(GDN_WIDTH)])
    small_shapes = [(1, N_HEADS), (1, N_HEADS), (1, HEAD_DIM), (2, GDN_WIDTH), (1, HEAD_DIM), (1, D_MODEL),
                    (1, D_MODEL), (D_MODEL,), (4, QKV_WIDTH), ()]
    small = _pack([d_alog_l[:, 0, 0], d_dt_l[:, 0, 0], d_gnw, d_lb, d_hnw, d_mix, d_ffn, d_final, d_conv_full,
                   loss_sum[0, 0]])
    red = allreduce_small(small, "allreduce_small")
    g_alog, g_dt, g_gnw, g_lb, g_hnw, g_mix, g_ffn, g_final, g_conv_full, loss = _unpack(red, small_shapes)
    g_conv = lax.dynamic_slice(g_conv_full, (0, me * shard_conv), (4, shard_conv)).reshape(1, 4, shard_conv)
    small_g = [g_alog, g_dt, g_gnw, g_lb, g_hnw, g_mix, g_ffn, g_final, g_conv]
    small_w = [gdn_a_log, gdn_dt_bias, gdn_norm_w, hgrn_lb_logits, hgrn_norm_w, norm_mix_w, norm_ffn_w, norm_final_w, conv_w]
    small_m = [m_gdn_a_log, m_gdn_dt_bias, m_gdn_norm_w, m_hgrn_lb_logits, m_hgrn_norm_w, m_norm_mix_w, m_norm_ffn_w,
               m_norm_final_w, m_conv_w]
    small_v = [v_gdn_a_log, v_gdn_dt_bias, v_gdn_norm_w, v_hgrn_lb_logits, v_hgrn_norm_w, v_norm_mix_w, v_norm_ffn_w,
               v_norm_final_w, v_conv_w]
    shapes = [a.shape for a in small_w]
    d_s, m_s, v_s = adamw_small(_pack(small_w), _pack(small_g), _pack(small_m), _pack(small_v), "adamw_small")
    d_alog, d_dt, d_gn, d_lbl, d_hn, d_nm, d_nf, d_nfin, d_cw = _unpack(d_s, shapes)
    m_alog, m_dt, m_gn, m_lbl, m_hn, m_nm, m_nf, m_nfin, m_cw = _unpack(m_s, shapes)
    v_alog, v_dt, v_gn, v_lbl, v_hn, v_nm, v_nf, v_nfin, v_cw = _unpack(v_s, shapes)

    lead = lambda a: a[None]
    grads = [lead(g_w_in), g_conv, g_alog, g_dt, g_gnw, g_lb, g_hnw, lead(g_w_out), g_mix, g_ffn,
             lead(g_w_ff1), lead(g_w_ff2), g_final]
    deltas = [lead(d_w_in), d_cw, d_alog, d_dt, d_gn, d_lbl, d_hn, lead(d_w_out), d_nm, d_nf,
              lead(d_w_ff1), lead(d_w_ff2), d_nfin]
    new_m = [lead(nm_w_in), m_cw, m_alog, m_dt, m_gn, m_lbl, m_hn, lead(nm_w_out), m_nm, m_nf,
             lead(nm_w_ff1), lead(nm_w_ff2), m_nfin]
    new_v = [lead(nv_w_in), v_cw, v_alog, v_dt, v_gn, v_lbl, v_hn, lead(nv_w_out), v_nm, v_nf,
             lead(nv_w_ff1), lead(nv_w_ff2), v_nfin]
    return (loss, dx[None], *grads, *deltas, *new_m, *new_v)
```

```python
import functools

import jax
import jax.numpy as jnp
from jax import lax
from jax.experimental import pallas as pl
from jax.experimental.pallas import tpu as pltpu

F32 = jnp.float32
BF16 = jnp.bfloat16
HI = lax.Precision.HIGHEST

N_DEV = 8
D_MODEL = 2048
CHUNK = 64
SUB_CHUNK = 16
HEAD_DIM = 128
N_HEADS = 8
GDN_WIDTH = N_HEADS * HEAD_DIM
D_FF = 4 * D_MODEL
QKV_WIDTH = 3 * GDN_WIDTH
MAIN_WIDTH = 8 * GDN_WIDTH
CAT_WIDTH = MAIN_WIDTH + 128
AB_BLOCK = MAIN_WIDTH // 128
NORM_EPS = 1e-6
L2_EPS = 1e-6
LANES = 128
VMEM_LIMIT = 56 * 1024 * 1024

ADAM_LR = 0.001
ADAM_B1 = 0.9
ADAM_B2 = 0.999
ADAM_EPS = 1e-08
ADAM_WD = 0.01
ADAM_STEP = 10

MESH = pl.DeviceIdType.MESH


def _params(sem=None):
    return pltpu.CompilerParams(dimension_semantics=sem, vmem_limit_bytes=VMEM_LIMIT)


def _dot(a, b, dims, prec=None):
    return lax.dot_general(a, b, (dims, ((), ())), precision=prec, preferred_element_type=F32)


NN = ((1,), (0,))
NT = ((1,), (1,))
TN = ((0,), (0,))


def _split_bf16(x, pieces):
    out = []
    for _ in range(pieces - 1):
        p = x.astype(BF16)
        out.append(p)
        x = x - p.astype(F32)
    out.append(x.astype(BF16))
    return out


def _mm_raw(a, b, dims, prec):
    if prec == "hi":
        return _dot(a, b, dims, HI)
    if prec == "bf":
        return _dot(a.astype(BF16), b.astype(BF16), dims)
    a_hi, a_lo = _split_bf16(a, 2)
    b_hi, b_lo = _split_bf16(b, 2)
    return _dot(a_hi, b_hi, dims) + (_dot(a_hi, b_lo, dims) + _dot(a_lo, b_hi, dims))


@functools.partial(jax.custom_vjp, nondiff_argnums=(2, 3))
def mm(a, b, dims, prec):
    return _mm_raw(a, b, dims, prec)


def _mm_fwd(a, b, dims, prec):
    return _mm_raw(a, b, dims, prec), (a, b)


def _mm_bwd(dims, prec, res, ct):
    a, b = res
    if dims == NN:
        return _mm_raw(ct, b, NT, prec), _mm_raw(a, ct, TN, prec)
    if dims == NT:
        return _mm_raw(ct, b, NN, prec), _mm_raw(ct, a, TN, prec)
    return _mm_raw(b, ct, NT, prec), _mm_raw(a, ct, NN, prec)


mm.defvjp(_mm_fwd, _mm_bwd)


def _sel_raw(sel, x, dims):
    sel = sel.astype(BF16)
    p0, p1, p2 = _split_bf16(x, 3)
    return _dot(sel, p0, dims) + (_dot(sel, p1, dims) + _dot(sel, p2, dims))


def _sel_parts(sel, x):
    c = x.shape[0]
    full = _sel_raw(sel, x, NN)
    return tuple(full[i * c:(i + 1) * c] for i in range(sel.shape[0] // c))


@jax.custom_vjp
def sel_sums(sel, x):
    return _sel_parts(sel, x)


def _sel_fwd(sel, x):
    return _sel_parts(sel, x), sel


def _sel_bwd(sel, cts):
    return jnp.zeros_like(sel), _sel_raw(sel, jnp.concatenate(cts, axis=0), TN)


sel_sums.defvjp(_sel_fwd, _sel_bwd)


@jax.custom_vjp
def _known_value(computed, known):
    del computed
    return known


_known_value.defvjp(lambda computed, known: (known, None), lambda _, ct: (ct, jnp.zeros_like(ct)))


def _my_flat():
    return 4 * lax.axis_index("x") + 2 * lax.axis_index("y") + lax.axis_index("c")


def _peer(k):
    x, y, c = lax.axis_index("x"), lax.axis_index("y"), lax.axis_index("c")
    kx, ky, kc = (k >> 2) & 1, (k >> 1) & 1, k & 1
    px = (1 - x) if kx else x
    py = (1 - y) if ky else y
    pc = (1 - c) if kc else c
    return (px, py, pc), 4 * px + 2 * py + pc


def gather_two_level(xs, name):
    n = len(xs)

    def body(*refs):
        x_refs, y_refs = refs[:n], refs[n:2 * n]
        send_sems, recv_sems, local_sems = refs[2 * n:]
        x, y, c = lax.axis_index("x"), lax.axis_index("y"), lax.axis_index("c")
        me, sibling = (x, y, c), (x, y, 1 - c)
        chips = [(1 - x, y), (x, 1 - y), (1 - x, 1 - y)]
        flat = lambda p: 4 * p[0] + 2 * p[1] + p[2]

        def copy(a, k, block, to, src=None):
            return pltpu.make_async_remote_copy(
                src_ref=y_refs[a].at[flat(block)] if src is None else src, dst_ref=y_refs[a].at[flat(block)],
                send_sem=send_sems.at[a, k], recv_sem=recv_sems.at[a, k], device_id=to, device_id_type=MESH)

        mine = [pltpu.make_async_copy(x_refs[a], y_refs[a].at[flat(me)], local_sems.at[a]) for a in range(n)]
        for cp in mine:
            cp.start()
        first = [copy(a, 0, me, sibling, src=x_refs[a]) for a in range(n)]
        first += [copy(a, 1 + j, me, (*chip, c), src=x_refs[a]) for j, chip in enumerate(chips) for a in range(n)]
        for cp in first:
            cp.start()
        passed = []
        for j, chip in enumerate(chips):
            for a in range(n):
                copy(a, 1 + j, (*chip, c), me).wait_recv()
                cp = copy(a, 4 + j, (*chip, c), sibling)
                cp.start()
                passed.append(cp)
        for a in range(n):
            copy(a, 0, sibling, me).wait_recv()
        for j, chip in enumerate(chips):
            for a in range(n):
                copy(a, 4 + j, (*chip, 1 - c), me).wait_recv()
        for cp in first + passed:
            cp.wait_send()
        for cp in mine:
            cp.wait()

    any_spec = pl.BlockSpec(memory_space=pl.ANY)
    return pl.pallas_call(
        body, name=name, out_shape=[jax.ShapeDtypeStruct((N_DEV,) + x.shape, x.dtype) for x in xs],
        in_specs=[any_spec] * n, out_specs=[any_spec] * n,
        scratch_shapes=[pltpu.SemaphoreType.DMA((n, N_DEV - 1)), pltpu.SemaphoreType.DMA((n, N_DEV - 1)),
                        pltpu.SemaphoreType.DMA((n,))],
    )(*xs)


HBM_SPEC = pl.BlockSpec(memory_space=pltpu.HBM)
SEM_SPEC = pl.BlockSpec(memory_space=pltpu.SEMAPHORE)
ANY_SPEC = pl.BlockSpec(memory_space=pl.ANY)
DATAFLOW = pltpu.SideEffectType.DATAFLOW_SIDE_EFFECTING


def _in_hbm(x):
    return pltpu.with_memory_space_constraint(x, pltpu.HBM)


ALL_PEERS = tuple(range(1, N_DEV))
CHIP_PEERS = (1, 2, 4, 6)
OTHER_CHIPS = (2, 4, 6)


def exchange_start(xs, gather, name, after=(), peers=ALL_PEERS):
    n, n_after = len(xs), len(after)

    def body(*refs):
        x_refs, land_refs = refs[:n], refs[n:2 * n]
        sems = refs[2 * n + n_after:2 * n + n_after + 2 * n]
        token = refs[-1]
        me = _my_flat()
        for k in peers:
            peer, peer_flat = _peer(k)
            for a in range(n):
                src = x_refs[a] if gather else x_refs[a].at[peer_flat]
                pltpu.make_async_remote_copy(src_ref=src, dst_ref=land_refs[a].at[me], send_sem=sems[a],
                                             recv_sem=sems[n + a], device_id=peer, device_id_type=MESH).start()
        token[...] = jnp.zeros_like(token)

    lands =[_in_hbm(lax.empty(((N_DEV,) + x.shape) if gather else x.shape, x.dtype)) for x in xs]
    hbm_out = [pltpu.HBM(x.shape, x.dtype) for x in xs] + [pltpu.HBM(l.shape, l.dtype) for l in lands]
    res = pl.pallas_call(
        body, name=name,
        out_shape=(*([pltpu.SemaphoreType.DMA(())] * (2 * n)), *hbm_out, jax.ShapeDtypeStruct((8, LANES), F32)),
        in_specs=[HBM_SPEC] * (2 * n) + [ANY_SPEC] * n_after,
        out_specs=(*([SEM_SPEC] * (2 * n)), *([HBM_SPEC] * (2 * n)), pl.BlockSpec(memory_space=pltpu.VMEM)),
        input_output_aliases={i: 2 * n + i for i in range(2 * n)},
        compiler_params=pltpu.CompilerParams(has_side_effects=DATAFLOW),
    )(*[_in_hbm(x) for x in xs], *lands, *after)
    return (list(res[:2 * n]), list(res[2 * n:3 * n]), list(res[3 * n:4 * n])), res[-1]


def forward_start(lands, name, after=()):
    n, n_after = len(lands), len(after)

    def body(*refs):
        land_refs = refs[:n]
        sems = refs[n + n_after:n + n_after + 2 * n]
        token = refs[-1]
        sibling, _ = _peer(1)
        for a in range(n):
            for k in OTHER_CHIPS:
                _, from_flat = _peer(k)
                slot = land_refs[a].at[from_flat]
                pltpu.make_async_remote_copy(src_ref=slot, dst_ref=slot, send_sem=sems[a], recv_sem=sems[n + a],
                                             device_id=sibling, device_id_type=MESH).start()
        token[...] = jnp.zeros_like(token)

    res = pl.pallas_call(
        body, name=name,
        out_shape=(*([pltpu.SemaphoreType.DMA(())] * (2 * n)), *[pltpu.HBM(l.shape, l.dtype) for l in lands],
                   jax.ShapeDtypeStruct((8, LANES), F32)),
        in_specs=[HBM_SPEC] * n + [ANY_SPEC] * n_after,
        out_specs=(*([SEM_SPEC] * (2 * n)), *([HBM_SPEC] * n), pl.BlockSpec(memory_space=pltpu.VMEM)),
        input_output_aliases={i: 2 * n + i for i in range(n)},
        compiler_params=pltpu.CompilerParams(has_side_effects=DATAFLOW),
    )(*lands, *after)
    return (list(res[:2 * n]), [], list(res[2 * n:3 * n])), res[-1]


def exchange_wait(handle, name, after=(), copies=N_DEV - 1):
    sems, xs, lands = handle
    n, n_x, n_after = len(lands), len(xs), len(after)

    def body(*refs):
        land_refs = refs[n_x:n_x + n]
        sem_refs = refs[n_x + n:n_x + 3 * n]
        for a in range(n):
            every = land_refs[a].at[pl.ds(0, copies)]
            cp = pltpu.make_async_remote_copy(src_ref=every, dst_ref=every, send_sem=sem_refs[a],
                                              recv_sem=sem_refs[n + a], device_id=_peer(1)[0], device_id_type=MESH)
            cp.wait_send()
            cp.wait_recv()

    res = pl.pallas_call(
        body, name=name,
        out_shape=[pltpu.HBM(x.shape, x.dtype) for x in xs] + [pltpu.HBM(l.shape, l.dtype) for l in lands],
        in_specs=[HBM_SPEC] * (n_x + n) + [SEM_SPEC] * (2 * n) + [ANY_SPEC] * n_after,
        out_specs=[HBM_SPEC] * (n_x + n),
        input_output_aliases={i: i for i in range(n_x + n)},
        compiler_params=pltpu.CompilerParams(has_side_effects=DATAFLOW),
    )(*xs, *lands, *sems, *after)
    return list(res[:n_x]), list(res[n_x:])


N_CHIPS = N_DEV // 2


def routed_start(x, routes, name, after=()):
    n_after = len(after)

    def body(*refs):
        x_ref, land_ref = refs[0], refs[1]
        send_sem, recv_sem = refs[2 + n_after], refs[3 + n_after]
        token = refs[-1]
        for src, dst, peer in routes():
            pltpu.make_async_remote_copy(src_ref=x_ref.at[src], dst_ref=land_ref.at[dst], send_sem=send_sem,
                                         recv_sem=recv_sem, device_id=peer, device_id_type=MESH).start()
        token[...] = jnp.zeros_like(token)

    land = _in_hbm(lax.empty((N_CHIPS,) + x.shape[1:], x.dtype))
    res = pl.pallas_call(
        body, name=name,
        out_shape=(pltpu.SemaphoreType.DMA(()), pltpu.SemaphoreType.DMA(()), pltpu.HBM(x.shape, x.dtype),
                   pltpu.HBM(land.shape, land.dtype), jax.ShapeDtypeStruct((8, LANES), F32)),
        in_specs=[HBM_SPEC, HBM_SPEC] + [ANY_SPEC] * n_after,
        out_specs=(SEM_SPEC, SEM_SPEC, HBM_SPEC, HBM_SPEC, pl.BlockSpec(memory_space=pltpu.VMEM)),
        input_output_aliases={0: 2, 1: 3},
        compiler_params=pltpu.CompilerParams(has_side_effects=DATAFLOW),
    )(_in_hbm(x), land, *after)
    return ([res[0], res[1]], [res[2]], [res[3]]), res[-1]


def _to_sibling_routes():
    c = lax.axis_index("c")
    sibling, _ = _peer(1)
    return [(2 * chip + 1 - c, chip, sibling) for chip in range(N_CHIPS)]


def _to_chips_routes():
    my_chip = _my_flat() // 2
    routes = []
    for k in OTHER_CHIPS:
        peer, peer_flat = _peer(k)
        routes.append((peer_flat // 2, my_chip, peer))
    return routes


def pair_sum(p, from_sibling, name, rb=1024):
    _, r, c = p.shape
    mine = lax.axis_index("c").astype(jnp.int32).reshape(1)

    def body(kind_ref, p_ref, s_ref, o_ref):
        del kind_ref
        o_ref[...] = (p_ref[...].astype(F32) + s_ref[...].astype(F32)).astype(BF16)

    return pl.pallas_call(
        body, name=name,
        grid_spec=pltpu.PrefetchScalarGridSpec(
            num_scalar_prefetch=1, grid=(N_CHIPS, r // rb),
            in_specs=[pl.BlockSpec((None, None, rb, c), lambda chip, i, kind: (chip, kind[0], i, 0)),
                      pl.BlockSpec((None, rb, c), lambda chip, i, kind: (chip, i, 0))],
            out_specs=pl.BlockSpec((None, rb, c), lambda chip, i, kind: (chip, i, 0))),
        out_shape=jax.ShapeDtypeStruct((N_CHIPS, r, c), BF16),
        compiler_params=_params(("parallel", "parallel")))(mine, p.reshape(N_CHIPS, 2, r, c), from_sibling)


def _one(handle, a):
    sems, xs, lands = handle
    n = len(lands)
    return [sems[a], sems[n + a]], xs[a:a + 1], [lands[a]]


def _own_slot(land, block):
    return lax.dynamic_update_slice(land, block[None], (_my_flat(),) + (0,) * block.ndim)


def allreduce_small(x, name):
    rows = x.shape[0]

    def body(x_ref, o_ref, buf, send_sems, recv_sems):
        me = _my_flat()
        buf[me] = x_ref[...]
        sends = []
        for k in range(1, N_DEV):
            peer, _ = _peer(k)
            cp = pltpu.make_async_remote_copy(
                src_ref=x_ref, dst_ref=buf.at[me], send_sem=send_sems.at[k], recv_sem=recv_sems.at[k],
                device_id=peer, device_id_type=MESH)
            cp.start()
            sends.append(cp)
        for k in range(1, N_DEV):
            peer, peer_flat = _peer(k)
            pltpu.make_async_remote_copy(
                src_ref=x_ref, dst_ref=buf.at[peer_flat], send_sem=send_sems.at[k], recv_sem=recv_sems.at[k],
                device_id=peer, device_id_type=MESH).wait_recv()
        for cp in sends:
            cp.wait_send()
        acc = buf[0]
        for d in range(1, N_DEV):
            acc = acc + buf[d]
        o_ref[...] = acc

    vmem = pl.BlockSpec(memory_space=pltpu.VMEM)
    return pl.pallas_call(
        body, name=name, out_shape=jax.ShapeDtypeStruct((rows, LANES), F32),
        in_specs=[vmem], out_specs=vmem,
        scratch_shapes=[pltpu.VMEM((N_DEV, rows, LANES), F32),
                        pltpu.SemaphoreType.DMA((N_DEV,)), pltpu.SemaphoreType.DMA((N_DEV,))],
    )(x)


def matmul(a, b, mode, name, out_dtypes=(F32,), epilogue=None, extra=None, tm=1024, tn=1024, tk=2048, after=(),
           b_shards=False, out_shards=False, k_group=1, k_blocks=None):
    if b_shards:
        n_sh, b_rows, b_cols = b.shape
    if mode == "nn":
        (m, kd), n = a.shape, (n_sh * b_cols if b_shards else b.shape[1])
        if b_shards:
            tn = b_cols
    elif mode == "nt":
        (m, kd), n = a.shape, (b_rows if b_shards else b.shape[0])
        if b_shards:
            tk = k_group * b_cols
    else:
        (kd, m), n = a.shape, b.shape[1]
    tm, tn, tk = min(tm, m), min(tn, n), min(tk, kd)
    assert m % tm == 0 and n % tn == 0 and kd % tk == 0, (name, m, n, kd, tm, tn, tk)
    k0, ksteps = (0, kd // tk) if k_blocks is None else k_blocks
    dims = {"nn": NN, "nt": NT, "tn": TN}[mode]
    n_out = len(out_dtypes)
    n_in = 2 + (extra is not None) + len(after)

    def finish(acc, e_ref, o_refs):
        outs = (acc,) if epilogue is None else epilogue(acc, e_ref[...] if e_ref is not None else None)
        for o_ref, o in zip(o_refs, outs):
            o_ref[...] = o.astype(o_ref.dtype)

    def product(a_ref, b_ref):
        if mode == "nt" and b_shards:
            w = b_cols
            parts = [_dot(a_ref[:, s * w:(s + 1) * w], b_ref[s], dims) for s in range(k_group)]
            return functools.reduce(lambda p, q: p + q, parts)
        return _dot(a_ref[...], b_ref[...], dims)

    def body(*refs):
        a_ref, b_ref = refs[0], refs[1]
        e_ref = refs[2] if extra is not None else None
        o_refs = refs[n_in:n_in + n_out]
        if ksteps == 1:
            finish(product(a_ref, b_ref), e_ref, o_refs)
            return
        acc_ref = refs[-1]
        kk = pl.program_id(2)

        @pl.when(kk == 0)
        def _():
            acc_ref[...] = jnp.zeros_like(acc_ref)

        acc_ref[...] += product(a_ref, b_ref)

        @pl.when(kk == ksteps - 1)
        def _():
            finish(acc_ref[...], e_ref, o_refs)

    if mode == "nn":
        a_spec = pl.BlockSpec((tm, tk), lambda i, j, k: (i, k0 + k))
        b_spec = (pl.BlockSpec((None, tk, tn), lambda i, j, k: (j, k, 0)) if b_shards
                  else pl.BlockSpec((tk, tn), lambda i, j, k: (k0 + k, j)))
    elif mode == "nt":
        a_spec = pl.BlockSpec((tm, tk), lambda i, j, k: (i, k))
        b_spec = (pl.BlockSpec((k_group, tn, b_cols), lambda i, j, k: (k, j, 0)) if b_shards
                  else pl.BlockSpec((tn, tk), lambda i, j, k: (j, k)))
    else:
        a_spec = pl.BlockSpec((tk, tm), lambda i, j, k: (k, i))
        b_spec = pl.BlockSpec((tk, tn), lambda i, j, k: (k, j))
    o_spec = pl.BlockSpec((tm, tn), lambda i, j, k: (i, j))
    res_spec = pl.BlockSpec((None, tm, tn), lambda i, j, k: (j, i, 0)) if out_shards else o_spec
    res_shape = (n // tn, m, tn) if out_shards else (m, n)
    in_specs = [a_spec, b_spec] + ([o_spec] if extra is not None else []) + [ANY_SPEC] * len(after)
    args = (a, b) + ((extra,) if extra is not None else ()) + tuple(after)
    res = pl.pallas_call(
        body, name=name, grid=(m // tm, n // tn, ksteps),
        in_specs=in_specs, out_specs=[res_spec] * n_out,
        out_shape=[jax.ShapeDtypeStruct(res_shape, dt) for dt in out_dtypes],
        scratch_shapes=[pltpu.VMEM((tm, tn), F32)] if ksteps > 1 else [],
        compiler_params=_params(("parallel", "parallel", "arbitrary")),
    )(*args)
    return res if n_out > 1 else res[0]


GATE_COL = 4 * GDN_WIDTH
RELAYOUT_ROWS = 256


def _cat_of_win(j):
    if j < GATE_COL:
        return j
    if j < GATE_COL + 2 * N_HEADS:
        return MAIN_WIDTH + (j - GATE_COL)
    return j - 2 * N_HEADS


def _win_of_cat(c):
    if c < GATE_COL:
        return c
    if c < MAIN_WIDTH:
        return c + 2 * N_HEADS
    if c < MAIN_WIDTH + 2 * N_HEADS:
        return GATE_COL + (c - MAIN_WIDTH)
    return None


def _runs(first, count, mapping):
    runs, i = [], 0
    while i < count:
        start, n = mapping(first + i), 1
        while i + n < count and mapping(first + i + n) == start + n:
            n += 1
        runs.append((start, n))
        i += n
    return runs


def weights_to_cat(g_in, name, total_rows, row0=0, into=None):
    n_dev, rows, shard = g_in.shape
    first = row0 // RELAYOUT_ROWS

    def body(x_ref, *rest):
        o_ref = rest[-1]
        for b in range(CAT_WIDTH // LANES):
            live = sum(_win_of_cat(LANES * b + i) is not None for i in range(LANES))
            parts = []
            for start, n in _runs(LANES * b, live, _win_of_cat):
                while n > 0:
                    d, o = divmod(start, shard)
                    take = min(n, shard - o)
                    parts.append(x_ref[d, :, o:o + take])
                    start, n = start + take, n - take
            if live < LANES:
                parts.append(jnp.zeros((RELAYOUT_ROWS, LANES - live), g_in.dtype))
            o_ref[:, LANES * b:LANES * (b + 1)] = parts[0] if len(parts) == 1 else jnp.concatenate(parts, axis=1)

    return pl.pallas_call(
        body, name=name, grid=(rows // RELAYOUT_ROWS,),
        in_specs=[pl.BlockSpec((n_dev, RELAYOUT_ROWS, shard), lambda i: (0, i, 0))] + ([ANY_SPEC] if into is not None else []),
        out_specs=pl.BlockSpec((RELAYOUT_ROWS, CAT_WIDTH), lambda i: (first + i, 0)),
        out_shape=jax.ShapeDtypeStruct((total_rows, CAT_WIDTH), g_in.dtype),
        input_output_aliases={1: 0} if into is not None else {},
        compiler_params=_params(("parallel",)))(*((g_in,) if into is None else (g_in, into)))


def cat_to_shards(dw_cat, shard):
    rows = dw_cat.shape[0]

    def body(x_ref, o_ref):
        for d in range(N_DEV):
            for t0 in range(0, shard, LANES):
                width = min(LANES, shard - t0)
                parts = [x_ref[:, c:c + n] for c, n in _runs(d * shard + t0, width, _cat_of_win)]
                o_ref[d, :, t0:t0 + width] = parts[0] if len(parts) == 1 else jnp.concatenate(parts, axis=1)

    return pl.pallas_call(
        body, name="cat_to_shards", grid=(rows // RELAYOUT_ROWS,),
        in_specs=[pl.BlockSpec((RELAYOUT_ROWS, CAT_WIDTH), lambda i: (i, 0))],
        out_specs=pl.BlockSpec((N_DEV, RELAYOUT_ROWS, shard), lambda i: (0, i, 0)),
        out_shape=jax.ShapeDtypeStruct((N_DEV, rows, shard), dw_cat.dtype),
        compiler_params=_params(("parallel",)))(dw_cat)


ROW_BLOCK = 512


def rms_fwd(x, w, name):
    t, d = x.shape

    def body(x_ref, w_ref, n_ref, r_ref):
        h = x_ref[...]
        r = lax.rsqrt(jnp.mean(h * h, axis=-1, keepdims=True) + NORM_EPS)
        n_ref[...] = (h * r * w_ref[...]).astype(BF16)
        r_ref[...] = r

    row = pl.BlockSpec((ROW_BLOCK, d), lambda i: (i, 0))
    return pl.pallas_call(
        body, name=name, grid=(t // ROW_BLOCK,),
        in_specs=[row, pl.BlockSpec((1, d), lambda i: (0, 0))],
        out_specs=[row, pl.BlockSpec((ROW_BLOCK, 1), lambda i: (i, 0))],
        out_shape=[jax.ShapeDtypeStruct((t, d), BF16), jax.ShapeDtypeStruct((t, 1), F32)],
        compiler_params=_params(("parallel",)))(x, w)


FUSED_ROWS = 512


def out_proj_rms(y, w_out, x, w_norm, name):
    t, d = x.shape

    def body(y_ref, w_ref, x_ref, g_ref, h_ref, n_ref, r_ref):
        h = x_ref[...] + _dot(y_ref[...], w_ref[...], NN)
        r = lax.rsqrt(jnp.mean(h * h, axis=-1, keepdims=True) + NORM_EPS)
        h_ref[...] = h
        n_ref[...] = (h * r * g_ref[...]).astype(BF16)
        r_ref[...] = r

    row = pl.BlockSpec((FUSED_ROWS, d), lambda i: (i, 0))
    return pl.pallas_call(
        body, name=name, grid=(t // FUSED_ROWS,),
        in_specs=[pl.BlockSpec((FUSED_ROWS, y.shape[1]), lambda i: (i, 0)), pl.BlockSpec(w_out.shape, lambda i: (0, 0)),
                  row, pl.BlockSpec((1, d), lambda i: (0, 0))],
        out_specs=[row, row, pl.BlockSpec((FUSED_ROWS, 1), lambda i: (i, 0))],
        out_shape=[jax.ShapeDtypeStruct((t, d), F32), jax.ShapeDtypeStruct((t, d), BF16),
                   jax.ShapeDtypeStruct((t, 1), F32)],
        compiler_params=_params(("parallel",)))(y, w_out, x, w_norm)


def ff2_loss(act, w_ff2, h1, w, target, name, tk=2048):
    t, d = h1.shape
    ksteps = act.shape[1] // tk

    def body(a_ref, b_ref, h_ref, w_ref, t_ref, loss_ref, dhb_ref, dw_ref, acc_ref):
        i, kk = pl.program_id(0), pl.program_id(1)

        @pl.when((i == 0) & (kk == 0))
        def _():
            loss_ref[...] = jnp.zeros_like(loss_ref)
            dw_ref[...] = jnp.zeros_like(dw_ref)

        @pl.when(kk == 0)
        def _():
            acc_ref[...] = h_ref[...]

        acc_ref[...] += _dot(a_ref[...], b_ref[...], NN)

        @pl.when(kk == ksteps - 1)
        def _():
            h = acc_ref[...]
            wv = w_ref[...]
            r = lax.rsqrt(jnp.mean(h * h, axis=-1, keepdims=True) + NORM_EPS)
            yn = h * r
            e = yn * wv - t_ref[...]
            loss_ref[...] += 0.5 * jnp.sum(jnp.sum(e * e, axis=-1, keepdims=True), axis=0, keepdims=True) / d
            dy = e / d
            dw_ref[...] += jnp.sum(dy * yn, axis=0, keepdims=True)
            dyn = dy * wv
            dhb_ref[...] = (r * (dyn - yn * jnp.mean(dyn * yn, axis=-1, keepdims=True))).astype(BF16)

    row = pl.BlockSpec((FUSED_ROWS, d), lambda i, k: (i, 0))
    wspec = pl.BlockSpec((1, d), lambda i, k: (0, 0))
    return pl.pallas_call(
        body, name=name, grid=(t // FUSED_ROWS, ksteps),
        in_specs=[pl.BlockSpec((FUSED_ROWS, tk), lambda i, k: (i, k)), pl.BlockSpec((tk, d), lambda i, k: (k, 0)),
                  row, wspec, row],
        out_specs=[pl.BlockSpec((1, 1), lambda i, k: (0, 0)), row, wspec],
        out_shape=[jax.ShapeDtypeStruct((1, 1), F32), jax.ShapeDtypeStruct((t, d), BF16),
                   jax.ShapeDtypeStruct((1, d), F32)],
        scratch_shapes=[pltpu.VMEM((FUSED_ROWS, d), F32)],
        compiler_params=_params(("arbitrary", "arbitrary")))(act, w_ff2, h1, w, target)


def rms_bwd(h, r, w, dn, dres, out_dtype, name):
    t, d = h.shape

    def body(h_ref, r_ref, w_ref, dn_ref, dres_ref, dh_ref, dw_ref):
        @pl.when(pl.program_id(0) == 0)
        def _():
            dw_ref[...] = jnp.zeros_like(dw_ref)

        rv = r_ref[...]
        yn = h_ref[...] * rv
        dnv = dn_ref[...].astype(F32)
        dw_ref[...] += jnp.sum(dnv * yn, axis=0, keepdims=True)
        dyn = dnv * w_ref[...]
        dh = dres_ref[...].astype(F32) + rv * (dyn - yn * jnp.mean(dyn * yn, axis=-1, keepdims=True))
        dh_ref[...] = dh.astype(out_dtype)

    row = pl.BlockSpec((ROW_BLOCK, d), lambda i: (i, 0))
    wspec = pl.BlockSpec((1, d), lambda i: (0, 0))
    rspec = pl.BlockSpec((ROW_BLOCK, 1), lambda i: (i, 0))
    return pl.pallas_call(
        body, name=name, grid=(t // ROW_BLOCK,),
        in_specs=[row, rspec, wspec, row, row], out_specs=[row, wspec],
        out_shape=[jax.ShapeDtypeStruct((t, d), out_dtype), jax.ShapeDtypeStruct((1, d), F32)],
        compiler_params=_params(("arbitrary",)))(h, r, w, dn, dres)


CONV_ROWS = 512
TILE_ROWS = 8


def _iota2(shape, axis):
    return lax.broadcasted_iota(jnp.int32, shape, axis)


def _silu(x):
    return x * jax.nn.sigmoid(x)


def _conv_rows(x_ref, w, first, rows):
    acc = None
    for j in range(4):
        term = x_ref[first - 3 + j:first - 3 + j + rows, :] * w[j:j + 1, :]
        acc = term if acc is None else acc + term
    return acc


def _head_shifts(head):
    rows = _iota2((TILE_ROWS, 1), 0)
    return [jnp.where(rows >= 3 - j, head if j == 3 else pltpu.roll(head, 3 - j, 0), 0.0) for j in range(4)]


def _conv_chunks(t):
    pieces = [(TILE_ROWS, min(CONV_ROWS, t) - TILE_ROWS)]
    pieces += [(r, CONV_ROWS) for r in range(CONV_ROWS, t, CONV_ROWS)]
    return pieces


def conv_fwd(proj, conv_w, name):
    t = proj.shape[0]

    def body(x_ref, w_ref, o_ref):
        w = w_ref[...]
        shifted = _head_shifts(x_ref[0:TILE_ROWS, :])
        o_ref[0:TILE_ROWS, :] = _silu(sum(shifted[j] * w[j:j + 1, :] for j in range(4)))
        for first, rows in _conv_chunks(t):
            o_ref[first:first + rows, :] = _silu(_conv_rows(x_ref, w, first, rows))

    col = pl.BlockSpec((t, LANES), lambda c: (0, c))
    return pl.pallas_call(
        body, name=name, grid=(QKV_WIDTH // LANES,),
        in_specs=[col, pl.BlockSpec((4, LANES), lambda c: (0, c))], out_specs=col,
        out_shape=jax.ShapeDtypeStruct((t, QKV_WIDTH), F32),
        compiler_params=_params(("parallel",)))(proj, conv_w)


def conv_bwd(proj, dout, conv_w, dproj, name):
    t = proj.shape[0]

    def dsilu(pre):
        sg = jax.nn.sigmoid(pre)
        return sg * (1.0 + pre * (1.0 - sg))

    def body(x_ref, d_ref, w_ref, dproj_in, dx_ref, dw_ref, stage):
        del dproj_in
        w = w_ref[...]
        shifted = _head_shifts(x_ref[0:TILE_ROWS, :])
        head_dpre = d_ref[0:TILE_ROWS, :] * dsilu(sum(shifted[j] * w[j:j + 1, :] for j in range(4)))
        stage[0:TILE_ROWS, :] = head_dpre
        for first, rows in _conv_chunks(t):
            stage[first:first + rows, :] = d_ref[first:first + rows, :] * dsilu(_conv_rows(x_ref, w, first, rows))
        stage[t:t + TILE_ROWS, :] = jnp.zeros((TILE_ROWS, LANES), F32)
        for first, rows in [(0, TILE_ROWS)] + _conv_chunks(t):
            dx = None
            for j in range(4):
                term = stage[first + 3 - j:first + 3 - j + rows, :] * w[j:j + 1, :]
                dx = term if dx is None else dx + term
            dx_ref[first:first + rows, :] = dx.astype(BF16)
        dw = [jnp.sum(head_dpre * shifted[j], axis=0, keepdims=True) for j in range(4)]
        for first, rows in _conv_chunks(t):
            dpre = stage[first:first + rows, :]
            for j in range(4):
                dw[j] = dw[j] + jnp.sum(dpre * x_ref[first - 3 + j:first - 3 + j + rows, :], axis=0, keepdims=True)
        dw_ref[...] = jnp.concatenate(dw, axis=0)

    col = pl.BlockSpec((t, LANES), lambda c: (0, c))
    taps = pl.BlockSpec((4, LANES), lambda c: (0, c))
    return pl.pallas_call(
        body, name=name, grid=(QKV_WIDTH // LANES,),
        in_specs=[col, col, taps, ANY_SPEC], out_specs=[col, taps],
        out_shape=[jax.ShapeDtypeStruct(dproj.shape, BF16), jax.ShapeDtypeStruct((4, QKV_WIDTH), F32)],
        scratch_shapes=[pltpu.VMEM((t + TILE_ROWS, LANES), F32)],
        input_output_aliases={3: 0},
        compiler_params=_params(("parallel",)))(proj, dout, conv_w, dproj)


def _softplus(x):
    return jnp.maximum(x, 0.0) + jnp.log(1.0 + jnp.exp(-jnp.abs(x)))


def _head_norm_gate(o, norm_w, gate):
    return o * lax.rsqrt(jnp.mean(o * o, axis=-1, keepdims=True) + NORM_EPS) * norm_w * _silu(gate)


GDN_PREC = ("bf", "bf")
HGRN_PREC = "bf"


def _each(fn, *cols):
    return [fn(*a) for a in zip(*cols)]


@functools.partial(jax.custom_vjp, nondiff_argnums=(2,))
def _known_inverse(low, inv, prec):
    del low, prec
    return inv


def _known_inverse_fwd(low, inv, prec):
    del low
    return inv, inv


def _known_inverse_bwd(prec, inv, ct):
    return -_mm_raw(_mm_raw(inv, ct, TN, prec), inv, NT, prec), jnp.zeros_like(inv)


_known_inverse.defvjp(_known_inverse_fwd, _known_inverse_bwd)


def gdn_stages(hs, qc, kc, vc, zc, ab, a_log_l, dt_l, norm_w, s, prec=GDN_PREC, inv_known=None):
    p_inv, p_mm = prec
    c = CHUNK
    ri, ci = _iota2((c, c), 0), _iota2((c, c), 1)
    incl, strict, eye = ri >= ci, ri > ci, ri == ci
    lane = _iota2((c, LANES), 1)
    last_row = _iota2((c, 1), 0) == c - 1
    rowsum = lambda x: jnp.sum(x, axis=1, keepdims=True)

    def row(col):
        return jnp.sum(jnp.where(eye, col, 0.0), axis=0, keepdims=True)

    q = _each(lambda x: x * lax.rsqrt(rowsum(x * x) + L2_EPS) * (HEAD_DIM ** -0.5), qc)
    k = _each(lambda x: x * lax.rsqrt(rowsum(x * x) + L2_EPS), kc)
    yield
    a_col = [rowsum(jnp.where(lane == h, ab, 0.0)) for h in hs]
    b_col = [rowsum(jnp.where(lane == h + N_HEADS, ab, 0.0)) for h in hs]
    beta = _each(jax.nn.sigmoid, b_col)
    g = _each(lambda a, al, dl: rowsum(jnp.where(lane == 0, -jnp.exp(al) * _softplus(a + dl), 0.0)), a_col, a_log_l, dt_l)
    gcum = _each(lambda x: rowsum(jnp.where(incl, row(x), 0.0)), g)
    g_last = _each(lambda x: jnp.sum(jnp.where(last_row, x, 0.0), axis=0, keepdims=True), gcum)
    decay = _each(lambda x: jnp.exp(jnp.where(incl, x - row(x), -jnp.inf)), gcum)
    yield
    kk = _each(lambda x: mm(x, x, NT, p_mm), k)
    low = _each(lambda b, x, d: jnp.where(strict, b * x * d, 0.0), beta, kk, decay)
    yield
    if inv_known is None:
        power = _each(lambda x: -x, low)
        inv = _each(lambda x: jnp.where(eye, 1.0, 0.0) + x, power)
        for _ in range(5):
            power = _each(lambda x: mm(x, x, NN, p_inv), power)
            yield
            inv = _each(lambda x, p: x + mm(x, p, NN, p_inv), inv, power)
            yield
    else:
        inv = _each(lambda x, known: _known_inverse(x, known, p_inv), low, inv_known)
    exp_g = _each(jnp.exp, gcum)
    yield
    u_v = _each(lambda i, b, x: mm(i, b * x, NN, p_mm), inv, beta, vc)
    w = _each(lambda i, b, e, x: mm(i, b * e * x, NN, p_mm), inv, beta, exp_g, k)
    yield
    attn = _each(lambda x, y, d: mm(x, y, NT, p_mm) * d, q, k, decay)
    yield
    u = _each(lambda x, y, z: x - mm(y, z, NN, p_mm), u_v, w, s)
    yield
    o = _each(lambda x, e, z: mm(x * e, z, NN, p_mm), q, exp_g, s)
    o = _each(lambda x, a, y: x + mm(a, y, NN, p_mm), o, attn, u)
    yield
    k_end = _each(lambda x, gl, gc: x * jnp.exp(gl - gc), k, g_last, gcum)
    s_new = _each(lambda z, gl, x, y: z * jnp.exp(gl) + mm(x, y, TN, p_mm), s, g_last, k_end, u)
    return (_each(lambda x, z: _head_norm_gate(x, norm_w, z), o, zc), s_new), inv


def gdn_chunk(h, qc, kc, vc, zc, ab, a_log_l, dt_l, norm_w, s, prec=GDN_PREC, reuse_inverse=False):
    args = ([h], [qc], [kc], [vc], [zc], ab, [a_log_l], [dt_l], norm_w, [s], prec)
    if reuse_inverse:
        inv = lax.stop_gradient(gdn_chunks(*args)[1])
        (y, s_new), _ = gdn_chunks(*args, inv_known=inv)
    else:
        (y, s_new), _ = gdn_chunks(*args)
    return y[0], s_new[0]


DIAG_ROWS = SUB_CHUNK // 2
SHIFT_PAD = 8
SHIFT_ROWS = SHIFT_PAD + CHUNK + SHIFT_PAD
SHIFT_WAYS = 4


class RolledRows:
    def down(self, x, which):
        del which
        return [x] + [pltpu.roll(x, off, 0) for off in range(1, DIAG_ROWS)]

    def up_sum(self, parts, which):
        del which
        acc = parts[0]
        for off in range(1, DIAG_ROWS):
            acc = acc + pltpu.roll(parts[off], CHUNK - off, 0)
        return acc


class SlotRows:
    def __init__(self, slots):
        self.slots = slots

    def down(self, x, which):
        self.slots[which, 0, SHIFT_PAD:SHIFT_PAD + CHUNK, :] = x
        return [x] + [self.slots[which, 0, SHIFT_PAD - off:SHIFT_PAD + CHUNK - off, :] for off in range(1, DIAG_ROWS)]

    def up_sum(self, parts, which):
        acc = parts[0]
        for off in range(1, DIAG_ROWS):
            way = 1 + off % (SHIFT_WAYS - 1)
            self.slots[which, way, SHIFT_PAD:SHIFT_PAD + CHUNK, :] = parts[off]
            acc = acc + self.slots[which, way, SHIFT_PAD + off:SHIFT_PAD + CHUNK + off, :]
        return acc


def _sub_block_rows():
    return jnp.bitwise_and(_iota2((CHUNK, 1), 0), DIAG_ROWS - 1)


def _diag_forward(rows, q, key, bc, v):
    rmod = _sub_block_rows()
    k_d, b_d, v_d = rows.down(key, 0), rows.down(bc, 1), rows.down(v, 2)
    o = None
    for off in range(DIAG_ROWS):
        e = jnp.exp(jnp.where(rmod >= off, bc - b_d[off], -jnp.inf))
        term = jnp.sum(q * k_d[off] * e, axis=-1, keepdims=True) * v_d[off]
        o = term if o is None else o + term
    return o


def _diag_backward(rows, q, key, bc, v, do):
    rmod = _sub_block_rows()
    k_d, b_d, v_d = rows.down(key, 0), rows.down(bc, 1), rows.down(v, 2)
    dq = db = None
    dk_parts, db_parts, dv_parts = [], [], []
    for off in range(DIAG_ROWS):
        e = jnp.exp(jnp.where(rmod >= off, bc - b_d[off], -jnp.inf))
        qe = q * e
        a = jnp.sum(qe * k_d[off], axis=-1, keepdims=True)
        da = jnp.sum(do * v_d[off], axis=-1, keepdims=True)
        dv_parts.append(a * do)
        dq_term = (da * e) * k_d[off]
        dk_term = da * qe
        s = dk_term * k_d[off]
        dq = dq_term if dq is None else dq + dq_term
        db = s if db is None else db + s
        dk_parts.append(dk_term)
        db_parts.append(s)
    return dq, rows.up_sum(dk_parts, 0), db - rows.up_sum(db_parts, 1), rows.up_sum(dv_parts, 2)


def diag_part(rows, differentiable=True):
    forward = functools.partial(_diag_forward, rows)
    if not differentiable:
        return forward
    part = jax.custom_vjp(forward)
    part.defvjp(lambda q, key, bc, v: (forward(q, key, bc, v), (q, key, bc, v)),
                lambda res, do: _diag_backward(rows, *res, do))
    return part


def hgrn_stages(qb, fb, ib, gb, l0, l1, norm_w, st, prec=HGRN_PREC, diags=None, o_known=None):
    c = CHUNK
    ri, ci = _iota2((4 * c, c), 0), _iota2((4 * c, c), 1)
    rcol = _iota2((c, 1), 0)
    blk0 = jnp.bitwise_and(ri, c - SUB_CHUNK)
    limit = jnp.where(ri < c, ri + 1, jnp.where(ri < 2 * c, blk0, jnp.where(ri < 3 * c, blk0 + SUB_CHUNK,
                                                                          blk0 + DIAG_ROWS)))
    sel = jnp.where(ci < limit, 1.0, 0.0)
    ri, ci = _iota2((c, c), 0), _iota2((c, c), 1)
    lb = _each(lambda a, b: jax.nn.sigmoid(a - b), l0, l1)
    forget = _each(lambda b, f: b + (1.0 - b) * jax.nn.sigmoid(f), lb, fb)
    key = _each(lambda b, f: (1.0 - b) * jax.nn.sigmoid(-f), lb, fb)
    q = _each(_silu, qb)
    v = ib
    logf = _each(jnp.log, forget)
    sums = _each(lambda x: sel_sums(sel, x), logf)
    bc, b_start, b_end, b_half = ([x[i] for x in sums] for i in range(4))
    b_last = _each(lambda x: jnp.sum(x, axis=0, keepdims=True), logf)
    o = _each(lambda x, b, z: mm(x * jnp.exp(b), z, NT, prec), q, bc, st)
    if diags is None:
        diags = [diag_part(RolledRows())] * len(qb)
    yield
    o = list(o)
    for h in range(len(o)):
        o[h] = o[h] + diags[h](q[h], key[h], bc[h], v[h])
        yield
    second = jnp.bitwise_and(rcol, SUB_CHUNK - 1) >= DIAG_ROWS
    same_sub = jnp.bitwise_and(ri, c - SUB_CHUNK) == jnp.bitwise_and(ci, c - SUB_CHUNK)
    q_half = _each(lambda x, b, bh: x * jnp.exp(jnp.where(second, b - bh, -jnp.inf)), q, bc, b_half)
    k_half = _each(lambda x, b, bh: x * jnp.exp(jnp.where(second, -jnp.inf, bh - b)), key, bc, b_half)
    a_half = _each(lambda x, z: jnp.where(same_sub, mm(x, z, NT, prec), 0.0), q_half, k_half)
    o = _each(lambda acc, a, val: acc + mm(a, val, NN, prec), o, a_half, v)
    yield
    q_rel = _each(lambda x, b, bs: x * jnp.exp(b - bs), q, bc, b_start)
    k_rel = _each(lambda x, b, be: x * jnp.exp(be - b), key, bc, b_end)
    for y in range(c // SUB_CHUNK - 1):
        def scaled(x, b, bs):
            end_y = jnp.sum(jnp.where(rcol == SUB_CHUNK * y + SUB_CHUNK - 1, b, 0.0), axis=0, keepdims=True)
            return x * jnp.exp(jnp.where(rcol >= SUB_CHUNK * (y + 1), bs - end_y, -jnp.inf))
        dq = _each(scaled, q_rel, bc, b_start)
        in_y = (ci >= SUB_CHUNK * y) & (ci < SUB_CHUNK * (y + 1))
        a_y = _each(lambda x, z: jnp.where(in_y, mm(x, z, NT, prec), 0.0), dq, k_rel)
        o = _each(lambda acc, a, val: acc + mm(a, val, NN, prec), o, a_y, v)
        yield
    k_state = _each(lambda x, bl, b: x * jnp.exp(bl - b), key, b_last, bc)
    st_new = _each(lambda z, bl, val, x: z * jnp.exp(bl) + mm(val, x, TN, prec), st, b_last, v, k_state)
    if o_known is not None:
        o = _each(_known_value, o, o_known)
    return (_each(lambda x, z: _head_norm_gate(x, norm_w, z), o, gb), st_new), o


def _drain(gen):
    try:
        while True:
            next(gen)
    except StopIteration as done:
        return done.value


def _alternate(gen_a, gen_b):
    out, live = [None, None], [gen_a, gen_b]
    while any(g is not None for g in live):
        for i, g in enumerate(live):
            if g is None:
                continue
            try:
                next(g)
            except StopIteration as done:
                out[i], live[i] = done.value, None
    return out


def gdn_chunks(*args, **kwargs):
    return _drain(gdn_stages(*args, **kwargs))


def hgrn_chunks(*args, **kwargs):
    return _drain(hgrn_stages(*args, **kwargs))


def hgrn_chunk(qb, fb, ib, gb, l0, l1, norm_w, st, prec=HGRN_PREC, reuse_output=False):
    args = ([qb], [fb], [ib], [gb], [l0], [l1], norm_w, [st], prec)
    if reuse_output:
        known = lax.stop_gradient(hgrn_chunks(*args)[1])
        (y, st_new), _ = hgrn_chunks(*args, o_known=known)
    else:
        (y, st_new), _ = hgrn_chunks(*args)
    return y[0], st_new[0]


HEAD_VEC = (N_HEADS, 1, LANES)


class _ChunkSpecs:
    def __init__(self, nc, rev):
        self.nc, self.rev = nc, rev

    def _c(self, c):
        return self.nc - 1 - c if self.rev else c

    def row(self, width, block=0):
        return pl.BlockSpec((CHUNK, width), lambda c: (self._c(c), block))

    def per_head(self, rows):
        return pl.BlockSpec((None, N_HEADS, rows, rows), lambda c: (self._c(c), 0, 0, 0))

    @staticmethod
    def whole(shape):
        return pl.BlockSpec(shape, lambda c: (0,) * len(shape))


def _lanes(j):
    return slice(j * LANES, (j + 1) * LANES)


def mixer_fwd(qkv_c, proj, a_log_l, dt_l, gdn_norm_w, l0, l1, hgrn_norm_w, name):
    t = qkv_c.shape[0]
    hb = N_HEADS
    sp = _ChunkSpecs(t // CHUNK, rev=False)
    hs = list(range(hb))

    def body(q_ref, k_ref, v_ref, z_ref, ab_ref, al_ref, dt_ref, gnw_ref, qb_ref, fb_ref, ib_ref, gb_ref, l0_ref, l1_ref,
             hnw_ref, y_ref, hist_a_ref, inv_ref, hist_b_ref, o_ref, sa_ref, sb_ref, shift_ref):
        @pl.when(pl.program_id(0) == 0)
        def _():
            sa_ref[...] = jnp.zeros_like(sa_ref)
            sb_ref[...] = jnp.zeros_like(sb_ref)
            shift_ref[...] = jnp.zeros_like(shift_ref)

        heads = lambda ref: [ref[:, _lanes(j)] for j in hs]
        s_a, s_b = [sa_ref[h] for h in hs], [sb_ref[h] for h in hs]
        for h in hs:
            hist_a_ref[h] = s_a[h]
            hist_b_ref[h] = s_b[h]
        diags = [diag_part(SlotRows(shift_ref.at[h]), differentiable=False) for h in hs]
        ((y_a, s_a_new), inv), ((y_b, s_b_new), o_pre) = _alternate(
            gdn_stages(hs, heads(q_ref), heads(k_ref), heads(v_ref), heads(z_ref), ab_ref[...],
                       [al_ref[h] for h in hs], [dt_ref[h] for h in hs], gnw_ref[...], s_a),
            hgrn_stages(heads(qb_ref), heads(fb_ref), heads(ib_ref), heads(gb_ref),
                        [l0_ref[h] for h in hs], [l1_ref[h] for h in hs], hnw_ref[...], s_b, diags=diags))
        for h in hs:
            y_ref[:, _lanes(h)] = y_a[h].astype(BF16)
            y_ref[:, _lanes(hb + h)] = y_b[h].astype(BF16)
            o_ref[:, _lanes(h)] = o_pre[h]
            sa_ref[h] = s_a_new[h]
            sb_ref[h] = s_b_new[h]
            inv_ref[h] = inv[h]

    vec, gain, slab = sp.whole(HEAD_VEC), sp.whole((1, LANES)), functools.partial(sp.row, GDN_WIDTH)
    states = jax.ShapeDtypeStruct((sp.nc, N_HEADS, HEAD_DIM, HEAD_DIM), F32)
    return pl.pallas_call(
        body, name=name, grid=(sp.nc,),
        in_specs=[slab(0), slab(1), slab(2), slab(3), sp.row(LANES, AB_BLOCK), vec, vec, gain,
                  slab(4), slab(5), slab(6), slab(7), vec, vec, gain],
        out_specs=[sp.row(2 * GDN_WIDTH), sp.per_head(HEAD_DIM), sp.per_head(CHUNK), sp.per_head(HEAD_DIM), slab(0)],
        out_shape=[jax.ShapeDtypeStruct((t, 2 * GDN_WIDTH), BF16), states,
                   jax.ShapeDtypeStruct((sp.nc, N_HEADS, CHUNK, CHUNK), F32), states,
                   jax.ShapeDtypeStruct((t, GDN_WIDTH), F32)],
        scratch_shapes=[pltpu.VMEM((N_HEADS, HEAD_DIM, HEAD_DIM), F32), pltpu.VMEM((N_HEADS, HEAD_DIM, HEAD_DIM), F32),
                        pltpu.VMEM((hb, 3, SHIFT_WAYS, SHIFT_ROWS, LANES), F32)],
        compiler_params=_params(("arbitrary",)),
    )(qkv_c, qkv_c, qkv_c, proj, proj, a_log_l, dt_l, gdn_norm_w, proj, proj, proj, proj, l0, l1, hgrn_norm_w)


def mixer_bwd(qkv_c, proj, a_log_l, dt_l, gdn_norm_w, l0, l1, hgrn_norm_w, hist_a, inv_hist, hist_b, o_pre, dy, name):
    t = qkv_c.shape[0]
    hb = N_HEADS
    sp = _ChunkSpecs(t // CHUNK, rev=True)
    hs = list(range(hb))

    def body(q_ref, k_ref, v_ref, z_ref, ab_ref, al_ref, dt_ref, gnw_ref, qb_ref, fb_ref, ib_ref, gb_ref, l0_ref, l1_ref,
             hnw_ref, hist_a_ref, inv_ref, hist_b_ref, o_ref, dy_ref,
             dqkv_ref, dproj_ref, dal_ref, ddt_ref, dgnw_ref, dl0_ref, dl1_ref, dhnw_ref, dsa_ref, dsb_ref, shift_ref):
        @pl.when(pl.program_id(0) == 0)
        def _():
            for ref in (dal_ref, ddt_ref, dgnw_ref, dl0_ref, dl1_ref, dhnw_ref, dsa_ref, dsb_ref, shift_ref):
                ref[...] = jnp.zeros_like(ref)

        heads = lambda ref, first=0: [ref[:, _lanes(first + j)] for j in hs]
        diags = [diag_part(SlotRows(shift_ref.at[h])) for h in hs]
        inv_known, o_known = [inv_ref[h] for h in hs], heads(o_ref)

        def both(ga, gb):
            (ra, inv), (rb, o_pre) = _alternate(gdn_stages(hs, *ga, inv_known=inv_known),
                                                hgrn_stages(*gb, diags=diags, o_known=o_known))
            return (ra, rb), (inv, o_pre)

        ga = (heads(q_ref), heads(k_ref), heads(v_ref), heads(z_ref), ab_ref[...], [al_ref[h] for h in hs],
              [dt_ref[h] for h in hs], gnw_ref[...], [hist_a_ref[h] for h in hs])
        gb = (heads(qb_ref), heads(fb_ref), heads(ib_ref), heads(gb_ref), [l0_ref[h] for h in hs],
              [l1_ref[h] for h in hs], hnw_ref[...], [hist_b_ref[h] for h in hs])
        _, vjp, _ = jax.vjp(both, ga, gb, has_aux=True)
        dy_a = [x.astype(F32) for x in heads(dy_ref)]
        dy_b = [x.astype(F32) for x in heads(dy_ref, hb)]
        (dq, dk, dv, dz, dab, dal, ddt, dgnw, ds_a), (dqb, dfb, dib, dgb, dl0, dl1, dhnw, ds_b) = vjp(
            ((dy_a, [dsa_ref[h] for h in hs]), (dy_b, [dsb_ref[h] for h in hs])))
        for h in hs:
            dqkv_ref[:, _lanes(h)] = dq[h]
            dqkv_ref[:, _lanes(hb + h)] = dk[h]
            dqkv_ref[:, _lanes(2 * hb + h)] = dv[h]
            for slab, val in enumerate((dz, dqb, dfb, dib, dgb)):
                dproj_ref[:, _lanes((3 + slab) * hb + h)] = val[h].astype(BF16)
            dal_ref[h] += dal[h]
            ddt_ref[h] += ddt[h]
            dl0_ref[h] += dl0[h]
            dl1_ref[h] += dl1[h]
            dsa_ref[h] = ds_a[h]
            dsb_ref[h] = ds_b[h]
        dproj_ref[:, MAIN_WIDTH:] = dab.astype(BF16)
        dgnw_ref[...] += dgnw
        dhnw_ref[...] += dhnw

    vec, gain, slab = sp.whole(HEAD_VEC), sp.whole((1, LANES)), functools.partial(sp.row, GDN_WIDTH)
    vec_shape, gain_shape = jax.ShapeDtypeStruct(HEAD_VEC, F32), jax.ShapeDtypeStruct((1, LANES), F32)
    return pl.pallas_call(
        body, name=name, grid=(sp.nc,),
        in_specs=[slab(0), slab(1), slab(2), slab(3), sp.row(LANES, AB_BLOCK), vec, vec, gain,
                  slab(4), slab(5), slab(6), slab(7), vec, vec, gain,
                  sp.per_head(HEAD_DIM), sp.per_head(CHUNK), sp.per_head(HEAD_DIM), slab(0), sp.row(2 * GDN_WIDTH)],
        out_specs=[sp.row(QKV_WIDTH), sp.row(CAT_WIDTH), vec, vec, gain, vec, vec, gain],
        out_shape=[jax.ShapeDtypeStruct((t, QKV_WIDTH), F32), jax.ShapeDtypeStruct((t, CAT_WIDTH), BF16),
                   vec_shape, vec_shape, gain_shape, vec_shape, vec_shape, gain_shape],
        scratch_shapes=[pltpu.VMEM((N_HEADS, HEAD_DIM, HEAD_DIM), F32), pltpu.VMEM((N_HEADS, HEAD_DIM, HEAD_DIM), F32),
                        pltpu.VMEM((hb, 3, SHIFT_WAYS, SHIFT_ROWS, LANES), F32)],
        compiler_params=_params(("arbitrary",)),
    )(qkv_c, qkv_c, qkv_c, proj, proj, a_log_l, dt_l, gdn_norm_w, proj, proj, proj, proj, l0, l1, hgrn_norm_w,
      hist_a, inv_hist, hist_b, o_pre, dy)


def _adamw(w, g, m, v):
    m = ADAM_B1 * m + (1.0 - ADAM_B1) * g
    v = ADAM_B2 * v + (1.0 - ADAM_B2) * jnp.square(g)
    m_hat = m / (1.0 - ADAM_B1 ** ADAM_STEP)
    v_hat = v / (1.0 - ADAM_B2 ** ADAM_STEP)
    delta = -ADAM_LR * (m_hat / (jnp.sqrt(v_hat) + ADAM_EPS) + ADAM_WD * w)
    return delta, m, v


def adamw_reduce(parts, mine, slot, w, m, v, name, rb=128):
    r, c = w.shape
    rb = min(rb, r)
    n_parts = parts.shape[0]

    def body(slot_ref, p_ref, own_ref, w_ref, m_ref, v_ref, g_ref, d_ref, mo_ref, vo_ref):
        part = lambda d: jnp.where(slot_ref[0] == d, own_ref[...], p_ref[d]).astype(F32)
        g = part(0)
        for d in range(1, n_parts):
            g = g + part(d)
        delta, mn, vn = _adamw(w_ref[...], g, m_ref[...], v_ref[...])
        g_ref[...] = g
        d_ref[...] = delta
        mo_ref[...] = mn
        vo_ref[...] = vn

    blk = pl.BlockSpec((rb, c), lambda i, s: (i, 0))
    return pl.pallas_call(
        body, name=name,
        grid_spec=pltpu.PrefetchScalarGridSpec(
            num_scalar_prefetch=1, grid=(r // rb,),
            in_specs=[pl.BlockSpec((n_parts, rb, c), lambda i, s: (0, i, 0)),
                      pl.BlockSpec((None, rb, c), lambda i, s: (s[0], i, 0)), blk, blk, blk],
            out_specs=[blk] * 4),
        out_shape=[jax.ShapeDtypeStruct((r, c), F32)] * 4,
        compiler_params=_params(("parallel",)))(slot.astype(jnp.int32).reshape(1), parts, mine, w, m, v)


def adamw_small(w, g, m, v, name):
    def body(w_ref, g_ref, m_ref, v_ref, d_ref, mo_ref, vo_ref):
        delta, mn, vn = _adamw(w_ref[...], g_ref[...], m_ref[...], v_ref[...])
        d_ref[...] = delta
        mo_ref[...] = mn
        vo_ref[...] = vn

    vmem = pl.BlockSpec(memory_space=pltpu.VMEM)
    return pl.pallas_call(body, name=name, in_specs=[vmem] * 4, out_specs=[vmem] * 3,
                          out_shape=[jax.ShapeDtypeStruct(w.shape, F32)] * 3)(w, g, m, v)


def _pack(arrays):
    flat = jnp.concatenate([a.reshape(-1).astype(F32) for a in arrays])
    rows = -(-flat.shape[0] // (8 * LANES)) * 8
    return jnp.pad(flat, (0, rows * LANES - flat.shape[0])).reshape(rows, LANES)


def _unpack(packed, shapes):
    flat, out, off = packed.reshape(-1), [], 0
    for s in shapes:
        n = 1
        for d in s:
            n *= d
        out.append(flat[off:off + n].reshape(s))
        off += n
    return out


def _relu2_epilogue(acc, _):
    r = jnp.maximum(acc, 0.0)
    return acc, r * r


def _relu2_bwd_epilogue(acc, a1):
    return (acc * (2.0 * jnp.maximum(a1, 0.0)),)


def kernel(x, w_in, conv_w, gdn_a_log, gdn_dt_bias, gdn_norm_w, hgrn_lb_logits, hgrn_norm_w, w_out, norm_mix_w, norm_ffn_w, w_ff1, w_ff2, norm_final_w, loss_target, m_w_in, m_conv_w, m_gdn_a_log, m_gdn_dt_bias, m_gdn_norm_w, m_hgrn_lb_logits, m_hgrn_norm_w, m_w_out, m_norm_mix_w, m_norm_ffn_w, m_w_ff1, m_w_ff2, m_norm_final_w, v_w_in, v_conv_w, v_gdn_a_log, v_gdn_dt_bias, v_gdn_norm_w, v_hgrn_lb_logits, v_hgrn_norm_w, v_w_out, v_norm_mix_w, v_norm_ffn_w, v_w_ff1, v_w_ff2, v_norm_final_w):
    me = _my_flat()
    xs = x[0]
    target = loss_target[0]
    shard_in = w_in.shape[2]
    shard_conv = conv_w.shape[2]

    tok = lambda t: t[0:1, 0:1]

    half = D_MODEL // 2
    w_in_b = w_in[0].astype(BF16)
    h_ga, t_ga = exchange_start([w_in_b[:half], conv_w[0]], True, "gather_w_in_high_start", peers=CHIP_PEERS)
    h_g0, t_g0 = exchange_start([w_in_b[half:]], True, "gather_w_in_low_start", after=[t_ga], peers=CHIP_PEERS)
    h_g1, t_g1 = exchange_start([w_out[0].astype(BF16), w_ff1[0].astype(BF16)], True, "gather_mid_start", after=[t_g0],
                                peers=CHIP_PEERS)
    h_g2, t_g2 = exchange_start([w_ff2[0].astype(BF16)], True, "gather_ff2_start", after=[t_g1], peers=CHIP_PEERS)

    lane_b = lambda p: jnp.broadcast_to(p.reshape(N_HEADS, 1, 1), HEAD_VEC)
    a_log_l, dt_l = lane_b(gdn_a_log[0]), lane_b(gdn_dt_bias[0])
    l0 = hgrn_lb_logits[0].reshape(HEAD_VEC)
    l1 = hgrn_lb_logits[1].reshape(HEAD_VEC)

    n1, r1 = rms_fwd(xs, norm_mix_w + tok(t_g1) + tok(t_g2), "rms_mix")
    (s_high, s_conv), (l_high, l_conv) = exchange_wait(h_ga, "gather_w_in_high_wait", after=[n1], copies=len(CHIP_PEERS))
    h_fa, _ = forward_start([l_high, l_conv], "gather_w_in_high_forward")
    _, (l_high, l_conv) = exchange_wait(h_fa, "forward_w_in_high_wait", copies=len(OTHER_CHIPS))
    w_cat = weights_to_cat(_own_slot(l_high, s_high), "weights_to_cat", D_MODEL)
    conv_full = jnp.transpose(_own_slot(l_conv, s_conv), (1, 0, 2)).reshape(4, QKV_WIDTH)
    proj = matmul(n1, w_cat, "nn", "in_proj_high", (BF16,), tn=CAT_WIDTH // 5, tk=half, k_blocks=(0, 1))
    (s_low,), (l_low,) = exchange_wait(h_g0, "gather_w_in_low_wait", after=[proj], copies=len(CHIP_PEERS))
    h_f0, _ = forward_start([l_low], "gather_w_in_low_forward")
    _, (l_low,) = exchange_wait(_one(h_f0, 0), "forward_w_in_low_wait", copies=len(OTHER_CHIPS))
    w_cat = weights_to_cat(_own_slot(l_low, s_low), "weights_to_cat_low", D_MODEL, row0=half, into=w_cat)
    proj = matmul(n1, w_cat, "nn", "in_proj_low", tn=CAT_WIDTH // 5, tk=half, k_blocks=(1, 1), extra=proj,
                  epilogue=lambda acc, high: (acc + high,))
    qkv_c = conv_fwd(proj, conv_full, "conv_fwd")
    y, hist_a, inv_a, hist_b, o_b = mixer_fwd(qkv_c, proj, a_log_l, dt_l, gdn_norm_w, l0, l1, hgrn_norm_w, "mixer_fwd")
    (s_out, s_ff1), (l_out, l_ff1) = exchange_wait(h_g1, "gather_mid_wait", after=[y], copies=len(CHIP_PEERS))
    (s_ff2,), (l_ff2,) = exchange_wait(h_g2, "gather_ff2_wait", after=[y], copies=len(CHIP_PEERS))
    h_fw, _ = forward_start([l_out, l_ff1, l_ff2], "gather_forward_start")
    _, (l_out,) = exchange_wait(_one(h_fw, 0), "forward_out_wait", copies=len(OTHER_CHIPS))
    w_out_full = _own_slot(l_out, s_out).reshape(D_MODEL, D_MODEL)
    h1, n2, r2 = out_proj_rms(y, w_out_full, xs, norm_ffn_w, "out_proj_rms")
    _, (l_ff1,) = exchange_wait(_one(h_fw, 1), "forward_ff1_wait", after=[n2], copies=len(OTHER_CHIPS))
    w_ff1_sh = _own_slot(l_ff1, s_ff1)
    a1, act = matmul(n2, w_ff1_sh, "nn", "ff1", out_dtypes=(F32, BF16), epilogue=_relu2_epilogue, b_shards=True)
    _, (l_ff2,) = exchange_wait(_one(h_fw, 2), "forward_ff2_wait", after=[act], copies=len(OTHER_CHIPS))
    w_ff2_full = _own_slot(l_ff2, s_ff2).reshape(D_FF, D_MODEL)
    loss_sum, dh2_b, d_final = ff2_loss(act, w_ff2_full, h1, norm_final_w.reshape(1, D_MODEL), target, "ff2_loss")

    da1 = matmul(dh2_b, w_ff2_full, "nt", "d_act", out_dtypes=(BF16,), epilogue=_relu2_bwd_epilogue, extra=a1)
    t_all = xs.shape[0]
    dw_ff2 = matmul(act, dh2_b, "tn", "dw_ff2", out_dtypes=(BF16,), tk=t_all)
    p_ff2 = dw_ff2.reshape(N_DEV, D_FF // N_DEV, D_MODEL)
    h_s1, t_s1 = exchange_start([p_ff2], False, "scatter_ff2_start")
    dn2 = matmul(da1, w_ff1_sh, "nt", "d_n2", out_dtypes=(BF16,), after=[t_s1], b_shards=True, k_group=4)
    p_ff1 = matmul(n2, da1, "tn", "dw_ff1", out_dtypes=(BF16,), tn=D_FF // N_DEV, tk=t_all, after=[t_s1], out_shards=True)
    h_s2, t_s2 = exchange_start([p_ff1], False, "scatter_ff1_start")
    dh1_b, d_ffn = rms_bwd(h1, r2, norm_ffn_w + tok(t_s2), dn2, dh2_b, BF16, "rms_ffn_bwd")
    dmix = matmul(dh1_b, w_out_full, "nt", "d_mix", out_dtypes=(BF16,))
    dw_out = matmul(y, dh1_b, "tn", "dw_out", out_dtypes=(BF16,), tk=t_all)
    p_out = dw_out.reshape(N_DEV, D_MODEL // N_DEV, D_MODEL)
    h_s3, t_s3 = exchange_start([p_out], False, "scatter_out_start")
    d_qkv_c, dproj, d_alog_l, d_dt_l, d_gnw, dl0, dl1, d_hnw = mixer_bwd(
        qkv_c, proj, a_log_l, dt_l, gdn_norm_w + tok(t_s3), l0, l1, hgrn_norm_w, hist_a, inv_a, hist_b, o_b, dmix,
        "mixer_bwd")
    dproj, d_conv_full = conv_bwd(proj, d_qkv_c, conv_full, dproj, "conv_bwd")
    dw_cat = matmul(n1, dproj, "tn", "dw_in", out_dtypes=(BF16,), tm=512, tn=CAT_WIDTH // 5, tk=t_all)
    p_in = cat_to_shards(dw_cat, shard_in)
    h_pair, t_s4 = routed_start(p_in, _to_sibling_routes, "scatter_in_pair_start")

    (s_ff2g,), (r_ff2,) = exchange_wait(h_s1, "scatter_ff2_wait", after=[t_s4])
    (s_ff1g,), (r_ff1,) = exchange_wait(h_s2, "scatter_ff1_wait", after=[t_s4])
    (s_outg,), (r_out,) = exchange_wait(h_s3, "scatter_out_wait", after=[t_s4])
    g_w_ff2, d_w_ff2, nm_w_ff2, nv_w_ff2 = adamw_reduce(
        r_ff2, s_ff2g, me, w_ff2[0], m_w_ff2[0], v_w_ff2[0], "adamw_w_ff2")
    g_w_ff1, d_w_ff1, nm_w_ff1, nv_w_ff1 = adamw_reduce(
        r_ff1, s_ff1g, me, w_ff1[0], m_w_ff1[0], v_w_ff1[0], "adamw_w_ff1")
    g_w_out, d_w_out, nm_w_out, nv_w_out = adamw_reduce(
        r_out, s_outg, me, w_out[0], m_w_out[0], v_w_out[0], "adamw_w_out")
    (p_in,), (from_sibling,) = exchange_wait(h_pair, "scatter_in_pair_wait", after=[d_w_ff2, d_w_ff1, d_w_out],
                                             copies=N_CHIPS)
    chip_sums = pair_sum(p_in, from_sibling, "scatter_in_pair_sum")
    h_chips, t_s5 = routed_start(chip_sums, _to_chips_routes, "scatter_in_chips_start")
    dn1 = matmul(dproj, w_cat, "nt", "d_n1", out_dtypes=(BF16,), tm=512, tn=512, tk=CAT_WIDTH, after=[t_s5])
    dx, d_mix = rms_bwd(xs, r1, norm_mix_w, dn1, dh1_b, F32, "rms_mix_bwd")
    (chip_sums,), (r_in,) = exchange_wait(h_chips, "scatter_in_chips_wait", after=[dx], copies=len(OTHER_CHIPS))
    g_w_in, d_w_in, nm_w_in, nv_w_in = adamw_reduce(
        r_in, chip_sums, me // 2, w_in[0], m_w_in[0], v_w_in[0], "adamw_w_in")

    d_lb = jnp.stack([dl0.reshape(GDN_WIDTH), dl1.reshape(GDN_WIDTH)])
    small_shapes = [(1, N_HEADS), (1, N_HEADS), (1, HEAD_DIM), (2, GDN_WIDTH), (1, HEAD_DIM), (1, D_MODEL),
                    (1, D_MODEL), (D_MODEL,), (4, QKV_WIDTH), ()]
    small = _pack([d_alog_l[:, 0, 0], d_dt_l[:, 0, 0], d_gnw, d_lb, d_hnw, d_mix, d_ffn, d_final, d_conv_full,
                   loss_sum[0, 0]])
    red = allreduce_small(small, "allreduce_small")
    g_alog, g_dt, g_gnw, g_lb, g_hnw, g_mix, g_ffn, g_final, g_conv_full, loss = _unpack(red, small_shapes)
    g_conv = lax.dynamic_slice(g_conv_full, (0, me * shard_conv), (4, shard_conv)).reshape(1, 4, shard_conv)
    small_g = [g_alog, g_dt, g_gnw, g_lb, g_hnw, g_mix, g_ffn, g_final, g_conv]
    small_w = [gdn_a_log, gdn_dt_bias, gdn_norm_w, hgrn_lb_logits, hgrn_norm_w, norm_mix_w, norm_ffn_w, norm_final_w, conv_w]
    small_m = [m_gdn_a_log, m_gdn_dt_bias, m_gdn_norm_w, m_hgrn_lb_logits, m_hgrn_norm_w, m_norm_mix_w, m_norm_ffn_w,
               m_norm_final_w, m_conv_w]
    small_v = [v_gdn_a_log, v_gdn_dt_bias, v_gdn_norm_w, v_hgrn_lb_logits, v_hgrn_norm_w, v_norm_mix_w, v_norm_ffn_w,
               v_norm_final_w, v_conv_w]
    shapes = [a.shape for a in small_w]
    d_s, m_s, v_s = adamw_small(_pack(small_w), _pack(small_g), _pack(small_m), _pack(small_v), "adamw_small")
    d_alog, d_dt, d_gn, d_lbl, d_hn, d_nm, d_nf, d_nfin, d_cw = _unpack(d_s, shapes)
    m_alog, m_dt, m_gn, m_lbl, m_hn, m_nm, m_nf, m_nfin, m_cw = _unpack(m_s, shapes)
    v_alog, v_dt, v_gn, v_lbl, v_hn, v_nm, v_nf, v_nfin, v_cw = _unpack(v_s, shapes)

    lead = lambda a: a[None]
    grads = [lead(g_w_in), g_conv, g_alog, g_dt, g_gnw, g_lb, g_hnw, lead(g_w_out), g_mix, g_ffn,
             lead(g_w_ff1), lead(g_w_ff2), g_final]
    deltas = [lead(d_w_in), d_cw, d_alog, d_dt, d_gn, d_lbl, d_hn, lead(d_w_out), d_nm, d_nf,
              lead(d_w_ff1), lead(d_w_ff2), d_nfin]
    new_m = [lead(nm_w_in), m_cw, m_alog, m_dt, m_gn, m_lbl, m_hn, lead(nm_w_out), m_nm, m_nf,
             lead(nm_w_ff1), lead(nm_w_ff2), m_nfin]
    new_v = [lead(nv_w_in), v_cw, v_alog, v_dt, v_gn, v_lbl, v_hn, lead(nv_w_out), v_nm, v_nf,
             lead(nv_w_ff1), lead(nv_w_ff2), v_nfin]
    return (loss, dx[None], *grads, *deltas, *new_m, *new_v)
```

```python
import functools

import jax
import jax.numpy as jnp
from jax import lax
from jax.experimental import pallas as pl
from jax.experimental.pallas import tpu as pltpu

F32 = jnp.float32
BF16 = jnp.bfloat16
HI = lax.Precision.HIGHEST

N_DEV = 8
D_MODEL = 2048
CHUNK = 64
SUB_CHUNK = 16
HEAD_DIM = 128
N_HEADS = 8
GDN_WIDTH = N_HEADS * HEAD_DIM
D_FF = 4 * D_MODEL
QKV_WIDTH = 3 * GDN_WIDTH
MAIN_WIDTH = 8 * GDN_WIDTH
CAT_WIDTH = MAIN_WIDTH + 128
AB_BLOCK = MAIN_WIDTH // 128
NORM_EPS = 1e-6
L2_EPS = 1e-6
LANES = 128
VMEM_LIMIT = 56 * 1024 * 1024

ADAM_LR = 0.001
ADAM_B1 = 0.9
ADAM_B2 = 0.999
ADAM_EPS = 1e-08
ADAM_WD = 0.01
ADAM_STEP = 10

MESH = pl.DeviceIdType.MESH


def _params(sem=None):
    return pltpu.CompilerParams(dimension_semantics=sem, vmem_limit_bytes=VMEM_LIMIT)


def _dot(a, b, dims, prec=None):
    return lax.dot_general(a, b, (dims, ((), ())), precision=prec, preferred_element_type=F32)


NN = ((1,), (0,))
NT = ((1,), (1,))
TN = ((0,), (0,))


def _split_bf16(x, pieces):
    out = []
    for _ in range(pieces - 1):
        p = x.astype(BF16)
        out.append(p)
        x = x - p.astype(F32)
    out.append(x.astype(BF16))
    return out


def _mm_raw(a, b, dims, prec):
    if prec == "hi":
        return _dot(a, b, dims, HI)
    if prec == "bf":
        return _dot(a.astype(BF16), b.astype(BF16), dims)
    a_hi, a_lo = _split_bf16(a, 2)
    b_hi, b_lo = _split_bf16(b, 2)
    return _dot(a_hi, b_hi, dims) + (_dot(a_hi, b_lo, dims) + _dot(a_lo, b_hi, dims))


@functools.partial(jax.custom_vjp, nondiff_argnums=(2, 3))
def mm(a, b, dims, prec):
    return _mm_raw(a, b, dims, prec)


def _mm_fwd(a, b, dims, prec):
    return _mm_raw(a, b, dims, prec), (a, b)


def _mm_bwd(dims, prec, res, ct):
    a, b = res
    if dims == NN:
        return _mm_raw(ct, b, NT, prec), _mm_raw(a, ct, TN, prec)
    if dims == NT:
        return _mm_raw(ct, b, NN, prec), _mm_raw(ct, a, TN, prec)
    return _mm_raw(b, ct, NT, prec), _mm_raw(a, ct, NN, prec)


mm.defvjp(_mm_fwd, _mm_bwd)


def _sel_raw(sel, x, dims):
    sel = sel.astype(BF16)
    p0, p1, p2 = _split_bf16(x, 3)
    return _dot(sel, p0, dims) + (_dot(sel, p1, dims) + _dot(sel, p2, dims))


def _sel_parts(sel, x):
    c = x.shape[0]
    full = _sel_raw(sel, x, NN)
    return tuple(full[i * c:(i + 1) * c] for i in range(sel.shape[0] // c))


@jax.custom_vjp
def sel_sums(sel, x):
    return _sel_parts(sel, x)


def _sel_fwd(sel, x):
    return _sel_parts(sel, x), sel


def _sel_bwd(sel, cts):
    return jnp.zeros_like(sel), _sel_raw(sel, jnp.concatenate(cts, axis=0), TN)


sel_sums.defvjp(_sel_fwd, _sel_bwd)


@jax.custom_vjp
def _known_value(computed, known):
    del computed
    return known


_known_value.defvjp(lambda computed, known: (known, None), lambda _, ct: (ct, jnp.zeros_like(ct)))


def _my_flat():
    return 4 * lax.axis_index("x") + 2 * lax.axis_index("y") + lax.axis_index("c")


def _peer(k):
    x, y, c = lax.axis_index("x"), lax.axis_index("y"), lax.axis_index("c")
    kx, ky, kc = (k >> 2) & 1, (k >> 1) & 1, k & 1
    px = (1 - x) if kx else x
    py = (1 - y) if ky else y
    pc = (1 - c) if kc else c
    return (px, py, pc), 4 * px + 2 * py + pc


HBM_SPEC = pl.BlockSpec(memory_space=pltpu.HBM)
SEM_SPEC = pl.BlockSpec(memory_space=pltpu.SEMAPHORE)
ANY_SPEC = pl.BlockSpec(memory_space=pl.ANY)
DATAFLOW = pltpu.SideEffectType.DATAFLOW_SIDE_EFFECTING


def _in_hbm(x):
    return pltpu.with_memory_space_constraint(x, pltpu.HBM)


ALL_PEERS = tuple(range(1, N_DEV))
CHIP_PEERS = (1, 2, 4, 6)
OTHER_CHIPS = (2, 4, 6)


def exchange_start(xs, gather, name, after=(), peers=ALL_PEERS):
    n, n_after = len(xs), len(after)

    def body(*refs):
        x_refs, land_refs = refs[:n], refs[n:2 * n]
        sems = refs[2 * n + n_after:2 * n + n_after + 2 * n]
        token = refs[-1]
        me = _my_flat()
        for k in peers:
            peer, peer_flat = _peer(k)
            for a in range(n):
                src = x_refs[a] if gather else x_refs[a].at[peer_flat]
                pltpu.make_async_remote_copy(src_ref=src, dst_ref=land_refs[a].at[me], send_sem=sems[a],
                                             recv_sem=sems[n + a], device_id=peer, device_id_type=MESH).start()
        token[...] = jnp.zeros_like(token)

    lands =[_in_hbm(lax.empty(((N_DEV,) + x.shape) if gather else x.shape, x.dtype)) for x in xs]
    hbm_out = [pltpu.HBM(x.shape, x.dtype) for x in xs] + [pltpu.HBM(l.shape, l.dtype) for l in lands]
    res = pl.pallas_call(
        body, name=name,
        out_shape=(*([pltpu.SemaphoreType.DMA(())] * (2 * n)), *hbm_out, jax.ShapeDtypeStruct((8, LANES), F32)),
        in_specs=[HBM_SPEC] * (2 * n) + [ANY_SPEC] * n_after,
        out_specs=(*([SEM_SPEC] * (2 * n)), *([HBM_SPEC] * (2 * n)), pl.BlockSpec(memory_space=pltpu.VMEM)),
        input_output_aliases={i: 2 * n + i for i in range(2 * n)},
        compiler_params=pltpu.CompilerParams(has_side_effects=DATAFLOW),
    )(*[_in_hbm(x) for x in xs], *lands, *after)
    return (list(res[:2 * n]), list(res[2 * n:3 * n]), list(res[3 * n:4 * n])), res[-1]


def forward_start(lands, name, after=()):
    n, n_after = len(lands), len(after)

    def body(*refs):
        land_refs = refs[:n]
        sems = refs[n + n_after:n + n_after + 2 * n]
        token = refs[-1]
        sibling, _ = _peer(1)
        for a in range(n):
            for k in OTHER_CHIPS:
                _, from_flat = _peer(k)
                slot = land_refs[a].at[from_flat]
                pltpu.make_async_remote_copy(src_ref=slot, dst_ref=slot, send_sem=sems[a], recv_sem=sems[n + a],
                                             device_id=sibling, device_id_type=MESH).start()
        token[...] = jnp.zeros_like(token)

    res = pl.pallas_call(
        body, name=name,
        out_shape=(*([pltpu.SemaphoreType.DMA(())] * (2 * n)), *[pltpu.HBM(l.shape, l.dtype) for l in lands],
                   jax.ShapeDtypeStruct((8, LANES), F32)),
        in_specs=[HBM_SPEC] * n + [ANY_SPEC] * n_after,
        out_specs=(*([SEM_SPEC] * (2 * n)), *([HBM_SPEC] * n), pl.BlockSpec(memory_space=pltpu.VMEM)),
        input_output_aliases={i: 2 * n + i for i in range(n)},
        compiler_params=pltpu.CompilerParams(has_side_effects=DATAFLOW),
    )(*lands, *after)
    return (list(res[:2 * n]), [], list(res[2 * n:3 * n])), res[-1]


def exchange_wait(handle, name, after=(), copies=N_DEV - 1):
    sems, xs, lands = handle
    n, n_x, n_after = len(lands), len(xs), len(after)

    def body(*refs):
        land_refs = refs[n_x:n_x + n]
        sem_refs = refs[n_x + n:n_x + 3 * n]
        for a in range(n):
            every = land_refs[a].at[pl.ds(0, copies)]
            cp = pltpu.make_async_remote_copy(src_ref=every, dst_ref=every, send_sem=sem_refs[a],
                                              recv_sem=sem_refs[n + a], device_id=_peer(1)[0], device_id_type=MESH)
            cp.wait_send()
            cp.wait_recv()

    res = pl.pallas_call(
        body, name=name,
        out_shape=[pltpu.HBM(x.shape, x.dtype) for x in xs] + [pltpu.HBM(l.shape, l.dtype) for l in lands],
        in_specs=[HBM_SPEC] * (n_x + n) + [SEM_SPEC] * (2 * n) + [ANY_SPEC] * n_after,
        out_specs=[HBM_SPEC] * (n_x + n),
        input_output_aliases={i: i for i in range(n_x + n)},
        compiler_params=pltpu.CompilerParams(has_side_effects=DATAFLOW),
    )(*xs, *lands, *sems, *after)
    return list(res[:n_x]), list(res[n_x:])


N_CHIPS = N_DEV // 2


def routed_start(x, routes, name, after=()):
    n_after = len(after)

    def body(*refs):
        x_ref, land_ref = refs[0], refs[1]
        send_sem, recv_sem = refs[2 + n_after], refs[3 + n_after]
        token = refs[-1]
        for src, dst, peer in routes():
            pltpu.make_async_remote_copy(src_ref=x_ref.at[src], dst_ref=land_ref.at[dst], send_sem=send_sem,
                                         recv_sem=recv_sem, device_id=peer, device_id_type=MESH).start()
        token[...] = jnp.zeros_like(token)

    land = _in_hbm(lax.empty((N_CHIPS,) + x.shape[1:], x.dtype))
    res = pl.pallas_call(
        body, name=name,
        out_shape=(pltpu.SemaphoreType.DMA(()), pltpu.SemaphoreType.DMA(()), pltpu.HBM(x.shape, x.dtype),
                   pltpu.HBM(land.shape, land.dtype), jax.ShapeDtypeStruct((8, LANES), F32)),
        in_specs=[HBM_SPEC, HBM_SPEC] + [ANY_SPEC] * n_after,
        out_specs=(SEM_SPEC, SEM_SPEC, HBM_SPEC, HBM_SPEC, pl.BlockSpec(memory_space=pltpu.VMEM)),
        input_output_aliases={0: 2, 1: 3},
        compiler_params=pltpu.CompilerParams(has_side_effects=DATAFLOW),
    )(_in_hbm(x), land, *after)
    return ([res[0], res[1]], [res[2]], [res[3]]), res[-1]


def _to_sibling_routes():
    c = lax.axis_index("c")
    sibling, _ = _peer(1)
    return [(2 * chip + 1 - c, chip, sibling) for chip in range(N_CHIPS)]


def _to_chips_routes():
    my_chip = _my_flat() // 2
    routes = []
    for k in OTHER_CHIPS:
        peer, peer_flat = _peer(k)
        routes.append((peer_flat // 2, my_chip, peer))
    return routes


def pair_sum(p, from_sibling, name, rb=1024):
    _, r, c = p.shape
    mine = lax.axis_index("c").astype(jnp.int32).reshape(1)

    def body(kind_ref, p_ref, s_ref, o_ref):
        del kind_ref
        o_ref[...] = (p_ref[...].astype(F32) + s_ref[...].astype(F32)).astype(BF16)

    return pl.pallas_call(
        body, name=name,
        grid_spec=pltpu.PrefetchScalarGridSpec(
            num_scalar_prefetch=1, grid=(N_CHIPS, r // rb),
            in_specs=[pl.BlockSpec((None, None, rb, c), lambda chip, i, kind: (chip, kind[0], i, 0)),
                      pl.BlockSpec((None, rb, c), lambda chip, i, kind: (chip, i, 0))],
            out_specs=pl.BlockSpec((None, rb, c), lambda chip, i, kind: (chip, i, 0))),
        out_shape=jax.ShapeDtypeStruct((N_CHIPS, r, c), BF16),
        compiler_params=_params(("parallel", "parallel")))(mine, p.reshape(N_CHIPS, 2, r, c), from_sibling)


def _one(handle, a):
    sems, xs, lands = handle
    n = len(lands)
    return [sems[a], sems[n + a]], xs[a:a + 1], [lands[a]]


def _own_slot(land, block):
    return lax.dynamic_update_slice(land, block[None], (_my_flat(),) + (0,) * block.ndim)


def allreduce_small(x, name):
    rows = x.shape[0]

    def body(x_ref, o_ref, buf, send_sems, recv_sems):
        me = _my_flat()
        buf[me] = x_ref[...]
        sends = []
        for k in range(1, N_DEV):
            peer, _ = _peer(k)
            cp = pltpu.make_async_remote_copy(
                src_ref=x_ref, dst_ref=buf.at[me], send_sem=send_sems.at[k], recv_sem=recv_sems.at[k],
                device_id=peer, device_id_type=MESH)
            cp.start()
            sends.append(cp)
        for k in range(1, N_DEV):
            peer, peer_flat = _peer(k)
            pltpu.make_async_remote_copy(
                src_ref=x_ref, dst_ref=buf.at[peer_flat], send_sem=send_sems.at[k], recv_sem=recv_sems.at[k],
                device_id=peer, device_id_type=MESH).wait_recv()
        for cp in sends:
            cp.wait_send()
        acc = buf[0]
        for d in range(1, N_DEV):
            acc = acc + buf[d]
        o_ref[...] = acc

    vmem = pl.BlockSpec(memory_space=pltpu.VMEM)
    return pl.pallas_call(
        body, name=name, out_shape=jax.ShapeDtypeStruct((rows, LANES), F32),
        in_specs=[vmem], out_specs=vmem,
        scratch_shapes=[pltpu.VMEM((N_DEV, rows, LANES), F32),
                        pltpu.SemaphoreType.DMA((N_DEV,)), pltpu.SemaphoreType.DMA((N_DEV,))],
    )(x)


def matmul(a, b, mode, name, out_dtypes=(F32,), epilogue=None, extra=None, tm=1024, tn=1024, tk=2048, after=(),
           b_shards=False, out_shards=False, k_group=1, k_blocks=None):
    if b_shards:
        n_sh, b_rows, b_cols = b.shape
    if mode == "nn":
        (m, kd), n = a.shape, (n_sh * b_cols if b_shards else b.shape[1])
        if b_shards:
            tn = b_cols
    elif mode == "nt":
        (m, kd), n = a.shape, (b_rows if b_shards else b.shape[0])
        if b_shards:
            tk = k_group * b_cols
    else:
        (kd, m), n = a.shape, b.shape[1]
    tm, tn, tk = min(tm, m), min(tn, n), min(tk, kd)
    assert m % tm == 0 and n % tn == 0 and kd % tk == 0, (name, m, n, kd, tm, tn, tk)
    k0, ksteps = (0, kd // tk) if k_blocks is None else k_blocks
    dims = {"nn": NN, "nt": NT, "tn": TN}[mode]
    n_out = len(out_dtypes)
    n_in = 2 + (extra is not None) + len(after)

    def finish(acc, e_ref, o_refs):
        outs = (acc,) if epilogue is None else epilogue(acc, e_ref[...] if e_ref is not None else None)
        for o_ref, o in zip(o_refs, outs):
            o_ref[...] = o.astype(o_ref.dtype)

    def product(a_ref, b_ref):
        if mode == "nt" and b_shards:
            w = b_cols
            parts = [_dot(a_ref[:, s * w:(s + 1) * w], b_ref[s], dims) for s in range(k_group)]
            return functools.reduce(lambda p, q: p + q, parts)
        return _dot(a_ref[...], b_ref[...], dims)

    def body(*refs):
        a_ref, b_ref = refs[0], refs[1]
        e_ref = refs[2] if extra is not None else None
        o_refs = refs[n_in:n_in + n_out]
        if ksteps == 1:
            finish(product(a_ref, b_ref), e_ref, o_refs)
            return
        acc_ref = refs[-1]
        kk = pl.program_id(2)

        @pl.when(kk == 0)
        def _():
            acc_ref[...] = jnp.zeros_like(acc_ref)

        acc_ref[...] += product(a_ref, b_ref)

        @pl.when(kk == ksteps - 1)
        def _():
            finish(acc_ref[...], e_ref, o_refs)

    if mode == "nn":
        a_spec = pl.BlockSpec((tm, tk), lambda i, j, k: (i, k0 + k))
        b_spec = (pl.BlockSpec((None, tk, tn), lambda i, j, k: (j, k, 0)) if b_shards
                  else pl.BlockSpec((tk, tn), lambda i, j, k: (k0 + k, j)))
    elif mode == "nt":
        a_spec = pl.BlockSpec((tm, tk), lambda i, j, k: (i, k))
        b_spec = (pl.BlockSpec((k_group, tn, b_cols), lambda i, j, k: (k, j, 0)) if b_shards
                  else pl.BlockSpec((tn, tk), lambda i, j, k: (j, k)))
    else:
        a_spec = pl.BlockSpec((tk, tm), lambda i, j, k: (k, i))
        b_spec = pl.BlockSpec((tk, tn), lambda i, j, k: (k, j))
    o_spec = pl.BlockSpec((tm, tn), lambda i, j, k: (i, j))
    res_spec = pl.BlockSpec((None, tm, tn), lambda i, j, k: (j, i, 0)) if out_shards else o_spec
    res_shape = (n // tn, m, tn) if out_shards else (m, n)
    in_specs = [a_spec, b_spec] + ([o_spec] if extra is not None else []) + [ANY_SPEC] * len(after)
    args = (a, b) + ((extra,) if extra is not None else ()) + tuple(after)
    res = pl.pallas_call(
        body, name=name, grid=(m // tm, n // tn, ksteps),
        in_specs=in_specs, out_specs=[res_spec] * n_out,
        out_shape=[jax.ShapeDtypeStruct(res_shape, dt) for dt in out_dtypes],
        scratch_shapes=[pltpu.VMEM((tm, tn), F32)] if ksteps > 1 else [],
        compiler_params=_params(("parallel", "parallel", "arbitrary")),
    )(*args)
    return res if n_out > 1 else res[0]


GATE_COL = 4 * GDN_WIDTH
RELAYOUT_ROWS = 256


def _cat_of_win(j):
    if j < GATE_COL:
        return j
    if j < GATE_COL + 2 * N_HEADS:
        return MAIN_WIDTH + (j - GATE_COL)
    return j - 2 * N_HEADS


def _win_of_cat(c):
    if c < GATE_COL:
        return c
    if c < MAIN_WIDTH:
        return c + 2 * N_HEADS
    if c < MAIN_WIDTH + 2 * N_HEADS:
        return GATE_COL + (c - MAIN_WIDTH)
    return None


def _runs(first, count, mapping):
    runs, i = [], 0
    while i < count:
        start, n = mapping(first + i), 1
        while i + n < count and mapping(first + i + n) == start + n:
            n += 1
        runs.append((start, n))
        i += n
    return runs


def weights_to_cat(g_in, name, total_rows, row0=0, into=None):
    n_dev, rows, shard = g_in.shape
    first = row0 // RELAYOUT_ROWS

    def body(x_ref, *rest):
        o_ref = rest[-1]
        for b in range(CAT_WIDTH // LANES):
            live = sum(_win_of_cat(LANES * b + i) is not None for i in range(LANES))
            parts = []
            for start, n in _runs(LANES * b, live, _win_of_cat):
                while n > 0:
                    d, o = divmod(start, shard)
                    take = min(n, shard - o)
                    parts.append(x_ref[d, :, o:o + take])
                    start, n = start + take, n - take
            if live < LANES:
                parts.append(jnp.zeros((RELAYOUT_ROWS, LANES - live), g_in.dtype))
            o_ref[:, LANES * b:LANES * (b + 1)] = parts[0] if len(parts) == 1 else jnp.concatenate(parts, axis=1)

    return pl.pallas_call(
        body, name=name, grid=(rows // RELAYOUT_ROWS,),
        in_specs=[pl.BlockSpec((n_dev, RELAYOUT_ROWS, shard), lambda i: (0, i, 0))] + ([ANY_SPEC] if into is not None else []),
        out_specs=pl.BlockSpec((RELAYOUT_ROWS, CAT_WIDTH), lambda i: (first + i, 0)),
        out_shape=jax.ShapeDtypeStruct((total_rows, CAT_WIDTH), g_in.dtype),
        input_output_aliases={1: 0} if into is not None else {},
        compiler_params=_params(("parallel",)))(*((g_in,) if into is None else (g_in, into)))


def cat_to_shards(dw_cat, shard):
    rows = dw_cat.shape[0]

    def body(x_ref, o_ref):
        for d in range(N_DEV):
            for t0 in range(0, shard, LANES):
                width = min(LANES, shard - t0)
                parts = [x_ref[:, c:c + n] for c, n in _runs(d * shard + t0, width, _cat_of_win)]
                o_ref[d, :, t0:t0 + width] = parts[0] if len(parts) == 1 else jnp.concatenate(parts, axis=1)

    return pl.pallas_call(
        body, name="cat_to_shards", grid=(rows // RELAYOUT_ROWS,),
        in_specs=[pl.BlockSpec((RELAYOUT_ROWS, CAT_WIDTH), lambda i: (i, 0))],
        out_specs=pl.BlockSpec((N_DEV, RELAYOUT_ROWS, shard), lambda i: (0, i, 0)),
        out_shape=jax.ShapeDtypeStruct((N_DEV, rows, shard), dw_cat.dtype),
        compiler_params=_params(("parallel",)))(dw_cat)


ROW_BLOCK = 512


def rms_fwd(x, w, name):
    t, d = x.shape

    def body(x_ref, w_ref, n_ref, r_ref):
        h = x_ref[...]
        r = lax.rsqrt(jnp.mean(h * h, axis=-1, keepdims=True) + NORM_EPS)
        n_ref[...] = (h * r * w_ref[...]).astype(BF16)
        r_ref[...] = r

    row = pl.BlockSpec((ROW_BLOCK, d), lambda i: (i, 0))
    return pl.pallas_call(
        body, name=name, grid=(t // ROW_BLOCK,),
        in_specs=[row, pl.BlockSpec((1, d), lambda i: (0, 0))],
        out_specs=[row, pl.BlockSpec((ROW_BLOCK, 1), lambda i: (i, 0))],
        out_shape=[jax.ShapeDtypeStruct((t, d), BF16), jax.ShapeDtypeStruct((t, 1), F32)],
        compiler_params=_params(("parallel",)))(x, w)


FUSED_ROWS = 512


def out_proj_rms(y, w_out, x, w_norm, name):
    t, d = x.shape

    def body(y_ref, w_ref, x_ref, g_ref, h_ref, n_ref, r_ref):
        h = x_ref[...] + _dot(y_ref[...], w_ref[...], NN)
        r = lax.rsqrt(jnp.mean(h * h, axis=-1, keepdims=True) + NORM_EPS)
        h_ref[...] = h
        n_ref[...] = (h * r * g_ref[...]).astype(BF16)
        r_ref[...] = r

    row = pl.BlockSpec((FUSED_ROWS, d), lambda i: (i, 0))
    return pl.pallas_call(
        body, name=name, grid=(t // FUSED_ROWS,),
        in_specs=[pl.BlockSpec((FUSED_ROWS, y.shape[1]), lambda i: (i, 0)), pl.BlockSpec(w_out.shape, lambda i: (0, 0)),
                  row, pl.BlockSpec((1, d), lambda i: (0, 0))],
        out_specs=[row, row, pl.BlockSpec((FUSED_ROWS, 1), lambda i: (i, 0))],
        out_shape=[jax.ShapeDtypeStruct((t, d), F32), jax.ShapeDtypeStruct((t, d), BF16),
                   jax.ShapeDtypeStruct((t, 1), F32)],
        compiler_params=_params(("parallel",)))(y, w_out, x, w_norm)


def ff2_loss(act, w_ff2, h1, w, target, name, tk=2048):
    t, d = h1.shape
    ksteps = act.shape[1] // tk

    def body(a_ref, b_ref, h_ref, w_ref, t_ref, loss_ref, dhb_ref, dw_ref, acc_ref):
        i, kk = pl.program_id(0), pl.program_id(1)

        @pl.when((i == 0) & (kk == 0))
        def _():
            loss_ref[...] = jnp.zeros_like(loss_ref)
            dw_ref[...] = jnp.zeros_like(dw_ref)

        @pl.when(kk == 0)
        def _():
            acc_ref[...] = h_ref[...]

        acc_ref[...] += _dot(a_ref[...], b_ref[...], NN)

        @pl.when(kk == ksteps - 1)
        def _():
            h = acc_ref[...]
            wv = w_ref[...]
            r = lax.rsqrt(jnp.mean(h * h, axis=-1, keepdims=True) + NORM_EPS)
            yn = h * r
            e = yn * wv - t_ref[...]
            loss_ref[...] += 0.5 * jnp.sum(jnp.sum(e * e, axis=-1, keepdims=True), axis=0, keepdims=True) / d
            dy = e / d
            dw_ref[...] += jnp.sum(dy * yn, axis=0, keepdims=True)
            dyn = dy * wv
            dhb_ref[...] = (r * (dyn - yn * jnp.mean(dyn * yn, axis=-1, keepdims=True))).astype(BF16)

    row = pl.BlockSpec((FUSED_ROWS, d), lambda i, k: (i, 0))
    wspec = pl.BlockSpec((1, d), lambda i, k: (0, 0))
    return pl.pallas_call(
        body, name=name, grid=(t // FUSED_ROWS, ksteps),
        in_specs=[pl.BlockSpec((FUSED_ROWS, tk), lambda i, k: (i, k)), pl.BlockSpec((tk, d), lambda i, k: (k, 0)),
                  row, wspec, row],
        out_specs=[pl.BlockSpec((1, 1), lambda i, k: (0, 0)), row, wspec],
        out_shape=[jax.ShapeDtypeStruct((1, 1), F32), jax.ShapeDtypeStruct((t, d), BF16),
                   jax.ShapeDtypeStruct((1, d), F32)],
        scratch_shapes=[pltpu.VMEM((FUSED_ROWS, d), F32)],
        compiler_params=_params(("arbitrary", "arbitrary")))(act, w_ff2, h1, w, target)


def rms_bwd(h, r, w, dn, dres, out_dtype, name):
    t, d = h.shape

    def body(h_ref, r_ref, w_ref, dn_ref, dres_ref, dh_ref, dw_ref):
        @pl.when(pl.program_id(0) == 0)
        def _():
            dw_ref[...] = jnp.zeros_like(dw_ref)

        rv = r_ref[...]
        yn = h_ref[...] * rv
        dnv = dn_ref[...].astype(F32)
        dw_ref[...] += jnp.sum(dnv * yn, axis=0, keepdims=True)
        dyn = dnv * w_ref[...]
        dh = dres_ref[...].astype(F32) + rv * (dyn - yn * jnp.mean(dyn * yn, axis=-1, keepdims=True))
        dh_ref[...] = dh.astype(out_dtype)

    row = pl.BlockSpec((ROW_BLOCK, d), lambda i: (i, 0))
    wspec = pl.BlockSpec((1, d), lambda i: (0, 0))
    rspec = pl.BlockSpec((ROW_BLOCK, 1), lambda i: (i, 0))
    return pl.pallas_call(
        body, name=name, grid=(t // ROW_BLOCK,),
        in_specs=[row, rspec, wspec, row, row], out_specs=[row, wspec],
        out_shape=[jax.ShapeDtypeStruct((t, d), out_dtype), jax.ShapeDtypeStruct((1, d), F32)],
        compiler_params=_params(("arbitrary",)))(h, r, w, dn, dres)


CONV_ROWS = 512
TILE_ROWS = 8


def _iota2(shape, axis):
    return lax.broadcasted_iota(jnp.int32, shape, axis)


def _silu(x):
    return x * jax.nn.sigmoid(x)


def _conv_rows(x_ref, w, first, rows):
    acc = None
    for j in range(4):
        term = x_ref[first - 3 + j:first - 3 + j + rows, :] * w[j:j + 1, :]
        acc = term if acc is None else acc + term
    return acc


def _head_shifts(head):
    rows = _iota2((TILE_ROWS, 1), 0)
    return [jnp.where(rows >= 3 - j, head if j == 3 else pltpu.roll(head, 3 - j, 0), 0.0) for j in range(4)]


def _conv_chunks(t):
    pieces = [(TILE_ROWS, min(CONV_ROWS, t) - TILE_ROWS)]
    pieces += [(r, CONV_ROWS) for r in range(CONV_ROWS, t, CONV_ROWS)]
    return pieces


def conv_fwd(proj, conv_w, name):
    t = proj.shape[0]

    def body(x_ref, w_ref, o_ref):
        w = w_ref[...]
        shifted = _head_shifts(x_ref[0:TILE_ROWS, :])
        o_ref[0:TILE_ROWS, :] = _silu(sum(shifted[j] * w[j:j + 1, :] for j in range(4)))
        for first, rows in _conv_chunks(t):
            o_ref[first:first + rows, :] = _silu(_conv_rows(x_ref, w, first, rows))

    col = pl.BlockSpec((t, LANES), lambda c: (0, c))
    return pl.pallas_call(
        body, name=name, grid=(QKV_WIDTH // LANES,),
        in_specs=[col, pl.BlockSpec((4, LANES), lambda c: (0, c))], out_specs=col,
        out_shape=jax.ShapeDtypeStruct((t, QKV_WIDTH), F32),
        compiler_params=_params(("parallel",)))(proj, conv_w)


def conv_bwd(proj, dout, conv_w, dproj, name):
    t = proj.shape[0]

    def dsilu(pre):
        sg = jax.nn.sigmoid(pre)
        return sg * (1.0 + pre * (1.0 - sg))

    def body(x_ref, d_ref, w_ref, dproj_in, dx_ref, dw_ref, stage):
        del dproj_in
        w = w_ref[...]
        shifted = _head_shifts(x_ref[0:TILE_ROWS, :])
        head_dpre = d_ref[0:TILE_ROWS, :] * dsilu(sum(shifted[j] * w[j:j + 1, :] for j in range(4)))
        stage[0:TILE_ROWS, :] = head_dpre
        for first, rows in _conv_chunks(t):
            stage[first:first + rows, :] = d_ref[first:first + rows, :] * dsilu(_conv_rows(x_ref, w, first, rows))
        stage[t:t + TILE_ROWS, :] = jnp.zeros((TILE_ROWS, LANES), F32)
        for first, rows in [(0, TILE_ROWS)] + _conv_chunks(t):
            dx = None
            for j in range(4):
                term = stage[first + 3 - j:first + 3 - j + rows, :] * w[j:j + 1, :]
                dx = term if dx is None else dx + term
            dx_ref[first:first + rows, :] = dx.astype(BF16)
        dw = [jnp.sum(head_dpre * shifted[j], axis=0, keepdims=True) for j in range(4)]
        for first, rows in _conv_chunks(t):
            dpre = stage[first:first + rows, :]
            for j in range(4):
                dw[j] = dw[j] + jnp.sum(dpre * x_ref[first - 3 + j:first - 3 + j + rows, :], axis=0, keepdims=True)
        dw_ref[...] = jnp.concatenate(dw, axis=0)

    col = pl.BlockSpec((t, LANES), lambda c: (0, c))
    taps = pl.BlockSpec((4, LANES), lambda c: (0, c))
    return pl.pallas_call(
        body, name=name, grid=(QKV_WIDTH // LANES,),
        in_specs=[col, col, taps, ANY_SPEC], out_specs=[col, taps],
        out_shape=[jax.ShapeDtypeStruct(dproj.shape, BF16), jax.ShapeDtypeStruct((4, QKV_WIDTH), F32)],
        scratch_shapes=[pltpu.VMEM((t + TILE_ROWS, LANES), F32)],
        input_output_aliases={3: 0},
        compiler_params=_params(("parallel",)))(proj, dout, conv_w, dproj)


def _softplus(x):
    return jnp.maximum(x, 0.0) + jnp.log(1.0 + jnp.exp(-jnp.abs(x)))


def _head_norm_gate(o, norm_w, gate):
    return o * lax.rsqrt(jnp.mean(o * o, axis=-1, keepdims=True) + NORM_EPS) * norm_w * _silu(gate)


GDN_PREC = ("bf", "bf")
HGRN_PREC = "bf"


def _each(fn, *cols):
    return [fn(*a) for a in zip(*cols)]


@functools.partial(jax.custom_vjp, nondiff_argnums=(2,))
def _known_inverse(low, inv, prec):
    del low, prec
    return inv


def _known_inverse_fwd(low, inv, prec):
    del low
    return inv, inv


def _known_inverse_bwd(prec, inv, ct):
    return -_mm_raw(_mm_raw(inv, ct, TN, prec), inv, NT, prec), jnp.zeros_like(inv)


_known_inverse.defvjp(_known_inverse_fwd, _known_inverse_bwd)


def gdn_stages(hs, qc, kc, vc, zc, ab, a_log_l, dt_l, norm_w, s, prec=GDN_PREC, inv_known=None):
    p_inv, p_mm = prec
    c = CHUNK
    ri, ci = _iota2((c, c), 0), _iota2((c, c), 1)
    incl, strict, eye = ri >= ci, ri > ci, ri == ci
    lane = _iota2((c, LANES), 1)
    last_row = _iota2((c, 1), 0) == c - 1
    rowsum = lambda x: jnp.sum(x, axis=1, keepdims=True)

    def row(col):
        return jnp.sum(jnp.where(eye, col, 0.0), axis=0, keepdims=True)

    q = _each(lambda x: x * lax.rsqrt(rowsum(x * x) + L2_EPS) * (HEAD_DIM ** -0.5), qc)
    k = _each(lambda x: x * lax.rsqrt(rowsum(x * x) + L2_EPS), kc)
    yield
    a_col = [rowsum(jnp.where(lane == h, ab, 0.0)) for h in hs]
    b_col = [rowsum(jnp.where(lane == h + N_HEADS, ab, 0.0)) for h in hs]
    beta = _each(jax.nn.sigmoid, b_col)
    g = _each(lambda a, al, dl: rowsum(jnp.where(lane == 0, -jnp.exp(al) * _softplus(a + dl), 0.0)), a_col, a_log_l, dt_l)
    gcum = _each(lambda x: rowsum(jnp.where(incl, row(x), 0.0)), g)
    g_last = _each(lambda x: jnp.sum(jnp.where(last_row, x, 0.0), axis=0, keepdims=True), gcum)
    decay = _each(lambda x: jnp.exp(jnp.where(incl, x - row(x), -jnp.inf)), gcum)
    yield
    kk = _each(lambda x: mm(x, x, NT, p_mm), k)
    low = _each(lambda b, x, d: jnp.where(strict, b * x * d, 0.0), beta, kk, decay)
    yield
    if inv_known is None:
        power = _each(lambda x: -x, low)
        inv = _each(lambda x: jnp.where(eye, 1.0, 0.0) + x, power)
        for _ in range(5):
            power = _each(lambda x: mm(x, x, NN, p_inv), power)
            yield
            inv = _each(lambda x, p: x + mm(x, p, NN, p_inv), inv, power)
            yield
    else:
        inv = _each(lambda x, known: _known_inverse(x, known, p_inv), low, inv_known)
    exp_g = _each(jnp.exp, gcum)
    yield
    u_v = _each(lambda i, b, x: mm(i, b * x, NN, p_mm), inv, beta, vc)
    w = _each(lambda i, b, e, x: mm(i, b * e * x, NN, p_mm), inv, beta, exp_g, k)
    yield
    attn = _each(lambda x, y, d: mm(x, y, NT, p_mm) * d, q, k, decay)
    yield
    u = _each(lambda x, y, z: x - mm(y, z, NN, p_mm), u_v, w, s)
    yield
    o = _each(lambda x, e, z: mm(x * e, z, NN, p_mm), q, exp_g, s)
    o = _each(lambda x, a, y: x + mm(a, y, NN, p_mm), o, attn, u)
    yield
    k_end = _each(lambda x, gl, gc: x * jnp.exp(gl - gc), k, g_last, gcum)
    s_new = _each(lambda z, gl, x, y: z * jnp.exp(gl) + mm(x, y, TN, p_mm), s, g_last, k_end, u)
    return (_each(lambda x, z: _head_norm_gate(x, norm_w, z), o, zc), s_new), inv


def gdn_chunk(h, qc, kc, vc, zc, ab, a_log_l, dt_l, norm_w, s, prec=GDN_PREC, reuse_inverse=False):
    args = ([h], [qc], [kc], [vc], [zc], ab, [a_log_l], [dt_l], norm_w, [s], prec)
    if reuse_inverse:
        inv = lax.stop_gradient(gdn_chunks(*args)[1])
        (y, s_new), _ = gdn_chunks(*args, inv_known=inv)
    else:
        (y, s_new), _ = gdn_chunks(*args)
    return y[0], s_new[0]


DIAG_ROWS = SUB_CHUNK // 2
SHIFT_PAD = 8
SHIFT_ROWS = SHIFT_PAD + CHUNK + SHIFT_PAD
SHIFT_WAYS = 4


class RolledRows:
    def down(self, x, which):
        del which
        return [x] + [pltpu.roll(x, off, 0) for off in range(1, DIAG_ROWS)]

    def up_sum(self, parts, which):
        del which
        acc = parts[0]
        for off in range(1, DIAG_ROWS):
            acc = acc + pltpu.roll(parts[off], CHUNK - off, 0)
        return acc


class SlotRows:
    def __init__(self, slots):
        self.slots = slots

    def down(self, x, which):
        self.slots[which, 0, SHIFT_PAD:SHIFT_PAD + CHUNK, :] = x
        return [x] + [self.slots[which, 0, SHIFT_PAD - off:SHIFT_PAD + CHUNK - off, :] for off in range(1, DIAG_ROWS)]

    def up_sum(self, parts, which):
        acc = parts[0]
        for off in range(1, DIAG_ROWS):
            way = 1 + off % (SHIFT_WAYS - 1)
            self.slots[which, way, SHIFT_PAD:SHIFT_PAD + CHUNK, :] = parts[off]
            acc = acc + self.slots[which, way, SHIFT_PAD + off:SHIFT_PAD + CHUNK + off, :]
        return acc


def _sub_block_rows():
    return jnp.bitwise_and(_iota2((CHUNK, 1), 0), DIAG_ROWS - 1)


def _diag_forward(rows, q, key, bc, v):
    rmod = _sub_block_rows()
    k_d, b_d, v_d = rows.down(key, 0), rows.down(bc, 1), rows.down(v, 2)
    o = None
    for off in range(DIAG_ROWS):
        e = jnp.exp(jnp.where(rmod >= off, bc - b_d[off], -jnp.inf))
        term = jnp.sum(q * k_d[off] * e, axis=-1, keepdims=True) * v_d[off]
        o = term if o is None else o + term
    return o


def _diag_backward(rows, q, key, bc, v, do):
    rmod = _sub_block_rows()
    k_d, b_d, v_d = rows.down(key, 0), rows.down(bc, 1), rows.down(v, 2)
    dq = db = None
    dk_parts, db_parts, dv_parts = [], [], []
    for off in range(DIAG_ROWS):
        e = jnp.exp(jnp.where(rmod >= off, bc - b_d[off], -jnp.inf))
        qe = q * e
        a = jnp.sum(qe * k_d[off], axis=-1, keepdims=True)
        da = jnp.sum(do * v_d[off], axis=-1, keepdims=True)
        dv_parts.append(a * do)
        dq_term = (da * e) * k_d[off]
        dk_term = da * qe
        s = dk_term * k_d[off]
        dq = dq_term if dq is None else dq + dq_term
        db = s if db is None else db + s
        dk_parts.append(dk_term)
        db_parts.append(s)
    return dq, rows.up_sum(dk_parts, 0), db - rows.up_sum(db_parts, 1), rows.up_sum(dv_parts, 2)


def diag_part(rows, differentiable=True):
    forward = functools.partial(_diag_forward, rows)
    if not differentiable:
        return forward
    part = jax.custom_vjp(forward)
    part.defvjp(lambda q, key, bc, v: (forward(q, key, bc, v), (q, key, bc, v)),
                lambda res, do: _diag_backward(rows, *res, do))
    return part


def hgrn_stages(qb, fb, ib, gb, l0, l1, norm_w, st, prec=HGRN_PREC, diags=None, o_known=None):
    c = CHUNK
    ri, ci = _iota2((4 * c, c), 0), _iota2((4 * c, c), 1)
    rcol = _iota2((c, 1), 0)
    blk0 = jnp.bitwise_and(ri, c - SUB_CHUNK)
    limit = jnp.where(ri < c, ri + 1, jnp.where(ri < 2 * c, blk0, jnp.where(ri < 3 * c, blk0 + SUB_CHUNK,
                                                                          blk0 + DIAG_ROWS)))
    sel = jnp.where(ci < limit, 1.0, 0.0)
    ri, ci = _iota2((c, c), 0), _iota2((c, c), 1)
    lb = _each(lambda a, b: jax.nn.sigmoid(a - b), l0, l1)
    forget = _each(lambda b, f: b + (1.0 - b) * jax.nn.sigmoid(f), lb, fb)
    key = _each(lambda b, f: (1.0 - b) * jax.nn.sigmoid(-f), lb, fb)
    q = _each(_silu, qb)
    v = ib
    logf = _each(jnp.log, forget)
    sums = _each(lambda x: sel_sums(sel, x), logf)
    bc, b_start, b_end, b_half = ([x[i] for x in sums] for i in range(4))
    b_last = _each(lambda x: jnp.sum(x, axis=0, keepdims=True), logf)
    o = _each(lambda x, b, z: mm(x * jnp.exp(b), z, NT, prec), q, bc, st)
    if diags is None:
        diags = [diag_part(RolledRows())] * len(qb)
    yield
    o = list(o)
    for h in range(len(o)):
        o[h] = o[h] + diags[h](q[h], key[h], bc[h], v[h])
        yield
    second = jnp.bitwise_and(rcol, SUB_CHUNK - 1) >= DIAG_ROWS
    same_sub = jnp.bitwise_and(ri, c - SUB_CHUNK) == jnp.bitwise_and(ci, c - SUB_CHUNK)
    q_half = _each(lambda x, b, bh: x * jnp.exp(jnp.where(second, b - bh, -jnp.inf)), q, bc, b_half)
    k_half = _each(lambda x, b, bh: x * jnp.exp(jnp.where(second, -jnp.inf, bh - b)), key, bc, b_half)
    a_half = _each(lambda x, z: jnp.where(same_sub, mm(x, z, NT, prec), 0.0), q_half, k_half)
    o = _each(lambda acc, a, val: acc + mm(a, val, NN, prec), o, a_half, v)
    yield
    q_rel = _each(lambda x, b, bs: x * jnp.exp(b - bs), q, bc, b_start)
    k_rel = _each(lambda x, b, be: x * jnp.exp(be - b), key, bc, b_end)
    for y in range(c // SUB_CHUNK - 1):
        def scaled(x, b, bs):
            end_y = jnp.sum(jnp.where(rcol == SUB_CHUNK * y + SUB_CHUNK - 1, b, 0.0), axis=0, keepdims=True)
            return x * jnp.exp(jnp.where(rcol >= SUB_CHUNK * (y + 1), bs - end_y, -jnp.inf))
        dq = _each(scaled, q_rel, bc, b_start)
        in_y = (ci >= SUB_CHUNK * y) & (ci < SUB_CHUNK * (y + 1))
        a_y = _each(lambda x, z: jnp.where(in_y, mm(x, z, NT, prec), 0.0), dq, k_rel)
        o = _each(lambda acc, a, val: acc + mm(a, val, NN, prec), o, a_y, v)
        yield
    k_state = _each(lambda x, bl, b: x * jnp.exp(bl - b), key, b_last, bc)
    st_new = _each(lambda z, bl, val, x: z * jnp.exp(bl) + mm(val, x, TN, prec), st, b_last, v, k_state)
    if o_known is not None:
        o = _each(_known_value, o, o_known)
    return (_each(lambda x, z: _head_norm_gate(x, norm_w, z), o, gb), st_new), o


def _drain(gen):
    try:
        while True:
            next(gen)
    except StopIteration as done:
        return done.value


def _alternate(gen_a, gen_b):
    out, live = [None, None], [gen_a, gen_b]
    while any(g is not None for g in live):
        for i, g in enumerate(live):
            if g is None:
                continue
            try:
                next(g)
            except StopIteration as done:
                out[i], live[i] = done.value, None
    return out


def gdn_chunks(*args, **kwargs):
    return _drain(gdn_stages(*args, **kwargs))


def hgrn_chunks(*args, **kwargs):
    return _drain(hgrn_stages(*args, **kwargs))


def hgrn_chunk(qb, fb, ib, gb, l0, l1, norm_w, st, prec=HGRN_PREC, reuse_output=False):
    args = ([qb], [fb], [ib], [gb], [l0], [l1], norm_w, [st], prec)
    if reuse_output:
        known = lax.stop_gradient(hgrn_chunks(*args)[1])
        (y, st_new), _ = hgrn_chunks(*args, o_known=known)
    else:
        (y, st_new), _ = hgrn_chunks(*args)
    return y[0], st_new[0]


HEAD_VEC = (N_HEADS, 1, LANES)


class _ChunkSpecs:
    def __init__(self, nc, rev):
        self.nc, self.rev = nc, rev

    def _c(self, c):
        return self.nc - 1 - c if self.rev else c

    def row(self, width, block=0):
        return pl.BlockSpec((CHUNK, width), lambda c: (self._c(c), block))

    def per_head(self, rows):
        return pl.BlockSpec((None, N_HEADS, rows, rows), lambda c: (self._c(c), 0, 0, 0))

    @staticmethod
    def whole(shape):
        return pl.BlockSpec(shape, lambda c: (0,) * len(shape))


def _lanes(j):
    return slice(j * LANES, (j + 1) * LANES)


def mixer_fwd(qkv_c, proj, a_log_l, dt_l, gdn_norm_w, l0, l1, hgrn_norm_w, name):
    t = qkv_c.shape[0]
    hb = N_HEADS
    sp = _ChunkSpecs(t // CHUNK, rev=False)
    hs = list(range(hb))

    def body(q_ref, k_ref, v_ref, z_ref, ab_ref, al_ref, dt_ref, gnw_ref, qb_ref, fb_ref, ib_ref, gb_ref, l0_ref, l1_ref,
             hnw_ref, y_ref, hist_a_ref, inv_ref, hist_b_ref, o_ref, sa_ref, sb_ref, shift_ref):
        @pl.when(pl.program_id(0) == 0)
        def _():
            sa_ref[...] = jnp.zeros_like(sa_ref)
            sb_ref[...] = jnp.zeros_like(sb_ref)
            shift_ref[...] = jnp.zeros_like(shift_ref)

        heads = lambda ref: [ref[:, _lanes(j)] for j in hs]
        s_a, s_b = [sa_ref[h] for h in hs], [sb_ref[h] for h in hs]
        for h in hs:
            hist_a_ref[h] = s_a[h]
            hist_b_ref[h] = s_b[h]
        diags = [diag_part(SlotRows(shift_ref.at[h]), differentiable=False) for h in hs]
        ((y_a, s_a_new), inv), ((y_b, s_b_new), o_pre) = _alternate(
            gdn_stages(hs, heads(q_ref), heads(k_ref), heads(v_ref), heads(z_ref), ab_ref[...],
                       [al_ref[h] for h in hs], [dt_ref[h] for h in hs], gnw_ref[...], s_a),
            hgrn_stages(heads(qb_ref), heads(fb_ref), heads(ib_ref), heads(gb_ref),
                        [l0_ref[h] for h in hs], [l1_ref[h] for h in hs], hnw_ref[...], s_b, diags=diags))
        for h in hs:
            y_ref[:, _lanes(h)] = y_a[h].astype(BF16)
            y_ref[:, _lanes(hb + h)] = y_b[h].astype(BF16)
            o_ref[:, _lanes(h)] = o_pre[h]
            sa_ref[h] = s_a_new[h]
            sb_ref[h] = s_b_new[h]
            inv_ref[h] = inv[h]

    vec, gain, slab = sp.whole(HEAD_VEC), sp.whole((1, LANES)), functools.partial(sp.row, GDN_WIDTH)
    states = jax.ShapeDtypeStruct((sp.nc, N_HEADS, HEAD_DIM, HEAD_DIM), F32)
    return pl.pallas_call(
        body, name=name, grid=(sp.nc,),
        in_specs=[slab(0), slab(1), slab(2), slab(3), sp.row(LANES, AB_BLOCK), vec, vec, gain,
                  slab(4), slab(5), slab(6), slab(7), vec, vec, gain],
        out_specs=[sp.row(2 * GDN_WIDTH), sp.per_head(HEAD_DIM), sp.per_head(CHUNK), sp.per_head(HEAD_DIM), slab(0)],
        out_shape=[jax.ShapeDtypeStruct((t, 2 * GDN_WIDTH), BF16), states,
                   jax.ShapeDtypeStruct((sp.nc, N_HEADS, CHUNK, CHUNK), F32), states,
                   jax.ShapeDtypeStruct((t, GDN_WIDTH), F32)],
        scratch_shapes=[pltpu.VMEM((N_HEADS, HEAD_DIM, HEAD_DIM), F32), pltpu.VMEM((N_HEADS, HEAD_DIM, HEAD_DIM), F32),
                        pltpu.VMEM((hb, 3, SHIFT_WAYS, SHIFT_ROWS, LANES), F32)],
        compiler_params=_params(("arbitrary",)),
    )(qkv_c, qkv_c, qkv_c, proj, proj, a_log_l, dt_l, gdn_norm_w, proj, proj, proj, proj, l0, l1, hgrn_norm_w)


def mixer_bwd(qkv_c, proj, a_log_l, dt_l, gdn_norm_w, l0, l1, hgrn_norm_w, hist_a, inv_hist, hist_b, o_pre, dy, name):
    t = qkv_c.shape[0]
    hb = N_HEADS
    sp = _ChunkSpecs(t // CHUNK, rev=True)
    hs = list(range(hb))

    def body(q_ref, k_ref, v_ref, z_ref, ab_ref, al_ref, dt_ref, gnw_ref, qb_ref, fb_ref, ib_ref, gb_ref, l0_ref, l1_ref,
             hnw_ref, hist_a_ref, inv_ref, hist_b_ref, o_ref, dy_ref,
             dqkv_ref, dproj_ref, dal_ref, ddt_ref, dgnw_ref, dl0_ref, dl1_ref, dhnw_ref, dsa_ref, dsb_ref, shift_ref):
        @pl.when(pl.program_id(0) == 0)
        def _():
            for ref in (dal_ref, ddt_ref, dgnw_ref, dl0_ref, dl1_ref, dhnw_ref, dsa_ref, dsb_ref, shift_ref):
                ref[...] = jnp.zeros_like(ref)

        heads = lambda ref, first=0: [ref[:, _lanes(first + j)] for j in hs]
        diags = [diag_part(SlotRows(shift_ref.at[h])) for h in hs]
        inv_known, o_known = [inv_ref[h] for h in hs], heads(o_ref)

        def both(ga, gb):
            (ra, inv), (rb, o_pre) = _alternate(gdn_stages(hs, *ga, inv_known=inv_known),
                                                hgrn_stages(*gb, diags=diags, o_known=o_known))
            return (ra, rb), (inv, o_pre)

        ga = (heads(q_ref), heads(k_ref), heads(v_ref), heads(z_ref), ab_ref[...], [al_ref[h] for h in hs],
              [dt_ref[h] for h in hs], gnw_ref[...], [hist_a_ref[h] for h in hs])
        gb = (heads(qb_ref), heads(fb_ref), heads(ib_ref), heads(gb_ref), [l0_ref[h] for h in hs],
              [l1_ref[h] for h in hs], hnw_ref[...], [hist_b_ref[h] for h in hs])
        _, vjp, _ = jax.vjp(both, ga, gb, has_aux=True)
        dy_a = [x.astype(F32) for x in heads(dy_ref)]
        dy_b = [x.astype(F32) for x in heads(dy_ref, hb)]
        (dq, dk, dv, dz, dab, dal, ddt, dgnw, ds_a), (dqb, dfb, dib, dgb, dl0, dl1, dhnw, ds_b) = vjp(
            ((dy_a, [dsa_ref[h] for h in hs]), (dy_b, [dsb_ref[h] for h in hs])))
        for h in hs:
            dqkv_ref[:, _lanes(h)] = dq[h]
            dqkv_ref[:, _lanes(hb + h)] = dk[h]
            dqkv_ref[:, _lanes(2 * hb + h)] = dv[h]
            for slab, val in enumerate((dz, dqb, dfb, dib, dgb)):
                dproj_ref[:, _lanes((3 + slab) * hb + h)] = val[h].astype(BF16)
            dal_ref[h] += dal[h]
            ddt_ref[h] += ddt[h]
            dl0_ref[h] += dl0[h]
            dl1_ref[h] += dl1[h]
            dsa_ref[h] = ds_a[h]
            dsb_ref[h] = ds_b[h]
        dproj_ref[:, MAIN_WIDTH:] = dab.astype(BF16)
        dgnw_ref[...] += dgnw
        dhnw_ref[...] += dhnw

    vec, gain, slab = sp.whole(HEAD_VEC), sp.whole((1, LANES)), functools.partial(sp.row, GDN_WIDTH)
    vec_shape, gain_shape = jax.ShapeDtypeStruct(HEAD_VEC, F32), jax.ShapeDtypeStruct((1, LANES), F32)
    return pl.pallas_call(
        body, name=name, grid=(sp.nc,),
        in_specs=[slab(0), slab(1), slab(2), slab(3), sp.row(LANES, AB_BLOCK), vec, vec, gain,
                  slab(4), slab(5), slab(6), slab(7), vec, vec, gain,
                  sp.per_head(HEAD_DIM), sp.per_head(CHUNK), sp.per_head(HEAD_DIM), slab(0), sp.row(2 * GDN_WIDTH)],
        out_specs=[sp.row(QKV_WIDTH), sp.row(CAT_WIDTH), vec, vec, gain, vec, vec, gain],
        out_shape=[jax.ShapeDtypeStruct((t, QKV_WIDTH), F32), jax.ShapeDtypeStruct((t, CAT_WIDTH), BF16),
                   vec_shape, vec_shape, gain_shape, vec_shape, vec_shape, gain_shape],
        scratch_shapes=[pltpu.VMEM((N_HEADS, HEAD_DIM, HEAD_DIM), F32), pltpu.VMEM((N_HEADS, HEAD_DIM, HEAD_DIM), F32),
                        pltpu.VMEM((hb, 3, SHIFT_WAYS, SHIFT_ROWS, LANES), F32)],
        compiler_params=_params(("arbitrary",)),
    )(qkv_c, qkv_c, qkv_c, proj, proj, a_log_l, dt_l, gdn_norm_w, proj, proj, proj, proj, l0, l1, hgrn_norm_w,
      hist_a, inv_hist, hist_b, o_pre, dy)


def _adamw(w, g, m, v):
    m = ADAM_B1 * m + (1.0 - ADAM_B1) * g
    v = ADAM_B2 * v + (1.0 - ADAM_B2) * jnp.square(g)
    m_hat = m / (1.0 - ADAM_B1 ** ADAM_STEP)
    v_hat = v / (1.0 - ADAM_B2 ** ADAM_STEP)
    delta = -ADAM_LR * (m_hat / (jnp.sqrt(v_hat) + ADAM_EPS) + ADAM_WD * w)
    return delta, m, v


def adamw_reduce(parts, mine, slot, w, m, v, name, rb=128):
    r, c = w.shape
    rb = min(rb, r)
    n_parts = parts.shape[0]

    def body(slot_ref, p_ref, own_ref, w_ref, m_ref, v_ref, g_ref, d_ref, mo_ref, vo_ref):
        part = lambda d: jnp.where(slot_ref[0] == d, own_ref[...], p_ref[d]).astype(F32)
        g = part(0)
        for d in range(1, n_parts):
            g = g + part(d)
        delta, mn, vn = _adamw(w_ref[...], g, m_ref[...], v_ref[...])
        g_ref[...] = g
        d_ref[...] = delta
        mo_ref[...] = mn
        vo_ref[...] = vn

    blk = pl.BlockSpec((rb, c), lambda i, s: (i, 0))
    return pl.pallas_call(
        body, name=name,
        grid_spec=pltpu.PrefetchScalarGridSpec(
            num_scalar_prefetch=1, grid=(r // rb,),
            in_specs=[pl.BlockSpec((n_parts, rb, c), lambda i, s: (0, i, 0)),
                      pl.BlockSpec((None, rb, c), lambda i, s: (s[0], i, 0)), blk, blk, blk],
            out_specs=[blk] * 4),
        out_shape=[jax.ShapeDtypeStruct((r, c), F32)] * 4,
        compiler_params=_params(("parallel",)))(slot.astype(jnp.int32).reshape(1), parts, mine, w, m, v)


def adamw_small(w, g, m, v, name):
    def body(w_ref, g_ref, m_ref, v_ref, d_ref, mo_ref, vo_ref):
        delta, mn, vn = _adamw(w_ref[...], g_ref[...], m_ref[...], v_ref[...])
        d_ref[...] = delta
        mo_ref[...] = mn
        vo_ref[...] = vn

    vmem = pl.BlockSpec(memory_space=pltpu.VMEM)
    return pl.pallas_call(body, name=name, in_specs=[vmem] * 4, out_specs=[vmem] * 3,
                          out_shape=[jax.ShapeDtypeStruct(w.shape, F32)] * 3)(w, g, m, v)


def _pack(arrays):
    flat = jnp.concatenate([a.reshape(-1).astype(F32) for a in arrays])
    rows = -(-flat.shape[0] // (8 * LANES)) * 8
    return jnp.pad(flat, (0, rows * LANES - flat.shape[0])).reshape(rows, LANES)


def _unpack(packed, shapes):
    flat, out, off = packed.reshape(-1), [], 0
    for s in shapes:
        n = 1
        for d in s:
            n *= d
        out.append(flat[off:off + n].reshape(s))
        off += n
    return out


def _relu2_epilogue(acc, _):
    r = jnp.maximum(acc, 0.0)
    return acc, r * r


def _relu2_bwd_epilogue(acc, a1):
    return (acc * (2.0 * jnp.maximum(a1, 0.0)),)


def kernel(x, w_in, conv_w, gdn_a_log, gdn_dt_bias, gdn_norm_w, hgrn_lb_logits, hgrn_norm_w, w_out, norm_mix_w, norm_ffn_w, w_ff1, w_ff2, norm_final_w, loss_target, m_w_in, m_conv_w, m_gdn_a_log, m_gdn_dt_bias, m_gdn_norm_w, m_hgrn_lb_logits, m_hgrn_norm_w, m_w_out, m_norm_mix_w, m_norm_ffn_w, m_w_ff1, m_w_ff2, m_norm_final_w, v_w_in, v_conv_w, v_gdn_a_log, v_gdn_dt_bias, v_gdn_norm_w, v_hgrn_lb_logits, v_hgrn_norm_w, v_w_out, v_norm_mix_w, v_norm_ffn_w, v_w_ff1, v_w_ff2, v_norm_final_w):
    me = _my_flat()
    xs = x[0]
    target = loss_target[0]
    shard_in = w_in.shape[2]
    shard_conv = conv_w.shape[2]

    tok = lambda t: t[0:1, 0:1]

    half = D_MODEL // 2
    w_in_b = w_in[0].astype(BF16)
    h_ga, t_ga = exchange_start([w_in_b[:half], conv_w[0]], True, "gather_w_in_high_start", peers=CHIP_PEERS)
    h_g0, t_g0 = exchange_start([w_in_b[half:]], True, "gather_w_in_low_start", after=[t_ga], peers=CHIP_PEERS)
    behind = lambda a: a + tok(t_g0)
    h_g1, t_g1 = exchange_start([behind(w_out[0]).astype(BF16), behind(w_ff1[0]).astype(BF16)], True,
                                "gather_mid_start", after=[t_g0], peers=CHIP_PEERS)
    h_g2, t_g2 = exchange_start([behind(w_ff2[0]).astype(BF16)], True, "gather_ff2_start", after=[t_g1],
                                peers=CHIP_PEERS)
    m_in, v_in = m_w_in[0] + tok(t_g2), v_w_in[0] + tok(t_g2)

    lane_b = lambda p: jnp.broadcast_to(p.reshape(N_HEADS, 1, 1), HEAD_VEC)
    a_log_l, dt_l = lane_b(gdn_a_log[0]), lane_b(gdn_dt_bias[0])
    l0 = hgrn_lb_logits[0].reshape(HEAD_VEC)
    l1 = hgrn_lb_logits[1].reshape(HEAD_VEC)

    n1, r1 = rms_fwd(xs, norm_mix_w + tok(t_g1) + tok(t_g2), "rms_mix")
    (s_high, s_conv), (l_high, l_conv) = exchange_wait(h_ga, "gather_w_in_high_wait", after=[n1, m_in, v_in],
                                                       copies=len(CHIP_PEERS))
    h_fa, _ = forward_start([l_high, l_conv], "gather_w_in_high_forward")
    _, (l_high, l_conv) = exchange_wait(h_fa, "forward_w_in_high_wait", copies=len(OTHER_CHIPS))
    w_cat = weights_to_cat(_own_slot(l_high, s_high), "weights_to_cat", D_MODEL)
    conv_full = jnp.transpose(_own_slot(l_conv, s_conv), (1, 0, 2)).reshape(4, QKV_WIDTH)
    proj = matmul(n1, w_cat, "nn", "in_proj_high", (BF16,), tn=CAT_WIDTH // 5, tk=half, k_blocks=(0, 1))
    (s_low,), (l_low,) = exchange_wait(h_g0, "gather_w_in_low_wait", after=[proj], copies=len(CHIP_PEERS))
    h_f0, _ = forward_start([l_low], "gather_w_in_low_forward")
    _, (l_low,) = exchange_wait(_one(h_f0, 0), "forward_w_in_low_wait", copies=len(OTHER_CHIPS))
    w_cat = weights_to_cat(_own_slot(l_low, s_low), "weights_to_cat_low", D_MODEL, row0=half, into=w_cat)
    proj = matmul(n1, w_cat, "nn", "in_proj_low", tn=CAT_WIDTH // 5, tk=half, k_blocks=(1, 1), extra=proj,
                  epilogue=lambda acc, high: (acc + high,))
    qkv_c = conv_fwd(proj, conv_full, "conv_fwd")
    y, hist_a, inv_a, hist_b, o_b = mixer_fwd(qkv_c, proj, a_log_l, dt_l, gdn_norm_w, l0, l1, hgrn_norm_w, "mixer_fwd")
    (s_out, s_ff1), (l_out, l_ff1) = exchange_wait(h_g1, "gather_mid_wait", after=[y], copies=len(CHIP_PEERS))
    (s_ff2,), (l_ff2,) = exchange_wait(h_g2, "gather_ff2_wait", after=[y], copies=len(CHIP_PEERS))
    h_fw, _ = forward_start([l_out, l_ff1, l_ff2], "gather_forward_start")
    _, (l_out,) = exchange_wait(_one(h_fw, 0), "forward_out_wait", copies=len(OTHER_CHIPS))
    w_out_full = _own_slot(l_out, s_out).reshape(D_MODEL, D_MODEL)
    h1, n2, r2 = out_proj_rms(y, w_out_full, xs, norm_ffn_w, "out_proj_rms")
    _, (l_ff1,) = exchange_wait(_one(h_fw, 1), "forward_ff1_wait", after=[n2], copies=len(OTHER_CHIPS))
    w_ff1_sh = _own_slot(l_ff1, s_ff1)
    a1, act = matmul(n2, w_ff1_sh, "nn", "ff1", out_dtypes=(F32, BF16), epilogue=_relu2_epilogue, b_shards=True)
    _, (l_ff2,) = exchange_wait(_one(h_fw, 2), "forward_ff2_wait", after=[act], copies=len(OTHER_CHIPS))
    w_ff2_full = _own_slot(l_ff2, s_ff2).reshape(D_FF, D_MODEL)
    loss_sum, dh2_b, d_final = ff2_loss(act, w_ff2_full, h1, norm_final_w.reshape(1, D_MODEL), target, "ff2_loss")

    da1 = matmul(dh2_b, w_ff2_full, "nt", "d_act", out_dtypes=(BF16,), epilogue=_relu2_bwd_epilogue, extra=a1)
    t_all = xs.shape[0]
    dw_ff2 = matmul(act, dh2_b, "tn", "dw_ff2", out_dtypes=(BF16,), tk=t_all)
    p_ff2 = dw_ff2.reshape(N_DEV, D_FF // N_DEV, D_MODEL)
    h_s1, t_s1 = exchange_start([p_ff2], False, "scatter_ff2_start")
    dn2 = matmul(da1, w_ff1_sh, "nt", "d_n2", out_dtypes=(BF16,), after=[t_s1], b_shards=True, k_group=4)
    p_ff1 = matmul(n2, da1, "tn", "dw_ff1", out_dtypes=(BF16,), tn=D_FF // N_DEV, tk=t_all, after=[t_s1], out_shards=True)
    h_s2, t_s2 = exchange_start([p_ff1], False, "scatter_ff1_start")
    dh1_b, d_ffn = rms_bwd(h1, r2, norm_ffn_w + tok(t_s2), dn2, dh2_b, BF16, "rms_ffn_bwd")
    dmix = matmul(dh1_b, w_out_full, "nt", "d_mix", out_dtypes=(BF16,))
    dw_out = matmul(y, dh1_b, "tn", "dw_out", out_dtypes=(BF16,), tk=t_all)
    p_out = dw_out.reshape(N_DEV, D_MODEL // N_DEV, D_MODEL)
    h_s3, t_s3 = exchange_start([p_out], False, "scatter_out_start")
    d_qkv_c, dproj, d_alog_l, d_dt_l, d_gnw, dl0, dl1, d_hnw = mixer_bwd(
        qkv_c, proj, a_log_l, dt_l, gdn_norm_w + tok(t_s3), l0, l1, hgrn_norm_w, hist_a, inv_a, hist_b, o_b, dmix,
        "mixer_bwd")
    dproj, d_conv_full = conv_bwd(proj, d_qkv_c, conv_full, dproj, "conv_bwd")
    dw_cat = matmul(n1, dproj, "tn", "dw_in", out_dtypes=(BF16,), tm=512, tn=CAT_WIDTH // 5, tk=t_all)
    p_in = cat_to_shards(dw_cat, shard_in)
    h_pair, t_s4 = routed_start(p_in, _to_sibling_routes, "scatter_in_pair_start")

    (s_ff2g,), (r_ff2,) = exchange_wait(h_s1, "scatter_ff2_wait", after=[t_s4])
    (s_ff1g,), (r_ff1,) = exchange_wait(h_s2, "scatter_ff1_wait", after=[t_s4])
    (s_outg,), (r_out,) = exchange_wait(h_s3, "scatter_out_wait", after=[t_s4])
    g_w_ff2, d_w_ff2, nm_w_ff2, nv_w_ff2 = adamw_reduce(
        r_ff2, s_ff2g, me, w_ff2[0], m_w_ff2[0], v_w_ff2[0], "adamw_w_ff2")
    g_w_ff1, d_w_ff1, nm_w_ff1, nv_w_ff1 = adamw_reduce(
        r_ff1, s_ff1g, me, w_ff1[0], m_w_ff1[0], v_w_ff1[0], "adamw_w_ff1")
    g_w_out, d_w_out, nm_w_out, nv_w_out = adamw_reduce(
        r_out, s_outg, me, w_out[0], m_w_out[0], v_w_out[0], "adamw_w_out")
    (p_in,), (from_sibling,) = exchange_wait(h_pair, "scatter_in_pair_wait", after=[d_w_ff2, d_w_ff1, d_w_out],
                                             copies=N_CHIPS)
    chip_sums = pair_sum(p_in, from_sibling, "scatter_in_pair_sum")
    h_chips, t_s5 = routed_start(chip_sums, _to_chips_routes, "scatter_in_chips_start")
    dn1 = matmul(dproj, w_cat, "nt", "d_n1", out_dtypes=(BF16,), tm=512, tn=512, tk=CAT_WIDTH, after=[t_s5])
    dx, d_mix = rms_bwd(xs, r1, norm_mix_w, dn1, dh1_b, F32, "rms_mix_bwd")
    (chip_sums,), (r_in,) = exchange_wait(h_chips, "scatter_in_chips_wait", after=[dx], copies=len(OTHER_CHIPS))
    g_w_in, d_w_in, nm_w_in, nv_w_in = adamw_reduce(
        r_in, chip_sums, me // 2, w_in[0], m_in, v_in, "adamw_w_in")

    d_lb = jnp.stack([dl0.reshape(GDN_WIDTH), dl1.reshape(GDN_WIDTH)])
    small_shapes = [(1, N_HEADS), (1, N_HEADS), (1, HEAD_DIM), (2, GDN_WIDTH), (1, HEAD_DIM), (1, D_MODEL),
                    (1, D_MODEL), (D_MODEL,), (4, QKV_WIDTH), ()]
    small = _pack([d_alog_l[:, 0, 0], d_dt_l[:, 0, 0], d_gnw, d_lb, d_hnw, d_mix, d_ffn, d_final, d_conv_full,
                   loss_sum[0, 0]])
    red = allreduce_small(small, "allreduce_small")
    g_alog, g_dt, g_gnw, g_lb, g_hnw, g_mix, g_ffn, g_final, g_conv_full, loss = _unpack(red, small_shapes)
    g_conv = lax.dynamic_slice(g_conv_full, (0, me * shard_conv), (4, shard_conv)).reshape(1, 4, shard_conv)
    small_g = [g_alog, g_dt, g_gnw, g_lb, g_hnw, g_mix, g_ffn, g_final, g_conv]
    small_w = [gdn_a_log, gdn_dt_bias, gdn_norm_w, hgrn_lb_logits, hgrn_norm_w, norm_mix_w, norm_ffn_w, norm_final_w, conv_w]
    small_m = [m_gdn_a_log, m_gdn_dt_bias, m_gdn_norm_w, m_hgrn_lb_logits, m_hgrn_norm_w, m_norm_mix_w, m_norm_ffn_w,
               m_norm_final_w, m_conv_w]
    small_v = [v_gdn_a_log, v_gdn_dt_bias, v_gdn_norm_w, v_hgrn_lb_logits, v_hgrn_norm_w, v_norm_mix_w, v_norm_ffn_w,
               v_norm_final_w, v_conv_w]
    shapes = [a.shape for a in small_w]
    d_s, m_s, v_s = adamw_small(_pack(small_w), _pack(small_g), _pack(small_m), _pack(small_v), "adamw_small")
    d_alog, d_dt, d_gn, d_lbl, d_hn, d_nm, d_nf, d_nfin, d_cw = _unpack(d_s, shapes)
    m_alog, m_dt, m_gn, m_lbl, m_hn, m_nm, m_nf, m_nfin, m_cw = _unpack(m_s, shapes)
    v_alog, v_dt, v_gn, v_lbl, v_hn, v_nm, v_nf, v_nfin, v_cw = _unpack(v_s, shapes)

    lead = lambda a: a[None]
    grads = [lead(g_w_in), g_conv, g_alog, g_dt, g_gnw, g_lb, g_hnw, lead(g_w_out), g_mix, g_ffn,
             lead(g_w_ff1), lead(g_w_ff2), g_final]
    deltas = [lead(d_w_in), d_cw, d_alog, d_dt, d_gn, d_lbl, d_hn, lead(d_w_out), d_nm, d_nf,
              lead(d_w_ff1), lead(d_w_ff2), d_nfin]
    new_m = [lead(nm_w_in), m_cw, m_alog, m_dt, m_gn, m_lbl, m_hn, lead(nm_w_out), m_nm, m_nf,
             lead(nm_w_ff1), lead(nm_w_ff2), m_nfin]
    new_v = [lead(nv_w_in), v_cw, v_alog, v_dt, v_gn, v_lbl, v_hn, lead(nv_w_out), v_nm, v_nf,
             lead(nv_w_ff1), lead(nv_w_ff2), v_nfin]
    return (loss, dx[None], *grads, *deltas, *new_m, *new_v)
```

```python
import functools

import jax
import jax.numpy as jnp
from jax import lax
from jax.experimental import pallas as pl
from jax.experimental.pallas import tpu as pltpu

F32 = jnp.float32
BF16 = jnp.bfloat16
HI = lax.Precision.HIGHEST

N_DEV = 8
D_MODEL = 2048
CHUNK = 64
SUB_CHUNK = 16
HEAD_DIM = 128
N_HEADS = 8
GDN_WIDTH = N_HEADS * HEAD_DIM
D_FF = 4 * D_MODEL
QKV_WIDTH = 3 * GDN_WIDTH
MAIN_WIDTH = 8 * GDN_WIDTH
CAT_WIDTH = MAIN_WIDTH + 128
AB_BLOCK = MAIN_WIDTH // 128
NORM_EPS = 1e-6
L2_EPS = 1e-6
LANES = 128
VMEM_LIMIT = 56 * 1024 * 1024

ADAM_LR = 0.001
ADAM_B1 = 0.9
ADAM_B2 = 0.999
ADAM_EPS = 1e-08
ADAM_WD = 0.01
ADAM_STEP = 10

MESH = pl.DeviceIdType.MESH


def _params(sem=None):
    return pltpu.CompilerParams(dimension_semantics=sem, vmem_limit_bytes=VMEM_LIMIT)


def _dot(a, b, dims, prec=None):
    return lax.dot_general(a, b, (dims, ((), ())), precision=prec, preferred_element_type=F32)


NN = ((1,), (0,))
NT = ((1,), (1,))
TN = ((0,), (0,))


def _split_bf16(x, pieces):
    out = []
    for _ in range(pieces - 1):
        p = x.astype(BF16)
        out.append(p)
        x = x - p.astype(F32)
    out.append(x.astype(BF16))
    return out


def _mm_raw(a, b, dims, prec):
    if prec == "hi":
        return _dot(a, b, dims, HI)
    if prec == "bf":
        return _dot(a.astype(BF16), b.astype(BF16), dims)
    a_hi, a_lo = _split_bf16(a, 2)
    b_hi, b_lo = _split_bf16(b, 2)
    return _dot(a_hi, b_hi, dims) + (_dot(a_hi, b_lo, dims) + _dot(a_lo, b_hi, dims))


@functools.partial(jax.custom_vjp, nondiff_argnums=(2, 3))
def mm(a, b, dims, prec):
    return _mm_raw(a, b, dims, prec)


def _mm_fwd(a, b, dims, prec):
    return _mm_raw(a, b, dims, prec), (a, b)


def _mm_bwd(dims, prec, res, ct):
    a, b = res
    if dims == NN:
        return _mm_raw(ct, b, NT, prec), _mm_raw(a, ct, TN, prec)
    if dims == NT:
        return _mm_raw(ct, b, NN, prec), _mm_raw(ct, a, TN, prec)
    return _mm_raw(b, ct, NT, prec), _mm_raw(a, ct, NN, prec)


mm.defvjp(_mm_fwd, _mm_bwd)


def _sel_raw(sel, x, dims):
    sel = sel.astype(BF16)
    p0, p1, p2 = _split_bf16(x, 3)
    return _dot(sel, p0, dims) + (_dot(sel, p1, dims) + _dot(sel, p2, dims))


def _sel_parts(sel, x):
    c = x.shape[0]
    full = _sel_raw(sel, x, NN)
    return tuple(full[i * c:(i + 1) * c] for i in range(sel.shape[0] // c))


@jax.custom_vjp
def sel_sums(sel, x):
    return _sel_parts(sel, x)


def _sel_fwd(sel, x):
    return _sel_parts(sel, x), sel


def _sel_bwd(sel, cts):
    return jnp.zeros_like(sel), _sel_raw(sel, jnp.concatenate(cts, axis=0), TN)


sel_sums.defvjp(_sel_fwd, _sel_bwd)


@jax.custom_vjp
def _known_value(computed, known):
    del computed
    return known


_known_value.defvjp(lambda computed, known: (known, None), lambda _, ct: (ct, jnp.zeros_like(ct)))


def _my_flat():
    return 4 * lax.axis_index("x") + 2 * lax.axis_index("y") + lax.axis_index("c")


def _peer(k):
    x, y, c = lax.axis_index("x"), lax.axis_index("y"), lax.axis_index("c")
    kx, ky, kc = (k >> 2) & 1, (k >> 1) & 1, k & 1
    px = (1 - x) if kx else x
    py = (1 - y) if ky else y
    pc = (1 - c) if kc else c
    return (px, py, pc), 4 * px + 2 * py + pc


HBM_SPEC = pl.BlockSpec(memory_space=pltpu.HBM)
SEM_SPEC = pl.BlockSpec(memory_space=pltpu.SEMAPHORE)
ANY_SPEC = pl.BlockSpec(memory_space=pl.ANY)
DATAFLOW = pltpu.SideEffectType.DATAFLOW_SIDE_EFFECTING


def _in_hbm(x):
    return pltpu.with_memory_space_constraint(x, pltpu.HBM)


ALL_PEERS = tuple(range(1, N_DEV))
CHIP_PEERS = (1, 2, 4, 6)
OTHER_CHIPS = (2, 4, 6)


def exchange_start(xs, gather, name, after=(), peers=ALL_PEERS):
    n, n_after = len(xs), len(after)

    def body(*refs):
        x_refs, land_refs = refs[:n], refs[n:2 * n]
        sems = refs[2 * n + n_after:2 * n + n_after + 2 * n]
        token = refs[-1]
        me = _my_flat()
        for k in peers:
            peer, peer_flat = _peer(k)
            for a in range(n):
                src = x_refs[a] if gather else x_refs[a].at[peer_flat]
                pltpu.make_async_remote_copy(src_ref=src, dst_ref=land_refs[a].at[me], send_sem=sems[a],
                                             recv_sem=sems[n + a], device_id=peer, device_id_type=MESH).start()
        token[...] = jnp.zeros_like(token)

    lands =[_in_hbm(lax.empty(((N_DEV,) + x.shape) if gather else x.shape, x.dtype)) for x in xs]
    hbm_out = [pltpu.HBM(x.shape, x.dtype) for x in xs] + [pltpu.HBM(l.shape, l.dtype) for l in lands]
    res = pl.pallas_call(
        body, name=name,
        out_shape=(*([pltpu.SemaphoreType.DMA(())] * (2 * n)), *hbm_out, jax.ShapeDtypeStruct((8, LANES), F32)),
        in_specs=[HBM_SPEC] * (2 * n) + [ANY_SPEC] * n_after,
        out_specs=(*([SEM_SPEC] * (2 * n)), *([HBM_SPEC] * (2 * n)), pl.BlockSpec(memory_space=pltpu.VMEM)),
        input_output_aliases={i: 2 * n + i for i in range(2 * n)},
        compiler_params=pltpu.CompilerParams(has_side_effects=DATAFLOW),
    )(*[_in_hbm(x) for x in xs], *lands, *after)
    return (list(res[:2 * n]), list(res[2 * n:3 * n]), list(res[3 * n:4 * n])), res[-1]


def forward_start(lands, name, after=()):
    n, n_after = len(lands), len(after)

    def body(*refs):
        land_refs = refs[:n]
        sems = refs[n + n_after:n + n_after + 2 * n]
        token = refs[-1]
        sibling, _ = _peer(1)
        for a in range(n):
            for k in OTHER_CHIPS:
                _, from_flat = _peer(k)
                slot = land_refs[a].at[from_flat]
                pltpu.make_async_remote_copy(src_ref=slot, dst_ref=slot, send_sem=sems[a], recv_sem=sems[n + a],
                                             device_id=sibling, device_id_type=MESH).start()
        token[...] = jnp.zeros_like(token)

    res = pl.pallas_call(
        body, name=name,
        out_shape=(*([pltpu.SemaphoreType.DMA(())] * (2 * n)), *[pltpu.HBM(l.shape, l.dtype) for l in lands],
                   jax.ShapeDtypeStruct((8, LANES), F32)),
        in_specs=[HBM_SPEC] * n + [ANY_SPEC] * n_after,
        out_specs=(*([SEM_SPEC] * (2 * n)), *([HBM_SPEC] * n), pl.BlockSpec(memory_space=pltpu.VMEM)),
        input_output_aliases={i: 2 * n + i for i in range(n)},
        compiler_params=pltpu.CompilerParams(has_side_effects=DATAFLOW),
    )(*lands, *after)
    return (list(res[:2 * n]), [], list(res[2 * n:3 * n])), res[-1]


def exchange_wait(handle, name, after=(), copies=N_DEV - 1):
    sems, xs, lands = handle
    n, n_x, n_after = len(lands), len(xs), len(after)

    def body(*refs):
        land_refs = refs[n_x:n_x + n]
        sem_refs = refs[n_x + n:n_x + 3 * n]
        for a in range(n):
            every = land_refs[a].at[pl.ds(0, copies)]
            cp = pltpu.make_async_remote_copy(src_ref=every, dst_ref=every, send_sem=sem_refs[a],
                                              recv_sem=sem_refs[n + a], device_id=_peer(1)[0], device_id_type=MESH)
            cp.wait_send()
            cp.wait_recv()

    res = pl.pallas_call(
        body, name=name,
        out_shape=[pltpu.HBM(x.shape, x.dtype) for x in xs] + [pltpu.HBM(l.shape, l.dtype) for l in lands],
        in_specs=[HBM_SPEC] * (n_x + n) + [SEM_SPEC] * (2 * n) + [ANY_SPEC] * n_after,
        out_specs=[HBM_SPEC] * (n_x + n),
        input_output_aliases={i: i for i in range(n_x + n)},
        compiler_params=pltpu.CompilerParams(has_side_effects=DATAFLOW),
    )(*xs, *lands, *sems, *after)
    return list(res[:n_x]), list(res[n_x:])


N_CHIPS = N_DEV // 2


def routed_start(x, routes, name, after=()):
    n_after = len(after)

    def body(*refs):
        x_ref, land_ref = refs[0], refs[1]
        send_sem, recv_sem = refs[2 + n_after], refs[3 + n_after]
        token = refs[-1]
        for src, dst, peer in routes():
            pltpu.make_async_remote_copy(src_ref=x_ref.at[src], dst_ref=land_ref.at[dst], send_sem=send_sem,
                                         recv_sem=recv_sem, device_id=peer, device_id_type=MESH).start()
        token[...] = jnp.zeros_like(token)

    land = _in_hbm(lax.empty((N_CHIPS,) + x.shape[1:], x.dtype))
    res = pl.pallas_call(
        body, name=name,
        out_shape=(pltpu.SemaphoreType.DMA(()), pltpu.SemaphoreType.DMA(()), pltpu.HBM(x.shape, x.dtype),
                   pltpu.HBM(land.shape, land.dtype), jax.ShapeDtypeStruct((8, LANES), F32)),
        in_specs=[HBM_SPEC, HBM_SPEC] + [ANY_SPEC] * n_after,
        out_specs=(SEM_SPEC, SEM_SPEC, HBM_SPEC, HBM_SPEC, pl.BlockSpec(memory_space=pltpu.VMEM)),
        input_output_aliases={0: 2, 1: 3},
        compiler_params=pltpu.CompilerParams(has_side_effects=DATAFLOW),
    )(_in_hbm(x), land, *after)
    return ([res[0], res[1]], [res[2]], [res[3]]), res[-1]


def _to_sibling_routes():
    c = lax.axis_index("c")
    sibling, _ = _peer(1)
    return [(2 * chip + 1 - c, chip, sibling) for chip in range(N_CHIPS)]


def _to_chips_routes():
    my_chip = _my_flat() // 2
    routes = []
    for k in OTHER_CHIPS:
        peer, peer_flat = _peer(k)
        routes.append((peer_flat // 2, my_chip, peer))
    return routes


def pair_sum(p, from_sibling, name, rb=1024):
    _, r, c = p.shape
    mine = lax.axis_index("c").astype(jnp.int32).reshape(1)

    def body(kind_ref, p_ref, s_ref, o_ref):
        del kind_ref
        o_ref[...] = (p_ref[...].astype(F32) + s_ref[...].astype(F32)).astype(BF16)

    return pl.pallas_call(
        body, name=name,
        grid_spec=pltpu.PrefetchScalarGridSpec(
            num_scalar_prefetch=1, grid=(N_CHIPS, r // rb),
            in_specs=[pl.BlockSpec((None, None, rb, c), lambda chip, i, kind: (chip, kind[0], i, 0)),
                      pl.BlockSpec((None, rb, c), lambda chip, i, kind: (chip, i, 0))],
            out_specs=pl.BlockSpec((None, rb, c), lambda chip, i, kind: (chip, i, 0))),
        out_shape=jax.ShapeDtypeStruct((N_CHIPS, r, c), BF16),
        compiler_params=_params(("parallel", "parallel")))(mine, p.reshape(N_CHIPS, 2, r, c), from_sibling)


def _one(handle, a):
    sems, xs, lands = handle
    n = len(lands)
    return [sems[a], sems[n + a]], xs[a:a + 1], [lands[a]]


def _own_slot(land, block):
    return lax.dynamic_update_slice(land, block[None], (_my_flat(),) + (0,) * block.ndim)


def allreduce_small(x, name):
    rows = x.shape[0]

    def body(x_ref, o_ref, buf, send_sems, recv_sems):
        me = _my_flat()
        buf[me] = x_ref[...]
        sends = []
        for k in range(1, N_DEV):
            peer, _ = _peer(k)
            cp = pltpu.make_async_remote_copy(
                src_ref=x_ref, dst_ref=buf.at[me], send_sem=send_sems.at[k], recv_sem=recv_sems.at[k],
                device_id=peer, device_id_type=MESH)
            cp.start()
            sends.append(cp)
        for k in range(1, N_DEV):
            peer, peer_flat = _peer(k)
            pltpu.make_async_remote_copy(
                src_ref=x_ref, dst_ref=buf.at[peer_flat], send_sem=send_sems.at[k], recv_sem=recv_sems.at[k],
                device_id=peer, device_id_type=MESH).wait_recv()
        for cp in sends:
            cp.wait_send()
        acc = buf[0]
        for d in range(1, N_DEV):
            acc = acc + buf[d]
        o_ref[...] = acc

    vmem = pl.BlockSpec(memory_space=pltpu.VMEM)
    return pl.pallas_call(
        body, name=name, out_shape=jax.ShapeDtypeStruct((rows, LANES), F32),
        in_specs=[vmem], out_specs=vmem,
        scratch_shapes=[pltpu.VMEM((N_DEV, rows, LANES), F32),
                        pltpu.SemaphoreType.DMA((N_DEV,)), pltpu.SemaphoreType.DMA((N_DEV,))],
    )(x)


def matmul(a, b, mode, name, out_dtypes=(F32,), epilogue=None, extra=None, tm=1024, tn=1024, tk=2048, after=(),
           b_shards=False, out_shards=False, k_group=1, k_blocks=None):
    if b_shards:
        n_sh, b_rows, b_cols = b.shape
    if mode == "nn":
        (m, kd), n = a.shape, (n_sh * b_cols if b_shards else b.shape[1])
        if b_shards:
            tn = b_cols
    elif mode == "nt":
        (m, kd), n = a.shape, (b_rows if b_shards else b.shape[0])
        if b_shards:
            tk = k_group * b_cols
    else:
        (kd, m), n = a.shape, b.shape[1]
    tm, tn, tk = min(tm, m), min(tn, n), min(tk, kd)
    assert m % tm == 0 and n % tn == 0 and kd % tk == 0, (name, m, n, kd, tm, tn, tk)
    k0, ksteps = (0, kd // tk) if k_blocks is None else k_blocks
    dims = {"nn": NN, "nt": NT, "tn": TN}[mode]
    n_out = len(out_dtypes)
    n_in = 2 + (extra is not None) + len(after)

    def finish(acc, e_ref, o_refs):
        outs = (acc,) if epilogue is None else epilogue(acc, e_ref[...] if e_ref is not None else None)
        for o_ref, o in zip(o_refs, outs):
            o_ref[...] = o.astype(o_ref.dtype)

    def product(a_ref, b_ref):
        if mode == "nt" and b_shards:
            w = b_cols
            parts = [_dot(a_ref[:, s * w:(s + 1) * w], b_ref[s], dims) for s in range(k_group)]
            return functools.reduce(lambda p, q: p + q, parts)
        return _dot(a_ref[...], b_ref[...], dims)

    def body(*refs):
        a_ref, b_ref = refs[0], refs[1]
        e_ref = refs[2] if extra is not None else None
        o_refs = refs[n_in:n_in + n_out]
        if ksteps == 1:
            finish(product(a_ref, b_ref), e_ref, o_refs)
            return
        acc_ref = refs[-1]
        kk = pl.program_id(2)

        @pl.when(kk == 0)
        def _():
            acc_ref[...] = jnp.zeros_like(acc_ref)

        acc_ref[...] += product(a_ref, b_ref)

        @pl.when(kk == ksteps - 1)
        def _():
            finish(acc_ref[...], e_ref, o_refs)

    if mode == "nn":
        a_spec = pl.BlockSpec((tm, tk), lambda i, j, k: (i, k0 + k))
        b_spec = (pl.BlockSpec((None, tk, tn), lambda i, j, k: (j, k, 0)) if b_shards
                  else pl.BlockSpec((tk, tn), lambda i, j, k: (k0 + k, j)))
    elif mode == "nt":
        a_spec = pl.BlockSpec((tm, tk), lambda i, j, k: (i, k))
        b_spec = (pl.BlockSpec((k_group, tn, b_cols), lambda i, j, k: (k, j, 0)) if b_shards
                  else pl.BlockSpec((tn, tk), lambda i, j, k: (j, k)))
    else:
        a_spec = pl.BlockSpec((tk, tm), lambda i, j, k: (k, i))
        b_spec = pl.BlockSpec((tk, tn), lambda i, j, k: (k, j))
    o_spec = pl.BlockSpec((tm, tn), lambda i, j, k: (i, j))
    res_spec = pl.BlockSpec((None, tm, tn), lambda i, j, k: (j, i, 0)) if out_shards else o_spec
    res_shape = (n // tn, m, tn) if out_shards else (m, n)
    in_specs = [a_spec, b_spec] + ([o_spec] if extra is not None else []) + [ANY_SPEC] * len(after)
    args = (a, b) + ((extra,) if extra is not None else ()) + tuple(after)
    res = pl.pallas_call(
        body, name=name, grid=(m // tm, n // tn, ksteps),
        in_specs=in_specs, out_specs=[res_spec] * n_out,
        out_shape=[jax.ShapeDtypeStruct(res_shape, dt) for dt in out_dtypes],
        scratch_shapes=[pltpu.VMEM((tm, tn), F32)] if ksteps > 1 else [],
        compiler_params=_params(("parallel", "parallel", "arbitrary")),
    )(*args)
    return res if n_out > 1 else res[0]


GATE_COL = 4 * GDN_WIDTH
RELAYOUT_ROWS = 256


def _cat_of_win(j):
    if j < GATE_COL:
        return j
    if j < GATE_COL + 2 * N_HEADS:
        return MAIN_WIDTH + (j - GATE_COL)
    return j - 2 * N_HEADS


def _win_of_cat(c):
    if c < GATE_COL:
        return c
    if c < MAIN_WIDTH:
        return c + 2 * N_HEADS
    if c < MAIN_WIDTH + 2 * N_HEADS:
        return GATE_COL + (c - MAIN_WIDTH)
    return None


def _runs(first, count, mapping):
    runs, i = [], 0
    while i < count:
        start, n = mapping(first + i), 1
        while i + n < count and mapping(first + i + n) == start + n:
            n += 1
        runs.append((start, n))
        i += n
    return runs


def weights_to_cat(g_in, name, total_rows, row0=0, into=None):
    n_dev, rows, shard = g_in.shape
    first = row0 // RELAYOUT_ROWS

    def body(x_ref, *rest):
        o_ref = rest[-1]
        for b in range(CAT_WIDTH // LANES):
            live = sum(_win_of_cat(LANES * b + i) is not None for i in range(LANES))
            parts = []
            for start, n in _runs(LANES * b, live, _win_of_cat):
                while n > 0:
                    d, o = divmod(start, shard)
                    take = min(n, shard - o)
                    parts.append(x_ref[d, :, o:o + take])
                    start, n = start + take, n - take
            if live < LANES:
                parts.append(jnp.zeros((RELAYOUT_ROWS, LANES - live), g_in.dtype))
            o_ref[:, LANES * b:LANES * (b + 1)] = parts[0] if len(parts) == 1 else jnp.concatenate(parts, axis=1)

    return pl.pallas_call(
        body, name=name, grid=(rows // RELAYOUT_ROWS,),
        in_specs=[pl.BlockSpec((n_dev, RELAYOUT_ROWS, shard), lambda i: (0, i, 0))] + ([ANY_SPEC] if into is not None else []),
        out_specs=pl.BlockSpec((RELAYOUT_ROWS, CAT_WIDTH), lambda i: (first + i, 0)),
        out_shape=jax.ShapeDtypeStruct((total_rows, CAT_WIDTH), g_in.dtype),
        input_output_aliases={1: 0} if into is not None else {},
        compiler_params=_params(("parallel",)))(*((g_in,) if into is None else (g_in, into)))


def cat_to_shards(dw_cat, shard):
    rows = dw_cat.shape[0]

    def body(x_ref, o_ref):
        for d in range(N_DEV):
            for t0 in range(0, shard, LANES):
                width = min(LANES, shard - t0)
                parts = [x_ref[:, c:c + n] for c, n in _runs(d * shard + t0, width, _cat_of_win)]
                o_ref[d, :, t0:t0 + width] = parts[0] if len(parts) == 1 else jnp.concatenate(parts, axis=1)

    return pl.pallas_call(
        body, name="cat_to_shards", grid=(rows // RELAYOUT_ROWS,),
        in_specs=[pl.BlockSpec((RELAYOUT_ROWS, CAT_WIDTH), lambda i: (i, 0))],
        out_specs=pl.BlockSpec((N_DEV, RELAYOUT_ROWS, shard), lambda i: (0, i, 0)),
        out_shape=jax.ShapeDtypeStruct((N_DEV, rows, shard), dw_cat.dtype),
        compiler_params=_params(("parallel",)))(dw_cat)


ROW_BLOCK = 512


def rms_fwd(x, w, name):
    t, d = x.shape

    def body(x_ref, w_ref, n_ref, r_ref):
        h = x_ref[...]
        r = lax.rsqrt(jnp.mean(h * h, axis=-1, keepdims=True) + NORM_EPS)
        n_ref[...] = (h * r * w_ref[...]).astype(BF16)
        r_ref[...] = r

    row = pl.BlockSpec((ROW_BLOCK, d), lambda i: (i, 0))
    return pl.pallas_call(
        body, name=name, grid=(t // ROW_BLOCK,),
        in_specs=[row, pl.BlockSpec((1, d), lambda i: (0, 0))],
        out_specs=[row, pl.BlockSpec((ROW_BLOCK, 1), lambda i: (i, 0))],
        out_shape=[jax.ShapeDtypeStruct((t, d), BF16), jax.ShapeDtypeStruct((t, 1), F32)],
        compiler_params=_params(("parallel",)))(x, w)


FUSED_ROWS = 512


def out_proj_rms(y, w_out, x, w_norm, name):
    t, d = x.shape

    def body(y_ref, w_ref, x_ref, g_ref, h_ref, n_ref, r_ref):
        h = x_ref[...] + _dot(y_ref[...], w_ref[...], NN)
        r = lax.rsqrt(jnp.mean(h * h, axis=-1, keepdims=True) + NORM_EPS)
        h_ref[...] = h
        n_ref[...] = (h * r * g_ref[...]).astype(BF16)
        r_ref[...] = r

    row = pl.BlockSpec((FUSED_ROWS, d), lambda i: (i, 0))
    return pl.pallas_call(
        body, name=name, grid=(t // FUSED_ROWS,),
        in_specs=[pl.BlockSpec((FUSED_ROWS, y.shape[1]), lambda i: (i, 0)), pl.BlockSpec(w_out.shape, lambda i: (0, 0)),
                  row, pl.BlockSpec((1, d), lambda i: (0, 0))],
        out_specs=[row, row, pl.BlockSpec((FUSED_ROWS, 1), lambda i: (i, 0))],
        out_shape=[jax.ShapeDtypeStruct((t, d), F32), jax.ShapeDtypeStruct((t, d), BF16),
                   jax.ShapeDtypeStruct((t, 1), F32)],
        compiler_params=_params(("parallel",)))(y, w_out, x, w_norm)


def ff2_loss(act, w_ff2, h1, w, target, name, tk=2048):
    t, d = h1.shape
    ksteps = act.shape[1] // tk

    def body(a_ref, b_ref, h_ref, w_ref, t_ref, loss_ref, dhb_ref, dw_ref, acc_ref):
        i, kk = pl.program_id(0), pl.program_id(1)

        @pl.when((i == 0) & (kk == 0))
        def _():
            loss_ref[...] = jnp.zeros_like(loss_ref)
            dw_ref[...] = jnp.zeros_like(dw_ref)

        @pl.when(kk == 0)
        def _():
            acc_ref[...] = h_ref[...]

        acc_ref[...] += _dot(a_ref[...], b_ref[...], NN)

        @pl.when(kk == ksteps - 1)
        def _():
            h = acc_ref[...]
            wv = w_ref[...]
            r = lax.rsqrt(jnp.mean(h * h, axis=-1, keepdims=True) + NORM_EPS)
            yn = h * r
            e = yn * wv - t_ref[...]
            loss_ref[...] += 0.5 * jnp.sum(jnp.sum(e * e, axis=-1, keepdims=True), axis=0, keepdims=True) / d
            dy = e / d
            dw_ref[...] += jnp.sum(dy * yn, axis=0, keepdims=True)
            dyn = dy * wv
            dhb_ref[...] = (r * (dyn - yn * jnp.mean(dyn * yn, axis=-1, keepdims=True))).astype(BF16)

    row = pl.BlockSpec((FUSED_ROWS, d), lambda i, k: (i, 0))
    wspec = pl.BlockSpec((1, d), lambda i, k: (0, 0))
    return pl.pallas_call(
        body, name=name, grid=(t // FUSED_ROWS, ksteps),
        in_specs=[pl.BlockSpec((FUSED_ROWS, tk), lambda i, k: (i, k)), pl.BlockSpec((tk, d), lambda i, k: (k, 0)),
                  row, wspec, row],
        out_specs=[pl.BlockSpec((1, 1), lambda i, k: (0, 0)), row, wspec],
        out_shape=[jax.ShapeDtypeStruct((1, 1), F32), jax.ShapeDtypeStruct((t, d), BF16),
                   jax.ShapeDtypeStruct((1, d), F32)],
        scratch_shapes=[pltpu.VMEM((FUSED_ROWS, d), F32)],
        compiler_params=_params(("arbitrary", "arbitrary")))(act, w_ff2, h1, w, target)


def rms_bwd(h, r, w, dn, dres, out_dtype, name):
    t, d = h.shape

    def body(h_ref, r_ref, w_ref, dn_ref, dres_ref, dh_ref, dw_ref):
        @pl.when(pl.program_id(0) == 0)
        def _():
            dw_ref[...] = jnp.zeros_like(dw_ref)

        rv = r_ref[...]
        yn = h_ref[...] * rv
        dnv = dn_ref[...].astype(F32)
        dw_ref[...] += jnp.sum(dnv * yn, axis=0, keepdims=True)
        dyn = dnv * w_ref[...]
        dh = dres_ref[...].astype(F32) + rv * (dyn - yn * jnp.mean(dyn * yn, axis=-1, keepdims=True))
        dh_ref[...] = dh.astype(out_dtype)

    row = pl.BlockSpec((ROW_BLOCK, d), lambda i: (i, 0))
    wspec = pl.BlockSpec((1, d), lambda i: (0, 0))
    rspec = pl.BlockSpec((ROW_BLOCK, 1), lambda i: (i, 0))
    return pl.pallas_call(
        body, name=name, grid=(t // ROW_BLOCK,),
        in_specs=[row, rspec, wspec, row, row], out_specs=[row, wspec],
        out_shape=[jax.ShapeDtypeStruct((t, d), out_dtype), jax.ShapeDtypeStruct((1, d), F32)],
        compiler_params=_params(("arbitrary",)))(h, r, w, dn, dres)


CONV_ROWS = 512
TILE_ROWS = 8


def _iota2(shape, axis):
    return lax.broadcasted_iota(jnp.int32, shape, axis)


def _silu(x):
    return x * jax.nn.sigmoid(x)


def _conv_rows(x_ref, w, first, rows):
    acc = None
    for j in range(4):
        term = x_ref[first - 3 + j:first - 3 + j + rows, :] * w[j:j + 1, :]
        acc = term if acc is None else acc + term
    return acc


def _head_shifts(head):
    rows = _iota2((TILE_ROWS, 1), 0)
    return [jnp.where(rows >= 3 - j, head if j == 3 else pltpu.roll(head, 3 - j, 0), 0.0) for j in range(4)]


def _conv_chunks(t):
    pieces = [(TILE_ROWS, min(CONV_ROWS, t) - TILE_ROWS)]
    pieces += [(r, CONV_ROWS) for r in range(CONV_ROWS, t, CONV_ROWS)]
    return pieces


def conv_fwd(proj, conv_w, name):
    t = proj.shape[0]

    def body(x_ref, w_ref, o_ref):
        w = w_ref[...]
        shifted = _head_shifts(x_ref[0:TILE_ROWS, :])
        o_ref[0:TILE_ROWS, :] = _silu(sum(shifted[j] * w[j:j + 1, :] for j in range(4)))
        for first, rows in _conv_chunks(t):
            o_ref[first:first + rows, :] = _silu(_conv_rows(x_ref, w, first, rows))

    col = pl.BlockSpec((t, LANES), lambda c: (0, c))
    return pl.pallas_call(
        body, name=name, grid=(QKV_WIDTH // LANES,),
        in_specs=[col, pl.BlockSpec((4, LANES), lambda c: (0, c))], out_specs=col,
        out_shape=jax.ShapeDtypeStruct((t, QKV_WIDTH), F32),
        compiler_params=_params(("parallel",)))(proj, conv_w)


def conv_bwd(proj, dout, conv_w, dproj, name):
    t = proj.shape[0]

    def dsilu(pre):
        sg = jax.nn.sigmoid(pre)
        return sg * (1.0 + pre * (1.0 - sg))

    def body(x_ref, d_ref, w_ref, dproj_in, dx_ref, dw_ref, stage):
        del dproj_in
        w = w_ref[...]
        shifted = _head_shifts(x_ref[0:TILE_ROWS, :])
        head_dpre = d_ref[0:TILE_ROWS, :] * dsilu(sum(shifted[j] * w[j:j + 1, :] for j in range(4)))
        stage[0:TILE_ROWS, :] = head_dpre
        for first, rows in _conv_chunks(t):
            stage[first:first + rows, :] = d_ref[first:first + rows, :] * dsilu(_conv_rows(x_ref, w, first, rows))
        stage[t:t + TILE_ROWS, :] = jnp.zeros((TILE_ROWS, LANES), F32)
        for first, rows in [(0, TILE_ROWS)] + _conv_chunks(t):
            dx = None
            for j in range(4):
                term = stage[first + 3 - j:first + 3 - j + rows, :] * w[j:j + 1, :]
                dx = term if dx is None else dx + term
            dx_ref[first:first + rows, :] = dx.astype(BF16)
        dw = [jnp.sum(head_dpre * shifted[j], axis=0, keepdims=True) for j in range(4)]
        for first, rows in _conv_chunks(t):
            dpre = stage[first:first + rows, :]
            for j in range(4):
                dw[j] = dw[j] + jnp.sum(dpre * x_ref[first - 3 + j:first - 3 + j + rows, :], axis=0, keepdims=True)
        dw_ref[...] = jnp.concatenate(dw, axis=0)

    col = pl.BlockSpec((t, LANES), lambda c: (0, c))
    taps = pl.BlockSpec((4, LANES), lambda c: (0, c))
    return pl.pallas_call(
        body, name=name, grid=(QKV_WIDTH // LANES,),
        in_specs=[col, col, taps, ANY_SPEC], out_specs=[col, taps],
        out_shape=[jax.ShapeDtypeStruct(dproj.shape, BF16), jax.ShapeDtypeStruct((4, QKV_WIDTH), F32)],
        scratch_shapes=[pltpu.VMEM((t + TILE_ROWS, LANES), F32)],
        input_output_aliases={3: 0},
        compiler_params=_params(("parallel",)))(proj, dout, conv_w, dproj)


def _softplus(x):
    return jnp.maximum(x, 0.0) + jnp.log(1.0 + jnp.exp(-jnp.abs(x)))


def _head_norm_gate(o, norm_w, gate):
    return o * lax.rsqrt(jnp.mean(o * o, axis=-1, keepdims=True) + NORM_EPS) * norm_w * _silu(gate)


GDN_PREC = ("bf", "bf")
HGRN_PREC = "bf"


def _each(fn, *cols):
    return [fn(*a) for a in zip(*cols)]


@functools.partial(jax.custom_vjp, nondiff_argnums=(2,))
def _known_inverse(low, inv, prec):
    del low, prec
    return inv


def _known_inverse_fwd(low, inv, prec):
    del low
    return inv, inv


def _known_inverse_bwd(prec, inv, ct):
    return -_mm_raw(_mm_raw(inv, ct, TN, prec), inv, NT, prec), jnp.zeros_like(inv)


_known_inverse.defvjp(_known_inverse_fwd, _known_inverse_bwd)


def gdn_stages(hs, qc, kc, vc, zc, ab, a_log_l, dt_l, norm_w, s, prec=GDN_PREC, inv_known=None):
    p_inv, p_mm = prec
    c = CHUNK
    ri, ci = _iota2((c, c), 0), _iota2((c, c), 1)
    incl, strict, eye = ri >= ci, ri > ci, ri == ci
    lane = _iota2((c, LANES), 1)
    last_row = _iota2((c, 1), 0) == c - 1
    rowsum = lambda x: jnp.sum(x, axis=1, keepdims=True)

    def row(col):
        return jnp.sum(jnp.where(eye, col, 0.0), axis=0, keepdims=True)

    q = _each(lambda x: x * lax.rsqrt(rowsum(x * x) + L2_EPS) * (HEAD_DIM ** -0.5), qc)
    k = _each(lambda x: x * lax.rsqrt(rowsum(x * x) + L2_EPS), kc)
    yield
    a_col = [rowsum(jnp.where(lane == h, ab, 0.0)) for h in hs]
    b_col = [rowsum(jnp.where(lane == h + N_HEADS, ab, 0.0)) for h in hs]
    beta = _each(jax.nn.sigmoid, b_col)
    g = _each(lambda a, al, dl: rowsum(jnp.where(lane == 0, -jnp.exp(al) * _softplus(a + dl), 0.0)), a_col, a_log_l, dt_l)
    gcum = _each(lambda x: rowsum(jnp.where(incl, row(x), 0.0)), g)
    g_last = _each(lambda x: jnp.sum(jnp.where(last_row, x, 0.0), axis=0, keepdims=True), gcum)
    decay = _each(lambda x: jnp.exp(jnp.where(incl, x - row(x), -jnp.inf)), gcum)
    yield
    kk = _each(lambda x: mm(x, x, NT, p_mm), k)
    low = _each(lambda b, x, d: jnp.where(strict, b * x * d, 0.0), beta, kk, decay)
    yield
    if inv_known is None:
        power = _each(lambda x: -x, low)
        inv = _each(lambda x: jnp.where(eye, 1.0, 0.0) + x, power)
        for _ in range(5):
            power = _each(lambda x: mm(x, x, NN, p_inv), power)
            yield
            inv = _each(lambda x, p: x + mm(x, p, NN, p_inv), inv, power)
            yield
    else:
        inv = _each(lambda x, known: _known_inverse(x, known, p_inv), low, inv_known)
    exp_g = _each(jnp.exp, gcum)
    yield
    u_v = _each(lambda i, b, x: mm(i, b * x, NN, p_mm), inv, beta, vc)
    w = _each(lambda i, b, e, x: mm(i, b * e * x, NN, p_mm), inv, beta, exp_g, k)
    yield
    attn = _each(lambda x, y, d: mm(x, y, NT, p_mm) * d, q, k, decay)
    yield
    u = _each(lambda x, y, z: x - mm(y, z, NN, p_mm), u_v, w, s)
    yield
    o = _each(lambda x, e, z: mm(x * e, z, NN, p_mm), q, exp_g, s)
    o = _each(lambda x, a, y: x + mm(a, y, NN, p_mm), o, attn, u)
    yield
    k_end = _each(lambda x, gl, gc: x * jnp.exp(gl - gc), k, g_last, gcum)
    s_new = _each(lambda z, gl, x, y: z * jnp.exp(gl) + mm(x, y, TN, p_mm), s, g_last, k_end, u)
    return (_each(lambda x, z: _head_norm_gate(x, norm_w, z), o, zc), s_new), inv


def gdn_chunk(h, qc, kc, vc, zc, ab, a_log_l, dt_l, norm_w, s, prec=GDN_PREC, reuse_inverse=False):
    args = ([h], [qc], [kc], [vc], [zc], ab, [a_log_l], [dt_l], norm_w, [s], prec)
    if reuse_inverse:
        inv = lax.stop_gradient(gdn_chunks(*args)[1])
        (y, s_new), _ = gdn_chunks(*args, inv_known=inv)
    else:
        (y, s_new), _ = gdn_chunks(*args)
    return y[0], s_new[0]


DIAG_ROWS = SUB_CHUNK // 2
SHIFT_PAD = 8
SHIFT_ROWS = SHIFT_PAD + CHUNK + SHIFT_PAD
SHIFT_WAYS = 4


class RolledRows:
    def down(self, x, which):
        del which
        return [x] + [pltpu.roll(x, off, 0) for off in range(1, DIAG_ROWS)]

    def up_sum(self, parts, which):
        del which
        acc = parts[0]
        for off in range(1, DIAG_ROWS):
            acc = acc + pltpu.roll(parts[off], CHUNK - off, 0)
        return acc


class SlotRows:
    def __init__(self, slots):
        self.slots = slots

    def down(self, x, which):
        self.slots[which, 0, SHIFT_PAD:SHIFT_PAD + CHUNK, :] = x
        return [x] + [self.slots[which, 0, SHIFT_PAD - off:SHIFT_PAD + CHUNK - off, :] for off in range(1, DIAG_ROWS)]

    def up_sum(self, parts, which):
        acc = parts[0]
        for off in range(1, DIAG_ROWS):
            way = 1 + off % (SHIFT_WAYS - 1)
            self.slots[which, way, SHIFT_PAD:SHIFT_PAD + CHUNK, :] = parts[off]
            acc = acc + self.slots[which, way, SHIFT_PAD + off:SHIFT_PAD + CHUNK + off, :]
        return acc


def _sub_block_rows():
    return jnp.bitwise_and(_iota2((CHUNK, 1), 0), DIAG_ROWS - 1)


def _diag_forward(rows, q, key, bc, v):
    rmod = _sub_block_rows()
    k_d, b_d, v_d = rows.down(key, 0), rows.down(bc, 1), rows.down(v, 2)
    o = None
    for off in range(DIAG_ROWS):
        e = jnp.exp(jnp.where(rmod >= off, bc - b_d[off], -jnp.inf))
        term = jnp.sum(q * k_d[off] * e, axis=-1, keepdims=True) * v_d[off]
        o = term if o is None else o + term
    return o


def _diag_backward(rows, q, key, bc, v, do):
    rmod = _sub_block_rows()
    k_d, b_d, v_d = rows.down(key, 0), rows.down(bc, 1), rows.down(v, 2)
    dq = db = None
    dk_parts, db_parts, dv_parts = [], [], []
    for off in range(DIAG_ROWS):
        e = jnp.exp(jnp.where(rmod >= off, bc - b_d[off], -jnp.inf))
        qe = q * e
        a = jnp.sum(qe * k_d[off], axis=-1, keepdims=True)
        da = jnp.sum(do * v_d[off], axis=-1, keepdims=True)
        dv_parts.append(a * do)
        dq_term = (da * e) * k_d[off]
        dk_term = da * qe
        s = dk_term * k_d[off]
        dq = dq_term if dq is None else dq + dq_term
        db = s if db is None else db + s
        dk_parts.append(dk_term)
        db_parts.append(s)
    return dq, rows.up_sum(dk_parts, 0), db - rows.up_sum(db_parts, 1), rows.up_sum(dv_parts, 2)


def diag_part(rows, differentiable=True):
    forward = functools.partial(_diag_forward, rows)
    if not differentiable:
        return forward
    part = jax.custom_vjp(forward)
    part.defvjp(lambda q, key, bc, v: (forward(q, key, bc, v), (q, key, bc, v)),
                lambda res, do: _diag_backward(rows, *res, do))
    return part


def hgrn_stages(qb, fb, ib, gb, l0, l1, norm_w, st, prec=HGRN_PREC, diags=None, o_known=None):
    c = CHUNK
    ri, ci = _iota2((4 * c, c), 0), _iota2((4 * c, c), 1)
    rcol = _iota2((c, 1), 0)
    blk0 = jnp.bitwise_and(ri, c - SUB_CHUNK)
    limit = jnp.where(ri < c, ri + 1, jnp.where(ri < 2 * c, blk0, jnp.where(ri < 3 * c, blk0 + SUB_CHUNK,
                                                                          blk0 + DIAG_ROWS)))
    sel = jnp.where(ci < limit, 1.0, 0.0)
    ri, ci = _iota2((c, c), 0), _iota2((c, c), 1)
    lb = _each(lambda a, b: jax.nn.sigmoid(a - b), l0, l1)
    forget = _each(lambda b, f: b + (1.0 - b) * jax.nn.sigmoid(f), lb, fb)
    key = _each(lambda b, f: (1.0 - b) * jax.nn.sigmoid(-f), lb, fb)
    q = _each(_silu, qb)
    v = ib
    logf = _each(jnp.log, forget)
    sums = _each(lambda x: sel_sums(sel, x), logf)
    bc, b_start, b_end, b_half = ([x[i] for x in sums] for i in range(4))
    b_last = _each(lambda x: jnp.sum(x, axis=0, keepdims=True), logf)
    o = _each(lambda x, b, z: mm(x * jnp.exp(b), z, NT, prec), q, bc, st)
    if diags is None:
        diags = [diag_part(RolledRows())] * len(qb)
    yield
    o = list(o)
    for h in range(len(o)):
        o[h] = o[h] + diags[h](q[h], key[h], bc[h], v[h])
        yield
    second = jnp.bitwise_and(rcol, SUB_CHUNK - 1) >= DIAG_ROWS
    same_sub = jnp.bitwise_and(ri, c - SUB_CHUNK) == jnp.bitwise_and(ci, c - SUB_CHUNK)
    q_half = _each(lambda x, b, bh: x * jnp.exp(jnp.where(second, b - bh, -jnp.inf)), q, bc, b_half)
    k_half = _each(lambda x, b, bh: x * jnp.exp(jnp.where(second, -jnp.inf, bh - b)), key, bc, b_half)
    a_half = _each(lambda x, z: jnp.where(same_sub, mm(x, z, NT, prec), 0.0), q_half, k_half)
    o = _each(lambda acc, a, val: acc + mm(a, val, NN, prec), o, a_half, v)
    yield
    q_rel = _each(lambda x, b, bs: x * jnp.exp(b - bs), q, bc, b_start)
    k_rel = _each(lambda x, b, be: x * jnp.exp(be - b), key, bc, b_end)
    for y in range(c // SUB_CHUNK - 1):
        def scaled(x, b, bs):
            end_y = jnp.sum(jnp.where(rcol == SUB_CHUNK * y + SUB_CHUNK - 1, b, 0.0), axis=0, keepdims=True)
            return x * jnp.exp(jnp.where(rcol >= SUB_CHUNK * (y + 1), bs - end_y, -jnp.inf))
        dq = _each(scaled, q_rel, bc, b_start)
        in_y = (ci >= SUB_CHUNK * y) & (ci < SUB_CHUNK * (y + 1))
        a_y = _each(lambda x, z: jnp.where(in_y, mm(x, z, NT, prec), 0.0), dq, k_rel)
        o = _each(lambda acc, a, val: acc + mm(a, val, NN, prec), o, a_y, v)
        yield
    k_state = _each(lambda x, bl, b: x * jnp.exp(bl - b), key, b_last, bc)
    st_new = _each(lambda z, bl, val, x: z * jnp.exp(bl) + mm(val, x, TN, prec), st, b_last, v, k_state)
    if o_known is not None:
        o = _each(_known_value, o, o_known)
    return (_each(lambda x, z: _head_norm_gate(x, norm_w, z), o, gb), st_new), o


def _drain(gen):
    try:
        while True:
            next(gen)
    except StopIteration as done:
        return done.value


def _alternate(gen_a, gen_b):
    out, live = [None, None], [gen_a, gen_b]
    while any(g is not None for g in live):
        for i, g in enumerate(live):
            if g is None:
                continue
            try:
                next(g)
            except StopIteration as done:
                out[i], live[i] = done.value, None
    return out


def gdn_chunks(*args, **kwargs):
    return _drain(gdn_stages(*args, **kwargs))


def hgrn_chunks(*args, **kwargs):
    return _drain(hgrn_stages(*args, **kwargs))


def hgrn_chunk(qb, fb, ib, gb, l0, l1, norm_w, st, prec=HGRN_PREC, reuse_output=False):
    args = ([qb], [fb], [ib], [gb], [l0], [l1], norm_w, [st], prec)
    if reuse_output:
        known = lax.stop_gradient(hgrn_chunks(*args)[1])
        (y, st_new), _ = hgrn_chunks(*args, o_known=known)
    else:
        (y, st_new), _ = hgrn_chunks(*args)
    return y[0], st_new[0]


HEAD_VEC = (N_HEADS, 1, LANES)


class _ChunkSpecs:
    def __init__(self, nc, rev):
        self.nc, self.rev = nc, rev

    def _c(self, c):
        return self.nc - 1 - c if self.rev else c

    def row(self, width, block=0):
        return pl.BlockSpec((CHUNK, width), lambda c: (self._c(c), block))

    def per_head(self, rows):
        return pl.BlockSpec((None, N_HEADS, rows, rows), lambda c: (self._c(c), 0, 0, 0))

    @staticmethod
    def whole(shape):
        return pl.BlockSpec(shape, lambda c: (0,) * len(shape))


def _lanes(j):
    return slice(j * LANES, (j + 1) * LANES)


def mixer_fwd(qkv_c, proj, a_log_l, dt_l, gdn_norm_w, l0, l1, hgrn_norm_w, name):
    t = qkv_c.shape[0]
    hb = N_HEADS
    sp = _ChunkSpecs(t // CHUNK, rev=False)
    hs = list(range(hb))

    def body(q_ref, k_ref, v_ref, z_ref, ab_ref, al_ref, dt_ref, gnw_ref, qb_ref, fb_ref, ib_ref, gb_ref, l0_ref, l1_ref,
             hnw_ref, y_ref, hist_a_ref, inv_ref, hist_b_ref, o_ref, sa_ref, sb_ref, shift_ref):
        @pl.when(pl.program_id(0) == 0)
        def _():
            sa_ref[...] = jnp.zeros_like(sa_ref)
            sb_ref[...] = jnp.zeros_like(sb_ref)
            shift_ref[...] = jnp.zeros_like(shift_ref)

        heads = lambda ref: [ref[:, _lanes(j)] for j in hs]
        s_a, s_b = [sa_ref[h] for h in hs], [sb_ref[h] for h in hs]
        for h in hs:
            hist_a_ref[h] = s_a[h]
            hist_b_ref[h] = s_b[h]
        diags = [diag_part(SlotRows(shift_ref.at[h]), differentiable=False) for h in hs]
        ((y_a, s_a_new), inv), ((y_b, s_b_new), o_pre) = _alternate(
            gdn_stages(hs, heads(q_ref), heads(k_ref), heads(v_ref), heads(z_ref), ab_ref[...],
                       [al_ref[h] for h in hs], [dt_ref[h] for h in hs], gnw_ref[...], s_a),
            hgrn_stages(heads(qb_ref), heads(fb_ref), heads(ib_ref), heads(gb_ref),
                        [l0_ref[h] for h in hs], [l1_ref[h] for h in hs], hnw_ref[...], s_b, diags=diags))
        for h in hs:
            y_ref[:, _lanes(h)] = y_a[h].astype(BF16)
            y_ref[:, _lanes(hb + h)] = y_b[h].astype(BF16)
            o_ref[:, _lanes(h)] = o_pre[h]
            sa_ref[h] = s_a_new[h]
            sb_ref[h] = s_b_new[h]
            inv_ref[h] = inv[h]

    vec, gain, slab = sp.whole(HEAD_VEC), sp.whole((1, LANES)), functools.partial(sp.row, GDN_WIDTH)
    states = jax.ShapeDtypeStruct((sp.nc, N_HEADS, HEAD_DIM, HEAD_DIM), F32)
    return pl.pallas_call(
        body, name=name, grid=(sp.nc,),
        in_specs=[slab(0), slab(1), slab(2), slab(3), sp.row(LANES, AB_BLOCK), vec, vec, gain,
                  slab(4), slab(5), slab(6), slab(7), vec, vec, gain],
        out_specs=[sp.row(2 * GDN_WIDTH), sp.per_head(HEAD_DIM), sp.per_head(CHUNK), sp.per_head(HEAD_DIM), slab(0)],
        out_shape=[jax.ShapeDtypeStruct((t, 2 * GDN_WIDTH), BF16), states,
                   jax.ShapeDtypeStruct((sp.nc, N_HEADS, CHUNK, CHUNK), F32), states,
                   jax.ShapeDtypeStruct((t, GDN_WIDTH), F32)],
        scratch_shapes=[pltpu.VMEM((N_HEADS, HEAD_DIM, HEAD_DIM), F32), pltpu.VMEM((N_HEADS, HEAD_DIM, HEAD_DIM), F32),
                        pltpu.VMEM((hb, 3, SHIFT_WAYS, SHIFT_ROWS, LANES), F32)],
        compiler_params=_params(("arbitrary",)),
    )(qkv_c, qkv_c, qkv_c, proj, proj, a_log_l, dt_l, gdn_norm_w, proj, proj, proj, proj, l0, l1, hgrn_norm_w)


def mixer_bwd(qkv_c, proj, a_log_l, dt_l, gdn_norm_w, l0, l1, hgrn_norm_w, hist_a, inv_hist, hist_b, o_pre, dy, name):
    t = qkv_c.shape[0]
    hb = N_HEADS
    sp = _ChunkSpecs(t // CHUNK, rev=True)
    hs = list(range(hb))

    def body(q_ref, k_ref, v_ref, z_ref, ab_ref, al_ref, dt_ref, gnw_ref, qb_ref, fb_ref, ib_ref, gb_ref, l0_ref, l1_ref,
             hnw_ref, hist_a_ref, inv_ref, hist_b_ref, o_ref, dy_ref,
             dqkv_ref, dproj_ref, dal_ref, ddt_ref, dgnw_ref, dl0_ref, dl1_ref, dhnw_ref, dsa_ref, dsb_ref, shift_ref):
        @pl.when(pl.program_id(0) == 0)
        def _():
            for ref in (dal_ref, ddt_ref, dgnw_ref, dl0_ref, dl1_ref, dhnw_ref, dsa_ref, dsb_ref, shift_ref):
                ref[...] = jnp.zeros_like(ref)

        heads = lambda ref, first=0: [ref[:, _lanes(first + j)] for j in hs]
        diags = [diag_part(SlotRows(shift_ref.at[h])) for h in hs]
        inv_known, o_known = [inv_ref[h] for h in hs], heads(o_ref)

        def both(ga, gb):
            (ra, inv), (rb, o_pre) = _alternate(gdn_stages(hs, *ga, inv_known=inv_known),
                                                hgrn_stages(*gb, diags=diags, o_known=o_known))
            return (ra, rb), (inv, o_pre)

        ga = (heads(q_ref), heads(k_ref), heads(v_ref), heads(z_ref), ab_ref[...], [al_ref[h] for h in hs],
              [dt_ref[h] for h in hs], gnw_ref[...], [hist_a_ref[h] for h in hs])
        gb = (heads(qb_ref), heads(fb_ref), heads(ib_ref), heads(gb_ref), [l0_ref[h] for h in hs],
              [l1_ref[h] for h in hs], hnw_ref[...], [hist_b_ref[h] for h in hs])
        _, vjp, _ = jax.vjp(both, ga, gb, has_aux=True)
        dy_a = [x.astype(F32) for x in heads(dy_ref)]
        dy_b = [x.astype(F32) for x in heads(dy_ref, hb)]
        (dq, dk, dv, dz, dab, dal, ddt, dgnw, ds_a), (dqb, dfb, dib, dgb, dl0, dl1, dhnw, ds_b) = vjp(
            ((dy_a, [dsa_ref[h] for h in hs]), (dy_b, [dsb_ref[h] for h in hs])))
        for h in hs:
            dqkv_ref[:, _lanes(h)] = dq[h]
            dqkv_ref[:, _lanes(hb + h)] = dk[h]
            dqkv_ref[:, _lanes(2 * hb + h)] = dv[h]
            for slab, val in enumerate((dz, dqb, dfb, dib, dgb)):
                dproj_ref[:, _lanes((3 + slab) * hb + h)] = val[h].astype(BF16)
            dal_ref[h] += dal[h]
            ddt_ref[h] += ddt[h]
            dl0_ref[h] += dl0[h]
            dl1_ref[h] += dl1[h]
            dsa_ref[h] = ds_a[h]
            dsb_ref[h] = ds_b[h]
        dproj_ref[:, MAIN_WIDTH:] = dab.astype(BF16)
        dgnw_ref[...] += dgnw
        dhnw_ref[...] += dhnw

    vec, gain, slab = sp.whole(HEAD_VEC), sp.whole((1, LANES)), functools.partial(sp.row, GDN_WIDTH)
    vec_shape, gain_shape = jax.ShapeDtypeStruct(HEAD_VEC, F32), jax.ShapeDtypeStruct((1, LANES), F32)
    return pl.pallas_call(
        body, name=name, grid=(sp.nc,),
        in_specs=[slab(0), slab(1), slab(2), slab(3), sp.row(LANES, AB_BLOCK), vec, vec, gain,
                  slab(4), slab(5), slab(6), slab(7), vec, vec, gain,
                  sp.per_head(HEAD_DIM), sp.per_head(CHUNK), sp.per_head(HEAD_DIM), slab(0), sp.row(2 * GDN_WIDTH)],
        out_specs=[sp.row(QKV_WIDTH), sp.row(CAT_WIDTH), vec, vec, gain, vec, vec, gain],
        out_shape=[jax.ShapeDtypeStruct((t, QKV_WIDTH), F32), jax.ShapeDtypeStruct((t, CAT_WIDTH), BF16),
                   vec_shape, vec_shape, gain_shape, vec_shape, vec_shape, gain_shape],
        scratch_shapes=[pltpu.VMEM((N_HEADS, HEAD_DIM, HEAD_DIM), F32), pltpu.VMEM((N_HEADS, HEAD_DIM, HEAD_DIM), F32),
                        pltpu.VMEM((hb, 3, SHIFT_WAYS, SHIFT_ROWS, LANES), F32)],
        compiler_params=_params(("arbitrary",)),
    )(qkv_c, qkv_c, qkv_c, proj, proj, a_log_l, dt_l, gdn_norm_w, proj, proj, proj, proj, l0, l1, hgrn_norm_w,
      hist_a, inv_hist, hist_b, o_pre, dy)


def _adamw(w, g, m, v):
    m = ADAM_B1 * m + (1.0 - ADAM_B1) * g
    v = ADAM_B2 * v + (1.0 - ADAM_B2) * jnp.square(g)
    m_hat = m / (1.0 - ADAM_B1 ** ADAM_STEP)
    v_hat = v / (1.0 - ADAM_B2 ** ADAM_STEP)
    delta = -ADAM_LR * (m_hat / (jnp.sqrt(v_hat) + ADAM_EPS) + ADAM_WD * w)
    return delta, m, v


def adamw_reduce(parts, mine, slot, w, m, v, name, rb=128):
    r, c = w.shape
    rb = min(rb, r)
    n_parts = parts.shape[0]

    def body(slot_ref, p_ref, own_ref, w_ref, m_ref, v_ref, g_ref, d_ref, mo_ref, vo_ref):
        part = lambda d: jnp.where(slot_ref[0] == d, own_ref[...], p_ref[d]).astype(F32)
        g = part(0)
        for d in range(1, n_parts):
            g = g + part(d)
        delta, mn, vn = _adamw(w_ref[...], g, m_ref[...], v_ref[...])
        g_ref[...] = g
        d_ref[...] = delta
        mo_ref[...] = mn
        vo_ref[...] = vn

    blk = pl.BlockSpec((rb, c), lambda i, s: (i, 0))
    return pl.pallas_call(
        body, name=name,
        grid_spec=pltpu.PrefetchScalarGridSpec(
            num_scalar_prefetch=1, grid=(r // rb,),
            in_specs=[pl.BlockSpec((n_parts, rb, c), lambda i, s: (0, i, 0)),
                      pl.BlockSpec((None, rb, c), lambda i, s: (s[0], i, 0)), blk, blk, blk],
            out_specs=[blk] * 4),
        out_shape=[jax.ShapeDtypeStruct((r, c), F32)] * 4,
        compiler_params=_params(("parallel",)))(slot.astype(jnp.int32).reshape(1), parts, mine, w, m, v)


def adamw_small(w, g, m, v, name):
    def body(w_ref, g_ref, m_ref, v_ref, d_ref, mo_ref, vo_ref):
        delta, mn, vn = _adamw(w_ref[...], g_ref[...], m_ref[...], v_ref[...])
        d_ref[...] = delta
        mo_ref[...] = mn
        vo_ref[...] = vn

    vmem = pl.BlockSpec(memory_space=pltpu.VMEM)
    return pl.pallas_call(body, name=name, in_specs=[vmem] * 4, out_specs=[vmem] * 3,
                          out_shape=[jax.ShapeDtypeStruct(w.shape, F32)] * 3)(w, g, m, v)


def _pack(arrays):
    flat = jnp.concatenate([a.reshape(-1).astype(F32) for a in arrays])
    rows = -(-flat.shape[0] // (8 * LANES)) * 8
    return jnp.pad(flat, (0, rows * LANES - flat.shape[0])).reshape(rows, LANES)


def _unpack(packed, shapes):
    flat, out, off = packed.reshape(-1), [], 0
    for s in shapes:
        n = 1
        for d in s:
            n *= d
        out.append(flat[off:off + n].reshape(s))
        off += n
    return out


def _relu2_epilogue(acc, _):
    r = jnp.maximum(acc, 0.0)
    return acc, r * r


def _relu2_bwd_epilogue(acc, a1):
    return (acc * (2.0 * jnp.maximum(a1, 0.0)),)


def kernel(x, w_in, conv_w, gdn_a_log, gdn_dt_bias, gdn_norm_w, hgrn_lb_logits, hgrn_norm_w, w_out, norm_mix_w, norm_ffn_w, w_ff1, w_ff2, norm_final_w, loss_target, m_w_in, m_conv_w, m_gdn_a_log, m_gdn_dt_bias, m_gdn_norm_w, m_hgrn_lb_logits, m_hgrn_norm_w, m_w_out, m_norm_mix_w, m_norm_ffn_w, m_w_ff1, m_w_ff2, m_norm_final_w, v_w_in, v_conv_w, v_gdn_a_log, v_gdn_dt_bias, v_gdn_norm_w, v_hgrn_lb_logits, v_hgrn_norm_w, v_w_out, v_norm_mix_w, v_norm_ffn_w, v_w_ff1, v_w_ff2, v_norm_final_w):
    me = _my_flat()
    xs = x[0]
    target = loss_target[0]
    shard_in = w_in.shape[2]
    shard_conv = conv_w.shape[2]

    tok = lambda t: t[0:1, 0:1]

    half = D_MODEL // 2
    w_in_b = w_in[0].astype(BF16)
    h_ga, t_ga = exchange_start([w_in_b[:half], conv_w[0]], True, "gather_w_in_high_start", peers=CHIP_PEERS)
    h_g0, t_g0 = exchange_start([w_in_b[half:]], True, "gather_w_in_low_start", after=[t_ga], peers=CHIP_PEERS)
    behind = lambda a: lax.optimization_barrier((a, t_g0))[0]
    h_g1, t_g1 = exchange_start([behind(w_out[0]).astype(BF16), behind(w_ff1[0]).astype(BF16)], True,
                                "gather_mid_start", after=[t_g0], peers=CHIP_PEERS)
    h_g2, t_g2 = exchange_start([behind(w_ff2[0]).astype(BF16)], True, "gather_ff2_start", after=[t_g1],
                                peers=CHIP_PEERS)
    m_in, v_in, _ = lax.optimization_barrier((m_w_in, v_w_in, t_g2))
    m_in, v_in = m_in[0], v_in[0]

    lane_b = lambda p: jnp.broadcast_to(p.reshape(N_HEADS, 1, 1), HEAD_VEC)
    a_log_l, dt_l = lane_b(gdn_a_log[0]), lane_b(gdn_dt_bias[0])
    l0 = hgrn_lb_logits[0].reshape(HEAD_VEC)
    l1 = hgrn_lb_logits[1].reshape(HEAD_VEC)

    n1, r1 = rms_fwd(xs, norm_mix_w + tok(t_g1) + tok(t_g2), "rms_mix")
    (s_high, s_conv), (l_high, l_conv) = exchange_wait(h_ga, "gather_w_in_high_wait", after=[n1, m_in, v_in],
                                                       copies=len(CHIP_PEERS))
    h_fa, _ = forward_start([l_high, l_conv], "gather_w_in_high_forward")
    _, (l_high, l_conv) = exchange_wait(h_fa, "forward_w_in_high_wait", copies=len(OTHER_CHIPS))
    w_cat = weights_to_cat(_own_slot(l_high, s_high), "weights_to_cat", D_MODEL)
    conv_full = jnp.transpose(_own_slot(l_conv, s_conv), (1, 0, 2)).reshape(4, QKV_WIDTH)
    proj = matmul(n1, w_cat, "nn", "in_proj_high", (BF16,), tn=CAT_WIDTH // 5, tk=half, k_blocks=(0, 1))
    (s_low,), (l_low,) = exchange_wait(h_g0, "gather_w_in_low_wait", after=[proj], copies=len(CHIP_PEERS))
    h_f0, _ = forward_start([l_low], "gather_w_in_low_forward")
    _, (l_low,) = exchange_wait(_one(h_f0, 0), "forward_w_in_low_wait", copies=len(OTHER_CHIPS))
    w_cat = weights_to_cat(_own_slot(l_low, s_low), "weights_to_cat_low", D_MODEL, row0=half, into=w_cat)
    proj = matmul(n1, w_cat, "nn", "in_proj_low", tn=CAT_WIDTH // 5, tk=half, k_blocks=(1, 1), extra=proj,
                  epilogue=lambda acc, high: (acc + high,))
    qkv_c = conv_fwd(proj, conv_full, "conv_fwd")
    y, hist_a, inv_a, hist_b, o_b = mixer_fwd(qkv_c, proj, a_log_l, dt_l, gdn_norm_w, l0, l1, hgrn_norm_w, "mixer_fwd")
    (s_out, s_ff1), (l_out, l_ff1) = exchange_wait(h_g1, "gather_mid_wait", after=[y], copies=len(CHIP_PEERS))
    (s_ff2,), (l_ff2,) = exchange_wait(h_g2, "gather_ff2_wait", after=[y], copies=len(CHIP_PEERS))
    h_fw, _ = forward_start([l_out, l_ff1, l_ff2], "gather_forward_start")
    _, (l_out,) = exchange_wait(_one(h_fw, 0), "forward_out_wait", copies=len(OTHER_CHIPS))
    w_out_full = _own_slot(l_out, s_out).reshape(D_MODEL, D_MODEL)
    h1, n2, r2 = out_proj_rms(y, w_out_full, xs, norm_ffn_w, "out_proj_rms")
    _, (l_ff1,) = exchange_wait(_one(h_fw, 1), "forward_ff1_wait", after=[n2], copies=len(OTHER_CHIPS))
    w_ff1_sh = _own_slot(l_ff1, s_ff1)
    a1, act = matmul(n2, w_ff1_sh, "nn", "ff1", out_dtypes=(F32, BF16), epilogue=_relu2_epilogue, b_shards=True)
    _, (l_ff2,) = exchange_wait(_one(h_fw, 2), "forward_ff2_wait", after=[act], copies=len(OTHER_CHIPS))
    w_ff2_full = _own_slot(l_ff2, s_ff2).reshape(D_FF, D_MODEL)
    loss_sum, dh2_b, d_final = ff2_loss(act, w_ff2_full, h1, norm_final_w.reshape(1, D_MODEL), target, "ff2_loss")

    da1 = matmul(dh2_b, w_ff2_full, "nt", "d_act", out_dtypes=(BF16,), epilogue=_relu2_bwd_epilogue, extra=a1)
    t_all = xs.shape[0]
    dw_ff2 = matmul(act, dh2_b, "tn", "dw_ff2", out_dtypes=(BF16,), tk=t_all)
    p_ff2 = dw_ff2.reshape(N_DEV, D_FF // N_DEV, D_MODEL)
    h_s1, t_s1 = exchange_start([p_ff2], False, "scatter_ff2_start")
    dn2 = matmul(da1, w_ff1_sh, "nt", "d_n2", out_dtypes=(BF16,), after=[t_s1], b_shards=True, k_group=4)
    p_ff1 = matmul(n2, da1, "tn", "dw_ff1", out_dtypes=(BF16,), tn=D_FF // N_DEV, tk=t_all, after=[t_s1], out_shards=True)
    h_s2, t_s2 = exchange_start([p_ff1], False, "scatter_ff1_start")
    dh1_b, d_ffn = rms_bwd(h1, r2, norm_ffn_w + tok(t_s2), dn2, dh2_b, BF16, "rms_ffn_bwd")
    dmix = matmul(dh1_b, w_out_full, "nt", "d_mix", out_dtypes=(BF16,))
    dw_out = matmul(y, dh1_b, "tn", "dw_out", out_dtypes=(BF16,), tk=t_all)
    p_out = dw_out.reshape(N_DEV, D_MODEL // N_DEV, D_MODEL)
    h_s3, t_s3 = exchange_start([p_out], False, "scatter_out_start")
    d_qkv_c, dproj, d_alog_l, d_dt_l, d_gnw, dl0, dl1, d_hnw = mixer_bwd(
        qkv_c, proj, a_log_l, dt_l, gdn_norm_w + tok(t_s3), l0, l1, hgrn_norm_w, hist_a, inv_a, hist_b, o_b, dmix,
        "mixer_bwd")
    dproj, d_conv_full = conv_bwd(proj, d_qkv_c, conv_full, dproj, "conv_bwd")
    dw_cat = matmul(n1, dproj, "tn", "dw_in", out_dtypes=(BF16,), tm=512, tn=CAT_WIDTH // 5, tk=t_all)
    p_in = cat_to_shards(dw_cat, shard_in)
    h_pair, t_s4 = routed_start(p_in, _to_sibling_routes, "scatter_in_pair_start")

    (s_ff2g,), (r_ff2,) = exchange_wait(h_s1, "scatter_ff2_wait", after=[t_s4])
    (s_ff1g,), (r_ff1,) = exchange_wait(h_s2, "scatter_ff1_wait", after=[t_s4])
    (s_outg,), (r_out,) = exchange_wait(h_s3, "scatter_out_wait", after=[t_s4])
    g_w_ff2, d_w_ff2, nm_w_ff2, nv_w_ff2 = adamw_reduce(
        r_ff2, s_ff2g, me, w_ff2[0], m_w_ff2[0], v_w_ff2[0], "adamw_w_ff2")
    g_w_ff1, d_w_ff1, nm_w_ff1, nv_w_ff1 = adamw_reduce(
        r_ff1, s_ff1g, me, w_ff1[0], m_w_ff1[0], v_w_ff1[0], "adamw_w_ff1")
    g_w_out, d_w_out, nm_w_out, nv_w_out = adamw_reduce(
        r_out, s_outg, me, w_out[0], m_w_out[0], v_w_out[0], "adamw_w_out")
    (p_in,), (from_sibling,) = exchange_wait(h_pair, "scatter_in_pair_wait", after=[d_w_ff2, d_w_ff1, d_w_out],
                                             copies=N_CHIPS)
    chip_sums = pair_sum(p_in, from_sibling, "scatter_in_pair_sum")
    h_chips, t_s5 = routed_start(chip_sums, _to_chips_routes, "scatter_in_chips_start")
    dn1 = matmul(dproj, w_cat, "nt", "d_n1", out_dtypes=(BF16,), tm=512, tn=512, tk=CAT_WIDTH, after=[t_s5])
    dx, d_mix = rms_bwd(xs, r1, norm_mix_w, dn1, dh1_b, F32, "rms_mix_bwd")
    (chip_sums,), (r_in,) = exchange_wait(h_chips, "scatter_in_chips_wait", after=[dx], copies=len(OTHER_CHIPS))
    g_w_in, d_w_in, nm_w_in, nv_w_in = adamw_reduce(
        r_in, chip_sums, me // 2, w_in[0], m_in, v_in, "adamw_w_in")

    d_lb = jnp.stack([dl0.reshape(GDN_WIDTH), dl1.reshape(GDN_WIDTH)])
    small_shapes = [(1, N_HEADS), (1, N_HEADS), (1, HEAD_DIM), (2, GDN_WIDTH), (1, HEAD_DIM), (1, D_MODEL),
                    (1, D_MODEL), (D_MODEL,), (4, QKV_WIDTH), ()]
    small = _pack([d_alog_l[:, 0, 0], d_dt_l[:, 0, 0], d_gnw, d_lb, d_hnw, d_mix, d_ffn, d_final, d_conv_full,
                   loss_sum[0, 0]])
    red = allreduce_small(small, "allreduce_small")
    g_alog, g_dt, g_gnw, g_lb, g_hnw, g_mix, g_ffn, g_final, g_conv_full, loss = _unpack(red, small_shapes)
    g_conv = lax.dynamic_slice(g_conv_full, (0, me * shard_conv), (4, shard_conv)).reshape(1, 4, shard_conv)
    small_g = [g_alog, g_dt, g_gnw, g_lb, g_hnw, g_mix, g_ffn, g_final, g_conv]
    small_w = [gdn_a_log, gdn_dt_bias, gdn_norm_w, hgrn_lb_logits, hgrn_norm_w, norm_mix_w, norm_ffn_w, norm_final_w, conv_w]
    small_m = [m_gdn_a_log, m_gdn_dt_bias, m_gdn_norm_w, m_hgrn_lb_logits, m_hgrn_norm_w, m_norm_mix_w, m_norm_ffn_w,
               m_norm_final_w, m_conv_w]
    small_v = [v_gdn_a_log, v_gdn_dt_bias, v_gdn_norm_w, v_hgrn_lb_logits, v_hgrn_norm_w, v_norm_mix_w, v_norm_ffn_w,
               v_norm_final_w, v_conv_w]
    shapes = [a.shape for a in small_w]
    d_s, m_s, v_s = adamw_small(_pack(small_w), _pack(small_g), _pack(small_m), _pack(small_v), "adamw_small")
    d_alog, d_dt, d_gn, d_lbl, d_hn, d_nm, d_nf, d_nfin, d_cw = _unpack(d_s, shapes)
    m_alog, m_dt, m_gn, m_lbl, m_hn, m_nm, m_nf, m_nfin, m_cw = _unpack(m_s, shapes)
    v_alog, v_dt, v_gn, v_lbl, v_hn, v_nm, v_nf, v_nfin, v_cw = _unpack(v_s, shapes)

    lead = lambda a: a[None]
    grads = [lead(g_w_in), g_conv, g_alog, g_dt, g_gnw, g_lb, g_hnw, lead(g_w_out), g_mix, g_ffn,
             lead(g_w_ff1), lead(g_w_ff2), g_final]
    deltas = [lead(d_w_in), d_cw, d_alog, d_dt, d_gn, d_lbl, d_hn, lead(d_w_out), d_nm, d_nf,
              lead(d_w_ff1), lead(d_w_ff2), d_nfin]
    new_m = [lead(nm_w_in), m_cw, m_alog, m_dt, m_gn, m_lbl, m_hn, lead(nm_w_out), m_nm, m_nf,
             lead(nm_w_ff1), lead(nm_w_ff2), m_nfin]
    new_v = [lead(nv_w_in), v_cw, v_alog, v_dt, v_gn, v_lbl, v_hn, lead(nv_w_out), v_nm, v_nf,
             lead(nv_w_ff1), lead(nv_w_ff2), v_nfin]
    return (loss, dx[None], *grads, *deltas, *new_m, *new_v)
```

```python
import functools

import jax
import jax.numpy as jnp
from jax import lax
from jax.experimental import pallas as pl
from jax.experimental.pallas import tpu as pltpu

F32 = jnp.float32
BF16 = jnp.bfloat16
HI = lax.Precision.HIGHEST

N_DEV = 8
D_MODEL = 2048
CHUNK = 64
SUB_CHUNK = 16
HEAD_DIM = 128
N_HEADS = 8
GDN_WIDTH = N_HEADS * HEAD_DIM
D_FF = 4 * D_MODEL
QKV_WIDTH = 3 * GDN_WIDTH
MAIN_WIDTH = 8 * GDN_WIDTH
CAT_WIDTH = MAIN_WIDTH + 128
AB_BLOCK = MAIN_WIDTH // 128
NORM_EPS = 1e-6
L2_EPS = 1e-6
LANES = 128
VMEM_LIMIT = 56 * 1024 * 1024

ADAM_LR = 0.001
ADAM_B1 = 0.9
ADAM_B2 = 0.999
ADAM_EPS = 1e-08
ADAM_WD = 0.01
ADAM_STEP = 10

MESH = pl.DeviceIdType.MESH


def _params(sem=None):
    return pltpu.CompilerParams(dimension_semantics=sem, vmem_limit_bytes=VMEM_LIMIT)


def _dot(a, b, dims, prec=None):
    return lax.dot_general(a, b, (dims, ((), ())), precision=prec, preferred_element_type=F32)


NN = ((1,), (0,))
NT = ((1,), (1,))
TN = ((0,), (0,))


def _split_bf16(x, pieces):
    out = []
    for _ in range(pieces - 1):
        p = x.astype(BF16)
        out.append(p)
        x = x - p.astype(F32)
    out.append(x.astype(BF16))
    return out


def _mm_raw(a, b, dims, prec):
    if prec == "hi":
        return _dot(a, b, dims, HI)
    if prec == "bf":
        return _dot(a.astype(BF16), b.astype(BF16), dims)
    a_hi, a_lo = _split_bf16(a, 2)
    b_hi, b_lo = _split_bf16(b, 2)
    return _dot(a_hi, b_hi, dims) + (_dot(a_hi, b_lo, dims) + _dot(a_lo, b_hi, dims))


@functools.partial(jax.custom_vjp, nondiff_argnums=(2, 3))
def mm(a, b, dims, prec):
    return _mm_raw(a, b, dims, prec)


def _mm_fwd(a, b, dims, prec):
    return _mm_raw(a, b, dims, prec), (a, b)


def _mm_bwd(dims, prec, res, ct):
    a, b = res
    if dims == NN:
        return _mm_raw(ct, b, NT, prec), _mm_raw(a, ct, TN, prec)
    if dims == NT:
        return _mm_raw(ct, b, NN, prec), _mm_raw(ct, a, TN, prec)
    return _mm_raw(b, ct, NT, prec), _mm_raw(a, ct, NN, prec)


mm.defvjp(_mm_fwd, _mm_bwd)


def _sel_raw(sel, x, dims):
    sel = sel.astype(BF16)
    p0, p1, p2 = _split_bf16(x, 3)
    return _dot(sel, p0, dims) + (_dot(sel, p1, dims) + _dot(sel, p2, dims))


def _sel_parts(sel, x):
    c = x.shape[0]
    full = _sel_raw(sel, x, NN)
    return tuple(full[i * c:(i + 1) * c] for i in range(sel.shape[0] // c))


@jax.custom_vjp
def sel_sums(sel, x):
    return _sel_parts(sel, x)


def _sel_fwd(sel, x):
    return _sel_parts(sel, x), sel


def _sel_bwd(sel, cts):
    return jnp.zeros_like(sel), _sel_raw(sel, jnp.concatenate(cts, axis=0), TN)


sel_sums.defvjp(_sel_fwd, _sel_bwd)


@jax.custom_vjp
def _known_value(computed, known):
    del computed
    return known


_known_value.defvjp(lambda computed, known: (known, None), lambda _, ct: (ct, jnp.zeros_like(ct)))


def _my_flat():
    return 4 * lax.axis_index("x") + 2 * lax.axis_index("y") + lax.axis_index("c")


def _peer(k):
    x, y, c = lax.axis_index("x"), lax.axis_index("y"), lax.axis_index("c")
    kx, ky, kc = (k >> 2) & 1, (k >> 1) & 1, k & 1
    px = (1 - x) if kx else x
    py = (1 - y) if ky else y
    pc = (1 - c) if kc else c
    return (px, py, pc), 4 * px + 2 * py + pc


HBM_SPEC = pl.BlockSpec(memory_space=pltpu.HBM)
SEM_SPEC = pl.BlockSpec(memory_space=pltpu.SEMAPHORE)
ANY_SPEC = pl.BlockSpec(memory_space=pl.ANY)
DATAFLOW = pltpu.SideEffectType.DATAFLOW_SIDE_EFFECTING


def _in_hbm(x):
    return pltpu.with_memory_space_constraint(x, pltpu.HBM)


ALL_PEERS = tuple(range(1, N_DEV))
CHIP_PEERS = (1, 2, 4, 6)
OTHER_CHIPS = (2, 4, 6)


def exchange_start(xs, gather, name, after=(), peers=ALL_PEERS):
    n, n_after = len(xs), len(after)

    def body(*refs):
        x_refs, land_refs = refs[:n], refs[n:2 * n]
        sems = refs[2 * n + n_after:2 * n + n_after + 2 * n]
        token = refs[-1]
        me = _my_flat()
        for k in peers:
            peer, peer_flat = _peer(k)
            for a in range(n):
                src = x_refs[a] if gather else x_refs[a].at[peer_flat]
                pltpu.make_async_remote_copy(src_ref=src, dst_ref=land_refs[a].at[me], send_sem=sems[a],
                                             recv_sem=sems[n + a], device_id=peer, device_id_type=MESH).start()
        token[...] = jnp.zeros_like(token)

    lands =[_in_hbm(lax.empty(((N_DEV,) + x.shape) if gather else x.shape, x.dtype)) for x in xs]
    hbm_out = [pltpu.HBM(x.shape, x.dtype) for x in xs] + [pltpu.HBM(l.shape, l.dtype) for l in lands]
    res = pl.pallas_call(
        body, name=name,
        out_shape=(*([pltpu.SemaphoreType.DMA(())] * (2 * n)), *hbm_out, jax.ShapeDtypeStruct((8, LANES), F32)),
        in_specs=[HBM_SPEC] * (2 * n) + [ANY_SPEC] * n_after,
        out_specs=(*([SEM_SPEC] * (2 * n)), *([HBM_SPEC] * (2 * n)), pl.BlockSpec(memory_space=pltpu.VMEM)),
        input_output_aliases={i: 2 * n + i for i in range(2 * n)},
        compiler_params=pltpu.CompilerParams(has_side_effects=DATAFLOW),
    )(*[_in_hbm(x) for x in xs], *lands, *after)
    return (list(res[:2 * n]), list(res[2 * n:3 * n]), list(res[3 * n:4 * n])), res[-1]


def forward_start(lands, name, after=()):
    n, n_after = len(lands), len(after)

    def body(*refs):
        land_refs = refs[:n]
        sems = refs[n + n_after:n + n_after + 2 * n]
        token = refs[-1]
        sibling, _ = _peer(1)
        for a in range(n):
            for k in OTHER_CHIPS:
                _, from_flat = _peer(k)
                slot = land_refs[a].at[from_flat]
                pltpu.make_async_remote_copy(src_ref=slot, dst_ref=slot, send_sem=sems[a], recv_sem=sems[n + a],
                                             device_id=sibling, device_id_type=MESH).start()
        token[...] = jnp.zeros_like(token)

    res = pl.pallas_call(
        body, name=name,
        out_shape=(*([pltpu.SemaphoreType.DMA(())] * (2 * n)), *[pltpu.HBM(l.shape, l.dtype) for l in lands],
                   jax.ShapeDtypeStruct((8, LANES), F32)),
        in_specs=[HBM_SPEC] * n + [ANY_SPEC] * n_after,
        out_specs=(*([SEM_SPEC] * (2 * n)), *([HBM_SPEC] * n), pl.BlockSpec(memory_space=pltpu.VMEM)),
        input_output_aliases={i: 2 * n + i for i in range(n)},
        compiler_params=pltpu.CompilerParams(has_side_effects=DATAFLOW),
    )(*lands, *after)
    return (list(res[:2 * n]), [], list(res[2 * n:3 * n])), res[-1]


def exchange_wait(handle, name, after=(), copies=N_DEV - 1):
    sems, xs, lands = handle
    n, n_x, n_after = len(lands), len(xs), len(after)

    def body(*refs):
        land_refs = refs[n_x:n_x + n]
        sem_refs = refs[n_x + n:n_x + 3 * n]
        for a in range(n):
            every = land_refs[a].at[pl.ds(0, copies)]
            cp = pltpu.make_async_remote_copy(src_ref=every, dst_ref=every, send_sem=sem_refs[a],
                                              recv_sem=sem_refs[n + a], device_id=_peer(1)[0], device_id_type=MESH)
            cp.wait_send()
            cp.wait_recv()

    res = pl.pallas_call(
        body, name=name,
        out_shape=[pltpu.HBM(x.shape, x.dtype) for x in xs] + [pltpu.HBM(l.shape, l.dtype) for l in lands],
        in_specs=[HBM_SPEC] * (n_x + n) + [SEM_SPEC] * (2 * n) + [ANY_SPEC] * n_after,
        out_specs=[HBM_SPEC] * (n_x + n),
        input_output_aliases={i: i for i in range(n_x + n)},
        compiler_params=pltpu.CompilerParams(has_side_effects=DATAFLOW),
    )(*xs, *lands, *sems, *after)
    return list(res[:n_x]), list(res[n_x:])


N_CHIPS = N_DEV // 2


def routed_start(x, routes, name, after=()):
    n_after = len(after)

    def body(*refs):
        x_ref, land_ref = refs[0], refs[1]
        send_sem, recv_sem = refs[2 + n_after], refs[3 + n_after]
        token = refs[-1]
        for src, dst, peer in routes():
            pltpu.make_async_remote_copy(src_ref=x_ref.at[src], dst_ref=land_ref.at[dst], send_sem=send_sem,
                                         recv_sem=recv_sem, device_id=peer, device_id_type=MESH).start()
        token[...] = jnp.zeros_like(token)

    land = _in_hbm(lax.empty((N_CHIPS,) + x.shape[1:], x.dtype))
    res = pl.pallas_call(
        body, name=name,
        out_shape=(pltpu.SemaphoreType.DMA(()), pltpu.SemaphoreType.DMA(()), pltpu.HBM(x.shape, x.dtype),
                   pltpu.HBM(land.shape, land.dtype), jax.ShapeDtypeStruct((8, LANES), F32)),
        in_specs=[HBM_SPEC, HBM_SPEC] + [ANY_SPEC] * n_after,
        out_specs=(SEM_SPEC, SEM_SPEC, HBM_SPEC, HBM_SPEC, pl.BlockSpec(memory_space=pltpu.VMEM)),
        input_output_aliases={0: 2, 1: 3},
        compiler_params=pltpu.CompilerParams(has_side_effects=DATAFLOW),
    )(_in_hbm(x), land, *after)
    return ([res[0], res[1]], [res[2]], [res[3]]), res[-1]


def _to_sibling_routes():
    c = lax.axis_index("c")
    sibling, _ = _peer(1)
    return [(2 * chip + 1 - c, chip, sibling) for chip in range(N_CHIPS)]


def _to_chips_routes():
    my_chip = _my_flat() // 2
    routes = []
    for k in OTHER_CHIPS:
        peer, peer_flat = _peer(k)
        routes.append((peer_flat // 2, my_chip, peer))
    return routes


def pair_sum(p, from_sibling, name, rb=1024):
    _, r, c = p.shape
    mine = lax.axis_index("c").astype(jnp.int32).reshape(1)

    def body(kind_ref, p_ref, s_ref, o_ref):
        del kind_ref
        o_ref[...] = (p_ref[...].astype(F32) + s_ref[...].astype(F32)).astype(BF16)

    return pl.pallas_call(
        body, name=name,
        grid_spec=pltpu.PrefetchScalarGridSpec(
            num_scalar_prefetch=1, grid=(N_CHIPS, r // rb),
            in_specs=[pl.BlockSpec((None, None, rb, c), lambda chip, i, kind: (chip, kind[0], i, 0)),
                      pl.BlockSpec((None, rb, c), lambda chip, i, kind: (chip, i, 0))],
            out_specs=pl.BlockSpec((None, rb, c), lambda chip, i, kind: (chip, i, 0))),
        out_shape=jax.ShapeDtypeStruct((N_CHIPS, r, c), BF16),
        compiler_params=_params(("parallel", "parallel")))(mine, p.reshape(N_CHIPS, 2, r, c), from_sibling)


def _one(handle, a):
    sems, xs, lands = handle
    n = len(lands)
    return [sems[a], sems[n + a]], xs[a:a + 1], [lands[a]]


def _own_slot(land, block):
    return lax.dynamic_update_slice(land, block[None], (_my_flat(),) + (0,) * block.ndim)


def allreduce_small(x, name):
    rows = x.shape[0]

    def body(x_ref, o_ref, buf, send_sems, recv_sems):
        me = _my_flat()
        buf[me] = x_ref[...]
        sends = []
        for k in range(1, N_DEV):
            peer, _ = _peer(k)
            cp = pltpu.make_async_remote_copy(
                src_ref=x_ref, dst_ref=buf.at[me], send_sem=send_sems.at[k], recv_sem=recv_sems.at[k],
                device_id=peer, device_id_type=MESH)
            cp.start()
            sends.append(cp)
        for k in range(1, N_DEV):
            peer, peer_flat = _peer(k)
            pltpu.make_async_remote_copy(
                src_ref=x_ref, dst_ref=buf.at[peer_flat], send_sem=send_sems.at[k], recv_sem=recv_sems.at[k],
                device_id=peer, device_id_type=MESH).wait_recv()
        for cp in sends:
            cp.wait_send()
        acc = buf[0]
        for d in range(1, N_DEV):
            acc = acc + buf[d]
        o_ref[...] = acc

    vmem = pl.BlockSpec(memory_space=pltpu.VMEM)
    return pl.pallas_call(
        body, name=name, out_shape=jax.ShapeDtypeStruct((rows, LANES), F32),
        in_specs=[vmem], out_specs=vmem,
        scratch_shapes=[pltpu.VMEM((N_DEV, rows, LANES), F32),
                        pltpu.SemaphoreType.DMA((N_DEV,)), pltpu.SemaphoreType.DMA((N_DEV,))],
    )(x)


def matmul(a, b, mode, name, out_dtypes=(F32,), epilogue=None, extra=None, tm=1024, tn=1024, tk=2048, after=(),
           b_shards=False, out_shards=False, k_group=1, k_blocks=None):
    if b_shards:
        n_sh, b_rows, b_cols = b.shape
    if mode == "nn":
        (m, kd), n = a.shape, (n_sh * b_cols if b_shards else b.shape[1])
        if b_shards:
            tn = b_cols
    elif mode == "nt":
        (m, kd), n = a.shape, (b_rows if b_shards else b.shape[0])
        if b_shards:
            tk = k_group * b_cols
    else:
        (kd, m), n = a.shape, b.shape[1]
    tm, tn, tk = min(tm, m), min(tn, n), min(tk, kd)
    assert m % tm == 0 and n % tn == 0 and kd % tk == 0, (name, m, n, kd, tm, tn, tk)
    k0, ksteps = (0, kd // tk) if k_blocks is None else k_blocks
    dims = {"nn": NN, "nt": NT, "tn": TN}[mode]
    n_out = len(out_dtypes)
    n_in = 2 + (extra is not None) + len(after)

    def finish(acc, e_ref, o_refs):
        outs = (acc,) if epilogue is None else epilogue(acc, e_ref[...] if e_ref is not None else None)
        for o_ref, o in zip(o_refs, outs):
            o_ref[...] = o.astype(o_ref.dtype)

    def product(a_ref, b_ref):
        if mode == "nt" and b_shards:
            w = b_cols
            parts = [_dot(a_ref[:, s * w:(s + 1) * w], b_ref[s], dims) for s in range(k_group)]
            return functools.reduce(lambda p, q: p + q, parts)
        return _dot(a_ref[...], b_ref[...], dims)

    def body(*refs):
        a_ref, b_ref = refs[0], refs[1]
        e_ref = refs[2] if extra is not None else None
        o_refs = refs[n_in:n_in + n_out]
        if ksteps == 1:
            finish(product(a_ref, b_ref), e_ref, o_refs)
            return
        acc_ref = refs[-1]
        kk = pl.program_id(2)

        @pl.when(kk == 0)
        def _():
            acc_ref[...] = jnp.zeros_like(acc_ref)

        acc_ref[...] += product(a_ref, b_ref)

        @pl.when(kk == ksteps - 1)
        def _():
            finish(acc_ref[...], e_ref, o_refs)

    if mode == "nn":
        a_spec = pl.BlockSpec((tm, tk), lambda i, j, k: (i, k0 + k))
        b_spec = (pl.BlockSpec((None, tk, tn), lambda i, j, k: (j, k, 0)) if b_shards
                  else pl.BlockSpec((tk, tn), lambda i, j, k: (k0 + k, j)))
    elif mode == "nt":
        a_spec = pl.BlockSpec((tm, tk), lambda i, j, k: (i, k))
        b_spec = (pl.BlockSpec((k_group, tn, b_cols), lambda i, j, k: (k, j, 0)) if b_shards
                  else pl.BlockSpec((tn, tk), lambda i, j, k: (j, k)))
    else:
        a_spec = pl.BlockSpec((tk, tm), lambda i, j, k: (k, i))
        b_spec = pl.BlockSpec((tk, tn), lambda i, j, k: (k, j))
    o_spec = pl.BlockSpec((tm, tn), lambda i, j, k: (i, j))
    res_spec = pl.BlockSpec((None, tm, tn), lambda i, j, k: (j, i, 0)) if out_shards else o_spec
    res_shape = (n // tn, m, tn) if out_shards else (m, n)
    in_specs = [a_spec, b_spec] + ([o_spec] if extra is not None else []) + [ANY_SPEC] * len(after)
    args = (a, b) + ((extra,) if extra is not None else ()) + tuple(after)
    res = pl.pallas_call(
        body, name=name, grid=(m // tm, n // tn, ksteps),
        in_specs=in_specs, out_specs=[res_spec] * n_out,
        out_shape=[jax.ShapeDtypeStruct(res_shape, dt) for dt in out_dtypes],
        scratch_shapes=[pltpu.VMEM((tm, tn), F32)] if ksteps > 1 else [],
        compiler_params=_params(("parallel", "parallel", "arbitrary")),
    )(*args)
    return res if n_out > 1 else res[0]


GATE_COL = 4 * GDN_WIDTH
RELAYOUT_ROWS = 256


def _cat_of_win(j):
    if j < GATE_COL:
        return j
    if j < GATE_COL + 2 * N_HEADS:
        return MAIN_WIDTH + (j - GATE_COL)
    return j - 2 * N_HEADS


def _win_of_cat(c):
    if c < GATE_COL:
        return c
    if c < MAIN_WIDTH:
        return c + 2 * N_HEADS
    if c < MAIN_WIDTH + 2 * N_HEADS:
        return GATE_COL + (c - MAIN_WIDTH)
    return None


def _runs(first, count, mapping):
    runs, i = [], 0
    while i < count:
        start, n = mapping(first + i), 1
        while i + n < count and mapping(first + i + n) == start + n:
            n += 1
        runs.append((start, n))
        i += n
    return runs


def weights_to_cat(g_in, name, total_rows, row0=0, into=None):
    n_dev, rows, shard = g_in.shape
    first = row0 // RELAYOUT_ROWS

    def body(x_ref, *rest):
        o_ref = rest[-1]
        for b in range(CAT_WIDTH // LANES):
            live = sum(_win_of_cat(LANES * b + i) is not None for i in range(LANES))
            parts = []
            for start, n in _runs(LANES * b, live, _win_of_cat):
                while n > 0:
                    d, o = divmod(start, shard)
                    take = min(n, shard - o)
                    parts.append(x_ref[d, :, o:o + take])
                    start, n = start + take, n - take
            if live < LANES:
                parts.append(jnp.zeros((RELAYOUT_ROWS, LANES - live), g_in.dtype))
            o_ref[:, LANES * b:LANES * (b + 1)] = parts[0] if len(parts) == 1 else jnp.concatenate(parts, axis=1)

    return pl.pallas_call(
        body, name=name, grid=(rows // RELAYOUT_ROWS,),
        in_specs=[pl.BlockSpec((n_dev, RELAYOUT_ROWS, shard), lambda i: (0, i, 0))] + ([ANY_SPEC] if into is not None else []),
        out_specs=pl.BlockSpec((RELAYOUT_ROWS, CAT_WIDTH), lambda i: (first + i, 0)),
        out_shape=jax.ShapeDtypeStruct((total_rows, CAT_WIDTH), g_in.dtype),
        input_output_aliases={1: 0} if into is not None else {},
        compiler_params=_params(("parallel",)))(*((g_in,) if into is None else (g_in, into)))


def cat_to_shards(dw_cat, shard):
    rows = dw_cat.shape[0]

    def body(x_ref, o_ref):
        for d in range(N_DEV):
            for t0 in range(0, shard, LANES):
                width = min(LANES, shard - t0)
                parts = [x_ref[:, c:c + n] for c, n in _runs(d * shard + t0, width, _cat_of_win)]
                o_ref[d, :, t0:t0 + width] = parts[0] if len(parts) == 1 else jnp.concatenate(parts, axis=1)

    return pl.pallas_call(
        body, name="cat_to_shards", grid=(rows // RELAYOUT_ROWS,),
        in_specs=[pl.BlockSpec((RELAYOUT_ROWS, CAT_WIDTH), lambda i: (i, 0))],
        out_specs=pl.BlockSpec((N_DEV, RELAYOUT_ROWS, shard), lambda i: (0, i, 0)),
        out_shape=jax.ShapeDtypeStruct((N_DEV, rows, shard), dw_cat.dtype),
        compiler_params=_params(("parallel",)))(dw_cat)


ROW_BLOCK = 512


def rms_fwd(x, w, name):
    t, d = x.shape

    def body(x_ref, w_ref, n_ref, r_ref):
        h = x_ref[...]
        r = lax.rsqrt(jnp.mean(h * h, axis=-1, keepdims=True) + NORM_EPS)
        n_ref[...] = (h * r * w_ref[...]).astype(BF16)
        r_ref[...] = r

    row = pl.BlockSpec((ROW_BLOCK, d), lambda i: (i, 0))
    return pl.pallas_call(
        body, name=name, grid=(t // ROW_BLOCK,),
        in_specs=[row, pl.BlockSpec((1, d), lambda i: (0, 0))],
        out_specs=[row, pl.BlockSpec((ROW_BLOCK, 1), lambda i: (i, 0))],
        out_shape=[jax.ShapeDtypeStruct((t, d), BF16), jax.ShapeDtypeStruct((t, 1), F32)],
        compiler_params=_params(("parallel",)))(x, w)


FUSED_ROWS = 512


def out_proj_rms(y, w_out, x, w_norm, name):
    t, d = x.shape

    def body(y_ref, w_ref, x_ref, g_ref, h_ref, n_ref, r_ref):
        h = x_ref[...] + _dot(y_ref[...], w_ref[...], NN)
        r = lax.rsqrt(jnp.mean(h * h, axis=-1, keepdims=True) + NORM_EPS)
        h_ref[...] = h
        n_ref[...] = (h * r * g_ref[...]).astype(BF16)
        r_ref[...] = r

    row = pl.BlockSpec((FUSED_ROWS, d), lambda i: (i, 0))
    return pl.pallas_call(
        body, name=name, grid=(t // FUSED_ROWS,),
        in_specs=[pl.BlockSpec((FUSED_ROWS, y.shape[1]), lambda i: (i, 0)), pl.BlockSpec(w_out.shape, lambda i: (0, 0)),
                  row, pl.BlockSpec((1, d), lambda i: (0, 0))],
        out_specs=[row, row, pl.BlockSpec((FUSED_ROWS, 1), lambda i: (i, 0))],
        out_shape=[jax.ShapeDtypeStruct((t, d), F32), jax.ShapeDtypeStruct((t, d), BF16),
                   jax.ShapeDtypeStruct((t, 1), F32)],
        compiler_params=_params(("parallel",)))(y, w_out, x, w_norm)


def ff2_loss(act, w_ff2, h1, w, target, name, tk=2048):
    t, d = h1.shape
    ksteps = act.shape[1] // tk

    def body(a_ref, b_ref, h_ref, w_ref, t_ref, loss_ref, dhb_ref, dw_ref, acc_ref):
        i, kk = pl.program_id(0), pl.program_id(1)

        @pl.when((i == 0) & (kk == 0))
        def _():
            loss_ref[...] = jnp.zeros_like(loss_ref)
            dw_ref[...] = jnp.zeros_like(dw_ref)

        @pl.when(kk == 0)
        def _():
            acc_ref[...] = h_ref[...]

        acc_ref[...] += _dot(a_ref[...], b_ref[...], NN)

        @pl.when(kk == ksteps - 1)
        def _():
            h = acc_ref[...]
            wv = w_ref[...]
            r = lax.rsqrt(jnp.mean(h * h, axis=-1, keepdims=True) + NORM_EPS)
            yn = h * r
            e = yn * wv - t_ref[...]
            loss_ref[...] += 0.5 * jnp.sum(jnp.sum(e * e, axis=-1, keepdims=True), axis=0, keepdims=True) / d
            dy = e / d
            dw_ref[...] += jnp.sum(dy * yn, axis=0, keepdims=True)
            dyn = dy * wv
            dhb_ref[...] = (r * (dyn - yn * jnp.mean(dyn * yn, axis=-1, keepdims=True))).astype(BF16)

    row = pl.BlockSpec((FUSED_ROWS, d), lambda i, k: (i, 0))
    wspec = pl.BlockSpec((1, d), lambda i, k: (0, 0))
    return pl.pallas_call(
        body, name=name, grid=(t // FUSED_ROWS, ksteps),
        in_specs=[pl.BlockSpec((FUSED_ROWS, tk), lambda i, k: (i, k)), pl.BlockSpec((tk, d), lambda i, k: (k, 0)),
                  row, wspec, row],
        out_specs=[pl.BlockSpec((1, 1), lambda i, k: (0, 0)), row, wspec],
        out_shape=[jax.ShapeDtypeStruct((1, 1), F32), jax.ShapeDtypeStruct((t, d), BF16),
                   jax.ShapeDtypeStruct((1, d), F32)],
        scratch_shapes=[pltpu.VMEM((FUSED_ROWS, d), F32)],
        compiler_params=_params(("arbitrary", "arbitrary")))(act, w_ff2, h1, w, target)


def rms_bwd(h, r, w, dn, dres, out_dtype, name):
    t, d = h.shape

    def body(h_ref, r_ref, w_ref, dn_ref, dres_ref, dh_ref, dw_ref):
        @pl.when(pl.program_id(0) == 0)
        def _():
            dw_ref[...] = jnp.zeros_like(dw_ref)

        rv = r_ref[...]
        yn = h_ref[...] * rv
        dnv = dn_ref[...].astype(F32)
        dw_ref[...] += jnp.sum(dnv * yn, axis=0, keepdims=True)
        dyn = dnv * w_ref[...]
        dh = dres_ref[...].astype(F32) + rv * (dyn - yn * jnp.mean(dyn * yn, axis=-1, keepdims=True))
        dh_ref[...] = dh.astype(out_dtype)

    row = pl.BlockSpec((ROW_BLOCK, d), lambda i: (i, 0))
    wspec = pl.BlockSpec((1, d), lambda i: (0, 0))
    rspec = pl.BlockSpec((ROW_BLOCK, 1), lambda i: (i, 0))
    return pl.pallas_call(
        body, name=name, grid=(t // ROW_BLOCK,),
        in_specs=[row, rspec, wspec, row, row], out_specs=[row, wspec],
        out_shape=[jax.ShapeDtypeStruct((t, d), out_dtype), jax.ShapeDtypeStruct((1, d), F32)],
        compiler_params=_params(("arbitrary",)))(h, r, w, dn, dres)


CONV_ROWS = 512
TILE_ROWS = 8


def _iota2(shape, axis):
    return lax.broadcasted_iota(jnp.int32, shape, axis)


def _silu(x):
    return x * jax.nn.sigmoid(x)


def _conv_rows(x_ref, w, first, rows):
    acc = None
    for j in range(4):
        term = x_ref[first - 3 + j:first - 3 + j + rows, :] * w[j:j + 1, :]
        acc = term if acc is None else acc + term
    return acc


def _head_shifts(head):
    rows = _iota2((TILE_ROWS, 1), 0)
    return [jnp.where(rows >= 3 - j, head if j == 3 else pltpu.roll(head, 3 - j, 0), 0.0) for j in range(4)]


def _conv_chunks(t):
    pieces = [(TILE_ROWS, min(CONV_ROWS, t) - TILE_ROWS)]
    pieces += [(r, CONV_ROWS) for r in range(CONV_ROWS, t, CONV_ROWS)]
    return pieces


def conv_fwd(proj, conv_w, name):
    t = proj.shape[0]

    def body(x_ref, w_ref, o_ref):
        w = w_ref[...]
        shifted = _head_shifts(x_ref[0:TILE_ROWS, :])
        o_ref[0:TILE_ROWS, :] = _silu(sum(shifted[j] * w[j:j + 1, :] for j in range(4)))
        for first, rows in _conv_chunks(t):
            o_ref[first:first + rows, :] = _silu(_conv_rows(x_ref, w, first, rows))

    col = pl.BlockSpec((t, LANES), lambda c: (0, c))
    return pl.pallas_call(
        body, name=name, grid=(QKV_WIDTH // LANES,),
        in_specs=[col, pl.BlockSpec((4, LANES), lambda c: (0, c))], out_specs=col,
        out_shape=jax.ShapeDtypeStruct((t, QKV_WIDTH), F32),
        compiler_params=_params(("parallel",)))(proj, conv_w)


def conv_bwd(proj, dout, conv_w, dproj, name):
    t = proj.shape[0]

    def dsilu(pre):
        sg = jax.nn.sigmoid(pre)
        return sg * (1.0 + pre * (1.0 - sg))

    def body(x_ref, d_ref, w_ref, dproj_in, dx_ref, dw_ref, stage):
        del dproj_in
        w = w_ref[...]
        shifted = _head_shifts(x_ref[0:TILE_ROWS, :])
        head_dpre = d_ref[0:TILE_ROWS, :] * dsilu(sum(shifted[j] * w[j:j + 1, :] for j in range(4)))
        stage[0:TILE_ROWS, :] = head_dpre
        for first, rows in _conv_chunks(t):
            stage[first:first + rows, :] = d_ref[first:first + rows, :] * dsilu(_conv_rows(x_ref, w, first, rows))
        stage[t:t + TILE_ROWS, :] = jnp.zeros((TILE_ROWS, LANES), F32)
        for first, rows in [(0, TILE_ROWS)] + _conv_chunks(t):
            dx = None
            for j in range(4):
                term = stage[first + 3 - j:first + 3 - j + rows, :] * w[j:j + 1, :]
                dx = term if dx is None else dx + term
            dx_ref[first:first + rows, :] = dx.astype(BF16)
        dw = [jnp.sum(head_dpre * shifted[j], axis=0, keepdims=True) for j in range(4)]
        for first, rows in _conv_chunks(t):
            dpre = stage[first:first + rows, :]
            for j in range(4):
                dw[j] = dw[j] + jnp.sum(dpre * x_ref[first - 3 + j:first - 3 + j + rows, :], axis=0, keepdims=True)
        dw_ref[...] = jnp.concatenate(dw, axis=0)

    col = pl.BlockSpec((t, LANES), lambda c: (0, c))
    taps = pl.BlockSpec((4, LANES), lambda c: (0, c))
    return pl.pallas_call(
        body, name=name, grid=(QKV_WIDTH // LANES,),
        in_specs=[col, col, taps, ANY_SPEC], out_specs=[col, taps],
        out_shape=[jax.ShapeDtypeStruct(dproj.shape, BF16), jax.ShapeDtypeStruct((4, QKV_WIDTH), F32)],
        scratch_shapes=[pltpu.VMEM((t + TILE_ROWS, LANES), F32)],
        input_output_aliases={3: 0},
        compiler_params=_params(("parallel",)))(proj, dout, conv_w, dproj)


def _softplus(x):
    return jnp.maximum(x, 0.0) + jnp.log(1.0 + jnp.exp(-jnp.abs(x)))


def _head_norm_gate(o, norm_w, gate):
    return o * lax.rsqrt(jnp.mean(o * o, axis=-1, keepdims=True) + NORM_EPS) * norm_w * _silu(gate)


GDN_PREC = ("bf", "bf")
HGRN_PREC = "bf"


def _each(fn, *cols):
    return [fn(*a) for a in zip(*cols)]


@functools.partial(jax.custom_vjp, nondiff_argnums=(2,))
def _known_inverse(low, inv, prec):
    del low, prec
    return inv


def _known_inverse_fwd(low, inv, prec):
    del low
    return inv, inv


def _known_inverse_bwd(prec, inv, ct):
    return -_mm_raw(_mm_raw(inv, ct, TN, prec), inv, NT, prec), jnp.zeros_like(inv)


_known_inverse.defvjp(_known_inverse_fwd, _known_inverse_bwd)


def gdn_stages(hs, qc, kc, vc, zc, ab, a_log_l, dt_l, norm_w, s, prec=GDN_PREC, inv_known=None):
    p_inv, p_mm = prec
    c = CHUNK
    ri, ci = _iota2((c, c), 0), _iota2((c, c), 1)
    incl, strict, eye = ri >= ci, ri > ci, ri == ci
    lane = _iota2((c, LANES), 1)
    last_row = _iota2((c, 1), 0) == c - 1
    rowsum = lambda x: jnp.sum(x, axis=1, keepdims=True)

    def row(col):
        return jnp.sum(jnp.where(eye, col, 0.0), axis=0, keepdims=True)

    q = _each(lambda x: x * lax.rsqrt(rowsum(x * x) + L2_EPS) * (HEAD_DIM ** -0.5), qc)
    k = _each(lambda x: x * lax.rsqrt(rowsum(x * x) + L2_EPS), kc)
    yield
    a_col = [rowsum(jnp.where(lane == h, ab, 0.0)) for h in hs]
    b_col = [rowsum(jnp.where(lane == h + N_HEADS, ab, 0.0)) for h in hs]
    beta = _each(jax.nn.sigmoid, b_col)
    g = _each(lambda a, al, dl: rowsum(jnp.where(lane == 0, -jnp.exp(al) * _softplus(a + dl), 0.0)), a_col, a_log_l, dt_l)
    gcum = _each(lambda x: rowsum(jnp.where(incl, row(x), 0.0)), g)
    g_last = _each(lambda x: jnp.sum(jnp.where(last_row, x, 0.0), axis=0, keepdims=True), gcum)
    decay = _each(lambda x: jnp.exp(jnp.where(incl, x - row(x), -jnp.inf)), gcum)
    yield
    kk = _each(lambda x: mm(x, x, NT, p_mm), k)
    low = _each(lambda b, x, d: jnp.where(strict, b * x * d, 0.0), beta, kk, decay)
    yield
    if inv_known is None:
        power = _each(lambda x: -x, low)
        inv = _each(lambda x: jnp.where(eye, 1.0, 0.0) + x, power)
        for _ in range(5):
            power = _each(lambda x: mm(x, x, NN, p_inv), power)
            yield
            inv = _each(lambda x, p: x + mm(x, p, NN, p_inv), inv, power)
            yield
    else:
        inv = _each(lambda x, known: _known_inverse(x, known, p_inv), low, inv_known)
    exp_g = _each(jnp.exp, gcum)
    yield
    u_v = _each(lambda i, b, x: mm(i, b * x, NN, p_mm), inv, beta, vc)
    w = _each(lambda i, b, e, x: mm(i, b * e * x, NN, p_mm), inv, beta, exp_g, k)
    yield
    attn = _each(lambda x, y, d: mm(x, y, NT, p_mm) * d, q, k, decay)
    yield
    u = _each(lambda x, y, z: x - mm(y, z, NN, p_mm), u_v, w, s)
    yield
    o = _each(lambda x, e, z: mm(x * e, z, NN, p_mm), q, exp_g, s)
    o = _each(lambda x, a, y: x + mm(a, y, NN, p_mm), o, attn, u)
    yield
    k_end = _each(lambda x, gl, gc: x * jnp.exp(gl - gc), k, g_last, gcum)
    s_new = _each(lambda z, gl, x, y: z * jnp.exp(gl) + mm(x, y, TN, p_mm), s, g_last, k_end, u)
    return (_each(lambda x, z: _head_norm_gate(x, norm_w, z), o, zc), s_new), inv


def gdn_chunk(h, qc, kc, vc, zc, ab, a_log_l, dt_l, norm_w, s, prec=GDN_PREC, reuse_inverse=False):
    args = ([h], [qc], [kc], [vc], [zc], ab, [a_log_l], [dt_l], norm_w, [s], prec)
    if reuse_inverse:
        inv = lax.stop_gradient(gdn_chunks(*args)[1])
        (y, s_new), _ = gdn_chunks(*args, inv_known=inv)
    else:
        (y, s_new), _ = gdn_chunks(*args)
    return y[0], s_new[0]


DIAG_ROWS = SUB_CHUNK // 2
SHIFT_PAD = 8
SHIFT_ROWS = SHIFT_PAD + CHUNK + SHIFT_PAD
SHIFT_WAYS = 4


class RolledRows:
    def down(self, x, which):
        del which
        return [x] + [pltpu.roll(x, off, 0) for off in range(1, DIAG_ROWS)]

    def up_sum(self, parts, which):
        del which
        acc = parts[0]
        for off in range(1, DIAG_ROWS):
            acc = acc + pltpu.roll(parts[off], CHUNK - off, 0)
        return acc


class SlotRows:
    def __init__(self, slots):
        self.slots = slots

    def down(self, x, which):
        self.slots[which, 0, SHIFT_PAD:SHIFT_PAD + CHUNK, :] = x
        return [x] + [self.slots[which, 0, SHIFT_PAD - off:SHIFT_PAD + CHUNK - off, :] for off in range(1, DIAG_ROWS)]

    def up_sum(self, parts, which):
        acc = parts[0]
        for off in range(1, DIAG_ROWS):
            way = 1 + off % (SHIFT_WAYS - 1)
            self.slots[which, way, SHIFT_PAD:SHIFT_PAD + CHUNK, :] = parts[off]
            acc = acc + self.slots[which, way, SHIFT_PAD + off:SHIFT_PAD + CHUNK + off, :]
        return acc


def _sub_block_rows():
    return jnp.bitwise_and(_iota2((CHUNK, 1), 0), DIAG_ROWS - 1)


def _diag_forward(rows, q, key, bc, v):
    rmod = _sub_block_rows()
    k_d, b_d, v_d = rows.down(key, 0), rows.down(bc, 1), rows.down(v, 2)
    o = None
    for off in range(DIAG_ROWS):
        e = jnp.exp(jnp.where(rmod >= off, bc - b_d[off], -jnp.inf))
        term = jnp.sum(q * k_d[off] * e, axis=-1, keepdims=True) * v_d[off]
        o = term if o is None else o + term
    return o


def _diag_backward(rows, q, key, bc, v, do):
    rmod = _sub_block_rows()
    k_d, b_d, v_d = rows.down(key, 0), rows.down(bc, 1), rows.down(v, 2)
    dq = db = None
    dk_parts, db_parts, dv_parts = [], [], []
    for off in range(DIAG_ROWS):
        e = jnp.exp(jnp.where(rmod >= off, bc - b_d[off], -jnp.inf))
        qe = q * e
        a = jnp.sum(qe * k_d[off], axis=-1, keepdims=True)
        da = jnp.sum(do * v_d[off], axis=-1, keepdims=True)
        dv_parts.append(a * do)
        dq_term = (da * e) * k_d[off]
        dk_term = da * qe
        s = dk_term * k_d[off]
        dq = dq_term if dq is None else dq + dq_term
        db = s if db is None else db + s
        dk_parts.append(dk_term)
        db_parts.append(s)
    return dq, rows.up_sum(dk_parts, 0), db - rows.up_sum(db_parts, 1), rows.up_sum(dv_parts, 2)


def diag_part(rows, differentiable=True):
    forward = functools.partial(_diag_forward, rows)
    if not differentiable:
        return forward
    part = jax.custom_vjp(forward)
    part.defvjp(lambda q, key, bc, v: (forward(q, key, bc, v), (q, key, bc, v)),
                lambda res, do: _diag_backward(rows, *res, do))
    return part


def hgrn_stages(qb, fb, ib, gb, l0, l1, norm_w, st, prec=HGRN_PREC, diags=None, o_known=None):
    c = CHUNK
    ri, ci = _iota2((4 * c, c), 0), _iota2((4 * c, c), 1)
    rcol = _iota2((c, 1), 0)
    blk0 = jnp.bitwise_and(ri, c - SUB_CHUNK)
    limit = jnp.where(ri < c, ri + 1, jnp.where(ri < 2 * c, blk0, jnp.where(ri < 3 * c, blk0 + SUB_CHUNK,
                                                                          blk0 + DIAG_ROWS)))
    sel = jnp.where(ci < limit, 1.0, 0.0)
    ri, ci = _iota2((c, c), 0), _iota2((c, c), 1)
    lb = _each(lambda a, b: jax.nn.sigmoid(a - b), l0, l1)
    forget = _each(lambda b, f: b + (1.0 - b) * jax.nn.sigmoid(f), lb, fb)
    key = _each(lambda b, f: (1.0 - b) * jax.nn.sigmoid(-f), lb, fb)
    q = _each(_silu, qb)
    v = ib
    logf = _each(jnp.log, forget)
    sums = _each(lambda x: sel_sums(sel, x), logf)
    bc, b_start, b_end, b_half = ([x[i] for x in sums] for i in range(4))
    b_last = _each(lambda x: jnp.sum(x, axis=0, keepdims=True), logf)
    o = _each(lambda x, b, z: mm(x * jnp.exp(b), z, NT, prec), q, bc, st)
    if diags is None:
        diags = [diag_part(RolledRows())] * len(qb)
    yield
    o = list(o)
    for h in range(len(o)):
        o[h] = o[h] + diags[h](q[h], key[h], bc[h], v[h])
        yield
    second = jnp.bitwise_and(rcol, SUB_CHUNK - 1) >= DIAG_ROWS
    same_sub = jnp.bitwise_and(ri, c - SUB_CHUNK) == jnp.bitwise_and(ci, c - SUB_CHUNK)
    q_half = _each(lambda x, b, bh: x * jnp.exp(jnp.where(second, b - bh, -jnp.inf)), q, bc, b_half)
    k_half = _each(lambda x, b, bh: x * jnp.exp(jnp.where(second, -jnp.inf, bh - b)), key, bc, b_half)
    a_half = _each(lambda x, z: jnp.where(same_sub, mm(x, z, NT, prec), 0.0), q_half, k_half)
    o = _each(lambda acc, a, val: acc + mm(a, val, NN, prec), o, a_half, v)
    yield
    q_rel = _each(lambda x, b, bs: x * jnp.exp(b - bs), q, bc, b_start)
    k_rel = _each(lambda x, b, be: x * jnp.exp(be - b), key, bc, b_end)
    for y in range(c // SUB_CHUNK - 1):
        def scaled(x, b, bs):
            end_y = jnp.sum(jnp.where(rcol == SUB_CHUNK * y + SUB_CHUNK - 1, b, 0.0), axis=0, keepdims=True)
            return x * jnp.exp(jnp.where(rcol >= SUB_CHUNK * (y + 1), bs - end_y, -jnp.inf))
        dq = _each(scaled, q_rel, bc, b_start)
        in_y = (ci >= SUB_CHUNK * y) & (ci < SUB_CHUNK * (y + 1))
        a_y = _each(lambda x, z: jnp.where(in_y, mm(x, z, NT, prec), 0.0), dq, k_rel)
        o = _each(lambda acc, a, val: acc + mm(a, val, NN, prec), o, a_y, v)
        yield
    k_state = _each(lambda x, bl, b: x * jnp.exp(bl - b), key, b_last, bc)
    st_new = _each(lambda z, bl, val, x: z * jnp.exp(bl) + mm(val, x, TN, prec), st, b_last, v, k_state)
    if o_known is not None:
        o = _each(_known_value, o, o_known)
    return (_each(lambda x, z: _head_norm_gate(x, norm_w, z), o, gb), st_new), o


def _drain(gen):
    try:
        while True:
            next(gen)
    except StopIteration as done:
        return done.value


def _alternate(gen_a, gen_b):
    out, live = [None, None], [gen_a, gen_b]
    while any(g is not None for g in live):
        for i, g in enumerate(live):
            if g is None:
                continue
            try:
                next(g)
            except StopIteration as done:
                out[i], live[i] = done.value, None
    return out


def gdn_chunks(*args, **kwargs):
    return _drain(gdn_stages(*args, **kwargs))


def hgrn_chunks(*args, **kwargs):
    return _drain(hgrn_stages(*args, **kwargs))


def hgrn_chunk(qb, fb, ib, gb, l0, l1, norm_w, st, prec=HGRN_PREC, reuse_output=False):
    args = ([qb], [fb], [ib], [gb], [l0], [l1], norm_w, [st], prec)
    if reuse_output:
        known = lax.stop_gradient(hgrn_chunks(*args)[1])
        (y, st_new), _ = hgrn_chunks(*args, o_known=known)
    else:
        (y, st_new), _ = hgrn_chunks(*args)
    return y[0], st_new[0]


HEAD_VEC = (N_HEADS, 1, LANES)


class _ChunkSpecs:
    def __init__(self, nc, rev):
        self.nc, self.rev = nc, rev

    def _c(self, c):
        return self.nc - 1 - c if self.rev else c

    def row(self, width, block=0):
        return pl.BlockSpec((CHUNK, width), lambda c: (self._c(c), block))

    def per_head(self, rows):
        return pl.BlockSpec((None, N_HEADS, rows, rows), lambda c: (self._c(c), 0, 0, 0))

    @staticmethod
    def whole(shape):
        return pl.BlockSpec(shape, lambda c: (0,) * len(shape))


def _lanes(j):
    return slice(j * LANES, (j + 1) * LANES)


def mixer_fwd(qkv_c, proj, a_log_l, dt_l, gdn_norm_w, l0, l1, hgrn_norm_w, name):
    t = qkv_c.shape[0]
    hb = N_HEADS
    sp = _ChunkSpecs(t // CHUNK, rev=False)
    hs = list(range(hb))

    def body(q_ref, k_ref, v_ref, z_ref, ab_ref, al_ref, dt_ref, gnw_ref, qb_ref, fb_ref, ib_ref, gb_ref, l0_ref, l1_ref,
             hnw_ref, y_ref, hist_a_ref, inv_ref, hist_b_ref, o_ref, sa_ref, sb_ref, shift_ref):
        @pl.when(pl.program_id(0) == 0)
        def _():
            sa_ref[...] = jnp.zeros_like(sa_ref)
            sb_ref[...] = jnp.zeros_like(sb_ref)
            shift_ref[...] = jnp.zeros_like(shift_ref)

        heads = lambda ref: [ref[:, _lanes(j)] for j in hs]
        s_a, s_b = [sa_ref[h] for h in hs], [sb_ref[h] for h in hs]
        for h in hs:
            hist_a_ref[h] = s_a[h]
            hist_b_ref[h] = s_b[h]
        diags = [diag_part(SlotRows(shift_ref.at[h]), differentiable=False) for h in hs]
        ((y_a, s_a_new), inv), ((y_b, s_b_new), o_pre) = _alternate(
            gdn_stages(hs, heads(q_ref), heads(k_ref), heads(v_ref), heads(z_ref), ab_ref[...],
                       [al_ref[h] for h in hs], [dt_ref[h] for h in hs], gnw_ref[...], s_a),
            hgrn_stages(heads(qb_ref), heads(fb_ref), heads(ib_ref), heads(gb_ref),
                        [l0_ref[h] for h in hs], [l1_ref[h] for h in hs], hnw_ref[...], s_b, diags=diags))
        for h in hs:
            y_ref[:, _lanes(h)] = y_a[h].astype(BF16)
            y_ref[:, _lanes(hb + h)] = y_b[h].astype(BF16)
            o_ref[:, _lanes(h)] = o_pre[h]
            sa_ref[h] = s_a_new[h]
            sb_ref[h] = s_b_new[h]
            inv_ref[h] = inv[h]

    vec, gain, slab = sp.whole(HEAD_VEC), sp.whole((1, LANES)), functools.partial(sp.row, GDN_WIDTH)
    states = jax.ShapeDtypeStruct((sp.nc, N_HEADS, HEAD_DIM, HEAD_DIM), F32)
    return pl.pallas_call(
        body, name=name, grid=(sp.nc,),
        in_specs=[slab(0), slab(1), slab(2), slab(3), sp.row(LANES, AB_BLOCK), vec, vec, gain,
                  slab(4), slab(5), slab(6), slab(7), vec, vec, gain],
        out_specs=[sp.row(2 * GDN_WIDTH), sp.per_head(HEAD_DIM), sp.per_head(CHUNK), sp.per_head(HEAD_DIM), slab(0)],
        out_shape=[jax.ShapeDtypeStruct((t, 2 * GDN_WIDTH), BF16), states,
                   jax.ShapeDtypeStruct((sp.nc, N_HEADS, CHUNK, CHUNK), F32), states,
                   jax.ShapeDtypeStruct((t, GDN_WIDTH), F32)],
        scratch_shapes=[pltpu.VMEM((N_HEADS, HEAD_DIM, HEAD_DIM), F32), pltpu.VMEM((N_HEADS, HEAD_DIM, HEAD_DIM), F32),
                        pltpu.VMEM((hb, 3, SHIFT_WAYS, SHIFT_ROWS, LANES), F32)],
        compiler_params=_params(("arbitrary",)),
    )(qkv_c, qkv_c, qkv_c, proj, proj, a_log_l, dt_l, gdn_norm_w, proj, proj, proj, proj, l0, l1, hgrn_norm_w)


def mixer_bwd(qkv_c, proj, a_log_l, dt_l, gdn_norm_w, l0, l1, hgrn_norm_w, hist_a, inv_hist, hist_b, o_pre, dy, name):
    t = qkv_c.shape[0]
    hb = N_HEADS
    sp = _ChunkSpecs(t // CHUNK, rev=True)
    hs = list(range(hb))

    def body(q_ref, k_ref, v_ref, z_ref, ab_ref, al_ref, dt_ref, gnw_ref, qb_ref, fb_ref, ib_ref, gb_ref, l0_ref, l1_ref,
             hnw_ref, hist_a_ref, inv_ref, hist_b_ref, o_ref, dy_ref,
             dqkv_ref, dproj_ref, dal_ref, ddt_ref, dgnw_ref, dl0_ref, dl1_ref, dhnw_ref, dsa_ref, dsb_ref, shift_ref):
        @pl.when(pl.program_id(0) == 0)
        def _():
            for ref in (dal_ref, ddt_ref, dgnw_ref, dl0_ref, dl1_ref, dhnw_ref, dsa_ref, dsb_ref, shift_ref):
                ref[...] = jnp.zeros_like(ref)

        heads = lambda ref, first=0: [ref[:, _lanes(first + j)] for j in hs]
        diags = [diag_part(SlotRows(shift_ref.at[h])) for h in hs]
        inv_known, o_known = [inv_ref[h] for h in hs], heads(o_ref)

        def both(ga, gb):
            (ra, inv), (rb, o_pre) = _alternate(gdn_stages(hs, *ga, inv_known=inv_known),
                                                hgrn_stages(*gb, diags=diags, o_known=o_known))
            return (ra, rb), (inv, o_pre)

        ga = (heads(q_ref), heads(k_ref), heads(v_ref), heads(z_ref), ab_ref[...], [al_ref[h] for h in hs],
              [dt_ref[h] for h in hs], gnw_ref[...], [hist_a_ref[h] for h in hs])
        gb = (heads(qb_ref), heads(fb_ref), heads(ib_ref), heads(gb_ref), [l0_ref[h] for h in hs],
              [l1_ref[h] for h in hs], hnw_ref[...], [hist_b_ref[h] for h in hs])
        _, vjp, _ = jax.vjp(both, ga, gb, has_aux=True)
        dy_a = [x.astype(F32) for x in heads(dy_ref)]
        dy_b = [x.astype(F32) for x in heads(dy_ref, hb)]
        (dq, dk, dv, dz, dab, dal, ddt, dgnw, ds_a), (dqb, dfb, dib, dgb, dl0, dl1, dhnw, ds_b) = vjp(
            ((dy_a, [dsa_ref[h] for h in hs]), (dy_b, [dsb_ref[h] for h in hs])))
        for h in hs:
            dqkv_ref[:, _lanes(h)] = dq[h]
            dqkv_ref[:, _lanes(hb + h)] = dk[h]
            dqkv_ref[:, _lanes(2 * hb + h)] = dv[h]
            for slab, val in enumerate((dz, dqb, dfb, dib, dgb)):
                dproj_ref[:, _lanes((3 + slab) * hb + h)] = val[h].astype(BF16)
            dal_ref[h] += dal[h]
            ddt_ref[h] += ddt[h]
            dl0_ref[h] += dl0[h]
            dl1_ref[h] += dl1[h]
            dsa_ref[h] = ds_a[h]
            dsb_ref[h] = ds_b[h]
        dproj_ref[:, MAIN_WIDTH:] = dab.astype(BF16)
        dgnw_ref[...] += dgnw
        dhnw_ref[...] += dhnw

    vec, gain, slab = sp.whole(HEAD_VEC), sp.whole((1, LANES)), functools.partial(sp.row, GDN_WIDTH)
    vec_shape, gain_shape = jax.ShapeDtypeStruct(HEAD_VEC, F32), jax.ShapeDtypeStruct((1, LANES), F32)
    return pl.pallas_call(
        body, name=name, grid=(sp.nc,),
        in_specs=[slab(0), slab(1), slab(2), slab(3), sp.row(LANES, AB_BLOCK), vec, vec, gain,
                  slab(4), slab(5), slab(6), slab(7), vec, vec, gain,
                  sp.per_head(HEAD_DIM), sp.per_head(CHUNK), sp.per_head(HEAD_DIM), slab(0), sp.row(2 * GDN_WIDTH)],
        out_specs=[sp.row(QKV_WIDTH), sp.row(CAT_WIDTH), vec, vec, gain, vec, vec, gain],
        out_shape=[jax.ShapeDtypeStruct((t, QKV_WIDTH), F32), jax.ShapeDtypeStruct((t, CAT_WIDTH), BF16),
                   vec_shape, vec_shape, gain_shape, vec_shape, vec_shape, gain_shape],
        scratch_shapes=[pltpu.VMEM((N_HEADS, HEAD_DIM, HEAD_DIM), F32), pltpu.VMEM((N_HEADS, HEAD_DIM, HEAD_DIM), F32),
                        pltpu.VMEM((hb, 3, SHIFT_WAYS, SHIFT_ROWS, LANES), F32)],
        compiler_params=_params(("arbitrary",)),
    )(qkv_c, qkv_c, qkv_c, proj, proj, a_log_l, dt_l, gdn_norm_w, proj, proj, proj, proj, l0, l1, hgrn_norm_w,
      hist_a, inv_hist, hist_b, o_pre, dy)


def _adamw(w, g, m, v):
    m = ADAM_B1 * m + (1.0 - ADAM_B1) * g
    v = ADAM_B2 * v + (1.0 - ADAM_B2) * jnp.square(g)
    m_hat = m / (1.0 - ADAM_B1 ** ADAM_STEP)
    v_hat = v / (1.0 - ADAM_B2 ** ADAM_STEP)
    delta = -ADAM_LR * (m_hat / (jnp.sqrt(v_hat) + ADAM_EPS) + ADAM_WD * w)
    return delta, m, v


def adamw_reduce(parts, mine, slot, w, m, v, name, rb=128):
    r, c = w.shape
    rb = min(rb, r)
    n_parts = parts.shape[0]

    def body(slot_ref, p_ref, own_ref, w_ref, m_ref, v_ref, g_ref, d_ref, mo_ref, vo_ref):
        part = lambda d: jnp.where(slot_ref[0] == d, own_ref[...], p_ref[d]).astype(F32)
        g = part(0)
        for d in range(1, n_parts):
            g = g + part(d)
        delta, mn, vn = _adamw(w_ref[...], g, m_ref[...], v_ref[...])
        g_ref[...] = g
        d_ref[...] = delta
        mo_ref[...] = mn
        vo_ref[...] = vn

    blk = pl.BlockSpec((rb, c), lambda i, s: (i, 0))
    return pl.pallas_call(
        body, name=name,
        grid_spec=pltpu.PrefetchScalarGridSpec(
            num_scalar_prefetch=1, grid=(r // rb,),
            in_specs=[pl.BlockSpec((n_parts, rb, c), lambda i, s: (0, i, 0)),
                      pl.BlockSpec((None, rb, c), lambda i, s: (s[0], i, 0)), blk, blk, blk],
            out_specs=[blk] * 4),
        out_shape=[jax.ShapeDtypeStruct((r, c), F32)] * 4,
        compiler_params=_params(("parallel",)))(slot.astype(jnp.int32).reshape(1), parts, mine, w, m, v)


def adamw_small(ws, gs, ms, vs, name):
    n = len(ws)

    def body(*refs):
        for i in range(n):
            w_ref, g_ref, m_ref, v_ref = (refs[j * n + i] for j in range(4))
            outs = _adamw(w_ref[...], g_ref[...], m_ref[...], v_ref[...])
            for j, o in enumerate(outs):
                refs[(4 + j) * n + i][...] = o

    vmem = pl.BlockSpec(memory_space=pltpu.VMEM)
    res = pl.pallas_call(body, name=name, in_specs=[vmem] * (4 * n), out_specs=[vmem] * (3 * n),
                         out_shape=[jax.ShapeDtypeStruct(w.shape, F32) for w in ws] * 3)(*ws, *gs, *ms, *vs)
    return res[:n], res[n:2 * n], res[2 * n:]


def _pack(arrays):
    flat = jnp.concatenate([a.reshape(-1).astype(F32) for a in arrays])
    rows = -(-flat.shape[0] // (8 * LANES)) * 8
    return jnp.pad(flat, (0, rows * LANES - flat.shape[0])).reshape(rows, LANES)


def _unpack(packed, shapes):
    flat, out, off = packed.reshape(-1), [], 0
    for s in shapes:
        n = 1
        for d in s:
            n *= d
        out.append(flat[off:off + n].reshape(s))
        off += n
    return out


def _relu2_epilogue(acc, _):
    r = jnp.maximum(acc, 0.0)
    return acc, r * r


def _relu2_bwd_epilogue(acc, a1):
    return (acc * (2.0 * jnp.maximum(a1, 0.0)),)


def kernel(x, w_in, conv_w, gdn_a_log, gdn_dt_bias, gdn_norm_w, hgrn_lb_logits, hgrn_norm_w, w_out, norm_mix_w, norm_ffn_w, w_ff1, w_ff2, norm_final_w, loss_target, m_w_in, m_conv_w, m_gdn_a_log, m_gdn_dt_bias, m_gdn_norm_w, m_hgrn_lb_logits, m_hgrn_norm_w, m_w_out, m_norm_mix_w, m_norm_ffn_w, m_w_ff1, m_w_ff2, m_norm_final_w, v_w_in, v_conv_w, v_gdn_a_log, v_gdn_dt_bias, v_gdn_norm_w, v_hgrn_lb_logits, v_hgrn_norm_w, v_w_out, v_norm_mix_w, v_norm_ffn_w, v_w_ff1, v_w_ff2, v_norm_final_w):
    me = _my_flat()
    xs = x[0]
    target = loss_target[0]
    shard_in = w_in.shape[2]
    shard_conv = conv_w.shape[2]

    tok = lambda t: t[0:1, 0:1]

    half = D_MODEL // 2
    w_in_b = w_in[0].astype(BF16)
    h_ga, t_ga = exchange_start([w_in_b[:half], conv_w[0]], True, "gather_w_in_high_start", peers=CHIP_PEERS)
    h_g0, t_g0 = exchange_start([w_in_b[half:]], True, "gather_w_in_low_start", after=[t_ga], peers=CHIP_PEERS)
    behind = lambda a: lax.optimization_barrier((a, t_g0))[0]
    h_g1, t_g1 = exchange_start([behind(w_out[0]).astype(BF16), behind(w_ff1[0]).astype(BF16)], True,
                                "gather_mid_start", after=[t_g0], peers=CHIP_PEERS)
    h_g2, t_g2 = exchange_start([behind(w_ff2[0]).astype(BF16)], True, "gather_ff2_start", after=[t_g1],
                                peers=CHIP_PEERS)
    m_in, v_in, _ = lax.optimization_barrier((m_w_in, v_w_in, t_g2))
    m_in, v_in = m_in[0], v_in[0]

    lane_b = lambda p: jnp.broadcast_to(p.reshape(N_HEADS, 1, 1), HEAD_VEC)
    a_log_l, dt_l = lane_b(gdn_a_log[0]), lane_b(gdn_dt_bias[0])
    l0 = hgrn_lb_logits[0].reshape(HEAD_VEC)
    l1 = hgrn_lb_logits[1].reshape(HEAD_VEC)

    n1, r1 = rms_fwd(xs, norm_mix_w + tok(t_g1) + tok(t_g2), "rms_mix")
    (s_high, s_conv), (l_high, l_conv) = exchange_wait(h_ga, "gather_w_in_high_wait", after=[n1, m_in, v_in],
                                                       copies=len(CHIP_PEERS))
    h_fa, _ = forward_start([l_high, l_conv], "gather_w_in_high_forward")
    _, (l_high, l_conv) = exchange_wait(h_fa, "forward_w_in_high_wait", copies=len(OTHER_CHIPS))
    w_cat = weights_to_cat(_own_slot(l_high, s_high), "weights_to_cat", D_MODEL)
    conv_full = jnp.transpose(_own_slot(l_conv, s_conv), (1, 0, 2)).reshape(4, QKV_WIDTH)
    proj = matmul(n1, w_cat, "nn", "in_proj_high", (BF16,), tn=CAT_WIDTH // 5, tk=half, k_blocks=(0, 1))
    (s_low,), (l_low,) = exchange_wait(h_g0, "gather_w_in_low_wait", after=[proj], copies=len(CHIP_PEERS))
    h_f0, _ = forward_start([l_low], "gather_w_in_low_forward")
    _, (l_low,) = exchange_wait(_one(h_f0, 0), "forward_w_in_low_wait", copies=len(OTHER_CHIPS))
    w_cat = weights_to_cat(_own_slot(l_low, s_low), "weights_to_cat_low", D_MODEL, row0=half, into=w_cat)
    proj = matmul(n1, w_cat, "nn", "in_proj_low", tn=CAT_WIDTH // 5, tk=half, k_blocks=(1, 1), extra=proj,
                  epilogue=lambda acc, high: (acc + high,))
    qkv_c = conv_fwd(proj, conv_full, "conv_fwd")
    y, hist_a, inv_a, hist_b, o_b = mixer_fwd(qkv_c, proj, a_log_l, dt_l, gdn_norm_w, l0, l1, hgrn_norm_w, "mixer_fwd")
    (s_out, s_ff1), (l_out, l_ff1) = exchange_wait(h_g1, "gather_mid_wait", after=[y], copies=len(CHIP_PEERS))
    (s_ff2,), (l_ff2,) = exchange_wait(h_g2, "gather_ff2_wait", after=[y], copies=len(CHIP_PEERS))
    h_fw, _ = forward_start([l_out, l_ff1, l_ff2], "gather_forward_start")
    _, (l_out,) = exchange_wait(_one(h_fw, 0), "forward_out_wait", copies=len(OTHER_CHIPS))
    w_out_full = _own_slot(l_out, s_out).reshape(D_MODEL, D_MODEL)
    h1, n2, r2 = out_proj_rms(y, w_out_full, xs, norm_ffn_w, "out_proj_rms")
    _, (l_ff1,) = exchange_wait(_one(h_fw, 1), "forward_ff1_wait", after=[n2], copies=len(OTHER_CHIPS))
    w_ff1_sh = _own_slot(l_ff1, s_ff1)
    a1, act = matmul(n2, w_ff1_sh, "nn", "ff1", out_dtypes=(F32, BF16), epilogue=_relu2_epilogue, b_shards=True)
    _, (l_ff2,) = exchange_wait(_one(h_fw, 2), "forward_ff2_wait", after=[act], copies=len(OTHER_CHIPS))
    w_ff2_full = _own_slot(l_ff2, s_ff2).reshape(D_FF, D_MODEL)
    loss_sum, dh2_b, d_final = ff2_loss(act, w_ff2_full, h1, norm_final_w.reshape(1, D_MODEL), target, "ff2_loss")

    da1 = matmul(dh2_b, w_ff2_full, "nt", "d_act", out_dtypes=(BF16,), epilogue=_relu2_bwd_epilogue, extra=a1)
    t_all = xs.shape[0]
    dw_ff2 = matmul(act, dh2_b, "tn", "dw_ff2", out_dtypes=(BF16,), tk=t_all)
    p_ff2 = dw_ff2.reshape(N_DEV, D_FF // N_DEV, D_MODEL)
    h_s1, t_s1 = exchange_start([p_ff2], False, "scatter_ff2_start")
    dn2 = matmul(da1, w_ff1_sh, "nt", "d_n2", out_dtypes=(BF16,), after=[t_s1], b_shards=True, k_group=4)
    p_ff1 = matmul(n2, da1, "tn", "dw_ff1", out_dtypes=(BF16,), tn=D_FF // N_DEV, tk=t_all, after=[t_s1], out_shards=True)
    h_s2, t_s2 = exchange_start([p_ff1], False, "scatter_ff1_start")
    dh1_b, d_ffn = rms_bwd(h1, r2, norm_ffn_w + tok(t_s2), dn2, dh2_b, BF16, "rms_ffn_bwd")
    dmix = matmul(dh1_b, w_out_full, "nt", "d_mix", out_dtypes=(BF16,))
    dw_out = matmul(y, dh1_b, "tn", "dw_out", out_dtypes=(BF16,), tk=t_all)
    p_out = dw_out.reshape(N_DEV, D_MODEL // N_DEV, D_MODEL)
    h_s3, t_s3 = exchange_start([p_out], False, "scatter_out_start")
    d_qkv_c, dproj, d_alog_l, d_dt_l, d_gnw, dl0, dl1, d_hnw = mixer_bwd(
        qkv_c, proj, a_log_l, dt_l, gdn_norm_w + tok(t_s3), l0, l1, hgrn_norm_w, hist_a, inv_a, hist_b, o_b, dmix,
        "mixer_bwd")
    dproj, d_conv_full = conv_bwd(proj, d_qkv_c, conv_full, dproj, "conv_bwd")
    dw_cat = matmul(n1, dproj, "tn", "dw_in", out_dtypes=(BF16,), tm=512, tn=CAT_WIDTH // 5, tk=t_all)
    p_in = cat_to_shards(dw_cat, shard_in)
    h_pair, t_s4 = routed_start(p_in, _to_sibling_routes, "scatter_in_pair_start")

    (s_ff2g,), (r_ff2,) = exchange_wait(h_s1, "scatter_ff2_wait", after=[t_s4])
    (s_ff1g,), (r_ff1,) = exchange_wait(h_s2, "scatter_ff1_wait", after=[t_s4])
    (s_outg,), (r_out,) = exchange_wait(h_s3, "scatter_out_wait", after=[t_s4])
    g_w_ff2, d_w_ff2, nm_w_ff2, nv_w_ff2 = adamw_reduce(
        r_ff2, s_ff2g, me, w_ff2[0], m_w_ff2[0], v_w_ff2[0], "adamw_w_ff2")
    g_w_ff1, d_w_ff1, nm_w_ff1, nv_w_ff1 = adamw_reduce(
        r_ff1, s_ff1g, me, w_ff1[0], m_w_ff1[0], v_w_ff1[0], "adamw_w_ff1")
    g_w_out, d_w_out, nm_w_out, nv_w_out = adamw_reduce(
        r_out, s_outg, me, w_out[0], m_w_out[0], v_w_out[0], "adamw_w_out")
    (p_in,), (from_sibling,) = exchange_wait(h_pair, "scatter_in_pair_wait", after=[d_w_ff2, d_w_ff1, d_w_out],
                                             copies=N_CHIPS)
    chip_sums = pair_sum(p_in, from_sibling, "scatter_in_pair_sum")
    h_chips, t_s5 = routed_start(chip_sums, _to_chips_routes, "scatter_in_chips_start")
    dn1 = matmul(dproj, w_cat, "nt", "d_n1", out_dtypes=(BF16,), tm=512, tn=512, tk=CAT_WIDTH, after=[t_s5])
    dx, d_mix = rms_bwd(xs, r1, norm_mix_w, dn1, dh1_b, F32, "rms_mix_bwd")
    (chip_sums,), (r_in,) = exchange_wait(h_chips, "scatter_in_chips_wait", after=[dx], copies=len(OTHER_CHIPS))
    g_w_in, d_w_in, nm_w_in, nv_w_in = adamw_reduce(
        r_in, chip_sums, me // 2, w_in[0], m_in, v_in, "adamw_w_in")

    d_lb = jnp.stack([dl0.reshape(GDN_WIDTH), dl1.reshape(GDN_WIDTH)])
    small_shapes = [(1, N_HEADS), (1, N_HEADS), (1, HEAD_DIM), (2, GDN_WIDTH), (1, HEAD_DIM), (1, D_MODEL),
                    (1, D_MODEL), (D_MODEL,), (4, QKV_WIDTH), ()]
    small = _pack([d_alog_l[:, 0, 0], d_dt_l[:, 0, 0], d_gnw, d_lb, d_hnw, d_mix, d_ffn, d_final, d_conv_full,
                   loss_sum[0, 0]])
    red = allreduce_small(small, "allreduce_small")
    g_alog, g_dt, g_gnw, g_lb, g_hnw, g_mix, g_ffn, g_final, g_conv_full, loss = _unpack(red, small_shapes)
    g_conv = lax.dynamic_slice(g_conv_full, (0, me * shard_conv), (4, shard_conv)).reshape(1, 4, shard_conv)
    small_g = [g_alog, g_dt, g_gnw, g_lb, g_hnw, g_mix, g_ffn, g_final, g_conv]
    small_w = [gdn_a_log, gdn_dt_bias, gdn_norm_w, hgrn_lb_logits, hgrn_norm_w, norm_mix_w, norm_ffn_w, norm_final_w, conv_w]
    small_m = [m_gdn_a_log, m_gdn_dt_bias, m_gdn_norm_w, m_hgrn_lb_logits, m_hgrn_norm_w, m_norm_mix_w, m_norm_ffn_w,
               m_norm_final_w, m_conv_w]
    small_v = [v_gdn_a_log, v_gdn_dt_bias, v_gdn_norm_w, v_hgrn_lb_logits, v_hgrn_norm_w, v_norm_mix_w, v_norm_ffn_w,
               v_norm_final_w, v_conv_w]
    rows = lambda arrays: [a.reshape(-1, a.shape[-1]) for a in arrays]
    like_w = lambda arrays: [a.reshape(w.shape) for a, w in zip(arrays, small_w)]
    d_s, m_s, v_s = adamw_small(rows(small_w), rows(small_g), rows(small_m), rows(small_v), "adamw_small")
    d_alog, d_dt, d_gn, d_lbl, d_hn, d_nm, d_nf, d_nfin, d_cw = like_w(d_s)
    m_alog, m_dt, m_gn, m_lbl, m_hn, m_nm, m_nf, m_nfin, m_cw = like_w(m_s)
    v_alog, v_dt, v_gn, v_lbl, v_hn, v_nm, v_nf, v_nfin, v_cw = like_w(v_s)

    lead = lambda a: a[None]
    grads = [lead(g_w_in), g_conv, g_alog, g_dt, g_gnw, g_lb, g_hnw, lead(g_w_out), g_mix, g_ffn,
             lead(g_w_ff1), lead(g_w_ff2), g_final]
    deltas = [lead(d_w_in), d_cw, d_alog, d_dt, d_gn, d_lbl, d_hn, lead(d_w_out), d_nm, d_nf,
              lead(d_w_ff1), lead(d_w_ff2), d_nfin]
    new_m = [lead(nm_w_in), m_cw, m_alog, m_dt, m_gn, m_lbl, m_hn, lead(nm_w_out), m_nm, m_nf,
             lead(nm_w_ff1), lead(nm_w_ff2), m_nfin]
    new_v = [lead(nv_w_in), v_cw, v_alog, v_dt, v_gn, v_lbl, v_hn, lead(nv_w_out), v_nm, v_nf,
             lead(nv_w_ff1), lead(nv_w_ff2), v_nfin]
    return (loss, dx[None], *grads, *deltas, *new_m, *new_v)
```

```python
import functools

import jax
import jax.numpy as jnp
from jax import lax
from jax.experimental import pallas as pl
from jax.experimental.pallas import tpu as pltpu

F32 = jnp.float32
BF16 = jnp.bfloat16
HI = lax.Precision.HIGHEST

N_DEV = 8
D_MODEL = 2048
CHUNK = 64
SUB_CHUNK = 16
HEAD_DIM = 128
N_HEADS = 8
GDN_WIDTH = N_HEADS * HEAD_DIM
D_FF = 4 * D_MODEL
QKV_WIDTH = 3 * GDN_WIDTH
MAIN_WIDTH = 8 * GDN_WIDTH
CAT_WIDTH = MAIN_WIDTH + 128
AB_BLOCK = MAIN_WIDTH // 128
NORM_EPS = 1e-6
L2_EPS = 1e-6
LANES = 128
VMEM_LIMIT = 56 * 1024 * 1024

ADAM_LR = 0.001
ADAM_B1 = 0.9
ADAM_B2 = 0.999
ADAM_EPS = 1e-08
ADAM_WD = 0.01
ADAM_STEP = 10

MESH = pl.DeviceIdType.MESH


def _params(sem=None):
    return pltpu.CompilerParams(dimension_semantics=sem, vmem_limit_bytes=VMEM_LIMIT)


def _dot(a, b, dims, prec=None):
    return lax.dot_general(a, b, (dims, ((), ())), precision=prec, preferred_element_type=F32)


NN = ((1,), (0,))
NT = ((1,), (1,))
TN = ((0,), (0,))


def _split_bf16(x, pieces):
    out = []
    for _ in range(pieces - 1):
        p = x.astype(BF16)
        out.append(p)
        x = x - p.astype(F32)
    out.append(x.astype(BF16))
    return out


def _mm_raw(a, b, dims, prec):
    if prec == "hi":
        return _dot(a, b, dims, HI)
    if prec == "bf":
        return _dot(a.astype(BF16), b.astype(BF16), dims)
    a_hi, a_lo = _split_bf16(a, 2)
    b_hi, b_lo = _split_bf16(b, 2)
    return _dot(a_hi, b_hi, dims) + (_dot(a_hi, b_lo, dims) + _dot(a_lo, b_hi, dims))


@functools.partial(jax.custom_vjp, nondiff_argnums=(2, 3))
def mm(a, b, dims, prec):
    return _mm_raw(a, b, dims, prec)


def _mm_fwd(a, b, dims, prec):
    return _mm_raw(a, b, dims, prec), (a, b)


def _mm_bwd(dims, prec, res, ct):
    a, b = res
    if dims == NN:
        return _mm_raw(ct, b, NT, prec), _mm_raw(a, ct, TN, prec)
    if dims == NT:
        return _mm_raw(ct, b, NN, prec), _mm_raw(ct, a, TN, prec)
    return _mm_raw(b, ct, NT, prec), _mm_raw(a, ct, NN, prec)


mm.defvjp(_mm_fwd, _mm_bwd)


def _sel_raw(sel, x, dims):
    sel = sel.astype(BF16)
    p0, p1, p2 = _split_bf16(x, 3)
    return _dot(sel, p0, dims) + (_dot(sel, p1, dims) + _dot(sel, p2, dims))


def _sel_parts(sel, x):
    c = x.shape[0]
    full = _sel_raw(sel, x, NN)
    return tuple(full[i * c:(i + 1) * c] for i in range(sel.shape[0] // c))


@jax.custom_vjp
def sel_sums(sel, x):
    return _sel_parts(sel, x)


def _sel_fwd(sel, x):
    return _sel_parts(sel, x), sel


def _sel_bwd(sel, cts):
    return jnp.zeros_like(sel), _sel_raw(sel, jnp.concatenate(cts, axis=0), TN)


sel_sums.defvjp(_sel_fwd, _sel_bwd)


@jax.custom_vjp
def _known_value(computed, known):
    del computed
    return known


_known_value.defvjp(lambda computed, known: (known, None), lambda _, ct: (ct, jnp.zeros_like(ct)))


def _my_flat():
    return 4 * lax.axis_index("x") + 2 * lax.axis_index("y") + lax.axis_index("c")


def _peer(k):
    x, y, c = lax.axis_index("x"), lax.axis_index("y"), lax.axis_index("c")
    kx, ky, kc = (k >> 2) & 1, (k >> 1) & 1, k & 1
    px = (1 - x) if kx else x
    py = (1 - y) if ky else y
    pc = (1 - c) if kc else c
    return (px, py, pc), 4 * px + 2 * py + pc


HBM_SPEC = pl.BlockSpec(memory_space=pltpu.HBM)
SEM_SPEC = pl.BlockSpec(memory_space=pltpu.SEMAPHORE)
ANY_SPEC = pl.BlockSpec(memory_space=pl.ANY)
DATAFLOW = pltpu.SideEffectType.DATAFLOW_SIDE_EFFECTING


def _in_hbm(x):
    return pltpu.with_memory_space_constraint(x, pltpu.HBM)


ALL_PEERS = tuple(range(1, N_DEV))
CHIP_PEERS = (1, 2, 4, 6)
OTHER_CHIPS = (2, 4, 6)


def exchange_start(xs, gather, name, after=(), peers=ALL_PEERS):
    n, n_after = len(xs), len(after)

    def body(*refs):
        x_refs, land_refs = refs[:n], refs[n:2 * n]
        sems = refs[2 * n + n_after:2 * n + n_after + 2 * n]
        token = refs[-1]
        me = _my_flat()
        for k in peers:
            peer, peer_flat = _peer(k)
            for a in range(n):
                src = x_refs[a] if gather else x_refs[a].at[peer_flat]
                pltpu.make_async_remote_copy(src_ref=src, dst_ref=land_refs[a].at[me], send_sem=sems[a],
                                             recv_sem=sems[n + a], device_id=peer, device_id_type=MESH).start()
        token[...] = jnp.zeros_like(token)

    lands =[_in_hbm(lax.empty(((N_DEV,) + x.shape) if gather else x.shape, x.dtype)) for x in xs]
    hbm_out = [pltpu.HBM(x.shape, x.dtype) for x in xs] + [pltpu.HBM(l.shape, l.dtype) for l in lands]
    res = pl.pallas_call(
        body, name=name,
        out_shape=(*([pltpu.SemaphoreType.DMA(())] * (2 * n)), *hbm_out, jax.ShapeDtypeStruct((8, LANES), F32)),
        in_specs=[HBM_SPEC] * (2 * n) + [ANY_SPEC] * n_after,
        out_specs=(*([SEM_SPEC] * (2 * n)), *([HBM_SPEC] * (2 * n)), pl.BlockSpec(memory_space=pltpu.VMEM)),
        input_output_aliases={i: 2 * n + i for i in range(2 * n)},
        compiler_params=pltpu.CompilerParams(has_side_effects=DATAFLOW),
    )(*[_in_hbm(x) for x in xs], *lands, *after)
    return (list(res[:2 * n]), list(res[2 * n:3 * n]), list(res[3 * n:4 * n])), res[-1]


def forward_start(lands, name, after=()):
    n, n_after = len(lands), len(after)

    def body(*refs):
        land_refs = refs[:n]
        sems = refs[n + n_after:n + n_after + 2 * n]
        token = refs[-1]
        sibling, _ = _peer(1)
        for a in range(n):
            for k in OTHER_CHIPS:
                _, from_flat = _peer(k)
                slot = land_refs[a].at[from_flat]
                pltpu.make_async_remote_copy(src_ref=slot, dst_ref=slot, send_sem=sems[a], recv_sem=sems[n + a],
                                             device_id=sibling, device_id_type=MESH).start()
        token[...] = jnp.zeros_like(token)

    res = pl.pallas_call(
        body, name=name,
        out_shape=(*([pltpu.SemaphoreType.DMA(())] * (2 * n)), *[pltpu.HBM(l.shape, l.dtype) for l in lands],
                   jax.ShapeDtypeStruct((8, LANES), F32)),
        in_specs=[HBM_SPEC] * n + [ANY_SPEC] * n_after,
        out_specs=(*([SEM_SPEC] * (2 * n)), *([HBM_SPEC] * n), pl.BlockSpec(memory_space=pltpu.VMEM)),
        input_output_aliases={i: 2 * n + i for i in range(n)},
        compiler_params=pltpu.CompilerParams(has_side_effects=DATAFLOW),
    )(*lands, *after)
    return (list(res[:2 * n]), [], list(res[2 * n:3 * n])), res[-1]


def exchange_wait(handle, name, after=(), copies=N_DEV - 1):
    sems, xs, lands = handle
    n, n_x, n_after = len(lands), len(xs), len(after)

    def body(*refs):
        land_refs = refs[n_x:n_x + n]
        sem_refs = refs[n_x + n:n_x + 3 * n]
        for a in range(n):
            every = land_refs[a].at[pl.ds(0, copies)]
            cp = pltpu.make_async_remote_copy(src_ref=every, dst_ref=every, send_sem=sem_refs[a],
                                              recv_sem=sem_refs[n + a], device_id=_peer(1)[0], device_id_type=MESH)
            cp.wait_send()
            cp.wait_recv()

    res = pl.pallas_call(
        body, name=name,
        out_shape=[pltpu.HBM(x.shape, x.dtype) for x in xs] + [pltpu.HBM(l.shape, l.dtype) for l in lands],
        in_specs=[HBM_SPEC] * (n_x + n) + [SEM_SPEC] * (2 * n) + [ANY_SPEC] * n_after,
        out_specs=[HBM_SPEC] * (n_x + n),
        input_output_aliases={i: i for i in range(n_x + n)},
        compiler_params=pltpu.CompilerParams(has_side_effects=DATAFLOW),
    )(*xs, *lands, *sems, *after)
    return list(res[:n_x]), list(res[n_x:])


N_CHIPS = N_DEV // 2


def routed_start(x, routes, name, after=()):
    n_after = len(after)

    def body(*refs):
        x_ref, land_ref = refs[0], refs[1]
        send_sem, recv_sem = refs[2 + n_after], refs[3 + n_after]
        token = refs[-1]
        for src, dst, peer in routes():
            pltpu.make_async_remote_copy(src_ref=x_ref.at[src], dst_ref=land_ref.at[dst], send_sem=send_sem,
                                         recv_sem=recv_sem, device_id=peer, device_id_type=MESH).start()
        token[...] = jnp.zeros_like(token)

    land = _in_hbm(lax.empty((N_CHIPS,) + x.shape[1:], x.dtype))
    res = pl.pallas_call(
        body, name=name,
        out_shape=(pltpu.SemaphoreType.DMA(()), pltpu.SemaphoreType.DMA(()), pltpu.HBM(x.shape, x.dtype),
                   pltpu.HBM(land.shape, land.dtype), jax.ShapeDtypeStruct((8, LANES), F32)),
        in_specs=[HBM_SPEC, HBM_SPEC] + [ANY_SPEC] * n_after,
        out_specs=(SEM_SPEC, SEM_SPEC, HBM_SPEC, HBM_SPEC, pl.BlockSpec(memory_space=pltpu.VMEM)),
        input_output_aliases={0: 2, 1: 3},
        compiler_params=pltpu.CompilerParams(has_side_effects=DATAFLOW),
    )(_in_hbm(x), land, *after)
    return ([res[0], res[1]], [res[2]], [res[3]]), res[-1]


def _to_sibling_routes():
    c = lax.axis_index("c")
    sibling, _ = _peer(1)
    return [(2 * chip + 1 - c, chip, sibling) for chip in range(N_CHIPS)]


def _to_chips_routes():
    my_chip = _my_flat() // 2
    routes = []
    for k in OTHER_CHIPS:
        peer, peer_flat = _peer(k)
        routes.append((peer_flat // 2, my_chip, peer))
    return routes


def pair_sum(p, from_sibling, name, rb=1024):
    _, r, c = p.shape
    mine = lax.axis_index("c").astype(jnp.int32).reshape(1)

    def body(kind_ref, p_ref, s_ref, o_ref):
        del kind_ref
        o_ref[...] = (p_ref[...].astype(F32) + s_ref[...].astype(F32)).astype(BF16)

    return pl.pallas_call(
        body, name=name,
        grid_spec=pltpu.PrefetchScalarGridSpec(
            num_scalar_prefetch=1, grid=(N_CHIPS, r // rb),
            in_specs=[pl.BlockSpec((None, None, rb, c), lambda chip, i, kind: (chip, kind[0], i, 0)),
                      pl.BlockSpec((None, rb, c), lambda chip, i, kind: (chip, i, 0))],
            out_specs=pl.BlockSpec((None, rb, c), lambda chip, i, kind: (chip, i, 0))),
        out_shape=jax.ShapeDtypeStruct((N_CHIPS, r, c), BF16),
        compiler_params=_params(("parallel", "parallel")))(mine, p.reshape(N_CHIPS, 2, r, c), from_sibling)


def _one(handle, a):
    sems, xs, lands = handle
    n = len(lands)
    return [sems[a], sems[n + a]], xs[a:a + 1], [lands[a]]


def _own_slot(land, block):
    return lax.dynamic_update_slice(land, block[None], (_my_flat(),) + (0,) * block.ndim)


def allreduce_small(x, name):
    rows = x.shape[0]

    def body(x_ref, o_ref, buf, send_sems, recv_sems):
        me = _my_flat()
        buf[me] = x_ref[...]
        sends = []
        for k in range(1, N_DEV):
            peer, _ = _peer(k)
            cp = pltpu.make_async_remote_copy(
                src_ref=x_ref, dst_ref=buf.at[me], send_sem=send_sems.at[k], recv_sem=recv_sems.at[k],
                device_id=peer, device_id_type=MESH)
            cp.start()
            sends.append(cp)
        for k in range(1, N_DEV):
            peer, peer_flat = _peer(k)
            pltpu.make_async_remote_copy(
                src_ref=x_ref, dst_ref=buf.at[peer_flat], send_sem=send_sems.at[k], recv_sem=recv_sems.at[k],
                device_id=peer, device_id_type=MESH).wait_recv()
        for cp in sends:
            cp.wait_send()
        acc = buf[0]
        for d in range(1, N_DEV):
            acc = acc + buf[d]
        o_ref[...] = acc

    vmem = pl.BlockSpec(memory_space=pltpu.VMEM)
    return pl.pallas_call(
        body, name=name, out_shape=jax.ShapeDtypeStruct((rows, LANES), F32),
        in_specs=[vmem], out_specs=vmem,
        scratch_shapes=[pltpu.VMEM((N_DEV, rows, LANES), F32),
                        pltpu.SemaphoreType.DMA((N_DEV,)), pltpu.SemaphoreType.DMA((N_DEV,))],
    )(x)


def matmul(a, b, mode, name, out_dtypes=(F32,), epilogue=None, extra=None, tm=1024, tn=1024, tk=2048, after=(),
           b_shards=False, out_shards=False, k_group=1, k_blocks=None):
    if b_shards:
        n_sh, b_rows, b_cols = b.shape
    if mode == "nn":
        (m, kd), n = a.shape, (n_sh * b_cols if b_shards else b.shape[1])
        if b_shards:
            tn = b_cols
    elif mode == "nt":
        (m, kd), n = a.shape, (b_rows if b_shards else b.shape[0])
        if b_shards:
            tk = k_group * b_cols
    else:
        (kd, m), n = a.shape, b.shape[1]
    tm, tn, tk = min(tm, m), min(tn, n), min(tk, kd)
    assert m % tm == 0 and n % tn == 0 and kd % tk == 0, (name, m, n, kd, tm, tn, tk)
    k0, ksteps = (0, kd // tk) if k_blocks is None else k_blocks
    dims = {"nn": NN, "nt": NT, "tn": TN}[mode]
    n_out = len(out_dtypes)
    n_in = 2 + (extra is not None) + len(after)

    def finish(acc, e_ref, o_refs):
        outs = (acc,) if epilogue is None else epilogue(acc, e_ref[...] if e_ref is not None else None)
        for o_ref, o in zip(o_refs, outs):
            o_ref[...] = o.astype(o_ref.dtype)

    def product(a_ref, b_ref):
        if mode == "nt" and b_shards:
            w = b_cols
            parts = [_dot(a_ref[:, s * w:(s + 1) * w], b_ref[s], dims) for s in range(k_group)]
            return functools.reduce(lambda p, q: p + q, parts)
        return _dot(a_ref[...], b_ref[...], dims)

    def body(*refs):
        a_ref, b_ref = refs[0], refs[1]
        e_ref = refs[2] if extra is not None else None
        o_refs = refs[n_in:n_in + n_out]
        if ksteps == 1:
            finish(product(a_ref, b_ref), e_ref, o_refs)
            return
        acc_ref = refs[-1]
        kk = pl.program_id(2)

        @pl.when(kk == 0)
        def _():
            acc_ref[...] = jnp.zeros_like(acc_ref)

        acc_ref[...] += product(a_ref, b_ref)

        @pl.when(kk == ksteps - 1)
        def _():
            finish(acc_ref[...], e_ref, o_refs)

    if mode == "nn":
        a_spec = pl.BlockSpec((tm, tk), lambda i, j, k: (i, k0 + k))
        b_spec = (pl.BlockSpec((None, tk, tn), lambda i, j, k: (j, k, 0)) if b_shards
                  else pl.BlockSpec((tk, tn), lambda i, j, k: (k0 + k, j)))
    elif mode == "nt":
        a_spec = pl.BlockSpec((tm, tk), lambda i, j, k: (i, k))
        b_spec = (pl.BlockSpec((k_group, tn, b_cols), lambda i, j, k: (k, j, 0)) if b_shards
                  else pl.BlockSpec((tn, tk), lambda i, j, k: (j, k)))
    else:
        a_spec = pl.BlockSpec((tk, tm), lambda i, j, k: (k, i))
        b_spec = pl.BlockSpec((tk, tn), lambda i, j, k: (k, j))
    o_spec = pl.BlockSpec((tm, tn), lambda i, j, k: (i, j))
    res_spec = pl.BlockSpec((None, tm, tn), lambda i, j, k: (j, i, 0)) if out_shards else o_spec
    res_shape = (n // tn, m, tn) if out_shards else (m, n)
    in_specs = [a_spec, b_spec] + ([o_spec] if extra is not None else []) + [ANY_SPEC] * len(after)
    args = (a, b) + ((extra,) if extra is not None else ()) + tuple(after)
    res = pl.pallas_call(
        body, name=name, grid=(m // tm, n // tn, ksteps),
        in_specs=in_specs, out_specs=[res_spec] * n_out,
        out_shape=[jax.ShapeDtypeStruct(res_shape, dt) for dt in out_dtypes],
        scratch_shapes=[pltpu.VMEM((tm, tn), F32)] if ksteps > 1 else [],
        compiler_params=_params(("parallel", "parallel", "arbitrary")),
    )(*args)
    return res if n_out > 1 else res[0]


GATE_COL = 4 * GDN_WIDTH
RELAYOUT_ROWS = 256


def _cat_of_win(j):
    if j < GATE_COL:
        return j
    if j < GATE_COL + 2 * N_HEADS:
        return MAIN_WIDTH + (j - GATE_COL)
    return j - 2 * N_HEADS


def _win_of_cat(c):
    if c < GATE_COL:
        return c
    if c < MAIN_WIDTH:
        return c + 2 * N_HEADS
    if c < MAIN_WIDTH + 2 * N_HEADS:
        return GATE_COL + (c - MAIN_WIDTH)
    return None


def _runs(first, count, mapping):
    runs, i = [], 0
    while i < count:
        start, n = mapping(first + i), 1
        while i + n < count and mapping(first + i + n) == start + n:
            n += 1
        runs.append((start, n))
        i += n
    return runs


def weights_to_cat(g_in, name, total_rows, row0=0, into=None):
    n_dev, rows, shard = g_in.shape
    first = row0 // RELAYOUT_ROWS

    def body(x_ref, *rest):
        o_ref = rest[-1]
        for b in range(CAT_WIDTH // LANES):
            live = sum(_win_of_cat(LANES * b + i) is not None for i in range(LANES))
            parts = []
            for start, n in _runs(LANES * b, live, _win_of_cat):
                while n > 0:
                    d, o = divmod(start, shard)
                    take = min(n, shard - o)
                    parts.append(x_ref[d, :, o:o + take])
                    start, n = start + take, n - take
            if live < LANES:
                parts.append(jnp.zeros((RELAYOUT_ROWS, LANES - live), g_in.dtype))
            o_ref[:, LANES * b:LANES * (b + 1)] = parts[0] if len(parts) == 1 else jnp.concatenate(parts, axis=1)

    return pl.pallas_call(
        body, name=name, grid=(rows // RELAYOUT_ROWS,),
        in_specs=[pl.BlockSpec((n_dev, RELAYOUT_ROWS, shard), lambda i: (0, i, 0))] + ([ANY_SPEC] if into is not None else []),
        out_specs=pl.BlockSpec((RELAYOUT_ROWS, CAT_WIDTH), lambda i: (first + i, 0)),
        out_shape=jax.ShapeDtypeStruct((total_rows, CAT_WIDTH), g_in.dtype),
        input_output_aliases={1: 0} if into is not None else {},
        compiler_params=_params(("parallel",)))(*((g_in,) if into is None else (g_in, into)))


def cat_to_shards(dw_cat, shard):
    rows = dw_cat.shape[0]

    def body(x_ref, o_ref):
        for d in range(N_DEV):
            for t0 in range(0, shard, LANES):
                width = min(LANES, shard - t0)
                parts = [x_ref[:, c:c + n] for c, n in _runs(d * shard + t0, width, _cat_of_win)]
                o_ref[d, :, t0:t0 + width] = parts[0] if len(parts) == 1 else jnp.concatenate(parts, axis=1)

    return pl.pallas_call(
        body, name="cat_to_shards", grid=(rows // RELAYOUT_ROWS,),
        in_specs=[pl.BlockSpec((RELAYOUT_ROWS, CAT_WIDTH), lambda i: (i, 0))],
        out_specs=pl.BlockSpec((N_DEV, RELAYOUT_ROWS, shard), lambda i: (0, i, 0)),
        out_shape=jax.ShapeDtypeStruct((N_DEV, rows, shard), dw_cat.dtype),
        compiler_params=_params(("parallel",)))(dw_cat)


ROW_BLOCK = 512


def rms_fwd(x, w, name):
    t, d = x.shape

    def body(x_ref, w_ref, n_ref, r_ref):
        h = x_ref[...]
        r = lax.rsqrt(jnp.mean(h * h, axis=-1, keepdims=True) + NORM_EPS)
        n_ref[...] = (h * r * w_ref[...]).astype(BF16)
        r_ref[...] = r

    row = pl.BlockSpec((ROW_BLOCK, d), lambda i: (i, 0))
    return pl.pallas_call(
        body, name=name, grid=(t // ROW_BLOCK,),
        in_specs=[row, pl.BlockSpec((1, d), lambda i: (0, 0))],
        out_specs=[row, pl.BlockSpec((ROW_BLOCK, 1), lambda i: (i, 0))],
        out_shape=[jax.ShapeDtypeStruct((t, d), BF16), jax.ShapeDtypeStruct((t, 1), F32)],
        compiler_params=_params(("parallel",)))(x, w)


FUSED_ROWS = 512


def out_proj_rms(y, w_out, x, w_norm, name):
    t, d = x.shape

    def body(y_ref, w_ref, x_ref, g_ref, h_ref, n_ref, r_ref):
        h = x_ref[...] + _dot(y_ref[...], w_ref[...], NN)
        r = lax.rsqrt(jnp.mean(h * h, axis=-1, keepdims=True) + NORM_EPS)
        h_ref[...] = h
        n_ref[...] = (h * r * g_ref[...]).astype(BF16)
        r_ref[...] = r

    row = pl.BlockSpec((FUSED_ROWS, d), lambda i: (i, 0))
    return pl.pallas_call(
        body, name=name, grid=(t // FUSED_ROWS,),
        in_specs=[pl.BlockSpec((FUSED_ROWS, y.shape[1]), lambda i: (i, 0)), pl.BlockSpec(w_out.shape, lambda i: (0, 0)),
                  row, pl.BlockSpec((1, d), lambda i: (0, 0))],
        out_specs=[row, row, pl.BlockSpec((FUSED_ROWS, 1), lambda i: (i, 0))],
        out_shape=[jax.ShapeDtypeStruct((t, d), F32), jax.ShapeDtypeStruct((t, d), BF16),
                   jax.ShapeDtypeStruct((t, 1), F32)],
        compiler_params=_params(("parallel",)))(y, w_out, x, w_norm)


def ff2_loss(act, w_ff2, h1, w, target, name, tk=2048):
    t, d = h1.shape
    ksteps = act.shape[1] // tk

    def body(a_ref, b_ref, h_ref, w_ref, t_ref, loss_ref, dhb_ref, dw_ref, acc_ref):
        i, kk = pl.program_id(0), pl.program_id(1)

        @pl.when((i == 0) & (kk == 0))
        def _():
            loss_ref[...] = jnp.zeros_like(loss_ref)
            dw_ref[...] = jnp.zeros_like(dw_ref)

        @pl.when(kk == 0)
        def _():
            acc_ref[...] = h_ref[...]

        acc_ref[...] += _dot(a_ref[...], b_ref[...], NN)

        @pl.when(kk == ksteps - 1)
        def _():
            h = acc_ref[...]
            wv = w_ref[...]
            r = lax.rsqrt(jnp.mean(h * h, axis=-1, keepdims=True) + NORM_EPS)
            yn = h * r
            e = yn * wv - t_ref[...]
            loss_ref[...] += 0.5 * jnp.sum(jnp.sum(e * e, axis=-1, keepdims=True), axis=0, keepdims=True) / d
            dy = e / d
            dw_ref[...] += jnp.sum(dy * yn, axis=0, keepdims=True)
            dyn = dy * wv
            dhb_ref[...] = (r * (dyn - yn * jnp.mean(dyn * yn, axis=-1, keepdims=True))).astype(BF16)

    row = pl.BlockSpec((FUSED_ROWS, d), lambda i, k: (i, 0))
    wspec = pl.BlockSpec((1, d), lambda i, k: (0, 0))
    return pl.pallas_call(
        body, name=name, grid=(t // FUSED_ROWS, ksteps),
        in_specs=[pl.BlockSpec((FUSED_ROWS, tk), lambda i, k: (i, k)), pl.BlockSpec((tk, d), lambda i, k: (k, 0)),
                  row, wspec, row],
        out_specs=[pl.BlockSpec((1, 1), lambda i, k: (0, 0)), row, wspec],
        out_shape=[jax.ShapeDtypeStruct((1, 1), F32), jax.ShapeDtypeStruct((t, d), BF16),
                   jax.ShapeDtypeStruct((1, d), F32)],
        scratch_shapes=[pltpu.VMEM((FUSED_ROWS, d), F32)],
        compiler_params=_params(("arbitrary", "arbitrary")))(act, w_ff2, h1, w, target)


def rms_bwd(h, r, w, dn, dres, out_dtype, name):
    t, d = h.shape

    def body(h_ref, r_ref, w_ref, dn_ref, dres_ref, dh_ref, dw_ref):
        @pl.when(pl.program_id(0) == 0)
        def _():
            dw_ref[...] = jnp.zeros_like(dw_ref)

        rv = r_ref[...]
        yn = h_ref[...] * rv
        dnv = dn_ref[...].astype(F32)
        dw_ref[...] += jnp.sum(dnv * yn, axis=0, keepdims=True)
        dyn = dnv * w_ref[...]
        dh = dres_ref[...].astype(F32) + rv * (dyn - yn * jnp.mean(dyn * yn, axis=-1, keepdims=True))
        dh_ref[...] = dh.astype(out_dtype)

    row = pl.BlockSpec((ROW_BLOCK, d), lambda i: (i, 0))
    wspec = pl.BlockSpec((1, d), lambda i: (0, 0))
    rspec = pl.BlockSpec((ROW_BLOCK, 1), lambda i: (i, 0))
    return pl.pallas_call(
        body, name=name, grid=(t // ROW_BLOCK,),
        in_specs=[row, rspec, wspec, row, row], out_specs=[row, wspec],
        out_shape=[jax.ShapeDtypeStruct((t, d), out_dtype), jax.ShapeDtypeStruct((1, d), F32)],
        compiler_params=_params(("arbitrary",)))(h, r, w, dn, dres)


CONV_ROWS = 512
TILE_ROWS = 8


def _iota2(shape, axis):
    return lax.broadcasted_iota(jnp.int32, shape, axis)


def _silu(x):
    return x * jax.nn.sigmoid(x)


def _conv_rows(x_ref, w, first, rows):
    acc = None
    for j in range(4):
        term = x_ref[first - 3 + j:first - 3 + j + rows, :] * w[j:j + 1, :]
        acc = term if acc is None else acc + term
    return acc


def _head_shifts(head):
    rows = _iota2((TILE_ROWS, 1), 0)
    return [jnp.where(rows >= 3 - j, head if j == 3 else pltpu.roll(head, 3 - j, 0), 0.0) for j in range(4)]


def _conv_chunks(t):
    pieces = [(TILE_ROWS, min(CONV_ROWS, t) - TILE_ROWS)]
    pieces += [(r, CONV_ROWS) for r in range(CONV_ROWS, t, CONV_ROWS)]
    return pieces


def conv_fwd(proj, conv_w, name):
    t = proj.shape[0]

    def body(x_ref, w_ref, o_ref):
        w = w_ref[...]
        shifted = _head_shifts(x_ref[0:TILE_ROWS, :])
        o_ref[0:TILE_ROWS, :] = _silu(sum(shifted[j] * w[j:j + 1, :] for j in range(4)))
        for first, rows in _conv_chunks(t):
            o_ref[first:first + rows, :] = _silu(_conv_rows(x_ref, w, first, rows))

    col = pl.BlockSpec((t, LANES), lambda c: (0, c))
    return pl.pallas_call(
        body, name=name, grid=(QKV_WIDTH // LANES,),
        in_specs=[col, pl.BlockSpec((4, LANES), lambda c: (0, c))], out_specs=col,
        out_shape=jax.ShapeDtypeStruct((t, QKV_WIDTH), F32),
        compiler_params=_params(("parallel",)))(proj, conv_w)


def conv_bwd(proj, dout, conv_w, dproj, name):
    t = proj.shape[0]

    def dsilu(pre):
        sg = jax.nn.sigmoid(pre)
        return sg * (1.0 + pre * (1.0 - sg))

    def body(x_ref, d_ref, w_ref, dproj_in, dx_ref, dw_ref, stage):
        del dproj_in
        w = w_ref[...]
        shifted = _head_shifts(x_ref[0:TILE_ROWS, :])
        head_dpre = d_ref[0:TILE_ROWS, :] * dsilu(sum(shifted[j] * w[j:j + 1, :] for j in range(4)))
        stage[0:TILE_ROWS, :] = head_dpre
        for first, rows in _conv_chunks(t):
            stage[first:first + rows, :] = d_ref[first:first + rows, :] * dsilu(_conv_rows(x_ref, w, first, rows))
        stage[t:t + TILE_ROWS, :] = jnp.zeros((TILE_ROWS, LANES), F32)
        for first, rows in [(0, TILE_ROWS)] + _conv_chunks(t):
            dx = None
            for j in range(4):
                term = stage[first + 3 - j:first + 3 - j + rows, :] * w[j:j + 1, :]
                dx = term if dx is None else dx + term
            dx_ref[first:first + rows, :] = dx.astype(BF16)
        dw = [jnp.sum(head_dpre * shifted[j], axis=0, keepdims=True) for j in range(4)]
        for first, rows in _conv_chunks(t):
            dpre = stage[first:first + rows, :]
            for j in range(4):
                dw[j] = dw[j] + jnp.sum(dpre * x_ref[first - 3 + j:first - 3 + j + rows, :], axis=0, keepdims=True)
        dw_ref[...] = jnp.concatenate(dw, axis=0)

    col = pl.BlockSpec((t, LANES), lambda c: (0, c))
    taps = pl.BlockSpec((4, LANES), lambda c: (0, c))
    return pl.pallas_call(
        body, name=name, grid=(QKV_WIDTH // LANES,),
        in_specs=[col, col, taps, ANY_SPEC], out_specs=[col, taps],
        out_shape=[jax.ShapeDtypeStruct(dproj.shape, BF16), jax.ShapeDtypeStruct((4, QKV_WIDTH), F32)],
        scratch_shapes=[pltpu.VMEM((t + TILE_ROWS, LANES), F32)],
        input_output_aliases={3: 0},
        compiler_params=_params(("parallel",)))(proj, dout, conv_w, dproj)


def _softplus(x):
    return jnp.maximum(x, 0.0) + jnp.log(1.0 + jnp.exp(-jnp.abs(x)))


def _head_norm_gate(o, norm_w, gate):
    return o * lax.rsqrt(jnp.mean(o * o, axis=-1, keepdims=True) + NORM_EPS) * norm_w * _silu(gate)


GDN_PREC = ("bf", "bf")
HGRN_PREC = "bf"


def _each(fn, *cols):
    return [fn(*a) for a in zip(*cols)]


@functools.partial(jax.custom_vjp, nondiff_argnums=(2,))
def _known_inverse(low, inv, prec):
    del low, prec
    return inv


def _known_inverse_fwd(low, inv, prec):
    del low
    return inv, inv


def _known_inverse_bwd(prec, inv, ct):
    return -_mm_raw(_mm_raw(inv, ct, TN, prec), inv, NT, prec), jnp.zeros_like(inv)


_known_inverse.defvjp(_known_inverse_fwd, _known_inverse_bwd)


def gdn_stages(hs, qc, kc, vc, zc, ab, a_log_l, dt_l, norm_w, s, prec=GDN_PREC, inv_known=None):
    p_inv, p_mm = prec
    c = CHUNK
    ri, ci = _iota2((c, c), 0), _iota2((c, c), 1)
    incl, strict, eye = ri >= ci, ri > ci, ri == ci
    lane = _iota2((c, LANES), 1)
    last_row = _iota2((c, 1), 0) == c - 1
    rowsum = lambda x: jnp.sum(x, axis=1, keepdims=True)

    def row(col):
        return jnp.sum(jnp.where(eye, col, 0.0), axis=0, keepdims=True)

    q = _each(lambda x: x * lax.rsqrt(rowsum(x * x) + L2_EPS) * (HEAD_DIM ** -0.5), qc)
    k = _each(lambda x: x * lax.rsqrt(rowsum(x * x) + L2_EPS), kc)
    yield
    a_col = [rowsum(jnp.where(lane == h, ab, 0.0)) for h in hs]
    b_col = [rowsum(jnp.where(lane == h + N_HEADS, ab, 0.0)) for h in hs]
    beta = _each(jax.nn.sigmoid, b_col)
    g = _each(lambda a, al, dl: rowsum(jnp.where(lane == 0, -jnp.exp(al) * _softplus(a + dl), 0.0)), a_col, a_log_l, dt_l)
    gcum = _each(lambda x: rowsum(jnp.where(incl, row(x), 0.0)), g)
    g_last = _each(lambda x: jnp.sum(jnp.where(last_row, x, 0.0), axis=0, keepdims=True), gcum)
    decay = _each(lambda x: jnp.exp(jnp.where(incl, x - row(x), -jnp.inf)), gcum)
    yield
    kk = _each(lambda x: mm(x, x, NT, p_mm), k)
    low = _each(lambda b, x, d: jnp.where(strict, b * x * d, 0.0), beta, kk, decay)
    yield
    if inv_known is None:
        power = _each(lambda x: -x, low)
        inv = _each(lambda x: jnp.where(eye, 1.0, 0.0) + x, power)
        for _ in range(5):
            power = _each(lambda x: mm(x, x, NN, p_inv), power)
            yield
            inv = _each(lambda x, p: x + mm(x, p, NN, p_inv), inv, power)
            yield
    else:
        inv = _each(lambda x, known: _known_inverse(x, known, p_inv), low, inv_known)
    exp_g = _each(jnp.exp, gcum)
    yield
    u_v = _each(lambda i, b, x: mm(i, b * x, NN, p_mm), inv, beta, vc)
    w = _each(lambda i, b, e, x: mm(i, b * e * x, NN, p_mm), inv, beta, exp_g, k)
    yield
    attn = _each(lambda x, y, d: mm(x, y, NT, p_mm) * d, q, k, decay)
    yield
    u = _each(lambda x, y, z: x - mm(y, z, NN, p_mm), u_v, w, s)
    yield
    o = _each(lambda x, e, z: mm(x * e, z, NN, p_mm), q, exp_g, s)
    o = _each(lambda x, a, y: x + mm(a, y, NN, p_mm), o, attn, u)
    yield
    k_end = _each(lambda x, gl, gc: x * jnp.exp(gl - gc), k, g_last, gcum)
    s_new = _each(lambda z, gl, x, y: z * jnp.exp(gl) + mm(x, y, TN, p_mm), s, g_last, k_end, u)
    return (_each(lambda x, z: _head_norm_gate(x, norm_w, z), o, zc), s_new), inv


def gdn_chunk(h, qc, kc, vc, zc, ab, a_log_l, dt_l, norm_w, s, prec=GDN_PREC, reuse_inverse=False):
    args = ([h], [qc], [kc], [vc], [zc], ab, [a_log_l], [dt_l], norm_w, [s], prec)
    if reuse_inverse:
        inv = lax.stop_gradient(gdn_chunks(*args)[1])
        (y, s_new), _ = gdn_chunks(*args, inv_known=inv)
    else:
        (y, s_new), _ = gdn_chunks(*args)
    return y[0], s_new[0]


DIAG_ROWS = SUB_CHUNK // 2
SHIFT_PAD = 8
SHIFT_ROWS = SHIFT_PAD + CHUNK + SHIFT_PAD
SHIFT_WAYS = 4


class RolledRows:
    def down(self, x, which):
        del which
        return [x] + [pltpu.roll(x, off, 0) for off in range(1, DIAG_ROWS)]

    def up_sum(self, parts, which):
        del which
        acc = parts[0]
        for off in range(1, DIAG_ROWS):
            acc = acc + pltpu.roll(parts[off], CHUNK - off, 0)
        return acc


class SlotRows:
    def __init__(self, slots):
        self.slots = slots

    def down(self, x, which):
        self.slots[which, 0, SHIFT_PAD:SHIFT_PAD + CHUNK, :] = x
        return [x] + [self.slots[which, 0, SHIFT_PAD - off:SHIFT_PAD + CHUNK - off, :] for off in range(1, DIAG_ROWS)]

    def up_sum(self, parts, which):
        acc = parts[0]
        for off in range(1, DIAG_ROWS):
            way = 1 + off % (SHIFT_WAYS - 1)
            self.slots[which, way, SHIFT_PAD:SHIFT_PAD + CHUNK, :] = parts[off]
            acc = acc + self.slots[which, way, SHIFT_PAD + off:SHIFT_PAD + CHUNK + off, :]
        return acc


def _sub_block_rows():
    return jnp.bitwise_and(_iota2((CHUNK, 1), 0), DIAG_ROWS - 1)


def _diag_forward(rows, q, key, bc, v):
    rmod = _sub_block_rows()
    k_d, b_d, v_d = rows.down(key, 0), rows.down(bc, 1), rows.down(v, 2)
    o = None
    for off in range(DIAG_ROWS):
        e = jnp.exp(jnp.where(rmod >= off, bc - b_d[off], -jnp.inf))
        term = jnp.sum(q * k_d[off] * e, axis=-1, keepdims=True) * v_d[off]
        o = term if o is None else o + term
    return o


def _diag_backward(rows, q, key, bc, v, do):
    rmod = _sub_block_rows()
    k_d, b_d, v_d = rows.down(key, 0), rows.down(bc, 1), rows.down(v, 2)
    dq = db = None
    dk_parts, db_parts, dv_parts = [], [], []
    for off in range(DIAG_ROWS):
        e = jnp.exp(jnp.where(rmod >= off, bc - b_d[off], -jnp.inf))
        qe = q * e
        a = jnp.sum(qe * k_d[off], axis=-1, keepdims=True)
        da = jnp.sum(do * v_d[off], axis=-1, keepdims=True)
        dv_parts.append(a * do)
        dq_term = (da * e) * k_d[off]
        dk_term = da * qe
        s = dk_term * k_d[off]
        dq = dq_term if dq is None else dq + dq_term
        db = s if db is None else db + s
        dk_parts.append(dk_term)
        db_parts.append(s)
    return dq, rows.up_sum(dk_parts, 0), db - rows.up_sum(db_parts, 1), rows.up_sum(dv_parts, 2)


def diag_part(rows, differentiable=True):
    forward = functools.partial(_diag_forward, rows)
    if not differentiable:
        return forward
    part = jax.custom_vjp(forward)
    part.defvjp(lambda q, key, bc, v: (forward(q, key, bc, v), (q, key, bc, v)),
                lambda res, do: _diag_backward(rows, *res, do))
    return part


def hgrn_stages(qb, fb, ib, gb, l0, l1, norm_w, st, prec=HGRN_PREC, diags=None, o_known=None):
    c = CHUNK
    ri, ci = _iota2((4 * c, c), 0), _iota2((4 * c, c), 1)
    rcol = _iota2((c, 1), 0)
    blk0 = jnp.bitwise_and(ri, c - SUB_CHUNK)
    limit = jnp.where(ri < c, ri + 1, jnp.where(ri < 2 * c, blk0, jnp.where(ri < 3 * c, blk0 + SUB_CHUNK,
                                                                          blk0 + DIAG_ROWS)))
    sel = jnp.where(ci < limit, 1.0, 0.0)
    ri, ci = _iota2((c, c), 0), _iota2((c, c), 1)
    lb = _each(lambda a, b: jax.nn.sigmoid(a - b), l0, l1)
    forget = _each(lambda b, f: b + (1.0 - b) * jax.nn.sigmoid(f), lb, fb)
    key = _each(lambda b, f: (1.0 - b) * jax.nn.sigmoid(-f), lb, fb)
    q = _each(_silu, qb)
    v = ib
    logf = _each(jnp.log, forget)
    sums = _each(lambda x: sel_sums(sel, x), logf)
    bc, b_start, b_end, b_half = ([x[i] for x in sums] for i in range(4))
    b_last = _each(lambda x: jnp.sum(x, axis=0, keepdims=True), logf)
    o = _each(lambda x, b, z: mm(x * jnp.exp(b), z, NT, prec), q, bc, st)
    if diags is None:
        diags = [diag_part(RolledRows())] * len(qb)
    yield
    o = list(o)
    for h in range(len(o)):
        o[h] = o[h] + diags[h](q[h], key[h], bc[h], v[h])
        yield
    second = jnp.bitwise_and(rcol, SUB_CHUNK - 1) >= DIAG_ROWS
    same_sub = jnp.bitwise_and(ri, c - SUB_CHUNK) == jnp.bitwise_and(ci, c - SUB_CHUNK)
    q_half = _each(lambda x, b, bh: x * jnp.exp(jnp.where(second, b - bh, -jnp.inf)), q, bc, b_half)
    k_half = _each(lambda x, b, bh: x * jnp.exp(jnp.where(second, -jnp.inf, bh - b)), key, bc, b_half)
    a_half = _each(lambda x, z: jnp.where(same_sub, mm(x, z, NT, prec), 0.0), q_half, k_half)
    o = _each(lambda acc, a, val: acc + mm(a, val, NN, prec), o, a_half, v)
    yield
    q_rel = _each(lambda x, b, bs: x * jnp.exp(b - bs), q, bc, b_start)
    k_rel = _each(lambda x, b, be: x * jnp.exp(be - b), key, bc, b_end)
    for y in range(c // SUB_CHUNK - 1):
        def scaled(x, b, bs):
            end_y = jnp.sum(jnp.where(rcol == SUB_CHUNK * y + SUB_CHUNK - 1, b, 0.0), axis=0, keepdims=True)
            return x * jnp.exp(jnp.where(rcol >= SUB_CHUNK * (y + 1), bs - end_y, -jnp.inf))
        dq = _each(scaled, q_rel, bc, b_start)
        in_y = (ci >= SUB_CHUNK * y) & (ci < SUB_CHUNK * (y + 1))
        a_y = _each(lambda x, z: jnp.where(in_y, mm(x, z, NT, prec), 0.0), dq, k_rel)
        o = _each(lambda acc, a, val: acc + mm(a, val, NN, prec), o, a_y, v)
        yield
    k_state = _each(lambda x, bl, b: x * jnp.exp(bl - b), key, b_last, bc)
    st_new = _each(lambda z, bl, val, x: z * jnp.exp(bl) + mm(val, x, TN, prec), st, b_last, v, k_state)
    if o_known is not None:
        o = _each(_known_value, o, o_known)
    return (_each(lambda x, z: _head_norm_gate(x, norm_w, z), o, gb), st_new), o


def _drain(gen):
    try:
        while True:
            next(gen)
    except StopIteration as done:
        return done.value


def _alternate(gen_a, gen_b):
    out, live = [None, None], [gen_a, gen_b]
    while any(g is not None for g in live):
        for i, g in enumerate(live):
            if g is None:
                continue
            try:
                next(g)
            except StopIteration as done:
                out[i], live[i] = done.value, None
    return out


def gdn_chunks(*args, **kwargs):
    return _drain(gdn_stages(*args, **kwargs))


def hgrn_chunks(*args, **kwargs):
    return _drain(hgrn_stages(*args, **kwargs))


def hgrn_chunk(qb, fb, ib, gb, l0, l1, norm_w, st, prec=HGRN_PREC, reuse_output=False):
    args = ([qb], [fb], [ib], [gb], [l0], [l1], norm_w, [st], prec)
    if reuse_output:
        known = lax.stop_gradient(hgrn_chunks(*args)[1])
        (y, st_new), _ = hgrn_chunks(*args, o_known=known)
    else:
        (y, st_new), _ = hgrn_chunks(*args)
    return y[0], st_new[0]


HEAD_VEC = (N_HEADS, 1, LANES)


class _ChunkSpecs:
    def __init__(self, nc, rev):
        self.nc, self.rev = nc, rev

    def _c(self, c):
        return self.nc - 1 - c if self.rev else c

    def row(self, width, block=0):
        return pl.BlockSpec((CHUNK, width), lambda c: (self._c(c), block))

    def per_head(self, rows):
        return pl.BlockSpec((None, N_HEADS, rows, rows), lambda c: (self._c(c), 0, 0, 0))

    @staticmethod
    def whole(shape):
        return pl.BlockSpec(shape, lambda c: (0,) * len(shape))


def _lanes(j):
    return slice(j * LANES, (j + 1) * LANES)


def mixer_fwd(qkv_c, proj, a_log_l, dt_l, gdn_norm_w, l0, l1, hgrn_norm_w, name):
    t = qkv_c.shape[0]
    hb = N_HEADS
    sp = _ChunkSpecs(t // CHUNK, rev=False)
    hs = list(range(hb))

    def body(q_ref, k_ref, v_ref, z_ref, ab_ref, al_ref, dt_ref, gnw_ref, qb_ref, fb_ref, ib_ref, gb_ref, l0_ref, l1_ref,
             hnw_ref, y_ref, hist_a_ref, inv_ref, hist_b_ref, o_ref, sa_ref, sb_ref, shift_ref):
        @pl.when(pl.program_id(0) == 0)
        def _():
            sa_ref[...] = jnp.zeros_like(sa_ref)
            sb_ref[...] = jnp.zeros_like(sb_ref)
            shift_ref[...] = jnp.zeros_like(shift_ref)

        heads = lambda ref: [ref[:, _lanes(j)] for j in hs]
        s_a, s_b = [sa_ref[h] for h in hs], [sb_ref[h] for h in hs]
        for h in hs:
            hist_a_ref[h] = s_a[h]
            hist_b_ref[h] = s_b[h]
        diags = [diag_part(SlotRows(shift_ref.at[h]), differentiable=False) for h in hs]
        ((y_a, s_a_new), inv), ((y_b, s_b_new), o_pre) = _alternate(
            gdn_stages(hs, heads(q_ref), heads(k_ref), heads(v_ref), heads(z_ref), ab_ref[...],
                       [al_ref[h] for h in hs], [dt_ref[h] for h in hs], gnw_ref[...], s_a),
            hgrn_stages(heads(qb_ref), heads(fb_ref), heads(ib_ref), heads(gb_ref),
                        [l0_ref[h] for h in hs], [l1_ref[h] for h in hs], hnw_ref[...], s_b, diags=diags))
        for h in hs:
            y_ref[:, _lanes(h)] = y_a[h].astype(BF16)
            y_ref[:, _lanes(hb + h)] = y_b[h].astype(BF16)
            o_ref[:, _lanes(h)] = o_pre[h]
            sa_ref[h] = s_a_new[h]
            sb_ref[h] = s_b_new[h]
            inv_ref[h] = inv[h]

    vec, gain, slab = sp.whole(HEAD_VEC), sp.whole((1, LANES)), functools.partial(sp.row, GDN_WIDTH)
    states = jax.ShapeDtypeStruct((sp.nc, N_HEADS, HEAD_DIM, HEAD_DIM), F32)
    return pl.pallas_call(
        body, name=name, grid=(sp.nc,),
        in_specs=[slab(0), slab(1), slab(2), slab(3), sp.row(LANES, AB_BLOCK), vec, vec, gain,
                  slab(4), slab(5), slab(6), slab(7), vec, vec, gain],
        out_specs=[sp.row(2 * GDN_WIDTH), sp.per_head(HEAD_DIM), sp.per_head(CHUNK), sp.per_head(HEAD_DIM), slab(0)],
        out_shape=[jax.ShapeDtypeStruct((t, 2 * GDN_WIDTH), BF16), states,
                   jax.ShapeDtypeStruct((sp.nc, N_HEADS, CHUNK, CHUNK), F32), states,
                   jax.ShapeDtypeStruct((t, GDN_WIDTH), F32)],
        scratch_shapes=[pltpu.VMEM((N_HEADS, HEAD_DIM, HEAD_DIM), F32), pltpu.VMEM((N_HEADS, HEAD_DIM, HEAD_DIM), F32),
                        pltpu.VMEM((hb, 3, SHIFT_WAYS, SHIFT_ROWS, LANES), F32)],
        compiler_params=_params(("arbitrary",)),
    )(qkv_c, qkv_c, qkv_c, proj, proj, a_log_l, dt_l, gdn_norm_w, proj, proj, proj, proj, l0, l1, hgrn_norm_w)


def mixer_bwd(qkv_c, proj, a_log_l, dt_l, gdn_norm_w, l0, l1, hgrn_norm_w, hist_a, inv_hist, hist_b, o_pre, dy, name):
    t = qkv_c.shape[0]
    hb = N_HEADS
    sp = _ChunkSpecs(t // CHUNK, rev=True)
    hs = list(range(hb))

    def body(q_ref, k_ref, v_ref, z_ref, ab_ref, al_ref, dt_ref, gnw_ref, qb_ref, fb_ref, ib_ref, gb_ref, l0_ref, l1_ref,
             hnw_ref, hist_a_ref, inv_ref, hist_b_ref, o_ref, dy_ref,
             dqkv_ref, dproj_ref, dal_ref, ddt_ref, dgnw_ref, dl0_ref, dl1_ref, dhnw_ref, dsa_ref, dsb_ref, shift_ref):
        @pl.when(pl.program_id(0) == 0)
        def _():
            for ref in (dal_ref, ddt_ref, dgnw_ref, dl0_ref, dl1_ref, dhnw_ref, dsa_ref, dsb_ref, shift_ref):
                ref[...] = jnp.zeros_like(ref)

        heads = lambda ref, first=0: [ref[:, _lanes(first + j)] for j in hs]
        diags = [diag_part(SlotRows(shift_ref.at[h])) for h in hs]
        inv_known, o_known = [inv_ref[h] for h in hs], heads(o_ref)

        def both(ga, gb):
            (ra, inv), (rb, o_pre) = _alternate(gdn_stages(hs, *ga, inv_known=inv_known),
                                                hgrn_stages(*gb, diags=diags, o_known=o_known))
            return (ra, rb), (inv, o_pre)

        ga = (heads(q_ref), heads(k_ref), heads(v_ref), heads(z_ref), ab_ref[...], [al_ref[h] for h in hs],
              [dt_ref[h] for h in hs], gnw_ref[...], [hist_a_ref[h] for h in hs])
        gb = (heads(qb_ref), heads(fb_ref), heads(ib_ref), heads(gb_ref), [l0_ref[h] for h in hs],
              [l1_ref[h] for h in hs], hnw_ref[...], [hist_b_ref[h] for h in hs])
        _, vjp, _ = jax.vjp(both, ga, gb, has_aux=True)
        dy_a = [x.astype(F32) for x in heads(dy_ref)]
        dy_b = [x.astype(F32) for x in heads(dy_ref, hb)]
        (dq, dk, dv, dz, dab, dal, ddt, dgnw, ds_a), (dqb, dfb, dib, dgb, dl0, dl1, dhnw, ds_b) = vjp(
            ((dy_a, [dsa_ref[h] for h in hs]), (dy_b, [dsb_ref[h] for h in hs])))
        for h in hs:
            dqkv_ref[:, _lanes(h)] = dq[h]
            dqkv_ref[:, _lanes(hb + h)] = dk[h]
            dqkv_ref[:, _lanes(2 * hb + h)] = dv[h]
            for slab, val in enumerate((dz, dqb, dfb, dib, dgb)):
                dproj_ref[:, _lanes((3 + slab) * hb + h)] = val[h].astype(BF16)
            dal_ref[h] += dal[h]
            ddt_ref[h] += ddt[h]
            dl0_ref[h] += dl0[h]
            dl1_ref[h] += dl1[h]
            dsa_ref[h] = ds_a[h]
            dsb_ref[h] = ds_b[h]
        dproj_ref[:, MAIN_WIDTH:] = dab.astype(BF16)
        dgnw_ref[...] += dgnw
        dhnw_ref[...] += dhnw

    vec, gain, slab = sp.whole(HEAD_VEC), sp.whole((1, LANES)), functools.partial(sp.row, GDN_WIDTH)
    vec_shape, gain_shape = jax.ShapeDtypeStruct(HEAD_VEC, F32), jax.ShapeDtypeStruct((1, LANES), F32)
    return pl.pallas_call(
        body, name=name, grid=(sp.nc,),
        in_specs=[slab(0), slab(1), slab(2), slab(3), sp.row(LANES, AB_BLOCK), vec, vec, gain,
                  slab(4), slab(5), slab(6), slab(7), vec, vec, gain,
                  sp.per_head(HEAD_DIM), sp.per_head(CHUNK), sp.per_head(HEAD_DIM), slab(0), sp.row(2 * GDN_WIDTH)],
        out_specs=[sp.row(QKV_WIDTH), sp.row(CAT_WIDTH), vec, vec, gain, vec, vec, gain],
        out_shape=[jax.ShapeDtypeStruct((t, QKV_WIDTH), F32), jax.ShapeDtypeStruct((t, CAT_WIDTH), BF16),
                   vec_shape, vec_shape, gain_shape, vec_shape, vec_shape, gain_shape],
        scratch_shapes=[pltpu.VMEM((N_HEADS, HEAD_DIM, HEAD_DIM), F32), pltpu.VMEM((N_HEADS, HEAD_DIM, HEAD_DIM), F32),
                        pltpu.VMEM((hb, 3, SHIFT_WAYS, SHIFT_ROWS, LANES), F32)],
        compiler_params=_params(("arbitrary",)),
    )(qkv_c, qkv_c, qkv_c, proj, proj, a_log_l, dt_l, gdn_norm_w, proj, proj, proj, proj, l0, l1, hgrn_norm_w,
      hist_a, inv_hist, hist_b, o_pre, dy)


def _adamw(w, g, m, v):
    m = ADAM_B1 * m + (1.0 - ADAM_B1) * g
    v = ADAM_B2 * v + (1.0 - ADAM_B2) * jnp.square(g)
    m_hat = m / (1.0 - ADAM_B1 ** ADAM_STEP)
    v_hat = v / (1.0 - ADAM_B2 ** ADAM_STEP)
    delta = -ADAM_LR * (m_hat / (jnp.sqrt(v_hat) + ADAM_EPS) + ADAM_WD * w)
    return delta, m, v


def adamw_reduce(parts, mine, slot, w, m, v, name, rb=128):
    r, c = w.shape
    rb = min(rb, r)
    n_parts = parts.shape[0]

    def body(slot_ref, p_ref, own_ref, w_ref, m_ref, v_ref, g_ref, d_ref, mo_ref, vo_ref):
        part = lambda d: jnp.where(slot_ref[0] == d, own_ref[...], p_ref[d]).astype(F32)
        g = part(0)
        for d in range(1, n_parts):
            g = g + part(d)
        delta, mn, vn = _adamw(w_ref[...], g, m_ref[...], v_ref[...])
        g_ref[...] = g
        d_ref[...] = delta
        mo_ref[...] = mn
        vo_ref[...] = vn

    blk = pl.BlockSpec((rb, c), lambda i, s: (i, 0))
    return pl.pallas_call(
        body, name=name,
        grid_spec=pltpu.PrefetchScalarGridSpec(
            num_scalar_prefetch=1, grid=(r // rb,),
            in_specs=[pl.BlockSpec((n_parts, rb, c), lambda i, s: (0, i, 0)),
                      pl.BlockSpec((None, rb, c), lambda i, s: (s[0], i, 0)), blk, blk, blk],
            out_specs=[blk] * 4),
        out_shape=[jax.ShapeDtypeStruct((r, c), F32)] * 4,
        compiler_params=_params(("parallel",)))(slot.astype(jnp.int32).reshape(1), parts, mine, w, m, v)


def adamw_small(ws, gs, ms, vs, name):
    n = len(ws)

    def body(*refs):
        for i in range(n):
            w_ref, g_ref, m_ref, v_ref = (refs[j * n + i] for j in range(4))
            outs = _adamw(w_ref[...], g_ref[...], m_ref[...], v_ref[...])
            for j, o in enumerate(outs):
                refs[(4 + j) * n + i][...] = o

    vmem = pl.BlockSpec(memory_space=pltpu.VMEM)
    res = pl.pallas_call(body, name=name, in_specs=[vmem] * (4 * n), out_specs=[vmem] * (3 * n),
                         out_shape=[jax.ShapeDtypeStruct(w.shape, F32) for w in ws] * 3)(*ws, *gs, *ms, *vs)
    return res[:n], res[n:2 * n], res[2 * n:]


def _pack(arrays):
    flat = jnp.concatenate([a.reshape(-1).astype(F32) for a in arrays])
    rows = -(-flat.shape[0] // (8 * LANES)) * 8
    return jnp.pad(flat, (0, rows * LANES - flat.shape[0])).reshape(rows, LANES)


def _unpack(packed, shapes):
    flat, out, off = packed.reshape(-1), [], 0
    for s in shapes:
        n = 1
        for d in s:
            n *= d
        out.append(flat[off:off + n].reshape(s))
        off += n
    return out


def _relu2_epilogue(acc, _):
    r = jnp.maximum(acc, 0.0)
    return acc, r * r


def _relu2_bwd_epilogue(acc, a1):
    return (acc * (2.0 * jnp.maximum(a1, 0.0)),)


def kernel(x, w_in, conv_w, gdn_a_log, gdn_dt_bias, gdn_norm_w, hgrn_lb_logits, hgrn_norm_w, w_out, norm_mix_w, norm_ffn_w, w_ff1, w_ff2, norm_final_w, loss_target, m_w_in, m_conv_w, m_gdn_a_log, m_gdn_dt_bias, m_gdn_norm_w, m_hgrn_lb_logits, m_hgrn_norm_w, m_w_out, m_norm_mix_w, m_norm_ffn_w, m_w_ff1, m_w_ff2, m_norm_final_w, v_w_in, v_conv_w, v_gdn_a_log, v_gdn_dt_bias, v_gdn_norm_w, v_hgrn_lb_logits, v_hgrn_norm_w, v_w_out, v_norm_mix_w, v_norm_ffn_w, v_w_ff1, v_w_ff2, v_norm_final_w):
    me = _my_flat()
    xs = x[0]
    target = loss_target[0]
    shard_in = w_in.shape[2]
    shard_conv = conv_w.shape[2]

    tok = lambda t: t[0:1, 0:1]

    half = D_MODEL // 2
    w_in_b = w_in[0].astype(BF16)
    h_ga, t_ga = exchange_start([w_in_b[:half], conv_w[0]], True, "gather_w_in_high_start", peers=CHIP_PEERS)
    h_g0, t_g0 = exchange_start([w_in_b[half:]], True, "gather_w_in_low_start", after=[t_ga], peers=CHIP_PEERS)
    behind = lambda a: lax.optimization_barrier((a, t_g0))[0]
    h_g1, t_g1 = exchange_start([behind(w_out[0]).astype(BF16), behind(w_ff1[0]).astype(BF16)], True,
                                "gather_mid_start", after=[t_g0], peers=CHIP_PEERS)
    h_g2, t_g2 = exchange_start([behind(w_ff2[0]).astype(BF16)], True, "gather_ff2_start", after=[t_g1],
                                peers=CHIP_PEERS)
    m_in, v_in, _ = lax.optimization_barrier((m_w_in, v_w_in, t_g2))
    m_in, v_in = m_in[0], v_in[0]

    lane_b = lambda p: jnp.broadcast_to(p.reshape(N_HEADS, 1, 1), HEAD_VEC)
    a_log_l, dt_l = lane_b(gdn_a_log[0]), lane_b(gdn_dt_bias[0])
    l0 = hgrn_lb_logits[0].reshape(HEAD_VEC)
    l1 = hgrn_lb_logits[1].reshape(HEAD_VEC)

    n1, r1 = rms_fwd(xs, norm_mix_w + tok(t_g1) + tok(t_g2), "rms_mix")
    (s_high, s_conv), (l_high, l_conv) = exchange_wait(h_ga, "gather_w_in_high_wait", after=[n1, m_in, v_in],
                                                       copies=len(CHIP_PEERS))
    h_fa, _ = forward_start([l_high, l_conv], "gather_w_in_high_forward")
    _, (l_high, l_conv) = exchange_wait(h_fa, "forward_w_in_high_wait", copies=len(OTHER_CHIPS))
    w_cat = weights_to_cat(_own_slot(l_high, s_high), "weights_to_cat", D_MODEL)
    conv_full = jnp.transpose(_own_slot(l_conv, s_conv), (1, 0, 2)).reshape(4, QKV_WIDTH)
    proj = matmul(n1, w_cat, "nn", "in_proj_high", (BF16,), tn=CAT_WIDTH // 5, tk=half, k_blocks=(0, 1))
    (s_low,), (l_low,) = exchange_wait(h_g0, "gather_w_in_low_wait", after=[proj], copies=len(CHIP_PEERS))
    h_f0, _ = forward_start([l_low], "gather_w_in_low_forward")
    _, (l_low,) = exchange_wait(_one(h_f0, 0), "forward_w_in_low_wait", copies=len(OTHER_CHIPS))
    w_cat = weights_to_cat(_own_slot(l_low, s_low), "weights_to_cat_low", D_MODEL, row0=half, into=w_cat)
    proj = matmul(n1, w_cat, "nn", "in_proj_low", tn=CAT_WIDTH // 5, tk=half, k_blocks=(1, 1), extra=proj,
                  epilogue=lambda acc, high: (acc + high,))
    qkv_c = conv_fwd(proj, conv_full, "conv_fwd")
    y, hist_a, inv_a, hist_b, o_b = mixer_fwd(qkv_c, proj, a_log_l, dt_l, gdn_norm_w, l0, l1, hgrn_norm_w, "mixer_fwd")
    (s_out, s_ff1), (l_out, l_ff1) = exchange_wait(h_g1, "gather_mid_wait", after=[y], copies=len(CHIP_PEERS))
    (s_ff2,), (l_ff2,) = exchange_wait(h_g2, "gather_ff2_wait", after=[y], copies=len(CHIP_PEERS))
    h_fw, _ = forward_start([l_out, l_ff1, l_ff2], "gather_forward_start")
    _, (l_out,) = exchange_wait(_one(h_fw, 0), "forward_out_wait", copies=len(OTHER_CHIPS))
    w_out_full = _own_slot(l_out, s_out).reshape(D_MODEL, D_MODEL)
    h1, n2, r2 = out_proj_rms(y, w_out_full, xs, norm_ffn_w, "out_proj_rms")
    _, (l_ff1,) = exchange_wait(_one(h_fw, 1), "forward_ff1_wait", after=[n2], copies=len(OTHER_CHIPS))
    w_ff1_sh = _own_slot(l_ff1, s_ff1)
    a1, act = matmul(n2, w_ff1_sh, "nn", "ff1", out_dtypes=(F32, BF16), epilogue=_relu2_epilogue, b_shards=True)
    _, (l_ff2,) = exchange_wait(_one(h_fw, 2), "forward_ff2_wait", after=[act], copies=len(OTHER_CHIPS))
    w_ff2_full = _own_slot(l_ff2, s_ff2).reshape(D_FF, D_MODEL)
    loss_sum, dh2_b, d_final = ff2_loss(act, w_ff2_full, h1, norm_final_w.reshape(1, D_MODEL), target, "ff2_loss")

    da1 = matmul(dh2_b, w_ff2_full, "nt", "d_act", out_dtypes=(BF16,), epilogue=_relu2_bwd_epilogue, extra=a1)
    t_all = xs.shape[0]
    dw_ff2 = matmul(act, dh2_b, "tn", "dw_ff2", out_dtypes=(BF16,), tk=t_all)
    p_ff2 = dw_ff2.reshape(N_DEV, D_FF // N_DEV, D_MODEL)
    h_s1, t_s1 = exchange_start([p_ff2], False, "scatter_ff2_start")
    dn2 = matmul(da1, w_ff1_sh, "nt", "d_n2", out_dtypes=(BF16,), after=[t_s1], b_shards=True, k_group=4)
    p_ff1 = matmul(n2, da1, "tn", "dw_ff1", out_dtypes=(BF16,), tn=D_FF // N_DEV, tk=t_all, after=[t_s1], out_shards=True)
    h_s2, t_s2 = exchange_start([p_ff1], False, "scatter_ff1_start")
    dh1_b, d_ffn = rms_bwd(h1, r2, norm_ffn_w + tok(t_s2), dn2, dh2_b, BF16, "rms_ffn_bwd")
    dmix = matmul(dh1_b, w_out_full, "nt", "d_mix", out_dtypes=(BF16,))
    dw_out = matmul(y, dh1_b, "tn", "dw_out", out_dtypes=(BF16,), tk=t_all)
    p_out = dw_out.reshape(N_DEV, D_MODEL // N_DEV, D_MODEL)
    h_s3, t_s3 = exchange_start([p_out], False, "scatter_out_start")
    d_qkv_c, dproj, d_alog_l, d_dt_l, d_gnw, dl0, dl1, d_hnw = mixer_bwd(
        qkv_c, proj, a_log_l, dt_l, gdn_norm_w + tok(t_s3), l0, l1, hgrn_norm_w, hist_a, inv_a, hist_b, o_b, dmix,
        "mixer_bwd")
    dproj, d_conv_full = conv_bwd(proj, d_qkv_c, conv_full, dproj, "conv_bwd")
    dw_cat = matmul(n1, dproj, "tn", "dw_in", out_dtypes=(BF16,), tm=512, tn=CAT_WIDTH // 5, tk=t_all)
    p_in = cat_to_shards(dw_cat, shard_in)
    h_pair, t_s4 = routed_start(p_in, _to_sibling_routes, "scatter_in_pair_start")

    (s_ff2g,), (r_ff2,) = exchange_wait(h_s1, "scatter_ff2_wait", after=[t_s4])
    g_w_ff2, d_w_ff2, nm_w_ff2, nv_w_ff2 = adamw_reduce(
        r_ff2, s_ff2g, me, w_ff2[0], m_w_ff2[0], v_w_ff2[0], "adamw_w_ff2")
    (p_in,), (from_sibling,) = exchange_wait(h_pair, "scatter_in_pair_wait", after=[d_w_ff2], copies=N_CHIPS)
    chip_sums = pair_sum(p_in, from_sibling, "scatter_in_pair_sum")
    h_chips, t_s5 = routed_start(chip_sums, _to_chips_routes, "scatter_in_chips_start")
    (s_ff1g,), (r_ff1,) = exchange_wait(h_s2, "scatter_ff1_wait", after=[t_s5])
    (s_outg,), (r_out,) = exchange_wait(h_s3, "scatter_out_wait", after=[t_s5])
    g_w_ff1, d_w_ff1, nm_w_ff1, nv_w_ff1 = adamw_reduce(
        r_ff1, s_ff1g, me, w_ff1[0], m_w_ff1[0], v_w_ff1[0], "adamw_w_ff1")
    g_w_out, d_w_out, nm_w_out, nv_w_out = adamw_reduce(
        r_out, s_outg, me, w_out[0], m_w_out[0], v_w_out[0], "adamw_w_out")
    dn1 = matmul(dproj, w_cat, "nt", "d_n1", out_dtypes=(BF16,), tm=512, tn=512, tk=CAT_WIDTH, after=[t_s5])
    (chip_sums,), (r_in,) = exchange_wait(h_chips, "scatter_in_chips_wait", after=[dn1, d_w_ff1, d_w_out],
                                          copies=len(OTHER_CHIPS))
    g_w_in, d_w_in, nm_w_in, nv_w_in = adamw_reduce(
        r_in, chip_sums, me // 2, w_in[0], m_in, v_in, "adamw_w_in")
    w_mix_late = lax.optimization_barrier((norm_mix_w, d_w_in))[0]
    dx, d_mix = rms_bwd(xs, r1, w_mix_late, dn1, dh1_b, F32, "rms_mix_bwd")

    d_lb = jnp.stack([dl0.reshape(GDN_WIDTH), dl1.reshape(GDN_WIDTH)])
    small_shapes = [(1, N_HEADS), (1, N_HEADS), (1, HEAD_DIM), (2, GDN_WIDTH), (1, HEAD_DIM), (1, D_MODEL),
                    (1, D_MODEL), (D_MODEL,), (4, QKV_WIDTH), ()]
    small = _pack([d_alog_l[:, 0, 0], d_dt_l[:, 0, 0], d_gnw, d_lb, d_hnw, d_mix, d_ffn, d_final, d_conv_full,
                   loss_sum[0, 0]])
    red = allreduce_small(small, "allreduce_small")
    g_alog, g_dt, g_gnw, g_lb, g_hnw, g_mix, g_ffn, g_final, g_conv_full, loss = _unpack(red, small_shapes)
    g_conv = lax.dynamic_slice(g_conv_full, (0, me * shard_conv), (4, shard_conv)).reshape(1, 4, shard_conv)
    small_g = [g_alog, g_dt, g_gnw, g_lb, g_hnw, g_mix, g_ffn, g_final, g_conv]
    small_w = [gdn_a_log, gdn_dt_bias, gdn_norm_w, hgrn_lb_logits, hgrn_norm_w, norm_mix_w, norm_ffn_w, norm_final_w, conv_w]
    small_m = [m_gdn_a_log, m_gdn_dt_bias, m_gdn_norm_w, m_hgrn_lb_logits, m_hgrn_norm_w, m_norm_mix_w, m_norm_ffn_w,
               m_norm_final_w, m_conv_w]
    small_v = [v_gdn_a_log, v_gdn_dt_bias, v_gdn_norm_w, v_hgrn_lb_logits, v_hgrn_norm_w, v_norm_mix_w, v_norm_ffn_w,
               v_norm_final_w, v_conv_w]
    rows = lambda arrays: [a.reshape(-1, a.shape[-1]) for a in arrays]
    like_w = lambda arrays: [a.reshape(w.shape) for a, w in zip(arrays, small_w)]
    d_s, m_s, v_s = adamw_small(rows(small_w), rows(small_g), rows(small_m), rows(small_v), "adamw_small")
    d_alog, d_dt, d_gn, d_lbl, d_hn, d_nm, d_nf, d_nfin, d_cw = like_w(d_s)
    m_alog, m_dt, m_gn, m_lbl, m_hn, m_nm, m_nf, m_nfin, m_cw = like_w(m_s)
    v_alog, v_dt, v_gn, v_lbl, v_hn, v_nm, v_nf, v_nfin, v_cw = like_w(v_s)

    lead = lambda a: a[None]
    grads = [lead(g_w_in), g_conv, g_alog, g_dt, g_gnw, g_lb, g_hnw, lead(g_w_out), g_mix, g_ffn,
             lead(g_w_ff1), lead(g_w_ff2), g_final]
    deltas = [lead(d_w_in), d_cw, d_alog, d_dt, d_gn, d_lbl, d_hn, lead(d_w_out), d_nm, d_nf,
              lead(d_w_ff1), lead(d_w_ff2), d_nfin]
    new_m = [lead(nm_w_in), m_cw, m_alog, m_dt, m_gn, m_lbl, m_hn, lead(nm_w_out), m_nm, m_nf,
             lead(nm_w_ff1), lead(nm_w_ff2), m_nfin]
    new_v = [lead(nv_w_in), v_cw, v_alog, v_dt, v_gn, v_lbl, v_hn, lead(nv_w_out), v_nm, v_nf,
             lead(nv_w_ff1), lead(nv_w_ff2), v_nfin]
    return (loss, dx[None], *grads, *deltas, *new_m, *new_v)
```

```python
import functools

import jax
import jax.numpy as jnp
from jax import lax
from jax.experimental import pallas as pl
from jax.experimental.pallas import tpu as pltpu

F32 = jnp.float32
BF16 = jnp.bfloat16
HI = lax.Precision.HIGHEST

N_DEV = 8
D_MODEL = 2048
CHUNK = 64
SUB_CHUNK = 16
HEAD_DIM = 128
N_HEADS = 8
GDN_WIDTH = N_HEADS * HEAD_DIM
D_FF = 4 * D_MODEL
QKV_WIDTH = 3 * GDN_WIDTH
MAIN_WIDTH = 8 * GDN_WIDTH
CAT_WIDTH = MAIN_WIDTH + 128
AB_BLOCK = MAIN_WIDTH // 128
NORM_EPS = 1e-6
L2_EPS = 1e-6
LANES = 128
VMEM_LIMIT = 56 * 1024 * 1024

ADAM_LR = 0.001
ADAM_B1 = 0.9
ADAM_B2 = 0.999
ADAM_EPS = 1e-08
ADAM_WD = 0.01
ADAM_STEP = 10

MESH = pl.DeviceIdType.MESH


def _params(sem=None):
    return pltpu.CompilerParams(dimension_semantics=sem, vmem_limit_bytes=VMEM_LIMIT)


def _dot(a, b, dims, prec=None):
    return lax.dot_general(a, b, (dims, ((), ())), precision=prec, preferred_element_type=F32)


NN = ((1,), (0,))
NT = ((1,), (1,))
TN = ((0,), (0,))


def _split_bf16(x, pieces):
    out = []
    for _ in range(pieces - 1):
        p = x.astype(BF16)
        out.append(p)
        x = x - p.astype(F32)
    out.append(x.astype(BF16))
    return out


def _mm_raw(a, b, dims, prec):
    if prec == "hi":
        return _dot(a, b, dims, HI)
    if prec == "bf":
        return _dot(a.astype(BF16), b.astype(BF16), dims)
    a_hi, a_lo = _split_bf16(a, 2)
    b_hi, b_lo = _split_bf16(b, 2)
    return _dot(a_hi, b_hi, dims) + (_dot(a_hi, b_lo, dims) + _dot(a_lo, b_hi, dims))


@functools.partial(jax.custom_vjp, nondiff_argnums=(2, 3))
def mm(a, b, dims, prec):
    return _mm_raw(a, b, dims, prec)


def _mm_fwd(a, b, dims, prec):
    return _mm_raw(a, b, dims, prec), (a, b)


def _mm_bwd(dims, prec, res, ct):
    a, b = res
    if dims == NN:
        return _mm_raw(ct, b, NT, prec), _mm_raw(a, ct, TN, prec)
    if dims == NT:
        return _mm_raw(ct, b, NN, prec), _mm_raw(ct, a, TN, prec)
    return _mm_raw(b, ct, NT, prec), _mm_raw(a, ct, NN, prec)


mm.defvjp(_mm_fwd, _mm_bwd)


def _sel_raw(sel, x, dims):
    sel = sel.astype(BF16)
    p0, p1, p2 = _split_bf16(x, 3)
    return _dot(sel, p0, dims) + (_dot(sel, p1, dims) + _dot(sel, p2, dims))


def _sel_parts(sel, x):
    c = x.shape[0]
    full = _sel_raw(sel, x, NN)
    return tuple(full[i * c:(i + 1) * c] for i in range(sel.shape[0] // c))


@jax.custom_vjp
def sel_sums(sel, x):
    return _sel_parts(sel, x)


def _sel_fwd(sel, x):
    return _sel_parts(sel, x), sel


def _sel_bwd(sel, cts):
    return jnp.zeros_like(sel), _sel_raw(sel, jnp.concatenate(cts, axis=0), TN)


sel_sums.defvjp(_sel_fwd, _sel_bwd)


@jax.custom_vjp
def _known_value(computed, known):
    del computed
    return known


_known_value.defvjp(lambda computed, known: (known, None), lambda _, ct: (ct, jnp.zeros_like(ct)))


def _my_flat():
    return 4 * lax.axis_index("x") + 2 * lax.axis_index("y") + lax.axis_index("c")


def _peer(k):
    x, y, c = lax.axis_index("x"), lax.axis_index("y"), lax.axis_index("c")
    kx, ky, kc = (k >> 2) & 1, (k >> 1) & 1, k & 1
    px = (1 - x) if kx else x
    py = (1 - y) if ky else y
    pc = (1 - c) if kc else c
    return (px, py, pc), 4 * px + 2 * py + pc


HBM_SPEC = pl.BlockSpec(memory_space=pltpu.HBM)
SEM_SPEC = pl.BlockSpec(memory_space=pltpu.SEMAPHORE)
ANY_SPEC = pl.BlockSpec(memory_space=pl.ANY)
DATAFLOW = pltpu.SideEffectType.DATAFLOW_SIDE_EFFECTING


def _in_hbm(x):
    return pltpu.with_memory_space_constraint(x, pltpu.HBM)


ALL_PEERS = tuple(range(1, N_DEV))
CHIP_PEERS = (1, 2, 4, 6)
OTHER_CHIPS = (2, 4, 6)


def exchange_start(xs, gather, name, after=(), peers=ALL_PEERS):
    n, n_after = len(xs), len(after)

    def body(*refs):
        x_refs, land_refs = refs[:n], refs[n:2 * n]
        sems = refs[2 * n + n_after:2 * n + n_after + 2 * n]
        token = refs[-1]
        me = _my_flat()
        for k in peers:
            peer, peer_flat = _peer(k)
            for a in range(n):
                src = x_refs[a] if gather else x_refs[a].at[peer_flat]
                pltpu.make_async_remote_copy(src_ref=src, dst_ref=land_refs[a].at[me], send_sem=sems[a],
                                             recv_sem=sems[n + a], device_id=peer, device_id_type=MESH).start()
        token[...] = jnp.zeros_like(token)

    lands =[_in_hbm(lax.empty(((N_DEV,) + x.shape) if gather else x.shape, x.dtype)) for x in xs]
    hbm_out = [pltpu.HBM(x.shape, x.dtype) for x in xs] + [pltpu.HBM(l.shape, l.dtype) for l in lands]
    res = pl.pallas_call(
        body, name=name,
        out_shape=(*([pltpu.SemaphoreType.DMA(())] * (2 * n)), *hbm_out, jax.ShapeDtypeStruct((8, LANES), F32)),
        in_specs=[HBM_SPEC] * (2 * n) + [ANY_SPEC] * n_after,
        out_specs=(*([SEM_SPEC] * (2 * n)), *([HBM_SPEC] * (2 * n)), pl.BlockSpec(memory_space=pltpu.VMEM)),
        input_output_aliases={i: 2 * n + i for i in range(2 * n)},
        compiler_params=pltpu.CompilerParams(has_side_effects=DATAFLOW),
    )(*[_in_hbm(x) for x in xs], *lands, *after)
    return (list(res[:2 * n]), list(res[2 * n:3 * n]), list(res[3 * n:4 * n])), res[-1]


def forward_start(lands, name, after=()):
    n, n_after = len(lands), len(after)

    def body(*refs):
        land_refs = refs[:n]
        sems = refs[n + n_after:n + n_after + 2 * n]
        token = refs[-1]
        sibling, _ = _peer(1)
        for a in range(n):
            for k in OTHER_CHIPS:
                _, from_flat = _peer(k)
                slot = land_refs[a].at[from_flat]
                pltpu.make_async_remote_copy(src_ref=slot, dst_ref=slot, send_sem=sems[a], recv_sem=sems[n + a],
                                             device_id=sibling, device_id_type=MESH).start()
        token[...] = jnp.zeros_like(token)

    res = pl.pallas_call(
        body, name=name,
        out_shape=(*([pltpu.SemaphoreType.DMA(())] * (2 * n)), *[pltpu.HBM(l.shape, l.dtype) for l in lands],
                   jax.ShapeDtypeStruct((8, LANES), F32)),
        in_specs=[HBM_SPEC] * n + [ANY_SPEC] * n_after,
        out_specs=(*([SEM_SPEC] * (2 * n)), *([HBM_SPEC] * n), pl.BlockSpec(memory_space=pltpu.VMEM)),
        input_output_aliases={i: 2 * n + i for i in range(n)},
        compiler_params=pltpu.CompilerParams(has_side_effects=DATAFLOW),
    )(*lands, *after)
    return (list(res[:2 * n]), [], list(res[2 * n:3 * n])), res[-1]


def exchange_wait(handle, name, after=(), copies=N_DEV - 1):
    sems, xs, lands = handle
    n, n_x, n_after = len(lands), len(xs), len(after)

    def body(*refs):
        land_refs = refs[n_x:n_x + n]
        sem_refs = refs[n_x + n:n_x + 3 * n]
        for a in range(n):
            every = land_refs[a].at[pl.ds(0, copies)]
            cp = pltpu.make_async_remote_copy(src_ref=every, dst_ref=every, send_sem=sem_refs[a],
                                              recv_sem=sem_refs[n + a], device_id=_peer(1)[0], device_id_type=MESH)
            cp.wait_send()
            cp.wait_recv()

    res = pl.pallas_call(
        body, name=name,
        out_shape=[pltpu.HBM(x.shape, x.dtype) for x in xs] + [pltpu.HBM(l.shape, l.dtype) for l in lands],
        in_specs=[HBM_SPEC] * (n_x + n) + [SEM_SPEC] * (2 * n) + [ANY_SPEC] * n_after,
        out_specs=[HBM_SPEC] * (n_x + n),
        input_output_aliases={i: i for i in range(n_x + n)},
        compiler_params=pltpu.CompilerParams(has_side_effects=DATAFLOW),
    )(*xs, *lands, *sems, *after)
    return list(res[:n_x]), list(res[n_x:])


N_CHIPS = N_DEV // 2


def routed_start(x, routes, name, after=()):
    n_after = len(after)

    def body(*refs):
        x_ref, land_ref = refs[0], refs[1]
        send_sem, recv_sem = refs[2 + n_after], refs[3 + n_after]
        token = refs[-1]
        for src, dst, peer in routes():
            pltpu.make_async_remote_copy(src_ref=x_ref.at[src], dst_ref=land_ref.at[dst], send_sem=send_sem,
                                         recv_sem=recv_sem, device_id=peer, device_id_type=MESH).start()
        token[...] = jnp.zeros_like(token)

    land = _in_hbm(lax.empty((N_CHIPS,) + x.shape[1:], x.dtype))
    res = pl.pallas_call(
        body, name=name,
        out_shape=(pltpu.SemaphoreType.DMA(()), pltpu.SemaphoreType.DMA(()), pltpu.HBM(x.shape, x.dtype),
                   pltpu.HBM(land.shape, land.dtype), jax.ShapeDtypeStruct((8, LANES), F32)),
        in_specs=[HBM_SPEC, HBM_SPEC] + [ANY_SPEC] * n_after,
        out_specs=(SEM_SPEC, SEM_SPEC, HBM_SPEC, HBM_SPEC, pl.BlockSpec(memory_space=pltpu.VMEM)),
        input_output_aliases={0: 2, 1: 3},
        compiler_params=pltpu.CompilerParams(has_side_effects=DATAFLOW),
    )(_in_hbm(x), land, *after)
    return ([res[0], res[1]], [res[2]], [res[3]]), res[-1]


def _to_sibling_routes():
    c = lax.axis_index("c")
    sibling, _ = _peer(1)
    return [(2 * chip + 1 - c, chip, sibling) for chip in range(N_CHIPS)]


def _to_chips_routes():
    my_chip = _my_flat() // 2
    routes = []
    for k in OTHER_CHIPS:
        peer, peer_flat = _peer(k)
        routes.append((peer_flat // 2, my_chip, peer))
    return routes


def pair_sum(p, from_sibling, name, rb=1024):
    _, r, c = p.shape
    mine = lax.axis_index("c").astype(jnp.int32).reshape(1)

    def body(kind_ref, p_ref, s_ref, o_ref):
        del kind_ref
        o_ref[...] = (p_ref[...].astype(F32) + s_ref[...].astype(F32)).astype(BF16)

    return pl.pallas_call(
        body, name=name,
        grid_spec=pltpu.PrefetchScalarGridSpec(
            num_scalar_prefetch=1, grid=(N_CHIPS, r // rb),
            in_specs=[pl.BlockSpec((None, None, rb, c), lambda chip, i, kind: (chip, kind[0], i, 0)),
                      pl.BlockSpec((None, rb, c), lambda chip, i, kind: (chip, i, 0))],
            out_specs=pl.BlockSpec((None, rb, c), lambda chip, i, kind: (chip, i, 0))),
        out_shape=jax.ShapeDtypeStruct((N_CHIPS, r, c), BF16),
        compiler_params=_params(("parallel", "parallel")))(mine, p.reshape(N_CHIPS, 2, r, c), from_sibling)


def _one(handle, a):
    sems, xs, lands = handle
    n = len(lands)
    return [sems[a], sems[n + a]], xs[a:a + 1], [lands[a]]


def _own_slot(land, block):
    return lax.dynamic_update_slice(land, block[None], (_my_flat(),) + (0,) * block.ndim)


def allreduce_small(x, name):
    rows = x.shape[0]

    def body(x_ref, o_ref, buf, send_sems, recv_sems):
        me = _my_flat()
        buf[me] = x_ref[...]
        sends = []
        for k in range(1, N_DEV):
            peer, _ = _peer(k)
            cp = pltpu.make_async_remote_copy(
                src_ref=x_ref, dst_ref=buf.at[me], send_sem=send_sems.at[k], recv_sem=recv_sems.at[k],
                device_id=peer, device_id_type=MESH)
            cp.start()
            sends.append(cp)
        for k in range(1, N_DEV):
            peer, peer_flat = _peer(k)
            pltpu.make_async_remote_copy(
                src_ref=x_ref, dst_ref=buf.at[peer_flat], send_sem=send_sems.at[k], recv_sem=recv_sems.at[k],
                device_id=peer, device_id_type=MESH).wait_recv()
        for cp in sends:
            cp.wait_send()
        acc = buf[0]
        for d in range(1, N_DEV):
            acc = acc + buf[d]
        o_ref[...] = acc

    vmem = pl.BlockSpec(memory_space=pltpu.VMEM)
    return pl.pallas_call(
        body, name=name, out_shape=jax.ShapeDtypeStruct((rows, LANES), F32),
        in_specs=[vmem], out_specs=vmem,
        scratch_shapes=[pltpu.VMEM((N_DEV, rows, LANES), F32),
                        pltpu.SemaphoreType.DMA((N_DEV,)), pltpu.SemaphoreType.DMA((N_DEV,))],
    )(x)


def matmul(a, b, mode, name, out_dtypes=(F32,), epilogue=None, extra=None, tm=1024, tn=1024, tk=2048, after=(),
           b_shards=False, out_shards=False, k_group=1, k_blocks=None):
    if b_shards:
        n_sh, b_rows, b_cols = b.shape
    if mode == "nn":
        (m, kd), n = a.shape, (n_sh * b_cols if b_shards else b.shape[1])
        if b_shards:
            tn = b_cols
    elif mode == "nt":
        (m, kd), n = a.shape, (b_rows if b_shards else b.shape[0])
        if b_shards:
            tk = k_group * b_cols
    else:
        (kd, m), n = a.shape, b.shape[1]
    tm, tn, tk = min(tm, m), min(tn, n), min(tk, kd)
    assert m % tm == 0 and n % tn == 0 and kd % tk == 0, (name, m, n, kd, tm, tn, tk)
    k0, ksteps = (0, kd // tk) if k_blocks is None else k_blocks
    dims = {"nn": NN, "nt": NT, "tn": TN}[mode]
    n_out = len(out_dtypes)
    n_in = 2 + (extra is not None) + len(after)

    def finish(acc, e_ref, o_refs):
        outs = (acc,) if epilogue is None else epilogue(acc, e_ref[...] if e_ref is not None else None)
        for o_ref, o in zip(o_refs, outs):
            o_ref[...] = o.astype(o_ref.dtype)

    def product(a_ref, b_ref):
        if mode == "nt" and b_shards:
            w = b_cols
            parts = [_dot(a_ref[:, s * w:(s + 1) * w], b_ref[s], dims) for s in range(k_group)]
            return functools.reduce(lambda p, q: p + q, parts)
        return _dot(a_ref[...], b_ref[...], dims)

    def body(*refs):
        a_ref, b_ref = refs[0], refs[1]
        e_ref = refs[2] if extra is not None else None
        o_refs = refs[n_in:n_in + n_out]
        if ksteps == 1:
            finish(product(a_ref, b_ref), e_ref, o_refs)
            return
        acc_ref = refs[-1]
        kk = pl.program_id(2)

        @pl.when(kk == 0)
        def _():
            acc_ref[...] = jnp.zeros_like(acc_ref)

        acc_ref[...] += product(a_ref, b_ref)

        @pl.when(kk == ksteps - 1)
        def _():
            finish(acc_ref[...], e_ref, o_refs)

    if mode == "nn":
        a_spec = pl.BlockSpec((tm, tk), lambda i, j, k: (i, k0 + k))
        b_spec = (pl.BlockSpec((None, tk, tn), lambda i, j, k: (j, k, 0)) if b_shards
                  else pl.BlockSpec((tk, tn), lambda i, j, k: (k0 + k, j)))
    elif mode == "nt":
        a_spec = pl.BlockSpec((tm, tk), lambda i, j, k: (i, k))
        b_spec = (pl.BlockSpec((k_group, tn, b_cols), lambda i, j, k: (k, j, 0)) if b_shards
                  else pl.BlockSpec((tn, tk), lambda i, j, k: (j, k)))
    else:
        a_spec = pl.BlockSpec((tk, tm), lambda i, j, k: (k, i))
        b_spec = pl.BlockSpec((tk, tn), lambda i, j, k: (k, j))
    o_spec = pl.BlockSpec((tm, tn), lambda i, j, k: (i, j))
    res_spec = pl.BlockSpec((None, tm, tn), lambda i, j, k: (j, i, 0)) if out_shards else o_spec
    res_shape = (n // tn, m, tn) if out_shards else (m, n)
    in_specs = [a_spec, b_spec] + ([o_spec] if extra is not None else []) + [ANY_SPEC] * len(after)
    args = (a, b) + ((extra,) if extra is not None else ()) + tuple(after)
    res = pl.pallas_call(
        body, name=name, grid=(m // tm, n // tn, ksteps),
        in_specs=in_specs, out_specs=[res_spec] * n_out,
        out_shape=[jax.ShapeDtypeStruct(res_shape, dt) for dt in out_dtypes],
        scratch_shapes=[pltpu.VMEM((tm, tn), F32)] if ksteps > 1 else [],
        compiler_params=_params(("parallel", "parallel", "arbitrary")),
    )(*args)
    return res if n_out > 1 else res[0]


RING_SLOTS = 3


def matmul_ring(a, b, name, extra, epilogue, tm, tn, tk, k_block):
    m, n = a.shape[0], b.shape[1]
    nj = n // tn
    steps = (m // tm) * nj
    row0 = k_block * tk

    def body(a_ref, b_hbm, e_ref, o_ref, slots, sems):
        s = pl.program_id(0) * nj + pl.program_id(1)

        def fetch(step):
            col = pl.multiple_of((step % nj) * tn, LANES)
            slot = step % RING_SLOTS
            return pltpu.make_async_copy(b_hbm.at[pl.ds(row0, tk), pl.ds(col, tn)], slots.at[slot], sems.at[slot])

        @pl.when(s == 0)
        def _():
            fetch(s).start()
            fetch(s + 1).start()

        @pl.when(s + 2 < steps)
        def _():
            fetch(s + 2).start()

        fetch(s).wait()
        for k in range(RING_SLOTS):
            @pl.when(s % RING_SLOTS == k)
            def _():
                (out,) = epilogue(_dot(a_ref[...], slots[k], NN), e_ref[...])
                o_ref[...] = out.astype(o_ref.dtype)

    o_spec = pl.BlockSpec((tm, tn), lambda i, j: (i, j))
    return pl.pallas_call(
        body, name=name, grid=(m // tm, nj),
        in_specs=[pl.BlockSpec((tm, tk), lambda i, j: (i, k_block)), ANY_SPEC, o_spec], out_specs=o_spec,
        out_shape=jax.ShapeDtypeStruct((m, n), F32),
        scratch_shapes=[pltpu.VMEM((RING_SLOTS, tk, tn), b.dtype), pltpu.SemaphoreType.DMA((RING_SLOTS,))],
        compiler_params=_params(("arbitrary", "arbitrary")))(a, b, extra)


GATE_COL = 4 * GDN_WIDTH
RELAYOUT_ROWS = 256


def _cat_of_win(j):
    if j < GATE_COL:
        return j
    if j < GATE_COL + 2 * N_HEADS:
        return MAIN_WIDTH + (j - GATE_COL)
    return j - 2 * N_HEADS


def _win_of_cat(c):
    if c < GATE_COL:
        return c
    if c < MAIN_WIDTH:
        return c + 2 * N_HEADS
    if c < MAIN_WIDTH + 2 * N_HEADS:
        return GATE_COL + (c - MAIN_WIDTH)
    return None


def _runs(first, count, mapping):
    runs, i = [], 0
    while i < count:
        start, n = mapping(first + i), 1
        while i + n < count and mapping(first + i + n) == start + n:
            n += 1
        runs.append((start, n))
        i += n
    return runs


def weights_to_cat(g_in, name, total_rows, row0=0, into=None):
    n_dev, rows, shard = g_in.shape
    first = row0 // RELAYOUT_ROWS

    def body(x_ref, *rest):
        o_ref = rest[-1]
        for b in range(CAT_WIDTH // LANES):
            live = sum(_win_of_cat(LANES * b + i) is not None for i in range(LANES))
            parts = []
            for start, n in _runs(LANES * b, live, _win_of_cat):
                while n > 0:
                    d, o = divmod(start, shard)
                    take = min(n, shard - o)
                    parts.append(x_ref[d, :, o:o + take])
                    start, n = start + take, n - take
            if live < LANES:
                parts.append(jnp.zeros((RELAYOUT_ROWS, LANES - live), g_in.dtype))
            o_ref[:, LANES * b:LANES * (b + 1)] = parts[0] if len(parts) == 1 else jnp.concatenate(parts, axis=1)

    return pl.pallas_call(
        body, name=name, grid=(rows // RELAYOUT_ROWS,),
        in_specs=[pl.BlockSpec((n_dev, RELAYOUT_ROWS, shard), lambda i: (0, i, 0))] + ([ANY_SPEC] if into is not None else []),
        out_specs=pl.BlockSpec((RELAYOUT_ROWS, CAT_WIDTH), lambda i: (first + i, 0)),
        out_shape=jax.ShapeDtypeStruct((total_rows, CAT_WIDTH), g_in.dtype),
        input_output_aliases={1: 0} if into is not None else {},
        compiler_params=_params(("parallel",)))(*((g_in,) if into is None else (g_in, into)))


def cat_to_shards(dw_cat, shard):
    rows = dw_cat.shape[0]

    def body(x_ref, o_ref):
        for d in range(N_DEV):
            for t0 in range(0, shard, LANES):
                width = min(LANES, shard - t0)
                parts = [x_ref[:, c:c + n] for c, n in _runs(d * shard + t0, width, _cat_of_win)]
                o_ref[d, :, t0:t0 + width] = parts[0] if len(parts) == 1 else jnp.concatenate(parts, axis=1)

    return pl.pallas_call(
        body, name="cat_to_shards", grid=(rows // RELAYOUT_ROWS,),
        in_specs=[pl.BlockSpec((RELAYOUT_ROWS, CAT_WIDTH), lambda i: (i, 0))],
        out_specs=pl.BlockSpec((N_DEV, RELAYOUT_ROWS, shard), lambda i: (0, i, 0)),
        out_shape=jax.ShapeDtypeStruct((N_DEV, rows, shard), dw_cat.dtype),
        compiler_params=_params(("parallel",)))(dw_cat)


ROW_BLOCK = 512


def rms_fwd(x, w, name):
    t, d = x.shape

    def body(x_ref, w_ref, n_ref, r_ref):
        h = x_ref[...]
        r = lax.rsqrt(jnp.mean(h * h, axis=-1, keepdims=True) + NORM_EPS)
        n_ref[...] = (h * r * w_ref[...]).astype(BF16)
        r_ref[...] = r

    row = pl.BlockSpec((ROW_BLOCK, d), lambda i: (i, 0))
    return pl.pallas_call(
        body, name=name, grid=(t // ROW_BLOCK,),
        in_specs=[row, pl.BlockSpec((1, d), lambda i: (0, 0))],
        out_specs=[row, pl.BlockSpec((ROW_BLOCK, 1), lambda i: (i, 0))],
        out_shape=[jax.ShapeDtypeStruct((t, d), BF16), jax.ShapeDtypeStruct((t, 1), F32)],
        compiler_params=_params(("parallel",)))(x, w)


FUSED_ROWS = 512


def out_proj_rms(y, w_out, x, w_norm, name):
    t, d = x.shape

    def body(y_ref, w_ref, x_ref, g_ref, h_ref, n_ref, r_ref):
        h = x_ref[...] + _dot(y_ref[...], w_ref[...], NN)
        r = lax.rsqrt(jnp.mean(h * h, axis=-1, keepdims=True) + NORM_EPS)
        h_ref[...] = h
        n_ref[...] = (h * r * g_ref[...]).astype(BF16)
        r_ref[...] = r

    row = pl.BlockSpec((FUSED_ROWS, d), lambda i: (i, 0))
    return pl.pallas_call(
        body, name=name, grid=(t // FUSED_ROWS,),
        in_specs=[pl.BlockSpec((FUSED_ROWS, y.shape[1]), lambda i: (i, 0)), pl.BlockSpec(w_out.shape, lambda i: (0, 0)),
                  row, pl.BlockSpec((1, d), lambda i: (0, 0))],
        out_specs=[row, row, pl.BlockSpec((FUSED_ROWS, 1), lambda i: (i, 0))],
        out_shape=[jax.ShapeDtypeStruct((t, d), F32), jax.ShapeDtypeStruct((t, d), BF16),
                   jax.ShapeDtypeStruct((t, 1), F32)],
        compiler_params=_params(("parallel",)))(y, w_out, x, w_norm)


def ff2_loss(act, w_ff2, h1, w, target, name, tk=2048):
    t, d = h1.shape
    ksteps = act.shape[1] // tk

    def body(a_ref, b_ref, h_ref, w_ref, t_ref, loss_ref, dhb_ref, dw_ref, acc_ref):
        i, kk = pl.program_id(0), pl.program_id(1)

        @pl.when((i == 0) & (kk == 0))
        def _():
            loss_ref[...] = jnp.zeros_like(loss_ref)
            dw_ref[...] = jnp.zeros_like(dw_ref)

        @pl.when(kk == 0)
        def _():
            acc_ref[...] = h_ref[...]

        acc_ref[...] += _dot(a_ref[...], b_ref[...], NN)

        @pl.when(kk == ksteps - 1)
        def _():
            h = acc_ref[...]
            wv = w_ref[...]
            r = lax.rsqrt(jnp.mean(h * h, axis=-1, keepdims=True) + NORM_EPS)
            yn = h * r
            e = yn * wv - t_ref[...]
            loss_ref[...] += 0.5 * jnp.sum(jnp.sum(e * e, axis=-1, keepdims=True), axis=0, keepdims=True) / d
            dy = e / d
            dw_ref[...] += jnp.sum(dy * yn, axis=0, keepdims=True)
            dyn = dy * wv
            dhb_ref[...] = (r * (dyn - yn * jnp.mean(dyn * yn, axis=-1, keepdims=True))).astype(BF16)

    row = pl.BlockSpec((FUSED_ROWS, d), lambda i, k: (i, 0))
    wspec = pl.BlockSpec((1, d), lambda i, k: (0, 0))
    return pl.pallas_call(
        body, name=name, grid=(t // FUSED_ROWS, ksteps),
        in_specs=[pl.BlockSpec((FUSED_ROWS, tk), lambda i, k: (i, k)), pl.BlockSpec((tk, d), lambda i, k: (k, 0)),
                  row, wspec, row],
        out_specs=[pl.BlockSpec((1, 1), lambda i, k: (0, 0)), row, wspec],
        out_shape=[jax.ShapeDtypeStruct((1, 1), F32), jax.ShapeDtypeStruct((t, d), BF16),
                   jax.ShapeDtypeStruct((1, d), F32)],
        scratch_shapes=[pltpu.VMEM((FUSED_ROWS, d), F32)],
        compiler_params=_params(("arbitrary", "arbitrary")))(act, w_ff2, h1, w, target)


def rms_bwd(h, r, w, dn, dres, out_dtype, name):
    t, d = h.shape

    def body(h_ref, r_ref, w_ref, dn_ref, dres_ref, dh_ref, dw_ref):
        @pl.when(pl.program_id(0) == 0)
        def _():
            dw_ref[...] = jnp.zeros_like(dw_ref)

        rv = r_ref[...]
        yn = h_ref[...] * rv
        dnv = dn_ref[...].astype(F32)
        dw_ref[...] += jnp.sum(dnv * yn, axis=0, keepdims=True)
        dyn = dnv * w_ref[...]
        dh = dres_ref[...].astype(F32) + rv * (dyn - yn * jnp.mean(dyn * yn, axis=-1, keepdims=True))
        dh_ref[...] = dh.astype(out_dtype)

    row = pl.BlockSpec((ROW_BLOCK, d), lambda i: (i, 0))
    wspec = pl.BlockSpec((1, d), lambda i: (0, 0))
    rspec = pl.BlockSpec((ROW_BLOCK, 1), lambda i: (i, 0))
    return pl.pallas_call(
        body, name=name, grid=(t // ROW_BLOCK,),
        in_specs=[row, rspec, wspec, row, row], out_specs=[row, wspec],
        out_shape=[jax.ShapeDtypeStruct((t, d), out_dtype), jax.ShapeDtypeStruct((1, d), F32)],
        compiler_params=_params(("arbitrary",)))(h, r, w, dn, dres)


CONV_ROWS = 512
TILE_ROWS = 8


def _iota2(shape, axis):
    return lax.broadcasted_iota(jnp.int32, shape, axis)


def _silu(x):
    return x * jax.nn.sigmoid(x)


def _conv_rows(x_ref, w, first, rows):
    acc = None
    for j in range(4):
        term = x_ref[first - 3 + j:first - 3 + j + rows, :] * w[j:j + 1, :]
        acc = term if acc is None else acc + term
    return acc


def _head_shifts(head):
    rows = _iota2((TILE_ROWS, 1), 0)
    return [jnp.where(rows >= 3 - j, head if j == 3 else pltpu.roll(head, 3 - j, 0), 0.0) for j in range(4)]


def _conv_chunks(t):
    pieces = [(TILE_ROWS, min(CONV_ROWS, t) - TILE_ROWS)]
    pieces += [(r, CONV_ROWS) for r in range(CONV_ROWS, t, CONV_ROWS)]
    return pieces


def conv_fwd(proj, conv_w, name):
    t = proj.shape[0]

    def body(x_ref, w_ref, o_ref):
        w = w_ref[...]
        shifted = _head_shifts(x_ref[0:TILE_ROWS, :])
        o_ref[0:TILE_ROWS, :] = _silu(sum(shifted[j] * w[j:j + 1, :] for j in range(4)))
        for first, rows in _conv_chunks(t):
            o_ref[first:first + rows, :] = _silu(_conv_rows(x_ref, w, first, rows))

    col = pl.BlockSpec((t, LANES), lambda c: (0, c))
    return pl.pallas_call(
        body, name=name, grid=(QKV_WIDTH // LANES,),
        in_specs=[col, pl.BlockSpec((4, LANES), lambda c: (0, c))], out_specs=col,
        out_shape=jax.ShapeDtypeStruct((t, QKV_WIDTH), F32),
        compiler_params=_params(("parallel",)))(proj, conv_w)


def conv_bwd(proj, dout, conv_w, dproj, name):
    t = proj.shape[0]

    def dsilu(pre):
        sg = jax.nn.sigmoid(pre)
        return sg * (1.0 + pre * (1.0 - sg))

    def body(x_ref, d_ref, w_ref, dproj_in, dx_ref, dw_ref, stage):
        del dproj_in
        w = w_ref[...]
        shifted = _head_shifts(x_ref[0:TILE_ROWS, :])
        head_dpre = d_ref[0:TILE_ROWS, :] * dsilu(sum(shifted[j] * w[j:j + 1, :] for j in range(4)))
        stage[0:TILE_ROWS, :] = head_dpre
        for first, rows in _conv_chunks(t):
            stage[first:first + rows, :] = d_ref[first:first + rows, :] * dsilu(_conv_rows(x_ref, w, first, rows))
        stage[t:t + TILE_ROWS, :] = jnp.zeros((TILE_ROWS, LANES), F32)
        for first, rows in [(0, TILE_ROWS)] + _conv_chunks(t):
            dx = None
            for j in range(4):
                term = stage[first + 3 - j:first + 3 - j + rows, :] * w[j:j + 1, :]
                dx = term if dx is None else dx + term
            dx_ref[first:first + rows, :] = dx.astype(BF16)
        dw = [jnp.sum(head_dpre * shifted[j], axis=0, keepdims=True) for j in range(4)]
        for first, rows in _conv_chunks(t):
            dpre = stage[first:first + rows, :]
            for j in range(4):
                dw[j] = dw[j] + jnp.sum(dpre * x_ref[first - 3 + j:first - 3 + j + rows, :], axis=0, keepdims=True)
        dw_ref[...] = jnp.concatenate(dw, axis=0)

    col = pl.BlockSpec((t, LANES), lambda c: (0, c))
    taps = pl.BlockSpec((4, LANES), lambda c: (0, c))
    return pl.pallas_call(
        body, name=name, grid=(QKV_WIDTH // LANES,),
        in_specs=[col, col, taps, ANY_SPEC], out_specs=[col, taps],
        out_shape=[jax.ShapeDtypeStruct(dproj.shape, BF16), jax.ShapeDtypeStruct((4, QKV_WIDTH), F32)],
        scratch_shapes=[pltpu.VMEM((t + TILE_ROWS, LANES), F32)],
        input_output_aliases={3: 0},
        compiler_params=_params(("parallel",)))(proj, dout, conv_w, dproj)


def _softplus(x):
    return jnp.maximum(x, 0.0) + jnp.log(1.0 + jnp.exp(-jnp.abs(x)))


def _head_norm_gate(o, norm_w, gate):
    return o * lax.rsqrt(jnp.mean(o * o, axis=-1, keepdims=True) + NORM_EPS) * norm_w * _silu(gate)


GDN_PREC = ("bf", "bf")
HGRN_PREC = "bf"


def _each(fn, *cols):
    return [fn(*a) for a in zip(*cols)]


@functools.partial(jax.custom_vjp, nondiff_argnums=(2,))
def _known_inverse(low, inv, prec):
    del low, prec
    return inv


def _known_inverse_fwd(low, inv, prec):
    del low
    return inv, inv


def _known_inverse_bwd(prec, inv, ct):
    return -_mm_raw(_mm_raw(inv, ct, TN, prec), inv, NT, prec), jnp.zeros_like(inv)


_known_inverse.defvjp(_known_inverse_fwd, _known_inverse_bwd)


def gdn_stages(hs, qc, kc, vc, zc, ab, a_log_l, dt_l, norm_w, s, prec=GDN_PREC, inv_known=None):
    p_inv, p_mm = prec
    c = CHUNK
    ri, ci = _iota2((c, c), 0), _iota2((c, c), 1)
    incl, strict, eye = ri >= ci, ri > ci, ri == ci
    lane = _iota2((c, LANES), 1)
    last_row = _iota2((c, 1), 0) == c - 1
    rowsum = lambda x: jnp.sum(x, axis=1, keepdims=True)

    def row(col):
        return jnp.sum(jnp.where(eye, col, 0.0), axis=0, keepdims=True)

    q = _each(lambda x: x * lax.rsqrt(rowsum(x * x) + L2_EPS) * (HEAD_DIM ** -0.5), qc)
    k = _each(lambda x: x * lax.rsqrt(rowsum(x * x) + L2_EPS), kc)
    yield
    a_col = [rowsum(jnp.where(lane == h, ab, 0.0)) for h in hs]
    b_col = [rowsum(jnp.where(lane == h + N_HEADS, ab, 0.0)) for h in hs]
    beta = _each(jax.nn.sigmoid, b_col)
    g = _each(lambda a, al, dl: rowsum(jnp.where(lane == 0, -jnp.exp(al) * _softplus(a + dl), 0.0)), a_col, a_log_l, dt_l)
    gcum = _each(lambda x: rowsum(jnp.where(incl, row(x), 0.0)), g)
    g_last = _each(lambda x: jnp.sum(jnp.where(last_row, x, 0.0), axis=0, keepdims=True), gcum)
    decay = _each(lambda x: jnp.exp(jnp.where(incl, x - row(x), -jnp.inf)), gcum)
    yield
    kk = _each(lambda x: mm(x, x, NT, p_mm), k)
    low = _each(lambda b, x, d: jnp.where(strict, b * x * d, 0.0), beta, kk, decay)
    yield
    if inv_known is None:
        power = _each(lambda x: -x, low)
        inv = _each(lambda x: jnp.where(eye, 1.0, 0.0) + x, power)
        for _ in range(5):
            power = _each(lambda x: mm(x, x, NN, p_inv), power)
            yield
            inv = _each(lambda x, p: x + mm(x, p, NN, p_inv), inv, power)
            yield
    else:
        inv = _each(lambda x, known: _known_inverse(x, known, p_inv), low, inv_known)
    exp_g = _each(jnp.exp, gcum)
    yield
    u_v = _each(lambda i, b, x: mm(i, b * x, NN, p_mm), inv, beta, vc)
    w = _each(lambda i, b, e, x: mm(i, b * e * x, NN, p_mm), inv, beta, exp_g, k)
    yield
    attn = _each(lambda x, y, d: mm(x, y, NT, p_mm) * d, q, k, decay)
    yield
    u = _each(lambda x, y, z: x - mm(y, z, NN, p_mm), u_v, w, s)
    yield
    o = _each(lambda x, e, z: mm(x * e, z, NN, p_mm), q, exp_g, s)
    o = _each(lambda x, a, y: x + mm(a, y, NN, p_mm), o, attn, u)
    yield
    k_end = _each(lambda x, gl, gc: x * jnp.exp(gl - gc), k, g_last, gcum)
    s_new = _each(lambda z, gl, x, y: z * jnp.exp(gl) + mm(x, y, TN, p_mm), s, g_last, k_end, u)
    return (_each(lambda x, z: _head_norm_gate(x, norm_w, z), o, zc), s_new), inv


def gdn_chunk(h, qc, kc, vc, zc, ab, a_log_l, dt_l, norm_w, s, prec=GDN_PREC, reuse_inverse=False):
    args = ([h], [qc], [kc], [vc], [zc], ab, [a_log_l], [dt_l], norm_w, [s], prec)
    if reuse_inverse:
        inv = lax.stop_gradient(gdn_chunks(*args)[1])
        (y, s_new), _ = gdn_chunks(*args, inv_known=inv)
    else:
        (y, s_new), _ = gdn_chunks(*args)
    return y[0], s_new[0]


DIAG_ROWS = SUB_CHUNK // 2
SHIFT_PAD = 8
SHIFT_ROWS = SHIFT_PAD + CHUNK + SHIFT_PAD
SHIFT_WAYS = 4


class RolledRows:
    def down(self, x, which):
        del which
        return [x] + [pltpu.roll(x, off, 0) for off in range(1, DIAG_ROWS)]

    def up_sum(self, parts, which):
        del which
        acc = parts[0]
        for off in range(1, DIAG_ROWS):
            acc = acc + pltpu.roll(parts[off], CHUNK - off, 0)
        return acc


class SlotRows:
    def __init__(self, slots):
        self.slots = slots

    def down(self, x, which):
        self.slots[which, 0, SHIFT_PAD:SHIFT_PAD + CHUNK, :] = x
        return [x] + [self.slots[which, 0, SHIFT_PAD - off:SHIFT_PAD + CHUNK - off, :] for off in range(1, DIAG_ROWS)]

    def up_sum(self, parts, which):
        acc = parts[0]
        for off in range(1, DIAG_ROWS):
            way = 1 + off % (SHIFT_WAYS - 1)
            self.slots[which, way, SHIFT_PAD:SHIFT_PAD + CHUNK, :] = parts[off]
            acc = acc + self.slots[which, way, SHIFT_PAD + off:SHIFT_PAD + CHUNK + off, :]
        return acc


def _sub_block_rows():
    return jnp.bitwise_and(_iota2((CHUNK, 1), 0), DIAG_ROWS - 1)


def _diag_forward(rows, q, key, bc, v):
    rmod = _sub_block_rows()
    k_d, b_d, v_d = rows.down(key, 0), rows.down(bc, 1), rows.down(v, 2)
    o = None
    for off in range(DIAG_ROWS):
        e = jnp.exp(jnp.where(rmod >= off, bc - b_d[off], -jnp.inf))
        term = jnp.sum(q * k_d[off] * e, axis=-1, keepdims=True) * v_d[off]
        o = term if o is None else o + term
    return o


def _diag_backward(rows, q, key, bc, v, do):
    rmod = _sub_block_rows()
    k_d, b_d, v_d = rows.down(key, 0), rows.down(bc, 1), rows.down(v, 2)
    dq = db = None
    dk_parts, db_parts, dv_parts = [], [], []
    for off in range(DIAG_ROWS):
        e = jnp.exp(jnp.where(rmod >= off, bc - b_d[off], -jnp.inf))
        qe = q * e
        a = jnp.sum(qe * k_d[off], axis=-1, keepdims=True)
        da = jnp.sum(do * v_d[off], axis=-1, keepdims=True)
        dv_parts.append(a * do)
        dq_term = (da * e) * k_d[off]
        dk_term = da * qe
        s = dk_term * k_d[off]
        dq = dq_term if dq is None else dq + dq_term
        db = s if db is None else db + s
        dk_parts.append(dk_term)
        db_parts.append(s)
    return dq, rows.up_sum(dk_parts, 0), db - rows.up_sum(db_parts, 1), rows.up_sum(dv_parts, 2)


def diag_part(rows, differentiable=True):
    forward = functools.partial(_diag_forward, rows)
    if not differentiable:
        return forward
    part = jax.custom_vjp(forward)
    part.defvjp(lambda q, key, bc, v: (forward(q, key, bc, v), (q, key, bc, v)),
                lambda res, do: _diag_backward(rows, *res, do))
    return part


def hgrn_stages(qb, fb, ib, gb, l0, l1, norm_w, st, prec=HGRN_PREC, diags=None, o_known=None):
    c = CHUNK
    ri, ci = _iota2((4 * c, c), 0), _iota2((4 * c, c), 1)
    rcol = _iota2((c, 1), 0)
    blk0 = jnp.bitwise_and(ri, c - SUB_CHUNK)
    limit = jnp.where(ri < c, ri + 1, jnp.where(ri < 2 * c, blk0, jnp.where(ri < 3 * c, blk0 + SUB_CHUNK,
                                                                          blk0 + DIAG_ROWS)))
    sel = jnp.where(ci < limit, 1.0, 0.0)
    ri, ci = _iota2((c, c), 0), _iota2((c, c), 1)
    lb = _each(lambda a, b: jax.nn.sigmoid(a - b), l0, l1)
    forget = _each(lambda b, f: b + (1.0 - b) * jax.nn.sigmoid(f), lb, fb)
    key = _each(lambda b, f: (1.0 - b) * jax.nn.sigmoid(-f), lb, fb)
    q = _each(_silu, qb)
    v = ib
    logf = _each(jnp.log, forget)
    sums = _each(lambda x: sel_sums(sel, x), logf)
    bc, b_start, b_end, b_half = ([x[i] for x in sums] for i in range(4))
    b_last = _each(lambda x: jnp.sum(x, axis=0, keepdims=True), logf)
    o = _each(lambda x, b, z: mm(x * jnp.exp(b), z, NT, prec), q, bc, st)
    if diags is None:
        diags = [diag_part(RolledRows())] * len(qb)
    yield
    o = list(o)
    for h in range(len(o)):
        o[h] = o[h] + diags[h](q[h], key[h], bc[h], v[h])
        yield
    second = jnp.bitwise_and(rcol, SUB_CHUNK - 1) >= DIAG_ROWS
    same_sub = jnp.bitwise_and(ri, c - SUB_CHUNK) == jnp.bitwise_and(ci, c - SUB_CHUNK)
    q_half = _each(lambda x, b, bh: x * jnp.exp(jnp.where(second, b - bh, -jnp.inf)), q, bc, b_half)
    k_half = _each(lambda x, b, bh: x * jnp.exp(jnp.where(second, -jnp.inf, bh - b)), key, bc, b_half)
    a_half = _each(lambda x, z: jnp.where(same_sub, mm(x, z, NT, prec), 0.0), q_half, k_half)
    o = _each(lambda acc, a, val: acc + mm(a, val, NN, prec), o, a_half, v)
    yield
    q_rel = _each(lambda x, b, bs: x * jnp.exp(b - bs), q, bc, b_start)
    k_rel = _each(lambda x, b, be: x * jnp.exp(be - b), key, bc, b_end)
    for y in range(c // SUB_CHUNK - 1):
        def scaled(x, b, bs):
            end_y = jnp.sum(jnp.where(rcol == SUB_CHUNK * y + SUB_CHUNK - 1, b, 0.0), axis=0, keepdims=True)
            return x * jnp.exp(jnp.where(rcol >= SUB_CHUNK * (y + 1), bs - end_y, -jnp.inf))
        dq = _each(scaled, q_rel, bc, b_start)
        in_y = (ci >= SUB_CHUNK * y) & (ci < SUB_CHUNK * (y + 1))
        a_y = _each(lambda x, z: jnp.where(in_y, mm(x, z, NT, prec), 0.0), dq, k_rel)
        o = _each(lambda acc, a, val: acc + mm(a, val, NN, prec), o, a_y, v)
        yield
    k_state = _each(lambda x, bl, b: x * jnp.exp(bl - b), key, b_last, bc)
    st_new = _each(lambda z, bl, val, x: z * jnp.exp(bl) + mm(val, x, TN, prec), st, b_last, v, k_state)
    if o_known is not None:
        o = _each(_known_value, o, o_known)
    return (_each(lambda x, z: _head_norm_gate(x, norm_w, z), o, gb), st_new), o


def _drain(gen):
    try:
        while True:
            next(gen)
    except StopIteration as done:
        return done.value


def _alternate(gen_a, gen_b):
    out, live = [None, None], [gen_a, gen_b]
    while any(g is not None for g in live):
        for i, g in enumerate(live):
            if g is None:
                continue
            try:
                next(g)
            except StopIteration as done:
                out[i], live[i] = done.value, None
    return out


def gdn_chunks(*args, **kwargs):
    return _drain(gdn_stages(*args, **kwargs))


def hgrn_chunks(*args, **kwargs):
    return _drain(hgrn_stages(*args, **kwargs))


def hgrn_chunk(qb, fb, ib, gb, l0, l1, norm_w, st, prec=HGRN_PREC, reuse_output=False):
    args = ([qb], [fb], [ib], [gb], [l0], [l1], norm_w, [st], prec)
    if reuse_output:
        known = lax.stop_gradient(hgrn_chunks(*args)[1])
        (y, st_new), _ = hgrn_chunks(*args, o_known=known)
    else:
        (y, st_new), _ = hgrn_chunks(*args)
    return y[0], st_new[0]


HEAD_VEC = (N_HEADS, 1, LANES)


class _ChunkSpecs:
    def __init__(self, nc, rev):
        self.nc, self.rev = nc, rev

    def _c(self, c):
        return self.nc - 1 - c if self.rev else c

    def row(self, width, block=0):
        return pl.BlockSpec((CHUNK, width), lambda c: (self._c(c), block))

    def per_head(self, rows):
        return pl.BlockSpec((None, N_HEADS, rows, rows), lambda c: (self._c(c), 0, 0, 0))

    @staticmethod
    def whole(shape):
        return pl.BlockSpec(shape, lambda c: (0,) * len(shape))


def _lanes(j):
    return slice(j * LANES, (j + 1) * LANES)


def mixer_fwd(qkv_c, proj, a_log_l, dt_l, gdn_norm_w, l0, l1, hgrn_norm_w, name):
    t = qkv_c.shape[0]
    hb = N_HEADS
    sp = _ChunkSpecs(t // CHUNK, rev=False)
    hs = list(range(hb))

    def body(q_ref, k_ref, v_ref, z_ref, ab_ref, al_ref, dt_ref, gnw_ref, qb_ref, fb_ref, ib_ref, gb_ref, l0_ref, l1_ref,
             hnw_ref, y_ref, hist_a_ref, inv_ref, hist_b_ref, o_ref, sa_ref, sb_ref, shift_ref):
        @pl.when(pl.program_id(0) == 0)
        def _():
            sa_ref[...] = jnp.zeros_like(sa_ref)
            sb_ref[...] = jnp.zeros_like(sb_ref)
            shift_ref[...] = jnp.zeros_like(shift_ref)

        heads = lambda ref: [ref[:, _lanes(j)] for j in hs]
        s_a, s_b = [sa_ref[h] for h in hs], [sb_ref[h] for h in hs]
        for h in hs:
            hist_a_ref[h] = s_a[h]
            hist_b_ref[h] = s_b[h]
        diags = [diag_part(SlotRows(shift_ref.at[h]), differentiable=False) for h in hs]
        ((y_a, s_a_new), inv), ((y_b, s_b_new), o_pre) = _alternate(
            gdn_stages(hs, heads(q_ref), heads(k_ref), heads(v_ref), heads(z_ref), ab_ref[...],
                       [al_ref[h] for h in hs], [dt_ref[h] for h in hs], gnw_ref[...], s_a),
            hgrn_stages(heads(qb_ref), heads(fb_ref), heads(ib_ref), heads(gb_ref),
                        [l0_ref[h] for h in hs], [l1_ref[h] for h in hs], hnw_ref[...], s_b, diags=diags))
        for h in hs:
            y_ref[:, _lanes(h)] = y_a[h].astype(BF16)
            y_ref[:, _lanes(hb + h)] = y_b[h].astype(BF16)
            o_ref[:, _lanes(h)] = o_pre[h]
            sa_ref[h] = s_a_new[h]
            sb_ref[h] = s_b_new[h]
            inv_ref[h] = inv[h]

    vec, gain, slab = sp.whole(HEAD_VEC), sp.whole((1, LANES)), functools.partial(sp.row, GDN_WIDTH)
    states = jax.ShapeDtypeStruct((sp.nc, N_HEADS, HEAD_DIM, HEAD_DIM), F32)
    return pl.pallas_call(
        body, name=name, grid=(sp.nc,),
        in_specs=[slab(0), slab(1), slab(2), slab(3), sp.row(LANES, AB_BLOCK), vec, vec, gain,
                  slab(4), slab(5), slab(6), slab(7), vec, vec, gain],
        out_specs=[sp.row(2 * GDN_WIDTH), sp.per_head(HEAD_DIM), sp.per_head(CHUNK), sp.per_head(HEAD_DIM), slab(0)],
        out_shape=[jax.ShapeDtypeStruct((t, 2 * GDN_WIDTH), BF16), states,
                   jax.ShapeDtypeStruct((sp.nc, N_HEADS, CHUNK, CHUNK), F32), states,
                   jax.ShapeDtypeStruct((t, GDN_WIDTH), F32)],
        scratch_shapes=[pltpu.VMEM((N_HEADS, HEAD_DIM, HEAD_DIM), F32), pltpu.VMEM((N_HEADS, HEAD_DIM, HEAD_DIM), F32),
                        pltpu.VMEM((hb, 3, SHIFT_WAYS, SHIFT_ROWS, LANES), F32)],
        compiler_params=_params(("arbitrary",)),
    )(qkv_c, qkv_c, qkv_c, proj, proj, a_log_l, dt_l, gdn_norm_w, proj, proj, proj, proj, l0, l1, hgrn_norm_w)


def mixer_bwd(qkv_c, proj, a_log_l, dt_l, gdn_norm_w, l0, l1, hgrn_norm_w, hist_a, inv_hist, hist_b, o_pre, dy, name):
    t = qkv_c.shape[0]
    hb = N_HEADS
    sp = _ChunkSpecs(t // CHUNK, rev=True)
    hs = list(range(hb))

    def body(q_ref, k_ref, v_ref, z_ref, ab_ref, al_ref, dt_ref, gnw_ref, qb_ref, fb_ref, ib_ref, gb_ref, l0_ref, l1_ref,
             hnw_ref, hist_a_ref, inv_ref, hist_b_ref, o_ref, dy_ref,
             dqkv_ref, dproj_ref, dal_ref, ddt_ref, dgnw_ref, dl0_ref, dl1_ref, dhnw_ref, dsa_ref, dsb_ref, shift_ref):
        @pl.when(pl.program_id(0) == 0)
        def _():
            for ref in (dal_ref, ddt_ref, dgnw_ref, dl0_ref, dl1_ref, dhnw_ref, dsa_ref, dsb_ref, shift_ref):
                ref[...] = jnp.zeros_like(ref)

        heads = lambda ref, first=0: [ref[:, _lanes(first + j)] for j in hs]
        diags = [diag_part(SlotRows(shift_ref.at[h])) for h in hs]
        inv_known, o_known = [inv_ref[h] for h in hs], heads(o_ref)

        def both(ga, gb):
            (ra, inv), (rb, o_pre) = _alternate(gdn_stages(hs, *ga, inv_known=inv_known),
                                                hgrn_stages(*gb, diags=diags, o_known=o_known))
            return (ra, rb), (inv, o_pre)

        ga = (heads(q_ref), heads(k_ref), heads(v_ref), heads(z_ref), ab_ref[...], [al_ref[h] for h in hs],
              [dt_ref[h] for h in hs], gnw_ref[...], [hist_a_ref[h] for h in hs])
        gb = (heads(qb_ref), heads(fb_ref), heads(ib_ref), heads(gb_ref), [l0_ref[h] for h in hs],
              [l1_ref[h] for h in hs], hnw_ref[...], [hist_b_ref[h] for h in hs])
        _, vjp, _ = jax.vjp(both, ga, gb, has_aux=True)
        dy_a = [x.astype(F32) for x in heads(dy_ref)]
        dy_b = [x.astype(F32) for x in heads(dy_ref, hb)]
        (dq, dk, dv, dz, dab, dal, ddt, dgnw, ds_a), (dqb, dfb, dib, dgb, dl0, dl1, dhnw, ds_b) = vjp(
            ((dy_a, [dsa_ref[h] for h in hs]), (dy_b, [dsb_ref[h] for h in hs])))
        for h in hs:
            dqkv_ref[:, _lanes(h)] = dq[h]
            dqkv_ref[:, _lanes(hb + h)] = dk[h]
            dqkv_ref[:, _lanes(2 * hb + h)] = dv[h]
            for slab, val in enumerate((dz, dqb, dfb, dib, dgb)):
                dproj_ref[:, _lanes((3 + slab) * hb + h)] = val[h].astype(BF16)
            dal_ref[h] += dal[h]
            ddt_ref[h] += ddt[h]
            dl0_ref[h] += dl0[h]
            dl1_ref[h] += dl1[h]
            dsa_ref[h] = ds_a[h]
            dsb_ref[h] = ds_b[h]
        dproj_ref[:, MAIN_WIDTH:] = dab.astype(BF16)
        dgnw_ref[...] += dgnw
        dhnw_ref[...] += dhnw

    vec, gain, slab = sp.whole(HEAD_VEC), sp.whole((1, LANES)), functools.partial(sp.row, GDN_WIDTH)
    vec_shape, gain_shape = jax.ShapeDtypeStruct(HEAD_VEC, F32), jax.ShapeDtypeStruct((1, LANES), F32)
    return pl.pallas_call(
        body, name=name, grid=(sp.nc,),
        in_specs=[slab(0), slab(1), slab(2), slab(3), sp.row(LANES, AB_BLOCK), vec, vec, gain,
                  slab(4), slab(5), slab(6), slab(7), vec, vec, gain,
                  sp.per_head(HEAD_DIM), sp.per_head(CHUNK), sp.per_head(HEAD_DIM), slab(0), sp.row(2 * GDN_WIDTH)],
        out_specs=[sp.row(QKV_WIDTH), sp.row(CAT_WIDTH), vec, vec, gain, vec, vec, gain],
        out_shape=[jax.ShapeDtypeStruct((t, QKV_WIDTH), F32), jax.ShapeDtypeStruct((t, CAT_WIDTH), BF16),
                   vec_shape, vec_shape, gain_shape, vec_shape, vec_shape, gain_shape],
        scratch_shapes=[pltpu.VMEM((N_HEADS, HEAD_DIM, HEAD_DIM), F32), pltpu.VMEM((N_HEADS, HEAD_DIM, HEAD_DIM), F32),
                        pltpu.VMEM((hb, 3, SHIFT_WAYS, SHIFT_ROWS, LANES), F32)],
        compiler_params=_params(("arbitrary",)),
    )(qkv_c, qkv_c, qkv_c, proj, proj, a_log_l, dt_l, gdn_norm_w, proj, proj, proj, proj, l0, l1, hgrn_norm_w,
      hist_a, inv_hist, hist_b, o_pre, dy)


def _adamw(w, g, m, v):
    m = ADAM_B1 * m + (1.0 - ADAM_B1) * g
    v = ADAM_B2 * v + (1.0 - ADAM_B2) * jnp.square(g)
    m_hat = m / (1.0 - ADAM_B1 ** ADAM_STEP)
    v_hat = v / (1.0 - ADAM_B2 ** ADAM_STEP)
    delta = -ADAM_LR * (m_hat / (jnp.sqrt(v_hat) + ADAM_EPS) + ADAM_WD * w)
    return delta, m, v


def adamw_reduce(parts, mine, slot, w, m, v, name, rb=128):
    r, c = w.shape
    rb = min(rb, r)
    n_parts = parts.shape[0]

    def body(slot_ref, p_ref, own_ref, w_ref, m_ref, v_ref, g_ref, d_ref, mo_ref, vo_ref):
        part = lambda d: jnp.where(slot_ref[0] == d, own_ref[...], p_ref[d]).astype(F32)
        g = part(0)
        for d in range(1, n_parts):
            g = g + part(d)
        delta, mn, vn = _adamw(w_ref[...], g, m_ref[...], v_ref[...])
        g_ref[...] = g
        d_ref[...] = delta
        mo_ref[...] = mn
        vo_ref[...] = vn

    blk = pl.BlockSpec((rb, c), lambda i, s: (i, 0))
    return pl.pallas_call(
        body, name=name,
        grid_spec=pltpu.PrefetchScalarGridSpec(
            num_scalar_prefetch=1, grid=(r // rb,),
            in_specs=[pl.BlockSpec((n_parts, rb, c), lambda i, s: (0, i, 0)),
                      pl.BlockSpec((None, rb, c), lambda i, s: (s[0], i, 0)), blk, blk, blk],
            out_specs=[blk] * 4),
        out_shape=[jax.ShapeDtypeStruct((r, c), F32)] * 4,
        compiler_params=_params(("parallel",)))(slot.astype(jnp.int32).reshape(1), parts, mine, w, m, v)


def adamw_small(ws, gs, ms, vs, name):
    n = len(ws)

    def body(*refs):
        for i in range(n):
            w_ref, g_ref, m_ref, v_ref = (refs[j * n + i] for j in range(4))
            outs = _adamw(w_ref[...], g_ref[...], m_ref[...], v_ref[...])
            for j, o in enumerate(outs):
                refs[(4 + j) * n + i][...] = o

    vmem = pl.BlockSpec(memory_space=pltpu.VMEM)
    res = pl.pallas_call(body, name=name, in_specs=[vmem] * (4 * n), out_specs=[vmem] * (3 * n),
                         out_shape=[jax.ShapeDtypeStruct(w.shape, F32) for w in ws] * 3)(*ws, *gs, *ms, *vs)
    return res[:n], res[n:2 * n], res[2 * n:]


def _pack(arrays):
    flat = jnp.concatenate([a.reshape(-1).astype(F32) for a in arrays])
    rows = -(-flat.shape[0] // (8 * LANES)) * 8
    return jnp.pad(flat, (0, rows * LANES - flat.shape[0])).reshape(rows, LANES)


def _unpack(packed, shapes):
    flat, out, off = packed.reshape(-1), [], 0
    for s in shapes:
        n = 1
        for d in s:
            n *= d
        out.append(flat[off:off + n].reshape(s))
        off += n
    return out


def _relu2_epilogue(acc, _):
    r = jnp.maximum(acc, 0.0)
    return acc, r * r


def _relu2_bwd_epilogue(acc, a1):
    return (acc * (2.0 * jnp.maximum(a1, 0.0)),)


def kernel(x, w_in, conv_w, gdn_a_log, gdn_dt_bias, gdn_norm_w, hgrn_lb_logits, hgrn_norm_w, w_out, norm_mix_w, norm_ffn_w, w_ff1, w_ff2, norm_final_w, loss_target, m_w_in, m_conv_w, m_gdn_a_log, m_gdn_dt_bias, m_gdn_norm_w, m_hgrn_lb_logits, m_hgrn_norm_w, m_w_out, m_norm_mix_w, m_norm_ffn_w, m_w_ff1, m_w_ff2, m_norm_final_w, v_w_in, v_conv_w, v_gdn_a_log, v_gdn_dt_bias, v_gdn_norm_w, v_hgrn_lb_logits, v_hgrn_norm_w, v_w_out, v_norm_mix_w, v_norm_ffn_w, v_w_ff1, v_w_ff2, v_norm_final_w):
    me = _my_flat()
    xs = x[0]
    target = loss_target[0]
    shard_in = w_in.shape[2]
    shard_conv = conv_w.shape[2]

    tok = lambda t: t[0:1, 0:1]

    half = D_MODEL // 2
    w_in_b = w_in[0].astype(BF16)
    h_ga, t_ga = exchange_start([w_in_b[:half], conv_w[0]], True, "gather_w_in_high_start", peers=CHIP_PEERS)
    h_g0, t_g0 = exchange_start([w_in_b[half:]], True, "gather_w_in_low_start", after=[t_ga], peers=CHIP_PEERS)
    behind = lambda a: lax.optimization_barrier((a, t_g0))[0]
    h_g1, t_g1 = exchange_start([behind(w_out[0]).astype(BF16), behind(w_ff1[0]).astype(BF16)], True,
                                "gather_mid_start", after=[t_g0], peers=CHIP_PEERS)
    h_g2, t_g2 = exchange_start([behind(w_ff2[0]).astype(BF16)], True, "gather_ff2_start", after=[t_g1],
                                peers=CHIP_PEERS)
    m_in, v_in, _ = lax.optimization_barrier((m_w_in, v_w_in, t_g2))
    m_in, v_in = m_in[0], v_in[0]

    lane_b = lambda p: jnp.broadcast_to(p.reshape(N_HEADS, 1, 1), HEAD_VEC)
    a_log_l, dt_l = lane_b(gdn_a_log[0]), lane_b(gdn_dt_bias[0])
    l0 = hgrn_lb_logits[0].reshape(HEAD_VEC)
    l1 = hgrn_lb_logits[1].reshape(HEAD_VEC)

    n1, r1 = rms_fwd(xs, norm_mix_w + tok(t_g1) + tok(t_g2), "rms_mix")
    (s_high, s_conv), (l_high, l_conv) = exchange_wait(h_ga, "gather_w_in_high_wait", after=[n1, m_in, v_in],
                                                       copies=len(CHIP_PEERS))
    h_fa, _ = forward_start([l_high, l_conv], "gather_w_in_high_forward")
    _, (l_high, l_conv) = exchange_wait(h_fa, "forward_w_in_high_wait", copies=len(OTHER_CHIPS))
    w_cat = weights_to_cat(_own_slot(l_high, s_high), "weights_to_cat", D_MODEL)
    conv_full = jnp.transpose(_own_slot(l_conv, s_conv), (1, 0, 2)).reshape(4, QKV_WIDTH)
    proj = matmul(n1, w_cat, "nn", "in_proj_high", (BF16,), tn=CAT_WIDTH // 5, tk=half, k_blocks=(0, 1))
    (s_low,), (l_low,) = exchange_wait(h_g0, "gather_w_in_low_wait", after=[proj], copies=len(CHIP_PEERS))
    h_f0, _ = forward_start([l_low], "gather_w_in_low_forward")
    _, (l_low,) = exchange_wait(_one(h_f0, 0), "forward_w_in_low_wait", copies=len(OTHER_CHIPS))
    w_cat = weights_to_cat(_own_slot(l_low, s_low), "weights_to_cat_low", D_MODEL, row0=half, into=w_cat)
    proj = matmul_ring(n1, w_cat, "in_proj_low", proj, lambda acc, high: (acc + high,), tm=1024, tn=CAT_WIDTH // 5,
                       tk=half, k_block=1)
    qkv_c = conv_fwd(proj, conv_full, "conv_fwd")
    y, hist_a, inv_a, hist_b, o_b = mixer_fwd(qkv_c, proj, a_log_l, dt_l, gdn_norm_w, l0, l1, hgrn_norm_w, "mixer_fwd")
    (s_out, s_ff1), (l_out, l_ff1) = exchange_wait(h_g1, "gather_mid_wait", after=[y], copies=len(CHIP_PEERS))
    (s_ff2,), (l_ff2,) = exchange_wait(h_g2, "gather_ff2_wait", after=[y], copies=len(CHIP_PEERS))
    h_fw, _ = forward_start([l_out, l_ff1, l_ff2], "gather_forward_start")
    _, (l_out,) = exchange_wait(_one(h_fw, 0), "forward_out_wait", copies=len(OTHER_CHIPS))
    w_out_full = _own_slot(l_out, s_out).reshape(D_MODEL, D_MODEL)
    h1, n2, r2 = out_proj_rms(y, w_out_full, xs, norm_ffn_w, "out_proj_rms")
    _, (l_ff1,) = exchange_wait(_one(h_fw, 1), "forward_ff1_wait", after=[n2], copies=len(OTHER_CHIPS))
    w_ff1_sh = _own_slot(l_ff1, s_ff1)
    a1, act = matmul(n2, w_ff1_sh, "nn", "ff1", out_dtypes=(F32, BF16), epilogue=_relu2_epilogue, b_shards=True)
    _, (l_ff2,) = exchange_wait(_one(h_fw, 2), "forward_ff2_wait", after=[act], copies=len(OTHER_CHIPS))
    w_ff2_full = _own_slot(l_ff2, s_ff2).reshape(D_FF, D_MODEL)
    loss_sum, dh2_b, d_final = ff2_loss(act, w_ff2_full, h1, norm_final_w.reshape(1, D_MODEL), target, "ff2_loss")

    da1 = matmul(dh2_b, w_ff2_full, "nt", "d_act", out_dtypes=(BF16,), epilogue=_relu2_bwd_epilogue, extra=a1)
    t_all = xs.shape[0]
    dw_ff2 = matmul(act, dh2_b, "tn", "dw_ff2", out_dtypes=(BF16,), tk=t_all)
    p_ff2 = dw_ff2.reshape(N_DEV, D_FF // N_DEV, D_MODEL)
    h_s1, t_s1 = exchange_start([p_ff2], False, "scatter_ff2_start")
    dn2 = matmul(da1, w_ff1_sh, "nt", "d_n2", out_dtypes=(BF16,), after=[t_s1], b_shards=True, k_group=4)
    p_ff1 = matmul(n2, da1, "tn", "dw_ff1", out_dtypes=(BF16,), tn=D_FF // N_DEV, tk=t_all, after=[t_s1], out_shards=True)
    h_s2, t_s2 = exchange_start([p_ff1], False, "scatter_ff1_start")
    dh1_b, d_ffn = rms_bwd(h1, r2, norm_ffn_w + tok(t_s2), dn2, dh2_b, BF16, "rms_ffn_bwd")
    dmix = matmul(dh1_b, w_out_full, "nt", "d_mix", out_dtypes=(BF16,))
    dw_out = matmul(y, dh1_b, "tn", "dw_out", out_dtypes=(BF16,), tk=t_all)
    p_out = dw_out.reshape(N_DEV, D_MODEL // N_DEV, D_MODEL)
    h_s3, t_s3 = exchange_start([p_out], False, "scatter_out_start")
    d_qkv_c, dproj, d_alog_l, d_dt_l, d_gnw, dl0, dl1, d_hnw = mixer_bwd(
        qkv_c, proj, a_log_l, dt_l, gdn_norm_w + tok(t_s3), l0, l1, hgrn_norm_w, hist_a, inv_a, hist_b, o_b, dmix,
        "mixer_bwd")
    dproj, d_conv_full = conv_bwd(proj, d_qkv_c, conv_full, dproj, "conv_bwd")
    dw_cat = matmul(n1, dproj, "tn", "dw_in", out_dtypes=(BF16,), tm=512, tn=CAT_WIDTH // 5, tk=t_all)
    p_in = cat_to_shards(dw_cat, shard_in)
    h_pair, t_s4 = routed_start(p_in, _to_sibling_routes, "scatter_in_pair_start")

    (s_ff2g,), (r_ff2,) = exchange_wait(h_s1, "scatter_ff2_wait", after=[t_s4])
    (s_ff1g,), (r_ff1,) = exchange_wait(h_s2, "scatter_ff1_wait", after=[t_s4])
    (s_outg,), (r_out,) = exchange_wait(h_s3, "scatter_out_wait", after=[t_s4])
    g_w_ff2, d_w_ff2, nm_w_ff2, nv_w_ff2 = adamw_reduce(
        r_ff2, s_ff2g, me, w_ff2[0], m_w_ff2[0], v_w_ff2[0], "adamw_w_ff2")
    g_w_ff1, d_w_ff1, nm_w_ff1, nv_w_ff1 = adamw_reduce(
        r_ff1, s_ff1g, me, w_ff1[0], m_w_ff1[0], v_w_ff1[0], "adamw_w_ff1")
    g_w_out, d_w_out, nm_w_out, nv_w_out = adamw_reduce(
        r_out, s_outg, me, w_out[0], m_w_out[0], v_w_out[0], "adamw_w_out")
    (p_in,), (from_sibling,) = exchange_wait(h_pair, "scatter_in_pair_wait", after=[d_w_ff2, d_w_ff1, d_w_out],
                                             copies=N_CHIPS)
    chip_sums = pair_sum(p_in, from_sibling, "scatter_in_pair_sum")
    h_chips, t_s5 = routed_start(chip_sums, _to_chips_routes, "scatter_in_chips_start")
    dn1 = matmul(dproj, w_cat, "nt", "d_n1", out_dtypes=(BF16,), tm=512, tn=512, tk=CAT_WIDTH, after=[t_s5])
    dx, d_mix = rms_bwd(xs, r1, norm_mix_w, dn1, dh1_b, F32, "rms_mix_bwd")
    (chip_sums,), (r_in,) = exchange_wait(h_chips, "scatter_in_chips_wait", after=[dx], copies=len(OTHER_CHIPS))
    g_w_in, d_w_in, nm_w_in, nv_w_in = adamw_reduce(
        r_in, chip_sums, me // 2, w_in[0], m_in, v_in, "adamw_w_in")

    d_lb = jnp.stack([dl0.reshape(GDN_WIDTH), dl1.reshape(GDN_WIDTH)])
    small_shapes = [(1, N_HEADS), (1, N_HEADS), (1, HEAD_DIM), (2, GDN_WIDTH), (1, HEAD_DIM), (1, D_MODEL),
                    (1, D_MODEL), (D_MODEL,), (4, QKV_WIDTH), ()]
    small = _pack([d_alog_l[:, 0, 0], d_dt_l[:, 0, 0], d_gnw, d_lb, d_hnw, d_mix, d_ffn, d_final, d_conv_full,
                   loss_sum[0, 0]])
    red = allreduce_small(small, "allreduce_small")
    g_alog, g_dt, g_gnw, g_lb, g_hnw, g_mix, g_ffn, g_final, g_conv_full, loss = _unpack(red, small_shapes)
    g_conv = lax.dynamic_slice(g_conv_full, (0, me * shard_conv), (4, shard_conv)).reshape(1, 4, shard_conv)
    small_g = [g_alog, g_dt, g_gnw, g_lb, g_hnw, g_mix, g_ffn, g_final, g_conv]
    small_w = [gdn_a_log, gdn_dt_bias, gdn_norm_w, hgrn_lb_logits, hgrn_norm_w, norm_mix_w, norm_ffn_w, norm_final_w, conv_w]
    small_m = [m_gdn_a_log, m_gdn_dt_bias, m_gdn_norm_w, m_hgrn_lb_logits, m_hgrn_norm_w, m_norm_mix_w, m_norm_ffn_w,
               m_norm_final_w, m_conv_w]
    small_v = [v_gdn_a_log, v_gdn_dt_bias, v_gdn_norm_w, v_hgrn_lb_logits, v_hgrn_norm_w, v_norm_mix_w, v_norm_ffn_w,
               v_norm_final_w, v_conv_w]
    rows = lambda arrays: [a.reshape(-1, a.shape[-1]) for a in arrays]
    like_w = lambda arrays: [a.reshape(w.shape) for a, w in zip(arrays, small_w)]
    d_s, m_s, v_s = adamw_small(rows(small_w), rows(small_g), rows(small_m), rows(small_v), "adamw_small")
    d_alog, d_dt, d_gn, d_lbl, d_hn, d_nm, d_nf, d_nfin, d_cw = like_w(d_s)
    m_alog, m_dt, m_gn, m_lbl, m_hn, m_nm, m_nf, m_nfin, m_cw = like_w(m_s)
    v_alog, v_dt, v_gn, v_lbl, v_hn, v_nm, v_nf, v_nfin, v_cw = like_w(v_s)

    lead = lambda a: a[None]
    grads = [lead(g_w_in), g_conv, g_alog, g_dt, g_gnw, g_lb, g_hnw, lead(g_w_out), g_mix, g_ffn,
             lead(g_w_ff1), lead(g_w_ff2), g_final]
    deltas = [lead(d_w_in), d_cw, d_alog, d_dt, d_gn, d_lbl, d_hn, lead(d_w_out), d_nm, d_nf,
              lead(d_w_ff1), lead(d_w_ff2), d_nfin]
    new_m = [lead(nm_w_in), m_cw, m_alog, m_dt, m_gn, m_lbl, m_hn, lead(nm_w_out), m_nm, m_nf,
             lead(nm_w_ff1), lead(nm_w_ff2), m_nfin]
    new_v = [lead(nv_w_in), v_cw, v_alog, v_dt, v_gn, v_lbl, v_hn, lead(nv_w_out), v_nm, v_nf,
             lead(nv_w_ff1), lead(nv_w_ff2), v_nfin]
    return (loss, dx[None], *grads, *deltas, *new_m, *new_v)
```

```python
import functools

import jax
import jax.numpy as jnp
from jax import lax
from jax.experimental import pallas as pl
from jax.experimental.pallas import tpu as pltpu

F32 = jnp.float32
BF16 = jnp.bfloat16
HI = lax.Precision.HIGHEST

N_DEV = 8
D_MODEL = 2048
CHUNK = 64
SUB_CHUNK = 16
HEAD_DIM = 128
N_HEADS = 8
GDN_WIDTH = N_HEADS * HEAD_DIM
D_FF = 4 * D_MODEL
QKV_WIDTH = 3 * GDN_WIDTH
MAIN_WIDTH = 8 * GDN_WIDTH
CAT_WIDTH = MAIN_WIDTH + 128
AB_BLOCK = MAIN_WIDTH // 128
NORM_EPS = 1e-6
L2_EPS = 1e-6
LANES = 128
VMEM_LIMIT = 56 * 1024 * 1024

ADAM_LR = 0.001
ADAM_B1 = 0.9
ADAM_B2 = 0.999
ADAM_EPS = 1e-08
ADAM_WD = 0.01
ADAM_STEP = 10

MESH = pl.DeviceIdType.MESH


def _params(sem=None):
    return pltpu.CompilerParams(dimension_semantics=sem, vmem_limit_bytes=VMEM_LIMIT)


def _dot(a, b, dims, prec=None):
    return lax.dot_general(a, b, (dims, ((), ())), precision=prec, preferred_element_type=F32)


NN = ((1,), (0,))
NT = ((1,), (1,))
TN = ((0,), (0,))


def _split_bf16(x, pieces):
    out = []
    for _ in range(pieces - 1):
        p = x.astype(BF16)
        out.append(p)
        x = x - p.astype(F32)
    out.append(x.astype(BF16))
    return out


def _mm_raw(a, b, dims, prec):
    if prec == "hi":
        return _dot(a, b, dims, HI)
    if prec == "bf":
        return _dot(a.astype(BF16), b.astype(BF16), dims)
    a_hi, a_lo = _split_bf16(a, 2)
    b_hi, b_lo = _split_bf16(b, 2)
    return _dot(a_hi, b_hi, dims) + (_dot(a_hi, b_lo, dims) + _dot(a_lo, b_hi, dims))


@functools.partial(jax.custom_vjp, nondiff_argnums=(2, 3))
def mm(a, b, dims, prec):
    return _mm_raw(a, b, dims, prec)


def _mm_fwd(a, b, dims, prec):
    return _mm_raw(a, b, dims, prec), (a, b)


def _mm_bwd(dims, prec, res, ct):
    a, b = res
    if dims == NN:
        return _mm_raw(ct, b, NT, prec), _mm_raw(a, ct, TN, prec)
    if dims == NT:
        return _mm_raw(ct, b, NN, prec), _mm_raw(ct, a, TN, prec)
    return _mm_raw(b, ct, NT, prec), _mm_raw(a, ct, NN, prec)


mm.defvjp(_mm_fwd, _mm_bwd)


def _sel_raw(sel, x, dims):
    sel = sel.astype(BF16)
    p0, p1, p2 = _split_bf16(x, 3)
    return _dot(sel, p0, dims) + (_dot(sel, p1, dims) + _dot(sel, p2, dims))


def _sel_parts(sel, x):
    c = x.shape[0]
    full = _sel_raw(sel, x, NN)
    return tuple(full[i * c:(i + 1) * c] for i in range(sel.shape[0] // c))


@jax.custom_vjp
def sel_sums(sel, x):
    return _sel_parts(sel, x)


def _sel_fwd(sel, x):
    return _sel_parts(sel, x), sel


def _sel_bwd(sel, cts):
    return jnp.zeros_like(sel), _sel_raw(sel, jnp.concatenate(cts, axis=0), TN)


sel_sums.defvjp(_sel_fwd, _sel_bwd)


@jax.custom_vjp
def _known_value(computed, known):
    del computed
    return known


_known_value.defvjp(lambda computed, known: (known, None), lambda _, ct: (ct, jnp.zeros_like(ct)))


def _my_flat():
    return 4 * lax.axis_index("x") + 2 * lax.axis_index("y") + lax.axis_index("c")


def _peer(k):
    x, y, c = lax.axis_index("x"), lax.axis_index("y"), lax.axis_index("c")
    kx, ky, kc = (k >> 2) & 1, (k >> 1) & 1, k & 1
    px = (1 - x) if kx else x
    py = (1 - y) if ky else y
    pc = (1 - c) if kc else c
    return (px, py, pc), 4 * px + 2 * py + pc


HBM_SPEC = pl.BlockSpec(memory_space=pltpu.HBM)
SEM_SPEC = pl.BlockSpec(memory_space=pltpu.SEMAPHORE)
ANY_SPEC = pl.BlockSpec(memory_space=pl.ANY)
DATAFLOW = pltpu.SideEffectType.DATAFLOW_SIDE_EFFECTING


def _in_hbm(x):
    return pltpu.with_memory_space_constraint(x, pltpu.HBM)


ALL_PEERS = tuple(range(1, N_DEV))
CHIP_PEERS = (1, 2, 4, 6)
OTHER_CHIPS = (2, 4, 6)


def exchange_start(xs, gather, name, after=(), peers=ALL_PEERS):
    n, n_after = len(xs), len(after)

    def body(*refs):
        x_refs, land_refs = refs[:n], refs[n:2 * n]
        sems = refs[2 * n + n_after:2 * n + n_after + 2 * n]
        token = refs[-1]
        me = _my_flat()
        for k in peers:
            peer, peer_flat = _peer(k)
            for a in range(n):
                src = x_refs[a] if gather else x_refs[a].at[peer_flat]
                pltpu.make_async_remote_copy(src_ref=src, dst_ref=land_refs[a].at[me], send_sem=sems[a],
                                             recv_sem=sems[n + a], device_id=peer, device_id_type=MESH).start()
        token[...] = jnp.zeros_like(token)

    lands =[_in_hbm(lax.empty(((N_DEV,) + x.shape) if gather else x.shape, x.dtype)) for x in xs]
    hbm_out = [pltpu.HBM(x.shape, x.dtype) for x in xs] + [pltpu.HBM(l.shape, l.dtype) for l in lands]
    res = pl.pallas_call(
        body, name=name,
        out_shape=(*([pltpu.SemaphoreType.DMA(())] * (2 * n)), *hbm_out, jax.ShapeDtypeStruct((8, LANES), F32)),
        in_specs=[HBM_SPEC] * (2 * n) + [ANY_SPEC] * n_after,
        out_specs=(*([SEM_SPEC] * (2 * n)), *([HBM_SPEC] * (2 * n)), pl.BlockSpec(memory_space=pltpu.VMEM)),
        input_output_aliases={i: 2 * n + i for i in range(2 * n)},
        compiler_params=pltpu.CompilerParams(has_side_effects=DATAFLOW),
    )(*[_in_hbm(x) for x in xs], *lands, *after)
    return (list(res[:2 * n]), list(res[2 * n:3 * n]), list(res[3 * n:4 * n])), res[-1]


def forward_start(lands, name, after=()):
    n, n_after = len(lands), len(after)

    def body(*refs):
        land_refs = refs[:n]
        sems = refs[n + n_after:n + n_after + 2 * n]
        token = refs[-1]
        sibling, _ = _peer(1)
        for a in range(n):
            for k in OTHER_CHIPS:
                _, from_flat = _peer(k)
                slot = land_refs[a].at[from_flat]
                pltpu.make_async_remote_copy(src_ref=slot, dst_ref=slot, send_sem=sems[a], recv_sem=sems[n + a],
                                             device_id=sibling, device_id_type=MESH).start()
        token[...] = jnp.zeros_like(token)

    res = pl.pallas_call(
        body, name=name,
        out_shape=(*([pltpu.SemaphoreType.DMA(())] * (2 * n)), *[pltpu.HBM(l.shape, l.dtype) for l in lands],
                   jax.ShapeDtypeStruct((8, LANES), F32)),
        in_specs=[HBM_SPEC] * n + [ANY_SPEC] * n_after,
        out_specs=(*([SEM_SPEC] * (2 * n)), *([HBM_SPEC] * n), pl.BlockSpec(memory_space=pltpu.VMEM)),
        input_output_aliases={i: 2 * n + i for i in range(n)},
        compiler_params=pltpu.CompilerParams(has_side_effects=DATAFLOW),
    )(*lands, *after)
    return (list(res[:2 * n]), [], list(res[2 * n:3 * n])), res[-1]


def exchange_wait(handle, name, after=(), copies=N_DEV - 1):
    sems, xs, lands = handle
    n, n_x, n_after = len(lands), len(xs), len(after)

    def body(*refs):
        land_refs = refs[n_x:n_x + n]
        sem_refs = refs[n_x + n:n_x + 3 * n]
        for a in range(n):
            every = land_refs[a].at[pl.ds(0, copies)]
            cp = pltpu.make_async_remote_copy(src_ref=every, dst_ref=every, send_sem=sem_refs[a],
                                              recv_sem=sem_refs[n + a], device_id=_peer(1)[0], device_id_type=MESH)
            cp.wait_send()
            cp.wait_recv()

    res = pl.pallas_call(
        body, name=name,
        out_shape=[pltpu.HBM(x.shape, x.dtype) for x in xs] + [pltpu.HBM(l.shape, l.dtype) for l in lands],
        in_specs=[HBM_SPEC] * (n_x + n) + [SEM_SPEC] * (2 * n) + [ANY_SPEC] * n_after,
        out_specs=[HBM_SPEC] * (n_x + n),
        input_output_aliases={i: i for i in range(n_x + n)},
        compiler_params=pltpu.CompilerParams(has_side_effects=DATAFLOW),
    )(*xs, *lands, *sems, *after)
    return list(res[:n_x]), list(res[n_x:])


N_CHIPS = N_DEV // 2


def routed_start(x, routes, name, after=()):
    n_after = len(after)

    def body(*refs):
        x_ref, land_ref = refs[0], refs[1]
        send_sem, recv_sem = refs[2 + n_after], refs[3 + n_after]
        token = refs[-1]
        for src, dst, peer in routes():
            pltpu.make_async_remote_copy(src_ref=x_ref.at[src], dst_ref=land_ref.at[dst], send_sem=send_sem,
                                         recv_sem=recv_sem, device_id=peer, device_id_type=MESH).start()
        token[...] = jnp.zeros_like(token)

    land = _in_hbm(lax.empty((N_CHIPS,) + x.shape[1:], x.dtype))
    res = pl.pallas_call(
        body, name=name,
        out_shape=(pltpu.SemaphoreType.DMA(()), pltpu.SemaphoreType.DMA(()), pltpu.HBM(x.shape, x.dtype),
                   pltpu.HBM(land.shape, land.dtype), jax.ShapeDtypeStruct((8, LANES), F32)),
        in_specs=[HBM_SPEC, HBM_SPEC] + [ANY_SPEC] * n_after,
        out_specs=(SEM_SPEC, SEM_SPEC, HBM_SPEC, HBM_SPEC, pl.BlockSpec(memory_space=pltpu.VMEM)),
        input_output_aliases={0: 2, 1: 3},
        compiler_params=pltpu.CompilerParams(has_side_effects=DATAFLOW),
    )(_in_hbm(x), land, *after)
    return ([res[0], res[1]], [res[2]], [res[3]]), res[-1]


def _to_sibling_routes():
    c = lax.axis_index("c")
    sibling, _ = _peer(1)
    return [(2 * chip + 1 - c, chip, sibling) for chip in range(N_CHIPS)]


def _to_chips_routes():
    my_chip = _my_flat() // 2
    routes = []
    for k in OTHER_CHIPS:
        peer, peer_flat = _peer(k)
        routes.append((peer_flat // 2, my_chip, peer))
    return routes


def pair_sum(p, from_sibling, name, rb=1024):
    _, r, c = p.shape
    mine = lax.axis_index("c").astype(jnp.int32).reshape(1)

    def body(kind_ref, p_ref, s_ref, o_ref):
        del kind_ref
        o_ref[...] = (p_ref[...].astype(F32) + s_ref[...].astype(F32)).astype(BF16)

    return pl.pallas_call(
        body, name=name,
        grid_spec=pltpu.PrefetchScalarGridSpec(
            num_scalar_prefetch=1, grid=(N_CHIPS, r // rb),
            in_specs=[pl.BlockSpec((None, None, rb, c), lambda chip, i, kind: (chip, kind[0], i, 0)),
                      pl.BlockSpec((None, rb, c), lambda chip, i, kind: (chip, i, 0))],
            out_specs=pl.BlockSpec((None, rb, c), lambda chip, i, kind: (chip, i, 0))),
        out_shape=jax.ShapeDtypeStruct((N_CHIPS, r, c), BF16),
        compiler_params=_params(("parallel", "parallel")))(mine, p.reshape(N_CHIPS, 2, r, c), from_sibling)


def _one(handle, a):
    sems, xs, lands = handle
    n = len(lands)
    return [sems[a], sems[n + a]], xs[a:a + 1], [lands[a]]


def _own_slot(land, block):
    return lax.dynamic_update_slice(land, block[None], (_my_flat(),) + (0,) * block.ndim)


def allreduce_small(x, name):
    rows = x.shape[0]

    def body(x_ref, o_ref, buf, send_sems, recv_sems):
        me = _my_flat()
        buf[me] = x_ref[...]
        sends = []
        for k in range(1, N_DEV):
            peer, _ = _peer(k)
            cp = pltpu.make_async_remote_copy(
                src_ref=x_ref, dst_ref=buf.at[me], send_sem=send_sems.at[k], recv_sem=recv_sems.at[k],
                device_id=peer, device_id_type=MESH)
            cp.start()
            sends.append(cp)
        for k in range(1, N_DEV):
            peer, peer_flat = _peer(k)
            pltpu.make_async_remote_copy(
                src_ref=x_ref, dst_ref=buf.at[peer_flat], send_sem=send_sems.at[k], recv_sem=recv_sems.at[k],
                device_id=peer, device_id_type=MESH).wait_recv()
        for cp in sends:
            cp.wait_send()
        acc = buf[0]
        for d in range(1, N_DEV):
            acc = acc + buf[d]
        o_ref[...] = acc

    vmem = pl.BlockSpec(memory_space=pltpu.VMEM)
    return pl.pallas_call(
        body, name=name, out_shape=jax.ShapeDtypeStruct((rows, LANES), F32),
        in_specs=[vmem], out_specs=vmem,
        scratch_shapes=[pltpu.VMEM((N_DEV, rows, LANES), F32),
                        pltpu.SemaphoreType.DMA((N_DEV,)), pltpu.SemaphoreType.DMA((N_DEV,))],
    )(x)


def matmul(a, b, mode, name, out_dtypes=(F32,), epilogue=None, extra=None, tm=1024, tn=1024, tk=2048, after=(),
           b_shards=False, out_shards=False, k_group=1, k_blocks=None):
    if b_shards:
        n_sh, b_rows, b_cols = b.shape
    if mode == "nn":
        (m, kd), n = a.shape, (n_sh * b_cols if b_shards else b.shape[1])
        if b_shards:
            tn = b_cols
    elif mode == "nt":
        (m, kd), n = a.shape, (b_rows if b_shards else b.shape[0])
        if b_shards:
            tk = k_group * b_cols
    else:
        (kd, m), n = a.shape, b.shape[1]
    tm, tn, tk = min(tm, m), min(tn, n), min(tk, kd)
    assert m % tm == 0 and n % tn == 0 and kd % tk == 0, (name, m, n, kd, tm, tn, tk)
    k0, ksteps = (0, kd // tk) if k_blocks is None else k_blocks
    dims = {"nn": NN, "nt": NT, "tn": TN}[mode]
    n_out = len(out_dtypes)
    n_in = 2 + (extra is not None) + len(after)

    def finish(acc, e_ref, o_refs):
        outs = (acc,) if epilogue is None else epilogue(acc, e_ref[...] if e_ref is not None else None)
        for o_ref, o in zip(o_refs, outs):
            o_ref[...] = o.astype(o_ref.dtype)

    def product(a_ref, b_ref):
        if mode == "nt" and b_shards:
            w = b_cols
            parts = [_dot(a_ref[:, s * w:(s + 1) * w], b_ref[s], dims) for s in range(k_group)]
            return functools.reduce(lambda p, q: p + q, parts)
        return _dot(a_ref[...], b_ref[...], dims)

    def body(*refs):
        a_ref, b_ref = refs[0], refs[1]
        e_ref = refs[2] if extra is not None else None
        o_refs = refs[n_in:n_in + n_out]
        if ksteps == 1:
            finish(product(a_ref, b_ref), e_ref, o_refs)
            return
        acc_ref = refs[-1]
        kk = pl.program_id(2)

        @pl.when(kk == 0)
        def _():
            acc_ref[...] = jnp.zeros_like(acc_ref)

        acc_ref[...] += product(a_ref, b_ref)

        @pl.when(kk == ksteps - 1)
        def _():
            finish(acc_ref[...], e_ref, o_refs)

    if mode == "nn":
        a_spec = pl.BlockSpec((tm, tk), lambda i, j, k: (i, k0 + k))
        b_spec = (pl.BlockSpec((None, tk, tn), lambda i, j, k: (j, k, 0)) if b_shards
                  else pl.BlockSpec((tk, tn), lambda i, j, k: (k0 + k, j)))
    elif mode == "nt":
        a_spec = pl.BlockSpec((tm, tk), lambda i, j, k: (i, k))
        b_spec = (pl.BlockSpec((k_group, tn, b_cols), lambda i, j, k: (k, j, 0)) if b_shards
                  else pl.BlockSpec((tn, tk), lambda i, j, k: (j, k)))
    else:
        a_spec = pl.BlockSpec((tk, tm), lambda i, j, k: (k, i))
        b_spec = pl.BlockSpec((tk, tn), lambda i, j, k: (k, j))
    o_spec = pl.BlockSpec((tm, tn), lambda i, j, k: (i, j))
    res_spec = pl.BlockSpec((None, tm, tn), lambda i, j, k: (j, i, 0)) if out_shards else o_spec
    res_shape = (n // tn, m, tn) if out_shards else (m, n)
    in_specs = [a_spec, b_spec] + ([o_spec] if extra is not None else []) + [ANY_SPEC] * len(after)
    args = (a, b) + ((extra,) if extra is not None else ()) + tuple(after)
    res = pl.pallas_call(
        body, name=name, grid=(m // tm, n // tn, ksteps),
        in_specs=in_specs, out_specs=[res_spec] * n_out,
        out_shape=[jax.ShapeDtypeStruct(res_shape, dt) for dt in out_dtypes],
        scratch_shapes=[pltpu.VMEM((tm, tn), F32)] if ksteps > 1 else [],
        compiler_params=_params(("parallel", "parallel", "arbitrary")),
    )(*args)
    return res if n_out > 1 else res[0]


RING_SLOTS = 3


def matmul_ring(a, b, mode, name, out_dtype, tm, tn, tk, k_block=0, extra=None, epilogue=None, after=()):
    m, n = a.shape[0], (b.shape[1] if mode == "nn" else b.shape[0])
    nj = n // tn
    steps = (m // tm) * nj
    k0 = k_block * tk
    dims, block = (NN, (tk, tn)) if mode == "nn" else (NT, (tn, tk))
    n_in = 2 + (extra is not None) + len(after)

    def body(*refs):
        a_ref, b_hbm = refs[0], refs[1]
        e_ref = refs[2] if extra is not None else None
        o_ref, slots, sems = refs[n_in:]
        s = pl.program_id(0) * nj + pl.program_id(1)

        def fetch(step):
            at_n = pl.ds(pl.multiple_of((step % nj) * tn, LANES), tn)
            src = b_hbm.at[pl.ds(k0, tk), at_n] if mode == "nn" else b_hbm.at[at_n, pl.ds(k0, tk)]
            slot = step % RING_SLOTS
            return pltpu.make_async_copy(src, slots.at[slot], sems.at[slot])

        @pl.when(s == 0)
        def _():
            fetch(s).start()
            fetch(s + 1).start()

        @pl.when(s + 2 < steps)
        def _():
            fetch(s + 2).start()

        fetch(s).wait()
        for k in range(RING_SLOTS):
            @pl.when(s % RING_SLOTS == k)
            def _():
                acc = _dot(a_ref[...], slots[k], dims)
                out = acc if epilogue is None else epilogue(acc, e_ref[...] if e_ref is not None else None)[0]
                o_ref[...] = out.astype(o_ref.dtype)

    o_spec = pl.BlockSpec((tm, tn), lambda i, j: (i, j))
    in_specs = ([pl.BlockSpec((tm, tk), lambda i, j: (i, k_block)), ANY_SPEC] + ([o_spec] if extra is not None else [])
                + [ANY_SPEC] * len(after))
    return pl.pallas_call(
        body, name=name, grid=(m // tm, nj), in_specs=in_specs, out_specs=o_spec,
        out_shape=jax.ShapeDtypeStruct((m, n), out_dtype),
        scratch_shapes=[pltpu.VMEM((RING_SLOTS,) + block, b.dtype), pltpu.SemaphoreType.DMA((RING_SLOTS,))],
        compiler_params=_params(("arbitrary", "arbitrary")))(
            a, b, *((extra,) if extra is not None else ()), *after)


GATE_COL = 4 * GDN_WIDTH
RELAYOUT_ROWS = 256


def _cat_of_win(j):
    if j < GATE_COL:
        return j
    if j < GATE_COL + 2 * N_HEADS:
        return MAIN_WIDTH + (j - GATE_COL)
    return j - 2 * N_HEADS


def _win_of_cat(c):
    if c < GATE_COL:
        return c
    if c < MAIN_WIDTH:
        return c + 2 * N_HEADS
    if c < MAIN_WIDTH + 2 * N_HEADS:
        return GATE_COL + (c - MAIN_WIDTH)
    return None


def _runs(first, count, mapping):
    runs, i = [], 0
    while i < count:
        start, n = mapping(first + i), 1
        while i + n < count and mapping(first + i + n) == start + n:
            n += 1
        runs.append((start, n))
        i += n
    return runs


def weights_to_cat(g_in, name, total_rows, row0=0, into=None):
    n_dev, rows, shard = g_in.shape
    first = row0 // RELAYOUT_ROWS

    def body(x_ref, *rest):
        o_ref = rest[-1]
        for b in range(CAT_WIDTH // LANES):
            live = sum(_win_of_cat(LANES * b + i) is not None for i in range(LANES))
            parts = []
            for start, n in _runs(LANES * b, live, _win_of_cat):
                while n > 0:
                    d, o = divmod(start, shard)
                    take = min(n, shard - o)
                    parts.append(x_ref[d, :, o:o + take])
                    start, n = start + take, n - take
            if live < LANES:
                parts.append(jnp.zeros((RELAYOUT_ROWS, LANES - live), g_in.dtype))
            o_ref[:, LANES * b:LANES * (b + 1)] = parts[0] if len(parts) == 1 else jnp.concatenate(parts, axis=1)

    return pl.pallas_call(
        body, name=name, grid=(rows // RELAYOUT_ROWS,),
        in_specs=[pl.BlockSpec((n_dev, RELAYOUT_ROWS, shard), lambda i: (0, i, 0))] + ([ANY_SPEC] if into is not None else []),
        out_specs=pl.BlockSpec((RELAYOUT_ROWS, CAT_WIDTH), lambda i: (first + i, 0)),
        out_shape=jax.ShapeDtypeStruct((total_rows, CAT_WIDTH), g_in.dtype),
        input_output_aliases={1: 0} if into is not None else {},
        compiler_params=_params(("parallel",)))(*((g_in,) if into is None else (g_in, into)))


def cat_to_shards(dw_cat, shard):
    rows = dw_cat.shape[0]

    def body(x_ref, o_ref):
        for d in range(N_DEV):
            for t0 in range(0, shard, LANES):
                width = min(LANES, shard - t0)
                parts = [x_ref[:, c:c + n] for c, n in _runs(d * shard + t0, width, _cat_of_win)]
                o_ref[d, :, t0:t0 + width] = parts[0] if len(parts) == 1 else jnp.concatenate(parts, axis=1)

    return pl.pallas_call(
        body, name="cat_to_shards", grid=(rows // RELAYOUT_ROWS,),
        in_specs=[pl.BlockSpec((RELAYOUT_ROWS, CAT_WIDTH), lambda i: (i, 0))],
        out_specs=pl.BlockSpec((N_DEV, RELAYOUT_ROWS, shard), lambda i: (0, i, 0)),
        out_shape=jax.ShapeDtypeStruct((N_DEV, rows, shard), dw_cat.dtype),
        compiler_params=_params(("parallel",)))(dw_cat)


ROW_BLOCK = 512


def rms_fwd(x, w, name):
    t, d = x.shape

    def body(x_ref, w_ref, n_ref, r_ref):
        h = x_ref[...]
        r = lax.rsqrt(jnp.mean(h * h, axis=-1, keepdims=True) + NORM_EPS)
        n_ref[...] = (h * r * w_ref[...]).astype(BF16)
        r_ref[...] = r

    row = pl.BlockSpec((ROW_BLOCK, d), lambda i: (i, 0))
    return pl.pallas_call(
        body, name=name, grid=(t // ROW_BLOCK,),
        in_specs=[row, pl.BlockSpec((1, d), lambda i: (0, 0))],
        out_specs=[row, pl.BlockSpec((ROW_BLOCK, 1), lambda i: (i, 0))],
        out_shape=[jax.ShapeDtypeStruct((t, d), BF16), jax.ShapeDtypeStruct((t, 1), F32)],
        compiler_params=_params(("parallel",)))(x, w)


FUSED_ROWS = 512


def out_proj_rms(y, w_out, x, w_norm, name):
    t, d = x.shape

    def body(y_ref, w_ref, x_ref, g_ref, h_ref, n_ref, r_ref):
        h = x_ref[...] + _dot(y_ref[...], w_ref[...], NN)
        r = lax.rsqrt(jnp.mean(h * h, axis=-1, keepdims=True) + NORM_EPS)
        h_ref[...] = h
        n_ref[...] = (h * r * g_ref[...]).astype(BF16)
        r_ref[...] = r

    row = pl.BlockSpec((FUSED_ROWS, d), lambda i: (i, 0))
    return pl.pallas_call(
        body, name=name, grid=(t // FUSED_ROWS,),
        in_specs=[pl.BlockSpec((FUSED_ROWS, y.shape[1]), lambda i: (i, 0)), pl.BlockSpec(w_out.shape, lambda i: (0, 0)),
                  row, pl.BlockSpec((1, d), lambda i: (0, 0))],
        out_specs=[row, row, pl.BlockSpec((FUSED_ROWS, 1), lambda i: (i, 0))],
        out_shape=[jax.ShapeDtypeStruct((t, d), F32), jax.ShapeDtypeStruct((t, d), BF16),
                   jax.ShapeDtypeStruct((t, 1), F32)],
        compiler_params=_params(("parallel",)))(y, w_out, x, w_norm)


def ff2_loss(act, w_ff2, h1, w, target, name, tk=2048):
    t, d = h1.shape
    ksteps = act.shape[1] // tk

    def body(a_ref, b_ref, h_ref, w_ref, t_ref, loss_ref, dhb_ref, dw_ref, acc_ref):
        i, kk = pl.program_id(0), pl.program_id(1)

        @pl.when((i == 0) & (kk == 0))
        def _():
            loss_ref[...] = jnp.zeros_like(loss_ref)
            dw_ref[...] = jnp.zeros_like(dw_ref)

        @pl.when(kk == 0)
        def _():
            acc_ref[...] = h_ref[...]

        acc_ref[...] += _dot(a_ref[...], b_ref[...], NN)

        @pl.when(kk == ksteps - 1)
        def _():
            h = acc_ref[...]
            wv = w_ref[...]
            r = lax.rsqrt(jnp.mean(h * h, axis=-1, keepdims=True) + NORM_EPS)
            yn = h * r
            e = yn * wv - t_ref[...]
            loss_ref[...] += 0.5 * jnp.sum(jnp.sum(e * e, axis=-1, keepdims=True), axis=0, keepdims=True) / d
            dy = e / d
            dw_ref[...] += jnp.sum(dy * yn, axis=0, keepdims=True)
            dyn = dy * wv
            dhb_ref[...] = (r * (dyn - yn * jnp.mean(dyn * yn, axis=-1, keepdims=True))).astype(BF16)

    row = pl.BlockSpec((FUSED_ROWS, d), lambda i, k: (i, 0))
    wspec = pl.BlockSpec((1, d), lambda i, k: (0, 0))
    return pl.pallas_call(
        body, name=name, grid=(t // FUSED_ROWS, ksteps),
        in_specs=[pl.BlockSpec((FUSED_ROWS, tk), lambda i, k: (i, k)), pl.BlockSpec((tk, d), lambda i, k: (k, 0)),
                  row, wspec, row],
        out_specs=[pl.BlockSpec((1, 1), lambda i, k: (0, 0)), row, wspec],
        out_shape=[jax.ShapeDtypeStruct((1, 1), F32), jax.ShapeDtypeStruct((t, d), BF16),
                   jax.ShapeDtypeStruct((1, d), F32)],
        scratch_shapes=[pltpu.VMEM((FUSED_ROWS, d), F32)],
        compiler_params=_params(("arbitrary", "arbitrary")))(act, w_ff2, h1, w, target)


def rms_bwd(h, r, w, dn, dres, out_dtype, name):
    t, d = h.shape

    def body(h_ref, r_ref, w_ref, dn_ref, dres_ref, dh_ref, dw_ref):
        @pl.when(pl.program_id(0) == 0)
        def _():
            dw_ref[...] = jnp.zeros_like(dw_ref)

        rv = r_ref[...]
        yn = h_ref[...] * rv
        dnv = dn_ref[...].astype(F32)
        dw_ref[...] += jnp.sum(dnv * yn, axis=0, keepdims=True)
        dyn = dnv * w_ref[...]
        dh = dres_ref[...].astype(F32) + rv * (dyn - yn * jnp.mean(dyn * yn, axis=-1, keepdims=True))
        dh_ref[...] = dh.astype(out_dtype)

    row = pl.BlockSpec((ROW_BLOCK, d), lambda i: (i, 0))
    wspec = pl.BlockSpec((1, d), lambda i: (0, 0))
    rspec = pl.BlockSpec((ROW_BLOCK, 1), lambda i: (i, 0))
    return pl.pallas_call(
        body, name=name, grid=(t // ROW_BLOCK,),
        in_specs=[row, rspec, wspec, row, row], out_specs=[row, wspec],
        out_shape=[jax.ShapeDtypeStruct((t, d), out_dtype), jax.ShapeDtypeStruct((1, d), F32)],
        compiler_params=_params(("arbitrary",)))(h, r, w, dn, dres)


CONV_ROWS = 512
TILE_ROWS = 8


def _iota2(shape, axis):
    return lax.broadcasted_iota(jnp.int32, shape, axis)


def _silu(x):
    return x * jax.nn.sigmoid(x)


def _conv_rows(x_ref, w, first, rows):
    acc = None
    for j in range(4):
        term = x_ref[first - 3 + j:first - 3 + j + rows, :] * w[j:j + 1, :]
        acc = term if acc is None else acc + term
    return acc


def _head_shifts(head):
    rows = _iota2((TILE_ROWS, 1), 0)
    return [jnp.where(rows >= 3 - j, head if j == 3 else pltpu.roll(head, 3 - j, 0), 0.0) for j in range(4)]


def _conv_chunks(t):
    pieces = [(TILE_ROWS, min(CONV_ROWS, t) - TILE_ROWS)]
    pieces += [(r, CONV_ROWS) for r in range(CONV_ROWS, t, CONV_ROWS)]
    return pieces


def conv_fwd(proj, conv_w, name):
    t = proj.shape[0]

    def body(x_ref, w_ref, o_ref):
        w = w_ref[...]
        shifted = _head_shifts(x_ref[0:TILE_ROWS, :])
        o_ref[0:TILE_ROWS, :] = _silu(sum(shifted[j] * w[j:j + 1, :] for j in range(4)))
        for first, rows in _conv_chunks(t):
            o_ref[first:first + rows, :] = _silu(_conv_rows(x_ref, w, first, rows))

    col = pl.BlockSpec((t, LANES), lambda c: (0, c))
    return pl.pallas_call(
        body, name=name, grid=(QKV_WIDTH // LANES,),
        in_specs=[col, pl.BlockSpec((4, LANES), lambda c: (0, c))], out_specs=col,
        out_shape=jax.ShapeDtypeStruct((t, QKV_WIDTH), F32),
        compiler_params=_params(("parallel",)))(proj, conv_w)


def conv_bwd(proj, dout, conv_w, dproj, name):
    t = proj.shape[0]

    def dsilu(pre):
        sg = jax.nn.sigmoid(pre)
        return sg * (1.0 + pre * (1.0 - sg))

    def body(x_ref, d_ref, w_ref, dproj_in, dx_ref, dw_ref, stage):
        del dproj_in
        w = w_ref[...]
        shifted = _head_shifts(x_ref[0:TILE_ROWS, :])
        head_dpre = d_ref[0:TILE_ROWS, :] * dsilu(sum(shifted[j] * w[j:j + 1, :] for j in range(4)))
        stage[0:TILE_ROWS, :] = head_dpre
        for first, rows in _conv_chunks(t):
            stage[first:first + rows, :] = d_ref[first:first + rows, :] * dsilu(_conv_rows(x_ref, w, first, rows))
        stage[t:t + TILE_ROWS, :] = jnp.zeros((TILE_ROWS, LANES), F32)
        for first, rows in [(0, TILE_ROWS)] + _conv_chunks(t):
            dx = None
            for j in range(4):
                term = stage[first + 3 - j:first + 3 - j + rows, :] * w[j:j + 1, :]
                dx = term if dx is None else dx + term
            dx_ref[first:first + rows, :] = dx.astype(BF16)
        dw = [jnp.sum(head_dpre * shifted[j], axis=0, keepdims=True) for j in range(4)]
        for first, rows in _conv_chunks(t):
            dpre = stage[first:first + rows, :]
            for j in range(4):
                dw[j] = dw[j] + jnp.sum(dpre * x_ref[first - 3 + j:first - 3 + j + rows, :], axis=0, keepdims=True)
        dw_ref[...] = jnp.concatenate(dw, axis=0)

    col = pl.BlockSpec((t, LANES), lambda c: (0, c))
    taps = pl.BlockSpec((4, LANES), lambda c: (0, c))
    return pl.pallas_call(
        body, name=name, grid=(QKV_WIDTH // LANES,),
        in_specs=[col, col, taps, ANY_SPEC], out_specs=[col, taps],
        out_shape=[jax.ShapeDtypeStruct(dproj.shape, BF16), jax.ShapeDtypeStruct((4, QKV_WIDTH), F32)],
        scratch_shapes=[pltpu.VMEM((t + TILE_ROWS, LANES), F32)],
        input_output_aliases={3: 0},
        compiler_params=_params(("parallel",)))(proj, dout, conv_w, dproj)


def _softplus(x):
    return jnp.maximum(x, 0.0) + jnp.log(1.0 + jnp.exp(-jnp.abs(x)))


def _head_norm_gate(o, norm_w, gate):
    return o * lax.rsqrt(jnp.mean(o * o, axis=-1, keepdims=True) + NORM_EPS) * norm_w * _silu(gate)


GDN_PREC = ("bf", "bf")
HGRN_PREC = "bf"


def _each(fn, *cols):
    return [fn(*a) for a in zip(*cols)]


@functools.partial(jax.custom_vjp, nondiff_argnums=(2,))
def _known_inverse(low, inv, prec):
    del low, prec
    return inv


def _known_inverse_fwd(low, inv, prec):
    del low
    return inv, inv


def _known_inverse_bwd(prec, inv, ct):
    return -_mm_raw(_mm_raw(inv, ct, TN, prec), inv, NT, prec), jnp.zeros_like(inv)


_known_inverse.defvjp(_known_inverse_fwd, _known_inverse_bwd)


def gdn_stages(hs, qc, kc, vc, zc, ab, a_log_l, dt_l, norm_w, s, prec=GDN_PREC, inv_known=None):
    p_inv, p_mm = prec
    c = CHUNK
    ri, ci = _iota2((c, c), 0), _iota2((c, c), 1)
    incl, strict, eye = ri >= ci, ri > ci, ri == ci
    lane = _iota2((c, LANES), 1)
    last_row = _iota2((c, 1), 0) == c - 1
    rowsum = lambda x: jnp.sum(x, axis=1, keepdims=True)

    def row(col):
        return jnp.sum(jnp.where(eye, col, 0.0), axis=0, keepdims=True)

    q = _each(lambda x: x * lax.rsqrt(rowsum(x * x) + L2_EPS) * (HEAD_DIM ** -0.5), qc)
    k = _each(lambda x: x * lax.rsqrt(rowsum(x * x) + L2_EPS), kc)
    yield
    a_col = [rowsum(jnp.where(lane == h, ab, 0.0)) for h in hs]
    b_col = [rowsum(jnp.where(lane == h + N_HEADS, ab, 0.0)) for h in hs]
    beta = _each(jax.nn.sigmoid, b_col)
    g = _each(lambda a, al, dl: rowsum(jnp.where(lane == 0, -jnp.exp(al) * _softplus(a + dl), 0.0)), a_col, a_log_l, dt_l)
    gcum = _each(lambda x: rowsum(jnp.where(incl, row(x), 0.0)), g)
    g_last = _each(lambda x: jnp.sum(jnp.where(last_row, x, 0.0), axis=0, keepdims=True), gcum)
    decay = _each(lambda x: jnp.exp(jnp.where(incl, x - row(x), -jnp.inf)), gcum)
    yield
    kk = _each(lambda x: mm(x, x, NT, p_mm), k)
    low = _each(lambda b, x, d: jnp.where(strict, b * x * d, 0.0), beta, kk, decay)
    yield
    if inv_known is None:
        power = _each(lambda x: -x, low)
        inv = _each(lambda x: jnp.where(eye, 1.0, 0.0) + x, power)
        for _ in range(5):
            power = _each(lambda x: mm(x, x, NN, p_inv), power)
            yield
            inv = _each(lambda x, p: x + mm(x, p, NN, p_inv), inv, power)
            yield
    else:
        inv = _each(lambda x, known: _known_inverse(x, known, p_inv), low, inv_known)
    exp_g = _each(jnp.exp, gcum)
    yield
    u_v = _each(lambda i, b, x: mm(i, b * x, NN, p_mm), inv, beta, vc)
    w = _each(lambda i, b, e, x: mm(i, b * e * x, NN, p_mm), inv, beta, exp_g, k)
    yield
    attn = _each(lambda x, y, d: mm(x, y, NT, p_mm) * d, q, k, decay)
    yield
    u = _each(lambda x, y, z: x - mm(y, z, NN, p_mm), u_v, w, s)
    yield
    o = _each(lambda x, e, z: mm(x * e, z, NN, p_mm), q, exp_g, s)
    o = _each(lambda x, a, y: x + mm(a, y, NN, p_mm), o, attn, u)
    yield
    k_end = _each(lambda x, gl, gc: x * jnp.exp(gl - gc), k, g_last, gcum)
    s_new = _each(lambda z, gl, x, y: z * jnp.exp(gl) + mm(x, y, TN, p_mm), s, g_last, k_end, u)
    return (_each(lambda x, z: _head_norm_gate(x, norm_w, z), o, zc), s_new), inv


def gdn_chunk(h, qc, kc, vc, zc, ab, a_log_l, dt_l, norm_w, s, prec=GDN_PREC, reuse_inverse=False):
    args = ([h], [qc], [kc], [vc], [zc], ab, [a_log_l], [dt_l], norm_w, [s], prec)
    if reuse_inverse:
        inv = lax.stop_gradient(gdn_chunks(*args)[1])
        (y, s_new), _ = gdn_chunks(*args, inv_known=inv)
    else:
        (y, s_new), _ = gdn_chunks(*args)
    return y[0], s_new[0]


DIAG_ROWS = SUB_CHUNK // 2
SHIFT_PAD = 8
SHIFT_ROWS = SHIFT_PAD + CHUNK + SHIFT_PAD
SHIFT_WAYS = 4


class RolledRows:
    def down(self, x, which):
        del which
        return [x] + [pltpu.roll(x, off, 0) for off in range(1, DIAG_ROWS)]

    def up_sum(self, parts, which):
        del which
        acc = parts[0]
        for off in range(1, DIAG_ROWS):
            acc = acc + pltpu.roll(parts[off], CHUNK - off, 0)
        return acc


class SlotRows:
    def __init__(self, slots):
        self.slots = slots

    def down(self, x, which):
        self.slots[which, 0, SHIFT_PAD:SHIFT_PAD + CHUNK, :] = x
        return [x] + [self.slots[which, 0, SHIFT_PAD - off:SHIFT_PAD + CHUNK - off, :] for off in range(1, DIAG_ROWS)]

    def up_sum(self, parts, which):
        acc = parts[0]
        for off in range(1, DIAG_ROWS):
            way = 1 + off % (SHIFT_WAYS - 1)
            self.slots[which, way, SHIFT_PAD:SHIFT_PAD + CHUNK, :] = parts[off]
            acc = acc + self.slots[which, way, SHIFT_PAD + off:SHIFT_PAD + CHUNK + off, :]
        return acc


def _sub_block_rows():
    return jnp.bitwise_and(_iota2((CHUNK, 1), 0), DIAG_ROWS - 1)


def _diag_forward(rows, q, key, bc, v):
    rmod = _sub_block_rows()
    k_d, b_d, v_d = rows.down(key, 0), rows.down(bc, 1), rows.down(v, 2)
    o = None
    for off in range(DIAG_ROWS):
        e = jnp.exp(jnp.where(rmod >= off, bc - b_d[off], -jnp.inf))
        term = jnp.sum(q * k_d[off] * e, axis=-1, keepdims=True) * v_d[off]
        o = term if o is None else o + term
    return o


def _diag_backward(rows, q, key, bc, v, do):
    rmod = _sub_block_rows()
    k_d, b_d, v_d = rows.down(key, 0), rows.down(bc, 1), rows.down(v, 2)
    dq = db = None
    dk_parts, db_parts, dv_parts = [], [], []
    for off in range(DIAG_ROWS):
        e = jnp.exp(jnp.where(rmod >= off, bc - b_d[off], -jnp.inf))
        qe = q * e
        a = jnp.sum(qe * k_d[off], axis=-1, keepdims=True)
        da = jnp.sum(do * v_d[off], axis=-1, keepdims=True)
        dv_parts.append(a * do)
        dq_term = (da * e) * k_d[off]
        dk_term = da * qe
        s = dk_term * k_d[off]
        dq = dq_term if dq is None else dq + dq_term
        db = s if db is None else db + s
        dk_parts.append(dk_term)
        db_parts.append(s)
    return dq, rows.up_sum(dk_parts, 0), db - rows.up_sum(db_parts, 1), rows.up_sum(dv_parts, 2)


def diag_part(rows, differentiable=True):
    forward = functools.partial(_diag_forward, rows)
    if not differentiable:
        return forward
    part = jax.custom_vjp(forward)
    part.defvjp(lambda q, key, bc, v: (forward(q, key, bc, v), (q, key, bc, v)),
                lambda res, do: _diag_backward(rows, *res, do))
    return part


def hgrn_stages(qb, fb, ib, gb, l0, l1, norm_w, st, prec=HGRN_PREC, diags=None, o_known=None):
    c = CHUNK
    ri, ci = _iota2((4 * c, c), 0), _iota2((4 * c, c), 1)
    rcol = _iota2((c, 1), 0)
    blk0 = jnp.bitwise_and(ri, c - SUB_CHUNK)
    limit = jnp.where(ri < c, ri + 1, jnp.where(ri < 2 * c, blk0, jnp.where(ri < 3 * c, blk0 + SUB_CHUNK,
                                                                          blk0 + DIAG_ROWS)))
    sel = jnp.where(ci < limit, 1.0, 0.0)
    ri, ci = _iota2((c, c), 0), _iota2((c, c), 1)
    lb = _each(lambda a, b: jax.nn.sigmoid(a - b), l0, l1)
    forget = _each(lambda b, f: b + (1.0 - b) * jax.nn.sigmoid(f), lb, fb)
    key = _each(lambda b, f: (1.0 - b) * jax.nn.sigmoid(-f), lb, fb)
    q = _each(_silu, qb)
    v = ib
    logf = _each(jnp.log, forget)
    sums = _each(lambda x: sel_sums(sel, x), logf)
    bc, b_start, b_end, b_half = ([x[i] for x in sums] for i in range(4))
    b_last = _each(lambda x: jnp.sum(x, axis=0, keepdims=True), logf)
    o = _each(lambda x, b, z: mm(x * jnp.exp(b), z, NT, prec), q, bc, st)
    if diags is None:
        diags = [diag_part(RolledRows())] * len(qb)
    yield
    o = list(o)
    for h in range(len(o)):
        o[h] = o[h] + diags[h](q[h], key[h], bc[h], v[h])
        yield
    second = jnp.bitwise_and(rcol, SUB_CHUNK - 1) >= DIAG_ROWS
    same_sub = jnp.bitwise_and(ri, c - SUB_CHUNK) == jnp.bitwise_and(ci, c - SUB_CHUNK)
    q_half = _each(lambda x, b, bh: x * jnp.exp(jnp.where(second, b - bh, -jnp.inf)), q, bc, b_half)
    k_half = _each(lambda x, b, bh: x * jnp.exp(jnp.where(second, -jnp.inf, bh - b)), key, bc, b_half)
    a_half = _each(lambda x, z: jnp.where(same_sub, mm(x, z, NT, prec), 0.0), q_half, k_half)
    o = _each(lambda acc, a, val: acc + mm(a, val, NN, prec), o, a_half, v)
    yield
    q_rel = _each(lambda x, b, bs: x * jnp.exp(b - bs), q, bc, b_start)
    k_rel = _each(lambda x, b, be: x * jnp.exp(be - b), key, bc, b_end)
    for y in range(c // SUB_CHUNK - 1):
        def scaled(x, b, bs):
            end_y = jnp.sum(jnp.where(rcol == SUB_CHUNK * y + SUB_CHUNK - 1, b, 0.0), axis=0, keepdims=True)
            return x * jnp.exp(jnp.where(rcol >= SUB_CHUNK * (y + 1), bs - end_y, -jnp.inf))
        dq = _each(scaled, q_rel, bc, b_start)
        in_y = (ci >= SUB_CHUNK * y) & (ci < SUB_CHUNK * (y + 1))
        a_y = _each(lambda x, z: jnp.where(in_y, mm(x, z, NT, prec), 0.0), dq, k_rel)
        o = _each(lambda acc, a, val: acc + mm(a, val, NN, prec), o, a_y, v)
        yield
    k_state = _each(lambda x, bl, b: x * jnp.exp(bl - b), key, b_last, bc)
    st_new = _each(lambda z, bl, val, x: z * jnp.exp(bl) + mm(val, x, TN, prec), st, b_last, v, k_state)
    if o_known is not None:
        o = _each(_known_value, o, o_known)
    return (_each(lambda x, z: _head_norm_gate(x, norm_w, z), o, gb), st_new), o


def _drain(gen):
    try:
        while True:
            next(gen)
    except StopIteration as done:
        return done.value


def _alternate(gen_a, gen_b):
    out, live = [None, None], [gen_a, gen_b]
    while any(g is not None for g in live):
        for i, g in enumerate(live):
            if g is None:
                continue
            try:
                next(g)
            except StopIteration as done:
                out[i], live[i] = done.value, None
    return out


def gdn_chunks(*args, **kwargs):
    return _drain(gdn_stages(*args, **kwargs))


def hgrn_chunks(*args, **kwargs):
    return _drain(hgrn_stages(*args, **kwargs))


def hgrn_chunk(qb, fb, ib, gb, l0, l1, norm_w, st, prec=HGRN_PREC, reuse_output=False):
    args = ([qb], [fb], [ib], [gb], [l0], [l1], norm_w, [st], prec)
    if reuse_output:
        known = lax.stop_gradient(hgrn_chunks(*args)[1])
        (y, st_new), _ = hgrn_chunks(*args, o_known=known)
    else:
        (y, st_new), _ = hgrn_chunks(*args)
    return y[0], st_new[0]


HEAD_VEC = (N_HEADS, 1, LANES)


class _ChunkSpecs:
    def __init__(self, nc, rev):
        self.nc, self.rev = nc, rev

    def _c(self, c):
        return self.nc - 1 - c if self.rev else c

    def row(self, width, block=0):
        return pl.BlockSpec((CHUNK, width), lambda c: (self._c(c), block))

    def per_head(self, rows):
        return pl.BlockSpec((None, N_HEADS, rows, rows), lambda c: (self._c(c), 0, 0, 0))

    @staticmethod
    def whole(shape):
        return pl.BlockSpec(shape, lambda c: (0,) * len(shape))


def _lanes(j):
    return slice(j * LANES, (j + 1) * LANES)


def mixer_fwd(qkv_c, proj, a_log_l, dt_l, gdn_norm_w, l0, l1, hgrn_norm_w, name):
    t = qkv_c.shape[0]
    hb = N_HEADS
    sp = _ChunkSpecs(t // CHUNK, rev=False)
    hs = list(range(hb))

    def body(q_ref, k_ref, v_ref, z_ref, ab_ref, al_ref, dt_ref, gnw_ref, qb_ref, fb_ref, ib_ref, gb_ref, l0_ref, l1_ref,
             hnw_ref, y_ref, hist_a_ref, inv_ref, hist_b_ref, o_ref, sa_ref, sb_ref, shift_ref):
        @pl.when(pl.program_id(0) == 0)
        def _():
            sa_ref[...] = jnp.zeros_like(sa_ref)
            sb_ref[...] = jnp.zeros_like(sb_ref)
            shift_ref[...] = jnp.zeros_like(shift_ref)

        heads = lambda ref: [ref[:, _lanes(j)] for j in hs]
        s_a, s_b = [sa_ref[h] for h in hs], [sb_ref[h] for h in hs]
        for h in hs:
            hist_a_ref[h] = s_a[h]
            hist_b_ref[h] = s_b[h]
        diags = [diag_part(SlotRows(shift_ref.at[h]), differentiable=False) for h in hs]
        ((y_a, s_a_new), inv), ((y_b, s_b_new), o_pre) = _alternate(
            gdn_stages(hs, heads(q_ref), heads(k_ref), heads(v_ref), heads(z_ref), ab_ref[...],
                       [al_ref[h] for h in hs], [dt_ref[h] for h in hs], gnw_ref[...], s_a),
            hgrn_stages(heads(qb_ref), heads(fb_ref), heads(ib_ref), heads(gb_ref),
                        [l0_ref[h] for h in hs], [l1_ref[h] for h in hs], hnw_ref[...], s_b, diags=diags))
        for h in hs:
            y_ref[:, _lanes(h)] = y_a[h].astype(BF16)
            y_ref[:, _lanes(hb + h)] = y_b[h].astype(BF16)
            o_ref[:, _lanes(h)] = o_pre[h]
            sa_ref[h] = s_a_new[h]
            sb_ref[h] = s_b_new[h]
            inv_ref[h] = inv[h]

    vec, gain, slab = sp.whole(HEAD_VEC), sp.whole((1, LANES)), functools.partial(sp.row, GDN_WIDTH)
    states = jax.ShapeDtypeStruct((sp.nc, N_HEADS, HEAD_DIM, HEAD_DIM), F32)
    return pl.pallas_call(
        body, name=name, grid=(sp.nc,),
        in_specs=[slab(0), slab(1), slab(2), slab(3), sp.row(LANES, AB_BLOCK), vec, vec, gain,
                  slab(4), slab(5), slab(6), slab(7), vec, vec, gain],
        out_specs=[sp.row(2 * GDN_WIDTH), sp.per_head(HEAD_DIM), sp.per_head(CHUNK), sp.per_head(HEAD_DIM), slab(0)],
        out_shape=[jax.ShapeDtypeStruct((t, 2 * GDN_WIDTH), BF16), states,
                   jax.ShapeDtypeStruct((sp.nc, N_HEADS, CHUNK, CHUNK), F32), states,
                   jax.ShapeDtypeStruct((t, GDN_WIDTH), F32)],
        scratch_shapes=[pltpu.VMEM((N_HEADS, HEAD_DIM, HEAD_DIM), F32), pltpu.VMEM((N_HEADS, HEAD_DIM, HEAD_DIM), F32),
                        pltpu.VMEM((hb, 3, SHIFT_WAYS, SHIFT_ROWS, LANES), F32)],
        compiler_params=_params(("arbitrary",)),
    )(qkv_c, qkv_c, qkv_c, proj, proj, a_log_l, dt_l, gdn_norm_w, proj, proj, proj, proj, l0, l1, hgrn_norm_w)


def mixer_bwd(qkv_c, proj, a_log_l, dt_l, gdn_norm_w, l0, l1, hgrn_norm_w, hist_a, inv_hist, hist_b, o_pre, dy, name):
    t = qkv_c.shape[0]
    hb = N_HEADS
    sp = _ChunkSpecs(t // CHUNK, rev=True)
    hs = list(range(hb))

    def body(q_ref, k_ref, v_ref, z_ref, ab_ref, al_ref, dt_ref, gnw_ref, qb_ref, fb_ref, ib_ref, gb_ref, l0_ref, l1_ref,
             hnw_ref, hist_a_ref, inv_ref, hist_b_ref, o_ref, dy_ref,
             dqkv_ref, dproj_ref, dal_ref, ddt_ref, dgnw_ref, dl0_ref, dl1_ref, dhnw_ref, dsa_ref, dsb_ref, shift_ref):
        @pl.when(pl.program_id(0) == 0)
        def _():
            for ref in (dal_ref, ddt_ref, dgnw_ref, dl0_ref, dl1_ref, dhnw_ref, dsa_ref, dsb_ref, shift_ref):
                ref[...] = jnp.zeros_like(ref)

        heads = lambda ref, first=0: [ref[:, _lanes(first + j)] for j in hs]
        diags = [diag_part(SlotRows(shift_ref.at[h])) for h in hs]
        inv_known, o_known = [inv_ref[h] for h in hs], heads(o_ref)

        def both(ga, gb):
            (ra, inv), (rb, o_pre) = _alternate(gdn_stages(hs, *ga, inv_known=inv_known),
                                                hgrn_stages(*gb, diags=diags, o_known=o_known))
            return (ra, rb), (inv, o_pre)

        ga = (heads(q_ref), heads(k_ref), heads(v_ref), heads(z_ref), ab_ref[...], [al_ref[h] for h in hs],
              [dt_ref[h] for h in hs], gnw_ref[...], [hist_a_ref[h] for h in hs])
        gb = (heads(qb_ref), heads(fb_ref), heads(ib_ref), heads(gb_ref), [l0_ref[h] for h in hs],
              [l1_ref[h] for h in hs], hnw_ref[...], [hist_b_ref[h] for h in hs])
        _, vjp, _ = jax.vjp(both, ga, gb, has_aux=True)
        dy_a = [x.astype(F32) for x in heads(dy_ref)]
        dy_b = [x.astype(F32) for x in heads(dy_ref, hb)]
        (dq, dk, dv, dz, dab, dal, ddt, dgnw, ds_a), (dqb, dfb, dib, dgb, dl0, dl1, dhnw, ds_b) = vjp(
            ((dy_a, [dsa_ref[h] for h in hs]), (dy_b, [dsb_ref[h] for h in hs])))
        for h in hs:
            dqkv_ref[:, _lanes(h)] = dq[h]
            dqkv_ref[:, _lanes(hb + h)] = dk[h]
            dqkv_ref[:, _lanes(2 * hb + h)] = dv[h]
            for slab, val in enumerate((dz, dqb, dfb, dib, dgb)):
                dproj_ref[:, _lanes((3 + slab) * hb + h)] = val[h].astype(BF16)
            dal_ref[h] += dal[h]
            ddt_ref[h] += ddt[h]
            dl0_ref[h] += dl0[h]
            dl1_ref[h] += dl1[h]
            dsa_ref[h] = ds_a[h]
            dsb_ref[h] = ds_b[h]
        dproj_ref[:, MAIN_WIDTH:] = dab.astype(BF16)
        dgnw_ref[...] += dgnw
        dhnw_ref[...] += dhnw

    vec, gain, slab = sp.whole(HEAD_VEC), sp.whole((1, LANES)), functools.partial(sp.row, GDN_WIDTH)
    vec_shape, gain_shape = jax.ShapeDtypeStruct(HEAD_VEC, F32), jax.ShapeDtypeStruct((1, LANES), F32)
    return pl.pallas_call(
        body, name=name, grid=(sp.nc,),
        in_specs=[slab(0), slab(1), slab(2), slab(3), sp.row(LANES, AB_BLOCK), vec, vec, gain,
                  slab(4), slab(5), slab(6), slab(7), vec, vec, gain,
                  sp.per_head(HEAD_DIM), sp.per_head(CHUNK), sp.per_head(HEAD_DIM), slab(0), sp.row(2 * GDN_WIDTH)],
        out_specs=[sp.row(QKV_WIDTH), sp.row(CAT_WIDTH), vec, vec, gain, vec, vec, gain],
        out_shape=[jax.ShapeDtypeStruct((t, QKV_WIDTH), F32), jax.ShapeDtypeStruct((t, CAT_WIDTH), BF16),
                   vec_shape, vec_shape, gain_shape, vec_shape, vec_shape, gain_shape],
        scratch_shapes=[pltpu.VMEM((N_HEADS, HEAD_DIM, HEAD_DIM), F32), pltpu.VMEM((N_HEADS, HEAD_DIM, HEAD_DIM), F32),
                        pltpu.VMEM((hb, 3, SHIFT_WAYS, SHIFT_ROWS, LANES), F32)],
        compiler_params=_params(("arbitrary",)),
    )(qkv_c, qkv_c, qkv_c, proj, proj, a_log_l, dt_l, gdn_norm_w, proj, proj, proj, proj, l0, l1, hgrn_norm_w,
      hist_a, inv_hist, hist_b, o_pre, dy)


def _adamw(w, g, m, v):
    m = ADAM_B1 * m + (1.0 - ADAM_B1) * g
    v = ADAM_B2 * v + (1.0 - ADAM_B2) * jnp.square(g)
    m_hat = m / (1.0 - ADAM_B1 ** ADAM_STEP)
    v_hat = v / (1.0 - ADAM_B2 ** ADAM_STEP)
    delta = -ADAM_LR * (m_hat / (jnp.sqrt(v_hat) + ADAM_EPS) + ADAM_WD * w)
    return delta, m, v


def adamw_reduce(parts, mine, slot, w, m, v, name, rb=128):
    r, c = w.shape
    rb = min(rb, r)
    n_parts = parts.shape[0]

    def body(slot_ref, p_ref, own_ref, w_ref, m_ref, v_ref, g_ref, d_ref, mo_ref, vo_ref):
        part = lambda d: jnp.where(slot_ref[0] == d, own_ref[...], p_ref[d]).astype(F32)
        g = part(0)
        for d in range(1, n_parts):
            g = g + part(d)
        delta, mn, vn = _adamw(w_ref[...], g, m_ref[...], v_ref[...])
        g_ref[...] = g
        d_ref[...] = delta
        mo_ref[...] = mn
        vo_ref[...] = vn

    blk = pl.BlockSpec((rb, c), lambda i, s: (i, 0))
    return pl.pallas_call(
        body, name=name,
        grid_spec=pltpu.PrefetchScalarGridSpec(
            num_scalar_prefetch=1, grid=(r // rb,),
            in_specs=[pl.BlockSpec((n_parts, rb, c), lambda i, s: (0, i, 0)),
                      pl.BlockSpec((None, rb, c), lambda i, s: (s[0], i, 0)), blk, blk, blk],
            out_specs=[blk] * 4),
        out_shape=[jax.ShapeDtypeStruct((r, c), F32)] * 4,
        compiler_params=_params(("parallel",)))(slot.astype(jnp.int32).reshape(1), parts, mine, w, m, v)


def adamw_small(ws, gs, ms, vs, name):
    n = len(ws)

    def body(*refs):
        for i in range(n):
            w_ref, g_ref, m_ref, v_ref = (refs[j * n + i] for j in range(4))
            outs = _adamw(w_ref[...], g_ref[...], m_ref[...], v_ref[...])
            for j, o in enumerate(outs):
                refs[(4 + j) * n + i][...] = o

    vmem = pl.BlockSpec(memory_space=pltpu.VMEM)
    res = pl.pallas_call(body, name=name, in_specs=[vmem] * (4 * n), out_specs=[vmem] * (3 * n),
                         out_shape=[jax.ShapeDtypeStruct(w.shape, F32) for w in ws] * 3)(*ws, *gs, *ms, *vs)
    return res[:n], res[n:2 * n], res[2 * n:]


def _pack(arrays):
    flat = jnp.concatenate([a.reshape(-1).astype(F32) for a in arrays])
    rows = -(-flat.shape[0] // (8 * LANES)) * 8
    return jnp.pad(flat, (0, rows * LANES - flat.shape[0])).reshape(rows, LANES)


def _unpack(packed, shapes):
    flat, out, off = packed.reshape(-1), [], 0
    for s in shapes:
        n = 1
        for d in s:
            n *= d
        out.append(flat[off:off + n].reshape(s))
        off += n
    return out


def _relu2_epilogue(acc, _):
    r = jnp.maximum(acc, 0.0)
    return acc, r * r


def _relu2_bwd_epilogue(acc, a1):
    return (acc * (2.0 * jnp.maximum(a1, 0.0)),)


def kernel(x, w_in, conv_w, gdn_a_log, gdn_dt_bias, gdn_norm_w, hgrn_lb_logits, hgrn_norm_w, w_out, norm_mix_w, norm_ffn_w, w_ff1, w_ff2, norm_final_w, loss_target, m_w_in, m_conv_w, m_gdn_a_log, m_gdn_dt_bias, m_gdn_norm_w, m_hgrn_lb_logits, m_hgrn_norm_w, m_w_out, m_norm_mix_w, m_norm_ffn_w, m_w_ff1, m_w_ff2, m_norm_final_w, v_w_in, v_conv_w, v_gdn_a_log, v_gdn_dt_bias, v_gdn_norm_w, v_hgrn_lb_logits, v_hgrn_norm_w, v_w_out, v_norm_mix_w, v_norm_ffn_w, v_w_ff1, v_w_ff2, v_norm_final_w):
    me = _my_flat()
    xs = x[0]
    target = loss_target[0]
    shard_in = w_in.shape[2]
    shard_conv = conv_w.shape[2]

    tok = lambda t: t[0:1, 0:1]

    half = D_MODEL // 2
    w_in_b = w_in[0].astype(BF16)
    h_ga, t_ga = exchange_start([w_in_b[:half], conv_w[0]], True, "gather_w_in_high_start", peers=CHIP_PEERS)
    h_g0, t_g0 = exchange_start([w_in_b[half:]], True, "gather_w_in_low_start", after=[t_ga], peers=CHIP_PEERS)
    behind = lambda a: lax.optimization_barrier((a, t_g0))[0]
    h_g1, t_g1 = exchange_start([behind(w_out[0]).astype(BF16), behind(w_ff1[0]).astype(BF16)], True,
                                "gather_mid_start", after=[t_g0], peers=CHIP_PEERS)
    h_g2, t_g2 = exchange_start([behind(w_ff2[0]).astype(BF16)], True, "gather_ff2_start", after=[t_g1],
                                peers=CHIP_PEERS)
    m_in, v_in, _ = lax.optimization_barrier((m_w_in, v_w_in, t_g2))
    m_in, v_in = m_in[0], v_in[0]

    lane_b = lambda p: jnp.broadcast_to(p.reshape(N_HEADS, 1, 1), HEAD_VEC)
    a_log_l, dt_l = lane_b(gdn_a_log[0]), lane_b(gdn_dt_bias[0])
    l0 = hgrn_lb_logits[0].reshape(HEAD_VEC)
    l1 = hgrn_lb_logits[1].reshape(HEAD_VEC)

    n1, r1 = rms_fwd(xs, norm_mix_w + tok(t_g1) + tok(t_g2), "rms_mix")
    (s_high, s_conv), (l_high, l_conv) = exchange_wait(h_ga, "gather_w_in_high_wait", after=[n1, m_in, v_in],
                                                       copies=len(CHIP_PEERS))
    h_fa, _ = forward_start([l_high, l_conv], "gather_w_in_high_forward")
    _, (l_high, l_conv) = exchange_wait(h_fa, "forward_w_in_high_wait", copies=len(OTHER_CHIPS))
    w_cat = weights_to_cat(_own_slot(l_high, s_high), "weights_to_cat", D_MODEL)
    conv_full = jnp.transpose(_own_slot(l_conv, s_conv), (1, 0, 2)).reshape(4, QKV_WIDTH)
    proj = matmul(n1, w_cat, "nn", "in_proj_high", (BF16,), tn=CAT_WIDTH // 5, tk=half, k_blocks=(0, 1))
    (s_low,), (l_low,) = exchange_wait(h_g0, "gather_w_in_low_wait", after=[proj], copies=len(CHIP_PEERS))
    h_f0, _ = forward_start([l_low], "gather_w_in_low_forward")
    _, (l_low,) = exchange_wait(_one(h_f0, 0), "forward_w_in_low_wait", copies=len(OTHER_CHIPS))
    w_cat = weights_to_cat(_own_slot(l_low, s_low), "weights_to_cat_low", D_MODEL, row0=half, into=w_cat)
    proj = matmul_ring(n1, w_cat, "nn", "in_proj_low", F32, tm=1024, tn=CAT_WIDTH // 5, tk=half, k_block=1,
                       extra=proj, epilogue=lambda acc, high: (acc + high,))
    qkv_c = conv_fwd(proj, conv_full, "conv_fwd")
    y, hist_a, inv_a, hist_b, o_b = mixer_fwd(qkv_c, proj, a_log_l, dt_l, gdn_norm_w, l0, l1, hgrn_norm_w, "mixer_fwd")
    (s_out, s_ff1), (l_out, l_ff1) = exchange_wait(h_g1, "gather_mid_wait", after=[y], copies=len(CHIP_PEERS))
    (s_ff2,), (l_ff2,) = exchange_wait(h_g2, "gather_ff2_wait", after=[y], copies=len(CHIP_PEERS))
    h_fw, _ = forward_start([l_out, l_ff1, l_ff2], "gather_forward_start")
    _, (l_out,) = exchange_wait(_one(h_fw, 0), "forward_out_wait", copies=len(OTHER_CHIPS))
    w_out_full = _own_slot(l_out, s_out).reshape(D_MODEL, D_MODEL)
    h1, n2, r2 = out_proj_rms(y, w_out_full, xs, norm_ffn_w, "out_proj_rms")
    _, (l_ff1,) = exchange_wait(_one(h_fw, 1), "forward_ff1_wait", after=[n2], copies=len(OTHER_CHIPS))
    w_ff1_sh = _own_slot(l_ff1, s_ff1)
    a1, act = matmul(n2, w_ff1_sh, "nn", "ff1", out_dtypes=(F32, BF16), epilogue=_relu2_epilogue, b_shards=True)
    _, (l_ff2,) = exchange_wait(_one(h_fw, 2), "forward_ff2_wait", after=[act], copies=len(OTHER_CHIPS))
    w_ff2_full = _own_slot(l_ff2, s_ff2).reshape(D_FF, D_MODEL)
    loss_sum, dh2_b, d_final = ff2_loss(act, w_ff2_full, h1, norm_final_w.reshape(1, D_MODEL), target, "ff2_loss")

    da1 = matmul(dh2_b, w_ff2_full, "nt", "d_act", out_dtypes=(BF16,), epilogue=_relu2_bwd_epilogue, extra=a1)
    t_all = xs.shape[0]
    dw_ff2 = matmul(act, dh2_b, "tn", "dw_ff2", out_dtypes=(BF16,), tk=t_all)
    p_ff2 = dw_ff2.reshape(N_DEV, D_FF // N_DEV, D_MODEL)
    h_s1, t_s1 = exchange_start([p_ff2], False, "scatter_ff2_start")
    dn2 = matmul(da1, w_ff1_sh, "nt", "d_n2", out_dtypes=(BF16,), after=[t_s1], b_shards=True, k_group=4)
    p_ff1 = matmul(n2, da1, "tn", "dw_ff1", out_dtypes=(BF16,), tn=D_FF // N_DEV, tk=t_all, after=[t_s1], out_shards=True)
    h_s2, t_s2 = exchange_start([p_ff1], False, "scatter_ff1_start")
    dh1_b, d_ffn = rms_bwd(h1, r2, norm_ffn_w + tok(t_s2), dn2, dh2_b, BF16, "rms_ffn_bwd")
    dmix = matmul(dh1_b, w_out_full, "nt", "d_mix", out_dtypes=(BF16,))
    dw_out = matmul(y, dh1_b, "tn", "dw_out", out_dtypes=(BF16,), tk=t_all)
    p_out = dw_out.reshape(N_DEV, D_MODEL // N_DEV, D_MODEL)
    h_s3, t_s3 = exchange_start([p_out], False, "scatter_out_start")
    d_qkv_c, dproj, d_alog_l, d_dt_l, d_gnw, dl0, dl1, d_hnw = mixer_bwd(
        qkv_c, proj, a_log_l, dt_l, gdn_norm_w + tok(t_s3), l0, l1, hgrn_norm_w, hist_a, inv_a, hist_b, o_b, dmix,
        "mixer_bwd")
    dproj, d_conv_full = conv_bwd(proj, d_qkv_c, conv_full, dproj, "conv_bwd")
    dw_cat = matmul(n1, dproj, "tn", "dw_in", out_dtypes=(BF16,), tm=512, tn=CAT_WIDTH // 5, tk=t_all)
    p_in = cat_to_shards(dw_cat, shard_in)
    h_pair, t_s4 = routed_start(p_in, _to_sibling_routes, "scatter_in_pair_start")

    (s_ff2g,), (r_ff2,) = exchange_wait(h_s1, "scatter_ff2_wait", after=[t_s4])
    (s_ff1g,), (r_ff1,) = exchange_wait(h_s2, "scatter_ff1_wait", after=[t_s4])
    (s_outg,), (r_out,) = exchange_wait(h_s3, "scatter_out_wait", after=[t_s4])
    g_w_ff2, d_w_ff2, nm_w_ff2, nv_w_ff2 = adamw_reduce(
        r_ff2, s_ff2g, me, w_ff2[0], m_w_ff2[0], v_w_ff2[0], "adamw_w_ff2")
    g_w_ff1, d_w_ff1, nm_w_ff1, nv_w_ff1 = adamw_reduce(
        r_ff1, s_ff1g, me, w_ff1[0], m_w_ff1[0], v_w_ff1[0], "adamw_w_ff1")
    g_w_out, d_w_out, nm_w_out, nv_w_out = adamw_reduce(
        r_out, s_outg, me, w_out[0], m_w_out[0], v_w_out[0], "adamw_w_out")
    (p_in,), (from_sibling,) = exchange_wait(h_pair, "scatter_in_pair_wait", after=[d_w_ff2, d_w_ff1, d_w_out],
                                             copies=N_CHIPS)
    chip_sums = pair_sum(p_in, from_sibling, "scatter_in_pair_sum")
    h_chips, t_s5 = routed_start(chip_sums, _to_chips_routes, "scatter_in_chips_start")
    dn1 = matmul_ring(dproj, w_cat, "nt", "d_n1", BF16, tm=512, tn=512, tk=CAT_WIDTH, after=[t_s5])
    dx, d_mix = rms_bwd(xs, r1, norm_mix_w, dn1, dh1_b, F32, "rms_mix_bwd")
    (chip_sums,), (r_in,) = exchange_wait(h_chips, "scatter_in_chips_wait", after=[dx], copies=len(OTHER_CHIPS))
    g_w_in, d_w_in, nm_w_in, nv_w_in = adamw_reduce(
        r_in, chip_sums, me // 2, w_in[0], m_in, v_in, "adamw_w_in")

    d_lb = jnp.stack([dl0.reshape(GDN_WIDTH), dl1.reshape(GDN_WIDTH)])
    small_shapes = [(1, N_HEADS), (1, N_HEADS), (1, HEAD_DIM), (2, GDN_WIDTH), (1, HEAD_DIM), (1, D_MODEL),
                    (1, D_MODEL), (D_MODEL,), (4, QKV_WIDTH), ()]
    small = _pack([d_alog_l[:, 0, 0], d_dt_l[:, 0, 0], d_gnw, d_lb, d_hnw, d_mix, d_ffn, d_final, d_conv_full,
                   loss_sum[0, 0]])
    red = allreduce_small(small, "allreduce_small")
    g_alog, g_dt, g_gnw, g_lb, g_hnw, g_mix, g_ffn, g_final, g_conv_full, loss = _unpack(red, small_shapes)
    g_conv = lax.dynamic_slice(g_conv_full, (0, me * shard_conv), (4, shard_conv)).reshape(1, 4, shard_conv)
    small_g = [g_alog, g_dt, g_gnw, g_lb, g_hnw, g_mix, g_ffn, g_final, g_conv]
    small_w = [gdn_a_log, gdn_dt_bias, gdn_norm_w, hgrn_lb_logits, hgrn_norm_w, norm_mix_w, norm_ffn_w, norm_final_w, conv_w]
    small_m = [m_gdn_a_log, m_gdn_dt_bias, m_gdn_norm_w, m_hgrn_lb_logits, m_hgrn_norm_w, m_norm_mix_w, m_norm_ffn_w,
               m_norm_final_w, m_conv_w]
    small_v = [v_gdn_a_log, v_gdn_dt_bias, v_gdn_norm_w, v_hgrn_lb_logits, v_hgrn_norm_w, v_norm_mix_w, v_norm_ffn_w,
               v_norm_final_w, v_conv_w]
    rows = lambda arrays: [a.reshape(-1, a.shape[-1]) for a in arrays]
    like_w = lambda arrays: [a.reshape(w.shape) for a, w in zip(arrays, small_w)]
    d_s, m_s, v_s = adamw_small(rows(small_w), rows(small_g), rows(small_m), rows(small_v), "adamw_small")
    d_alog, d_dt, d_gn, d_lbl, d_hn, d_nm, d_nf, d_nfin, d_cw = like_w(d_s)
    m_alog, m_dt, m_gn, m_lbl, m_hn, m_nm, m_nf, m_nfin, m_cw = like_w(m_s)
    v_alog, v_dt, v_gn, v_lbl, v_hn, v_nm, v_nf, v_nfin, v_cw = like_w(v_s)

    lead = lambda a: a[None]
    grads = [lead(g_w_in), g_conv, g_alog, g_dt, g_gnw, g_lb, g_hnw, lead(g_w_out), g_mix, g_ffn,
             lead(g_w_ff1), lead(g_w_ff2), g_final]
    deltas = [lead(d_w_in), d_cw, d_alog, d_dt, d_gn, d_lbl, d_hn, lead(d_w_out), d_nm, d_nf,
              lead(d_w_ff1), lead(d_w_ff2), d_nfin]
    new_m = [lead(nm_w_in), m_cw, m_alog, m_dt, m_gn, m_lbl, m_hn, lead(nm_w_out), m_nm, m_nf,
             lead(nm_w_ff1), lead(nm_w_ff2), m_nfin]
    new_v = [lead(nv_w_in), v_cw, v_alog, v_dt, v_gn, v_lbl, v_hn, lead(nv_w_out), v_nm, v_nf,
             lead(nv_w_ff1), lead(nv_w_ff2), v_nfin]
    return (loss, dx[None], *grads, *deltas, *new_m, *new_v)
```

```python
import functools

import jax
import jax.numpy as jnp
from jax import lax
from jax.experimental import pallas as pl
from jax.experimental.pallas import tpu as pltpu

F32 = jnp.float32
BF16 = jnp.bfloat16
HI = lax.Precision.HIGHEST

N_DEV = 8
D_MODEL = 2048
CHUNK = 64
SUB_CHUNK = 16
HEAD_DIM = 128
N_HEADS = 8
GDN_WIDTH = N_HEADS * HEAD_DIM
D_FF = 4 * D_MODEL
QKV_WIDTH = 3 * GDN_WIDTH
MAIN_WIDTH = 8 * GDN_WIDTH
CAT_WIDTH = MAIN_WIDTH + 128
AB_BLOCK = MAIN_WIDTH // 128
NORM_EPS = 1e-6
L2_EPS = 1e-6
LANES = 128
VMEM_LIMIT = 56 * 1024 * 1024

ADAM_LR = 0.001
ADAM_B1 = 0.9
ADAM_B2 = 0.999
ADAM_EPS = 1e-08
ADAM_WD = 0.01
ADAM_STEP = 10

MESH = pl.DeviceIdType.MESH


def _params(sem=None):
    return pltpu.CompilerParams(dimension_semantics=sem, vmem_limit_bytes=VMEM_LIMIT)


def _dot(a, b, dims, prec=None):
    return lax.dot_general(a, b, (dims, ((), ())), precision=prec, preferred_element_type=F32)


NN = ((1,), (0,))
NT = ((1,), (1,))
TN = ((0,), (0,))


def _split_bf16(x, pieces):
    out = []
    for _ in range(pieces - 1):
        p = x.astype(BF16)
        out.append(p)
        x = x - p.astype(F32)
    out.append(x.astype(BF16))
    return out


def _mm_raw(a, b, dims, prec):
    if prec == "hi":
        return _dot(a, b, dims, HI)
    if prec == "bf":
        return _dot(a.astype(BF16), b.astype(BF16), dims)
    a_hi, a_lo = _split_bf16(a, 2)
    b_hi, b_lo = _split_bf16(b, 2)
    return _dot(a_hi, b_hi, dims) + (_dot(a_hi, b_lo, dims) + _dot(a_lo, b_hi, dims))


@functools.partial(jax.custom_vjp, nondiff_argnums=(2, 3))
def mm(a, b, dims, prec):
    return _mm_raw(a, b, dims, prec)


def _mm_fwd(a, b, dims, prec):
    return _mm_raw(a, b, dims, prec), (a, b)


def _mm_bwd(dims, prec, res, ct):
    a, b = res
    if dims == NN:
        return _mm_raw(ct, b, NT, prec), _mm_raw(a, ct, TN, prec)
    if dims == NT:
        return _mm_raw(ct, b, NN, prec), _mm_raw(ct, a, TN, prec)
    return _mm_raw(b, ct, NT, prec), _mm_raw(a, ct, NN, prec)


mm.defvjp(_mm_fwd, _mm_bwd)


def _sel_raw(sel, x, dims):
    sel = sel.astype(BF16)
    p0, p1, p2 = _split_bf16(x, 3)
    return _dot(sel, p0, dims) + (_dot(sel, p1, dims) + _dot(sel, p2, dims))


def _sel_parts(sel, x):
    c = x.shape[0]
    full = _sel_raw(sel, x, NN)
    return tuple(full[i * c:(i + 1) * c] for i in range(sel.shape[0] // c))


@jax.custom_vjp
def sel_sums(sel, x):
    return _sel_parts(sel, x)


def _sel_fwd(sel, x):
    return _sel_parts(sel, x), sel


def _sel_bwd(sel, cts):
    return jnp.zeros_like(sel), _sel_raw(sel, jnp.concatenate(cts, axis=0), TN)


sel_sums.defvjp(_sel_fwd, _sel_bwd)


@jax.custom_vjp
def _known_value(computed, known):
    del computed
    return known


_known_value.defvjp(lambda computed, known: (known, None), lambda _, ct: (ct, jnp.zeros_like(ct)))


def _my_flat():
    return 4 * lax.axis_index("x") + 2 * lax.axis_index("y") + lax.axis_index("c")


def _peer(k):
    x, y, c = lax.axis_index("x"), lax.axis_index("y"), lax.axis_index("c")
    kx, ky, kc = (k >> 2) & 1, (k >> 1) & 1, k & 1
    px = (1 - x) if kx else x
    py = (1 - y) if ky else y
    pc = (1 - c) if kc else c
    return (px, py, pc), 4 * px + 2 * py + pc


HBM_SPEC = pl.BlockSpec(memory_space=pltpu.HBM)
SEM_SPEC = pl.BlockSpec(memory_space=pltpu.SEMAPHORE)
ANY_SPEC = pl.BlockSpec(memory_space=pl.ANY)
DATAFLOW = pltpu.SideEffectType.DATAFLOW_SIDE_EFFECTING


def _in_hbm(x):
    return pltpu.with_memory_space_constraint(x, pltpu.HBM)


ALL_PEERS = tuple(range(1, N_DEV))
CHIP_PEERS = (1, 2, 4, 6)
OTHER_CHIPS = (2, 4, 6)


def exchange_start(xs, gather, name, after=(), peers=ALL_PEERS):
    n, n_after = len(xs), len(after)

    def body(*refs):
        x_refs, land_refs = refs[:n], refs[n:2 * n]
        sems = refs[2 * n + n_after:2 * n + n_after + 2 * n]
        token = refs[-1]
        me = _my_flat()
        for k in peers:
            peer, peer_flat = _peer(k)
            for a in range(n):
                src = x_refs[a] if gather else x_refs[a].at[peer_flat]
                pltpu.make_async_remote_copy(src_ref=src, dst_ref=land_refs[a].at[me], send_sem=sems[a],
                                             recv_sem=sems[n + a], device_id=peer, device_id_type=MESH).start()
        token[...] = jnp.zeros_like(token)

    lands =[_in_hbm(lax.empty(((N_DEV,) + x.shape) if gather else x.shape, x.dtype)) for x in xs]
    hbm_out = [pltpu.HBM(x.shape, x.dtype) for x in xs] + [pltpu.HBM(l.shape, l.dtype) for l in lands]
    res = pl.pallas_call(
        body, name=name,
        out_shape=(*([pltpu.SemaphoreType.DMA(())] * (2 * n)), *hbm_out, jax.ShapeDtypeStruct((8, LANES), F32)),
        in_specs=[HBM_SPEC] * (2 * n) + [ANY_SPEC] * n_after,
        out_specs=(*([SEM_SPEC] * (2 * n)), *([HBM_SPEC] * (2 * n)), pl.BlockSpec(memory_space=pltpu.VMEM)),
        input_output_aliases={i: 2 * n + i for i in range(2 * n)},
        compiler_params=pltpu.CompilerParams(has_side_effects=DATAFLOW),
    )(*[_in_hbm(x) for x in xs], *lands, *after)
    return (list(res[:2 * n]), list(res[2 * n:3 * n]), list(res[3 * n:4 * n])), res[-1]


def forward_start(lands, name, after=()):
    n, n_after = len(lands), len(after)

    def body(*refs):
        land_refs = refs[:n]
        sems = refs[n + n_after:n + n_after + 2 * n]
        token = refs[-1]
        sibling, _ = _peer(1)
        for a in range(n):
            for k in OTHER_CHIPS:
                _, from_flat = _peer(k)
                slot = land_refs[a].at[from_flat]
                pltpu.make_async_remote_copy(src_ref=slot, dst_ref=slot, send_sem=sems[a], recv_sem=sems[n + a],
                                             device_id=sibling, device_id_type=MESH).start()
        token[...] = jnp.zeros_like(token)

    res = pl.pallas_call(
        body, name=name,
        out_shape=(*([pltpu.SemaphoreType.DMA(())] * (2 * n)), *[pltpu.HBM(l.shape, l.dtype) for l in lands],
                   jax.ShapeDtypeStruct((8, LANES), F32)),
        in_specs=[HBM_SPEC] * n + [ANY_SPEC] * n_after,
        out_specs=(*([SEM_SPEC] * (2 * n)), *([HBM_SPEC] * n), pl.BlockSpec(memory_space=pltpu.VMEM)),
        input_output_aliases={i: 2 * n + i for i in range(n)},
        compiler_params=pltpu.CompilerParams(has_side_effects=DATAFLOW),
    )(*lands, *after)
    return (list(res[:2 * n]), [], list(res[2 * n:3 * n])), res[-1]


def exchange_wait(handle, name, after=(), copies=N_DEV - 1):
    sems, xs, lands = handle
    n, n_x, n_after = len(lands), len(xs), len(after)

    def body(*refs):
        land_refs = refs[n_x:n_x + n]
        sem_refs = refs[n_x + n:n_x + 3 * n]
        for a in range(n):
            every = land_refs[a].at[pl.ds(0, copies)]
            cp = pltpu.make_async_remote_copy(src_ref=every, dst_ref=every, send_sem=sem_refs[a],
                                              recv_sem=sem_refs[n + a], device_id=_peer(1)[0], device_id_type=MESH)
            cp.wait_send()
            cp.wait_recv()

    res = pl.pallas_call(
        body, name=name,
        out_shape=[pltpu.HBM(x.shape, x.dtype) for x in xs] + [pltpu.HBM(l.shape, l.dtype) for l in lands],
        in_specs=[HBM_SPEC] * (n_x + n) + [SEM_SPEC] * (2 * n) + [ANY_SPEC] * n_after,
        out_specs=[HBM_SPEC] * (n_x + n),
        input_output_aliases={i: i for i in range(n_x + n)},
        compiler_params=pltpu.CompilerParams(has_side_effects=DATAFLOW),
    )(*xs, *lands, *sems, *after)
    return list(res[:n_x]), list(res[n_x:])


N_CHIPS = N_DEV // 2


def routed_start(x, routes, name, after=()):
    n_after = len(after)

    def body(*refs):
        x_ref, land_ref = refs[0], refs[1]
        send_sem, recv_sem = refs[2 + n_after], refs[3 + n_after]
        token = refs[-1]
        for src, dst, peer in routes():
            pltpu.make_async_remote_copy(src_ref=x_ref.at[src], dst_ref=land_ref.at[dst], send_sem=send_sem,
                                         recv_sem=recv_sem, device_id=peer, device_id_type=MESH).start()
        token[...] = jnp.zeros_like(token)

    land = _in_hbm(lax.empty((N_CHIPS,) + x.shape[1:], x.dtype))
    res = pl.pallas_call(
        body, name=name,
        out_shape=(pltpu.SemaphoreType.DMA(()), pltpu.SemaphoreType.DMA(()), pltpu.HBM(x.shape, x.dtype),
                   pltpu.HBM(land.shape, land.dtype), jax.ShapeDtypeStruct((8, LANES), F32)),
        in_specs=[HBM_SPEC, HBM_SPEC] + [ANY_SPEC] * n_after,
        out_specs=(SEM_SPEC, SEM_SPEC, HBM_SPEC, HBM_SPEC, pl.BlockSpec(memory_space=pltpu.VMEM)),
        input_output_aliases={0: 2, 1: 3},
        compiler_params=pltpu.CompilerParams(has_side_effects=DATAFLOW),
    )(_in_hbm(x), land, *after)
    return ([res[0], res[1]], [res[2]], [res[3]]), res[-1]


def _to_sibling_routes():
    c = lax.axis_index("c")
    sibling, _ = _peer(1)
    return [(2 * chip + 1 - c, chip, sibling) for chip in range(N_CHIPS)]


def _to_chips_routes():
    my_chip = _my_flat() // 2
    routes = []
    for k in OTHER_CHIPS:
        peer, peer_flat = _peer(k)
        routes.append((peer_flat // 2, my_chip, peer))
    return routes


def pair_sum(p, from_sibling, name, rb=1024):
    _, r, c = p.shape
    mine = lax.axis_index("c").astype(jnp.int32).reshape(1)

    def body(kind_ref, p_ref, s_ref, o_ref):
        del kind_ref
        o_ref[...] = (p_ref[...].astype(F32) + s_ref[...].astype(F32)).astype(BF16)

    return pl.pallas_call(
        body, name=name,
        grid_spec=pltpu.PrefetchScalarGridSpec(
            num_scalar_prefetch=1, grid=(N_CHIPS, r // rb),
            in_specs=[pl.BlockSpec((None, None, rb, c), lambda chip, i, kind: (chip, kind[0], i, 0)),
                      pl.BlockSpec((None, rb, c), lambda chip, i, kind: (chip, i, 0))],
            out_specs=pl.BlockSpec((None, rb, c), lambda chip, i, kind: (chip, i, 0))),
        out_shape=jax.ShapeDtypeStruct((N_CHIPS, r, c), BF16),
        compiler_params=_params(("parallel", "parallel")))(mine, p.reshape(N_CHIPS, 2, r, c), from_sibling)


def _one(handle, a):
    sems, xs, lands = handle
    n = len(lands)
    return [sems[a], sems[n + a]], xs[a:a + 1], [lands[a]]


def _own_slot(land, block):
    return lax.dynamic_update_slice(land, block[None], (_my_flat(),) + (0,) * block.ndim)


def allreduce_small(x, name):
    rows = x.shape[0]

    def body(x_ref, o_ref, buf, send_sems, recv_sems):
        me = _my_flat()
        buf[me] = x_ref[...]
        sends = []
        for k in range(1, N_DEV):
            peer, _ = _peer(k)
            cp = pltpu.make_async_remote_copy(
                src_ref=x_ref, dst_ref=buf.at[me], send_sem=send_sems.at[k], recv_sem=recv_sems.at[k],
                device_id=peer, device_id_type=MESH)
            cp.start()
            sends.append(cp)
        for k in range(1, N_DEV):
            peer, peer_flat = _peer(k)
            pltpu.make_async_remote_copy(
                src_ref=x_ref, dst_ref=buf.at[peer_flat], send_sem=send_sems.at[k], recv_sem=recv_sems.at[k],
                device_id=peer, device_id_type=MESH).wait_recv()
        for cp in sends:
            cp.wait_send()
        acc = buf[0]
        for d in range(1, N_DEV):
            acc = acc + buf[d]
        o_ref[...] = acc

    vmem = pl.BlockSpec(memory_space=pltpu.VMEM)
    return pl.pallas_call(
        body, name=name, out_shape=jax.ShapeDtypeStruct((rows, LANES), F32),
        in_specs=[vmem], out_specs=vmem,
        scratch_shapes=[pltpu.VMEM((N_DEV, rows, LANES), F32),
                        pltpu.SemaphoreType.DMA((N_DEV,)), pltpu.SemaphoreType.DMA((N_DEV,))],
    )(x)


def matmul(a, b, mode, name, out_dtypes=(F32,), epilogue=None, extra=None, tm=1024, tn=1024, tk=2048, after=(),
           b_shards=False, out_shards=False, k_group=1, k_blocks=None):
    if b_shards:
        n_sh, b_rows, b_cols = b.shape
    if mode == "nn":
        (m, kd), n = a.shape, (n_sh * b_cols if b_shards else b.shape[1])
        if b_shards:
            tn = b_cols
    elif mode == "nt":
        (m, kd), n = a.shape, (b_rows if b_shards else b.shape[0])
        if b_shards:
            tk = k_group * b_cols
    else:
        (kd, m), n = a.shape, b.shape[1]
    tm, tn, tk = min(tm, m), min(tn, n), min(tk, kd)
    assert m % tm == 0 and n % tn == 0 and kd % tk == 0, (name, m, n, kd, tm, tn, tk)
    k0, ksteps = (0, kd // tk) if k_blocks is None else k_blocks
    dims = {"nn": NN, "nt": NT, "tn": TN}[mode]
    n_out = len(out_dtypes)
    n_in = 2 + (extra is not None) + len(after)

    def finish(acc, e_ref, o_refs):
        outs = (acc,) if epilogue is None else epilogue(acc, e_ref[...] if e_ref is not None else None)
        for o_ref, o in zip(o_refs, outs):
            o_ref[...] = o.astype(o_ref.dtype)

    def product(a_ref, b_ref):
        if mode == "nt" and b_shards:
            w = b_cols
            parts = [_dot(a_ref[:, s * w:(s + 1) * w], b_ref[s], dims) for s in range(k_group)]
            return functools.reduce(lambda p, q: p + q, parts)
        return _dot(a_ref[...], b_ref[...], dims)

    def body(*refs):
        a_ref, b_ref = refs[0], refs[1]
        e_ref = refs[2] if extra is not None else None
        o_refs = refs[n_in:n_in + n_out]
        if ksteps == 1:
            finish(product(a_ref, b_ref), e_ref, o_refs)
            return
        acc_ref = refs[-1]
        kk = pl.program_id(2)

        @pl.when(kk == 0)
        def _():
            acc_ref[...] = jnp.zeros_like(acc_ref)

        acc_ref[...] += product(a_ref, b_ref)

        @pl.when(kk == ksteps - 1)
        def _():
            finish(acc_ref[...], e_ref, o_refs)

    if mode == "nn":
        a_spec = pl.BlockSpec((tm, tk), lambda i, j, k: (i, k0 + k))
        b_spec = (pl.BlockSpec((None, tk, tn), lambda i, j, k: (j, k, 0)) if b_shards
                  else pl.BlockSpec((tk, tn), lambda i, j, k: (k0 + k, j)))
    elif mode == "nt":
        a_spec = pl.BlockSpec((tm, tk), lambda i, j, k: (i, k))
        b_spec = (pl.BlockSpec((k_group, tn, b_cols), lambda i, j, k: (k, j, 0)) if b_shards
                  else pl.BlockSpec((tn, tk), lambda i, j, k: (j, k)))
    else:
        a_spec = pl.BlockSpec((tk, tm), lambda i, j, k: (k, i))
        b_spec = pl.BlockSpec((tk, tn), lambda i, j, k: (k, j))
    o_spec = pl.BlockSpec((tm, tn), lambda i, j, k: (i, j))
    res_spec = pl.BlockSpec((None, tm, tn), lambda i, j, k: (j, i, 0)) if out_shards else o_spec
    res_shape = (n // tn, m, tn) if out_shards else (m, n)
    in_specs = [a_spec, b_spec] + ([o_spec] if extra is not None else []) + [ANY_SPEC] * len(after)
    args = (a, b) + ((extra,) if extra is not None else ()) + tuple(after)
    res = pl.pallas_call(
        body, name=name, grid=(m // tm, n // tn, ksteps),
        in_specs=in_specs, out_specs=[res_spec] * n_out,
        out_shape=[jax.ShapeDtypeStruct(res_shape, dt) for dt in out_dtypes],
        scratch_shapes=[pltpu.VMEM((tm, tn), F32)] if ksteps > 1 else [],
        compiler_params=_params(("parallel", "parallel", "arbitrary")),
    )(*args)
    return res if n_out > 1 else res[0]


RING_SLOTS = 3


def matmul_ring(a, b, mode, name, out_dtype, tm, tn, tk, k_block=0, extra=None, epilogue=None, after=()):
    m, n = a.shape[0], (b.shape[1] if mode == "nn" else b.shape[0])
    nj = n // tn
    steps = (m // tm) * nj
    k0 = k_block * tk
    dims, block = (NN, (tk, tn)) if mode == "nn" else (NT, (tn, tk))
    n_in = 2 + (extra is not None) + len(after)

    def body(*refs):
        a_ref, b_hbm = refs[0], refs[1]
        e_ref = refs[2] if extra is not None else None
        o_ref, slots, sems = refs[n_in:]
        s = pl.program_id(0) * nj + pl.program_id(1)

        def fetch(step):
            at_n = pl.ds(pl.multiple_of((step % nj) * tn, LANES), tn)
            src = b_hbm.at[pl.ds(k0, tk), at_n] if mode == "nn" else b_hbm.at[at_n, pl.ds(k0, tk)]
            slot = step % RING_SLOTS
            return pltpu.make_async_copy(src, slots.at[slot], sems.at[slot])

        @pl.when(s == 0)
        def _():
            fetch(s).start()
            fetch(s + 1).start()

        @pl.when(s + 2 < steps)
        def _():
            fetch(s + 2).start()

        fetch(s).wait()
        for k in range(RING_SLOTS):
            @pl.when(s % RING_SLOTS == k)
            def _():
                acc = _dot(a_ref[...], slots[k], dims)
                out = acc if epilogue is None else epilogue(acc, e_ref[...] if e_ref is not None else None)[0]
                o_ref[...] = out.astype(o_ref.dtype)

    o_spec = pl.BlockSpec((tm, tn), lambda i, j: (i, j))
    in_specs = ([pl.BlockSpec((tm, tk), lambda i, j: (i, k_block)), ANY_SPEC] + ([o_spec] if extra is not None else [])
                + [ANY_SPEC] * len(after))
    return pl.pallas_call(
        body, name=name, grid=(m // tm, nj), in_specs=in_specs, out_specs=o_spec,
        out_shape=jax.ShapeDtypeStruct((m, n), out_dtype),
        scratch_shapes=[pltpu.VMEM((RING_SLOTS,) + block, b.dtype), pltpu.SemaphoreType.DMA((RING_SLOTS,))],
        compiler_params=_params(("arbitrary", "arbitrary")))(
            a, b, *((extra,) if extra is not None else ()), *after)


GATE_COL = 4 * GDN_WIDTH
RELAYOUT_ROWS = 256


def _cat_of_win(j):
    if j < GATE_COL:
        return j
    if j < GATE_COL + 2 * N_HEADS:
        return MAIN_WIDTH + (j - GATE_COL)
    return j - 2 * N_HEADS


def _win_of_cat(c):
    if c < GATE_COL:
        return c
    if c < MAIN_WIDTH:
        return c + 2 * N_HEADS
    if c < MAIN_WIDTH + 2 * N_HEADS:
        return GATE_COL + (c - MAIN_WIDTH)
    return None


def _runs(first, count, mapping):
    runs, i = [], 0
    while i < count:
        start, n = mapping(first + i), 1
        while i + n < count and mapping(first + i + n) == start + n:
            n += 1
        runs.append((start, n))
        i += n
    return runs


def weights_to_cat(g_in, name, total_rows, row0=0, into=None):
    n_dev, rows, shard = g_in.shape
    first = row0 // RELAYOUT_ROWS

    def body(x_ref, *rest):
        o_ref = rest[-1]
        for b in range(CAT_WIDTH // LANES):
            live = sum(_win_of_cat(LANES * b + i) is not None for i in range(LANES))
            parts = []
            for start, n in _runs(LANES * b, live, _win_of_cat):
                while n > 0:
                    d, o = divmod(start, shard)
                    take = min(n, shard - o)
                    parts.append(x_ref[d, :, o:o + take])
                    start, n = start + take, n - take
            if live < LANES:
                parts.append(jnp.zeros((RELAYOUT_ROWS, LANES - live), g_in.dtype))
            o_ref[:, LANES * b:LANES * (b + 1)] = parts[0] if len(parts) == 1 else jnp.concatenate(parts, axis=1)

    return pl.pallas_call(
        body, name=name, grid=(rows // RELAYOUT_ROWS,),
        in_specs=[pl.BlockSpec((n_dev, RELAYOUT_ROWS, shard), lambda i: (0, i, 0))] + ([ANY_SPEC] if into is not None else []),
        out_specs=pl.BlockSpec((RELAYOUT_ROWS, CAT_WIDTH), lambda i: (first + i, 0)),
        out_shape=jax.ShapeDtypeStruct((total_rows, CAT_WIDTH), g_in.dtype),
        input_output_aliases={1: 0} if into is not None else {},
        compiler_params=_params(("parallel",)))(*((g_in,) if into is None else (g_in, into)))


def cat_to_shards(dw_cat, shard):
    rows = dw_cat.shape[0]

    def body(x_ref, o_ref):
        for d in range(N_DEV):
            for t0 in range(0, shard, LANES):
                width = min(LANES, shard - t0)
                parts = [x_ref[:, c:c + n] for c, n in _runs(d * shard + t0, width, _cat_of_win)]
                o_ref[d, :, t0:t0 + width] = parts[0] if len(parts) == 1 else jnp.concatenate(parts, axis=1)

    return pl.pallas_call(
        body, name="cat_to_shards", grid=(rows // RELAYOUT_ROWS,),
        in_specs=[pl.BlockSpec((RELAYOUT_ROWS, CAT_WIDTH), lambda i: (i, 0))],
        out_specs=pl.BlockSpec((N_DEV, RELAYOUT_ROWS, shard), lambda i: (0, i, 0)),
        out_shape=jax.ShapeDtypeStruct((N_DEV, rows, shard), dw_cat.dtype),
        compiler_params=_params(("parallel",)))(dw_cat)


ROW_BLOCK = 512


def rms_fwd(x, w, name):
    t, d = x.shape

    def body(x_ref, w_ref, n_ref, r_ref):
        h = x_ref[...]
        r = lax.rsqrt(jnp.mean(h * h, axis=-1, keepdims=True) + NORM_EPS)
        n_ref[...] = (h * r * w_ref[...]).astype(BF16)
        r_ref[...] = r

    row = pl.BlockSpec((ROW_BLOCK, d), lambda i: (i, 0))
    return pl.pallas_call(
        body, name=name, grid=(t // ROW_BLOCK,),
        in_specs=[row, pl.BlockSpec((1, d), lambda i: (0, 0))],
        out_specs=[row, pl.BlockSpec((ROW_BLOCK, 1), lambda i: (i, 0))],
        out_shape=[jax.ShapeDtypeStruct((t, d), BF16), jax.ShapeDtypeStruct((t, 1), F32)],
        compiler_params=_params(("parallel",)))(x, w)


FUSED_ROWS = 512


def out_proj_rms(y, w_out, x, w_norm, name):
    t, d = x.shape

    def body(y_ref, w_ref, x_ref, g_ref, h_ref, n_ref, r_ref):
        h = x_ref[...] + _dot(y_ref[...], w_ref[...], NN)
        r = lax.rsqrt(jnp.mean(h * h, axis=-1, keepdims=True) + NORM_EPS)
        h_ref[...] = h
        n_ref[...] = (h * r * g_ref[...]).astype(BF16)
        r_ref[...] = r

    row = pl.BlockSpec((FUSED_ROWS, d), lambda i: (i, 0))
    return pl.pallas_call(
        body, name=name, grid=(t // FUSED_ROWS,),
        in_specs=[pl.BlockSpec((FUSED_ROWS, y.shape[1]), lambda i: (i, 0)), pl.BlockSpec(w_out.shape, lambda i: (0, 0)),
                  row, pl.BlockSpec((1, d), lambda i: (0, 0))],
        out_specs=[row, row, pl.BlockSpec((FUSED_ROWS, 1), lambda i: (i, 0))],
        out_shape=[jax.ShapeDtypeStruct((t, d), F32), jax.ShapeDtypeStruct((t, d), BF16),
                   jax.ShapeDtypeStruct((t, 1), F32)],
        compiler_params=_params(("parallel",)))(y, w_out, x, w_norm)


def ff2_loss(act, w_ff2, h1, w, target, name, tk=2048):
    t, d = h1.shape
    ksteps = act.shape[1] // tk

    def body(a_ref, b_ref, h_ref, w_ref, t_ref, loss_ref, dhb_ref, dw_ref, acc_ref):
        i, kk = pl.program_id(0), pl.program_id(1)

        @pl.when((i == 0) & (kk == 0))
        def _():
            loss_ref[...] = jnp.zeros_like(loss_ref)
            dw_ref[...] = jnp.zeros_like(dw_ref)

        @pl.when(kk == 0)
        def _():
            acc_ref[...] = h_ref[...]

        acc_ref[...] += _dot(a_ref[...], b_ref[...], NN)

        @pl.when(kk == ksteps - 1)
        def _():
            h = acc_ref[...]
            wv = w_ref[...]
            r = lax.rsqrt(jnp.mean(h * h, axis=-1, keepdims=True) + NORM_EPS)
            yn = h * r
            e = yn * wv - t_ref[...]
            loss_ref[...] += 0.5 * jnp.sum(jnp.sum(e * e, axis=-1, keepdims=True), axis=0, keepdims=True) / d
            dy = e / d
            dw_ref[...] += jnp.sum(dy * yn, axis=0, keepdims=True)
            dyn = dy * wv
            dhb_ref[...] = (r * (dyn - yn * jnp.mean(dyn * yn, axis=-1, keepdims=True))).astype(BF16)

    row = pl.BlockSpec((FUSED_ROWS, d), lambda i, k: (i, 0))
    wspec = pl.BlockSpec((1, d), lambda i, k: (0, 0))
    return pl.pallas_call(
        body, name=name, grid=(t // FUSED_ROWS, ksteps),
        in_specs=[pl.BlockSpec((FUSED_ROWS, tk), lambda i, k: (i, k)), pl.BlockSpec((tk, d), lambda i, k: (k, 0)),
                  row, wspec, row],
        out_specs=[pl.BlockSpec((1, 1), lambda i, k: (0, 0)), row, wspec],
        out_shape=[jax.ShapeDtypeStruct((1, 1), F32), jax.ShapeDtypeStruct((t, d), BF16),
                   jax.ShapeDtypeStruct((1, d), F32)],
        scratch_shapes=[pltpu.VMEM((FUSED_ROWS, d), F32)],
        compiler_params=_params(("arbitrary", "arbitrary")))(act, w_ff2, h1, w, target)


def rms_bwd(h, r, w, dn, dres, out_dtype, name):
    t, d = h.shape

    def body(h_ref, r_ref, w_ref, dn_ref, dres_ref, dh_ref, dw_ref):
        @pl.when(pl.program_id(0) == 0)
        def _():
            dw_ref[...] = jnp.zeros_like(dw_ref)

        rv = r_ref[...]
        yn = h_ref[...] * rv
        dnv = dn_ref[...].astype(F32)
        dw_ref[...] += jnp.sum(dnv * yn, axis=0, keepdims=True)
        dyn = dnv * w_ref[...]
        dh = dres_ref[...].astype(F32) + rv * (dyn - yn * jnp.mean(dyn * yn, axis=-1, keepdims=True))
        dh_ref[...] = dh.astype(out_dtype)

    row = pl.BlockSpec((ROW_BLOCK, d), lambda i: (i, 0))
    wspec = pl.BlockSpec((1, d), lambda i: (0, 0))
    rspec = pl.BlockSpec((ROW_BLOCK, 1), lambda i: (i, 0))
    return pl.pallas_call(
        body, name=name, grid=(t // ROW_BLOCK,),
        in_specs=[row, rspec, wspec, row, row], out_specs=[row, wspec],
        out_shape=[jax.ShapeDtypeStruct((t, d), out_dtype), jax.ShapeDtypeStruct((1, d), F32)],
        compiler_params=_params(("arbitrary",)))(h, r, w, dn, dres)


CONV_ROWS = 512
TILE_ROWS = 8


def _iota2(shape, axis):
    return lax.broadcasted_iota(jnp.int32, shape, axis)


def _silu(x):
    return x * jax.nn.sigmoid(x)


def _conv_rows(x_ref, w, first, rows):
    acc = None
    for j in range(4):
        term = x_ref[first - 3 + j:first - 3 + j + rows, :] * w[j:j + 1, :]
        acc = term if acc is None else acc + term
    return acc


def _head_shifts(head):
    rows = _iota2((TILE_ROWS, 1), 0)
    return [jnp.where(rows >= 3 - j, head if j == 3 else pltpu.roll(head, 3 - j, 0), 0.0) for j in range(4)]


def _conv_chunks(t):
    pieces = [(TILE_ROWS, min(CONV_ROWS, t) - TILE_ROWS)]
    pieces += [(r, CONV_ROWS) for r in range(CONV_ROWS, t, CONV_ROWS)]
    return pieces


def conv_fwd(proj, conv_w, name):
    t = proj.shape[0]

    def body(x_ref, w_ref, o_ref):
        w = w_ref[...]
        shifted = _head_shifts(x_ref[0:TILE_ROWS, :])
        o_ref[0:TILE_ROWS, :] = _silu(sum(shifted[j] * w[j:j + 1, :] for j in range(4)))
        for first, rows in _conv_chunks(t):
            o_ref[first:first + rows, :] = _silu(_conv_rows(x_ref, w, first, rows))

    col = pl.BlockSpec((t, LANES), lambda c: (0, c))
    return pl.pallas_call(
        body, name=name, grid=(QKV_WIDTH // LANES,),
        in_specs=[col, pl.BlockSpec((4, LANES), lambda c: (0, c))], out_specs=col,
        out_shape=jax.ShapeDtypeStruct((t, QKV_WIDTH), F32),
        compiler_params=_params(("parallel",)))(proj, conv_w)


def conv_bwd(proj, dout, conv_w, dproj, name):
    t = proj.shape[0]

    def dsilu(pre):
        sg = jax.nn.sigmoid(pre)
        return sg * (1.0 + pre * (1.0 - sg))

    def body(x_ref, d_ref, w_ref, dproj_in, dx_ref, dw_ref, stage):
        del dproj_in
        w = w_ref[...]
        shifted = _head_shifts(x_ref[0:TILE_ROWS, :])
        head_dpre = d_ref[0:TILE_ROWS, :] * dsilu(sum(shifted[j] * w[j:j + 1, :] for j in range(4)))
        stage[0:TILE_ROWS, :] = head_dpre
        for first, rows in _conv_chunks(t):
            stage[first:first + rows, :] = d_ref[first:first + rows, :] * dsilu(_conv_rows(x_ref, w, first, rows))
        stage[t:t + TILE_ROWS, :] = jnp.zeros((TILE_ROWS, LANES), F32)
        for first, rows in [(0, TILE_ROWS)] + _conv_chunks(t):
            dx = None
            for j in range(4):
                term = stage[first + 3 - j:first + 3 - j + rows, :] * w[j:j + 1, :]
                dx = term if dx is None else dx + term
            dx_ref[first:first + rows, :] = dx.astype(BF16)
        dw = [jnp.sum(head_dpre * shifted[j], axis=0, keepdims=True) for j in range(4)]
        for first, rows in _conv_chunks(t):
            dpre = stage[first:first + rows, :]
            for j in range(4):
                dw[j] = dw[j] + jnp.sum(dpre * x_ref[first - 3 + j:first - 3 + j + rows, :], axis=0, keepdims=True)
        dw_ref[...] = jnp.concatenate(dw, axis=0)

    col = pl.BlockSpec((t, LANES), lambda c: (0, c))
    taps = pl.BlockSpec((4, LANES), lambda c: (0, c))
    return pl.pallas_call(
        body, name=name, grid=(QKV_WIDTH // LANES,),
        in_specs=[col, col, taps, ANY_SPEC], out_specs=[col, taps],
        out_shape=[jax.ShapeDtypeStruct(dproj.shape, BF16), jax.ShapeDtypeStruct((4, QKV_WIDTH), F32)],
        scratch_shapes=[pltpu.VMEM((t + TILE_ROWS, LANES), F32)],
        input_output_aliases={3: 0},
        compiler_params=_params(("parallel",)))(proj, dout, conv_w, dproj)


def _softplus(x):
    return jnp.maximum(x, 0.0) + jnp.log(1.0 + jnp.exp(-jnp.abs(x)))


def _head_norm_gate(o, norm_w, gate):
    return o * lax.rsqrt(jnp.mean(o * o, axis=-1, keepdims=True) + NORM_EPS) * norm_w * _silu(gate)


GDN_PREC = ("bf", "bf")
HGRN_PREC = "bf"


def _each(fn, *cols):
    return [fn(*a) for a in zip(*cols)]


@functools.partial(jax.custom_vjp, nondiff_argnums=(2,))
def _known_inverse(low, inv, prec):
    del low, prec
    return inv


def _known_inverse_fwd(low, inv, prec):
    del low
    return inv, inv


def _known_inverse_bwd(prec, inv, ct):
    return -_mm_raw(_mm_raw(inv, ct, TN, prec), inv, NT, prec), jnp.zeros_like(inv)


_known_inverse.defvjp(_known_inverse_fwd, _known_inverse_bwd)


def gdn_stages(hs, qc, kc, vc, zc, ab, a_log_l, dt_l, norm_w, s, prec=GDN_PREC, inv_known=None):
    p_inv, p_mm = prec
    c = CHUNK
    ri, ci = _iota2((c, c), 0), _iota2((c, c), 1)
    incl, strict, eye = ri >= ci, ri > ci, ri == ci
    lane = _iota2((c, LANES), 1)
    last_row = _iota2((c, 1), 0) == c - 1
    rowsum = lambda x: jnp.sum(x, axis=1, keepdims=True)

    def row(col):
        return jnp.sum(jnp.where(eye, col, 0.0), axis=0, keepdims=True)

    q = _each(lambda x: x * lax.rsqrt(rowsum(x * x) + L2_EPS) * (HEAD_DIM ** -0.5), qc)
    k = _each(lambda x: x * lax.rsqrt(rowsum(x * x) + L2_EPS), kc)
    yield
    a_col = [rowsum(jnp.where(lane == h, ab, 0.0)) for h in hs]
    b_col = [rowsum(jnp.where(lane == h + N_HEADS, ab, 0.0)) for h in hs]
    beta = _each(jax.nn.sigmoid, b_col)
    g = _each(lambda a, al, dl: rowsum(jnp.where(lane == 0, -jnp.exp(al) * _softplus(a + dl), 0.0)), a_col, a_log_l, dt_l)
    gcum = _each(lambda x: rowsum(jnp.where(incl, row(x), 0.0)), g)
    g_last = _each(lambda x: jnp.sum(jnp.where(last_row, x, 0.0), axis=0, keepdims=True), gcum)
    decay = _each(lambda x: jnp.exp(jnp.where(incl, x - row(x), -jnp.inf)), gcum)
    yield
    kk = _each(lambda x: mm(x, x, NT, p_mm), k)
    low = _each(lambda b, x, d: jnp.where(strict, b * x * d, 0.0), beta, kk, decay)
    yield
    if inv_known is None:
        power = _each(lambda x: -x, low)
        inv = _each(lambda x: jnp.where(eye, 1.0, 0.0) + x, power)
        for _ in range(5):
            power = _each(lambda x: mm(x, x, NN, p_inv), power)
            yield
            inv = _each(lambda x, p: x + mm(x, p, NN, p_inv), inv, power)
            yield
    else:
        inv = _each(lambda x, known: _known_inverse(x, known, p_inv), low, inv_known)
    exp_g = _each(jnp.exp, gcum)
    yield
    u_v = _each(lambda i, b, x: mm(i, b * x, NN, p_mm), inv, beta, vc)
    w = _each(lambda i, b, e, x: mm(i, b * e * x, NN, p_mm), inv, beta, exp_g, k)
    yield
    attn = _each(lambda x, y, d: mm(x, y, NT, p_mm) * d, q, k, decay)
    yield
    u = _each(lambda x, y, z: x - mm(y, z, NN, p_mm), u_v, w, s)
    yield
    o = _each(lambda x, e, z: mm(x * e, z, NN, p_mm), q, exp_g, s)
    o = _each(lambda x, a, y: x + mm(a, y, NN, p_mm), o, attn, u)
    yield
    k_end = _each(lambda x, gl, gc: x * jnp.exp(gl - gc), k, g_last, gcum)
    s_new = _each(lambda z, gl, x, y: z * jnp.exp(gl) + mm(x, y, TN, p_mm), s, g_last, k_end, u)
    return (_each(lambda x, z: _head_norm_gate(x, norm_w, z), o, zc), s_new), inv


def gdn_chunk(h, qc, kc, vc, zc, ab, a_log_l, dt_l, norm_w, s, prec=GDN_PREC, reuse_inverse=False):
    args = ([h], [qc], [kc], [vc], [zc], ab, [a_log_l], [dt_l], norm_w, [s], prec)
    if reuse_inverse:
        inv = lax.stop_gradient(gdn_chunks(*args)[1])
        (y, s_new), _ = gdn_chunks(*args, inv_known=inv)
    else:
        (y, s_new), _ = gdn_chunks(*args)
    return y[0], s_new[0]


DIAG_ROWS = SUB_CHUNK // 2
SHIFT_PAD = 8
SHIFT_ROWS = SHIFT_PAD + CHUNK + SHIFT_PAD
SHIFT_WAYS = 4


class RolledRows:
    def down(self, x, which):
        del which
        return [x] + [pltpu.roll(x, off, 0) for off in range(1, DIAG_ROWS)]

    def up_sum(self, parts, which):
        del which
        acc = parts[0]
        for off in range(1, DIAG_ROWS):
            acc = acc + pltpu.roll(parts[off], CHUNK - off, 0)
        return acc


class SlotRows:
    def __init__(self, slots):
        self.slots = slots

    def down(self, x, which):
        self.slots[which, 0, SHIFT_PAD:SHIFT_PAD + CHUNK, :] = x
        return [x] + [self.slots[which, 0, SHIFT_PAD - off:SHIFT_PAD + CHUNK - off, :] for off in range(1, DIAG_ROWS)]

    def up_sum(self, parts, which):
        acc = parts[0]
        for off in range(1, DIAG_ROWS):
            way = 1 + off % (SHIFT_WAYS - 1)
            self.slots[which, way, SHIFT_PAD:SHIFT_PAD + CHUNK, :] = parts[off]
            acc = acc + self.slots[which, way, SHIFT_PAD + off:SHIFT_PAD + CHUNK + off, :]
        return acc


def _sub_block_rows():
    return jnp.bitwise_and(_iota2((CHUNK, 1), 0), DIAG_ROWS - 1)


def _diag_forward(rows, q, key, bc, v):
    rmod = _sub_block_rows()
    k_d, b_d, v_d = rows.down(key, 0), rows.down(bc, 1), rows.down(v, 2)
    o = None
    for off in range(DIAG_ROWS):
        e = jnp.exp(jnp.where(rmod >= off, bc - b_d[off], -jnp.inf))
        term = jnp.sum(q * k_d[off] * e, axis=-1, keepdims=True) * v_d[off]
        o = term if o is None else o + term
    return o


def _diag_backward(rows, q, key, bc, v, do):
    rmod = _sub_block_rows()
    k_d, b_d, v_d = rows.down(key, 0), rows.down(bc, 1), rows.down(v, 2)
    dq = db = None
    dk_parts, db_parts, dv_parts = [], [], []
    for off in range(DIAG_ROWS):
        e = jnp.exp(jnp.where(rmod >= off, bc - b_d[off], -jnp.inf))
        qe = q * e
        a = jnp.sum(qe * k_d[off], axis=-1, keepdims=True)
        da = jnp.sum(do * v_d[off], axis=-1, keepdims=True)
        dv_parts.append(a * do)
        dq_term = (da * e) * k_d[off]
        dk_term = da * qe
        s = dk_term * k_d[off]
        dq = dq_term if dq is None else dq + dq_term
        db = s if db is None else db + s
        dk_parts.append(dk_term)
        db_parts.append(s)
    return dq, rows.up_sum(dk_parts, 0), db - rows.up_sum(db_parts, 1), rows.up_sum(dv_parts, 2)


def diag_part(rows, differentiable=True):
    forward = functools.partial(_diag_forward, rows)
    if not differentiable:
        return forward
    part = jax.custom_vjp(forward)
    part.defvjp(lambda q, key, bc, v: (forward(q, key, bc, v), (q, key, bc, v)),
                lambda res, do: _diag_backward(rows, *res, do))
    return part


def hgrn_stages(qb, fb, ib, gb, l0, l1, norm_w, st, prec=HGRN_PREC, diags=None, o_known=None):
    c = CHUNK
    ri, ci = _iota2((4 * c, c), 0), _iota2((4 * c, c), 1)
    rcol = _iota2((c, 1), 0)
    blk0 = jnp.bitwise_and(ri, c - SUB_CHUNK)
    limit = jnp.where(ri < c, ri + 1, jnp.where(ri < 2 * c, blk0, jnp.where(ri < 3 * c, blk0 + SUB_CHUNK,
                                                                          blk0 + DIAG_ROWS)))
    sel = jnp.where(ci < limit, 1.0, 0.0)
    ri, ci = _iota2((c, c), 0), _iota2((c, c), 1)
    lb = _each(lambda a, b: jax.nn.sigmoid(a - b), l0, l1)
    forget = _each(lambda b, f: b + (1.0 - b) * jax.nn.sigmoid(f), lb, fb)
    key = _each(lambda b, f: (1.0 - b) * jax.nn.sigmoid(-f), lb, fb)
    q = _each(_silu, qb)
    v = ib
    logf = _each(jnp.log, forget)
    sums = _each(lambda x: sel_sums(sel, x), logf)
    bc, b_start, b_end, b_half = ([x[i] for x in sums] for i in range(4))
    b_last = _each(lambda x: jnp.sum(x, axis=0, keepdims=True), logf)
    o = _each(lambda x, b, z: mm(x * jnp.exp(b), z, NT, prec), q, bc, st)
    if diags is None:
        diags = [diag_part(RolledRows())] * len(qb)
    yield
    o = list(o)
    for h in range(len(o)):
        o[h] = o[h] + diags[h](q[h], key[h], bc[h], v[h])
        yield
    second = jnp.bitwise_and(rcol, SUB_CHUNK - 1) >= DIAG_ROWS
    same_sub = jnp.bitwise_and(ri, c - SUB_CHUNK) == jnp.bitwise_and(ci, c - SUB_CHUNK)
    q_half = _each(lambda x, b, bh: x * jnp.exp(jnp.where(second, b - bh, -jnp.inf)), q, bc, b_half)
    k_half = _each(lambda x, b, bh: x * jnp.exp(jnp.where(second, -jnp.inf, bh - b)), key, bc, b_half)
    a_half = _each(lambda x, z: jnp.where(same_sub, mm(x, z, NT, prec), 0.0), q_half, k_half)
    o = _each(lambda acc, a, val: acc + mm(a, val, NN, prec), o, a_half, v)
    yield
    q_rel = _each(lambda x, b, bs: x * jnp.exp(b - bs), q, bc, b_start)
    k_rel = _each(lambda x, b, be: x * jnp.exp(be - b), key, bc, b_end)
    for y in range(c // SUB_CHUNK - 1):
        def scaled(x, b, bs):
            end_y = jnp.sum(jnp.where(rcol == SUB_CHUNK * y + SUB_CHUNK - 1, b, 0.0), axis=0, keepdims=True)
            return x * jnp.exp(jnp.where(rcol >= SUB_CHUNK * (y + 1), bs - end_y, -jnp.inf))
        dq = _each(scaled, q_rel, bc, b_start)
        in_y = (ci >= SUB_CHUNK * y) & (ci < SUB_CHUNK * (y + 1))
        a_y = _each(lambda x, z: jnp.where(in_y, mm(x, z, NT, prec), 0.0), dq, k_rel)
        o = _each(lambda acc, a, val: acc + mm(a, val, NN, prec), o, a_y, v)
        yield
    k_state = _each(lambda x, bl, b: x * jnp.exp(bl - b), key, b_last, bc)
    st_new = _each(lambda z, bl, val, x: z * jnp.exp(bl) + mm(val, x, TN, prec), st, b_last, v, k_state)
    if o_known is not None:
        o = _each(_known_value, o, o_known)
    return (_each(lambda x, z: _head_norm_gate(x, norm_w, z), o, gb), st_new), o


def _drain(gen):
    try:
        while True:
            next(gen)
    except StopIteration as done:
        return done.value


def _alternate(gen_a, gen_b):
    out, live = [None, None], [gen_a, gen_b]
    while any(g is not None for g in live):
        for i, g in enumerate(live):
            if g is None:
                continue
            try:
                next(g)
            except StopIteration as done:
                out[i], live[i] = done.value, None
    return out


def gdn_chunks(*args, **kwargs):
    return _drain(gdn_stages(*args, **kwargs))


def hgrn_chunks(*args, **kwargs):
    return _drain(hgrn_stages(*args, **kwargs))


def hgrn_chunk(qb, fb, ib, gb, l0, l1, norm_w, st, prec=HGRN_PREC, reuse_output=False):
    args = ([qb], [fb], [ib], [gb], [l0], [l1], norm_w, [st], prec)
    if reuse_output:
        known = lax.stop_gradient(hgrn_chunks(*args)[1])
        (y, st_new), _ = hgrn_chunks(*args, o_known=known)
    else:
        (y, st_new), _ = hgrn_chunks(*args)
    return y[0], st_new[0]


HEAD_VEC = (N_HEADS, 1, LANES)


class _ChunkSpecs:
    def __init__(self, nc, rev):
        self.nc, self.rev = nc, rev

    def _c(self, c):
        return self.nc - 1 - c if self.rev else c

    def row(self, width, block=0):
        return pl.BlockSpec((CHUNK, width), lambda c: (self._c(c), block))

    def per_head(self, rows):
        return pl.BlockSpec((None, N_HEADS, rows, rows), lambda c: (self._c(c), 0, 0, 0))

    @staticmethod
    def whole(shape):
        return pl.BlockSpec(shape, lambda c: (0,) * len(shape))


def _lanes(j):
    return slice(j * LANES, (j + 1) * LANES)


def mixer_fwd(qkv_c, proj, a_log_l, dt_l, gdn_norm_w, l0, l1, hgrn_norm_w, name):
    t = qkv_c.shape[0]
    hb = N_HEADS
    sp = _ChunkSpecs(t // CHUNK, rev=False)
    hs = list(range(hb))

    def body(q_ref, k_ref, v_ref, z_ref, ab_ref, al_ref, dt_ref, gnw_ref, qb_ref, fb_ref, ib_ref, gb_ref, l0_ref, l1_ref,
             hnw_ref, y_ref, hist_a_ref, inv_ref, hist_b_ref, o_ref, sa_ref, sb_ref, shift_ref):
        @pl.when(pl.program_id(0) == 0)
        def _():
            sa_ref[...] = jnp.zeros_like(sa_ref)
            sb_ref[...] = jnp.zeros_like(sb_ref)
            shift_ref[...] = jnp.zeros_like(shift_ref)

        heads = lambda ref: [ref[:, _lanes(j)] for j in hs]
        s_a, s_b = [sa_ref[h] for h in hs], [sb_ref[h] for h in hs]
        for h in hs:
            hist_a_ref[h] = s_a[h]
            hist_b_ref[h] = s_b[h]
        diags = [diag_part(SlotRows(shift_ref.at[h]), differentiable=False) for h in hs]
        ((y_a, s_a_new), inv), ((y_b, s_b_new), o_pre) = _alternate(
            gdn_stages(hs, heads(q_ref), heads(k_ref), heads(v_ref), heads(z_ref), ab_ref[...],
                       [al_ref[h] for h in hs], [dt_ref[h] for h in hs], gnw_ref[...], s_a),
            hgrn_stages(heads(qb_ref), heads(fb_ref), heads(ib_ref), heads(gb_ref),
                        [l0_ref[h] for h in hs], [l1_ref[h] for h in hs], hnw_ref[...], s_b, diags=diags))
        for h in hs:
            y_ref[:, _lanes(h)] = y_a[h].astype(BF16)
            y_ref[:, _lanes(hb + h)] = y_b[h].astype(BF16)
            o_ref[:, _lanes(h)] = o_pre[h]
            sa_ref[h] = s_a_new[h]
            sb_ref[h] = s_b_new[h]
            inv_ref[h] = inv[h]

    vec, gain, slab = sp.whole(HEAD_VEC), sp.whole((1, LANES)), functools.partial(sp.row, GDN_WIDTH)
    states = jax.ShapeDtypeStruct((sp.nc, N_HEADS, HEAD_DIM, HEAD_DIM), F32)
    return pl.pallas_call(
        body, name=name, grid=(sp.nc,),
        in_specs=[slab(0), slab(1), slab(2), slab(3), sp.row(LANES, AB_BLOCK), vec, vec, gain,
                  slab(4), slab(5), slab(6), slab(7), vec, vec, gain],
        out_specs=[sp.row(2 * GDN_WIDTH), sp.per_head(HEAD_DIM), sp.per_head(CHUNK), sp.per_head(HEAD_DIM), slab(0)],
        out_shape=[jax.ShapeDtypeStruct((t, 2 * GDN_WIDTH), BF16), states,
                   jax.ShapeDtypeStruct((sp.nc, N_HEADS, CHUNK, CHUNK), F32), states,
                   jax.ShapeDtypeStruct((t, GDN_WIDTH), F32)],
        scratch_shapes=[pltpu.VMEM((N_HEADS, HEAD_DIM, HEAD_DIM), F32), pltpu.VMEM((N_HEADS, HEAD_DIM, HEAD_DIM), F32),
                        pltpu.VMEM((hb, 3, SHIFT_WAYS, SHIFT_ROWS, LANES), F32)],
        compiler_params=_params(("arbitrary",)),
    )(qkv_c, qkv_c, qkv_c, proj, proj, a_log_l, dt_l, gdn_norm_w, proj, proj, proj, proj, l0, l1, hgrn_norm_w)


def mixer_bwd(qkv_c, proj, a_log_l, dt_l, gdn_norm_w, l0, l1, hgrn_norm_w, hist_a, inv_hist, hist_b, o_pre, dy, name):
    t = qkv_c.shape[0]
    hb = N_HEADS
    sp = _ChunkSpecs(t // CHUNK, rev=True)
    hs = list(range(hb))

    def body(q_ref, k_ref, v_ref, z_ref, ab_ref, al_ref, dt_ref, gnw_ref, qb_ref, fb_ref, ib_ref, gb_ref, l0_ref, l1_ref,
             hnw_ref, hist_a_ref, inv_ref, hist_b_ref, o_ref, dy_ref,
             dqkv_ref, dproj_ref, dal_ref, ddt_ref, dgnw_ref, dl0_ref, dl1_ref, dhnw_ref, dsa_ref, dsb_ref, shift_ref):
        @pl.when(pl.program_id(0) == 0)
        def _():
            for ref in (dal_ref, ddt_ref, dgnw_ref, dl0_ref, dl1_ref, dhnw_ref, dsa_ref, dsb_ref, shift_ref):
                ref[...] = jnp.zeros_like(ref)

        heads = lambda ref, first=0: [ref[:, _lanes(first + j)] for j in hs]
        diags = [diag_part(SlotRows(shift_ref.at[h])) for h in hs]
        inv_known, o_known = [inv_ref[h] for h in hs], heads(o_ref)

        def both(ga, gb):
            (ra, inv), (rb, o_pre) = _alternate(gdn_stages(hs, *ga, inv_known=inv_known),
                                                hgrn_stages(*gb, diags=diags, o_known=o_known))
            return (ra, rb), (inv, o_pre)

        ga = (heads(q_ref), heads(k_ref), heads(v_ref), heads(z_ref), ab_ref[...], [al_ref[h] for h in hs],
              [dt_ref[h] for h in hs], gnw_ref[...], [hist_a_ref[h] for h in hs])
        gb = (heads(qb_ref), heads(fb_ref), heads(ib_ref), heads(gb_ref), [l0_ref[h] for h in hs],
              [l1_ref[h] for h in hs], hnw_ref[...], [hist_b_ref[h] for h in hs])
        _, vjp, _ = jax.vjp(both, ga, gb, has_aux=True)
        dy_a = [x.astype(F32) for x in heads(dy_ref)]
        dy_b = [x.astype(F32) for x in heads(dy_ref, hb)]
        (dq, dk, dv, dz, dab, dal, ddt, dgnw, ds_a), (dqb, dfb, dib, dgb, dl0, dl1, dhnw, ds_b) = vjp(
            ((dy_a, [dsa_ref[h] for h in hs]), (dy_b, [dsb_ref[h] for h in hs])))
        for h in hs:
            dqkv_ref[:, _lanes(h)] = dq[h]
            dqkv_ref[:, _lanes(hb + h)] = dk[h]
            dqkv_ref[:, _lanes(2 * hb + h)] = dv[h]
            for slab, val in enumerate((dz, dqb, dfb, dib, dgb)):
                dproj_ref[:, _lanes((3 + slab) * hb + h)] = val[h].astype(BF16)
            dal_ref[h] += dal[h]
            ddt_ref[h] += ddt[h]
            dl0_ref[h] += dl0[h]
            dl1_ref[h] += dl1[h]
            dsa_ref[h] = ds_a[h]
            dsb_ref[h] = ds_b[h]
        dproj_ref[:, MAIN_WIDTH:] = dab.astype(BF16)
        dgnw_ref[...] += dgnw
        dhnw_ref[...] += dhnw

    vec, gain, slab = sp.whole(HEAD_VEC), sp.whole((1, LANES)), functools.partial(sp.row, GDN_WIDTH)
    vec_shape, gain_shape = jax.ShapeDtypeStruct(HEAD_VEC, F32), jax.ShapeDtypeStruct((1, LANES), F32)
    return pl.pallas_call(
        body, name=name, grid=(sp.nc,),
        in_specs=[slab(0), slab(1), slab(2), slab(3), sp.row(LANES, AB_BLOCK), vec, vec, gain,
                  slab(4), slab(5), slab(6), slab(7), vec, vec, gain,
                  sp.per_head(HEAD_DIM), sp.per_head(CHUNK), sp.per_head(HEAD_DIM), slab(0), sp.row(2 * GDN_WIDTH)],
        out_specs=[sp.row(QKV_WIDTH), sp.row(CAT_WIDTH), vec, vec, gain, vec, vec, gain],
        out_shape=[jax.ShapeDtypeStruct((t, QKV_WIDTH), F32), jax.ShapeDtypeStruct((t, CAT_WIDTH), BF16),
                   vec_shape, vec_shape, gain_shape, vec_shape, vec_shape, gain_shape],
        scratch_shapes=[pltpu.VMEM((N_HEADS, HEAD_DIM, HEAD_DIM), F32), pltpu.VMEM((N_HEADS, HEAD_DIM, HEAD_DIM), F32),
                        pltpu.VMEM((hb, 3, SHIFT_WAYS, SHIFT_ROWS, LANES), F32)],
        compiler_params=_params(("arbitrary",)),
    )(qkv_c, qkv_c, qkv_c, proj, proj, a_log_l, dt_l, gdn_norm_w, proj, proj, proj, proj, l0, l1, hgrn_norm_w,
      hist_a, inv_hist, hist_b, o_pre, dy)


def _adamw(w, g, m, v):
    m = ADAM_B1 * m + (1.0 - ADAM_B1) * g
    v = ADAM_B2 * v + (1.0 - ADAM_B2) * jnp.square(g)
    m_hat = m / (1.0 - ADAM_B1 ** ADAM_STEP)
    v_hat = v / (1.0 - ADAM_B2 ** ADAM_STEP)
    delta = -ADAM_LR * (m_hat / (jnp.sqrt(v_hat) + ADAM_EPS) + ADAM_WD * w)
    return delta, m, v


def adamw_reduce(parts, mine, slot, w, m, v, name, rb=128):
    r, c = w.shape
    rb = min(rb, r)
    n_parts = parts.shape[0]

    def body(slot_ref, p_ref, own_ref, w_ref, m_ref, v_ref, g_ref, d_ref, mo_ref, vo_ref):
        part = lambda d: jnp.where(slot_ref[0] == d, own_ref[...], p_ref[d]).astype(F32)
        g = part(0)
        for d in range(1, n_parts):
            g = g + part(d)
        delta, mn, vn = _adamw(w_ref[...], g, m_ref[...], v_ref[...])
        g_ref[...] = g
        d_ref[...] = delta
        mo_ref[...] = mn
        vo_ref[...] = vn

    blk = pl.BlockSpec((rb, c), lambda i, s: (i, 0))
    return pl.pallas_call(
        body, name=name,
        grid_spec=pltpu.PrefetchScalarGridSpec(
            num_scalar_prefetch=1, grid=(r // rb,),
            in_specs=[pl.BlockSpec((n_parts, rb, c), lambda i, s: (0, i, 0)),
                      pl.BlockSpec((None, rb, c), lambda i, s: (s[0], i, 0)), blk, blk, blk],
            out_specs=[blk] * 4),
        out_shape=[jax.ShapeDtypeStruct((r, c), F32)] * 4,
        compiler_params=_params(("parallel",)))(slot.astype(jnp.int32).reshape(1), parts, mine, w, m, v)


def adamw_small(ws, gs, ms, vs, name):
    n = len(ws)

    def body(*refs):
        for i in range(n):
            w_ref, g_ref, m_ref, v_ref = (refs[j * n + i] for j in range(4))
            outs = _adamw(w_ref[...], g_ref[...], m_ref[...], v_ref[...])
            for j, o in enumerate(outs):
                refs[(4 + j) * n + i][...] = o

    vmem = pl.BlockSpec(memory_space=pltpu.VMEM)
    res = pl.pallas_call(body, name=name, in_specs=[vmem] * (4 * n), out_specs=[vmem] * (3 * n),
                         out_shape=[jax.ShapeDtypeStruct(w.shape, F32) for w in ws] * 3)(*ws, *gs, *ms, *vs)
    return res[:n], res[n:2 * n], res[2 * n:]


def _pack(arrays):
    flat = jnp.concatenate([a.reshape(-1).astype(F32) for a in arrays])
    rows = -(-flat.shape[0] // (8 * LANES)) * 8
    return jnp.pad(flat, (0, rows * LANES - flat.shape[0])).reshape(rows, LANES)


def _unpack(packed, shapes):
    flat, out, off = packed.reshape(-1), [], 0
    for s in shapes:
        n = 1
        for d in s:
            n *= d
        out.append(flat[off:off + n].reshape(s))
        off += n
    return out


def _relu2_epilogue(acc, _):
    r = jnp.maximum(acc, 0.0)
    return acc, r * r


def _relu2_bwd_epilogue(acc, a1):
    return (acc * (2.0 * jnp.maximum(a1, 0.0)),)


def kernel(x, w_in, conv_w, gdn_a_log, gdn_dt_bias, gdn_norm_w, hgrn_lb_logits, hgrn_norm_w, w_out, norm_mix_w, norm_ffn_w, w_ff1, w_ff2, norm_final_w, loss_target, m_w_in, m_conv_w, m_gdn_a_log, m_gdn_dt_bias, m_gdn_norm_w, m_hgrn_lb_logits, m_hgrn_norm_w, m_w_out, m_norm_mix_w, m_norm_ffn_w, m_w_ff1, m_w_ff2, m_norm_final_w, v_w_in, v_conv_w, v_gdn_a_log, v_gdn_dt_bias, v_gdn_norm_w, v_hgrn_lb_logits, v_hgrn_norm_w, v_w_out, v_norm_mix_w, v_norm_ffn_w, v_w_ff1, v_w_ff2, v_norm_final_w):
    me = _my_flat()
    xs = x[0]
    target = loss_target[0]
    shard_in = w_in.shape[2]
    shard_conv = conv_w.shape[2]

    tok = lambda t: t[0:1, 0:1]

    half = D_MODEL // 2
    w_in_b = w_in[0].astype(BF16)
    h_ga, t_ga = exchange_start([w_in_b[:half], conv_w[0]], True, "gather_w_in_high_start", peers=CHIP_PEERS)
    h_g0, t_g0 = exchange_start([w_in_b[half:]], True, "gather_w_in_low_start", after=[t_ga], peers=CHIP_PEERS)
    behind = lambda a: lax.optimization_barrier((a, t_g0))[0]
    h_g1, t_g1 = exchange_start([behind(w_out[0]).astype(BF16), behind(w_ff1[0]).astype(BF16)], True,
                                "gather_mid_start", after=[t_g0], peers=CHIP_PEERS)
    h_g2, t_g2 = exchange_start([behind(w_ff2[0]).astype(BF16)], True, "gather_ff2_start", after=[t_g1],
                                peers=CHIP_PEERS)
    m_in, v_in, _ = lax.optimization_barrier((m_w_in, v_w_in, t_g2))
    m_in, v_in = m_in[0], v_in[0]

    lane_b = lambda p: jnp.broadcast_to(p.reshape(N_HEADS, 1, 1), HEAD_VEC)
    a_log_l, dt_l = lane_b(gdn_a_log[0]), lane_b(gdn_dt_bias[0])
    l0 = hgrn_lb_logits[0].reshape(HEAD_VEC)
    l1 = hgrn_lb_logits[1].reshape(HEAD_VEC)

    n1, r1 = rms_fwd(xs, norm_mix_w + tok(t_g1) + tok(t_g2), "rms_mix")
    (s_high, s_conv), (l_high, l_conv) = exchange_wait(h_ga, "gather_w_in_high_wait", after=[n1, m_in, v_in],
                                                       copies=len(CHIP_PEERS))
    h_fa, _ = forward_start([l_high, l_conv], "gather_w_in_high_forward")
    _, (l_high, l_conv) = exchange_wait(h_fa, "forward_w_in_high_wait", copies=len(OTHER_CHIPS))
    w_cat = weights_to_cat(_own_slot(l_high, s_high), "weights_to_cat", D_MODEL)
    conv_full = jnp.transpose(_own_slot(l_conv, s_conv), (1, 0, 2)).reshape(4, QKV_WIDTH)
    proj = matmul(n1, w_cat, "nn", "in_proj_high", (BF16,), tn=CAT_WIDTH // 5, tk=half, k_blocks=(0, 1))
    (s_low,), (l_low,) = exchange_wait(h_g0, "gather_w_in_low_wait", after=[proj], copies=len(CHIP_PEERS))
    h_f0, _ = forward_start([l_low], "gather_w_in_low_forward")
    _, (l_low,) = exchange_wait(_one(h_f0, 0), "forward_w_in_low_wait", copies=len(OTHER_CHIPS))
    w_cat = weights_to_cat(_own_slot(l_low, s_low), "weights_to_cat_low", D_MODEL, row0=half, into=w_cat)
    proj = matmul_ring(n1, w_cat, "nn", "in_proj_low", F32, tm=1024, tn=CAT_WIDTH // 5, tk=half, k_block=1,
                       extra=proj, epilogue=lambda acc, high: (acc + high,))
    qkv_c = conv_fwd(proj, conv_full, "conv_fwd")
    y, hist_a, inv_a, hist_b, o_b = mixer_fwd(qkv_c, proj, a_log_l, dt_l, gdn_norm_w, l0, l1, hgrn_norm_w, "mixer_fwd")
    (s_out, s_ff1), (l_out, l_ff1) = exchange_wait(h_g1, "gather_mid_wait", after=[y], copies=len(CHIP_PEERS))
    (s_ff2,), (l_ff2,) = exchange_wait(h_g2, "gather_ff2_wait", after=[y], copies=len(CHIP_PEERS))
    h_fw, _ = forward_start([l_out, l_ff1, l_ff2], "gather_forward_start")
    _, (l_out,) = exchange_wait(_one(h_fw, 0), "forward_out_wait", copies=len(OTHER_CHIPS))
    w_out_full = _own_slot(l_out, s_out).reshape(D_MODEL, D_MODEL)
    h1, n2, r2 = out_proj_rms(y, w_out_full, xs, norm_ffn_w, "out_proj_rms")
    _, (l_ff1,) = exchange_wait(_one(h_fw, 1), "forward_ff1_wait", after=[n2], copies=len(OTHER_CHIPS))
    w_ff1_sh = _own_slot(l_ff1, s_ff1)
    a1, act = matmul(n2, w_ff1_sh, "nn", "ff1", out_dtypes=(F32, BF16), epilogue=_relu2_epilogue, b_shards=True)
    _, (l_ff2,) = exchange_wait(_one(h_fw, 2), "forward_ff2_wait", after=[act], copies=len(OTHER_CHIPS))
    w_ff2_full = _own_slot(l_ff2, s_ff2).reshape(D_FF, D_MODEL)
    loss_sum, dh2_b, d_final = ff2_loss(act, w_ff2_full, h1, norm_final_w.reshape(1, D_MODEL), target, "ff2_loss")

    da1 = matmul(dh2_b, w_ff2_full, "nt", "d_act", out_dtypes=(BF16,), epilogue=_relu2_bwd_epilogue, extra=a1)
    t_all = xs.shape[0]
    dw_ff2 = matmul(act, dh2_b, "tn", "dw_ff2", out_dtypes=(BF16,), tk=t_all)
    p_ff2 = dw_ff2.reshape(N_DEV, D_FF // N_DEV, D_MODEL)
    h_s1, t_s1 = exchange_start([p_ff2], False, "scatter_ff2_start")
    dn2 = matmul(da1, w_ff1_sh, "nt", "d_n2", out_dtypes=(BF16,), after=[t_s1], b_shards=True, k_group=4)
    p_ff1 = matmul(n2, da1, "tn", "dw_ff1", out_dtypes=(BF16,), tn=D_FF // N_DEV, tk=t_all, after=[t_s1], out_shards=True)
    h_s2, t_s2 = exchange_start([p_ff1], False, "scatter_ff1_start")
    dh1_b, d_ffn = rms_bwd(h1, r2, norm_ffn_w + tok(t_s2), dn2, dh2_b, BF16, "rms_ffn_bwd")
    dmix = matmul(dh1_b, w_out_full, "nt", "d_mix", out_dtypes=(BF16,))
    dw_out = matmul(y, dh1_b, "tn", "dw_out", out_dtypes=(BF16,), tk=t_all)
    p_out = dw_out.reshape(N_DEV, D_MODEL // N_DEV, D_MODEL)
    h_s3, t_s3 = exchange_start([p_out], False, "scatter_out_start")
    d_qkv_c, dproj, d_alog_l, d_dt_l, d_gnw, dl0, dl1, d_hnw = mixer_bwd(
        qkv_c, proj, a_log_l, dt_l, gdn_norm_w + tok(t_s3), l0, l1, hgrn_norm_w, hist_a, inv_a, hist_b, o_b, dmix,
        "mixer_bwd")
    dproj, d_conv_full = conv_bwd(proj, d_qkv_c, conv_full, dproj, "conv_bwd")
    dw_cat = matmul(n1, dproj, "tn", "dw_in", out_dtypes=(BF16,), tm=512, tn=CAT_WIDTH // 5, tk=t_all)
    p_in = cat_to_shards(dw_cat, shard_in)
    h_pair, t_s4 = routed_start(p_in, _to_sibling_routes, "scatter_in_pair_start")

    (s_ff2g,), (r_ff2,) = exchange_wait(h_s1, "scatter_ff2_wait", after=[t_s4])
    g_w_ff2, d_w_ff2, nm_w_ff2, nv_w_ff2 = adamw_reduce(
        r_ff2, s_ff2g, me, w_ff2[0], m_w_ff2[0], v_w_ff2[0], "adamw_w_ff2")
    (p_in,), (from_sibling,) = exchange_wait(h_pair, "scatter_in_pair_wait", after=[d_w_ff2], copies=N_CHIPS)
    chip_sums = pair_sum(p_in, from_sibling, "scatter_in_pair_sum")
    h_chips, t_s5 = routed_start(chip_sums, _to_chips_routes, "scatter_in_chips_start")
    (s_ff1g,), (r_ff1,) = exchange_wait(h_s2, "scatter_ff1_wait", after=[t_s5])
    (s_outg,), (r_out,) = exchange_wait(h_s3, "scatter_out_wait", after=[t_s5])
    g_w_ff1, d_w_ff1, nm_w_ff1, nv_w_ff1 = adamw_reduce(
        r_ff1, s_ff1g, me, w_ff1[0], m_w_ff1[0], v_w_ff1[0], "adamw_w_ff1")
    g_w_out, d_w_out, nm_w_out, nv_w_out = adamw_reduce(
        r_out, s_outg, me, w_out[0], m_w_out[0], v_w_out[0], "adamw_w_out")
    dn1 = matmul_ring(dproj, w_cat, "nt", "d_n1", BF16, tm=512, tn=512, tk=CAT_WIDTH, after=[t_s5])
    dx, d_mix = rms_bwd(xs, r1, norm_mix_w, dn1, dh1_b, F32, "rms_mix_bwd")
    (chip_sums,), (r_in,) = exchange_wait(h_chips, "scatter_in_chips_wait", after=[dx, d_w_ff1, d_w_out],
                                          copies=len(OTHER_CHIPS))
    g_w_in, d_w_in, nm_w_in, nv_w_in = adamw_reduce(
        r_in, chip_sums, me // 2, w_in[0], m_in, v_in, "adamw_w_in")

    d_lb = jnp.stack([dl0.reshape(GDN_WIDTH), dl1.reshape(GDN_WIDTH)])
    small_shapes = [(1, N_HEADS), (1, N_HEADS), (1, HEAD_DIM), (2, GDN_WIDTH), (1, HEAD_DIM), (1, D_MODEL),
                    (1, D_MODEL), (D_MODEL,), (4, QKV_WIDTH), ()]
    small = _pack([d_alog_l[:, 0, 0], d_dt_l[:, 0, 0], d_gnw, d_lb, d_hnw, d_mix, d_ffn, d_final, d_conv_full,
                   loss_sum[0, 0]])
    red = allreduce_small(small, "allreduce_small")
    g_alog, g_dt, g_gnw, g_lb, g_hnw, g_mix, g_ffn, g_final, g_conv_full, loss = _unpack(red, small_shapes)
    g_conv = lax.dynamic_slice(g_conv_full, (0, me * shard_conv), (4, shard_conv)).reshape(1, 4, shard_conv)
    small_g = [g_alog, g_dt, g_gnw, g_lb, g_hnw, g_mix, g_ffn, g_final, g_conv]
    small_w = [gdn_a_log, gdn_dt_bias, gdn_norm_w, hgrn_lb_logits, hgrn_norm_w, norm_mix_w, norm_ffn_w, norm_final_w, conv_w]
    small_m = [m_gdn_a_log, m_gdn_dt_bias, m_gdn_norm_w, m_hgrn_lb_logits, m_hgrn_norm_w, m_norm_mix_w, m_norm_ffn_w,
               m_norm_final_w, m_conv_w]
    small_v = [v_gdn_a_log, v_gdn_dt_bias, v_gdn_norm_w, v_hgrn_lb_logits, v_hgrn_norm_w, v_norm_mix_w, v_norm_ffn_w,
               v_norm_final_w, v_conv_w]
    rows = lambda arrays: [a.reshape(-1, a.shape[-1]) for a in arrays]
    like_w = lambda arrays: [a.reshape(w.shape) for a, w in zip(arrays, small_w)]
    d_s, m_s, v_s = adamw_small(rows(small_w), rows(small_g), rows(small_m), rows(small_v), "adamw_small")
    d_alog, d_dt, d_gn, d_lbl, d_hn, d_nm, d_nf, d_nfin, d_cw = like_w(d_s)
    m_alog, m_dt, m_gn, m_lbl, m_hn, m_nm, m_nf, m_nfin, m_cw = like_w(m_s)
    v_alog, v_dt, v_gn, v_lbl, v_hn, v_nm, v_nf, v_nfin, v_cw = like_w(v_s)

    lead = lambda a: a[None]
    grads = [lead(g_w_in), g_conv, g_alog, g_dt, g_gnw, g_lb, g_hnw, lead(g_w_out), g_mix, g_ffn,
             lead(g_w_ff1), lead(g_w_ff2), g_final]
    deltas = [lead(d_w_in), d_cw, d_alog, d_dt, d_gn, d_lbl, d_hn, lead(d_w_out), d_nm, d_nf,
              lead(d_w_ff1), lead(d_w_ff2), d_nfin]
    new_m = [lead(nm_w_in), m_cw, m_alog, m_dt, m_gn, m_lbl, m_hn, lead(nm_w_out), m_nm, m_nf,
             lead(nm_w_ff1), lead(nm_w_ff2), m_nfin]
    new_v = [lead(nv_w_in), v_cw, v_alog, v_dt, v_gn, v_lbl, v_hn, lead(nv_w_out), v_nm, v_nf,
             lead(nv_w_ff1), lead(nv_w_ff2), v_nfin]
    return (loss, dx[None], *grads, *deltas, *new_m, *new_v)
```
